```python
import jax, jax.numpy as jnp
from jax import lax
import numpy as np

D_MODEL = 1024
BATCH = 8
SEQ = 2048
DEPTH = 2

POOL_WINDOWS = (2, 4, 8, 16)
POOL_GROUPS = len(POOL_WINDOWS)
POOL_GROUP_DIM = D_MODEL // 8
POOL_WIDTH = POOL_GROUPS * POOL_GROUP_DIM
HGRN_HEAD_DIM = 128
HGRN_HEADS = D_MODEL // HGRN_HEAD_DIM
HGRN_WIDTH = HGRN_HEADS * HGRN_HEAD_DIM
CHUNK = 64
NORM_EPS = 1e-6
IN_SIZES = (POOL_WIDTH, POOL_WIDTH, HGRN_WIDTH, HGRN_WIDTH, HGRN_WIDTH, HGRN_WIDTH, D_MODEL, D_MODEL)
IN_WIDTH = sum(IN_SIZES)

kernel_name = "hybrid_pool_hgrn2_gated_block"


def rms_norm(x, g):
    xf = x.astype(jnp.float32)
    y = xf * lax.rsqrt(jnp.mean(xf * xf, axis=-1, keepdims=True) + NORM_EPS)
    return (y * g.astype(jnp.float32)).astype(x.dtype)


def multiscale_pool(u):
    b, s, _ = u.shape
    uf = u.astype(jnp.float32)
    csum = lax.cumsum(uf, axis=1)
    pos = jnp.arange(s, dtype=jnp.float32) + 1.0
    outs = []
    for gi, w in enumerate(POOL_WINDOWS):
        sl = slice(gi * POOL_GROUP_DIM, (gi + 1) * POOL_GROUP_DIM)
        cg = csum[:, :, sl]
        prev = jnp.pad(cg, ((0, 0), (w, 0), (0, 0)))[:, :s]
        count = jnp.minimum(pos, float(w))[None, :, None]
        outs.append((cg - prev) / count - uf[:, :, sl])
    return jnp.stack(outs, axis=2)


def _hgrn2_chunk_step(state, inp):
    q, k, v, logf = inp
    cum = jnp.cumsum(logf, axis=2)
    o_inter = jnp.einsum('bhtk,bhkv->bhtv', q * jnp.exp(cum), state)
    c = q.shape[2]
    causal = jnp.tril(jnp.ones((c, c), dtype=bool))[:, :, None]
    diff = cum[:, :, :, None, :] - cum[:, :, None, :, :]
    decay = jnp.where(causal, jnp.exp(jnp.minimum(diff, 0.0)), 0.0)
    scores = jnp.sum(q[:, :, :, None, :] * k[:, :, None, :, :] * decay, axis=-1)
    o = o_inter + jnp.einsum('bhts,bhsv->bhtv', scores, v)
    last = cum[:, :, -1:, :]
    new_state = (jnp.exp(last[:, :, 0, :])[..., None] * state
                 + jnp.einsum('bhsk,bhsv->bhkv', k * jnp.exp(last - cum), v))
    return new_state, o


def hgrn2(q, k, v, logf):
    b, s, h, dk = q.shape
    n = s // CHUNK

    def to_chunks(t):
        return t.reshape(b, n, CHUNK, h, t.shape[-1]).transpose(1, 0, 3, 2, 4)

    state0 = jnp.zeros((b, h, dk, v.shape[-1]), jnp.float32)
    _, o = lax.scan(_hgrn2_chunk_step, state0,
                    (to_chunks(q), to_chunks(k), to_chunks(v), to_chunks(logf)))
    return o.transpose(1, 0, 3, 2, 4).reshape(b, s, h, v.shape[-1])


def _fwd_setup_inputs(seed: int = 0) -> dict:
    key = jax.random.key(seed)
    ks = jax.random.split(key, 16)
    L, D = DEPTH, D_MODEL
    nrm = jax.random.normal
    return {
        "x": nrm(ks[0], (BATCH, SEQ, D), jnp.float32),
        "c": nrm(ks[1], (BATCH, D), jnp.float32),
        "w_ada": nrm(ks[2], (L, D, 3 * D), jnp.float32) * (0.5 * D ** -0.5),
        "b_ada": nrm(ks[3], (L, 3 * D), jnp.float32) * 0.02,
        "g_pre": 1.0 + 0.02 * nrm(ks[4], (L, D), jnp.float32),
        "g_post": 1.0 + 0.02 * nrm(ks[5], (L, D), jnp.float32),
        "w_in": nrm(ks[6], (L, D, IN_WIDTH), jnp.float32) * D ** -0.5,
        "pool_w": nrm(ks[7], (L, POOL_GROUPS, POOL_GROUP_DIM, POOL_GROUP_DIM), jnp.float32) * POOL_GROUP_DIM ** -0.5,
        "pool_scale": 1.0 + 0.02 * nrm(ks[8], (L, POOL_WIDTH), jnp.float32),
        "lb_logits": nrm(ks[9], (L, HGRN_WIDTH), jnp.float32),
        "hgrn_norm_g": 1.0 + 0.02 * nrm(ks[10], (L, HGRN_HEAD_DIM), jnp.float32),
        "w_pool_o": nrm(ks[11], (L, POOL_WIDTH, D), jnp.float32) * POOL_WIDTH ** -0.5,
        "w_hgrn_o": nrm(ks[12], (L, HGRN_WIDTH, D), jnp.float32) * HGRN_WIDTH ** -0.5,
        "w_out": nrm(ks[13], (L, D, D), jnp.float32) * D ** -0.5,
    }


def _fwd_reference(x, c, w_ada, b_ada, g_pre, g_post, w_in, pool_w, pool_scale, lb_logits,
              hgrn_norm_g, w_pool_o, w_hgrn_o, w_out):
    b, s, d = x.shape
    p = jax.nn.softmax(lb_logits.astype(jnp.float32), axis=0)
    lower_bounds = jnp.cumsum(p, axis=0) - p[0:1]
    split_idx = np.cumsum(IN_SIZES)[:-1].tolist()
    c_act = jax.nn.silu(c)
    for l in range(DEPTH):
        ada = c_act @ w_ada[l] + b_ada[l]
        shift, scale, gate = jnp.split(ada[:, None, :], 3, axis=-1)
        h = rms_norm(x, g_pre[l]) * (1.0 + scale) + shift
        (pv, pg, hq, hf, hi, hg, mg_pool, mg_hgrn) = jnp.split(h @ w_in[l], split_idx, axis=-1)

        pooled = multiscale_pool(pv)
        pooled = jnp.einsum('bsgc,gcd->bsgd', pooled, pool_w[l].astype(jnp.float32))
        pooled = pooled.reshape(b, s, POOL_WIDTH) * pool_scale[l]
        branch_a = (pooled.astype(x.dtype) * jax.nn.silu(pg)) @ w_pool_o[l]

        shp = (b, s, HGRN_HEADS, HGRN_HEAD_DIM)
        lb = jnp.clip(lower_bounds[l], 0.0, 1.0).reshape(HGRN_HEADS, HGRN_HEAD_DIM)
        zf = hf.astype(jnp.float32).reshape(shp)
        f = lb + (1.0 - lb) * jax.nn.sigmoid(zf)
        logf = jnp.log(jnp.maximum(f, 1e-30))
        k = 1.0 - f
        q = jax.nn.silu(hq.astype(jnp.float32)).reshape(shp)
        v = hi.astype(jnp.float32).reshape(shp)
        o = hgrn2(q, k, v, logf)
        o = rms_norm(o, hgrn_norm_g[l]).astype(x.dtype).reshape(b, s, HGRN_WIDTH)
        branch_b = (o * jax.nn.silu(hg)) @ w_hgrn_o[l]

        merged = jax.nn.sigmoid(mg_pool) * branch_a + jax.nn.sigmoid(mg_hgrn) * branch_b
        y = merged @ w_out[l]
        x = x + gate * rms_norm(y, g_post[l])
    return x


import jax as _jax
import jax.numpy as _jnp

TWIN_FORMAT = 'train_step'
FWD_PARAMS = ['x', 'c', 'w_ada', 'b_ada', 'g_pre', 'g_post', 'w_in', 'pool_w', 'pool_scale', 'lb_logits', 'hgrn_norm_g', 'w_pool_o', 'w_hgrn_o', 'w_out']
TWIN_WEIGHTS = ['w_ada', 'b_ada', 'g_pre', 'g_post', 'w_in', 'pool_w', 'pool_scale', 'lb_logits', 'hgrn_norm_g', 'w_pool_o', 'w_hgrn_o', 'w_out']
TWIN_DIFF_INPUT = 'x'
TWIN_INPUTS = ['x', 'c', 'w_ada', 'b_ada', 'g_pre', 'g_post', 'w_in', 'pool_w', 'pool_scale', 'lb_logits', 'hgrn_norm_g', 'w_pool_o', 'w_hgrn_o', 'w_out', 'loss_target', 'm_w_ada', 'm_b_ada', 'm_g_pre', 'm_g_post', 'm_w_in', 'm_pool_w', 'm_pool_scale', 'm_lb_logits', 'm_hgrn_norm_g', 'm_w_pool_o', 'm_w_hgrn_o', 'm_w_out', 'v_w_ada', 'v_b_ada', 'v_g_pre', 'v_g_post', 'v_w_in', 'v_pool_w', 'v_pool_scale', 'v_lb_logits', 'v_hgrn_norm_g', 'v_w_pool_o', 'v_w_hgrn_o', 'v_w_out']
TWIN_OUTPUTS = ['loss', 'grad_x', 'grad_w_ada', 'grad_b_ada', 'grad_g_pre', 'grad_g_post', 'grad_w_in', 'grad_pool_w', 'grad_pool_scale', 'grad_lb_logits', 'grad_hgrn_norm_g', 'grad_w_pool_o', 'grad_w_hgrn_o', 'grad_w_out', 'delta_w_ada', 'delta_b_ada', 'delta_g_pre', 'delta_g_post', 'delta_w_in', 'delta_pool_w', 'delta_pool_scale', 'delta_lb_logits', 'delta_hgrn_norm_g', 'delta_w_pool_o', 'delta_w_hgrn_o', 'delta_w_out', 'new_m_w_ada', 'new_m_b_ada', 'new_m_g_pre', 'new_m_g_post', 'new_m_w_in', 'new_m_pool_w', 'new_m_pool_scale', 'new_m_lb_logits', 'new_m_hgrn_norm_g', 'new_m_w_pool_o', 'new_m_w_hgrn_o', 'new_m_w_out', 'new_v_w_ada', 'new_v_b_ada', 'new_v_g_pre', 'new_v_g_post', 'new_v_w_in', 'new_v_pool_w', 'new_v_pool_scale', 'new_v_lb_logits', 'new_v_hgrn_norm_g', 'new_v_w_pool_o', 'new_v_w_hgrn_o', 'new_v_w_out']
TWIN_LEAF_KINDS = {'loss': 'loss', 'grad_x': 'grad_x', 'grad_w_ada': 'grad_w', 'grad_b_ada': 'grad_w', 'grad_g_pre': 'grad_w', 'grad_g_post': 'grad_w', 'grad_w_in': 'grad_w', 'grad_pool_w': 'grad_w', 'grad_pool_scale': 'grad_w', 'grad_lb_logits': 'grad_w', 'grad_hgrn_norm_g': 'grad_w', 'grad_w_pool_o': 'grad_w', 'grad_w_hgrn_o': 'grad_w', 'grad_w_out': 'grad_w', 'delta_w_ada': 'delta_w', 'delta_b_ada': 'delta_w', 'delta_g_pre': 'delta_w', 'delta_g_post': 'delta_w', 'delta_w_in': 'delta_w', 'delta_pool_w': 'delta_w', 'delta_pool_scale': 'delta_w', 'delta_lb_logits': 'delta_w', 'delta_hgrn_norm_g': 'delta_w', 'delta_w_pool_o': 'delta_w', 'delta_w_hgrn_o': 'delta_w', 'delta_w_out': 'delta_w', 'new_m_w_ada': 'new_m', 'new_m_b_ada': 'new_m', 'new_m_g_pre': 'new_m', 'new_m_g_post': 'new_m', 'new_m_w_in': 'new_m', 'new_m_pool_w': 'new_m', 'new_m_pool_scale': 'new_m', 'new_m_lb_logits': 'new_m', 'new_m_hgrn_norm_g': 'new_m', 'new_m_w_pool_o': 'new_m', 'new_m_w_hgrn_o': 'new_m', 'new_m_w_out': 'new_m', 'new_v_w_ada': 'new_v', 'new_v_b_ada': 'new_v', 'new_v_g_pre': 'new_v', 'new_v_g_post': 'new_v', 'new_v_w_in': 'new_v', 'new_v_pool_w': 'new_v', 'new_v_pool_scale': 'new_v', 'new_v_lb_logits': 'new_v', 'new_v_hgrn_norm_g': 'new_v', 'new_v_w_pool_o': 'new_v', 'new_v_w_hgrn_o': 'new_v', 'new_v_w_out': 'new_v'}


def _forward(args):
    return _fwd_reference(*[args[k] for k in FWD_PARAMS])


def _output_shape():
    out = _jax.eval_shape(lambda: _forward(_fwd_setup_inputs(0)))
    return out.shape, out.dtype

N_MICROBATCH = 1
ADAM_LR = 0.001
ADAM_B1 = 0.9
ADAM_B2 = 0.999
ADAM_EPS = 1e-08
ADAM_WD = 0.01
ADAM_STEP = 10
PER_EXAMPLE_BATCH_AXIS = {'x': 0, 'c': 0, 'loss_target': 0}
SHARED_INPUTS = []
_WEIGHT_DTYPES = {'w_ada': _jnp.float32, 'b_ada': _jnp.float32, 'g_pre': _jnp.float32, 'g_post': _jnp.float32, 'w_in': _jnp.float32, 'pool_w': _jnp.float32, 'pool_scale': _jnp.float32, 'lb_logits': _jnp.float32, 'hgrn_norm_g': _jnp.float32, 'w_pool_o': _jnp.float32, 'w_hgrn_o': _jnp.float32, 'w_out': _jnp.float32}
MOMENT_SCALE = {'w_ada': 8.832286e-01, 'b_ada': 1.617589e+00, 'g_pre': 8.872146e-02, 'g_post': 1.885165e+00, 'w_in': 3.919362e-02, 'pool_w': 6.214815e-02, 'pool_scale': 6.387186e-02, 'lb_logits': 2.964279e-03, 'hgrn_norm_g': 2.049185e-01, 'w_pool_o': 4.652122e-02, 'w_hgrn_o': 5.829345e-02, 'w_out': 7.803613e-02}


def _to_microbatches(a, axis):
    t = _jnp.moveaxis(a, axis, 0)
    t = t.reshape((N_MICROBATCH, t.shape[0] // N_MICROBATCH) + t.shape[1:])
    return _jnp.moveaxis(t, 1, axis + 1)


def setup_inputs(seed: int = 0) -> dict:
    inp = _fwd_setup_inputs(seed)
    key = _jax.random.fold_in(_jax.random.key(seed), 7919)
    shape, _ = _output_shape()
    out = dict(inp)
    out["loss_target"] = _jax.random.normal(_jax.random.fold_in(key, 0), shape, _jnp.float32)
    for i, name in enumerate(TWIN_WEIGHTS):
        w = inp[name].astype(_jnp.float32)
        if MOMENT_SCALE is None:
            s = _jnp.sqrt(_jnp.mean(_jnp.square(w)) + 1e-30)
        else:
            s = MOMENT_SCALE[name]
        km, kv = _jax.random.split(_jax.random.fold_in(key, i + 1))
        out[name] = w
        out["m_" + name] = s * _jax.random.normal(km, w.shape, _jnp.float32)
        out["v_" + name] = (s * s) * _jax.random.uniform(kv, w.shape, _jnp.float32, 0.5, 1.5)
    if N_MICROBATCH > 1:
        for name, axis in PER_EXAMPLE_BATCH_AXIS.items():
            out[name] = _to_microbatches(out[name], axis)
    return {'x': out['x'], 'c': out['c'], 'w_ada': out['w_ada'], 'b_ada': out['b_ada'], 'g_pre': out['g_pre'], 'g_post': out['g_post'], 'w_in': out['w_in'], 'pool_w': out['pool_w'], 'pool_scale': out['pool_scale'], 'lb_logits': out['lb_logits'], 'hgrn_norm_g': out['hgrn_norm_g'], 'w_pool_o': out['w_pool_o'], 'w_hgrn_o': out['w_hgrn_o'], 'w_out': out['w_out'], 'loss_target': out['loss_target'], 'm_w_ada': out['m_w_ada'], 'm_b_ada': out['m_b_ada'], 'm_g_pre': out['m_g_pre'], 'm_g_post': out['m_g_post'], 'm_w_in': out['m_w_in'], 'm_pool_w': out['m_pool_w'], 'm_pool_scale': out['m_pool_scale'], 'm_lb_logits': out['m_lb_logits'], 'm_hgrn_norm_g': out['m_hgrn_norm_g'], 'm_w_pool_o': out['m_w_pool_o'], 'm_w_hgrn_o': out['m_w_hgrn_o'], 'm_w_out': out['m_w_out'], 'v_w_ada': out['v_w_ada'], 'v_b_ada': out['v_b_ada'], 'v_g_pre': out['v_g_pre'], 'v_g_post': out['v_g_post'], 'v_w_in': out['v_w_in'], 'v_pool_w': out['v_pool_w'], 'v_pool_scale': out['v_pool_scale'], 'v_lb_logits': out['v_lb_logits'], 'v_hgrn_norm_g': out['v_hgrn_norm_g'], 'v_w_pool_o': out['v_w_pool_o'], 'v_w_hgrn_o': out['v_w_hgrn_o'], 'v_w_out': out['v_w_out']}


def _loss(weights, diff, rest, loss_target):
    with _jax.named_scope("forward"):
        args = {**rest, TWIN_DIFF_INPUT: diff, **{k: w.astype(_WEIGHT_DTYPES[k]) for k, w in weights.items()}}
        y = _forward(args)
    with _jax.named_scope("loss_head"):
        err = _jnp.square(y.astype(_jnp.float32) - loss_target)
        return 0.5 * _jnp.sum(_jnp.mean(err, axis=-1)) if err.ndim else 0.5 * err


def _adamw(w, g, m, v):
    m = ADAM_B1 * m + (1.0 - ADAM_B1) * g
    v = ADAM_B2 * v + (1.0 - ADAM_B2) * _jnp.square(g)
    m_hat = m / (1.0 - ADAM_B1 ** ADAM_STEP)
    v_hat = v / (1.0 - ADAM_B2 ** ADAM_STEP)
    delta = -ADAM_LR * (m_hat / (_jnp.sqrt(v_hat) + ADAM_EPS) + ADAM_WD * w)
    return delta, m, v


def reference(x, c, w_ada, b_ada, g_pre, g_post, w_in, pool_w, pool_scale, lb_logits, hgrn_norm_g, w_pool_o, w_hgrn_o, w_out, loss_target, m_w_ada, m_b_ada, m_g_pre, m_g_post, m_w_in, m_pool_w, m_pool_scale, m_lb_logits, m_hgrn_norm_g, m_w_pool_o, m_w_hgrn_o, m_w_out, v_w_ada, v_b_ada, v_g_pre, v_g_post, v_w_in, v_pool_w, v_pool_scale, v_lb_logits, v_hgrn_norm_g, v_w_pool_o, v_w_hgrn_o, v_w_out):
    given = dict(x=x, c=c, w_ada=w_ada, b_ada=b_ada, g_pre=g_pre, g_post=g_post, w_in=w_in, pool_w=pool_w, pool_scale=pool_scale, lb_logits=lb_logits, hgrn_norm_g=hgrn_norm_g, w_pool_o=w_pool_o, w_hgrn_o=w_hgrn_o, w_out=w_out, loss_target=loss_target, m_w_ada=m_w_ada, m_b_ada=m_b_ada, m_g_pre=m_g_pre, m_g_post=m_g_post, m_w_in=m_w_in, m_pool_w=m_pool_w, m_pool_scale=m_pool_scale, m_lb_logits=m_lb_logits, m_hgrn_norm_g=m_hgrn_norm_g, m_w_pool_o=m_w_pool_o, m_w_hgrn_o=m_w_hgrn_o, m_w_out=m_w_out, v_w_ada=v_w_ada, v_b_ada=v_b_ada, v_g_pre=v_g_pre, v_g_post=v_g_post, v_w_in=v_w_in, v_pool_w=v_pool_w, v_pool_scale=v_pool_scale, v_lb_logits=v_lb_logits, v_hgrn_norm_g=v_hgrn_norm_g, v_w_pool_o=v_w_pool_o, v_w_hgrn_o=v_w_hgrn_o, v_w_out=v_w_out)
    weights = {n: given[n] for n in TWIN_WEIGHTS}
    shared = {n: given[n] for n in SHARED_INPUTS}
    per_example = {n: given[n] for n in ['x', 'c']}
    grad_fn = _jax.value_and_grad(_loss, argnums=(0, 1))

    def one_microbatch(ex, loss_target):
        ex = dict(ex)
        diff = ex.pop(TWIN_DIFF_INPUT)
        return grad_fn(weights, diff, {**shared, **ex}, loss_target)

    if N_MICROBATCH == 1:
        loss, (grad_w, grad_x) = one_microbatch(per_example, given["loss_target"])
    else:
        def body(carry, xs):
            loss_sum, grad_sum = carry
            l_k, (gw_k, gx_k) = one_microbatch(xs[0], xs[1])
            with _jax.named_scope("update"):
                return (loss_sum + l_k, _jax.tree.map(_jnp.add, grad_sum, gw_k)), gx_k

        init = (_jnp.zeros((), _jnp.float32), _jax.tree.map(_jnp.zeros_like, weights))
        (loss, grad_w), grad_x = _jax.lax.scan(body, init, (per_example, given["loss_target"]))
    with _jax.named_scope("update"):
        delta_w, new_m, new_v = {}, {}, {}
        for n in TWIN_WEIGHTS:
            delta_w[n], new_m[n], new_v[n] = _adamw(weights[n], grad_w[n], given["m_" + n], given["v_" + n])
    return (loss, grad_x, *[grad_w[n] for n in TWIN_WEIGHTS], *[delta_w[n] for n in TWIN_WEIGHTS],
            *[new_m[n] for n in TWIN_WEIGHTS], *[new_v[n] for n in TWIN_WEIGHTS])
```

```python
import functools

import jax
import jax.numpy as jnp
from jax import lax
from jax.experimental import pallas as pl
from jax.experimental.pallas import tpu as pltpu

F32 = jnp.float32
BF16 = jnp.bfloat16
MESH = pl.DeviceIdType.MESH

D = 1024
HEADS = 8
HD = 128
GROUPS = 4
POOL_W = 512
WINDOWS = (2, 4, 8, 16)
CH = 64
SB = 16
IN_W = 7168
NCHIP = 4
NDEV = 8
EPS = 1e-6
PV0, PG0, HQ0, HF0, HI0, HG0 = 0, 4, 8, 16, 24, 32
MGP_BLK, MGH_BLK = 5, 6

LR, B1, B2, AEPS, WD, STEP = 0.001, 0.9, 0.999, 1e-08, 0.01, 10
VMEM_LIMIT = 56 * 1024 * 1024


def _cp(sem=None, **kw):
    if sem is not None:
        kw["dimension_semantics"] = sem
    return pltpu.CompilerParams(vmem_limit_bytes=VMEM_LIMIT, **kw)


def _sig(z):
    return 1.0 / (1.0 + jnp.exp(-z))


def _dsilu(z, s):
    return s * (1.0 + z * (1.0 - s))


def _row_tile(rows, cap):
    if rows <= cap:
        return rows
    t = 1 << (cap.bit_length() - 1)
    while rows % t:
        t //= 2
    return t


def _mm(a, b, *, name, b_mode="nn", out_shards=0, tm=1024, tn=256, tk=None, out_dtype=F32):
    M, K = a.shape
    if b_mode == "nn":
        N = b.shape[1]
    elif b_mode == "nt":
        N = b.shape[0]
    elif b_mode == "nn_sh":
        N = b.shape[0] * b.shape[2]
    else:
        N = b.shape[1]
    tm = _row_tile(M, tm)
    if b_mode == "nn_sh":
        tn = _row_tile(b.shape[2], tn)
    elif out_shards:
        tn = _row_tile(N // out_shards, tn)
    else:
        tn = _row_tile(N, tn)
    if tk is None:
        tk = K if b_mode != "nt_shk" else b.shape[2]
    if b_mode == "nt_shk":
        tk = _row_tile(b.shape[2], tk)
    nm, nn, nk = M // tm, N // tn, K // tk

    a_spec = pl.BlockSpec((tm, tk), lambda m, n, k: (m, k))
    if b_mode == "nn":
        b_spec = pl.BlockSpec((tk, tn), lambda m, n, k: (k, n))
    elif b_mode == "nt":
        b_spec = pl.BlockSpec((tn, tk), lambda m, n, k: (n, k))
    elif b_mode == "nn_sh":
        nps = b.shape[2] // tn
        b_spec = pl.BlockSpec((None, tk, tn), lambda m, n, k: (n // nps, k, n % nps))
    else:
        kps = b.shape[2] // tk
        b_spec = pl.BlockSpec((None, tn, tk), lambda m, n, k: (k // kps, n, k % kps))
    if out_shards:
        ops = (N // out_shards) // tn
        o_spec = pl.BlockSpec((None, tm, tn), lambda m, n, k: (n // ops, m, n % ops))
        o_shape = jax.ShapeDtypeStruct((out_shards, M, N // out_shards), out_dtype)
    else:
        o_spec = pl.BlockSpec((tm, tn), lambda m, n, k: (m, n))
        o_shape = jax.ShapeDtypeStruct((M, N), out_dtype)
    trans_b = b_mode in ("nt", "nt_shk")
    dn = (((1,), (1,)), ((), ())) if trans_b else (((1,), (0,)), ((), ()))

    def body(a_ref, b_ref, o_ref, acc_ref):
        k = pl.program_id(2)

        @pl.when(k == 0)
        def _():
            acc_ref[...] = jnp.zeros(acc_ref.shape, F32)

        acc_ref[...] += lax.dot_general(a_ref[...].astype(BF16), b_ref[...].astype(BF16), dn,
                                        preferred_element_type=F32)

        @pl.when(k == nk - 1)
        def _():
            o_ref[...] = acc_ref[...].astype(o_ref.dtype)

    return pl.pallas_call(
        body, name=name, grid=(nm, nn, nk), in_specs=[a_spec, b_spec], out_specs=o_spec,
        out_shape=o_shape, scratch_shapes=[pltpu.VMEM((tm, tn), F32)],
        compiler_params=_cp(("parallel", "parallel", "arbitrary")),
    )(a, b)


def _rowvec(n=D):
    return pl.BlockSpec((1, n), lambda i: (0, 0))


def _prenorm_fwd(x, g, scale, shift, name):
    S = x.shape[0]
    tr = _row_tile(S, 256)

    def body(x_ref, g_ref, sc_ref, sh_ref, h_ref):
        xv = x_ref[...]
        r = lax.rsqrt(jnp.mean(xv * xv, axis=-1, keepdims=True) + EPS)
        h_ref[...] = ((xv * r) * g_ref[...] * (1.0 + sc_ref[...]) + sh_ref[...]).astype(BF16)

    return pl.pallas_call(
        body, name=name, grid=(S // tr,),
        in_specs=[pl.BlockSpec((tr, D), lambda i: (i, 0)), _rowvec(), _rowvec(), _rowvec()],
        out_specs=pl.BlockSpec((tr, D), lambda i: (i, 0)),
        out_shape=jax.ShapeDtypeStruct((S, D), BF16), compiler_params=_cp(("parallel",)),
    )(x, g, scale, shift)


def _prenorm_bwd(dh, dxn, x, g, scale, name):
    S = x.shape[0]
    tr = _row_tile(S, 256)

    def body(dh_ref, dxn_ref, x_ref, g_ref, sc_ref, dx_ref, dsh_ref, dsc_ref, dg_ref):
        i = pl.program_id(0)

        @pl.when(i == 0)
        def _():
            dsh_ref[...] = jnp.zeros((1, D), F32)
            dsc_ref[...] = jnp.zeros((1, D), F32)
            dg_ref[...] = jnp.zeros((1, D), F32)

        xv = x_ref[...]
        dhv = dh_ref[...]
        gv = g_ref[...]
        mod = 1.0 + sc_ref[...]
        r = lax.rsqrt(jnp.mean(xv * xv, axis=-1, keepdims=True) + EPS)
        xh = xv * r
        dsh_ref[...] += jnp.sum(dhv, axis=0, keepdims=True)
        dsc_ref[...] += jnp.sum(dhv * (xh * gv), axis=0, keepdims=True)
        dg_ref[...] += jnp.sum(dhv * mod * xh, axis=0, keepdims=True)
        u = dhv * mod * gv
        dx_ref[...] = dxn_ref[...] + r * u - xv * (r * r * r) * jnp.mean(u * xv, axis=-1, keepdims=True)

    tile = pl.BlockSpec((tr, D), lambda i: (i, 0))
    return pl.pallas_call(
        body, name=name, grid=(S // tr,),
        in_specs=[tile, tile, tile, _rowvec(), _rowvec()],
        out_specs=[tile, _rowvec(), _rowvec(), _rowvec()],
        out_shape=[jax.ShapeDtypeStruct((S, D), F32)] + [jax.ShapeDtypeStruct((1, D), F32)] * 3,
        compiler_params=_cp(("arbitrary",)),
    )(dh, dxn, x, g, scale)


def _postnorm_fwd(x, y, gate, g, name):
    S = x.shape[0]
    tr = _row_tile(S, 256)

    def body(x_ref, y_ref, gate_ref, g_ref, o_ref):
        yv = y_ref[...]
        r = lax.rsqrt(jnp.mean(yv * yv, axis=-1, keepdims=True) + EPS)
        o_ref[...] = x_ref[...] + gate_ref[...] * ((yv * r) * g_ref[...])

    tile = pl.BlockSpec((tr, D), lambda i: (i, 0))
    return pl.pallas_call(
        body, name=name, grid=(S // tr,), in_specs=[tile, tile, _rowvec(), _rowvec()],
        out_specs=tile, out_shape=jax.ShapeDtypeStruct((S, D), F32), compiler_params=_cp(("parallel",)),
    )(x, y, gate, g)


def _postnorm_bwd(dxn, y, gate, g, name):
    S = y.shape[0]
    tr = _row_tile(S, 256)

    def body(dxn_ref, y_ref, gate_ref, g_ref, dy_ref, dgate_ref, dg_ref):
        i = pl.program_id(0)

        @pl.when(i == 0)
        def _():
            dgate_ref[...] = jnp.zeros((1, D), F32)
            dg_ref[...] = jnp.zeros((1, D), F32)

        yv = y_ref[...]
        dv = dxn_ref[...]
        gv = g_ref[...]
        gt = gate_ref[...]
        r = lax.rsqrt(jnp.mean(yv * yv, axis=-1, keepdims=True) + EPS)
        yh = yv * r
        dgate_ref[...] += jnp.sum(dv * (yh * gv), axis=0, keepdims=True)
        dg_ref[...] += jnp.sum(dv * gt * yh, axis=0, keepdims=True)
        u = dv * gt * gv
        dy_ref[...] = (r * u - yv * (r * r * r) * jnp.mean(u * yv, axis=-1, keepdims=True)).astype(BF16)

    tile = pl.BlockSpec((tr, D), lambda i: (i, 0))
    return pl.pallas_call(
        body, name=name, grid=(S // tr,), in_specs=[tile, tile, _rowvec(), _rowvec()],
        out_specs=[tile, _rowvec(), _rowvec()],
        out_shape=[jax.ShapeDtypeStruct((S, D), BF16), jax.ShapeDtypeStruct((1, D), F32),
                   jax.ShapeDtypeStruct((1, D), F32)],
        compiler_params=_cp(("arbitrary",)),
    )(dxn, y, gate, g)


def _loss_head(xo, target, name):
    S = xo.shape[0]
    tr = _row_tile(S, 256)

    def body(x_ref, t_ref, dx_ref, l_ref):
        i = pl.program_id(0)

        @pl.when(i == 0)
        def _():
            l_ref[...] = jnp.zeros((8, 128), F32)

        err = x_ref[...] - t_ref[...]
        dx_ref[...] = err * (1.0 / D)
        l_ref[...] += 0.5 * jnp.sum(jnp.mean(err * err, axis=-1, keepdims=True))

    tile = pl.BlockSpec((tr, D), lambda i: (i, 0))
    return pl.pallas_call(
        body, name=name, grid=(S // tr,), in_specs=[tile, tile],
        out_specs=[tile, pl.BlockSpec((8, 128), lambda i: (0, 0))],
        out_shape=[jax.ShapeDtypeStruct((S, D), F32), jax.ShapeDtypeStruct((8, 128), F32)],
        compiler_params=_cp(("arbitrary",)),
    )(xo, target)


def _merge_fwd(proj, br, name):
    S = proj.shape[0]
    tr = _row_tile(S, 256)

    def body(mgp_ref, mgh_ref, br_ref, o_ref):
        o_ref[...] = (_sig(mgp_ref[...]) * br_ref[0] + _sig(mgh_ref[...]) * br_ref[1]).astype(BF16)

    return pl.pallas_call(
        body, name=name, grid=(S // tr,),
        in_specs=[pl.BlockSpec((tr, D), lambda i: (i, MGP_BLK)), pl.BlockSpec((tr, D), lambda i: (i, MGH_BLK)),
                  pl.BlockSpec((2, tr, D), lambda i: (0, i, 0))],
        out_specs=pl.BlockSpec((tr, D), lambda i: (i, 0)),
        out_shape=jax.ShapeDtypeStruct((S, D), BF16), compiler_params=_cp(("parallel",)),
    )(proj, proj, br)


def _merge_bwd(dm, proj, br, name):
    S = proj.shape[0]
    tr = _row_tile(S, 256)

    def body(dm_ref, mg_ref, br_ref, dbr_ref, dmg_ref):
        s = _sig(mg_ref[...])
        dmv = dm_ref[...]
        dbr_ref[...] = (dmv * s).astype(BF16)
        dmg_ref[...] = (dmv * br_ref[...] * s * (1.0 - s)).astype(BF16)

    return pl.pallas_call(
        body, name=name, grid=(S // tr, 2),
        in_specs=[pl.BlockSpec((tr, D), lambda i, j: (i, 0)),
                  pl.BlockSpec((tr, D), lambda i, j: (i, MGP_BLK + j)),
                  pl.BlockSpec((None, tr, D), lambda i, j: (j, i, 0))],
        out_specs=[pl.BlockSpec((None, tr, D), lambda i, j: (j, i, 0)),
                   pl.BlockSpec((tr, D), lambda i, j: (i, j))],
        out_shape=[jax.ShapeDtypeStruct((2, S, D), BF16), jax.ShapeDtypeStruct((S, 2 * D), BF16)],
        compiler_params=_cp(("parallel", "parallel")),
    )(dm, proj, br)


def _pool_pieces(u, g, S):
    rowi = lax.broadcasted_iota(jnp.int32, (S, 1), 0)

    def down(z, k):
        return jnp.where(rowi >= k, pltpu.roll(z, k, axis=0), 0.0)

    s2 = u + down(u, 1)
    s4 = s2 + down(s2, 2)
    s8 = s4 + down(s4, 4)
    s16 = s8 + down(s8, 8)
    win = jnp.where(g == 0, s2, jnp.where(g == 1, s4, jnp.where(g == 2, s8, s16)))
    w = jnp.where(g == 0, 2, jnp.where(g == 1, 4, jnp.where(g == 2, 8, 16)))
    count = jnp.minimum(rowi + 1, w).astype(F32)
    return win / count - u, count, rowi


def _pool_fwd(proj, pw, pscale, name):
    S = proj.shape[0]

    def body(pv_ref, pg_ref, pw_ref, sc_ref, a_ref):
        g = pl.program_id(0)
        pooled, _, _ = _pool_pieces(pv_ref[...], g, S)
        pm = jnp.dot(pooled.astype(BF16), pw_ref[...].astype(BF16), preferred_element_type=F32)
        pgv = pg_ref[...]
        a_ref[...] = (pm * sc_ref[...] * (pgv * _sig(pgv))).astype(BF16)

    return pl.pallas_call(
        body, name=name, grid=(GROUPS,),
        in_specs=[pl.BlockSpec((S, 128), lambda g: (0, PV0 + g)), pl.BlockSpec((S, 128), lambda g: (0, PG0 + g)),
                  pl.BlockSpec((None, 128, 128), lambda g: (g, 0, 0)), pl.BlockSpec((1, 128), lambda g: (0, g))],
        out_specs=pl.BlockSpec((S, 128), lambda g: (0, g)),
        out_shape=jax.ShapeDtypeStruct((S, POOL_W), BF16), compiler_params=_cp(("parallel",)),
    )(proj, proj, pw, pscale)


def _pool_bwd(da, proj, pw, pscale, name):
    S = proj.shape[0]

    def body(da_ref, pv_ref, pg_ref, pw_ref, sc_ref, dpv_ref, dpg_ref, dpw_ref, dsc_ref):
        g = pl.program_id(0)
        pooled, count, rowi = _pool_pieces(pv_ref[...], g, S)
        pwb = pw_ref[...].astype(BF16)
        pm = jnp.dot(pooled.astype(BF16), pwb, preferred_element_type=F32)
        scv = sc_ref[...]
        pgv = pg_ref[...]
        sg = _sig(pgv)
        dav = da_ref[...]
        d_ps = dav * (pgv * sg)
        dpg_ref[...] = (dav * (pm * scv) * _dsilu(pgv, sg)).astype(BF16)
        dsc_ref[...] = jnp.sum(d_ps * pm, axis=0, keepdims=True)
        d_pm = (d_ps * scv).astype(BF16)
        dpw_ref[...] = lax.dot_general(pooled.astype(BF16), d_pm, (((0,), (0,)), ((), ())),
                                       preferred_element_type=F32)
        d_pooled = lax.dot_general(d_pm, pwb, (((1,), (1,)), ((), ())), preferred_element_type=F32)
        z = d_pooled / count

        def up(v, k):
            return jnp.where(rowi < S - k, pltpu.roll(v, S - k, axis=0), 0.0)

        t2 = z + up(z, 1)
        t4 = t2 + up(t2, 2)
        t8 = t4 + up(t4, 4)
        t16 = t8 + up(t8, 8)
        adj = jnp.where(g == 0, t2, jnp.where(g == 1, t4, jnp.where(g == 2, t8, t16)))
        dpv_ref[...] = (adj - d_pooled).astype(BF16)

    col = lambda g: (0, g)
    return pl.pallas_call(
        body, name=name, grid=(GROUPS,),
        in_specs=[pl.BlockSpec((S, 128), col), pl.BlockSpec((S, 128), lambda g: (0, PV0 + g)),
                  pl.BlockSpec((S, 128), lambda g: (0, PG0 + g)),
                  pl.BlockSpec((None, 128, 128), lambda g: (g, 0, 0)), pl.BlockSpec((1, 128), col)],
        out_specs=[pl.BlockSpec((S, 128), col), pl.BlockSpec((S, 128), col),
                   pl.BlockSpec((None, 128, 128), lambda g: (g, 0, 0)), pl.BlockSpec((1, 128), col)],
        out_shape=[jax.ShapeDtypeStruct((S, POOL_W), BF16), jax.ShapeDtypeStruct((S, POOL_W), BF16),
                   jax.ShapeDtypeStruct((GROUPS, 128, 128), F32), jax.ShapeDtypeStruct((1, POOL_W), F32)],
        compiler_params=_cp(("parallel",)),
    )(da, proj, proj, pw, pscale)


def _chunk_cumsum(z, rowi):
    for sh in (1, 2, 4, 8, 16, 32):
        z = z + jnp.where(rowi >= sh, pltpu.roll(z, sh, axis=0), 0.0)
    return z


def _chunk_rev_cumsum(z, rowi):
    for sh in (1, 2, 4, 8, 16, 32):
        z = z + jnp.where(rowi < CH - sh, pltpu.roll(z, CH - sh, axis=0), 0.0)
    return z


def _dot_nn(a, b):
    return jnp.dot(a.astype(BF16), b.astype(BF16), preferred_element_type=F32)


def _dot_nt(a, b):
    return lax.dot_general(a.astype(BF16), b.astype(BF16), (((1,), (1,)), ((), ())), preferred_element_type=F32)


def _dot_tn(a, b):
    return lax.dot_general(a.astype(BF16), b.astype(BF16), (((0,), (0,)), ((), ())), preferred_element_type=F32)


def _gates(hq, hf, lbv):
    sq = _sig(hq)
    sf = _sig(hf)
    f = lbv + (1.0 - lbv) * sf
    fc = jnp.maximum(f, 1e-30)
    return hq * sq, sq, sf, f, fc, jnp.log(fc)


def _hgrn_fwd(proj, lb, gn, name):
    S = proj.shape[0]
    nch = S // CH

    def body(hq_ref, hf_ref, hi_ref, hg_ref, lb_ref, gn_ref, bin_ref, oraw_ref, st_ref,
             q_s, k_s, c_s, v_s, o_s, state_s):
        state_s[...] = jnp.zeros((HD, HD), F32)
        rowi = lax.broadcasted_iota(jnp.int32, (CH, 1), 0)
        coli = lax.broadcasted_iota(jnp.int32, (1, CH), 1)
        sbi = lax.broadcasted_iota(jnp.int32, (SB, 1), 0)
        lbv = lb_ref[...]
        gnv = gn_ref[...]

        def chunk(n, carry):
            r0 = pl.multiple_of(n * CH, CH)
            rows = pl.ds(r0, CH)
            q, _, _, f, _, logf = _gates(hq_ref[rows, :], hf_ref[rows, :], lbv)
            k = 1.0 - f
            v = hi_ref[rows, :]
            c = _chunk_cumsum(logf, rowi)
            q_s[...] = q
            k_s[...] = k
            c_s[...] = c
            v_s[...] = v
            st = state_s[...]
            st_ref[n] = st.astype(BF16)
            o = _dot_nt(q * jnp.exp(c), st)
            a_off = jnp.zeros((CH, CH), F32)
            for i in range(1, CH // SB):
                r_i = c_s[SB * i - 1:SB * i, :]
                qi = q * jnp.exp(jnp.minimum(c - r_i, 0.0))
                kei = k * jnp.exp(jnp.minimum(r_i - c, 0.0))
                m_i = (rowi >= SB * i) & (rowi < SB * (i + 1)) & (coli < SB * i)
                a_off = a_off + jnp.where(m_i, _dot_nt(qi, kei), 0.0)
            o_s[...] = o + _dot_nn(a_off, v)
            for i in range(CH // SB):
                blk = slice(SB * i, SB * (i + 1))
                qb = q_s[blk, :]
                cb = c_s[blk, :]
                acc = jnp.zeros((SB, HD), F32)
                for s in range(SB):
                    row = SB * i + s
                    w = jnp.exp(jnp.minimum(cb - c_s[row:row + 1, :], 0.0))
                    a_col = jnp.sum(qb * k_s[row:row + 1, :] * w, axis=-1, keepdims=True)
                    acc = acc + jnp.where(sbi >= s, a_col, 0.0) * v_s[row:row + 1, :]
                o_s[blk, :] += acc
            last = c_s[CH - 1:CH, :]
            state_s[...] = st * jnp.exp(last) + _dot_tn(v, k * jnp.exp(last - c))
            ov = o_s[...]
            oraw_ref[rows, :] = ov
            r = lax.rsqrt(jnp.mean(ov * ov, axis=-1, keepdims=True) + EPS)
            hg = hg_ref[rows, :]
            bin_ref[rows, :] = ((ov * r) * gnv * (hg * _sig(hg))).astype(BF16)
            return carry

        lax.fori_loop(0, nch, chunk, 0)

    col = lambda off: pl.BlockSpec((S, HD), lambda h: (0, off + h))
    head = pl.BlockSpec((S, HD), lambda h: (0, h))
    return pl.pallas_call(
        body, name=name, grid=(HEADS,),
        in_specs=[col(HQ0), col(HF0), col(HI0), col(HG0), pl.BlockSpec((1, HD), lambda h: (0, h)),
                  pl.BlockSpec((1, HD), lambda h: (0, 0))],
        out_specs=[head, head, pl.BlockSpec((None, nch, HD, HD), lambda h: (h, 0, 0, 0))],
        out_shape=[jax.ShapeDtypeStruct((S, D), BF16), jax.ShapeDtypeStruct((S, D), F32),
                   jax.ShapeDtypeStruct((HEADS, nch, HD, HD), BF16)],
        scratch_shapes=[pltpu.VMEM((CH, HD), F32)] * 5 + [pltpu.VMEM((HD, HD), F32)],
        compiler_params=_cp(("parallel",)),
    )(proj, proj, proj, proj, lb, gn)


def _hgrn_bwd(dbin, proj, oraw, states, lb, gn, name):
    S = proj.shape[0]
    nch = S // CH

    def body(db_ref, hq_ref, hf_ref, hi_ref, hg_ref, or_ref, st_ref, lb_ref, gn_ref,
             dq_ref, df_ref, di_ref, dg_ref, dlb_ref, dgn_ref,
             q_s, k_s, c_s, v_s, do_s, dq_s, dk_s, dv_s, dst_s, dlb_s, dgn_s):
        dst_s[...] = jnp.zeros((HD, HD), F32)
        dlb_s[...] = jnp.zeros((1, HD), F32)
        dgn_s[...] = jnp.zeros((1, HD), F32)
        rowi = lax.broadcasted_iota(jnp.int32, (CH, 1), 0)
        coli = lax.broadcasted_iota(jnp.int32, (1, CH), 1)
        rowi2 = lax.broadcasted_iota(jnp.int32, (CH, CH), 0)
        coli2 = lax.broadcasted_iota(jnp.int32, (CH, CH), 1)
        sbi = lax.broadcasted_iota(jnp.int32, (SB, 1), 0)
        lbv = lb_ref[...]
        gnv = gn_ref[...]

        def chunk(j, carry):
            n = nch - 1 - j
            r0 = pl.multiple_of(n * CH, CH)
            rows = pl.ds(r0, CH)
            hq = hq_ref[rows, :]
            q, sq, sf, f, fc, logf = _gates(hq, hf_ref[rows, :], lbv)
            k = 1.0 - f
            v = hi_ref[rows, :]
            c = _chunk_cumsum(logf, rowi)
            ov = or_ref[rows, :]
            hg = hg_ref[rows, :]
            sg = _sig(hg)
            r = lax.rsqrt(jnp.mean(ov * ov, axis=-1, keepdims=True) + EPS)
            dbv = db_ref[rows, :]
            d_on = dbv * (hg * sg)
            dg_ref[rows, :] = (dbv * ((ov * r) * gnv) * _dsilu(hg, sg)).astype(BF16)
            dgn_s[...] += jnp.sum(d_on * (ov * r), axis=0, keepdims=True)
            u = d_on * gnv
            do = r * u - ov * (r * r * r) * jnp.mean(u * ov, axis=-1, keepdims=True)
            q_s[...] = q
            k_s[...] = k
            c_s[...] = c
            v_s[...] = v
            do_s[...] = do
            st = st_ref[n].astype(F32)
            dst = dst_s[...]
            ec = jnp.exp(c)
            last = c_s[CH - 1:CH, :]
            el = jnp.exp(last - c)
            elast = jnp.exp(last)
            dq = _dot_nn(do, st) * ec
            dk = _dot_nn(v, dst) * el
            dv = _dot_nt(k * el, dst)
            dst_s[...] = dst * elast + _dot_tn(do, q * ec)
            dcum = q * dq - k * dk
            dlast = jnp.sum(k * dk, axis=0, keepdims=True) + elast * jnp.sum(st * dst, axis=0, keepdims=True)
            d_a = _dot_nt(do, v).astype(BF16).astype(F32)
            d_at = d_a.T
            at_off = jnp.zeros((CH, CH), F32)
            for i in range(1, CH // SB):
                r_i = c_s[SB * i - 1:SB * i, :]
                eq = jnp.exp(jnp.minimum(c - r_i, 0.0))
                ek = jnp.exp(jnp.minimum(r_i - c, 0.0))
                qi = (q * eq).astype(BF16).astype(F32)
                kei = (k * ek).astype(BF16).astype(F32)
                m_ts = (rowi2 >= SB * i) & (rowi2 < SB * (i + 1)) & (coli2 < SB * i)
                m_st = (coli2 >= SB * i) & (coli2 < SB * (i + 1)) & (rowi2 < SB * i)
                at_off = at_off + jnp.where(m_st, _dot_nt(kei, qi), 0.0)
                dq_i = _dot_nn(jnp.where(m_ts, d_a, 0.0), kei)
                dk_i = _dot_nn(jnp.where(m_st, d_at, 0.0), qi)
                dq = dq + dq_i * eq
                dk = dk + dk_i * ek
                dcum = dcum + (qi * dq_i - kei * dk_i)
            dv = dv + _dot_nn(at_off, do)
            dq_s[...] = jnp.zeros((CH, HD), F32)
            dk_s[...] = jnp.zeros((CH, HD), F32)
            dv_s[...] = dv
            for i in range(CH // SB):
                blk = slice(SB * i, SB * (i + 1))
                qb = q_s[blk, :]
                cb = c_s[blk, :]
                dob = do_s[blk, :]
                dq_acc = jnp.zeros((SB, HD), F32)
                for s in range(SB):
                    row = SB * i + s
                    ks = k_s[row:row + 1, :]
                    vs = v_s[row:row + 1, :]
                    w = jnp.exp(jnp.minimum(cb - c_s[row:row + 1, :], 0.0))
                    live = sbi >= s
                    a_col = jnp.where(live, jnp.sum(qb * ks * w, axis=-1, keepdims=True), 0.0)
                    da_col = jnp.where(live, jnp.sum(dob * vs, axis=-1, keepdims=True), 0.0)
                    dq_acc = dq_acc + da_col * ks * w
                    dk_s[row:row + 1, :] += jnp.sum(da_col * qb * w, axis=0, keepdims=True)
                    dv_s[row:row + 1, :] += jnp.sum(a_col * dob, axis=0, keepdims=True)
                dq_s[blk, :] += dq_acc
            dq_d = dq_s[...]
            dk_d = dk_s[...]
            dq = dq + dq_d
            dk = dk + dk_d
            dcum = dcum + (q * dq_d - k * dk_d)
            dlogf = _chunk_rev_cumsum(dcum, rowi) + dlast
            dfv = jnp.where(f > 1e-30, dlogf / fc, 0.0) - dk
            dlb_s[...] += jnp.sum(dfv * (1.0 - sf), axis=0, keepdims=True)
            df_ref[rows, :] = (dfv * (1.0 - lbv) * sf * (1.0 - sf)).astype(BF16)
            dq_ref[rows, :] = (dq * _dsilu(hq, sq)).astype(BF16)
            di_ref[rows, :] = dv_s[...].astype(BF16)
            return carry

        lax.fori_loop(0, nch, chunk, 0)
        dlb_ref[...] = dlb_s[...]
        dgn_ref[...] = jnp.broadcast_to(dgn_s[...], (8, HD))

    col = lambda off: pl.BlockSpec((S, HD), lambda h: (0, off + h))
    head = pl.BlockSpec((S, HD), lambda h: (0, h))
    vec = pl.BlockSpec((1, HD), lambda h: (0, h))
    outs = pl.pallas_call(
        body, name=name, grid=(HEADS,),
        in_specs=[head, col(HQ0), col(HF0), col(HI0), col(HG0), head,
                  pl.BlockSpec((None, nch, HD, HD), lambda h: (h, 0, 0, 0)), vec,
                  pl.BlockSpec((1, HD), lambda h: (0, 0))],
        out_specs=[head, head, head, head, vec, pl.BlockSpec((8, HD), lambda h: (h, 0))],
        out_shape=[jax.ShapeDtypeStruct((S, D), BF16)] * 4
        + [jax.ShapeDtypeStruct((1, D), F32), jax.ShapeDtypeStruct((8 * HEADS, HD), F32)],
        scratch_shapes=[pltpu.VMEM((CH, HD), F32)] * 8
        + [pltpu.VMEM((HD, HD), F32)] + [pltpu.VMEM((1, HD), F32)] * 2,
        compiler_params=_cp(("parallel",)),
    )(dbin, proj, proj, proj, proj, oraw, states, lb, gn)
    dq, df, di, dg, dlb, dgn = outs
    return dq, df, di, dg, dlb, dgn.reshape(HEADS, 8, HD)[:, 0, :]


def _lower_bounds(l0, l1):
    m = jnp.maximum(l0, l1)
    e0 = jnp.exp(l0 - m)
    e1 = jnp.exp(l1 - m)
    tot = e0 + e1
    p0 = e0 / tot
    p1 = e1 / tot
    return jnp.clip(p0 - p0, 0.0, 1.0), jnp.clip((p0 + p1) - p0, 0.0, 1.0)


def _lb_fwd(logits):
    def body(l_ref, o_ref):
        lb0, lb1 = _lower_bounds(l_ref[0:1, :], l_ref[1:2, :])
        o_ref[0:1, :] = lb0
        o_ref[1:2, :] = lb1

    return pl.pallas_call(body, name="lb_fwd", out_shape=jax.ShapeDtypeStruct((2, D), F32))(logits)


def _lb_bwd(logits, dlb):
    def body(l_ref, d_ref, o_ref):
        _, vjp = jax.vjp(_lower_bounds, l_ref[0:1, :], l_ref[1:2, :])
        g0, g1 = vjp((d_ref[0:1, :], d_ref[1:2, :]))
        o_ref[0:1, :] = g0
        o_ref[1:2, :] = g1

    return pl.pallas_call(body, name="lb_bwd", out_shape=jax.ShapeDtypeStruct((2, D), F32))(logits, dlb)


ADA_PAD = 128


def _ada_fwd(c_pad, w_ada, b_sh):
    ns = w_ada.shape[2]

    def body(c_ref, w_ref, b_ref, o_ref):
        cv = c_ref[...]
        ca = (cv * _sig(cv)).astype(BF16)
        for l in range(2):
            res = jnp.dot(ca, w_ref[l].astype(BF16), preferred_element_type=F32)
            o_ref[:, l * ns:(l + 1) * ns] = res[0:NDEV, :] + b_ref[l:l + 1, :]

    return pl.pallas_call(body, name="ada_fwd", out_shape=jax.ShapeDtypeStruct((NDEV, 2 * ns), F32),
                          compiler_params=_cp())(c_pad, w_ada, b_sh)


def _ada_wgrad(c_pad_t, d_ada_sh):
    ns = d_ada_sh.shape[2]

    def body(c_ref, d_ref, o_ref):
        cv = c_ref[...]
        ca = (cv * _sig(cv)).astype(BF16)
        for l in range(2):
            o_ref[l] = jnp.dot(ca, d_ref[l].astype(BF16), preferred_element_type=F32)

    return pl.pallas_call(body, name="ada_wgrad", out_shape=jax.ShapeDtypeStruct((2, D, ns), F32),
                          compiler_params=_cp())(c_pad_t, d_ada_sh)


def _sum_devices(g):
    _, R, C = g.shape

    def body(g_ref, o_ref):
        acc = g_ref[0]
        for d in range(1, NDEV):
            acc = acc + g_ref[d]
        o_ref[...] = acc

    return pl.pallas_call(body, name="sum_devices", out_shape=jax.ShapeDtypeStruct((R, C), F32),
                          compiler_params=_cp())(g)


def _adamw(w, g, m, v, name):
    R, C = w.shape
    tr = _row_tile(R, max(8, (1 << 19) // C))

    def body(w_ref, g_ref, m_ref, v_ref, d_ref, nm_ref, nv_ref):
        gv = g_ref[...]
        nm = B1 * m_ref[...] + (1.0 - B1) * gv
        nv = B2 * v_ref[...] + (1.0 - B2) * (gv * gv)
        m_hat = nm / (1.0 - B1 ** STEP)
        v_hat = nv / (1.0 - B2 ** STEP)
        d_ref[...] = -LR * (m_hat / (jnp.sqrt(v_hat) + AEPS) + WD * w_ref[...])
        nm_ref[...] = nm
        nv_ref[...] = nv

    tile = pl.BlockSpec((tr, C), lambda i: (i, 0))
    return pl.pallas_call(
        body, name=name, grid=(R // tr,), in_specs=[tile] * 4, out_specs=[tile] * 3,
        out_shape=[jax.ShapeDtypeStruct((R, C), F32)] * 3, compiler_params=_cp(("parallel",)),
    )(w, g, m, v)


def _add2(a, b, out_dtype, name):
    R, C = a.shape
    tr = _row_tile(R, max(8, (1 << 19) // C))

    def body(a_ref, b_ref, o_ref):
        o_ref[...] = (a_ref[...].astype(F32) + b_ref[...].astype(F32)).astype(o_ref.dtype)

    tile = pl.BlockSpec((tr, C), lambda i: (i, 0))
    return pl.pallas_call(body, name=name, grid=(R // tr,), in_specs=[tile, tile], out_specs=tile,
                          out_shape=jax.ShapeDtypeStruct((R, C), out_dtype),
                          compiler_params=_cp(("parallel",)))(a, b)


def _sum4(a, name):
    _, R, C = a.shape
    tr = _row_tile(R, max(8, (1 << 18) // C))

    def body(a_ref, o_ref):
        acc = a_ref[0].astype(F32)
        for j in range(1, NCHIP):
            acc = acc + a_ref[j].astype(F32)
        o_ref[...] = acc

    return pl.pallas_call(body, name=name, grid=(R // tr,),
                          in_specs=[pl.BlockSpec((NCHIP, tr, C), lambda i: (0, i, 0))],
                          out_specs=pl.BlockSpec((tr, C), lambda i: (i, 0)),
                          out_shape=jax.ShapeDtypeStruct((R, C), F32), compiler_params=_cp(("parallel",)))(a)


ANY = pl.BlockSpec(memory_space=pl.ANY)


def _place():
    x, y, c = lax.axis_index("x"), lax.axis_index("y"), lax.axis_index("c")
    chips = [(1 - x, y), (x, 1 - y), (1 - x, 1 - y)]
    return x, y, c, chips


def _gather_small(blk, name):
    m_per, n = blk.shape

    def body(x_ref, out_ref, send_sems, recv_sems, local_sem):
        x, y, c, chips = _place()
        me, sibling = (x, y, c), (x, y, 1 - c)

        def rows(px, py, pc):
            return out_ref.at[pl.ds((4 * px + 2 * py + pc) * m_per, m_per), :]

        def copy(k, block, to, src=None):
            return pltpu.make_async_remote_copy(
                src_ref=rows(*block) if src is None else src, dst_ref=rows(*block),
                send_sem=send_sems.at[k], recv_sem=recv_sems.at[k], device_id=to, device_id_type=MESH)

        mine = pltpu.make_async_copy(x_ref, rows(*me), local_sem)
        mine.start()
        first = [copy(0, me, sibling, src=x_ref)]
        first += [copy(1 + j, me, (*chip, c), src=x_ref) for j, chip in enumerate(chips)]
        for cp in first:
            cp.start()
        passed = [copy(4 + j, (*chip, c), sibling) for j, chip in enumerate(chips)]
        for j, chip in enumerate(chips):
            copy(1 + j, (*chip, c), me).wait_recv()
            passed[j].start()
        copy(0, sibling, me).wait_recv()
        for j, chip in enumerate(chips):
            copy(4 + j, (*chip, 1 - c), me).wait_recv()
        for cp in first + passed:
            cp.wait_send()
        mine.wait()

    return pl.pallas_call(
        body, name=name, out_shape=jax.ShapeDtypeStruct((NDEV * m_per, n), blk.dtype),
        in_specs=[pl.BlockSpec(memory_space=pltpu.VMEM)], out_specs=pl.BlockSpec(memory_space=pltpu.VMEM),
        scratch_shapes=[pltpu.SemaphoreType.DMA((7,)), pltpu.SemaphoreType.DMA((7,)), pltpu.SemaphoreType.DMA],
        compiler_params=_cp(),
    )(blk)


def _gather_weights(shards):
    n = len(shards)

    def body(*refs):
        ins, outs = refs[:n], refs[n:2 * n]
        send_sems, recv_sems, local_sems = refs[2 * n:]
        x, y, c, chips = _place()
        me = 2 * x + y
        sends = []
        for a in range(n):
            pltpu.make_async_copy(ins[a], outs[a].at[me], local_sems.at[a]).start()
        for a in range(n):
            r2 = ins[a].shape[0] // 2
            half = pl.ds(c * r2, r2)
            for j, (cx, cy) in enumerate(chips):
                cp = pltpu.make_async_remote_copy(
                    src_ref=ins[a].at[half, :], dst_ref=outs[a].at[me, half, :],
                    send_sem=send_sems.at[a, j], recv_sem=recv_sems.at[a, j],
                    device_id=(cx, cy, c), device_id_type=MESH)
                cp.start()
                sends.append(cp)
        for a in range(n):
            r2 = ins[a].shape[0] // 2
            half = pl.ds(c * r2, r2)
            for j, (cx, cy) in enumerate(chips):
                blk = outs[a].at[2 * cx + cy, half, :]
                pltpu.make_async_remote_copy(
                    src_ref=blk, dst_ref=blk, send_sem=send_sems.at[a, j], recv_sem=recv_sems.at[a, j],
                    device_id=(cx, cy, c), device_id_type=MESH).wait_recv()
                fw = pltpu.make_async_remote_copy(
                    src_ref=blk, dst_ref=blk, send_sem=send_sems.at[a, 3 + j], recv_sem=recv_sems.at[a, 3 + j],
                    device_id=(x, y, 1 - c), device_id_type=MESH)
                fw.start()
                sends.append(fw)
        for a in range(n):
            r2 = ins[a].shape[0] // 2
            other = pl.ds((1 - c) * r2, r2)
            for j, (cx, cy) in enumerate(chips):
                blk = outs[a].at[2 * cx + cy, other, :]
                pltpu.make_async_remote_copy(
                    src_ref=blk, dst_ref=blk, send_sem=send_sems.at[a, 3 + j], recv_sem=recv_sems.at[a, 3 + j],
                    device_id=(x, y, 1 - c), device_id_type=MESH).wait_recv()
        for cp in sends:
            cp.wait_send()
        for a in range(n):
            pltpu.make_async_copy(ins[a], outs[a].at[me], local_sems.at[a]).wait()

    return pl.pallas_call(
        body, name="gather_weights",
        out_shape=[jax.ShapeDtypeStruct((NCHIP,) + s.shape, s.dtype) for s in shards],
        in_specs=[ANY] * n, out_specs=[ANY] * n,
        scratch_shapes=[pltpu.SemaphoreType.DMA((n, 6)), pltpu.SemaphoreType.DMA((n, 6)),
                        pltpu.SemaphoreType.DMA((n,))],
        compiler_params=_cp(),
    )(*shards)


def _rs_pair(grads):
    n = len(grads)

    def body(*refs):
        ins, owns, gots = refs[:n], refs[n:2 * n], refs[2 * n:3 * n]
        send_sems, recv_sems, local_sems = refs[3 * n:]
        x, y, c, _ = _place()
        cps, lcs = [], []
        for a in range(n):
            r2 = ins[a].shape[1] // 2
            lc = pltpu.make_async_copy(ins[a].at[:, pl.ds(c * r2, r2), :], owns[a], local_sems.at[a])
            lc.start()
            lcs.append(lc)
            cp = pltpu.make_async_remote_copy(
                src_ref=ins[a].at[:, pl.ds((1 - c) * r2, r2), :], dst_ref=gots[a],
                send_sem=send_sems.at[a], recv_sem=recv_sems.at[a],
                device_id=(x, y, 1 - c), device_id_type=MESH)
            cp.start()
            cps.append(cp)
        for cp in cps:
            cp.wait()
        for lc in lcs:
            lc.wait()

    half = [jax.ShapeDtypeStruct((NCHIP, g.shape[1] // 2, g.shape[2]), g.dtype) for g in grads]
    outs = pl.pallas_call(
        body, name="rs_pair", out_shape=half + half, in_specs=[ANY] * n, out_specs=[ANY] * (2 * n),
        scratch_shapes=[pltpu.SemaphoreType.DMA((n,)), pltpu.SemaphoreType.DMA((n,)),
                        pltpu.SemaphoreType.DMA((n,))],
        compiler_params=_cp(),
    )(*grads)
    return outs[:n], outs[n:]


def _rs_chips(parts):
    n = len(parts)

    def body(*refs):
        ins, outs = refs[:n], refs[n:2 * n]
        send_sems, recv_sems, local_sems = refs[2 * n:]
        x, y, c, chips = _place()
        me = 2 * x + y
        cps, lcs = [], []
        for a in range(n):
            lc = pltpu.make_async_copy(ins[a].at[me], outs[a].at[me], local_sems.at[a])
            lc.start()
            lcs.append(lc)
            for j, (cx, cy) in enumerate(chips):
                cp = pltpu.make_async_remote_copy(
                    src_ref=ins[a].at[2 * cx + cy], dst_ref=outs[a].at[me],
                    send_sem=send_sems.at[a, j], recv_sem=recv_sems.at[a, j],
                    device_id=(cx, cy, c), device_id_type=MESH)
                cp.start()
                cps.append(cp)
        for a in range(n):
            for j, (cx, cy) in enumerate(chips):
                blk = outs[a].at[2 * cx + cy]
                pltpu.make_async_remote_copy(
                    src_ref=blk, dst_ref=blk, send_sem=send_sems.at[a, j], recv_sem=recv_sems.at[a, j],
                    device_id=(cx, cy, c), device_id_type=MESH).wait_recv()
        for cp in cps:
            cp.wait_send()
        for lc in lcs:
            lc.wait()

    return pl.pallas_call(
        body, name="rs_chips", out_shape=[jax.ShapeDtypeStruct(p.shape, p.dtype) for p in parts],
        in_specs=[ANY] * n, out_specs=[ANY] * n,
        scratch_shapes=[pltpu.SemaphoreType.DMA((n, 3)), pltpu.SemaphoreType.DMA((n, 3)),
                        pltpu.SemaphoreType.DMA((n,))],
        compiler_params=_cp(),
    )(*parts)


def _rs_swap(halves):
    flat = [h for pair in halves for h in pair]
    n = len(flat)

    def body(*refs):
        ins, outs = refs[:n], refs[n:n + n // 2]
        send_sems, recv_sems, local_sems = refs[n + n // 2:]
        x, y, c, _ = _place()
        cps, lcs = [], []
        for a in range(n):
            r2 = ins[a].shape[0]
            o = outs[a // 2]
            lc = pltpu.make_async_copy(ins[a], o.at[a % 2, pl.ds(c * r2, r2), :], local_sems.at[a])
            lc.start()
            lcs.append(lc)
            cp = pltpu.make_async_remote_copy(
                src_ref=ins[a], dst_ref=o.at[a % 2, pl.ds(c * r2, r2), :],
                send_sem=send_sems.at[a], recv_sem=recv_sems.at[a],
                device_id=(x, y, 1 - c), device_id_type=MESH)
            cp.start()
            cps.append(cp)
        for a in range(n):
            r2 = ins[a].shape[0]
            o = outs[a // 2]
            blk = o.at[a % 2, pl.ds((1 - c) * r2, r2), :]
            pltpu.make_async_remote_copy(
                src_ref=blk, dst_ref=blk, send_sem=send_sems.at[a], recv_sem=recv_sems.at[a],
                device_id=(x, y, 1 - c), device_id_type=MESH).wait_recv()
        for cp in cps:
            cp.wait_send()
        for lc in lcs:
            lc.wait()

    return pl.pallas_call(
        body, name="rs_swap",
        out_shape=[jax.ShapeDtypeStruct((2, 2 * p[0].shape[0], p[0].shape[1]), F32) for p in halves],
        in_specs=[ANY] * n, out_specs=[ANY] * (n // 2),
        scratch_shapes=[pltpu.SemaphoreType.DMA((n,)), pltpu.SemaphoreType.DMA((n,)),
                        pltpu.SemaphoreType.DMA((n,))],
        compiler_params=_cp(),
    )(*flat)


def _layer_fwd(l, x, ada, w, small):
    shift, scale, gate = ada[:, 0:D], ada[:, D:2 * D], ada[:, 2 * D:3 * D]
    h = _prenorm_fwd(x, small["g_pre"][l], scale, shift, f"prenorm_fwd{l}")
    proj = _mm(h, w["w_in"][l], name=f"proj{l}", b_mode="nn_sh")
    a_in = _pool_fwd(proj, small["pool_w"][l], small["pool_scale"][l], f"pool_fwd{l}")
    b_in, o_raw, states = _hgrn_fwd(proj, small["lb"][l], small["hgrn_norm_g"][l], f"hgrn_fwd{l}")
    br_a = _mm(a_in, w["w_pool_o"][l], name=f"branch_a{l}", b_mode="nn_sh")
    br_b = _mm(b_in, w["w_hgrn_o"][l].reshape(D, D), name=f"branch_b{l}")
    br = jnp.stack([br_a, br_b])
    merged = _merge_fwd(proj, br, f"merge_fwd{l}")
    y = _mm(merged, w["w_out"][l].reshape(D, D), name=f"out_proj{l}")
    x_new = _postnorm_fwd(x, y, gate, small["g_post"][l], f"postnorm_fwd{l}")
    saved = dict(x=x, h=h, proj=proj, a_in=a_in, b_in=b_in, o_raw=o_raw, states=states, br=br,
                 merged=merged, y=y, scale=scale, gate=gate)
    return x_new, saved


def _layer_bwd(l, dxn, sv, w, small):
    dy, dgate, dg_post = _postnorm_bwd(dxn, sv["y"], sv["gate"], small["g_post"][l], f"postnorm_bwd{l}")
    w_out = w["w_out"][l].reshape(D, D)
    dmerged = _mm(dy, w_out, name=f"d_merged{l}", b_mode="nt")
    gw_out = _mm(sv["merged"].T, dy, name=f"gw_out{l}", out_dtype=BF16)
    dbr, dmg = _merge_bwd(dmerged, sv["proj"], sv["br"], f"merge_bwd{l}")
    da_in = _mm(dbr[0], w["w_pool_o"][l], name=f"d_a_in{l}", b_mode="nt_shk")
    gw_pool_o = _mm(sv["a_in"].T, dbr[0], name=f"gw_pool_o{l}", out_shards=NCHIP, out_dtype=BF16)
    db_in = _mm(dbr[1], w["w_hgrn_o"][l].reshape(D, D), name=f"d_b_in{l}", b_mode="nt")
    gw_hgrn_o = _mm(sv["b_in"].T, dbr[1], name=f"gw_hgrn_o{l}", out_dtype=BF16)
    dhq, dhf, dhi, dhg, dlb, dgn = _hgrn_bwd(db_in, sv["proj"], sv["o_raw"], sv["states"], small["lb"][l],
                                             small["hgrn_norm_g"][l], f"hgrn_bwd{l}")
    dpv, dpg, dpw, dpsc = _pool_bwd(da_in, sv["proj"], small["pool_w"][l], small["pool_scale"][l],
                                    f"pool_bwd{l}")
    dproj = jnp.concatenate([dpv, dpg, dhq, dhf, dhi, dhg, dmg], axis=1)
    dh = _mm(dproj, w["w_in"][l], name=f"d_h{l}", b_mode="nt_shk")
    gw_in = _mm(sv["h"].T, dproj, name=f"gw_in{l}", out_shards=NCHIP, out_dtype=BF16)
    dx, dshift, dscale, dg_pre = _prenorm_bwd(dh, dxn, sv["x"], small["g_pre"][l], sv["scale"],
                                              f"prenorm_bwd{l}")
    big = dict(w_in=gw_in, w_pool_o=gw_pool_o, w_hgrn_o=gw_hgrn_o.reshape(NCHIP, D // NCHIP, D),
               w_out=gw_out.reshape(NCHIP, D // NCHIP, D))
    little = dict(d_ada=jnp.concatenate([dshift, dscale, dgate], axis=1), g_pre=dg_pre, g_post=dg_post,
                  pool_w=dpw, pool_scale=dpsc, lb=dlb, hgrn_norm_g=jnp.sum(dgn, axis=0, keepdims=True))
    return dx, big, little


SMALL_ROWS = 176


def _rows8(t):
    t = t.reshape(-1, D)
    return jnp.pad(t, ((0, -t.shape[0] % 8), (0, 0)))


def _pack_small_weights(b_ada, g_pre, g_post, lb_logits, pool_w, pool_scale, hgrn_norm_g):
    gn = jnp.pad(hgrn_norm_g.reshape(1, 2 * HD), ((0, 0), (0, D - 2 * HD)))
    return jnp.concatenate([_rows8(b_ada), _rows8(g_pre), _rows8(g_post), _rows8(lb_logits), _rows8(pool_w),
                            _rows8(pool_scale), _rows8(gn)], axis=0)


def _pack_small(parts):
    both = lambda key: jnp.stack([parts[l][key] for l in range(2)])
    return _pack_small_weights(both("d_ada"), both("g_pre"), both("g_post"), both("lb"), both("pool_w"),
                               both("pool_scale"), both("hgrn_norm_g"))


def _unpack_small(p):
    return (p[0:6].reshape(2, 3 * D), p[8:10], p[16:18], p[24:26], p[32:160].reshape(2, GROUPS, 128, 128),
            p[160:161].reshape(2, POOL_W), p[168:169, 0:2 * HD].reshape(2, HD))


def kernel(x, c, w_ada, b_ada, g_pre, g_post, w_in, pool_w, pool_scale, lb_logits, hgrn_norm_g, w_pool_o, w_hgrn_o, w_out, loss_target, m_w_ada, m_b_ada, m_g_pre, m_g_post, m_w_in, m_pool_w, m_pool_scale, m_lb_logits, m_hgrn_norm_g, m_w_pool_o, m_w_hgrn_o, m_w_out, v_w_ada, v_b_ada, v_g_pre, v_g_post, v_w_in, v_pool_w, v_pool_scale, v_lb_logits, v_hgrn_norm_g, v_w_pool_o, v_w_hgrn_o, v_w_out):
    ax, ay, ac = lax.axis_index("x"), lax.axis_index("y"), lax.axis_index("c")
    chip = 2 * ax + ay
    dev = 2 * chip + ac
    xe, te = x[0], loss_target[0]
    ada_s = w_ada.shape[2]

    big_names = ("w_in", "w_pool_o", "w_hgrn_o", "w_out")
    big_w = (w_in, w_pool_o, w_hgrn_o, w_out)
    gathered = _gather_weights([t[l].astype(BF16) for l in range(2) for t in big_w])
    w = {k: (gathered[i], gathered[4 + i]) for i, k in enumerate(big_names)}

    c_all = _gather_small(jnp.broadcast_to(c, (8, D)), "gather_c").reshape(NDEV, 8, D)[:, 0, :]
    c_pad = jnp.pad(c_all, ((0, ADA_PAD - NDEV), (0, 0)))
    b_sh = lax.dynamic_slice(b_ada, (0, chip * ada_s), (2, ada_s))
    ada_cols = _gather_small(_ada_fwd(c_pad, w_ada, b_sh), "gather_ada")
    ada_cols = ada_cols.reshape(NCHIP, 2, NDEV, 2, ada_s)[:, 0]
    ada_all = jnp.transpose(ada_cols, (2, 1, 0, 3)).reshape(2, NDEV, 3 * D)
    ada_me = lax.dynamic_slice(ada_all, (0, dev, 0), (2, 1, 3 * D))

    lbs = _lb_fwd(lb_logits)
    small = dict(g_pre=g_pre[:, None, :], g_post=g_post[:, None, :], pool_w=pool_w,
                 pool_scale=pool_scale[:, None, :], lb=lbs[:, None, :], hgrn_norm_g=hgrn_norm_g[:, None, :])

    x1, sv0 = _layer_fwd(0, xe, ada_me[0], w, small)
    x2, sv1 = _layer_fwd(1, x1, ada_me[1], w, small)
    dx2, loss_blk = _loss_head(x2, te, "loss_head")
    dx1, big1, little1 = _layer_bwd(1, dx2, sv1, w, small)
    dx0, big0, little0 = _layer_bwd(0, dx1, sv0, w, small)
    loss = lax.psum(loss_blk[0, 0], ("x", "y", "c"))

    grads = [big0[k] for k in big_names] + [big1[k] for k in big_names]
    own, got = _rs_pair(grads)
    parts = [_add2(o.reshape(-1, o.shape[2]), g.reshape(-1, g.shape[2]), BF16, f"rs_add{i}").reshape(o.shape)
             for i, (o, g) in enumerate(zip(own, got))]
    recv = _rs_chips(parts)
    red = [_sum4(r, f"rs_sum{i}") for i, r in enumerate(recv)]
    g_big = _rs_swap([(red[i], red[4 + i]) for i in range(4)])
    g_big = dict(zip(big_names, g_big))

    packed = _gather_small(_pack_small([little0, little1]), "gather_small")
    packed = packed.reshape(NDEV, SMALL_ROWS, D)
    g_small = _sum_devices(packed)
    g_b_ada, g_g_pre, g_g_post, g_lb, g_pool_w, g_pool_scale, g_norm_g = _unpack_small(g_small)
    g_lb_logits = _lb_bwd(lb_logits, g_lb)
    d_ada_all = packed[:, 0:6, :].reshape(NDEV, 2, 3 * D)
    d_ada_sh = lax.dynamic_slice(jnp.transpose(d_ada_all, (1, 0, 2)), (0, 0, chip * ada_s), (2, NDEV, ada_s))
    d_ada_sh = jnp.pad(d_ada_sh, ((0, 0), (0, ADA_PAD - NDEV), (0, 0)))
    g_w_ada = _ada_wgrad(c_pad.T, d_ada_sh)

    def upd(wt, g, m, v, name):
        shp = wt.shape
        two = lambda t: t.reshape(-1, shp[-1])
        d, nm, nv = _adamw(two(wt), two(g), two(m), two(v), name)
        return d.reshape(shp), nm.reshape(shp), nv.reshape(shp)

    u_w_ada = upd(w_ada, g_w_ada, m_w_ada, v_w_ada, "adamw_w_ada")
    u_w_in = upd(w_in, g_big["w_in"], m_w_in, v_w_in, "adamw_w_in")
    u_w_pool_o = upd(w_pool_o, g_big["w_pool_o"], m_w_pool_o, v_w_pool_o, "adamw_w_pool_o")
    u_w_hgrn_o = upd(w_hgrn_o, g_big["w_hgrn_o"], m_w_hgrn_o, v_w_hgrn_o, "adamw_w_hgrn_o")
    u_w_out = upd(w_out, g_big["w_out"], m_w_out, v_w_out, "adamw_w_out")
    g_small_fixed = _pack_small_weights(g_b_ada, g_g_pre, g_g_post, g_lb_logits, g_pool_w, g_pool_scale,
                                        g_norm_g)
    sw = _pack_small_weights(b_ada, g_pre, g_post, lb_logits, pool_w, pool_scale, hgrn_norm_g)
    sm = _pack_small_weights(m_b_ada, m_g_pre, m_g_post, m_lb_logits, m_pool_w, m_pool_scale, m_hgrn_norm_g)
    sv = _pack_small_weights(v_b_ada, v_g_pre, v_g_post, v_lb_logits, v_pool_w, v_pool_scale, v_hgrn_norm_g)
    u_small = [_unpack_small(t) for t in _adamw(sw, g_small_fixed, sm, sv, "adamw_small")]

    grads_out = (g_w_ada, g_b_ada, g_g_pre, g_g_post, g_big["w_in"], g_pool_w, g_pool_scale, g_lb_logits,
                 g_norm_g, g_big["w_pool_o"], g_big["w_hgrn_o"], g_big["w_out"])

    def ordered(k):
        s = u_small[k]
        return (u_w_ada[k], s[0], s[1], s[2], u_w_in[k], s[4], s[5], s[3], s[6], u_w_pool_o[k], u_w_hgrn_o[k],
                u_w_out[k])

    return (loss, dx0[None], *grads_out, *ordered(0), *ordered(1), *ordered(2))
```

```python
import functools

import jax
import jax.numpy as jnp
from jax import lax
from jax.experimental import pallas as pl
from jax.experimental.pallas import tpu as pltpu

F32 = jnp.float32
BF16 = jnp.bfloat16
MESH = pl.DeviceIdType.MESH

D = 1024
HEADS = 8
HD = 128
GROUPS = 4
POOL_W = 512
WINDOWS = (2, 4, 8, 16)
CH = 64
SB = 16
IN_W = 7168
NCHIP = 4
NDEV = 8
EPS = 1e-6
PV0, PG0, HQ0, HF0, HI0, HG0 = 0, 4, 8, 16, 24, 32
MGP_BLK, MGH_BLK = 5, 6

LR, B1, B2, AEPS, WD, STEP = 0.001, 0.9, 0.999, 1e-08, 0.01, 10
VMEM_LIMIT = 56 * 1024 * 1024


def _cp(sem=None, **kw):
    if sem is not None:
        kw["dimension_semantics"] = sem
    return pltpu.CompilerParams(vmem_limit_bytes=VMEM_LIMIT, **kw)


def _sig(z):
    return 1.0 / (1.0 + jnp.exp(-z))


def _dsilu(z, s):
    return s * (1.0 + z * (1.0 - s))


def _row_tile(rows, cap):
    if rows <= cap:
        return rows
    t = 1 << (cap.bit_length() - 1)
    while rows % t:
        t //= 2
    return t


def _mm(a, b, *, name, b_mode="nn", out_shards=0, tm=1024, tn=256, tk=None, out_dtype=F32):
    M, K = a.shape
    if b_mode == "nn":
        N = b.shape[1]
    elif b_mode == "nt":
        N = b.shape[0]
    elif b_mode == "nn_sh":
        N = b.shape[0] * b.shape[2]
    else:
        N = b.shape[1]
    tm = _row_tile(M, tm)
    if b_mode == "nn_sh":
        tn = _row_tile(b.shape[2], tn)
    elif out_shards:
        tn = _row_tile(N // out_shards, tn)
    else:
        tn = _row_tile(N, tn)
    if tk is None:
        tk = K if b_mode != "nt_shk" else b.shape[2]
    if b_mode == "nt_shk":
        tk = _row_tile(b.shape[2], tk)
    nm, nn, nk = M // tm, N // tn, K // tk

    a_spec = pl.BlockSpec((tm, tk), lambda m, n, k: (m, k))
    if b_mode == "nn":
        b_spec = pl.BlockSpec((tk, tn), lambda m, n, k: (k, n))
    elif b_mode == "nt":
        b_spec = pl.BlockSpec((tn, tk), lambda m, n, k: (n, k))
    elif b_mode == "nn_sh":
        nps = b.shape[2] // tn
        b_spec = pl.BlockSpec((None, tk, tn), lambda m, n, k: (n // nps, k, n % nps))
    else:
        kps = b.shape[2] // tk
        b_spec = pl.BlockSpec((None, tn, tk), lambda m, n, k: (k // kps, n, k % kps))
    if out_shards:
        ops = (N // out_shards) // tn
        o_spec = pl.BlockSpec((None, tm, tn), lambda m, n, k: (n // ops, m, n % ops))
        o_shape = jax.ShapeDtypeStruct((out_shards, M, N // out_shards), out_dtype)
    else:
        o_spec = pl.BlockSpec((tm, tn), lambda m, n, k: (m, n))
        o_shape = jax.ShapeDtypeStruct((M, N), out_dtype)
    trans_b = b_mode in ("nt", "nt_shk")
    dn = (((1,), (1,)), ((), ())) if trans_b else (((1,), (0,)), ((), ()))

    def body(a_ref, b_ref, o_ref, acc_ref):
        k = pl.program_id(2)

        @pl.when(k == 0)
        def _():
            acc_ref[...] = jnp.zeros(acc_ref.shape, F32)

        acc_ref[...] += lax.dot_general(a_ref[...].astype(BF16), b_ref[...].astype(BF16), dn,
                                        preferred_element_type=F32)

        @pl.when(k == nk - 1)
        def _():
            o_ref[...] = acc_ref[...].astype(o_ref.dtype)

    return pl.pallas_call(
        body, name=name, grid=(nm, nn, nk), in_specs=[a_spec, b_spec], out_specs=o_spec,
        out_shape=o_shape, scratch_shapes=[pltpu.VMEM((tm, tn), F32)],
        compiler_params=_cp(("parallel", "parallel", "arbitrary")),
    )(a, b)


def _rowvec(n=D):
    return pl.BlockSpec((1, n), lambda i: (0, 0))


def _prenorm_fwd(x, g, scale, shift, name):
    S = x.shape[0]
    tr = _row_tile(S, 256)

    def body(x_ref, g_ref, sc_ref, sh_ref, h_ref):
        xv = x_ref[...]
        r = lax.rsqrt(jnp.mean(xv * xv, axis=-1, keepdims=True) + EPS)
        h_ref[...] = ((xv * r) * g_ref[...] * (1.0 + sc_ref[...]) + sh_ref[...]).astype(BF16)

    return pl.pallas_call(
        body, name=name, grid=(S // tr,),
        in_specs=[pl.BlockSpec((tr, D), lambda i: (i, 0)), _rowvec(), _rowvec(), _rowvec()],
        out_specs=pl.BlockSpec((tr, D), lambda i: (i, 0)),
        out_shape=jax.ShapeDtypeStruct((S, D), BF16), compiler_params=_cp(("parallel",)),
    )(x, g, scale, shift)


def _prenorm_bwd(dh, dxn, x, g, scale, name):
    S = x.shape[0]
    tr = _row_tile(S, 256)

    def body(dh_ref, dxn_ref, x_ref, g_ref, sc_ref, dx_ref, dsh_ref, dsc_ref, dg_ref):
        i = pl.program_id(0)

        @pl.when(i == 0)
        def _():
            dsh_ref[...] = jnp.zeros((1, D), F32)
            dsc_ref[...] = jnp.zeros((1, D), F32)
            dg_ref[...] = jnp.zeros((1, D), F32)

        xv = x_ref[...]
        dhv = dh_ref[...]
        gv = g_ref[...]
        mod = 1.0 + sc_ref[...]
        r = lax.rsqrt(jnp.mean(xv * xv, axis=-1, keepdims=True) + EPS)
        xh = xv * r
        dsh_ref[...] += jnp.sum(dhv, axis=0, keepdims=True)
        dsc_ref[...] += jnp.sum(dhv * (xh * gv), axis=0, keepdims=True)
        dg_ref[...] += jnp.sum(dhv * mod * xh, axis=0, keepdims=True)
        u = dhv * mod * gv
        dx_ref[...] = dxn_ref[...] + r * u - xv * (r * r * r) * jnp.mean(u * xv, axis=-1, keepdims=True)

    tile = pl.BlockSpec((tr, D), lambda i: (i, 0))
    return pl.pallas_call(
        body, name=name, grid=(S // tr,),
        in_specs=[tile, tile, tile, _rowvec(), _rowvec()],
        out_specs=[tile, _rowvec(), _rowvec(), _rowvec()],
        out_shape=[jax.ShapeDtypeStruct((S, D), F32)] + [jax.ShapeDtypeStruct((1, D), F32)] * 3,
        compiler_params=_cp(("arbitrary",)),
    )(dh, dxn, x, g, scale)


def _postnorm_fwd(x, y, gate, g, name):
    S = x.shape[0]
    tr = _row_tile(S, 256)

    def body(x_ref, y_ref, gate_ref, g_ref, o_ref):
        yv = y_ref[...]
        r = lax.rsqrt(jnp.mean(yv * yv, axis=-1, keepdims=True) + EPS)
        o_ref[...] = x_ref[...] + gate_ref[...] * ((yv * r) * g_ref[...])

    tile = pl.BlockSpec((tr, D), lambda i: (i, 0))
    return pl.pallas_call(
        body, name=name, grid=(S // tr,), in_specs=[tile, tile, _rowvec(), _rowvec()],
        out_specs=tile, out_shape=jax.ShapeDtypeStruct((S, D), F32), compiler_params=_cp(("parallel",)),
    )(x, y, gate, g)


def _postnorm_bwd(dxn, y, gate, g, name):
    S = y.shape[0]
    tr = _row_tile(S, 256)

    def body(dxn_ref, y_ref, gate_ref, g_ref, dy_ref, dgate_ref, dg_ref):
        i = pl.program_id(0)

        @pl.when(i == 0)
        def _():
            dgate_ref[...] = jnp.zeros((1, D), F32)
            dg_ref[...] = jnp.zeros((1, D), F32)

        yv = y_ref[...]
        dv = dxn_ref[...]
        gv = g_ref[...]
        gt = gate_ref[...]
        r = lax.rsqrt(jnp.mean(yv * yv, axis=-1, keepdims=True) + EPS)
        yh = yv * r
        dgate_ref[...] += jnp.sum(dv * (yh * gv), axis=0, keepdims=True)
        dg_ref[...] += jnp.sum(dv * gt * yh, axis=0, keepdims=True)
        u = dv * gt * gv
        dy_ref[...] = (r * u - yv * (r * r * r) * jnp.mean(u * yv, axis=-1, keepdims=True)).astype(BF16)

    tile = pl.BlockSpec((tr, D), lambda i: (i, 0))
    return pl.pallas_call(
        body, name=name, grid=(S // tr,), in_specs=[tile, tile, _rowvec(), _rowvec()],
        out_specs=[tile, _rowvec(), _rowvec()],
        out_shape=[jax.ShapeDtypeStruct((S, D), BF16), jax.ShapeDtypeStruct((1, D), F32),
                   jax.ShapeDtypeStruct((1, D), F32)],
        compiler_params=_cp(("arbitrary",)),
    )(dxn, y, gate, g)


def _loss_head(xo, target, name):
    S = xo.shape[0]
    tr = _row_tile(S, 256)

    def body(x_ref, t_ref, dx_ref, l_ref):
        i = pl.program_id(0)

        @pl.when(i == 0)
        def _():
            l_ref[...] = jnp.zeros((8, 128), F32)

        err = x_ref[...] - t_ref[...]
        dx_ref[...] = err * (1.0 / D)
        l_ref[...] += 0.5 * jnp.sum(jnp.mean(err * err, axis=-1, keepdims=True))

    tile = pl.BlockSpec((tr, D), lambda i: (i, 0))
    return pl.pallas_call(
        body, name=name, grid=(S // tr,), in_specs=[tile, tile],
        out_specs=[tile, pl.BlockSpec((8, 128), lambda i: (0, 0))],
        out_shape=[jax.ShapeDtypeStruct((S, D), F32), jax.ShapeDtypeStruct((8, 128), F32)],
        compiler_params=_cp(("arbitrary",)),
    )(xo, target)


def _merge_fwd(proj, br, name):
    S = proj.shape[0]
    tr = _row_tile(S, 256)

    def body(mgp_ref, mgh_ref, br_ref, o_ref):
        o_ref[...] = (_sig(mgp_ref[...]) * br_ref[0] + _sig(mgh_ref[...]) * br_ref[1]).astype(BF16)

    return pl.pallas_call(
        body, name=name, grid=(S // tr,),
        in_specs=[pl.BlockSpec((tr, D), lambda i: (i, MGP_BLK)), pl.BlockSpec((tr, D), lambda i: (i, MGH_BLK)),
                  pl.BlockSpec((2, tr, D), lambda i: (0, i, 0))],
        out_specs=pl.BlockSpec((tr, D), lambda i: (i, 0)),
        out_shape=jax.ShapeDtypeStruct((S, D), BF16), compiler_params=_cp(("parallel",)),
    )(proj, proj, br)


def _merge_bwd(dm, proj, br, name):
    S = proj.shape[0]
    tr = _row_tile(S, 256)

    def body(dm_ref, mg_ref, br_ref, dbr_ref, dmg_ref):
        s = _sig(mg_ref[...])
        dmv = dm_ref[...]
        dbr_ref[...] = (dmv * s).astype(BF16)
        dmg_ref[...] = (dmv * br_ref[...] * s * (1.0 - s)).astype(BF16)

    return pl.pallas_call(
        body, name=name, grid=(S // tr, 2),
        in_specs=[pl.BlockSpec((tr, D), lambda i, j: (i, 0)),
                  pl.BlockSpec((tr, D), lambda i, j: (i, MGP_BLK + j)),
                  pl.BlockSpec((None, tr, D), lambda i, j: (j, i, 0))],
        out_specs=[pl.BlockSpec((None, tr, D), lambda i, j: (j, i, 0)),
                   pl.BlockSpec((tr, D), lambda i, j: (i, j))],
        out_shape=[jax.ShapeDtypeStruct((2, S, D), BF16), jax.ShapeDtypeStruct((S, 2 * D), BF16)],
        compiler_params=_cp(("parallel", "parallel")),
    )(dm, proj, br)


def _pool_pieces(u, g, S):
    rowi = lax.broadcasted_iota(jnp.int32, (S, 1), 0)

    def down(z, k):
        return jnp.where(rowi >= k, pltpu.roll(z, k, axis=0), 0.0)

    s2 = u + down(u, 1)
    s4 = s2 + down(s2, 2)
    s8 = s4 + down(s4, 4)
    s16 = s8 + down(s8, 8)
    win = jnp.where(g == 0, s2, jnp.where(g == 1, s4, jnp.where(g == 2, s8, s16)))
    w = jnp.where(g == 0, 2, jnp.where(g == 1, 4, jnp.where(g == 2, 8, 16)))
    count = jnp.minimum(rowi + 1, w).astype(F32)
    return win / count - u, count, rowi


def _pool_fwd(proj, pw, pscale, name):
    S = proj.shape[0]

    def body(pv_ref, pg_ref, pw_ref, sc_ref, a_ref):
        g = pl.program_id(0)
        pooled, _, _ = _pool_pieces(pv_ref[...], g, S)
        pm = jnp.dot(pooled.astype(BF16), pw_ref[...].astype(BF16), preferred_element_type=F32)
        pgv = pg_ref[...]
        a_ref[...] = (pm * sc_ref[...] * (pgv * _sig(pgv))).astype(BF16)

    return pl.pallas_call(
        body, name=name, grid=(GROUPS,),
        in_specs=[pl.BlockSpec((S, 128), lambda g: (0, PV0 + g)), pl.BlockSpec((S, 128), lambda g: (0, PG0 + g)),
                  pl.BlockSpec((None, 128, 128), lambda g: (g, 0, 0)), pl.BlockSpec((1, 128), lambda g: (0, g))],
        out_specs=pl.BlockSpec((S, 128), lambda g: (0, g)),
        out_shape=jax.ShapeDtypeStruct((S, POOL_W), BF16), compiler_params=_cp(("parallel",)),
    )(proj, proj, pw, pscale)


def _pool_bwd(da, proj, pw, pscale, name):
    S = proj.shape[0]

    def body(da_ref, pv_ref, pg_ref, pw_ref, sc_ref, dpv_ref, dpg_ref, dpw_ref, dsc_ref):
        g = pl.program_id(0)
        pooled, count, rowi = _pool_pieces(pv_ref[...], g, S)
        pwb = pw_ref[...].astype(BF16)
        pm = jnp.dot(pooled.astype(BF16), pwb, preferred_element_type=F32)
        scv = sc_ref[...]
        pgv = pg_ref[...]
        sg = _sig(pgv)
        dav = da_ref[...]
        d_ps = dav * (pgv * sg)
        dpg_ref[...] = (dav * (pm * scv) * _dsilu(pgv, sg)).astype(BF16)
        dsc_ref[...] = jnp.sum(d_ps * pm, axis=0, keepdims=True)
        d_pm = (d_ps * scv).astype(BF16)
        dpw_ref[...] = lax.dot_general(pooled.astype(BF16), d_pm, (((0,), (0,)), ((), ())),
                                       preferred_element_type=F32)
        d_pooled = lax.dot_general(d_pm, pwb, (((1,), (1,)), ((), ())), preferred_element_type=F32)
        z = d_pooled / count

        def up(v, k):
            return jnp.where(rowi < S - k, pltpu.roll(v, S - k, axis=0), 0.0)

        t2 = z + up(z, 1)
        t4 = t2 + up(t2, 2)
        t8 = t4 + up(t4, 4)
        t16 = t8 + up(t8, 8)
        adj = jnp.where(g == 0, t2, jnp.where(g == 1, t4, jnp.where(g == 2, t8, t16)))
        dpv_ref[...] = (adj - d_pooled).astype(BF16)

    col = lambda g: (0, g)
    return pl.pallas_call(
        body, name=name, grid=(GROUPS,),
        in_specs=[pl.BlockSpec((S, 128), col), pl.BlockSpec((S, 128), lambda g: (0, PV0 + g)),
                  pl.BlockSpec((S, 128), lambda g: (0, PG0 + g)),
                  pl.BlockSpec((None, 128, 128), lambda g: (g, 0, 0)), pl.BlockSpec((1, 128), col)],
        out_specs=[pl.BlockSpec((S, 128), col), pl.BlockSpec((S, 128), col),
                   pl.BlockSpec((None, 128, 128), lambda g: (g, 0, 0)), pl.BlockSpec((1, 128), col)],
        out_shape=[jax.ShapeDtypeStruct((S, POOL_W), BF16), jax.ShapeDtypeStruct((S, POOL_W), BF16),
                   jax.ShapeDtypeStruct((GROUPS, 128, 128), F32), jax.ShapeDtypeStruct((1, POOL_W), F32)],
        compiler_params=_cp(("parallel",)),
    )(da, proj, proj, pw, pscale)


def _chunk_cumsum(z, rowi):
    for sh in (1, 2, 4, 8, 16, 32):
        z = z + jnp.where(rowi >= sh, pltpu.roll(z, sh, axis=0), 0.0)
    return z


def _chunk_rev_cumsum(z, rowi):
    for sh in (1, 2, 4, 8, 16, 32):
        z = z + jnp.where(rowi < CH - sh, pltpu.roll(z, CH - sh, axis=0), 0.0)
    return z


def _dot_nn(a, b):
    return jnp.dot(a.astype(BF16), b.astype(BF16), preferred_element_type=F32)


def _dot_nt(a, b):
    return lax.dot_general(a.astype(BF16), b.astype(BF16), (((1,), (1,)), ((), ())), preferred_element_type=F32)


def _dot_tn(a, b):
    return lax.dot_general(a.astype(BF16), b.astype(BF16), (((0,), (0,)), ((), ())), preferred_element_type=F32)


def _gates(hq, hf, lbv):
    sq = _sig(hq)
    sf = _sig(hf)
    f = lbv + (1.0 - lbv) * sf
    fc = jnp.maximum(f, 1e-30)
    return hq * sq, sq, sf, f, fc, jnp.log(fc)


def _hgrn_fwd(proj, lb, gn, name):
    S = proj.shape[0]
    nch = S // CH

    def body(hq_ref, hf_ref, hi_ref, hg_ref, lb_ref, gn_ref, bin_ref, oraw_ref, st_ref,
             q_s, k_s, c_s, v_s, o_s, state_s):
        state_s[...] = jnp.zeros((HD, HD), F32)
        rowi = lax.broadcasted_iota(jnp.int32, (CH, 1), 0)
        coli = lax.broadcasted_iota(jnp.int32, (1, CH), 1)
        sbi = lax.broadcasted_iota(jnp.int32, (SB, 1), 0)
        lbv = lb_ref[...]
        gnv = gn_ref[...]

        def chunk(n, carry):
            r0 = pl.multiple_of(n * CH, CH)
            rows = pl.ds(r0, CH)
            q, _, _, f, _, logf = _gates(hq_ref[rows, :], hf_ref[rows, :], lbv)
            k = 1.0 - f
            v = hi_ref[rows, :]
            c = _chunk_cumsum(logf, rowi)
            q_s[...] = q
            k_s[...] = k
            c_s[...] = c
            v_s[...] = v
            st = state_s[...]
            st_ref[n] = st.astype(BF16)
            o = _dot_nt(q * jnp.exp(c), st)
            a_off = jnp.zeros((CH, CH), F32)
            for i in range(1, CH // SB):
                r_i = c_s[SB * i - 1:SB * i, :]
                qi = q * jnp.exp(jnp.minimum(c - r_i, 0.0))
                kei = k * jnp.exp(jnp.minimum(r_i - c, 0.0))
                m_i = (rowi >= SB * i) & (rowi < SB * (i + 1)) & (coli < SB * i)
                a_off = a_off + jnp.where(m_i, _dot_nt(qi, kei), 0.0)
            o_s[...] = o + _dot_nn(a_off, v)
            for i in range(CH // SB):
                blk = slice(SB * i, SB * (i + 1))
                qb = q_s[blk, :]
                cb = c_s[blk, :]
                acc = jnp.zeros((SB, HD), F32)
                for s in range(SB):
                    row = SB * i + s
                    w = jnp.exp(jnp.minimum(cb - c_s[row:row + 1, :], 0.0))
                    a_col = jnp.sum(qb * k_s[row:row + 1, :] * w, axis=-1, keepdims=True)
                    acc = acc + jnp.where(sbi >= s, a_col, 0.0) * v_s[row:row + 1, :]
                o_s[blk, :] += acc
            last = c_s[CH - 1:CH, :]
            state_s[...] = st * jnp.exp(last) + _dot_tn(v, k * jnp.exp(last - c))
            ov = o_s[...]
            oraw_ref[rows, :] = ov
            r = lax.rsqrt(jnp.mean(ov * ov, axis=-1, keepdims=True) + EPS)
            hg = hg_ref[rows, :]
            bin_ref[rows, :] = ((ov * r) * gnv * (hg * _sig(hg))).astype(BF16)
            return carry

        lax.fori_loop(0, nch, chunk, 0)

    col = lambda off: pl.BlockSpec((S, HD), lambda h: (0, off + h))
    head = pl.BlockSpec((S, HD), lambda h: (0, h))
    return pl.pallas_call(
        body, name=name, grid=(HEADS,),
        in_specs=[col(HQ0), col(HF0), col(HI0), col(HG0), pl.BlockSpec((1, HD), lambda h: (0, h)),
                  pl.BlockSpec((1, HD), lambda h: (0, 0))],
        out_specs=[head, head, pl.BlockSpec((None, nch, HD, HD), lambda h: (h, 0, 0, 0))],
        out_shape=[jax.ShapeDtypeStruct((S, D), BF16), jax.ShapeDtypeStruct((S, D), F32),
                   jax.ShapeDtypeStruct((HEADS, nch, HD, HD), BF16)],
        scratch_shapes=[pltpu.VMEM((CH, HD), F32)] * 5 + [pltpu.VMEM((HD, HD), F32)],
        compiler_params=_cp(("parallel",)),
    )(proj, proj, proj, proj, lb, gn)


def _hgrn_bwd(dbin, proj, oraw, states, lb, gn, name):
    S = proj.shape[0]
    nch = S // CH

    def body(db_ref, hq_ref, hf_ref, hi_ref, hg_ref, or_ref, st_ref, lb_ref, gn_ref,
             dq_ref, df_ref, di_ref, dg_ref, dlb_ref, dgn_ref,
             q_s, k_s, c_s, v_s, do_s, dq_s, dk_s, dv_s, dst_s, dlb_s, dgn_s):
        dst_s[...] = jnp.zeros((HD, HD), F32)
        dlb_s[...] = jnp.zeros((1, HD), F32)
        dgn_s[...] = jnp.zeros((1, HD), F32)
        rowi = lax.broadcasted_iota(jnp.int32, (CH, 1), 0)
        coli = lax.broadcasted_iota(jnp.int32, (1, CH), 1)
        rowi2 = lax.broadcasted_iota(jnp.int32, (CH, CH), 0)
        coli2 = lax.broadcasted_iota(jnp.int32, (CH, CH), 1)
        sbi = lax.broadcasted_iota(jnp.int32, (SB, 1), 0)
        lbv = lb_ref[...]
        gnv = gn_ref[...]

        def chunk(j, carry):
            n = nch - 1 - j
            r0 = pl.multiple_of(n * CH, CH)
            rows = pl.ds(r0, CH)
            hq = hq_ref[rows, :]
            q, sq, sf, f, fc, logf = _gates(hq, hf_ref[rows, :], lbv)
            k = 1.0 - f
            v = hi_ref[rows, :]
            c = _chunk_cumsum(logf, rowi)
            ov = or_ref[rows, :]
            hg = hg_ref[rows, :]
            sg = _sig(hg)
            r = lax.rsqrt(jnp.mean(ov * ov, axis=-1, keepdims=True) + EPS)
            dbv = db_ref[rows, :]
            d_on = dbv * (hg * sg)
            dg_ref[rows, :] = (dbv * ((ov * r) * gnv) * _dsilu(hg, sg)).astype(BF16)
            dgn_s[...] += jnp.sum(d_on * (ov * r), axis=0, keepdims=True)
            u = d_on * gnv
            do = r * u - ov * (r * r * r) * jnp.mean(u * ov, axis=-1, keepdims=True)
            q_s[...] = q
            k_s[...] = k
            c_s[...] = c
            v_s[...] = v
            do_s[...] = do
            st = st_ref[n].astype(F32)
            dst = dst_s[...]
            ec = jnp.exp(c)
            last = c_s[CH - 1:CH, :]
            el = jnp.exp(last - c)
            elast = jnp.exp(last)
            dq = _dot_nn(do, st) * ec
            dk = _dot_nn(v, dst) * el
            dv = _dot_nt(k * el, dst)
            dst_s[...] = dst * elast + _dot_tn(do, q * ec)
            dcum = q * dq - k * dk
            dlast = jnp.sum(k * dk, axis=0, keepdims=True) + elast * jnp.sum(st * dst, axis=0, keepdims=True)
            d_a = _dot_nt(do, v).astype(BF16).astype(F32)
            d_at = d_a.T
            at_off = jnp.zeros((CH, CH), F32)
            for i in range(1, CH // SB):
                r_i = c_s[SB * i - 1:SB * i, :]
                eq = jnp.exp(jnp.minimum(c - r_i, 0.0))
                ek = jnp.exp(jnp.minimum(r_i - c, 0.0))
                qi = (q * eq).astype(BF16).astype(F32)
                kei = (k * ek).astype(BF16).astype(F32)
                m_ts = (rowi2 >= SB * i) & (rowi2 < SB * (i + 1)) & (coli2 < SB * i)
                m_st = (coli2 >= SB * i) & (coli2 < SB * (i + 1)) & (rowi2 < SB * i)
                at_off = at_off + jnp.where(m_st, _dot_nt(kei, qi), 0.0)
                dq_i = _dot_nn(jnp.where(m_ts, d_a, 0.0), kei)
                dk_i = _dot_nn(jnp.where(m_st, d_at, 0.0), qi)
                dq = dq + dq_i * eq
                dk = dk + dk_i * ek
                dcum = dcum + (qi * dq_i - kei * dk_i)
            dv = dv + _dot_nn(at_off, do)
            dq_s[...] = jnp.zeros((CH, HD), F32)
            dk_s[...] = jnp.zeros((CH, HD), F32)
            dv_s[...] = dv
            for i in range(CH // SB):
                blk = slice(SB * i, SB * (i + 1))
                qb = q_s[blk, :]
                cb = c_s[blk, :]
                dob = do_s[blk, :]
                dq_acc = jnp.zeros((SB, HD), F32)
                for s in range(SB):
                    row = SB * i + s
                    ks = k_s[row:row + 1, :]
                    vs = v_s[row:row + 1, :]
                    w = jnp.exp(jnp.minimum(cb - c_s[row:row + 1, :], 0.0))
                    live = sbi >= s
                    a_col = jnp.where(live, jnp.sum(qb * ks * w, axis=-1, keepdims=True), 0.0)
                    da_col = jnp.where(live, jnp.sum(dob * vs, axis=-1, keepdims=True), 0.0)
                    dq_acc = dq_acc + da_col * ks * w
                    dk_s[row:row + 1, :] += jnp.sum(da_col * qb * w, axis=0, keepdims=True)
                    dv_s[row:row + 1, :] += jnp.sum(a_col * dob, axis=0, keepdims=True)
                dq_s[blk, :] += dq_acc
            dq_d = dq_s[...]
            dk_d = dk_s[...]
            dq = dq + dq_d
            dk = dk + dk_d
            dcum = dcum + (q * dq_d - k * dk_d)
            dlogf = _chunk_rev_cumsum(dcum, rowi) + dlast
            dfv = jnp.where(f > 1e-30, dlogf / fc, 0.0) - dk
            dlb_s[...] += jnp.sum(dfv * (1.0 - sf), axis=0, keepdims=True)
            df_ref[rows, :] = (dfv * (1.0 - lbv) * sf * (1.0 - sf)).astype(BF16)
            dq_ref[rows, :] = (dq * _dsilu(hq, sq)).astype(BF16)
            di_ref[rows, :] = dv_s[...].astype(BF16)
            return carry

        lax.fori_loop(0, nch, chunk, 0)
        dlb_ref[...] = dlb_s[...]
        dgn_ref[...] = jnp.broadcast_to(dgn_s[...], (8, HD))

    col = lambda off: pl.BlockSpec((S, HD), lambda h: (0, off + h))
    head = pl.BlockSpec((S, HD), lambda h: (0, h))
    vec = pl.BlockSpec((1, HD), lambda h: (0, h))
    outs = pl.pallas_call(
        body, name=name, grid=(HEADS,),
        in_specs=[head, col(HQ0), col(HF0), col(HI0), col(HG0), head,
                  pl.BlockSpec((None, nch, HD, HD), lambda h: (h, 0, 0, 0)), vec,
                  pl.BlockSpec((1, HD), lambda h: (0, 0))],
        out_specs=[head, head, head, head, vec, pl.BlockSpec((8, HD), lambda h: (h, 0))],
        out_shape=[jax.ShapeDtypeStruct((S, D), BF16)] * 4
        + [jax.ShapeDtypeStruct((1, D), F32), jax.ShapeDtypeStruct((8 * HEADS, HD), F32)],
        scratch_shapes=[pltpu.VMEM((CH, HD), F32)] * 8
        + [pltpu.VMEM((HD, HD), F32)] + [pltpu.VMEM((1, HD), F32)] * 2,
        compiler_params=_cp(("parallel",)),
    )(dbin, proj, proj, proj, proj, oraw, states, lb, gn)
    dq, df, di, dg, dlb, dgn = outs
    return dq, df, di, dg, dlb, dgn.reshape(HEADS, 8, HD)[:, 0, :]


def _lower_bounds(l0, l1):
    m = jnp.maximum(l0, l1)
    e0 = jnp.exp(l0 - m)
    e1 = jnp.exp(l1 - m)
    tot = e0 + e1
    p0 = e0 / tot
    p1 = e1 / tot
    return jnp.clip(p0 - p0, 0.0, 1.0), jnp.clip((p0 + p1) - p0, 0.0, 1.0)


def _lb_fwd(logits):
    def body(l_ref, o_ref):
        lb0, lb1 = _lower_bounds(l_ref[0:1, :], l_ref[1:2, :])
        o_ref[0:1, :] = lb0
        o_ref[1:2, :] = lb1

    return pl.pallas_call(body, name="lb_fwd", out_shape=jax.ShapeDtypeStruct((2, D), F32))(logits)


def _lb_bwd(logits, dlb):
    def body(l_ref, d_ref, o_ref):
        _, vjp = jax.vjp(_lower_bounds, l_ref[0:1, :], l_ref[1:2, :])
        g0, g1 = vjp((d_ref[0:1, :], d_ref[1:2, :]))
        o_ref[0:1, :] = g0
        o_ref[1:2, :] = g1

    return pl.pallas_call(body, name="lb_bwd", out_shape=jax.ShapeDtypeStruct((2, D), F32))(logits, dlb)


ADA_PAD = 128


def _ada_fwd(c_pad, w_ada, b_sh):
    ns = w_ada.shape[2]

    def body(c_ref, w_ref, b_ref, o_ref):
        cv = c_ref[...]
        ca = (cv * _sig(cv)).astype(BF16)
        for l in range(2):
            res = jnp.dot(ca, w_ref[l].astype(BF16), preferred_element_type=F32)
            o_ref[:, l * ns:(l + 1) * ns] = res[0:NDEV, :] + b_ref[l:l + 1, :]

    return pl.pallas_call(body, name="ada_fwd", out_shape=jax.ShapeDtypeStruct((NDEV, 2 * ns), F32),
                          compiler_params=_cp())(c_pad, w_ada, b_sh)


def _ada_wgrad(c_pad_t, d_ada_sh):
    ns = d_ada_sh.shape[2]

    def body(c_ref, d_ref, o_ref):
        cv = c_ref[...]
        ca = (cv * _sig(cv)).astype(BF16)
        for l in range(2):
            o_ref[l] = jnp.dot(ca, d_ref[l].astype(BF16), preferred_element_type=F32)

    return pl.pallas_call(body, name="ada_wgrad", out_shape=jax.ShapeDtypeStruct((2, D, ns), F32),
                          compiler_params=_cp())(c_pad_t, d_ada_sh)


def _sum_devices(g):
    _, R, C = g.shape

    def body(g_ref, o_ref):
        acc = g_ref[0]
        for d in range(1, NDEV):
            acc = acc + g_ref[d]
        o_ref[...] = acc

    return pl.pallas_call(body, name="sum_devices", out_shape=jax.ShapeDtypeStruct((R, C), F32),
                          compiler_params=_cp())(g)


def _adamw(w, g, m, v, name):
    R, C = w.shape
    tr = _row_tile(R, max(8, (1 << 19) // C))

    def body(w_ref, g_ref, m_ref, v_ref, d_ref, nm_ref, nv_ref):
        gv = g_ref[...]
        nm = B1 * m_ref[...] + (1.0 - B1) * gv
        nv = B2 * v_ref[...] + (1.0 - B2) * (gv * gv)
        m_hat = nm / (1.0 - B1 ** STEP)
        v_hat = nv / (1.0 - B2 ** STEP)
        d_ref[...] = -LR * (m_hat / (jnp.sqrt(v_hat) + AEPS) + WD * w_ref[...])
        nm_ref[...] = nm
        nv_ref[...] = nv

    tile = pl.BlockSpec((tr, C), lambda i: (i, 0))
    return pl.pallas_call(
        body, name=name, grid=(R // tr,), in_specs=[tile] * 4, out_specs=[tile] * 3,
        out_shape=[jax.ShapeDtypeStruct((R, C), F32)] * 3, compiler_params=_cp(("parallel",)),
    )(w, g, m, v)


def _pair_add(core, g, got, name):
    _, R, C = g.shape
    r2 = R // 2
    tr = _row_tile(r2, max(8, (1 << 19) // C))
    nt = r2 // tr

    def body(c_ref, a_ref, b_ref, o_ref):
        o_ref[...] = (a_ref[...].astype(F32) + b_ref[...].astype(F32)).astype(o_ref.dtype)

    return pl.pallas_call(
        body, name=name, out_shape=jax.ShapeDtypeStruct((NCHIP, r2, C), BF16),
        grid_spec=pltpu.PrefetchScalarGridSpec(
            num_scalar_prefetch=1, grid=(NCHIP, nt),
            in_specs=[pl.BlockSpec((None, tr, C), lambda j, i, c_ref: (j, c_ref[0] * nt + i, 0)),
                      pl.BlockSpec((None, tr, C), lambda j, i, c_ref: (j, i, 0))],
            out_specs=pl.BlockSpec((None, tr, C), lambda j, i, c_ref: (j, i, 0))),
        compiler_params=_cp(("parallel", "parallel")),
    )(core, g, got)


def _chip_sum(place, part, recv, name):
    _, r2, C = part.shape
    tr = _row_tile(r2, max(8, (1 << 18) // C))
    nt = r2 // tr

    def body(p_ref, own_ref, r_ref, o_ref):
        me = p_ref[0]
        own = own_ref[...].astype(F32)
        acc = None
        for j in range(NCHIP):
            slot = jnp.minimum(jnp.where(j > me, j - 1, j), NCHIP - 2)
            term = jnp.where(me == j, own, r_ref[slot].astype(F32))
            acc = term if acc is None else acc + term
        o_ref[...] = acc

    return pl.pallas_call(
        body, name=name, out_shape=jax.ShapeDtypeStruct((2 * r2, C), F32),
        grid_spec=pltpu.PrefetchScalarGridSpec(
            num_scalar_prefetch=1, grid=(nt,),
            in_specs=[pl.BlockSpec((None, tr, C), lambda i, p_ref: (p_ref[0], i, 0)),
                      pl.BlockSpec((NCHIP - 1, tr, C), lambda i, p_ref: (0, i, 0))],
            out_specs=pl.BlockSpec((tr, C), lambda i, p_ref: (p_ref[1] * nt + i, 0))),
        compiler_params=_cp(("parallel",)),
    )(place, part, recv)


ANY = pl.BlockSpec(memory_space=pl.ANY)


def _place():
    x, y, c = lax.axis_index("x"), lax.axis_index("y"), lax.axis_index("c")
    chips = [(1 - x, y), (x, 1 - y), (1 - x, 1 - y)]
    return x, y, c, chips


def _gather_small(blk, name):
    m_per, n = blk.shape

    def body(x_ref, out_ref, send_sems, recv_sems, local_sem):
        x, y, c, chips = _place()
        me, sibling = (x, y, c), (x, y, 1 - c)

        def rows(px, py, pc):
            return out_ref.at[pl.ds((4 * px + 2 * py + pc) * m_per, m_per), :]

        def copy(k, block, to, src=None):
            return pltpu.make_async_remote_copy(
                src_ref=rows(*block) if src is None else src, dst_ref=rows(*block),
                send_sem=send_sems.at[k], recv_sem=recv_sems.at[k], device_id=to, device_id_type=MESH)

        mine = pltpu.make_async_copy(x_ref, rows(*me), local_sem)
        mine.start()
        first = [copy(0, me, sibling, src=x_ref)]
        first += [copy(1 + j, me, (*chip, c), src=x_ref) for j, chip in enumerate(chips)]
        for cp in first:
            cp.start()
        passed = [copy(4 + j, (*chip, c), sibling) for j, chip in enumerate(chips)]
        for j, chip in enumerate(chips):
            copy(1 + j, (*chip, c), me).wait_recv()
            passed[j].start()
        copy(0, sibling, me).wait_recv()
        for j, chip in enumerate(chips):
            copy(4 + j, (*chip, 1 - c), me).wait_recv()
        for cp in first + passed:
            cp.wait_send()
        mine.wait()

    return pl.pallas_call(
        body, name=name, out_shape=jax.ShapeDtypeStruct((NDEV * m_per, n), blk.dtype),
        in_specs=[pl.BlockSpec(memory_space=pltpu.VMEM)], out_specs=pl.BlockSpec(memory_space=pltpu.VMEM),
        scratch_shapes=[pltpu.SemaphoreType.DMA((7,)), pltpu.SemaphoreType.DMA((7,)), pltpu.SemaphoreType.DMA],
        compiler_params=_cp(),
    )(blk)


def _gather_weights(shards):
    n = len(shards)

    def body(*refs):
        outs = refs[n:2 * n]
        send_sems, recv_sems = refs[2 * n:]
        x, y, c, chips = _place()
        me = 2 * x + y
        sends = []
        for a in range(n):
            r2 = outs[a].shape[1] // 2
            half = pl.ds(c * r2, r2)
            for j, (cx, cy) in enumerate(chips):
                cp = pltpu.make_async_remote_copy(
                    src_ref=outs[a].at[me, half, :], dst_ref=outs[a].at[me, half, :],
                    send_sem=send_sems.at[a, j], recv_sem=recv_sems.at[a, j],
                    device_id=(cx, cy, c), device_id_type=MESH)
                cp.start()
                sends.append(cp)
        for a in range(n):
            r2 = outs[a].shape[1] // 2
            half = pl.ds(c * r2, r2)
            for j, (cx, cy) in enumerate(chips):
                blk = outs[a].at[2 * cx + cy, half, :]
                pltpu.make_async_remote_copy(
                    src_ref=blk, dst_ref=blk, send_sem=send_sems.at[a, j], recv_sem=recv_sems.at[a, j],
                    device_id=(cx, cy, c), device_id_type=MESH).wait_recv()
                fw = pltpu.make_async_remote_copy(
                    src_ref=blk, dst_ref=blk, send_sem=send_sems.at[a, 3 + j], recv_sem=recv_sems.at[a, 3 + j],
                    device_id=(x, y, 1 - c), device_id_type=MESH)
                fw.start()
                sends.append(fw)
        for a in range(n):
            r2 = outs[a].shape[1] // 2
            other = pl.ds((1 - c) * r2, r2)
            for j, (cx, cy) in enumerate(chips):
                blk = outs[a].at[2 * cx + cy, other, :]
                pltpu.make_async_remote_copy(
                    src_ref=blk, dst_ref=blk, send_sem=send_sems.at[a, 3 + j], recv_sem=recv_sems.at[a, 3 + j],
                    device_id=(x, y, 1 - c), device_id_type=MESH).wait_recv()
        for cp in sends:
            cp.wait_send()

    return pl.pallas_call(
        body, name="gather_weights",
        out_shape=[jax.ShapeDtypeStruct(s.shape, s.dtype) for s in shards],
        in_specs=[ANY] * n, out_specs=[ANY] * n, input_output_aliases={a: a for a in range(n)},
        scratch_shapes=[pltpu.SemaphoreType.DMA((n, 6)), pltpu.SemaphoreType.DMA((n, 6))],
        compiler_params=_cp(),
    )(*shards)


def _rs_pair(grads):
    n = len(grads)

    def body(*refs):
        ins, gots = refs[:n], refs[n:2 * n]
        send_sems, recv_sems = refs[2 * n:]
        x, y, c, _ = _place()
        cps = []
        for a in range(n):
            r2 = ins[a].shape[1] // 2
            cp = pltpu.make_async_remote_copy(
                src_ref=ins[a].at[:, pl.ds((1 - c) * r2, r2), :], dst_ref=gots[a],
                send_sem=send_sems.at[a], recv_sem=recv_sems.at[a],
                device_id=(x, y, 1 - c), device_id_type=MESH)
            cp.start()
            cps.append(cp)
        for cp in cps:
            cp.wait()

    half = [jax.ShapeDtypeStruct((NCHIP, g.shape[1] // 2, g.shape[2]), g.dtype) for g in grads]
    return pl.pallas_call(
        body, name="rs_pair", out_shape=half, in_specs=[ANY] * n, out_specs=[ANY] * n,
        scratch_shapes=[pltpu.SemaphoreType.DMA((n,)), pltpu.SemaphoreType.DMA((n,))],
        compiler_params=_cp(),
    )(*grads)


def _rs_chips(parts):
    n = len(parts)

    def body(*refs):
        ins, outs = refs[:n], refs[n:2 * n]
        send_sems, recv_sems = refs[2 * n:]
        x, y, c, chips = _place()
        me = 2 * x + y
        cps = []
        for a in range(n):
            for j, (cx, cy) in enumerate(chips):
                them = 2 * cx + cy
                cp = pltpu.make_async_remote_copy(
                    src_ref=ins[a].at[them], dst_ref=outs[a].at[me - (me > them).astype(jnp.int32)],
                    send_sem=send_sems.at[a, j], recv_sem=recv_sems.at[a, j],
                    device_id=(cx, cy, c), device_id_type=MESH)
                cp.start()
                cps.append(cp)
        for a in range(n):
            for j, (cx, cy) in enumerate(chips):
                them = 2 * cx + cy
                blk = outs[a].at[them - (them > me).astype(jnp.int32)]
                pltpu.make_async_remote_copy(
                    src_ref=blk, dst_ref=blk, send_sem=send_sems.at[a, j], recv_sem=recv_sems.at[a, j],
                    device_id=(cx, cy, c), device_id_type=MESH).wait_recv()
        for cp in cps:
            cp.wait_send()

    return pl.pallas_call(
        body, name="rs_chips",
        out_shape=[jax.ShapeDtypeStruct((NCHIP - 1,) + p.shape[1:], p.dtype) for p in parts],
        in_specs=[ANY] * n, out_specs=[ANY] * n,
        scratch_shapes=[pltpu.SemaphoreType.DMA((n, 3)), pltpu.SemaphoreType.DMA((n, 3))],
        compiler_params=_cp(),
    )(*parts)


def _rs_swap(fulls):
    n = len(fulls)

    def body(*refs):
        outs = refs[n:2 * n]
        send_sems, recv_sems = refs[2 * n:]
        x, y, c, _ = _place()
        cps = []
        for a in range(n):
            r2 = outs[a].shape[0] // 2
            mine = outs[a].at[pl.ds(c * r2, r2), :]
            cp = pltpu.make_async_remote_copy(
                src_ref=mine, dst_ref=mine, send_sem=send_sems.at[a], recv_sem=recv_sems.at[a],
                device_id=(x, y, 1 - c), device_id_type=MESH)
            cp.start()
            cps.append(cp)
        for a in range(n):
            r2 = outs[a].shape[0] // 2
            blk = outs[a].at[pl.ds((1 - c) * r2, r2), :]
            pltpu.make_async_remote_copy(
                src_ref=blk, dst_ref=blk, send_sem=send_sems.at[a], recv_sem=recv_sems.at[a],
                device_id=(x, y, 1 - c), device_id_type=MESH).wait_recv()
        for cp in cps:
            cp.wait_send()

    return pl.pallas_call(
        body, name="rs_swap", out_shape=[jax.ShapeDtypeStruct(f.shape, f.dtype) for f in fulls],
        in_specs=[ANY] * n, out_specs=[ANY] * n, input_output_aliases={a: a for a in range(n)},
        scratch_shapes=[pltpu.SemaphoreType.DMA((n,)), pltpu.SemaphoreType.DMA((n,))],
        compiler_params=_cp(),
    )(*fulls)


def _layer_fwd(l, x, ada, w, small):
    shift, scale, gate = ada[:, 0:D], ada[:, D:2 * D], ada[:, 2 * D:3 * D]
    h = _prenorm_fwd(x, small["g_pre"][l], scale, shift, f"prenorm_fwd{l}")
    proj = _mm(h, w["w_in"][l], name=f"proj{l}", b_mode="nn_sh")
    a_in = _pool_fwd(proj, small["pool_w"][l], small["pool_scale"][l], f"pool_fwd{l}")
    b_in, o_raw, states = _hgrn_fwd(proj, small["lb"][l], small["hgrn_norm_g"][l], f"hgrn_fwd{l}")
    br_a = _mm(a_in, w["w_pool_o"][l], name=f"branch_a{l}", b_mode="nn_sh")
    br_b = _mm(b_in, w["w_hgrn_o"][l].reshape(D, D), name=f"branch_b{l}")
    br = jnp.stack([br_a, br_b])
    merged = _merge_fwd(proj, br, f"merge_fwd{l}")
    y = _mm(merged, w["w_out"][l].reshape(D, D), name=f"out_proj{l}")
    x_new = _postnorm_fwd(x, y, gate, small["g_post"][l], f"postnorm_fwd{l}")
    saved = dict(x=x, h=h, proj=proj, a_in=a_in, b_in=b_in, o_raw=o_raw, states=states, br=br,
                 merged=merged, y=y, scale=scale, gate=gate)
    return x_new, saved


def _layer_bwd(l, dxn, sv, w, small):
    dy, dgate, dg_post = _postnorm_bwd(dxn, sv["y"], sv["gate"], small["g_post"][l], f"postnorm_bwd{l}")
    w_out = w["w_out"][l].reshape(D, D)
    dmerged = _mm(dy, w_out, name=f"d_merged{l}", b_mode="nt")
    gw_out = _mm(sv["merged"].T, dy, name=f"gw_out{l}", out_dtype=BF16)
    dbr, dmg = _merge_bwd(dmerged, sv["proj"], sv["br"], f"merge_bwd{l}")
    da_in = _mm(dbr[0], w["w_pool_o"][l], name=f"d_a_in{l}", b_mode="nt_shk")
    gw_pool_o = _mm(sv["a_in"].T, dbr[0], name=f"gw_pool_o{l}", out_shards=NCHIP, out_dtype=BF16)
    db_in = _mm(dbr[1], w["w_hgrn_o"][l].reshape(D, D), name=f"d_b_in{l}", b_mode="nt")
    gw_hgrn_o = _mm(sv["b_in"].T, dbr[1], name=f"gw_hgrn_o{l}", out_dtype=BF16)
    dhq, dhf, dhi, dhg, dlb, dgn = _hgrn_bwd(db_in, sv["proj"], sv["o_raw"], sv["states"], small["lb"][l],
                                             small["hgrn_norm_g"][l], f"hgrn_bwd{l}")
    dpv, dpg, dpw, dpsc = _pool_bwd(da_in, sv["proj"], small["pool_w"][l], small["pool_scale"][l],
                                    f"pool_bwd{l}")
    dproj = jnp.concatenate([dpv, dpg, dhq, dhf, dhi, dhg, dmg], axis=1)
    dh = _mm(dproj, w["w_in"][l], name=f"d_h{l}", b_mode="nt_shk")
    gw_in = _mm(sv["h"].T, dproj, name=f"gw_in{l}", out_shards=NCHIP, out_dtype=BF16)
    dx, dshift, dscale, dg_pre = _prenorm_bwd(dh, dxn, sv["x"], small["g_pre"][l], sv["scale"],
                                              f"prenorm_bwd{l}")
    big = dict(w_in=gw_in, w_pool_o=gw_pool_o, w_hgrn_o=gw_hgrn_o.reshape(NCHIP, D // NCHIP, D),
               w_out=gw_out.reshape(NCHIP, D // NCHIP, D))
    little = dict(d_ada=jnp.concatenate([dshift, dscale, dgate], axis=1), g_pre=dg_pre, g_post=dg_post,
                  pool_w=dpw, pool_scale=dpsc, lb=dlb, hgrn_norm_g=jnp.sum(dgn, axis=0, keepdims=True))
    return dx, big, little


SMALL_ROWS = 176


def _rows8(t):
    t = t.reshape(-1, D)
    return jnp.pad(t, ((0, -t.shape[0] % 8), (0, 0)))


def _pack_small_weights(b_ada, g_pre, g_post, lb_logits, pool_w, pool_scale, hgrn_norm_g):
    gn = jnp.pad(hgrn_norm_g.reshape(1, 2 * HD), ((0, 0), (0, D - 2 * HD)))
    return jnp.concatenate([_rows8(b_ada), _rows8(g_pre), _rows8(g_post), _rows8(lb_logits), _rows8(pool_w),
                            _rows8(pool_scale), _rows8(gn)], axis=0)


def _pack_small(parts):
    both = lambda key: jnp.stack([parts[l][key] for l in range(2)])
    return _pack_small_weights(both("d_ada"), both("g_pre"), both("g_post"), both("lb"), both("pool_w"),
                               both("pool_scale"), both("hgrn_norm_g"))


def _unpack_small(p):
    return (p[0:6].reshape(2, 3 * D), p[8:10], p[16:18], p[24:26], p[32:160].reshape(2, GROUPS, 128, 128),
            p[160:161].reshape(2, POOL_W), p[168:169, 0:2 * HD].reshape(2, HD))


def kernel(x, c, w_ada, b_ada, g_pre, g_post, w_in, pool_w, pool_scale, lb_logits, hgrn_norm_g, w_pool_o, w_hgrn_o, w_out, loss_target, m_w_ada, m_b_ada, m_g_pre, m_g_post, m_w_in, m_pool_w, m_pool_scale, m_lb_logits, m_hgrn_norm_g, m_w_pool_o, m_w_hgrn_o, m_w_out, v_w_ada, v_b_ada, v_g_pre, v_g_post, v_w_in, v_pool_w, v_pool_scale, v_lb_logits, v_hgrn_norm_g, v_w_pool_o, v_w_hgrn_o, v_w_out):
    ax, ay, ac = lax.axis_index("x"), lax.axis_index("y"), lax.axis_index("c")
    chip = 2 * ax + ay
    dev = 2 * chip + ac
    xe, te = x[0], loss_target[0]
    ada_s = w_ada.shape[2]

    big_names = ("w_in", "w_pool_o", "w_hgrn_o", "w_out")
    big_w = (w_in, w_pool_o, w_hgrn_o, w_out)
    gathered = _gather_weights([
        lax.dynamic_update_slice(lax.empty((NCHIP,) + t.shape[1:], BF16), t[l].astype(BF16)[None], (chip, 0, 0))
        for l in range(2) for t in big_w])
    w = {k: (gathered[i], gathered[4 + i]) for i, k in enumerate(big_names)}

    c_all = _gather_small(jnp.broadcast_to(c, (8, D)), "gather_c").reshape(NDEV, 8, D)[:, 0, :]
    c_pad = jnp.pad(c_all, ((0, ADA_PAD - NDEV), (0, 0)))
    b_sh = lax.dynamic_slice(b_ada, (0, chip * ada_s), (2, ada_s))
    ada_cols = _gather_small(_ada_fwd(c_pad, w_ada, b_sh), "gather_ada")
    ada_cols = ada_cols.reshape(NCHIP, 2, NDEV, 2, ada_s)[:, 0]
    ada_all = jnp.transpose(ada_cols, (2, 1, 0, 3)).reshape(2, NDEV, 3 * D)
    ada_me = lax.dynamic_slice(ada_all, (0, dev, 0), (2, 1, 3 * D))

    lbs = _lb_fwd(lb_logits)
    small = dict(g_pre=g_pre[:, None, :], g_post=g_post[:, None, :], pool_w=pool_w,
                 pool_scale=pool_scale[:, None, :], lb=lbs[:, None, :], hgrn_norm_g=hgrn_norm_g[:, None, :])

    x1, sv0 = _layer_fwd(0, xe, ada_me[0], w, small)
    x2, sv1 = _layer_fwd(1, x1, ada_me[1], w, small)
    dx2, loss_blk = _loss_head(x2, te, "loss_head")
    dx1, big1, little1 = _layer_bwd(1, dx2, sv1, w, small)
    dx0, big0, little0 = _layer_bwd(0, dx1, sv0, w, small)
    loss = lax.psum(loss_blk[0, 0], ("x", "y", "c"))

    grads = [big0[k] for k in big_names] + [big1[k] for k in big_names]
    core = jnp.stack([ac]).astype(jnp.int32)
    place = jnp.stack([chip, ac]).astype(jnp.int32)
    got = _rs_pair(grads)
    parts = [_pair_add(core, g, o, f"rs_add{i}") for i, (g, o) in enumerate(zip(grads, got))]
    recv = _rs_chips(parts)
    red = _rs_swap([_chip_sum(place, p, r, f"rs_sum{i}") for i, (p, r) in enumerate(zip(parts, recv))])
    g_big = {k: jnp.stack([red[i], red[4 + i]]) for i, k in enumerate(big_names)}

    packed = _gather_small(_pack_small([little0, little1]), "gather_small")
    packed = packed.reshape(NDEV, SMALL_ROWS, D)
    g_small = _sum_devices(packed)
    g_b_ada, g_g_pre, g_g_post, g_lb, g_pool_w, g_pool_scale, g_norm_g = _unpack_small(g_small)
    g_lb_logits = _lb_bwd(lb_logits, g_lb)
    d_ada_all = packed[:, 0:6, :].reshape(NDEV, 2, 3 * D)
    d_ada_sh = lax.dynamic_slice(jnp.transpose(d_ada_all, (1, 0, 2)), (0, 0, chip * ada_s), (2, NDEV, ada_s))
    d_ada_sh = jnp.pad(d_ada_sh, ((0, 0), (0, ADA_PAD - NDEV), (0, 0)))
    g_w_ada = _ada_wgrad(c_pad.T, d_ada_sh)

    def upd(wt, g, m, v, name):
        shp = wt.shape
        two = lambda t: t.reshape(-1, shp[-1])
        d, nm, nv = _adamw(two(wt), two(g), two(m), two(v), name)
        return d.reshape(shp), nm.reshape(shp), nv.reshape(shp)

    u_w_ada = upd(w_ada, g_w_ada, m_w_ada, v_w_ada, "adamw_w_ada")
    u_w_in = upd(w_in, g_big["w_in"], m_w_in, v_w_in, "adamw_w_in")
    u_w_pool_o = upd(w_pool_o, g_big["w_pool_o"], m_w_pool_o, v_w_pool_o, "adamw_w_pool_o")
    u_w_hgrn_o = upd(w_hgrn_o, g_big["w_hgrn_o"], m_w_hgrn_o, v_w_hgrn_o, "adamw_w_hgrn_o")
    u_w_out = upd(w_out, g_big["w_out"], m_w_out, v_w_out, "adamw_w_out")
    g_small_fixed = _pack_small_weights(g_b_ada, g_g_pre, g_g_post, g_lb_logits, g_pool_w, g_pool_scale,
                                        g_norm_g)
    sw = _pack_small_weights(b_ada, g_pre, g_post, lb_logits, pool_w, pool_scale, hgrn_norm_g)
    sm = _pack_small_weights(m_b_ada, m_g_pre, m_g_post, m_lb_logits, m_pool_w, m_pool_scale, m_hgrn_norm_g)
    sv = _pack_small_weights(v_b_ada, v_g_pre, v_g_post, v_lb_logits, v_pool_w, v_pool_scale, v_hgrn_norm_g)
    u_small = [_unpack_small(t) for t in _adamw(sw, g_small_fixed, sm, sv, "adamw_small")]

    grads_out = (g_w_ada, g_b_ada, g_g_pre, g_g_post, g_big["w_in"], g_pool_w, g_pool_scale, g_lb_logits,
                 g_norm_g, g_big["w_pool_o"], g_big["w_hgrn_o"], g_big["w_out"])

    def ordered(k):
        s = u_small[k]
        return (u_w_ada[k], s[0], s[1], s[2], u_w_in[k], s[4], s[5], s[3], s[6], u_w_pool_o[k], u_w_hgrn_o[k],
                u_w_out[k])

    return (loss, dx0[None], *grads_out, *ordered(0), *ordered(1), *ordered(2))
```

```python
import functools

import jax
import jax.numpy as jnp
from jax import lax
from jax.experimental import pallas as pl
from jax.experimental.pallas import tpu as pltpu

F32 = jnp.float32
BF16 = jnp.bfloat16
MESH = pl.DeviceIdType.MESH

D = 1024
HEADS = 8
HD = 128
GROUPS = 4
POOL_W = 512
WINDOWS = (2, 4, 8, 16)
CH = 64
SB = 16
NH = 2
IN_W = 7168
NCHIP = 4
NDEV = 8
EPS = 1e-6
PV0, PG0, HQ0, HF0, HI0, HG0 = 0, 4, 8, 16, 24, 32
MGP_BLK, MGH_BLK = 5, 6

LR, B1, B2, AEPS, WD, STEP = 0.001, 0.9, 0.999, 1e-08, 0.01, 10
VMEM_LIMIT = 56 * 1024 * 1024


def _cp(sem=None, **kw):
    if sem is not None:
        kw["dimension_semantics"] = sem
    return pltpu.CompilerParams(vmem_limit_bytes=VMEM_LIMIT, **kw)


def _sig(z):
    return 1.0 / (1.0 + jnp.exp(-z))


def _dsilu(z, s):
    return s * (1.0 + z * (1.0 - s))


def _row_tile(rows, cap):
    if rows <= cap:
        return rows
    t = 1 << (cap.bit_length() - 1)
    while rows % t:
        t //= 2
    return t


def _mm(a, b, *, name, b_mode="nn", out_shards=0, tm=1024, tn=256, tk=None, out_dtype=F32):
    M, K = a.shape
    if b_mode == "nn":
        N = b.shape[1]
    elif b_mode == "nt":
        N = b.shape[0]
    elif b_mode == "nn_sh":
        N = b.shape[0] * b.shape[2]
    else:
        N = b.shape[1]
    tm = _row_tile(M, tm)
    if b_mode == "nn_sh":
        tn = _row_tile(b.shape[2], tn)
    elif out_shards:
        tn = _row_tile(N // out_shards, tn)
    else:
        tn = _row_tile(N, tn)
    if tk is None:
        tk = K if b_mode != "nt_shk" else b.shape[2]
    if b_mode == "nt_shk":
        tk = _row_tile(b.shape[2], tk)
    nm, nn, nk = M // tm, N // tn, K // tk

    a_spec = pl.BlockSpec((tm, tk), lambda m, n, k: (m, k))
    if b_mode == "nn":
        b_spec = pl.BlockSpec((tk, tn), lambda m, n, k: (k, n))
    elif b_mode == "nt":
        b_spec = pl.BlockSpec((tn, tk), lambda m, n, k: (n, k))
    elif b_mode == "nn_sh":
        nps = b.shape[2] // tn
        b_spec = pl.BlockSpec((None, tk, tn), lambda m, n, k: (n // nps, k, n % nps))
    else:
        kps = b.shape[2] // tk
        b_spec = pl.BlockSpec((None, tn, tk), lambda m, n, k: (k // kps, n, k % kps))
    if out_shards:
        ops = (N // out_shards) // tn
        o_spec = pl.BlockSpec((None, tm, tn), lambda m, n, k: (n // ops, m, n % ops))
        o_shape = jax.ShapeDtypeStruct((out_shards, M, N // out_shards), out_dtype)
    else:
        o_spec = pl.BlockSpec((tm, tn), lambda m, n, k: (m, n))
        o_shape = jax.ShapeDtypeStruct((M, N), out_dtype)
    trans_b = b_mode in ("nt", "nt_shk")
    dn = (((1,), (1,)), ((), ())) if trans_b else (((1,), (0,)), ((), ()))

    def body(a_ref, b_ref, o_ref, acc_ref):
        k = pl.program_id(2)

        @pl.when(k == 0)
        def _():
            acc_ref[...] = jnp.zeros(acc_ref.shape, F32)

        acc_ref[...] += lax.dot_general(a_ref[...].astype(BF16), b_ref[...].astype(BF16), dn,
                                        preferred_element_type=F32)

        @pl.when(k == nk - 1)
        def _():
            o_ref[...] = acc_ref[...].astype(o_ref.dtype)

    return pl.pallas_call(
        body, name=name, grid=(nm, nn, nk), in_specs=[a_spec, b_spec], out_specs=o_spec,
        out_shape=o_shape, scratch_shapes=[pltpu.VMEM((tm, tn), F32)],
        compiler_params=_cp(("parallel", "parallel", "arbitrary")),
    )(a, b)


def _rowvec(n=D):
    return pl.BlockSpec((1, n), lambda i: (0, 0))


def _prenorm_fwd(x, g, scale, shift, name):
    S = x.shape[0]
    tr = _row_tile(S, 256)

    def body(x_ref, g_ref, sc_ref, sh_ref, h_ref, ht_ref):
        xv = x_ref[...]
        r = lax.rsqrt(jnp.mean(xv * xv, axis=-1, keepdims=True) + EPS)
        hv = (xv * r) * g_ref[...] * (1.0 + sc_ref[...]) + sh_ref[...]
        h_ref[...] = hv.astype(BF16)
        ht_ref[...] = hv.T.astype(BF16)

    return pl.pallas_call(
        body, name=name, grid=(S // tr,),
        in_specs=[pl.BlockSpec((tr, D), lambda i: (i, 0)), _rowvec(), _rowvec(), _rowvec()],
        out_specs=[pl.BlockSpec((tr, D), lambda i: (i, 0)), pl.BlockSpec((D, tr), lambda i: (0, i))],
        out_shape=[jax.ShapeDtypeStruct((S, D), BF16), jax.ShapeDtypeStruct((D, S), BF16)],
        compiler_params=_cp(("parallel",)),
    )(x, g, scale, shift)


def _prenorm_bwd(dh, dxn, x, g, scale, name):
    S = x.shape[0]
    tr = _row_tile(S, 256)

    def body(dh_ref, dxn_ref, x_ref, g_ref, sc_ref, dx_ref, dsh_ref, dsc_ref, dg_ref):
        i = pl.program_id(0)

        @pl.when(i == 0)
        def _():
            dsh_ref[...] = jnp.zeros((1, D), F32)
            dsc_ref[...] = jnp.zeros((1, D), F32)
            dg_ref[...] = jnp.zeros((1, D), F32)

        xv = x_ref[...]
        dhv = dh_ref[...]
        gv = g_ref[...]
        mod = 1.0 + sc_ref[...]
        r = lax.rsqrt(jnp.mean(xv * xv, axis=-1, keepdims=True) + EPS)
        xh = xv * r
        dsh_ref[...] += jnp.sum(dhv, axis=0, keepdims=True)
        dsc_ref[...] += jnp.sum(dhv * (xh * gv), axis=0, keepdims=True)
        dg_ref[...] += jnp.sum(dhv * mod * xh, axis=0, keepdims=True)
        u = dhv * mod * gv
        dx_ref[...] = dxn_ref[...] + r * u - xv * (r * r * r) * jnp.mean(u * xv, axis=-1, keepdims=True)

    tile = pl.BlockSpec((tr, D), lambda i: (i, 0))
    return pl.pallas_call(
        body, name=name, grid=(S // tr,),
        in_specs=[tile, tile, tile, _rowvec(), _rowvec()],
        out_specs=[tile, _rowvec(), _rowvec(), _rowvec()],
        out_shape=[jax.ShapeDtypeStruct((S, D), F32)] + [jax.ShapeDtypeStruct((1, D), F32)] * 3,
        compiler_params=_cp(("arbitrary",)),
    )(dh, dxn, x, g, scale)


def _postnorm_fwd(x, y, gate, g, name):
    S = x.shape[0]
    tr = _row_tile(S, 256)

    def body(x_ref, y_ref, gate_ref, g_ref, o_ref):
        yv = y_ref[...]
        r = lax.rsqrt(jnp.mean(yv * yv, axis=-1, keepdims=True) + EPS)
        o_ref[...] = x_ref[...] + gate_ref[...] * ((yv * r) * g_ref[...])

    tile = pl.BlockSpec((tr, D), lambda i: (i, 0))
    return pl.pallas_call(
        body, name=name, grid=(S // tr,), in_specs=[tile, tile, _rowvec(), _rowvec()],
        out_specs=tile, out_shape=jax.ShapeDtypeStruct((S, D), F32), compiler_params=_cp(("parallel",)),
    )(x, y, gate, g)


def _postnorm_bwd(dxn, y, gate, g, name):
    S = y.shape[0]
    tr = _row_tile(S, 256)

    def body(dxn_ref, y_ref, gate_ref, g_ref, dy_ref, dgate_ref, dg_ref):
        i = pl.program_id(0)

        @pl.when(i == 0)
        def _():
            dgate_ref[...] = jnp.zeros((1, D), F32)
            dg_ref[...] = jnp.zeros((1, D), F32)

        yv = y_ref[...]
        dv = dxn_ref[...]
        gv = g_ref[...]
        gt = gate_ref[...]
        r = lax.rsqrt(jnp.mean(yv * yv, axis=-1, keepdims=True) + EPS)
        yh = yv * r
        dgate_ref[...] += jnp.sum(dv * (yh * gv), axis=0, keepdims=True)
        dg_ref[...] += jnp.sum(dv * gt * yh, axis=0, keepdims=True)
        u = dv * gt * gv
        dy_ref[...] = (r * u - yv * (r * r * r) * jnp.mean(u * yv, axis=-1, keepdims=True)).astype(BF16)

    tile = pl.BlockSpec((tr, D), lambda i: (i, 0))
    return pl.pallas_call(
        body, name=name, grid=(S // tr,), in_specs=[tile, tile, _rowvec(), _rowvec()],
        out_specs=[tile, _rowvec(), _rowvec()],
        out_shape=[jax.ShapeDtypeStruct((S, D), BF16), jax.ShapeDtypeStruct((1, D), F32),
                   jax.ShapeDtypeStruct((1, D), F32)],
        compiler_params=_cp(("arbitrary",)),
    )(dxn, y, gate, g)


def _loss_head(xo, target, name):
    S = xo.shape[0]
    tr = _row_tile(S, 256)

    def body(x_ref, t_ref, dx_ref, l_ref):
        i = pl.program_id(0)

        @pl.when(i == 0)
        def _():
            l_ref[...] = jnp.zeros((8, 128), F32)

        err = x_ref[...] - t_ref[...]
        dx_ref[...] = err * (1.0 / D)
        l_ref[...] += 0.5 * jnp.sum(jnp.mean(err * err, axis=-1, keepdims=True))

    tile = pl.BlockSpec((tr, D), lambda i: (i, 0))
    return pl.pallas_call(
        body, name=name, grid=(S // tr,), in_specs=[tile, tile],
        out_specs=[tile, pl.BlockSpec((8, 128), lambda i: (0, 0))],
        out_shape=[jax.ShapeDtypeStruct((S, D), F32), jax.ShapeDtypeStruct((8, 128), F32)],
        compiler_params=_cp(("arbitrary",)),
    )(xo, target)


def _merge_fwd(proj, br_a, br_b, name):
    S = proj.shape[0]
    tr = _row_tile(S, 256)

    def body(mgp_ref, mgh_ref, a_ref, b_ref, o_ref, ot_ref):
        mv = _sig(mgp_ref[...]) * a_ref[...] + _sig(mgh_ref[...]) * b_ref[...]
        o_ref[...] = mv.astype(BF16)
        ot_ref[...] = mv.T.astype(BF16)

    tile = pl.BlockSpec((tr, D), lambda i: (i, 0))
    return pl.pallas_call(
        body, name=name, grid=(S // tr,),
        in_specs=[pl.BlockSpec((tr, D), lambda i: (i, MGP_BLK)), pl.BlockSpec((tr, D), lambda i: (i, MGH_BLK)),
                  tile, tile],
        out_specs=[tile, pl.BlockSpec((D, tr), lambda i: (0, i))],
        out_shape=[jax.ShapeDtypeStruct((S, D), BF16), jax.ShapeDtypeStruct((D, S), BF16)],
        compiler_params=_cp(("parallel",)),
    )(proj, proj, br_a, br_b)


def _merge_bwd(dm, proj, br_a, br_b, name):
    S = proj.shape[0]
    tr = _row_tile(S, 256)

    def body(dm_ref, mgp_ref, mgh_ref, a_ref, b_ref, da_ref, db_ref, dmg_ref):
        dmv = dm_ref[...]
        sp = _sig(mgp_ref[...])
        sh = _sig(mgh_ref[...])
        da_ref[...] = (dmv * sp).astype(BF16)
        db_ref[...] = (dmv * sh).astype(BF16)
        dmg_ref[:, 0:D] = (dmv * a_ref[...] * sp * (1.0 - sp)).astype(BF16)
        dmg_ref[:, D:2 * D] = (dmv * b_ref[...] * sh * (1.0 - sh)).astype(BF16)

    tile = pl.BlockSpec((tr, D), lambda i: (i, 0))
    return pl.pallas_call(
        body, name=name, grid=(S // tr,),
        in_specs=[tile, pl.BlockSpec((tr, D), lambda i: (i, MGP_BLK)),
                  pl.BlockSpec((tr, D), lambda i: (i, MGH_BLK)), tile, tile],
        out_specs=[tile, tile, pl.BlockSpec((tr, 2 * D), lambda i: (i, 0))],
        out_shape=[jax.ShapeDtypeStruct((S, D), BF16), jax.ShapeDtypeStruct((S, D), BF16),
                   jax.ShapeDtypeStruct((S, 2 * D), BF16)],
        compiler_params=_cp(("parallel",)),
    )(dm, proj, proj, br_a, br_b)


def _pool_pieces(u, g, S):
    rowi = lax.broadcasted_iota(jnp.int32, (S, 1), 0)

    def down(z, k):
        return jnp.where(rowi >= k, pltpu.roll(z, k, axis=0), 0.0)

    s2 = u + down(u, 1)
    s4 = s2 + down(s2, 2)
    s8 = s4 + down(s4, 4)
    s16 = s8 + down(s8, 8)
    win = jnp.where(g == 0, s2, jnp.where(g == 1, s4, jnp.where(g == 2, s8, s16)))
    w = jnp.where(g == 0, 2, jnp.where(g == 1, 4, jnp.where(g == 2, 8, 16)))
    count = jnp.minimum(rowi + 1, w).astype(F32)
    return win / count - u, count, rowi


def _pool_fwd(proj, pw, pscale, name):
    S = proj.shape[0]

    def body(pv_ref, pg_ref, pw_ref, sc_ref, a_ref, at_ref):
        g = pl.program_id(0)
        pooled, _, _ = _pool_pieces(pv_ref[...], g, S)
        pm = jnp.dot(pooled.astype(BF16), pw_ref[...].astype(BF16), preferred_element_type=F32)
        pgv = pg_ref[...]
        av = pm * sc_ref[...] * (pgv * _sig(pgv))
        a_ref[...] = av.astype(BF16)
        at_ref[...] = av.T.astype(BF16)

    return pl.pallas_call(
        body, name=name, grid=(GROUPS,),
        in_specs=[pl.BlockSpec((S, 128), lambda g: (0, PV0 + g)), pl.BlockSpec((S, 128), lambda g: (0, PG0 + g)),
                  pl.BlockSpec((None, 128, 128), lambda g: (g, 0, 0)), pl.BlockSpec((1, 128), lambda g: (0, g))],
        out_specs=[pl.BlockSpec((S, 128), lambda g: (0, g)), pl.BlockSpec((128, S), lambda g: (g, 0))],
        out_shape=[jax.ShapeDtypeStruct((S, POOL_W), BF16), jax.ShapeDtypeStruct((POOL_W, S), BF16)],
        compiler_params=_cp(("parallel",)),
    )(proj, proj, pw, pscale)


def _pool_bwd(da, proj, pw, pscale, name):
    S = proj.shape[0]

    def body(da_ref, pv_ref, pg_ref, pw_ref, sc_ref, dpv_ref, dpg_ref, dpw_ref, dsc_ref):
        g = pl.program_id(0)
        pooled, count, rowi = _pool_pieces(pv_ref[...], g, S)
        pwb = pw_ref[...].astype(BF16)
        pm = jnp.dot(pooled.astype(BF16), pwb, preferred_element_type=F32)
        scv = sc_ref[...]
        pgv = pg_ref[...]
        sg = _sig(pgv)
        dav = da_ref[...]
        d_ps = dav * (pgv * sg)
        dpg_ref[...] = (dav * (pm * scv) * _dsilu(pgv, sg)).astype(BF16)
        dsc_ref[...] = jnp.sum(d_ps * pm, axis=0, keepdims=True)
        d_pm = (d_ps * scv).astype(BF16)
        dpw_ref[...] = lax.dot_general(pooled.astype(BF16), d_pm, (((0,), (0,)), ((), ())),
                                       preferred_element_type=F32)
        d_pooled = lax.dot_general(d_pm, pwb, (((1,), (1,)), ((), ())), preferred_element_type=F32)
        z = d_pooled / count

        def up(v, k):
            return jnp.where(rowi < S - k, pltpu.roll(v, S - k, axis=0), 0.0)

        t2 = z + up(z, 1)
        t4 = t2 + up(t2, 2)
        t8 = t4 + up(t4, 4)
        t16 = t8 + up(t8, 8)
        adj = jnp.where(g == 0, t2, jnp.where(g == 1, t4, jnp.where(g == 2, t8, t16)))
        dpv_ref[...] = (adj - d_pooled).astype(BF16)

    col = lambda g: (0, g)
    return pl.pallas_call(
        body, name=name, grid=(GROUPS,),
        in_specs=[pl.BlockSpec((S, 128), col), pl.BlockSpec((S, 128), lambda g: (0, PV0 + g)),
                  pl.BlockSpec((S, 128), lambda g: (0, PG0 + g)),
                  pl.BlockSpec((None, 128, 128), lambda g: (g, 0, 0)), pl.BlockSpec((1, 128), col)],
        out_specs=[pl.BlockSpec((S, 128), col), pl.BlockSpec((S, 128), col),
                   pl.BlockSpec((None, 128, 128), lambda g: (g, 0, 0)), pl.BlockSpec((1, 128), col)],
        out_shape=[jax.ShapeDtypeStruct((S, POOL_W), BF16), jax.ShapeDtypeStruct((S, POOL_W), BF16),
                   jax.ShapeDtypeStruct((GROUPS, 128, 128), F32), jax.ShapeDtypeStruct((1, POOL_W), F32)],
        compiler_params=_cp(("parallel",)),
    )(da, proj, proj, pw, pscale)


def _chunk_cumsum(z, rowi):
    for sh in (1, 2, 4, 8, 16, 32):
        z = z + jnp.where(rowi >= sh, pltpu.roll(z, sh, axis=0), 0.0)
    return z


def _chunk_rev_cumsum(z, rowi):
    for sh in (1, 2, 4, 8, 16, 32):
        z = z + jnp.where(rowi < CH - sh, pltpu.roll(z, CH - sh, axis=0), 0.0)
    return z


def _dot_nn(a, b):
    return jnp.dot(a.astype(BF16), b.astype(BF16), preferred_element_type=F32)


def _dot_nt(a, b):
    return lax.dot_general(a.astype(BF16), b.astype(BF16), (((1,), (1,)), ((), ())), preferred_element_type=F32)


def _dot_tn(a, b):
    return lax.dot_general(a.astype(BF16), b.astype(BF16), (((0,), (0,)), ((), ())), preferred_element_type=F32)


def _gates(hq, hf, lbv):
    sq = _sig(hq)
    sf = _sig(hf)
    f = lbv + (1.0 - lbv) * sf
    fc = jnp.maximum(f, 1e-30)
    return hq * sq, sq, sf, f, fc, jnp.log(fc)


def _hgrn_fwd(proj, lb, gn, name):
    S = proj.shape[0]
    nch = S // CH
    W = NH * HD

    def body(hq_ref, hf_ref, hi_ref, hg_ref, lb_ref, gn_ref, bin_ref, bint_ref, oraw_ref, st_ref,
             q_s, k_s, c_s, v_s, o_s, state_s):
        state_s[...] = jnp.zeros((NH, HD, HD), F32)
        rowi = lax.broadcasted_iota(jnp.int32, (CH, 1), 0)
        coli = lax.broadcasted_iota(jnp.int32, (1, CH), 1)
        sbi = lax.broadcasted_iota(jnp.int32, (SB, 1), 0)
        gnv = gn_ref[...]

        def one_head(hh, n, rows):
            lanes = slice(hh * HD, (hh + 1) * HD)
            q, _, _, f, _, logf = _gates(hq_ref[rows, lanes], hf_ref[rows, lanes], lb_ref[:, lanes])
            k = 1.0 - f
            v = hi_ref[rows, lanes]
            c = _chunk_cumsum(logf, rowi)
            q_s[hh] = q
            k_s[hh] = k
            c_s[hh] = c
            v_s[hh] = v
            st = state_s[hh]
            st_ref[hh, n] = st.astype(BF16)
            o = _dot_nt(q * jnp.exp(c), st)
            a_off = jnp.zeros((CH, CH), F32)
            for i in range(1, CH // SB):
                r_i = c_s[hh, SB * i - 1:SB * i, :]
                qi = q * jnp.exp(jnp.minimum(c - r_i, 0.0))
                kei = k * jnp.exp(jnp.minimum(r_i - c, 0.0))
                m_i = (rowi >= SB * i) & (rowi < SB * (i + 1)) & (coli < SB * i)
                a_off = a_off + jnp.where(m_i, _dot_nt(qi, kei), 0.0)
            o_s[hh] = o + _dot_nn(a_off, v)
            for i in range(CH // SB):
                blk = slice(SB * i, SB * (i + 1))
                qb = q_s[hh, blk, :]
                cb = c_s[hh, blk, :]
                acc = jnp.zeros((SB, HD), F32)
                for s in range(SB):
                    row = SB * i + s
                    w = jnp.exp(jnp.minimum(cb - c_s[hh, row:row + 1, :], 0.0))
                    a_col = jnp.sum(qb * k_s[hh, row:row + 1, :] * w, axis=-1, keepdims=True)
                    acc = acc + jnp.where(sbi >= s, a_col, 0.0) * v_s[hh, row:row + 1, :]
                o_s[hh, blk, :] += acc
            last = c_s[hh, CH - 1:CH, :]
            state_s[hh] = st * jnp.exp(last) + _dot_tn(v, k * jnp.exp(last - c))
            ov = o_s[hh]
            oraw_ref[rows, lanes] = ov
            r = lax.rsqrt(jnp.mean(ov * ov, axis=-1, keepdims=True) + EPS)
            hg = hg_ref[rows, lanes]
            bin_ref[rows, lanes] = ((ov * r) * gnv * (hg * _sig(hg))).astype(BF16)

        def chunk(n, carry):
            rows = pl.ds(pl.multiple_of(n * CH, CH), CH)
            for hh in range(NH):
                one_head(hh, n, rows)
            return carry

        lax.fori_loop(0, nch, chunk, 0)
        bint_ref[...] = bin_ref[...].astype(F32).T.astype(BF16)

    col = lambda off: pl.BlockSpec((S, W), lambda h: (0, off // NH + h))
    head = pl.BlockSpec((S, W), lambda h: (0, h))
    return pl.pallas_call(
        body, name=name, grid=(HEADS // NH,),
        in_specs=[col(HQ0), col(HF0), col(HI0), col(HG0), pl.BlockSpec((1, W), lambda h: (0, h)),
                  pl.BlockSpec((1, HD), lambda h: (0, 0))],
        out_specs=[head, pl.BlockSpec((W, S), lambda h: (h, 0)), head,
                   pl.BlockSpec((NH, nch, HD, HD), lambda h: (h, 0, 0, 0))],
        out_shape=[jax.ShapeDtypeStruct((S, D), BF16), jax.ShapeDtypeStruct((D, S), BF16),
                   jax.ShapeDtypeStruct((S, D), F32), jax.ShapeDtypeStruct((HEADS, nch, HD, HD), BF16)],
        scratch_shapes=[pltpu.VMEM((NH, CH, HD), F32)] * 5 + [pltpu.VMEM((NH, HD, HD), F32)],
        compiler_params=_cp(("parallel",)),
    )(proj, proj, proj, proj, lb, gn)


def _hgrn_bwd(dbin, proj, oraw, states, lb, gn, name):
    S = proj.shape[0]
    nch = S // CH
    W = NH * HD

    def body(db_ref, hq_ref, hf_ref, hi_ref, hg_ref, or_ref, st_ref, lb_ref, gn_ref,
             dq_ref, df_ref, di_ref, dg_ref, dlb_ref, dgn_ref,
             q_s, k_s, c_s, v_s, do_s, dq_s, dk_s, dv_s, dst_s, dlb_s, dgn_s):
        dst_s[...] = jnp.zeros((NH, HD, HD), F32)
        dlb_s[...] = jnp.zeros((1, W), F32)
        dgn_s[...] = jnp.zeros((1, HD), F32)
        rowi = lax.broadcasted_iota(jnp.int32, (CH, 1), 0)
        rowi2 = lax.broadcasted_iota(jnp.int32, (CH, CH), 0)
        coli2 = lax.broadcasted_iota(jnp.int32, (CH, CH), 1)
        sbi = lax.broadcasted_iota(jnp.int32, (SB, 1), 0)
        gnv = gn_ref[...]
        scr = (q_s, k_s, c_s, v_s, do_s, dq_s, dk_s, dv_s)

        def one_head(hh, n, rows, scr=scr):
            lanes = slice(hh * HD, (hh + 1) * HD)
            q_s, k_s, c_s, v_s, do_s, dq_s, dk_s, dv_s = [t.at[hh] for t in scr]
            lbv = lb_ref[:, lanes]
            hq = hq_ref[rows, lanes]
            q, sq, sf, f, fc, logf = _gates(hq, hf_ref[rows, lanes], lbv)
            k = 1.0 - f
            v = hi_ref[rows, lanes]
            c = _chunk_cumsum(logf, rowi)
            ov = or_ref[rows, lanes]
            hg = hg_ref[rows, lanes]
            sg = _sig(hg)
            r = lax.rsqrt(jnp.mean(ov * ov, axis=-1, keepdims=True) + EPS)
            dbv = db_ref[rows, lanes]
            d_on = dbv * (hg * sg)
            dg_ref[rows, lanes] = (dbv * ((ov * r) * gnv) * _dsilu(hg, sg)).astype(BF16)
            dgn_s[...] += jnp.sum(d_on * (ov * r), axis=0, keepdims=True)
            u = d_on * gnv
            do = r * u - ov * (r * r * r) * jnp.mean(u * ov, axis=-1, keepdims=True)
            q_s[...] = q
            k_s[...] = k
            c_s[...] = c
            v_s[...] = v
            do_s[...] = do
            st = st_ref[hh, n].astype(F32)
            dst = dst_s[hh]
            ec = jnp.exp(c)
            last = c_s[CH - 1:CH, :]
            el = jnp.exp(last - c)
            elast = jnp.exp(last)
            dq = _dot_nn(do, st) * ec
            dk = _dot_nn(v, dst) * el
            dv = _dot_nt(k * el, dst)
            dst_s[hh] = dst * elast + _dot_tn(do, q * ec)
            dcum = q * dq - k * dk
            dlast = jnp.sum(k * dk, axis=0, keepdims=True) + elast * jnp.sum(st * dst, axis=0, keepdims=True)
            d_a = _dot_nt(do, v).astype(BF16).astype(F32)
            d_at = d_a.T
            at_off = jnp.zeros((CH, CH), F32)
            for i in range(1, CH // SB):
                r_i = c_s[SB * i - 1:SB * i, :]
                eq = jnp.exp(jnp.minimum(c - r_i, 0.0))
                ek = jnp.exp(jnp.minimum(r_i - c, 0.0))
                qi = (q * eq).astype(BF16).astype(F32)
                kei = (k * ek).astype(BF16).astype(F32)
                m_ts = (rowi2 >= SB * i) & (rowi2 < SB * (i + 1)) & (coli2 < SB * i)
                m_st = (coli2 >= SB * i) & (coli2 < SB * (i + 1)) & (rowi2 < SB * i)
                at_off = at_off + jnp.where(m_st, _dot_nt(kei, qi), 0.0)
                dq_i = _dot_nn(jnp.where(m_ts, d_a, 0.0), kei)
                dk_i = _dot_nn(jnp.where(m_st, d_at, 0.0), qi)
                dq = dq + dq_i * eq
                dk = dk + dk_i * ek
                dcum = dcum + (qi * dq_i - kei * dk_i)
            dv = dv + _dot_nn(at_off, do)
            dq_s[...] = jnp.zeros((CH, HD), F32)
            dk_s[...] = jnp.zeros((CH, HD), F32)
            dv_s[...] = dv
            for i in range(CH // SB):
                blk = slice(SB * i, SB * (i + 1))
                qb = q_s[blk, :]
                cb = c_s[blk, :]
                dob = do_s[blk, :]
                dq_acc = jnp.zeros((SB, HD), F32)
                for s in range(SB):
                    row = SB * i + s
                    ks = k_s[row:row + 1, :]
                    vs = v_s[row:row + 1, :]
                    w = jnp.exp(jnp.minimum(cb - c_s[row:row + 1, :], 0.0))
                    live = sbi >= s
                    a_col = jnp.where(live, jnp.sum(qb * ks * w, axis=-1, keepdims=True), 0.0)
                    da_col = jnp.where(live, jnp.sum(dob * vs, axis=-1, keepdims=True), 0.0)
                    dq_acc = dq_acc + da_col * ks * w
                    dk_s[row:row + 1, :] += jnp.sum(da_col * qb * w, axis=0, keepdims=True)
                    dv_s[row:row + 1, :] += jnp.sum(a_col * dob, axis=0, keepdims=True)
                dq_s[blk, :] += dq_acc
            dq_d = dq_s[...]
            dk_d = dk_s[...]
            dq = dq + dq_d
            dk = dk + dk_d
            dcum = dcum + (q * dq_d - k * dk_d)
            dlogf = _chunk_rev_cumsum(dcum, rowi) + dlast
            dfv = jnp.where(f > 1e-30, dlogf / fc, 0.0) - dk
            dlb_s[:, lanes] += jnp.sum(dfv * (1.0 - sf), axis=0, keepdims=True)
            df_ref[rows, lanes] = (dfv * (1.0 - lbv) * sf * (1.0 - sf)).astype(BF16)
            dq_ref[rows, lanes] = (dq * _dsilu(hq, sq)).astype(BF16)
            di_ref[rows, lanes] = dv_s[...].astype(BF16)

        def chunk(j, carry):
            n = nch - 1 - j
            rows = pl.ds(pl.multiple_of(n * CH, CH), CH)
            for hh in range(NH):
                one_head(hh, n, rows)
            return carry

        lax.fori_loop(0, nch, chunk, 0)
        dlb_ref[...] = dlb_s[...]
        dgn_ref[...] = jnp.broadcast_to(dgn_s[...], (8, HD))

    col = lambda off: pl.BlockSpec((S, W), lambda h: (0, off // NH + h))
    head = pl.BlockSpec((S, W), lambda h: (0, h))
    vec = pl.BlockSpec((1, W), lambda h: (0, h))
    outs = pl.pallas_call(
        body, name=name, grid=(HEADS // NH,),
        in_specs=[head, col(HQ0), col(HF0), col(HI0), col(HG0), head,
                  pl.BlockSpec((NH, nch, HD, HD), lambda h: (h, 0, 0, 0)), vec,
                  pl.BlockSpec((1, HD), lambda h: (0, 0))],
        out_specs=[head, head, head, head, vec, pl.BlockSpec((8, HD), lambda h: (h, 0))],
        out_shape=[jax.ShapeDtypeStruct((S, D), BF16)] * 4
        + [jax.ShapeDtypeStruct((1, D), F32), jax.ShapeDtypeStruct((8 * HEADS // NH, HD), F32)],
        scratch_shapes=[pltpu.VMEM((NH, CH, HD), F32)] * 8
        + [pltpu.VMEM((NH, HD, HD), F32), pltpu.VMEM((1, W), F32), pltpu.VMEM((1, HD), F32)],
        compiler_params=_cp(("parallel",)),
    )(dbin, proj, proj, proj, proj, oraw, states, lb, gn)
    dq, df, di, dg, dlb, dgn = outs
    return dq, df, di, dg, dlb, dgn.reshape(HEADS // NH, 8, HD)[:, 0, :]


def _lower_bounds(l0, l1):
    m = jnp.maximum(l0, l1)
    e0 = jnp.exp(l0 - m)
    e1 = jnp.exp(l1 - m)
    tot = e0 + e1
    p0 = e0 / tot
    p1 = e1 / tot
    return jnp.clip(p0 - p0, 0.0, 1.0), jnp.clip((p0 + p1) - p0, 0.0, 1.0)


def _lb_fwd(logits):
    def body(l_ref, o_ref):
        lb0, lb1 = _lower_bounds(l_ref[0:1, :], l_ref[1:2, :])
        o_ref[0:1, :] = lb0
        o_ref[1:2, :] = lb1

    return pl.pallas_call(body, name="lb_fwd", out_shape=jax.ShapeDtypeStruct((2, D), F32))(logits)


def _lb_bwd(logits, dlb):
    def body(l_ref, d_ref, o_ref):
        _, vjp = jax.vjp(_lower_bounds, l_ref[0:1, :], l_ref[1:2, :])
        g0, g1 = vjp((d_ref[0:1, :], d_ref[1:2, :]))
        o_ref[0:1, :] = g0
        o_ref[1:2, :] = g1

    return pl.pallas_call(body, name="lb_bwd", out_shape=jax.ShapeDtypeStruct((2, D), F32))(logits, dlb)


ADA_PAD = 128


def _ada_fwd(c_pad, w_ada, b_sh):
    ns = w_ada.shape[2]

    def body(c_ref, w_ref, b_ref, o_ref):
        cv = c_ref[...]
        ca = (cv * _sig(cv)).astype(BF16)
        for l in range(2):
            res = jnp.dot(ca, w_ref[l].astype(BF16), preferred_element_type=F32)
            o_ref[:, l * ns:(l + 1) * ns] = res[0:NDEV, :] + b_ref[l:l + 1, :]

    return pl.pallas_call(body, name="ada_fwd", out_shape=jax.ShapeDtypeStruct((NDEV, 2 * ns), F32),
                          compiler_params=_cp())(c_pad, w_ada, b_sh)


def _ada_wgrad(c_pad_t, d_ada_sh):
    ns = d_ada_sh.shape[2]

    def body(c_ref, d_ref, o_ref):
        cv = c_ref[...]
        ca = (cv * _sig(cv)).astype(BF16)
        for l in range(2):
            o_ref[l] = jnp.dot(ca, d_ref[l].astype(BF16), preferred_element_type=F32)

    return pl.pallas_call(body, name="ada_wgrad", out_shape=jax.ShapeDtypeStruct((2, D, ns), F32),
                          compiler_params=_cp())(c_pad_t, d_ada_sh)


def _sum_devices(g):
    _, R, C = g.shape

    def body(g_ref, o_ref):
        acc = g_ref[0]
        for d in range(1, NDEV):
            acc = acc + g_ref[d]
        o_ref[...] = acc

    return pl.pallas_call(body, name="sum_devices", out_shape=jax.ShapeDtypeStruct((R, C), F32),
                          compiler_params=_cp())(g)


def _adamw(w, g, m, v, name):
    R, C = w.shape
    tr = _row_tile(R, max(8, (1 << 19) // C))

    def body(w_ref, g_ref, m_ref, v_ref, d_ref, nm_ref, nv_ref):
        gv = g_ref[...]
        nm = B1 * m_ref[...] + (1.0 - B1) * gv
        nv = B2 * v_ref[...] + (1.0 - B2) * (gv * gv)
        m_hat = nm / (1.0 - B1 ** STEP)
        v_hat = nv / (1.0 - B2 ** STEP)
        d_ref[...] = -LR * (m_hat / (jnp.sqrt(v_hat) + AEPS) + WD * w_ref[...])
        nm_ref[...] = nm
        nv_ref[...] = nv

    tile = pl.BlockSpec((tr, C), lambda i: (i, 0))
    return pl.pallas_call(
        body, name=name, grid=(R // tr,), in_specs=[tile] * 4, out_specs=[tile] * 3,
        out_shape=[jax.ShapeDtypeStruct((R, C), F32)] * 3, compiler_params=_cp(("parallel",)),
    )(w, g, m, v)


def _cast_to_slot(place, w, l, name):
    _, R, C = w.shape
    tr = _row_tile(R, max(8, (1 << 19) // C))

    def body(p_ref, w_ref, o_ref):
        o_ref[...] = w_ref[...].astype(BF16)

    return pl.pallas_call(
        body, name=name, out_shape=jax.ShapeDtypeStruct((NCHIP, R, C), BF16),
        grid_spec=pltpu.PrefetchScalarGridSpec(
            num_scalar_prefetch=1, grid=(R // tr,),
            in_specs=[pl.BlockSpec((None, tr, C), lambda i, p_ref: (l, i, 0))],
            out_specs=pl.BlockSpec((None, tr, C), lambda i, p_ref: (p_ref[0], i, 0))),
        compiler_params=_cp(("parallel",)),
    )(place, w)


def _pair_add(core, g, got, name):
    _, R, C = g.shape
    r2 = R // 2
    tr = _row_tile(r2, max(8, (1 << 19) // C))
    nt = r2 // tr

    def body(c_ref, a_ref, b_ref, o_ref):
        o_ref[...] = (a_ref[...].astype(F32) + b_ref[...].astype(F32)).astype(o_ref.dtype)

    return pl.pallas_call(
        body, name=name, out_shape=jax.ShapeDtypeStruct((NCHIP, r2, C), BF16),
        grid_spec=pltpu.PrefetchScalarGridSpec(
            num_scalar_prefetch=1, grid=(NCHIP, nt),
            in_specs=[pl.BlockSpec((None, tr, C), lambda j, i, c_ref: (j, c_ref[0] * nt + i, 0)),
                      pl.BlockSpec((None, tr, C), lambda j, i, c_ref: (j, i, 0))],
            out_specs=pl.BlockSpec((None, tr, C), lambda j, i, c_ref: (j, i, 0))),
        compiler_params=_cp(("parallel", "parallel")),
    )(core, g, got)


def _chip_sum(place, part, recv, name):
    _, r2, C = part.shape
    tr = _row_tile(r2, max(8, (1 << 18) // C))
    nt = r2 // tr

    def body(p_ref, own_ref, r_ref, o_ref):
        me = p_ref[0]
        own = own_ref[...].astype(F32)
        acc = None
        for j in range(NCHIP):
            slot = jnp.minimum(jnp.where(j > me, j - 1, j), NCHIP - 2)
            term = jnp.where(me == j, own, r_ref[slot].astype(F32))
            acc = term if acc is None else acc + term
        o_ref[...] = acc

    return pl.pallas_call(
        body, name=name, out_shape=jax.ShapeDtypeStruct((2 * r2, C), F32),
        grid_spec=pltpu.PrefetchScalarGridSpec(
            num_scalar_prefetch=1, grid=(nt,),
            in_specs=[pl.BlockSpec((None, tr, C), lambda i, p_ref: (p_ref[0], i, 0)),
                      pl.BlockSpec((NCHIP - 1, tr, C), lambda i, p_ref: (0, i, 0))],
            out_specs=pl.BlockSpec((tr, C), lambda i, p_ref: (p_ref[1] * nt + i, 0))),
        compiler_params=_cp(("parallel",)),
    )(place, part, recv)


ANY = pl.BlockSpec(memory_space=pl.ANY)


def _place():
    x, y, c = lax.axis_index("x"), lax.axis_index("y"), lax.axis_index("c")
    chips = [(1 - x, y), (x, 1 - y), (1 - x, 1 - y)]
    return x, y, c, chips


def _gather_small(blk, name):
    m_per, n = blk.shape

    def body(x_ref, out_ref, send_sems, recv_sems, local_sem):
        x, y, c, chips = _place()
        me, sibling = (x, y, c), (x, y, 1 - c)

        def rows(px, py, pc):
            return out_ref.at[pl.ds((4 * px + 2 * py + pc) * m_per, m_per), :]

        def copy(k, block, to, src=None):
            return pltpu.make_async_remote_copy(
                src_ref=rows(*block) if src is None else src, dst_ref=rows(*block),
                send_sem=send_sems.at[k], recv_sem=recv_sems.at[k], device_id=to, device_id_type=MESH)

        mine = pltpu.make_async_copy(x_ref, rows(*me), local_sem)
        mine.start()
        first = [copy(0, me, sibling, src=x_ref)]
        first += [copy(1 + j, me, (*chip, c), src=x_ref) for j, chip in enumerate(chips)]
        for cp in first:
            cp.start()
        passed = [copy(4 + j, (*chip, c), sibling) for j, chip in enumerate(chips)]
        for j, chip in enumerate(chips):
            copy(1 + j, (*chip, c), me).wait_recv()
            passed[j].start()
        copy(0, sibling, me).wait_recv()
        for j, chip in enumerate(chips):
            copy(4 + j, (*chip, 1 - c), me).wait_recv()
        for cp in first + passed:
            cp.wait_send()
        mine.wait()

    return pl.pallas_call(
        body, name=name, out_shape=jax.ShapeDtypeStruct((NDEV * m_per, n), blk.dtype),
        in_specs=[pl.BlockSpec(memory_space=pltpu.VMEM)], out_specs=pl.BlockSpec(memory_space=pltpu.VMEM),
        scratch_shapes=[pltpu.SemaphoreType.DMA((7,)), pltpu.SemaphoreType.DMA((7,)), pltpu.SemaphoreType.DMA],
        compiler_params=_cp(),
    )(blk)


def _gather_weights(shards):
    n = len(shards)

    def body(*refs):
        outs = refs[n:2 * n]
        send_sems, recv_sems = refs[2 * n:]
        x, y, c, chips = _place()
        me = 2 * x + y
        sends = []
        for a in range(n):
            r2 = outs[a].shape[1] // 2
            half = pl.ds(c * r2, r2)
            for j, (cx, cy) in enumerate(chips):
                cp = pltpu.make_async_remote_copy(
                    src_ref=outs[a].at[me, half, :], dst_ref=outs[a].at[me, half, :],
                    send_sem=send_sems.at[a, j], recv_sem=recv_sems.at[a, j],
                    device_id=(cx, cy, c), device_id_type=MESH)
                cp.start()
                sends.append(cp)
        for a in range(n):
            r2 = outs[a].shape[1] // 2
            half = pl.ds(c * r2, r2)
            for j, (cx, cy) in enumerate(chips):
                blk = outs[a].at[2 * cx + cy, half, :]
                pltpu.make_async_remote_copy(
                    src_ref=blk, dst_ref=blk, send_sem=send_sems.at[a, j], recv_sem=recv_sems.at[a, j],
                    device_id=(cx, cy, c), device_id_type=MESH).wait_recv()
                fw = pltpu.make_async_remote_copy(
                    src_ref=blk, dst_ref=blk, send_sem=send_sems.at[a, 3 + j], recv_sem=recv_sems.at[a, 3 + j],
                    device_id=(x, y, 1 - c), device_id_type=MESH)
                fw.start()
                sends.append(fw)
        for a in range(n):
            r2 = outs[a].shape[1] // 2
            other = pl.ds((1 - c) * r2, r2)
            for j, (cx, cy) in enumerate(chips):
                blk = outs[a].at[2 * cx + cy, other, :]
                pltpu.make_async_remote_copy(
                    src_ref=blk, dst_ref=blk, send_sem=send_sems.at[a, 3 + j], recv_sem=recv_sems.at[a, 3 + j],
                    device_id=(x, y, 1 - c), device_id_type=MESH).wait_recv()
        for cp in sends:
            cp.wait_send()

    return pl.pallas_call(
        body, name="gather_weights",
        out_shape=[jax.ShapeDtypeStruct(s.shape, s.dtype) for s in shards],
        in_specs=[ANY] * n, out_specs=[ANY] * n, input_output_aliases={a: a for a in range(n)},
        scratch_shapes=[pltpu.SemaphoreType.DMA((n, 6)), pltpu.SemaphoreType.DMA((n, 6))],
        compiler_params=_cp(),
    )(*shards)


def _rs_pair(grads):
    n = len(grads)

    def body(*refs):
        ins, gots = refs[:n], refs[n:2 * n]
        send_sems, recv_sems = refs[2 * n:]
        x, y, c, _ = _place()
        cps = []
        for a in range(n):
            r2 = ins[a].shape[1] // 2
            cp = pltpu.make_async_remote_copy(
                src_ref=ins[a].at[:, pl.ds((1 - c) * r2, r2), :], dst_ref=gots[a],
                send_sem=send_sems.at[a], recv_sem=recv_sems.at[a],
                device_id=(x, y, 1 - c), device_id_type=MESH)
            cp.start()
            cps.append(cp)
        for cp in cps:
            cp.wait()

    half = [jax.ShapeDtypeStruct((NCHIP, g.shape[1] // 2, g.shape[2]), g.dtype) for g in grads]
    return pl.pallas_call(
        body, name="rs_pair", out_shape=half, in_specs=[ANY] * n, out_specs=[ANY] * n,
        scratch_shapes=[pltpu.SemaphoreType.DMA((n,)), pltpu.SemaphoreType.DMA((n,))],
        compiler_params=_cp(),
    )(*grads)


def _rs_chips(parts):
    n = len(parts)

    def body(*refs):
        ins, outs = refs[:n], refs[n:2 * n]
        send_sems, recv_sems = refs[2 * n:]
        x, y, c, chips = _place()
        me = 2 * x + y
        cps = []
        for a in range(n):
            for j, (cx, cy) in enumerate(chips):
                them = 2 * cx + cy
                cp = pltpu.make_async_remote_copy(
                    src_ref=ins[a].at[them], dst_ref=outs[a].at[me - (me > them).astype(jnp.int32)],
                    send_sem=send_sems.at[a, j], recv_sem=recv_sems.at[a, j],
                    device_id=(cx, cy, c), device_id_type=MESH)
                cp.start()
                cps.append(cp)
        for a in range(n):
            for j, (cx, cy) in enumerate(chips):
                them = 2 * cx + cy
                blk = outs[a].at[them - (them > me).astype(jnp.int32)]
                pltpu.make_async_remote_copy(
                    src_ref=blk, dst_ref=blk, send_sem=send_sems.at[a, j], recv_sem=recv_sems.at[a, j],
                    device_id=(cx, cy, c), device_id_type=MESH).wait_recv()
        for cp in cps:
            cp.wait_send()

    return pl.pallas_call(
        body, name="rs_chips",
        out_shape=[jax.ShapeDtypeStruct((NCHIP - 1,) + p.shape[1:], p.dtype) for p in parts],
        in_specs=[ANY] * n, out_specs=[ANY] * n,
        scratch_shapes=[pltpu.SemaphoreType.DMA((n, 3)), pltpu.SemaphoreType.DMA((n, 3))],
        compiler_params=_cp(),
    )(*parts)


def _rs_swap(fulls):
    n = len(fulls)

    def body(*refs):
        outs = refs[n:2 * n]
        send_sems, recv_sems = refs[2 * n:]
        x, y, c, _ = _place()
        cps = []
        for a in range(n):
            r2 = outs[a].shape[0] // 2
            mine = outs[a].at[pl.ds(c * r2, r2), :]
            cp = pltpu.make_async_remote_copy(
                src_ref=mine, dst_ref=mine, send_sem=send_sems.at[a], recv_sem=recv_sems.at[a],
                device_id=(x, y, 1 - c), device_id_type=MESH)
            cp.start()
            cps.append(cp)
        for a in range(n):
            r2 = outs[a].shape[0] // 2
            blk = outs[a].at[pl.ds((1 - c) * r2, r2), :]
            pltpu.make_async_remote_copy(
                src_ref=blk, dst_ref=blk, send_sem=send_sems.at[a], recv_sem=recv_sems.at[a],
                device_id=(x, y, 1 - c), device_id_type=MESH).wait_recv()
        for cp in cps:
            cp.wait_send()

    return pl.pallas_call(
        body, name="rs_swap", out_shape=[jax.ShapeDtypeStruct(f.shape, f.dtype) for f in fulls],
        in_specs=[ANY] * n, out_specs=[ANY] * n, input_output_aliases={a: a for a in range(n)},
        scratch_shapes=[pltpu.SemaphoreType.DMA((n,)), pltpu.SemaphoreType.DMA((n,))],
        compiler_params=_cp(),
    )(*fulls)


def _layer_fwd(l, x, ada, w, small):
    shift, scale, gate = ada[:, 0:D], ada[:, D:2 * D], ada[:, 2 * D:3 * D]
    h, h_t = _prenorm_fwd(x, small["g_pre"][l], scale, shift, f"prenorm_fwd{l}")
    proj = _mm(h, w["w_in"][l], name=f"proj{l}", b_mode="nn_sh")
    a_in, a_in_t = _pool_fwd(proj, small["pool_w"][l], small["pool_scale"][l], f"pool_fwd{l}")
    b_in, b_in_t, o_raw, states = _hgrn_fwd(proj, small["lb"][l], small["hgrn_norm_g"][l], f"hgrn_fwd{l}")
    br_a = _mm(a_in, w["w_pool_o"][l], name=f"branch_a{l}", b_mode="nn_sh")
    br_b = _mm(b_in, w["w_hgrn_o"][l].reshape(D, D), name=f"branch_b{l}")
    merged, merged_t = _merge_fwd(proj, br_a, br_b, f"merge_fwd{l}")
    y = _mm(merged, w["w_out"][l].reshape(D, D), name=f"out_proj{l}")
    x_new = _postnorm_fwd(x, y, gate, small["g_post"][l], f"postnorm_fwd{l}")
    saved = dict(x=x, h_t=h_t, proj=proj, a_in_t=a_in_t, b_in_t=b_in_t, o_raw=o_raw, states=states,
                 br_a=br_a, br_b=br_b, merged_t=merged_t, y=y, scale=scale, gate=gate)
    return x_new, saved


def _layer_bwd(l, dxn, sv, w, small):
    dy, dgate, dg_post = _postnorm_bwd(dxn, sv["y"], sv["gate"], small["g_post"][l], f"postnorm_bwd{l}")
    w_out = w["w_out"][l].reshape(D, D)
    dmerged = _mm(dy, w_out, name=f"d_merged{l}", b_mode="nt")
    gw_out = _mm(sv["merged_t"], dy, name=f"gw_out{l}", out_dtype=BF16)
    dbr_a, dbr_b, dmg = _merge_bwd(dmerged, sv["proj"], sv["br_a"], sv["br_b"], f"merge_bwd{l}")
    da_in = _mm(dbr_a, w["w_pool_o"][l], name=f"d_a_in{l}", b_mode="nt_shk")
    gw_pool_o = _mm(sv["a_in_t"], dbr_a, name=f"gw_pool_o{l}", out_shards=NCHIP, out_dtype=BF16)
    db_in = _mm(dbr_b, w["w_hgrn_o"][l].reshape(D, D), name=f"d_b_in{l}", b_mode="nt")
    gw_hgrn_o = _mm(sv["b_in_t"], dbr_b, name=f"gw_hgrn_o{l}", out_dtype=BF16)
    dhq, dhf, dhi, dhg, dlb, dgn = _hgrn_bwd(db_in, sv["proj"], sv["o_raw"], sv["states"], small["lb"][l],
                                             small["hgrn_norm_g"][l], f"hgrn_bwd{l}")
    dpv, dpg, dpw, dpsc = _pool_bwd(da_in, sv["proj"], small["pool_w"][l], small["pool_scale"][l],
                                    f"pool_bwd{l}")
    dproj = jnp.concatenate([dpv, dpg, dhq, dhf, dhi, dhg, dmg], axis=1)
    dh = _mm(dproj, w["w_in"][l], name=f"d_h{l}", b_mode="nt_shk")
    gw_in = _mm(sv["h_t"], dproj, name=f"gw_in{l}", out_shards=NCHIP, out_dtype=BF16)
    dx, dshift, dscale, dg_pre = _prenorm_bwd(dh, dxn, sv["x"], small["g_pre"][l], sv["scale"],
                                              f"prenorm_bwd{l}")
    big = dict(w_in=gw_in, w_pool_o=gw_pool_o, w_hgrn_o=gw_hgrn_o.reshape(NCHIP, D // NCHIP, D),
               w_out=gw_out.reshape(NCHIP, D // NCHIP, D))
    little = dict(d_ada=jnp.concatenate([dshift, dscale, dgate], axis=1), g_pre=dg_pre, g_post=dg_post,
                  pool_w=dpw, pool_scale=dpsc, lb=dlb, hgrn_norm_g=jnp.sum(dgn, axis=0, keepdims=True))
    return dx, big, little


SMALL_ROWS = 176


def _rows8(t):
    t = t.reshape(-1, D)
    return jnp.pad(t, ((0, -t.shape[0] % 8), (0, 0)))


def _pack_small_weights(b_ada, g_pre, g_post, lb_logits, pool_w, pool_scale, hgrn_norm_g):
    gn = jnp.pad(hgrn_norm_g.reshape(1, 2 * HD), ((0, 0), (0, D - 2 * HD)))
    return jnp.concatenate([_rows8(b_ada), _rows8(g_pre), _rows8(g_post), _rows8(lb_logits), _rows8(pool_w),
                            _rows8(pool_scale), _rows8(gn)], axis=0)


def _pack_small(parts):
    both = lambda key: jnp.stack([parts[l][key] for l in range(2)])
    return _pack_small_weights(both("d_ada"), both("g_pre"), both("g_post"), both("lb"), both("pool_w"),
                               both("pool_scale"), both("hgrn_norm_g"))


def _unpack_small(p):
    return (p[0:6].reshape(2, 3 * D), p[8:10], p[16:18], p[24:26], p[32:160].reshape(2, GROUPS, 128, 128),
            p[160:161].reshape(2, POOL_W), p[168:169, 0:2 * HD].reshape(2, HD))


def kernel(x, c, w_ada, b_ada, g_pre, g_post, w_in, pool_w, pool_scale, lb_logits, hgrn_norm_g, w_pool_o, w_hgrn_o, w_out, loss_target, m_w_ada, m_b_ada, m_g_pre, m_g_post, m_w_in, m_pool_w, m_pool_scale, m_lb_logits, m_hgrn_norm_g, m_w_pool_o, m_w_hgrn_o, m_w_out, v_w_ada, v_b_ada, v_g_pre, v_g_post, v_w_in, v_pool_w, v_pool_scale, v_lb_logits, v_hgrn_norm_g, v_w_pool_o, v_w_hgrn_o, v_w_out):
    ax, ay, ac = lax.axis_index("x"), lax.axis_index("y"), lax.axis_index("c")
    chip = 2 * ax + ay
    dev = 2 * chip + ac
    xe, te = x[0], loss_target[0]
    ada_s = w_ada.shape[2]

    big_names = ("w_in", "w_pool_o", "w_hgrn_o", "w_out")
    big_w = (w_in, w_pool_o, w_hgrn_o, w_out)
    core = jnp.stack([ac]).astype(jnp.int32)
    place = jnp.stack([chip, ac]).astype(jnp.int32)
    gathered = _gather_weights([_cast_to_slot(place, t, l, f"cast_{k}{l}")
                                for l in range(2) for k, t in zip(big_names, big_w)])
    w = {k: (gathered[i], gathered[4 + i]) for i, k in enumerate(big_names)}

    c_all = _gather_small(jnp.broadcast_to(c, (8, D)), "gather_c").reshape(NDEV, 8, D)[:, 0, :]
    c_pad = jnp.pad(c_all, ((0, ADA_PAD - NDEV), (0, 0)))
    b_sh = lax.dynamic_slice(b_ada, (0, chip * ada_s), (2, ada_s))
    ada_cols = _gather_small(_ada_fwd(c_pad, w_ada, b_sh), "gather_ada")
    ada_cols = ada_cols.reshape(NCHIP, 2, NDEV, 2, ada_s)[:, 0]
    ada_all = jnp.transpose(ada_cols, (2, 1, 0, 3)).reshape(2, NDEV, 3 * D)
    ada_me = lax.dynamic_slice(ada_all, (0, dev, 0), (2, 1, 3 * D))

    lbs = _lb_fwd(lb_logits)
    small = dict(g_pre=g_pre[:, None, :], g_post=g_post[:, None, :], pool_w=pool_w,
                 pool_scale=pool_scale[:, None, :], lb=lbs[:, None, :], hgrn_norm_g=hgrn_norm_g[:, None, :])

    x1, sv0 = _layer_fwd(0, xe, ada_me[0], w, small)
    x2, sv1 = _layer_fwd(1, x1, ada_me[1], w, small)
    dx2, loss_blk = _loss_head(x2, te, "loss_head")
    dx1, big1, little1 = _layer_bwd(1, dx2, sv1, w, small)
    dx0, big0, little0 = _layer_bwd(0, dx1, sv0, w, small)
    loss = lax.psum(loss_blk[0, 0], ("x", "y", "c"))

    grads = [big0[k] for k in big_names] + [big1[k] for k in big_names]
    got = _rs_pair(grads)
    parts = [_pair_add(core, g, o, f"rs_add{i}") for i, (g, o) in enumerate(zip(grads, got))]
    recv = _rs_chips(parts)
    red = _rs_swap([_chip_sum(place, p, r, f"rs_sum{i}") for i, (p, r) in enumerate(zip(parts, recv))])
    g_big = {k: jnp.stack([red[i], red[4 + i]]) for i, k in enumerate(big_names)}

    packed = _gather_small(_pack_small([little0, little1]), "gather_small")
    packed = packed.reshape(NDEV, SMALL_ROWS, D)
    g_small = _sum_devices(packed)
    g_b_ada, g_g_pre, g_g_post, g_lb, g_pool_w, g_pool_scale, g_norm_g = _unpack_small(g_small)
    g_lb_logits = _lb_bwd(lb_logits, g_lb)
    d_ada_all = packed[:, 0:6, :].reshape(NDEV, 2, 3 * D)
    d_ada_sh = lax.dynamic_slice(jnp.transpose(d_ada_all, (1, 0, 2)), (0, 0, chip * ada_s), (2, NDEV, ada_s))
    d_ada_sh = jnp.pad(d_ada_sh, ((0, 0), (0, ADA_PAD - NDEV), (0, 0)))
    g_w_ada = _ada_wgrad(c_pad.T, d_ada_sh)

    def upd(wt, g, m, v, name):
        shp = wt.shape
        two = lambda t: t.reshape(-1, shp[-1])
        d, nm, nv = _adamw(two(wt), two(g), two(m), two(v), name)
        return d.reshape(shp), nm.reshape(shp), nv.reshape(shp)

    u_w_ada = upd(w_ada, g_w_ada, m_w_ada, v_w_ada, "adamw_w_ada")
    u_w_in = upd(w_in, g_big["w_in"], m_w_in, v_w_in, "adamw_w_in")
    u_w_pool_o = upd(w_pool_o, g_big["w_pool_o"], m_w_pool_o, v_w_pool_o, "adamw_w_pool_o")
    u_w_hgrn_o = upd(w_hgrn_o, g_big["w_hgrn_o"], m_w_hgrn_o, v_w_hgrn_o, "adamw_w_hgrn_o")
    u_w_out = upd(w_out, g_big["w_out"], m_w_out, v_w_out, "adamw_w_out")
    g_small_fixed = _pack_small_weights(g_b_ada, g_g_pre, g_g_post, g_lb_logits, g_pool_w, g_pool_scale,
                                        g_norm_g)
    sw = _pack_small_weights(b_ada, g_pre, g_post, lb_logits, pool_w, pool_scale, hgrn_norm_g)
    sm = _pack_small_weights(m_b_ada, m_g_pre, m_g_post, m_lb_logits, m_pool_w, m_pool_scale, m_hgrn_norm_g)
    sv = _pack_small_weights(v_b_ada, v_g_pre, v_g_post, v_lb_logits, v_pool_w, v_pool_scale, v_hgrn_norm_g)
    u_small = [_unpack_small(t) for t in _adamw(sw, g_small_fixed, sm, sv, "adamw_small")]

    grads_out = (g_w_ada, g_b_ada, g_g_pre, g_g_post, g_big["w_in"], g_pool_w, g_pool_scale, g_lb_logits,
                 g_norm_g, g_big["w_pool_o"], g_big["w_hgrn_o"], g_big["w_out"])

    def ordered(k):
        s = u_small[k]
        return (u_w_ada[k], s[0], s[1], s[2], u_w_in[k], s[4], s[5], s[3], s[6], u_w_pool_o[k], u_w_hgrn_o[k],
                u_w_out[k])

    return (loss, dx0[None], *grads_out, *ordered(0), *ordered(1), *ordered(2))
```

```python
import functools

import jax
import jax.numpy as jnp
from jax import lax
from jax.experimental import pallas as pl
from jax.experimental.pallas import tpu as pltpu

F32 = jnp.float32
BF16 = jnp.bfloat16
MESH = pl.DeviceIdType.MESH

D = 1024
HEADS = 8
HD = 128
GROUPS = 4
POOL_W = 512
WINDOWS = (2, 4, 8, 16)
CH = 64
SB = 16
NH = 2
IN_W = 7168
NCHIP = 4
NDEV = 8
EPS = 1e-6
PV0, PG0, HQ0, HF0, HI0, HG0 = 0, 4, 8, 16, 24, 32
MGP_BLK, MGH_BLK = 5, 6

LR, B1, B2, AEPS, WD, STEP = 0.001, 0.9, 0.999, 1e-08, 0.01, 10
VMEM_LIMIT = 56 * 1024 * 1024


def _cp(sem=None, **kw):
    if sem is not None:
        kw["dimension_semantics"] = sem
    return pltpu.CompilerParams(vmem_limit_bytes=VMEM_LIMIT, **kw)


def _sig(z):
    return 1.0 / (1.0 + jnp.exp(-z))


def _dsilu(z, s):
    return s * (1.0 + z * (1.0 - s))


def _row_tile(rows, cap):
    if rows <= cap:
        return rows
    t = 1 << (cap.bit_length() - 1)
    while rows % t:
        t //= 2
    return t


ANY = pl.BlockSpec(memory_space=pl.ANY)


class _Carry:
    def __init__(self, ins, outs, aliases, n_sem, start, finish):
        self.ins, self.outs, self.aliases, self.n_sem = list(ins), list(outs), dict(aliases), n_sem
        self.start, self.finish = start, finish


def _call(body, *, name, grid, in_specs, out_specs, out_shape, args, scratch_shapes=(), sem=None, carry=None):
    in_specs, out_specs, out_shape = list(in_specs), list(out_specs), list(out_shape)
    scratch_shapes = list(scratch_shapes)
    if carry is None:
        outs = pl.pallas_call(body, name=name, grid=grid, in_specs=in_specs, out_specs=out_specs,
                              out_shape=out_shape, scratch_shapes=scratch_shapes,
                              compiler_params=_cp(sem))(*args)
        return list(outs)
    n_in, n_out, n_scr = len(in_specs), len(out_specs), len(scratch_shapes)
    c_in, c_out = len(carry.ins), len(carry.outs)

    def wrapped(*refs):
        k_in, rest = refs[:n_in], refs[n_in:]
        ci, rest = rest[:c_in], rest[c_in:]
        k_out, rest = rest[:n_out], rest[n_out:]
        co, rest = rest[:c_out], rest[c_out:]
        k_scr, (ssem, rsem) = rest[:n_scr], rest[n_scr:]
        pids = [pl.program_id(d) for d in range(len(grid))]
        first = functools.reduce(jnp.logical_and, [p == 0 for p in pids])
        last = functools.reduce(jnp.logical_and, [p == g - 1 for p, g in zip(pids, grid)])

        @pl.when(first)
        def _():
            carry.start(ci, co, ssem, rsem)

        body(*k_in, *k_out, *k_scr)

        @pl.when(last)
        def _():
            carry.finish(ci, co, ssem, rsem)

    outs = pl.pallas_call(
        wrapped, name=name, grid=grid, in_specs=in_specs + [ANY] * c_in, out_specs=out_specs + [ANY] * c_out,
        out_shape=out_shape + carry.outs,
        input_output_aliases={n_in + i: n_out + o for i, o in carry.aliases.items()},
        scratch_shapes=scratch_shapes + [pltpu.SemaphoreType.DMA((carry.n_sem,))] * 2,
        compiler_params=_cp(("arbitrary",) * len(grid)),
    )(*args, *carry.ins)
    return list(outs)


def _run_carry(carry, name):
    c_in, c_out = len(carry.ins), len(carry.outs)

    def body(*refs):
        ci, co, (ssem, rsem) = refs[:c_in], refs[c_in:c_in + c_out], refs[c_in + c_out:]
        carry.start(ci, co, ssem, rsem)
        carry.finish(ci, co, ssem, rsem)

    outs = pl.pallas_call(
        body, name=name, in_specs=[ANY] * c_in, out_specs=[ANY] * c_out, out_shape=carry.outs,
        input_output_aliases=carry.aliases,
        scratch_shapes=[pltpu.SemaphoreType.DMA((carry.n_sem,))] * 2, compiler_params=_cp(),
    )(*carry.ins)
    return list(outs)


def _mm(a, b, *, name, b_mode="nn", out_shards=0, tm=1024, tn=256, tk=None, out_dtype=F32, carry=None):
    M, K = a.shape
    if b_mode == "nn":
        N = b.shape[1]
    elif b_mode == "nt":
        N = b.shape[0]
    elif b_mode == "nn_sh":
        N = b.shape[0] * b.shape[2]
    else:
        N = b.shape[1]
    tm = _row_tile(M, tm)
    if b_mode == "nn_sh":
        tn = _row_tile(b.shape[2], tn)
    elif out_shards:
        tn = _row_tile(N // out_shards, tn)
    else:
        tn = _row_tile(N, tn)
    if tk is None:
        tk = K if b_mode != "nt_shk" else b.shape[2]
    if b_mode == "nt_shk":
        tk = _row_tile(b.shape[2], tk)
    nm, nn, nk = M // tm, N // tn, K // tk

    a_spec = pl.BlockSpec((tm, tk), lambda m, n, k: (m, k))
    if b_mode == "nn":
        b_spec = pl.BlockSpec((tk, tn), lambda m, n, k: (k, n))
    elif b_mode == "nt":
        b_spec = pl.BlockSpec((tn, tk), lambda m, n, k: (n, k))
    elif b_mode == "nn_sh":
        nps = b.shape[2] // tn
        b_spec = pl.BlockSpec((None, tk, tn), lambda m, n, k: (n // nps, k, n % nps))
    else:
        kps = b.shape[2] // tk
        b_spec = pl.BlockSpec((None, tn, tk), lambda m, n, k: (k // kps, n, k % kps))
    if out_shards:
        ops = (N // out_shards) // tn
        o_spec = pl.BlockSpec((None, tm, tn), lambda m, n, k: (n // ops, m, n % ops))
        o_shape = jax.ShapeDtypeStruct((out_shards, M, N // out_shards), out_dtype)
    else:
        o_spec = pl.BlockSpec((tm, tn), lambda m, n, k: (m, n))
        o_shape = jax.ShapeDtypeStruct((M, N), out_dtype)
    trans_b = b_mode in ("nt", "nt_shk")
    dn = (((1,), (1,)), ((), ())) if trans_b else (((1,), (0,)), ((), ()))

    def body(a_ref, b_ref, o_ref, acc_ref):
        k = pl.program_id(2)

        @pl.when(k == 0)
        def _():
            acc_ref[...] = jnp.zeros(acc_ref.shape, F32)

        acc_ref[...] += lax.dot_general(a_ref[...].astype(BF16), b_ref[...].astype(BF16), dn,
                                        preferred_element_type=F32)

        @pl.when(k == nk - 1)
        def _():
            o_ref[...] = acc_ref[...].astype(o_ref.dtype)

    outs = _call(body, name=name, grid=(nm, nn, nk), in_specs=[a_spec, b_spec], out_specs=[o_spec],
                 out_shape=[o_shape], scratch_shapes=[pltpu.VMEM((tm, tn), F32)],
                 sem=("parallel", "parallel", "arbitrary"), args=(a, b), carry=carry)
    return outs[0] if carry is None else (outs[0], outs[1:])


def _rowvec(n=D):
    return pl.BlockSpec((1, n), lambda i: (0, 0))


def _prenorm_fwd(x, g, scale, shift, name):
    S = x.shape[0]
    tr = _row_tile(S, 256)

    def body(x_ref, g_ref, sc_ref, sh_ref, h_ref, ht_ref):
        xv = x_ref[...]
        r = lax.rsqrt(jnp.mean(xv * xv, axis=-1, keepdims=True) + EPS)
        hv = (xv * r) * g_ref[...] * (1.0 + sc_ref[...]) + sh_ref[...]
        h_ref[...] = hv.astype(BF16)
        ht_ref[...] = hv.T.astype(BF16)

    return pl.pallas_call(
        body, name=name, grid=(S // tr,),
        in_specs=[pl.BlockSpec((tr, D), lambda i: (i, 0)), _rowvec(), _rowvec(), _rowvec()],
        out_specs=[pl.BlockSpec((tr, D), lambda i: (i, 0)), pl.BlockSpec((D, tr), lambda i: (0, i))],
        out_shape=[jax.ShapeDtypeStruct((S, D), BF16), jax.ShapeDtypeStruct((D, S), BF16)],
        compiler_params=_cp(("parallel",)),
    )(x, g, scale, shift)


def _prenorm_bwd(dh, dxn, x, g, scale, name):
    S = x.shape[0]
    tr = _row_tile(S, 256)

    def body(dh_ref, dxn_ref, x_ref, g_ref, sc_ref, dx_ref, dsh_ref, dsc_ref, dg_ref):
        i = pl.program_id(0)

        @pl.when(i == 0)
        def _():
            dsh_ref[...] = jnp.zeros((1, D), F32)
            dsc_ref[...] = jnp.zeros((1, D), F32)
            dg_ref[...] = jnp.zeros((1, D), F32)

        xv = x_ref[...]
        dhv = dh_ref[...]
        gv = g_ref[...]
        mod = 1.0 + sc_ref[...]
        r = lax.rsqrt(jnp.mean(xv * xv, axis=-1, keepdims=True) + EPS)
        xh = xv * r
        dsh_ref[...] += jnp.sum(dhv, axis=0, keepdims=True)
        dsc_ref[...] += jnp.sum(dhv * (xh * gv), axis=0, keepdims=True)
        dg_ref[...] += jnp.sum(dhv * mod * xh, axis=0, keepdims=True)
        u = dhv * mod * gv
        dx_ref[...] = dxn_ref[...] + r * u - xv * (r * r * r) * jnp.mean(u * xv, axis=-1, keepdims=True)

    tile = pl.BlockSpec((tr, D), lambda i: (i, 0))
    return pl.pallas_call(
        body, name=name, grid=(S // tr,),
        in_specs=[tile, tile, tile, _rowvec(), _rowvec()],
        out_specs=[tile, _rowvec(), _rowvec(), _rowvec()],
        out_shape=[jax.ShapeDtypeStruct((S, D), F32)] + [jax.ShapeDtypeStruct((1, D), F32)] * 3,
        compiler_params=_cp(("arbitrary",)),
    )(dh, dxn, x, g, scale)


def _postnorm_fwd(x, y, gate, g, name):
    S = x.shape[0]
    tr = _row_tile(S, 256)

    def body(x_ref, y_ref, gate_ref, g_ref, o_ref):
        yv = y_ref[...]
        r = lax.rsqrt(jnp.mean(yv * yv, axis=-1, keepdims=True) + EPS)
        o_ref[...] = x_ref[...] + gate_ref[...] * ((yv * r) * g_ref[...])

    tile = pl.BlockSpec((tr, D), lambda i: (i, 0))
    return pl.pallas_call(
        body, name=name, grid=(S // tr,), in_specs=[tile, tile, _rowvec(), _rowvec()],
        out_specs=tile, out_shape=jax.ShapeDtypeStruct((S, D), F32), compiler_params=_cp(("parallel",)),
    )(x, y, gate, g)


def _postnorm_bwd(dxn, y, gate, g, name):
    S = y.shape[0]
    tr = _row_tile(S, 256)

    def body(dxn_ref, y_ref, gate_ref, g_ref, dy_ref, dgate_ref, dg_ref):
        i = pl.program_id(0)

        @pl.when(i == 0)
        def _():
            dgate_ref[...] = jnp.zeros((1, D), F32)
            dg_ref[...] = jnp.zeros((1, D), F32)

        yv = y_ref[...]
        dv = dxn_ref[...]
        gv = g_ref[...]
        gt = gate_ref[...]
        r = lax.rsqrt(jnp.mean(yv * yv, axis=-1, keepdims=True) + EPS)
        yh = yv * r
        dgate_ref[...] += jnp.sum(dv * (yh * gv), axis=0, keepdims=True)
        dg_ref[...] += jnp.sum(dv * gt * yh, axis=0, keepdims=True)
        u = dv * gt * gv
        dy_ref[...] = (r * u - yv * (r * r * r) * jnp.mean(u * yv, axis=-1, keepdims=True)).astype(BF16)

    tile = pl.BlockSpec((tr, D), lambda i: (i, 0))
    return pl.pallas_call(
        body, name=name, grid=(S // tr,), in_specs=[tile, tile, _rowvec(), _rowvec()],
        out_specs=[tile, _rowvec(), _rowvec()],
        out_shape=[jax.ShapeDtypeStruct((S, D), BF16), jax.ShapeDtypeStruct((1, D), F32),
                   jax.ShapeDtypeStruct((1, D), F32)],
        compiler_params=_cp(("arbitrary",)),
    )(dxn, y, gate, g)


def _loss_head(xo, target, name):
    S = xo.shape[0]
    tr = _row_tile(S, 256)

    def body(x_ref, t_ref, dx_ref, l_ref):
        i = pl.program_id(0)

        @pl.when(i == 0)
        def _():
            l_ref[...] = jnp.zeros((8, 128), F32)

        err = x_ref[...] - t_ref[...]
        dx_ref[...] = err * (1.0 / D)
        l_ref[...] += 0.5 * jnp.sum(jnp.mean(err * err, axis=-1, keepdims=True))

    tile = pl.BlockSpec((tr, D), lambda i: (i, 0))
    return pl.pallas_call(
        body, name=name, grid=(S // tr,), in_specs=[tile, tile],
        out_specs=[tile, pl.BlockSpec((8, 128), lambda i: (0, 0))],
        out_shape=[jax.ShapeDtypeStruct((S, D), F32), jax.ShapeDtypeStruct((8, 128), F32)],
        compiler_params=_cp(("arbitrary",)),
    )(xo, target)


def _merge_fwd(proj, br_a, br_b, name):
    S = proj.shape[0]
    tr = _row_tile(S, 256)

    def body(mgp_ref, mgh_ref, a_ref, b_ref, o_ref, ot_ref):
        mv = _sig(mgp_ref[...]) * a_ref[...] + _sig(mgh_ref[...]) * b_ref[...]
        o_ref[...] = mv.astype(BF16)
        ot_ref[...] = mv.T.astype(BF16)

    tile = pl.BlockSpec((tr, D), lambda i: (i, 0))
    return pl.pallas_call(
        body, name=name, grid=(S // tr,),
        in_specs=[pl.BlockSpec((tr, D), lambda i: (i, MGP_BLK)), pl.BlockSpec((tr, D), lambda i: (i, MGH_BLK)),
                  tile, tile],
        out_specs=[tile, pl.BlockSpec((D, tr), lambda i: (0, i))],
        out_shape=[jax.ShapeDtypeStruct((S, D), BF16), jax.ShapeDtypeStruct((D, S), BF16)],
        compiler_params=_cp(("parallel",)),
    )(proj, proj, br_a, br_b)


def _merge_bwd(dm, proj, br_a, br_b, name):
    S = proj.shape[0]
    tr = _row_tile(S, 256)

    def body(dm_ref, mgp_ref, mgh_ref, a_ref, b_ref, da_ref, db_ref, dmg_ref):
        dmv = dm_ref[...]
        sp = _sig(mgp_ref[...])
        sh = _sig(mgh_ref[...])
        da_ref[...] = (dmv * sp).astype(BF16)
        db_ref[...] = (dmv * sh).astype(BF16)
        dmg_ref[:, 0:D] = (dmv * a_ref[...] * sp * (1.0 - sp)).astype(BF16)
        dmg_ref[:, D:2 * D] = (dmv * b_ref[...] * sh * (1.0 - sh)).astype(BF16)

    tile = pl.BlockSpec((tr, D), lambda i: (i, 0))
    return pl.pallas_call(
        body, name=name, grid=(S // tr,),
        in_specs=[tile, pl.BlockSpec((tr, D), lambda i: (i, MGP_BLK)),
                  pl.BlockSpec((tr, D), lambda i: (i, MGH_BLK)), tile, tile],
        out_specs=[tile, tile, pl.BlockSpec((tr, 2 * D), lambda i: (i, 0))],
        out_shape=[jax.ShapeDtypeStruct((S, D), BF16), jax.ShapeDtypeStruct((S, D), BF16),
                   jax.ShapeDtypeStruct((S, 2 * D), BF16)],
        compiler_params=_cp(("parallel",)),
    )(dm, proj, proj, br_a, br_b)


def _pool_pieces(u, g, S):
    rowi = lax.broadcasted_iota(jnp.int32, (S, 1), 0)

    def down(z, k):
        return jnp.where(rowi >= k, pltpu.roll(z, k, axis=0), 0.0)

    s2 = u + down(u, 1)
    s4 = s2 + down(s2, 2)
    s8 = s4 + down(s4, 4)
    s16 = s8 + down(s8, 8)
    win = jnp.where(g == 0, s2, jnp.where(g == 1, s4, jnp.where(g == 2, s8, s16)))
    w = jnp.where(g == 0, 2, jnp.where(g == 1, 4, jnp.where(g == 2, 8, 16)))
    count = jnp.minimum(rowi + 1, w).astype(F32)
    return win / count - u, count, rowi


def _pool_fwd(proj, pw, pscale, name):
    S = proj.shape[0]

    def body(pv_ref, pg_ref, pw_ref, sc_ref, a_ref, at_ref):
        g = pl.program_id(0)
        pooled, _, _ = _pool_pieces(pv_ref[...], g, S)
        pm = jnp.dot(pooled.astype(BF16), pw_ref[...].astype(BF16), preferred_element_type=F32)
        pgv = pg_ref[...]
        av = pm * sc_ref[...] * (pgv * _sig(pgv))
        a_ref[...] = av.astype(BF16)
        at_ref[...] = av.T.astype(BF16)

    return pl.pallas_call(
        body, name=name, grid=(GROUPS,),
        in_specs=[pl.BlockSpec((S, 128), lambda g: (0, PV0 + g)), pl.BlockSpec((S, 128), lambda g: (0, PG0 + g)),
                  pl.BlockSpec((None, 128, 128), lambda g: (g, 0, 0)), pl.BlockSpec((1, 128), lambda g: (0, g))],
        out_specs=[pl.BlockSpec((S, 128), lambda g: (0, g)), pl.BlockSpec((128, S), lambda g: (g, 0))],
        out_shape=[jax.ShapeDtypeStruct((S, POOL_W), BF16), jax.ShapeDtypeStruct((POOL_W, S), BF16)],
        compiler_params=_cp(("parallel",)),
    )(proj, proj, pw, pscale)


def _pool_bwd(da, proj, pw, pscale, name):
    S = proj.shape[0]

    def body(da_ref, pv_ref, pg_ref, pw_ref, sc_ref, dpv_ref, dpg_ref, dpw_ref, dsc_ref):
        g = pl.program_id(0)
        pooled, count, rowi = _pool_pieces(pv_ref[...], g, S)
        pwb = pw_ref[...].astype(BF16)
        pm = jnp.dot(pooled.astype(BF16), pwb, preferred_element_type=F32)
        scv = sc_ref[...]
        pgv = pg_ref[...]
        sg = _sig(pgv)
        dav = da_ref[...]
        d_ps = dav * (pgv * sg)
        dpg_ref[...] = (dav * (pm * scv) * _dsilu(pgv, sg)).astype(BF16)
        dsc_ref[...] = jnp.sum(d_ps * pm, axis=0, keepdims=True)
        d_pm = (d_ps * scv).astype(BF16)
        dpw_ref[...] = lax.dot_general(pooled.astype(BF16), d_pm, (((0,), (0,)), ((), ())),
                                       preferred_element_type=F32)
        d_pooled = lax.dot_general(d_pm, pwb, (((1,), (1,)), ((), ())), preferred_element_type=F32)
        z = d_pooled / count

        def up(v, k):
            return jnp.where(rowi < S - k, pltpu.roll(v, S - k, axis=0), 0.0)

        t2 = z + up(z, 1)
        t4 = t2 + up(t2, 2)
        t8 = t4 + up(t4, 4)
        t16 = t8 + up(t8, 8)
        adj = jnp.where(g == 0, t2, jnp.where(g == 1, t4, jnp.where(g == 2, t8, t16)))
        dpv_ref[...] = (adj - d_pooled).astype(BF16)

    col = lambda g: (0, g)
    return pl.pallas_call(
        body, name=name, grid=(GROUPS,),
        in_specs=[pl.BlockSpec((S, 128), col), pl.BlockSpec((S, 128), lambda g: (0, PV0 + g)),
                  pl.BlockSpec((S, 128), lambda g: (0, PG0 + g)),
                  pl.BlockSpec((None, 128, 128), lambda g: (g, 0, 0)), pl.BlockSpec((1, 128), col)],
        out_specs=[pl.BlockSpec((S, 128), col), pl.BlockSpec((S, 128), col),
                   pl.BlockSpec((None, 128, 128), lambda g: (g, 0, 0)), pl.BlockSpec((1, 128), col)],
        out_shape=[jax.ShapeDtypeStruct((S, POOL_W), BF16), jax.ShapeDtypeStruct((S, POOL_W), BF16),
                   jax.ShapeDtypeStruct((GROUPS, 128, 128), F32), jax.ShapeDtypeStruct((1, POOL_W), F32)],
        compiler_params=_cp(("parallel",)),
    )(da, proj, proj, pw, pscale)


def _chunk_cumsum(z, rowi):
    for sh in (1, 2, 4, 8, 16, 32):
        z = z + jnp.where(rowi >= sh, pltpu.roll(z, sh, axis=0), 0.0)
    return z


def _chunk_rev_cumsum(z, rowi):
    for sh in (1, 2, 4, 8, 16, 32):
        z = z + jnp.where(rowi < CH - sh, pltpu.roll(z, CH - sh, axis=0), 0.0)
    return z


def _dot_nn(a, b):
    return jnp.dot(a.astype(BF16), b.astype(BF16), preferred_element_type=F32)


def _dot_nt(a, b):
    return lax.dot_general(a.astype(BF16), b.astype(BF16), (((1,), (1,)), ((), ())), preferred_element_type=F32)


def _dot_tn(a, b):
    return lax.dot_general(a.astype(BF16), b.astype(BF16), (((0,), (0,)), ((), ())), preferred_element_type=F32)


def _gates(hq, hf, lbv):
    sq = _sig(hq)
    sf = _sig(hf)
    f = lbv + (1.0 - lbv) * sf
    fc = jnp.maximum(f, 1e-30)
    return hq * sq, sq, sf, f, fc, jnp.log(fc)


def _hgrn_fwd(proj, lb, gn, name, carry=None):
    S = proj.shape[0]
    nch = S // CH
    W = NH * HD

    def body(hq_ref, hf_ref, hi_ref, hg_ref, lb_ref, gn_ref, bin_ref, bint_ref, oraw_ref, st_ref,
             q_s, k_s, c_s, v_s, o_s, state_s):
        state_s[...] = jnp.zeros((NH, HD, HD), F32)
        rowi = lax.broadcasted_iota(jnp.int32, (CH, 1), 0)
        coli = lax.broadcasted_iota(jnp.int32, (1, CH), 1)
        sbi = lax.broadcasted_iota(jnp.int32, (SB, 1), 0)
        gnv = gn_ref[...]

        def one_head(hh, n, rows):
            lanes = slice(hh * HD, (hh + 1) * HD)
            q, _, _, f, _, logf = _gates(hq_ref[rows, lanes], hf_ref[rows, lanes], lb_ref[:, lanes])
            k = 1.0 - f
            v = hi_ref[rows, lanes]
            c = _chunk_cumsum(logf, rowi)
            q_s[hh] = q
            k_s[hh] = k
            c_s[hh] = c
            v_s[hh] = v
            st = state_s[hh]
            st_ref[hh, n] = st.astype(BF16)
            o = _dot_nt(q * jnp.exp(c), st)
            a_off = jnp.zeros((CH, CH), F32)
            for i in range(1, CH // SB):
                r_i = c_s[hh, SB * i - 1:SB * i, :]
                qi = q * jnp.exp(jnp.minimum(c - r_i, 0.0))
                kei = k * jnp.exp(jnp.minimum(r_i - c, 0.0))
                m_i = (rowi >= SB * i) & (rowi < SB * (i + 1)) & (coli < SB * i)
                a_off = a_off + jnp.where(m_i, _dot_nt(qi, kei), 0.0)
            o_s[hh] = o + _dot_nn(a_off, v)
            for i in range(CH // SB):
                blk = slice(SB * i, SB * (i + 1))
                qb = q_s[hh, blk, :]
                cb = c_s[hh, blk, :]
                acc = jnp.zeros((SB, HD), F32)
                for s in range(SB):
                    row = SB * i + s
                    w = jnp.exp(jnp.minimum(cb - c_s[hh, row:row + 1, :], 0.0))
                    a_col = jnp.sum(qb * k_s[hh, row:row + 1, :] * w, axis=-1, keepdims=True)
                    acc = acc + jnp.where(sbi >= s, a_col, 0.0) * v_s[hh, row:row + 1, :]
                o_s[hh, blk, :] += acc
            last = c_s[hh, CH - 1:CH, :]
            state_s[hh] = st * jnp.exp(last) + _dot_tn(v, k * jnp.exp(last - c))
            ov = o_s[hh]
            oraw_ref[rows, lanes] = ov
            r = lax.rsqrt(jnp.mean(ov * ov, axis=-1, keepdims=True) + EPS)
            hg = hg_ref[rows, lanes]
            bin_ref[rows, lanes] = ((ov * r) * gnv * (hg * _sig(hg))).astype(BF16)

        def chunk(n, carry):
            rows = pl.ds(pl.multiple_of(n * CH, CH), CH)
            for hh in range(NH):
                one_head(hh, n, rows)
            return carry

        lax.fori_loop(0, nch, chunk, 0)
        bint_ref[...] = bin_ref[...].astype(F32).T.astype(BF16)

    col = lambda off: pl.BlockSpec((S, W), lambda h: (0, off // NH + h))
    head = pl.BlockSpec((S, W), lambda h: (0, h))
    outs = _call(
        body, name=name, grid=(HEADS // NH,),
        in_specs=[col(HQ0), col(HF0), col(HI0), col(HG0), pl.BlockSpec((1, W), lambda h: (0, h)),
                  pl.BlockSpec((1, HD), lambda h: (0, 0))],
        out_specs=[head, pl.BlockSpec((W, S), lambda h: (h, 0)), head,
                   pl.BlockSpec((NH, nch, HD, HD), lambda h: (h, 0, 0, 0))],
        out_shape=[jax.ShapeDtypeStruct((S, D), BF16), jax.ShapeDtypeStruct((D, S), BF16),
                   jax.ShapeDtypeStruct((S, D), F32), jax.ShapeDtypeStruct((HEADS, nch, HD, HD), BF16)],
        scratch_shapes=[pltpu.VMEM((NH, CH, HD), F32)] * 5 + [pltpu.VMEM((NH, HD, HD), F32)],
        sem=("parallel",), args=(proj, proj, proj, proj, lb, gn), carry=carry)
    return outs[:4], outs[4:]


def _hgrn_bwd(dbin, proj, oraw, states, lb, gn, name, carry=None):
    S = proj.shape[0]
    nch = S // CH
    W = NH * HD

    def body(db_ref, hq_ref, hf_ref, hi_ref, hg_ref, or_ref, st_ref, lb_ref, gn_ref,
             dq_ref, df_ref, di_ref, dg_ref, dlb_ref, dgn_ref,
             q_s, k_s, c_s, v_s, do_s, dq_s, dk_s, dv_s, dst_s, dlb_s, dgn_s):
        dst_s[...] = jnp.zeros((NH, HD, HD), F32)
        dlb_s[...] = jnp.zeros((1, W), F32)
        dgn_s[...] = jnp.zeros((1, HD), F32)
        rowi = lax.broadcasted_iota(jnp.int32, (CH, 1), 0)
        rowi2 = lax.broadcasted_iota(jnp.int32, (CH, CH), 0)
        coli2 = lax.broadcasted_iota(jnp.int32, (CH, CH), 1)
        sbi = lax.broadcasted_iota(jnp.int32, (SB, 1), 0)
        gnv = gn_ref[...]
        scr = (q_s, k_s, c_s, v_s, do_s, dq_s, dk_s, dv_s)

        def one_head(hh, n, rows, scr=scr):
            lanes = slice(hh * HD, (hh + 1) * HD)
            q_s, k_s, c_s, v_s, do_s, dq_s, dk_s, dv_s = [t.at[hh] for t in scr]
            lbv = lb_ref[:, lanes]
            hq = hq_ref[rows, lanes]
            q, sq, sf, f, fc, logf = _gates(hq, hf_ref[rows, lanes], lbv)
            k = 1.0 - f
            v = hi_ref[rows, lanes]
            c = _chunk_cumsum(logf, rowi)
            ov = or_ref[rows, lanes]
            hg = hg_ref[rows, lanes]
            sg = _sig(hg)
            r = lax.rsqrt(jnp.mean(ov * ov, axis=-1, keepdims=True) + EPS)
            dbv = db_ref[rows, lanes]
            d_on = dbv * (hg * sg)
            dg_ref[rows, lanes] = (dbv * ((ov * r) * gnv) * _dsilu(hg, sg)).astype(BF16)
            dgn_s[...] += jnp.sum(d_on * (ov * r), axis=0, keepdims=True)
            u = d_on * gnv
            do = r * u - ov * (r * r * r) * jnp.mean(u * ov, axis=-1, keepdims=True)
            q_s[...] = q
            k_s[...] = k
            c_s[...] = c
            v_s[...] = v
            do_s[...] = do
            st = st_ref[hh, n].astype(F32)
            dst = dst_s[hh]
            ec = jnp.exp(c)
            last = c_s[CH - 1:CH, :]
            el = jnp.exp(last - c)
            elast = jnp.exp(last)
            dq = _dot_nn(do, st) * ec
            dk = _dot_nn(v, dst) * el
            dv = _dot_nt(k * el, dst)
            dst_s[hh] = dst * elast + _dot_tn(do, q * ec)
            dcum = q * dq - k * dk
            dlast = jnp.sum(k * dk, axis=0, keepdims=True) + elast * jnp.sum(st * dst, axis=0, keepdims=True)
            d_a = _dot_nt(do, v).astype(BF16).astype(F32)
            d_at = d_a.T
            at_off = jnp.zeros((CH, CH), F32)
            for i in range(1, CH // SB):
                r_i = c_s[SB * i - 1:SB * i, :]
                eq = jnp.exp(jnp.minimum(c - r_i, 0.0))
                ek = jnp.exp(jnp.minimum(r_i - c, 0.0))
                qi = (q * eq).astype(BF16).astype(F32)
                kei = (k * ek).astype(BF16).astype(F32)
                m_ts = (rowi2 >= SB * i) & (rowi2 < SB * (i + 1)) & (coli2 < SB * i)
                m_st = (coli2 >= SB * i) & (coli2 < SB * (i + 1)) & (rowi2 < SB * i)
                at_off = at_off + jnp.where(m_st, _dot_nt(kei, qi), 0.0)
                dq_i = _dot_nn(jnp.where(m_ts, d_a, 0.0), kei)
                dk_i = _dot_nn(jnp.where(m_st, d_at, 0.0), qi)
                dq = dq + dq_i * eq
                dk = dk + dk_i * ek
                dcum = dcum + (qi * dq_i - kei * dk_i)
            dv = dv + _dot_nn(at_off, do)
            dq_s[...] = jnp.zeros((CH, HD), F32)
            dk_s[...] = jnp.zeros((CH, HD), F32)
            dv_s[...] = dv
            for i in range(CH // SB):
                blk = slice(SB * i, SB * (i + 1))
                qb = q_s[blk, :]
                cb = c_s[blk, :]
                dob = do_s[blk, :]
                dq_acc = jnp.zeros((SB, HD), F32)
                for s in range(SB):
                    row = SB * i + s
                    ks = k_s[row:row + 1, :]
                    vs = v_s[row:row + 1, :]
                    w = jnp.exp(jnp.minimum(cb - c_s[row:row + 1, :], 0.0))
                    live = sbi >= s
                    a_col = jnp.where(live, jnp.sum(qb * ks * w, axis=-1, keepdims=True), 0.0)
                    da_col = jnp.where(live, jnp.sum(dob * vs, axis=-1, keepdims=True), 0.0)
                    dq_acc = dq_acc + da_col * ks * w
                    dk_s[row:row + 1, :] += jnp.sum(da_col * qb * w, axis=0, keepdims=True)
                    dv_s[row:row + 1, :] += jnp.sum(a_col * dob, axis=0, keepdims=True)
                dq_s[blk, :] += dq_acc
            dq_d = dq_s[...]
            dk_d = dk_s[...]
            dq = dq + dq_d
            dk = dk + dk_d
            dcum = dcum + (q * dq_d - k * dk_d)
            dlogf = _chunk_rev_cumsum(dcum, rowi) + dlast
            dfv = jnp.where(f > 1e-30, dlogf / fc, 0.0) - dk
            dlb_s[:, lanes] += jnp.sum(dfv * (1.0 - sf), axis=0, keepdims=True)
            df_ref[rows, lanes] = (dfv * (1.0 - lbv) * sf * (1.0 - sf)).astype(BF16)
            dq_ref[rows, lanes] = (dq * _dsilu(hq, sq)).astype(BF16)
            di_ref[rows, lanes] = dv_s[...].astype(BF16)

        def chunk(j, carry):
            n = nch - 1 - j
            rows = pl.ds(pl.multiple_of(n * CH, CH), CH)
            for hh in range(NH):
                one_head(hh, n, rows)
            return carry

        lax.fori_loop(0, nch, chunk, 0)
        dlb_ref[...] = dlb_s[...]
        dgn_ref[...] = jnp.broadcast_to(dgn_s[...], (8, HD))

    col = lambda off: pl.BlockSpec((S, W), lambda h: (0, off // NH + h))
    head = pl.BlockSpec((S, W), lambda h: (0, h))
    vec = pl.BlockSpec((1, W), lambda h: (0, h))
    outs = _call(
        body, name=name, grid=(HEADS // NH,),
        in_specs=[head, col(HQ0), col(HF0), col(HI0), col(HG0), head,
                  pl.BlockSpec((NH, nch, HD, HD), lambda h: (h, 0, 0, 0)), vec,
                  pl.BlockSpec((1, HD), lambda h: (0, 0))],
        out_specs=[head, head, head, head, vec, pl.BlockSpec((8, HD), lambda h: (h, 0))],
        out_shape=[jax.ShapeDtypeStruct((S, D), BF16)] * 4
        + [jax.ShapeDtypeStruct((1, D), F32), jax.ShapeDtypeStruct((8 * HEADS // NH, HD), F32)],
        scratch_shapes=[pltpu.VMEM((NH, CH, HD), F32)] * 8
        + [pltpu.VMEM((NH, HD, HD), F32), pltpu.VMEM((1, W), F32), pltpu.VMEM((1, HD), F32)],
        sem=("parallel",), args=(dbin, proj, proj, proj, proj, oraw, states, lb, gn), carry=carry)
    dq, df, di, dg, dlb, dgn = outs[:6]
    return (dq, df, di, dg, dlb, dgn.reshape(HEADS // NH, 8, HD)[:, 0, :]), outs[6:]


def _lower_bounds(l0, l1):
    m = jnp.maximum(l0, l1)
    e0 = jnp.exp(l0 - m)
    e1 = jnp.exp(l1 - m)
    tot = e0 + e1
    p0 = e0 / tot
    p1 = e1 / tot
    return jnp.clip(p0 - p0, 0.0, 1.0), jnp.clip((p0 + p1) - p0, 0.0, 1.0)


def _lb_fwd(logits):
    def body(l_ref, o_ref):
        lb0, lb1 = _lower_bounds(l_ref[0:1, :], l_ref[1:2, :])
        o_ref[0:1, :] = lb0
        o_ref[1:2, :] = lb1

    return pl.pallas_call(body, name="lb_fwd", out_shape=jax.ShapeDtypeStruct((2, D), F32))(logits)


def _lb_bwd(logits, dlb):
    def body(l_ref, d_ref, o_ref):
        _, vjp = jax.vjp(_lower_bounds, l_ref[0:1, :], l_ref[1:2, :])
        g0, g1 = vjp((d_ref[0:1, :], d_ref[1:2, :]))
        o_ref[0:1, :] = g0
        o_ref[1:2, :] = g1

    return pl.pallas_call(body, name="lb_bwd", out_shape=jax.ShapeDtypeStruct((2, D), F32))(logits, dlb)


ADA_PAD = 128


def _ada_fwd(c_pad, w_ada, b_sh):
    ns = w_ada.shape[2]

    def body(c_ref, w_ref, b_ref, o_ref):
        cv = c_ref[...]
        ca = (cv * _sig(cv)).astype(BF16)
        for l in range(2):
            res = jnp.dot(ca, w_ref[l].astype(BF16), preferred_element_type=F32)
            o_ref[:, l * ns:(l + 1) * ns] = res[0:NDEV, :] + b_ref[l:l + 1, :]

    return pl.pallas_call(body, name="ada_fwd", out_shape=jax.ShapeDtypeStruct((NDEV, 2 * ns), F32),
                          compiler_params=_cp())(c_pad, w_ada, b_sh)


def _ada_wgrad(c_pad_t, d_ada_sh):
    ns = d_ada_sh.shape[2]

    def body(c_ref, d_ref, o_ref):
        cv = c_ref[...]
        ca = (cv * _sig(cv)).astype(BF16)
        for l in range(2):
            o_ref[l] = jnp.dot(ca, d_ref[l].astype(BF16), preferred_element_type=F32)

    return pl.pallas_call(body, name="ada_wgrad", out_shape=jax.ShapeDtypeStruct((2, D, ns), F32),
                          compiler_params=_cp())(c_pad_t, d_ada_sh)


def _sum_devices(g):
    _, R, C = g.shape

    def body(g_ref, o_ref):
        acc = g_ref[0]
        for d in range(1, NDEV):
            acc = acc + g_ref[d]
        o_ref[...] = acc

    return pl.pallas_call(body, name="sum_devices", out_shape=jax.ShapeDtypeStruct((R, C), F32),
                          compiler_params=_cp())(g)


def _adamw(w, g, m, v, name):
    R, C = w.shape
    tr = _row_tile(R, max(8, (1 << 19) // C))

    def body(w_ref, g_ref, m_ref, v_ref, d_ref, nm_ref, nv_ref):
        gv = g_ref[...]
        nm = B1 * m_ref[...] + (1.0 - B1) * gv
        nv = B2 * v_ref[...] + (1.0 - B2) * (gv * gv)
        m_hat = nm / (1.0 - B1 ** STEP)
        v_hat = nv / (1.0 - B2 ** STEP)
        d_ref[...] = -LR * (m_hat / (jnp.sqrt(v_hat) + AEPS) + WD * w_ref[...])
        nm_ref[...] = nm
        nv_ref[...] = nv

    tile = pl.BlockSpec((tr, C), lambda i: (i, 0))
    return pl.pallas_call(
        body, name=name, grid=(R // tr,), in_specs=[tile] * 4, out_specs=[tile] * 3,
        out_shape=[jax.ShapeDtypeStruct((R, C), F32)] * 3, compiler_params=_cp(("parallel",)),
    )(w, g, m, v)


def _cast_to_slot(place, w, l, name):
    _, R, C = w.shape
    tr = _row_tile(R, max(8, (1 << 19) // C))

    def body(p_ref, w_ref, o_ref):
        o_ref[...] = w_ref[...].astype(BF16)

    return pl.pallas_call(
        body, name=name, out_shape=jax.ShapeDtypeStruct((NCHIP, R, C), BF16),
        grid_spec=pltpu.PrefetchScalarGridSpec(
            num_scalar_prefetch=1, grid=(R // tr,),
            in_specs=[pl.BlockSpec((None, tr, C), lambda i, p_ref: (l, i, 0))],
            out_specs=pl.BlockSpec((None, tr, C), lambda i, p_ref: (p_ref[0], i, 0))),
        compiler_params=_cp(("parallel",)),
    )(place, w)


def _pair_add(core, g, got, name):
    _, R, C = g.shape
    r2 = R // 2
    tr = _row_tile(r2, max(8, (1 << 19) // C))
    nt = r2 // tr

    def body(c_ref, a_ref, b_ref, o_ref):
        o_ref[...] = (a_ref[...].astype(F32) + b_ref[...].astype(F32)).astype(o_ref.dtype)

    return pl.pallas_call(
        body, name=name, out_shape=jax.ShapeDtypeStruct((NCHIP, r2, C), BF16),
        grid_spec=pltpu.PrefetchScalarGridSpec(
            num_scalar_prefetch=1, grid=(NCHIP, nt),
            in_specs=[pl.BlockSpec((None, tr, C), lambda j, i, c_ref: (j, c_ref[0] * nt + i, 0)),
                      pl.BlockSpec((None, tr, C), lambda j, i, c_ref: (j, i, 0))],
            out_specs=pl.BlockSpec((None, tr, C), lambda j, i, c_ref: (j, i, 0))),
        compiler_params=_cp(("parallel", "parallel")),
    )(core, g, got)


def _chip_sum(place, part, recv, name):
    _, r2, C = part.shape
    tr = _row_tile(r2, max(8, (1 << 18) // C))
    nt = r2 // tr

    def body(p_ref, own_ref, r_ref, o_ref):
        me = p_ref[0]
        own = own_ref[...].astype(F32)
        acc = None
        for j in range(NCHIP):
            slot = jnp.minimum(jnp.where(j > me, j - 1, j), NCHIP - 2)
            term = jnp.where(me == j, own, r_ref[slot].astype(F32))
            acc = term if acc is None else acc + term
        o_ref[...] = acc

    return pl.pallas_call(
        body, name=name, out_shape=jax.ShapeDtypeStruct((2 * r2, C), F32),
        grid_spec=pltpu.PrefetchScalarGridSpec(
            num_scalar_prefetch=1, grid=(nt,),
            in_specs=[pl.BlockSpec((None, tr, C), lambda i, p_ref: (p_ref[0], i, 0)),
                      pl.BlockSpec((NCHIP - 1, tr, C), lambda i, p_ref: (0, i, 0))],
            out_specs=pl.BlockSpec((tr, C), lambda i, p_ref: (p_ref[1] * nt + i, 0))),
        compiler_params=_cp(("parallel",)),
    )(place, part, recv)


def _place():
    x, y, c = lax.axis_index("x"), lax.axis_index("y"), lax.axis_index("c")
    chips = [(1 - x, y), (x, 1 - y), (1 - x, 1 - y)]
    return x, y, c, chips


def _gather_small(blk, name):
    m_per, n = blk.shape

    def body(x_ref, out_ref, send_sems, recv_sems, local_sem):
        x, y, c, chips = _place()
        me, sibling = (x, y, c), (x, y, 1 - c)

        def rows(px, py, pc):
            return out_ref.at[pl.ds((4 * px + 2 * py + pc) * m_per, m_per), :]

        def copy(k, block, to, src=None):
            return pltpu.make_async_remote_copy(
                src_ref=rows(*block) if src is None else src, dst_ref=rows(*block),
                send_sem=send_sems.at[k], recv_sem=recv_sems.at[k], device_id=to, device_id_type=MESH)

        mine = pltpu.make_async_copy(x_ref, rows(*me), local_sem)
        mine.start()
        first = [copy(0, me, sibling, src=x_ref)]
        first += [copy(1 + j, me, (*chip, c), src=x_ref) for j, chip in enumerate(chips)]
        for cp in first:
            cp.start()
        passed = [copy(4 + j, (*chip, c), sibling) for j, chip in enumerate(chips)]
        for j, chip in enumerate(chips):
            copy(1 + j, (*chip, c), me).wait_recv()
            passed[j].start()
        copy(0, sibling, me).wait_recv()
        for j, chip in enumerate(chips):
            copy(4 + j, (*chip, 1 - c), me).wait_recv()
        for cp in first + passed:
            cp.wait_send()
        mine.wait()

    return pl.pallas_call(
        body, name=name, out_shape=jax.ShapeDtypeStruct((NDEV * m_per, n), blk.dtype),
        in_specs=[pl.BlockSpec(memory_space=pltpu.VMEM)], out_specs=pl.BlockSpec(memory_space=pltpu.VMEM),
        scratch_shapes=[pltpu.SemaphoreType.DMA((7,)), pltpu.SemaphoreType.DMA((7,)), pltpu.SemaphoreType.DMA],
        compiler_params=_cp(),
    )(blk)


def _gather_carry(shards):
    n = len(shards)

    def over_ici(outs, send_sems, recv_sems, a, j, chip_xy, slot):
        x, y, c, _ = _place()
        r2 = outs[a].shape[1] // 2
        blk = outs[a].at[slot, pl.ds(c * r2, r2), :]
        return pltpu.make_async_remote_copy(
            src_ref=blk, dst_ref=blk, send_sem=send_sems.at[6 * a + j], recv_sem=recv_sems.at[6 * a + j],
            device_id=(*chip_xy, c), device_id_type=MESH)

    def over_d2d(outs, send_sems, recv_sems, a, j, slot, half):
        x, y, c, _ = _place()
        r2 = outs[a].shape[1] // 2
        blk = outs[a].at[slot, pl.ds(half * r2, r2), :]
        return pltpu.make_async_remote_copy(
            src_ref=blk, dst_ref=blk, send_sem=send_sems.at[6 * a + 3 + j], recv_sem=recv_sems.at[6 * a + 3 + j],
            device_id=(x, y, 1 - c), device_id_type=MESH)

    def start(ins, outs, send_sems, recv_sems):
        x, y, c, chips = _place()
        for a in range(n):
            for j, chip_xy in enumerate(chips):
                over_ici(outs, send_sems, recv_sems, a, j, chip_xy, 2 * x + y).start()

    def finish(ins, outs, send_sems, recv_sems):
        x, y, c, chips = _place()
        for a in range(n):
            for j, (cx, cy) in enumerate(chips):
                over_ici(outs, send_sems, recv_sems, a, j, (cx, cy), 2 * cx + cy).wait_recv()
                over_d2d(outs, send_sems, recv_sems, a, j, 2 * cx + cy, c).start()
        for a in range(n):
            for j, (cx, cy) in enumerate(chips):
                over_d2d(outs, send_sems, recv_sems, a, j, 2 * cx + cy, 1 - c).wait_recv()
        for a in range(n):
            for j, (cx, cy) in enumerate(chips):
                over_ici(outs, send_sems, recv_sems, a, j, (cx, cy), 2 * x + y).wait_send()
                over_d2d(outs, send_sems, recv_sems, a, j, 2 * cx + cy, c).wait_send()

    return _Carry(shards, [jax.ShapeDtypeStruct(s.shape, s.dtype) for s in shards],
                  {a: a for a in range(n)}, 6 * n, start, finish)


def _rs_pair(grads, name):
    n = len(grads)

    def body(*refs):
        ins, gots = refs[:n], refs[n:2 * n]
        send_sems, recv_sems = refs[2 * n:]
        x, y, c, _ = _place()
        cps = []
        for a in range(n):
            r2 = ins[a].shape[1] // 2
            cp = pltpu.make_async_remote_copy(
                src_ref=ins[a].at[:, pl.ds((1 - c) * r2, r2), :], dst_ref=gots[a],
                send_sem=send_sems.at[a], recv_sem=recv_sems.at[a],
                device_id=(x, y, 1 - c), device_id_type=MESH)
            cp.start()
            cps.append(cp)
        for cp in cps:
            cp.wait()

    half = [jax.ShapeDtypeStruct((NCHIP, g.shape[1] // 2, g.shape[2]), g.dtype) for g in grads]
    return pl.pallas_call(
        body, name=name, out_shape=half, in_specs=[ANY] * n, out_specs=[ANY] * n,
        scratch_shapes=[pltpu.SemaphoreType.DMA((n,)), pltpu.SemaphoreType.DMA((n,))],
        compiler_params=_cp(),
    )(*grads)


def _chips_carry(parts):
    n = len(parts)

    def send(ins, outs, send_sems, recv_sems, a, j, chip_xy):
        x, y, c, _ = _place()
        me, them = 2 * x + y, 2 * chip_xy[0] + chip_xy[1]
        return pltpu.make_async_remote_copy(
            src_ref=ins[a].at[them], dst_ref=outs[a].at[me - (me > them).astype(jnp.int32)],
            send_sem=send_sems.at[3 * a + j], recv_sem=recv_sems.at[3 * a + j],
            device_id=(*chip_xy, c), device_id_type=MESH)

    def start(ins, outs, send_sems, recv_sems):
        _, _, _, chips = _place()
        for a in range(n):
            for j, chip_xy in enumerate(chips):
                send(ins, outs, send_sems, recv_sems, a, j, chip_xy).start()

    def finish(ins, outs, send_sems, recv_sems):
        x, y, c, chips = _place()
        me = 2 * x + y
        for a in range(n):
            for j, (cx, cy) in enumerate(chips):
                them = 2 * cx + cy
                blk = outs[a].at[them - (them > me).astype(jnp.int32)]
                pltpu.make_async_remote_copy(
                    src_ref=blk, dst_ref=blk, send_sem=send_sems.at[3 * a + j], recv_sem=recv_sems.at[3 * a + j],
                    device_id=(cx, cy, c), device_id_type=MESH).wait_recv()
        for a in range(n):
            for j, chip_xy in enumerate(chips):
                send(ins, outs, send_sems, recv_sems, a, j, chip_xy).wait_send()

    return _Carry(parts, [jax.ShapeDtypeStruct((NCHIP - 1,) + p.shape[1:], p.dtype) for p in parts], {},
                  3 * n, start, finish)


def _rs_swap(fulls):
    n = len(fulls)

    def body(*refs):
        outs = refs[n:2 * n]
        send_sems, recv_sems = refs[2 * n:]
        x, y, c, _ = _place()
        cps = []
        for a in range(n):
            r2 = outs[a].shape[0] // 2
            mine = outs[a].at[pl.ds(c * r2, r2), :]
            cp = pltpu.make_async_remote_copy(
                src_ref=mine, dst_ref=mine, send_sem=send_sems.at[a], recv_sem=recv_sems.at[a],
                device_id=(x, y, 1 - c), device_id_type=MESH)
            cp.start()
            cps.append(cp)
        for a in range(n):
            r2 = outs[a].shape[0] // 2
            blk = outs[a].at[pl.ds((1 - c) * r2, r2), :]
            pltpu.make_async_remote_copy(
                src_ref=blk, dst_ref=blk, send_sem=send_sems.at[a], recv_sem=recv_sems.at[a],
                device_id=(x, y, 1 - c), device_id_type=MESH).wait_recv()
        for cp in cps:
            cp.wait_send()

    return pl.pallas_call(
        body, name="rs_swap", out_shape=[jax.ShapeDtypeStruct(f.shape, f.dtype) for f in fulls],
        in_specs=[ANY] * n, out_specs=[ANY] * n, input_output_aliases={a: a for a in range(n)},
        scratch_shapes=[pltpu.SemaphoreType.DMA((n,)), pltpu.SemaphoreType.DMA((n,))],
        compiler_params=_cp(),
    )(*fulls)


def _mm_ride(a, b, carry, **kw):
    if carry is None:
        return _mm(a, b, **kw), []
    return _mm(a, b, carry=carry, **kw)


def _layer_fwd(l, x, ada, w, small, ride):
    shift, scale, gate = ada[:, 0:D], ada[:, D:2 * D], ada[:, 2 * D:3 * D]
    h, h_t = _prenorm_fwd(x, small["g_pre"][l], scale, shift, f"prenorm_fwd{l}")
    proj, landed = _mm_ride(h, w["w_in"][l], ride["proj"][0], name=f"proj{l}", b_mode="nn_sh")
    ride["proj"][1](landed)
    a_in, a_in_t = _pool_fwd(proj, small["pool_w"][l], small["pool_scale"][l], f"pool_fwd{l}")
    (b_in, b_in_t, o_raw, states), landed = _hgrn_fwd(proj, small["lb"][l], small["hgrn_norm_g"][l],
                                                     f"hgrn_fwd{l}", carry=ride["hgrn"][0])
    ride["hgrn"][1](landed)
    br_a = _mm(a_in, w["w_pool_o"][l], name=f"branch_a{l}", b_mode="nn_sh")
    br_b = _mm(b_in, w["w_hgrn_o"][l].reshape(D, D), name=f"branch_b{l}")
    merged, merged_t = _merge_fwd(proj, br_a, br_b, f"merge_fwd{l}")
    y = _mm(merged, w["w_out"][l].reshape(D, D), name=f"out_proj{l}")
    x_new = _postnorm_fwd(x, y, gate, small["g_post"][l], f"postnorm_fwd{l}")
    saved = dict(x=x, h_t=h_t, proj=proj, a_in_t=a_in_t, b_in_t=b_in_t, o_raw=o_raw, states=states,
                 br_a=br_a, br_b=br_b, merged_t=merged_t, y=y, scale=scale, gate=gate)
    return x_new, saved


def _layer_bwd(l, dxn, sv, w, small, ride):
    dy, dgate, dg_post = _postnorm_bwd(dxn, sv["y"], sv["gate"], small["g_post"][l], f"postnorm_bwd{l}")
    w_out = w["w_out"][l].reshape(D, D)
    dmerged = _mm(dy, w_out, name=f"d_merged{l}", b_mode="nt")
    gw_out = _mm(sv["merged_t"], dy, name=f"gw_out{l}", out_dtype=BF16)
    dbr_a, dbr_b, dmg = _merge_bwd(dmerged, sv["proj"], sv["br_a"], sv["br_b"], f"merge_bwd{l}")
    da_in = _mm(dbr_a, w["w_pool_o"][l], name=f"d_a_in{l}", b_mode="nt_shk")
    gw_pool_o = _mm(sv["a_in_t"], dbr_a, name=f"gw_pool_o{l}", out_shards=NCHIP, out_dtype=BF16)
    db_in = _mm(dbr_b, w["w_hgrn_o"][l].reshape(D, D), name=f"d_b_in{l}", b_mode="nt")
    gw_hgrn_o = _mm(sv["b_in_t"], dbr_b, name=f"gw_hgrn_o{l}", out_dtype=BF16)
    big = dict(w_pool_o=gw_pool_o, w_hgrn_o=gw_hgrn_o.reshape(NCHIP, D // NCHIP, D),
               w_out=gw_out.reshape(NCHIP, D // NCHIP, D))
    carry, landed = ride["hgrn"](big)
    (dhq, dhf, dhi, dhg, dlb, dgn), outs = _hgrn_bwd(db_in, sv["proj"], sv["o_raw"], sv["states"],
                                                     small["lb"][l], small["hgrn_norm_g"][l], f"hgrn_bwd{l}",
                                                     carry=carry)
    landed(outs)
    dpv, dpg, dpw, dpsc = _pool_bwd(da_in, sv["proj"], small["pool_w"][l], small["pool_scale"][l],
                                    f"pool_bwd{l}")
    dproj = jnp.concatenate([dpv, dpg, dhq, dhf, dhi, dhg, dmg], axis=1)
    big["w_in"] = _mm(sv["h_t"], dproj, name=f"gw_in{l}", out_shards=NCHIP, out_dtype=BF16)
    carry, landed = ride["d_h"](big)
    dh, outs = _mm_ride(dproj, w["w_in"][l], carry, name=f"d_h{l}", b_mode="nt_shk")
    landed(outs)
    dx, dshift, dscale, dg_pre = _prenorm_bwd(dh, dxn, sv["x"], small["g_pre"][l], sv["scale"],
                                              f"prenorm_bwd{l}")
    little = dict(d_ada=jnp.concatenate([dshift, dscale, dgate], axis=1), g_pre=dg_pre, g_post=dg_post,
                  pool_w=dpw, pool_scale=dpsc, lb=dlb, hgrn_norm_g=jnp.sum(dgn, axis=0, keepdims=True))
    return dx, big, little


SMALL_ROWS = 176


def _rows8(t):
    t = t.reshape(-1, D)
    return jnp.pad(t, ((0, -t.shape[0] % 8), (0, 0)))


def _pack_small_weights(b_ada, g_pre, g_post, lb_logits, pool_w, pool_scale, hgrn_norm_g):
    gn = jnp.pad(hgrn_norm_g.reshape(1, 2 * HD), ((0, 0), (0, D - 2 * HD)))
    return jnp.concatenate([_rows8(b_ada), _rows8(g_pre), _rows8(g_post), _rows8(lb_logits), _rows8(pool_w),
                            _rows8(pool_scale), _rows8(gn)], axis=0)


def _pack_small(parts):
    both = lambda key: jnp.stack([parts[l][key] for l in range(2)])
    return _pack_small_weights(both("d_ada"), both("g_pre"), both("g_post"), both("lb"), both("pool_w"),
                               both("pool_scale"), both("hgrn_norm_g"))


def _unpack_small(p):
    return (p[0:6].reshape(2, 3 * D), p[8:10], p[16:18], p[24:26], p[32:160].reshape(2, GROUPS, 128, 128),
            p[160:161].reshape(2, POOL_W), p[168:169, 0:2 * HD].reshape(2, HD))


def kernel(x, c, w_ada, b_ada, g_pre, g_post, w_in, pool_w, pool_scale, lb_logits, hgrn_norm_g, w_pool_o, w_hgrn_o, w_out, loss_target, m_w_ada, m_b_ada, m_g_pre, m_g_post, m_w_in, m_pool_w, m_pool_scale, m_lb_logits, m_hgrn_norm_g, m_w_pool_o, m_w_hgrn_o, m_w_out, v_w_ada, v_b_ada, v_g_pre, v_g_post, v_w_in, v_pool_w, v_pool_scale, v_lb_logits, v_hgrn_norm_g, v_w_pool_o, v_w_hgrn_o, v_w_out):
    ax, ay, ac = lax.axis_index("x"), lax.axis_index("y"), lax.axis_index("c")
    chip = 2 * ax + ay
    dev = 2 * chip + ac
    xe, te = x[0], loss_target[0]
    ada_s = w_ada.shape[2]

    big_names = ("w_in", "w_pool_o", "w_hgrn_o", "w_out")
    big_w = (w_in, w_pool_o, w_hgrn_o, w_out)
    core = jnp.stack([ac]).astype(jnp.int32)
    place = jnp.stack([chip, ac]).astype(jnp.int32)
    slots = {(k, l): _cast_to_slot(place, t, l, f"cast_{k}{l}") for l in range(2) for k, t in zip(big_names, big_w)}
    w = {k: [None, None] for k in big_names}
    (w["w_in"][0],) = _run_carry(_gather_carry([slots["w_in", 0]]), "gather_w_in0")
    later = [(k, l) for l in range(2) for k in big_names[1:]]

    def landed_later(outs):
        for (k, l), o in zip(later, outs):
            w[k][l] = o

    def landed_w_in1(outs):
        (w["w_in"][1],) = outs

    ride_fwd0 = dict(proj=(_gather_carry([slots[t] for t in later]), landed_later),
                     hgrn=(_gather_carry([slots["w_in", 1]]), landed_w_in1))
    no_carry = (None, lambda outs: None)

    c_all = _gather_small(jnp.broadcast_to(c, (8, D)), "gather_c").reshape(NDEV, 8, D)[:, 0, :]
    c_pad = jnp.pad(c_all, ((0, ADA_PAD - NDEV), (0, 0)))
    b_sh = lax.dynamic_slice(b_ada, (0, chip * ada_s), (2, ada_s))
    ada_cols = _gather_small(_ada_fwd(c_pad, w_ada, b_sh), "gather_ada")
    ada_cols = ada_cols.reshape(NCHIP, 2, NDEV, 2, ada_s)[:, 0]
    ada_all = jnp.transpose(ada_cols, (2, 1, 0, 3)).reshape(2, NDEV, 3 * D)
    ada_me = lax.dynamic_slice(ada_all, (0, dev, 0), (2, 1, 3 * D))

    lbs = _lb_fwd(lb_logits)
    small = dict(g_pre=g_pre[:, None, :], g_post=g_post[:, None, :], pool_w=pool_w,
                 pool_scale=pool_scale[:, None, :], lb=lbs[:, None, :], hgrn_norm_g=hgrn_norm_g[:, None, :])

    x1, sv0 = _layer_fwd(0, xe, ada_me[0], w, small, ride_fwd0)
    x2, sv1 = _layer_fwd(1, x1, ada_me[1], w, small, dict(proj=no_carry, hgrn=no_carry))
    dx2, loss_blk = _loss_head(x2, te, "loss_head")

    parts, recv = {}, {}

    def pair_sums(keys, grads, tag):
        got = _rs_pair(grads, f"rs_pair_{tag}")
        for kl, g, o in zip(keys, grads, got):
            parts[kl] = _pair_add(core, g, o, f"rs_add_{kl[0]}{kl[1]}")

    def exchange(keys):
        def landed(outs):
            recv.update(zip(keys, outs))
        return _chips_carry([parts[kl] for kl in keys]), landed

    no_ride = lambda big: no_carry
    dx1, big1, little1 = _layer_bwd(1, dx2, sv1, w, small, dict(hgrn=no_ride, d_h=no_ride))
    keys1 = [(k, 1) for k in big_names]
    pair_sums(keys1, [big1[k] for k in big_names], "l1")

    def ride_hgrn0(big):
        early = [(k, 0) for k in big_names[1:]]
        pair_sums(early, [big[k] for k in big_names[1:]], "l0_early")
        return exchange(keys1 + early)

    def ride_d_h0(big):
        pair_sums([("w_in", 0)], [big["w_in"]], "l0_w_in")
        return exchange([("w_in", 0)])

    dx0, big0, little0 = _layer_bwd(0, dx1, sv0, w, small, dict(hgrn=ride_hgrn0, d_h=ride_d_h0))
    loss = lax.psum(loss_blk[0, 0], ("x", "y", "c"))
    order = [(k, l) for l in range(2) for k in big_names]
    red = _rs_swap([_chip_sum(place, parts[kl], recv[kl], f"rs_sum_{kl[0]}{kl[1]}") for kl in order])
    g_big = {k: jnp.stack([red[i], red[4 + i]]) for i, k in enumerate(big_names)}

    packed = _gather_small(_pack_small([little0, little1]), "gather_small")
    packed = packed.reshape(NDEV, SMALL_ROWS, D)
    g_small = _sum_devices(packed)
    g_b_ada, g_g_pre, g_g_post, g_lb, g_pool_w, g_pool_scale, g_norm_g = _unpack_small(g_small)
    g_lb_logits = _lb_bwd(lb_logits, g_lb)
    d_ada_all = packed[:, 0:6, :].reshape(NDEV, 2, 3 * D)
    d_ada_sh = lax.dynamic_slice(jnp.transpose(d_ada_all, (1, 0, 2)), (0, 0, chip * ada_s), (2, NDEV, ada_s))
    d_ada_sh = jnp.pad(d_ada_sh, ((0, 0), (0, ADA_PAD - NDEV), (0, 0)))
    g_w_ada = _ada_wgrad(c_pad.T, d_ada_sh)

    def upd(wt, g, m, v, name):
        shp = wt.shape
        two = lambda t: t.reshape(-1, shp[-1])
        d, nm, nv = _adamw(two(wt), two(g), two(m), two(v), name)
        return d.reshape(shp), nm.reshape(shp), nv.reshape(shp)

    u_w_ada = upd(w_ada, g_w_ada, m_w_ada, v_w_ada, "adamw_w_ada")
    u_w_in = upd(w_in, g_big["w_in"], m_w_in, v_w_in, "adamw_w_in")
    u_w_pool_o = upd(w_pool_o, g_big["w_pool_o"], m_w_pool_o, v_w_pool_o, "adamw_w_pool_o")
    u_w_hgrn_o = upd(w_hgrn_o, g_big["w_hgrn_o"], m_w_hgrn_o, v_w_hgrn_o, "adamw_w_hgrn_o")
    u_w_out = upd(w_out, g_big["w_out"], m_w_out, v_w_out, "adamw_w_out")
    g_small_fixed = _pack_small_weights(g_b_ada, g_g_pre, g_g_post, g_lb_logits, g_pool_w, g_pool_scale,
                                        g_norm_g)
    sw = _pack_small_weights(b_ada, g_pre, g_post, lb_logits, pool_w, pool_scale, hgrn_norm_g)
    sm = _pack_small_weights(m_b_ada, m_g_pre, m_g_post, m_lb_logits, m_pool_w, m_pool_scale, m_hgrn_norm_g)
    sv = _pack_small_weights(v_b_ada, v_g_pre, v_g_post, v_lb_logits, v_pool_w, v_pool_scale, v_hgrn_norm_g)
    u_small = [_unpack_small(t) for t in _adamw(sw, g_small_fixed, sm, sv, "adamw_small")]

    grads_out = (g_w_ada, g_b_ada, g_g_pre, g_g_post, g_big["w_in"], g_pool_w, g_pool_scale, g_lb_logits,
                 g_norm_g, g_big["w_pool_o"], g_big["w_hgrn_o"], g_big["w_out"])

    def ordered(k):
        s = u_small[k]
        return (u_w_ada[k], s[0], s[1], s[2], u_w_in[k], s[4], s[5], s[3], s[6], u_w_pool_o[k], u_w_hgrn_o[k],
                u_w_out[k])

    return (loss, dx0[None], *grads_out, *ordered(0), *ordered(1), *ordered(2))
```

```python
import functools

import jax
import jax.numpy as jnp
from jax import lax
from jax.experimental import pallas as pl
from jax.experimental.pallas import tpu as pltpu

F32 = jnp.float32
BF16 = jnp.bfloat16
MESH = pl.DeviceIdType.MESH

D = 1024
HEADS = 8
HD = 128
GROUPS = 4
POOL_W = 512
WINDOWS = (2, 4, 8, 16)
CH = 64
SB = 16
NH = 2
IN_W = 7168
NCHIP = 4
NDEV = 8
EPS = 1e-6
PV0, PG0, HQ0, HF0, HI0, HG0 = 0, 4, 8, 16, 24, 32
MGP_BLK, MGH_BLK = 5, 6

LR, B1, B2, AEPS, WD, STEP = 0.001, 0.9, 0.999, 1e-08, 0.01, 10
VMEM_LIMIT = 56 * 1024 * 1024


def _cp(sem=None, **kw):
    if sem is not None:
        kw["dimension_semantics"] = sem
    return pltpu.CompilerParams(vmem_limit_bytes=VMEM_LIMIT, **kw)


def _sig(z):
    return 1.0 / (1.0 + jnp.exp(-z))


def _dsilu(z, s):
    return s * (1.0 + z * (1.0 - s))


def _row_tile(rows, cap):
    if rows <= cap:
        return rows
    t = 1 << (cap.bit_length() - 1)
    while rows % t:
        t //= 2
    return t


ANY = pl.BlockSpec(memory_space=pl.ANY)


class _Carry:
    def __init__(self, ins, outs, aliases, n_sem, start, finish):
        self.ins, self.outs, self.aliases, self.n_sem = list(ins), list(outs), dict(aliases), n_sem
        self.start, self.finish = start, finish


def _call(body, *, name, grid, in_specs, out_specs, out_shape, args, scratch_shapes=(), sem=None, carry=None):
    in_specs, out_specs, out_shape = list(in_specs), list(out_specs), list(out_shape)
    scratch_shapes = list(scratch_shapes)
    if carry is None:
        outs = pl.pallas_call(body, name=name, grid=grid, in_specs=in_specs, out_specs=out_specs,
                              out_shape=out_shape, scratch_shapes=scratch_shapes,
                              compiler_params=_cp(sem))(*args)
        return list(outs)
    n_in, n_out, n_scr = len(in_specs), len(out_specs), len(scratch_shapes)
    c_in, c_out = len(carry.ins), len(carry.outs)

    def wrapped(*refs):
        k_in, rest = refs[:n_in], refs[n_in:]
        ci, rest = rest[:c_in], rest[c_in:]
        k_out, rest = rest[:n_out], rest[n_out:]
        co, rest = rest[:c_out], rest[c_out:]
        k_scr, (ssem, rsem) = rest[:n_scr], rest[n_scr:]
        pids = [pl.program_id(d) for d in range(len(grid))]
        first = functools.reduce(jnp.logical_and, [p == 0 for p in pids])
        last = functools.reduce(jnp.logical_and, [p == g - 1 for p, g in zip(pids, grid)])

        @pl.when(first)
        def _():
            carry.start(ci, co, ssem, rsem)

        body(*k_in, *k_out, *k_scr)

        @pl.when(last)
        def _():
            carry.finish(ci, co, ssem, rsem)

    outs = pl.pallas_call(
        wrapped, name=name, grid=grid, in_specs=in_specs + [ANY] * c_in, out_specs=out_specs + [ANY] * c_out,
        out_shape=out_shape + carry.outs,
        input_output_aliases={n_in + i: n_out + o for i, o in carry.aliases.items()},
        scratch_shapes=scratch_shapes + [pltpu.SemaphoreType.DMA((carry.n_sem,))] * 2,
        compiler_params=_cp(("arbitrary",) * len(grid)),
    )(*args, *carry.ins)
    return list(outs)


def _run_carry(carry, name):
    c_in, c_out = len(carry.ins), len(carry.outs)

    def body(*refs):
        ci, co, (ssem, rsem) = refs[:c_in], refs[c_in:c_in + c_out], refs[c_in + c_out:]
        carry.start(ci, co, ssem, rsem)
        carry.finish(ci, co, ssem, rsem)

    outs = pl.pallas_call(
        body, name=name, in_specs=[ANY] * c_in, out_specs=[ANY] * c_out, out_shape=carry.outs,
        input_output_aliases=carry.aliases,
        scratch_shapes=[pltpu.SemaphoreType.DMA((carry.n_sem,))] * 2, compiler_params=_cp(),
    )(*carry.ins)
    return list(outs)


def _mm(a, b, *, name, b_mode="nn", out_shards=0, tm=1024, tn=256, tk=None, out_dtype=F32, carry=None):
    M, K = a.shape
    if b_mode == "nn":
        N = b.shape[1]
    elif b_mode == "nt":
        N = b.shape[0]
    elif b_mode == "nn_sh":
        N = b.shape[0] * b.shape[2]
    else:
        N = b.shape[1]
    tm = _row_tile(M, tm)
    if b_mode == "nn_sh":
        tn = _row_tile(b.shape[2], tn)
    elif out_shards:
        tn = _row_tile(N // out_shards, tn)
    else:
        tn = _row_tile(N, tn)
    if tk is None:
        tk = K if b_mode != "nt_shk" else b.shape[2]
    if b_mode == "nt_shk":
        tk = _row_tile(b.shape[2], tk)
    nm, nn, nk = M // tm, N // tn, K // tk

    a_spec = pl.BlockSpec((tm, tk), lambda m, n, k: (m, k))
    if b_mode == "nn":
        b_spec = pl.BlockSpec((tk, tn), lambda m, n, k: (k, n))
    elif b_mode == "nt":
        b_spec = pl.BlockSpec((tn, tk), lambda m, n, k: (n, k))
    elif b_mode == "nn_sh":
        nps = b.shape[2] // tn
        b_spec = pl.BlockSpec((None, tk, tn), lambda m, n, k: (n // nps, k, n % nps))
    else:
        kps = b.shape[2] // tk
        b_spec = pl.BlockSpec((None, tn, tk), lambda m, n, k: (k // kps, n, k % kps))
    if out_shards:
        ops = (N // out_shards) // tn
        o_spec = pl.BlockSpec((None, tm, tn), lambda m, n, k: (n // ops, m, n % ops))
        o_shape = jax.ShapeDtypeStruct((out_shards, M, N // out_shards), out_dtype)
    else:
        o_spec = pl.BlockSpec((tm, tn), lambda m, n, k: (m, n))
        o_shape = jax.ShapeDtypeStruct((M, N), out_dtype)
    trans_b = b_mode in ("nt", "nt_shk")
    dn = (((1,), (1,)), ((), ())) if trans_b else (((1,), (0,)), ((), ()))

    def body(a_ref, b_ref, o_ref, acc_ref):
        k = pl.program_id(2)

        @pl.when(k == 0)
        def _():
            acc_ref[...] = jnp.zeros(acc_ref.shape, F32)

        acc_ref[...] += lax.dot_general(a_ref[...].astype(BF16), b_ref[...].astype(BF16), dn,
                                        preferred_element_type=F32)

        @pl.when(k == nk - 1)
        def _():
            o_ref[...] = acc_ref[...].astype(o_ref.dtype)

    outs = _call(body, name=name, grid=(nm, nn, nk), in_specs=[a_spec, b_spec], out_specs=[o_spec],
                 out_shape=[o_shape], scratch_shapes=[pltpu.VMEM((tm, tn), F32)],
                 sem=("parallel", "parallel", "arbitrary"), args=(a, b), carry=carry)
    return outs[0] if carry is None else (outs[0], outs[1:])


def _rowvec(n=D):
    return pl.BlockSpec((1, n), lambda i: (0, 0))


def _prenorm_fwd(x, g, scale, shift, name):
    S = x.shape[0]
    tr = _row_tile(S, 256)

    def body(x_ref, g_ref, sc_ref, sh_ref, h_ref, ht_ref):
        xv = x_ref[...]
        r = lax.rsqrt(jnp.mean(xv * xv, axis=-1, keepdims=True) + EPS)
        hv = (xv * r) * g_ref[...] * (1.0 + sc_ref[...]) + sh_ref[...]
        h_ref[...] = hv.astype(BF16)
        ht_ref[...] = hv.T.astype(BF16)

    return pl.pallas_call(
        body, name=name, grid=(S // tr,),
        in_specs=[pl.BlockSpec((tr, D), lambda i: (i, 0)), _rowvec(), _rowvec(), _rowvec()],
        out_specs=[pl.BlockSpec((tr, D), lambda i: (i, 0)), pl.BlockSpec((D, tr), lambda i: (0, i))],
        out_shape=[jax.ShapeDtypeStruct((S, D), BF16), jax.ShapeDtypeStruct((D, S), BF16)],
        compiler_params=_cp(("parallel",)),
    )(x, g, scale, shift)


def _prenorm_bwd(dh, dxn, x, g, scale, name):
    S = x.shape[0]
    tr = _row_tile(S, 256)

    def body(dh_ref, dxn_ref, x_ref, g_ref, sc_ref, dx_ref, dsh_ref, dsc_ref, dg_ref):
        i = pl.program_id(0)

        @pl.when(i == 0)
        def _():
            dsh_ref[...] = jnp.zeros((1, D), F32)
            dsc_ref[...] = jnp.zeros((1, D), F32)
            dg_ref[...] = jnp.zeros((1, D), F32)

        xv = x_ref[...]
        dhv = dh_ref[...]
        gv = g_ref[...]
        mod = 1.0 + sc_ref[...]
        r = lax.rsqrt(jnp.mean(xv * xv, axis=-1, keepdims=True) + EPS)
        xh = xv * r
        dsh_ref[...] += jnp.sum(dhv, axis=0, keepdims=True)
        dsc_ref[...] += jnp.sum(dhv * (xh * gv), axis=0, keepdims=True)
        dg_ref[...] += jnp.sum(dhv * mod * xh, axis=0, keepdims=True)
        u = dhv * mod * gv
        dx_ref[...] = dxn_ref[...] + r * u - xv * (r * r * r) * jnp.mean(u * xv, axis=-1, keepdims=True)

    tile = pl.BlockSpec((tr, D), lambda i: (i, 0))
    return pl.pallas_call(
        body, name=name, grid=(S // tr,),
        in_specs=[tile, tile, tile, _rowvec(), _rowvec()],
        out_specs=[tile, _rowvec(), _rowvec(), _rowvec()],
        out_shape=[jax.ShapeDtypeStruct((S, D), F32)] + [jax.ShapeDtypeStruct((1, D), F32)] * 3,
        compiler_params=_cp(("arbitrary",)),
    )(dh, dxn, x, g, scale)


def _postnorm_fwd(x, y, gate, g, name):
    S = x.shape[0]
    tr = _row_tile(S, 256)

    def body(x_ref, y_ref, gate_ref, g_ref, o_ref):
        yv = y_ref[...]
        r = lax.rsqrt(jnp.mean(yv * yv, axis=-1, keepdims=True) + EPS)
        o_ref[...] = x_ref[...] + gate_ref[...] * ((yv * r) * g_ref[...])

    tile = pl.BlockSpec((tr, D), lambda i: (i, 0))
    return pl.pallas_call(
        body, name=name, grid=(S // tr,), in_specs=[tile, tile, _rowvec(), _rowvec()],
        out_specs=tile, out_shape=jax.ShapeDtypeStruct((S, D), F32), compiler_params=_cp(("parallel",)),
    )(x, y, gate, g)


def _postnorm_bwd(dxn, y, gate, g, name):
    S = y.shape[0]
    tr = _row_tile(S, 256)

    def body(dxn_ref, y_ref, gate_ref, g_ref, dy_ref, dgate_ref, dg_ref):
        i = pl.program_id(0)

        @pl.when(i == 0)
        def _():
            dgate_ref[...] = jnp.zeros((1, D), F32)
            dg_ref[...] = jnp.zeros((1, D), F32)

        yv = y_ref[...]
        dv = dxn_ref[...]
        gv = g_ref[...]
        gt = gate_ref[...]
        r = lax.rsqrt(jnp.mean(yv * yv, axis=-1, keepdims=True) + EPS)
        yh = yv * r
        dgate_ref[...] += jnp.sum(dv * (yh * gv), axis=0, keepdims=True)
        dg_ref[...] += jnp.sum(dv * gt * yh, axis=0, keepdims=True)
        u = dv * gt * gv
        dy_ref[...] = (r * u - yv * (r * r * r) * jnp.mean(u * yv, axis=-1, keepdims=True)).astype(BF16)

    tile = pl.BlockSpec((tr, D), lambda i: (i, 0))
    return pl.pallas_call(
        body, name=name, grid=(S // tr,), in_specs=[tile, tile, _rowvec(), _rowvec()],
        out_specs=[tile, _rowvec(), _rowvec()],
        out_shape=[jax.ShapeDtypeStruct((S, D), BF16), jax.ShapeDtypeStruct((1, D), F32),
                   jax.ShapeDtypeStruct((1, D), F32)],
        compiler_params=_cp(("arbitrary",)),
    )(dxn, y, gate, g)


def _loss_head(xo, target, name):
    S = xo.shape[0]
    tr = _row_tile(S, 256)

    def body(x_ref, t_ref, dx_ref, l_ref):
        i = pl.program_id(0)

        @pl.when(i == 0)
        def _():
            l_ref[...] = jnp.zeros((8, 128), F32)

        err = x_ref[...] - t_ref[...]
        dx_ref[...] = err * (1.0 / D)
        l_ref[...] += 0.5 * jnp.sum(jnp.mean(err * err, axis=-1, keepdims=True))

    tile = pl.BlockSpec((tr, D), lambda i: (i, 0))
    return pl.pallas_call(
        body, name=name, grid=(S // tr,), in_specs=[tile, tile],
        out_specs=[tile, pl.BlockSpec((8, 128), lambda i: (0, 0))],
        out_shape=[jax.ShapeDtypeStruct((S, D), F32), jax.ShapeDtypeStruct((8, 128), F32)],
        compiler_params=_cp(("arbitrary",)),
    )(xo, target)


def _merge_fwd(proj, br_a, br_b, name):
    S = proj.shape[0]
    tr = _row_tile(S, 256)

    def body(mgp_ref, mgh_ref, a_ref, b_ref, o_ref, ot_ref):
        mv = _sig(mgp_ref[...]) * a_ref[...] + _sig(mgh_ref[...]) * b_ref[...]
        o_ref[...] = mv.astype(BF16)
        ot_ref[...] = mv.T.astype(BF16)

    tile = pl.BlockSpec((tr, D), lambda i: (i, 0))
    return pl.pallas_call(
        body, name=name, grid=(S // tr,),
        in_specs=[pl.BlockSpec((tr, D), lambda i: (i, MGP_BLK)), pl.BlockSpec((tr, D), lambda i: (i, MGH_BLK)),
                  tile, tile],
        out_specs=[tile, pl.BlockSpec((D, tr), lambda i: (0, i))],
        out_shape=[jax.ShapeDtypeStruct((S, D), BF16), jax.ShapeDtypeStruct((D, S), BF16)],
        compiler_params=_cp(("parallel",)),
    )(proj, proj, br_a, br_b)


def _merge_bwd(dm, proj, br_a, br_b, name):
    S = proj.shape[0]
    tr = _row_tile(S, 256)

    def body(dm_ref, mgp_ref, mgh_ref, a_ref, b_ref, da_ref, db_ref, dmg_ref):
        dmv = dm_ref[...]
        sp = _sig(mgp_ref[...])
        sh = _sig(mgh_ref[...])
        da_ref[...] = (dmv * sp).astype(BF16)
        db_ref[...] = (dmv * sh).astype(BF16)
        dmg_ref[:, 0:D] = (dmv * a_ref[...] * sp * (1.0 - sp)).astype(BF16)
        dmg_ref[:, D:2 * D] = (dmv * b_ref[...] * sh * (1.0 - sh)).astype(BF16)

    tile = pl.BlockSpec((tr, D), lambda i: (i, 0))
    return pl.pallas_call(
        body, name=name, grid=(S // tr,),
        in_specs=[tile, pl.BlockSpec((tr, D), lambda i: (i, MGP_BLK)),
                  pl.BlockSpec((tr, D), lambda i: (i, MGH_BLK)), tile, tile],
        out_specs=[tile, tile, pl.BlockSpec((tr, 2 * D), lambda i: (i, 0))],
        out_shape=[jax.ShapeDtypeStruct((S, D), BF16), jax.ShapeDtypeStruct((S, D), BF16),
                   jax.ShapeDtypeStruct((S, 2 * D), BF16)],
        compiler_params=_cp(("parallel",)),
    )(dm, proj, proj, br_a, br_b)


def _pool_pieces(u, g, S):
    rowi = lax.broadcasted_iota(jnp.int32, (S, 1), 0)

    def down(z, k):
        return jnp.where(rowi >= k, pltpu.roll(z, k, axis=0), 0.0)

    s2 = u + down(u, 1)
    s4 = s2 + down(s2, 2)
    s8 = s4 + down(s4, 4)
    s16 = s8 + down(s8, 8)
    win = jnp.where(g == 0, s2, jnp.where(g == 1, s4, jnp.where(g == 2, s8, s16)))
    w = jnp.where(g == 0, 2, jnp.where(g == 1, 4, jnp.where(g == 2, 8, 16)))
    count = jnp.minimum(rowi + 1, w).astype(F32)
    return win / count - u, count, rowi


def _pool_fwd(proj, pw, pscale, name):
    S = proj.shape[0]

    def body(pv_ref, pg_ref, pw_ref, sc_ref, a_ref, at_ref):
        g = pl.program_id(0)
        pooled, _, _ = _pool_pieces(pv_ref[...], g, S)
        pm = jnp.dot(pooled.astype(BF16), pw_ref[...].astype(BF16), preferred_element_type=F32)
        pgv = pg_ref[...]
        av = pm * sc_ref[...] * (pgv * _sig(pgv))
        a_ref[...] = av.astype(BF16)
        at_ref[...] = av.T.astype(BF16)

    return pl.pallas_call(
        body, name=name, grid=(GROUPS,),
        in_specs=[pl.BlockSpec((S, 128), lambda g: (0, PV0 + g)), pl.BlockSpec((S, 128), lambda g: (0, PG0 + g)),
                  pl.BlockSpec((None, 128, 128), lambda g: (g, 0, 0)), pl.BlockSpec((1, 128), lambda g: (0, g))],
        out_specs=[pl.BlockSpec((S, 128), lambda g: (0, g)), pl.BlockSpec((128, S), lambda g: (g, 0))],
        out_shape=[jax.ShapeDtypeStruct((S, POOL_W), BF16), jax.ShapeDtypeStruct((POOL_W, S), BF16)],
        compiler_params=_cp(("parallel",)),
    )(proj, proj, pw, pscale)


def _pool_bwd(da, proj, pw, pscale, name):
    S = proj.shape[0]

    def body(da_ref, pv_ref, pg_ref, pw_ref, sc_ref, dpv_ref, dpg_ref, dpw_ref, dsc_ref):
        g = pl.program_id(0)
        pooled, count, rowi = _pool_pieces(pv_ref[...], g, S)
        pwb = pw_ref[...].astype(BF16)
        pm = jnp.dot(pooled.astype(BF16), pwb, preferred_element_type=F32)
        scv = sc_ref[...]
        pgv = pg_ref[...]
        sg = _sig(pgv)
        dav = da_ref[...]
        d_ps = dav * (pgv * sg)
        dpg_ref[...] = (dav * (pm * scv) * _dsilu(pgv, sg)).astype(BF16)
        dsc_ref[...] = jnp.sum(d_ps * pm, axis=0, keepdims=True)
        d_pm = (d_ps * scv).astype(BF16)
        dpw_ref[...] = lax.dot_general(pooled.astype(BF16), d_pm, (((0,), (0,)), ((), ())),
                                       preferred_element_type=F32)
        d_pooled = lax.dot_general(d_pm, pwb, (((1,), (1,)), ((), ())), preferred_element_type=F32)
        z = d_pooled / count

        def up(v, k):
            return jnp.where(rowi < S - k, pltpu.roll(v, S - k, axis=0), 0.0)

        t2 = z + up(z, 1)
        t4 = t2 + up(t2, 2)
        t8 = t4 + up(t4, 4)
        t16 = t8 + up(t8, 8)
        adj = jnp.where(g == 0, t2, jnp.where(g == 1, t4, jnp.where(g == 2, t8, t16)))
        dpv_ref[...] = (adj - d_pooled).astype(BF16)

    col = lambda g: (0, g)
    return pl.pallas_call(
        body, name=name, grid=(GROUPS,),
        in_specs=[pl.BlockSpec((S, 128), col), pl.BlockSpec((S, 128), lambda g: (0, PV0 + g)),
                  pl.BlockSpec((S, 128), lambda g: (0, PG0 + g)),
                  pl.BlockSpec((None, 128, 128), lambda g: (g, 0, 0)), pl.BlockSpec((1, 128), col)],
        out_specs=[pl.BlockSpec((S, 128), col), pl.BlockSpec((S, 128), col),
                   pl.BlockSpec((None, 128, 128), lambda g: (g, 0, 0)), pl.BlockSpec((1, 128), col)],
        out_shape=[jax.ShapeDtypeStruct((S, POOL_W), BF16), jax.ShapeDtypeStruct((S, POOL_W), BF16),
                   jax.ShapeDtypeStruct((GROUPS, 128, 128), F32), jax.ShapeDtypeStruct((1, POOL_W), F32)],
        compiler_params=_cp(("parallel",)),
    )(da, proj, proj, pw, pscale)


def _chunk_cumsum(z, rowi):
    for sh in (1, 2, 4, 8, 16, 32):
        z = z + jnp.where(rowi >= sh, pltpu.roll(z, sh, axis=0), 0.0)
    return z


def _chunk_rev_cumsum(z, rowi):
    for sh in (1, 2, 4, 8, 16, 32):
        z = z + jnp.where(rowi < CH - sh, pltpu.roll(z, CH - sh, axis=0), 0.0)
    return z


def _dot_nn(a, b):
    return jnp.dot(a.astype(BF16), b.astype(BF16), preferred_element_type=F32)


def _dot_nt(a, b):
    return lax.dot_general(a.astype(BF16), b.astype(BF16), (((1,), (1,)), ((), ())), preferred_element_type=F32)


def _dot_tn(a, b):
    return lax.dot_general(a.astype(BF16), b.astype(BF16), (((0,), (0,)), ((), ())), preferred_element_type=F32)


def _gates(hq, hf, lbv):
    sq = _sig(hq)
    sf = _sig(hf)
    f = lbv + (1.0 - lbv) * sf
    fc = jnp.maximum(f, 1e-30)
    return hq * sq, sq, sf, f, fc, jnp.log(fc)


DECAY_CAP = 60.0


def _block_ref(c_ref, i):
    if i == 0:
        return jnp.zeros((1, HD), F32)
    return c_ref[SB * i - 1:SB * i, :]


def _block_decay(c_ref):
    spans = [_block_ref(c_ref, i) - c_ref[SB * (i + 1) - 1:SB * (i + 1), :] for i in range(CH // SB)]
    return functools.reduce(jnp.maximum, spans)


def _hgrn_fwd(proj, lb, gn, name, carry=None):
    S = proj.shape[0]
    nch = S // CH
    W = NH * HD

    def body(hq_ref, hf_ref, hi_ref, hg_ref, lb_ref, gn_ref, bin_ref, bint_ref, oraw_ref, st_ref,
             q_s, k_s, c_s, v_s, o_s, state_s):
        state_s[...] = jnp.zeros((NH, HD, HD), F32)
        rowi = lax.broadcasted_iota(jnp.int32, (CH, 1), 0)
        coli = lax.broadcasted_iota(jnp.int32, (1, CH), 1)
        sbi = lax.broadcasted_iota(jnp.int32, (SB, 1), 0)
        gnv = gn_ref[...]

        def between_chunks(hh, n, rows):
            lanes = slice(hh * HD, (hh + 1) * HD)
            q, _, _, f, _, logf = _gates(hq_ref[rows, lanes], hf_ref[rows, lanes], lb_ref[:, lanes])
            k = 1.0 - f
            v = hi_ref[rows, lanes]
            c = _chunk_cumsum(logf, rowi)
            q_s[hh] = q
            k_s[hh] = k
            c_s[hh] = c
            v_s[hh] = v
            st = state_s[hh]
            st_ref[hh, n] = st.astype(BF16)
            o_s[hh] = _dot_nt(q * jnp.exp(c), st)
            last = c_s[hh, CH - 1:CH, :]
            state_s[hh] = st * jnp.exp(last) + _dot_tn(v, k * jnp.exp(last - c))
            return _block_decay(c_s.at[hh])

        def within_chunk_matmul(hh):
            q, k, c, v = q_s[hh], k_s[hh], c_s[hh], v_s[hh]
            a = jnp.zeros((CH, CH), F32)
            for i in range(CH // SB):
                r_i = _block_ref(c_s.at[hh], i)
                qi = q * jnp.exp(jnp.minimum(c - r_i, 0.0))
                kei = k * jnp.exp(jnp.minimum(r_i - c, DECAY_CAP))
                m_i = (rowi >= SB * i) & (rowi < SB * (i + 1)) & (coli <= rowi)
                a = a + jnp.where(m_i, _dot_nt(qi, kei), 0.0)
            o_s[hh] += _dot_nn(a, v)

        def within_chunk_exact(hh):
            q, k, c, v = q_s[hh], k_s[hh], c_s[hh], v_s[hh]
            a_off = jnp.zeros((CH, CH), F32)
            for i in range(1, CH // SB):
                r_i = _block_ref(c_s.at[hh], i)
                qi = q * jnp.exp(jnp.minimum(c - r_i, 0.0))
                kei = k * jnp.exp(jnp.minimum(r_i - c, 0.0))
                m_i = (rowi >= SB * i) & (rowi < SB * (i + 1)) & (coli < SB * i)
                a_off = a_off + jnp.where(m_i, _dot_nt(qi, kei), 0.0)
            o_s[hh] += _dot_nn(a_off, v)
            for i in range(CH // SB):
                blk = slice(SB * i, SB * (i + 1))
                qb = q_s[hh, blk, :]
                cb = c_s[hh, blk, :]
                acc = jnp.zeros((SB, HD), F32)
                for s in range(SB):
                    row = SB * i + s
                    w = jnp.exp(jnp.minimum(cb - c_s[hh, row:row + 1, :], 0.0))
                    a_col = jnp.sum(qb * k_s[hh, row:row + 1, :] * w, axis=-1, keepdims=True)
                    acc = acc + jnp.where(sbi >= s, a_col, 0.0) * v_s[hh, row:row + 1, :]
                o_s[hh, blk, :] += acc

        def norm_and_gate(hh, rows):
            lanes = slice(hh * HD, (hh + 1) * HD)
            ov = o_s[hh]
            oraw_ref[rows, lanes] = ov
            r = lax.rsqrt(jnp.mean(ov * ov, axis=-1, keepdims=True) + EPS)
            hg = hg_ref[rows, lanes]
            bin_ref[rows, lanes] = ((ov * r) * gnv * (hg * _sig(hg))).astype(BF16)

        def chunk(n, carry):
            rows = pl.ds(pl.multiple_of(n * CH, CH), CH)
            decay = [between_chunks(hh, n, rows) for hh in range(NH)]
            mild = jnp.max(functools.reduce(jnp.maximum, decay)) <= DECAY_CAP

            @pl.when(mild)
            def _():
                for hh in range(NH):
                    within_chunk_matmul(hh)

            @pl.when(jnp.logical_not(mild))
            def _():
                for hh in range(NH):
                    within_chunk_exact(hh)

            for hh in range(NH):
                norm_and_gate(hh, rows)
            return carry

        lax.fori_loop(0, nch, chunk, 0)
        bint_ref[...] = bin_ref[...].astype(F32).T.astype(BF16)

    col = lambda off: pl.BlockSpec((S, W), lambda h: (0, off // NH + h))
    head = pl.BlockSpec((S, W), lambda h: (0, h))
    outs = _call(
        body, name=name, grid=(HEADS // NH,),
        in_specs=[col(HQ0), col(HF0), col(HI0), col(HG0), pl.BlockSpec((1, W), lambda h: (0, h)),
                  pl.BlockSpec((1, HD), lambda h: (0, 0))],
        out_specs=[head, pl.BlockSpec((W, S), lambda h: (h, 0)), head,
                   pl.BlockSpec((NH, nch, HD, HD), lambda h: (h, 0, 0, 0))],
        out_shape=[jax.ShapeDtypeStruct((S, D), BF16), jax.ShapeDtypeStruct((D, S), BF16),
                   jax.ShapeDtypeStruct((S, D), F32), jax.ShapeDtypeStruct((HEADS, nch, HD, HD), BF16)],
        scratch_shapes=[pltpu.VMEM((NH, CH, HD), F32)] * 5 + [pltpu.VMEM((NH, HD, HD), F32)],
        sem=("parallel",), args=(proj, proj, proj, proj, lb, gn), carry=carry)
    return outs[:4], outs[4:]


def _hgrn_bwd(dbin, proj, oraw, states, lb, gn, name, carry=None):
    S = proj.shape[0]
    nch = S // CH
    W = NH * HD

    def body(db_ref, hq_ref, hf_ref, hi_ref, hg_ref, or_ref, st_ref, lb_ref, gn_ref,
             dq_ref, df_ref, di_ref, dg_ref, dlb_ref, dgn_ref,
             q_s, k_s, c_s, v_s, do_s, dq_s, dk_s, dv_s, dc_s, dqd_s, dkd_s, dl_s, dst_s, dlb_s, dgn_s):
        dst_s[...] = jnp.zeros((NH, HD, HD), F32)
        dlb_s[...] = jnp.zeros((1, W), F32)
        dgn_s[...] = jnp.zeros((1, HD), F32)
        rowi = lax.broadcasted_iota(jnp.int32, (CH, 1), 0)
        rowi2 = lax.broadcasted_iota(jnp.int32, (CH, CH), 0)
        coli2 = lax.broadcasted_iota(jnp.int32, (CH, CH), 1)
        sbi = lax.broadcasted_iota(jnp.int32, (SB, 1), 0)
        gnv = gn_ref[...]
        def between_chunks(hh, n, rows):
            lanes = slice(hh * HD, (hh + 1) * HD)
            q, _, _, f, _, logf = _gates(hq_ref[rows, lanes], hf_ref[rows, lanes], lb_ref[:, lanes])
            k = 1.0 - f
            v = hi_ref[rows, lanes]
            c = _chunk_cumsum(logf, rowi)
            ov = or_ref[rows, lanes]
            hg = hg_ref[rows, lanes]
            sg = _sig(hg)
            r = lax.rsqrt(jnp.mean(ov * ov, axis=-1, keepdims=True) + EPS)
            dbv = db_ref[rows, lanes]
            d_on = dbv * (hg * sg)
            dg_ref[rows, lanes] = (dbv * ((ov * r) * gnv) * _dsilu(hg, sg)).astype(BF16)
            dgn_s[...] += jnp.sum(d_on * (ov * r), axis=0, keepdims=True)
            u = d_on * gnv
            do = r * u - ov * (r * r * r) * jnp.mean(u * ov, axis=-1, keepdims=True)
            q_s[hh] = q
            k_s[hh] = k
            c_s[hh] = c
            v_s[hh] = v
            do_s[hh] = do
            st = st_ref[hh, n].astype(F32)
            dst = dst_s[hh]
            ec = jnp.exp(c)
            last = c_s[hh, CH - 1:CH, :]
            el = jnp.exp(last - c)
            elast = jnp.exp(last)
            dq = _dot_nn(do, st) * ec
            dk = _dot_nn(v, dst) * el
            dq_s[hh] = dq
            dk_s[hh] = dk
            dv_s[hh] = _dot_nt(k * el, dst)
            dc_s[hh] = q * dq - k * dk
            dl_s[hh] = (jnp.sum(k * dk, axis=0, keepdims=True)
                        + elast * jnp.sum(st * dst, axis=0, keepdims=True))
            dst_s[hh] = dst * elast + _dot_tn(do, q * ec)
            return _block_decay(c_s.at[hh])

        def pairs_matmul(hh, first, cap, strict):
            q, k, c, v, do = q_s[hh], k_s[hh], c_s[hh], v_s[hh], do_s[hh]
            d_a = _dot_nt(do, v).astype(BF16).astype(F32)
            d_at = d_a.T
            at = jnp.zeros((CH, CH), F32)
            dq, dk, dcum = dq_s[hh], dk_s[hh], dc_s[hh]
            for i in range(first, CH // SB):
                r_i = _block_ref(c_s.at[hh], i)
                eq = jnp.exp(jnp.minimum(c - r_i, 0.0))
                ek = jnp.exp(jnp.minimum(r_i - c, cap))
                qi = (q * eq).astype(BF16).astype(F32)
                kei = (k * ek).astype(BF16).astype(F32)
                in_t = (rowi2 >= SB * i) & (rowi2 < SB * (i + 1))
                in_s = (coli2 >= SB * i) & (coli2 < SB * (i + 1))
                m_ts = in_t & ((coli2 < SB * i) if strict else (coli2 <= rowi2))
                m_st = in_s & ((rowi2 < SB * i) if strict else (rowi2 <= coli2))
                at = at + jnp.where(m_st, _dot_nt(kei, qi), 0.0)
                dq_i = _dot_nn(jnp.where(m_ts, d_a, 0.0), kei)
                dk_i = _dot_nn(jnp.where(m_st, d_at, 0.0), qi)
                dq = dq + dq_i * eq
                dk = dk + dk_i * ek
                dcum = dcum + (qi * dq_i - kei * dk_i)
            dq_s[hh] = dq
            dk_s[hh] = dk
            dc_s[hh] = dcum
            dv_s[hh] += _dot_nn(at, do)

        def pairs_exact(hh):
            dqd_s[hh] = jnp.zeros((CH, HD), F32)
            dkd_s[hh] = jnp.zeros((CH, HD), F32)
            for i in range(CH // SB):
                blk = slice(SB * i, SB * (i + 1))
                qb = q_s[hh, blk, :]
                cb = c_s[hh, blk, :]
                dob = do_s[hh, blk, :]
                dq_acc = jnp.zeros((SB, HD), F32)
                for s in range(SB):
                    row = SB * i + s
                    ks = k_s[hh, row:row + 1, :]
                    vs = v_s[hh, row:row + 1, :]
                    w = jnp.exp(jnp.minimum(cb - c_s[hh, row:row + 1, :], 0.0))
                    live = sbi >= s
                    a_col = jnp.where(live, jnp.sum(qb * ks * w, axis=-1, keepdims=True), 0.0)
                    da_col = jnp.where(live, jnp.sum(dob * vs, axis=-1, keepdims=True), 0.0)
                    dq_acc = dq_acc + da_col * ks * w
                    dkd_s[hh, row:row + 1, :] += jnp.sum(da_col * qb * w, axis=0, keepdims=True)
                    dv_s[hh, row:row + 1, :] += jnp.sum(a_col * dob, axis=0, keepdims=True)
                dqd_s[hh, blk, :] += dq_acc
            dq_d = dqd_s[hh]
            dk_d = dkd_s[hh]
            dq_s[hh] += dq_d
            dk_s[hh] += dk_d
            dc_s[hh] += q_s[hh] * dq_d - k_s[hh] * dk_d

        def gate_grads(hh, rows):
            lanes = slice(hh * HD, (hh + 1) * HD)
            lbv = lb_ref[:, lanes]
            hq = hq_ref[rows, lanes]
            _, sq, sf, f, fc, _ = _gates(hq, hf_ref[rows, lanes], lbv)
            dlogf = _chunk_rev_cumsum(dc_s[hh], rowi) + dl_s[hh]
            dfv = jnp.where(f > 1e-30, dlogf / fc, 0.0) - dk_s[hh]
            dlb_s[:, lanes] += jnp.sum(dfv * (1.0 - sf), axis=0, keepdims=True)
            df_ref[rows, lanes] = (dfv * (1.0 - lbv) * sf * (1.0 - sf)).astype(BF16)
            dq_ref[rows, lanes] = (dq_s[hh] * _dsilu(hq, sq)).astype(BF16)
            di_ref[rows, lanes] = dv_s[hh].astype(BF16)

        def chunk(j, carry):
            n = nch - 1 - j
            rows = pl.ds(pl.multiple_of(n * CH, CH), CH)
            decay = [between_chunks(hh, n, rows) for hh in range(NH)]
            mild = jnp.max(functools.reduce(jnp.maximum, decay)) <= DECAY_CAP

            @pl.when(mild)
            def _():
                for hh in range(NH):
                    pairs_matmul(hh, 0, DECAY_CAP, strict=False)

            @pl.when(jnp.logical_not(mild))
            def _():
                for hh in range(NH):
                    pairs_matmul(hh, 1, 0.0, strict=True)
                    pairs_exact(hh)

            for hh in range(NH):
                gate_grads(hh, rows)
            return carry

        lax.fori_loop(0, nch, chunk, 0)
        dlb_ref[...] = dlb_s[...]
        dgn_ref[...] = jnp.broadcast_to(dgn_s[...], (8, HD))

    col = lambda off: pl.BlockSpec((S, W), lambda h: (0, off // NH + h))
    head = pl.BlockSpec((S, W), lambda h: (0, h))
    vec = pl.BlockSpec((1, W), lambda h: (0, h))
    outs = _call(
        body, name=name, grid=(HEADS // NH,),
        in_specs=[head, col(HQ0), col(HF0), col(HI0), col(HG0), head,
                  pl.BlockSpec((NH, nch, HD, HD), lambda h: (h, 0, 0, 0)), vec,
                  pl.BlockSpec((1, HD), lambda h: (0, 0))],
        out_specs=[head, head, head, head, vec, pl.BlockSpec((8, HD), lambda h: (h, 0))],
        out_shape=[jax.ShapeDtypeStruct((S, D), BF16)] * 4
        + [jax.ShapeDtypeStruct((1, D), F32), jax.ShapeDtypeStruct((8 * HEADS // NH, HD), F32)],
        scratch_shapes=[pltpu.VMEM((NH, CH, HD), F32)] * 11
        + [pltpu.VMEM((NH, 1, HD), F32), pltpu.VMEM((NH, HD, HD), F32), pltpu.VMEM((1, W), F32),
           pltpu.VMEM((1, HD), F32)],
        sem=("parallel",), args=(dbin, proj, proj, proj, proj, oraw, states, lb, gn), carry=carry)
    dq, df, di, dg, dlb, dgn = outs[:6]
    return (dq, df, di, dg, dlb, dgn.reshape(HEADS // NH, 8, HD)[:, 0, :]), outs[6:]


def _lower_bounds(l0, l1):
    m = jnp.maximum(l0, l1)
    e0 = jnp.exp(l0 - m)
    e1 = jnp.exp(l1 - m)
    tot = e0 + e1
    p0 = e0 / tot
    p1 = e1 / tot
    return jnp.clip(p0 - p0, 0.0, 1.0), jnp.clip((p0 + p1) - p0, 0.0, 1.0)


def _lb_fwd(logits):
    def body(l_ref, o_ref):
        lb0, lb1 = _lower_bounds(l_ref[0:1, :], l_ref[1:2, :])
        o_ref[0:1, :] = lb0
        o_ref[1:2, :] = lb1

    return pl.pallas_call(body, name="lb_fwd", out_shape=jax.ShapeDtypeStruct((2, D), F32))(logits)


def _lb_bwd(logits, dlb):
    def body(l_ref, d_ref, o_ref):
        _, vjp = jax.vjp(_lower_bounds, l_ref[0:1, :], l_ref[1:2, :])
        g0, g1 = vjp((d_ref[0:1, :], d_ref[1:2, :]))
        o_ref[0:1, :] = g0
        o_ref[1:2, :] = g1

    return pl.pallas_call(body, name="lb_bwd", out_shape=jax.ShapeDtypeStruct((2, D), F32))(logits, dlb)


ADA_PAD = 128


def _ada_fwd(c_pad, w_ada, b_sh):
    ns = w_ada.shape[2]

    def body(c_ref, w_ref, b_ref, o_ref):
        cv = c_ref[...]
        ca = (cv * _sig(cv)).astype(BF16)
        for l in range(2):
            res = jnp.dot(ca, w_ref[l].astype(BF16), preferred_element_type=F32)
            o_ref[:, l * ns:(l + 1) * ns] = res[0:NDEV, :] + b_ref[l:l + 1, :]

    return pl.pallas_call(body, name="ada_fwd", out_shape=jax.ShapeDtypeStruct((NDEV, 2 * ns), F32),
                          compiler_params=_cp())(c_pad, w_ada, b_sh)


def _ada_wgrad(c_pad_t, d_ada_sh):
    ns = d_ada_sh.shape[2]

    def body(c_ref, d_ref, o_ref):
        cv = c_ref[...]
        ca = (cv * _sig(cv)).astype(BF16)
        for l in range(2):
            o_ref[l] = jnp.dot(ca, d_ref[l].astype(BF16), preferred_element_type=F32)

    return pl.pallas_call(body, name="ada_wgrad", out_shape=jax.ShapeDtypeStruct((2, D, ns), F32),
                          compiler_params=_cp())(c_pad_t, d_ada_sh)


def _sum_devices(g):
    _, R, C = g.shape

    def body(g_ref, o_ref):
        acc = g_ref[0]
        for d in range(1, NDEV):
            acc = acc + g_ref[d]
        o_ref[...] = acc

    return pl.pallas_call(body, name="sum_devices", out_shape=jax.ShapeDtypeStruct((R, C), F32),
                          compiler_params=_cp())(g)


def _adamw(w, g, m, v, name):
    R, C = w.shape
    tr = _row_tile(R, max(8, (1 << 19) // C))

    def body(w_ref, g_ref, m_ref, v_ref, d_ref, nm_ref, nv_ref):
        gv = g_ref[...]
        nm = B1 * m_ref[...] + (1.0 - B1) * gv
        nv = B2 * v_ref[...] + (1.0 - B2) * (gv * gv)
        m_hat = nm / (1.0 - B1 ** STEP)
        v_hat = nv / (1.0 - B2 ** STEP)
        d_ref[...] = -LR * (m_hat / (jnp.sqrt(v_hat) + AEPS) + WD * w_ref[...])
        nm_ref[...] = nm
        nv_ref[...] = nv

    tile = pl.BlockSpec((tr, C), lambda i: (i, 0))
    return pl.pallas_call(
        body, name=name, grid=(R // tr,), in_specs=[tile] * 4, out_specs=[tile] * 3,
        out_shape=[jax.ShapeDtypeStruct((R, C), F32)] * 3, compiler_params=_cp(("parallel",)),
    )(w, g, m, v)


def _cast_to_slot(place, w, l, name):
    _, R, C = w.shape
    tr = _row_tile(R, max(8, (1 << 19) // C))

    def body(p_ref, w_ref, o_ref):
        o_ref[...] = w_ref[...].astype(BF16)

    return pl.pallas_call(
        body, name=name, out_shape=jax.ShapeDtypeStruct((NCHIP, R, C), BF16),
        grid_spec=pltpu.PrefetchScalarGridSpec(
            num_scalar_prefetch=1, grid=(R // tr,),
            in_specs=[pl.BlockSpec((None, tr, C), lambda i, p_ref: (l, i, 0))],
            out_specs=pl.BlockSpec((None, tr, C), lambda i, p_ref: (p_ref[0], i, 0))),
        compiler_params=_cp(("parallel",)),
    )(place, w)


def _pair_add(core, g, got, name):
    _, R, C = g.shape
    r2 = R // 2
    tr = _row_tile(r2, max(8, (1 << 19) // C))
    nt = r2 // tr

    def body(c_ref, a_ref, b_ref, o_ref):
        o_ref[...] = (a_ref[...].astype(F32) + b_ref[...].astype(F32)).astype(o_ref.dtype)

    return pl.pallas_call(
        body, name=name, out_shape=jax.ShapeDtypeStruct((NCHIP, r2, C), BF16),
        grid_spec=pltpu.PrefetchScalarGridSpec(
            num_scalar_prefetch=1, grid=(NCHIP, nt),
            in_specs=[pl.BlockSpec((None, tr, C), lambda j, i, c_ref: (j, c_ref[0] * nt + i, 0)),
                      pl.BlockSpec((None, tr, C), lambda j, i, c_ref: (j, i, 0))],
            out_specs=pl.BlockSpec((None, tr, C), lambda j, i, c_ref: (j, i, 0))),
        compiler_params=_cp(("parallel", "parallel")),
    )(core, g, got)


def _chip_sum(place, part, recv, name):
    _, r2, C = part.shape
    tr = _row_tile(r2, max(8, (1 << 18) // C))
    nt = r2 // tr

    def body(p_ref, own_ref, r_ref, o_ref):
        me = p_ref[0]
        own = own_ref[...].astype(F32)
        acc = None
        for j in range(NCHIP):
            slot = jnp.minimum(jnp.where(j > me, j - 1, j), NCHIP - 2)
            term = jnp.where(me == j, own, r_ref[slot].astype(F32))
            acc = term if acc is None else acc + term
        o_ref[...] = acc

    return pl.pallas_call(
        body, name=name, out_shape=jax.ShapeDtypeStruct((2 * r2, C), F32),
        grid_spec=pltpu.PrefetchScalarGridSpec(
            num_scalar_prefetch=1, grid=(nt,),
            in_specs=[pl.BlockSpec((None, tr, C), lambda i, p_ref: (p_ref[0], i, 0)),
                      pl.BlockSpec((NCHIP - 1, tr, C), lambda i, p_ref: (0, i, 0))],
            out_specs=pl.BlockSpec((tr, C), lambda i, p_ref: (p_ref[1] * nt + i, 0))),
        compiler_params=_cp(("parallel",)),
    )(place, part, recv)


def _place():
    x, y, c = lax.axis_index("x"), lax.axis_index("y"), lax.axis_index("c")
    chips = [(1 - x, y), (x, 1 - y), (1 - x, 1 - y)]
    return x, y, c, chips


def _gather_small(blk, name):
    m_per, n = blk.shape

    def body(x_ref, out_ref, send_sems, recv_sems, local_sem):
        x, y, c, chips = _place()
        me, sibling = (x, y, c), (x, y, 1 - c)

        def rows(px, py, pc):
            return out_ref.at[pl.ds((4 * px + 2 * py + pc) * m_per, m_per), :]

        def copy(k, block, to, src=None):
            return pltpu.make_async_remote_copy(
                src_ref=rows(*block) if src is None else src, dst_ref=rows(*block),
                send_sem=send_sems.at[k], recv_sem=recv_sems.at[k], device_id=to, device_id_type=MESH)

        mine = pltpu.make_async_copy(x_ref, rows(*me), local_sem)
        mine.start()
        first = [copy(0, me, sibling, src=x_ref)]
        first += [copy(1 + j, me, (*chip, c), src=x_ref) for j, chip in enumerate(chips)]
        for cp in first:
            cp.start()
        passed = [copy(4 + j, (*chip, c), sibling) for j, chip in enumerate(chips)]
        for j, chip in enumerate(chips):
            copy(1 + j, (*chip, c), me).wait_recv()
            passed[j].start()
        copy(0, sibling, me).wait_recv()
        for j, chip in enumerate(chips):
            copy(4 + j, (*chip, 1 - c), me).wait_recv()
        for cp in first + passed:
            cp.wait_send()
        mine.wait()

    return pl.pallas_call(
        body, name=name, out_shape=jax.ShapeDtypeStruct((NDEV * m_per, n), blk.dtype),
        in_specs=[pl.BlockSpec(memory_space=pltpu.VMEM)], out_specs=pl.BlockSpec(memory_space=pltpu.VMEM),
        scratch_shapes=[pltpu.SemaphoreType.DMA((7,)), pltpu.SemaphoreType.DMA((7,)), pltpu.SemaphoreType.DMA],
        compiler_params=_cp(),
    )(blk)


def _gather_carry(shards):
    n = len(shards)

    def over_ici(outs, send_sems, recv_sems, a, j, chip_xy, slot):
        x, y, c, _ = _place()
        r2 = outs[a].shape[1] // 2
        blk = outs[a].at[slot, pl.ds(c * r2, r2), :]
        return pltpu.make_async_remote_copy(
            src_ref=blk, dst_ref=blk, send_sem=send_sems.at[6 * a + j], recv_sem=recv_sems.at[6 * a + j],
            device_id=(*chip_xy, c), device_id_type=MESH)

    def over_d2d(outs, send_sems, recv_sems, a, j, slot, half):
        x, y, c, _ = _place()
        r2 = outs[a].shape[1] // 2
        blk = outs[a].at[slot, pl.ds(half * r2, r2), :]
        return pltpu.make_async_remote_copy(
            src_ref=blk, dst_ref=blk, send_sem=send_sems.at[6 * a + 3 + j], recv_sem=recv_sems.at[6 * a + 3 + j],
            device_id=(x, y, 1 - c), device_id_type=MESH)

    def start(ins, outs, send_sems, recv_sems):
        x, y, c, chips = _place()
        for a in range(n):
            for j, chip_xy in enumerate(chips):
                over_ici(outs, send_sems, recv_sems, a, j, chip_xy, 2 * x + y).start()

    def finish(ins, outs, send_sems, recv_sems):
        x, y, c, chips = _place()
        for a in range(n):
            for j, (cx, cy) in enumerate(chips):
                over_ici(outs, send_sems, recv_sems, a, j, (cx, cy), 2 * cx + cy).wait_recv()
                over_d2d(outs, send_sems, recv_sems, a, j, 2 * cx + cy, c).start()
        for a in range(n):
            for j, (cx, cy) in enumerate(chips):
                over_d2d(outs, send_sems, recv_sems, a, j, 2 * cx + cy, 1 - c).wait_recv()
        for a in range(n):
            for j, (cx, cy) in enumerate(chips):
                over_ici(outs, send_sems, recv_sems, a, j, (cx, cy), 2 * x + y).wait_send()
                over_d2d(outs, send_sems, recv_sems, a, j, 2 * cx + cy, c).wait_send()

    return _Carry(shards, [jax.ShapeDtypeStruct(s.shape, s.dtype) for s in shards],
                  {a: a for a in range(n)}, 6 * n, start, finish)


def _rs_pair(grads, name):
    n = len(grads)

    def body(*refs):
        ins, gots = refs[:n], refs[n:2 * n]
        send_sems, recv_sems = refs[2 * n:]
        x, y, c, _ = _place()
        cps = []
        for a in range(n):
            r2 = ins[a].shape[1] // 2
            cp = pltpu.make_async_remote_copy(
                src_ref=ins[a].at[:, pl.ds((1 - c) * r2, r2), :], dst_ref=gots[a],
                send_sem=send_sems.at[a], recv_sem=recv_sems.at[a],
                device_id=(x, y, 1 - c), device_id_type=MESH)
            cp.start()
            cps.append(cp)
        for cp in cps:
            cp.wait()

    half = [jax.ShapeDtypeStruct((NCHIP, g.shape[1] // 2, g.shape[2]), g.dtype) for g in grads]
    return pl.pallas_call(
        body, name=name, out_shape=half, in_specs=[ANY] * n, out_specs=[ANY] * n,
        scratch_shapes=[pltpu.SemaphoreType.DMA((n,)), pltpu.SemaphoreType.DMA((n,))],
        compiler_params=_cp(),
    )(*grads)


def _chips_carry(parts):
    n = len(parts)

    def send(ins, outs, send_sems, recv_sems, a, j, chip_xy):
        x, y, c, _ = _place()
        me, them = 2 * x + y, 2 * chip_xy[0] + chip_xy[1]
        return pltpu.make_async_remote_copy(
            src_ref=ins[a].at[them], dst_ref=outs[a].at[me - (me > them).astype(jnp.int32)],
            send_sem=send_sems.at[3 * a + j], recv_sem=recv_sems.at[3 * a + j],
            device_id=(*chip_xy, c), device_id_type=MESH)

    def start(ins, outs, send_sems, recv_sems):
        _, _, _, chips = _place()
        for a in range(n):
            for j, chip_xy in enumerate(chips):
                send(ins, outs, send_sems, recv_sems, a, j, chip_xy).start()

    def finish(ins, outs, send_sems, recv_sems):
        x, y, c, chips = _place()
        me = 2 * x + y
        for a in range(n):
            for j, (cx, cy) in enumerate(chips):
                them = 2 * cx + cy
                blk = outs[a].at[them - (them > me).astype(jnp.int32)]
                pltpu.make_async_remote_copy(
                    src_ref=blk, dst_ref=blk, send_sem=send_sems.at[3 * a + j], recv_sem=recv_sems.at[3 * a + j],
                    device_id=(cx, cy, c), device_id_type=MESH).wait_recv()
        for a in range(n):
            for j, chip_xy in enumerate(chips):
                send(ins, outs, send_sems, recv_sems, a, j, chip_xy).wait_send()

    return _Carry(parts, [jax.ShapeDtypeStruct((NCHIP - 1,) + p.shape[1:], p.dtype) for p in parts], {},
                  3 * n, start, finish)


def _rs_swap(fulls):
    n = len(fulls)

    def body(*refs):
        outs = refs[n:2 * n]
        send_sems, recv_sems = refs[2 * n:]
        x, y, c, _ = _place()
        cps = []
        for a in range(n):
            r2 = outs[a].shape[0] // 2
            mine = outs[a].at[pl.ds(c * r2, r2), :]
            cp = pltpu.make_async_remote_copy(
                src_ref=mine, dst_ref=mine, send_sem=send_sems.at[a], recv_sem=recv_sems.at[a],
                device_id=(x, y, 1 - c), device_id_type=MESH)
            cp.start()
            cps.append(cp)
        for a in range(n):
            r2 = outs[a].shape[0] // 2
            blk = outs[a].at[pl.ds((1 - c) * r2, r2), :]
            pltpu.make_async_remote_copy(
                src_ref=blk, dst_ref=blk, send_sem=send_sems.at[a], recv_sem=recv_sems.at[a],
                device_id=(x, y, 1 - c), device_id_type=MESH).wait_recv()
        for cp in cps:
            cp.wait_send()

    return pl.pallas_call(
        body, name="rs_swap", out_shape=[jax.ShapeDtypeStruct(f.shape, f.dtype) for f in fulls],
        in_specs=[ANY] * n, out_specs=[ANY] * n, input_output_aliases={a: a for a in range(n)},
        scratch_shapes=[pltpu.SemaphoreType.DMA((n,)), pltpu.SemaphoreType.DMA((n,))],
        compiler_params=_cp(),
    )(*fulls)


def _mm_ride(a, b, carry, **kw):
    if carry is None:
        return _mm(a, b, **kw), []
    return _mm(a, b, carry=carry, **kw)


def _layer_fwd(l, x, ada, w, small, ride):
    shift, scale, gate = ada[:, 0:D], ada[:, D:2 * D], ada[:, 2 * D:3 * D]
    h, h_t = _prenorm_fwd(x, small["g_pre"][l], scale, shift, f"prenorm_fwd{l}")
    proj, landed = _mm_ride(h, w["w_in"][l], ride["proj"][0], name=f"proj{l}", b_mode="nn_sh")
    ride["proj"][1](landed)
    a_in, a_in_t = _pool_fwd(proj, small["pool_w"][l], small["pool_scale"][l], f"pool_fwd{l}")
    (b_in, b_in_t, o_raw, states), landed = _hgrn_fwd(proj, small["lb"][l], small["hgrn_norm_g"][l],
                                                     f"hgrn_fwd{l}", carry=ride["hgrn"][0])
    ride["hgrn"][1](landed)
    br_a = _mm(a_in, w["w_pool_o"][l], name=f"branch_a{l}", b_mode="nn_sh")
    br_b = _mm(b_in, w["w_hgrn_o"][l].reshape(D, D), name=f"branch_b{l}")
    merged, merged_t = _merge_fwd(proj, br_a, br_b, f"merge_fwd{l}")
    y = _mm(merged, w["w_out"][l].reshape(D, D), name=f"out_proj{l}")
    x_new = _postnorm_fwd(x, y, gate, small["g_post"][l], f"postnorm_fwd{l}")
    saved = dict(x=x, h_t=h_t, proj=proj, a_in_t=a_in_t, b_in_t=b_in_t, o_raw=o_raw, states=states,
                 br_a=br_a, br_b=br_b, merged_t=merged_t, y=y, scale=scale, gate=gate)
    return x_new, saved


def _layer_bwd(l, dxn, sv, w, small, ride):
    dy, dgate, dg_post = _postnorm_bwd(dxn, sv["y"], sv["gate"], small["g_post"][l], f"postnorm_bwd{l}")
    w_out = w["w_out"][l].reshape(D, D)
    dmerged = _mm(dy, w_out, name=f"d_merged{l}", b_mode="nt")
    gw_out = _mm(sv["merged_t"], dy, name=f"gw_out{l}", out_dtype=BF16)
    dbr_a, dbr_b, dmg = _merge_bwd(dmerged, sv["proj"], sv["br_a"], sv["br_b"], f"merge_bwd{l}")
    da_in = _mm(dbr_a, w["w_pool_o"][l], name=f"d_a_in{l}", b_mode="nt_shk")
    gw_pool_o = _mm(sv["a_in_t"], dbr_a, name=f"gw_pool_o{l}", out_shards=NCHIP, out_dtype=BF16)
    db_in = _mm(dbr_b, w["w_hgrn_o"][l].reshape(D, D), name=f"d_b_in{l}", b_mode="nt")
    gw_hgrn_o = _mm(sv["b_in_t"], dbr_b, name=f"gw_hgrn_o{l}", out_dtype=BF16)
    big = dict(w_pool_o=gw_pool_o, w_hgrn_o=gw_hgrn_o.reshape(NCHIP, D // NCHIP, D),
               w_out=gw_out.reshape(NCHIP, D // NCHIP, D))
    carry, landed = ride["hgrn"](big)
    (dhq, dhf, dhi, dhg, dlb, dgn), outs = _hgrn_bwd(db_in, sv["proj"], sv["o_raw"], sv["states"],
                                                     small["lb"][l], small["hgrn_norm_g"][l], f"hgrn_bwd{l}",
                                                     carry=carry)
    landed(outs)
    dpv, dpg, dpw, dpsc = _pool_bwd(da_in, sv["proj"], small["pool_w"][l], small["pool_scale"][l],
                                    f"pool_bwd{l}")
    dproj = jnp.concatenate([dpv, dpg, dhq, dhf, dhi, dhg, dmg], axis=1)
    big["w_in"] = _mm(sv["h_t"], dproj, name=f"gw_in{l}", out_shards=NCHIP, out_dtype=BF16)
    carry, landed = ride["d_h"](big)
    dh, outs = _mm_ride(dproj, w["w_in"][l], carry, name=f"d_h{l}", b_mode="nt_shk")
    landed(outs)
    dx, dshift, dscale, dg_pre = _prenorm_bwd(dh, dxn, sv["x"], small["g_pre"][l], sv["scale"],
                                              f"prenorm_bwd{l}")
    little = dict(d_ada=jnp.concatenate([dshift, dscale, dgate], axis=1), g_pre=dg_pre, g_post=dg_post,
                  pool_w=dpw, pool_scale=dpsc, lb=dlb, hgrn_norm_g=jnp.sum(dgn, axis=0, keepdims=True))
    return dx, big, little


SMALL_ROWS = 176


def _rows8(t):
    t = t.reshape(-1, D)
    return jnp.pad(t, ((0, -t.shape[0] % 8), (0, 0)))


def _pack_small_weights(b_ada, g_pre, g_post, lb_logits, pool_w, pool_scale, hgrn_norm_g):
    gn = jnp.pad(hgrn_norm_g.reshape(1, 2 * HD), ((0, 0), (0, D - 2 * HD)))
    return jnp.concatenate([_rows8(b_ada), _rows8(g_pre), _rows8(g_post), _rows8(lb_logits), _rows8(pool_w),
                            _rows8(pool_scale), _rows8(gn)], axis=0)


def _pack_small(parts):
    both = lambda key: jnp.stack([parts[l][key] for l in range(2)])
    return _pack_small_weights(both("d_ada"), both("g_pre"), both("g_post"), both("lb"), both("pool_w"),
                               both("pool_scale"), both("hgrn_norm_g"))


def _unpack_small(p):
    return (p[0:6].reshape(2, 3 * D), p[8:10], p[16:18], p[24:26], p[32:160].reshape(2, GROUPS, 128, 128),
            p[160:161].reshape(2, POOL_W), p[168:169, 0:2 * HD].reshape(2, HD))


def kernel(x, c, w_ada, b_ada, g_pre, g_post, w_in, pool_w, pool_scale, lb_logits, hgrn_norm_g, w_pool_o, w_hgrn_o, w_out, loss_target, m_w_ada, m_b_ada, m_g_pre, m_g_post, m_w_in, m_pool_w, m_pool_scale, m_lb_logits, m_hgrn_norm_g, m_w_pool_o, m_w_hgrn_o, m_w_out, v_w_ada, v_b_ada, v_g_pre, v_g_post, v_w_in, v_pool_w, v_pool_scale, v_lb_logits, v_hgrn_norm_g, v_w_pool_o, v_w_hgrn_o, v_w_out):
    ax, ay, ac = lax.axis_index("x"), lax.axis_index("y"), lax.axis_index("c")
    chip = 2 * ax + ay
    dev = 2 * chip + ac
    xe, te = x[0], loss_target[0]
    ada_s = w_ada.shape[2]

    big_names = ("w_in", "w_pool_o", "w_hgrn_o", "w_out")
    big_w = (w_in, w_pool_o, w_hgrn_o, w_out)
    core = jnp.stack([ac]).astype(jnp.int32)
    place = jnp.stack([chip, ac]).astype(jnp.int32)
    slots = {(k, l): _cast_to_slot(place, t, l, f"cast_{k}{l}") for l in range(2) for k, t in zip(big_names, big_w)}
    w = {k: [None, None] for k in big_names}
    (w["w_in"][0],) = _run_carry(_gather_carry([slots["w_in", 0]]), "gather_w_in0")
    later = [(k, l) for l in range(2) for k in big_names[1:]]

    def landed_later(outs):
        for (k, l), o in zip(later, outs):
            w[k][l] = o

    def landed_w_in1(outs):
        (w["w_in"][1],) = outs

    ride_fwd0 = dict(proj=(_gather_carry([slots[t] for t in later]), landed_later),
                     hgrn=(_gather_carry([slots["w_in", 1]]), landed_w_in1))
    no_carry = (None, lambda outs: None)

    c_all = _gather_small(jnp.broadcast_to(c, (8, D)), "gather_c").reshape(NDEV, 8, D)[:, 0, :]
    c_pad = jnp.pad(c_all, ((0, ADA_PAD - NDEV), (0, 0)))
    b_sh = lax.dynamic_slice(b_ada, (0, chip * ada_s), (2, ada_s))
    ada_cols = _gather_small(_ada_fwd(c_pad, w_ada, b_sh), "gather_ada")
    ada_cols = ada_cols.reshape(NCHIP, 2, NDEV, 2, ada_s)[:, 0]
    ada_all = jnp.transpose(ada_cols, (2, 1, 0, 3)).reshape(2, NDEV, 3 * D)
    ada_me = lax.dynamic_slice(ada_all, (0, dev, 0), (2, 1, 3 * D))

    lbs = _lb_fwd(lb_logits)
    small = dict(g_pre=g_pre[:, None, :], g_post=g_post[:, None, :], pool_w=pool_w,
                 pool_scale=pool_scale[:, None, :], lb=lbs[:, None, :], hgrn_norm_g=hgrn_norm_g[:, None, :])

    x1, sv0 = _layer_fwd(0, xe, ada_me[0], w, small, ride_fwd0)
    x2, sv1 = _layer_fwd(1, x1, ada_me[1], w, small, dict(proj=no_carry, hgrn=no_carry))
    dx2, loss_blk = _loss_head(x2, te, "loss_head")

    parts, recv = {}, {}

    def pair_sums(keys, grads, tag):
        got = _rs_pair(grads, f"rs_pair_{tag}")
        for kl, g, o in zip(keys, grads, got):
            parts[kl] = _pair_add(core, g, o, f"rs_add_{kl[0]}{kl[1]}")

    def exchange(keys):
        def landed(outs):
            recv.update(zip(keys, outs))
        return _chips_carry([parts[kl] for kl in keys]), landed

    no_ride = lambda big: no_carry
    dx1, big1, little1 = _layer_bwd(1, dx2, sv1, w, small, dict(hgrn=no_ride, d_h=no_ride))
    keys1 = [(k, 1) for k in big_names]
    pair_sums(keys1, [big1[k] for k in big_names], "l1")

    def ride_hgrn0(big):
        early = [(k, 0) for k in big_names[1:]]
        pair_sums(early, [big[k] for k in big_names[1:]], "l0_early")
        return exchange(keys1 + early)

    def ride_d_h0(big):
        pair_sums([("w_in", 0)], [big["w_in"]], "l0_w_in")
        return exchange([("w_in", 0)])

    dx0, big0, little0 = _layer_bwd(0, dx1, sv0, w, small, dict(hgrn=ride_hgrn0, d_h=ride_d_h0))
    loss = lax.psum(loss_blk[0, 0], ("x", "y", "c"))
    order = [(k, l) for l in range(2) for k in big_names]
    red = _rs_swap([_chip_sum(place, parts[kl], recv[kl], f"rs_sum_{kl[0]}{kl[1]}") for kl in order])
    g_big = {k: jnp.stack([red[i], red[4 + i]]) for i, k in enumerate(big_names)}

    packed = _gather_small(_pack_small([little0, little1]), "gather_small")
    packed = packed.reshape(NDEV, SMALL_ROWS, D)
    g_small = _sum_devices(packed)
    g_b_ada, g_g_pre, g_g_post, g_lb, g_pool_w, g_pool_scale, g_norm_g = _unpack_small(g_small)
    g_lb_logits = _lb_bwd(lb_logits, g_lb)
    d_ada_all = packed[:, 0:6, :].reshape(NDEV, 2, 3 * D)
    d_ada_sh = lax.dynamic_slice(jnp.transpose(d_ada_all, (1, 0, 2)), (0, 0, chip * ada_s), (2, NDEV, ada_s))
    d_ada_sh = jnp.pad(d_ada_sh, ((0, 0), (0, ADA_PAD - NDEV), (0, 0)))
    g_w_ada = _ada_wgrad(c_pad.T, d_ada_sh)

    def upd(wt, g, m, v, name):
        shp = wt.shape
        two = lambda t: t.reshape(-1, shp[-1])
        d, nm, nv = _adamw(two(wt), two(g), two(m), two(v), name)
        return d.reshape(shp), nm.reshape(shp), nv.reshape(shp)

    u_w_ada = upd(w_ada, g_w_ada, m_w_ada, v_w_ada, "adamw_w_ada")
    u_w_in = upd(w_in, g_big["w_in"], m_w_in, v_w_in, "adamw_w_in")
    u_w_pool_o = upd(w_pool_o, g_big["w_pool_o"], m_w_pool_o, v_w_pool_o, "adamw_w_pool_o")
    u_w_hgrn_o = upd(w_hgrn_o, g_big["w_hgrn_o"], m_w_hgrn_o, v_w_hgrn_o, "adamw_w_hgrn_o")
    u_w_out = upd(w_out, g_big["w_out"], m_w_out, v_w_out, "adamw_w_out")
    g_small_fixed = _pack_small_weights(g_b_ada, g_g_pre, g_g_post, g_lb_logits, g_pool_w, g_pool_scale,
                                        g_norm_g)
    sw = _pack_small_weights(b_ada, g_pre, g_post, lb_logits, pool_w, pool_scale, hgrn_norm_g)
    sm = _pack_small_weights(m_b_ada, m_g_pre, m_g_post, m_lb_logits, m_pool_w, m_pool_scale, m_hgrn_norm_g)
    sv = _pack_small_weights(v_b_ada, v_g_pre, v_g_post, v_lb_logits, v_pool_w, v_pool_scale, v_hgrn_norm_g)
    u_small = [_unpack_small(t) for t in _adamw(sw, g_small_fixed, sm, sv, "adamw_small")]

    grads_out = (g_w_ada, g_b_ada, g_g_pre, g_g_post, g_big["w_in"], g_pool_w, g_pool_scale, g_lb_logits,
                 g_norm_g, g_big["w_pool_o"], g_big["w_hgrn_o"], g_big["w_out"])

    def ordered(k):
        s = u_small[k]
        return (u_w_ada[k], s[0], s[1], s[2], u_w_in[k], s[4], s[5], s[3], s[6], u_w_pool_o[k], u_w_hgrn_o[k],
                u_w_out[k])

    return (loss, dx0[None], *grads_out, *ordered(0), *ordered(1), *ordered(2))
```

```python
import functools

import jax
import jax.numpy as jnp
from jax import lax
from jax.experimental import pallas as pl
from jax.experimental.pallas import tpu as pltpu

F32 = jnp.float32
BF16 = jnp.bfloat16
MESH = pl.DeviceIdType.MESH

D = 1024
HEADS = 8
HD = 128
GROUPS = 4
POOL_W = 512
WINDOWS = (2, 4, 8, 16)
CH = 64
SB = 16
NH = 2
IN_W = 7168
NCHIP = 4
NDEV = 8
EPS = 1e-6
PV0, PG0, HQ0, HF0, HI0, HG0 = 0, 4, 8, 16, 24, 32
MGP_BLK, MGH_BLK = 5, 6

LR, B1, B2, AEPS, WD, STEP = 0.001, 0.9, 0.999, 1e-08, 0.01, 10
VMEM_LIMIT = 56 * 1024 * 1024


def _cp(sem=None, **kw):
    if sem is not None:
        kw["dimension_semantics"] = sem
    return pltpu.CompilerParams(vmem_limit_bytes=VMEM_LIMIT, **kw)


def _sig(z):
    return 1.0 / (1.0 + jnp.exp(-z))


def _dsilu(z, s):
    return s * (1.0 + z * (1.0 - s))


def _row_tile(rows, cap):
    if rows <= cap:
        return rows
    t = 1 << (cap.bit_length() - 1)
    while rows % t:
        t //= 2
    return t


ANY = pl.BlockSpec(memory_space=pl.ANY)


class _Carry:
    def __init__(self, ins, outs, aliases, n_sem, start, finish):
        self.ins, self.outs, self.aliases, self.n_sem = list(ins), list(outs), dict(aliases), n_sem
        self.start, self.finish = start, finish


def _call(body, *, name, grid, in_specs, out_specs, out_shape, args, scratch_shapes=(), sem=None, carry=None):
    in_specs, out_specs, out_shape = list(in_specs), list(out_specs), list(out_shape)
    scratch_shapes = list(scratch_shapes)
    if carry is None:
        outs = pl.pallas_call(body, name=name, grid=grid, in_specs=in_specs, out_specs=out_specs,
                              out_shape=out_shape, scratch_shapes=scratch_shapes,
                              compiler_params=_cp(sem))(*args)
        return list(outs)
    n_in, n_out, n_scr = len(in_specs), len(out_specs), len(scratch_shapes)
    c_in, c_out = len(carry.ins), len(carry.outs)

    def wrapped(*refs):
        k_in, rest = refs[:n_in], refs[n_in:]
        ci, rest = rest[:c_in], rest[c_in:]
        k_out, rest = rest[:n_out], rest[n_out:]
        co, rest = rest[:c_out], rest[c_out:]
        k_scr, (ssem, rsem) = rest[:n_scr], rest[n_scr:]
        pids = [pl.program_id(d) for d in range(len(grid))]
        first = functools.reduce(jnp.logical_and, [p == 0 for p in pids])
        last = functools.reduce(jnp.logical_and, [p == g - 1 for p, g in zip(pids, grid)])

        @pl.when(first)
        def _():
            carry.start(ci, co, ssem, rsem)

        body(*k_in, *k_out, *k_scr)

        @pl.when(last)
        def _():
            carry.finish(ci, co, ssem, rsem)

    outs = pl.pallas_call(
        wrapped, name=name, grid=grid, in_specs=in_specs + [ANY] * c_in, out_specs=out_specs + [ANY] * c_out,
        out_shape=out_shape + carry.outs,
        input_output_aliases={n_in + i: n_out + o for i, o in carry.aliases.items()},
        scratch_shapes=scratch_shapes + [pltpu.SemaphoreType.DMA((carry.n_sem,))] * 2,
        compiler_params=_cp(("arbitrary",) * len(grid)),
    )(*args, *carry.ins)
    return list(outs)


def _run_carry(carry, name):
    c_in, c_out = len(carry.ins), len(carry.outs)

    def body(*refs):
        ci, co, (ssem, rsem) = refs[:c_in], refs[c_in:c_in + c_out], refs[c_in + c_out:]
        carry.start(ci, co, ssem, rsem)
        carry.finish(ci, co, ssem, rsem)

    outs = pl.pallas_call(
        body, name=name, in_specs=[ANY] * c_in, out_specs=[ANY] * c_out, out_shape=carry.outs,
        input_output_aliases=carry.aliases,
        scratch_shapes=[pltpu.SemaphoreType.DMA((carry.n_sem,))] * 2, compiler_params=_cp(),
    )(*carry.ins)
    return list(outs)


def _mm(a, b, *, name, b_mode="nn", out_shards=0, tm=1024, tn=256, tk=None, out_dtype=F32, carry=None):
    M, K = a.shape
    if b_mode == "nn":
        N = b.shape[1]
    elif b_mode == "nt":
        N = b.shape[0]
    elif b_mode == "nn_sh":
        N = b.shape[0] * b.shape[2]
    else:
        N = b.shape[1]
    tm = _row_tile(M, tm)
    if b_mode == "nn_sh":
        tn = _row_tile(b.shape[2], tn)
    elif out_shards:
        tn = _row_tile(N // out_shards, tn)
    else:
        tn = _row_tile(N, tn)
    if tk is None:
        tk = K if b_mode != "nt_shk" else b.shape[2]
    if b_mode == "nt_shk":
        tk = _row_tile(b.shape[2], tk)
    nm, nn, nk = M // tm, N // tn, K // tk

    a_spec = pl.BlockSpec((tm, tk), lambda m, n, k: (m, k))
    if b_mode == "nn":
        b_spec = pl.BlockSpec((tk, tn), lambda m, n, k: (k, n))
    elif b_mode == "nt":
        b_spec = pl.BlockSpec((tn, tk), lambda m, n, k: (n, k))
    elif b_mode == "nn_sh":
        nps = b.shape[2] // tn
        b_spec = pl.BlockSpec((None, tk, tn), lambda m, n, k: (n // nps, k, n % nps))
    else:
        kps = b.shape[2] // tk
        b_spec = pl.BlockSpec((None, tn, tk), lambda m, n, k: (k // kps, n, k % kps))
    if out_shards:
        ops = (N // out_shards) // tn
        o_spec = pl.BlockSpec((None, tm, tn), lambda m, n, k: (n // ops, m, n % ops))
        o_shape = jax.ShapeDtypeStruct((out_shards, M, N // out_shards), out_dtype)
    else:
        o_spec = pl.BlockSpec((tm, tn), lambda m, n, k: (m, n))
        o_shape = jax.ShapeDtypeStruct((M, N), out_dtype)
    trans_b = b_mode in ("nt", "nt_shk")
    dn = (((1,), (1,)), ((), ())) if trans_b else (((1,), (0,)), ((), ()))

    def body(a_ref, b_ref, o_ref, acc_ref):
        k = pl.program_id(2)

        @pl.when(k == 0)
        def _():
            acc_ref[...] = jnp.zeros(acc_ref.shape, F32)

        acc_ref[...] += lax.dot_general(a_ref[...].astype(BF16), b_ref[...].astype(BF16), dn,
                                        preferred_element_type=F32)

        @pl.when(k == nk - 1)
        def _():
            o_ref[...] = acc_ref[...].astype(o_ref.dtype)

    outs = _call(body, name=name, grid=(nm, nn, nk), in_specs=[a_spec, b_spec], out_specs=[o_spec],
                 out_shape=[o_shape], scratch_shapes=[pltpu.VMEM((tm, tn), F32)],
                 sem=("parallel", "parallel", "arbitrary"), args=(a, b), carry=carry)
    return outs[0] if carry is None else (outs[0], outs[1:])


def _rowvec(n=D):
    return pl.BlockSpec((1, n), lambda i: (0, 0))


def _prenorm_fwd(x, g, scale, shift, name):
    S = x.shape[0]
    tr = _row_tile(S, 256)

    def body(x_ref, g_ref, sc_ref, sh_ref, h_ref, ht_ref):
        xv = x_ref[...]
        r = lax.rsqrt(jnp.mean(xv * xv, axis=-1, keepdims=True) + EPS)
        hv = (xv * r) * g_ref[...] * (1.0 + sc_ref[...]) + sh_ref[...]
        h_ref[...] = hv.astype(BF16)
        ht_ref[...] = hv.T.astype(BF16)

    return pl.pallas_call(
        body, name=name, grid=(S // tr,),
        in_specs=[pl.BlockSpec((tr, D), lambda i: (i, 0)), _rowvec(), _rowvec(), _rowvec()],
        out_specs=[pl.BlockSpec((tr, D), lambda i: (i, 0)), pl.BlockSpec((D, tr), lambda i: (0, i))],
        out_shape=[jax.ShapeDtypeStruct((S, D), BF16), jax.ShapeDtypeStruct((D, S), BF16)],
        compiler_params=_cp(("parallel",)),
    )(x, g, scale, shift)


def _prenorm_bwd(dh, dxn, x, g, scale, name):
    S = x.shape[0]
    tr = _row_tile(S, 256)

    def body(dh_ref, dxn_ref, x_ref, g_ref, sc_ref, dx_ref, dsh_ref, dsc_ref, dg_ref):
        i = pl.program_id(0)

        @pl.when(i == 0)
        def _():
            dsh_ref[...] = jnp.zeros((1, D), F32)
            dsc_ref[...] = jnp.zeros((1, D), F32)
            dg_ref[...] = jnp.zeros((1, D), F32)

        xv = x_ref[...]
        dhv = dh_ref[...]
        gv = g_ref[...]
        mod = 1.0 + sc_ref[...]
        r = lax.rsqrt(jnp.mean(xv * xv, axis=-1, keepdims=True) + EPS)
        xh = xv * r
        dsh_ref[...] += jnp.sum(dhv, axis=0, keepdims=True)
        dsc_ref[...] += jnp.sum(dhv * (xh * gv), axis=0, keepdims=True)
        dg_ref[...] += jnp.sum(dhv * mod * xh, axis=0, keepdims=True)
        u = dhv * mod * gv
        dx_ref[...] = dxn_ref[...] + r * u - xv * (r * r * r) * jnp.mean(u * xv, axis=-1, keepdims=True)

    tile = pl.BlockSpec((tr, D), lambda i: (i, 0))
    return pl.pallas_call(
        body, name=name, grid=(S // tr,),
        in_specs=[tile, tile, tile, _rowvec(), _rowvec()],
        out_specs=[tile, _rowvec(), _rowvec(), _rowvec()],
        out_shape=[jax.ShapeDtypeStruct((S, D), F32)] + [jax.ShapeDtypeStruct((1, D), F32)] * 3,
        compiler_params=_cp(("arbitrary",)),
    )(dh, dxn, x, g, scale)


def _postnorm_fwd(x, y, gate, g, name):
    S = x.shape[0]
    tr = _row_tile(S, 256)

    def body(x_ref, y_ref, gate_ref, g_ref, o_ref):
        yv = y_ref[...]
        r = lax.rsqrt(jnp.mean(yv * yv, axis=-1, keepdims=True) + EPS)
        o_ref[...] = x_ref[...] + gate_ref[...] * ((yv * r) * g_ref[...])

    tile = pl.BlockSpec((tr, D), lambda i: (i, 0))
    return pl.pallas_call(
        body, name=name, grid=(S // tr,), in_specs=[tile, tile, _rowvec(), _rowvec()],
        out_specs=tile, out_shape=jax.ShapeDtypeStruct((S, D), F32), compiler_params=_cp(("parallel",)),
    )(x, y, gate, g)


def _postnorm_bwd(dxn, y, gate, g, name):
    S = y.shape[0]
    tr = _row_tile(S, 256)

    def body(dxn_ref, y_ref, gate_ref, g_ref, dy_ref, dgate_ref, dg_ref):
        i = pl.program_id(0)

        @pl.when(i == 0)
        def _():
            dgate_ref[...] = jnp.zeros((1, D), F32)
            dg_ref[...] = jnp.zeros((1, D), F32)

        yv = y_ref[...]
        dv = dxn_ref[...]
        gv = g_ref[...]
        gt = gate_ref[...]
        r = lax.rsqrt(jnp.mean(yv * yv, axis=-1, keepdims=True) + EPS)
        yh = yv * r
        dgate_ref[...] += jnp.sum(dv * (yh * gv), axis=0, keepdims=True)
        dg_ref[...] += jnp.sum(dv * gt * yh, axis=0, keepdims=True)
        u = dv * gt * gv
        dy_ref[...] = (r * u - yv * (r * r * r) * jnp.mean(u * yv, axis=-1, keepdims=True)).astype(BF16)

    tile = pl.BlockSpec((tr, D), lambda i: (i, 0))
    return pl.pallas_call(
        body, name=name, grid=(S // tr,), in_specs=[tile, tile, _rowvec(), _rowvec()],
        out_specs=[tile, _rowvec(), _rowvec()],
        out_shape=[jax.ShapeDtypeStruct((S, D), BF16), jax.ShapeDtypeStruct((1, D), F32),
                   jax.ShapeDtypeStruct((1, D), F32)],
        compiler_params=_cp(("arbitrary",)),
    )(dxn, y, gate, g)


def _loss_head(xo, target, name):
    S = xo.shape[0]
    tr = _row_tile(S, 256)

    def body(x_ref, t_ref, dx_ref, l_ref):
        i = pl.program_id(0)

        @pl.when(i == 0)
        def _():
            l_ref[...] = jnp.zeros((8, 128), F32)

        err = x_ref[...] - t_ref[...]
        dx_ref[...] = err * (1.0 / D)
        l_ref[...] += 0.5 * jnp.sum(jnp.mean(err * err, axis=-1, keepdims=True))

    tile = pl.BlockSpec((tr, D), lambda i: (i, 0))
    return pl.pallas_call(
        body, name=name, grid=(S // tr,), in_specs=[tile, tile],
        out_specs=[tile, pl.BlockSpec((8, 128), lambda i: (0, 0))],
        out_shape=[jax.ShapeDtypeStruct((S, D), F32), jax.ShapeDtypeStruct((8, 128), F32)],
        compiler_params=_cp(("arbitrary",)),
    )(xo, target)


def _merge_fwd(proj, br_a, br_b, name):
    S = proj.shape[0]
    tr = _row_tile(S, 256)

    def body(mgp_ref, mgh_ref, a_ref, b_ref, o_ref, ot_ref):
        mv = _sig(mgp_ref[...]) * a_ref[...] + _sig(mgh_ref[...]) * b_ref[...]
        o_ref[...] = mv.astype(BF16)
        ot_ref[...] = mv.T.astype(BF16)

    tile = pl.BlockSpec((tr, D), lambda i: (i, 0))
    return pl.pallas_call(
        body, name=name, grid=(S // tr,),
        in_specs=[pl.BlockSpec((tr, D), lambda i: (i, MGP_BLK)), pl.BlockSpec((tr, D), lambda i: (i, MGH_BLK)),
                  tile, tile],
        out_specs=[tile, pl.BlockSpec((D, tr), lambda i: (0, i))],
        out_shape=[jax.ShapeDtypeStruct((S, D), BF16), jax.ShapeDtypeStruct((D, S), BF16)],
        compiler_params=_cp(("parallel",)),
    )(proj, proj, br_a, br_b)


def _merge_bwd(dm, proj, br_a, br_b, name):
    S = proj.shape[0]
    tr = _row_tile(S, 256)

    def body(dm_ref, mgp_ref, mgh_ref, a_ref, b_ref, da_ref, db_ref, dmg_ref):
        dmv = dm_ref[...]
        sp = _sig(mgp_ref[...])
        sh = _sig(mgh_ref[...])
        da_ref[...] = (dmv * sp).astype(BF16)
        db_ref[...] = (dmv * sh).astype(BF16)
        dmg_ref[:, 0:D] = (dmv * a_ref[...] * sp * (1.0 - sp)).astype(BF16)
        dmg_ref[:, D:2 * D] = (dmv * b_ref[...] * sh * (1.0 - sh)).astype(BF16)

    tile = pl.BlockSpec((tr, D), lambda i: (i, 0))
    return pl.pallas_call(
        body, name=name, grid=(S // tr,),
        in_specs=[tile, pl.BlockSpec((tr, D), lambda i: (i, MGP_BLK)),
                  pl.BlockSpec((tr, D), lambda i: (i, MGH_BLK)), tile, tile],
        out_specs=[tile, tile, pl.BlockSpec((tr, 2 * D), lambda i: (i, 0))],
        out_shape=[jax.ShapeDtypeStruct((S, D), BF16), jax.ShapeDtypeStruct((S, D), BF16),
                   jax.ShapeDtypeStruct((S, 2 * D), BF16)],
        compiler_params=_cp(("parallel",)),
    )(dm, proj, proj, br_a, br_b)


def _pool_pieces(u, g, S):
    rowi = lax.broadcasted_iota(jnp.int32, (S, 1), 0)

    def down(z, k):
        return jnp.where(rowi >= k, pltpu.roll(z, k, axis=0), 0.0)

    s2 = u + down(u, 1)
    s4 = s2 + down(s2, 2)
    s8 = s4 + down(s4, 4)
    s16 = s8 + down(s8, 8)
    win = jnp.where(g == 0, s2, jnp.where(g == 1, s4, jnp.where(g == 2, s8, s16)))
    w = jnp.where(g == 0, 2, jnp.where(g == 1, 4, jnp.where(g == 2, 8, 16)))
    count = jnp.minimum(rowi + 1, w).astype(F32)
    return win / count - u, count, rowi


def _pool_fwd(proj, pw, pscale, name):
    S = proj.shape[0]

    def body(pv_ref, pg_ref, pw_ref, sc_ref, a_ref, at_ref):
        g = pl.program_id(0)
        pooled, _, _ = _pool_pieces(pv_ref[...], g, S)
        pm = jnp.dot(pooled.astype(BF16), pw_ref[...].astype(BF16), preferred_element_type=F32)
        pgv = pg_ref[...]
        av = pm * sc_ref[...] * (pgv * _sig(pgv))
        a_ref[...] = av.astype(BF16)
        at_ref[...] = av.T.astype(BF16)

    return pl.pallas_call(
        body, name=name, grid=(GROUPS,),
        in_specs=[pl.BlockSpec((S, 128), lambda g: (0, PV0 + g)), pl.BlockSpec((S, 128), lambda g: (0, PG0 + g)),
                  pl.BlockSpec((None, 128, 128), lambda g: (g, 0, 0)), pl.BlockSpec((1, 128), lambda g: (0, g))],
        out_specs=[pl.BlockSpec((S, 128), lambda g: (0, g)), pl.BlockSpec((128, S), lambda g: (g, 0))],
        out_shape=[jax.ShapeDtypeStruct((S, POOL_W), BF16), jax.ShapeDtypeStruct((POOL_W, S), BF16)],
        compiler_params=_cp(("parallel",)),
    )(proj, proj, pw, pscale)


def _pool_bwd(da, proj, pw, pscale, name):
    S = proj.shape[0]

    def body(da_ref, pv_ref, pg_ref, pw_ref, sc_ref, dpv_ref, dpg_ref, dpw_ref, dsc_ref):
        g = pl.program_id(0)
        pooled, count, rowi = _pool_pieces(pv_ref[...], g, S)
        pwb = pw_ref[...].astype(BF16)
        pm = jnp.dot(pooled.astype(BF16), pwb, preferred_element_type=F32)
        scv = sc_ref[...]
        pgv = pg_ref[...]
        sg = _sig(pgv)
        dav = da_ref[...]
        d_ps = dav * (pgv * sg)
        dpg_ref[...] = (dav * (pm * scv) * _dsilu(pgv, sg)).astype(BF16)
        dsc_ref[...] = jnp.sum(d_ps * pm, axis=0, keepdims=True)
        d_pm = (d_ps * scv).astype(BF16)
        dpw_ref[...] = lax.dot_general(pooled.astype(BF16), d_pm, (((0,), (0,)), ((), ())),
                                       preferred_element_type=F32)
        d_pooled = lax.dot_general(d_pm, pwb, (((1,), (1,)), ((), ())), preferred_element_type=F32)
        z = d_pooled / count

        def up(v, k):
            return jnp.where(rowi < S - k, pltpu.roll(v, S - k, axis=0), 0.0)

        t2 = z + up(z, 1)
        t4 = t2 + up(t2, 2)
        t8 = t4 + up(t4, 4)
        t16 = t8 + up(t8, 8)
        adj = jnp.where(g == 0, t2, jnp.where(g == 1, t4, jnp.where(g == 2, t8, t16)))
        dpv_ref[...] = (adj - d_pooled).astype(BF16)

    col = lambda g: (0, g)
    return pl.pallas_call(
        body, name=name, grid=(GROUPS,),
        in_specs=[pl.BlockSpec((S, 128), col), pl.BlockSpec((S, 128), lambda g: (0, PV0 + g)),
                  pl.BlockSpec((S, 128), lambda g: (0, PG0 + g)),
                  pl.BlockSpec((None, 128, 128), lambda g: (g, 0, 0)), pl.BlockSpec((1, 128), col)],
        out_specs=[pl.BlockSpec((S, 128), col), pl.BlockSpec((S, 128), col),
                   pl.BlockSpec((None, 128, 128), lambda g: (g, 0, 0)), pl.BlockSpec((1, 128), col)],
        out_shape=[jax.ShapeDtypeStruct((S, POOL_W), BF16), jax.ShapeDtypeStruct((S, POOL_W), BF16),
                   jax.ShapeDtypeStruct((GROUPS, 128, 128), F32), jax.ShapeDtypeStruct((1, POOL_W), F32)],
        compiler_params=_cp(("parallel",)),
    )(da, proj, proj, pw, pscale)


def _chunk_cumsum(z, rowi):
    for sh in (1, 2, 4, 8, 16, 32):
        z = z + jnp.where(rowi >= sh, pltpu.roll(z, sh, axis=0), 0.0)
    return z


def _chunk_rev_cumsum(z, rowi):
    for sh in (1, 2, 4, 8, 16, 32):
        z = z + jnp.where(rowi < CH - sh, pltpu.roll(z, CH - sh, axis=0), 0.0)
    return z


def _dot_nn(a, b):
    return jnp.dot(a.astype(BF16), b.astype(BF16), preferred_element_type=F32)


def _dot_nt(a, b):
    return lax.dot_general(a.astype(BF16), b.astype(BF16), (((1,), (1,)), ((), ())), preferred_element_type=F32)


def _dot_tn(a, b):
    return lax.dot_general(a.astype(BF16), b.astype(BF16), (((0,), (0,)), ((), ())), preferred_element_type=F32)


def _gates(hq, hf, lbv):
    sq = _sig(hq)
    sf = _sig(hf)
    f = lbv + (1.0 - lbv) * sf
    fc = jnp.maximum(f, 1e-30)
    return hq * sq, sq, sf, f, fc, jnp.log(fc)


DECAY_CAP = 60.0


def _block_ref(c_ref, i):
    if i == 0:
        return jnp.zeros((1, HD), F32)
    return c_ref[SB * i - 1:SB * i, :]


def _block_decay(c_ref):
    spans = [_block_ref(c_ref, i) - c_ref[SB * (i + 1) - 1:SB * (i + 1), :] for i in range(CH // SB)]
    return functools.reduce(jnp.maximum, spans)


def _hgrn_fwd(proj, lb, gn, name, carry=None):
    S = proj.shape[0]
    nch = S // CH
    W = NH * HD

    def body(hq_ref, hf_ref, hi_ref, hg_ref, lb_ref, gn_ref, bin_ref, bint_ref, oraw_ref, st_ref, mild_ref,
             q_s, k_s, c_s, v_s, o_s, state_s, qf_s, kf_s, cf_s):
        state_s[...] = jnp.zeros((NH, HD, HD), F32)
        rowi = lax.broadcasted_iota(jnp.int32, (CH, 1), 0)
        coli = lax.broadcasted_iota(jnp.int32, (1, CH), 1)
        sbi = lax.broadcasted_iota(jnp.int32, (SB, 1), 0)
        gnv = gn_ref[...]

        def gates_pass(n, worst):
            rows = pl.ds(pl.multiple_of(n * CH, CH), CH)
            for hh in range(NH):
                lanes = slice(hh * HD, (hh + 1) * HD)
                q, _, _, f, _, logf = _gates(hq_ref[rows, lanes], hf_ref[rows, lanes], lb_ref[:, lanes])
                c = _chunk_cumsum(logf, rowi)
                qf_s[hh, rows, :] = q
                kf_s[hh, rows, :] = 1.0 - f
                cf_s[hh, rows, :] = c
                c_s[hh] = c
                worst = jnp.maximum(worst, _block_decay(c_s.at[hh]))
            return worst

        def between_chunks(hh, n, rows):
            lanes = slice(hh * HD, (hh + 1) * HD)
            q = qf_s[hh, rows, :]
            k = kf_s[hh, rows, :]
            c = cf_s[hh, rows, :]
            v = hi_ref[rows, lanes]
            q_s[hh] = q
            k_s[hh] = k
            c_s[hh] = c
            v_s[hh] = v
            st = state_s[hh]
            st_ref[hh, n] = st.astype(BF16)
            o_s[hh] = _dot_nt(q * jnp.exp(c), st)
            last = c_s[hh, CH - 1:CH, :]
            state_s[hh] = st * jnp.exp(last) + _dot_tn(v, k * jnp.exp(last - c))

        def within_chunk_matmul(hh):
            q, k, c, v = q_s[hh], k_s[hh], c_s[hh], v_s[hh]
            a = jnp.zeros((CH, CH), F32)
            for i in range(CH // SB):
                r_i = _block_ref(c_s.at[hh], i)
                qi = q * jnp.exp(jnp.minimum(c - r_i, 0.0))
                kei = k * jnp.exp(jnp.minimum(r_i - c, DECAY_CAP))
                m_i = (rowi >= SB * i) & (rowi < SB * (i + 1)) & (coli <= rowi)
                a = a + jnp.where(m_i, _dot_nt(qi, kei), 0.0)
            o_s[hh] += _dot_nn(a, v)

        def within_chunk_exact(hh):
            q, k, c, v = q_s[hh], k_s[hh], c_s[hh], v_s[hh]
            a_off = jnp.zeros((CH, CH), F32)
            for i in range(1, CH // SB):
                r_i = _block_ref(c_s.at[hh], i)
                qi = q * jnp.exp(jnp.minimum(c - r_i, 0.0))
                kei = k * jnp.exp(jnp.minimum(r_i - c, 0.0))
                m_i = (rowi >= SB * i) & (rowi < SB * (i + 1)) & (coli < SB * i)
                a_off = a_off + jnp.where(m_i, _dot_nt(qi, kei), 0.0)
            o_s[hh] += _dot_nn(a_off, v)
            for i in range(CH // SB):
                blk = slice(SB * i, SB * (i + 1))
                qb = q_s[hh, blk, :]
                cb = c_s[hh, blk, :]
                acc = jnp.zeros((SB, HD), F32)
                for s in range(SB):
                    row = SB * i + s
                    w = jnp.exp(jnp.minimum(cb - c_s[hh, row:row + 1, :], 0.0))
                    a_col = jnp.sum(qb * k_s[hh, row:row + 1, :] * w, axis=-1, keepdims=True)
                    acc = acc + jnp.where(sbi >= s, a_col, 0.0) * v_s[hh, row:row + 1, :]
                o_s[hh, blk, :] += acc

        def norm_and_gate(hh, rows):
            lanes = slice(hh * HD, (hh + 1) * HD)
            ov = o_s[hh]
            oraw_ref[rows, lanes] = ov
            r = lax.rsqrt(jnp.mean(ov * ov, axis=-1, keepdims=True) + EPS)
            hg = hg_ref[rows, lanes]
            bin_ref[rows, lanes] = ((ov * r) * gnv * (hg * _sig(hg))).astype(BF16)

        def chunk_with(within_chunk):
            def chunk(n, carry):
                rows = pl.ds(pl.multiple_of(n * CH, CH), CH)
                for hh in range(NH):
                    between_chunks(hh, n, rows)
                for hh in range(NH):
                    within_chunk(hh)
                for hh in range(NH):
                    norm_and_gate(hh, rows)
                return carry
            return chunk

        worst = lax.fori_loop(0, nch, gates_pass, jnp.zeros((1, HD), F32))
        mild = jnp.max(worst) <= DECAY_CAP
        mild_ref[...] = jnp.broadcast_to(jnp.where(mild, 1.0, 0.0), (8, HD))

        @pl.when(mild)
        def _():
            lax.fori_loop(0, nch, chunk_with(within_chunk_matmul), 0, unroll=4)

        @pl.when(jnp.logical_not(mild))
        def _():
            lax.fori_loop(0, nch, chunk_with(within_chunk_exact), 0)

        bint_ref[...] = bin_ref[...].astype(F32).T.astype(BF16)

    col = lambda off: pl.BlockSpec((S, W), lambda h: (0, off // NH + h))
    head = pl.BlockSpec((S, W), lambda h: (0, h))
    outs = _call(
        body, name=name, grid=(HEADS // NH,),
        in_specs=[col(HQ0), col(HF0), col(HI0), col(HG0), pl.BlockSpec((1, W), lambda h: (0, h)),
                  pl.BlockSpec((1, HD), lambda h: (0, 0))],
        out_specs=[head, pl.BlockSpec((W, S), lambda h: (h, 0)), head,
                   pl.BlockSpec((NH, nch, HD, HD), lambda h: (h, 0, 0, 0)),
                   pl.BlockSpec((8, HD), lambda h: (h, 0))],
        out_shape=[jax.ShapeDtypeStruct((S, D), BF16), jax.ShapeDtypeStruct((D, S), BF16),
                   jax.ShapeDtypeStruct((S, D), F32), jax.ShapeDtypeStruct((HEADS, nch, HD, HD), BF16),
                   jax.ShapeDtypeStruct((8 * HEADS // NH, HD), F32)],
        scratch_shapes=[pltpu.VMEM((NH, CH, HD), F32)] * 5 + [pltpu.VMEM((NH, HD, HD), F32)]
        + [pltpu.VMEM((NH, S, HD), F32)] * 3,
        sem=("parallel",), args=(proj, proj, proj, proj, lb, gn), carry=carry)
    return outs[:5], outs[5:]


def _hgrn_bwd(dbin, proj, oraw, states, mild, lb, gn, name, carry=None):
    S = proj.shape[0]
    nch = S // CH
    W = NH * HD

    def body(db_ref, hq_ref, hf_ref, hi_ref, hg_ref, or_ref, st_ref, mild_ref, lb_ref, gn_ref,
             dq_ref, df_ref, di_ref, dg_ref, dlb_ref, dgn_ref,
             q_s, k_s, c_s, v_s, do_s, dq_s, dk_s, dv_s, dc_s, dqd_s, dkd_s, dl_s, dst_s, dlb_s, dgn_s):
        dst_s[...] = jnp.zeros((NH, HD, HD), F32)
        dlb_s[...] = jnp.zeros((1, W), F32)
        dgn_s[...] = jnp.zeros((1, HD), F32)
        rowi = lax.broadcasted_iota(jnp.int32, (CH, 1), 0)
        rowi2 = lax.broadcasted_iota(jnp.int32, (CH, CH), 0)
        coli2 = lax.broadcasted_iota(jnp.int32, (CH, CH), 1)
        sbi = lax.broadcasted_iota(jnp.int32, (SB, 1), 0)
        gnv = gn_ref[...]
        def between_chunks(hh, n, rows):
            lanes = slice(hh * HD, (hh + 1) * HD)
            q, _, _, f, _, logf = _gates(hq_ref[rows, lanes], hf_ref[rows, lanes], lb_ref[:, lanes])
            k = 1.0 - f
            v = hi_ref[rows, lanes]
            c = _chunk_cumsum(logf, rowi)
            ov = or_ref[rows, lanes]
            hg = hg_ref[rows, lanes]
            sg = _sig(hg)
            r = lax.rsqrt(jnp.mean(ov * ov, axis=-1, keepdims=True) + EPS)
            dbv = db_ref[rows, lanes]
            d_on = dbv * (hg * sg)
            dg_ref[rows, lanes] = (dbv * ((ov * r) * gnv) * _dsilu(hg, sg)).astype(BF16)
            dgn_s[...] += jnp.sum(d_on * (ov * r), axis=0, keepdims=True)
            u = d_on * gnv
            do = r * u - ov * (r * r * r) * jnp.mean(u * ov, axis=-1, keepdims=True)
            q_s[hh] = q
            k_s[hh] = k
            c_s[hh] = c
            v_s[hh] = v
            do_s[hh] = do
            st = st_ref[hh, n].astype(F32)
            dst = dst_s[hh]
            ec = jnp.exp(c)
            last = c_s[hh, CH - 1:CH, :]
            el = jnp.exp(last - c)
            elast = jnp.exp(last)
            dq = _dot_nn(do, st) * ec
            dk = _dot_nn(v, dst) * el
            dq_s[hh] = dq
            dk_s[hh] = dk
            dv_s[hh] = _dot_nt(k * el, dst)
            dc_s[hh] = q * dq - k * dk
            dl_s[hh] = (jnp.sum(k * dk, axis=0, keepdims=True)
                        + elast * jnp.sum(st * dst, axis=0, keepdims=True))
            dst_s[hh] = dst * elast + _dot_tn(do, q * ec)

        def pairs_matmul(hh, first, cap, strict):
            q, k, c, v, do = q_s[hh], k_s[hh], c_s[hh], v_s[hh], do_s[hh]
            d_a = _dot_nt(do, v).astype(BF16).astype(F32)
            d_at = d_a.T
            at = jnp.zeros((CH, CH), F32)
            dq, dk, dcum = dq_s[hh], dk_s[hh], dc_s[hh]
            for i in range(first, CH // SB):
                r_i = _block_ref(c_s.at[hh], i)
                eq = jnp.exp(jnp.minimum(c - r_i, 0.0))
                ek = jnp.exp(jnp.minimum(r_i - c, cap))
                qi = (q * eq).astype(BF16).astype(F32)
                kei = (k * ek).astype(BF16).astype(F32)
                in_t = (rowi2 >= SB * i) & (rowi2 < SB * (i + 1))
                in_s = (coli2 >= SB * i) & (coli2 < SB * (i + 1))
                m_ts = in_t & ((coli2 < SB * i) if strict else (coli2 <= rowi2))
                m_st = in_s & ((rowi2 < SB * i) if strict else (rowi2 <= coli2))
                at = at + jnp.where(m_st, _dot_nt(kei, qi), 0.0)
                dq_i = _dot_nn(jnp.where(m_ts, d_a, 0.0), kei)
                dk_i = _dot_nn(jnp.where(m_st, d_at, 0.0), qi)
                dq = dq + dq_i * eq
                dk = dk + dk_i * ek
                dcum = dcum + (qi * dq_i - kei * dk_i)
            dq_s[hh] = dq
            dk_s[hh] = dk
            dc_s[hh] = dcum
            dv_s[hh] += _dot_nn(at, do)

        def pairs_exact(hh):
            dqd_s[hh] = jnp.zeros((CH, HD), F32)
            dkd_s[hh] = jnp.zeros((CH, HD), F32)
            for i in range(CH // SB):
                blk = slice(SB * i, SB * (i + 1))
                qb = q_s[hh, blk, :]
                cb = c_s[hh, blk, :]
                dob = do_s[hh, blk, :]
                dq_acc = jnp.zeros((SB, HD), F32)
                for s in range(SB):
                    row = SB * i + s
                    ks = k_s[hh, row:row + 1, :]
                    vs = v_s[hh, row:row + 1, :]
                    w = jnp.exp(jnp.minimum(cb - c_s[hh, row:row + 1, :], 0.0))
                    live = sbi >= s
                    a_col = jnp.where(live, jnp.sum(qb * ks * w, axis=-1, keepdims=True), 0.0)
                    da_col = jnp.where(live, jnp.sum(dob * vs, axis=-1, keepdims=True), 0.0)
                    dq_acc = dq_acc + da_col * ks * w
                    dkd_s[hh, row:row + 1, :] += jnp.sum(da_col * qb * w, axis=0, keepdims=True)
                    dv_s[hh, row:row + 1, :] += jnp.sum(a_col * dob, axis=0, keepdims=True)
                dqd_s[hh, blk, :] += dq_acc
            dq_d = dqd_s[hh]
            dk_d = dkd_s[hh]
            dq_s[hh] += dq_d
            dk_s[hh] += dk_d
            dc_s[hh] += q_s[hh] * dq_d - k_s[hh] * dk_d

        def gate_grads(hh, rows):
            lanes = slice(hh * HD, (hh + 1) * HD)
            lbv = lb_ref[:, lanes]
            hq = hq_ref[rows, lanes]
            _, sq, sf, f, fc, _ = _gates(hq, hf_ref[rows, lanes], lbv)
            dlogf = _chunk_rev_cumsum(dc_s[hh], rowi) + dl_s[hh]
            dfv = jnp.where(f > 1e-30, dlogf / fc, 0.0) - dk_s[hh]
            dlb_s[:, lanes] += jnp.sum(dfv * (1.0 - sf), axis=0, keepdims=True)
            df_ref[rows, lanes] = (dfv * (1.0 - lbv) * sf * (1.0 - sf)).astype(BF16)
            dq_ref[rows, lanes] = (dq_s[hh] * _dsilu(hq, sq)).astype(BF16)
            di_ref[rows, lanes] = dv_s[hh].astype(BF16)

        def chunk_with(pairs):
            def chunk(j, carry):
                n = nch - 1 - j
                rows = pl.ds(pl.multiple_of(n * CH, CH), CH)
                for hh in range(NH):
                    between_chunks(hh, n, rows)
                for hh in range(NH):
                    pairs(hh)
                for hh in range(NH):
                    gate_grads(hh, rows)
                return carry
            return chunk

        def pairs_mild(hh):
            pairs_matmul(hh, 0, DECAY_CAP, strict=False)

        def pairs_any(hh):
            pairs_matmul(hh, 1, 0.0, strict=True)
            pairs_exact(hh)

        mild = jnp.max(mild_ref[...]) > 0.5

        @pl.when(mild)
        def _():
            lax.fori_loop(0, nch, chunk_with(pairs_mild), 0, unroll=2)

        @pl.when(jnp.logical_not(mild))
        def _():
            lax.fori_loop(0, nch, chunk_with(pairs_any), 0)

        dlb_ref[...] = dlb_s[...]
        dgn_ref[...] = jnp.broadcast_to(dgn_s[...], (8, HD))

    col = lambda off: pl.BlockSpec((S, W), lambda h: (0, off // NH + h))
    head = pl.BlockSpec((S, W), lambda h: (0, h))
    vec = pl.BlockSpec((1, W), lambda h: (0, h))
    outs = _call(
        body, name=name, grid=(HEADS // NH,),
        in_specs=[head, col(HQ0), col(HF0), col(HI0), col(HG0), head,
                  pl.BlockSpec((NH, nch, HD, HD), lambda h: (h, 0, 0, 0)),
                  pl.BlockSpec((8, HD), lambda h: (h, 0)), vec, pl.BlockSpec((1, HD), lambda h: (0, 0))],
        out_specs=[head, head, head, head, vec, pl.BlockSpec((8, HD), lambda h: (h, 0))],
        out_shape=[jax.ShapeDtypeStruct((S, D), BF16)] * 4
        + [jax.ShapeDtypeStruct((1, D), F32), jax.ShapeDtypeStruct((8 * HEADS // NH, HD), F32)],
        scratch_shapes=[pltpu.VMEM((NH, CH, HD), F32)] * 11
        + [pltpu.VMEM((NH, 1, HD), F32), pltpu.VMEM((NH, HD, HD), F32), pltpu.VMEM((1, W), F32),
           pltpu.VMEM((1, HD), F32)],
        sem=("parallel",), args=(dbin, proj, proj, proj, proj, oraw, states, mild, lb, gn), carry=carry)
    dq, df, di, dg, dlb, dgn = outs[:6]
    return (dq, df, di, dg, dlb, dgn.reshape(HEADS // NH, 8, HD)[:, 0, :]), outs[6:]


def _lower_bounds(l0, l1):
    m = jnp.maximum(l0, l1)
    e0 = jnp.exp(l0 - m)
    e1 = jnp.exp(l1 - m)
    tot = e0 + e1
    p0 = e0 / tot
    p1 = e1 / tot
    return jnp.clip(p0 - p0, 0.0, 1.0), jnp.clip((p0 + p1) - p0, 0.0, 1.0)


def _lb_fwd(logits):
    def body(l_ref, o_ref):
        lb0, lb1 = _lower_bounds(l_ref[0:1, :], l_ref[1:2, :])
        o_ref[0:1, :] = lb0
        o_ref[1:2, :] = lb1

    return pl.pallas_call(body, name="lb_fwd", out_shape=jax.ShapeDtypeStruct((2, D), F32))(logits)


def _lb_bwd(logits, dlb):
    def body(l_ref, d_ref, o_ref):
        _, vjp = jax.vjp(_lower_bounds, l_ref[0:1, :], l_ref[1:2, :])
        g0, g1 = vjp((d_ref[0:1, :], d_ref[1:2, :]))
        o_ref[0:1, :] = g0
        o_ref[1:2, :] = g1

    return pl.pallas_call(body, name="lb_bwd", out_shape=jax.ShapeDtypeStruct((2, D), F32))(logits, dlb)


ADA_PAD = 128


def _ada_fwd(c_pad, w_ada, b_sh):
    ns = w_ada.shape[2]

    def body(c_ref, w_ref, b_ref, o_ref):
        cv = c_ref[...]
        ca = (cv * _sig(cv)).astype(BF16)
        for l in range(2):
            res = jnp.dot(ca, w_ref[l].astype(BF16), preferred_element_type=F32)
            o_ref[:, l * ns:(l + 1) * ns] = res[0:NDEV, :] + b_ref[l:l + 1, :]

    return pl.pallas_call(body, name="ada_fwd", out_shape=jax.ShapeDtypeStruct((NDEV, 2 * ns), F32),
                          compiler_params=_cp())(c_pad, w_ada, b_sh)


def _ada_wgrad(c_pad_t, d_ada_sh):
    ns = d_ada_sh.shape[2]

    def body(c_ref, d_ref, o_ref):
        cv = c_ref[...]
        ca = (cv * _sig(cv)).astype(BF16)
        for l in range(2):
            o_ref[l] = jnp.dot(ca, d_ref[l].astype(BF16), preferred_element_type=F32)

    return pl.pallas_call(body, name="ada_wgrad", out_shape=jax.ShapeDtypeStruct((2, D, ns), F32),
                          compiler_params=_cp())(c_pad_t, d_ada_sh)


def _sum_devices(g):
    _, R, C = g.shape

    def body(g_ref, o_ref):
        acc = g_ref[0]
        for d in range(1, NDEV):
            acc = acc + g_ref[d]
        o_ref[...] = acc

    return pl.pallas_call(body, name="sum_devices", out_shape=jax.ShapeDtypeStruct((R, C), F32),
                          compiler_params=_cp())(g)


def _adamw(w, g, m, v, name):
    R, C = w.shape
    tr = _row_tile(R, max(8, (1 << 19) // C))

    def body(w_ref, g_ref, m_ref, v_ref, d_ref, nm_ref, nv_ref):
        gv = g_ref[...]
        nm = B1 * m_ref[...] + (1.0 - B1) * gv
        nv = B2 * v_ref[...] + (1.0 - B2) * (gv * gv)
        m_hat = nm / (1.0 - B1 ** STEP)
        v_hat = nv / (1.0 - B2 ** STEP)
        d_ref[...] = -LR * (m_hat / (jnp.sqrt(v_hat) + AEPS) + WD * w_ref[...])
        nm_ref[...] = nm
        nv_ref[...] = nv

    tile = pl.BlockSpec((tr, C), lambda i: (i, 0))
    return pl.pallas_call(
        body, name=name, grid=(R // tr,), in_specs=[tile] * 4, out_specs=[tile] * 3,
        out_shape=[jax.ShapeDtypeStruct((R, C), F32)] * 3, compiler_params=_cp(("parallel",)),
    )(w, g, m, v)


def _cast_to_slot(place, w, l, name):
    _, R, C = w.shape
    tr = _row_tile(R, max(8, (1 << 19) // C))

    def body(p_ref, w_ref, o_ref):
        o_ref[...] = w_ref[...].astype(BF16)

    return pl.pallas_call(
        body, name=name, out_shape=jax.ShapeDtypeStruct((NCHIP, R, C), BF16),
        grid_spec=pltpu.PrefetchScalarGridSpec(
            num_scalar_prefetch=1, grid=(R // tr,),
            in_specs=[pl.BlockSpec((None, tr, C), lambda i, p_ref: (l, i, 0))],
            out_specs=pl.BlockSpec((None, tr, C), lambda i, p_ref: (p_ref[0], i, 0))),
        compiler_params=_cp(("parallel",)),
    )(place, w)


def _pair_add(core, g, got, name):
    _, R, C = g.shape
    r2 = R // 2
    tr = _row_tile(r2, max(8, (1 << 19) // C))
    nt = r2 // tr

    def body(c_ref, a_ref, b_ref, o_ref):
        o_ref[...] = (a_ref[...].astype(F32) + b_ref[...].astype(F32)).astype(o_ref.dtype)

    return pl.pallas_call(
        body, name=name, out_shape=jax.ShapeDtypeStruct((NCHIP, r2, C), BF16),
        grid_spec=pltpu.PrefetchScalarGridSpec(
            num_scalar_prefetch=1, grid=(NCHIP, nt),
            in_specs=[pl.BlockSpec((None, tr, C), lambda j, i, c_ref: (j, c_ref[0] * nt + i, 0)),
                      pl.BlockSpec((None, tr, C), lambda j, i, c_ref: (j, i, 0))],
            out_specs=pl.BlockSpec((None, tr, C), lambda j, i, c_ref: (j, i, 0))),
        compiler_params=_cp(("parallel", "parallel")),
    )(core, g, got)


def _chip_sum(place, part, recv, name):
    _, r2, C = part.shape
    tr = _row_tile(r2, max(8, (1 << 18) // C))
    nt = r2 // tr

    def body(p_ref, own_ref, r_ref, o_ref):
        me = p_ref[0]
        own = own_ref[...].astype(F32)
        acc = None
        for j in range(NCHIP):
            slot = jnp.minimum(jnp.where(j > me, j - 1, j), NCHIP - 2)
            term = jnp.where(me == j, own, r_ref[slot].astype(F32))
            acc = term if acc is None else acc + term
        o_ref[...] = acc

    return pl.pallas_call(
        body, name=name, out_shape=jax.ShapeDtypeStruct((2 * r2, C), F32),
        grid_spec=pltpu.PrefetchScalarGridSpec(
            num_scalar_prefetch=1, grid=(nt,),
            in_specs=[pl.BlockSpec((None, tr, C), lambda i, p_ref: (p_ref[0], i, 0)),
                      pl.BlockSpec((NCHIP - 1, tr, C), lambda i, p_ref: (0, i, 0))],
            out_specs=pl.BlockSpec((tr, C), lambda i, p_ref: (p_ref[1] * nt + i, 0))),
        compiler_params=_cp(("parallel",)),
    )(place, part, recv)


def _place():
    x, y, c = lax.axis_index("x"), lax.axis_index("y"), lax.axis_index("c")
    chips = [(1 - x, y), (x, 1 - y), (1 - x, 1 - y)]
    return x, y, c, chips


def _gather_small(blk, name):
    m_per, n = blk.shape

    def body(x_ref, out_ref, send_sems, recv_sems, local_sem):
        x, y, c, chips = _place()
        me, sibling = (x, y, c), (x, y, 1 - c)

        def rows(px, py, pc):
            return out_ref.at[pl.ds((4 * px + 2 * py + pc) * m_per, m_per), :]

        def copy(k, block, to, src=None):
            return pltpu.make_async_remote_copy(
                src_ref=rows(*block) if src is None else src, dst_ref=rows(*block),
                send_sem=send_sems.at[k], recv_sem=recv_sems.at[k], device_id=to, device_id_type=MESH)

        mine = pltpu.make_async_copy(x_ref, rows(*me), local_sem)
        mine.start()
        first = [copy(0, me, sibling, src=x_ref)]
        first += [copy(1 + j, me, (*chip, c), src=x_ref) for j, chip in enumerate(chips)]
        for cp in first:
            cp.start()
        passed = [copy(4 + j, (*chip, c), sibling) for j, chip in enumerate(chips)]
        for j, chip in enumerate(chips):
            copy(1 + j, (*chip, c), me).wait_recv()
            passed[j].start()
        copy(0, sibling, me).wait_recv()
        for j, chip in enumerate(chips):
            copy(4 + j, (*chip, 1 - c), me).wait_recv()
        for cp in first + passed:
            cp.wait_send()
        mine.wait()

    return pl.pallas_call(
        body, name=name, out_shape=jax.ShapeDtypeStruct((NDEV * m_per, n), blk.dtype),
        in_specs=[pl.BlockSpec(memory_space=pltpu.VMEM)], out_specs=pl.BlockSpec(memory_space=pltpu.VMEM),
        scratch_shapes=[pltpu.SemaphoreType.DMA((7,)), pltpu.SemaphoreType.DMA((7,)), pltpu.SemaphoreType.DMA],
        compiler_params=_cp(),
    )(blk)


def _gather_carry(shards):
    n = len(shards)

    def over_ici(outs, send_sems, recv_sems, a, j, chip_xy, slot):
        x, y, c, _ = _place()
        r2 = outs[a].shape[1] // 2
        blk = outs[a].at[slot, pl.ds(c * r2, r2), :]
        return pltpu.make_async_remote_copy(
            src_ref=blk, dst_ref=blk, send_sem=send_sems.at[6 * a + j], recv_sem=recv_sems.at[6 * a + j],
            device_id=(*chip_xy, c), device_id_type=MESH)

    def over_d2d(outs, send_sems, recv_sems, a, j, slot, half):
        x, y, c, _ = _place()
        r2 = outs[a].shape[1] // 2
        blk = outs[a].at[slot, pl.ds(half * r2, r2), :]
        return pltpu.make_async_remote_copy(
            src_ref=blk, dst_ref=blk, send_sem=send_sems.at[6 * a + 3 + j], recv_sem=recv_sems.at[6 * a + 3 + j],
            device_id=(x, y, 1 - c), device_id_type=MESH)

    def start(ins, outs, send_sems, recv_sems):
        x, y, c, chips = _place()
        for a in range(n):
            for j, chip_xy in enumerate(chips):
                over_ici(outs, send_sems, recv_sems, a, j, chip_xy, 2 * x + y).start()

    def finish(ins, outs, send_sems, recv_sems):
        x, y, c, chips = _place()
        for a in range(n):
            for j, (cx, cy) in enumerate(chips):
                over_ici(outs, send_sems, recv_sems, a, j, (cx, cy), 2 * cx + cy).wait_recv()
                over_d2d(outs, send_sems, recv_sems, a, j, 2 * cx + cy, c).start()
        for a in range(n):
            for j, (cx, cy) in enumerate(chips):
                over_d2d(outs, send_sems, recv_sems, a, j, 2 * cx + cy, 1 - c).wait_recv()
        for a in range(n):
            for j, (cx, cy) in enumerate(chips):
                over_ici(outs, send_sems, recv_sems, a, j, (cx, cy), 2 * x + y).wait_send()
                over_d2d(outs, send_sems, recv_sems, a, j, 2 * cx + cy, c).wait_send()

    return _Carry(shards, [jax.ShapeDtypeStruct(s.shape, s.dtype) for s in shards],
                  {a: a for a in range(n)}, 6 * n, start, finish)


def _rs_pair(grads, name):
    n = len(grads)

    def body(*refs):
        ins, gots = refs[:n], refs[n:2 * n]
        send_sems, recv_sems = refs[2 * n:]
        x, y, c, _ = _place()
        cps = []
        for a in range(n):
            r2 = ins[a].shape[1] // 2
            cp = pltpu.make_async_remote_copy(
                src_ref=ins[a].at[:, pl.ds((1 - c) * r2, r2), :], dst_ref=gots[a],
                send_sem=send_sems.at[a], recv_sem=recv_sems.at[a],
                device_id=(x, y, 1 - c), device_id_type=MESH)
            cp.start()
            cps.append(cp)
        for cp in cps:
            cp.wait()

    half = [jax.ShapeDtypeStruct((NCHIP, g.shape[1] // 2, g.shape[2]), g.dtype) for g in grads]
    return pl.pallas_call(
        body, name=name, out_shape=half, in_specs=[ANY] * n, out_specs=[ANY] * n,
        scratch_shapes=[pltpu.SemaphoreType.DMA((n,)), pltpu.SemaphoreType.DMA((n,))],
        compiler_params=_cp(),
    )(*grads)


def _chips_carry(parts):
    n = len(parts)

    def send(ins, outs, send_sems, recv_sems, a, j, chip_xy):
        x, y, c, _ = _place()
        me, them = 2 * x + y, 2 * chip_xy[0] + chip_xy[1]
        return pltpu.make_async_remote_copy(
            src_ref=ins[a].at[them], dst_ref=outs[a].at[me - (me > them).astype(jnp.int32)],
            send_sem=send_sems.at[3 * a + j], recv_sem=recv_sems.at[3 * a + j],
            device_id=(*chip_xy, c), device_id_type=MESH)

    def start(ins, outs, send_sems, recv_sems):
        _, _, _, chips = _place()
        for a in range(n):
            for j, chip_xy in enumerate(chips):
                send(ins, outs, send_sems, recv_sems, a, j, chip_xy).start()

    def finish(ins, outs, send_sems, recv_sems):
        x, y, c, chips = _place()
        me = 2 * x + y
        for a in range(n):
            for j, (cx, cy) in enumerate(chips):
                them = 2 * cx + cy
                blk = outs[a].at[them - (them > me).astype(jnp.int32)]
                pltpu.make_async_remote_copy(
                    src_ref=blk, dst_ref=blk, send_sem=send_sems.at[3 * a + j], recv_sem=recv_sems.at[3 * a + j],
                    device_id=(cx, cy, c), device_id_type=MESH).wait_recv()
        for a in range(n):
            for j, chip_xy in enumerate(chips):
                send(ins, outs, send_sems, recv_sems, a, j, chip_xy).wait_send()

    return _Carry(parts, [jax.ShapeDtypeStruct((NCHIP - 1,) + p.shape[1:], p.dtype) for p in parts], {},
                  3 * n, start, finish)


def _rs_swap(fulls):
    n = len(fulls)

    def body(*refs):
        outs = refs[n:2 * n]
        send_sems, recv_sems = refs[2 * n:]
        x, y, c, _ = _place()
        cps = []
        for a in range(n):
            r2 = outs[a].shape[0] // 2
            mine = outs[a].at[pl.ds(c * r2, r2), :]
            cp = pltpu.make_async_remote_copy(
                src_ref=mine, dst_ref=mine, send_sem=send_sems.at[a], recv_sem=recv_sems.at[a],
                device_id=(x, y, 1 - c), device_id_type=MESH)
            cp.start()
            cps.append(cp)
        for a in range(n):
            r2 = outs[a].shape[0] // 2
            blk = outs[a].at[pl.ds((1 - c) * r2, r2), :]
            pltpu.make_async_remote_copy(
                src_ref=blk, dst_ref=blk, send_sem=send_sems.at[a], recv_sem=recv_sems.at[a],
                device_id=(x, y, 1 - c), device_id_type=MESH).wait_recv()
        for cp in cps:
            cp.wait_send()

    return pl.pallas_call(
        body, name="rs_swap", out_shape=[jax.ShapeDtypeStruct(f.shape, f.dtype) for f in fulls],
        in_specs=[ANY] * n, out_specs=[ANY] * n, input_output_aliases={a: a for a in range(n)},
        scratch_shapes=[pltpu.SemaphoreType.DMA((n,)), pltpu.SemaphoreType.DMA((n,))],
        compiler_params=_cp(),
    )(*fulls)


def _mm_ride(a, b, carry, **kw):
    if carry is None:
        return _mm(a, b, **kw), []
    return _mm(a, b, carry=carry, **kw)


def _layer_fwd(l, x, ada, w, small, ride):
    shift, scale, gate = ada[:, 0:D], ada[:, D:2 * D], ada[:, 2 * D:3 * D]
    h, h_t = _prenorm_fwd(x, small["g_pre"][l], scale, shift, f"prenorm_fwd{l}")
    proj, landed = _mm_ride(h, w["w_in"][l], ride["proj"][0], name=f"proj{l}", b_mode="nn_sh")
    ride["proj"][1](landed)
    a_in, a_in_t = _pool_fwd(proj, small["pool_w"][l], small["pool_scale"][l], f"pool_fwd{l}")
    (b_in, b_in_t, o_raw, states, mild), landed = _hgrn_fwd(proj, small["lb"][l], small["hgrn_norm_g"][l],
                                                           f"hgrn_fwd{l}", carry=ride["hgrn"][0])
    ride["hgrn"][1](landed)
    br_a = _mm(a_in, w["w_pool_o"][l], name=f"branch_a{l}", b_mode="nn_sh")
    br_b = _mm(b_in, w["w_hgrn_o"][l].reshape(D, D), name=f"branch_b{l}")
    merged, merged_t = _merge_fwd(proj, br_a, br_b, f"merge_fwd{l}")
    y = _mm(merged, w["w_out"][l].reshape(D, D), name=f"out_proj{l}")
    x_new = _postnorm_fwd(x, y, gate, small["g_post"][l], f"postnorm_fwd{l}")
    saved = dict(x=x, h_t=h_t, proj=proj, a_in_t=a_in_t, b_in_t=b_in_t, o_raw=o_raw, states=states, mild=mild,
                 br_a=br_a, br_b=br_b, merged_t=merged_t, y=y, scale=scale, gate=gate)
    return x_new, saved


def _layer_bwd(l, dxn, sv, w, small, ride):
    dy, dgate, dg_post = _postnorm_bwd(dxn, sv["y"], sv["gate"], small["g_post"][l], f"postnorm_bwd{l}")
    w_out = w["w_out"][l].reshape(D, D)
    dmerged = _mm(dy, w_out, name=f"d_merged{l}", b_mode="nt")
    gw_out = _mm(sv["merged_t"], dy, name=f"gw_out{l}", out_dtype=BF16)
    dbr_a, dbr_b, dmg = _merge_bwd(dmerged, sv["proj"], sv["br_a"], sv["br_b"], f"merge_bwd{l}")
    da_in = _mm(dbr_a, w["w_pool_o"][l], name=f"d_a_in{l}", b_mode="nt_shk")
    gw_pool_o = _mm(sv["a_in_t"], dbr_a, name=f"gw_pool_o{l}", out_shards=NCHIP, out_dtype=BF16)
    db_in = _mm(dbr_b, w["w_hgrn_o"][l].reshape(D, D), name=f"d_b_in{l}", b_mode="nt")
    gw_hgrn_o = _mm(sv["b_in_t"], dbr_b, name=f"gw_hgrn_o{l}", out_dtype=BF16)
    big = dict(w_pool_o=gw_pool_o, w_hgrn_o=gw_hgrn_o.reshape(NCHIP, D // NCHIP, D),
               w_out=gw_out.reshape(NCHIP, D // NCHIP, D))
    carry, landed = ride["hgrn"](big)
    (dhq, dhf, dhi, dhg, dlb, dgn), outs = _hgrn_bwd(db_in, sv["proj"], sv["o_raw"], sv["states"], sv["mild"],
                                                     small["lb"][l], small["hgrn_norm_g"][l], f"hgrn_bwd{l}",
                                                     carry=carry)
    landed(outs)
    dpv, dpg, dpw, dpsc = _pool_bwd(da_in, sv["proj"], small["pool_w"][l], small["pool_scale"][l],
                                    f"pool_bwd{l}")
    dproj = jnp.concatenate([dpv, dpg, dhq, dhf, dhi, dhg, dmg], axis=1)
    big["w_in"] = _mm(sv["h_t"], dproj, name=f"gw_in{l}", out_shards=NCHIP, out_dtype=BF16)
    carry, landed = ride["d_h"](big)
    dh, outs = _mm_ride(dproj, w["w_in"][l], carry, name=f"d_h{l}", b_mode="nt_shk")
    landed(outs)
    dx, dshift, dscale, dg_pre = _prenorm_bwd(dh, dxn, sv["x"], small["g_pre"][l], sv["scale"],
                                              f"prenorm_bwd{l}")
    little = dict(d_ada=jnp.concatenate([dshift, dscale, dgate], axis=1), g_pre=dg_pre, g_post=dg_post,
                  pool_w=dpw, pool_scale=dpsc, lb=dlb, hgrn_norm_g=jnp.sum(dgn, axis=0, keepdims=True))
    return dx, big, little


SMALL_ROWS = 176


def _rows8(t):
    t = t.reshape(-1, D)
    return jnp.pad(t, ((0, -t.shape[0] % 8), (0, 0)))


def _pack_small_weights(b_ada, g_pre, g_post, lb_logits, pool_w, pool_scale, hgrn_norm_g):
    gn = jnp.pad(hgrn_norm_g.reshape(1, 2 * HD), ((0, 0), (0, D - 2 * HD)))
    return jnp.concatenate([_rows8(b_ada), _rows8(g_pre), _rows8(g_post), _rows8(lb_logits), _rows8(pool_w),
                            _rows8(pool_scale), _rows8(gn)], axis=0)


def _pack_small(parts):
    both = lambda key: jnp.stack([parts[l][key] for l in range(2)])
    return _pack_small_weights(both("d_ada"), both("g_pre"), both("g_post"), both("lb"), both("pool_w"),
                               both("pool_scale"), both("hgrn_norm_g"))


def _unpack_small(p):
    return (p[0:6].reshape(2, 3 * D), p[8:10], p[16:18], p[24:26], p[32:160].reshape(2, GROUPS, 128, 128),
            p[160:161].reshape(2, POOL_W), p[168:169, 0:2 * HD].reshape(2, HD))


def kernel(x, c, w_ada, b_ada, g_pre, g_post, w_in, pool_w, pool_scale, lb_logits, hgrn_norm_g, w_pool_o, w_hgrn_o, w_out, loss_target, m_w_ada, m_b_ada, m_g_pre, m_g_post, m_w_in, m_pool_w, m_pool_scale, m_lb_logits, m_hgrn_norm_g, m_w_pool_o, m_w_hgrn_o, m_w_out, v_w_ada, v_b_ada, v_g_pre, v_g_post, v_w_in, v_pool_w, v_pool_scale, v_lb_logits, v_hgrn_norm_g, v_w_pool_o, v_w_hgrn_o, v_w_out):
    ax, ay, ac = lax.axis_index("x"), lax.axis_index("y"), lax.axis_index("c")
    chip = 2 * ax + ay
    dev = 2 * chip + ac
    xe, te = x[0], loss_target[0]
    ada_s = w_ada.shape[2]

    big_names = ("w_in", "w_pool_o", "w_hgrn_o", "w_out")
    big_w = (w_in, w_pool_o, w_hgrn_o, w_out)
    core = jnp.stack([ac]).astype(jnp.int32)
    place = jnp.stack([chip, ac]).astype(jnp.int32)
    slots = {(k, l): _cast_to_slot(place, t, l, f"cast_{k}{l}") for l in range(2) for k, t in zip(big_names, big_w)}
    w = {k: [None, None] for k in big_names}
    (w["w_in"][0],) = _run_carry(_gather_carry([slots["w_in", 0]]), "gather_w_in0")
    later = [(k, l) for l in range(2) for k in big_names[1:]]

    def landed_later(outs):
        for (k, l), o in zip(later, outs):
            w[k][l] = o

    def landed_w_in1(outs):
        (w["w_in"][1],) = outs

    ride_fwd0 = dict(proj=(_gather_carry([slots[t] for t in later]), landed_later),
                     hgrn=(_gather_carry([slots["w_in", 1]]), landed_w_in1))
    no_carry = (None, lambda outs: None)

    c_all = _gather_small(jnp.broadcast_to(c, (8, D)), "gather_c").reshape(NDEV, 8, D)[:, 0, :]
    c_pad = jnp.pad(c_all, ((0, ADA_PAD - NDEV), (0, 0)))
    b_sh = lax.dynamic_slice(b_ada, (0, chip * ada_s), (2, ada_s))
    ada_cols = _gather_small(_ada_fwd(c_pad, w_ada, b_sh), "gather_ada")
    ada_cols = ada_cols.reshape(NCHIP, 2, NDEV, 2, ada_s)[:, 0]
    ada_all = jnp.transpose(ada_cols, (2, 1, 0, 3)).reshape(2, NDEV, 3 * D)
    ada_me = lax.dynamic_slice(ada_all, (0, dev, 0), (2, 1, 3 * D))

    lbs = _lb_fwd(lb_logits)
    small = dict(g_pre=g_pre[:, None, :], g_post=g_post[:, None, :], pool_w=pool_w,
                 pool_scale=pool_scale[:, None, :], lb=lbs[:, None, :], hgrn_norm_g=hgrn_norm_g[:, None, :])

    x1, sv0 = _layer_fwd(0, xe, ada_me[0], w, small, ride_fwd0)
    x2, sv1 = _layer_fwd(1, x1, ada_me[1], w, small, dict(proj=no_carry, hgrn=no_carry))
    dx2, loss_blk = _loss_head(x2, te, "loss_head")

    parts, recv = {}, {}

    def pair_sums(keys, grads, tag):
        got = _rs_pair(grads, f"rs_pair_{tag}")
        for kl, g, o in zip(keys, grads, got):
            parts[kl] = _pair_add(core, g, o, f"rs_add_{kl[0]}{kl[1]}")

    def exchange(keys):
        def landed(outs):
            recv.update(zip(keys, outs))
        return _chips_carry([parts[kl] for kl in keys]), landed

    no_ride = lambda big: no_carry
    dx1, big1, little1 = _layer_bwd(1, dx2, sv1, w, small, dict(hgrn=no_ride, d_h=no_ride))
    keys1 = [(k, 1) for k in big_names]
    pair_sums(keys1, [big1[k] for k in big_names], "l1")

    def ride_hgrn0(big):
        early = [(k, 0) for k in big_names[1:]]
        pair_sums(early, [big[k] for k in big_names[1:]], "l0_early")
        return exchange(keys1 + early)

    def ride_d_h0(big):
        pair_sums([("w_in", 0)], [big["w_in"]], "l0_w_in")
        return exchange([("w_in", 0)])

    dx0, big0, little0 = _layer_bwd(0, dx1, sv0, w, small, dict(hgrn=ride_hgrn0, d_h=ride_d_h0))
    loss = lax.psum(loss_blk[0, 0], ("x", "y", "c"))
    order = [(k, l) for l in range(2) for k in big_names]
    red = _rs_swap([_chip_sum(place, parts[kl], recv[kl], f"rs_sum_{kl[0]}{kl[1]}") for kl in order])
    g_big = {k: jnp.stack([red[i], red[4 + i]]) for i, k in enumerate(big_names)}

    packed = _gather_small(_pack_small([little0, little1]), "gather_small")
    packed = packed.reshape(NDEV, SMALL_ROWS, D)
    g_small = _sum_devices(packed)
    g_b_ada, g_g_pre, g_g_post, g_lb, g_pool_w, g_pool_scale, g_norm_g = _unpack_small(g_small)
    g_lb_logits = _lb_bwd(lb_logits, g_lb)
    d_ada_all = packed[:, 0:6, :].reshape(NDEV, 2, 3 * D)
    d_ada_sh = lax.dynamic_slice(jnp.transpose(d_ada_all, (1, 0, 2)), (0, 0, chip * ada_s), (2, NDEV, ada_s))
    d_ada_sh = jnp.pad(d_ada_sh, ((0, 0), (0, ADA_PAD - NDEV), (0, 0)))
    g_w_ada = _ada_wgrad(c_pad.T, d_ada_sh)

    def upd(wt, g, m, v, name):
        shp = wt.shape
        two = lambda t: t.reshape(-1, shp[-1])
        d, nm, nv = _adamw(two(wt), two(g), two(m), two(v), name)
        return d.reshape(shp), nm.reshape(shp), nv.reshape(shp)

    u_w_ada = upd(w_ada, g_w_ada, m_w_ada, v_w_ada, "adamw_w_ada")
    u_w_in = upd(w_in, g_big["w_in"], m_w_in, v_w_in, "adamw_w_in")
    u_w_pool_o = upd(w_pool_o, g_big["w_pool_o"], m_w_pool_o, v_w_pool_o, "adamw_w_pool_o")
    u_w_hgrn_o = upd(w_hgrn_o, g_big["w_hgrn_o"], m_w_hgrn_o, v_w_hgrn_o, "adamw_w_hgrn_o")
    u_w_out = upd(w_out, g_big["w_out"], m_w_out, v_w_out, "adamw_w_out")
    g_small_fixed = _pack_small_weights(g_b_ada, g_g_pre, g_g_post, g_lb_logits, g_pool_w, g_pool_scale,
                                        g_norm_g)
    sw = _pack_small_weights(b_ada, g_pre, g_post, lb_logits, pool_w, pool_scale, hgrn_norm_g)
    sm = _pack_small_weights(m_b_ada, m_g_pre, m_g_post, m_lb_logits, m_pool_w, m_pool_scale, m_hgrn_norm_g)
    sv = _pack_small_weights(v_b_ada, v_g_pre, v_g_post, v_lb_logits, v_pool_w, v_pool_scale, v_hgrn_norm_g)
    u_small = [_unpack_small(t) for t in _adamw(sw, g_small_fixed, sm, sv, "adamw_small")]

    grads_out = (g_w_ada, g_b_ada, g_g_pre, g_g_post, g_big["w_in"], g_pool_w, g_pool_scale, g_lb_logits,
                 g_norm_g, g_big["w_pool_o"], g_big["w_hgrn_o"], g_big["w_out"])

    def ordered(k):
        s = u_small[k]
        return (u_w_ada[k], s[0], s[1], s[2], u_w_in[k], s[4], s[5], s[3], s[6], u_w_pool_o[k], u_w_hgrn_o[k],
                u_w_out[k])

    return (loss, dx0[None], *grads_out, *ordered(0), *ordered(1), *ordered(2))
```

```python
import functools

import jax
import jax.numpy as jnp
from jax import lax
from jax.experimental import pallas as pl
from jax.experimental.pallas import tpu as pltpu

F32 = jnp.float32
BF16 = jnp.bfloat16
MESH = pl.DeviceIdType.MESH

D = 1024
HEADS = 8
HD = 128
GROUPS = 4
POOL_W = 512
WINDOWS = (2, 4, 8, 16)
CH = 64
SB = 16
NH = 2
IN_W = 7168
NCHIP = 4
NDEV = 8
EPS = 1e-6
PV0, PG0, HQ0, HF0, HI0, HG0 = 0, 4, 8, 16, 24, 32
MGP_BLK, MGH_BLK = 5, 6

LR, B1, B2, AEPS, WD, STEP = 0.001, 0.9, 0.999, 1e-08, 0.01, 10
VMEM_LIMIT = 56 * 1024 * 1024


def _cp(sem=None, **kw):
    if sem is not None:
        kw["dimension_semantics"] = sem
    return pltpu.CompilerParams(vmem_limit_bytes=VMEM_LIMIT, **kw)


def _sig(z):
    return 1.0 / (1.0 + jnp.exp(-z))


def _dsilu(z, s):
    return s * (1.0 + z * (1.0 - s))


def _row_tile(rows, cap):
    if rows <= cap:
        return rows
    t = 1 << (cap.bit_length() - 1)
    while rows % t:
        t //= 2
    return t


ANY = pl.BlockSpec(memory_space=pl.ANY)


class _Carry:
    def __init__(self, ins, outs, aliases, n_sem, start, finish):
        self.ins, self.outs, self.aliases, self.n_sem = list(ins), list(outs), dict(aliases), n_sem
        self.start, self.finish = start, finish


def _call(body, *, name, grid, in_specs, out_specs, out_shape, args, scratch_shapes=(), sem=None, carry=None):
    in_specs, out_specs, out_shape = list(in_specs), list(out_specs), list(out_shape)
    scratch_shapes = list(scratch_shapes)
    if carry is None:
        outs = pl.pallas_call(body, name=name, grid=grid, in_specs=in_specs, out_specs=out_specs,
                              out_shape=out_shape, scratch_shapes=scratch_shapes,
                              compiler_params=_cp(sem))(*args)
        return list(outs)
    n_in, n_out, n_scr = len(in_specs), len(out_specs), len(scratch_shapes)
    c_in, c_out = len(carry.ins), len(carry.outs)

    def wrapped(*refs):
        k_in, rest = refs[:n_in], refs[n_in:]
        ci, rest = rest[:c_in], rest[c_in:]
        k_out, rest = rest[:n_out], rest[n_out:]
        co, rest = rest[:c_out], rest[c_out:]
        k_scr, (ssem, rsem) = rest[:n_scr], rest[n_scr:]
        pids = [pl.program_id(d) for d in range(len(grid))]
        first = functools.reduce(jnp.logical_and, [p == 0 for p in pids])
        last = functools.reduce(jnp.logical_and, [p == g - 1 for p, g in zip(pids, grid)])

        @pl.when(first)
        def _():
            carry.start(ci, co, ssem, rsem)

        body(*k_in, *k_out, *k_scr)

        @pl.when(last)
        def _():
            carry.finish(ci, co, ssem, rsem)

    outs = pl.pallas_call(
        wrapped, name=name, grid=grid, in_specs=in_specs + [ANY] * c_in, out_specs=out_specs + [ANY] * c_out,
        out_shape=out_shape + carry.outs,
        input_output_aliases={n_in + i: n_out + o for i, o in carry.aliases.items()},
        scratch_shapes=scratch_shapes + [pltpu.SemaphoreType.DMA((carry.n_sem,))] * 2,
        compiler_params=_cp(("arbitrary",) * len(grid)),
    )(*args, *carry.ins)
    return list(outs)


def _run_carry(carry, name):
    c_in, c_out = len(carry.ins), len(carry.outs)

    def body(*refs):
        ci, co, (ssem, rsem) = refs[:c_in], refs[c_in:c_in + c_out], refs[c_in + c_out:]
        carry.start(ci, co, ssem, rsem)
        carry.finish(ci, co, ssem, rsem)

    outs = pl.pallas_call(
        body, name=name, in_specs=[ANY] * c_in, out_specs=[ANY] * c_out, out_shape=carry.outs,
        input_output_aliases=carry.aliases,
        scratch_shapes=[pltpu.SemaphoreType.DMA((carry.n_sem,))] * 2, compiler_params=_cp(),
    )(*carry.ins)
    return list(outs)


def _mm(a, b, *, name, b_mode="nn", out_shards=0, tm=1024, tn=256, tk=None, out_dtype=F32, carry=None):
    M, K = a.shape
    if b_mode == "nn":
        N = b.shape[1]
    elif b_mode == "nt":
        N = b.shape[0]
    elif b_mode == "nn_sh":
        N = b.shape[0] * b.shape[2]
    else:
        N = b.shape[1]
    tm = _row_tile(M, tm)
    if b_mode == "nn_sh":
        tn = _row_tile(b.shape[2], tn)
    elif out_shards:
        tn = _row_tile(N // out_shards, tn)
    else:
        tn = _row_tile(N, tn)
    if tk is None:
        tk = K if b_mode != "nt_shk" else b.shape[2]
    if b_mode == "nt_shk":
        tk = _row_tile(b.shape[2], tk)
    nm, nn, nk = M // tm, N // tn, K // tk

    a_spec = pl.BlockSpec((tm, tk), lambda m, n, k: (m, k))
    if b_mode == "nn":
        b_spec = pl.BlockSpec((tk, tn), lambda m, n, k: (k, n))
    elif b_mode == "nt":
        b_spec = pl.BlockSpec((tn, tk), lambda m, n, k: (n, k))
    elif b_mode == "nn_sh":
        nps = b.shape[2] // tn
        b_spec = pl.BlockSpec((None, tk, tn), lambda m, n, k: (n // nps, k, n % nps))
    else:
        kps = b.shape[2] // tk
        b_spec = pl.BlockSpec((None, tn, tk), lambda m, n, k: (k // kps, n, k % kps))
    if out_shards:
        ops = (N // out_shards) // tn
        o_spec = pl.BlockSpec((None, tm, tn), lambda m, n, k: (n // ops, m, n % ops))
        o_shape = jax.ShapeDtypeStruct((out_shards, M, N // out_shards), out_dtype)
    else:
        o_spec = pl.BlockSpec((tm, tn), lambda m, n, k: (m, n))
        o_shape = jax.ShapeDtypeStruct((M, N), out_dtype)
    trans_b = b_mode in ("nt", "nt_shk")
    dn = (((1,), (1,)), ((), ())) if trans_b else (((1,), (0,)), ((), ()))

    def body(a_ref, b_ref, o_ref, acc_ref):
        k = pl.program_id(2)

        @pl.when(k == 0)
        def _():
            acc_ref[...] = jnp.zeros(acc_ref.shape, F32)

        acc_ref[...] += lax.dot_general(a_ref[...].astype(BF16), b_ref[...].astype(BF16), dn,
                                        preferred_element_type=F32)

        @pl.when(k == nk - 1)
        def _():
            o_ref[...] = acc_ref[...].astype(o_ref.dtype)

    outs = _call(body, name=name, grid=(nm, nn, nk), in_specs=[a_spec, b_spec], out_specs=[o_spec],
                 out_shape=[o_shape], scratch_shapes=[pltpu.VMEM((tm, tn), F32)],
                 sem=("parallel", "parallel", "arbitrary"), args=(a, b), carry=carry)
    return outs[0] if carry is None else (outs[0], outs[1:])


def _rowvec(n=D):
    return pl.BlockSpec((1, n), lambda i: (0, 0))


def _prenorm_fwd(x, g, scale, shift, name):
    S = x.shape[0]
    tr = _row_tile(S, 256)

    def body(x_ref, g_ref, sc_ref, sh_ref, h_ref, ht_ref):
        xv = x_ref[...]
        r = lax.rsqrt(jnp.mean(xv * xv, axis=-1, keepdims=True) + EPS)
        hv = (xv * r) * g_ref[...] * (1.0 + sc_ref[...]) + sh_ref[...]
        h_ref[...] = hv.astype(BF16)
        ht_ref[...] = hv.T.astype(BF16)

    return pl.pallas_call(
        body, name=name, grid=(S // tr,),
        in_specs=[pl.BlockSpec((tr, D), lambda i: (i, 0)), _rowvec(), _rowvec(), _rowvec()],
        out_specs=[pl.BlockSpec((tr, D), lambda i: (i, 0)), pl.BlockSpec((D, tr), lambda i: (0, i))],
        out_shape=[jax.ShapeDtypeStruct((S, D), BF16), jax.ShapeDtypeStruct((D, S), BF16)],
        compiler_params=_cp(("parallel",)),
    )(x, g, scale, shift)


def _prenorm_bwd(dh, dxn, x, g, scale, name):
    S = x.shape[0]
    tr = _row_tile(S, 256)

    def body(dh_ref, dxn_ref, x_ref, g_ref, sc_ref, dx_ref, dsh_ref, dsc_ref, dg_ref):
        i = pl.program_id(0)

        @pl.when(i == 0)
        def _():
            dsh_ref[...] = jnp.zeros((1, D), F32)
            dsc_ref[...] = jnp.zeros((1, D), F32)
            dg_ref[...] = jnp.zeros((1, D), F32)

        xv = x_ref[...]
        dhv = dh_ref[...]
        gv = g_ref[...]
        mod = 1.0 + sc_ref[...]
        r = lax.rsqrt(jnp.mean(xv * xv, axis=-1, keepdims=True) + EPS)
        xh = xv * r
        dsh_ref[...] += jnp.sum(dhv, axis=0, keepdims=True)
        dsc_ref[...] += jnp.sum(dhv * (xh * gv), axis=0, keepdims=True)
        dg_ref[...] += jnp.sum(dhv * mod * xh, axis=0, keepdims=True)
        u = dhv * mod * gv
        dx_ref[...] = dxn_ref[...] + r * u - xv * (r * r * r) * jnp.mean(u * xv, axis=-1, keepdims=True)

    tile = pl.BlockSpec((tr, D), lambda i: (i, 0))
    return pl.pallas_call(
        body, name=name, grid=(S // tr,),
        in_specs=[tile, tile, tile, _rowvec(), _rowvec()],
        out_specs=[tile, _rowvec(), _rowvec(), _rowvec()],
        out_shape=[jax.ShapeDtypeStruct((S, D), F32)] + [jax.ShapeDtypeStruct((1, D), F32)] * 3,
        compiler_params=_cp(("arbitrary",)),
    )(dh, dxn, x, g, scale)


def _postnorm_fwd(x, y, gate, g, name):
    S = x.shape[0]
    tr = _row_tile(S, 256)

    def body(x_ref, y_ref, gate_ref, g_ref, o_ref):
        yv = y_ref[...]
        r = lax.rsqrt(jnp.mean(yv * yv, axis=-1, keepdims=True) + EPS)
        o_ref[...] = x_ref[...] + gate_ref[...] * ((yv * r) * g_ref[...])

    tile = pl.BlockSpec((tr, D), lambda i: (i, 0))
    return pl.pallas_call(
        body, name=name, grid=(S // tr,), in_specs=[tile, tile, _rowvec(), _rowvec()],
        out_specs=tile, out_shape=jax.ShapeDtypeStruct((S, D), F32), compiler_params=_cp(("parallel",)),
    )(x, y, gate, g)


def _postnorm_bwd(dxn, y, gate, g, name):
    S = y.shape[0]
    tr = _row_tile(S, 256)

    def body(dxn_ref, y_ref, gate_ref, g_ref, dy_ref, dgate_ref, dg_ref):
        i = pl.program_id(0)

        @pl.when(i == 0)
        def _():
            dgate_ref[...] = jnp.zeros((1, D), F32)
            dg_ref[...] = jnp.zeros((1, D), F32)

        yv = y_ref[...]
        dv = dxn_ref[...]
        gv = g_ref[...]
        gt = gate_ref[...]
        r = lax.rsqrt(jnp.mean(yv * yv, axis=-1, keepdims=True) + EPS)
        yh = yv * r
        dgate_ref[...] += jnp.sum(dv * (yh * gv), axis=0, keepdims=True)
        dg_ref[...] += jnp.sum(dv * gt * yh, axis=0, keepdims=True)
        u = dv * gt * gv
        dy_ref[...] = (r * u - yv * (r * r * r) * jnp.mean(u * yv, axis=-1, keepdims=True)).astype(BF16)

    tile = pl.BlockSpec((tr, D), lambda i: (i, 0))
    return pl.pallas_call(
        body, name=name, grid=(S // tr,), in_specs=[tile, tile, _rowvec(), _rowvec()],
        out_specs=[tile, _rowvec(), _rowvec()],
        out_shape=[jax.ShapeDtypeStruct((S, D), BF16), jax.ShapeDtypeStruct((1, D), F32),
                   jax.ShapeDtypeStruct((1, D), F32)],
        compiler_params=_cp(("arbitrary",)),
    )(dxn, y, gate, g)


def _loss_head(xo, target, name):
    S = xo.shape[0]
    tr = _row_tile(S, 256)

    def body(x_ref, t_ref, dx_ref, l_ref):
        i = pl.program_id(0)

        @pl.when(i == 0)
        def _():
            l_ref[...] = jnp.zeros((8, 128), F32)

        err = x_ref[...] - t_ref[...]
        dx_ref[...] = err * (1.0 / D)
        l_ref[...] += 0.5 * jnp.sum(jnp.mean(err * err, axis=-1, keepdims=True))

    tile = pl.BlockSpec((tr, D), lambda i: (i, 0))
    return pl.pallas_call(
        body, name=name, grid=(S // tr,), in_specs=[tile, tile],
        out_specs=[tile, pl.BlockSpec((8, 128), lambda i: (0, 0))],
        out_shape=[jax.ShapeDtypeStruct((S, D), F32), jax.ShapeDtypeStruct((8, 128), F32)],
        compiler_params=_cp(("arbitrary",)),
    )(xo, target)


def _merge_fwd(proj, br_a, br_b, name):
    S = proj.shape[0]
    tr = _row_tile(S, 256)

    def body(mgp_ref, mgh_ref, a_ref, b_ref, o_ref, ot_ref):
        mv = _sig(mgp_ref[...]) * a_ref[...] + _sig(mgh_ref[...]) * b_ref[...]
        o_ref[...] = mv.astype(BF16)
        ot_ref[...] = mv.T.astype(BF16)

    tile = pl.BlockSpec((tr, D), lambda i: (i, 0))
    return pl.pallas_call(
        body, name=name, grid=(S // tr,),
        in_specs=[pl.BlockSpec((tr, D), lambda i: (i, MGP_BLK)), pl.BlockSpec((tr, D), lambda i: (i, MGH_BLK)),
                  tile, tile],
        out_specs=[tile, pl.BlockSpec((D, tr), lambda i: (0, i))],
        out_shape=[jax.ShapeDtypeStruct((S, D), BF16), jax.ShapeDtypeStruct((D, S), BF16)],
        compiler_params=_cp(("parallel",)),
    )(proj, proj, br_a, br_b)


def _merge_bwd(dm, proj, br_a, br_b, name):
    S = proj.shape[0]
    tr = _row_tile(S, 256)

    def body(dm_ref, mgp_ref, mgh_ref, a_ref, b_ref, da_ref, db_ref, dmg_ref):
        dmv = dm_ref[...]
        sp = _sig(mgp_ref[...])
        sh = _sig(mgh_ref[...])
        da_ref[...] = (dmv * sp).astype(BF16)
        db_ref[...] = (dmv * sh).astype(BF16)
        dmg_ref[:, 0:D] = (dmv * a_ref[...] * sp * (1.0 - sp)).astype(BF16)
        dmg_ref[:, D:2 * D] = (dmv * b_ref[...] * sh * (1.0 - sh)).astype(BF16)

    tile = pl.BlockSpec((tr, D), lambda i: (i, 0))
    return pl.pallas_call(
        body, name=name, grid=(S // tr,),
        in_specs=[tile, pl.BlockSpec((tr, D), lambda i: (i, MGP_BLK)),
                  pl.BlockSpec((tr, D), lambda i: (i, MGH_BLK)), tile, tile],
        out_specs=[tile, tile, pl.BlockSpec((tr, 2 * D), lambda i: (i, 0))],
        out_shape=[jax.ShapeDtypeStruct((S, D), BF16), jax.ShapeDtypeStruct((S, D), BF16),
                   jax.ShapeDtypeStruct((S, 2 * D), BF16)],
        compiler_params=_cp(("parallel",)),
    )(dm, proj, proj, br_a, br_b)


def _pool_pieces(u, g, S):
    rowi = lax.broadcasted_iota(jnp.int32, (S, 1), 0)

    def down(z, k):
        return jnp.where(rowi >= k, pltpu.roll(z, k, axis=0), 0.0)

    s2 = u + down(u, 1)
    s4 = s2 + down(s2, 2)
    s8 = s4 + down(s4, 4)
    s16 = s8 + down(s8, 8)
    win = jnp.where(g == 0, s2, jnp.where(g == 1, s4, jnp.where(g == 2, s8, s16)))
    w = jnp.where(g == 0, 2, jnp.where(g == 1, 4, jnp.where(g == 2, 8, 16)))
    count = jnp.minimum(rowi + 1, w).astype(F32)
    return win / count - u, count, rowi


def _pool_fwd(proj, pw, pscale, name):
    S = proj.shape[0]

    def body(pv_ref, pg_ref, pw_ref, sc_ref, a_ref, at_ref):
        g = pl.program_id(0)
        pooled, _, _ = _pool_pieces(pv_ref[...], g, S)
        pm = jnp.dot(pooled.astype(BF16), pw_ref[...].astype(BF16), preferred_element_type=F32)
        pgv = pg_ref[...]
        av = pm * sc_ref[...] * (pgv * _sig(pgv))
        a_ref[...] = av.astype(BF16)
        at_ref[...] = av.T.astype(BF16)

    return pl.pallas_call(
        body, name=name, grid=(GROUPS,),
        in_specs=[pl.BlockSpec((S, 128), lambda g: (0, PV0 + g)), pl.BlockSpec((S, 128), lambda g: (0, PG0 + g)),
                  pl.BlockSpec((None, 128, 128), lambda g: (g, 0, 0)), pl.BlockSpec((1, 128), lambda g: (0, g))],
        out_specs=[pl.BlockSpec((S, 128), lambda g: (0, g)), pl.BlockSpec((128, S), lambda g: (g, 0))],
        out_shape=[jax.ShapeDtypeStruct((S, POOL_W), BF16), jax.ShapeDtypeStruct((POOL_W, S), BF16)],
        compiler_params=_cp(("parallel",)),
    )(proj, proj, pw, pscale)


def _pool_bwd(da, proj, pw, pscale, name):
    S = proj.shape[0]

    def body(da_ref, pv_ref, pg_ref, pw_ref, sc_ref, dpv_ref, dpg_ref, dpw_ref, dsc_ref):
        g = pl.program_id(0)
        pooled, count, rowi = _pool_pieces(pv_ref[...], g, S)
        pwb = pw_ref[...].astype(BF16)
        pm = jnp.dot(pooled.astype(BF16), pwb, preferred_element_type=F32)
        scv = sc_ref[...]
        pgv = pg_ref[...]
        sg = _sig(pgv)
        dav = da_ref[...]
        d_ps = dav * (pgv * sg)
        dpg_ref[...] = (dav * (pm * scv) * _dsilu(pgv, sg)).astype(BF16)
        dsc_ref[...] = jnp.sum(d_ps * pm, axis=0, keepdims=True)
        d_pm = (d_ps * scv).astype(BF16)
        dpw_ref[...] = lax.dot_general(pooled.astype(BF16), d_pm, (((0,), (0,)), ((), ())),
                                       preferred_element_type=F32)
        d_pooled = lax.dot_general(d_pm, pwb, (((1,), (1,)), ((), ())), preferred_element_type=F32)
        z = d_pooled / count

        def up(v, k):
            return jnp.where(rowi < S - k, pltpu.roll(v, S - k, axis=0), 0.0)

        t2 = z + up(z, 1)
        t4 = t2 + up(t2, 2)
        t8 = t4 + up(t4, 4)
        t16 = t8 + up(t8, 8)
        adj = jnp.where(g == 0, t2, jnp.where(g == 1, t4, jnp.where(g == 2, t8, t16)))
        dpv_ref[...] = (adj - d_pooled).astype(BF16)

    col = lambda g: (0, g)
    return pl.pallas_call(
        body, name=name, grid=(GROUPS,),
        in_specs=[pl.BlockSpec((S, 128), col), pl.BlockSpec((S, 128), lambda g: (0, PV0 + g)),
                  pl.BlockSpec((S, 128), lambda g: (0, PG0 + g)),
                  pl.BlockSpec((None, 128, 128), lambda g: (g, 0, 0)), pl.BlockSpec((1, 128), col)],
        out_specs=[pl.BlockSpec((S, 128), col), pl.BlockSpec((S, 128), col),
                   pl.BlockSpec((None, 128, 128), lambda g: (g, 0, 0)), pl.BlockSpec((1, 128), col)],
        out_shape=[jax.ShapeDtypeStruct((S, POOL_W), BF16), jax.ShapeDtypeStruct((S, POOL_W), BF16),
                   jax.ShapeDtypeStruct((GROUPS, 128, 128), F32), jax.ShapeDtypeStruct((1, POOL_W), F32)],
        compiler_params=_cp(("parallel",)),
    )(da, proj, proj, pw, pscale)


def _chunk_cumsum(z, rowi):
    for sh in (1, 2, 4, 8, 16, 32):
        z = z + jnp.where(rowi >= sh, pltpu.roll(z, sh, axis=0), 0.0)
    return z


def _chunk_rev_cumsum(z, rowi):
    for sh in (1, 2, 4, 8, 16, 32):
        z = z + jnp.where(rowi < CH - sh, pltpu.roll(z, CH - sh, axis=0), 0.0)
    return z


def _dot_nn(a, b):
    return jnp.dot(a.astype(BF16), b.astype(BF16), preferred_element_type=F32)


def _dot_nt(a, b):
    return lax.dot_general(a.astype(BF16), b.astype(BF16), (((1,), (1,)), ((), ())), preferred_element_type=F32)


def _dot_tn(a, b):
    return lax.dot_general(a.astype(BF16), b.astype(BF16), (((0,), (0,)), ((), ())), preferred_element_type=F32)


def _gates(hq, hf, lbv):
    sq = _sig(hq)
    sf = _sig(hf)
    f = lbv + (1.0 - lbv) * sf
    fc = jnp.maximum(f, 1e-30)
    return hq * sq, sq, sf, f, fc, jnp.log(fc)


DECAY_CAP = 60.0


def _block_ref(c_ref, i):
    if i == 0:
        return jnp.zeros((1, HD), F32)
    return c_ref[SB * i - 1:SB * i, :]


def _block_decay(c_ref):
    spans = [_block_ref(c_ref, i) - c_ref[SB * (i + 1) - 1:SB * (i + 1), :] for i in range(CH // SB)]
    return functools.reduce(jnp.maximum, spans)


def _hgrn_fwd(proj, lb, gn, name, carry=None):
    S = proj.shape[0]
    nch = S // CH
    W = NH * HD

    def body(hq_ref, hf_ref, hi_ref, hg_ref, lb_ref, gn_ref, bin_ref, bint_ref, oraw_ref, st_ref, mild_ref,
             q_s, k_s, c_s, v_s, o_s, state_s, qf_s, kf_s, cf_s):
        state_s[...] = jnp.zeros((NH, HD, HD), F32)
        rowi = lax.broadcasted_iota(jnp.int32, (CH, 1), 0)
        coli = lax.broadcasted_iota(jnp.int32, (1, CH), 1)
        sbi = lax.broadcasted_iota(jnp.int32, (SB, 1), 0)
        gnv = gn_ref[...]

        def gates_pass(n, worst):
            rows = pl.ds(pl.multiple_of(n * CH, CH), CH)
            for hh in range(NH):
                lanes = slice(hh * HD, (hh + 1) * HD)
                q, _, _, f, _, logf = _gates(hq_ref[rows, lanes], hf_ref[rows, lanes], lb_ref[:, lanes])
                c = _chunk_cumsum(logf, rowi)
                qf_s[hh, rows, :] = q
                kf_s[hh, rows, :] = 1.0 - f
                cf_s[hh, rows, :] = c
                c_s[hh] = c
                worst = jnp.maximum(worst, _block_decay(c_s.at[hh]))
            return worst

        def between_chunks(hh, n, rows):
            lanes = slice(hh * HD, (hh + 1) * HD)
            q = qf_s[hh, rows, :]
            k = kf_s[hh, rows, :]
            c = cf_s[hh, rows, :]
            v = hi_ref[rows, lanes]
            q_s[hh] = q
            k_s[hh] = k
            c_s[hh] = c
            v_s[hh] = v
            st = state_s[hh]
            st_ref[hh, n] = st.astype(BF16)
            o_s[hh] = _dot_nt(q * jnp.exp(c), st)
            last = c_s[hh, CH - 1:CH, :]
            state_s[hh] = st * jnp.exp(last) + _dot_tn(v, k * jnp.exp(last - c))

        def within_chunk_matmul(hh):
            q, k, c, v = q_s[hh], k_s[hh], c_s[hh], v_s[hh]
            a = jnp.zeros((CH, CH), F32)
            for i in range(CH // SB):
                r_i = _block_ref(c_s.at[hh], i)
                qi = q * jnp.exp(jnp.minimum(c - r_i, 0.0))
                kei = k * jnp.exp(jnp.minimum(r_i - c, DECAY_CAP))
                m_i = (rowi >= SB * i) & (rowi < SB * (i + 1)) & (coli <= rowi)
                a = a + jnp.where(m_i, _dot_nt(qi, kei), 0.0)
            o_s[hh] += _dot_nn(a, v)

        def within_chunk_exact(hh):
            q, k, c, v = q_s[hh], k_s[hh], c_s[hh], v_s[hh]
            a_off = jnp.zeros((CH, CH), F32)
            for i in range(1, CH // SB):
                r_i = _block_ref(c_s.at[hh], i)
                qi = q * jnp.exp(jnp.minimum(c - r_i, 0.0))
                kei = k * jnp.exp(jnp.minimum(r_i - c, 0.0))
                m_i = (rowi >= SB * i) & (rowi < SB * (i + 1)) & (coli < SB * i)
                a_off = a_off + jnp.where(m_i, _dot_nt(qi, kei), 0.0)
            o_s[hh] += _dot_nn(a_off, v)
            for i in range(CH // SB):
                blk = slice(SB * i, SB * (i + 1))
                qb = q_s[hh, blk, :]
                cb = c_s[hh, blk, :]
                acc = jnp.zeros((SB, HD), F32)
                for s in range(SB):
                    row = SB * i + s
                    w = jnp.exp(jnp.minimum(cb - c_s[hh, row:row + 1, :], 0.0))
                    a_col = jnp.sum(qb * k_s[hh, row:row + 1, :] * w, axis=-1, keepdims=True)
                    acc = acc + jnp.where(sbi >= s, a_col, 0.0) * v_s[hh, row:row + 1, :]
                o_s[hh, blk, :] += acc

        def norm_and_gate(hh, rows):
            lanes = slice(hh * HD, (hh + 1) * HD)
            ov = o_s[hh]
            oraw_ref[rows, lanes] = ov
            r = lax.rsqrt(jnp.mean(ov * ov, axis=-1, keepdims=True) + EPS)
            hg = hg_ref[rows, lanes]
            bin_ref[rows, lanes] = ((ov * r) * gnv * (hg * _sig(hg))).astype(BF16)

        def chunk_with(within_chunk):
            def chunk(n, carry):
                rows = pl.ds(pl.multiple_of(n * CH, CH), CH)
                for hh in range(NH):
                    between_chunks(hh, n, rows)
                for hh in range(NH):
                    within_chunk(hh)
                for hh in range(NH):
                    norm_and_gate(hh, rows)
                return carry
            return chunk

        worst = lax.fori_loop(0, nch, gates_pass, jnp.zeros((1, HD), F32))
        mild = jnp.max(worst) <= DECAY_CAP
        mild_ref[...] = jnp.broadcast_to(jnp.where(mild, 1.0, 0.0), (8, HD))

        @pl.when(mild)
        def _():
            lax.fori_loop(0, nch, chunk_with(within_chunk_matmul), 0, unroll=4)

        @pl.when(jnp.logical_not(mild))
        def _():
            lax.fori_loop(0, nch, chunk_with(within_chunk_exact), 0)

        bint_ref[...] = bin_ref[...].astype(F32).T.astype(BF16)

    col = lambda off: pl.BlockSpec((S, W), lambda h: (0, off // NH + h))
    head = pl.BlockSpec((S, W), lambda h: (0, h))
    outs = _call(
        body, name=name, grid=(HEADS // NH,),
        in_specs=[col(HQ0), col(HF0), col(HI0), col(HG0), pl.BlockSpec((1, W), lambda h: (0, h)),
                  pl.BlockSpec((1, HD), lambda h: (0, 0))],
        out_specs=[head, pl.BlockSpec((W, S), lambda h: (h, 0)), head,
                   pl.BlockSpec((NH, nch, HD, HD), lambda h: (h, 0, 0, 0)),
                   pl.BlockSpec((8, HD), lambda h: (h, 0))],
        out_shape=[jax.ShapeDtypeStruct((S, D), BF16), jax.ShapeDtypeStruct((D, S), BF16),
                   jax.ShapeDtypeStruct((S, D), F32), jax.ShapeDtypeStruct((HEADS, nch, HD, HD), BF16),
                   jax.ShapeDtypeStruct((8 * HEADS // NH, HD), F32)],
        scratch_shapes=[pltpu.VMEM((NH, CH, HD), F32)] * 5 + [pltpu.VMEM((NH, HD, HD), F32)]
        + [pltpu.VMEM((NH, S, HD), F32)] * 3,
        sem=("parallel",), args=(proj, proj, proj, proj, lb, gn), carry=carry)
    return outs[:5], outs[5:]


def _hgrn_bwd(dbin, proj, oraw, states, mild, lb, gn, name, carry=None):
    S = proj.shape[0]
    nch = S // CH
    W = NH * HD

    def body(db_ref, hq_ref, hf_ref, hi_ref, hg_ref, or_ref, st_ref, mild_ref, lb_ref, gn_ref,
             dq_ref, df_ref, di_ref, dg_ref, dlb_ref, dgn_ref,
             q_s, k_s, c_s, v_s, do_s, dq_s, dk_s, dv_s, dc_s, dqd_s, dkd_s, dl_s, dst_s, dlb_s, dgn_s):
        dst_s[...] = jnp.zeros((NH, HD, HD), F32)
        dlb_s[...] = jnp.zeros((1, W), F32)
        dgn_s[...] = jnp.zeros((1, HD), F32)
        rowi = lax.broadcasted_iota(jnp.int32, (CH, 1), 0)
        rowi2 = lax.broadcasted_iota(jnp.int32, (CH, CH), 0)
        coli2 = lax.broadcasted_iota(jnp.int32, (CH, CH), 1)
        sbi = lax.broadcasted_iota(jnp.int32, (SB, 1), 0)
        gnv = gn_ref[...]
        def between_chunks(hh, n, rows):
            lanes = slice(hh * HD, (hh + 1) * HD)
            q, _, _, f, _, logf = _gates(hq_ref[rows, lanes], hf_ref[rows, lanes], lb_ref[:, lanes])
            k = 1.0 - f
            v = hi_ref[rows, lanes]
            c = _chunk_cumsum(logf, rowi)
            ov = or_ref[rows, lanes]
            hg = hg_ref[rows, lanes]
            sg = _sig(hg)
            r = lax.rsqrt(jnp.mean(ov * ov, axis=-1, keepdims=True) + EPS)
            dbv = db_ref[rows, lanes]
            d_on = dbv * (hg * sg)
            dg_ref[rows, lanes] = (dbv * ((ov * r) * gnv) * _dsilu(hg, sg)).astype(BF16)
            dgn_s[...] += jnp.sum(d_on * (ov * r), axis=0, keepdims=True)
            u = d_on * gnv
            do = r * u - ov * (r * r * r) * jnp.mean(u * ov, axis=-1, keepdims=True)
            q_s[hh] = q
            k_s[hh] = k
            c_s[hh] = c
            v_s[hh] = v
            do_s[hh] = do
            st = st_ref[hh, n].astype(F32)
            dst = dst_s[hh]
            ec = jnp.exp(c)
            last = c_s[hh, CH - 1:CH, :]
            el = jnp.exp(last - c)
            elast = jnp.exp(last)
            dq = _dot_nn(do, st) * ec
            dk = _dot_nn(v, dst) * el
            dq_s[hh] = dq
            dk_s[hh] = dk
            dv_s[hh] = _dot_nt(k * el, dst)
            dc_s[hh] = q * dq - k * dk
            dl_s[hh] = (jnp.sum(k * dk, axis=0, keepdims=True)
                        + elast * jnp.sum(st * dst, axis=0, keepdims=True))
            dst_s[hh] = dst * elast + _dot_tn(do, q * ec)

        def pairs_matmul(hh, first, cap, strict):
            q, k, c, v, do = q_s[hh], k_s[hh], c_s[hh], v_s[hh], do_s[hh]
            d_a = _dot_nt(do, v).astype(BF16).astype(F32)
            d_at = d_a.T
            at = jnp.zeros((CH, CH), F32)
            dq, dk, dcum = dq_s[hh], dk_s[hh], dc_s[hh]
            for i in range(first, CH // SB):
                r_i = _block_ref(c_s.at[hh], i)
                eq = jnp.exp(jnp.minimum(c - r_i, 0.0))
                ek = jnp.exp(jnp.minimum(r_i - c, cap))
                qi = (q * eq).astype(BF16).astype(F32)
                kei = (k * ek).astype(BF16).astype(F32)
                in_t = (rowi2 >= SB * i) & (rowi2 < SB * (i + 1))
                in_s = (coli2 >= SB * i) & (coli2 < SB * (i + 1))
                m_ts = in_t & ((coli2 < SB * i) if strict else (coli2 <= rowi2))
                m_st = in_s & ((rowi2 < SB * i) if strict else (rowi2 <= coli2))
                at = at + jnp.where(m_st, _dot_nt(kei, qi), 0.0)
                dq_i = _dot_nn(jnp.where(m_ts, d_a, 0.0), kei)
                dk_i = _dot_nn(jnp.where(m_st, d_at, 0.0), qi)
                dq = dq + dq_i * eq
                dk = dk + dk_i * ek
                dcum = dcum + (qi * dq_i - kei * dk_i)
            dq_s[hh] = dq
            dk_s[hh] = dk
            dc_s[hh] = dcum
            dv_s[hh] += _dot_nn(at, do)

        def pairs_exact(hh):
            dqd_s[hh] = jnp.zeros((CH, HD), F32)
            dkd_s[hh] = jnp.zeros((CH, HD), F32)
            for i in range(CH // SB):
                blk = slice(SB * i, SB * (i + 1))
                qb = q_s[hh, blk, :]
                cb = c_s[hh, blk, :]
                dob = do_s[hh, blk, :]
                dq_acc = jnp.zeros((SB, HD), F32)
                for s in range(SB):
                    row = SB * i + s
                    ks = k_s[hh, row:row + 1, :]
                    vs = v_s[hh, row:row + 1, :]
                    w = jnp.exp(jnp.minimum(cb - c_s[hh, row:row + 1, :], 0.0))
                    live = sbi >= s
                    a_col = jnp.where(live, jnp.sum(qb * ks * w, axis=-1, keepdims=True), 0.0)
                    da_col = jnp.where(live, jnp.sum(dob * vs, axis=-1, keepdims=True), 0.0)
                    dq_acc = dq_acc + da_col * ks * w
                    dkd_s[hh, row:row + 1, :] += jnp.sum(da_col * qb * w, axis=0, keepdims=True)
                    dv_s[hh, row:row + 1, :] += jnp.sum(a_col * dob, axis=0, keepdims=True)
                dqd_s[hh, blk, :] += dq_acc
            dq_d = dqd_s[hh]
            dk_d = dkd_s[hh]
            dq_s[hh] += dq_d
            dk_s[hh] += dk_d
            dc_s[hh] += q_s[hh] * dq_d - k_s[hh] * dk_d

        def gate_grads(hh, rows):
            lanes = slice(hh * HD, (hh + 1) * HD)
            lbv = lb_ref[:, lanes]
            hq = hq_ref[rows, lanes]
            _, sq, sf, f, fc, _ = _gates(hq, hf_ref[rows, lanes], lbv)
            dlogf = _chunk_rev_cumsum(dc_s[hh], rowi) + dl_s[hh]
            dfv = jnp.where(f > 1e-30, dlogf / fc, 0.0) - dk_s[hh]
            dlb_s[:, lanes] += jnp.sum(dfv * (1.0 - sf), axis=0, keepdims=True)
            df_ref[rows, lanes] = (dfv * (1.0 - lbv) * sf * (1.0 - sf)).astype(BF16)
            dq_ref[rows, lanes] = (dq_s[hh] * _dsilu(hq, sq)).astype(BF16)
            di_ref[rows, lanes] = dv_s[hh].astype(BF16)

        def chunk_with(pairs):
            def chunk(j, carry):
                n = nch - 1 - j
                rows = pl.ds(pl.multiple_of(n * CH, CH), CH)
                for hh in range(NH):
                    between_chunks(hh, n, rows)
                for hh in range(NH):
                    pairs(hh)
                for hh in range(NH):
                    gate_grads(hh, rows)
                return carry
            return chunk

        def pairs_mild(hh):
            pairs_matmul(hh, 0, DECAY_CAP, strict=False)

        def pairs_any(hh):
            pairs_matmul(hh, 1, 0.0, strict=True)
            pairs_exact(hh)

        mild = jnp.max(mild_ref[...]) > 0.5

        @pl.when(mild)
        def _():
            lax.fori_loop(0, nch, chunk_with(pairs_mild), 0, unroll=2)

        @pl.when(jnp.logical_not(mild))
        def _():
            lax.fori_loop(0, nch, chunk_with(pairs_any), 0)

        dlb_ref[...] = dlb_s[...]
        dgn_ref[...] = jnp.broadcast_to(dgn_s[...], (8, HD))

    col = lambda off: pl.BlockSpec((S, W), lambda h: (0, off // NH + h))
    head = pl.BlockSpec((S, W), lambda h: (0, h))
    vec = pl.BlockSpec((1, W), lambda h: (0, h))
    outs = _call(
        body, name=name, grid=(HEADS // NH,),
        in_specs=[head, col(HQ0), col(HF0), col(HI0), col(HG0), head,
                  pl.BlockSpec((NH, nch, HD, HD), lambda h: (h, 0, 0, 0)),
                  pl.BlockSpec((8, HD), lambda h: (h, 0)), vec, pl.BlockSpec((1, HD), lambda h: (0, 0))],
        out_specs=[head, head, head, head, vec, pl.BlockSpec((8, HD), lambda h: (h, 0))],
        out_shape=[jax.ShapeDtypeStruct((S, D), BF16)] * 4
        + [jax.ShapeDtypeStruct((1, D), F32), jax.ShapeDtypeStruct((8 * HEADS // NH, HD), F32)],
        scratch_shapes=[pltpu.VMEM((NH, CH, HD), F32)] * 11
        + [pltpu.VMEM((NH, 1, HD), F32), pltpu.VMEM((NH, HD, HD), F32), pltpu.VMEM((1, W), F32),
           pltpu.VMEM((1, HD), F32)],
        sem=("parallel",), args=(dbin, proj, proj, proj, proj, oraw, states, mild, lb, gn), carry=carry)
    dq, df, di, dg, dlb, dgn = outs[:6]
    return (dq, df, di, dg, dlb, dgn.reshape(HEADS // NH, 8, HD)[:, 0, :]), outs[6:]


def _lower_bounds(l0, l1):
    m = jnp.maximum(l0, l1)
    e0 = jnp.exp(l0 - m)
    e1 = jnp.exp(l1 - m)
    tot = e0 + e1
    p0 = e0 / tot
    p1 = e1 / tot
    return jnp.clip(p0 - p0, 0.0, 1.0), jnp.clip((p0 + p1) - p0, 0.0, 1.0)


def _lb_fwd(logits):
    def body(l_ref, o_ref):
        lb0, lb1 = _lower_bounds(l_ref[0:1, :], l_ref[1:2, :])
        o_ref[0:1, :] = lb0
        o_ref[1:2, :] = lb1

    return pl.pallas_call(body, name="lb_fwd", out_shape=jax.ShapeDtypeStruct((2, D), F32))(logits)


def _lb_bwd(logits, dlb):
    def body(l_ref, d_ref, o_ref):
        _, vjp = jax.vjp(_lower_bounds, l_ref[0:1, :], l_ref[1:2, :])
        g0, g1 = vjp((d_ref[0:1, :], d_ref[1:2, :]))
        o_ref[0:1, :] = g0
        o_ref[1:2, :] = g1

    return pl.pallas_call(body, name="lb_bwd", out_shape=jax.ShapeDtypeStruct((2, D), F32))(logits, dlb)


ADA_PAD = 128


def _ada_fwd(c_pad, w_ada, b_sh):
    ns = w_ada.shape[2]

    def body(c_ref, w_ref, b_ref, o_ref):
        cv = c_ref[...]
        ca = (cv * _sig(cv)).astype(BF16)
        for l in range(2):
            res = jnp.dot(ca, w_ref[l].astype(BF16), preferred_element_type=F32)
            o_ref[:, l * ns:(l + 1) * ns] = res[0:NDEV, :] + b_ref[l:l + 1, :]

    return pl.pallas_call(body, name="ada_fwd", out_shape=jax.ShapeDtypeStruct((NDEV, 2 * ns), F32),
                          compiler_params=_cp())(c_pad, w_ada, b_sh)


def _ada_wgrad(c_pad_t, d_ada_sh):
    ns = d_ada_sh.shape[2]

    def body(c_ref, d_ref, o_ref):
        cv = c_ref[...]
        ca = (cv * _sig(cv)).astype(BF16)
        for l in range(2):
            o_ref[l] = jnp.dot(ca, d_ref[l].astype(BF16), preferred_element_type=F32)

    return pl.pallas_call(body, name="ada_wgrad", out_shape=jax.ShapeDtypeStruct((2, D, ns), F32),
                          compiler_params=_cp())(c_pad_t, d_ada_sh)


def _sum_devices(g):
    _, R, C = g.shape

    def body(g_ref, o_ref):
        acc = g_ref[0]
        for d in range(1, NDEV):
            acc = acc + g_ref[d]
        o_ref[...] = acc

    return pl.pallas_call(body, name="sum_devices", out_shape=jax.ShapeDtypeStruct((R, C), F32),
                          compiler_params=_cp())(g)


def _adamw(w, g, m, v, name, carry=None):
    R, C = w.shape
    tr = _row_tile(R, max(8, (1 << 19) // C))

    def body(w_ref, g_ref, m_ref, v_ref, d_ref, nm_ref, nv_ref):
        gv = g_ref[...]
        nm = B1 * m_ref[...] + (1.0 - B1) * gv
        nv = B2 * v_ref[...] + (1.0 - B2) * (gv * gv)
        m_hat = nm / (1.0 - B1 ** STEP)
        v_hat = nv / (1.0 - B2 ** STEP)
        d_ref[...] = -LR * (m_hat / (jnp.sqrt(v_hat) + AEPS) + WD * w_ref[...])
        nm_ref[...] = nm
        nv_ref[...] = nv

    tile = pl.BlockSpec((tr, C), lambda i: (i, 0))
    return _call(body, name=name, grid=(R // tr,), in_specs=[tile] * 4, out_specs=[tile] * 3,
                 out_shape=[jax.ShapeDtypeStruct((R, C), F32)] * 3, sem=("parallel",), args=(w, g, m, v),
                 carry=carry)


def _cast_to_slot(place, w, l, name):
    _, R, C = w.shape
    tr = _row_tile(R, max(8, (1 << 19) // C))

    def body(p_ref, w_ref, o_ref):
        o_ref[...] = w_ref[...].astype(BF16)

    return pl.pallas_call(
        body, name=name, out_shape=jax.ShapeDtypeStruct((NCHIP, R, C), BF16),
        grid_spec=pltpu.PrefetchScalarGridSpec(
            num_scalar_prefetch=1, grid=(R // tr,),
            in_specs=[pl.BlockSpec((None, tr, C), lambda i, p_ref: (l, i, 0))],
            out_specs=pl.BlockSpec((None, tr, C), lambda i, p_ref: (p_ref[0], i, 0))),
        compiler_params=_cp(("parallel",)),
    )(place, w)


def _pair_add(core, g, got, name):
    _, R, C = g.shape
    r2 = R // 2
    tr = _row_tile(r2, max(8, (1 << 19) // C))
    nt = r2 // tr

    def body(c_ref, a_ref, b_ref, o_ref):
        o_ref[...] = (a_ref[...].astype(F32) + b_ref[...].astype(F32)).astype(o_ref.dtype)

    return pl.pallas_call(
        body, name=name, out_shape=jax.ShapeDtypeStruct((NCHIP, r2, C), BF16),
        grid_spec=pltpu.PrefetchScalarGridSpec(
            num_scalar_prefetch=1, grid=(NCHIP, nt),
            in_specs=[pl.BlockSpec((None, tr, C), lambda j, i, c_ref: (j, c_ref[0] * nt + i, 0)),
                      pl.BlockSpec((None, tr, C), lambda j, i, c_ref: (j, i, 0))],
            out_specs=pl.BlockSpec((None, tr, C), lambda j, i, c_ref: (j, i, 0))),
        compiler_params=_cp(("parallel", "parallel")),
    )(core, g, got)


def _chip_sum(place, part, recv, layer, both, name):
    _, r2, C = part.shape
    tr = _row_tile(r2, max(8, (1 << 18) // C))
    nt = r2 // tr

    def body(p_ref, own_ref, r_ref, *rest):
        o_ref = rest[-1]
        me = p_ref[0]
        own = own_ref[...].astype(F32)
        acc = None
        for j in range(NCHIP):
            slot = jnp.minimum(jnp.where(j > me, j - 1, j), NCHIP - 2)
            term = jnp.where(me == j, own, r_ref[slot].astype(F32))
            acc = term if acc is None else acc + term
        o_ref[...] = acc

    args = (place, part, recv) if both is None else (place, part, recv, both)
    return pl.pallas_call(
        body, name=name, out_shape=jax.ShapeDtypeStruct((2, 2 * r2, C), F32),
        grid_spec=pltpu.PrefetchScalarGridSpec(
            num_scalar_prefetch=1, grid=(nt,),
            in_specs=[pl.BlockSpec((None, tr, C), lambda i, p_ref: (p_ref[0], i, 0)),
                      pl.BlockSpec((NCHIP - 1, tr, C), lambda i, p_ref: (0, i, 0))] + [ANY] * (len(args) - 3),
            out_specs=pl.BlockSpec((None, tr, C), lambda i, p_ref: (layer, p_ref[1] * nt + i, 0))),
        input_output_aliases={} if both is None else {3: 0},
        compiler_params=_cp(("parallel",)),
    )(*args)


def _place():
    x, y, c = lax.axis_index("x"), lax.axis_index("y"), lax.axis_index("c")
    chips = [(1 - x, y), (x, 1 - y), (1 - x, 1 - y)]
    return x, y, c, chips


def _gather_small(blk, name):
    m_per, n = blk.shape

    def body(x_ref, out_ref, send_sems, recv_sems, local_sem):
        x, y, c, chips = _place()
        me, sibling = (x, y, c), (x, y, 1 - c)

        def rows(px, py, pc):
            return out_ref.at[pl.ds((4 * px + 2 * py + pc) * m_per, m_per), :]

        def copy(k, block, to, src=None):
            return pltpu.make_async_remote_copy(
                src_ref=rows(*block) if src is None else src, dst_ref=rows(*block),
                send_sem=send_sems.at[k], recv_sem=recv_sems.at[k], device_id=to, device_id_type=MESH)

        mine = pltpu.make_async_copy(x_ref, rows(*me), local_sem)
        mine.start()
        first = [copy(0, me, sibling, src=x_ref)]
        first += [copy(1 + j, me, (*chip, c), src=x_ref) for j, chip in enumerate(chips)]
        for cp in first:
            cp.start()
        passed = [copy(4 + j, (*chip, c), sibling) for j, chip in enumerate(chips)]
        for j, chip in enumerate(chips):
            copy(1 + j, (*chip, c), me).wait_recv()
            passed[j].start()
        copy(0, sibling, me).wait_recv()
        for j, chip in enumerate(chips):
            copy(4 + j, (*chip, 1 - c), me).wait_recv()
        for cp in first + passed:
            cp.wait_send()
        mine.wait()

    return pl.pallas_call(
        body, name=name, out_shape=jax.ShapeDtypeStruct((NDEV * m_per, n), blk.dtype),
        in_specs=[pl.BlockSpec(memory_space=pltpu.VMEM)], out_specs=pl.BlockSpec(memory_space=pltpu.VMEM),
        scratch_shapes=[pltpu.SemaphoreType.DMA((7,)), pltpu.SemaphoreType.DMA((7,)), pltpu.SemaphoreType.DMA],
        compiler_params=_cp(),
    )(blk)


def _gather_rows_carry(blk):
    m_per, n = blk.shape

    def rows(ref, px, py, pc):
        return ref.at[pl.ds((4 * px + 2 * py + pc) * m_per, m_per), :]

    def copy(ins, outs, send_sems, recv_sems, k, block, to, own=False):
        return pltpu.make_async_remote_copy(
            src_ref=ins[0] if own else rows(outs[0], *block), dst_ref=rows(outs[0], *block),
            send_sem=send_sems.at[k], recv_sem=recv_sems.at[k], device_id=to, device_id_type=MESH)

    def mine(ins, outs, send_sems):
        x, y, c, _ = _place()
        return pltpu.make_async_copy(ins[0], rows(outs[0], x, y, c), send_sems.at[7])

    def start(ins, outs, send_sems, recv_sems):
        x, y, c, chips = _place()
        mine(ins, outs, send_sems).start()
        copy(ins, outs, send_sems, recv_sems, 0, (x, y, c), (x, y, 1 - c), own=True).start()
        for j, chip in enumerate(chips):
            copy(ins, outs, send_sems, recv_sems, 1 + j, (x, y, c), (*chip, c), own=True).start()

    def finish(ins, outs, send_sems, recv_sems):
        x, y, c, chips = _place()
        for j, chip in enumerate(chips):
            copy(ins, outs, send_sems, recv_sems, 1 + j, (*chip, c), (x, y, c)).wait_recv()
            copy(ins, outs, send_sems, recv_sems, 4 + j, (*chip, c), (x, y, 1 - c)).start()
        copy(ins, outs, send_sems, recv_sems, 0, (x, y, 1 - c), (x, y, c)).wait_recv()
        for j, chip in enumerate(chips):
            copy(ins, outs, send_sems, recv_sems, 4 + j, (*chip, 1 - c), (x, y, c)).wait_recv()
        copy(ins, outs, send_sems, recv_sems, 0, (x, y, c), (x, y, 1 - c), own=True).wait_send()
        for j, chip in enumerate(chips):
            copy(ins, outs, send_sems, recv_sems, 1 + j, (x, y, c), (*chip, c), own=True).wait_send()
            copy(ins, outs, send_sems, recv_sems, 4 + j, (*chip, c), (x, y, 1 - c)).wait_send()
        mine(ins, outs, send_sems).wait()

    return _Carry([blk], [jax.ShapeDtypeStruct((NDEV * m_per, n), blk.dtype)], {}, 8, start, finish)


def _gather_carry(shards):
    n = len(shards)

    def over_ici(outs, send_sems, recv_sems, a, j, chip_xy, slot):
        x, y, c, _ = _place()
        r2 = outs[a].shape[1] // 2
        blk = outs[a].at[slot, pl.ds(c * r2, r2), :]
        return pltpu.make_async_remote_copy(
            src_ref=blk, dst_ref=blk, send_sem=send_sems.at[6 * a + j], recv_sem=recv_sems.at[6 * a + j],
            device_id=(*chip_xy, c), device_id_type=MESH)

    def over_d2d(outs, send_sems, recv_sems, a, j, slot, half):
        x, y, c, _ = _place()
        r2 = outs[a].shape[1] // 2
        blk = outs[a].at[slot, pl.ds(half * r2, r2), :]
        return pltpu.make_async_remote_copy(
            src_ref=blk, dst_ref=blk, send_sem=send_sems.at[6 * a + 3 + j], recv_sem=recv_sems.at[6 * a + 3 + j],
            device_id=(x, y, 1 - c), device_id_type=MESH)

    def start(ins, outs, send_sems, recv_sems):
        x, y, c, chips = _place()
        for a in range(n):
            for j, chip_xy in enumerate(chips):
                over_ici(outs, send_sems, recv_sems, a, j, chip_xy, 2 * x + y).start()

    def finish(ins, outs, send_sems, recv_sems):
        x, y, c, chips = _place()
        for a in range(n):
            for j, (cx, cy) in enumerate(chips):
                over_ici(outs, send_sems, recv_sems, a, j, (cx, cy), 2 * cx + cy).wait_recv()
                over_d2d(outs, send_sems, recv_sems, a, j, 2 * cx + cy, c).start()
        for a in range(n):
            for j, (cx, cy) in enumerate(chips):
                over_d2d(outs, send_sems, recv_sems, a, j, 2 * cx + cy, 1 - c).wait_recv()
        for a in range(n):
            for j, (cx, cy) in enumerate(chips):
                over_ici(outs, send_sems, recv_sems, a, j, (cx, cy), 2 * x + y).wait_send()
                over_d2d(outs, send_sems, recv_sems, a, j, 2 * cx + cy, c).wait_send()

    return _Carry(shards, [jax.ShapeDtypeStruct(s.shape, s.dtype) for s in shards],
                  {a: a for a in range(n)}, 6 * n, start, finish)


def _rs_pair(grads, name):
    n = len(grads)

    def body(*refs):
        ins, gots = refs[:n], refs[n:2 * n]
        send_sems, recv_sems = refs[2 * n:]
        x, y, c, _ = _place()
        cps = []
        for a in range(n):
            r2 = ins[a].shape[1] // 2
            cp = pltpu.make_async_remote_copy(
                src_ref=ins[a].at[:, pl.ds((1 - c) * r2, r2), :], dst_ref=gots[a],
                send_sem=send_sems.at[a], recv_sem=recv_sems.at[a],
                device_id=(x, y, 1 - c), device_id_type=MESH)
            cp.start()
            cps.append(cp)
        for cp in cps:
            cp.wait()

    half = [jax.ShapeDtypeStruct((NCHIP, g.shape[1] // 2, g.shape[2]), g.dtype) for g in grads]
    return pl.pallas_call(
        body, name=name, out_shape=half, in_specs=[ANY] * n, out_specs=[ANY] * n,
        scratch_shapes=[pltpu.SemaphoreType.DMA((n,)), pltpu.SemaphoreType.DMA((n,))],
        compiler_params=_cp(),
    )(*grads)


def _chips_carry(parts):
    n = len(parts)

    def send(ins, outs, send_sems, recv_sems, a, j, chip_xy):
        x, y, c, _ = _place()
        me, them = 2 * x + y, 2 * chip_xy[0] + chip_xy[1]
        return pltpu.make_async_remote_copy(
            src_ref=ins[a].at[them], dst_ref=outs[a].at[me - (me > them).astype(jnp.int32)],
            send_sem=send_sems.at[3 * a + j], recv_sem=recv_sems.at[3 * a + j],
            device_id=(*chip_xy, c), device_id_type=MESH)

    def start(ins, outs, send_sems, recv_sems):
        _, _, _, chips = _place()
        for a in range(n):
            for j, chip_xy in enumerate(chips):
                send(ins, outs, send_sems, recv_sems, a, j, chip_xy).start()

    def finish(ins, outs, send_sems, recv_sems):
        x, y, c, chips = _place()
        me = 2 * x + y
        for a in range(n):
            for j, (cx, cy) in enumerate(chips):
                them = 2 * cx + cy
                blk = outs[a].at[them - (them > me).astype(jnp.int32)]
                pltpu.make_async_remote_copy(
                    src_ref=blk, dst_ref=blk, send_sem=send_sems.at[3 * a + j], recv_sem=recv_sems.at[3 * a + j],
                    device_id=(cx, cy, c), device_id_type=MESH).wait_recv()
        for a in range(n):
            for j, chip_xy in enumerate(chips):
                send(ins, outs, send_sems, recv_sems, a, j, chip_xy).wait_send()

    return _Carry(parts, [jax.ShapeDtypeStruct((NCHIP - 1,) + p.shape[1:], p.dtype) for p in parts], {},
                  3 * n, start, finish)


def _rs_swap(fulls):
    n = len(fulls)

    def body(*refs):
        outs = refs[n:2 * n]
        send_sems, recv_sems = refs[2 * n:]
        x, y, c, _ = _place()
        cps = []
        for a in range(n):
            r2 = outs[a].shape[1] // 2
            mine = outs[a].at[:, pl.ds(c * r2, r2), :]
            cp = pltpu.make_async_remote_copy(
                src_ref=mine, dst_ref=mine, send_sem=send_sems.at[a], recv_sem=recv_sems.at[a],
                device_id=(x, y, 1 - c), device_id_type=MESH)
            cp.start()
            cps.append(cp)
        for a in range(n):
            r2 = outs[a].shape[1] // 2
            blk = outs[a].at[:, pl.ds((1 - c) * r2, r2), :]
            pltpu.make_async_remote_copy(
                src_ref=blk, dst_ref=blk, send_sem=send_sems.at[a], recv_sem=recv_sems.at[a],
                device_id=(x, y, 1 - c), device_id_type=MESH).wait_recv()
        for cp in cps:
            cp.wait_send()

    return pl.pallas_call(
        body, name="rs_swap", out_shape=[jax.ShapeDtypeStruct(f.shape, f.dtype) for f in fulls],
        in_specs=[ANY] * n, out_specs=[ANY] * n, input_output_aliases={a: a for a in range(n)},
        scratch_shapes=[pltpu.SemaphoreType.DMA((n,)), pltpu.SemaphoreType.DMA((n,))],
        compiler_params=_cp(),
    )(*fulls)


def _mm_ride(a, b, carry, **kw):
    if carry is None:
        return _mm(a, b, **kw), []
    return _mm(a, b, carry=carry, **kw)


def _layer_fwd(l, x, ada, w, small, ride):
    shift, scale, gate = ada[:, 0:D], ada[:, D:2 * D], ada[:, 2 * D:3 * D]
    h, h_t = _prenorm_fwd(x, small["g_pre"][l], scale, shift, f"prenorm_fwd{l}")
    proj, landed = _mm_ride(h, w["w_in"][l], ride["proj"][0], name=f"proj{l}", b_mode="nn_sh", tm=2048)
    ride["proj"][1](landed)
    a_in, a_in_t = _pool_fwd(proj, small["pool_w"][l], small["pool_scale"][l], f"pool_fwd{l}")
    (b_in, b_in_t, o_raw, states, mild), landed = _hgrn_fwd(proj, small["lb"][l], small["hgrn_norm_g"][l],
                                                           f"hgrn_fwd{l}", carry=ride["hgrn"][0])
    ride["hgrn"][1](landed)
    br_a = _mm(a_in, w["w_pool_o"][l], name=f"branch_a{l}", b_mode="nn_sh")
    br_b = _mm(b_in, w["w_hgrn_o"][l].reshape(D, D), name=f"branch_b{l}")
    merged, merged_t = _merge_fwd(proj, br_a, br_b, f"merge_fwd{l}")
    y = _mm(merged, w["w_out"][l].reshape(D, D), name=f"out_proj{l}")
    x_new = _postnorm_fwd(x, y, gate, small["g_post"][l], f"postnorm_fwd{l}")
    saved = dict(x=x, h_t=h_t, proj=proj, a_in_t=a_in_t, b_in_t=b_in_t, o_raw=o_raw, states=states, mild=mild,
                 br_a=br_a, br_b=br_b, merged_t=merged_t, y=y, scale=scale, gate=gate)
    return x_new, saved


def _layer_bwd(l, dxn, sv, w, small, ride):
    dy, dgate, dg_post = _postnorm_bwd(dxn, sv["y"], sv["gate"], small["g_post"][l], f"postnorm_bwd{l}")
    w_out = w["w_out"][l].reshape(D, D)
    dmerged = _mm(dy, w_out, name=f"d_merged{l}", b_mode="nt")
    gw_out = _mm(sv["merged_t"], dy, name=f"gw_out{l}", out_dtype=BF16)
    dbr_a, dbr_b, dmg = _merge_bwd(dmerged, sv["proj"], sv["br_a"], sv["br_b"], f"merge_bwd{l}")
    da_in = _mm(dbr_a, w["w_pool_o"][l], name=f"d_a_in{l}", b_mode="nt_shk", tn=512)
    gw_pool_o = _mm(sv["a_in_t"], dbr_a, name=f"gw_pool_o{l}", out_shards=NCHIP, out_dtype=BF16)
    db_in = _mm(dbr_b, w["w_hgrn_o"][l].reshape(D, D), name=f"d_b_in{l}", b_mode="nt")
    gw_hgrn_o = _mm(sv["b_in_t"], dbr_b, name=f"gw_hgrn_o{l}", out_dtype=BF16)
    big = dict(w_pool_o=gw_pool_o, w_hgrn_o=gw_hgrn_o.reshape(NCHIP, D // NCHIP, D),
               w_out=gw_out.reshape(NCHIP, D // NCHIP, D))
    carry, landed = ride["hgrn"](big)
    (dhq, dhf, dhi, dhg, dlb, dgn), outs = _hgrn_bwd(db_in, sv["proj"], sv["o_raw"], sv["states"], sv["mild"],
                                                     small["lb"][l], small["hgrn_norm_g"][l], f"hgrn_bwd{l}",
                                                     carry=carry)
    landed(outs)
    dpv, dpg, dpw, dpsc = _pool_bwd(da_in, sv["proj"], small["pool_w"][l], small["pool_scale"][l],
                                    f"pool_bwd{l}")
    dproj = jnp.concatenate([dpv, dpg, dhq, dhf, dhi, dhg, dmg], axis=1)
    big["w_in"] = _mm(sv["h_t"], dproj, name=f"gw_in{l}", out_shards=NCHIP, out_dtype=BF16)
    carry, landed = ride["d_h"](big)
    dh, outs = _mm_ride(dproj, w["w_in"][l], carry, name=f"d_h{l}", b_mode="nt_shk", tn=1024)
    landed(outs)
    dx, dshift, dscale, dg_pre = _prenorm_bwd(dh, dxn, sv["x"], small["g_pre"][l], sv["scale"],
                                              f"prenorm_bwd{l}")
    little = dict(d_ada=jnp.concatenate([dshift, dscale, dgate], axis=1), g_pre=dg_pre, g_post=dg_post,
                  pool_w=dpw, pool_scale=dpsc, lb=dlb, hgrn_norm_g=jnp.sum(dgn, axis=0, keepdims=True))
    return dx, big, little


SMALL_ROWS = 176


def _rows8(t):
    t = t.reshape(-1, D)
    return jnp.pad(t, ((0, -t.shape[0] % 8), (0, 0)))


def _pack_small_weights(b_ada, g_pre, g_post, lb_logits, pool_w, pool_scale, hgrn_norm_g):
    gn = jnp.pad(hgrn_norm_g.reshape(1, 2 * HD), ((0, 0), (0, D - 2 * HD)))
    return jnp.concatenate([_rows8(b_ada), _rows8(g_pre), _rows8(g_post), _rows8(lb_logits), _rows8(pool_w),
                            _rows8(pool_scale), _rows8(gn)], axis=0)


def _pack_small(parts):
    both = lambda key: jnp.stack([parts[l][key] for l in range(2)])
    return _pack_small_weights(both("d_ada"), both("g_pre"), both("g_post"), both("lb"), both("pool_w"),
                               both("pool_scale"), both("hgrn_norm_g"))


def _unpack_small(p):
    return (p[0:6].reshape(2, 3 * D), p[8:10], p[16:18], p[24:26], p[32:160].reshape(2, GROUPS, 128, 128),
            p[160:161].reshape(2, POOL_W), p[168:169, 0:2 * HD].reshape(2, HD))


def kernel(x, c, w_ada, b_ada, g_pre, g_post, w_in, pool_w, pool_scale, lb_logits, hgrn_norm_g, w_pool_o, w_hgrn_o, w_out, loss_target, m_w_ada, m_b_ada, m_g_pre, m_g_post, m_w_in, m_pool_w, m_pool_scale, m_lb_logits, m_hgrn_norm_g, m_w_pool_o, m_w_hgrn_o, m_w_out, v_w_ada, v_b_ada, v_g_pre, v_g_post, v_w_in, v_pool_w, v_pool_scale, v_lb_logits, v_hgrn_norm_g, v_w_pool_o, v_w_hgrn_o, v_w_out):
    ax, ay, ac = lax.axis_index("x"), lax.axis_index("y"), lax.axis_index("c")
    chip = 2 * ax + ay
    dev = 2 * chip + ac
    xe, te = x[0], loss_target[0]
    ada_s = w_ada.shape[2]

    big_names = ("w_in", "w_pool_o", "w_hgrn_o", "w_out")
    big_w = (w_in, w_pool_o, w_hgrn_o, w_out)
    core = jnp.stack([ac]).astype(jnp.int32)
    place = jnp.stack([chip, ac]).astype(jnp.int32)
    slots = {(k, l): _cast_to_slot(place, t, l, f"cast_{k}{l}") for l in range(2) for k, t in zip(big_names, big_w)}
    w = {k: [None, None] for k in big_names}
    (w["w_in"][0],) = _run_carry(_gather_carry([slots["w_in", 0]]), "gather_w_in0")
    later = [(k, l) for l in range(2) for k in big_names[1:]]

    def landed_later(outs):
        for (k, l), o in zip(later, outs):
            w[k][l] = o

    def landed_w_in1(outs):
        (w["w_in"][1],) = outs

    ride_fwd0 = dict(proj=(_gather_carry([slots[t] for t in later]), landed_later),
                     hgrn=(_gather_carry([slots["w_in", 1]]), landed_w_in1))
    no_carry = (None, lambda outs: None)

    c_all = _gather_small(jnp.broadcast_to(c, (8, D)), "gather_c").reshape(NDEV, 8, D)[:, 0, :]
    c_pad = jnp.pad(c_all, ((0, ADA_PAD - NDEV), (0, 0)))
    b_sh = lax.dynamic_slice(b_ada, (0, chip * ada_s), (2, ada_s))
    ada_cols = _gather_small(_ada_fwd(c_pad, w_ada, b_sh), "gather_ada")
    ada_cols = ada_cols.reshape(NCHIP, 2, NDEV, 2, ada_s)[:, 0]
    ada_all = jnp.transpose(ada_cols, (2, 1, 0, 3)).reshape(2, NDEV, 3 * D)
    ada_me = lax.dynamic_slice(ada_all, (0, dev, 0), (2, 1, 3 * D))

    lbs = _lb_fwd(lb_logits)
    small = dict(g_pre=g_pre[:, None, :], g_post=g_post[:, None, :], pool_w=pool_w,
                 pool_scale=pool_scale[:, None, :], lb=lbs[:, None, :], hgrn_norm_g=hgrn_norm_g[:, None, :])

    x1, sv0 = _layer_fwd(0, xe, ada_me[0], w, small, ride_fwd0)
    x2, sv1 = _layer_fwd(1, x1, ada_me[1], w, small, dict(proj=no_carry, hgrn=no_carry))
    dx2, loss_blk = _loss_head(x2, te, "loss_head")

    parts, recv = {}, {}

    def pair_sums(keys, grads, tag):
        got = _rs_pair(grads, f"rs_pair_{tag}")
        for kl, g, o in zip(keys, grads, got):
            parts[kl] = _pair_add(core, g, o, f"rs_add_{kl[0]}{kl[1]}")

    def exchange(keys):
        def landed(outs):
            recv.update(zip(keys, outs))
        return _chips_carry([parts[kl] for kl in keys]), landed

    def early(l):
        return [(k, l) for k in big_names[1:]]

    def ride_hgrn1(big):
        pair_sums(early(1), [big[k] for k in big_names[1:]], "l1_early")
        return exchange(early(1))

    def ride_d_h1(big):
        pair_sums([("w_in", 1)], [big["w_in"]], "l1_w_in")
        return no_carry

    def ride_hgrn0(big):
        pair_sums(early(0), [big[k] for k in big_names[1:]], "l0_early")
        return exchange([("w_in", 1)] + early(0))

    def ride_d_h0(big):
        pair_sums([("w_in", 0)], [big["w_in"]], "l0_w_in")
        return exchange([("w_in", 0)])

    dx1, big1, little1 = _layer_bwd(1, dx2, sv1, w, small, dict(hgrn=ride_hgrn1, d_h=ride_d_h1))
    dx0, big0, little0 = _layer_bwd(0, dx1, sv0, w, small, dict(hgrn=ride_hgrn0, d_h=ride_d_h0))
    loss = lax.psum(loss_blk[0, 0], ("x", "y", "c"))
    red = []
    for k in big_names:
        both = _chip_sum(place, parts[k, 1], recv[k, 1], 1, None, f"rs_sum_{k}1")
        red.append(_chip_sum(place, parts[k, 0], recv[k, 0], 0, both, f"rs_sum_{k}0"))
    g_big = dict(zip(big_names, _rs_swap(red)))

    def upd(wt, g, m, v, name, carry=None):
        shp = wt.shape
        two = lambda t: t.reshape(-1, shp[-1])
        res = _adamw(two(wt), two(g), two(m), two(v), name, carry)
        return [t.reshape(shp) for t in res[:3]], res[3:]

    u_w_in, (packed,) = upd(w_in, g_big["w_in"], m_w_in, v_w_in, "adamw_w_in",
                            _gather_rows_carry(_pack_small([little0, little1])))
    packed = packed.reshape(NDEV, SMALL_ROWS, D)
    g_small = _sum_devices(packed)
    g_b_ada, g_g_pre, g_g_post, g_lb, g_pool_w, g_pool_scale, g_norm_g = _unpack_small(g_small)
    g_lb_logits = _lb_bwd(lb_logits, g_lb)
    d_ada_all = packed[:, 0:6, :].reshape(NDEV, 2, 3 * D)
    d_ada_sh = lax.dynamic_slice(jnp.transpose(d_ada_all, (1, 0, 2)), (0, 0, chip * ada_s), (2, NDEV, ada_s))
    d_ada_sh = jnp.pad(d_ada_sh, ((0, 0), (0, ADA_PAD - NDEV), (0, 0)))
    g_w_ada = _ada_wgrad(c_pad.T, d_ada_sh)

    u_w_ada, _ = upd(w_ada, g_w_ada, m_w_ada, v_w_ada, "adamw_w_ada")
    u_w_pool_o, _ = upd(w_pool_o, g_big["w_pool_o"], m_w_pool_o, v_w_pool_o, "adamw_w_pool_o")
    u_w_hgrn_o, _ = upd(w_hgrn_o, g_big["w_hgrn_o"], m_w_hgrn_o, v_w_hgrn_o, "adamw_w_hgrn_o")
    u_w_out, _ = upd(w_out, g_big["w_out"], m_w_out, v_w_out, "adamw_w_out")
    g_small_fixed = _pack_small_weights(g_b_ada, g_g_pre, g_g_post, g_lb_logits, g_pool_w, g_pool_scale,
                                        g_norm_g)
    sw = _pack_small_weights(b_ada, g_pre, g_post, lb_logits, pool_w, pool_scale, hgrn_norm_g)
    sm = _pack_small_weights(m_b_ada, m_g_pre, m_g_post, m_lb_logits, m_pool_w, m_pool_scale, m_hgrn_norm_g)
    sv = _pack_small_weights(v_b_ada, v_g_pre, v_g_post, v_lb_logits, v_pool_w, v_pool_scale, v_hgrn_norm_g)
    u_small = [_unpack_small(t) for t in _adamw(sw, g_small_fixed, sm, sv, "adamw_small")]

    grads_out = (g_w_ada, g_b_ada, g_g_pre, g_g_post, g_big["w_in"], g_pool_w, g_pool_scale, g_lb_logits,
                 g_norm_g, g_big["w_pool_o"], g_big["w_hgrn_o"], g_big["w_out"])

    def ordered(k):
        s = u_small[k]
        return (u_w_ada[k], s[0], s[1], s[2], u_w_in[k], s[4], s[5], s[3], s[6], u_w_pool_o[k], u_w_hgrn_o[k],
                u_w_out[k])

    return (loss, dx0[None], *grads_out, *ordered(0), *ordered(1), *ordered(2))
```

```python
import functools

import jax
import jax.numpy as jnp
from jax import lax
from jax.experimental import pallas as pl
from jax.experimental.pallas import tpu as pltpu

F32 = jnp.float32
BF16 = jnp.bfloat16
MESH = pl.DeviceIdType.MESH

D = 1024
HEADS = 8
HD = 128
GROUPS = 4
POOL_W = 512
WINDOWS = (2, 4, 8, 16)
CH = 64
SB = 16
NH = 2
IN_W = 7168
NCHIP = 4
NDEV = 8
EPS = 1e-6
PV0, PG0, HQ0, HF0, HI0, HG0 = 0, 4, 8, 16, 24, 32
MGP_BLK, MGH_BLK = 5, 6

LR, B1, B2, AEPS, WD, STEP = 0.001, 0.9, 0.999, 1e-08, 0.01, 10
VMEM_LIMIT = 56 * 1024 * 1024


def _cp(sem=None, **kw):
    if sem is not None:
        kw["dimension_semantics"] = sem
    return pltpu.CompilerParams(vmem_limit_bytes=VMEM_LIMIT, **kw)


def _sig(z):
    return 1.0 / (1.0 + jnp.exp(-z))


def _dsilu(z, s):
    return s * (1.0 + z * (1.0 - s))


def _row_tile(rows, cap):
    if rows <= cap:
        return rows
    t = 1 << (cap.bit_length() - 1)
    while rows % t:
        t //= 2
    return t


ANY = pl.BlockSpec(memory_space=pl.ANY)


class _Carry:
    def __init__(self, ins, outs, aliases, n_sem, start, finish):
        self.ins, self.outs, self.aliases, self.n_sem = list(ins), list(outs), dict(aliases), n_sem
        self.start, self.finish = start, finish


def _call(body, *, name, grid, in_specs, out_specs, out_shape, args, scratch_shapes=(), sem=None, carry=None):
    in_specs, out_specs, out_shape = list(in_specs), list(out_specs), list(out_shape)
    scratch_shapes = list(scratch_shapes)
    if carry is None:
        outs = pl.pallas_call(body, name=name, grid=grid, in_specs=in_specs, out_specs=out_specs,
                              out_shape=out_shape, scratch_shapes=scratch_shapes,
                              compiler_params=_cp(sem))(*args)
        return list(outs)
    n_in, n_out, n_scr = len(in_specs), len(out_specs), len(scratch_shapes)
    c_in, c_out = len(carry.ins), len(carry.outs)

    def wrapped(*refs):
        k_in, rest = refs[:n_in], refs[n_in:]
        ci, rest = rest[:c_in], rest[c_in:]
        k_out, rest = rest[:n_out], rest[n_out:]
        co, rest = rest[:c_out], rest[c_out:]
        k_scr, (ssem, rsem) = rest[:n_scr], rest[n_scr:]
        pids = [pl.program_id(d) for d in range(len(grid))]
        first = functools.reduce(jnp.logical_and, [p == 0 for p in pids])
        last = functools.reduce(jnp.logical_and, [p == g - 1 for p, g in zip(pids, grid)])

        @pl.when(first)
        def _():
            carry.start(ci, co, ssem, rsem)

        body(*k_in, *k_out, *k_scr)

        @pl.when(last)
        def _():
            carry.finish(ci, co, ssem, rsem)

    outs = pl.pallas_call(
        wrapped, name=name, grid=grid, in_specs=in_specs + [ANY] * c_in, out_specs=out_specs + [ANY] * c_out,
        out_shape=out_shape + carry.outs,
        input_output_aliases={n_in + i: n_out + o for i, o in carry.aliases.items()},
        scratch_shapes=scratch_shapes + [pltpu.SemaphoreType.DMA((carry.n_sem,))] * 2,
        compiler_params=_cp(("arbitrary",) * len(grid)),
    )(*args, *carry.ins)
    return list(outs)


def _run_carry(carry, name):
    c_in, c_out = len(carry.ins), len(carry.outs)

    def body(*refs):
        ci, co, (ssem, rsem) = refs[:c_in], refs[c_in:c_in + c_out], refs[c_in + c_out:]
        carry.start(ci, co, ssem, rsem)
        carry.finish(ci, co, ssem, rsem)

    outs = pl.pallas_call(
        body, name=name, in_specs=[ANY] * c_in, out_specs=[ANY] * c_out, out_shape=carry.outs,
        input_output_aliases=carry.aliases,
        scratch_shapes=[pltpu.SemaphoreType.DMA((carry.n_sem,))] * 2, compiler_params=_cp(),
    )(*carry.ins)
    return list(outs)


def _mm(a, b, *, name, b_mode="nn", out_shards=0, tm=1024, tn=256, tk=None, out_dtype=F32, carry=None):
    M, K = a.shape
    if b_mode == "nn":
        N = b.shape[1]
    elif b_mode == "nt":
        N = b.shape[0]
    elif b_mode == "nn_sh":
        N = b.shape[0] * b.shape[2]
    else:
        N = b.shape[1]
    tm = _row_tile(M, tm)
    if b_mode == "nn_sh":
        tn = _row_tile(b.shape[2], tn)
    elif out_shards:
        tn = _row_tile(N // out_shards, tn)
    else:
        tn = _row_tile(N, tn)
    if tk is None:
        tk = K if b_mode != "nt_shk" else b.shape[2]
    if b_mode == "nt_shk":
        tk = _row_tile(b.shape[2], tk)
    nm, nn, nk = M // tm, N // tn, K // tk

    a_spec = pl.BlockSpec((tm, tk), lambda m, n, k: (m, k))
    if b_mode == "nn":
        b_spec = pl.BlockSpec((tk, tn), lambda m, n, k: (k, n))
    elif b_mode == "nt":
        b_spec = pl.BlockSpec((tn, tk), lambda m, n, k: (n, k))
    elif b_mode == "nn_sh":
        nps = b.shape[2] // tn
        b_spec = pl.BlockSpec((None, tk, tn), lambda m, n, k: (n // nps, k, n % nps))
    else:
        kps = b.shape[2] // tk
        b_spec = pl.BlockSpec((None, tn, tk), lambda m, n, k: (k // kps, n, k % kps))
    if out_shards:
        ops = (N // out_shards) // tn
        o_spec = pl.BlockSpec((None, tm, tn), lambda m, n, k: (n // ops, m, n % ops))
        o_shape = jax.ShapeDtypeStruct((out_shards, M, N // out_shards), out_dtype)
    else:
        o_spec = pl.BlockSpec((tm, tn), lambda m, n, k: (m, n))
        o_shape = jax.ShapeDtypeStruct((M, N), out_dtype)
    trans_b = b_mode in ("nt", "nt_shk")
    dn = (((1,), (1,)), ((), ())) if trans_b else (((1,), (0,)), ((), ()))

    def body(a_ref, b_ref, o_ref, acc_ref):
        k = pl.program_id(2)

        @pl.when(k == 0)
        def _():
            acc_ref[...] = jnp.zeros(acc_ref.shape, F32)

        acc_ref[...] += lax.dot_general(a_ref[...].astype(BF16), b_ref[...].astype(BF16), dn,
                                        preferred_element_type=F32)

        @pl.when(k == nk - 1)
        def _():
            o_ref[...] = acc_ref[...].astype(o_ref.dtype)

    outs = _call(body, name=name, grid=(nm, nn, nk), in_specs=[a_spec, b_spec], out_specs=[o_spec],
                 out_shape=[o_shape], scratch_shapes=[pltpu.VMEM((tm, tn), F32)],
                 sem=("parallel", "parallel", "arbitrary"), args=(a, b), carry=carry)
    return outs[0] if carry is None else (outs[0], outs[1:])


def _rowvec(n=D):
    return pl.BlockSpec((1, n), lambda i: (0, 0))


def _prenorm_fwd(x, g, scale, shift, name):
    S = x.shape[0]
    tr = _row_tile(S, 256)

    def body(x_ref, g_ref, sc_ref, sh_ref, h_ref, ht_ref):
        xv = x_ref[...]
        r = lax.rsqrt(jnp.mean(xv * xv, axis=-1, keepdims=True) + EPS)
        hv = (xv * r) * g_ref[...] * (1.0 + sc_ref[...]) + sh_ref[...]
        h_ref[...] = hv.astype(BF16)
        ht_ref[...] = hv.T.astype(BF16)

    return pl.pallas_call(
        body, name=name, grid=(S // tr,),
        in_specs=[pl.BlockSpec((tr, D), lambda i: (i, 0)), _rowvec(), _rowvec(), _rowvec()],
        out_specs=[pl.BlockSpec((tr, D), lambda i: (i, 0)), pl.BlockSpec((D, tr), lambda i: (0, i))],
        out_shape=[jax.ShapeDtypeStruct((S, D), BF16), jax.ShapeDtypeStruct((D, S), BF16)],
        compiler_params=_cp(("parallel",)),
    )(x, g, scale, shift)


def _prenorm_bwd(dh, dxn, x, g, scale, name):
    S = x.shape[0]
    tr = _row_tile(S, 256)

    def body(dh_ref, dxn_ref, x_ref, g_ref, sc_ref, dx_ref, dsh_ref, dsc_ref, dg_ref):
        i = pl.program_id(0)

        @pl.when(i == 0)
        def _():
            dsh_ref[...] = jnp.zeros((1, D), F32)
            dsc_ref[...] = jnp.zeros((1, D), F32)
            dg_ref[...] = jnp.zeros((1, D), F32)

        xv = x_ref[...]
        dhv = dh_ref[...]
        gv = g_ref[...]
        mod = 1.0 + sc_ref[...]
        r = lax.rsqrt(jnp.mean(xv * xv, axis=-1, keepdims=True) + EPS)
        xh = xv * r
        dsh_ref[...] += jnp.sum(dhv, axis=0, keepdims=True)
        dsc_ref[...] += jnp.sum(dhv * (xh * gv), axis=0, keepdims=True)
        dg_ref[...] += jnp.sum(dhv * mod * xh, axis=0, keepdims=True)
        u = dhv * mod * gv
        dx_ref[...] = dxn_ref[...] + r * u - xv * (r * r * r) * jnp.mean(u * xv, axis=-1, keepdims=True)

    tile = pl.BlockSpec((tr, D), lambda i: (i, 0))
    return pl.pallas_call(
        body, name=name, grid=(S // tr,),
        in_specs=[tile, tile, tile, _rowvec(), _rowvec()],
        out_specs=[tile, _rowvec(), _rowvec(), _rowvec()],
        out_shape=[jax.ShapeDtypeStruct((S, D), F32)] + [jax.ShapeDtypeStruct((1, D), F32)] * 3,
        compiler_params=_cp(("arbitrary",)),
    )(dh, dxn, x, g, scale)


def _postnorm_fwd(x, y, gate, g, name):
    S = x.shape[0]
    tr = _row_tile(S, 256)

    def body(x_ref, y_ref, gate_ref, g_ref, o_ref):
        yv = y_ref[...]
        r = lax.rsqrt(jnp.mean(yv * yv, axis=-1, keepdims=True) + EPS)
        o_ref[...] = x_ref[...] + gate_ref[...] * ((yv * r) * g_ref[...])

    tile = pl.BlockSpec((tr, D), lambda i: (i, 0))
    return pl.pallas_call(
        body, name=name, grid=(S // tr,), in_specs=[tile, tile, _rowvec(), _rowvec()],
        out_specs=tile, out_shape=jax.ShapeDtypeStruct((S, D), F32), compiler_params=_cp(("parallel",)),
    )(x, y, gate, g)


def _postnorm_bwd(dxn, y, gate, g, name):
    S = y.shape[0]
    tr = _row_tile(S, 256)

    def body(dxn_ref, y_ref, gate_ref, g_ref, dy_ref, dgate_ref, dg_ref):
        i = pl.program_id(0)

        @pl.when(i == 0)
        def _():
            dgate_ref[...] = jnp.zeros((1, D), F32)
            dg_ref[...] = jnp.zeros((1, D), F32)

        yv = y_ref[...]
        dv = dxn_ref[...]
        gv = g_ref[...]
        gt = gate_ref[...]
        r = lax.rsqrt(jnp.mean(yv * yv, axis=-1, keepdims=True) + EPS)
        yh = yv * r
        dgate_ref[...] += jnp.sum(dv * (yh * gv), axis=0, keepdims=True)
        dg_ref[...] += jnp.sum(dv * gt * yh, axis=0, keepdims=True)
        u = dv * gt * gv
        dy_ref[...] = (r * u - yv * (r * r * r) * jnp.mean(u * yv, axis=-1, keepdims=True)).astype(BF16)

    tile = pl.BlockSpec((tr, D), lambda i: (i, 0))
    return pl.pallas_call(
        body, name=name, grid=(S // tr,), in_specs=[tile, tile, _rowvec(), _rowvec()],
        out_specs=[tile, _rowvec(), _rowvec()],
        out_shape=[jax.ShapeDtypeStruct((S, D), BF16), jax.ShapeDtypeStruct((1, D), F32),
                   jax.ShapeDtypeStruct((1, D), F32)],
        compiler_params=_cp(("arbitrary",)),
    )(dxn, y, gate, g)


def _loss_head(xo, target, name):
    S = xo.shape[0]
    tr = _row_tile(S, 256)

    def body(x_ref, t_ref, dx_ref, l_ref):
        i = pl.program_id(0)

        @pl.when(i == 0)
        def _():
            l_ref[...] = jnp.zeros((8, 128), F32)

        err = x_ref[...] - t_ref[...]
        dx_ref[...] = err * (1.0 / D)
        l_ref[...] += 0.5 * jnp.sum(jnp.mean(err * err, axis=-1, keepdims=True))

    tile = pl.BlockSpec((tr, D), lambda i: (i, 0))
    return pl.pallas_call(
        body, name=name, grid=(S // tr,), in_specs=[tile, tile],
        out_specs=[tile, pl.BlockSpec((8, 128), lambda i: (0, 0))],
        out_shape=[jax.ShapeDtypeStruct((S, D), F32), jax.ShapeDtypeStruct((8, 128), F32)],
        compiler_params=_cp(("arbitrary",)),
    )(xo, target)


def _layer_tail_fwd(proj, a_in, b_in, x, w_po, w_ho, w_out, gate, g, name):
    S = proj.shape[0]
    tr = _row_tile(S, 256)
    nsh, _, wsh = w_po.shape

    def body(mgp_ref, mgh_ref, a_ref, b_ref, x_ref, wpo_ref, who_ref, wout_ref, gate_ref, g_ref,
             bra_ref, brb_ref, mt_ref, y_ref, xn_ref):
        av = a_ref[...]
        bra = jnp.concatenate([jnp.dot(av, wpo_ref[j], preferred_element_type=F32) for j in range(nsh)], axis=1)
        brb = jnp.dot(b_ref[...], who_ref[...], preferred_element_type=F32)
        mv = _sig(mgp_ref[...]) * bra + _sig(mgh_ref[...]) * brb
        bra_ref[...] = bra.astype(BF16)
        brb_ref[...] = brb.astype(BF16)
        mt_ref[...] = mv.T.astype(BF16)
        yv = jnp.dot(mv.astype(BF16), wout_ref[...], preferred_element_type=F32)
        y_ref[...] = yv
        r = lax.rsqrt(jnp.mean(yv * yv, axis=-1, keepdims=True) + EPS)
        xn_ref[...] = x_ref[...] + gate_ref[...] * ((yv * r) * g_ref[...])

    tile = pl.BlockSpec((tr, D), lambda i: (i, 0))
    whole = lambda t: pl.BlockSpec(t.shape, lambda i: (0,) * t.ndim)
    return pl.pallas_call(
        body, name=name, grid=(S // tr,),
        in_specs=[pl.BlockSpec((tr, D), lambda i: (i, MGP_BLK)), pl.BlockSpec((tr, D), lambda i: (i, MGH_BLK)),
                  pl.BlockSpec((tr, POOL_W), lambda i: (i, 0)), tile, tile, whole(w_po), whole(w_ho),
                  whole(w_out), _rowvec(), _rowvec()],
        out_specs=[tile, tile, pl.BlockSpec((D, tr), lambda i: (0, i)), tile, tile],
        out_shape=[jax.ShapeDtypeStruct((S, D), BF16), jax.ShapeDtypeStruct((S, D), BF16),
                   jax.ShapeDtypeStruct((D, S), BF16), jax.ShapeDtypeStruct((S, D), F32),
                   jax.ShapeDtypeStruct((S, D), F32)],
        compiler_params=_cp(("parallel",)),
    )(proj, proj, a_in, b_in, x, w_po, w_ho, w_out, gate, g)


def _layer_head_bwd(dxn, y, proj, br_a, br_b, w_po, w_ho, w_out, gate, g, name):
    S = y.shape[0]
    tr = _row_tile(S, 256)
    nsh, _, wsh = w_po.shape

    def body(dxn_ref, y_ref, mgp_ref, mgh_ref, bra_ref, brb_ref, wpo_ref, who_ref, wout_ref, gate_ref, g_ref,
             dy_ref, dba_ref, dbb_ref, dmg_ref, dain_ref, dbin_ref, dgate_ref, dg_ref):
        i = pl.program_id(0)

        @pl.when(i == 0)
        def _():
            dgate_ref[...] = jnp.zeros((1, D), F32)
            dg_ref[...] = jnp.zeros((1, D), F32)

        yv = y_ref[...]
        dv = dxn_ref[...]
        gv = g_ref[...]
        gt = gate_ref[...]
        r = lax.rsqrt(jnp.mean(yv * yv, axis=-1, keepdims=True) + EPS)
        yh = yv * r
        dgate_ref[...] += jnp.sum(dv * (yh * gv), axis=0, keepdims=True)
        dg_ref[...] += jnp.sum(dv * gt * yh, axis=0, keepdims=True)
        u = dv * gt * gv
        dy = (r * u - yv * (r * r * r) * jnp.mean(u * yv, axis=-1, keepdims=True)).astype(BF16)
        dy_ref[...] = dy
        dm = _dot_nt(dy, wout_ref[...])
        sp = _sig(mgp_ref[...])
        sh = _sig(mgh_ref[...])
        dba = (dm * sp).astype(BF16)
        dbb = (dm * sh).astype(BF16)
        dba_ref[...] = dba
        dbb_ref[...] = dbb
        dmg_ref[:, 0:D] = (dm * bra_ref[...].astype(F32) * sp * (1.0 - sp)).astype(BF16)
        dmg_ref[:, D:2 * D] = (dm * brb_ref[...].astype(F32) * sh * (1.0 - sh)).astype(BF16)
        dain = _dot_nt(dba[:, 0:wsh], wpo_ref[0])
        for j in range(1, nsh):
            dain = dain + _dot_nt(dba[:, j * wsh:(j + 1) * wsh], wpo_ref[j])
        dain_ref[...] = dain
        dbin_ref[...] = _dot_nt(dbb, who_ref[...])

    tile = pl.BlockSpec((tr, D), lambda i: (i, 0))
    whole = lambda t: pl.BlockSpec(t.shape, lambda i: (0,) * t.ndim)
    return pl.pallas_call(
        body, name=name, grid=(S // tr,),
        in_specs=[tile, tile, pl.BlockSpec((tr, D), lambda i: (i, MGP_BLK)),
                  pl.BlockSpec((tr, D), lambda i: (i, MGH_BLK)), tile, tile, whole(w_po), whole(w_ho),
                  whole(w_out), _rowvec(), _rowvec()],
        out_specs=[tile, tile, tile, pl.BlockSpec((tr, 2 * D), lambda i: (i, 0)),
                   pl.BlockSpec((tr, POOL_W), lambda i: (i, 0)), tile, _rowvec(), _rowvec()],
        out_shape=[jax.ShapeDtypeStruct((S, D), BF16)] * 3
        + [jax.ShapeDtypeStruct((S, 2 * D), BF16), jax.ShapeDtypeStruct((S, POOL_W), F32),
           jax.ShapeDtypeStruct((S, D), F32), jax.ShapeDtypeStruct((1, D), F32), jax.ShapeDtypeStruct((1, D), F32)],
        compiler_params=_cp(("arbitrary",)),
    )(dxn, y, proj, proj, br_a, br_b, w_po, w_ho, w_out, gate, g)


def _merge_fwd(proj, br_a, br_b, name):
    S = proj.shape[0]
    tr = _row_tile(S, 256)

    def body(mgp_ref, mgh_ref, a_ref, b_ref, o_ref, ot_ref):
        mv = _sig(mgp_ref[...]) * a_ref[...] + _sig(mgh_ref[...]) * b_ref[...]
        o_ref[...] = mv.astype(BF16)
        ot_ref[...] = mv.T.astype(BF16)

    tile = pl.BlockSpec((tr, D), lambda i: (i, 0))
    return pl.pallas_call(
        body, name=name, grid=(S // tr,),
        in_specs=[pl.BlockSpec((tr, D), lambda i: (i, MGP_BLK)), pl.BlockSpec((tr, D), lambda i: (i, MGH_BLK)),
                  tile, tile],
        out_specs=[tile, pl.BlockSpec((D, tr), lambda i: (0, i))],
        out_shape=[jax.ShapeDtypeStruct((S, D), BF16), jax.ShapeDtypeStruct((D, S), BF16)],
        compiler_params=_cp(("parallel",)),
    )(proj, proj, br_a, br_b)


def _merge_bwd(dm, proj, br_a, br_b, name):
    S = proj.shape[0]
    tr = _row_tile(S, 256)

    def body(dm_ref, mgp_ref, mgh_ref, a_ref, b_ref, da_ref, db_ref, dmg_ref):
        dmv = dm_ref[...]
        sp = _sig(mgp_ref[...])
        sh = _sig(mgh_ref[...])
        da_ref[...] = (dmv * sp).astype(BF16)
        db_ref[...] = (dmv * sh).astype(BF16)
        dmg_ref[:, 0:D] = (dmv * a_ref[...] * sp * (1.0 - sp)).astype(BF16)
        dmg_ref[:, D:2 * D] = (dmv * b_ref[...] * sh * (1.0 - sh)).astype(BF16)

    tile = pl.BlockSpec((tr, D), lambda i: (i, 0))
    return pl.pallas_call(
        body, name=name, grid=(S // tr,),
        in_specs=[tile, pl.BlockSpec((tr, D), lambda i: (i, MGP_BLK)),
                  pl.BlockSpec((tr, D), lambda i: (i, MGH_BLK)), tile, tile],
        out_specs=[tile, tile, pl.BlockSpec((tr, 2 * D), lambda i: (i, 0))],
        out_shape=[jax.ShapeDtypeStruct((S, D), BF16), jax.ShapeDtypeStruct((S, D), BF16),
                   jax.ShapeDtypeStruct((S, 2 * D), BF16)],
        compiler_params=_cp(("parallel",)),
    )(dm, proj, proj, br_a, br_b)


def _pool_pieces(u, g, S):
    rowi = lax.broadcasted_iota(jnp.int32, (S, 1), 0)

    def down(z, k):
        return jnp.where(rowi >= k, pltpu.roll(z, k, axis=0), 0.0)

    s2 = u + down(u, 1)
    s4 = s2 + down(s2, 2)
    s8 = s4 + down(s4, 4)
    s16 = s8 + down(s8, 8)
    win = jnp.where(g == 0, s2, jnp.where(g == 1, s4, jnp.where(g == 2, s8, s16)))
    w = jnp.where(g == 0, 2, jnp.where(g == 1, 4, jnp.where(g == 2, 8, 16)))
    count = jnp.minimum(rowi + 1, w).astype(F32)
    return win / count - u, count, rowi


def _pool_fwd(proj, pw, pscale, name):
    S = proj.shape[0]

    def body(pv_ref, pg_ref, pw_ref, sc_ref, a_ref, at_ref):
        g = pl.program_id(0)
        pooled, _, _ = _pool_pieces(pv_ref[...], g, S)
        pm = jnp.dot(pooled.astype(BF16), pw_ref[...].astype(BF16), preferred_element_type=F32)
        pgv = pg_ref[...]
        av = pm * sc_ref[...] * (pgv * _sig(pgv))
        a_ref[...] = av.astype(BF16)
        at_ref[...] = av.T.astype(BF16)

    return pl.pallas_call(
        body, name=name, grid=(GROUPS,),
        in_specs=[pl.BlockSpec((S, 128), lambda g: (0, PV0 + g)), pl.BlockSpec((S, 128), lambda g: (0, PG0 + g)),
                  pl.BlockSpec((None, 128, 128), lambda g: (g, 0, 0)), pl.BlockSpec((1, 128), lambda g: (0, g))],
        out_specs=[pl.BlockSpec((S, 128), lambda g: (0, g)), pl.BlockSpec((128, S), lambda g: (g, 0))],
        out_shape=[jax.ShapeDtypeStruct((S, POOL_W), BF16), jax.ShapeDtypeStruct((POOL_W, S), BF16)],
        compiler_params=_cp(("parallel",)),
    )(proj, proj, pw, pscale)


def _pool_bwd(da, proj, pw, pscale, name):
    S = proj.shape[0]

    def body(da_ref, pv_ref, pg_ref, pw_ref, sc_ref, dpv_ref, dpg_ref, dpw_ref, dsc_ref):
        g = pl.program_id(0)
        pooled, count, rowi = _pool_pieces(pv_ref[...], g, S)
        pwb = pw_ref[...].astype(BF16)
        pm = jnp.dot(pooled.astype(BF16), pwb, preferred_element_type=F32)
        scv = sc_ref[...]
        pgv = pg_ref[...]
        sg = _sig(pgv)
        dav = da_ref[...]
        d_ps = dav * (pgv * sg)
        dpg_ref[...] = (dav * (pm * scv) * _dsilu(pgv, sg)).astype(BF16)
        dsc_ref[...] = jnp.sum(d_ps * pm, axis=0, keepdims=True)
        d_pm = (d_ps * scv).astype(BF16)
        dpw_ref[...] = lax.dot_general(pooled.astype(BF16), d_pm, (((0,), (0,)), ((), ())),
                                       preferred_element_type=F32)
        d_pooled = lax.dot_general(d_pm, pwb, (((1,), (1,)), ((), ())), preferred_element_type=F32)
        z = d_pooled / count

        def up(v, k):
            return jnp.where(rowi < S - k, pltpu.roll(v, S - k, axis=0), 0.0)

        t2 = z + up(z, 1)
        t4 = t2 + up(t2, 2)
        t8 = t4 + up(t4, 4)
        t16 = t8 + up(t8, 8)
        adj = jnp.where(g == 0, t2, jnp.where(g == 1, t4, jnp.where(g == 2, t8, t16)))
        dpv_ref[...] = (adj - d_pooled).astype(BF16)

    col = lambda g: (0, g)
    return pl.pallas_call(
        body, name=name, grid=(GROUPS,),
        in_specs=[pl.BlockSpec((S, 128), col), pl.BlockSpec((S, 128), lambda g: (0, PV0 + g)),
                  pl.BlockSpec((S, 128), lambda g: (0, PG0 + g)),
                  pl.BlockSpec((None, 128, 128), lambda g: (g, 0, 0)), pl.BlockSpec((1, 128), col)],
        out_specs=[pl.BlockSpec((S, 128), col), pl.BlockSpec((S, 128), col),
                   pl.BlockSpec((None, 128, 128), lambda g: (g, 0, 0)), pl.BlockSpec((1, 128), col)],
        out_shape=[jax.ShapeDtypeStruct((S, POOL_W), BF16), jax.ShapeDtypeStruct((S, POOL_W), BF16),
                   jax.ShapeDtypeStruct((GROUPS, 128, 128), F32), jax.ShapeDtypeStruct((1, POOL_W), F32)],
        compiler_params=_cp(("parallel",)),
    )(da, proj, proj, pw, pscale)


def _chunk_cumsum(z, rowi):
    for sh in (1, 2, 4, 8, 16, 32):
        z = z + jnp.where(rowi >= sh, pltpu.roll(z, sh, axis=0), 0.0)
    return z


def _chunk_rev_cumsum(z, rowi):
    for sh in (1, 2, 4, 8, 16, 32):
        z = z + jnp.where(rowi < CH - sh, pltpu.roll(z, CH - sh, axis=0), 0.0)
    return z


def _dot_nn(a, b):
    return jnp.dot(a.astype(BF16), b.astype(BF16), preferred_element_type=F32)


def _dot_nt(a, b):
    return lax.dot_general(a.astype(BF16), b.astype(BF16), (((1,), (1,)), ((), ())), preferred_element_type=F32)


def _dot_tn(a, b):
    return lax.dot_general(a.astype(BF16), b.astype(BF16), (((0,), (0,)), ((), ())), preferred_element_type=F32)


def _gates(hq, hf, lbv):
    sq = _sig(hq)
    sf = _sig(hf)
    f = lbv + (1.0 - lbv) * sf
    fc = jnp.maximum(f, 1e-30)
    return hq * sq, sq, sf, f, fc, jnp.log(fc)


DECAY_CAP = 60.0


def _block_ref(c_ref, i):
    if i == 0:
        return jnp.zeros((1, HD), F32)
    return c_ref[SB * i - 1:SB * i, :]


def _block_decay(c_ref):
    spans = [_block_ref(c_ref, i) - c_ref[SB * (i + 1) - 1:SB * (i + 1), :] for i in range(CH // SB)]
    return functools.reduce(jnp.maximum, spans)


def _hgrn_fwd(proj, lb, gn, name, carry=None):
    S = proj.shape[0]
    nch = S // CH
    W = NH * HD

    def body(hq_ref, hf_ref, hi_ref, hg_ref, lb_ref, gn_ref, bin_ref, bint_ref, oraw_ref, st_ref, mild_ref,
             q_s, k_s, c_s, v_s, o_s, state_s, qf_s, kf_s, cf_s):
        state_s[...] = jnp.zeros((NH, HD, HD), F32)
        rowi = lax.broadcasted_iota(jnp.int32, (CH, 1), 0)
        coli = lax.broadcasted_iota(jnp.int32, (1, CH), 1)
        sbi = lax.broadcasted_iota(jnp.int32, (SB, 1), 0)
        gnv = gn_ref[...]

        def gates_pass(n, worst):
            rows = pl.ds(pl.multiple_of(n * CH, CH), CH)
            for hh in range(NH):
                lanes = slice(hh * HD, (hh + 1) * HD)
                q, _, _, f, _, logf = _gates(hq_ref[rows, lanes], hf_ref[rows, lanes], lb_ref[:, lanes])
                c = _chunk_cumsum(logf, rowi)
                qf_s[hh, rows, :] = q
                kf_s[hh, rows, :] = 1.0 - f
                cf_s[hh, rows, :] = c
                c_s[hh] = c
                worst = jnp.maximum(worst, _block_decay(c_s.at[hh]))
            return worst

        def between_chunks(hh, n, rows):
            lanes = slice(hh * HD, (hh + 1) * HD)
            q = qf_s[hh, rows, :]
            k = kf_s[hh, rows, :]
            c = cf_s[hh, rows, :]
            v = hi_ref[rows, lanes]
            q_s[hh] = q
            k_s[hh] = k
            c_s[hh] = c
            v_s[hh] = v
            st = state_s[hh]
            st_ref[hh, n] = st.astype(BF16)
            o_s[hh] = _dot_nt(q * jnp.exp(c), st)
            last = c_s[hh, CH - 1:CH, :]
            state_s[hh] = st * jnp.exp(last) + _dot_tn(v, k * jnp.exp(last - c))

        def within_chunk_matmul(hh):
            q, k, c, v = q_s[hh], k_s[hh], c_s[hh], v_s[hh]
            a = jnp.zeros((CH, CH), F32)
            for i in range(CH // SB):
                r_i = _block_ref(c_s.at[hh], i)
                qi = q * jnp.exp(jnp.minimum(c - r_i, 0.0))
                kei = k * jnp.exp(jnp.minimum(r_i - c, DECAY_CAP))
                m_i = (rowi >= SB * i) & (rowi < SB * (i + 1)) & (coli <= rowi)
                a = a + jnp.where(m_i, _dot_nt(qi, kei), 0.0)
            o_s[hh] += _dot_nn(a, v)

        def within_chunk_exact(hh):
            q, k, c, v = q_s[hh], k_s[hh], c_s[hh], v_s[hh]
            a_off = jnp.zeros((CH, CH), F32)
            for i in range(1, CH // SB):
                r_i = _block_ref(c_s.at[hh], i)
                qi = q * jnp.exp(jnp.minimum(c - r_i, 0.0))
                kei = k * jnp.exp(jnp.minimum(r_i - c, 0.0))
                m_i = (rowi >= SB * i) & (rowi < SB * (i + 1)) & (coli < SB * i)
                a_off = a_off + jnp.where(m_i, _dot_nt(qi, kei), 0.0)
            o_s[hh] += _dot_nn(a_off, v)
            for i in range(CH // SB):
                blk = slice(SB * i, SB * (i + 1))
                qb = q_s[hh, blk, :]
                cb = c_s[hh, blk, :]
                acc = jnp.zeros((SB, HD), F32)
                for s in range(SB):
                    row = SB * i + s
                    w = jnp.exp(jnp.minimum(cb - c_s[hh, row:row + 1, :], 0.0))
                    a_col = jnp.sum(qb * k_s[hh, row:row + 1, :] * w, axis=-1, keepdims=True)
                    acc = acc + jnp.where(sbi >= s, a_col, 0.0) * v_s[hh, row:row + 1, :]
                o_s[hh, blk, :] += acc

        def norm_and_gate(hh, rows):
            lanes = slice(hh * HD, (hh + 1) * HD)
            ov = o_s[hh]
            oraw_ref[rows, lanes] = ov
            r = lax.rsqrt(jnp.mean(ov * ov, axis=-1, keepdims=True) + EPS)
            hg = hg_ref[rows, lanes]
            bin_ref[rows, lanes] = ((ov * r) * gnv * (hg * _sig(hg))).astype(BF16)

        def chunk_with(within_chunk):
            def chunk(n, carry):
                rows = pl.ds(pl.multiple_of(n * CH, CH), CH)
                for hh in range(NH):
                    between_chunks(hh, n, rows)
                for hh in range(NH):
                    within_chunk(hh)
                for hh in range(NH):
                    norm_and_gate(hh, rows)
                return carry
            return chunk

        worst = lax.fori_loop(0, nch, gates_pass, jnp.zeros((1, HD), F32))
        mild = jnp.max(worst) <= DECAY_CAP
        mild_ref[...] = jnp.broadcast_to(jnp.where(mild, 1.0, 0.0), (8, HD))

        @pl.when(mild)
        def _():
            lax.fori_loop(0, nch, chunk_with(within_chunk_matmul), 0, unroll=4)

        @pl.when(jnp.logical_not(mild))
        def _():
            lax.fori_loop(0, nch, chunk_with(within_chunk_exact), 0)

        bint_ref[...] = bin_ref[...].astype(F32).T.astype(BF16)

    col = lambda off: pl.BlockSpec((S, W), lambda h: (0, off // NH + h))
    head = pl.BlockSpec((S, W), lambda h: (0, h))
    outs = _call(
        body, name=name, grid=(HEADS // NH,),
        in_specs=[col(HQ0), col(HF0), col(HI0), col(HG0), pl.BlockSpec((1, W), lambda h: (0, h)),
                  pl.BlockSpec((1, HD), lambda h: (0, 0))],
        out_specs=[head, pl.BlockSpec((W, S), lambda h: (h, 0)), head,
                   pl.BlockSpec((NH, nch, HD, HD), lambda h: (h, 0, 0, 0)),
                   pl.BlockSpec((8, HD), lambda h: (h, 0))],
        out_shape=[jax.ShapeDtypeStruct((S, D), BF16), jax.ShapeDtypeStruct((D, S), BF16),
                   jax.ShapeDtypeStruct((S, D), F32), jax.ShapeDtypeStruct((HEADS, nch, HD, HD), BF16),
                   jax.ShapeDtypeStruct((8 * HEADS // NH, HD), F32)],
        scratch_shapes=[pltpu.VMEM((NH, CH, HD), F32)] * 5 + [pltpu.VMEM((NH, HD, HD), F32)]
        + [pltpu.VMEM((NH, S, HD), F32)] * 3,
        sem=("parallel",), args=(proj, proj, proj, proj, lb, gn), carry=carry)
    return outs[:5], outs[5:]


def _hgrn_bwd(dbin, proj, oraw, states, mild, lb, gn, name, carry=None):
    S = proj.shape[0]
    nch = S // CH
    W = NH * HD

    def body(db_ref, hq_ref, hf_ref, hi_ref, hg_ref, or_ref, st_ref, mild_ref, lb_ref, gn_ref,
             dq_ref, df_ref, di_ref, dg_ref, dlb_ref, dgn_ref,
             q_s, k_s, c_s, v_s, do_s, dq_s, dk_s, dv_s, dc_s, dqd_s, dkd_s, dl_s, dst_s, dlb_s, dgn_s):
        dst_s[...] = jnp.zeros((NH, HD, HD), F32)
        dlb_s[...] = jnp.zeros((1, W), F32)
        dgn_s[...] = jnp.zeros((1, HD), F32)
        rowi = lax.broadcasted_iota(jnp.int32, (CH, 1), 0)
        rowi2 = lax.broadcasted_iota(jnp.int32, (CH, CH), 0)
        coli2 = lax.broadcasted_iota(jnp.int32, (CH, CH), 1)
        sbi = lax.broadcasted_iota(jnp.int32, (SB, 1), 0)
        gnv = gn_ref[...]
        def between_chunks(hh, n, rows):
            lanes = slice(hh * HD, (hh + 1) * HD)
            q, _, _, f, _, logf = _gates(hq_ref[rows, lanes], hf_ref[rows, lanes], lb_ref[:, lanes])
            k = 1.0 - f
            v = hi_ref[rows, lanes]
            c = _chunk_cumsum(logf, rowi)
            ov = or_ref[rows, lanes]
            hg = hg_ref[rows, lanes]
            sg = _sig(hg)
            r = lax.rsqrt(jnp.mean(ov * ov, axis=-1, keepdims=True) + EPS)
            dbv = db_ref[rows, lanes]
            d_on = dbv * (hg * sg)
            dg_ref[rows, lanes] = (dbv * ((ov * r) * gnv) * _dsilu(hg, sg)).astype(BF16)
            dgn_s[...] += jnp.sum(d_on * (ov * r), axis=0, keepdims=True)
            u = d_on * gnv
            do = r * u - ov * (r * r * r) * jnp.mean(u * ov, axis=-1, keepdims=True)
            q_s[hh] = q
            k_s[hh] = k
            c_s[hh] = c
            v_s[hh] = v
            do_s[hh] = do
            st = st_ref[hh, n].astype(F32)
            dst = dst_s[hh]
            ec = jnp.exp(c)
            last = c_s[hh, CH - 1:CH, :]
            el = jnp.exp(last - c)
            elast = jnp.exp(last)
            dq = _dot_nn(do, st) * ec
            dk = _dot_nn(v, dst) * el
            dq_s[hh] = dq
            dk_s[hh] = dk
            dv_s[hh] = _dot_nt(k * el, dst)
            dc_s[hh] = q * dq - k * dk
            dl_s[hh] = (jnp.sum(k * dk, axis=0, keepdims=True)
                        + elast * jnp.sum(st * dst, axis=0, keepdims=True))
            dst_s[hh] = dst * elast + _dot_tn(do, q * ec)

        def pairs_matmul(hh, first, cap, strict):
            q, k, c, v, do = q_s[hh], k_s[hh], c_s[hh], v_s[hh], do_s[hh]
            d_a = _dot_nt(do, v).astype(BF16).astype(F32)
            d_at = d_a.T
            at = jnp.zeros((CH, CH), F32)
            dq, dk, dcum = dq_s[hh], dk_s[hh], dc_s[hh]
            for i in range(first, CH // SB):
                r_i = _block_ref(c_s.at[hh], i)
                eq = jnp.exp(jnp.minimum(c - r_i, 0.0))
                ek = jnp.exp(jnp.minimum(r_i - c, cap))
                qi = (q * eq).astype(BF16).astype(F32)
                kei = (k * ek).astype(BF16).astype(F32)
                in_t = (rowi2 >= SB * i) & (rowi2 < SB * (i + 1))
                in_s = (coli2 >= SB * i) & (coli2 < SB * (i + 1))
                m_ts = in_t & ((coli2 < SB * i) if strict else (coli2 <= rowi2))
                m_st = in_s & ((rowi2 < SB * i) if strict else (rowi2 <= coli2))
                at = at + jnp.where(m_st, _dot_nt(kei, qi), 0.0)
                dq_i = _dot_nn(jnp.where(m_ts, d_a, 0.0), kei)
                dk_i = _dot_nn(jnp.where(m_st, d_at, 0.0), qi)
                dq = dq + dq_i * eq
                dk = dk + dk_i * ek
                dcum = dcum + (qi * dq_i - kei * dk_i)
            dq_s[hh] = dq
            dk_s[hh] = dk
            dc_s[hh] = dcum
            dv_s[hh] += _dot_nn(at, do)

        def pairs_exact(hh):
            dqd_s[hh] = jnp.zeros((CH, HD), F32)
            dkd_s[hh] = jnp.zeros((CH, HD), F32)
            for i in range(CH // SB):
                blk = slice(SB * i, SB * (i + 1))
                qb = q_s[hh, blk, :]
                cb = c_s[hh, blk, :]
                dob = do_s[hh, blk, :]
                dq_acc = jnp.zeros((SB, HD), F32)
                for s in range(SB):
                    row = SB * i + s
                    ks = k_s[hh, row:row + 1, :]
                    vs = v_s[hh, row:row + 1, :]
                    w = jnp.exp(jnp.minimum(cb - c_s[hh, row:row + 1, :], 0.0))
                    live = sbi >= s
                    a_col = jnp.where(live, jnp.sum(qb * ks * w, axis=-1, keepdims=True), 0.0)
                    da_col = jnp.where(live, jnp.sum(dob * vs, axis=-1, keepdims=True), 0.0)
                    dq_acc = dq_acc + da_col * ks * w
                    dkd_s[hh, row:row + 1, :] += jnp.sum(da_col * qb * w, axis=0, keepdims=True)
                    dv_s[hh, row:row + 1, :] += jnp.sum(a_col * dob, axis=0, keepdims=True)
                dqd_s[hh, blk, :] += dq_acc
            dq_d = dqd_s[hh]
            dk_d = dkd_s[hh]
            dq_s[hh] += dq_d
            dk_s[hh] += dk_d
            dc_s[hh] += q_s[hh] * dq_d - k_s[hh] * dk_d

        def gate_grads(hh, rows):
            lanes = slice(hh * HD, (hh + 1) * HD)
            lbv = lb_ref[:, lanes]
            hq = hq_ref[rows, lanes]
            _, sq, sf, f, fc, _ = _gates(hq, hf_ref[rows, lanes], lbv)
            dlogf = _chunk_rev_cumsum(dc_s[hh], rowi) + dl_s[hh]
            dfv = jnp.where(f > 1e-30, dlogf / fc, 0.0) - dk_s[hh]
            dlb_s[:, lanes] += jnp.sum(dfv * (1.0 - sf), axis=0, keepdims=True)
            df_ref[rows, lanes] = (dfv * (1.0 - lbv) * sf * (1.0 - sf)).astype(BF16)
            dq_ref[rows, lanes] = (dq_s[hh] * _dsilu(hq, sq)).astype(BF16)
            di_ref[rows, lanes] = dv_s[hh].astype(BF16)

        def chunk_with(pairs):
            def chunk(j, carry):
                n = nch - 1 - j
                rows = pl.ds(pl.multiple_of(n * CH, CH), CH)
                for hh in range(NH):
                    between_chunks(hh, n, rows)
                for hh in range(NH):
                    pairs(hh)
                for hh in range(NH):
                    gate_grads(hh, rows)
                return carry
            return chunk

        def pairs_mild(hh):
            pairs_matmul(hh, 0, DECAY_CAP, strict=False)

        def pairs_any(hh):
            pairs_matmul(hh, 1, 0.0, strict=True)
            pairs_exact(hh)

        mild = jnp.max(mild_ref[...]) > 0.5

        @pl.when(mild)
        def _():
            lax.fori_loop(0, nch, chunk_with(pairs_mild), 0, unroll=2)

        @pl.when(jnp.logical_not(mild))
        def _():
            lax.fori_loop(0, nch, chunk_with(pairs_any), 0)

        dlb_ref[...] = dlb_s[...]
        dgn_ref[...] = jnp.broadcast_to(dgn_s[...], (8, HD))

    col = lambda off: pl.BlockSpec((S, W), lambda h: (0, off // NH + h))
    head = pl.BlockSpec((S, W), lambda h: (0, h))
    vec = pl.BlockSpec((1, W), lambda h: (0, h))
    outs = _call(
        body, name=name, grid=(HEADS // NH,),
        in_specs=[head, col(HQ0), col(HF0), col(HI0), col(HG0), head,
                  pl.BlockSpec((NH, nch, HD, HD), lambda h: (h, 0, 0, 0)),
                  pl.BlockSpec((8, HD), lambda h: (h, 0)), vec, pl.BlockSpec((1, HD), lambda h: (0, 0))],
        out_specs=[head, head, head, head, vec, pl.BlockSpec((8, HD), lambda h: (h, 0))],
        out_shape=[jax.ShapeDtypeStruct((S, D), BF16)] * 4
        + [jax.ShapeDtypeStruct((1, D), F32), jax.ShapeDtypeStruct((8 * HEADS // NH, HD), F32)],
        scratch_shapes=[pltpu.VMEM((NH, CH, HD), F32)] * 11
        + [pltpu.VMEM((NH, 1, HD), F32), pltpu.VMEM((NH, HD, HD), F32), pltpu.VMEM((1, W), F32),
           pltpu.VMEM((1, HD), F32)],
        sem=("parallel",), args=(dbin, proj, proj, proj, proj, oraw, states, mild, lb, gn), carry=carry)
    dq, df, di, dg, dlb, dgn = outs[:6]
    return (dq, df, di, dg, dlb, dgn.reshape(HEADS // NH, 8, HD)[:, 0, :]), outs[6:]


def _lower_bounds(l0, l1):
    m = jnp.maximum(l0, l1)
    e0 = jnp.exp(l0 - m)
    e1 = jnp.exp(l1 - m)
    tot = e0 + e1
    p0 = e0 / tot
    p1 = e1 / tot
    return jnp.clip(p0 - p0, 0.0, 1.0), jnp.clip((p0 + p1) - p0, 0.0, 1.0)


def _lb_fwd(logits):
    def body(l_ref, o_ref):
        lb0, lb1 = _lower_bounds(l_ref[0:1, :], l_ref[1:2, :])
        o_ref[0:1, :] = lb0
        o_ref[1:2, :] = lb1

    return pl.pallas_call(body, name="lb_fwd", out_shape=jax.ShapeDtypeStruct((2, D), F32))(logits)


def _lb_bwd(logits, dlb):
    def body(l_ref, d_ref, o_ref):
        _, vjp = jax.vjp(_lower_bounds, l_ref[0:1, :], l_ref[1:2, :])
        g0, g1 = vjp((d_ref[0:1, :], d_ref[1:2, :]))
        o_ref[0:1, :] = g0
        o_ref[1:2, :] = g1

    return pl.pallas_call(body, name="lb_bwd", out_shape=jax.ShapeDtypeStruct((2, D), F32))(logits, dlb)


ADA_PAD = 128


def _ada_fwd(c_pad, w_ada, b_sh):
    ns = w_ada.shape[2]

    def body(c_ref, w_ref, b_ref, o_ref):
        cv = c_ref[...]
        ca = (cv * _sig(cv)).astype(BF16)
        for l in range(2):
            res = jnp.dot(ca, w_ref[l].astype(BF16), preferred_element_type=F32)
            o_ref[:, l * ns:(l + 1) * ns] = res[0:NDEV, :] + b_ref[l:l + 1, :]

    return pl.pallas_call(body, name="ada_fwd", out_shape=jax.ShapeDtypeStruct((NDEV, 2 * ns), F32),
                          compiler_params=_cp())(c_pad, w_ada, b_sh)


def _ada_wgrad(c_pad_t, d_ada_sh):
    ns = d_ada_sh.shape[2]

    def body(c_ref, d_ref, o_ref):
        cv = c_ref[...]
        ca = (cv * _sig(cv)).astype(BF16)
        for l in range(2):
            o_ref[l] = jnp.dot(ca, d_ref[l].astype(BF16), preferred_element_type=F32)

    return pl.pallas_call(body, name="ada_wgrad", out_shape=jax.ShapeDtypeStruct((2, D, ns), F32),
                          compiler_params=_cp())(c_pad_t, d_ada_sh)


def _sum_devices(g):
    _, R, C = g.shape

    def body(g_ref, o_ref):
        acc = g_ref[0]
        for d in range(1, NDEV):
            acc = acc + g_ref[d]
        o_ref[...] = acc

    return pl.pallas_call(body, name="sum_devices", out_shape=jax.ShapeDtypeStruct((R, C), F32),
                          compiler_params=_cp())(g)


def _adamw(w, g, m, v, name, carry=None):
    R, C = w.shape
    tr = _row_tile(R, max(8, (1 << 19) // C))

    def body(w_ref, g_ref, m_ref, v_ref, d_ref, nm_ref, nv_ref):
        gv = g_ref[...]
        nm = B1 * m_ref[...] + (1.0 - B1) * gv
        nv = B2 * v_ref[...] + (1.0 - B2) * (gv * gv)
        m_hat = nm / (1.0 - B1 ** STEP)
        v_hat = nv / (1.0 - B2 ** STEP)
        d_ref[...] = -LR * (m_hat / (jnp.sqrt(v_hat) + AEPS) + WD * w_ref[...])
        nm_ref[...] = nm
        nv_ref[...] = nv

    tile = pl.BlockSpec((tr, C), lambda i: (i, 0))
    return _call(body, name=name, grid=(R // tr,), in_specs=[tile] * 4, out_specs=[tile] * 3,
                 out_shape=[jax.ShapeDtypeStruct((R, C), F32)] * 3, sem=("parallel",), args=(w, g, m, v),
                 carry=carry)


def _cast_to_slot(place, w, l, name):
    _, R, C = w.shape
    tr = _row_tile(R, max(8, (1 << 19) // C))

    def body(p_ref, w_ref, o_ref):
        o_ref[...] = w_ref[...].astype(BF16)

    return pl.pallas_call(
        body, name=name, out_shape=jax.ShapeDtypeStruct((NCHIP, R, C), BF16),
        grid_spec=pltpu.PrefetchScalarGridSpec(
            num_scalar_prefetch=1, grid=(R // tr,),
            in_specs=[pl.BlockSpec((None, tr, C), lambda i, p_ref: (l, i, 0))],
            out_specs=pl.BlockSpec((None, tr, C), lambda i, p_ref: (p_ref[0], i, 0))),
        compiler_params=_cp(("parallel",)),
    )(place, w)


def _pair_add(core, g, got, name):
    _, R, C = g.shape
    r2 = R // 2
    tr = _row_tile(r2, max(8, (1 << 19) // C))
    nt = r2 // tr

    def body(c_ref, a_ref, b_ref, o_ref):
        o_ref[...] = (a_ref[...].astype(F32) + b_ref[...].astype(F32)).astype(o_ref.dtype)

    return pl.pallas_call(
        body, name=name, out_shape=jax.ShapeDtypeStruct((NCHIP, r2, C), BF16),
        grid_spec=pltpu.PrefetchScalarGridSpec(
            num_scalar_prefetch=1, grid=(NCHIP, nt),
            in_specs=[pl.BlockSpec((None, tr, C), lambda j, i, c_ref: (j, c_ref[0] * nt + i, 0)),
                      pl.BlockSpec((None, tr, C), lambda j, i, c_ref: (j, i, 0))],
            out_specs=pl.BlockSpec((None, tr, C), lambda j, i, c_ref: (j, i, 0))),
        compiler_params=_cp(("parallel", "parallel")),
    )(core, g, got)


def _chip_sum(place, part, recv, layer, both, name):
    _, r2, C = part.shape
    tr = _row_tile(r2, max(8, (1 << 18) // C))
    nt = r2 // tr

    def body(p_ref, own_ref, r_ref, *rest):
        o_ref = rest[-1]
        me = p_ref[0]
        own = own_ref[...].astype(F32)
        acc = None
        for j in range(NCHIP):
            slot = jnp.minimum(jnp.where(j > me, j - 1, j), NCHIP - 2)
            term = jnp.where(me == j, own, r_ref[slot].astype(F32))
            acc = term if acc is None else acc + term
        o_ref[...] = acc

    args = (place, part, recv) if both is None else (place, part, recv, both)
    return pl.pallas_call(
        body, name=name, out_shape=jax.ShapeDtypeStruct((2, 2 * r2, C), F32),
        grid_spec=pltpu.PrefetchScalarGridSpec(
            num_scalar_prefetch=1, grid=(nt,),
            in_specs=[pl.BlockSpec((None, tr, C), lambda i, p_ref: (p_ref[0], i, 0)),
                      pl.BlockSpec((NCHIP - 1, tr, C), lambda i, p_ref: (0, i, 0))] + [ANY] * (len(args) - 3),
            out_specs=pl.BlockSpec((None, tr, C), lambda i, p_ref: (layer, p_ref[1] * nt + i, 0))),
        input_output_aliases={} if both is None else {3: 0},
        compiler_params=_cp(("parallel",)),
    )(*args)


def _place():
    x, y, c = lax.axis_index("x"), lax.axis_index("y"), lax.axis_index("c")
    chips = [(1 - x, y), (x, 1 - y), (1 - x, 1 - y)]
    return x, y, c, chips


def _gather_small(blk, name):
    m_per, n = blk.shape

    def body(x_ref, out_ref, send_sems, recv_sems, local_sem):
        x, y, c, chips = _place()
        me, sibling = (x, y, c), (x, y, 1 - c)

        def rows(px, py, pc):
            return out_ref.at[pl.ds((4 * px + 2 * py + pc) * m_per, m_per), :]

        def copy(k, block, to, src=None):
            return pltpu.make_async_remote_copy(
                src_ref=rows(*block) if src is None else src, dst_ref=rows(*block),
                send_sem=send_sems.at[k], recv_sem=recv_sems.at[k], device_id=to, device_id_type=MESH)

        mine = pltpu.make_async_copy(x_ref, rows(*me), local_sem)
        mine.start()
        first = [copy(0, me, sibling, src=x_ref)]
        first += [copy(1 + j, me, (*chip, c), src=x_ref) for j, chip in enumerate(chips)]
        for cp in first:
            cp.start()
        passed = [copy(4 + j, (*chip, c), sibling) for j, chip in enumerate(chips)]
        for j, chip in enumerate(chips):
            copy(1 + j, (*chip, c), me).wait_recv()
            passed[j].start()
        copy(0, sibling, me).wait_recv()
        for j, chip in enumerate(chips):
            copy(4 + j, (*chip, 1 - c), me).wait_recv()
        for cp in first + passed:
            cp.wait_send()
        mine.wait()

    return pl.pallas_call(
        body, name=name, out_shape=jax.ShapeDtypeStruct((NDEV * m_per, n), blk.dtype),
        in_specs=[pl.BlockSpec(memory_space=pltpu.VMEM)], out_specs=pl.BlockSpec(memory_space=pltpu.VMEM),
        scratch_shapes=[pltpu.SemaphoreType.DMA((7,)), pltpu.SemaphoreType.DMA((7,)), pltpu.SemaphoreType.DMA],
        compiler_params=_cp(),
    )(blk)


def _gather_rows_carry(blk):
    m_per, n = blk.shape

    def rows(ref, px, py, pc):
        return ref.at[pl.ds((4 * px + 2 * py + pc) * m_per, m_per), :]

    def copy(ins, outs, send_sems, recv_sems, k, block, to, own=False):
        return pltpu.make_async_remote_copy(
            src_ref=ins[0] if own else rows(outs[0], *block), dst_ref=rows(outs[0], *block),
            send_sem=send_sems.at[k], recv_sem=recv_sems.at[k], device_id=to, device_id_type=MESH)

    def mine(ins, outs, send_sems):
        x, y, c, _ = _place()
        return pltpu.make_async_copy(ins[0], rows(outs[0], x, y, c), send_sems.at[7])

    def start(ins, outs, send_sems, recv_sems):
        x, y, c, chips = _place()
        mine(ins, outs, send_sems).start()
        copy(ins, outs, send_sems, recv_sems, 0, (x, y, c), (x, y, 1 - c), own=True).start()
        for j, chip in enumerate(chips):
            copy(ins, outs, send_sems, recv_sems, 1 + j, (x, y, c), (*chip, c), own=True).start()

    def finish(ins, outs, send_sems, recv_sems):
        x, y, c, chips = _place()
        for j, chip in enumerate(chips):
            copy(ins, outs, send_sems, recv_sems, 1 + j, (*chip, c), (x, y, c)).wait_recv()
            copy(ins, outs, send_sems, recv_sems, 4 + j, (*chip, c), (x, y, 1 - c)).start()
        copy(ins, outs, send_sems, recv_sems, 0, (x, y, 1 - c), (x, y, c)).wait_recv()
        for j, chip in enumerate(chips):
            copy(ins, outs, send_sems, recv_sems, 4 + j, (*chip, 1 - c), (x, y, c)).wait_recv()
        copy(ins, outs, send_sems, recv_sems, 0, (x, y, c), (x, y, 1 - c), own=True).wait_send()
        for j, chip in enumerate(chips):
            copy(ins, outs, send_sems, recv_sems, 1 + j, (x, y, c), (*chip, c), own=True).wait_send()
            copy(ins, outs, send_sems, recv_sems, 4 + j, (*chip, c), (x, y, 1 - c)).wait_send()
        mine(ins, outs, send_sems).wait()

    return _Carry([blk], [jax.ShapeDtypeStruct((NDEV * m_per, n), blk.dtype)], {}, 8, start, finish)


def _gather_carry(shards):
    n = len(shards)

    def over_ici(outs, send_sems, recv_sems, a, j, chip_xy, slot):
        x, y, c, _ = _place()
        r2 = outs[a].shape[1] // 2
        blk = outs[a].at[slot, pl.ds(c * r2, r2), :]
        return pltpu.make_async_remote_copy(
            src_ref=blk, dst_ref=blk, send_sem=send_sems.at[6 * a + j], recv_sem=recv_sems.at[6 * a + j],
            device_id=(*chip_xy, c), device_id_type=MESH)

    def over_d2d(outs, send_sems, recv_sems, a, j, slot, half):
        x, y, c, _ = _place()
        r2 = outs[a].shape[1] // 2
        blk = outs[a].at[slot, pl.ds(half * r2, r2), :]
        return pltpu.make_async_remote_copy(
            src_ref=blk, dst_ref=blk, send_sem=send_sems.at[6 * a + 3 + j], recv_sem=recv_sems.at[6 * a + 3 + j],
            device_id=(x, y, 1 - c), device_id_type=MESH)

    def start(ins, outs, send_sems, recv_sems):
        x, y, c, chips = _place()
        for a in range(n):
            for j, chip_xy in enumerate(chips):
                over_ici(outs, send_sems, recv_sems, a, j, chip_xy, 2 * x + y).start()

    def finish(ins, outs, send_sems, recv_sems):
        x, y, c, chips = _place()
        for a in range(n):
            for j, (cx, cy) in enumerate(chips):
                over_ici(outs, send_sems, recv_sems, a, j, (cx, cy), 2 * cx + cy).wait_recv()
                over_d2d(outs, send_sems, recv_sems, a, j, 2 * cx + cy, c).start()
        for a in range(n):
            for j, (cx, cy) in enumerate(chips):
                over_d2d(outs, send_sems, recv_sems, a, j, 2 * cx + cy, 1 - c).wait_recv()
        for a in range(n):
            for j, (cx, cy) in enumerate(chips):
                over_ici(outs, send_sems, recv_sems, a, j, (cx, cy), 2 * x + y).wait_send()
                over_d2d(outs, send_sems, recv_sems, a, j, 2 * cx + cy, c).wait_send()

    return _Carry(shards, [jax.ShapeDtypeStruct(s.shape, s.dtype) for s in shards],
                  {a: a for a in range(n)}, 6 * n, start, finish)


def _rs_pair(grads, name):
    n = len(grads)

    def body(*refs):
        ins, gots = refs[:n], refs[n:2 * n]
        send_sems, recv_sems = refs[2 * n:]
        x, y, c, _ = _place()
        cps = []
        for a in range(n):
            r2 = ins[a].shape[1] // 2
            cp = pltpu.make_async_remote_copy(
                src_ref=ins[a].at[:, pl.ds((1 - c) * r2, r2), :], dst_ref=gots[a],
                send_sem=send_sems.at[a], recv_sem=recv_sems.at[a],
                device_id=(x, y, 1 - c), device_id_type=MESH)
            cp.start()
            cps.append(cp)
        for cp in cps:
            cp.wait()

    half = [jax.ShapeDtypeStruct((NCHIP, g.shape[1] // 2, g.shape[2]), g.dtype) for g in grads]
    return pl.pallas_call(
        body, name=name, out_shape=half, in_specs=[ANY] * n, out_specs=[ANY] * n,
        scratch_shapes=[pltpu.SemaphoreType.DMA((n,)), pltpu.SemaphoreType.DMA((n,))],
        compiler_params=_cp(),
    )(*grads)


def _chips_carry(parts):
    n = len(parts)

    def send(ins, outs, send_sems, recv_sems, a, j, chip_xy):
        x, y, c, _ = _place()
        me, them = 2 * x + y, 2 * chip_xy[0] + chip_xy[1]
        return pltpu.make_async_remote_copy(
            src_ref=ins[a].at[them], dst_ref=outs[a].at[me - (me > them).astype(jnp.int32)],
            send_sem=send_sems.at[3 * a + j], recv_sem=recv_sems.at[3 * a + j],
            device_id=(*chip_xy, c), device_id_type=MESH)

    def start(ins, outs, send_sems, recv_sems):
        _, _, _, chips = _place()
        for a in range(n):
            for j, chip_xy in enumerate(chips):
                send(ins, outs, send_sems, recv_sems, a, j, chip_xy).start()

    def finish(ins, outs, send_sems, recv_sems):
        x, y, c, chips = _place()
        me = 2 * x + y
        for a in range(n):
            for j, (cx, cy) in enumerate(chips):
                them = 2 * cx + cy
                blk = outs[a].at[them - (them > me).astype(jnp.int32)]
                pltpu.make_async_remote_copy(
                    src_ref=blk, dst_ref=blk, send_sem=send_sems.at[3 * a + j], recv_sem=recv_sems.at[3 * a + j],
                    device_id=(cx, cy, c), device_id_type=MESH).wait_recv()
        for a in range(n):
            for j, chip_xy in enumerate(chips):
                send(ins, outs, send_sems, recv_sems, a, j, chip_xy).wait_send()

    return _Carry(parts, [jax.ShapeDtypeStruct((NCHIP - 1,) + p.shape[1:], p.dtype) for p in parts], {},
                  3 * n, start, finish)


def _rs_swap(fulls):
    n = len(fulls)

    def body(*refs):
        outs = refs[n:2 * n]
        send_sems, recv_sems = refs[2 * n:]
        x, y, c, _ = _place()
        cps = []
        for a in range(n):
            r2 = outs[a].shape[1] // 2
            mine = outs[a].at[:, pl.ds(c * r2, r2), :]
            cp = pltpu.make_async_remote_copy(
                src_ref=mine, dst_ref=mine, send_sem=send_sems.at[a], recv_sem=recv_sems.at[a],
                device_id=(x, y, 1 - c), device_id_type=MESH)
            cp.start()
            cps.append(cp)
        for a in range(n):
            r2 = outs[a].shape[1] // 2
            blk = outs[a].at[:, pl.ds((1 - c) * r2, r2), :]
            pltpu.make_async_remote_copy(
                src_ref=blk, dst_ref=blk, send_sem=send_sems.at[a], recv_sem=recv_sems.at[a],
                device_id=(x, y, 1 - c), device_id_type=MESH).wait_recv()
        for cp in cps:
            cp.wait_send()

    return pl.pallas_call(
        body, name="rs_swap", out_shape=[jax.ShapeDtypeStruct(f.shape, f.dtype) for f in fulls],
        in_specs=[ANY] * n, out_specs=[ANY] * n, input_output_aliases={a: a for a in range(n)},
        scratch_shapes=[pltpu.SemaphoreType.DMA((n,)), pltpu.SemaphoreType.DMA((n,))],
        compiler_params=_cp(),
    )(*fulls)


def _mm_ride(a, b, carry, **kw):
    if carry is None:
        return _mm(a, b, **kw), []
    return _mm(a, b, carry=carry, **kw)


def _layer_fwd(l, x, ada, w, small, ride):
    shift, scale, gate = ada[:, 0:D], ada[:, D:2 * D], ada[:, 2 * D:3 * D]
    h, h_t = _prenorm_fwd(x, small["g_pre"][l], scale, shift, f"prenorm_fwd{l}")
    proj, landed = _mm_ride(h, w["w_in"][l], ride["proj"][0], name=f"proj{l}", b_mode="nn_sh", tm=2048)
    ride["proj"][1](landed)
    a_in, a_in_t = _pool_fwd(proj, small["pool_w"][l], small["pool_scale"][l], f"pool_fwd{l}")
    (b_in, b_in_t, o_raw, states, mild), landed = _hgrn_fwd(proj, small["lb"][l], small["hgrn_norm_g"][l],
                                                           f"hgrn_fwd{l}", carry=ride["hgrn"][0])
    ride["hgrn"][1](landed)
    br_a, br_b, merged_t, y, x_new = _layer_tail_fwd(
        proj, a_in, b_in, x, w["w_pool_o"][l], w["w_hgrn_o"][l].reshape(D, D), w["w_out"][l].reshape(D, D),
        gate, small["g_post"][l], f"tail_fwd{l}")
    saved = dict(x=x, h_t=h_t, proj=proj, a_in_t=a_in_t, b_in_t=b_in_t, o_raw=o_raw, states=states, mild=mild,
                 br_a=br_a, br_b=br_b, merged_t=merged_t, y=y, scale=scale, gate=gate)
    return x_new, saved


def _layer_bwd(l, dxn, sv, w, small, ride):
    dy, dbr_a, dbr_b, dmg, da_in, db_in, dgate, dg_post = _layer_head_bwd(
        dxn, sv["y"], sv["proj"], sv["br_a"], sv["br_b"], w["w_pool_o"][l], w["w_hgrn_o"][l].reshape(D, D),
        w["w_out"][l].reshape(D, D), sv["gate"], small["g_post"][l], f"head_bwd{l}")
    gw_out = _mm(sv["merged_t"], dy, name=f"gw_out{l}", out_dtype=BF16)
    gw_pool_o = _mm(sv["a_in_t"], dbr_a, name=f"gw_pool_o{l}", out_shards=NCHIP, out_dtype=BF16)
    gw_hgrn_o = _mm(sv["b_in_t"], dbr_b, name=f"gw_hgrn_o{l}", out_dtype=BF16)
    big = dict(w_pool_o=gw_pool_o, w_hgrn_o=gw_hgrn_o.reshape(NCHIP, D // NCHIP, D),
               w_out=gw_out.reshape(NCHIP, D // NCHIP, D))
    carry, landed = ride["hgrn"](big)
    (dhq, dhf, dhi, dhg, dlb, dgn), outs = _hgrn_bwd(db_in, sv["proj"], sv["o_raw"], sv["states"], sv["mild"],
                                                     small["lb"][l], small["hgrn_norm_g"][l], f"hgrn_bwd{l}",
                                                     carry=carry)
    landed(outs)
    dpv, dpg, dpw, dpsc = _pool_bwd(da_in, sv["proj"], small["pool_w"][l], small["pool_scale"][l],
                                    f"pool_bwd{l}")
    dproj = jnp.concatenate([dpv, dpg, dhq, dhf, dhi, dhg, dmg], axis=1)
    big["w_in"] = _mm(sv["h_t"], dproj, name=f"gw_in{l}", out_shards=NCHIP, out_dtype=BF16)
    carry, landed = ride["d_h"](big)
    dh, outs = _mm_ride(dproj, w["w_in"][l], carry, name=f"d_h{l}", b_mode="nt_shk", tn=1024)
    landed(outs)
    dx, dshift, dscale, dg_pre = _prenorm_bwd(dh, dxn, sv["x"], small["g_pre"][l], sv["scale"],
                                              f"prenorm_bwd{l}")
    little = dict(d_ada=jnp.concatenate([dshift, dscale, dgate], axis=1), g_pre=dg_pre, g_post=dg_post,
                  pool_w=dpw, pool_scale=dpsc, lb=dlb, hgrn_norm_g=jnp.sum(dgn, axis=0, keepdims=True))
    return dx, big, little


SMALL_ROWS = 176


def _rows8(t):
    t = t.reshape(-1, D)
    return jnp.pad(t, ((0, -t.shape[0] % 8), (0, 0)))


def _pack_small_weights(b_ada, g_pre, g_post, lb_logits, pool_w, pool_scale, hgrn_norm_g):
    gn = jnp.pad(hgrn_norm_g.reshape(1, 2 * HD), ((0, 0), (0, D - 2 * HD)))
    return jnp.concatenate([_rows8(b_ada), _rows8(g_pre), _rows8(g_post), _rows8(lb_logits), _rows8(pool_w),
                            _rows8(pool_scale), _rows8(gn)], axis=0)


def _pack_small(parts):
    both = lambda key: jnp.stack([parts[l][key] for l in range(2)])
    return _pack_small_weights(both("d_ada"), both("g_pre"), both("g_post"), both("lb"), both("pool_w"),
                               both("pool_scale"), both("hgrn_norm_g"))


def _unpack_small(p):
    return (p[0:6].reshape(2, 3 * D), p[8:10], p[16:18], p[24:26], p[32:160].reshape(2, GROUPS, 128, 128),
            p[160:161].reshape(2, POOL_W), p[168:169, 0:2 * HD].reshape(2, HD))


def kernel(x, c, w_ada, b_ada, g_pre, g_post, w_in, pool_w, pool_scale, lb_logits, hgrn_norm_g, w_pool_o, w_hgrn_o, w_out, loss_target, m_w_ada, m_b_ada, m_g_pre, m_g_post, m_w_in, m_pool_w, m_pool_scale, m_lb_logits, m_hgrn_norm_g, m_w_pool_o, m_w_hgrn_o, m_w_out, v_w_ada, v_b_ada, v_g_pre, v_g_post, v_w_in, v_pool_w, v_pool_scale, v_lb_logits, v_hgrn_norm_g, v_w_pool_o, v_w_hgrn_o, v_w_out):
    ax, ay, ac = lax.axis_index("x"), lax.axis_index("y"), lax.axis_index("c")
    chip = 2 * ax + ay
    dev = 2 * chip + ac
    xe, te = x[0], loss_target[0]
    ada_s = w_ada.shape[2]

    big_names = ("w_in", "w_pool_o", "w_hgrn_o", "w_out")
    big_w = (w_in, w_pool_o, w_hgrn_o, w_out)
    core = jnp.stack([ac]).astype(jnp.int32)
    place = jnp.stack([chip, ac]).astype(jnp.int32)
    slots = {(k, l): _cast_to_slot(place, t, l, f"cast_{k}{l}") for l in range(2) for k, t in zip(big_names, big_w)}
    w = {k: [None, None] for k in big_names}
    (w["w_in"][0],) = _run_carry(_gather_carry([slots["w_in", 0]]), "gather_w_in0")
    later = [(k, l) for l in range(2) for k in big_names[1:]]

    def landed_later(outs):
        for (k, l), o in zip(later, outs):
            w[k][l] = o

    def landed_w_in1(outs):
        (w["w_in"][1],) = outs

    ride_fwd0 = dict(proj=(_gather_carry([slots[t] for t in later]), landed_later),
                     hgrn=(_gather_carry([slots["w_in", 1]]), landed_w_in1))
    no_carry = (None, lambda outs: None)

    c_all = _gather_small(jnp.broadcast_to(c, (8, D)), "gather_c").reshape(NDEV, 8, D)[:, 0, :]
    c_pad = jnp.pad(c_all, ((0, ADA_PAD - NDEV), (0, 0)))
    b_sh = lax.dynamic_slice(b_ada, (0, chip * ada_s), (2, ada_s))
    ada_cols = _gather_small(_ada_fwd(c_pad, w_ada, b_sh), "gather_ada")
    ada_cols = ada_cols.reshape(NCHIP, 2, NDEV, 2, ada_s)[:, 0]
    ada_all = jnp.transpose(ada_cols, (2, 1, 0, 3)).reshape(2, NDEV, 3 * D)
    ada_me = lax.dynamic_slice(ada_all, (0, dev, 0), (2, 1, 3 * D))

    lbs = _lb_fwd(lb_logits)
    small = dict(g_pre=g_pre[:, None, :], g_post=g_post[:, None, :], pool_w=pool_w,
                 pool_scale=pool_scale[:, None, :], lb=lbs[:, None, :], hgrn_norm_g=hgrn_norm_g[:, None, :])

    x1, sv0 = _layer_fwd(0, xe, ada_me[0], w, small, ride_fwd0)
    x2, sv1 = _layer_fwd(1, x1, ada_me[1], w, small, dict(proj=no_carry, hgrn=no_carry))
    dx2, loss_blk = _loss_head(x2, te, "loss_head")

    parts, recv = {}, {}

    def pair_sums(keys, grads, tag):
        got = _rs_pair(grads, f"rs_pair_{tag}")
        for kl, g, o in zip(keys, grads, got):
            parts[kl] = _pair_add(core, g, o, f"rs_add_{kl[0]}{kl[1]}")

    def exchange(keys):
        def landed(outs):
            recv.update(zip(keys, outs))
        return _chips_carry([parts[kl] for kl in keys]), landed

    def early(l):
        return [(k, l) for k in big_names[1:]]

    def ride_hgrn1(big):
        pair_sums(early(1), [big[k] for k in big_names[1:]], "l1_early")
        return exchange(early(1))

    def ride_d_h1(big):
        pair_sums([("w_in", 1)], [big["w_in"]], "l1_w_in")
        return no_carry

    def ride_hgrn0(big):
        pair_sums(early(0), [big[k] for k in big_names[1:]], "l0_early")
        return exchange([("w_in", 1)] + early(0))

    def ride_d_h0(big):
        pair_sums([("w_in", 0)], [big["w_in"]], "l0_w_in")
        return exchange([("w_in", 0)])

    dx1, big1, little1 = _layer_bwd(1, dx2, sv1, w, small, dict(hgrn=ride_hgrn1, d_h=ride_d_h1))
    dx0, big0, little0 = _layer_bwd(0, dx1, sv0, w, small, dict(hgrn=ride_hgrn0, d_h=ride_d_h0))
    loss = lax.psum(loss_blk[0, 0], ("x", "y", "c"))
    red = []
    for k in big_names:
        both = _chip_sum(place, parts[k, 1], recv[k, 1], 1, None, f"rs_sum_{k}1")
        red.append(_chip_sum(place, parts[k, 0], recv[k, 0], 0, both, f"rs_sum_{k}0"))
    g_big = dict(zip(big_names, _rs_swap(red)))

    def upd(wt, g, m, v, name, carry=None):
        shp = wt.shape
        two = lambda t: t.reshape(-1, shp[-1])
        res = _adamw(two(wt), two(g), two(m), two(v), name, carry)
        return [t.reshape(shp) for t in res[:3]], res[3:]

    u_w_in, (packed,) = upd(w_in, g_big["w_in"], m_w_in, v_w_in, "adamw_w_in",
                            _gather_rows_carry(_pack_small([little0, little1])))
    packed = packed.reshape(NDEV, SMALL_ROWS, D)
    g_small = _sum_devices(packed)
    g_b_ada, g_g_pre, g_g_post, g_lb, g_pool_w, g_pool_scale, g_norm_g = _unpack_small(g_small)
    g_lb_logits = _lb_bwd(lb_logits, g_lb)
    d_ada_all = packed[:, 0:6, :].reshape(NDEV, 2, 3 * D)
    d_ada_sh = lax.dynamic_slice(jnp.transpose(d_ada_all, (1, 0, 2)), (0, 0, chip * ada_s), (2, NDEV, ada_s))
    d_ada_sh = jnp.pad(d_ada_sh, ((0, 0), (0, ADA_PAD - NDEV), (0, 0)))
    g_w_ada = _ada_wgrad(c_pad.T, d_ada_sh)

    u_w_ada, _ = upd(w_ada, g_w_ada, m_w_ada, v_w_ada, "adamw_w_ada")
    u_w_pool_o, _ = upd(w_pool_o, g_big["w_pool_o"], m_w_pool_o, v_w_pool_o, "adamw_w_pool_o")
    u_w_hgrn_o, _ = upd(w_hgrn_o, g_big["w_hgrn_o"], m_w_hgrn_o, v_w_hgrn_o, "adamw_w_hgrn_o")
    u_w_out, _ = upd(w_out, g_big["w_out"], m_w_out, v_w_out, "adamw_w_out")
    g_small_fixed = _pack_small_weights(g_b_ada, g_g_pre, g_g_post, g_lb_logits, g_pool_w, g_pool_scale,
                                        g_norm_g)
    sw = _pack_small_weights(b_ada, g_pre, g_post, lb_logits, pool_w, pool_scale, hgrn_norm_g)
    sm = _pack_small_weights(m_b_ada, m_g_pre, m_g_post, m_lb_logits, m_pool_w, m_pool_scale, m_hgrn_norm_g)
    sv = _pack_small_weights(v_b_ada, v_g_pre, v_g_post, v_lb_logits, v_pool_w, v_pool_scale, v_hgrn_norm_g)
    u_small = [_unpack_small(t) for t in _adamw(sw, g_small_fixed, sm, sv, "adamw_small")]

    grads_out = (g_w_ada, g_b_ada, g_g_pre, g_g_post, g_big["w_in"], g_pool_w, g_pool_scale, g_lb_logits,
                 g_norm_g, g_big["w_pool_o"], g_big["w_hgrn_o"], g_big["w_out"])

    def ordered(k):
        s = u_small[k]
        return (u_w_ada[k], s[0], s[1], s[2], u_w_in[k], s[4], s[5], s[3], s[6], u_w_pool_o[k], u_w_hgrn_o[k],
                u_w_out[k])

    return (loss, dx0[None], *grads_out, *ordered(0), *ordered(1), *ordered(2))
```

```python
import functools

import jax
import jax.numpy as jnp
from jax import lax
from jax.experimental import pallas as pl
from jax.experimental.pallas import tpu as pltpu

F32 = jnp.float32
BF16 = jnp.bfloat16
MESH = pl.DeviceIdType.MESH

D = 1024
HEADS = 8
HD = 128
GROUPS = 4
POOL_W = 512
WINDOWS = (2, 4, 8, 16)
CH = 64
SB = 16
NH = 2
IN_W = 7168
NCHIP = 4
NDEV = 8
EPS = 1e-6
PV0, PG0, HQ0, HF0, HI0, HG0 = 0, 4, 8, 16, 24, 32
MGP_BLK, MGH_BLK = 5, 6

LR, B1, B2, AEPS, WD, STEP = 0.001, 0.9, 0.999, 1e-08, 0.01, 10
VMEM_LIMIT = 56 * 1024 * 1024


def _cp(sem=None, **kw):
    if sem is not None:
        kw["dimension_semantics"] = sem
    return pltpu.CompilerParams(vmem_limit_bytes=VMEM_LIMIT, **kw)


def _sig(z):
    return 1.0 / (1.0 + jnp.exp(-z))


def _dsilu(z, s):
    return s * (1.0 + z * (1.0 - s))


def _row_tile(rows, cap):
    if rows <= cap:
        return rows
    t = 1 << (cap.bit_length() - 1)
    while rows % t:
        t //= 2
    return t


ANY = pl.BlockSpec(memory_space=pl.ANY)


class _Carry:
    def __init__(self, ins, outs, aliases, n_sem, start, finish):
        self.ins, self.outs, self.aliases, self.n_sem = list(ins), list(outs), dict(aliases), n_sem
        self.start, self.finish = start, finish


def _call(body, *, name, grid, in_specs, out_specs, out_shape, args, scratch_shapes=(), sem=None, carry=None):
    in_specs, out_specs, out_shape = list(in_specs), list(out_specs), list(out_shape)
    scratch_shapes = list(scratch_shapes)
    if carry is None:
        outs = pl.pallas_call(body, name=name, grid=grid, in_specs=in_specs, out_specs=out_specs,
                              out_shape=out_shape, scratch_shapes=scratch_shapes,
                              compiler_params=_cp(sem))(*args)
        return list(outs)
    n_in, n_out, n_scr = len(in_specs), len(out_specs), len(scratch_shapes)
    c_in, c_out = len(carry.ins), len(carry.outs)

    def wrapped(*refs):
        k_in, rest = refs[:n_in], refs[n_in:]
        ci, rest = rest[:c_in], rest[c_in:]
        k_out, rest = rest[:n_out], rest[n_out:]
        co, rest = rest[:c_out], rest[c_out:]
        k_scr, (ssem, rsem) = rest[:n_scr], rest[n_scr:]
        pids = [pl.program_id(d) for d in range(len(grid))]
        first = functools.reduce(jnp.logical_and, [p == 0 for p in pids])
        last = functools.reduce(jnp.logical_and, [p == g - 1 for p, g in zip(pids, grid)])

        @pl.when(first)
        def _():
            carry.start(ci, co, ssem, rsem)

        body(*k_in, *k_out, *k_scr)

        @pl.when(last)
        def _():
            carry.finish(ci, co, ssem, rsem)

    outs = pl.pallas_call(
        wrapped, name=name, grid=grid, in_specs=in_specs + [ANY] * c_in, out_specs=out_specs + [ANY] * c_out,
        out_shape=out_shape + carry.outs,
        input_output_aliases={n_in + i: n_out + o for i, o in carry.aliases.items()},
        scratch_shapes=scratch_shapes + [pltpu.SemaphoreType.DMA((carry.n_sem,))] * 2,
        compiler_params=_cp(("arbitrary",) * len(grid)),
    )(*args, *carry.ins)
    return list(outs)


def _run_carry(carry, name):
    c_in, c_out = len(carry.ins), len(carry.outs)

    def body(*refs):
        ci, co, (ssem, rsem) = refs[:c_in], refs[c_in:c_in + c_out], refs[c_in + c_out:]
        carry.start(ci, co, ssem, rsem)
        carry.finish(ci, co, ssem, rsem)

    outs = pl.pallas_call(
        body, name=name, in_specs=[ANY] * c_in, out_specs=[ANY] * c_out, out_shape=carry.outs,
        input_output_aliases=carry.aliases,
        scratch_shapes=[pltpu.SemaphoreType.DMA((carry.n_sem,))] * 2, compiler_params=_cp(),
    )(*carry.ins)
    return list(outs)


def _mm(a, b, *, name, b_mode="nn", out_shards=0, tm=1024, tn=256, tk=None, out_dtype=F32, carry=None):
    M, K = a.shape
    if b_mode == "nn":
        N = b.shape[1]
    elif b_mode == "nt":
        N = b.shape[0]
    elif b_mode == "nn_sh":
        N = b.shape[0] * b.shape[2]
    else:
        N = b.shape[1]
    tm = _row_tile(M, tm)
    if b_mode == "nn_sh":
        tn = _row_tile(b.shape[2], tn)
    elif out_shards:
        tn = _row_tile(N // out_shards, tn)
    else:
        tn = _row_tile(N, tn)
    if tk is None:
        tk = K if b_mode != "nt_shk" else b.shape[2]
    if b_mode == "nt_shk":
        tk = _row_tile(b.shape[2], tk)
    nm, nn, nk = M // tm, N // tn, K // tk

    a_spec = pl.BlockSpec((tm, tk), lambda m, n, k: (m, k))
    if b_mode == "nn":
        b_spec = pl.BlockSpec((tk, tn), lambda m, n, k: (k, n))
    elif b_mode == "nt":
        b_spec = pl.BlockSpec((tn, tk), lambda m, n, k: (n, k))
    elif b_mode == "nn_sh":
        nps = b.shape[2] // tn
        b_spec = pl.BlockSpec((None, tk, tn), lambda m, n, k: (n // nps, k, n % nps))
    else:
        kps = b.shape[2] // tk
        b_spec = pl.BlockSpec((None, tn, tk), lambda m, n, k: (k // kps, n, k % kps))
    if out_shards:
        ops = (N // out_shards) // tn
        o_spec = pl.BlockSpec((None, tm, tn), lambda m, n, k: (n // ops, m, n % ops))
        o_shape = jax.ShapeDtypeStruct((out_shards, M, N // out_shards), out_dtype)
    else:
        o_spec = pl.BlockSpec((tm, tn), lambda m, n, k: (m, n))
        o_shape = jax.ShapeDtypeStruct((M, N), out_dtype)
    trans_b = b_mode in ("nt", "nt_shk")
    dn = (((1,), (1,)), ((), ())) if trans_b else (((1,), (0,)), ((), ()))

    def body(a_ref, b_ref, o_ref, acc_ref):
        k = pl.program_id(2)

        @pl.when(k == 0)
        def _():
            acc_ref[...] = jnp.zeros(acc_ref.shape, F32)

        acc_ref[...] += lax.dot_general(a_ref[...].astype(BF16), b_ref[...].astype(BF16), dn,
                                        preferred_element_type=F32)

        @pl.when(k == nk - 1)
        def _():
            o_ref[...] = acc_ref[...].astype(o_ref.dtype)

    outs = _call(body, name=name, grid=(nm, nn, nk), in_specs=[a_spec, b_spec], out_specs=[o_spec],
                 out_shape=[o_shape], scratch_shapes=[pltpu.VMEM((tm, tn), F32)],
                 sem=("parallel", "parallel", "arbitrary"), args=(a, b), carry=carry)
    return outs[0] if carry is None else (outs[0], outs[1:])


def _rowvec(n=D):
    return pl.BlockSpec((1, n), lambda i: (0, 0))


def _prenorm_fwd(x, g, scale, shift, name):
    S = x.shape[0]
    tr = _row_tile(S, 256)

    def body(x_ref, g_ref, sc_ref, sh_ref, h_ref, ht_ref):
        xv = x_ref[...]
        r = lax.rsqrt(jnp.mean(xv * xv, axis=-1, keepdims=True) + EPS)
        hv = (xv * r) * g_ref[...] * (1.0 + sc_ref[...]) + sh_ref[...]
        h_ref[...] = hv.astype(BF16)
        ht_ref[...] = hv.T.astype(BF16)

    return pl.pallas_call(
        body, name=name, grid=(S // tr,),
        in_specs=[pl.BlockSpec((tr, D), lambda i: (i, 0)), _rowvec(), _rowvec(), _rowvec()],
        out_specs=[pl.BlockSpec((tr, D), lambda i: (i, 0)), pl.BlockSpec((D, tr), lambda i: (0, i))],
        out_shape=[jax.ShapeDtypeStruct((S, D), BF16), jax.ShapeDtypeStruct((D, S), BF16)],
        compiler_params=_cp(("parallel",)),
    )(x, g, scale, shift)


def _prenorm_bwd(dh, dxn, x, g, scale, name):
    S = x.shape[0]
    tr = _row_tile(S, 256)

    def body(dh_ref, dxn_ref, x_ref, g_ref, sc_ref, dx_ref, dsh_ref, dsc_ref, dg_ref):
        i = pl.program_id(0)

        @pl.when(i == 0)
        def _():
            dsh_ref[...] = jnp.zeros((1, D), F32)
            dsc_ref[...] = jnp.zeros((1, D), F32)
            dg_ref[...] = jnp.zeros((1, D), F32)

        xv = x_ref[...]
        dhv = dh_ref[...]
        gv = g_ref[...]
        mod = 1.0 + sc_ref[...]
        r = lax.rsqrt(jnp.mean(xv * xv, axis=-1, keepdims=True) + EPS)
        xh = xv * r
        dsh_ref[...] += jnp.sum(dhv, axis=0, keepdims=True)
        dsc_ref[...] += jnp.sum(dhv * (xh * gv), axis=0, keepdims=True)
        dg_ref[...] += jnp.sum(dhv * mod * xh, axis=0, keepdims=True)
        u = dhv * mod * gv
        dx_ref[...] = dxn_ref[...] + r * u - xv * (r * r * r) * jnp.mean(u * xv, axis=-1, keepdims=True)

    tile = pl.BlockSpec((tr, D), lambda i: (i, 0))
    return pl.pallas_call(
        body, name=name, grid=(S // tr,),
        in_specs=[tile, tile, tile, _rowvec(), _rowvec()],
        out_specs=[tile, _rowvec(), _rowvec(), _rowvec()],
        out_shape=[jax.ShapeDtypeStruct((S, D), F32)] + [jax.ShapeDtypeStruct((1, D), F32)] * 3,
        compiler_params=_cp(("arbitrary",)),
    )(dh, dxn, x, g, scale)


def _postnorm_fwd(x, y, gate, g, name):
    S = x.shape[0]
    tr = _row_tile(S, 256)

    def body(x_ref, y_ref, gate_ref, g_ref, o_ref):
        yv = y_ref[...]
        r = lax.rsqrt(jnp.mean(yv * yv, axis=-1, keepdims=True) + EPS)
        o_ref[...] = x_ref[...] + gate_ref[...] * ((yv * r) * g_ref[...])

    tile = pl.BlockSpec((tr, D), lambda i: (i, 0))
    return pl.pallas_call(
        body, name=name, grid=(S // tr,), in_specs=[tile, tile, _rowvec(), _rowvec()],
        out_specs=tile, out_shape=jax.ShapeDtypeStruct((S, D), F32), compiler_params=_cp(("parallel",)),
    )(x, y, gate, g)


def _postnorm_bwd(dxn, y, gate, g, name):
    S = y.shape[0]
    tr = _row_tile(S, 256)

    def body(dxn_ref, y_ref, gate_ref, g_ref, dy_ref, dgate_ref, dg_ref):
        i = pl.program_id(0)

        @pl.when(i == 0)
        def _():
            dgate_ref[...] = jnp.zeros((1, D), F32)
            dg_ref[...] = jnp.zeros((1, D), F32)

        yv = y_ref[...]
        dv = dxn_ref[...]
        gv = g_ref[...]
        gt = gate_ref[...]
        r = lax.rsqrt(jnp.mean(yv * yv, axis=-1, keepdims=True) + EPS)
        yh = yv * r
        dgate_ref[...] += jnp.sum(dv * (yh * gv), axis=0, keepdims=True)
        dg_ref[...] += jnp.sum(dv * gt * yh, axis=0, keepdims=True)
        u = dv * gt * gv
        dy_ref[...] = (r * u - yv * (r * r * r) * jnp.mean(u * yv, axis=-1, keepdims=True)).astype(BF16)

    tile = pl.BlockSpec((tr, D), lambda i: (i, 0))
    return pl.pallas_call(
        body, name=name, grid=(S // tr,), in_specs=[tile, tile, _rowvec(), _rowvec()],
        out_specs=[tile, _rowvec(), _rowvec()],
        out_shape=[jax.ShapeDtypeStruct((S, D), BF16), jax.ShapeDtypeStruct((1, D), F32),
                   jax.ShapeDtypeStruct((1, D), F32)],
        compiler_params=_cp(("arbitrary",)),
    )(dxn, y, gate, g)


def _loss_head(xo, target, name):
    S = xo.shape[0]
    tr = _row_tile(S, 256)

    def body(x_ref, t_ref, dx_ref, l_ref):
        i = pl.program_id(0)

        @pl.when(i == 0)
        def _():
            l_ref[...] = jnp.zeros((8, 128), F32)

        err = x_ref[...] - t_ref[...]
        dx_ref[...] = err * (1.0 / D)
        l_ref[...] += 0.5 * jnp.sum(jnp.mean(err * err, axis=-1, keepdims=True))

    tile = pl.BlockSpec((tr, D), lambda i: (i, 0))
    return pl.pallas_call(
        body, name=name, grid=(S // tr,), in_specs=[tile, tile],
        out_specs=[tile, pl.BlockSpec((8, 128), lambda i: (0, 0))],
        out_shape=[jax.ShapeDtypeStruct((S, D), F32), jax.ShapeDtypeStruct((8, 128), F32)],
        compiler_params=_cp(("arbitrary",)),
    )(xo, target)


def _layer_tail_fwd(proj, a_in, b_in, x, w_po, w_ho, w_out, gate, g, name):
    S = proj.shape[0]
    tr = _row_tile(S, 256)
    nsh, _, wsh = w_po.shape

    def body(mgp_ref, mgh_ref, a_ref, b_ref, x_ref, wpo_ref, who_ref, wout_ref, gate_ref, g_ref,
             bra_ref, brb_ref, mt_ref, y_ref, xn_ref):
        av = a_ref[...]
        bra = jnp.concatenate([jnp.dot(av, wpo_ref[j], preferred_element_type=F32) for j in range(nsh)], axis=1)
        brb = jnp.dot(b_ref[...], who_ref[...], preferred_element_type=F32)
        mv = _sig(mgp_ref[...]) * bra + _sig(mgh_ref[...]) * brb
        bra_ref[...] = bra.astype(BF16)
        brb_ref[...] = brb.astype(BF16)
        mt_ref[...] = mv.T.astype(BF16)
        yv = jnp.dot(mv.astype(BF16), wout_ref[...], preferred_element_type=F32)
        y_ref[...] = yv
        r = lax.rsqrt(jnp.mean(yv * yv, axis=-1, keepdims=True) + EPS)
        xn_ref[...] = x_ref[...] + gate_ref[...] * ((yv * r) * g_ref[...])

    tile = pl.BlockSpec((tr, D), lambda i: (i, 0))
    whole = lambda t: pl.BlockSpec(t.shape, lambda i: (0,) * t.ndim)
    return pl.pallas_call(
        body, name=name, grid=(S // tr,),
        in_specs=[pl.BlockSpec((tr, D), lambda i: (i, MGP_BLK)), pl.BlockSpec((tr, D), lambda i: (i, MGH_BLK)),
                  pl.BlockSpec((tr, POOL_W), lambda i: (i, 0)), tile, tile, whole(w_po), whole(w_ho),
                  whole(w_out), _rowvec(), _rowvec()],
        out_specs=[tile, tile, pl.BlockSpec((D, tr), lambda i: (0, i)), tile, tile],
        out_shape=[jax.ShapeDtypeStruct((S, D), BF16), jax.ShapeDtypeStruct((S, D), BF16),
                   jax.ShapeDtypeStruct((D, S), BF16), jax.ShapeDtypeStruct((S, D), F32),
                   jax.ShapeDtypeStruct((S, D), F32)],
        compiler_params=_cp(("parallel",)),
    )(proj, proj, a_in, b_in, x, w_po, w_ho, w_out, gate, g)


def _layer_head_bwd(dxn, y, proj, br_a, br_b, w_po, w_ho, w_out, gate, g, name):
    S = y.shape[0]
    tr = _row_tile(S, 256)
    nsh, _, wsh = w_po.shape

    def body(dxn_ref, y_ref, mgp_ref, mgh_ref, bra_ref, brb_ref, wpo_ref, who_ref, wout_ref, gate_ref, g_ref,
             dy_ref, dba_ref, dbb_ref, dmg_ref, dain_ref, dbin_ref, dgate_ref, dg_ref):
        i = pl.program_id(0)

        @pl.when(i == 0)
        def _():
            dgate_ref[...] = jnp.zeros((1, D), F32)
            dg_ref[...] = jnp.zeros((1, D), F32)

        yv = y_ref[...]
        dv = dxn_ref[...]
        gv = g_ref[...]
        gt = gate_ref[...]
        r = lax.rsqrt(jnp.mean(yv * yv, axis=-1, keepdims=True) + EPS)
        yh = yv * r
        dgate_ref[...] += jnp.sum(dv * (yh * gv), axis=0, keepdims=True)
        dg_ref[...] += jnp.sum(dv * gt * yh, axis=0, keepdims=True)
        u = dv * gt * gv
        dy = (r * u - yv * (r * r * r) * jnp.mean(u * yv, axis=-1, keepdims=True)).astype(BF16)
        dy_ref[...] = dy
        dm = _dot_nt(dy, wout_ref[...])
        sp = _sig(mgp_ref[...])
        sh = _sig(mgh_ref[...])
        dba = (dm * sp).astype(BF16)
        dbb = (dm * sh).astype(BF16)
        dba_ref[...] = dba
        dbb_ref[...] = dbb
        dmg_ref[:, 0:D] = (dm * bra_ref[...].astype(F32) * sp * (1.0 - sp)).astype(BF16)
        dmg_ref[:, D:2 * D] = (dm * brb_ref[...].astype(F32) * sh * (1.0 - sh)).astype(BF16)
        dain = _dot_nt(dba[:, 0:wsh], wpo_ref[0])
        for j in range(1, nsh):
            dain = dain + _dot_nt(dba[:, j * wsh:(j + 1) * wsh], wpo_ref[j])
        dain_ref[...] = dain
        dbin_ref[...] = _dot_nt(dbb, who_ref[...])

    tile = pl.BlockSpec((tr, D), lambda i: (i, 0))
    whole = lambda t: pl.BlockSpec(t.shape, lambda i: (0,) * t.ndim)
    return pl.pallas_call(
        body, name=name, grid=(S // tr,),
        in_specs=[tile, tile, pl.BlockSpec((tr, D), lambda i: (i, MGP_BLK)),
                  pl.BlockSpec((tr, D), lambda i: (i, MGH_BLK)), tile, tile, whole(w_po), whole(w_ho),
                  whole(w_out), _rowvec(), _rowvec()],
        out_specs=[tile, tile, tile, pl.BlockSpec((tr, 2 * D), lambda i: (i, 0)),
                   pl.BlockSpec((tr, POOL_W), lambda i: (i, 0)), tile, _rowvec(), _rowvec()],
        out_shape=[jax.ShapeDtypeStruct((S, D), BF16)] * 3
        + [jax.ShapeDtypeStruct((S, 2 * D), BF16), jax.ShapeDtypeStruct((S, POOL_W), F32),
           jax.ShapeDtypeStruct((S, D), F32), jax.ShapeDtypeStruct((1, D), F32), jax.ShapeDtypeStruct((1, D), F32)],
        compiler_params=_cp(("arbitrary",)),
    )(dxn, y, proj, proj, br_a, br_b, w_po, w_ho, w_out, gate, g)


def _merge_fwd(proj, br_a, br_b, name):
    S = proj.shape[0]
    tr = _row_tile(S, 256)

    def body(mgp_ref, mgh_ref, a_ref, b_ref, o_ref, ot_ref):
        mv = _sig(mgp_ref[...]) * a_ref[...] + _sig(mgh_ref[...]) * b_ref[...]
        o_ref[...] = mv.astype(BF16)
        ot_ref[...] = mv.T.astype(BF16)

    tile = pl.BlockSpec((tr, D), lambda i: (i, 0))
    return pl.pallas_call(
        body, name=name, grid=(S // tr,),
        in_specs=[pl.BlockSpec((tr, D), lambda i: (i, MGP_BLK)), pl.BlockSpec((tr, D), lambda i: (i, MGH_BLK)),
                  tile, tile],
        out_specs=[tile, pl.BlockSpec((D, tr), lambda i: (0, i))],
        out_shape=[jax.ShapeDtypeStruct((S, D), BF16), jax.ShapeDtypeStruct((D, S), BF16)],
        compiler_params=_cp(("parallel",)),
    )(proj, proj, br_a, br_b)


def _merge_bwd(dm, proj, br_a, br_b, name):
    S = proj.shape[0]
    tr = _row_tile(S, 256)

    def body(dm_ref, mgp_ref, mgh_ref, a_ref, b_ref, da_ref, db_ref, dmg_ref):
        dmv = dm_ref[...]
        sp = _sig(mgp_ref[...])
        sh = _sig(mgh_ref[...])
        da_ref[...] = (dmv * sp).astype(BF16)
        db_ref[...] = (dmv * sh).astype(BF16)
        dmg_ref[:, 0:D] = (dmv * a_ref[...] * sp * (1.0 - sp)).astype(BF16)
        dmg_ref[:, D:2 * D] = (dmv * b_ref[...] * sh * (1.0 - sh)).astype(BF16)

    tile = pl.BlockSpec((tr, D), lambda i: (i, 0))
    return pl.pallas_call(
        body, name=name, grid=(S // tr,),
        in_specs=[tile, pl.BlockSpec((tr, D), lambda i: (i, MGP_BLK)),
                  pl.BlockSpec((tr, D), lambda i: (i, MGH_BLK)), tile, tile],
        out_specs=[tile, tile, pl.BlockSpec((tr, 2 * D), lambda i: (i, 0))],
        out_shape=[jax.ShapeDtypeStruct((S, D), BF16), jax.ShapeDtypeStruct((S, D), BF16),
                   jax.ShapeDtypeStruct((S, 2 * D), BF16)],
        compiler_params=_cp(("parallel",)),
    )(dm, proj, proj, br_a, br_b)


def _pool_pieces(u, g, S):
    rowi = lax.broadcasted_iota(jnp.int32, (S, 1), 0)

    def down(z, k):
        return jnp.where(rowi >= k, pltpu.roll(z, k, axis=0), 0.0)

    s2 = u + down(u, 1)
    s4 = s2 + down(s2, 2)
    s8 = s4 + down(s4, 4)
    s16 = s8 + down(s8, 8)
    win = jnp.where(g == 0, s2, jnp.where(g == 1, s4, jnp.where(g == 2, s8, s16)))
    w = jnp.where(g == 0, 2, jnp.where(g == 1, 4, jnp.where(g == 2, 8, 16)))
    count = jnp.minimum(rowi + 1, w).astype(F32)
    return win / count - u, count, rowi


def _pool_fwd(proj, pw, pscale, name):
    S = proj.shape[0]

    def body(pv_ref, pg_ref, pw_ref, sc_ref, a_ref, at_ref):
        g = pl.program_id(0)
        pooled, _, _ = _pool_pieces(pv_ref[...], g, S)
        pm = jnp.dot(pooled.astype(BF16), pw_ref[...].astype(BF16), preferred_element_type=F32)
        pgv = pg_ref[...]
        av = pm * sc_ref[...] * (pgv * _sig(pgv))
        a_ref[...] = av.astype(BF16)
        at_ref[...] = av.T.astype(BF16)

    return pl.pallas_call(
        body, name=name, grid=(GROUPS,),
        in_specs=[pl.BlockSpec((S, 128), lambda g: (0, PV0 + g)), pl.BlockSpec((S, 128), lambda g: (0, PG0 + g)),
                  pl.BlockSpec((None, 128, 128), lambda g: (g, 0, 0)), pl.BlockSpec((1, 128), lambda g: (0, g))],
        out_specs=[pl.BlockSpec((S, 128), lambda g: (0, g)), pl.BlockSpec((128, S), lambda g: (g, 0))],
        out_shape=[jax.ShapeDtypeStruct((S, POOL_W), BF16), jax.ShapeDtypeStruct((POOL_W, S), BF16)],
        compiler_params=_cp(("parallel",)),
    )(proj, proj, pw, pscale)


def _pool_bwd(da, proj, pw, pscale, name):
    S = proj.shape[0]

    def body(da_ref, pv_ref, pg_ref, pw_ref, sc_ref, dpv_ref, dpg_ref, dpw_ref, dsc_ref):
        g = pl.program_id(0)
        pooled, count, rowi = _pool_pieces(pv_ref[...], g, S)
        pwb = pw_ref[...].astype(BF16)
        pm = jnp.dot(pooled.astype(BF16), pwb, preferred_element_type=F32)
        scv = sc_ref[...]
        pgv = pg_ref[...]
        sg = _sig(pgv)
        dav = da_ref[...]
        d_ps = dav * (pgv * sg)
        dpg_ref[...] = (dav * (pm * scv) * _dsilu(pgv, sg)).astype(BF16)
        dsc_ref[...] = jnp.sum(d_ps * pm, axis=0, keepdims=True)
        d_pm = (d_ps * scv).astype(BF16)
        dpw_ref[...] = lax.dot_general(pooled.astype(BF16), d_pm, (((0,), (0,)), ((), ())),
                                       preferred_element_type=F32)
        d_pooled = lax.dot_general(d_pm, pwb, (((1,), (1,)), ((), ())), preferred_element_type=F32)
        z = d_pooled / count

        def up(v, k):
            return jnp.where(rowi < S - k, pltpu.roll(v, S - k, axis=0), 0.0)

        t2 = z + up(z, 1)
        t4 = t2 + up(t2, 2)
        t8 = t4 + up(t4, 4)
        t16 = t8 + up(t8, 8)
        adj = jnp.where(g == 0, t2, jnp.where(g == 1, t4, jnp.where(g == 2, t8, t16)))
        dpv_ref[...] = (adj - d_pooled).astype(BF16)

    col = lambda g: (0, g)
    return pl.pallas_call(
        body, name=name, grid=(GROUPS,),
        in_specs=[pl.BlockSpec((S, 128), col), pl.BlockSpec((S, 128), lambda g: (0, PV0 + g)),
                  pl.BlockSpec((S, 128), lambda g: (0, PG0 + g)),
                  pl.BlockSpec((None, 128, 128), lambda g: (g, 0, 0)), pl.BlockSpec((1, 128), col)],
        out_specs=[pl.BlockSpec((S, 128), col), pl.BlockSpec((S, 128), col),
                   pl.BlockSpec((None, 128, 128), lambda g: (g, 0, 0)), pl.BlockSpec((1, 128), col)],
        out_shape=[jax.ShapeDtypeStruct((S, POOL_W), BF16), jax.ShapeDtypeStruct((S, POOL_W), BF16),
                   jax.ShapeDtypeStruct((GROUPS, 128, 128), F32), jax.ShapeDtypeStruct((1, POOL_W), F32)],
        compiler_params=_cp(("parallel",)),
    )(da, proj, proj, pw, pscale)


def _chunk_cumsum(z, rowi):
    for sh in (1, 2, 4, 8, 16, 32):
        z = z + jnp.where(rowi >= sh, pltpu.roll(z, sh, axis=0), 0.0)
    return z


def _chunk_rev_cumsum(z, rowi):
    for sh in (1, 2, 4, 8, 16, 32):
        z = z + jnp.where(rowi < CH - sh, pltpu.roll(z, CH - sh, axis=0), 0.0)
    return z


def _dot_nn(a, b):
    return jnp.dot(a.astype(BF16), b.astype(BF16), preferred_element_type=F32)


def _dot_nt(a, b):
    return lax.dot_general(a.astype(BF16), b.astype(BF16), (((1,), (1,)), ((), ())), preferred_element_type=F32)


def _dot_tn(a, b):
    return lax.dot_general(a.astype(BF16), b.astype(BF16), (((0,), (0,)), ((), ())), preferred_element_type=F32)


def _gates(hq, hf, lbv):
    sq = _sig(hq)
    sf = _sig(hf)
    f = lbv + (1.0 - lbv) * sf
    fc = jnp.maximum(f, 1e-30)
    return hq * sq, sq, sf, f, fc, jnp.log(fc)


DECAY_CAP = 60.0


def _block_ref(c_ref, i):
    if i == 0:
        return jnp.zeros((1, HD), F32)
    return c_ref[SB * i - 1:SB * i, :]


def _block_decay(c_ref):
    spans = [_block_ref(c_ref, i) - c_ref[SB * (i + 1) - 1:SB * (i + 1), :] for i in range(CH // SB)]
    return functools.reduce(jnp.maximum, spans)


def _hgrn_fwd(proj, lb, gn, name, carry=None):
    S = proj.shape[0]
    nch = S // CH
    W = NH * HD

    def body(hq_ref, hf_ref, hi_ref, hg_ref, lb_ref, gn_ref, bin_ref, bint_ref, oraw_ref, st_ref, mild_ref,
             q_s, k_s, c_s, v_s, o_s, state_s, qf_s, kf_s, cf_s):
        state_s[...] = jnp.zeros((NH, HD, HD), F32)
        rowi = lax.broadcasted_iota(jnp.int32, (CH, 1), 0)
        coli = lax.broadcasted_iota(jnp.int32, (1, CH), 1)
        sbi = lax.broadcasted_iota(jnp.int32, (SB, 1), 0)
        gnv = gn_ref[...]

        def gates_pass(n, worst):
            rows = pl.ds(pl.multiple_of(n * CH, CH), CH)
            for hh in range(NH):
                lanes = slice(hh * HD, (hh + 1) * HD)
                q, _, _, f, _, logf = _gates(hq_ref[rows, lanes], hf_ref[rows, lanes], lb_ref[:, lanes])
                c = _chunk_cumsum(logf, rowi)
                qf_s[hh, rows, :] = q
                kf_s[hh, rows, :] = 1.0 - f
                cf_s[hh, rows, :] = c
                c_s[hh] = c
                worst = jnp.maximum(worst, _block_decay(c_s.at[hh]))
            return worst

        def between_chunks(hh, n, rows):
            lanes = slice(hh * HD, (hh + 1) * HD)
            q = qf_s[hh, rows, :]
            k = kf_s[hh, rows, :]
            c = cf_s[hh, rows, :]
            v = hi_ref[rows, lanes]
            q_s[hh] = q
            k_s[hh] = k
            c_s[hh] = c
            v_s[hh] = v
            st = state_s[hh]
            st_ref[hh, n] = st.astype(BF16)
            o_s[hh] = _dot_nt(q * jnp.exp(c), st)
            last = c_s[hh, CH - 1:CH, :]
            state_s[hh] = st * jnp.exp(last) + _dot_tn(v, k * jnp.exp(last - c))

        def within_chunk_matmul(hh):
            q, k, c, v = q_s[hh], k_s[hh], c_s[hh], v_s[hh]
            a = jnp.zeros((CH, CH), F32)
            for i in range(CH // SB):
                r_i = _block_ref(c_s.at[hh], i)
                qi = q * jnp.exp(jnp.minimum(c - r_i, 0.0))
                kei = k * jnp.exp(jnp.minimum(r_i - c, DECAY_CAP))
                m_i = (rowi >= SB * i) & (rowi < SB * (i + 1)) & (coli <= rowi)
                a = a + jnp.where(m_i, _dot_nt(qi, kei), 0.0)
            o_s[hh] += _dot_nn(a, v)

        def within_chunk_exact(hh):
            q, k, c, v = q_s[hh], k_s[hh], c_s[hh], v_s[hh]
            a_off = jnp.zeros((CH, CH), F32)
            for i in range(1, CH // SB):
                r_i = _block_ref(c_s.at[hh], i)
                qi = q * jnp.exp(jnp.minimum(c - r_i, 0.0))
                kei = k * jnp.exp(jnp.minimum(r_i - c, 0.0))
                m_i = (rowi >= SB * i) & (rowi < SB * (i + 1)) & (coli < SB * i)
                a_off = a_off + jnp.where(m_i, _dot_nt(qi, kei), 0.0)
            o_s[hh] += _dot_nn(a_off, v)
            for i in range(CH // SB):
                blk = slice(SB * i, SB * (i + 1))
                qb = q_s[hh, blk, :]
                cb = c_s[hh, blk, :]
                acc = jnp.zeros((SB, HD), F32)
                for s in range(SB):
                    row = SB * i + s
                    w = jnp.exp(jnp.minimum(cb - c_s[hh, row:row + 1, :], 0.0))
                    a_col = jnp.sum(qb * k_s[hh, row:row + 1, :] * w, axis=-1, keepdims=True)
                    acc = acc + jnp.where(sbi >= s, a_col, 0.0) * v_s[hh, row:row + 1, :]
                o_s[hh, blk, :] += acc

        def norm_and_gate(hh, rows):
            lanes = slice(hh * HD, (hh + 1) * HD)
            ov = o_s[hh]
            oraw_ref[rows, lanes] = ov
            r = lax.rsqrt(jnp.mean(ov * ov, axis=-1, keepdims=True) + EPS)
            hg = hg_ref[rows, lanes]
            bin_ref[rows, lanes] = ((ov * r) * gnv * (hg * _sig(hg))).astype(BF16)

        def chunk_with(within_chunk):
            def chunk(n, carry):
                rows = pl.ds(pl.multiple_of(n * CH, CH), CH)
                for hh in range(NH):
                    between_chunks(hh, n, rows)
                for hh in range(NH):
                    within_chunk(hh)
                for hh in range(NH):
                    norm_and_gate(hh, rows)
                return carry
            return chunk

        worst = lax.fori_loop(0, nch, gates_pass, jnp.zeros((1, HD), F32))
        mild = jnp.max(worst) <= DECAY_CAP
        mild_ref[...] = jnp.broadcast_to(jnp.where(mild, 1.0, 0.0), (8, HD))

        @pl.when(mild)
        def _():
            lax.fori_loop(0, nch, chunk_with(within_chunk_matmul), 0, unroll=4)

        @pl.when(jnp.logical_not(mild))
        def _():
            lax.fori_loop(0, nch, chunk_with(within_chunk_exact), 0)

        bint_ref[...] = bin_ref[...].astype(F32).T.astype(BF16)

    col = lambda off: pl.BlockSpec((S, W), lambda h: (0, off // NH + h))
    head = pl.BlockSpec((S, W), lambda h: (0, h))
    outs = _call(
        body, name=name, grid=(HEADS // NH,),
        in_specs=[col(HQ0), col(HF0), col(HI0), col(HG0), pl.BlockSpec((1, W), lambda h: (0, h)),
                  pl.BlockSpec((1, HD), lambda h: (0, 0))],
        out_specs=[head, pl.BlockSpec((W, S), lambda h: (h, 0)), head,
                   pl.BlockSpec((NH, nch, HD, HD), lambda h: (h, 0, 0, 0)),
                   pl.BlockSpec((8, HD), lambda h: (h, 0))],
        out_shape=[jax.ShapeDtypeStruct((S, D), BF16), jax.ShapeDtypeStruct((D, S), BF16),
                   jax.ShapeDtypeStruct((S, D), F32), jax.ShapeDtypeStruct((HEADS, nch, HD, HD), BF16),
                   jax.ShapeDtypeStruct((8 * HEADS // NH, HD), F32)],
        scratch_shapes=[pltpu.VMEM((NH, CH, HD), F32)] * 5 + [pltpu.VMEM((NH, HD, HD), F32)]
        + [pltpu.VMEM((NH, S, HD), F32)] * 3,
        sem=("parallel",), args=(proj, proj, proj, proj, lb, gn), carry=carry)
    return outs[:5], outs[5:]


def _hgrn_bwd(dbin, proj, oraw, states, mild, lb, gn, name, carry=None):
    S = proj.shape[0]
    nch = S // CH
    W = NH * HD

    def body(db_ref, hq_ref, hf_ref, hi_ref, hg_ref, or_ref, st_ref, mild_ref, lb_ref, gn_ref,
             dq_ref, df_ref, di_ref, dg_ref, dlb_ref, dgn_ref,
             q_s, k_s, c_s, v_s, do_s, dq_s, dk_s, dv_s, dc_s, dqd_s, dkd_s, dl_s, dst_s, dlb_s, dgn_s):
        dst_s[...] = jnp.zeros((NH, HD, HD), F32)
        dlb_s[...] = jnp.zeros((1, W), F32)
        dgn_s[...] = jnp.zeros((1, HD), F32)
        rowi = lax.broadcasted_iota(jnp.int32, (CH, 1), 0)
        rowi2 = lax.broadcasted_iota(jnp.int32, (CH, CH), 0)
        coli2 = lax.broadcasted_iota(jnp.int32, (CH, CH), 1)
        sbi = lax.broadcasted_iota(jnp.int32, (SB, 1), 0)
        gnv = gn_ref[...]
        def between_chunks(hh, n, rows):
            lanes = slice(hh * HD, (hh + 1) * HD)
            q, _, _, f, _, logf = _gates(hq_ref[rows, lanes], hf_ref[rows, lanes], lb_ref[:, lanes])
            k = 1.0 - f
            v = hi_ref[rows, lanes]
            c = _chunk_cumsum(logf, rowi)
            ov = or_ref[rows, lanes]
            hg = hg_ref[rows, lanes]
            sg = _sig(hg)
            r = lax.rsqrt(jnp.mean(ov * ov, axis=-1, keepdims=True) + EPS)
            dbv = db_ref[rows, lanes]
            d_on = dbv * (hg * sg)
            dg_ref[rows, lanes] = (dbv * ((ov * r) * gnv) * _dsilu(hg, sg)).astype(BF16)
            dgn_s[...] += jnp.sum(d_on * (ov * r), axis=0, keepdims=True)
            u = d_on * gnv
            do = r * u - ov * (r * r * r) * jnp.mean(u * ov, axis=-1, keepdims=True)
            q_s[hh] = q
            k_s[hh] = k
            c_s[hh] = c
            v_s[hh] = v
            do_s[hh] = do
            st = st_ref[hh, n].astype(F32)
            dst = dst_s[hh]
            ec = jnp.exp(c)
            last = c_s[hh, CH - 1:CH, :]
            el = jnp.exp(last - c)
            elast = jnp.exp(last)
            dq = _dot_nn(do, st) * ec
            dk = _dot_nn(v, dst) * el
            dq_s[hh] = dq
            dk_s[hh] = dk
            dv_s[hh] = _dot_nt(k * el, dst)
            dc_s[hh] = q * dq - k * dk
            dl_s[hh] = (jnp.sum(k * dk, axis=0, keepdims=True)
                        + elast * jnp.sum(st * dst, axis=0, keepdims=True))
            dst_s[hh] = dst * elast + _dot_tn(do, q * ec)

        def pairs_matmul(hh, first, cap, strict):
            q, k, c, v, do = q_s[hh], k_s[hh], c_s[hh], v_s[hh], do_s[hh]
            d_a = _dot_nt(do, v).astype(BF16).astype(F32)
            d_at = d_a.T
            at = jnp.zeros((CH, CH), F32)
            dq, dk, dcum = dq_s[hh], dk_s[hh], dc_s[hh]
            for i in range(first, CH // SB):
                r_i = _block_ref(c_s.at[hh], i)
                eq = jnp.exp(jnp.minimum(c - r_i, 0.0))
                ek = jnp.exp(jnp.minimum(r_i - c, cap))
                qi = (q * eq).astype(BF16).astype(F32)
                kei = (k * ek).astype(BF16).astype(F32)
                in_t = (rowi2 >= SB * i) & (rowi2 < SB * (i + 1))
                in_s = (coli2 >= SB * i) & (coli2 < SB * (i + 1))
                m_ts = in_t & ((coli2 < SB * i) if strict else (coli2 <= rowi2))
                m_st = in_s & ((rowi2 < SB * i) if strict else (rowi2 <= coli2))
                at = at + jnp.where(m_st, _dot_nt(kei, qi), 0.0)
                dq_i = _dot_nn(jnp.where(m_ts, d_a, 0.0), kei)
                dk_i = _dot_nn(jnp.where(m_st, d_at, 0.0), qi)
                dq = dq + dq_i * eq
                dk = dk + dk_i * ek
                dcum = dcum + (qi * dq_i - kei * dk_i)
            dq_s[hh] = dq
            dk_s[hh] = dk
            dc_s[hh] = dcum
            dv_s[hh] += _dot_nn(at, do)

        def pairs_exact(hh):
            dqd_s[hh] = jnp.zeros((CH, HD), F32)
            dkd_s[hh] = jnp.zeros((CH, HD), F32)
            for i in range(CH // SB):
                blk = slice(SB * i, SB * (i + 1))
                qb = q_s[hh, blk, :]
                cb = c_s[hh, blk, :]
                dob = do_s[hh, blk, :]
                dq_acc = jnp.zeros((SB, HD), F32)
                for s in range(SB):
                    row = SB * i + s
                    ks = k_s[hh, row:row + 1, :]
                    vs = v_s[hh, row:row + 1, :]
                    w = jnp.exp(jnp.minimum(cb - c_s[hh, row:row + 1, :], 0.0))
                    live = sbi >= s
                    a_col = jnp.where(live, jnp.sum(qb * ks * w, axis=-1, keepdims=True), 0.0)
                    da_col = jnp.where(live, jnp.sum(dob * vs, axis=-1, keepdims=True), 0.0)
                    dq_acc = dq_acc + da_col * ks * w
                    dkd_s[hh, row:row + 1, :] += jnp.sum(da_col * qb * w, axis=0, keepdims=True)
                    dv_s[hh, row:row + 1, :] += jnp.sum(a_col * dob, axis=0, keepdims=True)
                dqd_s[hh, blk, :] += dq_acc
            dq_d = dqd_s[hh]
            dk_d = dkd_s[hh]
            dq_s[hh] += dq_d
            dk_s[hh] += dk_d
            dc_s[hh] += q_s[hh] * dq_d - k_s[hh] * dk_d

        def gate_grads(hh, rows):
            lanes = slice(hh * HD, (hh + 1) * HD)
            lbv = lb_ref[:, lanes]
            hq = hq_ref[rows, lanes]
            _, sq, sf, f, fc, _ = _gates(hq, hf_ref[rows, lanes], lbv)
            dlogf = _chunk_rev_cumsum(dc_s[hh], rowi) + dl_s[hh]
            dfv = jnp.where(f > 1e-30, dlogf / fc, 0.0) - dk_s[hh]
            dlb_s[:, lanes] += jnp.sum(dfv * (1.0 - sf), axis=0, keepdims=True)
            df_ref[rows, lanes] = (dfv * (1.0 - lbv) * sf * (1.0 - sf)).astype(BF16)
            dq_ref[rows, lanes] = (dq_s[hh] * _dsilu(hq, sq)).astype(BF16)
            di_ref[rows, lanes] = dv_s[hh].astype(BF16)

        def chunk_with(pairs):
            def chunk(j, carry):
                n = nch - 1 - j
                rows = pl.ds(pl.multiple_of(n * CH, CH), CH)
                for hh in range(NH):
                    between_chunks(hh, n, rows)
                for hh in range(NH):
                    pairs(hh)
                for hh in range(NH):
                    gate_grads(hh, rows)
                return carry
            return chunk

        def pairs_mild(hh):
            pairs_matmul(hh, 0, DECAY_CAP, strict=False)

        def pairs_any(hh):
            pairs_matmul(hh, 1, 0.0, strict=True)
            pairs_exact(hh)

        mild = jnp.max(mild_ref[...]) > 0.5

        @pl.when(mild)
        def _():
            lax.fori_loop(0, nch, chunk_with(pairs_mild), 0, unroll=2)

        @pl.when(jnp.logical_not(mild))
        def _():
            lax.fori_loop(0, nch, chunk_with(pairs_any), 0)

        dlb_ref[...] = dlb_s[...]
        dgn_ref[...] = jnp.broadcast_to(dgn_s[...], (8, HD))

    col = lambda off: pl.BlockSpec((S, W), lambda h: (0, off // NH + h))
    head = pl.BlockSpec((S, W), lambda h: (0, h))
    vec = pl.BlockSpec((1, W), lambda h: (0, h))
    outs = _call(
        body, name=name, grid=(HEADS // NH,),
        in_specs=[head, col(HQ0), col(HF0), col(HI0), col(HG0), head,
                  pl.BlockSpec((NH, nch, HD, HD), lambda h: (h, 0, 0, 0)),
                  pl.BlockSpec((8, HD), lambda h: (h, 0)), vec, pl.BlockSpec((1, HD), lambda h: (0, 0))],
        out_specs=[head, head, head, head, vec, pl.BlockSpec((8, HD), lambda h: (h, 0))],
        out_shape=[jax.ShapeDtypeStruct((S, D), BF16)] * 4
        + [jax.ShapeDtypeStruct((1, D), F32), jax.ShapeDtypeStruct((8 * HEADS // NH, HD), F32)],
        scratch_shapes=[pltpu.VMEM((NH, CH, HD), F32)] * 11
        + [pltpu.VMEM((NH, 1, HD), F32), pltpu.VMEM((NH, HD, HD), F32), pltpu.VMEM((1, W), F32),
           pltpu.VMEM((1, HD), F32)],
        sem=("parallel",), args=(dbin, proj, proj, proj, proj, oraw, states, mild, lb, gn), carry=carry)
    dq, df, di, dg, dlb, dgn = outs[:6]
    return (dq, df, di, dg, dlb, dgn.reshape(HEADS // NH, 8, HD)[:, 0, :]), outs[6:]


def _lower_bounds(l0, l1):
    m = jnp.maximum(l0, l1)
    e0 = jnp.exp(l0 - m)
    e1 = jnp.exp(l1 - m)
    tot = e0 + e1
    p0 = e0 / tot
    p1 = e1 / tot
    return jnp.clip(p0 - p0, 0.0, 1.0), jnp.clip((p0 + p1) - p0, 0.0, 1.0)


def _lb_fwd(logits):
    def body(l_ref, o_ref):
        lb0, lb1 = _lower_bounds(l_ref[0:1, :], l_ref[1:2, :])
        o_ref[0:1, :] = lb0
        o_ref[1:2, :] = lb1

    return pl.pallas_call(body, name="lb_fwd", out_shape=jax.ShapeDtypeStruct((2, D), F32))(logits)


def _lb_bwd(logits, dlb):
    def body(l_ref, d_ref, o_ref):
        _, vjp = jax.vjp(_lower_bounds, l_ref[0:1, :], l_ref[1:2, :])
        g0, g1 = vjp((d_ref[0:1, :], d_ref[1:2, :]))
        o_ref[0:1, :] = g0
        o_ref[1:2, :] = g1

    return pl.pallas_call(body, name="lb_bwd", out_shape=jax.ShapeDtypeStruct((2, D), F32))(logits, dlb)


ADA_PAD = 128


def _ada_fwd(c_pad, w_ada, b_sh):
    ns = w_ada.shape[2]

    def body(c_ref, w_ref, b_ref, o_ref):
        cv = c_ref[...]
        ca = (cv * _sig(cv)).astype(BF16)
        for l in range(2):
            res = jnp.dot(ca, w_ref[l].astype(BF16), preferred_element_type=F32)
            o_ref[:, l * ns:(l + 1) * ns] = res[0:NDEV, :] + b_ref[l:l + 1, :]

    return pl.pallas_call(body, name="ada_fwd", out_shape=jax.ShapeDtypeStruct((NDEV, 2 * ns), F32),
                          compiler_params=_cp())(c_pad, w_ada, b_sh)


def _ada_wgrad(c_pad_t, d_ada_sh):
    ns = d_ada_sh.shape[2]

    def body(c_ref, d_ref, o_ref):
        cv = c_ref[...]
        ca = (cv * _sig(cv)).astype(BF16)
        for l in range(2):
            o_ref[l] = jnp.dot(ca, d_ref[l].astype(BF16), preferred_element_type=F32)

    return pl.pallas_call(body, name="ada_wgrad", out_shape=jax.ShapeDtypeStruct((2, D, ns), F32),
                          compiler_params=_cp())(c_pad_t, d_ada_sh)


def _sum_devices(g):
    _, R, C = g.shape

    def body(g_ref, o_ref):
        acc = g_ref[0]
        for d in range(1, NDEV):
            acc = acc + g_ref[d]
        o_ref[...] = acc

    return pl.pallas_call(body, name="sum_devices", out_shape=jax.ShapeDtypeStruct((R, C), F32),
                          compiler_params=_cp())(g)


def _adamw(w, g, m, v, name, carry=None):
    R, C = w.shape
    tr = _row_tile(R, max(8, (1 << 19) // C))

    def body(w_ref, g_ref, m_ref, v_ref, d_ref, nm_ref, nv_ref):
        d_ref[...], nm_ref[...], nv_ref[...] = _adamw_update(w_ref[...], g_ref[...], m_ref[...], v_ref[...])

    tile = pl.BlockSpec((tr, C), lambda i: (i, 0))
    return _call(body, name=name, grid=(R // tr,), in_specs=[tile] * 4, out_specs=[tile] * 3,
                 out_shape=[jax.ShapeDtypeStruct((R, C), F32)] * 3, sem=("parallel",), args=(w, g, m, v),
                 carry=carry)


def _adamw_update(w, g, m, v):
    nm = B1 * m + (1.0 - B1) * g
    nv = B2 * v + (1.0 - B2) * (g * g)
    m_hat = nm / (1.0 - B1 ** STEP)
    v_hat = nv / (1.0 - B2 ** STEP)
    return -LR * (m_hat / (jnp.sqrt(v_hat) + AEPS) + WD * w), nm, nv


SMALL_PARTS = (("b_ada", 0, 6, D), ("g_pre", 8, 2, D), ("g_post", 16, 2, D), ("lb_logits", 24, 2, D),
               ("pool_w", 32, 128, D), ("pool_scale", 160, 1, D), ("hgrn_norm_g", 168, 1, 2 * HD))


def _adamw_small(g_small, g_lb_logits, wmv):
    n = len(SMALL_PARTS)

    def body(g_ref, glb_ref, *refs):
        ins, outs = refs[:3 * n], refs[3 * n:]
        for p, (key, row0, rows, width) in enumerate(SMALL_PARTS):
            gv = glb_ref[...] if key == "lb_logits" else g_ref[row0:row0 + rows, 0:width]
            res = _adamw_update(ins[3 * p][...], gv, ins[3 * p + 1][...], ins[3 * p + 2][...])
            for t in range(3):
                outs[3 * p + t][...] = res[t]

    flat = [t for triple in wmv for t in triple]
    outs = pl.pallas_call(body, name="adamw_small",
                          out_shape=[jax.ShapeDtypeStruct(t.shape, F32) for t in flat],
                          compiler_params=_cp())(g_small, g_lb_logits, *flat)
    return [outs[3 * p:3 * p + 3] for p in range(n)]


def _cast_to_slot(place, w, l, name):
    _, R, C = w.shape
    tr = _row_tile(R, max(8, (1 << 19) // C))

    def body(p_ref, w_ref, o_ref):
        o_ref[...] = w_ref[...].astype(BF16)

    return pl.pallas_call(
        body, name=name, out_shape=jax.ShapeDtypeStruct((NCHIP, R, C), BF16),
        grid_spec=pltpu.PrefetchScalarGridSpec(
            num_scalar_prefetch=1, grid=(R // tr,),
            in_specs=[pl.BlockSpec((None, tr, C), lambda i, p_ref: (l, i, 0))],
            out_specs=pl.BlockSpec((None, tr, C), lambda i, p_ref: (p_ref[0], i, 0))),
        compiler_params=_cp(("parallel",)),
    )(place, w)


def _pair_add(core, g, got, name):
    _, R, C = g.shape
    r2 = R // 2
    tr = _row_tile(r2, max(8, (1 << 19) // C))
    nt = r2 // tr

    def body(c_ref, a_ref, b_ref, o_ref):
        o_ref[...] = (a_ref[...].astype(F32) + b_ref[...].astype(F32)).astype(o_ref.dtype)

    return pl.pallas_call(
        body, name=name, out_shape=jax.ShapeDtypeStruct((NCHIP, r2, C), BF16),
        grid_spec=pltpu.PrefetchScalarGridSpec(
            num_scalar_prefetch=1, grid=(NCHIP, nt),
            in_specs=[pl.BlockSpec((None, tr, C), lambda j, i, c_ref: (j, c_ref[0] * nt + i, 0)),
                      pl.BlockSpec((None, tr, C), lambda j, i, c_ref: (j, i, 0))],
            out_specs=pl.BlockSpec((None, tr, C), lambda j, i, c_ref: (j, i, 0))),
        compiler_params=_cp(("parallel", "parallel")),
    )(core, g, got)


def _chip_sum(place, part, recv, layer, both, name):
    _, r2, C = part.shape
    tr = _row_tile(r2, max(8, (1 << 18) // C))
    nt = r2 // tr

    def body(p_ref, own_ref, r_ref, *rest):
        o_ref = rest[-1]
        me = p_ref[0]
        own = own_ref[...].astype(F32)
        acc = None
        for j in range(NCHIP):
            slot = jnp.minimum(jnp.where(j > me, j - 1, j), NCHIP - 2)
            term = jnp.where(me == j, own, r_ref[slot].astype(F32))
            acc = term if acc is None else acc + term
        o_ref[...] = acc

    args = (place, part, recv) if both is None else (place, part, recv, both)
    return pl.pallas_call(
        body, name=name, out_shape=jax.ShapeDtypeStruct((2, 2 * r2, C), F32),
        grid_spec=pltpu.PrefetchScalarGridSpec(
            num_scalar_prefetch=1, grid=(nt,),
            in_specs=[pl.BlockSpec((None, tr, C), lambda i, p_ref: (p_ref[0], i, 0)),
                      pl.BlockSpec((NCHIP - 1, tr, C), lambda i, p_ref: (0, i, 0))] + [ANY] * (len(args) - 3),
            out_specs=pl.BlockSpec((None, tr, C), lambda i, p_ref: (layer, p_ref[1] * nt + i, 0))),
        input_output_aliases={} if both is None else {3: 0},
        compiler_params=_cp(("parallel",)),
    )(*args)


def _place():
    x, y, c = lax.axis_index("x"), lax.axis_index("y"), lax.axis_index("c")
    chips = [(1 - x, y), (x, 1 - y), (1 - x, 1 - y)]
    return x, y, c, chips


def _gather_small(blk, name):
    m_per, n = blk.shape

    def body(x_ref, out_ref, send_sems, recv_sems, local_sem):
        x, y, c, chips = _place()
        me, sibling = (x, y, c), (x, y, 1 - c)

        def rows(px, py, pc):
            return out_ref.at[pl.ds((4 * px + 2 * py + pc) * m_per, m_per), :]

        def copy(k, block, to, src=None):
            return pltpu.make_async_remote_copy(
                src_ref=rows(*block) if src is None else src, dst_ref=rows(*block),
                send_sem=send_sems.at[k], recv_sem=recv_sems.at[k], device_id=to, device_id_type=MESH)

        mine = pltpu.make_async_copy(x_ref, rows(*me), local_sem)
        mine.start()
        first = [copy(0, me, sibling, src=x_ref)]
        first += [copy(1 + j, me, (*chip, c), src=x_ref) for j, chip in enumerate(chips)]
        for cp in first:
            cp.start()
        passed = [copy(4 + j, (*chip, c), sibling) for j, chip in enumerate(chips)]
        for j, chip in enumerate(chips):
            copy(1 + j, (*chip, c), me).wait_recv()
            passed[j].start()
        copy(0, sibling, me).wait_recv()
        for j, chip in enumerate(chips):
            copy(4 + j, (*chip, 1 - c), me).wait_recv()
        for cp in first + passed:
            cp.wait_send()
        mine.wait()

    return pl.pallas_call(
        body, name=name, out_shape=jax.ShapeDtypeStruct((NDEV * m_per, n), blk.dtype),
        in_specs=[pl.BlockSpec(memory_space=pltpu.VMEM)], out_specs=pl.BlockSpec(memory_space=pltpu.VMEM),
        scratch_shapes=[pltpu.SemaphoreType.DMA((7,)), pltpu.SemaphoreType.DMA((7,)), pltpu.SemaphoreType.DMA],
        compiler_params=_cp(),
    )(blk)


def _gather_rows_carry(blk):
    m_per, n = blk.shape

    def rows(ref, px, py, pc):
        return ref.at[pl.ds((4 * px + 2 * py + pc) * m_per, m_per), :]

    def copy(ins, outs, send_sems, recv_sems, k, block, to, own=False):
        return pltpu.make_async_remote_copy(
            src_ref=ins[0] if own else rows(outs[0], *block), dst_ref=rows(outs[0], *block),
            send_sem=send_sems.at[k], recv_sem=recv_sems.at[k], device_id=to, device_id_type=MESH)

    def mine(ins, outs, send_sems):
        x, y, c, _ = _place()
        return pltpu.make_async_copy(ins[0], rows(outs[0], x, y, c), send_sems.at[7])

    def start(ins, outs, send_sems, recv_sems):
        x, y, c, chips = _place()
        mine(ins, outs, send_sems).start()
        copy(ins, outs, send_sems, recv_sems, 0, (x, y, c), (x, y, 1 - c), own=True).start()
        for j, chip in enumerate(chips):
            copy(ins, outs, send_sems, recv_sems, 1 + j, (x, y, c), (*chip, c), own=True).start()

    def finish(ins, outs, send_sems, recv_sems):
        x, y, c, chips = _place()
        for j, chip in enumerate(chips):
            copy(ins, outs, send_sems, recv_sems, 1 + j, (*chip, c), (x, y, c)).wait_recv()
            copy(ins, outs, send_sems, recv_sems, 4 + j, (*chip, c), (x, y, 1 - c)).start()
        copy(ins, outs, send_sems, recv_sems, 0, (x, y, 1 - c), (x, y, c)).wait_recv()
        for j, chip in enumerate(chips):
            copy(ins, outs, send_sems, recv_sems, 4 + j, (*chip, 1 - c), (x, y, c)).wait_recv()
        copy(ins, outs, send_sems, recv_sems, 0, (x, y, c), (x, y, 1 - c), own=True).wait_send()
        for j, chip in enumerate(chips):
            copy(ins, outs, send_sems, recv_sems, 1 + j, (x, y, c), (*chip, c), own=True).wait_send()
            copy(ins, outs, send_sems, recv_sems, 4 + j, (*chip, c), (x, y, 1 - c)).wait_send()
        mine(ins, outs, send_sems).wait()

    return _Carry([blk], [jax.ShapeDtypeStruct((NDEV * m_per, n), blk.dtype)], {}, 8, start, finish)


def _gather_carry(shards):
    n = len(shards)

    def over_ici(outs, send_sems, recv_sems, a, j, chip_xy, slot):
        x, y, c, _ = _place()
        r2 = outs[a].shape[1] // 2
        blk = outs[a].at[slot, pl.ds(c * r2, r2), :]
        return pltpu.make_async_remote_copy(
            src_ref=blk, dst_ref=blk, send_sem=send_sems.at[6 * a + j], recv_sem=recv_sems.at[6 * a + j],
            device_id=(*chip_xy, c), device_id_type=MESH)

    def over_d2d(outs, send_sems, recv_sems, a, j, slot, half):
        x, y, c, _ = _place()
        r2 = outs[a].shape[1] // 2
        blk = outs[a].at[slot, pl.ds(half * r2, r2), :]
        return pltpu.make_async_remote_copy(
            src_ref=blk, dst_ref=blk, send_sem=send_sems.at[6 * a + 3 + j], recv_sem=recv_sems.at[6 * a + 3 + j],
            device_id=(x, y, 1 - c), device_id_type=MESH)

    def start(ins, outs, send_sems, recv_sems):
        x, y, c, chips = _place()
        for a in range(n):
            for j, chip_xy in enumerate(chips):
                over_ici(outs, send_sems, recv_sems, a, j, chip_xy, 2 * x + y).start()

    def finish(ins, outs, send_sems, recv_sems):
        x, y, c, chips = _place()
        for a in range(n):
            for j, (cx, cy) in enumerate(chips):
                over_ici(outs, send_sems, recv_sems, a, j, (cx, cy), 2 * cx + cy).wait_recv()
                over_d2d(outs, send_sems, recv_sems, a, j, 2 * cx + cy, c).start()
        for a in range(n):
            for j, (cx, cy) in enumerate(chips):
                over_d2d(outs, send_sems, recv_sems, a, j, 2 * cx + cy, 1 - c).wait_recv()
        for a in range(n):
            for j, (cx, cy) in enumerate(chips):
                over_ici(outs, send_sems, recv_sems, a, j, (cx, cy), 2 * x + y).wait_send()
                over_d2d(outs, send_sems, recv_sems, a, j, 2 * cx + cy, c).wait_send()

    return _Carry(shards, [jax.ShapeDtypeStruct(s.shape, s.dtype) for s in shards],
                  {a: a for a in range(n)}, 6 * n, start, finish)


def _rs_pair(grads, name):
    n = len(grads)

    def body(*refs):
        ins, gots = refs[:n], refs[n:2 * n]
        send_sems, recv_sems = refs[2 * n:]
        x, y, c, _ = _place()
        cps = []
        for a in range(n):
            r2 = ins[a].shape[1] // 2
            cp = pltpu.make_async_remote_copy(
                src_ref=ins[a].at[:, pl.ds((1 - c) * r2, r2), :], dst_ref=gots[a],
                send_sem=send_sems.at[a], recv_sem=recv_sems.at[a],
                device_id=(x, y, 1 - c), device_id_type=MESH)
            cp.start()
            cps.append(cp)
        for cp in cps:
            cp.wait()

    half = [jax.ShapeDtypeStruct((NCHIP, g.shape[1] // 2, g.shape[2]), g.dtype) for g in grads]
    return pl.pallas_call(
        body, name=name, out_shape=half, in_specs=[ANY] * n, out_specs=[ANY] * n,
        scratch_shapes=[pltpu.SemaphoreType.DMA((n,)), pltpu.SemaphoreType.DMA((n,))],
        compiler_params=_cp(),
    )(*grads)


def _chips_carry(parts):
    n = len(parts)

    def send(ins, outs, send_sems, recv_sems, a, j, chip_xy):
        x, y, c, _ = _place()
        me, them = 2 * x + y, 2 * chip_xy[0] + chip_xy[1]
        return pltpu.make_async_remote_copy(
            src_ref=ins[a].at[them], dst_ref=outs[a].at[me - (me > them).astype(jnp.int32)],
            send_sem=send_sems.at[3 * a + j], recv_sem=recv_sems.at[3 * a + j],
            device_id=(*chip_xy, c), device_id_type=MESH)

    def start(ins, outs, send_sems, recv_sems):
        _, _, _, chips = _place()
        for a in range(n):
            for j, chip_xy in enumerate(chips):
                send(ins, outs, send_sems, recv_sems, a, j, chip_xy).start()

    def finish(ins, outs, send_sems, recv_sems):
        x, y, c, chips = _place()
        me = 2 * x + y
        for a in range(n):
            for j, (cx, cy) in enumerate(chips):
                them = 2 * cx + cy
                blk = outs[a].at[them - (them > me).astype(jnp.int32)]
                pltpu.make_async_remote_copy(
                    src_ref=blk, dst_ref=blk, send_sem=send_sems.at[3 * a + j], recv_sem=recv_sems.at[3 * a + j],
                    device_id=(cx, cy, c), device_id_type=MESH).wait_recv()
        for a in range(n):
            for j, chip_xy in enumerate(chips):
                send(ins, outs, send_sems, recv_sems, a, j, chip_xy).wait_send()

    return _Carry(parts, [jax.ShapeDtypeStruct((NCHIP - 1,) + p.shape[1:], p.dtype) for p in parts], {},
                  3 * n, start, finish)


def _rs_swap(fulls):
    n = len(fulls)

    def body(*refs):
        outs = refs[n:2 * n]
        send_sems, recv_sems = refs[2 * n:]
        x, y, c, _ = _place()
        cps = []
        for a in range(n):
            r2 = outs[a].shape[1] // 2
            mine = outs[a].at[:, pl.ds(c * r2, r2), :]
            cp = pltpu.make_async_remote_copy(
                src_ref=mine, dst_ref=mine, send_sem=send_sems.at[a], recv_sem=recv_sems.at[a],
                device_id=(x, y, 1 - c), device_id_type=MESH)
            cp.start()
            cps.append(cp)
        for a in range(n):
            r2 = outs[a].shape[1] // 2
            blk = outs[a].at[:, pl.ds((1 - c) * r2, r2), :]
            pltpu.make_async_remote_copy(
                src_ref=blk, dst_ref=blk, send_sem=send_sems.at[a], recv_sem=recv_sems.at[a],
                device_id=(x, y, 1 - c), device_id_type=MESH).wait_recv()
        for cp in cps:
            cp.wait_send()

    return pl.pallas_call(
        body, name="rs_swap", out_shape=[jax.ShapeDtypeStruct(f.shape, f.dtype) for f in fulls],
        in_specs=[ANY] * n, out_specs=[ANY] * n, input_output_aliases={a: a for a in range(n)},
        scratch_shapes=[pltpu.SemaphoreType.DMA((n,)), pltpu.SemaphoreType.DMA((n,))],
        compiler_params=_cp(),
    )(*fulls)


def _mm_ride(a, b, carry, **kw):
    if carry is None:
        return _mm(a, b, **kw), []
    return _mm(a, b, carry=carry, **kw)


def _layer_fwd(l, x, ada, w, small, ride):
    shift, scale, gate = ada[:, 0:D], ada[:, D:2 * D], ada[:, 2 * D:3 * D]
    h, h_t = _prenorm_fwd(x, small["g_pre"][l], scale, shift, f"prenorm_fwd{l}")
    proj, landed = _mm_ride(h, w["w_in"][l], ride["proj"][0], name=f"proj{l}", b_mode="nn_sh", tm=2048)
    ride["proj"][1](landed)
    a_in, a_in_t = _pool_fwd(proj, small["pool_w"][l], small["pool_scale"][l], f"pool_fwd{l}")
    (b_in, b_in_t, o_raw, states, mild), landed = _hgrn_fwd(proj, small["lb"][l], small["hgrn_norm_g"][l],
                                                           f"hgrn_fwd{l}", carry=ride["hgrn"][0])
    ride["hgrn"][1](landed)
    br_a, br_b, merged_t, y, x_new = _layer_tail_fwd(
        proj, a_in, b_in, x, w["w_pool_o"][l], w["w_hgrn_o"][l].reshape(D, D), w["w_out"][l].reshape(D, D),
        gate, small["g_post"][l], f"tail_fwd{l}")
    saved = dict(x=x, h_t=h_t, proj=proj, a_in_t=a_in_t, b_in_t=b_in_t, o_raw=o_raw, states=states, mild=mild,
                 br_a=br_a, br_b=br_b, merged_t=merged_t, y=y, scale=scale, gate=gate)
    return x_new, saved


def _layer_bwd(l, dxn, sv, w, small, ride):
    dy, dbr_a, dbr_b, dmg, da_in, db_in, dgate, dg_post = _layer_head_bwd(
        dxn, sv["y"], sv["proj"], sv["br_a"], sv["br_b"], w["w_pool_o"][l], w["w_hgrn_o"][l].reshape(D, D),
        w["w_out"][l].reshape(D, D), sv["gate"], small["g_post"][l], f"head_bwd{l}")
    gw_out = _mm(sv["merged_t"], dy, name=f"gw_out{l}", out_dtype=BF16)
    gw_pool_o = _mm(sv["a_in_t"], dbr_a, name=f"gw_pool_o{l}", out_shards=NCHIP, out_dtype=BF16)
    gw_hgrn_o = _mm(sv["b_in_t"], dbr_b, name=f"gw_hgrn_o{l}", out_dtype=BF16)
    big = dict(w_pool_o=gw_pool_o, w_hgrn_o=gw_hgrn_o.reshape(NCHIP, D // NCHIP, D),
               w_out=gw_out.reshape(NCHIP, D // NCHIP, D))
    carry, landed = ride["hgrn"](big)
    (dhq, dhf, dhi, dhg, dlb, dgn), outs = _hgrn_bwd(db_in, sv["proj"], sv["o_raw"], sv["states"], sv["mild"],
                                                     small["lb"][l], small["hgrn_norm_g"][l], f"hgrn_bwd{l}",
                                                     carry=carry)
    landed(outs)
    dpv, dpg, dpw, dpsc = _pool_bwd(da_in, sv["proj"], small["pool_w"][l], small["pool_scale"][l],
                                    f"pool_bwd{l}")
    dproj = jnp.concatenate([dpv, dpg, dhq, dhf, dhi, dhg, dmg], axis=1)
    little = dict(dgate=dgate, g_post=dg_post, pool_w=dpw, pool_scale=dpsc, lb=dlb,
                  hgrn_norm_g=jnp.sum(dgn, axis=0, keepdims=True))
    carry, landed = ride["gw_in"](little)
    big["w_in"], outs = _mm_ride(sv["h_t"], dproj, carry, name=f"gw_in{l}", out_shards=NCHIP, out_dtype=BF16)
    landed(outs)
    carry, landed = ride["d_h"](big)
    dh, outs = _mm_ride(dproj, w["w_in"][l], carry, name=f"d_h{l}", b_mode="nt_shk", tn=1024)
    landed(outs)
    dx, dshift, dscale, dg_pre = _prenorm_bwd(dh, dxn, sv["x"], small["g_pre"][l], sv["scale"],
                                              f"prenorm_bwd{l}")
    little.update(dshift=dshift, dscale=dscale, g_pre=dg_pre)
    return dx, big, little


SMALL_ROWS = 176


def _rows8(t):
    t = t.reshape(-1, D)
    return jnp.pad(t, ((0, -t.shape[0] % 8), (0, 0)))


def _pack_small_weights(b_ada, g_pre, g_post, lb_logits, pool_w, pool_scale, hgrn_norm_g):
    gn = jnp.pad(hgrn_norm_g.reshape(1, 2 * HD), ((0, 0), (0, D - 2 * HD)))
    return jnp.concatenate([_rows8(b_ada), _rows8(g_pre), _rows8(g_post), _rows8(lb_logits), _rows8(pool_w),
                            _rows8(pool_scale), _rows8(gn)], axis=0)


def _pack_small(parts):
    both = lambda key: jnp.stack([parts[l][key] for l in range(2)])
    d_ada = jnp.stack([jnp.concatenate([p["dshift"], p["dscale"], p["dgate"]], axis=1) for p in parts])
    return _pack_small_weights(d_ada, both("g_pre"), both("g_post"), both("lb"), both("pool_w"),
                               both("pool_scale"), both("hgrn_norm_g"))


def _unpack_small(p):
    return (p[0:6].reshape(2, 3 * D), p[8:10], p[16:18], p[24:26], p[32:160].reshape(2, GROUPS, 128, 128),
            p[160:161].reshape(2, POOL_W), p[168:169, 0:2 * HD].reshape(2, HD))


def kernel(x, c, w_ada, b_ada, g_pre, g_post, w_in, pool_w, pool_scale, lb_logits, hgrn_norm_g, w_pool_o, w_hgrn_o, w_out, loss_target, m_w_ada, m_b_ada, m_g_pre, m_g_post, m_w_in, m_pool_w, m_pool_scale, m_lb_logits, m_hgrn_norm_g, m_w_pool_o, m_w_hgrn_o, m_w_out, v_w_ada, v_b_ada, v_g_pre, v_g_post, v_w_in, v_pool_w, v_pool_scale, v_lb_logits, v_hgrn_norm_g, v_w_pool_o, v_w_hgrn_o, v_w_out):
    ax, ay, ac = lax.axis_index("x"), lax.axis_index("y"), lax.axis_index("c")
    chip = 2 * ax + ay
    dev = 2 * chip + ac
    xe, te = x[0], loss_target[0]
    ada_s = w_ada.shape[2]

    big_names = ("w_in", "w_pool_o", "w_hgrn_o", "w_out")
    big_w = (w_in, w_pool_o, w_hgrn_o, w_out)
    core = jnp.stack([ac]).astype(jnp.int32)
    place = jnp.stack([chip, ac]).astype(jnp.int32)
    slots = {(k, l): _cast_to_slot(place, t, l, f"cast_{k}{l}") for l in range(2) for k, t in zip(big_names, big_w)}
    w = {k: [None, None] for k in big_names}
    (w["w_in"][0],) = _run_carry(_gather_carry([slots["w_in", 0]]), "gather_w_in0")
    later = [(k, l) for l in range(2) for k in big_names[1:]]

    def landed_later(outs):
        for (k, l), o in zip(later, outs):
            w[k][l] = o

    def landed_w_in1(outs):
        (w["w_in"][1],) = outs

    ride_fwd0 = dict(proj=(_gather_carry([slots[t] for t in later]), landed_later),
                     hgrn=(_gather_carry([slots["w_in", 1]]), landed_w_in1))
    no_carry = (None, lambda outs: None)

    c_all = _gather_small(jnp.broadcast_to(c, (8, D)), "gather_c").reshape(NDEV, 8, D)[:, 0, :]
    c_pad = jnp.pad(c_all, ((0, ADA_PAD - NDEV), (0, 0)))
    b_sh = lax.dynamic_slice(b_ada, (0, chip * ada_s), (2, ada_s))
    ada_cols = _gather_small(_ada_fwd(c_pad, w_ada, b_sh), "gather_ada")
    ada_cols = ada_cols.reshape(NCHIP, 2, NDEV, 2, ada_s)[:, 0]
    ada_all = jnp.transpose(ada_cols, (2, 1, 0, 3)).reshape(2, NDEV, 3 * D)
    ada_me = lax.dynamic_slice(ada_all, (0, dev, 0), (2, 1, 3 * D))

    lbs = _lb_fwd(lb_logits)
    small = dict(g_pre=g_pre[:, None, :], g_post=g_post[:, None, :], pool_w=pool_w,
                 pool_scale=pool_scale[:, None, :], lb=lbs[:, None, :], hgrn_norm_g=hgrn_norm_g[:, None, :])

    x1, sv0 = _layer_fwd(0, xe, ada_me[0], w, small, ride_fwd0)
    x2, sv1 = _layer_fwd(1, x1, ada_me[1], w, small, dict(proj=no_carry, hgrn=no_carry))
    dx2, loss_blk = _loss_head(x2, te, "loss_head")

    parts, recv = {}, {}

    def pair_sums(keys, grads, tag):
        got = _rs_pair(grads, f"rs_pair_{tag}")
        for kl, g, o in zip(keys, grads, got):
            parts[kl] = _pair_add(core, g, o, f"rs_add_{kl[0]}{kl[1]}")

    def exchange(keys):
        def landed(outs):
            recv.update(zip(keys, outs))
        return _chips_carry([parts[kl] for kl in keys]), landed

    def early(l):
        return [(k, l) for k in big_names[1:]]

    def ride_hgrn1(big):
        pair_sums(early(1), [big[k] for k in big_names[1:]], "l1_early")
        return exchange(early(1))

    def ride_d_h1(big):
        pair_sums([("w_in", 1)], [big["w_in"]], "l1_w_in")
        return no_carry

    def ride_hgrn0(big):
        pair_sums(early(0), [big[k] for k in big_names[1:]], "l0_early")
        return exchange([("w_in", 1)] + early(0))

    def ride_d_h0(big):
        pair_sums([("w_in", 0)], [big["w_in"]], "l0_w_in")
        return exchange([("w_in", 0)])

    dx1, big1, little1 = _layer_bwd(1, dx2, sv1, w, small,
                                    dict(hgrn=ride_hgrn1, gw_in=lambda little: no_carry, d_h=ride_d_h1))

    gathered = {}
    zero_row = jnp.zeros((1, D), F32)

    def ride_gw_in0(little):
        so_far = dict(little, dshift=zero_row, dscale=zero_row, g_pre=zero_row)

        def landed(outs):
            (gathered["early"],) = outs
        return _gather_rows_carry(_pack_small([so_far, little1])), landed

    dx0, big0, little0 = _layer_bwd(0, dx1, sv0, w, small,
                                    dict(hgrn=ride_hgrn0, gw_in=ride_gw_in0, d_h=ride_d_h0))
    loss = lax.psum(loss_blk[0, 0], ("x", "y", "c"))
    late = _rows8(jnp.stack([little0["dshift"], little0["dscale"], little0["g_pre"]]))
    late = _gather_small(late, "gather_small_late").reshape(NDEV, 8, D)
    packed = gathered["early"].reshape(NDEV, SMALL_ROWS, D)
    packed = packed.at[:, 0:2, :].set(late[:, 0:2, :]).at[:, 8:9, :].set(late[:, 2:3, :])
    red = []
    for k in big_names:
        both = _chip_sum(place, parts[k, 1], recv[k, 1], 1, None, f"rs_sum_{k}1")
        red.append(_chip_sum(place, parts[k, 0], recv[k, 0], 0, both, f"rs_sum_{k}0"))
    g_big = dict(zip(big_names, _rs_swap(red)))

    def upd(wt, g, m, v, name, carry=None):
        shp = wt.shape
        two = lambda t: t.reshape(-1, shp[-1])
        res = _adamw(two(wt), two(g), two(m), two(v), name, carry)
        return [t.reshape(shp) for t in res[:3]], res[3:]

    u_w_in, _ = upd(w_in, g_big["w_in"], m_w_in, v_w_in, "adamw_w_in")
    g_small = _sum_devices(packed)
    g_b_ada, g_g_pre, g_g_post, g_lb, g_pool_w, g_pool_scale, g_norm_g = _unpack_small(g_small)
    g_lb_logits = _lb_bwd(lb_logits, g_lb)
    d_ada_all = packed[:, 0:6, :].reshape(NDEV, 2, 3 * D)
    d_ada_sh = lax.dynamic_slice(jnp.transpose(d_ada_all, (1, 0, 2)), (0, 0, chip * ada_s), (2, NDEV, ada_s))
    d_ada_sh = jnp.pad(d_ada_sh, ((0, 0), (0, ADA_PAD - NDEV), (0, 0)))
    g_w_ada = _ada_wgrad(c_pad.T, d_ada_sh)

    u_w_ada, _ = upd(w_ada, g_w_ada, m_w_ada, v_w_ada, "adamw_w_ada")
    u_w_pool_o, _ = upd(w_pool_o, g_big["w_pool_o"], m_w_pool_o, v_w_pool_o, "adamw_w_pool_o")
    u_w_hgrn_o, _ = upd(w_hgrn_o, g_big["w_hgrn_o"], m_w_hgrn_o, v_w_hgrn_o, "adamw_w_hgrn_o")
    u_w_out, _ = upd(w_out, g_big["w_out"], m_w_out, v_w_out, "adamw_w_out")
    small_w = dict(b_ada=(b_ada, m_b_ada, v_b_ada), g_pre=(g_pre, m_g_pre, v_g_pre),
                   g_post=(g_post, m_g_post, v_g_post), lb_logits=(lb_logits, m_lb_logits, v_lb_logits),
                   pool_w=(pool_w, m_pool_w, v_pool_w), pool_scale=(pool_scale, m_pool_scale, v_pool_scale),
                   hgrn_norm_g=(hgrn_norm_g, m_hgrn_norm_g, v_hgrn_norm_g))
    in_rows = [tuple(t.reshape(rows, width) for t in small_w[key]) for key, _, rows, width in SMALL_PARTS]
    u_rows = _adamw_small(g_small, g_lb_logits, in_rows)
    u_small = {key: [t.reshape(small_w[key][0].shape) for t in u_rows[p]]
               for p, (key, _, _, _) in enumerate(SMALL_PARTS)}

    grads_out = (g_w_ada, g_b_ada, g_g_pre, g_g_post, g_big["w_in"], g_pool_w, g_pool_scale, g_lb_logits,
                 g_norm_g, g_big["w_pool_o"], g_big["w_hgrn_o"], g_big["w_out"])

    def ordered(k):
        s = lambda key: u_small[key][k]
        return (u_w_ada[k], s("b_ada"), s("g_pre"), s("g_post"), u_w_in[k], s("pool_w"), s("pool_scale"),
                s("lb_logits"), s("hgrn_norm_g"), u_w_pool_o[k], u_w_hgrn_o[k], u_w_out[k])

    return (loss, dx0[None], *grads_out, *ordered(0), *ordered(1), *ordered(2))
```

```python
import functools

import jax
import jax.numpy as jnp
from jax import lax
from jax.experimental import pallas as pl
from jax.experimental.pallas import tpu as pltpu

F32 = jnp.float32
BF16 = jnp.bfloat16
MESH = pl.DeviceIdType.MESH

D = 1024
HEADS = 8
HD = 128
GROUPS = 4
POOL_W = 512
WINDOWS = (2, 4, 8, 16)
CH = 64
SB = 16
NH = 2
IN_W = 7168
NCHIP = 4
NDEV = 8
EPS = 1e-6
PV0, PG0, HQ0, HF0, HI0, HG0 = 0, 4, 8, 16, 24, 32
MGP_BLK, MGH_BLK = 5, 6

LR, B1, B2, AEPS, WD, STEP = 0.001, 0.9, 0.999, 1e-08, 0.01, 10
VMEM_LIMIT = 56 * 1024 * 1024


def _cp(sem=None, **kw):
    if sem is not None:
        kw["dimension_semantics"] = sem
    return pltpu.CompilerParams(vmem_limit_bytes=VMEM_LIMIT, **kw)


def _sig(z):
    return 1.0 / (1.0 + jnp.exp(-z))


def _dsilu(z, s):
    return s * (1.0 + z * (1.0 - s))


def _row_tile(rows, cap):
    if rows <= cap:
        return rows
    t = 1 << (cap.bit_length() - 1)
    while rows % t:
        t //= 2
    return t


ANY = pl.BlockSpec(memory_space=pl.ANY)


class _Carry:
    def __init__(self, ins, outs, aliases, n_sem, start, finish):
        self.ins, self.outs, self.aliases, self.n_sem = list(ins), list(outs), dict(aliases), n_sem
        self.start, self.finish = start, finish


class _SemWindow:
    def __init__(self, ref, base):
        self._ref, self._base = ref, base

    @property
    def at(self):
        return self

    def __getitem__(self, k):
        return self._ref.at[self._base + k]


def _join_carries(*carries):
    ins, outs, aliases, spans, n_sem = [], [], {}, [], 0
    for cr in carries:
        aliases.update({len(ins) + i: len(outs) + o for i, o in cr.aliases.items()})
        spans.append((len(ins), len(cr.ins), len(outs), len(cr.outs), n_sem))
        ins, outs, n_sem = ins + cr.ins, outs + cr.outs, n_sem + cr.n_sem

    def run(which):
        def fn(i_refs, o_refs, send_sems, recv_sems):
            for cr, (i0, ni, o0, no, s0) in zip(carries, spans):
                getattr(cr, which)(i_refs[i0:i0 + ni], o_refs[o0:o0 + no], _SemWindow(send_sems, s0),
                                   _SemWindow(recv_sems, s0))
        return fn

    return _Carry(ins, outs, aliases, n_sem, run("start"), run("finish"))


def _call(body, *, name, grid, in_specs, out_specs, out_shape, args, scratch_shapes=(), sem=None, carry=None):
    in_specs, out_specs, out_shape = list(in_specs), list(out_specs), list(out_shape)
    scratch_shapes = list(scratch_shapes)
    if carry is None:
        outs = pl.pallas_call(body, name=name, grid=grid, in_specs=in_specs, out_specs=out_specs,
                              out_shape=out_shape, scratch_shapes=scratch_shapes,
                              compiler_params=_cp(sem))(*args)
        return list(outs)
    n_in, n_out, n_scr = len(in_specs), len(out_specs), len(scratch_shapes)
    c_in, c_out = len(carry.ins), len(carry.outs)

    def wrapped(*refs):
        k_in, rest = refs[:n_in], refs[n_in:]
        ci, rest = rest[:c_in], rest[c_in:]
        k_out, rest = rest[:n_out], rest[n_out:]
        co, rest = rest[:c_out], rest[c_out:]
        k_scr, (ssem, rsem) = rest[:n_scr], rest[n_scr:]
        pids = [pl.program_id(d) for d in range(len(grid))]
        first = functools.reduce(jnp.logical_and, [p == 0 for p in pids])
        last = functools.reduce(jnp.logical_and, [p == g - 1 for p, g in zip(pids, grid)])

        @pl.when(first)
        def _():
            carry.start(ci, co, ssem, rsem)

        body(*k_in, *k_out, *k_scr)

        @pl.when(last)
        def _():
            carry.finish(ci, co, ssem, rsem)

    outs = pl.pallas_call(
        wrapped, name=name, grid=grid, in_specs=in_specs + [ANY] * c_in, out_specs=out_specs + [ANY] * c_out,
        out_shape=out_shape + carry.outs,
        input_output_aliases={n_in + i: n_out + o for i, o in carry.aliases.items()},
        scratch_shapes=scratch_shapes + [pltpu.SemaphoreType.DMA((carry.n_sem,))] * 2,
        compiler_params=_cp(("arbitrary",) * len(grid)),
    )(*args, *carry.ins)
    return list(outs)


def _run_carry(carry, name):
    c_in, c_out = len(carry.ins), len(carry.outs)

    def body(*refs):
        ci, co, (ssem, rsem) = refs[:c_in], refs[c_in:c_in + c_out], refs[c_in + c_out:]
        carry.start(ci, co, ssem, rsem)
        carry.finish(ci, co, ssem, rsem)

    outs = pl.pallas_call(
        body, name=name, in_specs=[ANY] * c_in, out_specs=[ANY] * c_out, out_shape=carry.outs,
        input_output_aliases=carry.aliases,
        scratch_shapes=[pltpu.SemaphoreType.DMA((carry.n_sem,))] * 2, compiler_params=_cp(),
    )(*carry.ins)
    return list(outs)


def _mm(a, b, *, name, b_mode="nn", out_shards=0, tm=1024, tn=256, tk=None, out_dtype=F32, carry=None):
    M, K = a.shape
    if b_mode == "nn":
        N = b.shape[1]
    elif b_mode == "nt":
        N = b.shape[0]
    elif b_mode == "nn_sh":
        N = b.shape[0] * b.shape[2]
    else:
        N = b.shape[1]
    tm = _row_tile(M, tm)
    if b_mode == "nn_sh":
        tn = _row_tile(b.shape[2], tn)
    elif out_shards:
        tn = _row_tile(N // out_shards, tn)
    else:
        tn = _row_tile(N, tn)
    if tk is None:
        tk = K if b_mode != "nt_shk" else b.shape[2]
    if b_mode == "nt_shk":
        tk = _row_tile(b.shape[2], tk)
    nm, nn, nk = M // tm, N // tn, K // tk

    a_spec = pl.BlockSpec((tm, tk), lambda m, n, k: (m, k))
    if b_mode == "nn":
        b_spec = pl.BlockSpec((tk, tn), lambda m, n, k: (k, n))
    elif b_mode == "nt":
        b_spec = pl.BlockSpec((tn, tk), lambda m, n, k: (n, k))
    elif b_mode == "nn_sh":
        nps = b.shape[2] // tn
        b_spec = pl.BlockSpec((None, tk, tn), lambda m, n, k: (n // nps, k, n % nps))
    else:
        kps = b.shape[2] // tk
        b_spec = pl.BlockSpec((None, tn, tk), lambda m, n, k: (k // kps, n, k % kps))
    if out_shards:
        ops = (N // out_shards) // tn
        o_spec = pl.BlockSpec((None, tm, tn), lambda m, n, k: (n // ops, m, n % ops))
        o_shape = jax.ShapeDtypeStruct((out_shards, M, N // out_shards), out_dtype)
    else:
        o_spec = pl.BlockSpec((tm, tn), lambda m, n, k: (m, n))
        o_shape = jax.ShapeDtypeStruct((M, N), out_dtype)
    trans_b = b_mode in ("nt", "nt_shk")
    dn = (((1,), (1,)), ((), ())) if trans_b else (((1,), (0,)), ((), ()))

    def body(a_ref, b_ref, o_ref, acc_ref):
        k = pl.program_id(2)

        @pl.when(k == 0)
        def _():
            acc_ref[...] = jnp.zeros(acc_ref.shape, F32)

        acc_ref[...] += lax.dot_general(a_ref[...].astype(BF16), b_ref[...].astype(BF16), dn,
                                        preferred_element_type=F32)

        @pl.when(k == nk - 1)
        def _():
            o_ref[...] = acc_ref[...].astype(o_ref.dtype)

    outs = _call(body, name=name, grid=(nm, nn, nk), in_specs=[a_spec, b_spec], out_specs=[o_spec],
                 out_shape=[o_shape], scratch_shapes=[pltpu.VMEM((tm, tn), F32)],
                 sem=("parallel", "parallel", "arbitrary"), args=(a, b), carry=carry)
    return outs[0] if carry is None else (outs[0], outs[1:])


def _rowvec(n=D):
    return pl.BlockSpec((1, n), lambda i: (0, 0))


def _prenorm_fwd(x, g, scale, shift, name):
    S = x.shape[0]
    tr = _row_tile(S, 256)

    def body(x_ref, g_ref, sc_ref, sh_ref, h_ref, ht_ref):
        xv = x_ref[...]
        r = lax.rsqrt(jnp.mean(xv * xv, axis=-1, keepdims=True) + EPS)
        hv = (xv * r) * g_ref[...] * (1.0 + sc_ref[...]) + sh_ref[...]
        h_ref[...] = hv.astype(BF16)
        ht_ref[...] = hv.T.astype(BF16)

    return pl.pallas_call(
        body, name=name, grid=(S // tr,),
        in_specs=[pl.BlockSpec((tr, D), lambda i: (i, 0)), _rowvec(), _rowvec(), _rowvec()],
        out_specs=[pl.BlockSpec((tr, D), lambda i: (i, 0)), pl.BlockSpec((D, tr), lambda i: (0, i))],
        out_shape=[jax.ShapeDtypeStruct((S, D), BF16), jax.ShapeDtypeStruct((D, S), BF16)],
        compiler_params=_cp(("parallel",)),
    )(x, g, scale, shift)


def _prenorm_bwd(dh, dxn, x, g, scale, name):
    S = x.shape[0]
    tr = _row_tile(S, 256)

    def body(dh_ref, dxn_ref, x_ref, g_ref, sc_ref, dx_ref, dsh_ref, dsc_ref, dg_ref):
        i = pl.program_id(0)

        @pl.when(i == 0)
        def _():
            dsh_ref[...] = jnp.zeros((1, D), F32)
            dsc_ref[...] = jnp.zeros((1, D), F32)
            dg_ref[...] = jnp.zeros((1, D), F32)

        xv = x_ref[...]
        dhv = dh_ref[...]
        gv = g_ref[...]
        mod = 1.0 + sc_ref[...]
        r = lax.rsqrt(jnp.mean(xv * xv, axis=-1, keepdims=True) + EPS)
        xh = xv * r
        dsh_ref[...] += jnp.sum(dhv, axis=0, keepdims=True)
        dsc_ref[...] += jnp.sum(dhv * (xh * gv), axis=0, keepdims=True)
        dg_ref[...] += jnp.sum(dhv * mod * xh, axis=0, keepdims=True)
        u = dhv * mod * gv
        dx_ref[...] = dxn_ref[...] + r * u - xv * (r * r * r) * jnp.mean(u * xv, axis=-1, keepdims=True)

    tile = pl.BlockSpec((tr, D), lambda i: (i, 0))
    return pl.pallas_call(
        body, name=name, grid=(S // tr,),
        in_specs=[tile, tile, tile, _rowvec(), _rowvec()],
        out_specs=[tile, _rowvec(), _rowvec(), _rowvec()],
        out_shape=[jax.ShapeDtypeStruct((S, D), F32)] + [jax.ShapeDtypeStruct((1, D), F32)] * 3,
        compiler_params=_cp(("arbitrary",)),
    )(dh, dxn, x, g, scale)


def _postnorm_fwd(x, y, gate, g, name):
    S = x.shape[0]
    tr = _row_tile(S, 256)

    def body(x_ref, y_ref, gate_ref, g_ref, o_ref):
        yv = y_ref[...]
        r = lax.rsqrt(jnp.mean(yv * yv, axis=-1, keepdims=True) + EPS)
        o_ref[...] = x_ref[...] + gate_ref[...] * ((yv * r) * g_ref[...])

    tile = pl.BlockSpec((tr, D), lambda i: (i, 0))
    return pl.pallas_call(
        body, name=name, grid=(S // tr,), in_specs=[tile, tile, _rowvec(), _rowvec()],
        out_specs=tile, out_shape=jax.ShapeDtypeStruct((S, D), F32), compiler_params=_cp(("parallel",)),
    )(x, y, gate, g)


def _postnorm_bwd(dxn, y, gate, g, name):
    S = y.shape[0]
    tr = _row_tile(S, 256)

    def body(dxn_ref, y_ref, gate_ref, g_ref, dy_ref, dgate_ref, dg_ref):
        i = pl.program_id(0)

        @pl.when(i == 0)
        def _():
            dgate_ref[...] = jnp.zeros((1, D), F32)
            dg_ref[...] = jnp.zeros((1, D), F32)

        yv = y_ref[...]
        dv = dxn_ref[...]
        gv = g_ref[...]
        gt = gate_ref[...]
        r = lax.rsqrt(jnp.mean(yv * yv, axis=-1, keepdims=True) + EPS)
        yh = yv * r
        dgate_ref[...] += jnp.sum(dv * (yh * gv), axis=0, keepdims=True)
        dg_ref[...] += jnp.sum(dv * gt * yh, axis=0, keepdims=True)
        u = dv * gt * gv
        dy_ref[...] = (r * u - yv * (r * r * r) * jnp.mean(u * yv, axis=-1, keepdims=True)).astype(BF16)

    tile = pl.BlockSpec((tr, D), lambda i: (i, 0))
    return pl.pallas_call(
        body, name=name, grid=(S // tr,), in_specs=[tile, tile, _rowvec(), _rowvec()],
        out_specs=[tile, _rowvec(), _rowvec()],
        out_shape=[jax.ShapeDtypeStruct((S, D), BF16), jax.ShapeDtypeStruct((1, D), F32),
                   jax.ShapeDtypeStruct((1, D), F32)],
        compiler_params=_cp(("arbitrary",)),
    )(dxn, y, gate, g)


def _loss_head(xo, target, name):
    S = xo.shape[0]
    tr = _row_tile(S, 256)

    def body(x_ref, t_ref, dx_ref, l_ref):
        i = pl.program_id(0)

        @pl.when(i == 0)
        def _():
            l_ref[...] = jnp.zeros((8, 128), F32)

        err = x_ref[...] - t_ref[...]
        dx_ref[...] = err * (1.0 / D)
        l_ref[...] += 0.5 * jnp.sum(jnp.mean(err * err, axis=-1, keepdims=True))

    tile = pl.BlockSpec((tr, D), lambda i: (i, 0))
    return pl.pallas_call(
        body, name=name, grid=(S // tr,), in_specs=[tile, tile],
        out_specs=[tile, pl.BlockSpec((8, 128), lambda i: (0, 0))],
        out_shape=[jax.ShapeDtypeStruct((S, D), F32), jax.ShapeDtypeStruct((8, 128), F32)],
        compiler_params=_cp(("arbitrary",)),
    )(xo, target)


def _layer_tail_fwd(proj, a_in, b_in, x, w_po, w_ho, w_out, gate, g, name, target=None, carry=None):
    S = proj.shape[0]
    tr = _row_tile(S, 256)
    nsh, _, wsh = w_po.shape
    n_in = 10 + (target is not None)

    def body(*refs):
        (mgp_ref, mgh_ref, a_ref, b_ref, x_ref, wpo_ref, who_ref, wout_ref, gate_ref, g_ref) = refs[:10]
        bra_ref, brb_ref, mt_ref, y_ref, xn_ref = refs[n_in:n_in + 5]
        av = a_ref[...]
        bra = jnp.concatenate([jnp.dot(av, wpo_ref[j], preferred_element_type=F32) for j in range(nsh)], axis=1)
        brb = jnp.dot(b_ref[...], who_ref[...], preferred_element_type=F32)
        mv = _sig(mgp_ref[...]) * bra + _sig(mgh_ref[...]) * brb
        bra_ref[...] = bra.astype(BF16)
        brb_ref[...] = brb.astype(BF16)
        mt_ref[...] = mv.T.astype(BF16)
        yv = jnp.dot(mv.astype(BF16), wout_ref[...], preferred_element_type=F32)
        y_ref[...] = yv
        r = lax.rsqrt(jnp.mean(yv * yv, axis=-1, keepdims=True) + EPS)
        xn = x_ref[...] + gate_ref[...] * ((yv * r) * g_ref[...])
        if target is None:
            xn_ref[...] = xn
        else:
            t_ref, l_ref = refs[10], refs[n_in + 5]

            @pl.when(pl.program_id(0) == 0)
            def _():
                l_ref[...] = jnp.zeros((8, 128), F32)

            err = xn - t_ref[...]
            xn_ref[...] = err * (1.0 / D)
            l_ref[...] += 0.5 * jnp.sum(jnp.mean(err * err, axis=-1, keepdims=True))

    tile = pl.BlockSpec((tr, D), lambda i: (i, 0))
    whole = lambda t: pl.BlockSpec(t.shape, lambda i: (0,) * t.ndim)
    last = target is not None
    outs = _call(
        body, name=name, grid=(S // tr,),
        in_specs=[pl.BlockSpec((tr, D), lambda i: (i, MGP_BLK)), pl.BlockSpec((tr, D), lambda i: (i, MGH_BLK)),
                  pl.BlockSpec((tr, POOL_W), lambda i: (i, 0)), tile, tile, whole(w_po), whole(w_ho),
                  whole(w_out), _rowvec(), _rowvec()] + [tile] * last,
        out_specs=[tile, tile, pl.BlockSpec((D, tr), lambda i: (0, i)), tile, tile]
        + [pl.BlockSpec((8, 128), lambda i: (0, 0))] * last,
        out_shape=[jax.ShapeDtypeStruct((S, D), BF16), jax.ShapeDtypeStruct((S, D), BF16),
                   jax.ShapeDtypeStruct((D, S), BF16), jax.ShapeDtypeStruct((S, D), F32),
                   jax.ShapeDtypeStruct((S, D), F32)] + [jax.ShapeDtypeStruct((8, 128), F32)] * last,
        sem=("arbitrary",) if last else ("parallel",),
        args=(proj, proj, a_in, b_in, x, w_po, w_ho, w_out, gate, g) + ((target,) if last else ()), carry=carry)
    return outs[:5 + last], outs[5 + last:]


def _layer_head_bwd(dxn, y, proj, br_a, br_b, w_po, w_ho, w_out, gate, g, name):
    S = y.shape[0]
    tr = _row_tile(S, 256)
    nsh, _, wsh = w_po.shape

    def body(dxn_ref, y_ref, mgp_ref, mgh_ref, bra_ref, brb_ref, wpo_ref, who_ref, wout_ref, gate_ref, g_ref,
             dy_ref, dba_ref, dbb_ref, dmg_ref, dain_ref, dbin_ref, dgate_ref, dg_ref):
        i = pl.program_id(0)

        @pl.when(i == 0)
        def _():
            dgate_ref[...] = jnp.zeros((1, D), F32)
            dg_ref[...] = jnp.zeros((1, D), F32)

        yv = y_ref[...]
        dv = dxn_ref[...]
        gv = g_ref[...]
        gt = gate_ref[...]
        r = lax.rsqrt(jnp.mean(yv * yv, axis=-1, keepdims=True) + EPS)
        yh = yv * r
        dgate_ref[...] += jnp.sum(dv * (yh * gv), axis=0, keepdims=True)
        dg_ref[...] += jnp.sum(dv * gt * yh, axis=0, keepdims=True)
        u = dv * gt * gv
        dy = (r * u - yv * (r * r * r) * jnp.mean(u * yv, axis=-1, keepdims=True)).astype(BF16)
        dy_ref[...] = dy
        dm = _dot_nt(dy, wout_ref[...])
        sp = _sig(mgp_ref[...])
        sh = _sig(mgh_ref[...])
        dba = (dm * sp).astype(BF16)
        dbb = (dm * sh).astype(BF16)
        dba_ref[...] = dba
        dbb_ref[...] = dbb
        dmg_ref[:, 0:D] = (dm * bra_ref[...].astype(F32) * sp * (1.0 - sp)).astype(BF16)
        dmg_ref[:, D:2 * D] = (dm * brb_ref[...].astype(F32) * sh * (1.0 - sh)).astype(BF16)
        dain = _dot_nt(dba[:, 0:wsh], wpo_ref[0])
        for j in range(1, nsh):
            dain = dain + _dot_nt(dba[:, j * wsh:(j + 1) * wsh], wpo_ref[j])
        dain_ref[...] = dain
        dbin_ref[...] = _dot_nt(dbb, who_ref[...])

    tile = pl.BlockSpec((tr, D), lambda i: (i, 0))
    whole = lambda t: pl.BlockSpec(t.shape, lambda i: (0,) * t.ndim)
    return pl.pallas_call(
        body, name=name, grid=(S // tr,),
        in_specs=[tile, tile, pl.BlockSpec((tr, D), lambda i: (i, MGP_BLK)),
                  pl.BlockSpec((tr, D), lambda i: (i, MGH_BLK)), tile, tile, whole(w_po), whole(w_ho),
                  whole(w_out), _rowvec(), _rowvec()],
        out_specs=[tile, tile, tile, pl.BlockSpec((tr, 2 * D), lambda i: (i, 0)),
                   pl.BlockSpec((tr, POOL_W), lambda i: (i, 0)), tile, _rowvec(), _rowvec()],
        out_shape=[jax.ShapeDtypeStruct((S, D), BF16)] * 3
        + [jax.ShapeDtypeStruct((S, 2 * D), BF16), jax.ShapeDtypeStruct((S, POOL_W), F32),
           jax.ShapeDtypeStruct((S, D), F32), jax.ShapeDtypeStruct((1, D), F32), jax.ShapeDtypeStruct((1, D), F32)],
        compiler_params=_cp(("arbitrary",)),
    )(dxn, y, proj, proj, br_a, br_b, w_po, w_ho, w_out, gate, g)


def _merge_fwd(proj, br_a, br_b, name):
    S = proj.shape[0]
    tr = _row_tile(S, 256)

    def body(mgp_ref, mgh_ref, a_ref, b_ref, o_ref, ot_ref):
        mv = _sig(mgp_ref[...]) * a_ref[...] + _sig(mgh_ref[...]) * b_ref[...]
        o_ref[...] = mv.astype(BF16)
        ot_ref[...] = mv.T.astype(BF16)

    tile = pl.BlockSpec((tr, D), lambda i: (i, 0))
    return pl.pallas_call(
        body, name=name, grid=(S // tr,),
        in_specs=[pl.BlockSpec((tr, D), lambda i: (i, MGP_BLK)), pl.BlockSpec((tr, D), lambda i: (i, MGH_BLK)),
                  tile, tile],
        out_specs=[tile, pl.BlockSpec((D, tr), lambda i: (0, i))],
        out_shape=[jax.ShapeDtypeStruct((S, D), BF16), jax.ShapeDtypeStruct((D, S), BF16)],
        compiler_params=_cp(("parallel",)),
    )(proj, proj, br_a, br_b)


def _merge_bwd(dm, proj, br_a, br_b, name):
    S = proj.shape[0]
    tr = _row_tile(S, 256)

    def body(dm_ref, mgp_ref, mgh_ref, a_ref, b_ref, da_ref, db_ref, dmg_ref):
        dmv = dm_ref[...]
        sp = _sig(mgp_ref[...])
        sh = _sig(mgh_ref[...])
        da_ref[...] = (dmv * sp).astype(BF16)
        db_ref[...] = (dmv * sh).astype(BF16)
        dmg_ref[:, 0:D] = (dmv * a_ref[...] * sp * (1.0 - sp)).astype(BF16)
        dmg_ref[:, D:2 * D] = (dmv * b_ref[...] * sh * (1.0 - sh)).astype(BF16)

    tile = pl.BlockSpec((tr, D), lambda i: (i, 0))
    return pl.pallas_call(
        body, name=name, grid=(S // tr,),
        in_specs=[tile, pl.BlockSpec((tr, D), lambda i: (i, MGP_BLK)),
                  pl.BlockSpec((tr, D), lambda i: (i, MGH_BLK)), tile, tile],
        out_specs=[tile, tile, pl.BlockSpec((tr, 2 * D), lambda i: (i, 0))],
        out_shape=[jax.ShapeDtypeStruct((S, D), BF16), jax.ShapeDtypeStruct((S, D), BF16),
                   jax.ShapeDtypeStruct((S, 2 * D), BF16)],
        compiler_params=_cp(("parallel",)),
    )(dm, proj, proj, br_a, br_b)


def _pool_pieces(u, g, S):
    rowi = lax.broadcasted_iota(jnp.int32, (S, 1), 0)

    def down(z, k):
        return jnp.where(rowi >= k, pltpu.roll(z, k, axis=0), 0.0)

    s2 = u + down(u, 1)
    s4 = s2 + down(s2, 2)
    s8 = s4 + down(s4, 4)
    s16 = s8 + down(s8, 8)
    win = jnp.where(g == 0, s2, jnp.where(g == 1, s4, jnp.where(g == 2, s8, s16)))
    w = jnp.where(g == 0, 2, jnp.where(g == 1, 4, jnp.where(g == 2, 8, 16)))
    count = jnp.minimum(rowi + 1, w).astype(F32)
    return win / count - u, count, rowi


def _pool_fwd(proj, pw, pscale, name):
    S = proj.shape[0]

    def body(pv_ref, pg_ref, pw_ref, sc_ref, a_ref, at_ref):
        g = pl.program_id(0)
        pooled, _, _ = _pool_pieces(pv_ref[...], g, S)
        pm = jnp.dot(pooled.astype(BF16), pw_ref[...].astype(BF16), preferred_element_type=F32)
        pgv = pg_ref[...]
        av = pm * sc_ref[...] * (pgv * _sig(pgv))
        a_ref[...] = av.astype(BF16)
        at_ref[...] = av.T.astype(BF16)

    return pl.pallas_call(
        body, name=name, grid=(GROUPS,),
        in_specs=[pl.BlockSpec((S, 128), lambda g: (0, PV0 + g)), pl.BlockSpec((S, 128), lambda g: (0, PG0 + g)),
                  pl.BlockSpec((None, 128, 128), lambda g: (g, 0, 0)), pl.BlockSpec((1, 128), lambda g: (0, g))],
        out_specs=[pl.BlockSpec((S, 128), lambda g: (0, g)), pl.BlockSpec((128, S), lambda g: (g, 0))],
        out_shape=[jax.ShapeDtypeStruct((S, POOL_W), BF16), jax.ShapeDtypeStruct((POOL_W, S), BF16)],
        compiler_params=_cp(("parallel",)),
    )(proj, proj, pw, pscale)


def _pool_bwd(da, proj, pw, pscale, name):
    S = proj.shape[0]

    def body(da_ref, pv_ref, pg_ref, pw_ref, sc_ref, dpv_ref, dpg_ref, dpw_ref, dsc_ref):
        g = pl.program_id(0)
        pooled, count, rowi = _pool_pieces(pv_ref[...], g, S)
        pwb = pw_ref[...].astype(BF16)
        pm = jnp.dot(pooled.astype(BF16), pwb, preferred_element_type=F32)
        scv = sc_ref[...]
        pgv = pg_ref[...]
        sg = _sig(pgv)
        dav = da_ref[...]
        d_ps = dav * (pgv * sg)
        dpg_ref[...] = (dav * (pm * scv) * _dsilu(pgv, sg)).astype(BF16)
        dsc_ref[...] = jnp.sum(d_ps * pm, axis=0, keepdims=True)
        d_pm = (d_ps * scv).astype(BF16)
        dpw_ref[...] = lax.dot_general(pooled.astype(BF16), d_pm, (((0,), (0,)), ((), ())),
                                       preferred_element_type=F32)
        d_pooled = lax.dot_general(d_pm, pwb, (((1,), (1,)), ((), ())), preferred_element_type=F32)
        z = d_pooled / count

        def up(v, k):
            return jnp.where(rowi < S - k, pltpu.roll(v, S - k, axis=0), 0.0)

        t2 = z + up(z, 1)
        t4 = t2 + up(t2, 2)
        t8 = t4 + up(t4, 4)
        t16 = t8 + up(t8, 8)
        adj = jnp.where(g == 0, t2, jnp.where(g == 1, t4, jnp.where(g == 2, t8, t16)))
        dpv_ref[...] = (adj - d_pooled).astype(BF16)

    col = lambda g: (0, g)
    return pl.pallas_call(
        body, name=name, grid=(GROUPS,),
        in_specs=[pl.BlockSpec((S, 128), col), pl.BlockSpec((S, 128), lambda g: (0, PV0 + g)),
                  pl.BlockSpec((S, 128), lambda g: (0, PG0 + g)),
                  pl.BlockSpec((None, 128, 128), lambda g: (g, 0, 0)), pl.BlockSpec((1, 128), col)],
        out_specs=[pl.BlockSpec((S, 128), col), pl.BlockSpec((S, 128), col),
                   pl.BlockSpec((None, 128, 128), lambda g: (g, 0, 0)), pl.BlockSpec((1, 128), col)],
        out_shape=[jax.ShapeDtypeStruct((S, POOL_W), BF16), jax.ShapeDtypeStruct((S, POOL_W), BF16),
                   jax.ShapeDtypeStruct((GROUPS, 128, 128), F32), jax.ShapeDtypeStruct((1, POOL_W), F32)],
        compiler_params=_cp(("parallel",)),
    )(da, proj, proj, pw, pscale)


def _chunk_cumsum(z, rowi):
    for sh in (1, 2, 4, 8, 16, 32):
        z = z + jnp.where(rowi >= sh, pltpu.roll(z, sh, axis=0), 0.0)
    return z


def _chunk_rev_cumsum(z, rowi):
    for sh in (1, 2, 4, 8, 16, 32):
        z = z + jnp.where(rowi < CH - sh, pltpu.roll(z, CH - sh, axis=0), 0.0)
    return z


def _dot_nn(a, b):
    return jnp.dot(a.astype(BF16), b.astype(BF16), preferred_element_type=F32)


def _dot_nt(a, b):
    return lax.dot_general(a.astype(BF16), b.astype(BF16), (((1,), (1,)), ((), ())), preferred_element_type=F32)


def _dot_tn(a, b):
    return lax.dot_general(a.astype(BF16), b.astype(BF16), (((0,), (0,)), ((), ())), preferred_element_type=F32)


def _gates(hq, hf, lbv):
    sq = _sig(hq)
    sf = _sig(hf)
    f = lbv + (1.0 - lbv) * sf
    fc = jnp.maximum(f, 1e-30)
    return hq * sq, sq, sf, f, fc, jnp.log(fc)


DECAY_CAP = 60.0


def _block_ref(c_ref, i):
    if i == 0:
        return jnp.zeros((1, HD), F32)
    return c_ref[SB * i - 1:SB * i, :]


def _block_decay(c_ref):
    spans = [_block_ref(c_ref, i) - c_ref[SB * (i + 1) - 1:SB * (i + 1), :] for i in range(CH // SB)]
    return functools.reduce(jnp.maximum, spans)


def _hgrn_fwd(proj, lb, gn, name, carry=None):
    S = proj.shape[0]
    nch = S // CH
    W = NH * HD

    def body(hq_ref, hf_ref, hi_ref, hg_ref, lb_ref, gn_ref, bin_ref, bint_ref, oraw_ref, st_ref, mild_ref,
             q_s, k_s, c_s, v_s, o_s, state_s, qf_s, kf_s, cf_s):
        state_s[...] = jnp.zeros((NH, HD, HD), F32)
        rowi = lax.broadcasted_iota(jnp.int32, (CH, 1), 0)
        coli = lax.broadcasted_iota(jnp.int32, (1, CH), 1)
        sbi = lax.broadcasted_iota(jnp.int32, (SB, 1), 0)
        gnv = gn_ref[...]

        def gates_pass(n, worst):
            rows = pl.ds(pl.multiple_of(n * CH, CH), CH)
            for hh in range(NH):
                lanes = slice(hh * HD, (hh + 1) * HD)
                q, _, _, f, _, logf = _gates(hq_ref[rows, lanes], hf_ref[rows, lanes], lb_ref[:, lanes])
                c = _chunk_cumsum(logf, rowi)
                qf_s[hh, rows, :] = q
                kf_s[hh, rows, :] = 1.0 - f
                cf_s[hh, rows, :] = c
                c_s[hh] = c
                worst = jnp.maximum(worst, _block_decay(c_s.at[hh]))
            return worst

        def between_chunks(hh, n, rows):
            lanes = slice(hh * HD, (hh + 1) * HD)
            q = qf_s[hh, rows, :]
            k = kf_s[hh, rows, :]
            c = cf_s[hh, rows, :]
            v = hi_ref[rows, lanes]
            q_s[hh] = q
            k_s[hh] = k
            c_s[hh] = c
            v_s[hh] = v
            st = state_s[hh]
            st_ref[hh, n] = st.astype(BF16)
            o_s[hh] = _dot_nt(q * jnp.exp(c), st)
            last = c_s[hh, CH - 1:CH, :]
            state_s[hh] = st * jnp.exp(last) + _dot_tn(v, k * jnp.exp(last - c))

        def within_chunk_matmul(hh):
            q, k, c, v = q_s[hh], k_s[hh], c_s[hh], v_s[hh]
            a = jnp.zeros((CH, CH), F32)
            for i in range(CH // SB):
                r_i = _block_ref(c_s.at[hh], i)
                qi = q * jnp.exp(jnp.minimum(c - r_i, 0.0))
                kei = k * jnp.exp(jnp.minimum(r_i - c, DECAY_CAP))
                m_i = (rowi >= SB * i) & (rowi < SB * (i + 1)) & (coli <= rowi)
                a = a + jnp.where(m_i, _dot_nt(qi, kei), 0.0)
            o_s[hh] += _dot_nn(a, v)

        def within_chunk_exact(hh):
            q, k, c, v = q_s[hh], k_s[hh], c_s[hh], v_s[hh]
            a_off = jnp.zeros((CH, CH), F32)
            for i in range(1, CH // SB):
                r_i = _block_ref(c_s.at[hh], i)
                qi = q * jnp.exp(jnp.minimum(c - r_i, 0.0))
                kei = k * jnp.exp(jnp.minimum(r_i - c, 0.0))
                m_i = (rowi >= SB * i) & (rowi < SB * (i + 1)) & (coli < SB * i)
                a_off = a_off + jnp.where(m_i, _dot_nt(qi, kei), 0.0)
            o_s[hh] += _dot_nn(a_off, v)
            for i in range(CH // SB):
                blk = slice(SB * i, SB * (i + 1))
                qb = q_s[hh, blk, :]
                cb = c_s[hh, blk, :]
                acc = jnp.zeros((SB, HD), F32)
                for s in range(SB):
                    row = SB * i + s
                    w = jnp.exp(jnp.minimum(cb - c_s[hh, row:row + 1, :], 0.0))
                    a_col = jnp.sum(qb * k_s[hh, row:row + 1, :] * w, axis=-1, keepdims=True)
                    acc = acc + jnp.where(sbi >= s, a_col, 0.0) * v_s[hh, row:row + 1, :]
                o_s[hh, blk, :] += acc

        def norm_and_gate(hh, rows):
            lanes = slice(hh * HD, (hh + 1) * HD)
            ov = o_s[hh]
            oraw_ref[rows, lanes] = ov
            r = lax.rsqrt(jnp.mean(ov * ov, axis=-1, keepdims=True) + EPS)
            hg = hg_ref[rows, lanes]
            bin_ref[rows, lanes] = ((ov * r) * gnv * (hg * _sig(hg))).astype(BF16)

        def chunk_with(within_chunk):
            def chunk(n, carry):
                rows = pl.ds(pl.multiple_of(n * CH, CH), CH)
                for hh in range(NH):
                    between_chunks(hh, n, rows)
                for hh in range(NH):
                    within_chunk(hh)
                for hh in range(NH):
                    norm_and_gate(hh, rows)
                return carry
            return chunk

        worst = lax.fori_loop(0, nch, gates_pass, jnp.zeros((1, HD), F32))
        mild = jnp.max(worst) <= DECAY_CAP
        mild_ref[...] = jnp.broadcast_to(jnp.where(mild, 1.0, 0.0), (8, HD))

        @pl.when(mild)
        def _():
            lax.fori_loop(0, nch, chunk_with(within_chunk_matmul), 0, unroll=4)

        @pl.when(jnp.logical_not(mild))
        def _():
            lax.fori_loop(0, nch, chunk_with(within_chunk_exact), 0)

        bint_ref[...] = bin_ref[...].astype(F32).T.astype(BF16)

    col = lambda off: pl.BlockSpec((S, W), lambda h: (0, off // NH + h))
    head = pl.BlockSpec((S, W), lambda h: (0, h))
    outs = _call(
        body, name=name, grid=(HEADS // NH,),
        in_specs=[col(HQ0), col(HF0), col(HI0), col(HG0), pl.BlockSpec((1, W), lambda h: (0, h)),
                  pl.BlockSpec((1, HD), lambda h: (0, 0))],
        out_specs=[head, pl.BlockSpec((W, S), lambda h: (h, 0)), head,
                   pl.BlockSpec((NH, nch, HD, HD), lambda h: (h, 0, 0, 0)),
                   pl.BlockSpec((8, HD), lambda h: (h, 0))],
        out_shape=[jax.ShapeDtypeStruct((S, D), BF16), jax.ShapeDtypeStruct((D, S), BF16),
                   jax.ShapeDtypeStruct((S, D), F32), jax.ShapeDtypeStruct((HEADS, nch, HD, HD), BF16),
                   jax.ShapeDtypeStruct((8 * HEADS // NH, HD), F32)],
        scratch_shapes=[pltpu.VMEM((NH, CH, HD), F32)] * 5 + [pltpu.VMEM((NH, HD, HD), F32)]
        + [pltpu.VMEM((NH, S, HD), F32)] * 3,
        sem=("parallel",), args=(proj, proj, proj, proj, lb, gn), carry=carry)
    return outs[:5], outs[5:]


def _hgrn_bwd(dbin, proj, oraw, states, mild, lb, gn, name, carry=None):
    S = proj.shape[0]
    nch = S // CH
    W = NH * HD

    def body(db_ref, hq_ref, hf_ref, hi_ref, hg_ref, or_ref, st_ref, mild_ref, lb_ref, gn_ref,
             dq_ref, df_ref, di_ref, dg_ref, dlb_ref, dgn_ref,
             q_s, k_s, c_s, v_s, do_s, dq_s, dk_s, dv_s, dc_s, dqd_s, dkd_s, dl_s, dst_s, dlb_s, dgn_s):
        dst_s[...] = jnp.zeros((NH, HD, HD), F32)
        dlb_s[...] = jnp.zeros((1, W), F32)
        dgn_s[...] = jnp.zeros((1, HD), F32)
        rowi = lax.broadcasted_iota(jnp.int32, (CH, 1), 0)
        rowi2 = lax.broadcasted_iota(jnp.int32, (CH, CH), 0)
        coli2 = lax.broadcasted_iota(jnp.int32, (CH, CH), 1)
        sbi = lax.broadcasted_iota(jnp.int32, (SB, 1), 0)
        gnv = gn_ref[...]
        def between_chunks(hh, n, rows):
            lanes = slice(hh * HD, (hh + 1) * HD)
            q, _, _, f, _, logf = _gates(hq_ref[rows, lanes], hf_ref[rows, lanes], lb_ref[:, lanes])
            k = 1.0 - f
            v = hi_ref[rows, lanes]
            c = _chunk_cumsum(logf, rowi)
            ov = or_ref[rows, lanes]
            hg = hg_ref[rows, lanes]
            sg = _sig(hg)
            r = lax.rsqrt(jnp.mean(ov * ov, axis=-1, keepdims=True) + EPS)
            dbv = db_ref[rows, lanes]
            d_on = dbv * (hg * sg)
            dg_ref[rows, lanes] = (dbv * ((ov * r) * gnv) * _dsilu(hg, sg)).astype(BF16)
            dgn_s[...] += jnp.sum(d_on * (ov * r), axis=0, keepdims=True)
            u = d_on * gnv
            do = r * u - ov * (r * r * r) * jnp.mean(u * ov, axis=-1, keepdims=True)
            q_s[hh] = q
            k_s[hh] = k
            c_s[hh] = c
            v_s[hh] = v
            do_s[hh] = do
            st = st_ref[hh, n].astype(F32)
            dst = dst_s[hh]
            ec = jnp.exp(c)
            last = c_s[hh, CH - 1:CH, :]
            el = jnp.exp(last - c)
            elast = jnp.exp(last)
            dq = _dot_nn(do, st) * ec
            dk = _dot_nn(v, dst) * el
            dq_s[hh] = dq
            dk_s[hh] = dk
            dv_s[hh] = _dot_nt(k * el, dst)
            dc_s[hh] = q * dq - k * dk
            dl_s[hh] = (jnp.sum(k * dk, axis=0, keepdims=True)
                        + elast * jnp.sum(st * dst, axis=0, keepdims=True))
            dst_s[hh] = dst * elast + _dot_tn(do, q * ec)

        def pairs_matmul(hh, first, cap, strict):
            q, k, c, v, do = q_s[hh], k_s[hh], c_s[hh], v_s[hh], do_s[hh]
            d_a = _dot_nt(do, v).astype(BF16).astype(F32)
            d_at = d_a.T
            at = jnp.zeros((CH, CH), F32)
            dq, dk, dcum = dq_s[hh], dk_s[hh], dc_s[hh]
            for i in range(first, CH // SB):
                r_i = _block_ref(c_s.at[hh], i)
                eq = jnp.exp(jnp.minimum(c - r_i, 0.0))
                ek = jnp.exp(jnp.minimum(r_i - c, cap))
                qi = (q * eq).astype(BF16).astype(F32)
                kei = (k * ek).astype(BF16).astype(F32)
                in_t = (rowi2 >= SB * i) & (rowi2 < SB * (i + 1))
                in_s = (coli2 >= SB * i) & (coli2 < SB * (i + 1))
                m_ts = in_t & ((coli2 < SB * i) if strict else (coli2 <= rowi2))
                m_st = in_s & ((rowi2 < SB * i) if strict else (rowi2 <= coli2))
                at = at + jnp.where(m_st, _dot_nt(kei, qi), 0.0)
                dq_i = _dot_nn(jnp.where(m_ts, d_a, 0.0), kei)
                dk_i = _dot_nn(jnp.where(m_st, d_at, 0.0), qi)
                dq = dq + dq_i * eq
                dk = dk + dk_i * ek
                dcum = dcum + (qi * dq_i - kei * dk_i)
            dq_s[hh] = dq
            dk_s[hh] = dk
            dc_s[hh] = dcum
            dv_s[hh] += _dot_nn(at, do)

        def pairs_exact(hh):
            dqd_s[hh] = jnp.zeros((CH, HD), F32)
            dkd_s[hh] = jnp.zeros((CH, HD), F32)
            for i in range(CH // SB):
                blk = slice(SB * i, SB * (i + 1))
                qb = q_s[hh, blk, :]
                cb = c_s[hh, blk, :]
                dob = do_s[hh, blk, :]
                dq_acc = jnp.zeros((SB, HD), F32)
                for s in range(SB):
                    row = SB * i + s
                    ks = k_s[hh, row:row + 1, :]
                    vs = v_s[hh, row:row + 1, :]
                    w = jnp.exp(jnp.minimum(cb - c_s[hh, row:row + 1, :], 0.0))
                    live = sbi >= s
                    a_col = jnp.where(live, jnp.sum(qb * ks * w, axis=-1, keepdims=True), 0.0)
                    da_col = jnp.where(live, jnp.sum(dob * vs, axis=-1, keepdims=True), 0.0)
                    dq_acc = dq_acc + da_col * ks * w
                    dkd_s[hh, row:row + 1, :] += jnp.sum(da_col * qb * w, axis=0, keepdims=True)
                    dv_s[hh, row:row + 1, :] += jnp.sum(a_col * dob, axis=0, keepdims=True)
                dqd_s[hh, blk, :] += dq_acc
            dq_d = dqd_s[hh]
            dk_d = dkd_s[hh]
            dq_s[hh] += dq_d
            dk_s[hh] += dk_d
            dc_s[hh] += q_s[hh] * dq_d - k_s[hh] * dk_d

        def gate_grads(hh, rows):
            lanes = slice(hh * HD, (hh + 1) * HD)
            lbv = lb_ref[:, lanes]
            hq = hq_ref[rows, lanes]
            _, sq, sf, f, fc, _ = _gates(hq, hf_ref[rows, lanes], lbv)
            dlogf = _chunk_rev_cumsum(dc_s[hh], rowi) + dl_s[hh]
            dfv = jnp.where(f > 1e-30, dlogf / fc, 0.0) - dk_s[hh]
            dlb_s[:, lanes] += jnp.sum(dfv * (1.0 - sf), axis=0, keepdims=True)
            df_ref[rows, lanes] = (dfv * (1.0 - lbv) * sf * (1.0 - sf)).astype(BF16)
            dq_ref[rows, lanes] = (dq_s[hh] * _dsilu(hq, sq)).astype(BF16)
            di_ref[rows, lanes] = dv_s[hh].astype(BF16)

        def chunk_with(pairs):
            def chunk(j, carry):
                n = nch - 1 - j
                rows = pl.ds(pl.multiple_of(n * CH, CH), CH)
                for hh in range(NH):
                    between_chunks(hh, n, rows)
                for hh in range(NH):
                    pairs(hh)
                for hh in range(NH):
                    gate_grads(hh, rows)
                return carry
            return chunk

        def pairs_mild(hh):
            pairs_matmul(hh, 0, DECAY_CAP, strict=False)

        def pairs_any(hh):
            pairs_matmul(hh, 1, 0.0, strict=True)
            pairs_exact(hh)

        mild = jnp.max(mild_ref[...]) > 0.5

        @pl.when(mild)
        def _():
            lax.fori_loop(0, nch, chunk_with(pairs_mild), 0, unroll=2)

        @pl.when(jnp.logical_not(mild))
        def _():
            lax.fori_loop(0, nch, chunk_with(pairs_any), 0)

        dlb_ref[...] = dlb_s[...]
        dgn_ref[...] = jnp.broadcast_to(dgn_s[...], (8, HD))

    col = lambda off: pl.BlockSpec((S, W), lambda h: (0, off // NH + h))
    head = pl.BlockSpec((S, W), lambda h: (0, h))
    vec = pl.BlockSpec((1, W), lambda h: (0, h))
    outs = _call(
        body, name=name, grid=(HEADS // NH,),
        in_specs=[head, col(HQ0), col(HF0), col(HI0), col(HG0), head,
                  pl.BlockSpec((NH, nch, HD, HD), lambda h: (h, 0, 0, 0)),
                  pl.BlockSpec((8, HD), lambda h: (h, 0)), vec, pl.BlockSpec((1, HD), lambda h: (0, 0))],
        out_specs=[head, head, head, head, vec, pl.BlockSpec((8, HD), lambda h: (h, 0))],
        out_shape=[jax.ShapeDtypeStruct((S, D), BF16)] * 4
        + [jax.ShapeDtypeStruct((1, D), F32), jax.ShapeDtypeStruct((8 * HEADS // NH, HD), F32)],
        scratch_shapes=[pltpu.VMEM((NH, CH, HD), F32)] * 11
        + [pltpu.VMEM((NH, 1, HD), F32), pltpu.VMEM((NH, HD, HD), F32), pltpu.VMEM((1, W), F32),
           pltpu.VMEM((1, HD), F32)],
        sem=("parallel",), args=(dbin, proj, proj, proj, proj, oraw, states, mild, lb, gn), carry=carry)
    dq, df, di, dg, dlb, dgn = outs[:6]
    return (dq, df, di, dg, dlb, dgn.reshape(HEADS // NH, 8, HD)[:, 0, :]), outs[6:]


def _lower_bounds(l0, l1):
    m = jnp.maximum(l0, l1)
    e0 = jnp.exp(l0 - m)
    e1 = jnp.exp(l1 - m)
    tot = e0 + e1
    p0 = e0 / tot
    p1 = e1 / tot
    return jnp.clip(p0 - p0, 0.0, 1.0), jnp.clip((p0 + p1) - p0, 0.0, 1.0)


def _lb_fwd(logits):
    def body(l_ref, o_ref):
        lb0, lb1 = _lower_bounds(l_ref[0:1, :], l_ref[1:2, :])
        o_ref[0:1, :] = lb0
        o_ref[1:2, :] = lb1

    return pl.pallas_call(body, name="lb_fwd", out_shape=jax.ShapeDtypeStruct((2, D), F32))(logits)


def _lb_bwd(logits, dlb):
    def body(l_ref, d_ref, o_ref):
        _, vjp = jax.vjp(_lower_bounds, l_ref[0:1, :], l_ref[1:2, :])
        g0, g1 = vjp((d_ref[0:1, :], d_ref[1:2, :]))
        o_ref[0:1, :] = g0
        o_ref[1:2, :] = g1

    return pl.pallas_call(body, name="lb_bwd", out_shape=jax.ShapeDtypeStruct((2, D), F32))(logits, dlb)


ADA_PAD = 128


def _ada_fwd(c_pad, w_ada, b_sh):
    ns = w_ada.shape[2]

    def body(c_ref, w_ref, b_ref, o_ref):
        cv = c_ref[...]
        ca = (cv * _sig(cv)).astype(BF16)
        for l in range(2):
            res = jnp.dot(ca, w_ref[l].astype(BF16), preferred_element_type=F32)
            o_ref[:, l * ns:(l + 1) * ns] = res[0:NDEV, :] + b_ref[l:l + 1, :]

    return pl.pallas_call(body, name="ada_fwd", out_shape=jax.ShapeDtypeStruct((NDEV, 2 * ns), F32),
                          compiler_params=_cp())(c_pad, w_ada, b_sh)


def _ada_wgrad(c_pad_t, d_ada_sh):
    ns = d_ada_sh.shape[2]

    def body(c_ref, d_ref, o_ref):
        cv = c_ref[...]
        ca = (cv * _sig(cv)).astype(BF16)
        for l in range(2):
            o_ref[l] = jnp.dot(ca, d_ref[l].astype(BF16), preferred_element_type=F32)

    return pl.pallas_call(body, name="ada_wgrad", out_shape=jax.ShapeDtypeStruct((2, D, ns), F32),
                          compiler_params=_cp())(c_pad_t, d_ada_sh)


def _sum_devices(g):
    _, R, C = g.shape

    def body(g_ref, o_ref):
        acc = g_ref[0]
        for d in range(1, NDEV):
            acc = acc + g_ref[d]
        o_ref[...] = acc

    return pl.pallas_call(body, name="sum_devices", out_shape=jax.ShapeDtypeStruct((R, C), F32),
                          compiler_params=_cp())(g)


def _adamw(w, g, m, v, name, carry=None):
    R, C = w.shape
    tr = _row_tile(R, max(8, (1 << 19) // C))

    def body(w_ref, g_ref, m_ref, v_ref, d_ref, nm_ref, nv_ref):
        d_ref[...], nm_ref[...], nv_ref[...] = _adamw_update(w_ref[...], g_ref[...], m_ref[...], v_ref[...])

    tile = pl.BlockSpec((tr, C), lambda i: (i, 0))
    return _call(body, name=name, grid=(R // tr,), in_specs=[tile] * 4, out_specs=[tile] * 3,
                 out_shape=[jax.ShapeDtypeStruct((R, C), F32)] * 3, sem=("parallel",), args=(w, g, m, v),
                 carry=carry)


def _adamw_update(w, g, m, v):
    nm = B1 * m + (1.0 - B1) * g
    nv = B2 * v + (1.0 - B2) * (g * g)
    m_hat = nm / (1.0 - B1 ** STEP)
    v_hat = nv / (1.0 - B2 ** STEP)
    return -LR * (m_hat / (jnp.sqrt(v_hat) + AEPS) + WD * w), nm, nv


SMALL_PARTS = (("b_ada", 0, 6, D), ("g_pre", 8, 2, D), ("g_post", 16, 2, D), ("lb_logits", 24, 2, D),
               ("pool_w", 32, 128, D), ("pool_scale", 160, 1, D), ("hgrn_norm_g", 168, 1, 2 * HD))


def _adamw_small(g_small, g_lb_logits, wmv):
    n = len(SMALL_PARTS)

    def body(g_ref, glb_ref, *refs):
        ins, outs = refs[:3 * n], refs[3 * n:]
        for p, (key, row0, rows, width) in enumerate(SMALL_PARTS):
            gv = glb_ref[...] if key == "lb_logits" else g_ref[row0:row0 + rows, 0:width]
            res = _adamw_update(ins[3 * p][...], gv, ins[3 * p + 1][...], ins[3 * p + 2][...])
            for t in range(3):
                outs[3 * p + t][...] = res[t]

    flat = [t for triple in wmv for t in triple]
    outs = pl.pallas_call(body, name="adamw_small",
                          out_shape=[jax.ShapeDtypeStruct(t.shape, F32) for t in flat],
                          compiler_params=_cp())(g_small, g_lb_logits, *flat)
    return [outs[3 * p:3 * p + 3] for p in range(n)]


def _cast_to_slot(place, w, l, name):
    _, R, C = w.shape
    tr = _row_tile(R, max(8, (1 << 19) // C))

    def body(p_ref, w_ref, o_ref):
        o_ref[...] = w_ref[...].astype(BF16)

    return pl.pallas_call(
        body, name=name, out_shape=jax.ShapeDtypeStruct((NCHIP, R, C), BF16),
        grid_spec=pltpu.PrefetchScalarGridSpec(
            num_scalar_prefetch=1, grid=(R // tr,),
            in_specs=[pl.BlockSpec((None, tr, C), lambda i, p_ref: (l, i, 0))],
            out_specs=pl.BlockSpec((None, tr, C), lambda i, p_ref: (p_ref[0], i, 0))),
        compiler_params=_cp(("parallel",)),
    )(place, w)


def _pair_add(core, g, got, name):
    _, R, C = g.shape
    r2 = R // 2
    tr = _row_tile(r2, max(8, (1 << 19) // C))
    nt = r2 // tr

    def body(c_ref, a_ref, b_ref, o_ref):
        o_ref[...] = (a_ref[...].astype(F32) + b_ref[...].astype(F32)).astype(o_ref.dtype)

    return pl.pallas_call(
        body, name=name, out_shape=jax.ShapeDtypeStruct((NCHIP, r2, C), BF16),
        grid_spec=pltpu.PrefetchScalarGridSpec(
            num_scalar_prefetch=1, grid=(NCHIP, nt),
            in_specs=[pl.BlockSpec((None, tr, C), lambda j, i, c_ref: (j, c_ref[0] * nt + i, 0)),
                      pl.BlockSpec((None, tr, C), lambda j, i, c_ref: (j, i, 0))],
            out_specs=pl.BlockSpec((None, tr, C), lambda j, i, c_ref: (j, i, 0))),
        compiler_params=_cp(("parallel", "parallel")),
    )(core, g, got)


def _chip_sum(place, part, recv, layer, both, name):
    _, r2, C = part.shape
    tr = _row_tile(r2, max(8, (1 << 18) // C))
    nt = r2 // tr

    def body(p_ref, own_ref, r_ref, *rest):
        o_ref = rest[-1]
        me = p_ref[0]
        own = own_ref[...].astype(F32)
        acc = None
        for j in range(NCHIP):
            slot = jnp.minimum(jnp.where(j > me, j - 1, j), NCHIP - 2)
            term = jnp.where(me == j, own, r_ref[slot].astype(F32))
            acc = term if acc is None else acc + term
        o_ref[...] = acc

    args = (place, part, recv) if both is None else (place, part, recv, both)
    return pl.pallas_call(
        body, name=name, out_shape=jax.ShapeDtypeStruct((2, 2 * r2, C), F32),
        grid_spec=pltpu.PrefetchScalarGridSpec(
            num_scalar_prefetch=1, grid=(nt,),
            in_specs=[pl.BlockSpec((None, tr, C), lambda i, p_ref: (p_ref[0], i, 0)),
                      pl.BlockSpec((NCHIP - 1, tr, C), lambda i, p_ref: (0, i, 0))] + [ANY] * (len(args) - 3),
            out_specs=pl.BlockSpec((None, tr, C), lambda i, p_ref: (layer, p_ref[1] * nt + i, 0))),
        input_output_aliases={} if both is None else {3: 0},
        compiler_params=_cp(("parallel",)),
    )(*args)


def _place():
    x, y, c = lax.axis_index("x"), lax.axis_index("y"), lax.axis_index("c")
    chips = [(1 - x, y), (x, 1 - y), (1 - x, 1 - y)]
    return x, y, c, chips


def _gather_small(blk, name):
    m_per, n = blk.shape

    def body(x_ref, out_ref, send_sems, recv_sems, local_sem):
        x, y, c, chips = _place()
        me, sibling = (x, y, c), (x, y, 1 - c)

        def rows(px, py, pc):
            return out_ref.at[pl.ds((4 * px + 2 * py + pc) * m_per, m_per), :]

        def copy(k, block, to, src=None):
            return pltpu.make_async_remote_copy(
                src_ref=rows(*block) if src is None else src, dst_ref=rows(*block),
                send_sem=send_sems.at[k], recv_sem=recv_sems.at[k], device_id=to, device_id_type=MESH)

        mine = pltpu.make_async_copy(x_ref, rows(*me), local_sem)
        mine.start()
        first = [copy(0, me, sibling, src=x_ref)]
        first += [copy(1 + j, me, (*chip, c), src=x_ref) for j, chip in enumerate(chips)]
        for cp in first:
            cp.start()
        passed = [copy(4 + j, (*chip, c), sibling) for j, chip in enumerate(chips)]
        for j, chip in enumerate(chips):
            copy(1 + j, (*chip, c), me).wait_recv()
            passed[j].start()
        copy(0, sibling, me).wait_recv()
        for j, chip in enumerate(chips):
            copy(4 + j, (*chip, 1 - c), me).wait_recv()
        for cp in first + passed:
            cp.wait_send()
        mine.wait()

    return pl.pallas_call(
        body, name=name, out_shape=jax.ShapeDtypeStruct((NDEV * m_per, n), blk.dtype),
        in_specs=[pl.BlockSpec(memory_space=pltpu.VMEM)], out_specs=pl.BlockSpec(memory_space=pltpu.VMEM),
        scratch_shapes=[pltpu.SemaphoreType.DMA((7,)), pltpu.SemaphoreType.DMA((7,)), pltpu.SemaphoreType.DMA],
        compiler_params=_cp(),
    )(blk)


def _gather_rows_carry(blk):
    m_per, n = blk.shape

    def rows(ref, px, py, pc):
        return ref.at[pl.ds((4 * px + 2 * py + pc) * m_per, m_per), :]

    def copy(ins, outs, send_sems, recv_sems, k, block, to, own=False):
        return pltpu.make_async_remote_copy(
            src_ref=ins[0] if own else rows(outs[0], *block), dst_ref=rows(outs[0], *block),
            send_sem=send_sems.at[k], recv_sem=recv_sems.at[k], device_id=to, device_id_type=MESH)

    def mine(ins, outs, send_sems):
        x, y, c, _ = _place()
        return pltpu.make_async_copy(ins[0], rows(outs[0], x, y, c), send_sems.at[7])

    def start(ins, outs, send_sems, recv_sems):
        x, y, c, chips = _place()
        mine(ins, outs, send_sems).start()
        copy(ins, outs, send_sems, recv_sems, 0, (x, y, c), (x, y, 1 - c), own=True).start()
        for j, chip in enumerate(chips):
            copy(ins, outs, send_sems, recv_sems, 1 + j, (x, y, c), (*chip, c), own=True).start()

    def finish(ins, outs, send_sems, recv_sems):
        x, y, c, chips = _place()
        for j, chip in enumerate(chips):
            copy(ins, outs, send_sems, recv_sems, 1 + j, (*chip, c), (x, y, c)).wait_recv()
            copy(ins, outs, send_sems, recv_sems, 4 + j, (*chip, c), (x, y, 1 - c)).start()
        copy(ins, outs, send_sems, recv_sems, 0, (x, y, 1 - c), (x, y, c)).wait_recv()
        for j, chip in enumerate(chips):
            copy(ins, outs, send_sems, recv_sems, 4 + j, (*chip, 1 - c), (x, y, c)).wait_recv()
        copy(ins, outs, send_sems, recv_sems, 0, (x, y, c), (x, y, 1 - c), own=True).wait_send()
        for j, chip in enumerate(chips):
            copy(ins, outs, send_sems, recv_sems, 1 + j, (x, y, c), (*chip, c), own=True).wait_send()
            copy(ins, outs, send_sems, recv_sems, 4 + j, (*chip, c), (x, y, 1 - c)).wait_send()
        mine(ins, outs, send_sems).wait()

    return _Carry([blk], [jax.ShapeDtypeStruct((NDEV * m_per, n), blk.dtype)], {}, 8, start, finish)


def _gather_carry(shards, piece=(0, 1, 1)):
    n = len(shards)
    first, count, of = piece

    def rows(ref, half):
        r2 = ref.shape[1] // 2
        return pl.ds(half * r2 + first * (r2 // of), count * (r2 // of))

    def over_ici(outs, send_sems, recv_sems, a, j, chip_xy, slot):
        x, y, c, _ = _place()
        blk = outs[a].at[slot, rows(outs[a], c), :]
        return pltpu.make_async_remote_copy(
            src_ref=blk, dst_ref=blk, send_sem=send_sems.at[6 * a + j], recv_sem=recv_sems.at[6 * a + j],
            device_id=(*chip_xy, c), device_id_type=MESH)

    def over_d2d(outs, send_sems, recv_sems, a, j, slot, half):
        x, y, c, _ = _place()
        blk = outs[a].at[slot, rows(outs[a], half), :]
        return pltpu.make_async_remote_copy(
            src_ref=blk, dst_ref=blk, send_sem=send_sems.at[6 * a + 3 + j], recv_sem=recv_sems.at[6 * a + 3 + j],
            device_id=(x, y, 1 - c), device_id_type=MESH)

    def start(ins, outs, send_sems, recv_sems):
        x, y, c, chips = _place()
        for a in range(n):
            for j, chip_xy in enumerate(chips):
                over_ici(outs, send_sems, recv_sems, a, j, chip_xy, 2 * x + y).start()

    def finish(ins, outs, send_sems, recv_sems):
        x, y, c, chips = _place()
        for a in range(n):
            for j, (cx, cy) in enumerate(chips):
                over_ici(outs, send_sems, recv_sems, a, j, (cx, cy), 2 * cx + cy).wait_recv()
                over_d2d(outs, send_sems, recv_sems, a, j, 2 * cx + cy, c).start()
        for a in range(n):
            for j, (cx, cy) in enumerate(chips):
                over_d2d(outs, send_sems, recv_sems, a, j, 2 * cx + cy, 1 - c).wait_recv()
        for a in range(n):
            for j, (cx, cy) in enumerate(chips):
                over_ici(outs, send_sems, recv_sems, a, j, (cx, cy), 2 * x + y).wait_send()
                over_d2d(outs, send_sems, recv_sems, a, j, 2 * cx + cy, c).wait_send()

    return _Carry(shards, [jax.ShapeDtypeStruct(s.shape, s.dtype) for s in shards],
                  {a: a for a in range(n)}, 6 * n, start, finish)


def _rs_pair(grads, name):
    n = len(grads)

    def body(*refs):
        ins, gots = refs[:n], refs[n:2 * n]
        send_sems, recv_sems = refs[2 * n:]
        x, y, c, _ = _place()
        cps = []
        for a in range(n):
            r2 = ins[a].shape[1] // 2
            cp = pltpu.make_async_remote_copy(
                src_ref=ins[a].at[:, pl.ds((1 - c) * r2, r2), :], dst_ref=gots[a],
                send_sem=send_sems.at[a], recv_sem=recv_sems.at[a],
                device_id=(x, y, 1 - c), device_id_type=MESH)
            cp.start()
            cps.append(cp)
        for cp in cps:
            cp.wait()

    half = [jax.ShapeDtypeStruct((NCHIP, g.shape[1] // 2, g.shape[2]), g.dtype) for g in grads]
    return pl.pallas_call(
        body, name=name, out_shape=half, in_specs=[ANY] * n, out_specs=[ANY] * n,
        scratch_shapes=[pltpu.SemaphoreType.DMA((n,)), pltpu.SemaphoreType.DMA((n,))],
        compiler_params=_cp(),
    )(*grads)


def _chips_carry(parts):
    n = len(parts)

    def send(ins, outs, send_sems, recv_sems, a, j, chip_xy):
        x, y, c, _ = _place()
        me, them = 2 * x + y, 2 * chip_xy[0] + chip_xy[1]
        return pltpu.make_async_remote_copy(
            src_ref=ins[a].at[them], dst_ref=outs[a].at[me - (me > them).astype(jnp.int32)],
            send_sem=send_sems.at[3 * a + j], recv_sem=recv_sems.at[3 * a + j],
            device_id=(*chip_xy, c), device_id_type=MESH)

    def start(ins, outs, send_sems, recv_sems):
        _, _, _, chips = _place()
        for a in range(n):
            for j, chip_xy in enumerate(chips):
                send(ins, outs, send_sems, recv_sems, a, j, chip_xy).start()

    def finish(ins, outs, send_sems, recv_sems):
        x, y, c, chips = _place()
        me = 2 * x + y
        for a in range(n):
            for j, (cx, cy) in enumerate(chips):
                them = 2 * cx + cy
                blk = outs[a].at[them - (them > me).astype(jnp.int32)]
                pltpu.make_async_remote_copy(
                    src_ref=blk, dst_ref=blk, send_sem=send_sems.at[3 * a + j], recv_sem=recv_sems.at[3 * a + j],
                    device_id=(cx, cy, c), device_id_type=MESH).wait_recv()
        for a in range(n):
            for j, chip_xy in enumerate(chips):
                send(ins, outs, send_sems, recv_sems, a, j, chip_xy).wait_send()

    return _Carry(parts, [jax.ShapeDtypeStruct((NCHIP - 1,) + p.shape[1:], p.dtype) for p in parts], {},
                  3 * n, start, finish)


def _rs_swap(fulls):
    n = len(fulls)

    def body(*refs):
        outs = refs[n:2 * n]
        send_sems, recv_sems = refs[2 * n:]
        x, y, c, _ = _place()
        cps = []
        for a in range(n):
            r2 = outs[a].shape[1] // 2
            mine = outs[a].at[:, pl.ds(c * r2, r2), :]
            cp = pltpu.make_async_remote_copy(
                src_ref=mine, dst_ref=mine, send_sem=send_sems.at[a], recv_sem=recv_sems.at[a],
                device_id=(x, y, 1 - c), device_id_type=MESH)
            cp.start()
            cps.append(cp)
        for a in range(n):
            r2 = outs[a].shape[1] // 2
            blk = outs[a].at[:, pl.ds((1 - c) * r2, r2), :]
            pltpu.make_async_remote_copy(
                src_ref=blk, dst_ref=blk, send_sem=send_sems.at[a], recv_sem=recv_sems.at[a],
                device_id=(x, y, 1 - c), device_id_type=MESH).wait_recv()
        for cp in cps:
            cp.wait_send()

    return pl.pallas_call(
        body, name="rs_swap", out_shape=[jax.ShapeDtypeStruct(f.shape, f.dtype) for f in fulls],
        in_specs=[ANY] * n, out_specs=[ANY] * n, input_output_aliases={a: a for a in range(n)},
        scratch_shapes=[pltpu.SemaphoreType.DMA((n,)), pltpu.SemaphoreType.DMA((n,))],
        compiler_params=_cp(),
    )(*fulls)


def _mm_ride(a, b, carry, **kw):
    if carry is None:
        return _mm(a, b, **kw), []
    return _mm(a, b, carry=carry, **kw)


def _layer_fwd(l, x, ada, w, small, ride, target=None):
    shift, scale, gate = ada[:, 0:D], ada[:, D:2 * D], ada[:, 2 * D:3 * D]
    h, h_t = _prenorm_fwd(x, small["g_pre"][l], scale, shift, f"prenorm_fwd{l}")
    carry, landed = ride("proj")
    proj, outs = _mm_ride(h, w["w_in"][l], carry, name=f"proj{l}", b_mode="nn_sh", tm=2048)
    landed(outs)
    a_in, a_in_t = _pool_fwd(proj, small["pool_w"][l], small["pool_scale"][l], f"pool_fwd{l}")
    carry, landed = ride("hgrn")
    (b_in, b_in_t, o_raw, states, mild), outs = _hgrn_fwd(proj, small["lb"][l], small["hgrn_norm_g"][l],
                                                         f"hgrn_fwd{l}", carry=carry)
    landed(outs)
    carry, landed = ride("tail")
    (br_a, br_b, merged_t, y, *x_new), outs = _layer_tail_fwd(
        proj, a_in, b_in, x, w["w_pool_o"][l], w["w_hgrn_o"][l].reshape(D, D), w["w_out"][l].reshape(D, D),
        gate, small["g_post"][l], f"tail_fwd{l}", target=target, carry=carry)
    landed(outs)
    saved = dict(x=x, h_t=h_t, proj=proj, a_in_t=a_in_t, b_in_t=b_in_t, o_raw=o_raw, states=states, mild=mild,
                 br_a=br_a, br_b=br_b, merged_t=merged_t, y=y, scale=scale, gate=gate)
    return x_new, saved


def _layer_bwd(l, dxn, sv, w, small, ride):
    dy, dbr_a, dbr_b, dmg, da_in, db_in, dgate, dg_post = _layer_head_bwd(
        dxn, sv["y"], sv["proj"], sv["br_a"], sv["br_b"], w["w_pool_o"][l], w["w_hgrn_o"][l].reshape(D, D),
        w["w_out"][l].reshape(D, D), sv["gate"], small["g_post"][l], f"head_bwd{l}")
    gw_out = _mm(sv["merged_t"], dy, name=f"gw_out{l}", out_dtype=BF16)
    gw_pool_o = _mm(sv["a_in_t"], dbr_a, name=f"gw_pool_o{l}", out_shards=NCHIP, out_dtype=BF16)
    gw_hgrn_o = _mm(sv["b_in_t"], dbr_b, name=f"gw_hgrn_o{l}", out_dtype=BF16)
    big = dict(w_pool_o=gw_pool_o, w_hgrn_o=gw_hgrn_o.reshape(NCHIP, D // NCHIP, D),
               w_out=gw_out.reshape(NCHIP, D // NCHIP, D))
    carry, landed = ride["hgrn"](big)
    (dhq, dhf, dhi, dhg, dlb, dgn), outs = _hgrn_bwd(db_in, sv["proj"], sv["o_raw"], sv["states"], sv["mild"],
                                                     small["lb"][l], small["hgrn_norm_g"][l], f"hgrn_bwd{l}",
                                                     carry=carry)
    landed(outs)
    dpv, dpg, dpw, dpsc = _pool_bwd(da_in, sv["proj"], small["pool_w"][l], small["pool_scale"][l],
                                    f"pool_bwd{l}")
    dproj = jnp.concatenate([dpv, dpg, dhq, dhf, dhi, dhg, dmg], axis=1)
    little = dict(dgate=dgate, g_post=dg_post, pool_w=dpw, pool_scale=dpsc, lb=dlb,
                  hgrn_norm_g=jnp.sum(dgn, axis=0, keepdims=True))
    carry, landed = ride["gw_in"](little)
    big["w_in"], outs = _mm_ride(sv["h_t"], dproj, carry, name=f"gw_in{l}", out_shards=NCHIP, out_dtype=BF16)
    landed(outs)
    carry, landed = ride["d_h"](big)
    dh, outs = _mm_ride(dproj, w["w_in"][l], carry, name=f"d_h{l}", b_mode="nt_shk", tn=1024)
    landed(outs)
    dx, dshift, dscale, dg_pre = _prenorm_bwd(dh, dxn, sv["x"], small["g_pre"][l], sv["scale"],
                                              f"prenorm_bwd{l}")
    little.update(dshift=dshift, dscale=dscale, g_pre=dg_pre)
    return dx, big, little


SMALL_ROWS = 176


def _rows8(t):
    t = t.reshape(-1, D)
    return jnp.pad(t, ((0, -t.shape[0] % 8), (0, 0)))


def _pack_small_weights(b_ada, g_pre, g_post, lb_logits, pool_w, pool_scale, hgrn_norm_g):
    gn = jnp.pad(hgrn_norm_g.reshape(1, 2 * HD), ((0, 0), (0, D - 2 * HD)))
    return jnp.concatenate([_rows8(b_ada), _rows8(g_pre), _rows8(g_post), _rows8(lb_logits), _rows8(pool_w),
                            _rows8(pool_scale), _rows8(gn)], axis=0)


def _pack_small(parts):
    both = lambda key: jnp.stack([parts[l][key] for l in range(2)])
    d_ada = jnp.stack([jnp.concatenate([p["dshift"], p["dscale"], p["dgate"]], axis=1) for p in parts])
    return _pack_small_weights(d_ada, both("g_pre"), both("g_post"), both("lb"), both("pool_w"),
                               both("pool_scale"), both("hgrn_norm_g"))


def _unpack_small(p):
    return (p[0:6].reshape(2, 3 * D), p[8:10], p[16:18], p[24:26], p[32:160].reshape(2, GROUPS, 128, 128),
            p[160:161].reshape(2, POOL_W), p[168:169, 0:2 * HD].reshape(2, HD))


def kernel(x, c, w_ada, b_ada, g_pre, g_post, w_in, pool_w, pool_scale, lb_logits, hgrn_norm_g, w_pool_o, w_hgrn_o, w_out, loss_target, m_w_ada, m_b_ada, m_g_pre, m_g_post, m_w_in, m_pool_w, m_pool_scale, m_lb_logits, m_hgrn_norm_g, m_w_pool_o, m_w_hgrn_o, m_w_out, v_w_ada, v_b_ada, v_g_pre, v_g_post, v_w_in, v_pool_w, v_pool_scale, v_lb_logits, v_hgrn_norm_g, v_w_pool_o, v_w_hgrn_o, v_w_out):
    ax, ay, ac = lax.axis_index("x"), lax.axis_index("y"), lax.axis_index("c")
    chip = 2 * ax + ay
    dev = 2 * chip + ac
    xe, te = x[0], loss_target[0]
    ada_s = w_ada.shape[2]

    big_names = ("w_in", "w_pool_o", "w_hgrn_o", "w_out")
    big_w = (w_in, w_pool_o, w_hgrn_o, w_out)
    core = jnp.stack([ac]).astype(jnp.int32)
    place = jnp.stack([chip, ac]).astype(jnp.int32)
    slots = {(k, l): _cast_to_slot(place, t, l, f"cast_{k}{l}") for l in range(2) for k, t in zip(big_names, big_w)}
    w = {k: [None, None] for k in big_names}
    (w["w_in"][0],) = _run_carry(_gather_carry([slots["w_in", 0]]), "gather_w_in0")
    def fills(keys):
        def landed(outs):
            for (k, l), o in zip(keys, outs):
                w[k][l] = slots[k, l] = o
        return landed

    rest0 = [(k, 0) for k in big_names[1:]]
    rest1 = [(k, 1) for k in big_names[1:]]
    no_carry = (None, lambda outs: None)

    def ride_fwd0(stage):
        if stage == "proj":
            return (_join_carries(_gather_carry([slots[t] for t in rest0]),
                                  _gather_carry([slots["w_in", 1]], piece=(0, 1, 4))),
                    fills(rest0 + [("w_in", 1)]))
        if stage == "hgrn":
            return _gather_carry([slots["w_in", 1]], piece=(1, 3, 4)), fills([("w_in", 1)])
        return _gather_carry([slots[t] for t in rest1]), fills(rest1)

    c_all = _gather_small(jnp.broadcast_to(c, (8, D)), "gather_c").reshape(NDEV, 8, D)[:, 0, :]
    c_pad = jnp.pad(c_all, ((0, ADA_PAD - NDEV), (0, 0)))
    b_sh = lax.dynamic_slice(b_ada, (0, chip * ada_s), (2, ada_s))
    ada_cols = _gather_small(_ada_fwd(c_pad, w_ada, b_sh), "gather_ada")
    ada_cols = ada_cols.reshape(NCHIP, 2, NDEV, 2, ada_s)[:, 0]
    ada_all = jnp.transpose(ada_cols, (2, 1, 0, 3)).reshape(2, NDEV, 3 * D)
    ada_me = lax.dynamic_slice(ada_all, (0, dev, 0), (2, 1, 3 * D))

    lbs = _lb_fwd(lb_logits)
    small = dict(g_pre=g_pre[:, None, :], g_post=g_post[:, None, :], pool_w=pool_w,
                 pool_scale=pool_scale[:, None, :], lb=lbs[:, None, :], hgrn_norm_g=hgrn_norm_g[:, None, :])

    (x1,), sv0 = _layer_fwd(0, xe, ada_me[0], w, small, ride_fwd0)
    (dx2, loss_blk), sv1 = _layer_fwd(1, x1, ada_me[1], w, small, lambda stage: no_carry, target=te)

    parts, recv = {}, {}

    def pair_sums(keys, grads, tag):
        got = _rs_pair(grads, f"rs_pair_{tag}")
        for kl, g, o in zip(keys, grads, got):
            parts[kl] = _pair_add(core, g, o, f"rs_add_{kl[0]}{kl[1]}")

    def exchange(keys):
        def landed(outs):
            recv.update(zip(keys, outs))
        return _chips_carry([parts[kl] for kl in keys]), landed

    def early(l):
        return [(k, l) for k in big_names[1:]]

    def ride_hgrn1(big):
        pair_sums(early(1), [big[k] for k in big_names[1:]], "l1_early")
        return exchange(early(1))

    def ride_d_h1(big):
        pair_sums([("w_in", 1)], [big["w_in"]], "l1_w_in")
        return no_carry

    def ride_hgrn0(big):
        pair_sums(early(0), [big[k] for k in big_names[1:]], "l0_early")
        return exchange([("w_in", 1)] + early(0))

    def ride_d_h0(big):
        pair_sums([("w_in", 0)], [big["w_in"]], "l0_w_in")
        return exchange([("w_in", 0)])

    dx1, big1, little1 = _layer_bwd(1, dx2, sv1, w, small,
                                    dict(hgrn=ride_hgrn1, gw_in=lambda little: no_carry, d_h=ride_d_h1))

    gathered = {}
    zero_row = jnp.zeros((1, D), F32)

    def ride_gw_in0(little):
        so_far = dict(little, dshift=zero_row, dscale=zero_row, g_pre=zero_row)

        def landed(outs):
            (gathered["early"],) = outs
        return _gather_rows_carry(_pack_small([so_far, little1])), landed

    dx0, big0, little0 = _layer_bwd(0, dx1, sv0, w, small,
                                    dict(hgrn=ride_hgrn0, gw_in=ride_gw_in0, d_h=ride_d_h0))
    loss = lax.psum(loss_blk[0, 0], ("x", "y", "c"))
    late = _rows8(jnp.stack([little0["dshift"], little0["dscale"], little0["g_pre"]]))
    late = _gather_small(late, "gather_small_late").reshape(NDEV, 8, D)
    packed = gathered["early"].reshape(NDEV, SMALL_ROWS, D)
    packed = packed.at[:, 0:2, :].set(late[:, 0:2, :]).at[:, 8:9, :].set(late[:, 2:3, :])
    red = []
    for k in big_names:
        both = _chip_sum(place, parts[k, 1], recv[k, 1], 1, None, f"rs_sum_{k}1")
        red.append(_chip_sum(place, parts[k, 0], recv[k, 0], 0, both, f"rs_sum_{k}0"))
    g_big = dict(zip(big_names, _rs_swap(red)))

    def upd(wt, g, m, v, name, carry=None):
        shp = wt.shape
        two = lambda t: t.reshape(-1, shp[-1])
        res = _adamw(two(wt), two(g), two(m), two(v), name, carry)
        return [t.reshape(shp) for t in res[:3]], res[3:]

    u_w_in, _ = upd(w_in, g_big["w_in"], m_w_in, v_w_in, "adamw_w_in")
    g_small = _sum_devices(packed)
    g_b_ada, g_g_pre, g_g_post, g_lb, g_pool_w, g_pool_scale, g_norm_g = _unpack_small(g_small)
    g_lb_logits = _lb_bwd(lb_logits, g_lb)
    d_ada_all = packed[:, 0:6, :].reshape(NDEV, 2, 3 * D)
    d_ada_sh = lax.dynamic_slice(jnp.transpose(d_ada_all, (1, 0, 2)), (0, 0, chip * ada_s), (2, NDEV, ada_s))
    d_ada_sh = jnp.pad(d_ada_sh, ((0, 0), (0, ADA_PAD - NDEV), (0, 0)))
    g_w_ada = _ada_wgrad(c_pad.T, d_ada_sh)

    u_w_ada, _ = upd(w_ada, g_w_ada, m_w_ada, v_w_ada, "adamw_w_ada")
    u_w_pool_o, _ = upd(w_pool_o, g_big["w_pool_o"], m_w_pool_o, v_w_pool_o, "adamw_w_pool_o")
    u_w_hgrn_o, _ = upd(w_hgrn_o, g_big["w_hgrn_o"], m_w_hgrn_o, v_w_hgrn_o, "adamw_w_hgrn_o")
    u_w_out, _ = upd(w_out, g_big["w_out"], m_w_out, v_w_out, "adamw_w_out")
    small_w = dict(b_ada=(b_ada, m_b_ada, v_b_ada), g_pre=(g_pre, m_g_pre, v_g_pre),
                   g_post=(g_post, m_g_post, v_g_post), lb_logits=(lb_logits, m_lb_logits, v_lb_logits),
                   pool_w=(pool_w, m_pool_w, v_pool_w), pool_scale=(pool_scale, m_pool_scale, v_pool_scale),
                   hgrn_norm_g=(hgrn_norm_g, m_hgrn_norm_g, v_hgrn_norm_g))
    in_rows = [tuple(t.reshape(rows, width) for t in small_w[key]) for key, _, rows, width in SMALL_PARTS]
    u_rows = _adamw_small(g_small, g_lb_logits, in_rows)
    u_small = {key: [t.reshape(small_w[key][0].shape) for t in u_rows[p]]
               for p, (key, _, _, _) in enumerate(SMALL_PARTS)}

    grads_out = (g_w_ada, g_b_ada, g_g_pre, g_g_post, g_big["w_in"], g_pool_w, g_pool_scale, g_lb_logits,
                 g_norm_g, g_big["w_pool_o"], g_big["w_hgrn_o"], g_big["w_out"])

    def ordered(k):
        s = lambda key: u_small[key][k]
        return (u_w_ada[k], s("b_ada"), s("g_pre"), s("g_post"), u_w_in[k], s("pool_w"), s("pool_scale"),
                s("lb_logits"), s("hgrn_norm_g"), u_w_pool_o[k], u_w_hgrn_o[k], u_w_out[k])

    return (loss, dx0[None], *grads_out, *ordered(0), *ordered(1), *ordered(2))
```

```python
import functools

import jax
import jax.numpy as jnp
from jax import lax
from jax.experimental import pallas as pl
from jax.experimental.pallas import tpu as pltpu

F32 = jnp.float32
BF16 = jnp.bfloat16
MESH = pl.DeviceIdType.MESH

D = 1024
HEADS = 8
HD = 128
GROUPS = 4
POOL_W = 512
WINDOWS = (2, 4, 8, 16)
CH = 64
SB = 16
NH = 2
IN_W = 7168
NCHIP = 4
NDEV = 8
EPS = 1e-6
PV0, PG0, HQ0, HF0, HI0, HG0 = 0, 4, 8, 16, 24, 32
MGP_BLK, MGH_BLK = 5, 6

LR, B1, B2, AEPS, WD, STEP = 0.001, 0.9, 0.999, 1e-08, 0.01, 10
VMEM_LIMIT = 56 * 1024 * 1024


def _cp(sem=None, **kw):
    if sem is not None:
        kw["dimension_semantics"] = sem
    return pltpu.CompilerParams(vmem_limit_bytes=VMEM_LIMIT, **kw)


def _sig(z):
    return 1.0 / (1.0 + jnp.exp(-z))


def _dsilu(z, s):
    return s * (1.0 + z * (1.0 - s))


def _row_tile(rows, cap):
    if rows <= cap:
        return rows
    t = 1 << (cap.bit_length() - 1)
    while rows % t:
        t //= 2
    return t


ANY = pl.BlockSpec(memory_space=pl.ANY)


class _Carry:
    def __init__(self, ins, outs, aliases, n_sem, start, finish):
        self.ins, self.outs, self.aliases, self.n_sem = list(ins), list(outs), dict(aliases), n_sem
        self.start, self.finish = start, finish


class _SemWindow:
    def __init__(self, ref, base):
        self._ref, self._base = ref, base

    @property
    def at(self):
        return self

    def __getitem__(self, k):
        return self._ref.at[self._base + k]


def _join_carries(*carries):
    ins, outs, aliases, spans, n_sem = [], [], {}, [], 0
    for cr in carries:
        aliases.update({len(ins) + i: len(outs) + o for i, o in cr.aliases.items()})
        spans.append((len(ins), len(cr.ins), len(outs), len(cr.outs), n_sem))
        ins, outs, n_sem = ins + cr.ins, outs + cr.outs, n_sem + cr.n_sem

    def run(which):
        def fn(i_refs, o_refs, send_sems, recv_sems):
            for cr, (i0, ni, o0, no, s0) in zip(carries, spans):
                getattr(cr, which)(i_refs[i0:i0 + ni], o_refs[o0:o0 + no], _SemWindow(send_sems, s0),
                                   _SemWindow(recv_sems, s0))
        return fn

    return _Carry(ins, outs, aliases, n_sem, run("start"), run("finish"))


def _call(body, *, name, grid, in_specs, out_specs, out_shape, args, scratch_shapes=(), sem=None, carry=None):
    in_specs, out_specs, out_shape = list(in_specs), list(out_specs), list(out_shape)
    scratch_shapes = list(scratch_shapes)
    if carry is None:
        outs = pl.pallas_call(body, name=name, grid=grid, in_specs=in_specs, out_specs=out_specs,
                              out_shape=out_shape, scratch_shapes=scratch_shapes,
                              compiler_params=_cp(sem))(*args)
        return list(outs)
    n_in, n_out, n_scr = len(in_specs), len(out_specs), len(scratch_shapes)
    c_in, c_out = len(carry.ins), len(carry.outs)

    def wrapped(*refs):
        k_in, rest = refs[:n_in], refs[n_in:]
        ci, rest = rest[:c_in], rest[c_in:]
        k_out, rest = rest[:n_out], rest[n_out:]
        co, rest = rest[:c_out], rest[c_out:]
        k_scr, (ssem, rsem) = rest[:n_scr], rest[n_scr:]
        pids = [pl.program_id(d) for d in range(len(grid))]
        first = functools.reduce(jnp.logical_and, [p == 0 for p in pids])
        last = functools.reduce(jnp.logical_and, [p == g - 1 for p, g in zip(pids, grid)])

        @pl.when(first)
        def _():
            carry.start(ci, co, ssem, rsem)

        body(*k_in, *k_out, *k_scr)

        @pl.when(last)
        def _():
            carry.finish(ci, co, ssem, rsem)

    outs = pl.pallas_call(
        wrapped, name=name, grid=grid, in_specs=in_specs + [ANY] * c_in, out_specs=out_specs + [ANY] * c_out,
        out_shape=out_shape + carry.outs,
        input_output_aliases={n_in + i: n_out + o for i, o in carry.aliases.items()},
        scratch_shapes=scratch_shapes + [pltpu.SemaphoreType.DMA((carry.n_sem,))] * 2,
        compiler_params=_cp(("arbitrary",) * len(grid)),
    )(*args, *carry.ins)
    return list(outs)


def _run_carry(carry, name):
    c_in, c_out = len(carry.ins), len(carry.outs)

    def body(*refs):
        ci, co, (ssem, rsem) = refs[:c_in], refs[c_in:c_in + c_out], refs[c_in + c_out:]
        carry.start(ci, co, ssem, rsem)
        carry.finish(ci, co, ssem, rsem)

    outs = pl.pallas_call(
        body, name=name, in_specs=[ANY] * c_in, out_specs=[ANY] * c_out, out_shape=carry.outs,
        input_output_aliases=carry.aliases,
        scratch_shapes=[pltpu.SemaphoreType.DMA((carry.n_sem,))] * 2, compiler_params=_cp(),
    )(*carry.ins)
    return list(outs)


def _mm(a, b, *, name, b_mode="nn", out_shards=0, tm=1024, tn=256, tk=None, out_dtype=F32, carry=None):
    M, K = a.shape
    if b_mode == "nn":
        N = b.shape[1]
    elif b_mode == "nt":
        N = b.shape[0]
    elif b_mode == "nn_sh":
        N = b.shape[0] * b.shape[2]
    else:
        N = b.shape[1]
    tm = _row_tile(M, tm)
    if b_mode == "nn_sh":
        tn = _row_tile(b.shape[2], tn)
    elif out_shards:
        tn = _row_tile(N // out_shards, tn)
    else:
        tn = _row_tile(N, tn)
    if tk is None:
        tk = K if b_mode != "nt_shk" else b.shape[2]
    if b_mode == "nt_shk":
        tk = _row_tile(b.shape[2], tk)
    nm, nn, nk = M // tm, N // tn, K // tk

    a_spec = pl.BlockSpec((tm, tk), lambda m, n, k: (m, k))
    if b_mode == "nn":
        b_spec = pl.BlockSpec((tk, tn), lambda m, n, k: (k, n))
    elif b_mode == "nt":
        b_spec = pl.BlockSpec((tn, tk), lambda m, n, k: (n, k))
    elif b_mode == "nn_sh":
        nps = b.shape[2] // tn
        b_spec = pl.BlockSpec((None, tk, tn), lambda m, n, k: (n // nps, k, n % nps))
    else:
        kps = b.shape[2] // tk
        b_spec = pl.BlockSpec((None, tn, tk), lambda m, n, k: (k // kps, n, k % kps))
    if out_shards:
        ops = (N // out_shards) // tn
        o_spec = pl.BlockSpec((None, tm, tn), lambda m, n, k: (n // ops, m, n % ops))
        o_shape = jax.ShapeDtypeStruct((out_shards, M, N // out_shards), out_dtype)
    else:
        o_spec = pl.BlockSpec((tm, tn), lambda m, n, k: (m, n))
        o_shape = jax.ShapeDtypeStruct((M, N), out_dtype)
    trans_b = b_mode in ("nt", "nt_shk")
    dn = (((1,), (1,)), ((), ())) if trans_b else (((1,), (0,)), ((), ()))

    def body(a_ref, b_ref, o_ref, acc_ref):
        k = pl.program_id(2)

        @pl.when(k == 0)
        def _():
            acc_ref[...] = jnp.zeros(acc_ref.shape, F32)

        acc_ref[...] += lax.dot_general(a_ref[...].astype(BF16), b_ref[...].astype(BF16), dn,
                                        preferred_element_type=F32)

        @pl.when(k == nk - 1)
        def _():
            o_ref[...] = acc_ref[...].astype(o_ref.dtype)

    outs = _call(body, name=name, grid=(nm, nn, nk), in_specs=[a_spec, b_spec], out_specs=[o_spec],
                 out_shape=[o_shape], scratch_shapes=[pltpu.VMEM((tm, tn), F32)],
                 sem=("parallel", "parallel", "arbitrary"), args=(a, b), carry=carry)
    return outs[0] if carry is None else (outs[0], outs[1:])


def _rowvec(n=D):
    return pl.BlockSpec((1, n), lambda i: (0, 0))


def _prenorm_fwd(x, g, scale, shift, name):
    S = x.shape[0]
    tr = _row_tile(S, 256)

    def body(x_ref, g_ref, sc_ref, sh_ref, h_ref, ht_ref):
        xv = x_ref[...]
        r = lax.rsqrt(jnp.mean(xv * xv, axis=-1, keepdims=True) + EPS)
        hv = (xv * r) * g_ref[...] * (1.0 + sc_ref[...]) + sh_ref[...]
        h_ref[...] = hv.astype(BF16)
        ht_ref[...] = hv.T.astype(BF16)

    return pl.pallas_call(
        body, name=name, grid=(S // tr,),
        in_specs=[pl.BlockSpec((tr, D), lambda i: (i, 0)), _rowvec(), _rowvec(), _rowvec()],
        out_specs=[pl.BlockSpec((tr, D), lambda i: (i, 0)), pl.BlockSpec((D, tr), lambda i: (0, i))],
        out_shape=[jax.ShapeDtypeStruct((S, D), BF16), jax.ShapeDtypeStruct((D, S), BF16)],
        compiler_params=_cp(("parallel",)),
    )(x, g, scale, shift)


def _prenorm_bwd(dh, dxn, x, g, scale, name):
    S = x.shape[0]
    tr = _row_tile(S, 256)

    def body(dh_ref, dxn_ref, x_ref, g_ref, sc_ref, dx_ref, dsh_ref, dsc_ref, dg_ref):
        i = pl.program_id(0)

        @pl.when(i == 0)
        def _():
            dsh_ref[...] = jnp.zeros((1, D), F32)
            dsc_ref[...] = jnp.zeros((1, D), F32)
            dg_ref[...] = jnp.zeros((1, D), F32)

        xv = x_ref[...]
        dhv = dh_ref[...]
        gv = g_ref[...]
        mod = 1.0 + sc_ref[...]
        r = lax.rsqrt(jnp.mean(xv * xv, axis=-1, keepdims=True) + EPS)
        xh = xv * r
        dsh_ref[...] += jnp.sum(dhv, axis=0, keepdims=True)
        dsc_ref[...] += jnp.sum(dhv * (xh * gv), axis=0, keepdims=True)
        dg_ref[...] += jnp.sum(dhv * mod * xh, axis=0, keepdims=True)
        u = dhv * mod * gv
        dx_ref[...] = dxn_ref[...] + r * u - xv * (r * r * r) * jnp.mean(u * xv, axis=-1, keepdims=True)

    tile = pl.BlockSpec((tr, D), lambda i: (i, 0))
    return pl.pallas_call(
        body, name=name, grid=(S // tr,),
        in_specs=[tile, tile, tile, _rowvec(), _rowvec()],
        out_specs=[tile, _rowvec(), _rowvec(), _rowvec()],
        out_shape=[jax.ShapeDtypeStruct((S, D), F32)] + [jax.ShapeDtypeStruct((1, D), F32)] * 3,
        compiler_params=_cp(("arbitrary",)),
    )(dh, dxn, x, g, scale)


def _postnorm_fwd(x, y, gate, g, name):
    S = x.shape[0]
    tr = _row_tile(S, 256)

    def body(x_ref, y_ref, gate_ref, g_ref, o_ref):
        yv = y_ref[...]
        r = lax.rsqrt(jnp.mean(yv * yv, axis=-1, keepdims=True) + EPS)
        o_ref[...] = x_ref[...] + gate_ref[...] * ((yv * r) * g_ref[...])

    tile = pl.BlockSpec((tr, D), lambda i: (i, 0))
    return pl.pallas_call(
        body, name=name, grid=(S // tr,), in_specs=[tile, tile, _rowvec(), _rowvec()],
        out_specs=tile, out_shape=jax.ShapeDtypeStruct((S, D), F32), compiler_params=_cp(("parallel",)),
    )(x, y, gate, g)


def _postnorm_bwd(dxn, y, gate, g, name):
    S = y.shape[0]
    tr = _row_tile(S, 256)

    def body(dxn_ref, y_ref, gate_ref, g_ref, dy_ref, dgate_ref, dg_ref):
        i = pl.program_id(0)

        @pl.when(i == 0)
        def _():
            dgate_ref[...] = jnp.zeros((1, D), F32)
            dg_ref[...] = jnp.zeros((1, D), F32)

        yv = y_ref[...]
        dv = dxn_ref[...]
        gv = g_ref[...]
        gt = gate_ref[...]
        r = lax.rsqrt(jnp.mean(yv * yv, axis=-1, keepdims=True) + EPS)
        yh = yv * r
        dgate_ref[...] += jnp.sum(dv * (yh * gv), axis=0, keepdims=True)
        dg_ref[...] += jnp.sum(dv * gt * yh, axis=0, keepdims=True)
        u = dv * gt * gv
        dy_ref[...] = (r * u - yv * (r * r * r) * jnp.mean(u * yv, axis=-1, keepdims=True)).astype(BF16)

    tile = pl.BlockSpec((tr, D), lambda i: (i, 0))
    return pl.pallas_call(
        body, name=name, grid=(S // tr,), in_specs=[tile, tile, _rowvec(), _rowvec()],
        out_specs=[tile, _rowvec(), _rowvec()],
        out_shape=[jax.ShapeDtypeStruct((S, D), BF16), jax.ShapeDtypeStruct((1, D), F32),
                   jax.ShapeDtypeStruct((1, D), F32)],
        compiler_params=_cp(("arbitrary",)),
    )(dxn, y, gate, g)


def _loss_head(xo, target, name):
    S = xo.shape[0]
    tr = _row_tile(S, 256)

    def body(x_ref, t_ref, dx_ref, l_ref):
        i = pl.program_id(0)

        @pl.when(i == 0)
        def _():
            l_ref[...] = jnp.zeros((8, 128), F32)

        err = x_ref[...] - t_ref[...]
        dx_ref[...] = err * (1.0 / D)
        l_ref[...] += 0.5 * jnp.sum(jnp.mean(err * err, axis=-1, keepdims=True))

    tile = pl.BlockSpec((tr, D), lambda i: (i, 0))
    return pl.pallas_call(
        body, name=name, grid=(S // tr,), in_specs=[tile, tile],
        out_specs=[tile, pl.BlockSpec((8, 128), lambda i: (0, 0))],
        out_shape=[jax.ShapeDtypeStruct((S, D), F32), jax.ShapeDtypeStruct((8, 128), F32)],
        compiler_params=_cp(("arbitrary",)),
    )(xo, target)


def _layer_tail_fwd(proj, a_in, b_in, x, w_po, w_ho, w_out, gate, g, name, target=None, carry=None):
    S = proj.shape[0]
    tr = _row_tile(S, 256)
    nsh, _, wsh = w_po.shape
    n_in = 10 + (target is not None)

    def body(*refs):
        (mgp_ref, mgh_ref, a_ref, b_ref, x_ref, wpo_ref, who_ref, wout_ref, gate_ref, g_ref) = refs[:10]
        bra_ref, brb_ref, mt_ref, y_ref, xn_ref = refs[n_in:n_in + 5]
        av = a_ref[...]
        bra = jnp.concatenate([jnp.dot(av, wpo_ref[j], preferred_element_type=F32) for j in range(nsh)], axis=1)
        brb = jnp.dot(b_ref[...], who_ref[...], preferred_element_type=F32)
        mv = _sig(mgp_ref[...].astype(F32)) * bra + _sig(mgh_ref[...].astype(F32)) * brb
        bra_ref[...] = bra.astype(BF16)
        brb_ref[...] = brb.astype(BF16)
        mt_ref[...] = mv.T.astype(BF16)
        yv = jnp.dot(mv.astype(BF16), wout_ref[...], preferred_element_type=F32)
        y_ref[...] = yv
        r = lax.rsqrt(jnp.mean(yv * yv, axis=-1, keepdims=True) + EPS)
        xn = x_ref[...] + gate_ref[...] * ((yv * r) * g_ref[...])
        if target is None:
            xn_ref[...] = xn
        else:
            t_ref, l_ref = refs[10], refs[n_in + 5]

            @pl.when(pl.program_id(0) == 0)
            def _():
                l_ref[...] = jnp.zeros((8, 128), F32)

            err = xn - t_ref[...]
            xn_ref[...] = err * (1.0 / D)
            l_ref[...] += 0.5 * jnp.sum(jnp.mean(err * err, axis=-1, keepdims=True))

    tile = pl.BlockSpec((tr, D), lambda i: (i, 0))
    whole = lambda t: pl.BlockSpec(t.shape, lambda i: (0,) * t.ndim)
    last = target is not None
    outs = _call(
        body, name=name, grid=(S // tr,),
        in_specs=[pl.BlockSpec((tr, D), lambda i: (i, MGP_BLK)), pl.BlockSpec((tr, D), lambda i: (i, MGH_BLK)),
                  pl.BlockSpec((tr, POOL_W), lambda i: (i, 0)), tile, tile, whole(w_po), whole(w_ho),
                  whole(w_out), _rowvec(), _rowvec()] + [tile] * last,
        out_specs=[tile, tile, pl.BlockSpec((D, tr), lambda i: (0, i)), tile, tile]
        + [pl.BlockSpec((8, 128), lambda i: (0, 0))] * last,
        out_shape=[jax.ShapeDtypeStruct((S, D), BF16), jax.ShapeDtypeStruct((S, D), BF16),
                   jax.ShapeDtypeStruct((D, S), BF16), jax.ShapeDtypeStruct((S, D), F32),
                   jax.ShapeDtypeStruct((S, D), F32)] + [jax.ShapeDtypeStruct((8, 128), F32)] * last,
        sem=("arbitrary",) if last else ("parallel",),
        args=(proj, proj, a_in, b_in, x, w_po, w_ho, w_out, gate, g) + ((target,) if last else ()), carry=carry)
    return outs[:5 + last], outs[5 + last:]


def _layer_head_bwd(dxn, y, proj, br_a, br_b, w_po, w_ho, w_out, gate, g, name):
    S = y.shape[0]
    tr = _row_tile(S, 256)
    nsh, _, wsh = w_po.shape

    def body(dxn_ref, y_ref, mgp_ref, mgh_ref, bra_ref, brb_ref, wpo_ref, who_ref, wout_ref, gate_ref, g_ref,
             dy_ref, dba_ref, dbb_ref, dmg_ref, dain_ref, dbin_ref, dgate_ref, dg_ref):
        i = pl.program_id(0)

        @pl.when(i == 0)
        def _():
            dgate_ref[...] = jnp.zeros((1, D), F32)
            dg_ref[...] = jnp.zeros((1, D), F32)

        yv = y_ref[...]
        dv = dxn_ref[...]
        gv = g_ref[...]
        gt = gate_ref[...]
        r = lax.rsqrt(jnp.mean(yv * yv, axis=-1, keepdims=True) + EPS)
        yh = yv * r
        dgate_ref[...] += jnp.sum(dv * (yh * gv), axis=0, keepdims=True)
        dg_ref[...] += jnp.sum(dv * gt * yh, axis=0, keepdims=True)
        u = dv * gt * gv
        dy = (r * u - yv * (r * r * r) * jnp.mean(u * yv, axis=-1, keepdims=True)).astype(BF16)
        dy_ref[...] = dy
        dm = _dot_nt(dy, wout_ref[...])
        sp = _sig(mgp_ref[...].astype(F32))
        sh = _sig(mgh_ref[...].astype(F32))
        dba = (dm * sp).astype(BF16)
        dbb = (dm * sh).astype(BF16)
        dba_ref[...] = dba
        dbb_ref[...] = dbb
        dmg_ref[:, 0:D] = (dm * bra_ref[...].astype(F32) * sp * (1.0 - sp)).astype(BF16)
        dmg_ref[:, D:2 * D] = (dm * brb_ref[...].astype(F32) * sh * (1.0 - sh)).astype(BF16)
        dain = _dot_nt(dba[:, 0:wsh], wpo_ref[0])
        for j in range(1, nsh):
            dain = dain + _dot_nt(dba[:, j * wsh:(j + 1) * wsh], wpo_ref[j])
        dain_ref[...] = dain
        dbin_ref[...] = _dot_nt(dbb, who_ref[...])

    tile = pl.BlockSpec((tr, D), lambda i: (i, 0))
    whole = lambda t: pl.BlockSpec(t.shape, lambda i: (0,) * t.ndim)
    return pl.pallas_call(
        body, name=name, grid=(S // tr,),
        in_specs=[tile, tile, pl.BlockSpec((tr, D), lambda i: (i, MGP_BLK)),
                  pl.BlockSpec((tr, D), lambda i: (i, MGH_BLK)), tile, tile, whole(w_po), whole(w_ho),
                  whole(w_out), _rowvec(), _rowvec()],
        out_specs=[tile, tile, tile, pl.BlockSpec((tr, 2 * D), lambda i: (i, 0)),
                   pl.BlockSpec((tr, POOL_W), lambda i: (i, 0)), tile, _rowvec(), _rowvec()],
        out_shape=[jax.ShapeDtypeStruct((S, D), BF16)] * 3
        + [jax.ShapeDtypeStruct((S, 2 * D), BF16), jax.ShapeDtypeStruct((S, POOL_W), F32),
           jax.ShapeDtypeStruct((S, D), F32), jax.ShapeDtypeStruct((1, D), F32), jax.ShapeDtypeStruct((1, D), F32)],
        compiler_params=_cp(("arbitrary",)),
    )(dxn, y, proj, proj, br_a, br_b, w_po, w_ho, w_out, gate, g)


def _merge_fwd(proj, br_a, br_b, name):
    S = proj.shape[0]
    tr = _row_tile(S, 256)

    def body(mgp_ref, mgh_ref, a_ref, b_ref, o_ref, ot_ref):
        mv = _sig(mgp_ref[...]) * a_ref[...] + _sig(mgh_ref[...]) * b_ref[...]
        o_ref[...] = mv.astype(BF16)
        ot_ref[...] = mv.T.astype(BF16)

    tile = pl.BlockSpec((tr, D), lambda i: (i, 0))
    return pl.pallas_call(
        body, name=name, grid=(S // tr,),
        in_specs=[pl.BlockSpec((tr, D), lambda i: (i, MGP_BLK)), pl.BlockSpec((tr, D), lambda i: (i, MGH_BLK)),
                  tile, tile],
        out_specs=[tile, pl.BlockSpec((D, tr), lambda i: (0, i))],
        out_shape=[jax.ShapeDtypeStruct((S, D), BF16), jax.ShapeDtypeStruct((D, S), BF16)],
        compiler_params=_cp(("parallel",)),
    )(proj, proj, br_a, br_b)


def _merge_bwd(dm, proj, br_a, br_b, name):
    S = proj.shape[0]
    tr = _row_tile(S, 256)

    def body(dm_ref, mgp_ref, mgh_ref, a_ref, b_ref, da_ref, db_ref, dmg_ref):
        dmv = dm_ref[...]
        sp = _sig(mgp_ref[...])
        sh = _sig(mgh_ref[...])
        da_ref[...] = (dmv * sp).astype(BF16)
        db_ref[...] = (dmv * sh).astype(BF16)
        dmg_ref[:, 0:D] = (dmv * a_ref[...] * sp * (1.0 - sp)).astype(BF16)
        dmg_ref[:, D:2 * D] = (dmv * b_ref[...] * sh * (1.0 - sh)).astype(BF16)

    tile = pl.BlockSpec((tr, D), lambda i: (i, 0))
    return pl.pallas_call(
        body, name=name, grid=(S // tr,),
        in_specs=[tile, pl.BlockSpec((tr, D), lambda i: (i, MGP_BLK)),
                  pl.BlockSpec((tr, D), lambda i: (i, MGH_BLK)), tile, tile],
        out_specs=[tile, tile, pl.BlockSpec((tr, 2 * D), lambda i: (i, 0))],
        out_shape=[jax.ShapeDtypeStruct((S, D), BF16), jax.ShapeDtypeStruct((S, D), BF16),
                   jax.ShapeDtypeStruct((S, 2 * D), BF16)],
        compiler_params=_cp(("parallel",)),
    )(dm, proj, proj, br_a, br_b)


def _pool_pieces(u, g, S):
    rowi = lax.broadcasted_iota(jnp.int32, (S, 1), 0)

    def down(z, k):
        return jnp.where(rowi >= k, pltpu.roll(z, k, axis=0), 0.0)

    s2 = u + down(u, 1)
    s4 = s2 + down(s2, 2)
    s8 = s4 + down(s4, 4)
    s16 = s8 + down(s8, 8)
    win = jnp.where(g == 0, s2, jnp.where(g == 1, s4, jnp.where(g == 2, s8, s16)))
    w = jnp.where(g == 0, 2, jnp.where(g == 1, 4, jnp.where(g == 2, 8, 16)))
    count = jnp.minimum(rowi + 1, w).astype(F32)
    return win / count - u, count, rowi


def _pool_fwd(proj, pw, pscale, name):
    S = proj.shape[0]

    def body(pv_ref, pg_ref, pw_ref, sc_ref, a_ref, at_ref):
        g = pl.program_id(0)
        pooled, _, _ = _pool_pieces(pv_ref[...].astype(F32), g, S)
        pm = jnp.dot(pooled.astype(BF16), pw_ref[...].astype(BF16), preferred_element_type=F32)
        pgv = pg_ref[...].astype(F32)
        av = pm * sc_ref[...] * (pgv * _sig(pgv))
        a_ref[...] = av.astype(BF16)
        at_ref[...] = av.T.astype(BF16)

    return pl.pallas_call(
        body, name=name, grid=(GROUPS,),
        in_specs=[pl.BlockSpec((S, 128), lambda g: (0, PV0 + g)), pl.BlockSpec((S, 128), lambda g: (0, PG0 + g)),
                  pl.BlockSpec((None, 128, 128), lambda g: (g, 0, 0)), pl.BlockSpec((1, 128), lambda g: (0, g))],
        out_specs=[pl.BlockSpec((S, 128), lambda g: (0, g)), pl.BlockSpec((128, S), lambda g: (g, 0))],
        out_shape=[jax.ShapeDtypeStruct((S, POOL_W), BF16), jax.ShapeDtypeStruct((POOL_W, S), BF16)],
        compiler_params=_cp(("parallel",)),
    )(proj, proj, pw, pscale)


def _pool_bwd(da, proj, pw, pscale, name):
    S = proj.shape[0]

    def body(da_ref, pv_ref, pg_ref, pw_ref, sc_ref, dpv_ref, dpg_ref, dpw_ref, dsc_ref):
        g = pl.program_id(0)
        pooled, count, rowi = _pool_pieces(pv_ref[...].astype(F32), g, S)
        pwb = pw_ref[...].astype(BF16)
        pm = jnp.dot(pooled.astype(BF16), pwb, preferred_element_type=F32)
        scv = sc_ref[...]
        pgv = pg_ref[...].astype(F32)
        sg = _sig(pgv)
        dav = da_ref[...]
        d_ps = dav * (pgv * sg)
        dpg_ref[...] = (dav * (pm * scv) * _dsilu(pgv, sg)).astype(BF16)
        dsc_ref[...] = jnp.sum(d_ps * pm, axis=0, keepdims=True)
        d_pm = (d_ps * scv).astype(BF16)
        dpw_ref[...] = lax.dot_general(pooled.astype(BF16), d_pm, (((0,), (0,)), ((), ())),
                                       preferred_element_type=F32)
        d_pooled = lax.dot_general(d_pm, pwb, (((1,), (1,)), ((), ())), preferred_element_type=F32)
        z = d_pooled / count

        def up(v, k):
            return jnp.where(rowi < S - k, pltpu.roll(v, S - k, axis=0), 0.0)

        t2 = z + up(z, 1)
        t4 = t2 + up(t2, 2)
        t8 = t4 + up(t4, 4)
        t16 = t8 + up(t8, 8)
        adj = jnp.where(g == 0, t2, jnp.where(g == 1, t4, jnp.where(g == 2, t8, t16)))
        dpv_ref[...] = (adj - d_pooled).astype(BF16)

    col = lambda g: (0, g)
    return pl.pallas_call(
        body, name=name, grid=(GROUPS,),
        in_specs=[pl.BlockSpec((S, 128), col), pl.BlockSpec((S, 128), lambda g: (0, PV0 + g)),
                  pl.BlockSpec((S, 128), lambda g: (0, PG0 + g)),
                  pl.BlockSpec((None, 128, 128), lambda g: (g, 0, 0)), pl.BlockSpec((1, 128), col)],
        out_specs=[pl.BlockSpec((S, 128), col), pl.BlockSpec((S, 128), col),
                   pl.BlockSpec((None, 128, 128), lambda g: (g, 0, 0)), pl.BlockSpec((1, 128), col)],
        out_shape=[jax.ShapeDtypeStruct((S, POOL_W), BF16), jax.ShapeDtypeStruct((S, POOL_W), BF16),
                   jax.ShapeDtypeStruct((GROUPS, 128, 128), F32), jax.ShapeDtypeStruct((1, POOL_W), F32)],
        compiler_params=_cp(("parallel",)),
    )(da, proj, proj, pw, pscale)


def _chunk_cumsum(z, rowi):
    for sh in (1, 2, 4, 8, 16, 32):
        z = z + jnp.where(rowi >= sh, pltpu.roll(z, sh, axis=0), 0.0)
    return z


def _chunk_rev_cumsum(z, rowi):
    for sh in (1, 2, 4, 8, 16, 32):
        z = z + jnp.where(rowi < CH - sh, pltpu.roll(z, CH - sh, axis=0), 0.0)
    return z


def _dot_nn(a, b):
    return jnp.dot(a.astype(BF16), b.astype(BF16), preferred_element_type=F32)


def _dot_nt(a, b):
    return lax.dot_general(a.astype(BF16), b.astype(BF16), (((1,), (1,)), ((), ())), preferred_element_type=F32)


def _dot_tn(a, b):
    return lax.dot_general(a.astype(BF16), b.astype(BF16), (((0,), (0,)), ((), ())), preferred_element_type=F32)


def _gates(hq, hf, lbv):
    hq, hf = hq.astype(F32), hf.astype(F32)
    sq = _sig(hq)
    sf = _sig(hf)
    f = lbv + (1.0 - lbv) * sf
    fc = jnp.maximum(f, 1e-30)
    return hq * sq, sq, sf, f, fc, jnp.log(fc)


DECAY_CAP = 60.0


def _block_ref(c_ref, i):
    if i == 0:
        return jnp.zeros((1, HD), F32)
    return c_ref[SB * i - 1:SB * i, :]


def _block_decay(c_ref):
    spans = [_block_ref(c_ref, i) - c_ref[SB * (i + 1) - 1:SB * (i + 1), :] for i in range(CH // SB)]
    return functools.reduce(jnp.maximum, spans)


def _hgrn_fwd(proj, lb, gn, name, carry=None):
    S = proj.shape[0]
    nch = S // CH
    W = NH * HD

    def body(hq_ref, hf_ref, hi_ref, hg_ref, lb_ref, gn_ref, bin_ref, bint_ref, oraw_ref, st_ref, mild_ref,
             q_s, k_s, c_s, v_s, o_s, state_s, qf_s, kf_s, cf_s):
        state_s[...] = jnp.zeros((NH, HD, HD), F32)
        rowi = lax.broadcasted_iota(jnp.int32, (CH, 1), 0)
        coli = lax.broadcasted_iota(jnp.int32, (1, CH), 1)
        sbi = lax.broadcasted_iota(jnp.int32, (SB, 1), 0)
        gnv = gn_ref[...]

        def gates_pass(n, worst):
            rows = pl.ds(pl.multiple_of(n * CH, CH), CH)
            for hh in range(NH):
                lanes = slice(hh * HD, (hh + 1) * HD)
                q, _, _, f, _, logf = _gates(hq_ref[rows, lanes], hf_ref[rows, lanes], lb_ref[:, lanes])
                c = _chunk_cumsum(logf, rowi)
                qf_s[hh, rows, :] = q
                kf_s[hh, rows, :] = 1.0 - f
                cf_s[hh, rows, :] = c
                c_s[hh] = c
                worst = jnp.maximum(worst, _block_decay(c_s.at[hh]))
            return worst

        def between_chunks(hh, n, rows):
            lanes = slice(hh * HD, (hh + 1) * HD)
            q = qf_s[hh, rows, :]
            k = kf_s[hh, rows, :]
            c = cf_s[hh, rows, :]
            v = hi_ref[rows, lanes].astype(F32)
            q_s[hh] = q
            k_s[hh] = k
            c_s[hh] = c
            v_s[hh] = v
            st = state_s[hh]
            st_ref[hh, n] = st.astype(BF16)
            o_s[hh] = _dot_nt(q * jnp.exp(c), st)
            last = c_s[hh, CH - 1:CH, :]
            state_s[hh] = st * jnp.exp(last) + _dot_tn(v, k * jnp.exp(last - c))

        def within_chunk_matmul(hh):
            q, k, c, v = q_s[hh], k_s[hh], c_s[hh], v_s[hh]
            a = jnp.zeros((CH, CH), F32)
            for i in range(CH // SB):
                r_i = _block_ref(c_s.at[hh], i)
                qi = q * jnp.exp(jnp.minimum(c - r_i, 0.0))
                kei = k * jnp.exp(jnp.minimum(r_i - c, DECAY_CAP))
                m_i = (rowi >= SB * i) & (rowi < SB * (i + 1)) & (coli <= rowi)
                a = a + jnp.where(m_i, _dot_nt(qi, kei), 0.0)
            o_s[hh] += _dot_nn(a, v)

        def within_chunk_exact(hh):
            q, k, c, v = q_s[hh], k_s[hh], c_s[hh], v_s[hh]
            a_off = jnp.zeros((CH, CH), F32)
            for i in range(1, CH // SB):
                r_i = _block_ref(c_s.at[hh], i)
                qi = q * jnp.exp(jnp.minimum(c - r_i, 0.0))
                kei = k * jnp.exp(jnp.minimum(r_i - c, 0.0))
                m_i = (rowi >= SB * i) & (rowi < SB * (i + 1)) & (coli < SB * i)
                a_off = a_off + jnp.where(m_i, _dot_nt(qi, kei), 0.0)
            o_s[hh] += _dot_nn(a_off, v)
            for i in range(CH // SB):
                blk = slice(SB * i, SB * (i + 1))
                qb = q_s[hh, blk, :]
                cb = c_s[hh, blk, :]
                acc = jnp.zeros((SB, HD), F32)
                for s in range(SB):
                    row = SB * i + s
                    w = jnp.exp(jnp.minimum(cb - c_s[hh, row:row + 1, :], 0.0))
                    a_col = jnp.sum(qb * k_s[hh, row:row + 1, :] * w, axis=-1, keepdims=True)
                    acc = acc + jnp.where(sbi >= s, a_col, 0.0) * v_s[hh, row:row + 1, :]
                o_s[hh, blk, :] += acc

        def norm_and_gate(hh, rows):
            lanes = slice(hh * HD, (hh + 1) * HD)
            ov = o_s[hh]
            oraw_ref[rows, lanes] = ov
            r = lax.rsqrt(jnp.mean(ov * ov, axis=-1, keepdims=True) + EPS)
            hg = hg_ref[rows, lanes].astype(F32)
            bin_ref[rows, lanes] = ((ov * r) * gnv * (hg * _sig(hg))).astype(BF16)

        def chunk_with(within_chunk):
            def chunk(n, carry):
                rows = pl.ds(pl.multiple_of(n * CH, CH), CH)
                for hh in range(NH):
                    between_chunks(hh, n, rows)
                for hh in range(NH):
                    within_chunk(hh)
                for hh in range(NH):
                    norm_and_gate(hh, rows)
                return carry
            return chunk

        worst = lax.fori_loop(0, nch, gates_pass, jnp.zeros((1, HD), F32))
        mild = jnp.max(worst) <= DECAY_CAP
        mild_ref[...] = jnp.broadcast_to(jnp.where(mild, 1.0, 0.0), (8, HD))

        @pl.when(mild)
        def _():
            lax.fori_loop(0, nch, chunk_with(within_chunk_matmul), 0, unroll=4)

        @pl.when(jnp.logical_not(mild))
        def _():
            lax.fori_loop(0, nch, chunk_with(within_chunk_exact), 0)

        bint_ref[...] = bin_ref[...].astype(F32).T.astype(BF16)

    col = lambda off: pl.BlockSpec((S, W), lambda h: (0, off // NH + h))
    head = pl.BlockSpec((S, W), lambda h: (0, h))
    outs = _call(
        body, name=name, grid=(HEADS // NH,),
        in_specs=[col(HQ0), col(HF0), col(HI0), col(HG0), pl.BlockSpec((1, W), lambda h: (0, h)),
                  pl.BlockSpec((1, HD), lambda h: (0, 0))],
        out_specs=[head, pl.BlockSpec((W, S), lambda h: (h, 0)), head,
                   pl.BlockSpec((NH, nch, HD, HD), lambda h: (h, 0, 0, 0)),
                   pl.BlockSpec((8, HD), lambda h: (h, 0))],
        out_shape=[jax.ShapeDtypeStruct((S, D), BF16), jax.ShapeDtypeStruct((D, S), BF16),
                   jax.ShapeDtypeStruct((S, D), F32), jax.ShapeDtypeStruct((HEADS, nch, HD, HD), BF16),
                   jax.ShapeDtypeStruct((8 * HEADS // NH, HD), F32)],
        scratch_shapes=[pltpu.VMEM((NH, CH, HD), F32)] * 5 + [pltpu.VMEM((NH, HD, HD), F32)]
        + [pltpu.VMEM((NH, S, HD), F32)] * 3,
        sem=("parallel",), args=(proj, proj, proj, proj, lb, gn), carry=carry)
    return outs[:5], outs[5:]


def _hgrn_bwd(dbin, proj, oraw, states, mild, lb, gn, name, carry=None):
    S = proj.shape[0]
    nch = S // CH
    W = NH * HD

    def body(db_ref, hq_ref, hf_ref, hi_ref, hg_ref, or_ref, st_ref, mild_ref, lb_ref, gn_ref,
             dq_ref, df_ref, di_ref, dg_ref, dlb_ref, dgn_ref,
             q_s, k_s, c_s, v_s, do_s, dq_s, dk_s, dv_s, dc_s, dqd_s, dkd_s, dl_s, dst_s, dlb_s, dgn_s):
        dst_s[...] = jnp.zeros((NH, HD, HD), F32)
        dlb_s[...] = jnp.zeros((1, W), F32)
        dgn_s[...] = jnp.zeros((1, HD), F32)
        rowi = lax.broadcasted_iota(jnp.int32, (CH, 1), 0)
        rowi2 = lax.broadcasted_iota(jnp.int32, (CH, CH), 0)
        coli2 = lax.broadcasted_iota(jnp.int32, (CH, CH), 1)
        sbi = lax.broadcasted_iota(jnp.int32, (SB, 1), 0)
        gnv = gn_ref[...]
        def between_chunks(hh, n, rows):
            lanes = slice(hh * HD, (hh + 1) * HD)
            q, _, _, f, _, logf = _gates(hq_ref[rows, lanes], hf_ref[rows, lanes], lb_ref[:, lanes])
            k = 1.0 - f
            v = hi_ref[rows, lanes].astype(F32)
            c = _chunk_cumsum(logf, rowi)
            ov = or_ref[rows, lanes]
            hg = hg_ref[rows, lanes].astype(F32)
            sg = _sig(hg)
            r = lax.rsqrt(jnp.mean(ov * ov, axis=-1, keepdims=True) + EPS)
            dbv = db_ref[rows, lanes]
            d_on = dbv * (hg * sg)
            dg_ref[rows, lanes] = (dbv * ((ov * r) * gnv) * _dsilu(hg, sg)).astype(BF16)
            dgn_s[...] += jnp.sum(d_on * (ov * r), axis=0, keepdims=True)
            u = d_on * gnv
            do = r * u - ov * (r * r * r) * jnp.mean(u * ov, axis=-1, keepdims=True)
            q_s[hh] = q
            k_s[hh] = k
            c_s[hh] = c
            v_s[hh] = v
            do_s[hh] = do
            st = st_ref[hh, n].astype(F32)
            dst = dst_s[hh]
            ec = jnp.exp(c)
            last = c_s[hh, CH - 1:CH, :]
            el = jnp.exp(last - c)
            elast = jnp.exp(last)
            dq = _dot_nn(do, st) * ec
            dk = _dot_nn(v, dst) * el
            dq_s[hh] = dq
            dk_s[hh] = dk
            dv_s[hh] = _dot_nt(k * el, dst)
            dc_s[hh] = q * dq - k * dk
            dl_s[hh] = (jnp.sum(k * dk, axis=0, keepdims=True)
                        + elast * jnp.sum(st * dst, axis=0, keepdims=True))
            dst_s[hh] = dst * elast + _dot_tn(do, q * ec)

        def pairs_matmul(hh, first, cap, strict):
            q, k, c, v, do = q_s[hh], k_s[hh], c_s[hh], v_s[hh], do_s[hh]
            d_a = _dot_nt(do, v).astype(BF16).astype(F32)
            d_at = d_a.T
            at = jnp.zeros((CH, CH), F32)
            dq, dk, dcum = dq_s[hh], dk_s[hh], dc_s[hh]
            for i in range(first, CH // SB):
                r_i = _block_ref(c_s.at[hh], i)
                eq = jnp.exp(jnp.minimum(c - r_i, 0.0))
                ek = jnp.exp(jnp.minimum(r_i - c, cap))
                qi = (q * eq).astype(BF16).astype(F32)
                kei = (k * ek).astype(BF16).astype(F32)
                in_t = (rowi2 >= SB * i) & (rowi2 < SB * (i + 1))
                in_s = (coli2 >= SB * i) & (coli2 < SB * (i + 1))
                m_ts = in_t & ((coli2 < SB * i) if strict else (coli2 <= rowi2))
                m_st = in_s & ((rowi2 < SB * i) if strict else (rowi2 <= coli2))
                at = at + jnp.where(m_st, _dot_nt(kei, qi), 0.0)
                dq_i = _dot_nn(jnp.where(m_ts, d_a, 0.0), kei)
                dk_i = _dot_nn(jnp.where(m_st, d_at, 0.0), qi)
                dq = dq + dq_i * eq
                dk = dk + dk_i * ek
                dcum = dcum + (qi * dq_i - kei * dk_i)
            dq_s[hh] = dq
            dk_s[hh] = dk
            dc_s[hh] = dcum
            dv_s[hh] += _dot_nn(at, do)

        def pairs_exact(hh):
            dqd_s[hh] = jnp.zeros((CH, HD), F32)
            dkd_s[hh] = jnp.zeros((CH, HD), F32)
            for i in range(CH // SB):
                blk = slice(SB * i, SB * (i + 1))
                qb = q_s[hh, blk, :]
                cb = c_s[hh, blk, :]
                dob = do_s[hh, blk, :]
                dq_acc = jnp.zeros((SB, HD), F32)
                for s in range(SB):
                    row = SB * i + s
                    ks = k_s[hh, row:row + 1, :]
                    vs = v_s[hh, row:row + 1, :]
                    w = jnp.exp(jnp.minimum(cb - c_s[hh, row:row + 1, :], 0.0))
                    live = sbi >= s
                    a_col = jnp.where(live, jnp.sum(qb * ks * w, axis=-1, keepdims=True), 0.0)
                    da_col = jnp.where(live, jnp.sum(dob * vs, axis=-1, keepdims=True), 0.0)
                    dq_acc = dq_acc + da_col * ks * w
                    dkd_s[hh, row:row + 1, :] += jnp.sum(da_col * qb * w, axis=0, keepdims=True)
                    dv_s[hh, row:row + 1, :] += jnp.sum(a_col * dob, axis=0, keepdims=True)
                dqd_s[hh, blk, :] += dq_acc
            dq_d = dqd_s[hh]
            dk_d = dkd_s[hh]
            dq_s[hh] += dq_d
            dk_s[hh] += dk_d
            dc_s[hh] += q_s[hh] * dq_d - k_s[hh] * dk_d

        def gate_grads(hh, rows):
            lanes = slice(hh * HD, (hh + 1) * HD)
            lbv = lb_ref[:, lanes]
            hq = hq_ref[rows, lanes].astype(F32)
            _, sq, sf, f, fc, _ = _gates(hq, hf_ref[rows, lanes], lbv)
            dlogf = _chunk_rev_cumsum(dc_s[hh], rowi) + dl_s[hh]
            dfv = jnp.where(f > 1e-30, dlogf / fc, 0.0) - dk_s[hh]
            dlb_s[:, lanes] += jnp.sum(dfv * (1.0 - sf), axis=0, keepdims=True)
            df_ref[rows, lanes] = (dfv * (1.0 - lbv) * sf * (1.0 - sf)).astype(BF16)
            dq_ref[rows, lanes] = (dq_s[hh] * _dsilu(hq, sq)).astype(BF16)
            di_ref[rows, lanes] = dv_s[hh].astype(BF16)

        def chunk_with(pairs):
            def chunk(j, carry):
                n = nch - 1 - j
                rows = pl.ds(pl.multiple_of(n * CH, CH), CH)
                for hh in range(NH):
                    between_chunks(hh, n, rows)
                for hh in range(NH):
                    pairs(hh)
                for hh in range(NH):
                    gate_grads(hh, rows)
                return carry
            return chunk

        def pairs_mild(hh):
            pairs_matmul(hh, 0, DECAY_CAP, strict=False)

        def pairs_any(hh):
            pairs_matmul(hh, 1, 0.0, strict=True)
            pairs_exact(hh)

        mild = jnp.max(mild_ref[...]) > 0.5

        @pl.when(mild)
        def _():
            lax.fori_loop(0, nch, chunk_with(pairs_mild), 0, unroll=2)

        @pl.when(jnp.logical_not(mild))
        def _():
            lax.fori_loop(0, nch, chunk_with(pairs_any), 0)

        dlb_ref[...] = dlb_s[...]
        dgn_ref[...] = jnp.broadcast_to(dgn_s[...], (8, HD))

    col = lambda off: pl.BlockSpec((S, W), lambda h: (0, off // NH + h))
    head = pl.BlockSpec((S, W), lambda h: (0, h))
    vec = pl.BlockSpec((1, W), lambda h: (0, h))
    outs = _call(
        body, name=name, grid=(HEADS // NH,),
        in_specs=[head, col(HQ0), col(HF0), col(HI0), col(HG0), head,
                  pl.BlockSpec((NH, nch, HD, HD), lambda h: (h, 0, 0, 0)),
                  pl.BlockSpec((8, HD), lambda h: (h, 0)), vec, pl.BlockSpec((1, HD), lambda h: (0, 0))],
        out_specs=[head, head, head, head, vec, pl.BlockSpec((8, HD), lambda h: (h, 0))],
        out_shape=[jax.ShapeDtypeStruct((S, D), BF16)] * 4
        + [jax.ShapeDtypeStruct((1, D), F32), jax.ShapeDtypeStruct((8 * HEADS // NH, HD), F32)],
        scratch_shapes=[pltpu.VMEM((NH, CH, HD), F32)] * 11
        + [pltpu.VMEM((NH, 1, HD), F32), pltpu.VMEM((NH, HD, HD), F32), pltpu.VMEM((1, W), F32),
           pltpu.VMEM((1, HD), F32)],
        sem=("parallel",), args=(dbin, proj, proj, proj, proj, oraw, states, mild, lb, gn), carry=carry)
    dq, df, di, dg, dlb, dgn = outs[:6]
    return (dq, df, di, dg, dlb, dgn.reshape(HEADS // NH, 8, HD)[:, 0, :]), outs[6:]


def _lower_bounds(l0, l1):
    m = jnp.maximum(l0, l1)
    e0 = jnp.exp(l0 - m)
    e1 = jnp.exp(l1 - m)
    tot = e0 + e1
    p0 = e0 / tot
    p1 = e1 / tot
    return jnp.clip(p0 - p0, 0.0, 1.0), jnp.clip((p0 + p1) - p0, 0.0, 1.0)


def _lb_fwd(logits):
    def body(l_ref, o_ref):
        lb0, lb1 = _lower_bounds(l_ref[0:1, :], l_ref[1:2, :])
        o_ref[0:1, :] = lb0
        o_ref[1:2, :] = lb1

    return pl.pallas_call(body, name="lb_fwd", out_shape=jax.ShapeDtypeStruct((2, D), F32))(logits)


def _lb_bwd(logits, dlb):
    def body(l_ref, d_ref, o_ref):
        _, vjp = jax.vjp(_lower_bounds, l_ref[0:1, :], l_ref[1:2, :])
        g0, g1 = vjp((d_ref[0:1, :], d_ref[1:2, :]))
        o_ref[0:1, :] = g0
        o_ref[1:2, :] = g1

    return pl.pallas_call(body, name="lb_bwd", out_shape=jax.ShapeDtypeStruct((2, D), F32))(logits, dlb)


ADA_PAD = 128


def _ada_fwd(c_pad, w_ada, b_sh):
    ns = w_ada.shape[2]

    def body(c_ref, w_ref, b_ref, o_ref):
        cv = c_ref[...]
        ca = (cv * _sig(cv)).astype(BF16)
        for l in range(2):
            res = jnp.dot(ca, w_ref[l].astype(BF16), preferred_element_type=F32)
            o_ref[:, l * ns:(l + 1) * ns] = res[0:NDEV, :] + b_ref[l:l + 1, :]

    return pl.pallas_call(body, name="ada_fwd", out_shape=jax.ShapeDtypeStruct((NDEV, 2 * ns), F32),
                          compiler_params=_cp())(c_pad, w_ada, b_sh)


def _ada_wgrad(c_pad_t, d_ada_sh):
    ns = d_ada_sh.shape[2]

    def body(c_ref, d_ref, o_ref):
        cv = c_ref[...]
        ca = (cv * _sig(cv)).astype(BF16)
        for l in range(2):
            o_ref[l] = jnp.dot(ca, d_ref[l].astype(BF16), preferred_element_type=F32)

    return pl.pallas_call(body, name="ada_wgrad", out_shape=jax.ShapeDtypeStruct((2, D, ns), F32),
                          compiler_params=_cp())(c_pad_t, d_ada_sh)


def _sum_devices(g):
    _, R, C = g.shape

    def body(g_ref, o_ref):
        acc = g_ref[0]
        for d in range(1, NDEV):
            acc = acc + g_ref[d]
        o_ref[...] = acc

    return pl.pallas_call(body, name="sum_devices", out_shape=jax.ShapeDtypeStruct((R, C), F32),
                          compiler_params=_cp())(g)


def _adamw(w, g, m, v, name, carry=None):
    R, C = w.shape
    tr = _row_tile(R, max(8, (1 << 19) // C))

    def body(w_ref, g_ref, m_ref, v_ref, d_ref, nm_ref, nv_ref):
        d_ref[...], nm_ref[...], nv_ref[...] = _adamw_update(w_ref[...], g_ref[...], m_ref[...], v_ref[...])

    tile = pl.BlockSpec((tr, C), lambda i: (i, 0))
    return _call(body, name=name, grid=(R // tr,), in_specs=[tile] * 4, out_specs=[tile] * 3,
                 out_shape=[jax.ShapeDtypeStruct((R, C), F32)] * 3, sem=("parallel",), args=(w, g, m, v),
                 carry=carry)


def _adamw_update(w, g, m, v):
    nm = B1 * m + (1.0 - B1) * g
    nv = B2 * v + (1.0 - B2) * (g * g)
    m_hat = nm / (1.0 - B1 ** STEP)
    v_hat = nv / (1.0 - B2 ** STEP)
    return -LR * (m_hat / (jnp.sqrt(v_hat) + AEPS) + WD * w), nm, nv


SMALL_PARTS = (("b_ada", 0, 6, D), ("g_pre", 8, 2, D), ("g_post", 16, 2, D), ("lb_logits", 24, 2, D),
               ("pool_w", 32, 128, D), ("pool_scale", 160, 1, D), ("hgrn_norm_g", 168, 1, 2 * HD))


def _adamw_small(g_small, g_lb_logits, wmv):
    n = len(SMALL_PARTS)

    def body(g_ref, glb_ref, *refs):
        ins, outs = refs[:3 * n], refs[3 * n:]
        for p, (key, row0, rows, width) in enumerate(SMALL_PARTS):
            gv = glb_ref[...] if key == "lb_logits" else g_ref[row0:row0 + rows, 0:width]
            res = _adamw_update(ins[3 * p][...], gv, ins[3 * p + 1][...], ins[3 * p + 2][...])
            for t in range(3):
                outs[3 * p + t][...] = res[t]

    flat = [t for triple in wmv for t in triple]
    outs = pl.pallas_call(body, name="adamw_small",
                          out_shape=[jax.ShapeDtypeStruct(t.shape, F32) for t in flat],
                          compiler_params=_cp())(g_small, g_lb_logits, *flat)
    return [outs[3 * p:3 * p + 3] for p in range(n)]


def _cast_to_slot(place, w, l, name):
    _, R, C = w.shape
    tr = _row_tile(R, max(8, (1 << 19) // C))

    def body(p_ref, w_ref, o_ref):
        o_ref[...] = w_ref[...].astype(BF16)

    return pl.pallas_call(
        body, name=name, out_shape=jax.ShapeDtypeStruct((NCHIP, R, C), BF16),
        grid_spec=pltpu.PrefetchScalarGridSpec(
            num_scalar_prefetch=1, grid=(R // tr,),
            in_specs=[pl.BlockSpec((None, tr, C), lambda i, p_ref: (l, i, 0))],
            out_specs=pl.BlockSpec((None, tr, C), lambda i, p_ref: (p_ref[0], i, 0))),
        compiler_params=_cp(("parallel",)),
    )(place, w)


def _pair_add(core, g, got, name):
    _, R, C = g.shape
    r2 = R // 2
    tr = _row_tile(r2, max(8, (1 << 19) // C))
    nt = r2 // tr

    def body(c_ref, a_ref, b_ref, o_ref):
        o_ref[...] = (a_ref[...].astype(F32) + b_ref[...].astype(F32)).astype(o_ref.dtype)

    return pl.pallas_call(
        body, name=name, out_shape=jax.ShapeDtypeStruct((NCHIP, r2, C), BF16),
        grid_spec=pltpu.PrefetchScalarGridSpec(
            num_scalar_prefetch=1, grid=(NCHIP, nt),
            in_specs=[pl.BlockSpec((None, tr, C), lambda j, i, c_ref: (j, c_ref[0] * nt + i, 0)),
                      pl.BlockSpec((None, tr, C), lambda j, i, c_ref: (j, i, 0))],
            out_specs=pl.BlockSpec((None, tr, C), lambda j, i, c_ref: (j, i, 0))),
        compiler_params=_cp(("parallel", "parallel")),
    )(core, g, got)


def _chip_sum(place, part, recv, layer, both, name):
    _, r2, C = part.shape
    tr = _row_tile(r2, max(8, (1 << 18) // C))
    nt = r2 // tr

    def body(p_ref, own_ref, r_ref, *rest):
        o_ref = rest[-1]
        me = p_ref[0]
        own = own_ref[...].astype(F32)
        acc = None
        for j in range(NCHIP):
            slot = jnp.minimum(jnp.where(j > me, j - 1, j), NCHIP - 2)
            term = jnp.where(me == j, own, r_ref[slot].astype(F32))
            acc = term if acc is None else acc + term
        o_ref[...] = acc

    args = (place, part, recv) if both is None else (place, part, recv, both)
    return pl.pallas_call(
        body, name=name, out_shape=jax.ShapeDtypeStruct((2, 2 * r2, C), F32),
        grid_spec=pltpu.PrefetchScalarGridSpec(
            num_scalar_prefetch=1, grid=(nt,),
            in_specs=[pl.BlockSpec((None, tr, C), lambda i, p_ref: (p_ref[0], i, 0)),
                      pl.BlockSpec((NCHIP - 1, tr, C), lambda i, p_ref: (0, i, 0))] + [ANY] * (len(args) - 3),
            out_specs=pl.BlockSpec((None, tr, C), lambda i, p_ref: (layer, p_ref[1] * nt + i, 0))),
        input_output_aliases={} if both is None else {3: 0},
        compiler_params=_cp(("parallel",)),
    )(*args)


def _place():
    x, y, c = lax.axis_index("x"), lax.axis_index("y"), lax.axis_index("c")
    chips = [(1 - x, y), (x, 1 - y), (1 - x, 1 - y)]
    return x, y, c, chips


def _gather_small(blk, name):
    m_per, n = blk.shape

    def body(x_ref, out_ref, send_sems, recv_sems, local_sem):
        x, y, c, chips = _place()
        me, sibling = (x, y, c), (x, y, 1 - c)

        def rows(px, py, pc):
            return out_ref.at[pl.ds((4 * px + 2 * py + pc) * m_per, m_per), :]

        def copy(k, block, to, src=None):
            return pltpu.make_async_remote_copy(
                src_ref=rows(*block) if src is None else src, dst_ref=rows(*block),
                send_sem=send_sems.at[k], recv_sem=recv_sems.at[k], device_id=to, device_id_type=MESH)

        mine = pltpu.make_async_copy(x_ref, rows(*me), local_sem)
        mine.start()
        first = [copy(0, me, sibling, src=x_ref)]
        first += [copy(1 + j, me, (*chip, c), src=x_ref) for j, chip in enumerate(chips)]
        for cp in first:
            cp.start()
        passed = [copy(4 + j, (*chip, c), sibling) for j, chip in enumerate(chips)]
        for j, chip in enumerate(chips):
            copy(1 + j, (*chip, c), me).wait_recv()
            passed[j].start()
        copy(0, sibling, me).wait_recv()
        for j, chip in enumerate(chips):
            copy(4 + j, (*chip, 1 - c), me).wait_recv()
        for cp in first + passed:
            cp.wait_send()
        mine.wait()

    return pl.pallas_call(
        body, name=name, out_shape=jax.ShapeDtypeStruct((NDEV * m_per, n), blk.dtype),
        in_specs=[pl.BlockSpec(memory_space=pltpu.VMEM)], out_specs=pl.BlockSpec(memory_space=pltpu.VMEM),
        scratch_shapes=[pltpu.SemaphoreType.DMA((7,)), pltpu.SemaphoreType.DMA((7,)), pltpu.SemaphoreType.DMA],
        compiler_params=_cp(),
    )(blk)


def _gather_rows_carry(blk):
    m_per, n = blk.shape

    def rows(ref, px, py, pc):
        return ref.at[pl.ds((4 * px + 2 * py + pc) * m_per, m_per), :]

    def copy(ins, outs, send_sems, recv_sems, k, block, to, own=False):
        return pltpu.make_async_remote_copy(
            src_ref=ins[0] if own else rows(outs[0], *block), dst_ref=rows(outs[0], *block),
            send_sem=send_sems.at[k], recv_sem=recv_sems.at[k], device_id=to, device_id_type=MESH)

    def mine(ins, outs, send_sems):
        x, y, c, _ = _place()
        return pltpu.make_async_copy(ins[0], rows(outs[0], x, y, c), send_sems.at[7])

    def start(ins, outs, send_sems, recv_sems):
        x, y, c, chips = _place()
        mine(ins, outs, send_sems).start()
        copy(ins, outs, send_sems, recv_sems, 0, (x, y, c), (x, y, 1 - c), own=True).start()
        for j, chip in enumerate(chips):
            copy(ins, outs, send_sems, recv_sems, 1 + j, (x, y, c), (*chip, c), own=True).start()

    def finish(ins, outs, send_sems, recv_sems):
        x, y, c, chips = _place()
        for j, chip in enumerate(chips):
            copy(ins, outs, send_sems, recv_sems, 1 + j, (*chip, c), (x, y, c)).wait_recv()
            copy(ins, outs, send_sems, recv_sems, 4 + j, (*chip, c), (x, y, 1 - c)).start()
        copy(ins, outs, send_sems, recv_sems, 0, (x, y, 1 - c), (x, y, c)).wait_recv()
        for j, chip in enumerate(chips):
            copy(ins, outs, send_sems, recv_sems, 4 + j, (*chip, 1 - c), (x, y, c)).wait_recv()
        copy(ins, outs, send_sems, recv_sems, 0, (x, y, c), (x, y, 1 - c), own=True).wait_send()
        for j, chip in enumerate(chips):
            copy(ins, outs, send_sems, recv_sems, 1 + j, (x, y, c), (*chip, c), own=True).wait_send()
            copy(ins, outs, send_sems, recv_sems, 4 + j, (*chip, c), (x, y, 1 - c)).wait_send()
        mine(ins, outs, send_sems).wait()

    return _Carry([blk], [jax.ShapeDtypeStruct((NDEV * m_per, n), blk.dtype)], {}, 8, start, finish)


def _gather_carry(shards, piece=(0, 1, 1)):
    n = len(shards)
    first, count, of = piece

    def rows(ref, half):
        r2 = ref.shape[1] // 2
        return pl.ds(half * r2 + first * (r2 // of), count * (r2 // of))

    def over_ici(outs, send_sems, recv_sems, a, j, chip_xy, slot):
        x, y, c, _ = _place()
        blk = outs[a].at[slot, rows(outs[a], c), :]
        return pltpu.make_async_remote_copy(
            src_ref=blk, dst_ref=blk, send_sem=send_sems.at[6 * a + j], recv_sem=recv_sems.at[6 * a + j],
            device_id=(*chip_xy, c), device_id_type=MESH)

    def over_d2d(outs, send_sems, recv_sems, a, j, slot, half):
        x, y, c, _ = _place()
        blk = outs[a].at[slot, rows(outs[a], half), :]
        return pltpu.make_async_remote_copy(
            src_ref=blk, dst_ref=blk, send_sem=send_sems.at[6 * a + 3 + j], recv_sem=recv_sems.at[6 * a + 3 + j],
            device_id=(x, y, 1 - c), device_id_type=MESH)

    def start(ins, outs, send_sems, recv_sems):
        x, y, c, chips = _place()
        for a in range(n):
            for j, chip_xy in enumerate(chips):
                over_ici(outs, send_sems, recv_sems, a, j, chip_xy, 2 * x + y).start()

    def finish(ins, outs, send_sems, recv_sems):
        x, y, c, chips = _place()
        for a in range(n):
            for j, (cx, cy) in enumerate(chips):
                over_ici(outs, send_sems, recv_sems, a, j, (cx, cy), 2 * cx + cy).wait_recv()
                over_d2d(outs, send_sems, recv_sems, a, j, 2 * cx + cy, c).start()
        for a in range(n):
            for j, (cx, cy) in enumerate(chips):
                over_d2d(outs, send_sems, recv_sems, a, j, 2 * cx + cy, 1 - c).wait_recv()
        for a in range(n):
            for j, (cx, cy) in enumerate(chips):
                over_ici(outs, send_sems, recv_sems, a, j, (cx, cy), 2 * x + y).wait_send()
                over_d2d(outs, send_sems, recv_sems, a, j, 2 * cx + cy, c).wait_send()

    return _Carry(shards, [jax.ShapeDtypeStruct(s.shape, s.dtype) for s in shards],
                  {a: a for a in range(n)}, 6 * n, start, finish)


def _rs_pair(grads, name):
    n = len(grads)

    def body(*refs):
        ins, gots = refs[:n], refs[n:2 * n]
        send_sems, recv_sems = refs[2 * n:]
        x, y, c, _ = _place()
        cps = []
        for a in range(n):
            r2 = ins[a].shape[1] // 2
            cp = pltpu.make_async_remote_copy(
                src_ref=ins[a].at[:, pl.ds((1 - c) * r2, r2), :], dst_ref=gots[a],
                send_sem=send_sems.at[a], recv_sem=recv_sems.at[a],
                device_id=(x, y, 1 - c), device_id_type=MESH)
            cp.start()
            cps.append(cp)
        for cp in cps:
            cp.wait()

    half = [jax.ShapeDtypeStruct((NCHIP, g.shape[1] // 2, g.shape[2]), g.dtype) for g in grads]
    return pl.pallas_call(
        body, name=name, out_shape=half, in_specs=[ANY] * n, out_specs=[ANY] * n,
        scratch_shapes=[pltpu.SemaphoreType.DMA((n,)), pltpu.SemaphoreType.DMA((n,))],
        compiler_params=_cp(),
    )(*grads)


def _chips_carry(parts):
    n = len(parts)

    def send(ins, outs, send_sems, recv_sems, a, j, chip_xy):
        x, y, c, _ = _place()
        me, them = 2 * x + y, 2 * chip_xy[0] + chip_xy[1]
        return pltpu.make_async_remote_copy(
            src_ref=ins[a].at[them], dst_ref=outs[a].at[me - (me > them).astype(jnp.int32)],
            send_sem=send_sems.at[3 * a + j], recv_sem=recv_sems.at[3 * a + j],
            device_id=(*chip_xy, c), device_id_type=MESH)

    def start(ins, outs, send_sems, recv_sems):
        _, _, _, chips = _place()
        for a in range(n):
            for j, chip_xy in enumerate(chips):
                send(ins, outs, send_sems, recv_sems, a, j, chip_xy).start()

    def finish(ins, outs, send_sems, recv_sems):
        x, y, c, chips = _place()
        me = 2 * x + y
        for a in range(n):
            for j, (cx, cy) in enumerate(chips):
                them = 2 * cx + cy
                blk = outs[a].at[them - (them > me).astype(jnp.int32)]
                pltpu.make_async_remote_copy(
                    src_ref=blk, dst_ref=blk, send_sem=send_sems.at[3 * a + j], recv_sem=recv_sems.at[3 * a + j],
                    device_id=(cx, cy, c), device_id_type=MESH).wait_recv()
        for a in range(n):
            for j, chip_xy in enumerate(chips):
                send(ins, outs, send_sems, recv_sems, a, j, chip_xy).wait_send()

    return _Carry(parts, [jax.ShapeDtypeStruct((NCHIP - 1,) + p.shape[1:], p.dtype) for p in parts], {},
                  3 * n, start, finish)


def _rs_swap(fulls):
    n = len(fulls)

    def body(*refs):
        outs = refs[n:2 * n]
        send_sems, recv_sems = refs[2 * n:]
        x, y, c, _ = _place()
        cps = []
        for a in range(n):
            r2 = outs[a].shape[1] // 2
            mine = outs[a].at[:, pl.ds(c * r2, r2), :]
            cp = pltpu.make_async_remote_copy(
                src_ref=mine, dst_ref=mine, send_sem=send_sems.at[a], recv_sem=recv_sems.at[a],
                device_id=(x, y, 1 - c), device_id_type=MESH)
            cp.start()
            cps.append(cp)
        for a in range(n):
            r2 = outs[a].shape[1] // 2
            blk = outs[a].at[:, pl.ds((1 - c) * r2, r2), :]
            pltpu.make_async_remote_copy(
                src_ref=blk, dst_ref=blk, send_sem=send_sems.at[a], recv_sem=recv_sems.at[a],
                device_id=(x, y, 1 - c), device_id_type=MESH).wait_recv()
        for cp in cps:
            cp.wait_send()

    return pl.pallas_call(
        body, name="rs_swap", out_shape=[jax.ShapeDtypeStruct(f.shape, f.dtype) for f in fulls],
        in_specs=[ANY] * n, out_specs=[ANY] * n, input_output_aliases={a: a for a in range(n)},
        scratch_shapes=[pltpu.SemaphoreType.DMA((n,)), pltpu.SemaphoreType.DMA((n,))],
        compiler_params=_cp(),
    )(*fulls)


def _mm_ride(a, b, carry, **kw):
    if carry is None:
        return _mm(a, b, **kw), []
    return _mm(a, b, carry=carry, **kw)


def _layer_fwd(l, x, ada, w, small, ride, target=None):
    shift, scale, gate = ada[:, 0:D], ada[:, D:2 * D], ada[:, 2 * D:3 * D]
    h, h_t = _prenorm_fwd(x, small["g_pre"][l], scale, shift, f"prenorm_fwd{l}")
    carry, landed = ride("proj")
    proj, outs = _mm_ride(h, w["w_in"][l], carry, name=f"proj{l}", b_mode="nn_sh", tm=2048, out_dtype=BF16)
    landed(outs)
    a_in, a_in_t = _pool_fwd(proj, small["pool_w"][l], small["pool_scale"][l], f"pool_fwd{l}")
    carry, landed = ride("hgrn")
    (b_in, b_in_t, o_raw, states, mild), outs = _hgrn_fwd(proj, small["lb"][l], small["hgrn_norm_g"][l],
                                                         f"hgrn_fwd{l}", carry=carry)
    landed(outs)
    carry, landed = ride("tail")
    (br_a, br_b, merged_t, y, *x_new), outs = _layer_tail_fwd(
        proj, a_in, b_in, x, w["w_pool_o"][l], w["w_hgrn_o"][l].reshape(D, D), w["w_out"][l].reshape(D, D),
        gate, small["g_post"][l], f"tail_fwd{l}", target=target, carry=carry)
    landed(outs)
    saved = dict(x=x, h_t=h_t, proj=proj, a_in_t=a_in_t, b_in_t=b_in_t, o_raw=o_raw, states=states, mild=mild,
                 br_a=br_a, br_b=br_b, merged_t=merged_t, y=y, scale=scale, gate=gate)
    return x_new, saved


def _layer_bwd(l, dxn, sv, w, small, ride):
    dy, dbr_a, dbr_b, dmg, da_in, db_in, dgate, dg_post = _layer_head_bwd(
        dxn, sv["y"], sv["proj"], sv["br_a"], sv["br_b"], w["w_pool_o"][l], w["w_hgrn_o"][l].reshape(D, D),
        w["w_out"][l].reshape(D, D), sv["gate"], small["g_post"][l], f"head_bwd{l}")
    gw_out = _mm(sv["merged_t"], dy, name=f"gw_out{l}", out_dtype=BF16)
    gw_pool_o = _mm(sv["a_in_t"], dbr_a, name=f"gw_pool_o{l}", out_shards=NCHIP, out_dtype=BF16)
    gw_hgrn_o = _mm(sv["b_in_t"], dbr_b, name=f"gw_hgrn_o{l}", out_dtype=BF16)
    big = dict(w_pool_o=gw_pool_o, w_hgrn_o=gw_hgrn_o.reshape(NCHIP, D // NCHIP, D),
               w_out=gw_out.reshape(NCHIP, D // NCHIP, D))
    carry, landed = ride["hgrn"](big)
    (dhq, dhf, dhi, dhg, dlb, dgn), outs = _hgrn_bwd(db_in, sv["proj"], sv["o_raw"], sv["states"], sv["mild"],
                                                     small["lb"][l], small["hgrn_norm_g"][l], f"hgrn_bwd{l}",
                                                     carry=carry)
    landed(outs)
    dpv, dpg, dpw, dpsc = _pool_bwd(da_in, sv["proj"], small["pool_w"][l], small["pool_scale"][l],
                                    f"pool_bwd{l}")
    dproj = jnp.concatenate([dpv, dpg, dhq, dhf, dhi, dhg, dmg], axis=1)
    little = dict(dgate=dgate, g_post=dg_post, pool_w=dpw, pool_scale=dpsc, lb=dlb,
                  hgrn_norm_g=jnp.sum(dgn, axis=0, keepdims=True))
    carry, landed = ride["gw_in"](little)
    big["w_in"], outs = _mm_ride(sv["h_t"], dproj, carry, name=f"gw_in{l}", out_shards=NCHIP, out_dtype=BF16)
    landed(outs)
    carry, landed = ride["d_h"](big)
    dh, outs = _mm_ride(dproj, w["w_in"][l], carry, name=f"d_h{l}", b_mode="nt_shk", tn=1024)
    landed(outs)
    dx, dshift, dscale, dg_pre = _prenorm_bwd(dh, dxn, sv["x"], small["g_pre"][l], sv["scale"],
                                              f"prenorm_bwd{l}")
    little.update(dshift=dshift, dscale=dscale, g_pre=dg_pre)
    return dx, big, little


SMALL_ROWS = 176


def _rows8(t):
    t = t.reshape(-1, D)
    return jnp.pad(t, ((0, -t.shape[0] % 8), (0, 0)))


def _pack_small_weights(b_ada, g_pre, g_post, lb_logits, pool_w, pool_scale, hgrn_norm_g):
    gn = jnp.pad(hgrn_norm_g.reshape(1, 2 * HD), ((0, 0), (0, D - 2 * HD)))
    return jnp.concatenate([_rows8(b_ada), _rows8(g_pre), _rows8(g_post), _rows8(lb_logits), _rows8(pool_w),
                            _rows8(pool_scale), _rows8(gn)], axis=0)


def _pack_small(parts):
    both = lambda key: jnp.stack([parts[l][key] for l in range(2)])
    d_ada = jnp.stack([jnp.concatenate([p["dshift"], p["dscale"], p["dgate"]], axis=1) for p in parts])
    return _pack_small_weights(d_ada, both("g_pre"), both("g_post"), both("lb"), both("pool_w"),
                               both("pool_scale"), both("hgrn_norm_g"))


def _unpack_small(p):
    return (p[0:6].reshape(2, 3 * D), p[8:10], p[16:18], p[24:26], p[32:160].reshape(2, GROUPS, 128, 128),
            p[160:161].reshape(2, POOL_W), p[168:169, 0:2 * HD].reshape(2, HD))


def kernel(x, c, w_ada, b_ada, g_pre, g_post, w_in, pool_w, pool_scale, lb_logits, hgrn_norm_g, w_pool_o, w_hgrn_o, w_out, loss_target, m_w_ada, m_b_ada, m_g_pre, m_g_post, m_w_in, m_pool_w, m_pool_scale, m_lb_logits, m_hgrn_norm_g, m_w_pool_o, m_w_hgrn_o, m_w_out, v_w_ada, v_b_ada, v_g_pre, v_g_post, v_w_in, v_pool_w, v_pool_scale, v_lb_logits, v_hgrn_norm_g, v_w_pool_o, v_w_hgrn_o, v_w_out):
    ax, ay, ac = lax.axis_index("x"), lax.axis_index("y"), lax.axis_index("c")
    chip = 2 * ax + ay
    dev = 2 * chip + ac
    xe, te = x[0], loss_target[0]
    ada_s = w_ada.shape[2]

    big_names = ("w_in", "w_pool_o", "w_hgrn_o", "w_out")
    big_w = (w_in, w_pool_o, w_hgrn_o, w_out)
    core = jnp.stack([ac]).astype(jnp.int32)
    place = jnp.stack([chip, ac]).astype(jnp.int32)
    slots = {(k, l): _cast_to_slot(place, t, l, f"cast_{k}{l}") for l in range(2) for k, t in zip(big_names, big_w)}
    w = {k: [None, None] for k in big_names}
    (w["w_in"][0],) = _run_carry(_gather_carry([slots["w_in", 0]]), "gather_w_in0")
    def fills(keys):
        def landed(outs):
            for (k, l), o in zip(keys, outs):
                w[k][l] = slots[k, l] = o
        return landed

    rest0 = [(k, 0) for k in big_names[1:]]
    rest1 = [(k, 1) for k in big_names[1:]]
    no_carry = (None, lambda outs: None)

    def ride_fwd0(stage):
        if stage == "proj":
            return (_join_carries(_gather_carry([slots[t] for t in rest0]),
                                  _gather_carry([slots["w_in", 1]], piece=(0, 1, 4))),
                    fills(rest0 + [("w_in", 1)]))
        if stage == "hgrn":
            return _gather_carry([slots["w_in", 1]], piece=(1, 3, 4)), fills([("w_in", 1)])
        return _gather_carry([slots[t] for t in rest1]), fills(rest1)

    c_all = _gather_small(jnp.broadcast_to(c, (8, D)), "gather_c").reshape(NDEV, 8, D)[:, 0, :]
    c_pad = jnp.pad(c_all, ((0, ADA_PAD - NDEV), (0, 0)))
    b_sh = lax.dynamic_slice(b_ada, (0, chip * ada_s), (2, ada_s))
    ada_cols = _gather_small(_ada_fwd(c_pad, w_ada, b_sh), "gather_ada")
    ada_cols = ada_cols.reshape(NCHIP, 2, NDEV, 2, ada_s)[:, 0]
    ada_all = jnp.transpose(ada_cols, (2, 1, 0, 3)).reshape(2, NDEV, 3 * D)
    ada_me = lax.dynamic_slice(ada_all, (0, dev, 0), (2, 1, 3 * D))

    lbs = _lb_fwd(lb_logits)
    small = dict(g_pre=g_pre[:, None, :], g_post=g_post[:, None, :], pool_w=pool_w,
                 pool_scale=pool_scale[:, None, :], lb=lbs[:, None, :], hgrn_norm_g=hgrn_norm_g[:, None, :])

    (x1,), sv0 = _layer_fwd(0, xe, ada_me[0], w, small, ride_fwd0)
    (dx2, loss_blk), sv1 = _layer_fwd(1, x1, ada_me[1], w, small, lambda stage: no_carry, target=te)

    parts, recv = {}, {}

    def pair_sums(keys, grads, tag):
        got = _rs_pair(grads, f"rs_pair_{tag}")
        for kl, g, o in zip(keys, grads, got):
            parts[kl] = _pair_add(core, g, o, f"rs_add_{kl[0]}{kl[1]}")

    def exchange(keys):
        def landed(outs):
            recv.update(zip(keys, outs))
        return _chips_carry([parts[kl] for kl in keys]), landed

    def early(l):
        return [(k, l) for k in big_names[1:]]

    def ride_hgrn1(big):
        pair_sums(early(1), [big[k] for k in big_names[1:]], "l1_early")
        return exchange(early(1))

    def ride_d_h1(big):
        pair_sums([("w_in", 1)], [big["w_in"]], "l1_w_in")
        return no_carry

    def ride_hgrn0(big):
        pair_sums(early(0), [big[k] for k in big_names[1:]], "l0_early")
        return exchange([("w_in", 1)] + early(0))

    def ride_d_h0(big):
        pair_sums([("w_in", 0)], [big["w_in"]], "l0_w_in")
        return exchange([("w_in", 0)])

    dx1, big1, little1 = _layer_bwd(1, dx2, sv1, w, small,
                                    dict(hgrn=ride_hgrn1, gw_in=lambda little: no_carry, d_h=ride_d_h1))

    gathered = {}
    zero_row = jnp.zeros((1, D), F32)

    def ride_gw_in0(little):
        so_far = dict(little, dshift=zero_row, dscale=zero_row, g_pre=zero_row)

        def landed(outs):
            (gathered["early"],) = outs
        return _gather_rows_carry(_pack_small([so_far, little1])), landed

    dx0, big0, little0 = _layer_bwd(0, dx1, sv0, w, small,
                                    dict(hgrn=ride_hgrn0, gw_in=ride_gw_in0, d_h=ride_d_h0))
    loss = lax.psum(loss_blk[0, 0], ("x", "y", "c"))
    late = _rows8(jnp.stack([little0["dshift"], little0["dscale"], little0["g_pre"]]))
    late = _gather_small(late, "gather_small_late").reshape(NDEV, 8, D)
    packed = gathered["early"].reshape(NDEV, SMALL_ROWS, D)
    packed = packed.at[:, 0:2, :].set(late[:, 0:2, :]).at[:, 8:9, :].set(late[:, 2:3, :])
    red = []
    for k in big_names:
        both = _chip_sum(place, parts[k, 1], recv[k, 1], 1, None, f"rs_sum_{k}1")
        red.append(_chip_sum(place, parts[k, 0], recv[k, 0], 0, both, f"rs_sum_{k}0"))
    g_big = dict(zip(big_names, _rs_swap(red)))

    def upd(wt, g, m, v, name, carry=None):
        shp = wt.shape
        two = lambda t: t.reshape(-1, shp[-1])
        res = _adamw(two(wt), two(g), two(m), two(v), name, carry)
        return [t.reshape(shp) for t in res[:3]], res[3:]

    u_w_in, _ = upd(w_in, g_big["w_in"], m_w_in, v_w_in, "adamw_w_in")
    g_small = _sum_devices(packed)
    g_b_ada, g_g_pre, g_g_post, g_lb, g_pool_w, g_pool_scale, g_norm_g = _unpack_small(g_small)
    g_lb_logits = _lb_bwd(lb_logits, g_lb)
    d_ada_all = packed[:, 0:6, :].reshape(NDEV, 2, 3 * D)
    d_ada_sh = lax.dynamic_slice(jnp.transpose(d_ada_all, (1, 0, 2)), (0, 0, chip * ada_s), (2, NDEV, ada_s))
    d_ada_sh = jnp.pad(d_ada_sh, ((0, 0), (0, ADA_PAD - NDEV), (0, 0)))
    g_w_ada = _ada_wgrad(c_pad.T, d_ada_sh)

    u_w_ada, _ = upd(w_ada, g_w_ada, m_w_ada, v_w_ada, "adamw_w_ada")
    u_w_pool_o, _ = upd(w_pool_o, g_big["w_pool_o"], m_w_pool_o, v_w_pool_o, "adamw_w_pool_o")
    u_w_hgrn_o, _ = upd(w_hgrn_o, g_big["w_hgrn_o"], m_w_hgrn_o, v_w_hgrn_o, "adamw_w_hgrn_o")
    u_w_out, _ = upd(w_out, g_big["w_out"], m_w_out, v_w_out, "adamw_w_out")
    small_w = dict(b_ada=(b_ada, m_b_ada, v_b_ada), g_pre=(g_pre, m_g_pre, v_g_pre),
                   g_post=(g_post, m_g_post, v_g_post), lb_logits=(lb_logits, m_lb_logits, v_lb_logits),
                   pool_w=(pool_w, m_pool_w, v_pool_w), pool_scale=(pool_scale, m_pool_scale, v_pool_scale),
                   hgrn_norm_g=(hgrn_norm_g, m_hgrn_norm_g, v_hgrn_norm_g))
    in_rows = [tuple(t.reshape(rows, width) for t in small_w[key]) for key, _, rows, width in SMALL_PARTS]
    u_rows = _adamw_small(g_small, g_lb_logits, in_rows)
    u_small = {key: [t.reshape(small_w[key][0].shape) for t in u_rows[p]]
               for p, (key, _, _, _) in enumerate(SMALL_PARTS)}

    grads_out = (g_w_ada, g_b_ada, g_g_pre, g_g_post, g_big["w_in"], g_pool_w, g_pool_scale, g_lb_logits,
                 g_norm_g, g_big["w_pool_o"], g_big["w_hgrn_o"], g_big["w_out"])

    def ordered(k):
        s = lambda key: u_small[key][k]
        return (u_w_ada[k], s("b_ada"), s("g_pre"), s("g_post"), u_w_in[k], s("pool_w"), s("pool_scale"),
                s("lb_logits"), s("hgrn_norm_g"), u_w_pool_o[k], u_w_hgrn_o[k], u_w_out[k])

    return (loss, dx0[None], *grads_out, *ordered(0), *ordered(1), *ordered(2))
```

```python
import functools

import jax
import jax.numpy as jnp
from jax import lax
from jax.experimental import pallas as pl
from jax.experimental.pallas import tpu as pltpu

F32 = jnp.float32
BF16 = jnp.bfloat16
MESH = pl.DeviceIdType.MESH

D = 1024
HEADS = 8
HD = 128
GROUPS = 4
POOL_W = 512
WINDOWS = (2, 4, 8, 16)
CH = 64
SB = 16
NH = 2
IN_W = 7168
NCHIP = 4
NDEV = 8
EPS = 1e-6
PV0, PG0, HQ0, HF0, HI0, HG0 = 0, 4, 8, 16, 24, 32
MGP_BLK, MGH_BLK = 5, 6

LR, B1, B2, AEPS, WD, STEP = 0.001, 0.9, 0.999, 1e-08, 0.01, 10
VMEM_LIMIT = 56 * 1024 * 1024


def _cp(sem=None, **kw):
    if sem is not None:
        kw["dimension_semantics"] = sem
    return pltpu.CompilerParams(vmem_limit_bytes=VMEM_LIMIT, **kw)


def _sig(z):
    return 1.0 / (1.0 + jnp.exp(-z))


def _dsilu(z, s):
    return s * (1.0 + z * (1.0 - s))


def _row_tile(rows, cap):
    if rows <= cap:
        return rows
    t = 1 << (cap.bit_length() - 1)
    while rows % t:
        t //= 2
    return t


ANY = pl.BlockSpec(memory_space=pl.ANY)


class _Carry:
    def __init__(self, ins, outs, aliases, n_sem, start, finish):
        self.ins, self.outs, self.aliases, self.n_sem = list(ins), list(outs), dict(aliases), n_sem
        self.start, self.finish = start, finish


class _SemWindow:
    def __init__(self, ref, base):
        self._ref, self._base = ref, base

    @property
    def at(self):
        return self

    def __getitem__(self, k):
        return self._ref.at[self._base + k]


def _join_carries(*carries):
    ins, outs, aliases, spans, n_sem = [], [], {}, [], 0
    for cr in carries:
        aliases.update({len(ins) + i: len(outs) + o for i, o in cr.aliases.items()})
        spans.append((len(ins), len(cr.ins), len(outs), len(cr.outs), n_sem))
        ins, outs, n_sem = ins + cr.ins, outs + cr.outs, n_sem + cr.n_sem

    def run(which):
        def fn(i_refs, o_refs, send_sems, recv_sems):
            for cr, (i0, ni, o0, no, s0) in zip(carries, spans):
                getattr(cr, which)(i_refs[i0:i0 + ni], o_refs[o0:o0 + no], _SemWindow(send_sems, s0),
                                   _SemWindow(recv_sems, s0))
        return fn

    return _Carry(ins, outs, aliases, n_sem, run("start"), run("finish"))


def _call(body, *, name, grid, in_specs, out_specs, out_shape, args, scratch_shapes=(), sem=None, carry=None):
    in_specs, out_specs, out_shape = list(in_specs), list(out_specs), list(out_shape)
    scratch_shapes = list(scratch_shapes)
    if carry is None:
        outs = pl.pallas_call(body, name=name, grid=grid, in_specs=in_specs, out_specs=out_specs,
                              out_shape=out_shape, scratch_shapes=scratch_shapes,
                              compiler_params=_cp(sem))(*args)
        return list(outs)
    n_in, n_out, n_scr = len(in_specs), len(out_specs), len(scratch_shapes)
    c_in, c_out = len(carry.ins), len(carry.outs)

    def wrapped(*refs):
        k_in, rest = refs[:n_in], refs[n_in:]
        ci, rest = rest[:c_in], rest[c_in:]
        k_out, rest = rest[:n_out], rest[n_out:]
        co, rest = rest[:c_out], rest[c_out:]
        k_scr, (ssem, rsem) = rest[:n_scr], rest[n_scr:]
        pids = [pl.program_id(d) for d in range(len(grid))]
        first = functools.reduce(jnp.logical_and, [p == 0 for p in pids])
        last = functools.reduce(jnp.logical_and, [p == g - 1 for p, g in zip(pids, grid)])

        @pl.when(first)
        def _():
            carry.start(ci, co, ssem, rsem)

        body(*k_in, *k_out, *k_scr)

        @pl.when(last)
        def _():
            carry.finish(ci, co, ssem, rsem)

    outs = pl.pallas_call(
        wrapped, name=name, grid=grid, in_specs=in_specs + [ANY] * c_in, out_specs=out_specs + [ANY] * c_out,
        out_shape=out_shape + carry.outs,
        input_output_aliases={n_in + i: n_out + o for i, o in carry.aliases.items()},
        scratch_shapes=scratch_shapes + [pltpu.SemaphoreType.DMA((carry.n_sem,))] * 2,
        compiler_params=_cp(("arbitrary",) * len(grid)),
    )(*args, *carry.ins)
    return list(outs)


def _run_carry(carry, name):
    c_in, c_out = len(carry.ins), len(carry.outs)

    def body(*refs):
        ci, co, (ssem, rsem) = refs[:c_in], refs[c_in:c_in + c_out], refs[c_in + c_out:]
        carry.start(ci, co, ssem, rsem)
        carry.finish(ci, co, ssem, rsem)

    outs = pl.pallas_call(
        body, name=name, in_specs=[ANY] * c_in, out_specs=[ANY] * c_out, out_shape=carry.outs,
        input_output_aliases=carry.aliases,
        scratch_shapes=[pltpu.SemaphoreType.DMA((carry.n_sem,))] * 2, compiler_params=_cp(),
    )(*carry.ins)
    return list(outs)


def _mm(a, b, *, name, b_mode="nn", out_shards=0, tm=1024, tn=256, tk=None, out_dtype=F32, carry=None):
    M, K = a.shape
    if b_mode == "nn":
        N = b.shape[1]
    elif b_mode == "nt":
        N = b.shape[0]
    elif b_mode == "nn_sh":
        N = b.shape[0] * b.shape[2]
    else:
        N = b.shape[1]
    tm = _row_tile(M, tm)
    if b_mode == "nn_sh":
        tn = _row_tile(b.shape[2], tn)
    elif out_shards:
        tn = _row_tile(N // out_shards, tn)
    else:
        tn = _row_tile(N, tn)
    if tk is None:
        tk = K if b_mode != "nt_shk" else b.shape[2]
    if b_mode == "nt_shk":
        tk = _row_tile(b.shape[2], tk)
    nm, nn, nk = M // tm, N // tn, K // tk

    a_spec = pl.BlockSpec((tm, tk), lambda m, n, k: (m, k))
    if b_mode == "nn":
        b_spec = pl.BlockSpec((tk, tn), lambda m, n, k: (k, n))
    elif b_mode == "nt":
        b_spec = pl.BlockSpec((tn, tk), lambda m, n, k: (n, k))
    elif b_mode == "nn_sh":
        nps = b.shape[2] // tn
        b_spec = pl.BlockSpec((None, tk, tn), lambda m, n, k: (n // nps, k, n % nps))
    else:
        kps = b.shape[2] // tk
        b_spec = pl.BlockSpec((None, tn, tk), lambda m, n, k: (k // kps, n, k % kps))
    if out_shards:
        ops = (N // out_shards) // tn
        o_spec = pl.BlockSpec((None, tm, tn), lambda m, n, k: (n // ops, m, n % ops))
        o_shape = jax.ShapeDtypeStruct((out_shards, M, N // out_shards), out_dtype)
    else:
        o_spec = pl.BlockSpec((tm, tn), lambda m, n, k: (m, n))
        o_shape = jax.ShapeDtypeStruct((M, N), out_dtype)
    trans_b = b_mode in ("nt", "nt_shk")
    dn = (((1,), (1,)), ((), ())) if trans_b else (((1,), (0,)), ((), ()))

    def body(a_ref, b_ref, o_ref, acc_ref):
        k = pl.program_id(2)

        @pl.when(k == 0)
        def _():
            acc_ref[...] = jnp.zeros(acc_ref.shape, F32)

        acc_ref[...] += lax.dot_general(a_ref[...].astype(BF16), b_ref[...].astype(BF16), dn,
                                        preferred_element_type=F32)

        @pl.when(k == nk - 1)
        def _():
            o_ref[...] = acc_ref[...].astype(o_ref.dtype)

    outs = _call(body, name=name, grid=(nm, nn, nk), in_specs=[a_spec, b_spec], out_specs=[o_spec],
                 out_shape=[o_shape], scratch_shapes=[pltpu.VMEM((tm, tn), F32)],
                 sem=("parallel", "parallel", "arbitrary"), args=(a, b), carry=carry)
    return outs[0] if carry is None else (outs[0], outs[1:])


def _rowvec(n=D):
    return pl.BlockSpec((1, n), lambda i: (0, 0))


def _prenorm_fwd(x, g, scale, shift, name):
    S = x.shape[0]
    tr = _row_tile(S, 256)

    def body(x_ref, g_ref, sc_ref, sh_ref, h_ref, ht_ref):
        xv = x_ref[...]
        r = lax.rsqrt(jnp.mean(xv * xv, axis=-1, keepdims=True) + EPS)
        hv = (xv * r) * g_ref[...] * (1.0 + sc_ref[...]) + sh_ref[...]
        h_ref[...] = hv.astype(BF16)
        ht_ref[...] = hv.T.astype(BF16)

    return pl.pallas_call(
        body, name=name, grid=(S // tr,),
        in_specs=[pl.BlockSpec((tr, D), lambda i: (i, 0)), _rowvec(), _rowvec(), _rowvec()],
        out_specs=[pl.BlockSpec((tr, D), lambda i: (i, 0)), pl.BlockSpec((D, tr), lambda i: (0, i))],
        out_shape=[jax.ShapeDtypeStruct((S, D), BF16), jax.ShapeDtypeStruct((D, S), BF16)],
        compiler_params=_cp(("parallel",)),
    )(x, g, scale, shift)


def _prenorm_bwd(dh, dxn, x, g, scale, name):
    S = x.shape[0]
    tr = _row_tile(S, 256)

    def body(dh_ref, dxn_ref, x_ref, g_ref, sc_ref, dx_ref, dsh_ref, dsc_ref, dg_ref):
        i = pl.program_id(0)

        @pl.when(i == 0)
        def _():
            dsh_ref[...] = jnp.zeros((1, D), F32)
            dsc_ref[...] = jnp.zeros((1, D), F32)
            dg_ref[...] = jnp.zeros((1, D), F32)

        xv = x_ref[...]
        dhv = dh_ref[...]
        gv = g_ref[...]
        mod = 1.0 + sc_ref[...]
        r = lax.rsqrt(jnp.mean(xv * xv, axis=-1, keepdims=True) + EPS)
        xh = xv * r
        dsh_ref[...] += jnp.sum(dhv, axis=0, keepdims=True)
        dsc_ref[...] += jnp.sum(dhv * (xh * gv), axis=0, keepdims=True)
        dg_ref[...] += jnp.sum(dhv * mod * xh, axis=0, keepdims=True)
        u = dhv * mod * gv
        dx_ref[...] = dxn_ref[...] + r * u - xv * (r * r * r) * jnp.mean(u * xv, axis=-1, keepdims=True)

    tile = pl.BlockSpec((tr, D), lambda i: (i, 0))
    return pl.pallas_call(
        body, name=name, grid=(S // tr,),
        in_specs=[tile, tile, tile, _rowvec(), _rowvec()],
        out_specs=[tile, _rowvec(), _rowvec(), _rowvec()],
        out_shape=[jax.ShapeDtypeStruct((S, D), F32)] + [jax.ShapeDtypeStruct((1, D), F32)] * 3,
        compiler_params=_cp(("arbitrary",)),
    )(dh, dxn, x, g, scale)


def _postnorm_fwd(x, y, gate, g, name):
    S = x.shape[0]
    tr = _row_tile(S, 256)

    def body(x_ref, y_ref, gate_ref, g_ref, o_ref):
        yv = y_ref[...]
        r = lax.rsqrt(jnp.mean(yv * yv, axis=-1, keepdims=True) + EPS)
        o_ref[...] = x_ref[...] + gate_ref[...] * ((yv * r) * g_ref[...])

    tile = pl.BlockSpec((tr, D), lambda i: (i, 0))
    return pl.pallas_call(
        body, name=name, grid=(S // tr,), in_specs=[tile, tile, _rowvec(), _rowvec()],
        out_specs=tile, out_shape=jax.ShapeDtypeStruct((S, D), F32), compiler_params=_cp(("parallel",)),
    )(x, y, gate, g)


def _postnorm_bwd(dxn, y, gate, g, name):
    S = y.shape[0]
    tr = _row_tile(S, 256)

    def body(dxn_ref, y_ref, gate_ref, g_ref, dy_ref, dgate_ref, dg_ref):
        i = pl.program_id(0)

        @pl.when(i == 0)
        def _():
            dgate_ref[...] = jnp.zeros((1, D), F32)
            dg_ref[...] = jnp.zeros((1, D), F32)

        yv = y_ref[...]
        dv = dxn_ref[...]
        gv = g_ref[...]
        gt = gate_ref[...]
        r = lax.rsqrt(jnp.mean(yv * yv, axis=-1, keepdims=True) + EPS)
        yh = yv * r
        dgate_ref[...] += jnp.sum(dv * (yh * gv), axis=0, keepdims=True)
        dg_ref[...] += jnp.sum(dv * gt * yh, axis=0, keepdims=True)
        u = dv * gt * gv
        dy_ref[...] = (r * u - yv * (r * r * r) * jnp.mean(u * yv, axis=-1, keepdims=True)).astype(BF16)

    tile = pl.BlockSpec((tr, D), lambda i: (i, 0))
    return pl.pallas_call(
        body, name=name, grid=(S // tr,), in_specs=[tile, tile, _rowvec(), _rowvec()],
        out_specs=[tile, _rowvec(), _rowvec()],
        out_shape=[jax.ShapeDtypeStruct((S, D), BF16), jax.ShapeDtypeStruct((1, D), F32),
                   jax.ShapeDtypeStruct((1, D), F32)],
        compiler_params=_cp(("arbitrary",)),
    )(dxn, y, gate, g)


def _loss_head(xo, target, name):
    S = xo.shape[0]
    tr = _row_tile(S, 256)

    def body(x_ref, t_ref, dx_ref, l_ref):
        i = pl.program_id(0)

        @pl.when(i == 0)
        def _():
            l_ref[...] = jnp.zeros((8, 128), F32)

        err = x_ref[...] - t_ref[...]
        dx_ref[...] = err * (1.0 / D)
        l_ref[...] += 0.5 * jnp.sum(jnp.mean(err * err, axis=-1, keepdims=True))

    tile = pl.BlockSpec((tr, D), lambda i: (i, 0))
    return pl.pallas_call(
        body, name=name, grid=(S // tr,), in_specs=[tile, tile],
        out_specs=[tile, pl.BlockSpec((8, 128), lambda i: (0, 0))],
        out_shape=[jax.ShapeDtypeStruct((S, D), F32), jax.ShapeDtypeStruct((8, 128), F32)],
        compiler_params=_cp(("arbitrary",)),
    )(xo, target)


def _layer_tail_fwd(proj, a_in, b_in, x, w_po, w_ho, w_out, gate, g, name, target=None, carry=None):
    S = proj.shape[0]
    tr = _row_tile(S, 256)
    nsh, _, wsh = w_po.shape
    n_in = 10 + (target is not None)

    def body(*refs):
        (mgp_ref, mgh_ref, a_ref, b_ref, x_ref, wpo_ref, who_ref, wout_ref, gate_ref, g_ref) = refs[:10]
        bra_ref, brb_ref, mt_ref, y_ref, xn_ref = refs[n_in:n_in + 5]
        av = a_ref[...]
        bra = jnp.concatenate([jnp.dot(av, wpo_ref[j], preferred_element_type=F32) for j in range(nsh)], axis=1)
        brb = jnp.dot(b_ref[...], who_ref[...], preferred_element_type=F32)
        mv = _sig(mgp_ref[...].astype(F32)) * bra + _sig(mgh_ref[...].astype(F32)) * brb
        bra_ref[...] = bra.astype(BF16)
        brb_ref[...] = brb.astype(BF16)
        mt_ref[...] = mv.T.astype(BF16)
        yv = jnp.dot(mv.astype(BF16), wout_ref[...], preferred_element_type=F32)
        y_ref[...] = yv
        r = lax.rsqrt(jnp.mean(yv * yv, axis=-1, keepdims=True) + EPS)
        xn = x_ref[...] + gate_ref[...] * ((yv * r) * g_ref[...])
        if target is None:
            xn_ref[...] = xn
        else:
            t_ref, l_ref = refs[10], refs[n_in + 5]

            @pl.when(pl.program_id(0) == 0)
            def _():
                l_ref[...] = jnp.zeros((8, 128), F32)

            err = xn - t_ref[...]
            xn_ref[...] = err * (1.0 / D)
            l_ref[...] += 0.5 * jnp.sum(jnp.mean(err * err, axis=-1, keepdims=True))

    tile = pl.BlockSpec((tr, D), lambda i: (i, 0))
    whole = lambda t: pl.BlockSpec(t.shape, lambda i: (0,) * t.ndim)
    last = target is not None
    outs = _call(
        body, name=name, grid=(S // tr,),
        in_specs=[pl.BlockSpec((tr, D), lambda i: (i, MGP_BLK)), pl.BlockSpec((tr, D), lambda i: (i, MGH_BLK)),
                  pl.BlockSpec((tr, POOL_W), lambda i: (i, 0)), tile, tile, whole(w_po), whole(w_ho),
                  whole(w_out), _rowvec(), _rowvec()] + [tile] * last,
        out_specs=[tile, tile, pl.BlockSpec((D, tr), lambda i: (0, i)), tile, tile]
        + [pl.BlockSpec((8, 128), lambda i: (0, 0))] * last,
        out_shape=[jax.ShapeDtypeStruct((S, D), BF16), jax.ShapeDtypeStruct((S, D), BF16),
                   jax.ShapeDtypeStruct((D, S), BF16), jax.ShapeDtypeStruct((S, D), F32),
                   jax.ShapeDtypeStruct((S, D), F32)] + [jax.ShapeDtypeStruct((8, 128), F32)] * last,
        sem=("arbitrary",) if last else ("parallel",),
        args=(proj, proj, a_in, b_in, x, w_po, w_ho, w_out, gate, g) + ((target,) if last else ()), carry=carry)
    return outs[:5 + last], outs[5 + last:]


def _layer_head_bwd(dxn, y, proj, br_a, br_b, w_po, w_ho, w_out, gate, g, name):
    S = y.shape[0]
    tr = _row_tile(S, 256)
    nsh, _, wsh = w_po.shape

    def body(dxn_ref, y_ref, mgp_ref, mgh_ref, bra_ref, brb_ref, wpo_ref, who_ref, wout_ref, gate_ref, g_ref,
             dy_ref, dba_ref, dbb_ref, dmg_ref, dain_ref, dbin_ref, dgate_ref, dg_ref):
        i = pl.program_id(0)

        @pl.when(i == 0)
        def _():
            dgate_ref[...] = jnp.zeros((1, D), F32)
            dg_ref[...] = jnp.zeros((1, D), F32)

        yv = y_ref[...]
        dv = dxn_ref[...]
        gv = g_ref[...]
        gt = gate_ref[...]
        r = lax.rsqrt(jnp.mean(yv * yv, axis=-1, keepdims=True) + EPS)
        yh = yv * r
        dgate_ref[...] += jnp.sum(dv * (yh * gv), axis=0, keepdims=True)
        dg_ref[...] += jnp.sum(dv * gt * yh, axis=0, keepdims=True)
        u = dv * gt * gv
        dy = (r * u - yv * (r * r * r) * jnp.mean(u * yv, axis=-1, keepdims=True)).astype(BF16)
        dy_ref[...] = dy
        dm = _dot_nt(dy, wout_ref[...])
        sp = _sig(mgp_ref[...].astype(F32))
        sh = _sig(mgh_ref[...].astype(F32))
        dba = (dm * sp).astype(BF16)
        dbb = (dm * sh).astype(BF16)
        dba_ref[...] = dba
        dbb_ref[...] = dbb
        dmg_ref[:, 0:D] = (dm * bra_ref[...].astype(F32) * sp * (1.0 - sp)).astype(BF16)
        dmg_ref[:, D:2 * D] = (dm * brb_ref[...].astype(F32) * sh * (1.0 - sh)).astype(BF16)
        dain = _dot_nt(dba[:, 0:wsh], wpo_ref[0])
        for j in range(1, nsh):
            dain = dain + _dot_nt(dba[:, j * wsh:(j + 1) * wsh], wpo_ref[j])
        dain_ref[...] = dain
        dbin_ref[...] = _dot_nt(dbb, who_ref[...])

    tile = pl.BlockSpec((tr, D), lambda i: (i, 0))
    whole = lambda t: pl.BlockSpec(t.shape, lambda i: (0,) * t.ndim)
    return pl.pallas_call(
        body, name=name, grid=(S // tr,),
        in_specs=[tile, tile, pl.BlockSpec((tr, D), lambda i: (i, MGP_BLK)),
                  pl.BlockSpec((tr, D), lambda i: (i, MGH_BLK)), tile, tile, whole(w_po), whole(w_ho),
                  whole(w_out), _rowvec(), _rowvec()],
        out_specs=[tile, tile, tile, pl.BlockSpec((tr, 2 * D), lambda i: (i, 0)),
                   pl.BlockSpec((tr, POOL_W), lambda i: (i, 0)), tile, _rowvec(), _rowvec()],
        out_shape=[jax.ShapeDtypeStruct((S, D), BF16)] * 3
        + [jax.ShapeDtypeStruct((S, 2 * D), BF16), jax.ShapeDtypeStruct((S, POOL_W), F32),
           jax.ShapeDtypeStruct((S, D), F32), jax.ShapeDtypeStruct((1, D), F32), jax.ShapeDtypeStruct((1, D), F32)],
        compiler_params=_cp(("arbitrary",)),
    )(dxn, y, proj, proj, br_a, br_b, w_po, w_ho, w_out, gate, g)


def _merge_fwd(proj, br_a, br_b, name):
    S = proj.shape[0]
    tr = _row_tile(S, 256)

    def body(mgp_ref, mgh_ref, a_ref, b_ref, o_ref, ot_ref):
        mv = _sig(mgp_ref[...]) * a_ref[...] + _sig(mgh_ref[...]) * b_ref[...]
        o_ref[...] = mv.astype(BF16)
        ot_ref[...] = mv.T.astype(BF16)

    tile = pl.BlockSpec((tr, D), lambda i: (i, 0))
    return pl.pallas_call(
        body, name=name, grid=(S // tr,),
        in_specs=[pl.BlockSpec((tr, D), lambda i: (i, MGP_BLK)), pl.BlockSpec((tr, D), lambda i: (i, MGH_BLK)),
                  tile, tile],
        out_specs=[tile, pl.BlockSpec((D, tr), lambda i: (0, i))],
        out_shape=[jax.ShapeDtypeStruct((S, D), BF16), jax.ShapeDtypeStruct((D, S), BF16)],
        compiler_params=_cp(("parallel",)),
    )(proj, proj, br_a, br_b)


def _merge_bwd(dm, proj, br_a, br_b, name):
    S = proj.shape[0]
    tr = _row_tile(S, 256)

    def body(dm_ref, mgp_ref, mgh_ref, a_ref, b_ref, da_ref, db_ref, dmg_ref):
        dmv = dm_ref[...]
        sp = _sig(mgp_ref[...])
        sh = _sig(mgh_ref[...])
        da_ref[...] = (dmv * sp).astype(BF16)
        db_ref[...] = (dmv * sh).astype(BF16)
        dmg_ref[:, 0:D] = (dmv * a_ref[...] * sp * (1.0 - sp)).astype(BF16)
        dmg_ref[:, D:2 * D] = (dmv * b_ref[...] * sh * (1.0 - sh)).astype(BF16)

    tile = pl.BlockSpec((tr, D), lambda i: (i, 0))
    return pl.pallas_call(
        body, name=name, grid=(S // tr,),
        in_specs=[tile, pl.BlockSpec((tr, D), lambda i: (i, MGP_BLK)),
                  pl.BlockSpec((tr, D), lambda i: (i, MGH_BLK)), tile, tile],
        out_specs=[tile, tile, pl.BlockSpec((tr, 2 * D), lambda i: (i, 0))],
        out_shape=[jax.ShapeDtypeStruct((S, D), BF16), jax.ShapeDtypeStruct((S, D), BF16),
                   jax.ShapeDtypeStruct((S, 2 * D), BF16)],
        compiler_params=_cp(("parallel",)),
    )(dm, proj, proj, br_a, br_b)


def _pool_pieces(u, g, S):
    rowi = lax.broadcasted_iota(jnp.int32, (S, 1), 0)

    def down(z, k):
        return jnp.where(rowi >= k, pltpu.roll(z, k, axis=0), 0.0)

    s2 = u + down(u, 1)
    s4 = s2 + down(s2, 2)
    s8 = s4 + down(s4, 4)
    s16 = s8 + down(s8, 8)
    win = jnp.where(g == 0, s2, jnp.where(g == 1, s4, jnp.where(g == 2, s8, s16)))
    w = jnp.where(g == 0, 2, jnp.where(g == 1, 4, jnp.where(g == 2, 8, 16)))
    count = jnp.minimum(rowi + 1, w).astype(F32)
    return win / count - u, count, rowi


def _pool_fwd(proj, pw, pscale, name):
    S = proj.shape[0]

    def body(pv_ref, pg_ref, pw_ref, sc_ref, a_ref, at_ref):
        g = pl.program_id(0)
        pooled, _, _ = _pool_pieces(pv_ref[...].astype(F32), g, S)
        pm = jnp.dot(pooled.astype(BF16), pw_ref[...].astype(BF16), preferred_element_type=F32)
        pgv = pg_ref[...].astype(F32)
        av = pm * sc_ref[...] * (pgv * _sig(pgv))
        a_ref[...] = av.astype(BF16)
        at_ref[...] = av.T.astype(BF16)

    return pl.pallas_call(
        body, name=name, grid=(GROUPS,),
        in_specs=[pl.BlockSpec((S, 128), lambda g: (0, PV0 + g)), pl.BlockSpec((S, 128), lambda g: (0, PG0 + g)),
                  pl.BlockSpec((None, 128, 128), lambda g: (g, 0, 0)), pl.BlockSpec((1, 128), lambda g: (0, g))],
        out_specs=[pl.BlockSpec((S, 128), lambda g: (0, g)), pl.BlockSpec((128, S), lambda g: (g, 0))],
        out_shape=[jax.ShapeDtypeStruct((S, POOL_W), BF16), jax.ShapeDtypeStruct((POOL_W, S), BF16)],
        compiler_params=_cp(("parallel",)),
    )(proj, proj, pw, pscale)


def _pool_bwd(da, proj, pw, pscale, name):
    S = proj.shape[0]

    def body(da_ref, pv_ref, pg_ref, pw_ref, sc_ref, dpv_ref, dpg_ref, dpw_ref, dsc_ref):
        g = pl.program_id(0)
        pooled, count, rowi = _pool_pieces(pv_ref[...].astype(F32), g, S)
        pwb = pw_ref[...].astype(BF16)
        pm = jnp.dot(pooled.astype(BF16), pwb, preferred_element_type=F32)
        scv = sc_ref[...]
        pgv = pg_ref[...].astype(F32)
        sg = _sig(pgv)
        dav = da_ref[...]
        d_ps = dav * (pgv * sg)
        dpg_ref[...] = (dav * (pm * scv) * _dsilu(pgv, sg)).astype(BF16)
        dsc_ref[...] = jnp.sum(d_ps * pm, axis=0, keepdims=True)
        d_pm = (d_ps * scv).astype(BF16)
        dpw_ref[...] = lax.dot_general(pooled.astype(BF16), d_pm, (((0,), (0,)), ((), ())),
                                       preferred_element_type=F32)
        d_pooled = lax.dot_general(d_pm, pwb, (((1,), (1,)), ((), ())), preferred_element_type=F32)
        z = d_pooled / count

        def up(v, k):
            return jnp.where(rowi < S - k, pltpu.roll(v, S - k, axis=0), 0.0)

        t2 = z + up(z, 1)
        t4 = t2 + up(t2, 2)
        t8 = t4 + up(t4, 4)
        t16 = t8 + up(t8, 8)
        adj = jnp.where(g == 0, t2, jnp.where(g == 1, t4, jnp.where(g == 2, t8, t16)))
        dpv_ref[...] = (adj - d_pooled).astype(BF16)

    col = lambda g: (0, g)
    return pl.pallas_call(
        body, name=name, grid=(GROUPS,),
        in_specs=[pl.BlockSpec((S, 128), col), pl.BlockSpec((S, 128), lambda g: (0, PV0 + g)),
                  pl.BlockSpec((S, 128), lambda g: (0, PG0 + g)),
                  pl.BlockSpec((None, 128, 128), lambda g: (g, 0, 0)), pl.BlockSpec((1, 128), col)],
        out_specs=[pl.BlockSpec((S, 128), col), pl.BlockSpec((S, 128), col),
                   pl.BlockSpec((None, 128, 128), lambda g: (g, 0, 0)), pl.BlockSpec((1, 128), col)],
        out_shape=[jax.ShapeDtypeStruct((S, POOL_W), BF16), jax.ShapeDtypeStruct((S, POOL_W), BF16),
                   jax.ShapeDtypeStruct((GROUPS, 128, 128), F32), jax.ShapeDtypeStruct((1, POOL_W), F32)],
        compiler_params=_cp(("parallel",)),
    )(da, proj, proj, pw, pscale)


def _chunk_cumsum(z, rowi):
    for sh in (1, 2, 4, 8, 16, 32):
        z = z + jnp.where(rowi >= sh, pltpu.roll(z, sh, axis=0), 0.0)
    return z


def _chunk_rev_cumsum(z, rowi):
    for sh in (1, 2, 4, 8, 16, 32):
        z = z + jnp.where(rowi < CH - sh, pltpu.roll(z, CH - sh, axis=0), 0.0)
    return z


def _dot_nn(a, b):
    return jnp.dot(a.astype(BF16), b.astype(BF16), preferred_element_type=F32)


def _dot_nt(a, b):
    return lax.dot_general(a.astype(BF16), b.astype(BF16), (((1,), (1,)), ((), ())), preferred_element_type=F32)


def _dot_tn(a, b):
    return lax.dot_general(a.astype(BF16), b.astype(BF16), (((0,), (0,)), ((), ())), preferred_element_type=F32)


def _gates(hq, hf, lbv):
    hq, hf = hq.astype(F32), hf.astype(F32)
    sq = _sig(hq)
    sf = _sig(hf)
    f = lbv + (1.0 - lbv) * sf
    fc = jnp.maximum(f, 1e-30)
    return hq * sq, sq, sf, f, fc, jnp.log(fc)


DECAY_CAP = 60.0


def _block_ref(c_ref, i):
    if i == 0:
        return jnp.zeros((1, HD), F32)
    return c_ref[SB * i - 1:SB * i, :]


def _block_decay(c_ref):
    spans = [_block_ref(c_ref, i) - c_ref[SB * (i + 1) - 1:SB * (i + 1), :] for i in range(CH // SB)]
    return functools.reduce(jnp.maximum, spans)


def _hgrn_fwd(proj, lb, gn, name, carry=None):
    S = proj.shape[0]
    nch = S // CH
    W = NH * HD

    def body(hq_ref, hf_ref, hi_ref, hg_ref, lb_ref, gn_ref, bin_ref, bint_ref, oraw_ref, st_ref, mild_ref,
             cum_ref, q_s, k_s, c_s, v_s, o_s, state_s, qf_s, kf_s, cf_s):
        state_s[...] = jnp.zeros((NH, HD, HD), F32)
        rowi = lax.broadcasted_iota(jnp.int32, (CH, 1), 0)
        coli = lax.broadcasted_iota(jnp.int32, (1, CH), 1)
        sbi = lax.broadcasted_iota(jnp.int32, (SB, 1), 0)
        gnv = gn_ref[...]

        def gates_pass(n, worst):
            rows = pl.ds(pl.multiple_of(n * CH, CH), CH)
            for hh in range(NH):
                lanes = slice(hh * HD, (hh + 1) * HD)
                q, _, _, f, _, logf = _gates(hq_ref[rows, lanes], hf_ref[rows, lanes], lb_ref[:, lanes])
                c = _chunk_cumsum(logf, rowi)
                qf_s[hh, rows, :] = q
                kf_s[hh, rows, :] = 1.0 - f
                cf_s[hh, rows, :] = c
                cum_ref[rows, lanes] = c
                c_s[hh] = c
                worst = jnp.maximum(worst, _block_decay(c_s.at[hh]))
            return worst

        def between_chunks(hh, n, rows):
            lanes = slice(hh * HD, (hh + 1) * HD)
            q = qf_s[hh, rows, :]
            k = kf_s[hh, rows, :]
            c = cf_s[hh, rows, :]
            v = hi_ref[rows, lanes].astype(F32)
            q_s[hh] = q
            k_s[hh] = k
            c_s[hh] = c
            v_s[hh] = v
            st = state_s[hh]
            st_ref[hh, n] = st.astype(BF16)
            o_s[hh] = _dot_nt(q * jnp.exp(c), st)
            last = c_s[hh, CH - 1:CH, :]
            state_s[hh] = st * jnp.exp(last) + _dot_tn(v, k * jnp.exp(last - c))

        def within_chunk_matmul(hh):
            q, k, c, v = q_s[hh], k_s[hh], c_s[hh], v_s[hh]
            a = jnp.zeros((CH, CH), F32)
            for i in range(CH // SB):
                r_i = _block_ref(c_s.at[hh], i)
                qi = q * jnp.exp(jnp.minimum(c - r_i, 0.0))
                kei = k * jnp.exp(jnp.minimum(r_i - c, DECAY_CAP))
                m_i = (rowi >= SB * i) & (rowi < SB * (i + 1)) & (coli <= rowi)
                a = a + jnp.where(m_i, _dot_nt(qi, kei), 0.0)
            o_s[hh] += _dot_nn(a, v)

        def within_chunk_exact(hh):
            q, k, c, v = q_s[hh], k_s[hh], c_s[hh], v_s[hh]
            a_off = jnp.zeros((CH, CH), F32)
            for i in range(1, CH // SB):
                r_i = _block_ref(c_s.at[hh], i)
                qi = q * jnp.exp(jnp.minimum(c - r_i, 0.0))
                kei = k * jnp.exp(jnp.minimum(r_i - c, 0.0))
                m_i = (rowi >= SB * i) & (rowi < SB * (i + 1)) & (coli < SB * i)
                a_off = a_off + jnp.where(m_i, _dot_nt(qi, kei), 0.0)
            o_s[hh] += _dot_nn(a_off, v)
            for i in range(CH // SB):
                blk = slice(SB * i, SB * (i + 1))
                qb = q_s[hh, blk, :]
                cb = c_s[hh, blk, :]
                acc = jnp.zeros((SB, HD), F32)
                for s in range(SB):
                    row = SB * i + s
                    w = jnp.exp(jnp.minimum(cb - c_s[hh, row:row + 1, :], 0.0))
                    a_col = jnp.sum(qb * k_s[hh, row:row + 1, :] * w, axis=-1, keepdims=True)
                    acc = acc + jnp.where(sbi >= s, a_col, 0.0) * v_s[hh, row:row + 1, :]
                o_s[hh, blk, :] += acc

        def norm_and_gate(hh, rows):
            lanes = slice(hh * HD, (hh + 1) * HD)
            ov = o_s[hh]
            oraw_ref[rows, lanes] = ov
            r = lax.rsqrt(jnp.mean(ov * ov, axis=-1, keepdims=True) + EPS)
            hg = hg_ref[rows, lanes].astype(F32)
            bin_ref[rows, lanes] = ((ov * r) * gnv * (hg * _sig(hg))).astype(BF16)

        def chunk_with(within_chunk):
            def chunk(n, carry):
                rows = pl.ds(pl.multiple_of(n * CH, CH), CH)
                for hh in range(NH):
                    between_chunks(hh, n, rows)
                for hh in range(NH):
                    within_chunk(hh)
                for hh in range(NH):
                    norm_and_gate(hh, rows)
                return carry
            return chunk

        worst = lax.fori_loop(0, nch, gates_pass, jnp.zeros((1, HD), F32))
        mild = jnp.max(worst) <= DECAY_CAP
        mild_ref[...] = jnp.broadcast_to(jnp.where(mild, 1.0, 0.0), (8, HD))

        @pl.when(mild)
        def _():
            lax.fori_loop(0, nch, chunk_with(within_chunk_matmul), 0, unroll=4)

        @pl.when(jnp.logical_not(mild))
        def _():
            lax.fori_loop(0, nch, chunk_with(within_chunk_exact), 0)

        bint_ref[...] = bin_ref[...].astype(F32).T.astype(BF16)

    col = lambda off: pl.BlockSpec((S, W), lambda h: (0, off // NH + h))
    head = pl.BlockSpec((S, W), lambda h: (0, h))
    outs = _call(
        body, name=name, grid=(HEADS // NH,),
        in_specs=[col(HQ0), col(HF0), col(HI0), col(HG0), pl.BlockSpec((1, W), lambda h: (0, h)),
                  pl.BlockSpec((1, HD), lambda h: (0, 0))],
        out_specs=[head, pl.BlockSpec((W, S), lambda h: (h, 0)), head,
                   pl.BlockSpec((NH, nch, HD, HD), lambda h: (h, 0, 0, 0)),
                   pl.BlockSpec((8, HD), lambda h: (h, 0)), head],
        out_shape=[jax.ShapeDtypeStruct((S, D), BF16), jax.ShapeDtypeStruct((D, S), BF16),
                   jax.ShapeDtypeStruct((S, D), F32), jax.ShapeDtypeStruct((HEADS, nch, HD, HD), BF16),
                   jax.ShapeDtypeStruct((8 * HEADS // NH, HD), F32), jax.ShapeDtypeStruct((S, D), F32)],
        scratch_shapes=[pltpu.VMEM((NH, CH, HD), F32)] * 5 + [pltpu.VMEM((NH, HD, HD), F32)]
        + [pltpu.VMEM((NH, S, HD), F32)] * 3,
        sem=("parallel",), args=(proj, proj, proj, proj, lb, gn), carry=carry)
    return outs[:6], outs[6:]


def _hgrn_bwd(dbin, proj, oraw, states, mild, cum, lb, gn, name, carry=None):
    S = proj.shape[0]
    nch = S // CH
    W = NH * HD

    def body(db_ref, hq_ref, hf_ref, hi_ref, hg_ref, or_ref, st_ref, mild_ref, cum_ref, lb_ref, gn_ref,
             dq_ref, df_ref, di_ref, dg_ref, dlb_ref, dgn_ref,
             q_s, k_s, c_s, v_s, do_s, dq_s, dk_s, dv_s, dc_s, dqd_s, dkd_s, f_s, sf_s, sq_s, dl_s, dst_s,
             dlb_s, dgn_s):
        dst_s[...] = jnp.zeros((NH, HD, HD), F32)
        dlb_s[...] = jnp.zeros((1, W), F32)
        dgn_s[...] = jnp.zeros((1, HD), F32)
        rowi = lax.broadcasted_iota(jnp.int32, (CH, 1), 0)
        rowi2 = lax.broadcasted_iota(jnp.int32, (CH, CH), 0)
        coli2 = lax.broadcasted_iota(jnp.int32, (CH, CH), 1)
        sbi = lax.broadcasted_iota(jnp.int32, (SB, 1), 0)
        gnv = gn_ref[...]
        def between_chunks(hh, n, rows):
            lanes = slice(hh * HD, (hh + 1) * HD)
            lbv = lb_ref[:, lanes]
            hq = hq_ref[rows, lanes].astype(F32)
            sq = _sig(hq)
            sf = _sig(hf_ref[rows, lanes].astype(F32))
            f = lbv + (1.0 - lbv) * sf
            q = hq * sq
            k = 1.0 - f
            f_s[hh] = f
            sf_s[hh] = sf
            sq_s[hh] = sq
            v = hi_ref[rows, lanes].astype(F32)
            c = cum_ref[rows, lanes]
            ov = or_ref[rows, lanes]
            hg = hg_ref[rows, lanes].astype(F32)
            sg = _sig(hg)
            r = lax.rsqrt(jnp.mean(ov * ov, axis=-1, keepdims=True) + EPS)
            dbv = db_ref[rows, lanes]
            d_on = dbv * (hg * sg)
            dg_ref[rows, lanes] = (dbv * ((ov * r) * gnv) * _dsilu(hg, sg)).astype(BF16)
            dgn_s[...] += jnp.sum(d_on * (ov * r), axis=0, keepdims=True)
            u = d_on * gnv
            do = r * u - ov * (r * r * r) * jnp.mean(u * ov, axis=-1, keepdims=True)
            q_s[hh] = q
            k_s[hh] = k
            c_s[hh] = c
            v_s[hh] = v
            do_s[hh] = do
            st = st_ref[hh, n].astype(F32)
            dst = dst_s[hh]
            ec = jnp.exp(c)
            last = c_s[hh, CH - 1:CH, :]
            el = jnp.exp(last - c)
            elast = jnp.exp(last)
            dq = _dot_nn(do, st) * ec
            dk = _dot_nn(v, dst) * el
            dq_s[hh] = dq
            dk_s[hh] = dk
            dv_s[hh] = _dot_nt(k * el, dst)
            dc_s[hh] = q * dq - k * dk
            dl_s[hh] = (jnp.sum(k * dk, axis=0, keepdims=True)
                        + elast * jnp.sum(st * dst, axis=0, keepdims=True))
            dst_s[hh] = dst * elast + _dot_tn(do, q * ec)

        def pairs_matmul(hh, first, cap, strict):
            q, k, c, v, do = q_s[hh], k_s[hh], c_s[hh], v_s[hh], do_s[hh]
            d_a = _dot_nt(do, v).astype(BF16).astype(F32)
            d_at = d_a.T
            at = jnp.zeros((CH, CH), F32)
            dq, dk, dcum = dq_s[hh], dk_s[hh], dc_s[hh]
            for i in range(first, CH // SB):
                r_i = _block_ref(c_s.at[hh], i)
                eq = jnp.exp(jnp.minimum(c - r_i, 0.0))
                ek = jnp.exp(jnp.minimum(r_i - c, cap))
                qi = (q * eq).astype(BF16).astype(F32)
                kei = (k * ek).astype(BF16).astype(F32)
                in_t = (rowi2 >= SB * i) & (rowi2 < SB * (i + 1))
                in_s = (coli2 >= SB * i) & (coli2 < SB * (i + 1))
                m_ts = in_t & ((coli2 < SB * i) if strict else (coli2 <= rowi2))
                m_st = in_s & ((rowi2 < SB * i) if strict else (rowi2 <= coli2))
                at = at + jnp.where(m_st, _dot_nt(kei, qi), 0.0)
                dq_i = _dot_nn(jnp.where(m_ts, d_a, 0.0), kei)
                dk_i = _dot_nn(jnp.where(m_st, d_at, 0.0), qi)
                dq = dq + dq_i * eq
                dk = dk + dk_i * ek
                dcum = dcum + (qi * dq_i - kei * dk_i)
            dq_s[hh] = dq
            dk_s[hh] = dk
            dc_s[hh] = dcum
            dv_s[hh] += _dot_nn(at, do)

        def pairs_exact(hh):
            dqd_s[hh] = jnp.zeros((CH, HD), F32)
            dkd_s[hh] = jnp.zeros((CH, HD), F32)
            for i in range(CH // SB):
                blk = slice(SB * i, SB * (i + 1))
                qb = q_s[hh, blk, :]
                cb = c_s[hh, blk, :]
                dob = do_s[hh, blk, :]
                dq_acc = jnp.zeros((SB, HD), F32)
                for s in range(SB):
                    row = SB * i + s
                    ks = k_s[hh, row:row + 1, :]
                    vs = v_s[hh, row:row + 1, :]
                    w = jnp.exp(jnp.minimum(cb - c_s[hh, row:row + 1, :], 0.0))
                    live = sbi >= s
                    a_col = jnp.where(live, jnp.sum(qb * ks * w, axis=-1, keepdims=True), 0.0)
                    da_col = jnp.where(live, jnp.sum(dob * vs, axis=-1, keepdims=True), 0.0)
                    dq_acc = dq_acc + da_col * ks * w
                    dkd_s[hh, row:row + 1, :] += jnp.sum(da_col * qb * w, axis=0, keepdims=True)
                    dv_s[hh, row:row + 1, :] += jnp.sum(a_col * dob, axis=0, keepdims=True)
                dqd_s[hh, blk, :] += dq_acc
            dq_d = dqd_s[hh]
            dk_d = dkd_s[hh]
            dq_s[hh] += dq_d
            dk_s[hh] += dk_d
            dc_s[hh] += q_s[hh] * dq_d - k_s[hh] * dk_d

        def gate_grads(hh, rows):
            lanes = slice(hh * HD, (hh + 1) * HD)
            lbv = lb_ref[:, lanes]
            hq = hq_ref[rows, lanes].astype(F32)
            f, sf, sq = f_s[hh], sf_s[hh], sq_s[hh]
            dlogf = _chunk_rev_cumsum(dc_s[hh], rowi) + dl_s[hh]
            dfv = jnp.where(f > 1e-30, dlogf / jnp.maximum(f, 1e-30), 0.0) - dk_s[hh]
            dlb_s[:, lanes] += jnp.sum(dfv * (1.0 - sf), axis=0, keepdims=True)
            df_ref[rows, lanes] = (dfv * (1.0 - lbv) * sf * (1.0 - sf)).astype(BF16)
            dq_ref[rows, lanes] = (dq_s[hh] * _dsilu(hq, sq)).astype(BF16)
            di_ref[rows, lanes] = dv_s[hh].astype(BF16)

        def chunk_with(pairs):
            def chunk(j, carry):
                n = nch - 1 - j
                rows = pl.ds(pl.multiple_of(n * CH, CH), CH)
                for hh in range(NH):
                    between_chunks(hh, n, rows)
                for hh in range(NH):
                    pairs(hh)
                for hh in range(NH):
                    gate_grads(hh, rows)
                return carry
            return chunk

        def pairs_mild(hh):
            pairs_matmul(hh, 0, DECAY_CAP, strict=False)

        def pairs_any(hh):
            pairs_matmul(hh, 1, 0.0, strict=True)
            pairs_exact(hh)

        mild = jnp.max(mild_ref[...]) > 0.5

        @pl.when(mild)
        def _():
            lax.fori_loop(0, nch, chunk_with(pairs_mild), 0, unroll=2)

        @pl.when(jnp.logical_not(mild))
        def _():
            lax.fori_loop(0, nch, chunk_with(pairs_any), 0)

        dlb_ref[...] = dlb_s[...]
        dgn_ref[...] = jnp.broadcast_to(dgn_s[...], (8, HD))

    col = lambda off: pl.BlockSpec((S, W), lambda h: (0, off // NH + h))
    head = pl.BlockSpec((S, W), lambda h: (0, h))
    vec = pl.BlockSpec((1, W), lambda h: (0, h))
    outs = _call(
        body, name=name, grid=(HEADS // NH,),
        in_specs=[head, col(HQ0), col(HF0), col(HI0), col(HG0), head,
                  pl.BlockSpec((NH, nch, HD, HD), lambda h: (h, 0, 0, 0)),
                  pl.BlockSpec((8, HD), lambda h: (h, 0)), head, vec, pl.BlockSpec((1, HD), lambda h: (0, 0))],
        out_specs=[head, head, head, head, vec, pl.BlockSpec((8, HD), lambda h: (h, 0))],
        out_shape=[jax.ShapeDtypeStruct((S, D), BF16)] * 4
        + [jax.ShapeDtypeStruct((1, D), F32), jax.ShapeDtypeStruct((8 * HEADS // NH, HD), F32)],
        scratch_shapes=[pltpu.VMEM((NH, CH, HD), F32)] * 14
        + [pltpu.VMEM((NH, 1, HD), F32), pltpu.VMEM((NH, HD, HD), F32), pltpu.VMEM((1, W), F32),
           pltpu.VMEM((1, HD), F32)],
        sem=("parallel",), args=(dbin, proj, proj, proj, proj, oraw, states, mild, cum, lb, gn), carry=carry)
    dq, df, di, dg, dlb, dgn = outs[:6]
    return (dq, df, di, dg, dlb, dgn.reshape(HEADS // NH, 8, HD)[:, 0, :]), outs[6:]


def _lower_bounds(l0, l1):
    m = jnp.maximum(l0, l1)
    e0 = jnp.exp(l0 - m)
    e1 = jnp.exp(l1 - m)
    tot = e0 + e1
    p0 = e0 / tot
    p1 = e1 / tot
    return jnp.clip(p0 - p0, 0.0, 1.0), jnp.clip((p0 + p1) - p0, 0.0, 1.0)


def _lb_fwd(logits):
    def body(l_ref, o_ref):
        lb0, lb1 = _lower_bounds(l_ref[0:1, :], l_ref[1:2, :])
        o_ref[0:1, :] = lb0
        o_ref[1:2, :] = lb1

    return pl.pallas_call(body, name="lb_fwd", out_shape=jax.ShapeDtypeStruct((2, D), F32))(logits)


def _lb_bwd(logits, dlb):
    def body(l_ref, d_ref, o_ref):
        _, vjp = jax.vjp(_lower_bounds, l_ref[0:1, :], l_ref[1:2, :])
        g0, g1 = vjp((d_ref[0:1, :], d_ref[1:2, :]))
        o_ref[0:1, :] = g0
        o_ref[1:2, :] = g1

    return pl.pallas_call(body, name="lb_bwd", out_shape=jax.ShapeDtypeStruct((2, D), F32))(logits, dlb)


ADA_PAD = 128


def _ada_fwd(c_pad, w_ada, b_sh):
    ns = w_ada.shape[2]

    def body(c_ref, w_ref, b_ref, o_ref):
        cv = c_ref[...]
        ca = (cv * _sig(cv)).astype(BF16)
        for l in range(2):
            res = jnp.dot(ca, w_ref[l].astype(BF16), preferred_element_type=F32)
            o_ref[:, l * ns:(l + 1) * ns] = res[0:NDEV, :] + b_ref[l:l + 1, :]

    return pl.pallas_call(body, name="ada_fwd", out_shape=jax.ShapeDtypeStruct((NDEV, 2 * ns), F32),
                          compiler_params=_cp())(c_pad, w_ada, b_sh)


def _ada_wgrad(c_pad_t, d_ada_sh):
    ns = d_ada_sh.shape[2]

    def body(c_ref, d_ref, o_ref):
        cv = c_ref[...]
        ca = (cv * _sig(cv)).astype(BF16)
        for l in range(2):
            o_ref[l] = jnp.dot(ca, d_ref[l].astype(BF16), preferred_element_type=F32)

    return pl.pallas_call(body, name="ada_wgrad", out_shape=jax.ShapeDtypeStruct((2, D, ns), F32),
                          compiler_params=_cp())(c_pad_t, d_ada_sh)


def _sum_devices(g):
    _, R, C = g.shape

    def body(g_ref, o_ref):
        acc = g_ref[0]
        for d in range(1, NDEV):
            acc = acc + g_ref[d]
        o_ref[...] = acc

    return pl.pallas_call(body, name="sum_devices", out_shape=jax.ShapeDtypeStruct((R, C), F32),
                          compiler_params=_cp())(g)


def _adamw(w, g, m, v, name, carry=None):
    R, C = w.shape
    tr = _row_tile(R, max(8, (1 << 19) // C))

    def body(w_ref, g_ref, m_ref, v_ref, d_ref, nm_ref, nv_ref):
        d_ref[...], nm_ref[...], nv_ref[...] = _adamw_update(w_ref[...], g_ref[...], m_ref[...], v_ref[...])

    tile = pl.BlockSpec((tr, C), lambda i: (i, 0))
    return _call(body, name=name, grid=(R // tr,), in_specs=[tile] * 4, out_specs=[tile] * 3,
                 out_shape=[jax.ShapeDtypeStruct((R, C), F32)] * 3, sem=("parallel",), args=(w, g, m, v),
                 carry=carry)


def _adamw_update(w, g, m, v):
    nm = B1 * m + (1.0 - B1) * g
    nv = B2 * v + (1.0 - B2) * (g * g)
    m_hat = nm / (1.0 - B1 ** STEP)
    v_hat = nv / (1.0 - B2 ** STEP)
    return -LR * (m_hat / (jnp.sqrt(v_hat) + AEPS) + WD * w), nm, nv


SMALL_PARTS = (("b_ada", 0, 6, D), ("g_pre", 8, 2, D), ("g_post", 16, 2, D), ("lb_logits", 24, 2, D),
               ("pool_w", 32, 128, D), ("pool_scale", 160, 1, D), ("hgrn_norm_g", 168, 1, 2 * HD))


def _adamw_small(g_small, g_lb_logits, wmv):
    n = len(SMALL_PARTS)

    def body(g_ref, glb_ref, *refs):
        ins, outs = refs[:3 * n], refs[3 * n:]
        for p, (key, row0, rows, width) in enumerate(SMALL_PARTS):
            gv = glb_ref[...] if key == "lb_logits" else g_ref[row0:row0 + rows, 0:width]
            res = _adamw_update(ins[3 * p][...], gv, ins[3 * p + 1][...], ins[3 * p + 2][...])
            for t in range(3):
                outs[3 * p + t][...] = res[t]

    flat = [t for triple in wmv for t in triple]
    outs = pl.pallas_call(body, name="adamw_small",
                          out_shape=[jax.ShapeDtypeStruct(t.shape, F32) for t in flat],
                          compiler_params=_cp())(g_small, g_lb_logits, *flat)
    return [outs[3 * p:3 * p + 3] for p in range(n)]


def _cast_to_slot(place, w, l, name):
    _, R, C = w.shape
    tr = _row_tile(R, max(8, (1 << 19) // C))

    def body(p_ref, w_ref, o_ref):
        o_ref[...] = w_ref[...].astype(BF16)

    return pl.pallas_call(
        body, name=name, out_shape=jax.ShapeDtypeStruct((NCHIP, R, C), BF16),
        grid_spec=pltpu.PrefetchScalarGridSpec(
            num_scalar_prefetch=1, grid=(R // tr,),
            in_specs=[pl.BlockSpec((None, tr, C), lambda i, p_ref: (l, i, 0))],
            out_specs=pl.BlockSpec((None, tr, C), lambda i, p_ref: (p_ref[0], i, 0))),
        compiler_params=_cp(("parallel",)),
    )(place, w)


def _pair_add(core, g, got, name):
    _, R, C = g.shape
    r2 = R // 2
    tr = _row_tile(r2, max(8, (1 << 19) // C))
    nt = r2 // tr

    def body(c_ref, a_ref, b_ref, o_ref):
        o_ref[...] = (a_ref[...].astype(F32) + b_ref[...].astype(F32)).astype(o_ref.dtype)

    return pl.pallas_call(
        body, name=name, out_shape=jax.ShapeDtypeStruct((NCHIP, r2, C), BF16),
        grid_spec=pltpu.PrefetchScalarGridSpec(
            num_scalar_prefetch=1, grid=(NCHIP, nt),
            in_specs=[pl.BlockSpec((None, tr, C), lambda j, i, c_ref: (j, c_ref[0] * nt + i, 0)),
                      pl.BlockSpec((None, tr, C), lambda j, i, c_ref: (j, i, 0))],
            out_specs=pl.BlockSpec((None, tr, C), lambda j, i, c_ref: (j, i, 0))),
        compiler_params=_cp(("parallel", "parallel")),
    )(core, g, got)


def _chip_sum(place, part, recv, layer, both, name):
    _, r2, C = part.shape
    tr = _row_tile(r2, max(8, (1 << 18) // C))
    nt = r2 // tr

    def body(p_ref, own_ref, r_ref, *rest):
        o_ref = rest[-1]
        me = p_ref[0]
        own = own_ref[...].astype(F32)
        acc = None
        for j in range(NCHIP):
            slot = jnp.minimum(jnp.where(j > me, j - 1, j), NCHIP - 2)
            term = jnp.where(me == j, own, r_ref[slot].astype(F32))
            acc = term if acc is None else acc + term
        o_ref[...] = acc

    args = (place, part, recv) if both is None else (place, part, recv, both)
    return pl.pallas_call(
        body, name=name, out_shape=jax.ShapeDtypeStruct((2, 2 * r2, C), F32),
        grid_spec=pltpu.PrefetchScalarGridSpec(
            num_scalar_prefetch=1, grid=(nt,),
            in_specs=[pl.BlockSpec((None, tr, C), lambda i, p_ref: (p_ref[0], i, 0)),
                      pl.BlockSpec((NCHIP - 1, tr, C), lambda i, p_ref: (0, i, 0))] + [ANY] * (len(args) - 3),
            out_specs=pl.BlockSpec((None, tr, C), lambda i, p_ref: (layer, p_ref[1] * nt + i, 0))),
        input_output_aliases={} if both is None else {3: 0},
        compiler_params=_cp(("parallel",)),
    )(*args)


def _place():
    x, y, c = lax.axis_index("x"), lax.axis_index("y"), lax.axis_index("c")
    chips = [(1 - x, y), (x, 1 - y), (1 - x, 1 - y)]
    return x, y, c, chips


def _gather_small(blk, name):
    m_per, n = blk.shape

    def body(x_ref, out_ref, send_sems, recv_sems, local_sem):
        x, y, c, chips = _place()
        me, sibling = (x, y, c), (x, y, 1 - c)

        def rows(px, py, pc):
            return out_ref.at[pl.ds((4 * px + 2 * py + pc) * m_per, m_per), :]

        def copy(k, block, to, src=None):
            return pltpu.make_async_remote_copy(
                src_ref=rows(*block) if src is None else src, dst_ref=rows(*block),
                send_sem=send_sems.at[k], recv_sem=recv_sems.at[k], device_id=to, device_id_type=MESH)

        mine = pltpu.make_async_copy(x_ref, rows(*me), local_sem)
        mine.start()
        first = [copy(0, me, sibling, src=x_ref)]
        first += [copy(1 + j, me, (*chip, c), src=x_ref) for j, chip in enumerate(chips)]
        for cp in first:
            cp.start()
        passed = [copy(4 + j, (*chip, c), sibling) for j, chip in enumerate(chips)]
        for j, chip in enumerate(chips):
            copy(1 + j, (*chip, c), me).wait_recv()
            passed[j].start()
        copy(0, sibling, me).wait_recv()
        for j, chip in enumerate(chips):
            copy(4 + j, (*chip, 1 - c), me).wait_recv()
        for cp in first + passed:
            cp.wait_send()
        mine.wait()

    return pl.pallas_call(
        body, name=name, out_shape=jax.ShapeDtypeStruct((NDEV * m_per, n), blk.dtype),
        in_specs=[pl.BlockSpec(memory_space=pltpu.VMEM)], out_specs=pl.BlockSpec(memory_space=pltpu.VMEM),
        scratch_shapes=[pltpu.SemaphoreType.DMA((7,)), pltpu.SemaphoreType.DMA((7,)), pltpu.SemaphoreType.DMA],
        compiler_params=_cp(),
    )(blk)


def _gather_rows_carry(blk):
    m_per, n = blk.shape

    def rows(ref, px, py, pc):
        return ref.at[pl.ds((4 * px + 2 * py + pc) * m_per, m_per), :]

    def copy(ins, outs, send_sems, recv_sems, k, block, to, own=False):
        return pltpu.make_async_remote_copy(
            src_ref=ins[0] if own else rows(outs[0], *block), dst_ref=rows(outs[0], *block),
            send_sem=send_sems.at[k], recv_sem=recv_sems.at[k], device_id=to, device_id_type=MESH)

    def mine(ins, outs, send_sems):
        x, y, c, _ = _place()
        return pltpu.make_async_copy(ins[0], rows(outs[0], x, y, c), send_sems.at[7])

    def start(ins, outs, send_sems, recv_sems):
        x, y, c, chips = _place()
        mine(ins, outs, send_sems).start()
        copy(ins, outs, send_sems, recv_sems, 0, (x, y, c), (x, y, 1 - c), own=True).start()
        for j, chip in enumerate(chips):
            copy(ins, outs, send_sems, recv_sems, 1 + j, (x, y, c), (*chip, c), own=True).start()

    def finish(ins, outs, send_sems, recv_sems):
        x, y, c, chips = _place()
        for j, chip in enumerate(chips):
            copy(ins, outs, send_sems, recv_sems, 1 + j, (*chip, c), (x, y, c)).wait_recv()
            copy(ins, outs, send_sems, recv_sems, 4 + j, (*chip, c), (x, y, 1 - c)).start()
        copy(ins, outs, send_sems, recv_sems, 0, (x, y, 1 - c), (x, y, c)).wait_recv()
        for j, chip in enumerate(chips):
            copy(ins, outs, send_sems, recv_sems, 4 + j, (*chip, 1 - c), (x, y, c)).wait_recv()
        copy(ins, outs, send_sems, recv_sems, 0, (x, y, c), (x, y, 1 - c), own=True).wait_send()
        for j, chip in enumerate(chips):
            copy(ins, outs, send_sems, recv_sems, 1 + j, (x, y, c), (*chip, c), own=True).wait_send()
            copy(ins, outs, send_sems, recv_sems, 4 + j, (*chip, c), (x, y, 1 - c)).wait_send()
        mine(ins, outs, send_sems).wait()

    return _Carry([blk], [jax.ShapeDtypeStruct((NDEV * m_per, n), blk.dtype)], {}, 8, start, finish)


def _gather_carry(shards, piece=(0, 1, 1)):
    n = len(shards)
    first, count, of = piece

    def rows(ref, half):
        r2 = ref.shape[1] // 2
        return pl.ds(half * r2 + first * (r2 // of), count * (r2 // of))

    def over_ici(outs, send_sems, recv_sems, a, j, chip_xy, slot):
        x, y, c, _ = _place()
        blk = outs[a].at[slot, rows(outs[a], c), :]
        return pltpu.make_async_remote_copy(
            src_ref=blk, dst_ref=blk, send_sem=send_sems.at[6 * a + j], recv_sem=recv_sems.at[6 * a + j],
            device_id=(*chip_xy, c), device_id_type=MESH)

    def over_d2d(outs, send_sems, recv_sems, a, j, slot, half):
        x, y, c, _ = _place()
        blk = outs[a].at[slot, rows(outs[a], half), :]
        return pltpu.make_async_remote_copy(
            src_ref=blk, dst_ref=blk, send_sem=send_sems.at[6 * a + 3 + j], recv_sem=recv_sems.at[6 * a + 3 + j],
            device_id=(x, y, 1 - c), device_id_type=MESH)

    def start(ins, outs, send_sems, recv_sems):
        x, y, c, chips = _place()
        for a in range(n):
            for j, chip_xy in enumerate(chips):
                over_ici(outs, send_sems, recv_sems, a, j, chip_xy, 2 * x + y).start()

    def finish(ins, outs, send_sems, recv_sems):
        x, y, c, chips = _place()
        for a in range(n):
            for j, (cx, cy) in enumerate(chips):
                over_ici(outs, send_sems, recv_sems, a, j, (cx, cy), 2 * cx + cy).wait_recv()
                over_d2d(outs, send_sems, recv_sems, a, j, 2 * cx + cy, c).start()
        for a in range(n):
            for j, (cx, cy) in enumerate(chips):
                over_d2d(outs, send_sems, recv_sems, a, j, 2 * cx + cy, 1 - c).wait_recv()
        for a in range(n):
            for j, (cx, cy) in enumerate(chips):
                over_ici(outs, send_sems, recv_sems, a, j, (cx, cy), 2 * x + y).wait_send()
                over_d2d(outs, send_sems, recv_sems, a, j, 2 * cx + cy, c).wait_send()

    return _Carry(shards, [jax.ShapeDtypeStruct(s.shape, s.dtype) for s in shards],
                  {a: a for a in range(n)}, 6 * n, start, finish)


def _rs_pair(grads, name):
    n = len(grads)

    def body(*refs):
        ins, gots = refs[:n], refs[n:2 * n]
        send_sems, recv_sems = refs[2 * n:]
        x, y, c, _ = _place()
        cps = []
        for a in range(n):
            r2 = ins[a].shape[1] // 2
            cp = pltpu.make_async_remote_copy(
                src_ref=ins[a].at[:, pl.ds((1 - c) * r2, r2), :], dst_ref=gots[a],
                send_sem=send_sems.at[a], recv_sem=recv_sems.at[a],
                device_id=(x, y, 1 - c), device_id_type=MESH)
            cp.start()
            cps.append(cp)
        for cp in cps:
            cp.wait()

    half = [jax.ShapeDtypeStruct((NCHIP, g.shape[1] // 2, g.shape[2]), g.dtype) for g in grads]
    return pl.pallas_call(
        body, name=name, out_shape=half, in_specs=[ANY] * n, out_specs=[ANY] * n,
        scratch_shapes=[pltpu.SemaphoreType.DMA((n,)), pltpu.SemaphoreType.DMA((n,))],
        compiler_params=_cp(),
    )(*grads)


def _chips_carry(parts):
    n = len(parts)

    def send(ins, outs, send_sems, recv_sems, a, j, chip_xy):
        x, y, c, _ = _place()
        me, them = 2 * x + y, 2 * chip_xy[0] + chip_xy[1]
        return pltpu.make_async_remote_copy(
            src_ref=ins[a].at[them], dst_ref=outs[a].at[me - (me > them).astype(jnp.int32)],
            send_sem=send_sems.at[3 * a + j], recv_sem=recv_sems.at[3 * a + j],
            device_id=(*chip_xy, c), device_id_type=MESH)

    def start(ins, outs, send_sems, recv_sems):
        _, _, _, chips = _place()
        for a in range(n):
            for j, chip_xy in enumerate(chips):
                send(ins, outs, send_sems, recv_sems, a, j, chip_xy).start()

    def finish(ins, outs, send_sems, recv_sems):
        x, y, c, chips = _place()
        me = 2 * x + y
        for a in range(n):
            for j, (cx, cy) in enumerate(chips):
                them = 2 * cx + cy
                blk = outs[a].at[them - (them > me).astype(jnp.int32)]
                pltpu.make_async_remote_copy(
                    src_ref=blk, dst_ref=blk, send_sem=send_sems.at[3 * a + j], recv_sem=recv_sems.at[3 * a + j],
                    device_id=(cx, cy, c), device_id_type=MESH).wait_recv()
        for a in range(n):
            for j, chip_xy in enumerate(chips):
                send(ins, outs, send_sems, recv_sems, a, j, chip_xy).wait_send()

    return _Carry(parts, [jax.ShapeDtypeStruct((NCHIP - 1,) + p.shape[1:], p.dtype) for p in parts], {},
                  3 * n, start, finish)


def _rs_swap(fulls):
    n = len(fulls)

    def body(*refs):
        outs = refs[n:2 * n]
        send_sems, recv_sems = refs[2 * n:]
        x, y, c, _ = _place()
        cps = []
        for a in range(n):
            r2 = outs[a].shape[1] // 2
            mine = outs[a].at[:, pl.ds(c * r2, r2), :]
            cp = pltpu.make_async_remote_copy(
                src_ref=mine, dst_ref=mine, send_sem=send_sems.at[a], recv_sem=recv_sems.at[a],
                device_id=(x, y, 1 - c), device_id_type=MESH)
            cp.start()
            cps.append(cp)
        for a in range(n):
            r2 = outs[a].shape[1] // 2
            blk = outs[a].at[:, pl.ds((1 - c) * r2, r2), :]
            pltpu.make_async_remote_copy(
                src_ref=blk, dst_ref=blk, send_sem=send_sems.at[a], recv_sem=recv_sems.at[a],
                device_id=(x, y, 1 - c), device_id_type=MESH).wait_recv()
        for cp in cps:
            cp.wait_send()

    return pl.pallas_call(
        body, name="rs_swap", out_shape=[jax.ShapeDtypeStruct(f.shape, f.dtype) for f in fulls],
        in_specs=[ANY] * n, out_specs=[ANY] * n, input_output_aliases={a: a for a in range(n)},
        scratch_shapes=[pltpu.SemaphoreType.DMA((n,)), pltpu.SemaphoreType.DMA((n,))],
        compiler_params=_cp(),
    )(*fulls)


def _mm_ride(a, b, carry, **kw):
    if carry is None:
        return _mm(a, b, **kw), []
    return _mm(a, b, carry=carry, **kw)


def _layer_fwd(l, x, ada, w, small, ride, target=None):
    shift, scale, gate = ada[:, 0:D], ada[:, D:2 * D], ada[:, 2 * D:3 * D]
    h, h_t = _prenorm_fwd(x, small["g_pre"][l], scale, shift, f"prenorm_fwd{l}")
    carry, landed = ride("proj")
    proj, outs = _mm_ride(h, w["w_in"][l], carry, name=f"proj{l}", b_mode="nn_sh", tm=2048, out_dtype=BF16)
    landed(outs)
    a_in, a_in_t = _pool_fwd(proj, small["pool_w"][l], small["pool_scale"][l], f"pool_fwd{l}")
    carry, landed = ride("hgrn")
    (b_in, b_in_t, o_raw, states, mild, cum), outs = _hgrn_fwd(proj, small["lb"][l], small["hgrn_norm_g"][l],
                                                              f"hgrn_fwd{l}", carry=carry)
    landed(outs)
    carry, landed = ride("tail")
    (br_a, br_b, merged_t, y, *x_new), outs = _layer_tail_fwd(
        proj, a_in, b_in, x, w["w_pool_o"][l], w["w_hgrn_o"][l].reshape(D, D), w["w_out"][l].reshape(D, D),
        gate, small["g_post"][l], f"tail_fwd{l}", target=target, carry=carry)
    landed(outs)
    saved = dict(x=x, h_t=h_t, proj=proj, a_in_t=a_in_t, b_in_t=b_in_t, o_raw=o_raw, states=states, mild=mild,
                 cum=cum,
                 br_a=br_a, br_b=br_b, merged_t=merged_t, y=y, scale=scale, gate=gate)
    return x_new, saved


def _layer_bwd(l, dxn, sv, w, small, ride):
    dy, dbr_a, dbr_b, dmg, da_in, db_in, dgate, dg_post = _layer_head_bwd(
        dxn, sv["y"], sv["proj"], sv["br_a"], sv["br_b"], w["w_pool_o"][l], w["w_hgrn_o"][l].reshape(D, D),
        w["w_out"][l].reshape(D, D), sv["gate"], small["g_post"][l], f"head_bwd{l}")
    gw_out = _mm(sv["merged_t"], dy, name=f"gw_out{l}", out_dtype=BF16)
    gw_pool_o = _mm(sv["a_in_t"], dbr_a, name=f"gw_pool_o{l}", out_shards=NCHIP, out_dtype=BF16)
    gw_hgrn_o = _mm(sv["b_in_t"], dbr_b, name=f"gw_hgrn_o{l}", out_dtype=BF16)
    big = dict(w_pool_o=gw_pool_o, w_hgrn_o=gw_hgrn_o.reshape(NCHIP, D // NCHIP, D),
               w_out=gw_out.reshape(NCHIP, D // NCHIP, D))
    carry, landed = ride["hgrn"](big)
    (dhq, dhf, dhi, dhg, dlb, dgn), outs = _hgrn_bwd(db_in, sv["proj"], sv["o_raw"], sv["states"], sv["mild"],
                                                     sv["cum"], small["lb"][l], small["hgrn_norm_g"][l],
                                                     f"hgrn_bwd{l}", carry=carry)
    landed(outs)
    dpv, dpg, dpw, dpsc = _pool_bwd(da_in, sv["proj"], small["pool_w"][l], small["pool_scale"][l],
                                    f"pool_bwd{l}")
    dproj = jnp.concatenate([dpv, dpg, dhq, dhf, dhi, dhg, dmg], axis=1)
    little = dict(dgate=dgate, g_post=dg_post, pool_w=dpw, pool_scale=dpsc, lb=dlb,
                  hgrn_norm_g=jnp.sum(dgn, axis=0, keepdims=True))
    carry, landed = ride["gw_in"](little)
    big["w_in"], outs = _mm_ride(sv["h_t"], dproj, carry, name=f"gw_in{l}", out_shards=NCHIP, out_dtype=BF16)
    landed(outs)
    carry, landed = ride["d_h"](big)
    dh, outs = _mm_ride(dproj, w["w_in"][l], carry, name=f"d_h{l}", b_mode="nt_shk", tn=1024)
    landed(outs)
    dx, dshift, dscale, dg_pre = _prenorm_bwd(dh, dxn, sv["x"], small["g_pre"][l], sv["scale"],
                                              f"prenorm_bwd{l}")
    little.update(dshift=dshift, dscale=dscale, g_pre=dg_pre)
    return dx, big, little


SMALL_ROWS = 176


def _rows8(t):
    t = t.reshape(-1, D)
    return jnp.pad(t, ((0, -t.shape[0] % 8), (0, 0)))


def _pack_small_weights(b_ada, g_pre, g_post, lb_logits, pool_w, pool_scale, hgrn_norm_g):
    gn = jnp.pad(hgrn_norm_g.reshape(1, 2 * HD), ((0, 0), (0, D - 2 * HD)))
    return jnp.concatenate([_rows8(b_ada), _rows8(g_pre), _rows8(g_post), _rows8(lb_logits), _rows8(pool_w),
                            _rows8(pool_scale), _rows8(gn)], axis=0)


def _pack_small(parts):
    both = lambda key: jnp.stack([parts[l][key] for l in range(2)])
    d_ada = jnp.stack([jnp.concatenate([p["dshift"], p["dscale"], p["dgate"]], axis=1) for p in parts])
    return _pack_small_weights(d_ada, both("g_pre"), both("g_post"), both("lb"), both("pool_w"),
                               both("pool_scale"), both("hgrn_norm_g"))


def _unpack_small(p):
    return (p[0:6].reshape(2, 3 * D), p[8:10], p[16:18], p[24:26], p[32:160].reshape(2, GROUPS, 128, 128),
            p[160:161].reshape(2, POOL_W), p[168:169, 0:2 * HD].reshape(2, HD))


def kernel(x, c, w_ada, b_ada, g_pre, g_post, w_in, pool_w, pool_scale, lb_logits, hgrn_norm_g, w_pool_o, w_hgrn_o, w_out, loss_target, m_w_ada, m_b_ada, m_g_pre, m_g_post, m_w_in, m_pool_w, m_pool_scale, m_lb_logits, m_hgrn_norm_g, m_w_pool_o, m_w_hgrn_o, m_w_out, v_w_ada, v_b_ada, v_g_pre, v_g_post, v_w_in, v_pool_w, v_pool_scale, v_lb_logits, v_hgrn_norm_g, v_w_pool_o, v_w_hgrn_o, v_w_out):
    ax, ay, ac = lax.axis_index("x"), lax.axis_index("y"), lax.axis_index("c")
    chip = 2 * ax + ay
    dev = 2 * chip + ac
    xe, te = x[0], loss_target[0]
    ada_s = w_ada.shape[2]

    big_names = ("w_in", "w_pool_o", "w_hgrn_o", "w_out")
    big_w = (w_in, w_pool_o, w_hgrn_o, w_out)
    core = jnp.stack([ac]).astype(jnp.int32)
    place = jnp.stack([chip, ac]).astype(jnp.int32)
    slots = {(k, l): _cast_to_slot(place, t, l, f"cast_{k}{l}") for l in range(2) for k, t in zip(big_names, big_w)}
    w = {k: [None, None] for k in big_names}
    (w["w_in"][0],) = _run_carry(_gather_carry([slots["w_in", 0]]), "gather_w_in0")
    def fills(keys):
        def landed(outs):
            for (k, l), o in zip(keys, outs):
                w[k][l] = slots[k, l] = o
        return landed

    rest0 = [(k, 0) for k in big_names[1:]]
    rest1 = [(k, 1) for k in big_names[1:]]
    no_carry = (None, lambda outs: None)

    def ride_fwd0(stage):
        if stage == "proj":
            return (_join_carries(_gather_carry([slots[t] for t in rest0]),
                                  _gather_carry([slots["w_in", 1]], piece=(0, 1, 4))),
                    fills(rest0 + [("w_in", 1)]))
        if stage == "hgrn":
            return _gather_carry([slots["w_in", 1]], piece=(1, 3, 4)), fills([("w_in", 1)])
        return _gather_carry([slots[t] for t in rest1]), fills(rest1)

    c_all = _gather_small(jnp.broadcast_to(c, (8, D)), "gather_c").reshape(NDEV, 8, D)[:, 0, :]
    c_pad = jnp.pad(c_all, ((0, ADA_PAD - NDEV), (0, 0)))
    b_sh = lax.dynamic_slice(b_ada, (0, chip * ada_s), (2, ada_s))
    ada_cols = _gather_small(_ada_fwd(c_pad, w_ada, b_sh), "gather_ada")
    ada_cols = ada_cols.reshape(NCHIP, 2, NDEV, 2, ada_s)[:, 0]
    ada_all = jnp.transpose(ada_cols, (2, 1, 0, 3)).reshape(2, NDEV, 3 * D)
    ada_me = lax.dynamic_slice(ada_all, (0, dev, 0), (2, 1, 3 * D))

    lbs = _lb_fwd(lb_logits)
    small = dict(g_pre=g_pre[:, None, :], g_post=g_post[:, None, :], pool_w=pool_w,
                 pool_scale=pool_scale[:, None, :], lb=lbs[:, None, :], hgrn_norm_g=hgrn_norm_g[:, None, :])

    (x1,), sv0 = _layer_fwd(0, xe, ada_me[0], w, small, ride_fwd0)
    (dx2, loss_blk), sv1 = _layer_fwd(1, x1, ada_me[1], w, small, lambda stage: no_carry, target=te)

    parts, recv = {}, {}

    def pair_sums(keys, grads, tag):
        got = _rs_pair(grads, f"rs_pair_{tag}")
        for kl, g, o in zip(keys, grads, got):
            parts[kl] = _pair_add(core, g, o, f"rs_add_{kl[0]}{kl[1]}")

    def exchange(keys):
        def landed(outs):
            recv.update(zip(keys, outs))
        return _chips_carry([parts[kl] for kl in keys]), landed

    def early(l):
        return [(k, l) for k in big_names[1:]]

    def ride_hgrn1(big):
        pair_sums(early(1), [big[k] for k in big_names[1:]], "l1_early")
        return exchange(early(1))

    def ride_d_h1(big):
        pair_sums([("w_in", 1)], [big["w_in"]], "l1_w_in")
        return no_carry

    def ride_hgrn0(big):
        pair_sums(early(0), [big[k] for k in big_names[1:]], "l0_early")
        return exchange([("w_in", 1)] + early(0))

    def ride_d_h0(big):
        pair_sums([("w_in", 0)], [big["w_in"]], "l0_w_in")
        return exchange([("w_in", 0)])

    dx1, big1, little1 = _layer_bwd(1, dx2, sv1, w, small,
                                    dict(hgrn=ride_hgrn1, gw_in=lambda little: no_carry, d_h=ride_d_h1))

    gathered = {}
    zero_row = jnp.zeros((1, D), F32)

    def ride_gw_in0(little):
        so_far = dict(little, dshift=zero_row, dscale=zero_row, g_pre=zero_row)

        def landed(outs):
            (gathered["early"],) = outs
        return _gather_rows_carry(_pack_small([so_far, little1])), landed

    dx0, big0, little0 = _layer_bwd(0, dx1, sv0, w, small,
                                    dict(hgrn=ride_hgrn0, gw_in=ride_gw_in0, d_h=ride_d_h0))
    loss = lax.psum(loss_blk[0, 0], ("x", "y", "c"))
    late = _rows8(jnp.stack([little0["dshift"], little0["dscale"], little0["g_pre"]]))
    late = _gather_small(late, "gather_small_late").reshape(NDEV, 8, D)
    packed = gathered["early"].reshape(NDEV, SMALL_ROWS, D)
    packed = packed.at[:, 0:2, :].set(late[:, 0:2, :]).at[:, 8:9, :].set(late[:, 2:3, :])
    red = []
    for k in big_names:
        both = _chip_sum(place, parts[k, 1], recv[k, 1], 1, None, f"rs_sum_{k}1")
        red.append(_chip_sum(place, parts[k, 0], recv[k, 0], 0, both, f"rs_sum_{k}0"))
    g_big = dict(zip(big_names, _rs_swap(red)))

    def upd(wt, g, m, v, name, carry=None):
        shp = wt.shape
        two = lambda t: t.reshape(-1, shp[-1])
        res = _adamw(two(wt), two(g), two(m), two(v), name, carry)
        return [t.reshape(shp) for t in res[:3]], res[3:]

    u_w_in, _ = upd(w_in, g_big["w_in"], m_w_in, v_w_in, "adamw_w_in")
    g_small = _sum_devices(packed)
    g_b_ada, g_g_pre, g_g_post, g_lb, g_pool_w, g_pool_scale, g_norm_g = _unpack_small(g_small)
    g_lb_logits = _lb_bwd(lb_logits, g_lb)
    d_ada_all = packed[:, 0:6, :].reshape(NDEV, 2, 3 * D)
    d_ada_sh = lax.dynamic_slice(jnp.transpose(d_ada_all, (1, 0, 2)), (0, 0, chip * ada_s), (2, NDEV, ada_s))
    d_ada_sh = jnp.pad(d_ada_sh, ((0, 0), (0, ADA_PAD - NDEV), (0, 0)))
    g_w_ada = _ada_wgrad(c_pad.T, d_ada_sh)

    u_w_ada, _ = upd(w_ada, g_w_ada, m_w_ada, v_w_ada, "adamw_w_ada")
    u_w_pool_o, _ = upd(w_pool_o, g_big["w_pool_o"], m_w_pool_o, v_w_pool_o, "adamw_w_pool_o")
    u_w_hgrn_o, _ = upd(w_hgrn_o, g_big["w_hgrn_o"], m_w_hgrn_o, v_w_hgrn_o, "adamw_w_hgrn_o")
    u_w_out, _ = upd(w_out, g_big["w_out"], m_w_out, v_w_out, "adamw_w_out")
    small_w = dict(b_ada=(b_ada, m_b_ada, v_b_ada), g_pre=(g_pre, m_g_pre, v_g_pre),
                   g_post=(g_post, m_g_post, v_g_post), lb_logits=(lb_logits, m_lb_logits, v_lb_logits),
                   pool_w=(pool_w, m_pool_w, v_pool_w), pool_scale=(pool_scale, m_pool_scale, v_pool_scale),
                   hgrn_norm_g=(hgrn_norm_g, m_hgrn_norm_g, v_hgrn_norm_g))
    in_rows = [tuple(t.reshape(rows, width) for t in small_w[key]) for key, _, rows, width in SMALL_PARTS]
    u_rows = _adamw_small(g_small, g_lb_logits, in_rows)
    u_small = {key: [t.reshape(small_w[key][0].shape) for t in u_rows[p]]
               for p, (key, _, _, _) in enumerate(SMALL_PARTS)}

    grads_out = (g_w_ada, g_b_ada, g_g_pre, g_g_post, g_big["w_in"], g_pool_w, g_pool_scale, g_lb_logits,
                 g_norm_g, g_big["w_pool_o"], g_big["w_hgrn_o"], g_big["w_out"])

    def ordered(k):
        s = lambda key: u_small[key][k]
        return (u_w_ada[k], s("b_ada"), s("g_pre"), s("g_post"), u_w_in[k], s("pool_w"), s("pool_scale"),
                s("lb_logits"), s("hgrn_norm_g"), u_w_pool_o[k], u_w_hgrn_o[k], u_w_out[k])

    return (loss, dx0[None], *grads_out, *ordered(0), *ordered(1), *ordered(2))
```

```python
import functools

import jax
import jax.numpy as jnp
from jax import lax
from jax.experimental import pallas as pl
from jax.experimental.pallas import tpu as pltpu

F32 = jnp.float32
BF16 = jnp.bfloat16
MESH = pl.DeviceIdType.MESH

D = 1024
HEADS = 8
HD = 128
GROUPS = 4
POOL_W = 512
WINDOWS = (2, 4, 8, 16)
CH = 64
SB = 16
NH = 2
IN_W = 7168
NCHIP = 4
NDEV = 8
EPS = 1e-6
PV0, PG0, HQ0, HF0, HI0, HG0 = 0, 4, 8, 16, 24, 32
MGP_BLK, MGH_BLK = 5, 6

LR, B1, B2, AEPS, WD, STEP = 0.001, 0.9, 0.999, 1e-08, 0.01, 10
VMEM_LIMIT = 56 * 1024 * 1024


def _cp(sem=None, **kw):
    if sem is not None:
        kw["dimension_semantics"] = sem
    return pltpu.CompilerParams(vmem_limit_bytes=VMEM_LIMIT, **kw)


def _sig(z):
    return 1.0 / (1.0 + jnp.exp(-z))


def _dsilu(z, s):
    return s * (1.0 + z * (1.0 - s))


def _row_tile(rows, cap):
    if rows <= cap:
        return rows
    t = 1 << (cap.bit_length() - 1)
    while rows % t:
        t //= 2
    return t


ANY = pl.BlockSpec(memory_space=pl.ANY)


class _Carry:
    def __init__(self, ins, outs, aliases, n_sem, start, finish):
        self.ins, self.outs, self.aliases, self.n_sem = list(ins), list(outs), dict(aliases), n_sem
        self.start, self.finish = start, finish


class _SemWindow:
    def __init__(self, ref, base):
        self._ref, self._base = ref, base

    @property
    def at(self):
        return self

    def __getitem__(self, k):
        return self._ref.at[self._base + k]


def _join_carries(*carries):
    ins, outs, aliases, spans, n_sem = [], [], {}, [], 0
    for cr in carries:
        aliases.update({len(ins) + i: len(outs) + o for i, o in cr.aliases.items()})
        spans.append((len(ins), len(cr.ins), len(outs), len(cr.outs), n_sem))
        ins, outs, n_sem = ins + cr.ins, outs + cr.outs, n_sem + cr.n_sem

    def run(which):
        def fn(i_refs, o_refs, send_sems, recv_sems):
            for cr, (i0, ni, o0, no, s0) in zip(carries, spans):
                getattr(cr, which)(i_refs[i0:i0 + ni], o_refs[o0:o0 + no], _SemWindow(send_sems, s0),
                                   _SemWindow(recv_sems, s0))
        return fn

    return _Carry(ins, outs, aliases, n_sem, run("start"), run("finish"))


def _call(body, *, name, grid, in_specs, out_specs, out_shape, args, scratch_shapes=(), sem=None, carry=None):
    in_specs, out_specs, out_shape = list(in_specs), list(out_specs), list(out_shape)
    scratch_shapes = list(scratch_shapes)
    if carry is None:
        outs = pl.pallas_call(body, name=name, grid=grid, in_specs=in_specs, out_specs=out_specs,
                              out_shape=out_shape, scratch_shapes=scratch_shapes,
                              compiler_params=_cp(sem))(*args)
        return list(outs)
    n_in, n_out, n_scr = len(in_specs), len(out_specs), len(scratch_shapes)
    c_in, c_out = len(carry.ins), len(carry.outs)

    def wrapped(*refs):
        k_in, rest = refs[:n_in], refs[n_in:]
        ci, rest = rest[:c_in], rest[c_in:]
        k_out, rest = rest[:n_out], rest[n_out:]
        co, rest = rest[:c_out], rest[c_out:]
        k_scr, (ssem, rsem) = rest[:n_scr], rest[n_scr:]
        pids = [pl.program_id(d) for d in range(len(grid))]
        first = functools.reduce(jnp.logical_and, [p == 0 for p in pids])
        last = functools.reduce(jnp.logical_and, [p == g - 1 for p, g in zip(pids, grid)])

        @pl.when(first)
        def _():
            carry.start(ci, co, ssem, rsem)

        body(*k_in, *k_out, *k_scr)

        @pl.when(last)
        def _():
            carry.finish(ci, co, ssem, rsem)

    outs = pl.pallas_call(
        wrapped, name=name, grid=grid, in_specs=in_specs + [ANY] * c_in, out_specs=out_specs + [ANY] * c_out,
        out_shape=out_shape + carry.outs,
        input_output_aliases={n_in + i: n_out + o for i, o in carry.aliases.items()},
        scratch_shapes=scratch_shapes + [pltpu.SemaphoreType.DMA((carry.n_sem,))] * 2,
        compiler_params=_cp(("arbitrary",) * len(grid)),
    )(*args, *carry.ins)
    return list(outs)


def _run_carry(carry, name):
    c_in, c_out = len(carry.ins), len(carry.outs)

    def body(*refs):
        ci, co, (ssem, rsem) = refs[:c_in], refs[c_in:c_in + c_out], refs[c_in + c_out:]
        carry.start(ci, co, ssem, rsem)
        carry.finish(ci, co, ssem, rsem)

    outs = pl.pallas_call(
        body, name=name, in_specs=[ANY] * c_in, out_specs=[ANY] * c_out, out_shape=carry.outs,
        input_output_aliases=carry.aliases,
        scratch_shapes=[pltpu.SemaphoreType.DMA((carry.n_sem,))] * 2, compiler_params=_cp(),
    )(*carry.ins)
    return list(outs)


def _mm(a, b, *, name, b_mode="nn", out_shards=0, tm=1024, tn=256, tk=None, out_dtype=F32, carry=None):
    M, K = a.shape
    if b_mode == "nn":
        N = b.shape[1]
    elif b_mode == "nt":
        N = b.shape[0]
    elif b_mode == "nn_sh":
        N = b.shape[0] * b.shape[2]
    else:
        N = b.shape[1]
    tm = _row_tile(M, tm)
    if b_mode == "nn_sh":
        tn = _row_tile(b.shape[2], tn)
    elif out_shards:
        tn = _row_tile(N // out_shards, tn)
    else:
        tn = _row_tile(N, tn)
    if tk is None:
        tk = K if b_mode != "nt_shk" else b.shape[2]
    if b_mode == "nt_shk":
        tk = _row_tile(b.shape[2], tk)
    nm, nn, nk = M // tm, N // tn, K // tk

    a_spec = pl.BlockSpec((tm, tk), lambda m, n, k: (m, k))
    if b_mode == "nn":
        b_spec = pl.BlockSpec((tk, tn), lambda m, n, k: (k, n))
    elif b_mode == "nt":
        b_spec = pl.BlockSpec((tn, tk), lambda m, n, k: (n, k))
    elif b_mode == "nn_sh":
        nps = b.shape[2] // tn
        b_spec = pl.BlockSpec((None, tk, tn), lambda m, n, k: (n // nps, k, n % nps))
    else:
        kps = b.shape[2] // tk
        b_spec = pl.BlockSpec((None, tn, tk), lambda m, n, k: (k // kps, n, k % kps))
    if out_shards:
        ops = (N // out_shards) // tn
        o_spec = pl.BlockSpec((None, tm, tn), lambda m, n, k: (n // ops, m, n % ops))
        o_shape = jax.ShapeDtypeStruct((out_shards, M, N // out_shards), out_dtype)
    else:
        o_spec = pl.BlockSpec((tm, tn), lambda m, n, k: (m, n))
        o_shape = jax.ShapeDtypeStruct((M, N), out_dtype)
    trans_b = b_mode in ("nt", "nt_shk")
    dn = (((1,), (1,)), ((), ())) if trans_b else (((1,), (0,)), ((), ()))

    def body(a_ref, b_ref, o_ref, acc_ref):
        k = pl.program_id(2)

        @pl.when(k == 0)
        def _():
            acc_ref[...] = jnp.zeros(acc_ref.shape, F32)

        acc_ref[...] += lax.dot_general(a_ref[...].astype(BF16), b_ref[...].astype(BF16), dn,
                                        preferred_element_type=F32)

        @pl.when(k == nk - 1)
        def _():
            o_ref[...] = acc_ref[...].astype(o_ref.dtype)

    outs = _call(body, name=name, grid=(nm, nn, nk), in_specs=[a_spec, b_spec], out_specs=[o_spec],
                 out_shape=[o_shape], scratch_shapes=[pltpu.VMEM((tm, tn), F32)],
                 sem=("parallel", "parallel", "arbitrary"), args=(a, b), carry=carry)
    return outs[0] if carry is None else (outs[0], outs[1:])


def _rowvec(n=D):
    return pl.BlockSpec((1, n), lambda i: (0, 0))


def _prenorm_fwd(x, g, scale, shift, name):
    S = x.shape[0]
    tr = _row_tile(S, 256)

    def body(x_ref, g_ref, sc_ref, sh_ref, h_ref, ht_ref):
        xv = x_ref[...]
        r = lax.rsqrt(jnp.mean(xv * xv, axis=-1, keepdims=True) + EPS)
        hv = (xv * r) * g_ref[...] * (1.0 + sc_ref[...]) + sh_ref[...]
        h_ref[...] = hv.astype(BF16)
        ht_ref[...] = hv.T.astype(BF16)

    return pl.pallas_call(
        body, name=name, grid=(S // tr,),
        in_specs=[pl.BlockSpec((tr, D), lambda i: (i, 0)), _rowvec(), _rowvec(), _rowvec()],
        out_specs=[pl.BlockSpec((tr, D), lambda i: (i, 0)), pl.BlockSpec((D, tr), lambda i: (0, i))],
        out_shape=[jax.ShapeDtypeStruct((S, D), BF16), jax.ShapeDtypeStruct((D, S), BF16)],
        compiler_params=_cp(("parallel",)),
    )(x, g, scale, shift)


def _prenorm_bwd(dh, dxn, x, g, scale, name):
    S = x.shape[0]
    tr = _row_tile(S, 256)

    def body(dh_ref, dxn_ref, x_ref, g_ref, sc_ref, dx_ref, dsh_ref, dsc_ref, dg_ref):
        i = pl.program_id(0)

        @pl.when(i == 0)
        def _():
            dsh_ref[...] = jnp.zeros((1, D), F32)
            dsc_ref[...] = jnp.zeros((1, D), F32)
            dg_ref[...] = jnp.zeros((1, D), F32)

        xv = x_ref[...]
        dhv = dh_ref[...]
        gv = g_ref[...]
        mod = 1.0 + sc_ref[...]
        r = lax.rsqrt(jnp.mean(xv * xv, axis=-1, keepdims=True) + EPS)
        xh = xv * r
        dsh_ref[...] += jnp.sum(dhv, axis=0, keepdims=True)
        dsc_ref[...] += jnp.sum(dhv * (xh * gv), axis=0, keepdims=True)
        dg_ref[...] += jnp.sum(dhv * mod * xh, axis=0, keepdims=True)
        u = dhv * mod * gv
        dx_ref[...] = dxn_ref[...] + r * u - xv * (r * r * r) * jnp.mean(u * xv, axis=-1, keepdims=True)

    tile = pl.BlockSpec((tr, D), lambda i: (i, 0))
    return pl.pallas_call(
        body, name=name, grid=(S // tr,),
        in_specs=[tile, tile, tile, _rowvec(), _rowvec()],
        out_specs=[tile, _rowvec(), _rowvec(), _rowvec()],
        out_shape=[jax.ShapeDtypeStruct((S, D), F32)] + [jax.ShapeDtypeStruct((1, D), F32)] * 3,
        compiler_params=_cp(("arbitrary",)),
    )(dh, dxn, x, g, scale)


def _postnorm_fwd(x, y, gate, g, name):
    S = x.shape[0]
    tr = _row_tile(S, 256)

    def body(x_ref, y_ref, gate_ref, g_ref, o_ref):
        yv = y_ref[...]
        r = lax.rsqrt(jnp.mean(yv * yv, axis=-1, keepdims=True) + EPS)
        o_ref[...] = x_ref[...] + gate_ref[...] * ((yv * r) * g_ref[...])

    tile = pl.BlockSpec((tr, D), lambda i: (i, 0))
    return pl.pallas_call(
        body, name=name, grid=(S // tr,), in_specs=[tile, tile, _rowvec(), _rowvec()],
        out_specs=tile, out_shape=jax.ShapeDtypeStruct((S, D), F32), compiler_params=_cp(("parallel",)),
    )(x, y, gate, g)


def _postnorm_bwd(dxn, y, gate, g, name):
    S = y.shape[0]
    tr = _row_tile(S, 256)

    def body(dxn_ref, y_ref, gate_ref, g_ref, dy_ref, dgate_ref, dg_ref):
        i = pl.program_id(0)

        @pl.when(i == 0)
        def _():
            dgate_ref[...] = jnp.zeros((1, D), F32)
            dg_ref[...] = jnp.zeros((1, D), F32)

        yv = y_ref[...]
        dv = dxn_ref[...]
        gv = g_ref[...]
        gt = gate_ref[...]
        r = lax.rsqrt(jnp.mean(yv * yv, axis=-1, keepdims=True) + EPS)
        yh = yv * r
        dgate_ref[...] += jnp.sum(dv * (yh * gv), axis=0, keepdims=True)
        dg_ref[...] += jnp.sum(dv * gt * yh, axis=0, keepdims=True)
        u = dv * gt * gv
        dy_ref[...] = (r * u - yv * (r * r * r) * jnp.mean(u * yv, axis=-1, keepdims=True)).astype(BF16)

    tile = pl.BlockSpec((tr, D), lambda i: (i, 0))
    return pl.pallas_call(
        body, name=name, grid=(S // tr,), in_specs=[tile, tile, _rowvec(), _rowvec()],
        out_specs=[tile, _rowvec(), _rowvec()],
        out_shape=[jax.ShapeDtypeStruct((S, D), BF16), jax.ShapeDtypeStruct((1, D), F32),
                   jax.ShapeDtypeStruct((1, D), F32)],
        compiler_params=_cp(("arbitrary",)),
    )(dxn, y, gate, g)


def _loss_head(xo, target, name):
    S = xo.shape[0]
    tr = _row_tile(S, 256)

    def body(x_ref, t_ref, dx_ref, l_ref):
        i = pl.program_id(0)

        @pl.when(i == 0)
        def _():
            l_ref[...] = jnp.zeros((8, 128), F32)

        err = x_ref[...] - t_ref[...]
        dx_ref[...] = err * (1.0 / D)
        l_ref[...] += 0.5 * jnp.sum(jnp.mean(err * err, axis=-1, keepdims=True))

    tile = pl.BlockSpec((tr, D), lambda i: (i, 0))
    return pl.pallas_call(
        body, name=name, grid=(S // tr,), in_specs=[tile, tile],
        out_specs=[tile, pl.BlockSpec((8, 128), lambda i: (0, 0))],
        out_shape=[jax.ShapeDtypeStruct((S, D), F32), jax.ShapeDtypeStruct((8, 128), F32)],
        compiler_params=_cp(("arbitrary",)),
    )(xo, target)


def _layer_tail_fwd(proj, a_in, b_in, x, w_po, w_ho, w_out, gate, g, name, target=None, carry=None):
    S = proj.shape[0]
    tr = _row_tile(S, 256)
    nsh, _, wsh = w_po.shape
    n_in = 10 + (target is not None)

    def body(*refs):
        (mgp_ref, mgh_ref, a_ref, b_ref, x_ref, wpo_ref, who_ref, wout_ref, gate_ref, g_ref) = refs[:10]
        bra_ref, brb_ref, mt_ref, y_ref, xn_ref = refs[n_in:n_in + 5]
        av = a_ref[...]
        bra = jnp.concatenate([jnp.dot(av, wpo_ref[j], preferred_element_type=F32) for j in range(nsh)], axis=1)
        brb = jnp.dot(b_ref[...], who_ref[...], preferred_element_type=F32)
        mv = _sig(mgp_ref[...].astype(F32)) * bra + _sig(mgh_ref[...].astype(F32)) * brb
        bra_ref[...] = bra.astype(BF16)
        brb_ref[...] = brb.astype(BF16)
        mt_ref[...] = mv.T.astype(BF16)
        yv = jnp.dot(mv.astype(BF16), wout_ref[...], preferred_element_type=F32)
        y_ref[...] = yv
        r = lax.rsqrt(jnp.mean(yv * yv, axis=-1, keepdims=True) + EPS)
        xn = x_ref[...] + gate_ref[...] * ((yv * r) * g_ref[...])
        if target is None:
            xn_ref[...] = xn
        else:
            t_ref, l_ref = refs[10], refs[n_in + 5]

            @pl.when(pl.program_id(0) == 0)
            def _():
                l_ref[...] = jnp.zeros((8, 128), F32)

            err = xn - t_ref[...]
            xn_ref[...] = err * (1.0 / D)
            l_ref[...] += 0.5 * jnp.sum(jnp.mean(err * err, axis=-1, keepdims=True))

    tile = pl.BlockSpec((tr, D), lambda i: (i, 0))
    whole = lambda t: pl.BlockSpec(t.shape, lambda i: (0,) * t.ndim)
    last = target is not None
    outs = _call(
        body, name=name, grid=(S // tr,),
        in_specs=[pl.BlockSpec((tr, D), lambda i: (i, MGP_BLK)), pl.BlockSpec((tr, D), lambda i: (i, MGH_BLK)),
                  pl.BlockSpec((tr, POOL_W), lambda i: (i, 0)), tile, tile, whole(w_po), whole(w_ho),
                  whole(w_out), _rowvec(), _rowvec()] + [tile] * last,
        out_specs=[tile, tile, pl.BlockSpec((D, tr), lambda i: (0, i)), tile, tile]
        + [pl.BlockSpec((8, 128), lambda i: (0, 0))] * last,
        out_shape=[jax.ShapeDtypeStruct((S, D), BF16), jax.ShapeDtypeStruct((S, D), BF16),
                   jax.ShapeDtypeStruct((D, S), BF16), jax.ShapeDtypeStruct((S, D), F32),
                   jax.ShapeDtypeStruct((S, D), F32)] + [jax.ShapeDtypeStruct((8, 128), F32)] * last,
        sem=("arbitrary",) if last else ("parallel",),
        args=(proj, proj, a_in, b_in, x, w_po, w_ho, w_out, gate, g) + ((target,) if last else ()), carry=carry)
    return outs[:5 + last], outs[5 + last:]


def _layer_head_bwd(dxn, y, proj, br_a, br_b, w_po, w_ho, w_out, gate, g, name):
    S = y.shape[0]
    tr = _row_tile(S, 256)
    nsh, _, wsh = w_po.shape

    def body(dxn_ref, y_ref, mgp_ref, mgh_ref, bra_ref, brb_ref, wpo_ref, who_ref, wout_ref, gate_ref, g_ref,
             dy_ref, dba_ref, dbb_ref, dmg_ref, dain_ref, dbin_ref, dgate_ref, dg_ref):
        i = pl.program_id(0)

        @pl.when(i == 0)
        def _():
            dgate_ref[...] = jnp.zeros((1, D), F32)
            dg_ref[...] = jnp.zeros((1, D), F32)

        yv = y_ref[...]
        dv = dxn_ref[...]
        gv = g_ref[...]
        gt = gate_ref[...]
        r = lax.rsqrt(jnp.mean(yv * yv, axis=-1, keepdims=True) + EPS)
        yh = yv * r
        dgate_ref[...] += jnp.sum(dv * (yh * gv), axis=0, keepdims=True)
        dg_ref[...] += jnp.sum(dv * gt * yh, axis=0, keepdims=True)
        u = dv * gt * gv
        dy = (r * u - yv * (r * r * r) * jnp.mean(u * yv, axis=-1, keepdims=True)).astype(BF16)
        dy_ref[...] = dy
        dm = _dot_nt(dy, wout_ref[...])
        sp = _sig(mgp_ref[...].astype(F32))
        sh = _sig(mgh_ref[...].astype(F32))
        dba = (dm * sp).astype(BF16)
        dbb = (dm * sh).astype(BF16)
        dba_ref[...] = dba
        dbb_ref[...] = dbb
        dmg_ref[:, 0:D] = (dm * bra_ref[...].astype(F32) * sp * (1.0 - sp)).astype(BF16)
        dmg_ref[:, D:2 * D] = (dm * brb_ref[...].astype(F32) * sh * (1.0 - sh)).astype(BF16)
        dain = _dot_nt(dba[:, 0:wsh], wpo_ref[0])
        for j in range(1, nsh):
            dain = dain + _dot_nt(dba[:, j * wsh:(j + 1) * wsh], wpo_ref[j])
        dain_ref[...] = dain
        dbin_ref[...] = _dot_nt(dbb, who_ref[...])

    tile = pl.BlockSpec((tr, D), lambda i: (i, 0))
    whole = lambda t: pl.BlockSpec(t.shape, lambda i: (0,) * t.ndim)
    return pl.pallas_call(
        body, name=name, grid=(S // tr,),
        in_specs=[tile, tile, pl.BlockSpec((tr, D), lambda i: (i, MGP_BLK)),
                  pl.BlockSpec((tr, D), lambda i: (i, MGH_BLK)), tile, tile, whole(w_po), whole(w_ho),
                  whole(w_out), _rowvec(), _rowvec()],
        out_specs=[tile, tile, tile, pl.BlockSpec((tr, 2 * D), lambda i: (i, 0)),
                   pl.BlockSpec((tr, POOL_W), lambda i: (i, 0)), tile, _rowvec(), _rowvec()],
        out_shape=[jax.ShapeDtypeStruct((S, D), BF16)] * 3
        + [jax.ShapeDtypeStruct((S, 2 * D), BF16), jax.ShapeDtypeStruct((S, POOL_W), F32),
           jax.ShapeDtypeStruct((S, D), F32), jax.ShapeDtypeStruct((1, D), F32), jax.ShapeDtypeStruct((1, D), F32)],
        compiler_params=_cp(("arbitrary",)),
    )(dxn, y, proj, proj, br_a, br_b, w_po, w_ho, w_out, gate, g)


def _merge_fwd(proj, br_a, br_b, name):
    S = proj.shape[0]
    tr = _row_tile(S, 256)

    def body(mgp_ref, mgh_ref, a_ref, b_ref, o_ref, ot_ref):
        mv = _sig(mgp_ref[...]) * a_ref[...] + _sig(mgh_ref[...]) * b_ref[...]
        o_ref[...] = mv.astype(BF16)
        ot_ref[...] = mv.T.astype(BF16)

    tile = pl.BlockSpec((tr, D), lambda i: (i, 0))
    return pl.pallas_call(
        body, name=name, grid=(S // tr,),
        in_specs=[pl.BlockSpec((tr, D), lambda i: (i, MGP_BLK)), pl.BlockSpec((tr, D), lambda i: (i, MGH_BLK)),
                  tile, tile],
        out_specs=[tile, pl.BlockSpec((D, tr), lambda i: (0, i))],
        out_shape=[jax.ShapeDtypeStruct((S, D), BF16), jax.ShapeDtypeStruct((D, S), BF16)],
        compiler_params=_cp(("parallel",)),
    )(proj, proj, br_a, br_b)


def _merge_bwd(dm, proj, br_a, br_b, name):
    S = proj.shape[0]
    tr = _row_tile(S, 256)

    def body(dm_ref, mgp_ref, mgh_ref, a_ref, b_ref, da_ref, db_ref, dmg_ref):
        dmv = dm_ref[...]
        sp = _sig(mgp_ref[...])
        sh = _sig(mgh_ref[...])
        da_ref[...] = (dmv * sp).astype(BF16)
        db_ref[...] = (dmv * sh).astype(BF16)
        dmg_ref[:, 0:D] = (dmv * a_ref[...] * sp * (1.0 - sp)).astype(BF16)
        dmg_ref[:, D:2 * D] = (dmv * b_ref[...] * sh * (1.0 - sh)).astype(BF16)

    tile = pl.BlockSpec((tr, D), lambda i: (i, 0))
    return pl.pallas_call(
        body, name=name, grid=(S // tr,),
        in_specs=[tile, pl.BlockSpec((tr, D), lambda i: (i, MGP_BLK)),
                  pl.BlockSpec((tr, D), lambda i: (i, MGH_BLK)), tile, tile],
        out_specs=[tile, tile, pl.BlockSpec((tr, 2 * D), lambda i: (i, 0))],
        out_shape=[jax.ShapeDtypeStruct((S, D), BF16), jax.ShapeDtypeStruct((S, D), BF16),
                   jax.ShapeDtypeStruct((S, 2 * D), BF16)],
        compiler_params=_cp(("parallel",)),
    )(dm, proj, proj, br_a, br_b)


def _pool_pieces(u, g, S):
    rowi = lax.broadcasted_iota(jnp.int32, (S, 1), 0)

    def down(z, k):
        return jnp.where(rowi >= k, pltpu.roll(z, k, axis=0), 0.0)

    s2 = u + down(u, 1)
    s4 = s2 + down(s2, 2)
    s8 = s4 + down(s4, 4)
    s16 = s8 + down(s8, 8)
    win = jnp.where(g == 0, s2, jnp.where(g == 1, s4, jnp.where(g == 2, s8, s16)))
    w = jnp.where(g == 0, 2, jnp.where(g == 1, 4, jnp.where(g == 2, 8, 16)))
    count = jnp.minimum(rowi + 1, w).astype(F32)
    return win / count - u, count, rowi


def _pool_fwd(proj, pw, pscale, name):
    S = proj.shape[0]

    def body(pv_ref, pg_ref, pw_ref, sc_ref, a_ref, at_ref):
        g = pl.program_id(0)
        pooled, _, _ = _pool_pieces(pv_ref[...].astype(F32), g, S)
        pm = jnp.dot(pooled.astype(BF16), pw_ref[...].astype(BF16), preferred_element_type=F32)
        pgv = pg_ref[...].astype(F32)
        av = pm * sc_ref[...] * (pgv * _sig(pgv))
        a_ref[...] = av.astype(BF16)
        at_ref[...] = av.T.astype(BF16)

    return pl.pallas_call(
        body, name=name, grid=(GROUPS,),
        in_specs=[pl.BlockSpec((S, 128), lambda g: (0, PV0 + g)), pl.BlockSpec((S, 128), lambda g: (0, PG0 + g)),
                  pl.BlockSpec((None, 128, 128), lambda g: (g, 0, 0)), pl.BlockSpec((1, 128), lambda g: (0, g))],
        out_specs=[pl.BlockSpec((S, 128), lambda g: (0, g)), pl.BlockSpec((128, S), lambda g: (g, 0))],
        out_shape=[jax.ShapeDtypeStruct((S, POOL_W), BF16), jax.ShapeDtypeStruct((POOL_W, S), BF16)],
        compiler_params=_cp(("parallel",)),
    )(proj, proj, pw, pscale)


def _pool_bwd(da, proj, pw, pscale, name):
    S = proj.shape[0]

    def body(da_ref, pv_ref, pg_ref, pw_ref, sc_ref, dpv_ref, dpg_ref, dpw_ref, dsc_ref):
        g = pl.program_id(0)
        pooled, count, rowi = _pool_pieces(pv_ref[...].astype(F32), g, S)
        pwb = pw_ref[...].astype(BF16)
        pm = jnp.dot(pooled.astype(BF16), pwb, preferred_element_type=F32)
        scv = sc_ref[...]
        pgv = pg_ref[...].astype(F32)
        sg = _sig(pgv)
        dav = da_ref[...]
        d_ps = dav * (pgv * sg)
        dpg_ref[...] = (dav * (pm * scv) * _dsilu(pgv, sg)).astype(BF16)
        dsc_ref[...] = jnp.sum(d_ps * pm, axis=0, keepdims=True)
        d_pm = (d_ps * scv).astype(BF16)
        dpw_ref[...] = lax.dot_general(pooled.astype(BF16), d_pm, (((0,), (0,)), ((), ())),
                                       preferred_element_type=F32)
        d_pooled = lax.dot_general(d_pm, pwb, (((1,), (1,)), ((), ())), preferred_element_type=F32)
        z = d_pooled / count

        def up(v, k):
            return jnp.where(rowi < S - k, pltpu.roll(v, S - k, axis=0), 0.0)

        t2 = z + up(z, 1)
        t4 = t2 + up(t2, 2)
        t8 = t4 + up(t4, 4)
        t16 = t8 + up(t8, 8)
        adj = jnp.where(g == 0, t2, jnp.where(g == 1, t4, jnp.where(g == 2, t8, t16)))
        dpv_ref[...] = (adj - d_pooled).astype(BF16)

    col = lambda g: (0, g)
    return pl.pallas_call(
        body, name=name, grid=(GROUPS,),
        in_specs=[pl.BlockSpec((S, 128), col), pl.BlockSpec((S, 128), lambda g: (0, PV0 + g)),
                  pl.BlockSpec((S, 128), lambda g: (0, PG0 + g)),
                  pl.BlockSpec((None, 128, 128), lambda g: (g, 0, 0)), pl.BlockSpec((1, 128), col)],
        out_specs=[pl.BlockSpec((S, 128), col), pl.BlockSpec((S, 128), col),
                   pl.BlockSpec((None, 128, 128), lambda g: (g, 0, 0)), pl.BlockSpec((1, 128), col)],
        out_shape=[jax.ShapeDtypeStruct((S, POOL_W), BF16), jax.ShapeDtypeStruct((S, POOL_W), BF16),
                   jax.ShapeDtypeStruct((GROUPS, 128, 128), F32), jax.ShapeDtypeStruct((1, POOL_W), F32)],
        compiler_params=_cp(("parallel",)),
    )(da, proj, proj, pw, pscale)


def _chunk_cumsum(z, rowi):
    for sh in (1, 2, 4, 8, 16, 32):
        z = z + jnp.where(rowi >= sh, pltpu.roll(z, sh, axis=0), 0.0)
    return z


def _chunk_rev_cumsum(z, rowi):
    for sh in (1, 2, 4, 8, 16, 32):
        z = z + jnp.where(rowi < CH - sh, pltpu.roll(z, CH - sh, axis=0), 0.0)
    return z


def _dot_nn(a, b):
    return jnp.dot(a.astype(BF16), b.astype(BF16), preferred_element_type=F32)


def _dot_nt(a, b):
    return lax.dot_general(a.astype(BF16), b.astype(BF16), (((1,), (1,)), ((), ())), preferred_element_type=F32)


def _dot_tn(a, b):
    return lax.dot_general(a.astype(BF16), b.astype(BF16), (((0,), (0,)), ((), ())), preferred_element_type=F32)


def _gates(hq, hf, lbv):
    hq, hf = hq.astype(F32), hf.astype(F32)
    sq = _sig(hq)
    sf = _sig(hf)
    f = lbv + (1.0 - lbv) * sf
    fc = jnp.maximum(f, 1e-30)
    return hq * sq, sq, sf, f, fc, jnp.log(fc)


DECAY_CAP = 60.0


def _block_ref(c_ref, i):
    if i == 0:
        return jnp.zeros((1, HD), F32)
    return c_ref[SB * i - 1:SB * i, :]


def _block_decay(c_ref):
    spans = [_block_ref(c_ref, i) - c_ref[SB * (i + 1) - 1:SB * (i + 1), :] for i in range(CH // SB)]
    return functools.reduce(jnp.maximum, spans)


def _hgrn_fwd(proj, lb, gn, name, carry=None):
    S = proj.shape[0]
    nch = S // CH
    W = NH * HD

    def body(hq_ref, hf_ref, hi_ref, hg_ref, lb_ref, gn_ref, bin_ref, bint_ref, oraw_ref, st_ref, mild_ref,
             cum_ref, q_s, k_s, c_s, v_s, o_s, state_s, qf_s, kf_s, cf_s):
        state_s[...] = jnp.zeros((NH, HD, HD), F32)
        rowi = lax.broadcasted_iota(jnp.int32, (CH, 1), 0)
        coli = lax.broadcasted_iota(jnp.int32, (1, CH), 1)
        sbi = lax.broadcasted_iota(jnp.int32, (SB, 1), 0)
        gnv = gn_ref[...]

        def gates_pass(n, worst):
            rows = pl.ds(pl.multiple_of(n * CH, CH), CH)
            for hh in range(NH):
                lanes = slice(hh * HD, (hh + 1) * HD)
                q, _, _, f, _, logf = _gates(hq_ref[rows, lanes], hf_ref[rows, lanes], lb_ref[:, lanes])
                c = _chunk_cumsum(logf, rowi)
                qf_s[hh, rows, :] = q
                kf_s[hh, rows, :] = 1.0 - f
                cf_s[hh, rows, :] = c
                cum_ref[rows, lanes] = c
                c_s[hh] = c
                worst = jnp.maximum(worst, _block_decay(c_s.at[hh]))
            return worst

        def between_chunks(hh, n, rows):
            lanes = slice(hh * HD, (hh + 1) * HD)
            q = qf_s[hh, rows, :]
            k = kf_s[hh, rows, :]
            c = cf_s[hh, rows, :]
            v = hi_ref[rows, lanes].astype(F32)
            q_s[hh] = q
            k_s[hh] = k
            c_s[hh] = c
            v_s[hh] = v
            st = state_s[hh]
            st_ref[hh, n] = st.astype(BF16)
            o_s[hh] = _dot_nt(q * jnp.exp(c), st)
            last = c_s[hh, CH - 1:CH, :]
            state_s[hh] = st * jnp.exp(last) + _dot_tn(v, k * jnp.exp(last - c))

        def within_chunk_matmul(hh):
            q, k, c, v = q_s[hh], k_s[hh], c_s[hh], v_s[hh]
            a = jnp.zeros((CH, CH), F32)
            for i in range(CH // SB):
                r_i = _block_ref(c_s.at[hh], i)
                qi = q * jnp.exp(jnp.minimum(c - r_i, 0.0))
                kei = k * jnp.exp(jnp.minimum(r_i - c, DECAY_CAP))
                m_i = (rowi >= SB * i) & (rowi < SB * (i + 1)) & (coli <= rowi)
                a = a + jnp.where(m_i, _dot_nt(qi, kei), 0.0)
            o_s[hh] += _dot_nn(a, v)

        def within_chunk_exact(hh):
            q, k, c, v = q_s[hh], k_s[hh], c_s[hh], v_s[hh]
            a_off = jnp.zeros((CH, CH), F32)
            for i in range(1, CH // SB):
                r_i = _block_ref(c_s.at[hh], i)
                qi = q * jnp.exp(jnp.minimum(c - r_i, 0.0))
                kei = k * jnp.exp(jnp.minimum(r_i - c, 0.0))
                m_i = (rowi >= SB * i) & (rowi < SB * (i + 1)) & (coli < SB * i)
                a_off = a_off + jnp.where(m_i, _dot_nt(qi, kei), 0.0)
            o_s[hh] += _dot_nn(a_off, v)
            for i in range(CH // SB):
                blk = slice(SB * i, SB * (i + 1))
                qb = q_s[hh, blk, :]
                cb = c_s[hh, blk, :]
                acc = jnp.zeros((SB, HD), F32)
                for s in range(SB):
                    row = SB * i + s
                    w = jnp.exp(jnp.minimum(cb - c_s[hh, row:row + 1, :], 0.0))
                    a_col = jnp.sum(qb * k_s[hh, row:row + 1, :] * w, axis=-1, keepdims=True)
                    acc = acc + jnp.where(sbi >= s, a_col, 0.0) * v_s[hh, row:row + 1, :]
                o_s[hh, blk, :] += acc

        def norm_and_gate(hh, rows):
            lanes = slice(hh * HD, (hh + 1) * HD)
            ov = o_s[hh]
            oraw_ref[rows, lanes] = ov
            r = lax.rsqrt(jnp.mean(ov * ov, axis=-1, keepdims=True) + EPS)
            hg = hg_ref[rows, lanes].astype(F32)
            bin_ref[rows, lanes] = ((ov * r) * gnv * (hg * _sig(hg))).astype(BF16)

        def chunk_with(within_chunk):
            def chunk(n, carry):
                rows = pl.ds(pl.multiple_of(n * CH, CH), CH)
                for hh in range(NH):
                    between_chunks(hh, n, rows)
                for hh in range(NH):
                    within_chunk(hh)
                for hh in range(NH):
                    norm_and_gate(hh, rows)
                return carry
            return chunk

        worst = lax.fori_loop(0, nch, gates_pass, jnp.zeros((1, HD), F32))
        mild = jnp.max(worst) <= DECAY_CAP
        mild_ref[...] = jnp.broadcast_to(jnp.where(mild, 1.0, 0.0), (8, HD))

        @pl.when(mild)
        def _():
            lax.fori_loop(0, nch, chunk_with(within_chunk_matmul), 0, unroll=4)

        @pl.when(jnp.logical_not(mild))
        def _():
            lax.fori_loop(0, nch, chunk_with(within_chunk_exact), 0)

        bint_ref[...] = bin_ref[...].astype(F32).T.astype(BF16)

    col = lambda off: pl.BlockSpec((S, W), lambda h: (0, off // NH + h))
    head = pl.BlockSpec((S, W), lambda h: (0, h))
    outs = _call(
        body, name=name, grid=(HEADS // NH,),
        in_specs=[col(HQ0), col(HF0), col(HI0), col(HG0), pl.BlockSpec((1, W), lambda h: (0, h)),
                  pl.BlockSpec((1, HD), lambda h: (0, 0))],
        out_specs=[head, pl.BlockSpec((W, S), lambda h: (h, 0)), head,
                   pl.BlockSpec((NH, nch, HD, HD), lambda h: (h, 0, 0, 0)),
                   pl.BlockSpec((8, HD), lambda h: (h, 0)), head],
        out_shape=[jax.ShapeDtypeStruct((S, D), BF16), jax.ShapeDtypeStruct((D, S), BF16),
                   jax.ShapeDtypeStruct((S, D), F32), jax.ShapeDtypeStruct((HEADS, nch, HD, HD), BF16),
                   jax.ShapeDtypeStruct((8 * HEADS // NH, HD), F32), jax.ShapeDtypeStruct((S, D), F32)],
        scratch_shapes=[pltpu.VMEM((NH, CH, HD), F32)] * 5 + [pltpu.VMEM((NH, HD, HD), F32)]
        + [pltpu.VMEM((NH, S, HD), F32)] * 3,
        sem=("parallel",), args=(proj, proj, proj, proj, lb, gn), carry=carry)
    return outs[:6], outs[6:]


def _hgrn_bwd(dbin, proj, oraw, states, mild, cum, lb, gn, name, carry=None):
    S = proj.shape[0]
    nch = S // CH
    W = NH * HD

    def body(db_ref, hq_ref, hf_ref, hi_ref, hg_ref, or_ref, st_ref, mild_ref, cum_ref, lb_ref, gn_ref,
             dq_ref, df_ref, di_ref, dg_ref, dlb_ref, dgn_ref,
             q_s, k_s, c_s, v_s, do_s, dq_s, dk_s, dv_s, dc_s, dqd_s, dkd_s, f_s, sf_s, sq_s, dl_s, dst_s,
             dlb_s, dgn_s):
        dst_s[...] = jnp.zeros((NH, HD, HD), F32)
        dlb_s[...] = jnp.zeros((1, W), F32)
        dgn_s[...] = jnp.zeros((1, HD), F32)
        rowi = lax.broadcasted_iota(jnp.int32, (CH, 1), 0)
        rowi2 = lax.broadcasted_iota(jnp.int32, (CH, CH), 0)
        coli2 = lax.broadcasted_iota(jnp.int32, (CH, CH), 1)
        sbi = lax.broadcasted_iota(jnp.int32, (SB, 1), 0)
        gnv = gn_ref[...]
        def between_chunks(hh, n, rows):
            lanes = slice(hh * HD, (hh + 1) * HD)
            lbv = lb_ref[:, lanes]
            hq = hq_ref[rows, lanes].astype(F32)
            sq = _sig(hq)
            sf = _sig(hf_ref[rows, lanes].astype(F32))
            f = lbv + (1.0 - lbv) * sf
            q = hq * sq
            k = 1.0 - f
            f_s[hh] = f
            sf_s[hh] = sf
            sq_s[hh] = sq
            v = hi_ref[rows, lanes].astype(F32)
            c = cum_ref[rows, lanes]
            ov = or_ref[rows, lanes]
            hg = hg_ref[rows, lanes].astype(F32)
            sg = _sig(hg)
            r = lax.rsqrt(jnp.mean(ov * ov, axis=-1, keepdims=True) + EPS)
            dbv = db_ref[rows, lanes]
            d_on = dbv * (hg * sg)
            dg_ref[rows, lanes] = (dbv * ((ov * r) * gnv) * _dsilu(hg, sg)).astype(BF16)
            dgn_s[...] += jnp.sum(d_on * (ov * r), axis=0, keepdims=True)
            u = d_on * gnv
            do = r * u - ov * (r * r * r) * jnp.mean(u * ov, axis=-1, keepdims=True)
            q_s[hh] = q
            k_s[hh] = k
            c_s[hh] = c
            v_s[hh] = v
            do_s[hh] = do
            st = st_ref[hh, n].astype(F32)
            dst = dst_s[hh]
            ec = jnp.exp(c)
            last = c_s[hh, CH - 1:CH, :]
            el = jnp.exp(last - c)
            elast = jnp.exp(last)
            dq = _dot_nn(do, st) * ec
            dk = _dot_nn(v, dst) * el
            dq_s[hh] = dq
            dk_s[hh] = dk
            dv_s[hh] = _dot_nt(k * el, dst)
            dc_s[hh] = q * dq - k * dk
            dl_s[hh] = (jnp.sum(k * dk, axis=0, keepdims=True)
                        + elast * jnp.sum(st * dst, axis=0, keepdims=True))
            dst_s[hh] = dst * elast + _dot_tn(do, q * ec)

        def pairs_matmul(hh, first, cap, strict):
            q, k, c, v, do = q_s[hh], k_s[hh], c_s[hh], v_s[hh], do_s[hh]
            d_a = _dot_nt(do, v).astype(BF16).astype(F32)
            d_at = d_a.T
            at = jnp.zeros((CH, CH), F32)
            dq, dk, dcum = dq_s[hh], dk_s[hh], dc_s[hh]
            for i in range(first, CH // SB):
                r_i = _block_ref(c_s.at[hh], i)
                eq = jnp.exp(jnp.minimum(c - r_i, 0.0))
                ek = jnp.exp(jnp.minimum(r_i - c, cap))
                qi = (q * eq).astype(BF16).astype(F32)
                kei = (k * ek).astype(BF16).astype(F32)
                in_t = (rowi2 >= SB * i) & (rowi2 < SB * (i + 1))
                in_s = (coli2 >= SB * i) & (coli2 < SB * (i + 1))
                m_ts = in_t & ((coli2 < SB * i) if strict else (coli2 <= rowi2))
                m_st = in_s & ((rowi2 < SB * i) if strict else (rowi2 <= coli2))
                at = at + jnp.where(m_st, _dot_nt(kei, qi), 0.0)
                dq_i = _dot_nn(jnp.where(m_ts, d_a, 0.0), kei)
                dk_i = _dot_nn(jnp.where(m_st, d_at, 0.0), qi)
                dq = dq + dq_i * eq
                dk = dk + dk_i * ek
                dcum = dcum + (qi * dq_i - kei * dk_i)
            dq_s[hh] = dq
            dk_s[hh] = dk
            dc_s[hh] = dcum
            dv_s[hh] += _dot_nn(at, do)

        def pairs_exact(hh):
            dqd_s[hh] = jnp.zeros((CH, HD), F32)
            dkd_s[hh] = jnp.zeros((CH, HD), F32)
            for i in range(CH // SB):
                blk = slice(SB * i, SB * (i + 1))
                qb = q_s[hh, blk, :]
                cb = c_s[hh, blk, :]
                dob = do_s[hh, blk, :]
                dq_acc = jnp.zeros((SB, HD), F32)
                for s in range(SB):
                    row = SB * i + s
                    ks = k_s[hh, row:row + 1, :]
                    vs = v_s[hh, row:row + 1, :]
                    w = jnp.exp(jnp.minimum(cb - c_s[hh, row:row + 1, :], 0.0))
                    live = sbi >= s
                    a_col = jnp.where(live, jnp.sum(qb * ks * w, axis=-1, keepdims=True), 0.0)
                    da_col = jnp.where(live, jnp.sum(dob * vs, axis=-1, keepdims=True), 0.0)
                    dq_acc = dq_acc + da_col * ks * w
                    dkd_s[hh, row:row + 1, :] += jnp.sum(da_col * qb * w, axis=0, keepdims=True)
                    dv_s[hh, row:row + 1, :] += jnp.sum(a_col * dob, axis=0, keepdims=True)
                dqd_s[hh, blk, :] += dq_acc
            dq_d = dqd_s[hh]
            dk_d = dkd_s[hh]
            dq_s[hh] += dq_d
            dk_s[hh] += dk_d
            dc_s[hh] += q_s[hh] * dq_d - k_s[hh] * dk_d

        def gate_grads(hh, rows):
            lanes = slice(hh * HD, (hh + 1) * HD)
            lbv = lb_ref[:, lanes]
            hq = hq_ref[rows, lanes].astype(F32)
            f, sf, sq = f_s[hh], sf_s[hh], sq_s[hh]
            dlogf = _chunk_rev_cumsum(dc_s[hh], rowi) + dl_s[hh]
            dfv = jnp.where(f > 1e-30, dlogf / jnp.maximum(f, 1e-30), 0.0) - dk_s[hh]
            dlb_s[:, lanes] += jnp.sum(dfv * (1.0 - sf), axis=0, keepdims=True)
            df_ref[rows, lanes] = (dfv * (1.0 - lbv) * sf * (1.0 - sf)).astype(BF16)
            dq_ref[rows, lanes] = (dq_s[hh] * _dsilu(hq, sq)).astype(BF16)
            di_ref[rows, lanes] = dv_s[hh].astype(BF16)

        def chunk_with(pairs):
            def chunk(j, carry):
                n = nch - 1 - j
                rows = pl.ds(pl.multiple_of(n * CH, CH), CH)
                for hh in range(NH):
                    between_chunks(hh, n, rows)
                for hh in range(NH):
                    pairs(hh)
                for hh in range(NH):
                    gate_grads(hh, rows)
                return carry
            return chunk

        def pairs_mild(hh):
            pairs_matmul(hh, 0, DECAY_CAP, strict=False)

        def pairs_any(hh):
            pairs_matmul(hh, 1, 0.0, strict=True)
            pairs_exact(hh)

        mild = jnp.max(mild_ref[...]) > 0.5

        @pl.when(mild)
        def _():
            lax.fori_loop(0, nch, chunk_with(pairs_mild), 0, unroll=4)

        @pl.when(jnp.logical_not(mild))
        def _():
            lax.fori_loop(0, nch, chunk_with(pairs_any), 0)

        dlb_ref[...] = dlb_s[...]
        dgn_ref[...] = jnp.broadcast_to(dgn_s[...], (8, HD))

    col = lambda off: pl.BlockSpec((S, W), lambda h: (0, off // NH + h))
    head = pl.BlockSpec((S, W), lambda h: (0, h))
    vec = pl.BlockSpec((1, W), lambda h: (0, h))
    outs = _call(
        body, name=name, grid=(HEADS // NH,),
        in_specs=[head, col(HQ0), col(HF0), col(HI0), col(HG0), head,
                  pl.BlockSpec((NH, nch, HD, HD), lambda h: (h, 0, 0, 0)),
                  pl.BlockSpec((8, HD), lambda h: (h, 0)), head, vec, pl.BlockSpec((1, HD), lambda h: (0, 0))],
        out_specs=[head, head, head, head, vec, pl.BlockSpec((8, HD), lambda h: (h, 0))],
        out_shape=[jax.ShapeDtypeStruct((S, D), BF16)] * 4
        + [jax.ShapeDtypeStruct((1, D), F32), jax.ShapeDtypeStruct((8 * HEADS // NH, HD), F32)],
        scratch_shapes=[pltpu.VMEM((NH, CH, HD), F32)] * 14
        + [pltpu.VMEM((NH, 1, HD), F32), pltpu.VMEM((NH, HD, HD), F32), pltpu.VMEM((1, W), F32),
           pltpu.VMEM((1, HD), F32)],
        sem=("parallel",), args=(dbin, proj, proj, proj, proj, oraw, states, mild, cum, lb, gn), carry=carry)
    dq, df, di, dg, dlb, dgn = outs[:6]
    return (dq, df, di, dg, dlb, dgn.reshape(HEADS // NH, 8, HD)[:, 0, :]), outs[6:]


def _lower_bounds(l0, l1):
    m = jnp.maximum(l0, l1)
    e0 = jnp.exp(l0 - m)
    e1 = jnp.exp(l1 - m)
    tot = e0 + e1
    p0 = e0 / tot
    p1 = e1 / tot
    return jnp.clip(p0 - p0, 0.0, 1.0), jnp.clip((p0 + p1) - p0, 0.0, 1.0)


def _lb_fwd(logits):
    def body(l_ref, o_ref):
        lb0, lb1 = _lower_bounds(l_ref[0:1, :], l_ref[1:2, :])
        o_ref[0:1, :] = lb0
        o_ref[1:2, :] = lb1

    return pl.pallas_call(body, name="lb_fwd", out_shape=jax.ShapeDtypeStruct((2, D), F32))(logits)


def _lb_bwd(logits, dlb):
    def body(l_ref, d_ref, o_ref):
        _, vjp = jax.vjp(_lower_bounds, l_ref[0:1, :], l_ref[1:2, :])
        g0, g1 = vjp((d_ref[0:1, :], d_ref[1:2, :]))
        o_ref[0:1, :] = g0
        o_ref[1:2, :] = g1

    return pl.pallas_call(body, name="lb_bwd", out_shape=jax.ShapeDtypeStruct((2, D), F32))(logits, dlb)


ADA_PAD = 128


def _ada_fwd(c_pad, w_ada, b_sh):
    ns = w_ada.shape[2]

    def body(c_ref, w_ref, b_ref, o_ref):
        cv = c_ref[...]
        ca = (cv * _sig(cv)).astype(BF16)
        for l in range(2):
            res = jnp.dot(ca, w_ref[l].astype(BF16), preferred_element_type=F32)
            o_ref[:, l * ns:(l + 1) * ns] = res[0:NDEV, :] + b_ref[l:l + 1, :]

    return pl.pallas_call(body, name="ada_fwd", out_shape=jax.ShapeDtypeStruct((NDEV, 2 * ns), F32),
                          compiler_params=_cp())(c_pad, w_ada, b_sh)


def _ada_wgrad(c_pad_t, d_ada_sh):
    ns = d_ada_sh.shape[2]

    def body(c_ref, d_ref, o_ref):
        cv = c_ref[...]
        ca = (cv * _sig(cv)).astype(BF16)
        for l in range(2):
            o_ref[l] = jnp.dot(ca, d_ref[l].astype(BF16), preferred_element_type=F32)

    return pl.pallas_call(body, name="ada_wgrad", out_shape=jax.ShapeDtypeStruct((2, D, ns), F32),
                          compiler_params=_cp())(c_pad_t, d_ada_sh)


def _sum_devices(g):
    _, R, C = g.shape

    def body(g_ref, o_ref):
        acc = g_ref[0]
        for d in range(1, NDEV):
            acc = acc + g_ref[d]
        o_ref[...] = acc

    return pl.pallas_call(body, name="sum_devices", out_shape=jax.ShapeDtypeStruct((R, C), F32),
                          compiler_params=_cp())(g)


def _adamw(w, g, m, v, name, carry=None):
    R, C = w.shape
    tr = _row_tile(R, max(8, (1 << 19) // C))

    def body(w_ref, g_ref, m_ref, v_ref, d_ref, nm_ref, nv_ref):
        d_ref[...], nm_ref[...], nv_ref[...] = _adamw_update(w_ref[...], g_ref[...], m_ref[...], v_ref[...])

    tile = pl.BlockSpec((tr, C), lambda i: (i, 0))
    return _call(body, name=name, grid=(R // tr,), in_specs=[tile] * 4, out_specs=[tile] * 3,
                 out_shape=[jax.ShapeDtypeStruct((R, C), F32)] * 3, sem=("parallel",), args=(w, g, m, v),
                 carry=carry)


def _adamw_update(w, g, m, v):
    nm = B1 * m + (1.0 - B1) * g
    nv = B2 * v + (1.0 - B2) * (g * g)
    m_hat = nm / (1.0 - B1 ** STEP)
    v_hat = nv / (1.0 - B2 ** STEP)
    return -LR * (m_hat / (jnp.sqrt(v_hat) + AEPS) + WD * w), nm, nv


SMALL_PARTS = (("b_ada", 0, 6, D), ("g_pre", 8, 2, D), ("g_post", 16, 2, D), ("lb_logits", 24, 2, D),
               ("pool_w", 32, 128, D), ("pool_scale", 160, 1, D), ("hgrn_norm_g", 168, 1, 2 * HD))


def _adamw_small(g_small, g_lb_logits, wmv):
    n = len(SMALL_PARTS)

    def body(g_ref, glb_ref, *refs):
        ins, outs = refs[:3 * n], refs[3 * n:]
        for p, (key, row0, rows, width) in enumerate(SMALL_PARTS):
            gv = glb_ref[...] if key == "lb_logits" else g_ref[row0:row0 + rows, 0:width]
            res = _adamw_update(ins[3 * p][...], gv, ins[3 * p + 1][...], ins[3 * p + 2][...])
            for t in range(3):
                outs[3 * p + t][...] = res[t]

    flat = [t for triple in wmv for t in triple]
    outs = pl.pallas_call(body, name="adamw_small",
                          out_shape=[jax.ShapeDtypeStruct(t.shape, F32) for t in flat],
                          compiler_params=_cp())(g_small, g_lb_logits, *flat)
    return [outs[3 * p:3 * p + 3] for p in range(n)]


def _cast_to_slot(place, w, l, name):
    _, R, C = w.shape
    tr = _row_tile(R, max(8, (1 << 19) // C))

    def body(p_ref, w_ref, o_ref):
        o_ref[...] = w_ref[...].astype(BF16)

    return pl.pallas_call(
        body, name=name, out_shape=jax.ShapeDtypeStruct((NCHIP, R, C), BF16),
        grid_spec=pltpu.PrefetchScalarGridSpec(
            num_scalar_prefetch=1, grid=(R // tr,),
            in_specs=[pl.BlockSpec((None, tr, C), lambda i, p_ref: (l, i, 0))],
            out_specs=pl.BlockSpec((None, tr, C), lambda i, p_ref: (p_ref[0], i, 0))),
        compiler_params=_cp(("parallel",)),
    )(place, w)


def _pair_add(core, g, got, name):
    _, R, C = g.shape
    r2 = R // 2
    tr = _row_tile(r2, max(8, (1 << 19) // C))
    nt = r2 // tr

    def body(c_ref, a_ref, b_ref, o_ref):
        o_ref[...] = (a_ref[...].astype(F32) + b_ref[...].astype(F32)).astype(o_ref.dtype)

    return pl.pallas_call(
        body, name=name, out_shape=jax.ShapeDtypeStruct((NCHIP, r2, C), BF16),
        grid_spec=pltpu.PrefetchScalarGridSpec(
            num_scalar_prefetch=1, grid=(NCHIP, nt),
            in_specs=[pl.BlockSpec((None, tr, C), lambda j, i, c_ref: (j, c_ref[0] * nt + i, 0)),
                      pl.BlockSpec((None, tr, C), lambda j, i, c_ref: (j, i, 0))],
            out_specs=pl.BlockSpec((None, tr, C), lambda j, i, c_ref: (j, i, 0))),
        compiler_params=_cp(("parallel", "parallel")),
    )(core, g, got)


def _chip_sum(place, part, recv, layer, both, name):
    _, r2, C = part.shape
    tr = _row_tile(r2, max(8, (1 << 18) // C))
    nt = r2 // tr

    def body(p_ref, own_ref, r_ref, *rest):
        o_ref = rest[-1]
        me = p_ref[0]
        own = own_ref[...].astype(F32)
        acc = None
        for j in range(NCHIP):
            slot = jnp.minimum(jnp.where(j > me, j - 1, j), NCHIP - 2)
            term = jnp.where(me == j, own, r_ref[slot].astype(F32))
            acc = term if acc is None else acc + term
        o_ref[...] = acc

    args = (place, part, recv) if both is None else (place, part, recv, both)
    return pl.pallas_call(
        body, name=name, out_shape=jax.ShapeDtypeStruct((2, 2 * r2, C), F32),
        grid_spec=pltpu.PrefetchScalarGridSpec(
            num_scalar_prefetch=1, grid=(nt,),
            in_specs=[pl.BlockSpec((None, tr, C), lambda i, p_ref: (p_ref[0], i, 0)),
                      pl.BlockSpec((NCHIP - 1, tr, C), lambda i, p_ref: (0, i, 0))] + [ANY] * (len(args) - 3),
            out_specs=pl.BlockSpec((None, tr, C), lambda i, p_ref: (layer, p_ref[1] * nt + i, 0))),
        input_output_aliases={} if both is None else {3: 0},
        compiler_params=_cp(("parallel",)),
    )(*args)


def _place():
    x, y, c = lax.axis_index("x"), lax.axis_index("y"), lax.axis_index("c")
    chips = [(1 - x, y), (x, 1 - y), (1 - x, 1 - y)]
    return x, y, c, chips


def _gather_small(blk, name):
    m_per, n = blk.shape

    def body(x_ref, out_ref, send_sems, recv_sems, local_sem):
        x, y, c, chips = _place()
        me, sibling = (x, y, c), (x, y, 1 - c)

        def rows(px, py, pc):
            return out_ref.at[pl.ds((4 * px + 2 * py + pc) * m_per, m_per), :]

        def copy(k, block, to, src=None):
            return pltpu.make_async_remote_copy(
                src_ref=rows(*block) if src is None else src, dst_ref=rows(*block),
                send_sem=send_sems.at[k], recv_sem=recv_sems.at[k], device_id=to, device_id_type=MESH)

        mine = pltpu.make_async_copy(x_ref, rows(*me), local_sem)
        mine.start()
        first = [copy(0, me, sibling, src=x_ref)]
        first += [copy(1 + j, me, (*chip, c), src=x_ref) for j, chip in enumerate(chips)]
        for cp in first:
            cp.start()
        passed = [copy(4 + j, (*chip, c), sibling) for j, chip in enumerate(chips)]
        for j, chip in enumerate(chips):
            copy(1 + j, (*chip, c), me).wait_recv()
            passed[j].start()
        copy(0, sibling, me).wait_recv()
        for j, chip in enumerate(chips):
            copy(4 + j, (*chip, 1 - c), me).wait_recv()
        for cp in first + passed:
            cp.wait_send()
        mine.wait()

    return pl.pallas_call(
        body, name=name, out_shape=jax.ShapeDtypeStruct((NDEV * m_per, n), blk.dtype),
        in_specs=[pl.BlockSpec(memory_space=pltpu.VMEM)], out_specs=pl.BlockSpec(memory_space=pltpu.VMEM),
        scratch_shapes=[pltpu.SemaphoreType.DMA((7,)), pltpu.SemaphoreType.DMA((7,)), pltpu.SemaphoreType.DMA],
        compiler_params=_cp(),
    )(blk)


def _gather_rows_carry(blk):
    m_per, n = blk.shape

    def rows(ref, px, py, pc):
        return ref.at[pl.ds((4 * px + 2 * py + pc) * m_per, m_per), :]

    def copy(ins, outs, send_sems, recv_sems, k, block, to, own=False):
        return pltpu.make_async_remote_copy(
            src_ref=ins[0] if own else rows(outs[0], *block), dst_ref=rows(outs[0], *block),
            send_sem=send_sems.at[k], recv_sem=recv_sems.at[k], device_id=to, device_id_type=MESH)

    def mine(ins, outs, send_sems):
        x, y, c, _ = _place()
        return pltpu.make_async_copy(ins[0], rows(outs[0], x, y, c), send_sems.at[7])

    def start(ins, outs, send_sems, recv_sems):
        x, y, c, chips = _place()
        mine(ins, outs, send_sems).start()
        copy(ins, outs, send_sems, recv_sems, 0, (x, y, c), (x, y, 1 - c), own=True).start()
        for j, chip in enumerate(chips):
            copy(ins, outs, send_sems, recv_sems, 1 + j, (x, y, c), (*chip, c), own=True).start()

    def finish(ins, outs, send_sems, recv_sems):
        x, y, c, chips = _place()
        for j, chip in enumerate(chips):
            copy(ins, outs, send_sems, recv_sems, 1 + j, (*chip, c), (x, y, c)).wait_recv()
            copy(ins, outs, send_sems, recv_sems, 4 + j, (*chip, c), (x, y, 1 - c)).start()
        copy(ins, outs, send_sems, recv_sems, 0, (x, y, 1 - c), (x, y, c)).wait_recv()
        for j, chip in enumerate(chips):
            copy(ins, outs, send_sems, recv_sems, 4 + j, (*chip, 1 - c), (x, y, c)).wait_recv()
        copy(ins, outs, send_sems, recv_sems, 0, (x, y, c), (x, y, 1 - c), own=True).wait_send()
        for j, chip in enumerate(chips):
            copy(ins, outs, send_sems, recv_sems, 1 + j, (x, y, c), (*chip, c), own=True).wait_send()
            copy(ins, outs, send_sems, recv_sems, 4 + j, (*chip, c), (x, y, 1 - c)).wait_send()
        mine(ins, outs, send_sems).wait()

    return _Carry([blk], [jax.ShapeDtypeStruct((NDEV * m_per, n), blk.dtype)], {}, 8, start, finish)


def _gather_carry(shards, piece=(0, 1, 1)):
    n = len(shards)
    first, count, of = piece

    def rows(ref, half):
        r2 = ref.shape[1] // 2
        return pl.ds(half * r2 + first * (r2 // of), count * (r2 // of))

    def over_ici(outs, send_sems, recv_sems, a, j, chip_xy, slot):
        x, y, c, _ = _place()
        blk = outs[a].at[slot, rows(outs[a], c), :]
        return pltpu.make_async_remote_copy(
            src_ref=blk, dst_ref=blk, send_sem=send_sems.at[6 * a + j], recv_sem=recv_sems.at[6 * a + j],
            device_id=(*chip_xy, c), device_id_type=MESH)

    def over_d2d(outs, send_sems, recv_sems, a, j, slot, half):
        x, y, c, _ = _place()
        blk = outs[a].at[slot, rows(outs[a], half), :]
        return pltpu.make_async_remote_copy(
            src_ref=blk, dst_ref=blk, send_sem=send_sems.at[6 * a + 3 + j], recv_sem=recv_sems.at[6 * a + 3 + j],
            device_id=(x, y, 1 - c), device_id_type=MESH)

    def start(ins, outs, send_sems, recv_sems):
        x, y, c, chips = _place()
        for a in range(n):
            for j, chip_xy in enumerate(chips):
                over_ici(outs, send_sems, recv_sems, a, j, chip_xy, 2 * x + y).start()

    def finish(ins, outs, send_sems, recv_sems):
        x, y, c, chips = _place()
        for a in range(n):
            for j, (cx, cy) in enumerate(chips):
                over_ici(outs, send_sems, recv_sems, a, j, (cx, cy), 2 * cx + cy).wait_recv()
                over_d2d(outs, send_sems, recv_sems, a, j, 2 * cx + cy, c).start()
        for a in range(n):
            for j, (cx, cy) in enumerate(chips):
                over_d2d(outs, send_sems, recv_sems, a, j, 2 * cx + cy, 1 - c).wait_recv()
        for a in range(n):
            for j, (cx, cy) in enumerate(chips):
                over_ici(outs, send_sems, recv_sems, a, j, (cx, cy), 2 * x + y).wait_send()
                over_d2d(outs, send_sems, recv_sems, a, j, 2 * cx + cy, c).wait_send()

    return _Carry(shards, [jax.ShapeDtypeStruct(s.shape, s.dtype) for s in shards],
                  {a: a for a in range(n)}, 6 * n, start, finish)


def _rs_pair(grads, name):
    n = len(grads)

    def body(*refs):
        ins, gots = refs[:n], refs[n:2 * n]
        send_sems, recv_sems = refs[2 * n:]
        x, y, c, _ = _place()
        cps = []
        for a in range(n):
            r2 = ins[a].shape[1] // 2
            cp = pltpu.make_async_remote_copy(
                src_ref=ins[a].at[:, pl.ds((1 - c) * r2, r2), :], dst_ref=gots[a],
                send_sem=send_sems.at[a], recv_sem=recv_sems.at[a],
                device_id=(x, y, 1 - c), device_id_type=MESH)
            cp.start()
            cps.append(cp)
        for cp in cps:
            cp.wait()

    half = [jax.ShapeDtypeStruct((NCHIP, g.shape[1] // 2, g.shape[2]), g.dtype) for g in grads]
    return pl.pallas_call(
        body, name=name, out_shape=half, in_specs=[ANY] * n, out_specs=[ANY] * n,
        scratch_shapes=[pltpu.SemaphoreType.DMA((n,)), pltpu.SemaphoreType.DMA((n,))],
        compiler_params=_cp(),
    )(*grads)


def _chips_carry(parts):
    n = len(parts)

    def send(ins, outs, send_sems, recv_sems, a, j, chip_xy):
        x, y, c, _ = _place()
        me, them = 2 * x + y, 2 * chip_xy[0] + chip_xy[1]
        return pltpu.make_async_remote_copy(
            src_ref=ins[a].at[them], dst_ref=outs[a].at[me - (me > them).astype(jnp.int32)],
            send_sem=send_sems.at[3 * a + j], recv_sem=recv_sems.at[3 * a + j],
            device_id=(*chip_xy, c), device_id_type=MESH)

    def start(ins, outs, send_sems, recv_sems):
        _, _, _, chips = _place()
        for a in range(n):
            for j, chip_xy in enumerate(chips):
                send(ins, outs, send_sems, recv_sems, a, j, chip_xy).start()

    def finish(ins, outs, send_sems, recv_sems):
        x, y, c, chips = _place()
        me = 2 * x + y
        for a in range(n):
            for j, (cx, cy) in enumerate(chips):
                them = 2 * cx + cy
                blk = outs[a].at[them - (them > me).astype(jnp.int32)]
                pltpu.make_async_remote_copy(
                    src_ref=blk, dst_ref=blk, send_sem=send_sems.at[3 * a + j], recv_sem=recv_sems.at[3 * a + j],
                    device_id=(cx, cy, c), device_id_type=MESH).wait_recv()
        for a in range(n):
            for j, chip_xy in enumerate(chips):
                send(ins, outs, send_sems, recv_sems, a, j, chip_xy).wait_send()

    return _Carry(parts, [jax.ShapeDtypeStruct((NCHIP - 1,) + p.shape[1:], p.dtype) for p in parts], {},
                  3 * n, start, finish)


def _rs_swap(fulls):
    n = len(fulls)

    def body(*refs):
        outs = refs[n:2 * n]
        send_sems, recv_sems = refs[2 * n:]
        x, y, c, _ = _place()
        cps = []
        for a in range(n):
            r2 = outs[a].shape[1] // 2
            mine = outs[a].at[:, pl.ds(c * r2, r2), :]
            cp = pltpu.make_async_remote_copy(
                src_ref=mine, dst_ref=mine, send_sem=send_sems.at[a], recv_sem=recv_sems.at[a],
                device_id=(x, y, 1 - c), device_id_type=MESH)
            cp.start()
            cps.append(cp)
        for a in range(n):
            r2 = outs[a].shape[1] // 2
            blk = outs[a].at[:, pl.ds((1 - c) * r2, r2), :]
            pltpu.make_async_remote_copy(
                src_ref=blk, dst_ref=blk, send_sem=send_sems.at[a], recv_sem=recv_sems.at[a],
                device_id=(x, y, 1 - c), device_id_type=MESH).wait_recv()
        for cp in cps:
            cp.wait_send()

    return pl.pallas_call(
        body, name="rs_swap", out_shape=[jax.ShapeDtypeStruct(f.shape, f.dtype) for f in fulls],
        in_specs=[ANY] * n, out_specs=[ANY] * n, input_output_aliases={a: a for a in range(n)},
        scratch_shapes=[pltpu.SemaphoreType.DMA((n,)), pltpu.SemaphoreType.DMA((n,))],
        compiler_params=_cp(),
    )(*fulls)


class _GatherInProj:
    def __init__(self, slot, order):
        self.slot, self.order = slot, order


def _proj_with_gather(h, w_slot, order, name, tn=256):
    S, K = h.shape
    nsh, _, ns = w_slot.shape
    tps = ns // tn
    nt = nsh * tps
    r2 = K // 2

    def body(ord_ref, h_ref, w_in_ref, o_ref, w_ref, wbuf, tile_sems, send_sems, recv_sems):
        n = pl.program_id(0)
        x, y, c, chips = _place()

        def half(slot, which):
            return w_ref.at[slot, pl.ds(which * r2, r2), :]

        def over_ici(j, slot):
            blk = half(slot, c)
            return pltpu.make_async_remote_copy(src_ref=blk, dst_ref=blk, send_sem=send_sems.at[j],
                                                recv_sem=recv_sems.at[j], device_id=(*chips[j], c),
                                                device_id_type=MESH)

        def over_d2d(j, which):
            blk = half(2 * chips[j][0] + chips[j][1], which)
            return pltpu.make_async_remote_copy(src_ref=blk, dst_ref=blk, send_sem=send_sems.at[3 + j],
                                                recv_sem=recv_sems.at[3 + j], device_id=(x, y, 1 - c),
                                                device_id_type=MESH)

        def tile_copy(step, slot):
            shard = ord_ref[step // tps]
            return pltpu.make_async_copy(w_ref.at[shard, :, pl.ds((step % tps) * tn, tn)], wbuf.at[slot],
                                         tile_sems.at[slot])

        @pl.when(n == 0)
        def _():
            for j in range(3):
                over_ici(j, 2 * x + y).start()
            tile_copy(0, 0).start()

        for j in range(3):
            @pl.when(n == (j + 1) * tps - 1)
            def _(j=j):
                over_ici(j, 2 * chips[j][0] + chips[j][1]).wait_recv()
                over_d2d(j, c).start()
                over_d2d(j, 1 - c).wait_recv()

        @pl.when(n + 1 < nt)
        def _():
            tile_copy(n + 1, (n + 1) % 2).start()

        tile_copy(n, n % 2).wait()
        o_ref[...] = jnp.dot(h_ref[...], wbuf[n % 2], preferred_element_type=F32).astype(o_ref.dtype)

        @pl.when(n == nt - 1)
        def _():
            for j in range(3):
                over_ici(j, 2 * x + y).wait_send()
                over_d2d(j, c).wait_send()

    return pl.pallas_call(
        body, name=name,
        out_shape=[jax.ShapeDtypeStruct((S, nsh * ns), BF16), jax.ShapeDtypeStruct(w_slot.shape, w_slot.dtype)],
        grid_spec=pltpu.PrefetchScalarGridSpec(
            num_scalar_prefetch=1, grid=(nt,),
            in_specs=[pl.BlockSpec((S, K), lambda n, o_ref: (0, 0)), ANY],
            out_specs=[pl.BlockSpec((S, tn), lambda n, o_ref: (0, o_ref[n // tps] * tps + n % tps)), ANY],
            scratch_shapes=[pltpu.VMEM((2, K, tn), w_slot.dtype), pltpu.SemaphoreType.DMA((2,)),
                            pltpu.SemaphoreType.DMA((6,)), pltpu.SemaphoreType.DMA((6,))]),
        input_output_aliases={2: 1},
        compiler_params=_cp(("arbitrary",)),
    )(order, h, w_slot)


def _mm_ride(a, b, carry, **kw):
    if carry is None:
        return _mm(a, b, **kw), []
    return _mm(a, b, carry=carry, **kw)


def _layer_fwd(l, x, ada, w, small, ride, target=None):
    shift, scale, gate = ada[:, 0:D], ada[:, D:2 * D], ada[:, 2 * D:3 * D]
    h, h_t = _prenorm_fwd(x, small["g_pre"][l], scale, shift, f"prenorm_fwd{l}")
    carry, landed = ride("proj")
    if isinstance(carry, _GatherInProj):
        proj, full = _proj_with_gather(h, carry.slot, carry.order, f"proj{l}")
        outs = [full]
    else:
        proj, outs = _mm_ride(h, w["w_in"][l], carry, name=f"proj{l}", b_mode="nn_sh", tm=2048, out_dtype=BF16)
    landed(outs)
    a_in, a_in_t = _pool_fwd(proj, small["pool_w"][l], small["pool_scale"][l], f"pool_fwd{l}")
    carry, landed = ride("hgrn")
    (b_in, b_in_t, o_raw, states, mild, cum), outs = _hgrn_fwd(proj, small["lb"][l], small["hgrn_norm_g"][l],
                                                              f"hgrn_fwd{l}", carry=carry)
    landed(outs)
    carry, landed = ride("tail")
    (br_a, br_b, merged_t, y, *x_new), outs = _layer_tail_fwd(
        proj, a_in, b_in, x, w["w_pool_o"][l], w["w_hgrn_o"][l].reshape(D, D), w["w_out"][l].reshape(D, D),
        gate, small["g_post"][l], f"tail_fwd{l}", target=target, carry=carry)
    landed(outs)
    saved = dict(x=x, h_t=h_t, proj=proj, a_in_t=a_in_t, b_in_t=b_in_t, o_raw=o_raw, states=states, mild=mild,
                 cum=cum,
                 br_a=br_a, br_b=br_b, merged_t=merged_t, y=y, scale=scale, gate=gate)
    return x_new, saved


def _layer_bwd(l, dxn, sv, w, small, ride):
    dy, dbr_a, dbr_b, dmg, da_in, db_in, dgate, dg_post = _layer_head_bwd(
        dxn, sv["y"], sv["proj"], sv["br_a"], sv["br_b"], w["w_pool_o"][l], w["w_hgrn_o"][l].reshape(D, D),
        w["w_out"][l].reshape(D, D), sv["gate"], small["g_post"][l], f"head_bwd{l}")
    gw_out = _mm(sv["merged_t"], dy, name=f"gw_out{l}", out_dtype=BF16)
    gw_pool_o = _mm(sv["a_in_t"], dbr_a, name=f"gw_pool_o{l}", out_shards=NCHIP, out_dtype=BF16)
    gw_hgrn_o = _mm(sv["b_in_t"], dbr_b, name=f"gw_hgrn_o{l}", out_dtype=BF16)
    big = dict(w_pool_o=gw_pool_o, w_hgrn_o=gw_hgrn_o.reshape(NCHIP, D // NCHIP, D),
               w_out=gw_out.reshape(NCHIP, D // NCHIP, D))
    carry, landed = ride["hgrn"](big)
    (dhq, dhf, dhi, dhg, dlb, dgn), outs = _hgrn_bwd(db_in, sv["proj"], sv["o_raw"], sv["states"], sv["mild"],
                                                     sv["cum"], small["lb"][l], small["hgrn_norm_g"][l],
                                                     f"hgrn_bwd{l}", carry=carry)
    landed(outs)
    dpv, dpg, dpw, dpsc = _pool_bwd(da_in, sv["proj"], small["pool_w"][l], small["pool_scale"][l],
                                    f"pool_bwd{l}")
    dproj = jnp.concatenate([dpv, dpg, dhq, dhf, dhi, dhg, dmg], axis=1)
    little = dict(dgate=dgate, g_post=dg_post, pool_w=dpw, pool_scale=dpsc, lb=dlb,
                  hgrn_norm_g=jnp.sum(dgn, axis=0, keepdims=True))
    carry, landed = ride["gw_in"](little)
    big["w_in"], outs = _mm_ride(sv["h_t"], dproj, carry, name=f"gw_in{l}", out_shards=NCHIP, out_dtype=BF16)
    landed(outs)
    carry, landed = ride["d_h"](big)
    dh, outs = _mm_ride(dproj, w["w_in"][l], carry, name=f"d_h{l}", b_mode="nt_shk", tn=1024)
    landed(outs)
    dx, dshift, dscale, dg_pre = _prenorm_bwd(dh, dxn, sv["x"], small["g_pre"][l], sv["scale"],
                                              f"prenorm_bwd{l}")
    little.update(dshift=dshift, dscale=dscale, g_pre=dg_pre)
    return dx, big, little


SMALL_ROWS = 176


def _rows8(t):
    t = t.reshape(-1, D)
    return jnp.pad(t, ((0, -t.shape[0] % 8), (0, 0)))


def _pack_small_weights(b_ada, g_pre, g_post, lb_logits, pool_w, pool_scale, hgrn_norm_g):
    gn = jnp.pad(hgrn_norm_g.reshape(1, 2 * HD), ((0, 0), (0, D - 2 * HD)))
    return jnp.concatenate([_rows8(b_ada), _rows8(g_pre), _rows8(g_post), _rows8(lb_logits), _rows8(pool_w),
                            _rows8(pool_scale), _rows8(gn)], axis=0)


def _pack_small(parts):
    both = lambda key: jnp.stack([parts[l][key] for l in range(2)])
    d_ada = jnp.stack([jnp.concatenate([p["dshift"], p["dscale"], p["dgate"]], axis=1) for p in parts])
    return _pack_small_weights(d_ada, both("g_pre"), both("g_post"), both("lb"), both("pool_w"),
                               both("pool_scale"), both("hgrn_norm_g"))


def _unpack_small(p):
    return (p[0:6].reshape(2, 3 * D), p[8:10], p[16:18], p[24:26], p[32:160].reshape(2, GROUPS, 128, 128),
            p[160:161].reshape(2, POOL_W), p[168:169, 0:2 * HD].reshape(2, HD))


def kernel(x, c, w_ada, b_ada, g_pre, g_post, w_in, pool_w, pool_scale, lb_logits, hgrn_norm_g, w_pool_o, w_hgrn_o, w_out, loss_target, m_w_ada, m_b_ada, m_g_pre, m_g_post, m_w_in, m_pool_w, m_pool_scale, m_lb_logits, m_hgrn_norm_g, m_w_pool_o, m_w_hgrn_o, m_w_out, v_w_ada, v_b_ada, v_g_pre, v_g_post, v_w_in, v_pool_w, v_pool_scale, v_lb_logits, v_hgrn_norm_g, v_w_pool_o, v_w_hgrn_o, v_w_out):
    ax, ay, ac = lax.axis_index("x"), lax.axis_index("y"), lax.axis_index("c")
    chip = 2 * ax + ay
    dev = 2 * chip + ac
    xe, te = x[0], loss_target[0]
    ada_s = w_ada.shape[2]

    big_names = ("w_in", "w_pool_o", "w_hgrn_o", "w_out")
    big_w = (w_in, w_pool_o, w_hgrn_o, w_out)
    core = jnp.stack([ac]).astype(jnp.int32)
    place = jnp.stack([chip, ac]).astype(jnp.int32)
    slots = {(k, l): _cast_to_slot(place, t, l, f"cast_{k}{l}") for l in range(2) for k, t in zip(big_names, big_w)}
    w = {k: [None, None] for k in big_names}
    def fills(keys):
        def landed(outs):
            for (k, l), o in zip(keys, outs):
                w[k][l] = slots[k, l] = o
        return landed

    rest0 = [(k, 0) for k in big_names[1:]]
    rest1 = [(k, 1) for k in big_names[1:]]
    no_carry = (None, lambda outs: None)
    order = jnp.stack([chip, 2 * (1 - ax) + ay, 2 * ax + (1 - ay), 2 * (1 - ax) + (1 - ay)]).astype(jnp.int32)

    def ride_fwd0(stage):
        if stage == "proj":
            return _GatherInProj(slots["w_in", 0], order), fills([("w_in", 0)])
        if stage == "hgrn":
            return (_join_carries(_gather_carry([slots[t] for t in rest0]),
                                  _gather_carry([slots["w_in", 1]], piece=(0, 3, 4))),
                    fills(rest0 + [("w_in", 1)]))
        return _gather_carry([slots["w_in", 1]], piece=(3, 1, 4)), fills([("w_in", 1)])

    def ride_fwd1(stage):
        if stage == "hgrn":
            return _gather_carry([slots[t] for t in rest1]), fills(rest1)
        return no_carry

    c_all = _gather_small(jnp.broadcast_to(c, (8, D)), "gather_c").reshape(NDEV, 8, D)[:, 0, :]
    c_pad = jnp.pad(c_all, ((0, ADA_PAD - NDEV), (0, 0)))
    b_sh = lax.dynamic_slice(b_ada, (0, chip * ada_s), (2, ada_s))
    ada_cols = _gather_small(_ada_fwd(c_pad, w_ada, b_sh), "gather_ada")
    ada_cols = ada_cols.reshape(NCHIP, 2, NDEV, 2, ada_s)[:, 0]
    ada_all = jnp.transpose(ada_cols, (2, 1, 0, 3)).reshape(2, NDEV, 3 * D)
    ada_me = lax.dynamic_slice(ada_all, (0, dev, 0), (2, 1, 3 * D))

    lbs = _lb_fwd(lb_logits)
    small = dict(g_pre=g_pre[:, None, :], g_post=g_post[:, None, :], pool_w=pool_w,
                 pool_scale=pool_scale[:, None, :], lb=lbs[:, None, :], hgrn_norm_g=hgrn_norm_g[:, None, :])

    (x1,), sv0 = _layer_fwd(0, xe, ada_me[0], w, small, ride_fwd0)
    (dx2, loss_blk), sv1 = _layer_fwd(1, x1, ada_me[1], w, small, ride_fwd1, target=te)

    parts, recv = {}, {}

    def pair_sums(keys, grads, tag):
        got = _rs_pair(grads, f"rs_pair_{tag}")
        for kl, g, o in zip(keys, grads, got):
            parts[kl] = _pair_add(core, g, o, f"rs_add_{kl[0]}{kl[1]}")

    def exchange(keys):
        def landed(outs):
            recv.update(zip(keys, outs))
        return _chips_carry([parts[kl] for kl in keys]), landed

    def early(l):
        return [(k, l) for k in big_names[1:]]

    def ride_hgrn1(big):
        pair_sums(early(1), [big[k] for k in big_names[1:]], "l1_early")
        return exchange(early(1))

    def ride_d_h1(big):
        pair_sums([("w_in", 1)], [big["w_in"]], "l1_w_in")
        return no_carry

    def ride_hgrn0(big):
        pair_sums(early(0), [big[k] for k in big_names[1:]], "l0_early")
        return exchange([("w_in", 1)] + early(0))

    def ride_d_h0(big):
        pair_sums([("w_in", 0)], [big["w_in"]], "l0_w_in")
        return exchange([("w_in", 0)])

    dx1, big1, little1 = _layer_bwd(1, dx2, sv1, w, small,
                                    dict(hgrn=ride_hgrn1, gw_in=lambda little: no_carry, d_h=ride_d_h1))

    gathered = {}
    zero_row = jnp.zeros((1, D), F32)

    def ride_gw_in0(little):
        so_far = dict(little, dshift=zero_row, dscale=zero_row, g_pre=zero_row)

        def landed(outs):
            (gathered["early"],) = outs
        return _gather_rows_carry(_pack_small([so_far, little1])), landed

    dx0, big0, little0 = _layer_bwd(0, dx1, sv0, w, small,
                                    dict(hgrn=ride_hgrn0, gw_in=ride_gw_in0, d_h=ride_d_h0))
    loss = lax.psum(loss_blk[0, 0], ("x", "y", "c"))
    late = _rows8(jnp.stack([little0["dshift"], little0["dscale"], little0["g_pre"]]))
    late = _gather_small(late, "gather_small_late").reshape(NDEV, 8, D)
    packed = gathered["early"].reshape(NDEV, SMALL_ROWS, D)
    packed = packed.at[:, 0:2, :].set(late[:, 0:2, :]).at[:, 8:9, :].set(late[:, 2:3, :])
    red = []
    for k in big_names:
        both = _chip_sum(place, parts[k, 1], recv[k, 1], 1, None, f"rs_sum_{k}1")
        red.append(_chip_sum(place, parts[k, 0], recv[k, 0], 0, both, f"rs_sum_{k}0"))
    g_big = dict(zip(big_names, _rs_swap(red)))

    def upd(wt, g, m, v, name, carry=None):
        shp = wt.shape
        two = lambda t: t.reshape(-1, shp[-1])
        res = _adamw(two(wt), two(g), two(m), two(v), name, carry)
        return [t.reshape(shp) for t in res[:3]], res[3:]

    u_w_in, _ = upd(w_in, g_big["w_in"], m_w_in, v_w_in, "adamw_w_in")
    g_small = _sum_devices(packed)
    g_b_ada, g_g_pre, g_g_post, g_lb, g_pool_w, g_pool_scale, g_norm_g = _unpack_small(g_small)
    g_lb_logits = _lb_bwd(lb_logits, g_lb)
    d_ada_all = packed[:, 0:6, :].reshape(NDEV, 2, 3 * D)
    d_ada_sh = lax.dynamic_slice(jnp.transpose(d_ada_all, (1, 0, 2)), (0, 0, chip * ada_s), (2, NDEV, ada_s))
    d_ada_sh = jnp.pad(d_ada_sh, ((0, 0), (0, ADA_PAD - NDEV), (0, 0)))
    g_w_ada = _ada_wgrad(c_pad.T, d_ada_sh)

    u_w_ada, _ = upd(w_ada, g_w_ada, m_w_ada, v_w_ada, "adamw_w_ada")
    u_w_pool_o, _ = upd(w_pool_o, g_big["w_pool_o"], m_w_pool_o, v_w_pool_o, "adamw_w_pool_o")
    u_w_hgrn_o, _ = upd(w_hgrn_o, g_big["w_hgrn_o"], m_w_hgrn_o, v_w_hgrn_o, "adamw_w_hgrn_o")
    u_w_out, _ = upd(w_out, g_big["w_out"], m_w_out, v_w_out, "adamw_w_out")
    small_w = dict(b_ada=(b_ada, m_b_ada, v_b_ada), g_pre=(g_pre, m_g_pre, v_g_pre),
                   g_post=(g_post, m_g_post, v_g_post), lb_logits=(lb_logits, m_lb_logits, v_lb_logits),
                   pool_w=(pool_w, m_pool_w, v_pool_w), pool_scale=(pool_scale, m_pool_scale, v_pool_scale),
                   hgrn_norm_g=(hgrn_norm_g, m_hgrn_norm_g, v_hgrn_norm_g))
    in_rows = [tuple(t.reshape(rows, width) for t in small_w[key]) for key, _, rows, width in SMALL_PARTS]
    u_rows = _adamw_small(g_small, g_lb_logits, in_rows)
    u_small = {key: [t.reshape(small_w[key][0].shape) for t in u_rows[p]]
               for p, (key, _, _, _) in enumerate(SMALL_PARTS)}

    grads_out = (g_w_ada, g_b_ada, g_g_pre, g_g_post, g_big["w_in"], g_pool_w, g_pool_scale, g_lb_logits,
                 g_norm_g, g_big["w_pool_o"], g_big["w_hgrn_o"], g_big["w_out"])

    def ordered(k):
        s = lambda key: u_small[key][k]
        return (u_w_ada[k], s("b_ada"), s("g_pre"), s("g_post"), u_w_in[k], s("pool_w"), s("pool_scale"),
                s("lb_logits"), s("hgrn_norm_g"), u_w_pool_o[k], u_w_hgrn_o[k], u_w_out[k])

    return (loss, dx0[None], *grads_out, *ordered(0), *ordered(1), *ordered(2))
```

```python
import functools

import jax
import jax.numpy as jnp
from jax import lax
from jax.experimental import pallas as pl
from jax.experimental.pallas import tpu as pltpu

F32 = jnp.float32
BF16 = jnp.bfloat16
MESH = pl.DeviceIdType.MESH

D = 1024
HEADS = 8
HD = 128
GROUPS = 4
POOL_W = 512
WINDOWS = (2, 4, 8, 16)
CH = 64
SB = 16
NH = 2
IN_W = 7168
NCHIP = 4
NDEV = 8
EPS = 1e-6
PV0, PG0, HQ0, HF0, HI0, HG0 = 0, 4, 8, 16, 24, 32
MGP_BLK, MGH_BLK = 5, 6

LR, B1, B2, AEPS, WD, STEP = 0.001, 0.9, 0.999, 1e-08, 0.01, 10
VMEM_LIMIT = 56 * 1024 * 1024


def _cp(sem=None, **kw):
    if sem is not None:
        kw["dimension_semantics"] = sem
    return pltpu.CompilerParams(vmem_limit_bytes=VMEM_LIMIT, **kw)


def _sig(z):
    return 1.0 / (1.0 + jnp.exp(-z))


def _dsilu(z, s):
    return s * (1.0 + z * (1.0 - s))


def _row_tile(rows, cap):
    if rows <= cap:
        return rows
    t = 1 << (cap.bit_length() - 1)
    while rows % t:
        t //= 2
    return t


ANY = pl.BlockSpec(memory_space=pl.ANY)


class _Carry:
    def __init__(self, ins, outs, aliases, n_sem, start, finish):
        self.ins, self.outs, self.aliases, self.n_sem = list(ins), list(outs), dict(aliases), n_sem
        self.start, self.finish = start, finish


class _SemWindow:
    def __init__(self, ref, base):
        self._ref, self._base = ref, base

    @property
    def at(self):
        return self

    def __getitem__(self, k):
        return self._ref.at[self._base + k]


def _join_carries(*carries):
    ins, outs, aliases, spans, n_sem = [], [], {}, [], 0
    for cr in carries:
        aliases.update({len(ins) + i: len(outs) + o for i, o in cr.aliases.items()})
        spans.append((len(ins), len(cr.ins), len(outs), len(cr.outs), n_sem))
        ins, outs, n_sem = ins + cr.ins, outs + cr.outs, n_sem + cr.n_sem

    def run(which):
        def fn(i_refs, o_refs, send_sems, recv_sems):
            for cr, (i0, ni, o0, no, s0) in zip(carries, spans):
                getattr(cr, which)(i_refs[i0:i0 + ni], o_refs[o0:o0 + no], _SemWindow(send_sems, s0),
                                   _SemWindow(recv_sems, s0))
        return fn

    return _Carry(ins, outs, aliases, n_sem, run("start"), run("finish"))


def _call(body, *, name, grid, in_specs, out_specs, out_shape, args, scratch_shapes=(), sem=None, carry=None):
    in_specs, out_specs, out_shape = list(in_specs), list(out_specs), list(out_shape)
    scratch_shapes = list(scratch_shapes)
    if carry is None:
        outs = pl.pallas_call(body, name=name, grid=grid, in_specs=in_specs, out_specs=out_specs,
                              out_shape=out_shape, scratch_shapes=scratch_shapes,
                              compiler_params=_cp(sem))(*args)
        return list(outs)
    n_in, n_out, n_scr = len(in_specs), len(out_specs), len(scratch_shapes)
    c_in, c_out = len(carry.ins), len(carry.outs)

    def wrapped(*refs):
        k_in, rest = refs[:n_in], refs[n_in:]
        ci, rest = rest[:c_in], rest[c_in:]
        k_out, rest = rest[:n_out], rest[n_out:]
        co, rest = rest[:c_out], rest[c_out:]
        k_scr, (ssem, rsem) = rest[:n_scr], rest[n_scr:]
        pids = [pl.program_id(d) for d in range(len(grid))]
        first = functools.reduce(jnp.logical_and, [p == 0 for p in pids])
        last = functools.reduce(jnp.logical_and, [p == g - 1 for p, g in zip(pids, grid)])

        @pl.when(first)
        def _():
            carry.start(ci, co, ssem, rsem)

        body(*k_in, *k_out, *k_scr)

        @pl.when(last)
        def _():
            carry.finish(ci, co, ssem, rsem)

    outs = pl.pallas_call(
        wrapped, name=name, grid=grid, in_specs=in_specs + [ANY] * c_in, out_specs=out_specs + [ANY] * c_out,
        out_shape=out_shape + carry.outs,
        input_output_aliases={n_in + i: n_out + o for i, o in carry.aliases.items()},
        scratch_shapes=scratch_shapes + [pltpu.SemaphoreType.DMA((carry.n_sem,))] * 2,
        compiler_params=_cp(("arbitrary",) * len(grid)),
    )(*args, *carry.ins)
    return list(outs)


def _run_carry(carry, name):
    c_in, c_out = len(carry.ins), len(carry.outs)

    def body(*refs):
        ci, co, (ssem, rsem) = refs[:c_in], refs[c_in:c_in + c_out], refs[c_in + c_out:]
        carry.start(ci, co, ssem, rsem)
        carry.finish(ci, co, ssem, rsem)

    outs = pl.pallas_call(
        body, name=name, in_specs=[ANY] * c_in, out_specs=[ANY] * c_out, out_shape=carry.outs,
        input_output_aliases=carry.aliases,
        scratch_shapes=[pltpu.SemaphoreType.DMA((carry.n_sem,))] * 2, compiler_params=_cp(),
    )(*carry.ins)
    return list(outs)


def _mm(a, b, *, name, b_mode="nn", out_shards=0, tm=1024, tn=256, tk=None, out_dtype=F32, carry=None):
    M, K = a.shape
    if b_mode == "nn":
        N = b.shape[1]
    elif b_mode == "nt":
        N = b.shape[0]
    elif b_mode == "nn_sh":
        N = b.shape[0] * b.shape[2]
    else:
        N = b.shape[1]
    tm = _row_tile(M, tm)
    if b_mode == "nn_sh":
        tn = _row_tile(b.shape[2], tn)
    elif out_shards:
        tn = _row_tile(N // out_shards, tn)
    else:
        tn = _row_tile(N, tn)
    if tk is None:
        tk = K if b_mode != "nt_shk" else b.shape[2]
    if b_mode == "nt_shk":
        tk = _row_tile(b.shape[2], tk)
    nm, nn, nk = M // tm, N // tn, K // tk

    a_spec = pl.BlockSpec((tm, tk), lambda m, n, k: (m, k))
    if b_mode == "nn":
        b_spec = pl.BlockSpec((tk, tn), lambda m, n, k: (k, n))
    elif b_mode == "nt":
        b_spec = pl.BlockSpec((tn, tk), lambda m, n, k: (n, k))
    elif b_mode == "nn_sh":
        nps = b.shape[2] // tn
        b_spec = pl.BlockSpec((None, tk, tn), lambda m, n, k: (n // nps, k, n % nps))
    else:
        kps = b.shape[2] // tk
        b_spec = pl.BlockSpec((None, tn, tk), lambda m, n, k: (k // kps, n, k % kps))
    if out_shards:
        ops = (N // out_shards) // tn
        o_spec = pl.BlockSpec((None, tm, tn), lambda m, n, k: (n // ops, m, n % ops))
        o_shape = jax.ShapeDtypeStruct((out_shards, M, N // out_shards), out_dtype)
    else:
        o_spec = pl.BlockSpec((tm, tn), lambda m, n, k: (m, n))
        o_shape = jax.ShapeDtypeStruct((M, N), out_dtype)
    trans_b = b_mode in ("nt", "nt_shk")
    dn = (((1,), (1,)), ((), ())) if trans_b else (((1,), (0,)), ((), ()))

    def body(a_ref, b_ref, o_ref, acc_ref):
        k = pl.program_id(2)

        @pl.when(k == 0)
        def _():
            acc_ref[...] = jnp.zeros(acc_ref.shape, F32)

        acc_ref[...] += lax.dot_general(a_ref[...].astype(BF16), b_ref[...].astype(BF16), dn,
                                        preferred_element_type=F32)

        @pl.when(k == nk - 1)
        def _():
            o_ref[...] = acc_ref[...].astype(o_ref.dtype)

    outs = _call(body, name=name, grid=(nm, nn, nk), in_specs=[a_spec, b_spec], out_specs=[o_spec],
                 out_shape=[o_shape], scratch_shapes=[pltpu.VMEM((tm, tn), F32)],
                 sem=("parallel", "parallel", "arbitrary"), args=(a, b), carry=carry)
    return outs[0] if carry is None else (outs[0], outs[1:])


def _rowvec(n=D):
    return pl.BlockSpec((1, n), lambda i: (0, 0))


def _prenorm_fwd(x, g, scale, shift, name, carry=None):
    S = x.shape[0]
    tr = _row_tile(S, 256)

    def body(x_ref, g_ref, sc_ref, sh_ref, h_ref, ht_ref):
        xv = x_ref[...]
        r = lax.rsqrt(jnp.mean(xv * xv, axis=-1, keepdims=True) + EPS)
        hv = (xv * r) * g_ref[...] * (1.0 + sc_ref[...]) + sh_ref[...]
        h_ref[...] = hv.astype(BF16)
        ht_ref[...] = hv.T.astype(BF16)

    outs = _call(
        body, name=name, grid=(S // tr,),
        in_specs=[pl.BlockSpec((tr, D), lambda i: (i, 0)), _rowvec(), _rowvec(), _rowvec()],
        out_specs=[pl.BlockSpec((tr, D), lambda i: (i, 0)), pl.BlockSpec((D, tr), lambda i: (0, i))],
        out_shape=[jax.ShapeDtypeStruct((S, D), BF16), jax.ShapeDtypeStruct((D, S), BF16)],
        sem=("parallel",), args=(x, g, scale, shift), carry=carry)
    return outs[:2], outs[2:]


def _prenorm_bwd(dh, dxn, x, g, scale, name, carry=None):
    S = x.shape[0]
    tr = _row_tile(S, 256)

    def body(dh_ref, dxn_ref, x_ref, g_ref, sc_ref, dx_ref, dsh_ref, dsc_ref, dg_ref):
        i = pl.program_id(0)

        @pl.when(i == 0)
        def _():
            dsh_ref[...] = jnp.zeros((1, D), F32)
            dsc_ref[...] = jnp.zeros((1, D), F32)
            dg_ref[...] = jnp.zeros((1, D), F32)

        xv = x_ref[...]
        dhv = dh_ref[...]
        gv = g_ref[...]
        mod = 1.0 + sc_ref[...]
        r = lax.rsqrt(jnp.mean(xv * xv, axis=-1, keepdims=True) + EPS)
        xh = xv * r
        dsh_ref[...] += jnp.sum(dhv, axis=0, keepdims=True)
        dsc_ref[...] += jnp.sum(dhv * (xh * gv), axis=0, keepdims=True)
        dg_ref[...] += jnp.sum(dhv * mod * xh, axis=0, keepdims=True)
        u = dhv * mod * gv
        dx_ref[...] = dxn_ref[...] + r * u - xv * (r * r * r) * jnp.mean(u * xv, axis=-1, keepdims=True)

    tile = pl.BlockSpec((tr, D), lambda i: (i, 0))
    outs = _call(
        body, name=name, grid=(S // tr,),
        in_specs=[tile, tile, tile, _rowvec(), _rowvec()],
        out_specs=[tile, _rowvec(), _rowvec(), _rowvec()],
        out_shape=[jax.ShapeDtypeStruct((S, D), F32)] + [jax.ShapeDtypeStruct((1, D), F32)] * 3,
        sem=("arbitrary",), args=(dh, dxn, x, g, scale), carry=carry)
    return outs[:4], outs[4:]


def _postnorm_fwd(x, y, gate, g, name):
    S = x.shape[0]
    tr = _row_tile(S, 256)

    def body(x_ref, y_ref, gate_ref, g_ref, o_ref):
        yv = y_ref[...]
        r = lax.rsqrt(jnp.mean(yv * yv, axis=-1, keepdims=True) + EPS)
        o_ref[...] = x_ref[...] + gate_ref[...] * ((yv * r) * g_ref[...])

    tile = pl.BlockSpec((tr, D), lambda i: (i, 0))
    return pl.pallas_call(
        body, name=name, grid=(S // tr,), in_specs=[tile, tile, _rowvec(), _rowvec()],
        out_specs=tile, out_shape=jax.ShapeDtypeStruct((S, D), F32), compiler_params=_cp(("parallel",)),
    )(x, y, gate, g)


def _postnorm_bwd(dxn, y, gate, g, name):
    S = y.shape[0]
    tr = _row_tile(S, 256)

    def body(dxn_ref, y_ref, gate_ref, g_ref, dy_ref, dgate_ref, dg_ref):
        i = pl.program_id(0)

        @pl.when(i == 0)
        def _():
            dgate_ref[...] = jnp.zeros((1, D), F32)
            dg_ref[...] = jnp.zeros((1, D), F32)

        yv = y_ref[...]
        dv = dxn_ref[...]
        gv = g_ref[...]
        gt = gate_ref[...]
        r = lax.rsqrt(jnp.mean(yv * yv, axis=-1, keepdims=True) + EPS)
        yh = yv * r
        dgate_ref[...] += jnp.sum(dv * (yh * gv), axis=0, keepdims=True)
        dg_ref[...] += jnp.sum(dv * gt * yh, axis=0, keepdims=True)
        u = dv * gt * gv
        dy_ref[...] = (r * u - yv * (r * r * r) * jnp.mean(u * yv, axis=-1, keepdims=True)).astype(BF16)

    tile = pl.BlockSpec((tr, D), lambda i: (i, 0))
    return pl.pallas_call(
        body, name=name, grid=(S // tr,), in_specs=[tile, tile, _rowvec(), _rowvec()],
        out_specs=[tile, _rowvec(), _rowvec()],
        out_shape=[jax.ShapeDtypeStruct((S, D), BF16), jax.ShapeDtypeStruct((1, D), F32),
                   jax.ShapeDtypeStruct((1, D), F32)],
        compiler_params=_cp(("arbitrary",)),
    )(dxn, y, gate, g)


def _loss_head(xo, target, name):
    S = xo.shape[0]
    tr = _row_tile(S, 256)

    def body(x_ref, t_ref, dx_ref, l_ref):
        i = pl.program_id(0)

        @pl.when(i == 0)
        def _():
            l_ref[...] = jnp.zeros((8, 128), F32)

        err = x_ref[...] - t_ref[...]
        dx_ref[...] = err * (1.0 / D)
        l_ref[...] += 0.5 * jnp.sum(jnp.mean(err * err, axis=-1, keepdims=True))

    tile = pl.BlockSpec((tr, D), lambda i: (i, 0))
    return pl.pallas_call(
        body, name=name, grid=(S // tr,), in_specs=[tile, tile],
        out_specs=[tile, pl.BlockSpec((8, 128), lambda i: (0, 0))],
        out_shape=[jax.ShapeDtypeStruct((S, D), F32), jax.ShapeDtypeStruct((8, 128), F32)],
        compiler_params=_cp(("arbitrary",)),
    )(xo, target)


def _layer_tail_fwd(proj, a_in, b_in, x, w_po, w_ho, w_out, gate, g, name, target=None, carry=None):
    S = proj.shape[0]
    tr = _row_tile(S, 256)
    nsh, _, wsh = w_po.shape
    n_in = 10 + (target is not None)

    def body(*refs):
        (mgp_ref, mgh_ref, a_ref, b_ref, x_ref, wpo_ref, who_ref, wout_ref, gate_ref, g_ref) = refs[:10]
        bra_ref, brb_ref, mt_ref, y_ref, xn_ref = refs[n_in:n_in + 5]
        av = a_ref[...]
        bra = jnp.concatenate([jnp.dot(av, wpo_ref[j], preferred_element_type=F32) for j in range(nsh)], axis=1)
        brb = jnp.dot(b_ref[...], who_ref[...], preferred_element_type=F32)
        mv = _sig(mgp_ref[...].astype(F32)) * bra + _sig(mgh_ref[...].astype(F32)) * brb
        bra_ref[...] = bra.astype(BF16)
        brb_ref[...] = brb.astype(BF16)
        mt_ref[...] = mv.T.astype(BF16)
        yv = jnp.dot(mv.astype(BF16), wout_ref[...], preferred_element_type=F32)
        y_ref[...] = yv
        r = lax.rsqrt(jnp.mean(yv * yv, axis=-1, keepdims=True) + EPS)
        xn = x_ref[...] + gate_ref[...] * ((yv * r) * g_ref[...])
        if target is None:
            xn_ref[...] = xn
        else:
            t_ref, l_ref = refs[10], refs[n_in + 5]

            @pl.when(pl.program_id(0) == 0)
            def _():
                l_ref[...] = jnp.zeros((8, 128), F32)

            err = xn - t_ref[...]
            xn_ref[...] = err * (1.0 / D)
            l_ref[...] += 0.5 * jnp.sum(jnp.mean(err * err, axis=-1, keepdims=True))

    tile = pl.BlockSpec((tr, D), lambda i: (i, 0))
    whole = lambda t: pl.BlockSpec(t.shape, lambda i: (0,) * t.ndim)
    last = target is not None
    outs = _call(
        body, name=name, grid=(S // tr,),
        in_specs=[pl.BlockSpec((tr, D), lambda i: (i, MGP_BLK)), pl.BlockSpec((tr, D), lambda i: (i, MGH_BLK)),
                  pl.BlockSpec((tr, POOL_W), lambda i: (i, 0)), tile, tile, whole(w_po), whole(w_ho),
                  whole(w_out), _rowvec(), _rowvec()] + [tile] * last,
        out_specs=[tile, tile, pl.BlockSpec((D, tr), lambda i: (0, i)), tile, tile]
        + [pl.BlockSpec((8, 128), lambda i: (0, 0))] * last,
        out_shape=[jax.ShapeDtypeStruct((S, D), BF16), jax.ShapeDtypeStruct((S, D), BF16),
                   jax.ShapeDtypeStruct((D, S), BF16), jax.ShapeDtypeStruct((S, D), F32),
                   jax.ShapeDtypeStruct((S, D), F32)] + [jax.ShapeDtypeStruct((8, 128), F32)] * last,
        sem=("arbitrary",) if last else ("parallel",),
        args=(proj, proj, a_in, b_in, x, w_po, w_ho, w_out, gate, g) + ((target,) if last else ()), carry=carry)
    return outs[:5 + last], outs[5 + last:]


def _layer_head_bwd(dxn, y, proj, br_a, br_b, w_po, w_ho, w_out, gate, g, name):
    S = y.shape[0]
    tr = _row_tile(S, 256)
    nsh, _, wsh = w_po.shape

    def body(dxn_ref, y_ref, mgp_ref, mgh_ref, bra_ref, brb_ref, wpo_ref, who_ref, wout_ref, gate_ref, g_ref,
             dy_ref, dba_ref, dbb_ref, dmg_ref, dain_ref, dbin_ref, dgate_ref, dg_ref):
        i = pl.program_id(0)

        @pl.when(i == 0)
        def _():
            dgate_ref[...] = jnp.zeros((1, D), F32)
            dg_ref[...] = jnp.zeros((1, D), F32)

        yv = y_ref[...]
        dv = dxn_ref[...]
        gv = g_ref[...]
        gt = gate_ref[...]
        r = lax.rsqrt(jnp.mean(yv * yv, axis=-1, keepdims=True) + EPS)
        yh = yv * r
        dgate_ref[...] += jnp.sum(dv * (yh * gv), axis=0, keepdims=True)
        dg_ref[...] += jnp.sum(dv * gt * yh, axis=0, keepdims=True)
        u = dv * gt * gv
        dy = (r * u - yv * (r * r * r) * jnp.mean(u * yv, axis=-1, keepdims=True)).astype(BF16)
        dy_ref[...] = dy
        dm = _dot_nt(dy, wout_ref[...])
        sp = _sig(mgp_ref[...].astype(F32))
        sh = _sig(mgh_ref[...].astype(F32))
        dba = (dm * sp).astype(BF16)
        dbb = (dm * sh).astype(BF16)
        dba_ref[...] = dba
        dbb_ref[...] = dbb
        dmg_ref[:, 0:D] = (dm * bra_ref[...].astype(F32) * sp * (1.0 - sp)).astype(BF16)
        dmg_ref[:, D:2 * D] = (dm * brb_ref[...].astype(F32) * sh * (1.0 - sh)).astype(BF16)
        dain = _dot_nt(dba[:, 0:wsh], wpo_ref[0])
        for j in range(1, nsh):
            dain = dain + _dot_nt(dba[:, j * wsh:(j + 1) * wsh], wpo_ref[j])
        dain_ref[...] = dain
        dbin_ref[...] = _dot_nt(dbb, who_ref[...])

    tile = pl.BlockSpec((tr, D), lambda i: (i, 0))
    whole = lambda t: pl.BlockSpec(t.shape, lambda i: (0,) * t.ndim)
    return pl.pallas_call(
        body, name=name, grid=(S // tr,),
        in_specs=[tile, tile, pl.BlockSpec((tr, D), lambda i: (i, MGP_BLK)),
                  pl.BlockSpec((tr, D), lambda i: (i, MGH_BLK)), tile, tile, whole(w_po), whole(w_ho),
                  whole(w_out), _rowvec(), _rowvec()],
        out_specs=[tile, tile, tile, pl.BlockSpec((tr, 2 * D), lambda i: (i, 0)),
                   pl.BlockSpec((tr, POOL_W), lambda i: (i, 0)), tile, _rowvec(), _rowvec()],
        out_shape=[jax.ShapeDtypeStruct((S, D), BF16)] * 3
        + [jax.ShapeDtypeStruct((S, 2 * D), BF16), jax.ShapeDtypeStruct((S, POOL_W), F32),
           jax.ShapeDtypeStruct((S, D), F32), jax.ShapeDtypeStruct((1, D), F32), jax.ShapeDtypeStruct((1, D), F32)],
        compiler_params=_cp(("arbitrary",)),
    )(dxn, y, proj, proj, br_a, br_b, w_po, w_ho, w_out, gate, g)


def _merge_fwd(proj, br_a, br_b, name):
    S = proj.shape[0]
    tr = _row_tile(S, 256)

    def body(mgp_ref, mgh_ref, a_ref, b_ref, o_ref, ot_ref):
        mv = _sig(mgp_ref[...]) * a_ref[...] + _sig(mgh_ref[...]) * b_ref[...]
        o_ref[...] = mv.astype(BF16)
        ot_ref[...] = mv.T.astype(BF16)

    tile = pl.BlockSpec((tr, D), lambda i: (i, 0))
    return pl.pallas_call(
        body, name=name, grid=(S // tr,),
        in_specs=[pl.BlockSpec((tr, D), lambda i: (i, MGP_BLK)), pl.BlockSpec((tr, D), lambda i: (i, MGH_BLK)),
                  tile, tile],
        out_specs=[tile, pl.BlockSpec((D, tr), lambda i: (0, i))],
        out_shape=[jax.ShapeDtypeStruct((S, D), BF16), jax.ShapeDtypeStruct((D, S), BF16)],
        compiler_params=_cp(("parallel",)),
    )(proj, proj, br_a, br_b)


def _merge_bwd(dm, proj, br_a, br_b, name):
    S = proj.shape[0]
    tr = _row_tile(S, 256)

    def body(dm_ref, mgp_ref, mgh_ref, a_ref, b_ref, da_ref, db_ref, dmg_ref):
        dmv = dm_ref[...]
        sp = _sig(mgp_ref[...])
        sh = _sig(mgh_ref[...])
        da_ref[...] = (dmv * sp).astype(BF16)
        db_ref[...] = (dmv * sh).astype(BF16)
        dmg_ref[:, 0:D] = (dmv * a_ref[...] * sp * (1.0 - sp)).astype(BF16)
        dmg_ref[:, D:2 * D] = (dmv * b_ref[...] * sh * (1.0 - sh)).astype(BF16)

    tile = pl.BlockSpec((tr, D), lambda i: (i, 0))
    return pl.pallas_call(
        body, name=name, grid=(S // tr,),
        in_specs=[tile, pl.BlockSpec((tr, D), lambda i: (i, MGP_BLK)),
                  pl.BlockSpec((tr, D), lambda i: (i, MGH_BLK)), tile, tile],
        out_specs=[tile, tile, pl.BlockSpec((tr, 2 * D), lambda i: (i, 0))],
        out_shape=[jax.ShapeDtypeStruct((S, D), BF16), jax.ShapeDtypeStruct((S, D), BF16),
                   jax.ShapeDtypeStruct((S, 2 * D), BF16)],
        compiler_params=_cp(("parallel",)),
    )(dm, proj, proj, br_a, br_b)


def _pool_pieces(u, g, S):
    rowi = lax.broadcasted_iota(jnp.int32, (S, 1), 0)

    def down(z, k):
        return jnp.where(rowi >= k, pltpu.roll(z, k, axis=0), 0.0)

    s2 = u + down(u, 1)
    s4 = s2 + down(s2, 2)
    s8 = s4 + down(s4, 4)
    s16 = s8 + down(s8, 8)
    win = jnp.where(g == 0, s2, jnp.where(g == 1, s4, jnp.where(g == 2, s8, s16)))
    w = jnp.where(g == 0, 2, jnp.where(g == 1, 4, jnp.where(g == 2, 8, 16)))
    count = jnp.minimum(rowi + 1, w).astype(F32)
    return win / count - u, count, rowi


def _pool_fwd(proj, pw, pscale, name):
    S = proj.shape[0]

    def body(pv_ref, pg_ref, pw_ref, sc_ref, a_ref, at_ref):
        g = pl.program_id(0)
        pooled, _, _ = _pool_pieces(pv_ref[...].astype(F32), g, S)
        pm = jnp.dot(pooled.astype(BF16), pw_ref[...].astype(BF16), preferred_element_type=F32)
        pgv = pg_ref[...].astype(F32)
        av = pm * sc_ref[...] * (pgv * _sig(pgv))
        a_ref[...] = av.astype(BF16)
        at_ref[...] = av.T.astype(BF16)

    return pl.pallas_call(
        body, name=name, grid=(GROUPS,),
        in_specs=[pl.BlockSpec((S, 128), lambda g: (0, PV0 + g)), pl.BlockSpec((S, 128), lambda g: (0, PG0 + g)),
                  pl.BlockSpec((None, 128, 128), lambda g: (g, 0, 0)), pl.BlockSpec((1, 128), lambda g: (0, g))],
        out_specs=[pl.BlockSpec((S, 128), lambda g: (0, g)), pl.BlockSpec((128, S), lambda g: (g, 0))],
        out_shape=[jax.ShapeDtypeStruct((S, POOL_W), BF16), jax.ShapeDtypeStruct((POOL_W, S), BF16)],
        compiler_params=_cp(("parallel",)),
    )(proj, proj, pw, pscale)


def _pool_bwd(da, proj, pw, pscale, name):
    S = proj.shape[0]

    def body(da_ref, pv_ref, pg_ref, pw_ref, sc_ref, dpv_ref, dpg_ref, dpw_ref, dsc_ref):
        g = pl.program_id(0)
        pooled, count, rowi = _pool_pieces(pv_ref[...].astype(F32), g, S)
        pwb = pw_ref[...].astype(BF16)
        pm = jnp.dot(pooled.astype(BF16), pwb, preferred_element_type=F32)
        scv = sc_ref[...]
        pgv = pg_ref[...].astype(F32)
        sg = _sig(pgv)
        dav = da_ref[...]
        d_ps = dav * (pgv * sg)
        dpg_ref[...] = (dav * (pm * scv) * _dsilu(pgv, sg)).astype(BF16)
        dsc_ref[...] = jnp.sum(d_ps * pm, axis=0, keepdims=True)
        d_pm = (d_ps * scv).astype(BF16)
        dpw_ref[...] = lax.dot_general(pooled.astype(BF16), d_pm, (((0,), (0,)), ((), ())),
                                       preferred_element_type=F32)
        d_pooled = lax.dot_general(d_pm, pwb, (((1,), (1,)), ((), ())), preferred_element_type=F32)
        z = d_pooled / count

        def up(v, k):
            return jnp.where(rowi < S - k, pltpu.roll(v, S - k, axis=0), 0.0)

        t2 = z + up(z, 1)
        t4 = t2 + up(t2, 2)
        t8 = t4 + up(t4, 4)
        t16 = t8 + up(t8, 8)
        adj = jnp.where(g == 0, t2, jnp.where(g == 1, t4, jnp.where(g == 2, t8, t16)))
        dpv_ref[...] = (adj - d_pooled).astype(BF16)

    col = lambda g: (0, g)
    return pl.pallas_call(
        body, name=name, grid=(GROUPS,),
        in_specs=[pl.BlockSpec((S, 128), col), pl.BlockSpec((S, 128), lambda g: (0, PV0 + g)),
                  pl.BlockSpec((S, 128), lambda g: (0, PG0 + g)),
                  pl.BlockSpec((None, 128, 128), lambda g: (g, 0, 0)), pl.BlockSpec((1, 128), col)],
        out_specs=[pl.BlockSpec((S, 128), col), pl.BlockSpec((S, 128), col),
                   pl.BlockSpec((None, 128, 128), lambda g: (g, 0, 0)), pl.BlockSpec((1, 128), col)],
        out_shape=[jax.ShapeDtypeStruct((S, POOL_W), BF16), jax.ShapeDtypeStruct((S, POOL_W), BF16),
                   jax.ShapeDtypeStruct((GROUPS, 128, 128), F32), jax.ShapeDtypeStruct((1, POOL_W), F32)],
        compiler_params=_cp(("parallel",)),
    )(da, proj, proj, pw, pscale)


def _chunk_cumsum(z, rowi):
    for sh in (1, 2, 4, 8, 16, 32):
        z = z + jnp.where(rowi >= sh, pltpu.roll(z, sh, axis=0), 0.0)
    return z


def _chunk_rev_cumsum(z, rowi):
    for sh in (1, 2, 4, 8, 16, 32):
        z = z + jnp.where(rowi < CH - sh, pltpu.roll(z, CH - sh, axis=0), 0.0)
    return z


def _dot_nn(a, b):
    return jnp.dot(a.astype(BF16), b.astype(BF16), preferred_element_type=F32)


def _dot_nt(a, b):
    return lax.dot_general(a.astype(BF16), b.astype(BF16), (((1,), (1,)), ((), ())), preferred_element_type=F32)


def _dot_tn(a, b):
    return lax.dot_general(a.astype(BF16), b.astype(BF16), (((0,), (0,)), ((), ())), preferred_element_type=F32)


def _gates(hq, hf, lbv):
    hq, hf = hq.astype(F32), hf.astype(F32)
    sq = _sig(hq)
    sf = _sig(hf)
    f = lbv + (1.0 - lbv) * sf
    fc = jnp.maximum(f, 1e-30)
    return hq * sq, sq, sf, f, fc, jnp.log(fc)


DECAY_CAP = 60.0


def _block_ref(c_ref, i):
    if i == 0:
        return jnp.zeros((1, HD), F32)
    return c_ref[SB * i - 1:SB * i, :]


def _block_decay(c_ref):
    spans = [_block_ref(c_ref, i) - c_ref[SB * (i + 1) - 1:SB * (i + 1), :] for i in range(CH // SB)]
    return functools.reduce(jnp.maximum, spans)


def _hgrn_fwd(proj, lb, gn, name, carry=None):
    S = proj.shape[0]
    nch = S // CH
    W = NH * HD

    def body(hq_ref, hf_ref, hi_ref, hg_ref, lb_ref, gn_ref, bin_ref, bint_ref, oraw_ref, st_ref, mild_ref,
             cum_ref, q_s, k_s, c_s, v_s, o_s, state_s, qf_s, kf_s, cf_s):
        state_s[...] = jnp.zeros((NH, HD, HD), F32)
        rowi = lax.broadcasted_iota(jnp.int32, (CH, 1), 0)
        coli = lax.broadcasted_iota(jnp.int32, (1, CH), 1)
        sbi = lax.broadcasted_iota(jnp.int32, (SB, 1), 0)
        gnv = gn_ref[...]

        def gates_pass(n, worst):
            rows = pl.ds(pl.multiple_of(n * CH, CH), CH)
            for hh in range(NH):
                lanes = slice(hh * HD, (hh + 1) * HD)
                q, _, _, f, _, logf = _gates(hq_ref[rows, lanes], hf_ref[rows, lanes], lb_ref[:, lanes])
                c = _chunk_cumsum(logf, rowi)
                qf_s[hh, rows, :] = q
                kf_s[hh, rows, :] = 1.0 - f
                cf_s[hh, rows, :] = c
                cum_ref[rows, lanes] = c
                c_s[hh] = c
                worst = jnp.maximum(worst, _block_decay(c_s.at[hh]))
            return worst

        def between_chunks(hh, n, rows):
            lanes = slice(hh * HD, (hh + 1) * HD)
            q = qf_s[hh, rows, :]
            k = kf_s[hh, rows, :]
            c = cf_s[hh, rows, :]
            v = hi_ref[rows, lanes].astype(F32)
            q_s[hh] = q
            k_s[hh] = k
            c_s[hh] = c
            v_s[hh] = v
            st = state_s[hh]
            st_ref[hh, n] = st.astype(BF16)
            o_s[hh] = _dot_nt(q * jnp.exp(c), st)
            last = c_s[hh, CH - 1:CH, :]
            state_s[hh] = st * jnp.exp(last) + _dot_tn(v, k * jnp.exp(last - c))

        def within_chunk_matmul(hh):
            q, k, c, v = q_s[hh], k_s[hh], c_s[hh], v_s[hh]
            a = jnp.zeros((CH, CH), F32)
            for i in range(CH // SB):
                r_i = _block_ref(c_s.at[hh], i)
                qi = q * jnp.exp(jnp.minimum(c - r_i, 0.0))
                kei = k * jnp.exp(jnp.minimum(r_i - c, DECAY_CAP))
                m_i = (rowi >= SB * i) & (rowi < SB * (i + 1)) & (coli <= rowi)
                a = a + jnp.where(m_i, _dot_nt(qi, kei), 0.0)
            o_s[hh] += _dot_nn(a, v)

        def within_chunk_exact(hh):
            q, k, c, v = q_s[hh], k_s[hh], c_s[hh], v_s[hh]
            a_off = jnp.zeros((CH, CH), F32)
            for i in range(1, CH // SB):
                r_i = _block_ref(c_s.at[hh], i)
                qi = q * jnp.exp(jnp.minimum(c - r_i, 0.0))
                kei = k * jnp.exp(jnp.minimum(r_i - c, 0.0))
                m_i = (rowi >= SB * i) & (rowi < SB * (i + 1)) & (coli < SB * i)
                a_off = a_off + jnp.where(m_i, _dot_nt(qi, kei), 0.0)
            o_s[hh] += _dot_nn(a_off, v)
            for i in range(CH // SB):
                blk = slice(SB * i, SB * (i + 1))
                qb = q_s[hh, blk, :]
                cb = c_s[hh, blk, :]
                acc = jnp.zeros((SB, HD), F32)
                for s in range(SB):
                    row = SB * i + s
                    w = jnp.exp(jnp.minimum(cb - c_s[hh, row:row + 1, :], 0.0))
                    a_col = jnp.sum(qb * k_s[hh, row:row + 1, :] * w, axis=-1, keepdims=True)
                    acc = acc + jnp.where(sbi >= s, a_col, 0.0) * v_s[hh, row:row + 1, :]
                o_s[hh, blk, :] += acc

        def norm_and_gate(hh, rows):
            lanes = slice(hh * HD, (hh + 1) * HD)
            ov = o_s[hh]
            oraw_ref[rows, lanes] = ov
            r = lax.rsqrt(jnp.mean(ov * ov, axis=-1, keepdims=True) + EPS)
            hg = hg_ref[rows, lanes].astype(F32)
            bin_ref[rows, lanes] = ((ov * r) * gnv * (hg * _sig(hg))).astype(BF16)

        def chunk_with(within_chunk):
            def chunk(n, carry):
                rows = pl.ds(pl.multiple_of(n * CH, CH), CH)
                for hh in range(NH):
                    between_chunks(hh, n, rows)
                for hh in range(NH):
                    within_chunk(hh)
                for hh in range(NH):
                    norm_and_gate(hh, rows)
                return carry
            return chunk

        worst = lax.fori_loop(0, nch, gates_pass, jnp.zeros((1, HD), F32))
        mild = jnp.max(worst) <= DECAY_CAP
        mild_ref[...] = jnp.broadcast_to(jnp.where(mild, 1.0, 0.0), (8, HD))

        @pl.when(mild)
        def _():
            lax.fori_loop(0, nch, chunk_with(within_chunk_matmul), 0, unroll=4)

        @pl.when(jnp.logical_not(mild))
        def _():
            lax.fori_loop(0, nch, chunk_with(within_chunk_exact), 0)

        bint_ref[...] = bin_ref[...].astype(F32).T.astype(BF16)

    col = lambda off: pl.BlockSpec((S, W), lambda h: (0, off // NH + h))
    head = pl.BlockSpec((S, W), lambda h: (0, h))
    outs = _call(
        body, name=name, grid=(HEADS // NH,),
        in_specs=[col(HQ0), col(HF0), col(HI0), col(HG0), pl.BlockSpec((1, W), lambda h: (0, h)),
                  pl.BlockSpec((1, HD), lambda h: (0, 0))],
        out_specs=[head, pl.BlockSpec((W, S), lambda h: (h, 0)), head,
                   pl.BlockSpec((NH, nch, HD, HD), lambda h: (h, 0, 0, 0)),
                   pl.BlockSpec((8, HD), lambda h: (h, 0)), head],
        out_shape=[jax.ShapeDtypeStruct((S, D), BF16), jax.ShapeDtypeStruct((D, S), BF16),
                   jax.ShapeDtypeStruct((S, D), F32), jax.ShapeDtypeStruct((HEADS, nch, HD, HD), BF16),
                   jax.ShapeDtypeStruct((8 * HEADS // NH, HD), F32), jax.ShapeDtypeStruct((S, D), F32)],
        scratch_shapes=[pltpu.VMEM((NH, CH, HD), F32)] * 5 + [pltpu.VMEM((NH, HD, HD), F32)]
        + [pltpu.VMEM((NH, S, HD), F32)] * 3,
        sem=("parallel",), args=(proj, proj, proj, proj, lb, gn), carry=carry)
    return outs[:6], outs[6:]


def _hgrn_bwd(dbin, proj, oraw, states, mild, cum, lb, gn, name, carry=None):
    S = proj.shape[0]
    nch = S // CH
    W = NH * HD

    def body(db_ref, hq_ref, hf_ref, hi_ref, hg_ref, or_ref, st_ref, mild_ref, cum_ref, lb_ref, gn_ref,
             dq_ref, df_ref, di_ref, dg_ref, dlb_ref, dgn_ref,
             q_s, k_s, c_s, v_s, do_s, dq_s, dk_s, dv_s, dc_s, dqd_s, dkd_s, f_s, sf_s, sq_s, dl_s, dst_s,
             dlb_s, dgn_s):
        dst_s[...] = jnp.zeros((NH, HD, HD), F32)
        dlb_s[...] = jnp.zeros((1, W), F32)
        dgn_s[...] = jnp.zeros((1, HD), F32)
        rowi = lax.broadcasted_iota(jnp.int32, (CH, 1), 0)
        rowi2 = lax.broadcasted_iota(jnp.int32, (CH, CH), 0)
        coli2 = lax.broadcasted_iota(jnp.int32, (CH, CH), 1)
        sbi = lax.broadcasted_iota(jnp.int32, (SB, 1), 0)
        gnv = gn_ref[...]
        def between_chunks(hh, n, rows):
            lanes = slice(hh * HD, (hh + 1) * HD)
            lbv = lb_ref[:, lanes]
            hq = hq_ref[rows, lanes].astype(F32)
            sq = _sig(hq)
            sf = _sig(hf_ref[rows, lanes].astype(F32))
            f = lbv + (1.0 - lbv) * sf
            q = hq * sq
            k = 1.0 - f
            f_s[hh] = f
            sf_s[hh] = sf
            sq_s[hh] = sq
            v = hi_ref[rows, lanes].astype(F32)
            c = cum_ref[rows, lanes]
            ov = or_ref[rows, lanes]
            hg = hg_ref[rows, lanes].astype(F32)
            sg = _sig(hg)
            r = lax.rsqrt(jnp.mean(ov * ov, axis=-1, keepdims=True) + EPS)
            dbv = db_ref[rows, lanes]
            d_on = dbv * (hg * sg)
            dg_ref[rows, lanes] = (dbv * ((ov * r) * gnv) * _dsilu(hg, sg)).astype(BF16)
            dgn_s[...] += jnp.sum(d_on * (ov * r), axis=0, keepdims=True)
            u = d_on * gnv
            do = r * u - ov * (r * r * r) * jnp.mean(u * ov, axis=-1, keepdims=True)
            q_s[hh] = q
            k_s[hh] = k
            c_s[hh] = c
            v_s[hh] = v
            do_s[hh] = do
            st = st_ref[hh, n].astype(F32)
            dst = dst_s[hh]
            ec = jnp.exp(c)
            last = c_s[hh, CH - 1:CH, :]
            el = jnp.exp(last - c)
            elast = jnp.exp(last)
            dq = _dot_nn(do, st) * ec
            dk = _dot_nn(v, dst) * el
            dq_s[hh] = dq
            dk_s[hh] = dk
            dv_s[hh] = _dot_nt(k * el, dst)
            dc_s[hh] = q * dq - k * dk
            dl_s[hh] = (jnp.sum(k * dk, axis=0, keepdims=True)
                        + elast * jnp.sum(st * dst, axis=0, keepdims=True))
            dst_s[hh] = dst * elast + _dot_tn(do, q * ec)

        def pairs_matmul(hh, first, cap, strict):
            q, k, c, v, do = q_s[hh], k_s[hh], c_s[hh], v_s[hh], do_s[hh]
            d_a = _dot_nt(do, v).astype(BF16).astype(F32)
            d_at = d_a.T
            at = jnp.zeros((CH, CH), F32)
            dq, dk, dcum = dq_s[hh], dk_s[hh], dc_s[hh]
            for i in range(first, CH // SB):
                r_i = _block_ref(c_s.at[hh], i)
                eq = jnp.exp(jnp.minimum(c - r_i, 0.0))
                ek = jnp.exp(jnp.minimum(r_i - c, cap))
                qi = (q * eq).astype(BF16).astype(F32)
                kei = (k * ek).astype(BF16).astype(F32)
                in_t = (rowi2 >= SB * i) & (rowi2 < SB * (i + 1))
                in_s = (coli2 >= SB * i) & (coli2 < SB * (i + 1))
                m_ts = in_t & ((coli2 < SB * i) if strict else (coli2 <= rowi2))
                m_st = in_s & ((rowi2 < SB * i) if strict else (rowi2 <= coli2))
                at = at + jnp.where(m_st, _dot_nt(kei, qi), 0.0)
                dq_i = _dot_nn(jnp.where(m_ts, d_a, 0.0), kei)
                dk_i = _dot_nn(jnp.where(m_st, d_at, 0.0), qi)
                dq = dq + dq_i * eq
                dk = dk + dk_i * ek
                dcum = dcum + (qi * dq_i - kei * dk_i)
            dq_s[hh] = dq
            dk_s[hh] = dk
            dc_s[hh] = dcum
            dv_s[hh] += _dot_nn(at, do)

        def pairs_exact(hh):
            dqd_s[hh] = jnp.zeros((CH, HD), F32)
            dkd_s[hh] = jnp.zeros((CH, HD), F32)
            for i in range(CH // SB):
                blk = slice(SB * i, SB * (i + 1))
                qb = q_s[hh, blk, :]
                cb = c_s[hh, blk, :]
                dob = do_s[hh, blk, :]
                dq_acc = jnp.zeros((SB, HD), F32)
                for s in range(SB):
                    row = SB * i + s
                    ks = k_s[hh, row:row + 1, :]
                    vs = v_s[hh, row:row + 1, :]
                    w = jnp.exp(jnp.minimum(cb - c_s[hh, row:row + 1, :], 0.0))
                    live = sbi >= s
                    a_col = jnp.where(live, jnp.sum(qb * ks * w, axis=-1, keepdims=True), 0.0)
                    da_col = jnp.where(live, jnp.sum(dob * vs, axis=-1, keepdims=True), 0.0)
                    dq_acc = dq_acc + da_col * ks * w
                    dkd_s[hh, row:row + 1, :] += jnp.sum(da_col * qb * w, axis=0, keepdims=True)
                    dv_s[hh, row:row + 1, :] += jnp.sum(a_col * dob, axis=0, keepdims=True)
                dqd_s[hh, blk, :] += dq_acc
            dq_d = dqd_s[hh]
            dk_d = dkd_s[hh]
            dq_s[hh] += dq_d
            dk_s[hh] += dk_d
            dc_s[hh] += q_s[hh] * dq_d - k_s[hh] * dk_d

        def gate_grads(hh, rows):
            lanes = slice(hh * HD, (hh + 1) * HD)
            lbv = lb_ref[:, lanes]
            hq = hq_ref[rows, lanes].astype(F32)
            f, sf, sq = f_s[hh], sf_s[hh], sq_s[hh]
            dlogf = _chunk_rev_cumsum(dc_s[hh], rowi) + dl_s[hh]
            dfv = jnp.where(f > 1e-30, dlogf / jnp.maximum(f, 1e-30), 0.0) - dk_s[hh]
            dlb_s[:, lanes] += jnp.sum(dfv * (1.0 - sf), axis=0, keepdims=True)
            df_ref[rows, lanes] = (dfv * (1.0 - lbv) * sf * (1.0 - sf)).astype(BF16)
            dq_ref[rows, lanes] = (dq_s[hh] * _dsilu(hq, sq)).astype(BF16)
            di_ref[rows, lanes] = dv_s[hh].astype(BF16)

        def chunk_with(pairs):
            def chunk(j, carry):
                n = nch - 1 - j
                rows = pl.ds(pl.multiple_of(n * CH, CH), CH)
                for hh in range(NH):
                    between_chunks(hh, n, rows)
                for hh in range(NH):
                    pairs(hh)
                for hh in range(NH):
                    gate_grads(hh, rows)
                return carry
            return chunk

        def pairs_mild(hh):
            pairs_matmul(hh, 0, DECAY_CAP, strict=False)

        def pairs_any(hh):
            pairs_matmul(hh, 1, 0.0, strict=True)
            pairs_exact(hh)

        mild = jnp.max(mild_ref[...]) > 0.5

        @pl.when(mild)
        def _():
            lax.fori_loop(0, nch, chunk_with(pairs_mild), 0, unroll=4)

        @pl.when(jnp.logical_not(mild))
        def _():
            lax.fori_loop(0, nch, chunk_with(pairs_any), 0)

        dlb_ref[...] = dlb_s[...]
        dgn_ref[...] = jnp.broadcast_to(dgn_s[...], (8, HD))

    col = lambda off: pl.BlockSpec((S, W), lambda h: (0, off // NH + h))
    head = pl.BlockSpec((S, W), lambda h: (0, h))
    vec = pl.BlockSpec((1, W), lambda h: (0, h))
    outs = _call(
        body, name=name, grid=(HEADS // NH,),
        in_specs=[head, col(HQ0), col(HF0), col(HI0), col(HG0), head,
                  pl.BlockSpec((NH, nch, HD, HD), lambda h: (h, 0, 0, 0)),
                  pl.BlockSpec((8, HD), lambda h: (h, 0)), head, vec, pl.BlockSpec((1, HD), lambda h: (0, 0))],
        out_specs=[head, head, head, head, vec, pl.BlockSpec((8, HD), lambda h: (h, 0))],
        out_shape=[jax.ShapeDtypeStruct((S, D), BF16)] * 4
        + [jax.ShapeDtypeStruct((1, D), F32), jax.ShapeDtypeStruct((8 * HEADS // NH, HD), F32)],
        scratch_shapes=[pltpu.VMEM((NH, CH, HD), F32)] * 14
        + [pltpu.VMEM((NH, 1, HD), F32), pltpu.VMEM((NH, HD, HD), F32), pltpu.VMEM((1, W), F32),
           pltpu.VMEM((1, HD), F32)],
        sem=("parallel",), args=(dbin, proj, proj, proj, proj, oraw, states, mild, cum, lb, gn), carry=carry)
    dq, df, di, dg, dlb, dgn = outs[:6]
    return (dq, df, di, dg, dlb, dgn.reshape(HEADS // NH, 8, HD)[:, 0, :]), outs[6:]


def _lower_bounds(l0, l1):
    m = jnp.maximum(l0, l1)
    e0 = jnp.exp(l0 - m)
    e1 = jnp.exp(l1 - m)
    tot = e0 + e1
    p0 = e0 / tot
    p1 = e1 / tot
    return jnp.clip(p0 - p0, 0.0, 1.0), jnp.clip((p0 + p1) - p0, 0.0, 1.0)


def _lb_fwd(logits):
    def body(l_ref, o_ref):
        lb0, lb1 = _lower_bounds(l_ref[0:1, :], l_ref[1:2, :])
        o_ref[0:1, :] = lb0
        o_ref[1:2, :] = lb1

    return pl.pallas_call(body, name="lb_fwd", out_shape=jax.ShapeDtypeStruct((2, D), F32))(logits)


def _lb_bwd(logits, dlb):
    def body(l_ref, d_ref, o_ref):
        _, vjp = jax.vjp(_lower_bounds, l_ref[0:1, :], l_ref[1:2, :])
        g0, g1 = vjp((d_ref[0:1, :], d_ref[1:2, :]))
        o_ref[0:1, :] = g0
        o_ref[1:2, :] = g1

    return pl.pallas_call(body, name="lb_bwd", out_shape=jax.ShapeDtypeStruct((2, D), F32))(logits, dlb)


ADA_PAD = 128


def _ada_fwd(c_pad, w_ada, b_sh):
    ns = w_ada.shape[2]

    def body(c_ref, w_ref, b_ref, o_ref):
        cv = c_ref[...]
        ca = (cv * _sig(cv)).astype(BF16)
        for l in range(2):
            res = jnp.dot(ca, w_ref[l].astype(BF16), preferred_element_type=F32)
            o_ref[:, l * ns:(l + 1) * ns] = res[0:NDEV, :] + b_ref[l:l + 1, :]

    return pl.pallas_call(body, name="ada_fwd", out_shape=jax.ShapeDtypeStruct((NDEV, 2 * ns), F32),
                          compiler_params=_cp())(c_pad, w_ada, b_sh)


def _ada_wgrad(c_pad_t, d_ada_sh):
    ns = d_ada_sh.shape[2]

    def body(c_ref, d_ref, o_ref):
        cv = c_ref[...]
        ca = (cv * _sig(cv)).astype(BF16)
        for l in range(2):
            o_ref[l] = jnp.dot(ca, d_ref[l].astype(BF16), preferred_element_type=F32)

    return pl.pallas_call(body, name="ada_wgrad", out_shape=jax.ShapeDtypeStruct((2, D, ns), F32),
                          compiler_params=_cp())(c_pad_t, d_ada_sh)


def _sum_devices(g):
    _, R, C = g.shape

    def body(g_ref, o_ref):
        acc = g_ref[0]
        for d in range(1, NDEV):
            acc = acc + g_ref[d]
        o_ref[...] = acc

    return pl.pallas_call(body, name="sum_devices", out_shape=jax.ShapeDtypeStruct((R, C), F32),
                          compiler_params=_cp())(g)


def _adamw(w, g, m, v, name, carry=None):
    R, C = w.shape
    tr = _row_tile(R, max(8, (1 << 19) // C))

    def body(w_ref, g_ref, m_ref, v_ref, d_ref, nm_ref, nv_ref):
        d_ref[...], nm_ref[...], nv_ref[...] = _adamw_update(w_ref[...], g_ref[...], m_ref[...], v_ref[...])

    tile = pl.BlockSpec((tr, C), lambda i: (i, 0))
    return _call(body, name=name, grid=(R // tr,), in_specs=[tile] * 4, out_specs=[tile] * 3,
                 out_shape=[jax.ShapeDtypeStruct((R, C), F32)] * 3, sem=("parallel",), args=(w, g, m, v),
                 carry=carry)


def _adamw_update(w, g, m, v):
    nm = B1 * m + (1.0 - B1) * g
    nv = B2 * v + (1.0 - B2) * (g * g)
    m_hat = nm / (1.0 - B1 ** STEP)
    v_hat = nv / (1.0 - B2 ** STEP)
    return -LR * (m_hat / (jnp.sqrt(v_hat) + AEPS) + WD * w), nm, nv


SMALL_PARTS = (("b_ada", 0, 6, D), ("g_pre", 8, 2, D), ("g_post", 16, 2, D), ("lb_logits", 24, 2, D),
               ("pool_w", 32, 128, D), ("pool_scale", 160, 1, D), ("hgrn_norm_g", 168, 1, 2 * HD))


def _adamw_small(g_small, g_lb_logits, wmv):
    n = len(SMALL_PARTS)

    def body(g_ref, glb_ref, *refs):
        ins, outs = refs[:3 * n], refs[3 * n:]
        for p, (key, row0, rows, width) in enumerate(SMALL_PARTS):
            gv = glb_ref[...] if key == "lb_logits" else g_ref[row0:row0 + rows, 0:width]
            res = _adamw_update(ins[3 * p][...], gv, ins[3 * p + 1][...], ins[3 * p + 2][...])
            for t in range(3):
                outs[3 * p + t][...] = res[t]

    flat = [t for triple in wmv for t in triple]
    outs = pl.pallas_call(body, name="adamw_small",
                          out_shape=[jax.ShapeDtypeStruct(t.shape, F32) for t in flat],
                          compiler_params=_cp())(g_small, g_lb_logits, *flat)
    return [outs[3 * p:3 * p + 3] for p in range(n)]


def _cast_to_slot(place, w, l, name):
    _, R, C = w.shape
    tr = _row_tile(R, max(8, (1 << 19) // C))

    def body(p_ref, w_ref, o_ref):
        o_ref[...] = w_ref[...].astype(BF16)

    return pl.pallas_call(
        body, name=name, out_shape=jax.ShapeDtypeStruct((NCHIP, R, C), BF16),
        grid_spec=pltpu.PrefetchScalarGridSpec(
            num_scalar_prefetch=1, grid=(R // tr,),
            in_specs=[pl.BlockSpec((None, tr, C), lambda i, p_ref: (l, i, 0))],
            out_specs=pl.BlockSpec((None, tr, C), lambda i, p_ref: (p_ref[0], i, 0))),
        compiler_params=_cp(("parallel",)),
    )(place, w)


def _pair_add(core, g, got, name):
    _, R, C = g.shape
    r2 = R // 2
    tr = _row_tile(r2, max(8, (1 << 19) // C))
    nt = r2 // tr

    def body(c_ref, a_ref, b_ref, o_ref):
        o_ref[...] = (a_ref[...].astype(F32) + b_ref[...].astype(F32)).astype(o_ref.dtype)

    return pl.pallas_call(
        body, name=name, out_shape=jax.ShapeDtypeStruct((NCHIP, r2, C), BF16),
        grid_spec=pltpu.PrefetchScalarGridSpec(
            num_scalar_prefetch=1, grid=(NCHIP, nt),
            in_specs=[pl.BlockSpec((None, tr, C), lambda j, i, c_ref: (j, c_ref[0] * nt + i, 0)),
                      pl.BlockSpec((None, tr, C), lambda j, i, c_ref: (j, i, 0))],
            out_specs=pl.BlockSpec((None, tr, C), lambda j, i, c_ref: (j, i, 0))),
        compiler_params=_cp(("parallel", "parallel")),
    )(core, g, got)


def _chip_sum(place, part, recv, layer, both, name):
    _, r2, C = part.shape
    tr = _row_tile(r2, max(8, (1 << 18) // C))
    nt = r2 // tr

    def body(p_ref, own_ref, r_ref, *rest):
        o_ref = rest[-1]
        me = p_ref[0]
        own = own_ref[...].astype(F32)
        acc = None
        for j in range(NCHIP):
            slot = jnp.minimum(jnp.where(j > me, j - 1, j), NCHIP - 2)
            term = jnp.where(me == j, own, r_ref[slot].astype(F32))
            acc = term if acc is None else acc + term
        o_ref[...] = acc

    args = (place, part, recv) if both is None else (place, part, recv, both)
    return pl.pallas_call(
        body, name=name, out_shape=jax.ShapeDtypeStruct((2, 2 * r2, C), F32),
        grid_spec=pltpu.PrefetchScalarGridSpec(
            num_scalar_prefetch=1, grid=(nt,),
            in_specs=[pl.BlockSpec((None, tr, C), lambda i, p_ref: (p_ref[0], i, 0)),
                      pl.BlockSpec((NCHIP - 1, tr, C), lambda i, p_ref: (0, i, 0))] + [ANY] * (len(args) - 3),
            out_specs=pl.BlockSpec((None, tr, C), lambda i, p_ref: (layer, p_ref[1] * nt + i, 0))),
        input_output_aliases={} if both is None else {3: 0},
        compiler_params=_cp(("parallel",)),
    )(*args)


def _place():
    x, y, c = lax.axis_index("x"), lax.axis_index("y"), lax.axis_index("c")
    chips = [(1 - x, y), (x, 1 - y), (1 - x, 1 - y)]
    return x, y, c, chips


def _gather_small(blk, name):
    m_per, n = blk.shape

    def body(x_ref, out_ref, send_sems, recv_sems, local_sem):
        x, y, c, chips = _place()
        me, sibling = (x, y, c), (x, y, 1 - c)

        def rows(px, py, pc):
            return out_ref.at[pl.ds((4 * px + 2 * py + pc) * m_per, m_per), :]

        def copy(k, block, to, src=None):
            return pltpu.make_async_remote_copy(
                src_ref=rows(*block) if src is None else src, dst_ref=rows(*block),
                send_sem=send_sems.at[k], recv_sem=recv_sems.at[k], device_id=to, device_id_type=MESH)

        mine = pltpu.make_async_copy(x_ref, rows(*me), local_sem)
        mine.start()
        first = [copy(0, me, sibling, src=x_ref)]
        first += [copy(1 + j, me, (*chip, c), src=x_ref) for j, chip in enumerate(chips)]
        for cp in first:
            cp.start()
        passed = [copy(4 + j, (*chip, c), sibling) for j, chip in enumerate(chips)]
        for j, chip in enumerate(chips):
            copy(1 + j, (*chip, c), me).wait_recv()
            passed[j].start()
        copy(0, sibling, me).wait_recv()
        for j, chip in enumerate(chips):
            copy(4 + j, (*chip, 1 - c), me).wait_recv()
        for cp in first + passed:
            cp.wait_send()
        mine.wait()

    return pl.pallas_call(
        body, name=name, out_shape=jax.ShapeDtypeStruct((NDEV * m_per, n), blk.dtype),
        in_specs=[pl.BlockSpec(memory_space=pltpu.VMEM)], out_specs=pl.BlockSpec(memory_space=pltpu.VMEM),
        scratch_shapes=[pltpu.SemaphoreType.DMA((7,)), pltpu.SemaphoreType.DMA((7,)), pltpu.SemaphoreType.DMA],
        compiler_params=_cp(),
    )(blk)


def _gather_rows_carry(blk):
    m_per, n = blk.shape

    def rows(ref, px, py, pc):
        return ref.at[pl.ds((4 * px + 2 * py + pc) * m_per, m_per), :]

    def copy(ins, outs, send_sems, recv_sems, k, block, to, own=False):
        return pltpu.make_async_remote_copy(
            src_ref=ins[0] if own else rows(outs[0], *block), dst_ref=rows(outs[0], *block),
            send_sem=send_sems.at[k], recv_sem=recv_sems.at[k], device_id=to, device_id_type=MESH)

    def mine(ins, outs, send_sems):
        x, y, c, _ = _place()
        return pltpu.make_async_copy(ins[0], rows(outs[0], x, y, c), send_sems.at[7])

    def start(ins, outs, send_sems, recv_sems):
        x, y, c, chips = _place()
        mine(ins, outs, send_sems).start()
        copy(ins, outs, send_sems, recv_sems, 0, (x, y, c), (x, y, 1 - c), own=True).start()
        for j, chip in enumerate(chips):
            copy(ins, outs, send_sems, recv_sems, 1 + j, (x, y, c), (*chip, c), own=True).start()

    def finish(ins, outs, send_sems, recv_sems):
        x, y, c, chips = _place()
        for j, chip in enumerate(chips):
            copy(ins, outs, send_sems, recv_sems, 1 + j, (*chip, c), (x, y, c)).wait_recv()
            copy(ins, outs, send_sems, recv_sems, 4 + j, (*chip, c), (x, y, 1 - c)).start()
        copy(ins, outs, send_sems, recv_sems, 0, (x, y, 1 - c), (x, y, c)).wait_recv()
        for j, chip in enumerate(chips):
            copy(ins, outs, send_sems, recv_sems, 4 + j, (*chip, 1 - c), (x, y, c)).wait_recv()
        copy(ins, outs, send_sems, recv_sems, 0, (x, y, c), (x, y, 1 - c), own=True).wait_send()
        for j, chip in enumerate(chips):
            copy(ins, outs, send_sems, recv_sems, 1 + j, (x, y, c), (*chip, c), own=True).wait_send()
            copy(ins, outs, send_sems, recv_sems, 4 + j, (*chip, c), (x, y, 1 - c)).wait_send()
        mine(ins, outs, send_sems).wait()

    return _Carry([blk], [jax.ShapeDtypeStruct((NDEV * m_per, n), blk.dtype)], {}, 8, start, finish)


def _gather_carry(shards, piece=(0, 1, 1)):
    n = len(shards)
    first, count, of = piece

    def rows(ref, half):
        r2 = ref.shape[1] // 2
        return pl.ds(half * r2 + first * (r2 // of), count * (r2 // of))

    def over_ici(outs, send_sems, recv_sems, a, j, chip_xy, slot):
        x, y, c, _ = _place()
        blk = outs[a].at[slot, rows(outs[a], c), :]
        return pltpu.make_async_remote_copy(
            src_ref=blk, dst_ref=blk, send_sem=send_sems.at[6 * a + j], recv_sem=recv_sems.at[6 * a + j],
            device_id=(*chip_xy, c), device_id_type=MESH)

    def over_d2d(outs, send_sems, recv_sems, a, j, slot, half):
        x, y, c, _ = _place()
        blk = outs[a].at[slot, rows(outs[a], half), :]
        return pltpu.make_async_remote_copy(
            src_ref=blk, dst_ref=blk, send_sem=send_sems.at[6 * a + 3 + j], recv_sem=recv_sems.at[6 * a + 3 + j],
            device_id=(x, y, 1 - c), device_id_type=MESH)

    def start(ins, outs, send_sems, recv_sems):
        x, y, c, chips = _place()
        for a in range(n):
            for j, chip_xy in enumerate(chips):
                over_ici(outs, send_sems, recv_sems, a, j, chip_xy, 2 * x + y).start()

    def finish(ins, outs, send_sems, recv_sems):
        x, y, c, chips = _place()
        for a in range(n):
            for j, (cx, cy) in enumerate(chips):
                over_ici(outs, send_sems, recv_sems, a, j, (cx, cy), 2 * cx + cy).wait_recv()
                over_d2d(outs, send_sems, recv_sems, a, j, 2 * cx + cy, c).start()
        for a in range(n):
            for j, (cx, cy) in enumerate(chips):
                over_d2d(outs, send_sems, recv_sems, a, j, 2 * cx + cy, 1 - c).wait_recv()
        for a in range(n):
            for j, (cx, cy) in enumerate(chips):
                over_ici(outs, send_sems, recv_sems, a, j, (cx, cy), 2 * x + y).wait_send()
                over_d2d(outs, send_sems, recv_sems, a, j, 2 * cx + cy, c).wait_send()

    return _Carry(shards, [jax.ShapeDtypeStruct(s.shape, s.dtype) for s in shards],
                  {a: a for a in range(n)}, 6 * n, start, finish)


def _rs_pair(grads, name):
    n = len(grads)

    def body(*refs):
        ins, gots = refs[:n], refs[n:2 * n]
        send_sems, recv_sems = refs[2 * n:]
        x, y, c, _ = _place()
        cps = []
        for a in range(n):
            r2 = ins[a].shape[1] // 2
            cp = pltpu.make_async_remote_copy(
                src_ref=ins[a].at[:, pl.ds((1 - c) * r2, r2), :], dst_ref=gots[a],
                send_sem=send_sems.at[a], recv_sem=recv_sems.at[a],
                device_id=(x, y, 1 - c), device_id_type=MESH)
            cp.start()
            cps.append(cp)
        for cp in cps:
            cp.wait()

    half = [jax.ShapeDtypeStruct((NCHIP, g.shape[1] // 2, g.shape[2]), g.dtype) for g in grads]
    return pl.pallas_call(
        body, name=name, out_shape=half, in_specs=[ANY] * n, out_specs=[ANY] * n,
        scratch_shapes=[pltpu.SemaphoreType.DMA((n,)), pltpu.SemaphoreType.DMA((n,))],
        compiler_params=_cp(),
    )(*grads)


def _chips_carry(parts, piece=(0, 1, 1), into=None):
    n = len(parts)
    first, count, of = piece

    def rows(ref):
        step = ref.shape[1] // of
        return pl.ds(first * step, count * step)

    def send(ins, outs, send_sems, recv_sems, a, j, chip_xy):
        x, y, c, _ = _place()
        me, them = 2 * x + y, 2 * chip_xy[0] + chip_xy[1]
        return pltpu.make_async_remote_copy(
            src_ref=ins[a].at[them, rows(ins[a]), :],
            dst_ref=outs[a].at[me - (me > them).astype(jnp.int32), rows(outs[a]), :],
            send_sem=send_sems.at[3 * a + j], recv_sem=recv_sems.at[3 * a + j],
            device_id=(*chip_xy, c), device_id_type=MESH)

    def start(ins, outs, send_sems, recv_sems):
        _, _, _, chips = _place()
        for a in range(n):
            for j, chip_xy in enumerate(chips):
                send(ins, outs, send_sems, recv_sems, a, j, chip_xy).start()

    def finish(ins, outs, send_sems, recv_sems):
        x, y, c, chips = _place()
        me = 2 * x + y
        for a in range(n):
            for j, (cx, cy) in enumerate(chips):
                them = 2 * cx + cy
                blk = outs[a].at[them - (them > me).astype(jnp.int32), rows(outs[a]), :]
                pltpu.make_async_remote_copy(
                    src_ref=blk, dst_ref=blk, send_sem=send_sems.at[3 * a + j], recv_sem=recv_sems.at[3 * a + j],
                    device_id=(cx, cy, c), device_id_type=MESH).wait_recv()
        for a in range(n):
            for j, chip_xy in enumerate(chips):
                send(ins, outs, send_sems, recv_sems, a, j, chip_xy).wait_send()

    landing = [jax.ShapeDtypeStruct((NCHIP - 1,) + p.shape[1:], p.dtype) for p in parts]
    if into is None:
        return _Carry(parts, landing, {}, 3 * n, start, finish)
    return _Carry(list(parts) + list(into), landing, {n + a: a for a in range(n)}, 3 * n, start, finish)


def _rs_swap(fulls):
    n = len(fulls)

    def body(*refs):
        outs = refs[n:2 * n]
        send_sems, recv_sems = refs[2 * n:]
        x, y, c, _ = _place()
        cps = []
        for a in range(n):
            r2 = outs[a].shape[1] // 2
            mine = outs[a].at[:, pl.ds(c * r2, r2), :]
            cp = pltpu.make_async_remote_copy(
                src_ref=mine, dst_ref=mine, send_sem=send_sems.at[a], recv_sem=recv_sems.at[a],
                device_id=(x, y, 1 - c), device_id_type=MESH)
            cp.start()
            cps.append(cp)
        for a in range(n):
            r2 = outs[a].shape[1] // 2
            blk = outs[a].at[:, pl.ds((1 - c) * r2, r2), :]
            pltpu.make_async_remote_copy(
                src_ref=blk, dst_ref=blk, send_sem=send_sems.at[a], recv_sem=recv_sems.at[a],
                device_id=(x, y, 1 - c), device_id_type=MESH).wait_recv()
        for cp in cps:
            cp.wait_send()

    return pl.pallas_call(
        body, name="rs_swap", out_shape=[jax.ShapeDtypeStruct(f.shape, f.dtype) for f in fulls],
        in_specs=[ANY] * n, out_specs=[ANY] * n, input_output_aliases={a: a for a in range(n)},
        scratch_shapes=[pltpu.SemaphoreType.DMA((n,)), pltpu.SemaphoreType.DMA((n,))],
        compiler_params=_cp(),
    )(*fulls)


class _GatherInProj:
    def __init__(self, slot, order):
        self.slot, self.order = slot, order


def _proj_with_gather(h, w_slot, order, name, tn=256):
    S, K = h.shape
    nsh, _, ns = w_slot.shape
    tps = ns // tn
    nt = nsh * tps
    r2 = K // 2

    def body(ord_ref, h_ref, w_in_ref, o_ref, w_ref, wbuf, tile_sems, send_sems, recv_sems):
        n = pl.program_id(0)
        x, y, c, chips = _place()

        def half(slot, which):
            return w_ref.at[slot, pl.ds(which * r2, r2), :]

        def over_ici(j, slot):
            blk = half(slot, c)
            return pltpu.make_async_remote_copy(src_ref=blk, dst_ref=blk, send_sem=send_sems.at[j],
                                                recv_sem=recv_sems.at[j], device_id=(*chips[j], c),
                                                device_id_type=MESH)

        def over_d2d(j, which):
            blk = half(2 * chips[j][0] + chips[j][1], which)
            return pltpu.make_async_remote_copy(src_ref=blk, dst_ref=blk, send_sem=send_sems.at[3 + j],
                                                recv_sem=recv_sems.at[3 + j], device_id=(x, y, 1 - c),
                                                device_id_type=MESH)

        def tile_copy(step, slot):
            shard = ord_ref[step // tps]
            return pltpu.make_async_copy(w_ref.at[shard, :, pl.ds((step % tps) * tn, tn)], wbuf.at[slot],
                                         tile_sems.at[slot])

        @pl.when(n == 0)
        def _():
            for j in range(3):
                over_ici(j, 2 * x + y).start()
            tile_copy(0, 0).start()

        for j in range(3):
            @pl.when(n == (j + 1) * tps - 1)
            def _(j=j):
                over_ici(j, 2 * chips[j][0] + chips[j][1]).wait_recv()
                over_d2d(j, c).start()
                over_d2d(j, 1 - c).wait_recv()

        @pl.when(n + 1 < nt)
        def _():
            tile_copy(n + 1, (n + 1) % 2).start()

        tile_copy(n, n % 2).wait()
        o_ref[...] = jnp.dot(h_ref[...], wbuf[n % 2], preferred_element_type=F32).astype(o_ref.dtype)

        @pl.when(n == nt - 1)
        def _():
            for j in range(3):
                over_ici(j, 2 * x + y).wait_send()
                over_d2d(j, c).wait_send()

    return pl.pallas_call(
        body, name=name,
        out_shape=[jax.ShapeDtypeStruct((S, nsh * ns), BF16), jax.ShapeDtypeStruct(w_slot.shape, w_slot.dtype)],
        grid_spec=pltpu.PrefetchScalarGridSpec(
            num_scalar_prefetch=1, grid=(nt,),
            in_specs=[pl.BlockSpec((S, K), lambda n, o_ref: (0, 0)), ANY],
            out_specs=[pl.BlockSpec((S, tn), lambda n, o_ref: (0, o_ref[n // tps] * tps + n % tps)), ANY],
            scratch_shapes=[pltpu.VMEM((2, K, tn), w_slot.dtype), pltpu.SemaphoreType.DMA((2,)),
                            pltpu.SemaphoreType.DMA((6,)), pltpu.SemaphoreType.DMA((6,))]),
        input_output_aliases={2: 1},
        compiler_params=_cp(("arbitrary",)),
    )(order, h, w_slot)


def _mm_ride(a, b, carry, **kw):
    if carry is None:
        return _mm(a, b, **kw), []
    return _mm(a, b, carry=carry, **kw)


def _layer_fwd(l, x, ada, w, small, ride, target=None):
    shift, scale, gate = ada[:, 0:D], ada[:, D:2 * D], ada[:, 2 * D:3 * D]
    carry, landed = ride("prenorm")
    (h, h_t), outs = _prenorm_fwd(x, small["g_pre"][l], scale, shift, f"prenorm_fwd{l}", carry)
    landed(outs)
    carry, landed = ride("proj")
    if isinstance(carry, _GatherInProj):
        proj, full = _proj_with_gather(h, carry.slot, carry.order, f"proj{l}")
        outs = [full]
    else:
        proj, outs = _mm_ride(h, w["w_in"][l], carry, name=f"proj{l}", b_mode="nn_sh", tm=2048, out_dtype=BF16)
    landed(outs)
    a_in, a_in_t = _pool_fwd(proj, small["pool_w"][l], small["pool_scale"][l], f"pool_fwd{l}")
    carry, landed = ride("hgrn")
    (b_in, b_in_t, o_raw, states, mild, cum), outs = _hgrn_fwd(proj, small["lb"][l], small["hgrn_norm_g"][l],
                                                              f"hgrn_fwd{l}", carry=carry)
    landed(outs)
    carry, landed = ride("tail")
    (br_a, br_b, merged_t, y, *x_new), outs = _layer_tail_fwd(
        proj, a_in, b_in, x, w["w_pool_o"][l], w["w_hgrn_o"][l].reshape(D, D), w["w_out"][l].reshape(D, D),
        gate, small["g_post"][l], f"tail_fwd{l}", target=target, carry=carry)
    landed(outs)
    saved = dict(x=x, h_t=h_t, proj=proj, a_in_t=a_in_t, b_in_t=b_in_t, o_raw=o_raw, states=states, mild=mild,
                 cum=cum,
                 br_a=br_a, br_b=br_b, merged_t=merged_t, y=y, scale=scale, gate=gate)
    return x_new, saved


def _layer_bwd(l, dxn, sv, w, small, ride):
    dy, dbr_a, dbr_b, dmg, da_in, db_in, dgate, dg_post = _layer_head_bwd(
        dxn, sv["y"], sv["proj"], sv["br_a"], sv["br_b"], w["w_pool_o"][l], w["w_hgrn_o"][l].reshape(D, D),
        w["w_out"][l].reshape(D, D), sv["gate"], small["g_post"][l], f"head_bwd{l}")
    gw_out = _mm(sv["merged_t"], dy, name=f"gw_out{l}", out_dtype=BF16)
    gw_pool_o = _mm(sv["a_in_t"], dbr_a, name=f"gw_pool_o{l}", out_shards=NCHIP, out_dtype=BF16)
    gw_hgrn_o = _mm(sv["b_in_t"], dbr_b, name=f"gw_hgrn_o{l}", out_dtype=BF16)
    big = dict(w_pool_o=gw_pool_o, w_hgrn_o=gw_hgrn_o.reshape(NCHIP, D // NCHIP, D),
               w_out=gw_out.reshape(NCHIP, D // NCHIP, D))
    carry, landed = ride["hgrn"](big)
    (dhq, dhf, dhi, dhg, dlb, dgn), outs = _hgrn_bwd(db_in, sv["proj"], sv["o_raw"], sv["states"], sv["mild"],
                                                     sv["cum"], small["lb"][l], small["hgrn_norm_g"][l],
                                                     f"hgrn_bwd{l}", carry=carry)
    landed(outs)
    dpv, dpg, dpw, dpsc = _pool_bwd(da_in, sv["proj"], small["pool_w"][l], small["pool_scale"][l],
                                    f"pool_bwd{l}")
    dproj = jnp.concatenate([dpv, dpg, dhq, dhf, dhi, dhg, dmg], axis=1)
    little = dict(dgate=dgate, g_post=dg_post, pool_w=dpw, pool_scale=dpsc, lb=dlb,
                  hgrn_norm_g=jnp.sum(dgn, axis=0, keepdims=True))
    carry, landed = ride["gw_in"](little)
    big["w_in"], outs = _mm_ride(sv["h_t"], dproj, carry, name=f"gw_in{l}", out_shards=NCHIP, out_dtype=BF16)
    landed(outs)
    carry, landed = ride["d_h"](big)
    dh, outs = _mm_ride(dproj, w["w_in"][l], carry, name=f"d_h{l}", b_mode="nt_shk", tn=1024)
    landed(outs)
    carry, landed = ride["prenorm"](big)
    (dx, dshift, dscale, dg_pre), outs = _prenorm_bwd(dh, dxn, sv["x"], small["g_pre"][l], sv["scale"],
                                                      f"prenorm_bwd{l}", carry)
    landed(outs)
    little.update(dshift=dshift, dscale=dscale, g_pre=dg_pre)
    return dx, big, little


SMALL_ROWS = 176


def _rows8(t):
    t = t.reshape(-1, D)
    return jnp.pad(t, ((0, -t.shape[0] % 8), (0, 0)))


def _pack_small_weights(b_ada, g_pre, g_post, lb_logits, pool_w, pool_scale, hgrn_norm_g):
    gn = jnp.pad(hgrn_norm_g.reshape(1, 2 * HD), ((0, 0), (0, D - 2 * HD)))
    return jnp.concatenate([_rows8(b_ada), _rows8(g_pre), _rows8(g_post), _rows8(lb_logits), _rows8(pool_w),
                            _rows8(pool_scale), _rows8(gn)], axis=0)


def _pack_small(parts):
    both = lambda key: jnp.stack([parts[l][key] for l in range(2)])
    d_ada = jnp.stack([jnp.concatenate([p["dshift"], p["dscale"], p["dgate"]], axis=1) for p in parts])
    return _pack_small_weights(d_ada, both("g_pre"), both("g_post"), both("lb"), both("pool_w"),
                               both("pool_scale"), both("hgrn_norm_g"))


def _unpack_small(p):
    return (p[0:6].reshape(2, 3 * D), p[8:10], p[16:18], p[24:26], p[32:160].reshape(2, GROUPS, 128, 128),
            p[160:161].reshape(2, POOL_W), p[168:169, 0:2 * HD].reshape(2, HD))


def kernel(x, c, w_ada, b_ada, g_pre, g_post, w_in, pool_w, pool_scale, lb_logits, hgrn_norm_g, w_pool_o, w_hgrn_o, w_out, loss_target, m_w_ada, m_b_ada, m_g_pre, m_g_post, m_w_in, m_pool_w, m_pool_scale, m_lb_logits, m_hgrn_norm_g, m_w_pool_o, m_w_hgrn_o, m_w_out, v_w_ada, v_b_ada, v_g_pre, v_g_post, v_w_in, v_pool_w, v_pool_scale, v_lb_logits, v_hgrn_norm_g, v_w_pool_o, v_w_hgrn_o, v_w_out):
    ax, ay, ac = lax.axis_index("x"), lax.axis_index("y"), lax.axis_index("c")
    chip = 2 * ax + ay
    dev = 2 * chip + ac
    xe, te = x[0], loss_target[0]
    ada_s = w_ada.shape[2]

    big_names = ("w_in", "w_pool_o", "w_hgrn_o", "w_out")
    big_w = (w_in, w_pool_o, w_hgrn_o, w_out)
    core = jnp.stack([ac]).astype(jnp.int32)
    place = jnp.stack([chip, ac]).astype(jnp.int32)
    slots = {(k, l): _cast_to_slot(place, t, l, f"cast_{k}{l}") for l in range(2) for k, t in zip(big_names, big_w)}
    w = {k: [None, None] for k in big_names}
    def fills(keys):
        def landed(outs):
            for (k, l), o in zip(keys, outs):
                w[k][l] = slots[k, l] = o
        return landed

    rest0 = [(k, 0) for k in big_names[1:]]
    rest1 = [(k, 1) for k in big_names[1:]]
    no_carry = (None, lambda outs: None)
    order = jnp.stack([chip, 2 * (1 - ax) + ay, 2 * ax + (1 - ay), 2 * (1 - ax) + (1 - ay)]).astype(jnp.int32)

    def ride_fwd0(stage):
        if stage == "proj":
            return _GatherInProj(slots["w_in", 0], order), fills([("w_in", 0)])
        if stage == "hgrn":
            return (_join_carries(_gather_carry([slots[t] for t in rest0]),
                                  _gather_carry([slots["w_in", 1]], piece=(0, 2, 4))),
                    fills(rest0 + [("w_in", 1)]))
        if stage == "tail":
            return _gather_carry([slots["w_in", 1]], piece=(2, 1, 4)), fills([("w_in", 1)])
        return no_carry

    def ride_fwd1(stage):
        if stage == "prenorm":
            return _gather_carry([slots["w_in", 1]], piece=(3, 1, 4)), fills([("w_in", 1)])
        if stage == "hgrn":
            return _gather_carry([slots[t] for t in rest1]), fills(rest1)
        return no_carry

    c_all = _gather_small(jnp.broadcast_to(c, (8, D)), "gather_c").reshape(NDEV, 8, D)[:, 0, :]
    c_pad = jnp.pad(c_all, ((0, ADA_PAD - NDEV), (0, 0)))
    b_sh = lax.dynamic_slice(b_ada, (0, chip * ada_s), (2, ada_s))
    ada_cols = _gather_small(_ada_fwd(c_pad, w_ada, b_sh), "gather_ada")
    ada_cols = ada_cols.reshape(NCHIP, 2, NDEV, 2, ada_s)[:, 0]
    ada_all = jnp.transpose(ada_cols, (2, 1, 0, 3)).reshape(2, NDEV, 3 * D)
    ada_me = lax.dynamic_slice(ada_all, (0, dev, 0), (2, 1, 3 * D))

    lbs = _lb_fwd(lb_logits)
    small = dict(g_pre=g_pre[:, None, :], g_post=g_post[:, None, :], pool_w=pool_w,
                 pool_scale=pool_scale[:, None, :], lb=lbs[:, None, :], hgrn_norm_g=hgrn_norm_g[:, None, :])

    (x1,), sv0 = _layer_fwd(0, xe, ada_me[0], w, small, ride_fwd0)
    (dx2, loss_blk), sv1 = _layer_fwd(1, x1, ada_me[1], w, small, ride_fwd1, target=te)

    parts, recv = {}, {}

    def pair_sums(keys, grads, tag):
        got = _rs_pair(grads, f"rs_pair_{tag}")
        for kl, g, o in zip(keys, grads, got):
            parts[kl] = _pair_add(core, g, o, f"rs_add_{kl[0]}{kl[1]}")

    def exchange(keys):
        def landed(outs):
            recv.update(zip(keys, outs))
        return _chips_carry([parts[kl] for kl in keys]), landed

    def early(l):
        return [(k, l) for k in big_names[1:]]

    def ride_hgrn1(big):
        pair_sums(early(1), [big[k] for k in big_names[1:]], "l1_early")
        return exchange(early(1))

    def ride_d_h1(big):
        pair_sums([("w_in", 1)], [big["w_in"]], "l1_w_in")
        return no_carry

    def ride_hgrn0(big):
        pair_sums(early(0), [big[k] for k in big_names[1:]], "l0_early")
        return exchange([("w_in", 1)] + early(0))

    def ride_d_h0(big):
        pair_sums([("w_in", 0)], [big["w_in"]], "l0_w_in")

        def landed(outs):
            (recv["w_in", 0],) = outs
        return _chips_carry([parts["w_in", 0]], piece=(0, 1, 2)), landed

    def ride_prenorm0(big):
        def landed(outs):
            (recv["w_in", 0],) = outs
        return _chips_carry([parts["w_in", 0]], piece=(1, 1, 2), into=[recv["w_in", 0]]), landed

    no_ride = lambda so_far: no_carry
    dx1, big1, little1 = _layer_bwd(1, dx2, sv1, w, small,
                                    dict(hgrn=ride_hgrn1, gw_in=no_ride, d_h=ride_d_h1, prenorm=no_ride))

    gathered = {}
    zero_row = jnp.zeros((1, D), F32)

    def ride_gw_in0(little):
        so_far = dict(little, dshift=zero_row, dscale=zero_row, g_pre=zero_row)

        def landed(outs):
            (gathered["early"],) = outs
        return _gather_rows_carry(_pack_small([so_far, little1])), landed

    dx0, big0, little0 = _layer_bwd(0, dx1, sv0, w, small,
                                    dict(hgrn=ride_hgrn0, gw_in=ride_gw_in0, d_h=ride_d_h0, prenorm=ride_prenorm0))
    loss = lax.psum(loss_blk[0, 0], ("x", "y", "c"))
    late = _rows8(jnp.stack([little0["dshift"], little0["dscale"], little0["g_pre"]]))
    late = _gather_small(late, "gather_small_late").reshape(NDEV, 8, D)
    packed = gathered["early"].reshape(NDEV, SMALL_ROWS, D)
    packed = packed.at[:, 0:2, :].set(late[:, 0:2, :]).at[:, 8:9, :].set(late[:, 2:3, :])
    red = []
    for k in big_names:
        both = _chip_sum(place, parts[k, 1], recv[k, 1], 1, None, f"rs_sum_{k}1")
        red.append(_chip_sum(place, parts[k, 0], recv[k, 0], 0, both, f"rs_sum_{k}0"))
    g_big = dict(zip(big_names, _rs_swap(red)))

    def upd(wt, g, m, v, name, carry=None):
        shp = wt.shape
        two = lambda t: t.reshape(-1, shp[-1])
        res = _adamw(two(wt), two(g), two(m), two(v), name, carry)
        return [t.reshape(shp) for t in res[:3]], res[3:]

    u_w_in, _ = upd(w_in, g_big["w_in"], m_w_in, v_w_in, "adamw_w_in")
    g_small = _sum_devices(packed)
    g_b_ada, g_g_pre, g_g_post, g_lb, g_pool_w, g_pool_scale, g_norm_g = _unpack_small(g_small)
    g_lb_logits = _lb_bwd(lb_logits, g_lb)
    d_ada_all = packed[:, 0:6, :].reshape(NDEV, 2, 3 * D)
    d_ada_sh = lax.dynamic_slice(jnp.transpose(d_ada_all, (1, 0, 2)), (0, 0, chip * ada_s), (2, NDEV, ada_s))
    d_ada_sh = jnp.pad(d_ada_sh, ((0, 0), (0, ADA_PAD - NDEV), (0, 0)))
    g_w_ada = _ada_wgrad(c_pad.T, d_ada_sh)

    u_w_ada, _ = upd(w_ada, g_w_ada, m_w_ada, v_w_ada, "adamw_w_ada")
    u_w_pool_o, _ = upd(w_pool_o, g_big["w_pool_o"], m_w_pool_o, v_w_pool_o, "adamw_w_pool_o")
    u_w_hgrn_o, _ = upd(w_hgrn_o, g_big["w_hgrn_o"], m_w_hgrn_o, v_w_hgrn_o, "adamw_w_hgrn_o")
    u_w_out, _ = upd(w_out, g_big["w_out"], m_w_out, v_w_out, "adamw_w_out")
    small_w = dict(b_ada=(b_ada, m_b_ada, v_b_ada), g_pre=(g_pre, m_g_pre, v_g_pre),
                   g_post=(g_post, m_g_post, v_g_post), lb_logits=(lb_logits, m_lb_logits, v_lb_logits),
                   pool_w=(pool_w, m_pool_w, v_pool_w), pool_scale=(pool_scale, m_pool_scale, v_pool_scale),
                   hgrn_norm_g=(hgrn_norm_g, m_hgrn_norm_g, v_hgrn_norm_g))
    in_rows = [tuple(t.reshape(rows, width) for t in small_w[key]) for key, _, rows, width in SMALL_PARTS]
    u_rows = _adamw_small(g_small, g_lb_logits, in_rows)
    u_small = {key: [t.reshape(small_w[key][0].shape) for t in u_rows[p]]
               for p, (key, _, _, _) in enumerate(SMALL_PARTS)}

    grads_out = (g_w_ada, g_b_ada, g_g_pre, g_g_post, g_big["w_in"], g_pool_w, g_pool_scale, g_lb_logits,
                 g_norm_g, g_big["w_pool_o"], g_big["w_hgrn_o"], g_big["w_out"])

    def ordered(k):
        s = lambda key: u_small[key][k]
        return (u_w_ada[k], s("b_ada"), s("g_pre"), s("g_post"), u_w_in[k], s("pool_w"), s("pool_scale"),
                s("lb_logits"), s("hgrn_norm_g"), u_w_pool_o[k], u_w_hgrn_o[k], u_w_out[k])

    return (loss, dx0[None], *grads_out, *ordered(0), *ordered(1), *ordered(2))
```

```python
import functools

import jax
import jax.numpy as jnp
from jax import lax
from jax.experimental import pallas as pl
from jax.experimental.pallas import tpu as pltpu

F32 = jnp.float32
BF16 = jnp.bfloat16
MESH = pl.DeviceIdType.MESH

D = 1024
HEADS = 8
HD = 128
GROUPS = 4
POOL_W = 512
WINDOWS = (2, 4, 8, 16)
CH = 64
SB = 16
NH = 2
IN_W = 7168
NCHIP = 4
NDEV = 8
EPS = 1e-6
PV0, PG0, HQ0, HF0, HI0, HG0 = 0, 4, 8, 16, 24, 32
MGP_BLK, MGH_BLK = 5, 6

LR, B1, B2, AEPS, WD, STEP = 0.001, 0.9, 0.999, 1e-08, 0.01, 10
VMEM_LIMIT = 56 * 1024 * 1024


def _cp(sem=None, **kw):
    if sem is not None:
        kw["dimension_semantics"] = sem
    return pltpu.CompilerParams(vmem_limit_bytes=VMEM_LIMIT, **kw)


def _sig(z):
    return 1.0 / (1.0 + jnp.exp(-z))


def _dsilu(z, s):
    return s * (1.0 + z * (1.0 - s))


def _row_tile(rows, cap):
    if rows <= cap:
        return rows
    t = 1 << (cap.bit_length() - 1)
    while rows % t:
        t //= 2
    return t


ANY = pl.BlockSpec(memory_space=pl.ANY)


class _Carry:
    def __init__(self, ins, outs, aliases, n_sem, start, finish):
        self.ins, self.outs, self.aliases, self.n_sem = list(ins), list(outs), dict(aliases), n_sem
        self.start, self.finish = start, finish


class _SemWindow:
    def __init__(self, ref, base):
        self._ref, self._base = ref, base

    @property
    def at(self):
        return self

    def __getitem__(self, k):
        return self._ref.at[self._base + k]


def _join_carries(*carries):
    ins, outs, aliases, spans, n_sem = [], [], {}, [], 0
    for cr in carries:
        aliases.update({len(ins) + i: len(outs) + o for i, o in cr.aliases.items()})
        spans.append((len(ins), len(cr.ins), len(outs), len(cr.outs), n_sem))
        ins, outs, n_sem = ins + cr.ins, outs + cr.outs, n_sem + cr.n_sem

    def run(which):
        def fn(i_refs, o_refs, send_sems, recv_sems):
            for cr, (i0, ni, o0, no, s0) in zip(carries, spans):
                getattr(cr, which)(i_refs[i0:i0 + ni], o_refs[o0:o0 + no], _SemWindow(send_sems, s0),
                                   _SemWindow(recv_sems, s0))
        return fn

    return _Carry(ins, outs, aliases, n_sem, run("start"), run("finish"))


def _call(body, *, name, grid, in_specs, out_specs, out_shape, args, scratch_shapes=(), sem=None, carry=None):
    in_specs, out_specs, out_shape = list(in_specs), list(out_specs), list(out_shape)
    scratch_shapes = list(scratch_shapes)
    if carry is None:
        outs = pl.pallas_call(body, name=name, grid=grid, in_specs=in_specs, out_specs=out_specs,
                              out_shape=out_shape, scratch_shapes=scratch_shapes,
                              compiler_params=_cp(sem))(*args)
        return list(outs)
    n_in, n_out, n_scr = len(in_specs), len(out_specs), len(scratch_shapes)
    c_in, c_out = len(carry.ins), len(carry.outs)

    def wrapped(*refs):
        k_in, rest = refs[:n_in], refs[n_in:]
        ci, rest = rest[:c_in], rest[c_in:]
        k_out, rest = rest[:n_out], rest[n_out:]
        co, rest = rest[:c_out], rest[c_out:]
        k_scr, (ssem, rsem) = rest[:n_scr], rest[n_scr:]
        pids = [pl.program_id(d) for d in range(len(grid))]
        first = functools.reduce(jnp.logical_and, [p == 0 for p in pids])
        last = functools.reduce(jnp.logical_and, [p == g - 1 for p, g in zip(pids, grid)])

        @pl.when(first)
        def _():
            carry.start(ci, co, ssem, rsem)

        body(*k_in, *k_out, *k_scr)

        @pl.when(last)
        def _():
            carry.finish(ci, co, ssem, rsem)

    outs = pl.pallas_call(
        wrapped, name=name, grid=grid, in_specs=in_specs + [ANY] * c_in, out_specs=out_specs + [ANY] * c_out,
        out_shape=out_shape + carry.outs,
        input_output_aliases={n_in + i: n_out + o for i, o in carry.aliases.items()},
        scratch_shapes=scratch_shapes + [pltpu.SemaphoreType.DMA((carry.n_sem,))] * 2,
        compiler_params=_cp(("arbitrary",) * len(grid)),
    )(*args, *carry.ins)
    return list(outs)


def _run_carry(carry, name):
    c_in, c_out = len(carry.ins), len(carry.outs)

    def body(*refs):
        ci, co, (ssem, rsem) = refs[:c_in], refs[c_in:c_in + c_out], refs[c_in + c_out:]
        carry.start(ci, co, ssem, rsem)
        carry.finish(ci, co, ssem, rsem)

    outs = pl.pallas_call(
        body, name=name, in_specs=[ANY] * c_in, out_specs=[ANY] * c_out, out_shape=carry.outs,
        input_output_aliases=carry.aliases,
        scratch_shapes=[pltpu.SemaphoreType.DMA((carry.n_sem,))] * 2, compiler_params=_cp(),
    )(*carry.ins)
    return list(outs)


def _mm(a, b, *, name, b_mode="nn", out_shards=0, tm=1024, tn=256, tk=None, out_dtype=F32, carry=None):
    M, K = a.shape
    if b_mode == "nn":
        N = b.shape[1]
    elif b_mode == "nt":
        N = b.shape[0]
    elif b_mode == "nn_sh":
        N = b.shape[0] * b.shape[2]
    else:
        N = b.shape[1]
    tm = _row_tile(M, tm)
    if b_mode == "nn_sh":
        tn = _row_tile(b.shape[2], tn)
    elif out_shards:
        tn = _row_tile(N // out_shards, tn)
    else:
        tn = _row_tile(N, tn)
    if tk is None:
        tk = K if b_mode != "nt_shk" else b.shape[2]
    if b_mode == "nt_shk":
        tk = _row_tile(b.shape[2], tk)
    nm, nn, nk = M // tm, N // tn, K // tk

    a_spec = pl.BlockSpec((tm, tk), lambda m, n, k: (m, k))
    if b_mode == "nn":
        b_spec = pl.BlockSpec((tk, tn), lambda m, n, k: (k, n))
    elif b_mode == "nt":
        b_spec = pl.BlockSpec((tn, tk), lambda m, n, k: (n, k))
    elif b_mode == "nn_sh":
        nps = b.shape[2] // tn
        b_spec = pl.BlockSpec((None, tk, tn), lambda m, n, k: (n // nps, k, n % nps))
    else:
        kps = b.shape[2] // tk
        b_spec = pl.BlockSpec((None, tn, tk), lambda m, n, k: (k // kps, n, k % kps))
    if out_shards:
        ops = (N // out_shards) // tn
        o_spec = pl.BlockSpec((None, tm, tn), lambda m, n, k: (n // ops, m, n % ops))
        o_shape = jax.ShapeDtypeStruct((out_shards, M, N // out_shards), out_dtype)
    else:
        o_spec = pl.BlockSpec((tm, tn), lambda m, n, k: (m, n))
        o_shape = jax.ShapeDtypeStruct((M, N), out_dtype)
    trans_b = b_mode in ("nt", "nt_shk")
    dn = (((1,), (1,)), ((), ())) if trans_b else (((1,), (0,)), ((), ()))

    def body(a_ref, b_ref, o_ref, acc_ref):
        k = pl.program_id(2)

        @pl.when(k == 0)
        def _():
            acc_ref[...] = jnp.zeros(acc_ref.shape, F32)

        acc_ref[...] += lax.dot_general(a_ref[...].astype(BF16), b_ref[...].astype(BF16), dn,
                                        preferred_element_type=F32)

        @pl.when(k == nk - 1)
        def _():
            o_ref[...] = acc_ref[...].astype(o_ref.dtype)

    outs = _call(body, name=name, grid=(nm, nn, nk), in_specs=[a_spec, b_spec], out_specs=[o_spec],
                 out_shape=[o_shape], scratch_shapes=[pltpu.VMEM((tm, tn), F32)],
                 sem=("parallel", "parallel", "arbitrary"), args=(a, b), carry=carry)
    return outs[0] if carry is None else (outs[0], outs[1:])


def _rowvec(n=D):
    return pl.BlockSpec((1, n), lambda i: (0, 0))


def _prenorm_fwd(x, g, scale, shift, name, carry=None):
    S = x.shape[0]
    tr = _row_tile(S, 256)

    def body(x_ref, g_ref, sc_ref, sh_ref, h_ref, ht_ref):
        xv = x_ref[...]
        r = lax.rsqrt(jnp.mean(xv * xv, axis=-1, keepdims=True) + EPS)
        hv = (xv * r) * g_ref[...] * (1.0 + sc_ref[...]) + sh_ref[...]
        h_ref[...] = hv.astype(BF16)
        ht_ref[...] = hv.T.astype(BF16)

    outs = _call(
        body, name=name, grid=(S // tr,),
        in_specs=[pl.BlockSpec((tr, D), lambda i: (i, 0)), _rowvec(), _rowvec(), _rowvec()],
        out_specs=[pl.BlockSpec((tr, D), lambda i: (i, 0)), pl.BlockSpec((D, tr), lambda i: (0, i))],
        out_shape=[jax.ShapeDtypeStruct((S, D), BF16), jax.ShapeDtypeStruct((D, S), BF16)],
        sem=("parallel",), args=(x, g, scale, shift), carry=carry)
    return outs[:2], outs[2:]


def _prenorm_bwd(dh, dxn, x, g, scale, name, carry=None):
    S = x.shape[0]
    tr = _row_tile(S, 256)

    def body(dh_ref, dxn_ref, x_ref, g_ref, sc_ref, dx_ref, dsh_ref, dsc_ref, dg_ref):
        i = pl.program_id(0)

        @pl.when(i == 0)
        def _():
            dsh_ref[...] = jnp.zeros((1, D), F32)
            dsc_ref[...] = jnp.zeros((1, D), F32)
            dg_ref[...] = jnp.zeros((1, D), F32)

        xv = x_ref[...]
        dhv = dh_ref[...]
        gv = g_ref[...]
        mod = 1.0 + sc_ref[...]
        r = lax.rsqrt(jnp.mean(xv * xv, axis=-1, keepdims=True) + EPS)
        xh = xv * r
        dsh_ref[...] += jnp.sum(dhv, axis=0, keepdims=True)
        dsc_ref[...] += jnp.sum(dhv * (xh * gv), axis=0, keepdims=True)
        dg_ref[...] += jnp.sum(dhv * mod * xh, axis=0, keepdims=True)
        u = dhv * mod * gv
        dx_ref[...] = dxn_ref[...] + r * u - xv * (r * r * r) * jnp.mean(u * xv, axis=-1, keepdims=True)

    tile = pl.BlockSpec((tr, D), lambda i: (i, 0))
    outs = _call(
        body, name=name, grid=(S // tr,),
        in_specs=[tile, tile, tile, _rowvec(), _rowvec()],
        out_specs=[tile, _rowvec(), _rowvec(), _rowvec()],
        out_shape=[jax.ShapeDtypeStruct((S, D), F32)] + [jax.ShapeDtypeStruct((1, D), F32)] * 3,
        sem=("arbitrary",), args=(dh, dxn, x, g, scale), carry=carry)
    return outs[:4], outs[4:]


def _postnorm_fwd(x, y, gate, g, name):
    S = x.shape[0]
    tr = _row_tile(S, 256)

    def body(x_ref, y_ref, gate_ref, g_ref, o_ref):
        yv = y_ref[...]
        r = lax.rsqrt(jnp.mean(yv * yv, axis=-1, keepdims=True) + EPS)
        o_ref[...] = x_ref[...] + gate_ref[...] * ((yv * r) * g_ref[...])

    tile = pl.BlockSpec((tr, D), lambda i: (i, 0))
    return pl.pallas_call(
        body, name=name, grid=(S // tr,), in_specs=[tile, tile, _rowvec(), _rowvec()],
        out_specs=tile, out_shape=jax.ShapeDtypeStruct((S, D), F32), compiler_params=_cp(("parallel",)),
    )(x, y, gate, g)


def _postnorm_bwd(dxn, y, gate, g, name):
    S = y.shape[0]
    tr = _row_tile(S, 256)

    def body(dxn_ref, y_ref, gate_ref, g_ref, dy_ref, dgate_ref, dg_ref):
        i = pl.program_id(0)

        @pl.when(i == 0)
        def _():
            dgate_ref[...] = jnp.zeros((1, D), F32)
            dg_ref[...] = jnp.zeros((1, D), F32)

        yv = y_ref[...]
        dv = dxn_ref[...]
        gv = g_ref[...]
        gt = gate_ref[...]
        r = lax.rsqrt(jnp.mean(yv * yv, axis=-1, keepdims=True) + EPS)
        yh = yv * r
        dgate_ref[...] += jnp.sum(dv * (yh * gv), axis=0, keepdims=True)
        dg_ref[...] += jnp.sum(dv * gt * yh, axis=0, keepdims=True)
        u = dv * gt * gv
        dy_ref[...] = (r * u - yv * (r * r * r) * jnp.mean(u * yv, axis=-1, keepdims=True)).astype(BF16)

    tile = pl.BlockSpec((tr, D), lambda i: (i, 0))
    return pl.pallas_call(
        body, name=name, grid=(S // tr,), in_specs=[tile, tile, _rowvec(), _rowvec()],
        out_specs=[tile, _rowvec(), _rowvec()],
        out_shape=[jax.ShapeDtypeStruct((S, D), BF16), jax.ShapeDtypeStruct((1, D), F32),
                   jax.ShapeDtypeStruct((1, D), F32)],
        compiler_params=_cp(("arbitrary",)),
    )(dxn, y, gate, g)


def _loss_head(xo, target, name):
    S = xo.shape[0]
    tr = _row_tile(S, 256)

    def body(x_ref, t_ref, dx_ref, l_ref):
        i = pl.program_id(0)

        @pl.when(i == 0)
        def _():
            l_ref[...] = jnp.zeros((8, 128), F32)

        err = x_ref[...] - t_ref[...]
        dx_ref[...] = err * (1.0 / D)
        l_ref[...] += 0.5 * jnp.sum(jnp.mean(err * err, axis=-1, keepdims=True))

    tile = pl.BlockSpec((tr, D), lambda i: (i, 0))
    return pl.pallas_call(
        body, name=name, grid=(S // tr,), in_specs=[tile, tile],
        out_specs=[tile, pl.BlockSpec((8, 128), lambda i: (0, 0))],
        out_shape=[jax.ShapeDtypeStruct((S, D), F32), jax.ShapeDtypeStruct((8, 128), F32)],
        compiler_params=_cp(("arbitrary",)),
    )(xo, target)


def _layer_tail_fwd(proj, a_in, b_in, x, w_po, w_ho, w_out, gate, g, name, target=None, carry=None):
    S = proj.shape[0]
    tr = _row_tile(S, 256)
    nsh, _, wsh = w_po.shape
    n_in = 10 + (target is not None)

    def body(*refs):
        (mgp_ref, mgh_ref, a_ref, b_ref, x_ref, wpo_ref, who_ref, wout_ref, gate_ref, g_ref) = refs[:10]
        bra_ref, brb_ref, mt_ref, y_ref, xn_ref = refs[n_in:n_in + 5]
        av = a_ref[...]
        bra = jnp.concatenate([jnp.dot(av, wpo_ref[j], preferred_element_type=F32) for j in range(nsh)], axis=1)
        brb = jnp.dot(b_ref[...], who_ref[...], preferred_element_type=F32)
        mv = _sig(mgp_ref[...].astype(F32)) * bra + _sig(mgh_ref[...].astype(F32)) * brb
        bra_ref[...] = bra.astype(BF16)
        brb_ref[...] = brb.astype(BF16)
        mt_ref[...] = mv.T.astype(BF16)
        yv = jnp.dot(mv.astype(BF16), wout_ref[...], preferred_element_type=F32)
        y_ref[...] = yv
        r = lax.rsqrt(jnp.mean(yv * yv, axis=-1, keepdims=True) + EPS)
        xn = x_ref[...] + gate_ref[...] * ((yv * r) * g_ref[...])
        if target is None:
            xn_ref[...] = xn
        else:
            t_ref, l_ref = refs[10], refs[n_in + 5]

            @pl.when(pl.program_id(0) == 0)
            def _():
                l_ref[...] = jnp.zeros((8, 128), F32)

            err = xn - t_ref[...]
            xn_ref[...] = err * (1.0 / D)
            l_ref[...] += 0.5 * jnp.sum(jnp.mean(err * err, axis=-1, keepdims=True))

    tile = pl.BlockSpec((tr, D), lambda i: (i, 0))
    whole = lambda t: pl.BlockSpec(t.shape, lambda i: (0,) * t.ndim)
    last = target is not None
    outs = _call(
        body, name=name, grid=(S // tr,),
        in_specs=[pl.BlockSpec((tr, D), lambda i: (i, MGP_BLK)), pl.BlockSpec((tr, D), lambda i: (i, MGH_BLK)),
                  pl.BlockSpec((tr, POOL_W), lambda i: (i, 0)), tile, tile, whole(w_po), whole(w_ho),
                  whole(w_out), _rowvec(), _rowvec()] + [tile] * last,
        out_specs=[tile, tile, pl.BlockSpec((D, tr), lambda i: (0, i)), tile, tile]
        + [pl.BlockSpec((8, 128), lambda i: (0, 0))] * last,
        out_shape=[jax.ShapeDtypeStruct((S, D), BF16), jax.ShapeDtypeStruct((S, D), BF16),
                   jax.ShapeDtypeStruct((D, S), BF16), jax.ShapeDtypeStruct((S, D), F32),
                   jax.ShapeDtypeStruct((S, D), F32)] + [jax.ShapeDtypeStruct((8, 128), F32)] * last,
        sem=("arbitrary",) if last else ("parallel",),
        args=(proj, proj, a_in, b_in, x, w_po, w_ho, w_out, gate, g) + ((target,) if last else ()), carry=carry)
    return outs[:5 + last], outs[5 + last:]


def _layer_head_bwd(dxn, y, proj, br_a, br_b, w_po, w_ho, w_out, gate, g, name):
    S = y.shape[0]
    tr = _row_tile(S, 256)
    nsh, _, wsh = w_po.shape

    def body(dxn_ref, y_ref, mgp_ref, mgh_ref, bra_ref, brb_ref, wpo_ref, who_ref, wout_ref, gate_ref, g_ref,
             dy_ref, dba_ref, dbb_ref, dmg_ref, dain_ref, dbin_ref, dgate_ref, dg_ref):
        i = pl.program_id(0)

        @pl.when(i == 0)
        def _():
            dgate_ref[...] = jnp.zeros((1, D), F32)
            dg_ref[...] = jnp.zeros((1, D), F32)

        yv = y_ref[...]
        dv = dxn_ref[...]
        gv = g_ref[...]
        gt = gate_ref[...]
        r = lax.rsqrt(jnp.mean(yv * yv, axis=-1, keepdims=True) + EPS)
        yh = yv * r
        dgate_ref[...] += jnp.sum(dv * (yh * gv), axis=0, keepdims=True)
        dg_ref[...] += jnp.sum(dv * gt * yh, axis=0, keepdims=True)
        u = dv * gt * gv
        dy = (r * u - yv * (r * r * r) * jnp.mean(u * yv, axis=-1, keepdims=True)).astype(BF16)
        dy_ref[...] = dy
        dm = _dot_nt(dy, wout_ref[...])
        sp = _sig(mgp_ref[...].astype(F32))
        sh = _sig(mgh_ref[...].astype(F32))
        dba = (dm * sp).astype(BF16)
        dbb = (dm * sh).astype(BF16)
        dba_ref[...] = dba
        dbb_ref[...] = dbb
        dmg_ref[:, 0:D] = (dm * bra_ref[...].astype(F32) * sp * (1.0 - sp)).astype(BF16)
        dmg_ref[:, D:2 * D] = (dm * brb_ref[...].astype(F32) * sh * (1.0 - sh)).astype(BF16)
        dain = _dot_nt(dba[:, 0:wsh], wpo_ref[0])
        for j in range(1, nsh):
            dain = dain + _dot_nt(dba[:, j * wsh:(j + 1) * wsh], wpo_ref[j])
        dain_ref[...] = dain
        dbin_ref[...] = _dot_nt(dbb, who_ref[...])

    tile = pl.BlockSpec((tr, D), lambda i: (i, 0))
    whole = lambda t: pl.BlockSpec(t.shape, lambda i: (0,) * t.ndim)
    return pl.pallas_call(
        body, name=name, grid=(S // tr,),
        in_specs=[tile, tile, pl.BlockSpec((tr, D), lambda i: (i, MGP_BLK)),
                  pl.BlockSpec((tr, D), lambda i: (i, MGH_BLK)), tile, tile, whole(w_po), whole(w_ho),
                  whole(w_out), _rowvec(), _rowvec()],
        out_specs=[tile, tile, tile, pl.BlockSpec((tr, 2 * D), lambda i: (i, 0)),
                   pl.BlockSpec((tr, POOL_W), lambda i: (i, 0)), tile, _rowvec(), _rowvec()],
        out_shape=[jax.ShapeDtypeStruct((S, D), BF16)] * 3
        + [jax.ShapeDtypeStruct((S, 2 * D), BF16), jax.ShapeDtypeStruct((S, POOL_W), F32),
           jax.ShapeDtypeStruct((S, D), F32), jax.ShapeDtypeStruct((1, D), F32), jax.ShapeDtypeStruct((1, D), F32)],
        compiler_params=_cp(("arbitrary",)),
    )(dxn, y, proj, proj, br_a, br_b, w_po, w_ho, w_out, gate, g)


def _merge_fwd(proj, br_a, br_b, name):
    S = proj.shape[0]
    tr = _row_tile(S, 256)

    def body(mgp_ref, mgh_ref, a_ref, b_ref, o_ref, ot_ref):
        mv = _sig(mgp_ref[...]) * a_ref[...] + _sig(mgh_ref[...]) * b_ref[...]
        o_ref[...] = mv.astype(BF16)
        ot_ref[...] = mv.T.astype(BF16)

    tile = pl.BlockSpec((tr, D), lambda i: (i, 0))
    return pl.pallas_call(
        body, name=name, grid=(S // tr,),
        in_specs=[pl.BlockSpec((tr, D), lambda i: (i, MGP_BLK)), pl.BlockSpec((tr, D), lambda i: (i, MGH_BLK)),
                  tile, tile],
        out_specs=[tile, pl.BlockSpec((D, tr), lambda i: (0, i))],
        out_shape=[jax.ShapeDtypeStruct((S, D), BF16), jax.ShapeDtypeStruct((D, S), BF16)],
        compiler_params=_cp(("parallel",)),
    )(proj, proj, br_a, br_b)


def _merge_bwd(dm, proj, br_a, br_b, name):
    S = proj.shape[0]
    tr = _row_tile(S, 256)

    def body(dm_ref, mgp_ref, mgh_ref, a_ref, b_ref, da_ref, db_ref, dmg_ref):
        dmv = dm_ref[...]
        sp = _sig(mgp_ref[...])
        sh = _sig(mgh_ref[...])
        da_ref[...] = (dmv * sp).astype(BF16)
        db_ref[...] = (dmv * sh).astype(BF16)
        dmg_ref[:, 0:D] = (dmv * a_ref[...] * sp * (1.0 - sp)).astype(BF16)
        dmg_ref[:, D:2 * D] = (dmv * b_ref[...] * sh * (1.0 - sh)).astype(BF16)

    tile = pl.BlockSpec((tr, D), lambda i: (i, 0))
    return pl.pallas_call(
        body, name=name, grid=(S // tr,),
        in_specs=[tile, pl.BlockSpec((tr, D), lambda i: (i, MGP_BLK)),
                  pl.BlockSpec((tr, D), lambda i: (i, MGH_BLK)), tile, tile],
        out_specs=[tile, tile, pl.BlockSpec((tr, 2 * D), lambda i: (i, 0))],
        out_shape=[jax.ShapeDtypeStruct((S, D), BF16), jax.ShapeDtypeStruct((S, D), BF16),
                   jax.ShapeDtypeStruct((S, 2 * D), BF16)],
        compiler_params=_cp(("parallel",)),
    )(dm, proj, proj, br_a, br_b)


def _pool_pieces(u, g, S):
    rowi = lax.broadcasted_iota(jnp.int32, (S, 1), 0)

    def down(z, k):
        return jnp.where(rowi >= k, pltpu.roll(z, k, axis=0), 0.0)

    s2 = u + down(u, 1)
    s4 = s2 + down(s2, 2)
    s8 = s4 + down(s4, 4)
    s16 = s8 + down(s8, 8)
    win = jnp.where(g == 0, s2, jnp.where(g == 1, s4, jnp.where(g == 2, s8, s16)))
    w = jnp.where(g == 0, 2, jnp.where(g == 1, 4, jnp.where(g == 2, 8, 16)))
    count = jnp.minimum(rowi + 1, w).astype(F32)
    return win / count - u, count, rowi


def _pool_fwd(proj, pw, pscale, name):
    S = proj.shape[0]

    def body(pv_ref, pg_ref, pw_ref, sc_ref, a_ref, at_ref):
        g = pl.program_id(0)
        pooled, _, _ = _pool_pieces(pv_ref[...].astype(F32), g, S)
        pm = jnp.dot(pooled.astype(BF16), pw_ref[...].astype(BF16), preferred_element_type=F32)
        pgv = pg_ref[...].astype(F32)
        av = pm * sc_ref[...] * (pgv * _sig(pgv))
        a_ref[...] = av.astype(BF16)
        at_ref[...] = av.T.astype(BF16)

    return pl.pallas_call(
        body, name=name, grid=(GROUPS,),
        in_specs=[pl.BlockSpec((S, 128), lambda g: (0, PV0 + g)), pl.BlockSpec((S, 128), lambda g: (0, PG0 + g)),
                  pl.BlockSpec((None, 128, 128), lambda g: (g, 0, 0)), pl.BlockSpec((1, 128), lambda g: (0, g))],
        out_specs=[pl.BlockSpec((S, 128), lambda g: (0, g)), pl.BlockSpec((128, S), lambda g: (g, 0))],
        out_shape=[jax.ShapeDtypeStruct((S, POOL_W), BF16), jax.ShapeDtypeStruct((POOL_W, S), BF16)],
        compiler_params=_cp(("parallel",)),
    )(proj, proj, pw, pscale)


def _pool_bwd(da, proj, pw, pscale, name):
    S = proj.shape[0]

    def body(da_ref, pv_ref, pg_ref, pw_ref, sc_ref, dpv_ref, dpg_ref, dpw_ref, dsc_ref):
        g = pl.program_id(0)
        pooled, count, rowi = _pool_pieces(pv_ref[...].astype(F32), g, S)
        pwb = pw_ref[...].astype(BF16)
        pm = jnp.dot(pooled.astype(BF16), pwb, preferred_element_type=F32)
        scv = sc_ref[...]
        pgv = pg_ref[...].astype(F32)
        sg = _sig(pgv)
        dav = da_ref[...]
        d_ps = dav * (pgv * sg)
        dpg_ref[...] = (dav * (pm * scv) * _dsilu(pgv, sg)).astype(BF16)
        dsc_ref[...] = jnp.sum(d_ps * pm, axis=0, keepdims=True)
        d_pm = (d_ps * scv).astype(BF16)
        dpw_ref[...] = lax.dot_general(pooled.astype(BF16), d_pm, (((0,), (0,)), ((), ())),
                                       preferred_element_type=F32)
        d_pooled = lax.dot_general(d_pm, pwb, (((1,), (1,)), ((), ())), preferred_element_type=F32)
        z = d_pooled / count

        def up(v, k):
            return jnp.where(rowi < S - k, pltpu.roll(v, S - k, axis=0), 0.0)

        t2 = z + up(z, 1)
        t4 = t2 + up(t2, 2)
        t8 = t4 + up(t4, 4)
        t16 = t8 + up(t8, 8)
        adj = jnp.where(g == 0, t2, jnp.where(g == 1, t4, jnp.where(g == 2, t8, t16)))
        dpv_ref[...] = (adj - d_pooled).astype(BF16)

    col = lambda g: (0, g)
    return pl.pallas_call(
        body, name=name, grid=(GROUPS,),
        in_specs=[pl.BlockSpec((S, 128), col), pl.BlockSpec((S, 128), lambda g: (0, PV0 + g)),
                  pl.BlockSpec((S, 128), lambda g: (0, PG0 + g)),
                  pl.BlockSpec((None, 128, 128), lambda g: (g, 0, 0)), pl.BlockSpec((1, 128), col)],
        out_specs=[pl.BlockSpec((S, 128), col), pl.BlockSpec((S, 128), col),
                   pl.BlockSpec((None, 128, 128), lambda g: (g, 0, 0)), pl.BlockSpec((1, 128), col)],
        out_shape=[jax.ShapeDtypeStruct((S, POOL_W), BF16), jax.ShapeDtypeStruct((S, POOL_W), BF16),
                   jax.ShapeDtypeStruct((GROUPS, 128, 128), F32), jax.ShapeDtypeStruct((1, POOL_W), F32)],
        compiler_params=_cp(("parallel",)),
    )(da, proj, proj, pw, pscale)


def _chunk_cumsum(z, rowi):
    for sh in (1, 2, 4, 8, 16, 32):
        z = z + jnp.where(rowi >= sh, pltpu.roll(z, sh, axis=0), 0.0)
    return z


def _chunk_rev_cumsum(z, rowi):
    for sh in (1, 2, 4, 8, 16, 32):
        z = z + jnp.where(rowi < CH - sh, pltpu.roll(z, CH - sh, axis=0), 0.0)
    return z


def _dot_nn(a, b):
    return jnp.dot(a.astype(BF16), b.astype(BF16), preferred_element_type=F32)


def _dot_nt(a, b):
    return lax.dot_general(a.astype(BF16), b.astype(BF16), (((1,), (1,)), ((), ())), preferred_element_type=F32)


def _dot_tn(a, b):
    return lax.dot_general(a.astype(BF16), b.astype(BF16), (((0,), (0,)), ((), ())), preferred_element_type=F32)


def _gates(hq, hf, lbv):
    hq, hf = hq.astype(F32), hf.astype(F32)
    sq = _sig(hq)
    sf = _sig(hf)
    f = lbv + (1.0 - lbv) * sf
    fc = jnp.maximum(f, 1e-30)
    return hq * sq, sq, sf, f, fc, jnp.log(fc)


DECAY_CAP = 60.0


def _block_ref(c_ref, i):
    if i == 0:
        return jnp.zeros((1, HD), F32)
    return c_ref[SB * i - 1:SB * i, :]


def _block_decay(c_ref):
    spans = [_block_ref(c_ref, i) - c_ref[SB * (i + 1) - 1:SB * (i + 1), :] for i in range(CH // SB)]
    return functools.reduce(jnp.maximum, spans)


def _hgrn_fwd(proj, lb, gn, name, carry=None):
    S = proj.shape[0]
    nch = S // CH
    W = NH * HD

    def body(hq_ref, hf_ref, hi_ref, hg_ref, lb_ref, gn_ref, bin_ref, bint_ref, oraw_ref, st_ref, mild_ref,
             cum_ref, q_s, k_s, c_s, v_s, o_s, state_s, qf_s, kf_s, cf_s):
        state_s[...] = jnp.zeros((NH, HD, HD), F32)
        rowi = lax.broadcasted_iota(jnp.int32, (CH, 1), 0)
        coli = lax.broadcasted_iota(jnp.int32, (1, CH), 1)
        sbi = lax.broadcasted_iota(jnp.int32, (SB, 1), 0)
        gnv = gn_ref[...]

        def gates_pass(n, worst):
            rows = pl.ds(pl.multiple_of(n * CH, CH), CH)
            for hh in range(NH):
                lanes = slice(hh * HD, (hh + 1) * HD)
                q, _, _, f, _, logf = _gates(hq_ref[rows, lanes], hf_ref[rows, lanes], lb_ref[:, lanes])
                c = _chunk_cumsum(logf, rowi)
                qf_s[hh, rows, :] = q
                kf_s[hh, rows, :] = 1.0 - f
                cf_s[hh, rows, :] = c
                cum_ref[rows, lanes] = c
                c_s[hh] = c
                worst = jnp.maximum(worst, _block_decay(c_s.at[hh]))
            return worst

        def between_chunks(hh, n, rows):
            lanes = slice(hh * HD, (hh + 1) * HD)
            q = qf_s[hh, rows, :]
            k = kf_s[hh, rows, :]
            c = cf_s[hh, rows, :]
            v = hi_ref[rows, lanes].astype(F32)
            q_s[hh] = q
            k_s[hh] = k
            c_s[hh] = c
            v_s[hh] = v
            st = state_s[hh]
            st_ref[hh, n] = st.astype(BF16)
            o_s[hh] = _dot_nt(q * jnp.exp(c), st)
            last = c_s[hh, CH - 1:CH, :]
            state_s[hh] = st * jnp.exp(last) + _dot_tn(v, k * jnp.exp(last - c))

        def within_chunk_matmul(hh):
            q, k, c, v = q_s[hh], k_s[hh], c_s[hh], v_s[hh]
            a = jnp.zeros((CH, CH), F32)
            for i in range(CH // SB):
                r_i = _block_ref(c_s.at[hh], i)
                qi = q * jnp.exp(jnp.minimum(c - r_i, 0.0))
                kei = k * jnp.exp(jnp.minimum(r_i - c, DECAY_CAP))
                m_i = (rowi >= SB * i) & (rowi < SB * (i + 1)) & (coli <= rowi)
                a = a + jnp.where(m_i, _dot_nt(qi, kei), 0.0)
            o_s[hh] += _dot_nn(a, v)

        def within_chunk_exact(hh):
            q, k, c, v = q_s[hh], k_s[hh], c_s[hh], v_s[hh]
            a_off = jnp.zeros((CH, CH), F32)
            for i in range(1, CH // SB):
                r_i = _block_ref(c_s.at[hh], i)
                qi = q * jnp.exp(jnp.minimum(c - r_i, 0.0))
                kei = k * jnp.exp(jnp.minimum(r_i - c, 0.0))
                m_i = (rowi >= SB * i) & (rowi < SB * (i + 1)) & (coli < SB * i)
                a_off = a_off + jnp.where(m_i, _dot_nt(qi, kei), 0.0)
            o_s[hh] += _dot_nn(a_off, v)
            for i in range(CH // SB):
                blk = slice(SB * i, SB * (i + 1))
                qb = q_s[hh, blk, :]
                cb = c_s[hh, blk, :]
                acc = jnp.zeros((SB, HD), F32)
                for s in range(SB):
                    row = SB * i + s
                    w = jnp.exp(jnp.minimum(cb - c_s[hh, row:row + 1, :], 0.0))
                    a_col = jnp.sum(qb * k_s[hh, row:row + 1, :] * w, axis=-1, keepdims=True)
                    acc = acc + jnp.where(sbi >= s, a_col, 0.0) * v_s[hh, row:row + 1, :]
                o_s[hh, blk, :] += acc

        def norm_and_gate(hh, rows):
            lanes = slice(hh * HD, (hh + 1) * HD)
            ov = o_s[hh]
            oraw_ref[rows, lanes] = ov
            r = lax.rsqrt(jnp.mean(ov * ov, axis=-1, keepdims=True) + EPS)
            hg = hg_ref[rows, lanes].astype(F32)
            bin_ref[rows, lanes] = ((ov * r) * gnv * (hg * _sig(hg))).astype(BF16)

        def chunk_with(within_chunk):
            def chunk(n, carry):
                rows = pl.ds(pl.multiple_of(n * CH, CH), CH)
                for hh in range(NH):
                    between_chunks(hh, n, rows)
                for hh in range(NH):
                    within_chunk(hh)
                for hh in range(NH):
                    norm_and_gate(hh, rows)
                return carry
            return chunk

        worst = lax.fori_loop(0, nch, gates_pass, jnp.zeros((1, HD), F32))
        mild = jnp.max(worst) <= DECAY_CAP
        mild_ref[...] = jnp.broadcast_to(jnp.where(mild, 1.0, 0.0), (8, HD))

        @pl.when(mild)
        def _():
            lax.fori_loop(0, nch, chunk_with(within_chunk_matmul), 0, unroll=4)

        @pl.when(jnp.logical_not(mild))
        def _():
            lax.fori_loop(0, nch, chunk_with(within_chunk_exact), 0)

        bint_ref[...] = bin_ref[...].astype(F32).T.astype(BF16)

    col = lambda off: pl.BlockSpec((S, W), lambda h: (0, off // NH + h))
    head = pl.BlockSpec((S, W), lambda h: (0, h))
    outs = _call(
        body, name=name, grid=(HEADS // NH,),
        in_specs=[col(HQ0), col(HF0), col(HI0), col(HG0), pl.BlockSpec((1, W), lambda h: (0, h)),
                  pl.BlockSpec((1, HD), lambda h: (0, 0))],
        out_specs=[head, pl.BlockSpec((W, S), lambda h: (h, 0)), head,
                   pl.BlockSpec((NH, nch, HD, HD), lambda h: (h, 0, 0, 0)),
                   pl.BlockSpec((8, HD), lambda h: (h, 0)), head],
        out_shape=[jax.ShapeDtypeStruct((S, D), BF16), jax.ShapeDtypeStruct((D, S), BF16),
                   jax.ShapeDtypeStruct((S, D), F32), jax.ShapeDtypeStruct((HEADS, nch, HD, HD), BF16),
                   jax.ShapeDtypeStruct((8 * HEADS // NH, HD), F32), jax.ShapeDtypeStruct((S, D), F32)],
        scratch_shapes=[pltpu.VMEM((NH, CH, HD), F32)] * 5 + [pltpu.VMEM((NH, HD, HD), F32)]
        + [pltpu.VMEM((NH, S, HD), F32)] * 3,
        sem=("parallel",), args=(proj, proj, proj, proj, lb, gn), carry=carry)
    return outs[:6], outs[6:]


def _hgrn_bwd(dbin, proj, oraw, states, mild, cum, lb, gn, name, carry=None):
    S = proj.shape[0]
    nch = S // CH
    W = NH * HD

    def body(db_ref, hq_ref, hf_ref, hi_ref, hg_ref, or_ref, st_ref, mild_ref, cum_ref, lb_ref, gn_ref,
             dq_ref, df_ref, di_ref, dg_ref, dlb_ref, dgn_ref,
             q_s, k_s, c_s, v_s, do_s, dq_s, dk_s, dv_s, dc_s, dqd_s, dkd_s, f_s, sf_s, sq_s, dl_s, dst_s,
             dlb_s, dgn_s):
        dst_s[...] = jnp.zeros((NH, HD, HD), F32)
        dlb_s[...] = jnp.zeros((1, W), F32)
        dgn_s[...] = jnp.zeros((1, HD), F32)
        rowi = lax.broadcasted_iota(jnp.int32, (CH, 1), 0)
        rowi2 = lax.broadcasted_iota(jnp.int32, (CH, CH), 0)
        coli2 = lax.broadcasted_iota(jnp.int32, (CH, CH), 1)
        sbi = lax.broadcasted_iota(jnp.int32, (SB, 1), 0)
        gnv = gn_ref[...]
        def between_chunks(hh, n, rows):
            lanes = slice(hh * HD, (hh + 1) * HD)
            lbv = lb_ref[:, lanes]
            hq = hq_ref[rows, lanes].astype(F32)
            sq = _sig(hq)
            sf = _sig(hf_ref[rows, lanes].astype(F32))
            f = lbv + (1.0 - lbv) * sf
            q = hq * sq
            k = 1.0 - f
            f_s[hh] = f
            sf_s[hh] = sf
            sq_s[hh] = sq
            v = hi_ref[rows, lanes].astype(F32)
            c = cum_ref[rows, lanes]
            ov = or_ref[rows, lanes]
            hg = hg_ref[rows, lanes].astype(F32)
            sg = _sig(hg)
            r = lax.rsqrt(jnp.mean(ov * ov, axis=-1, keepdims=True) + EPS)
            dbv = db_ref[rows, lanes]
            d_on = dbv * (hg * sg)
            dg_ref[rows, lanes] = (dbv * ((ov * r) * gnv) * _dsilu(hg, sg)).astype(BF16)
            dgn_s[...] += jnp.sum(d_on * (ov * r), axis=0, keepdims=True)
            u = d_on * gnv
            do = r * u - ov * (r * r * r) * jnp.mean(u * ov, axis=-1, keepdims=True)
            q_s[hh] = q
            k_s[hh] = k
            c_s[hh] = c
            v_s[hh] = v
            do_s[hh] = do
            st = st_ref[hh, n].astype(F32)
            dst = dst_s[hh]
            ec = jnp.exp(c)
            last = c_s[hh, CH - 1:CH, :]
            el = jnp.exp(last - c)
            elast = jnp.exp(last)
            dq = _dot_nn(do, st) * ec
            dk = _dot_nn(v, dst) * el
            dq_s[hh] = dq
            dk_s[hh] = dk
            dv_s[hh] = _dot_nt(k * el, dst)
            dc_s[hh] = q * dq - k * dk
            dl_s[hh] = (jnp.sum(k * dk, axis=0, keepdims=True)
                        + elast * jnp.sum(st * dst, axis=0, keepdims=True))
            dst_s[hh] = dst * elast + _dot_tn(do, q * ec)

        def pairs_matmul(hh, first, cap, strict):
            q, k, c, v, do = q_s[hh], k_s[hh], c_s[hh], v_s[hh], do_s[hh]
            d_a = _dot_nt(do, v).astype(BF16).astype(F32)
            d_at = d_a.T
            at = jnp.zeros((CH, CH), F32)
            dq, dk, dcum = dq_s[hh], dk_s[hh], dc_s[hh]
            for i in range(first, CH // SB):
                r_i = _block_ref(c_s.at[hh], i)
                eq = jnp.exp(jnp.minimum(c - r_i, 0.0))
                ek = jnp.exp(jnp.minimum(r_i - c, cap))
                qi = (q * eq).astype(BF16).astype(F32)
                kei = (k * ek).astype(BF16).astype(F32)
                in_t = (rowi2 >= SB * i) & (rowi2 < SB * (i + 1))
                in_s = (coli2 >= SB * i) & (coli2 < SB * (i + 1))
                m_ts = in_t & ((coli2 < SB * i) if strict else (coli2 <= rowi2))
                m_st = in_s & ((rowi2 < SB * i) if strict else (rowi2 <= coli2))
                at = at + jnp.where(m_st, _dot_nt(kei, qi), 0.0)
                dq_i = _dot_nn(jnp.where(m_ts, d_a, 0.0), kei)
                dk_i = _dot_nn(jnp.where(m_st, d_at, 0.0), qi)
                dq = dq + dq_i * eq
                dk = dk + dk_i * ek
                dcum = dcum + (qi * dq_i - kei * dk_i)
            dq_s[hh] = dq
            dk_s[hh] = dk
            dc_s[hh] = dcum
            dv_s[hh] += _dot_nn(at, do)

        def pairs_exact(hh):
            dqd_s[hh] = jnp.zeros((CH, HD), F32)
            dkd_s[hh] = jnp.zeros((CH, HD), F32)
            for i in range(CH // SB):
                blk = slice(SB * i, SB * (i + 1))
                qb = q_s[hh, blk, :]
                cb = c_s[hh, blk, :]
                dob = do_s[hh, blk, :]
                dq_acc = jnp.zeros((SB, HD), F32)
                for s in range(SB):
                    row = SB * i + s
                    ks = k_s[hh, row:row + 1, :]
                    vs = v_s[hh, row:row + 1, :]
                    w = jnp.exp(jnp.minimum(cb - c_s[hh, row:row + 1, :], 0.0))
                    live = sbi >= s
                    a_col = jnp.where(live, jnp.sum(qb * ks * w, axis=-1, keepdims=True), 0.0)
                    da_col = jnp.where(live, jnp.sum(dob * vs, axis=-1, keepdims=True), 0.0)
                    dq_acc = dq_acc + da_col * ks * w
                    dkd_s[hh, row:row + 1, :] += jnp.sum(da_col * qb * w, axis=0, keepdims=True)
                    dv_s[hh, row:row + 1, :] += jnp.sum(a_col * dob, axis=0, keepdims=True)
                dqd_s[hh, blk, :] += dq_acc
            dq_d = dqd_s[hh]
            dk_d = dkd_s[hh]
            dq_s[hh] += dq_d
            dk_s[hh] += dk_d
            dc_s[hh] += q_s[hh] * dq_d - k_s[hh] * dk_d

        def gate_grads(hh, rows):
            lanes = slice(hh * HD, (hh + 1) * HD)
            lbv = lb_ref[:, lanes]
            hq = hq_ref[rows, lanes].astype(F32)
            f, sf, sq = f_s[hh], sf_s[hh], sq_s[hh]
            dlogf = _chunk_rev_cumsum(dc_s[hh], rowi) + dl_s[hh]
            dfv = jnp.where(f > 1e-30, dlogf / jnp.maximum(f, 1e-30), 0.0) - dk_s[hh]
            dlb_s[:, lanes] += jnp.sum(dfv * (1.0 - sf), axis=0, keepdims=True)
            df_ref[rows, lanes] = (dfv * (1.0 - lbv) * sf * (1.0 - sf)).astype(BF16)
            dq_ref[rows, lanes] = (dq_s[hh] * _dsilu(hq, sq)).astype(BF16)
            di_ref[rows, lanes] = dv_s[hh].astype(BF16)

        def chunk_with(pairs):
            def chunk(j, carry):
                n = nch - 1 - j
                rows = pl.ds(pl.multiple_of(n * CH, CH), CH)
                for hh in range(NH):
                    between_chunks(hh, n, rows)
                for hh in range(NH):
                    pairs(hh)
                for hh in range(NH):
                    gate_grads(hh, rows)
                return carry
            return chunk

        def pairs_mild(hh):
            pairs_matmul(hh, 0, DECAY_CAP, strict=False)

        def pairs_any(hh):
            pairs_matmul(hh, 1, 0.0, strict=True)
            pairs_exact(hh)

        mild = jnp.max(mild_ref[...]) > 0.5

        @pl.when(mild)
        def _():
            lax.fori_loop(0, nch, chunk_with(pairs_mild), 0, unroll=4)

        @pl.when(jnp.logical_not(mild))
        def _():
            lax.fori_loop(0, nch, chunk_with(pairs_any), 0)

        dlb_ref[...] = dlb_s[...]
        dgn_ref[...] = jnp.broadcast_to(dgn_s[...], (8, HD))

    col = lambda off: pl.BlockSpec((S, W), lambda h: (0, off // NH + h))
    head = pl.BlockSpec((S, W), lambda h: (0, h))
    vec = pl.BlockSpec((1, W), lambda h: (0, h))
    outs = _call(
        body, name=name, grid=(HEADS // NH,),
        in_specs=[head, col(HQ0), col(HF0), col(HI0), col(HG0), head,
                  pl.BlockSpec((NH, nch, HD, HD), lambda h: (h, 0, 0, 0)),
                  pl.BlockSpec((8, HD), lambda h: (h, 0)), head, vec, pl.BlockSpec((1, HD), lambda h: (0, 0))],
        out_specs=[head, head, head, head, vec, pl.BlockSpec((8, HD), lambda h: (h, 0))],
        out_shape=[jax.ShapeDtypeStruct((S, D), BF16)] * 4
        + [jax.ShapeDtypeStruct((1, D), F32), jax.ShapeDtypeStruct((8 * HEADS // NH, HD), F32)],
        scratch_shapes=[pltpu.VMEM((NH, CH, HD), F32)] * 14
        + [pltpu.VMEM((NH, 1, HD), F32), pltpu.VMEM((NH, HD, HD), F32), pltpu.VMEM((1, W), F32),
           pltpu.VMEM((1, HD), F32)],
        sem=("parallel",), args=(dbin, proj, proj, proj, proj, oraw, states, mild, cum, lb, gn), carry=carry)
    dq, df, di, dg, dlb, dgn = outs[:6]
    return (dq, df, di, dg, dlb, dgn.reshape(HEADS // NH, 8, HD)[:, 0, :]), outs[6:]


def _lower_bounds(l0, l1):
    m = jnp.maximum(l0, l1)
    e0 = jnp.exp(l0 - m)
    e1 = jnp.exp(l1 - m)
    tot = e0 + e1
    p0 = e0 / tot
    p1 = e1 / tot
    return jnp.clip(p0 - p0, 0.0, 1.0), jnp.clip((p0 + p1) - p0, 0.0, 1.0)


def _lb_fwd(logits):
    def body(l_ref, o_ref):
        lb0, lb1 = _lower_bounds(l_ref[0:1, :], l_ref[1:2, :])
        o_ref[0:1, :] = lb0
        o_ref[1:2, :] = lb1

    return pl.pallas_call(body, name="lb_fwd", out_shape=jax.ShapeDtypeStruct((2, D), F32))(logits)


def _lb_bwd(logits, dlb):
    def body(l_ref, d_ref, o_ref):
        _, vjp = jax.vjp(_lower_bounds, l_ref[0:1, :], l_ref[1:2, :])
        g0, g1 = vjp((d_ref[0:1, :], d_ref[1:2, :]))
        o_ref[0:1, :] = g0
        o_ref[1:2, :] = g1

    return pl.pallas_call(body, name="lb_bwd", out_shape=jax.ShapeDtypeStruct((2, D), F32))(logits, dlb)


ADA_PAD = 128


def _ada_fwd(c_pad, w_ada, b_sh):
    ns = w_ada.shape[2]

    def body(c_ref, w_ref, b_ref, o_ref):
        cv = c_ref[...]
        ca = (cv * _sig(cv)).astype(BF16)
        for l in range(2):
            res = jnp.dot(ca, w_ref[l].astype(BF16), preferred_element_type=F32)
            o_ref[:, l * ns:(l + 1) * ns] = res[0:NDEV, :] + b_ref[l:l + 1, :]

    return pl.pallas_call(body, name="ada_fwd", out_shape=jax.ShapeDtypeStruct((NDEV, 2 * ns), F32),
                          compiler_params=_cp())(c_pad, w_ada, b_sh)


def _ada_wgrad(c_pad_t, d_ada_sh):
    ns = d_ada_sh.shape[2]

    def body(c_ref, d_ref, o_ref):
        cv = c_ref[...]
        ca = (cv * _sig(cv)).astype(BF16)
        for l in range(2):
            o_ref[l] = jnp.dot(ca, d_ref[l].astype(BF16), preferred_element_type=F32)

    return pl.pallas_call(body, name="ada_wgrad", out_shape=jax.ShapeDtypeStruct((2, D, ns), F32),
                          compiler_params=_cp())(c_pad_t, d_ada_sh)


def _sum_devices(g):
    _, R, C = g.shape

    def body(g_ref, o_ref):
        acc = g_ref[0]
        for d in range(1, NDEV):
            acc = acc + g_ref[d]
        o_ref[...] = acc

    return pl.pallas_call(body, name="sum_devices", out_shape=jax.ShapeDtypeStruct((R, C), F32),
                          compiler_params=_cp())(g)


def _adamw(w, g, m, v, name, carry=None):
    R, C = w.shape
    tr = _row_tile(R, max(8, (1 << 19) // C))

    def body(w_ref, g_ref, m_ref, v_ref, d_ref, nm_ref, nv_ref):
        d_ref[...], nm_ref[...], nv_ref[...] = _adamw_update(w_ref[...], g_ref[...], m_ref[...], v_ref[...])

    tile = pl.BlockSpec((tr, C), lambda i: (i, 0))
    return _call(body, name=name, grid=(R // tr,), in_specs=[tile] * 4, out_specs=[tile] * 3,
                 out_shape=[jax.ShapeDtypeStruct((R, C), F32)] * 3, sem=("parallel",), args=(w, g, m, v),
                 carry=carry)


def _adamw_update(w, g, m, v):
    nm = B1 * m + (1.0 - B1) * g
    nv = B2 * v + (1.0 - B2) * (g * g)
    m_hat = nm / (1.0 - B1 ** STEP)
    v_hat = nv / (1.0 - B2 ** STEP)
    return -LR * (m_hat / (jnp.sqrt(v_hat) + AEPS) + WD * w), nm, nv


SMALL_PARTS = (("b_ada", 0, 6, D), ("g_pre", 8, 2, D), ("g_post", 16, 2, D), ("lb_logits", 24, 2, D),
               ("pool_w", 32, 128, D), ("pool_scale", 160, 1, D), ("hgrn_norm_g", 168, 1, 2 * HD))


def _adamw_small(g_small, g_lb_logits, wmv):
    n = len(SMALL_PARTS)

    def body(g_ref, glb_ref, *refs):
        ins, outs = refs[:3 * n], refs[3 * n:]
        for p, (key, row0, rows, width) in enumerate(SMALL_PARTS):
            gv = glb_ref[...] if key == "lb_logits" else g_ref[row0:row0 + rows, 0:width]
            res = _adamw_update(ins[3 * p][...], gv, ins[3 * p + 1][...], ins[3 * p + 2][...])
            for t in range(3):
                outs[3 * p + t][...] = res[t]

    flat = [t for triple in wmv for t in triple]
    outs = pl.pallas_call(body, name="adamw_small",
                          out_shape=[jax.ShapeDtypeStruct(t.shape, F32) for t in flat],
                          compiler_params=_cp())(g_small, g_lb_logits, *flat)
    return [outs[3 * p:3 * p + 3] for p in range(n)]


def _cast_to_slot(place, w, l, name):
    _, R, C = w.shape
    tr = _row_tile(R, max(8, (1 << 19) // C))

    def body(p_ref, w_ref, o_ref):
        o_ref[...] = w_ref[...].astype(BF16)

    return pl.pallas_call(
        body, name=name, out_shape=jax.ShapeDtypeStruct((NCHIP, R, C), BF16),
        grid_spec=pltpu.PrefetchScalarGridSpec(
            num_scalar_prefetch=1, grid=(R // tr,),
            in_specs=[pl.BlockSpec((None, tr, C), lambda i, p_ref: (l, i, 0))],
            out_specs=pl.BlockSpec((None, tr, C), lambda i, p_ref: (p_ref[0], i, 0))),
        compiler_params=_cp(("parallel",)),
    )(place, w)


def _pair_add(core, g, got, name):
    _, R, C = g.shape
    r2 = R // 2
    tr = _row_tile(r2, max(8, (1 << 19) // C))
    nt = r2 // tr

    def body(c_ref, a_ref, b_ref, o_ref):
        o_ref[...] = (a_ref[...].astype(F32) + b_ref[...].astype(F32)).astype(o_ref.dtype)

    return pl.pallas_call(
        body, name=name, out_shape=jax.ShapeDtypeStruct((NCHIP, r2, C), BF16),
        grid_spec=pltpu.PrefetchScalarGridSpec(
            num_scalar_prefetch=1, grid=(NCHIP, nt),
            in_specs=[pl.BlockSpec((None, tr, C), lambda j, i, c_ref: (j, c_ref[0] * nt + i, 0)),
                      pl.BlockSpec((None, tr, C), lambda j, i, c_ref: (j, i, 0))],
            out_specs=pl.BlockSpec((None, tr, C), lambda j, i, c_ref: (j, i, 0))),
        compiler_params=_cp(("parallel", "parallel")),
    )(core, g, got)


def _chip_sum(place, part, recv, layer, both, name):
    _, r2, C = part.shape
    tr = _row_tile(r2, max(8, (1 << 18) // C))
    nt = r2 // tr

    def body(p_ref, own_ref, r_ref, *rest):
        o_ref = rest[-1]
        me = p_ref[0]
        own = own_ref[...].astype(F32)
        acc = None
        for j in range(NCHIP):
            slot = jnp.minimum(jnp.where(j > me, j - 1, j), NCHIP - 2)
            term = jnp.where(me == j, own, r_ref[slot].astype(F32))
            acc = term if acc is None else acc + term
        o_ref[...] = acc

    args = (place, part, recv) if both is None else (place, part, recv, both)
    return pl.pallas_call(
        body, name=name, out_shape=jax.ShapeDtypeStruct((2, 2 * r2, C), F32),
        grid_spec=pltpu.PrefetchScalarGridSpec(
            num_scalar_prefetch=1, grid=(nt,),
            in_specs=[pl.BlockSpec((None, tr, C), lambda i, p_ref: (p_ref[0], i, 0)),
                      pl.BlockSpec((NCHIP - 1, tr, C), lambda i, p_ref: (0, i, 0))] + [ANY] * (len(args) - 3),
            out_specs=pl.BlockSpec((None, tr, C), lambda i, p_ref: (layer, p_ref[1] * nt + i, 0))),
        input_output_aliases={} if both is None else {3: 0},
        compiler_params=_cp(("parallel",)),
    )(*args)


def _place():
    x, y, c = lax.axis_index("x"), lax.axis_index("y"), lax.axis_index("c")
    chips = [(1 - x, y), (x, 1 - y), (1 - x, 1 - y)]
    return x, y, c, chips


def _gather_small(blk, name):
    m_per, n = blk.shape

    def body(x_ref, out_ref, send_sems, recv_sems, local_sem):
        x, y, c, chips = _place()
        me, sibling = (x, y, c), (x, y, 1 - c)

        def rows(px, py, pc):
            return out_ref.at[pl.ds((4 * px + 2 * py + pc) * m_per, m_per), :]

        def copy(k, block, to, src=None):
            return pltpu.make_async_remote_copy(
                src_ref=rows(*block) if src is None else src, dst_ref=rows(*block),
                send_sem=send_sems.at[k], recv_sem=recv_sems.at[k], device_id=to, device_id_type=MESH)

        mine = pltpu.make_async_copy(x_ref, rows(*me), local_sem)
        mine.start()
        first = [copy(0, me, sibling, src=x_ref)]
        first += [copy(1 + j, me, (*chip, c), src=x_ref) for j, chip in enumerate(chips)]
        for cp in first:
            cp.start()
        passed = [copy(4 + j, (*chip, c), sibling) for j, chip in enumerate(chips)]
        for j, chip in enumerate(chips):
            copy(1 + j, (*chip, c), me).wait_recv()
            passed[j].start()
        copy(0, sibling, me).wait_recv()
        for j, chip in enumerate(chips):
            copy(4 + j, (*chip, 1 - c), me).wait_recv()
        for cp in first + passed:
            cp.wait_send()
        mine.wait()

    return pl.pallas_call(
        body, name=name, out_shape=jax.ShapeDtypeStruct((NDEV * m_per, n), blk.dtype),
        in_specs=[pl.BlockSpec(memory_space=pltpu.VMEM)], out_specs=pl.BlockSpec(memory_space=pltpu.VMEM),
        scratch_shapes=[pltpu.SemaphoreType.DMA((7,)), pltpu.SemaphoreType.DMA((7,)), pltpu.SemaphoreType.DMA],
        compiler_params=_cp(),
    )(blk)


def _gather_rows_carry(blk):
    m_per, n = blk.shape

    def rows(ref, px, py, pc):
        return ref.at[pl.ds((4 * px + 2 * py + pc) * m_per, m_per), :]

    def copy(ins, outs, send_sems, recv_sems, k, block, to, own=False):
        return pltpu.make_async_remote_copy(
            src_ref=ins[0] if own else rows(outs[0], *block), dst_ref=rows(outs[0], *block),
            send_sem=send_sems.at[k], recv_sem=recv_sems.at[k], device_id=to, device_id_type=MESH)

    def mine(ins, outs, send_sems):
        x, y, c, _ = _place()
        return pltpu.make_async_copy(ins[0], rows(outs[0], x, y, c), send_sems.at[7])

    def start(ins, outs, send_sems, recv_sems):
        x, y, c, chips = _place()
        mine(ins, outs, send_sems).start()
        copy(ins, outs, send_sems, recv_sems, 0, (x, y, c), (x, y, 1 - c), own=True).start()
        for j, chip in enumerate(chips):
            copy(ins, outs, send_sems, recv_sems, 1 + j, (x, y, c), (*chip, c), own=True).start()

    def finish(ins, outs, send_sems, recv_sems):
        x, y, c, chips = _place()
        for j, chip in enumerate(chips):
            copy(ins, outs, send_sems, recv_sems, 1 + j, (*chip, c), (x, y, c)).wait_recv()
            copy(ins, outs, send_sems, recv_sems, 4 + j, (*chip, c), (x, y, 1 - c)).start()
        copy(ins, outs, send_sems, recv_sems, 0, (x, y, 1 - c), (x, y, c)).wait_recv()
        for j, chip in enumerate(chips):
            copy(ins, outs, send_sems, recv_sems, 4 + j, (*chip, 1 - c), (x, y, c)).wait_recv()
        copy(ins, outs, send_sems, recv_sems, 0, (x, y, c), (x, y, 1 - c), own=True).wait_send()
        for j, chip in enumerate(chips):
            copy(ins, outs, send_sems, recv_sems, 1 + j, (x, y, c), (*chip, c), own=True).wait_send()
            copy(ins, outs, send_sems, recv_sems, 4 + j, (*chip, c), (x, y, 1 - c)).wait_send()
        mine(ins, outs, send_sems).wait()

    return _Carry([blk], [jax.ShapeDtypeStruct((NDEV * m_per, n), blk.dtype)], {}, 8, start, finish)


def _gather_carry(shards, piece=(0, 1, 1)):
    n = len(shards)
    first, count, of = piece

    def rows(ref, half):
        r2 = ref.shape[1] // 2
        return pl.ds(half * r2 + first * (r2 // of), count * (r2 // of))

    def over_ici(outs, send_sems, recv_sems, a, j, chip_xy, slot):
        x, y, c, _ = _place()
        blk = outs[a].at[slot, rows(outs[a], c), :]
        return pltpu.make_async_remote_copy(
            src_ref=blk, dst_ref=blk, send_sem=send_sems.at[6 * a + j], recv_sem=recv_sems.at[6 * a + j],
            device_id=(*chip_xy, c), device_id_type=MESH)

    def over_d2d(outs, send_sems, recv_sems, a, j, slot, half):
        x, y, c, _ = _place()
        blk = outs[a].at[slot, rows(outs[a], half), :]
        return pltpu.make_async_remote_copy(
            src_ref=blk, dst_ref=blk, send_sem=send_sems.at[6 * a + 3 + j], recv_sem=recv_sems.at[6 * a + 3 + j],
            device_id=(x, y, 1 - c), device_id_type=MESH)

    def start(ins, outs, send_sems, recv_sems):
        x, y, c, chips = _place()
        for a in range(n):
            for j, chip_xy in enumerate(chips):
                over_ici(outs, send_sems, recv_sems, a, j, chip_xy, 2 * x + y).start()

    def finish(ins, outs, send_sems, recv_sems):
        x, y, c, chips = _place()
        for a in range(n):
            for j, (cx, cy) in enumerate(chips):
                over_ici(outs, send_sems, recv_sems, a, j, (cx, cy), 2 * cx + cy).wait_recv()
                over_d2d(outs, send_sems, recv_sems, a, j, 2 * cx + cy, c).start()
        for a in range(n):
            for j, (cx, cy) in enumerate(chips):
                over_d2d(outs, send_sems, recv_sems, a, j, 2 * cx + cy, 1 - c).wait_recv()
        for a in range(n):
            for j, (cx, cy) in enumerate(chips):
                over_ici(outs, send_sems, recv_sems, a, j, (cx, cy), 2 * x + y).wait_send()
                over_d2d(outs, send_sems, recv_sems, a, j, 2 * cx + cy, c).wait_send()

    return _Carry(shards, [jax.ShapeDtypeStruct(s.shape, s.dtype) for s in shards],
                  {a: a for a in range(n)}, 6 * n, start, finish)


def _rs_pair(grads, name):
    n = len(grads)

    def body(*refs):
        ins, gots = refs[:n], refs[n:2 * n]
        send_sems, recv_sems = refs[2 * n:]
        x, y, c, _ = _place()
        cps = []
        for a in range(n):
            r2 = ins[a].shape[1] // 2
            cp = pltpu.make_async_remote_copy(
                src_ref=ins[a].at[:, pl.ds((1 - c) * r2, r2), :], dst_ref=gots[a],
                send_sem=send_sems.at[a], recv_sem=recv_sems.at[a],
                device_id=(x, y, 1 - c), device_id_type=MESH)
            cp.start()
            cps.append(cp)
        for cp in cps:
            cp.wait()

    half = [jax.ShapeDtypeStruct((NCHIP, g.shape[1] // 2, g.shape[2]), g.dtype) for g in grads]
    return pl.pallas_call(
        body, name=name, out_shape=half, in_specs=[ANY] * n, out_specs=[ANY] * n,
        scratch_shapes=[pltpu.SemaphoreType.DMA((n,)), pltpu.SemaphoreType.DMA((n,))],
        compiler_params=_cp(),
    )(*grads)


def _chips_carry(parts, piece=(0, 1, 1), into=None):
    n = len(parts)
    first, count, of = piece

    def rows(ref):
        step = ref.shape[1] // of
        return pl.ds(first * step, count * step)

    def send(ins, outs, send_sems, recv_sems, a, j, chip_xy):
        x, y, c, _ = _place()
        me, them = 2 * x + y, 2 * chip_xy[0] + chip_xy[1]
        return pltpu.make_async_remote_copy(
            src_ref=ins[a].at[them, rows(ins[a]), :],
            dst_ref=outs[a].at[me - (me > them).astype(jnp.int32), rows(outs[a]), :],
            send_sem=send_sems.at[3 * a + j], recv_sem=recv_sems.at[3 * a + j],
            device_id=(*chip_xy, c), device_id_type=MESH)

    def start(ins, outs, send_sems, recv_sems):
        _, _, _, chips = _place()
        for a in range(n):
            for j, chip_xy in enumerate(chips):
                send(ins, outs, send_sems, recv_sems, a, j, chip_xy).start()

    def finish(ins, outs, send_sems, recv_sems):
        x, y, c, chips = _place()
        me = 2 * x + y
        for a in range(n):
            for j, (cx, cy) in enumerate(chips):
                them = 2 * cx + cy
                blk = outs[a].at[them - (them > me).astype(jnp.int32), rows(outs[a]), :]
                pltpu.make_async_remote_copy(
                    src_ref=blk, dst_ref=blk, send_sem=send_sems.at[3 * a + j], recv_sem=recv_sems.at[3 * a + j],
                    device_id=(cx, cy, c), device_id_type=MESH).wait_recv()
        for a in range(n):
            for j, chip_xy in enumerate(chips):
                send(ins, outs, send_sems, recv_sems, a, j, chip_xy).wait_send()

    landing = [jax.ShapeDtypeStruct((NCHIP - 1,) + p.shape[1:], p.dtype) for p in parts]
    if into is None:
        return _Carry(parts, landing, {}, 3 * n, start, finish)
    return _Carry(list(parts) + list(into), landing, {n + a: a for a in range(n)}, 3 * n, start, finish)


def _rs_swap(fulls):
    n = len(fulls)

    def body(*refs):
        outs = refs[n:2 * n]
        send_sems, recv_sems = refs[2 * n:]
        x, y, c, _ = _place()
        cps = []
        for a in range(n):
            r2 = outs[a].shape[1] // 2
            mine = outs[a].at[:, pl.ds(c * r2, r2), :]
            cp = pltpu.make_async_remote_copy(
                src_ref=mine, dst_ref=mine, send_sem=send_sems.at[a], recv_sem=recv_sems.at[a],
                device_id=(x, y, 1 - c), device_id_type=MESH)
            cp.start()
            cps.append(cp)
        for a in range(n):
            r2 = outs[a].shape[1] // 2
            blk = outs[a].at[:, pl.ds((1 - c) * r2, r2), :]
            pltpu.make_async_remote_copy(
                src_ref=blk, dst_ref=blk, send_sem=send_sems.at[a], recv_sem=recv_sems.at[a],
                device_id=(x, y, 1 - c), device_id_type=MESH).wait_recv()
        for cp in cps:
            cp.wait_send()

    return pl.pallas_call(
        body, name="rs_swap", out_shape=[jax.ShapeDtypeStruct(f.shape, f.dtype) for f in fulls],
        in_specs=[ANY] * n, out_specs=[ANY] * n, input_output_aliases={a: a for a in range(n)},
        scratch_shapes=[pltpu.SemaphoreType.DMA((n,)), pltpu.SemaphoreType.DMA((n,))],
        compiler_params=_cp(),
    )(*fulls)


def _tail_weight_grads(merged_t, b_in_t, a_in_t, dy, dbr_b, dbr_a, name, tn=256):
    S = dy.shape[0]
    nn = D // tn

    def body(mt_ref, bt_ref, at_ref, dy_ref, db_ref, da_ref, go_ref, gh_ref, gp_ref):
        go_ref[...] = jnp.dot(mt_ref[...], dy_ref[...], preferred_element_type=F32).astype(BF16)
        gh_ref[...] = jnp.dot(bt_ref[...], db_ref[...], preferred_element_type=F32).astype(BF16)
        gp_ref[...] = jnp.dot(at_ref[...], da_ref[...], preferred_element_type=F32).astype(BF16)

    left = lambda rows: pl.BlockSpec((rows, S), lambda n: (0, 0))
    right = pl.BlockSpec((S, tn), lambda n: (0, n))
    out = pl.BlockSpec((D, tn), lambda n: (0, n))
    return pl.pallas_call(
        body, name=name, grid=(nn,), in_specs=[left(D), left(D), left(POOL_W), right, right, right],
        out_specs=[out, out, pl.BlockSpec((None, POOL_W, tn), lambda n: (n, 0, 0))],
        out_shape=[jax.ShapeDtypeStruct((D, D), BF16), jax.ShapeDtypeStruct((D, D), BF16),
                   jax.ShapeDtypeStruct((NCHIP, POOL_W, D // NCHIP), BF16)],
        compiler_params=_cp(("parallel",)),
    )(merged_t, b_in_t, a_in_t, dy, dbr_b, dbr_a)


class _GatherInProj:
    def __init__(self, slot, order):
        self.slot, self.order = slot, order


def _proj_with_gather(h, w_slot, order, name, tn=256):
    S, K = h.shape
    nsh, _, ns = w_slot.shape
    tps = ns // tn
    nt = nsh * tps
    r2 = K // 2

    def body(ord_ref, h_ref, w_in_ref, o_ref, w_ref, wbuf, tile_sems, send_sems, recv_sems):
        n = pl.program_id(0)
        x, y, c, chips = _place()

        def half(slot, which):
            return w_ref.at[slot, pl.ds(which * r2, r2), :]

        def over_ici(j, slot):
            blk = half(slot, c)
            return pltpu.make_async_remote_copy(src_ref=blk, dst_ref=blk, send_sem=send_sems.at[j],
                                                recv_sem=recv_sems.at[j], device_id=(*chips[j], c),
                                                device_id_type=MESH)

        def over_d2d(j, which):
            blk = half(2 * chips[j][0] + chips[j][1], which)
            return pltpu.make_async_remote_copy(src_ref=blk, dst_ref=blk, send_sem=send_sems.at[3 + j],
                                                recv_sem=recv_sems.at[3 + j], device_id=(x, y, 1 - c),
                                                device_id_type=MESH)

        def tile_copy(step, slot):
            shard = ord_ref[step // tps]
            return pltpu.make_async_copy(w_ref.at[shard, :, pl.ds((step % tps) * tn, tn)], wbuf.at[slot],
                                         tile_sems.at[slot])

        @pl.when(n == 0)
        def _():
            for j in range(3):
                over_ici(j, 2 * x + y).start()
            tile_copy(0, 0).start()

        for j in range(3):
            @pl.when(n == (j + 1) * tps - 1)
            def _(j=j):
                over_ici(j, 2 * chips[j][0] + chips[j][1]).wait_recv()
                over_d2d(j, c).start()
                over_d2d(j, 1 - c).wait_recv()

        @pl.when(n + 1 < nt)
        def _():
            tile_copy(n + 1, (n + 1) % 2).start()

        tile_copy(n, n % 2).wait()
        o_ref[...] = jnp.dot(h_ref[...], wbuf[n % 2], preferred_element_type=F32).astype(o_ref.dtype)

        @pl.when(n == nt - 1)
        def _():
            for j in range(3):
                over_ici(j, 2 * x + y).wait_send()
                over_d2d(j, c).wait_send()

    return pl.pallas_call(
        body, name=name,
        out_shape=[jax.ShapeDtypeStruct((S, nsh * ns), BF16), jax.ShapeDtypeStruct(w_slot.shape, w_slot.dtype)],
        grid_spec=pltpu.PrefetchScalarGridSpec(
            num_scalar_prefetch=1, grid=(nt,),
            in_specs=[pl.BlockSpec((S, K), lambda n, o_ref: (0, 0)), ANY],
            out_specs=[pl.BlockSpec((S, tn), lambda n, o_ref: (0, o_ref[n // tps] * tps + n % tps)), ANY],
            scratch_shapes=[pltpu.VMEM((2, K, tn), w_slot.dtype), pltpu.SemaphoreType.DMA((2,)),
                            pltpu.SemaphoreType.DMA((6,)), pltpu.SemaphoreType.DMA((6,))]),
        input_output_aliases={2: 1},
        compiler_params=_cp(("arbitrary",)),
    )(order, h, w_slot)


def _mm_ride(a, b, carry, **kw):
    if carry is None:
        return _mm(a, b, **kw), []
    return _mm(a, b, carry=carry, **kw)


def _layer_fwd(l, x, ada, w, small, ride, target=None):
    shift, scale, gate = ada[:, 0:D], ada[:, D:2 * D], ada[:, 2 * D:3 * D]
    carry, landed = ride("prenorm")
    (h, h_t), outs = _prenorm_fwd(x, small["g_pre"][l], scale, shift, f"prenorm_fwd{l}", carry)
    landed(outs)
    carry, landed = ride("proj")
    if isinstance(carry, _GatherInProj):
        proj, full = _proj_with_gather(h, carry.slot, carry.order, f"proj{l}")
        outs = [full]
    else:
        proj, outs = _mm_ride(h, w["w_in"][l], carry, name=f"proj{l}", b_mode="nn_sh", tm=2048, out_dtype=BF16)
    landed(outs)
    a_in, a_in_t = _pool_fwd(proj, small["pool_w"][l], small["pool_scale"][l], f"pool_fwd{l}")
    carry, landed = ride("hgrn")
    (b_in, b_in_t, o_raw, states, mild, cum), outs = _hgrn_fwd(proj, small["lb"][l], small["hgrn_norm_g"][l],
                                                              f"hgrn_fwd{l}", carry=carry)
    landed(outs)
    carry, landed = ride("tail")
    (br_a, br_b, merged_t, y, *x_new), outs = _layer_tail_fwd(
        proj, a_in, b_in, x, w["w_pool_o"][l], w["w_hgrn_o"][l].reshape(D, D), w["w_out"][l].reshape(D, D),
        gate, small["g_post"][l], f"tail_fwd{l}", target=target, carry=carry)
    landed(outs)
    saved = dict(x=x, h_t=h_t, proj=proj, a_in_t=a_in_t, b_in_t=b_in_t, o_raw=o_raw, states=states, mild=mild,
                 cum=cum,
                 br_a=br_a, br_b=br_b, merged_t=merged_t, y=y, scale=scale, gate=gate)
    return x_new, saved


def _layer_bwd(l, dxn, sv, w, small, ride):
    dy, dbr_a, dbr_b, dmg, da_in, db_in, dgate, dg_post = _layer_head_bwd(
        dxn, sv["y"], sv["proj"], sv["br_a"], sv["br_b"], w["w_pool_o"][l], w["w_hgrn_o"][l].reshape(D, D),
        w["w_out"][l].reshape(D, D), sv["gate"], small["g_post"][l], f"head_bwd{l}")
    gw_out, gw_hgrn_o, gw_pool_o = _tail_weight_grads(sv["merged_t"], sv["b_in_t"], sv["a_in_t"], dy, dbr_b,
                                                      dbr_a, f"gw_tail{l}")
    big = dict(w_pool_o=gw_pool_o, w_hgrn_o=gw_hgrn_o.reshape(NCHIP, D // NCHIP, D),
               w_out=gw_out.reshape(NCHIP, D // NCHIP, D))
    carry, landed = ride["hgrn"](big)
    (dhq, dhf, dhi, dhg, dlb, dgn), outs = _hgrn_bwd(db_in, sv["proj"], sv["o_raw"], sv["states"], sv["mild"],
                                                     sv["cum"], small["lb"][l], small["hgrn_norm_g"][l],
                                                     f"hgrn_bwd{l}", carry=carry)
    landed(outs)
    dpv, dpg, dpw, dpsc = _pool_bwd(da_in, sv["proj"], small["pool_w"][l], small["pool_scale"][l],
                                    f"pool_bwd{l}")
    dproj = jnp.concatenate([dpv, dpg, dhq, dhf, dhi, dhg, dmg], axis=1)
    little = dict(dgate=dgate, g_post=dg_post, pool_w=dpw, pool_scale=dpsc, lb=dlb,
                  hgrn_norm_g=jnp.sum(dgn, axis=0, keepdims=True))
    carry, landed = ride["gw_in"](little)
    big["w_in"], outs = _mm_ride(sv["h_t"], dproj, carry, name=f"gw_in{l}", out_shards=NCHIP, out_dtype=BF16)
    landed(outs)
    carry, landed = ride["d_h"](big)
    dh, outs = _mm_ride(dproj, w["w_in"][l], carry, name=f"d_h{l}", b_mode="nt_shk", tn=1024)
    landed(outs)
    carry, landed = ride["prenorm"](big)
    (dx, dshift, dscale, dg_pre), outs = _prenorm_bwd(dh, dxn, sv["x"], small["g_pre"][l], sv["scale"],
                                                      f"prenorm_bwd{l}", carry)
    landed(outs)
    little.update(dshift=dshift, dscale=dscale, g_pre=dg_pre)
    return dx, big, little


SMALL_ROWS = 176


def _rows8(t):
    t = t.reshape(-1, D)
    return jnp.pad(t, ((0, -t.shape[0] % 8), (0, 0)))


def _pack_small_weights(b_ada, g_pre, g_post, lb_logits, pool_w, pool_scale, hgrn_norm_g):
    gn = jnp.pad(hgrn_norm_g.reshape(1, 2 * HD), ((0, 0), (0, D - 2 * HD)))
    return jnp.concatenate([_rows8(b_ada), _rows8(g_pre), _rows8(g_post), _rows8(lb_logits), _rows8(pool_w),
                            _rows8(pool_scale), _rows8(gn)], axis=0)


def _pack_small(parts):
    row_keys = ("dshift", "dscale", "dgate", "g_pre", "g_post", "lb", "pool_scale", "hgrn_norm_g")
    flat = [p[k] for p in parts for k in row_keys] + [p["pool_w"].reshape(GROUPS * 128 * 128 // D, D) for p in parts]
    nk = len(row_keys)

    def body(*refs):
        o_ref = refs[-1]
        o_ref[...] = jnp.zeros((SMALL_ROWS, D), F32)
        for l in range(2):
            dshift, dscale, dgate, g_pre, g_post, lb, pscale, gn = refs[l * nk:(l + 1) * nk]
            for r, ref in enumerate((dshift, dscale, dgate)):
                o_ref[3 * l + r:3 * l + r + 1, :] = ref[...]
            o_ref[8 + l:9 + l, :] = g_pre[...]
            o_ref[16 + l:17 + l, :] = g_post[...]
            o_ref[24 + l:25 + l, :] = lb[...]
            o_ref[160:161, l * POOL_W:(l + 1) * POOL_W] = pscale[...]
            o_ref[168:169, l * HD:(l + 1) * HD] = gn[...]
            pw = refs[2 * nk + l]
            rows = pw.shape[0]
            o_ref[32 + l * rows:32 + (l + 1) * rows, :] = pw[...]

    return pl.pallas_call(body, name="pack_small", out_shape=jax.ShapeDtypeStruct((SMALL_ROWS, D), F32),
                          compiler_params=_cp())(*flat)


def _unpack_small(p):
    return (p[0:6].reshape(2, 3 * D), p[8:10], p[16:18], p[24:26], p[32:160].reshape(2, GROUPS, 128, 128),
            p[160:161].reshape(2, POOL_W), p[168:169, 0:2 * HD].reshape(2, HD))


def kernel(x, c, w_ada, b_ada, g_pre, g_post, w_in, pool_w, pool_scale, lb_logits, hgrn_norm_g, w_pool_o, w_hgrn_o, w_out, loss_target, m_w_ada, m_b_ada, m_g_pre, m_g_post, m_w_in, m_pool_w, m_pool_scale, m_lb_logits, m_hgrn_norm_g, m_w_pool_o, m_w_hgrn_o, m_w_out, v_w_ada, v_b_ada, v_g_pre, v_g_post, v_w_in, v_pool_w, v_pool_scale, v_lb_logits, v_hgrn_norm_g, v_w_pool_o, v_w_hgrn_o, v_w_out):
    ax, ay, ac = lax.axis_index("x"), lax.axis_index("y"), lax.axis_index("c")
    chip = 2 * ax + ay
    dev = 2 * chip + ac
    xe, te = x[0], loss_target[0]
    ada_s = w_ada.shape[2]

    big_names = ("w_in", "w_pool_o", "w_hgrn_o", "w_out")
    big_w = (w_in, w_pool_o, w_hgrn_o, w_out)
    core = jnp.stack([ac]).astype(jnp.int32)
    place = jnp.stack([chip, ac]).astype(jnp.int32)
    slots = {(k, l): _cast_to_slot(place, t, l, f"cast_{k}{l}") for l in range(2) for k, t in zip(big_names, big_w)}
    w = {k: [None, None] for k in big_names}
    def fills(keys):
        def landed(outs):
            for (k, l), o in zip(keys, outs):
                w[k][l] = slots[k, l] = o
        return landed

    rest0 = [(k, 0) for k in big_names[1:]]
    rest1 = [(k, 1) for k in big_names[1:]]
    no_carry = (None, lambda outs: None)
    order = jnp.stack([chip, 2 * (1 - ax) + ay, 2 * ax + (1 - ay), 2 * (1 - ax) + (1 - ay)]).astype(jnp.int32)

    def ride_fwd0(stage):
        if stage == "proj":
            return _GatherInProj(slots["w_in", 0], order), fills([("w_in", 0)])
        if stage == "hgrn":
            return (_join_carries(_gather_carry([slots[t] for t in rest0]),
                                  _gather_carry([slots["w_in", 1]], piece=(0, 2, 4))),
                    fills(rest0 + [("w_in", 1)]))
        if stage == "tail":
            return _gather_carry([slots["w_in", 1]], piece=(2, 1, 4)), fills([("w_in", 1)])
        return no_carry

    def ride_fwd1(stage):
        if stage == "prenorm":
            return _gather_carry([slots["w_in", 1]], piece=(3, 1, 4)), fills([("w_in", 1)])
        if stage == "hgrn":
            return _gather_carry([slots[t] for t in rest1]), fills(rest1)
        return no_carry

    c_all = _gather_small(jnp.broadcast_to(c, (8, D)), "gather_c").reshape(NDEV, 8, D)[:, 0, :]
    c_pad = jnp.pad(c_all, ((0, ADA_PAD - NDEV), (0, 0)))
    b_sh = lax.dynamic_slice(b_ada, (0, chip * ada_s), (2, ada_s))
    ada_cols = _gather_small(_ada_fwd(c_pad, w_ada, b_sh), "gather_ada")
    ada_cols = ada_cols.reshape(NCHIP, 2, NDEV, 2, ada_s)[:, 0]
    ada_all = jnp.transpose(ada_cols, (2, 1, 0, 3)).reshape(2, NDEV, 3 * D)
    ada_me = lax.dynamic_slice(ada_all, (0, dev, 0), (2, 1, 3 * D))

    lbs = _lb_fwd(lb_logits)
    small = dict(g_pre=g_pre[:, None, :], g_post=g_post[:, None, :], pool_w=pool_w,
                 pool_scale=pool_scale[:, None, :], lb=lbs[:, None, :], hgrn_norm_g=hgrn_norm_g[:, None, :])

    (x1,), sv0 = _layer_fwd(0, xe, ada_me[0], w, small, ride_fwd0)
    (dx2, loss_blk), sv1 = _layer_fwd(1, x1, ada_me[1], w, small, ride_fwd1, target=te)

    parts, recv = {}, {}

    def pair_sums(keys, grads, tag):
        got = _rs_pair(grads, f"rs_pair_{tag}")
        for kl, g, o in zip(keys, grads, got):
            parts[kl] = _pair_add(core, g, o, f"rs_add_{kl[0]}{kl[1]}")

    def exchange(keys):
        def landed(outs):
            recv.update(zip(keys, outs))
        return _chips_carry([parts[kl] for kl in keys]), landed

    def early(l):
        return [(k, l) for k in big_names[1:]]

    def ride_hgrn1(big):
        pair_sums(early(1), [big[k] for k in big_names[1:]], "l1_early")
        return exchange(early(1))

    def ride_d_h1(big):
        pair_sums([("w_in", 1)], [big["w_in"]], "l1_w_in")
        return no_carry

    def ride_hgrn0(big):
        pair_sums(early(0), [big[k] for k in big_names[1:]], "l0_early")
        return exchange([("w_in", 1)] + early(0))

    def ride_d_h0(big):
        pair_sums([("w_in", 0)], [big["w_in"]], "l0_w_in")

        def landed(outs):
            (recv["w_in", 0],) = outs
        return _chips_carry([parts["w_in", 0]], piece=(0, 1, 2)), landed

    def ride_prenorm0(big):
        def landed(outs):
            (recv["w_in", 0],) = outs
        return _chips_carry([parts["w_in", 0]], piece=(1, 1, 2), into=[recv["w_in", 0]]), landed

    no_ride = lambda so_far: no_carry
    dx1, big1, little1 = _layer_bwd(1, dx2, sv1, w, small,
                                    dict(hgrn=ride_hgrn1, gw_in=no_ride, d_h=ride_d_h1, prenorm=no_ride))

    gathered = {}
    zero_row = jnp.zeros((1, D), F32)

    def ride_gw_in0(little):
        so_far = dict(little, dshift=zero_row, dscale=zero_row, g_pre=zero_row)

        def landed(outs):
            (gathered["early"],) = outs
        return _gather_rows_carry(_pack_small([so_far, little1])), landed

    dx0, big0, little0 = _layer_bwd(0, dx1, sv0, w, small,
                                    dict(hgrn=ride_hgrn0, gw_in=ride_gw_in0, d_h=ride_d_h0, prenorm=ride_prenorm0))
    loss = lax.psum(loss_blk[0, 0], ("x", "y", "c"))
    late = _rows8(jnp.stack([little0["dshift"], little0["dscale"], little0["g_pre"]]))
    late = _gather_small(late, "gather_small_late").reshape(NDEV, 8, D)
    packed = gathered["early"].reshape(NDEV, SMALL_ROWS, D)
    packed = packed.at[:, 0:2, :].set(late[:, 0:2, :]).at[:, 8:9, :].set(late[:, 2:3, :])
    red = []
    for k in big_names:
        both = _chip_sum(place, parts[k, 1], recv[k, 1], 1, None, f"rs_sum_{k}1")
        red.append(_chip_sum(place, parts[k, 0], recv[k, 0], 0, both, f"rs_sum_{k}0"))
    g_big = dict(zip(big_names, _rs_swap(red)))

    def upd(wt, g, m, v, name, carry=None):
        shp = wt.shape
        two = lambda t: t.reshape(-1, shp[-1])
        res = _adamw(two(wt), two(g), two(m), two(v), name, carry)
        return [t.reshape(shp) for t in res[:3]], res[3:]

    u_w_in, _ = upd(w_in, g_big["w_in"], m_w_in, v_w_in, "adamw_w_in")
    g_small = _sum_devices(packed)
    g_b_ada, g_g_pre, g_g_post, g_lb, g_pool_w, g_pool_scale, g_norm_g = _unpack_small(g_small)
    g_lb_logits = _lb_bwd(lb_logits, g_lb)
    d_ada_all = packed[:, 0:6, :].reshape(NDEV, 2, 3 * D)
    d_ada_sh = lax.dynamic_slice(jnp.transpose(d_ada_all, (1, 0, 2)), (0, 0, chip * ada_s), (2, NDEV, ada_s))
    d_ada_sh = jnp.pad(d_ada_sh, ((0, 0), (0, ADA_PAD - NDEV), (0, 0)))
    g_w_ada = _ada_wgrad(c_pad.T, d_ada_sh)

    u_w_ada, _ = upd(w_ada, g_w_ada, m_w_ada, v_w_ada, "adamw_w_ada")
    u_w_pool_o, _ = upd(w_pool_o, g_big["w_pool_o"], m_w_pool_o, v_w_pool_o, "adamw_w_pool_o")
    u_w_hgrn_o, _ = upd(w_hgrn_o, g_big["w_hgrn_o"], m_w_hgrn_o, v_w_hgrn_o, "adamw_w_hgrn_o")
    u_w_out, _ = upd(w_out, g_big["w_out"], m_w_out, v_w_out, "adamw_w_out")
    small_w = dict(b_ada=(b_ada, m_b_ada, v_b_ada), g_pre=(g_pre, m_g_pre, v_g_pre),
                   g_post=(g_post, m_g_post, v_g_post), lb_logits=(lb_logits, m_lb_logits, v_lb_logits),
                   pool_w=(pool_w, m_pool_w, v_pool_w), pool_scale=(pool_scale, m_pool_scale, v_pool_scale),
                   hgrn_norm_g=(hgrn_norm_g, m_hgrn_norm_g, v_hgrn_norm_g))
    in_rows = [tuple(t.reshape(rows, width) for t in small_w[key]) for key, _, rows, width in SMALL_PARTS]
    u_rows = _adamw_small(g_small, g_lb_logits, in_rows)
    u_small = {key: [t.reshape(small_w[key][0].shape) for t in u_rows[p]]
               for p, (key, _, _, _) in enumerate(SMALL_PARTS)}

    grads_out = (g_w_ada, g_b_ada, g_g_pre, g_g_post, g_big["w_in"], g_pool_w, g_pool_scale, g_lb_logits,
                 g_norm_g, g_big["w_pool_o"], g_big["w_hgrn_o"], g_big["w_out"])

    def ordered(k):
        s = lambda key: u_small[key][k]
        return (u_w_ada[k], s("b_ada"), s("g_pre"), s("g_post"), u_w_in[k], s("pool_w"), s("pool_scale"),
                s("lb_logits"), s("hgrn_norm_g"), u_w_pool_o[k], u_w_hgrn_o[k], u_w_out[k])

    return (loss, dx0[None], *grads_out, *ordered(0), *ordered(1), *ordered(2))
```

```python
import functools

import jax
import jax.numpy as jnp
from jax import lax
from jax.experimental import pallas as pl
from jax.experimental.pallas import tpu as pltpu

F32 = jnp.float32
BF16 = jnp.bfloat16
MESH = pl.DeviceIdType.MESH

D = 1024
HEADS = 8
HD = 128
GROUPS = 4
POOL_W = 512
WINDOWS = (2, 4, 8, 16)
CH = 64
SB = 16
NH = 2
IN_W = 7168
NCHIP = 4
NDEV = 8
EPS = 1e-6
PV0, PG0, HQ0, HF0, HI0, HG0 = 0, 4, 8, 16, 24, 32
MGP_BLK, MGH_BLK = 5, 6

LR, B1, B2, AEPS, WD, STEP = 0.001, 0.9, 0.999, 1e-08, 0.01, 10
VMEM_LIMIT = 56 * 1024 * 1024


def _cp(sem=None, **kw):
    if sem is not None:
        kw["dimension_semantics"] = sem
    return pltpu.CompilerParams(vmem_limit_bytes=VMEM_LIMIT, **kw)


def _sig(z):
    return 1.0 / (1.0 + jnp.exp(-z))


def _dsilu(z, s):
    return s * (1.0 + z * (1.0 - s))


def _row_tile(rows, cap):
    if rows <= cap:
        return rows
    t = 1 << (cap.bit_length() - 1)
    while rows % t:
        t //= 2
    return t


ANY = pl.BlockSpec(memory_space=pl.ANY)


class _Carry:
    def __init__(self, ins, outs, aliases, n_sem, start, finish):
        self.ins, self.outs, self.aliases, self.n_sem = list(ins), list(outs), dict(aliases), n_sem
        self.start, self.finish = start, finish


class _SemWindow:
    def __init__(self, ref, base):
        self._ref, self._base = ref, base

    @property
    def at(self):
        return self

    def __getitem__(self, k):
        return self._ref.at[self._base + k]


def _join_carries(*carries):
    ins, outs, aliases, spans, n_sem = [], [], {}, [], 0
    for cr in carries:
        aliases.update({len(ins) + i: len(outs) + o for i, o in cr.aliases.items()})
        spans.append((len(ins), len(cr.ins), len(outs), len(cr.outs), n_sem))
        ins, outs, n_sem = ins + cr.ins, outs + cr.outs, n_sem + cr.n_sem

    def run(which):
        def fn(i_refs, o_refs, send_sems, recv_sems):
            for cr, (i0, ni, o0, no, s0) in zip(carries, spans):
                getattr(cr, which)(i_refs[i0:i0 + ni], o_refs[o0:o0 + no], _SemWindow(send_sems, s0),
                                   _SemWindow(recv_sems, s0))
        return fn

    return _Carry(ins, outs, aliases, n_sem, run("start"), run("finish"))


def _call(body, *, name, grid, in_specs, out_specs, out_shape, args, scratch_shapes=(), sem=None, carry=None,
          aliases=None):
    in_specs, out_specs, out_shape = list(in_specs), list(out_specs), list(out_shape)
    scratch_shapes = list(scratch_shapes)
    aliases = dict(aliases or {})
    if carry is None:
        outs = pl.pallas_call(body, name=name, grid=grid, in_specs=in_specs, out_specs=out_specs,
                              out_shape=out_shape, scratch_shapes=scratch_shapes, input_output_aliases=aliases,
                              compiler_params=_cp(sem))(*args)
        return list(outs)
    n_in, n_out, n_scr = len(in_specs), len(out_specs), len(scratch_shapes)
    c_in, c_out = len(carry.ins), len(carry.outs)

    def wrapped(*refs):
        k_in, rest = refs[:n_in], refs[n_in:]
        ci, rest = rest[:c_in], rest[c_in:]
        k_out, rest = rest[:n_out], rest[n_out:]
        co, rest = rest[:c_out], rest[c_out:]
        k_scr, (ssem, rsem) = rest[:n_scr], rest[n_scr:]
        pids = [pl.program_id(d) for d in range(len(grid))]
        first = functools.reduce(jnp.logical_and, [p == 0 for p in pids])
        last = functools.reduce(jnp.logical_and, [p == g - 1 for p, g in zip(pids, grid)])

        @pl.when(first)
        def _():
            carry.start(ci, co, ssem, rsem)

        body(*k_in, *k_out, *k_scr)

        @pl.when(last)
        def _():
            carry.finish(ci, co, ssem, rsem)

    outs = pl.pallas_call(
        wrapped, name=name, grid=grid, in_specs=in_specs + [ANY] * c_in, out_specs=out_specs + [ANY] * c_out,
        out_shape=out_shape + carry.outs,
        input_output_aliases={**aliases, **{n_in + i: n_out + o for i, o in carry.aliases.items()}},
        scratch_shapes=scratch_shapes + [pltpu.SemaphoreType.DMA((carry.n_sem,))] * 2,
        compiler_params=_cp(("arbitrary",) * len(grid)),
    )(*args, *carry.ins)
    return list(outs)


def _run_carry(carry, name):
    c_in, c_out = len(carry.ins), len(carry.outs)

    def body(*refs):
        ci, co, (ssem, rsem) = refs[:c_in], refs[c_in:c_in + c_out], refs[c_in + c_out:]
        carry.start(ci, co, ssem, rsem)
        carry.finish(ci, co, ssem, rsem)

    outs = pl.pallas_call(
        body, name=name, in_specs=[ANY] * c_in, out_specs=[ANY] * c_out, out_shape=carry.outs,
        input_output_aliases=carry.aliases,
        scratch_shapes=[pltpu.SemaphoreType.DMA((carry.n_sem,))] * 2, compiler_params=_cp(),
    )(*carry.ins)
    return list(outs)


def _mm(a, b, *, name, b_mode="nn", out_shards=0, tm=1024, tn=256, tk=None, out_dtype=F32, carry=None):
    M, K = a.shape
    if b_mode == "nn":
        N = b.shape[1]
    elif b_mode == "nt":
        N = b.shape[0]
    elif b_mode == "nn_sh":
        N = b.shape[0] * b.shape[2]
    else:
        N = b.shape[1]
    tm = _row_tile(M, tm)
    if b_mode == "nn_sh":
        tn = _row_tile(b.shape[2], tn)
    elif out_shards:
        tn = _row_tile(N // out_shards, tn)
    else:
        tn = _row_tile(N, tn)
    if tk is None:
        tk = K if b_mode != "nt_shk" else b.shape[2]
    if b_mode == "nt_shk":
        tk = _row_tile(b.shape[2], tk)
    nm, nn, nk = M // tm, N // tn, K // tk

    a_spec = pl.BlockSpec((tm, tk), lambda m, n, k: (m, k))
    if b_mode == "nn":
        b_spec = pl.BlockSpec((tk, tn), lambda m, n, k: (k, n))
    elif b_mode == "nt":
        b_spec = pl.BlockSpec((tn, tk), lambda m, n, k: (n, k))
    elif b_mode == "nn_sh":
        nps = b.shape[2] // tn
        b_spec = pl.BlockSpec((None, tk, tn), lambda m, n, k: (n // nps, k, n % nps))
    else:
        kps = b.shape[2] // tk
        b_spec = pl.BlockSpec((None, tn, tk), lambda m, n, k: (k // kps, n, k % kps))
    if out_shards:
        ops = (N // out_shards) // tn
        o_spec = pl.BlockSpec((None, tm, tn), lambda m, n, k: (n // ops, m, n % ops))
        o_shape = jax.ShapeDtypeStruct((out_shards, M, N // out_shards), out_dtype)
    else:
        o_spec = pl.BlockSpec((tm, tn), lambda m, n, k: (m, n))
        o_shape = jax.ShapeDtypeStruct((M, N), out_dtype)
    trans_b = b_mode in ("nt", "nt_shk")
    dn = (((1,), (1,)), ((), ())) if trans_b else (((1,), (0,)), ((), ()))

    def body(a_ref, b_ref, o_ref, acc_ref):
        k = pl.program_id(2)

        @pl.when(k == 0)
        def _():
            acc_ref[...] = jnp.zeros(acc_ref.shape, F32)

        acc_ref[...] += lax.dot_general(a_ref[...].astype(BF16), b_ref[...].astype(BF16), dn,
                                        preferred_element_type=F32)

        @pl.when(k == nk - 1)
        def _():
            o_ref[...] = acc_ref[...].astype(o_ref.dtype)

    outs = _call(body, name=name, grid=(nm, nn, nk), in_specs=[a_spec, b_spec], out_specs=[o_spec],
                 out_shape=[o_shape], scratch_shapes=[pltpu.VMEM((tm, tn), F32)],
                 sem=("parallel", "parallel", "arbitrary"), args=(a, b), carry=carry)
    return outs[0] if carry is None else (outs[0], outs[1:])


def _rowvec(n=D):
    return pl.BlockSpec((1, n), lambda i: (0, 0))


def _prenorm_fwd(x, g, scale, shift, name, carry=None):
    S = x.shape[0]
    tr = _row_tile(S, 256)

    def body(x_ref, g_ref, sc_ref, sh_ref, h_ref, ht_ref):
        xv = x_ref[...]
        r = lax.rsqrt(jnp.mean(xv * xv, axis=-1, keepdims=True) + EPS)
        hv = (xv * r) * g_ref[...] * (1.0 + sc_ref[...]) + sh_ref[...]
        h_ref[...] = hv.astype(BF16)
        ht_ref[...] = hv.T.astype(BF16)

    outs = _call(
        body, name=name, grid=(S // tr,),
        in_specs=[pl.BlockSpec((tr, D), lambda i: (i, 0)), _rowvec(), _rowvec(), _rowvec()],
        out_specs=[pl.BlockSpec((tr, D), lambda i: (i, 0)), pl.BlockSpec((D, tr), lambda i: (0, i))],
        out_shape=[jax.ShapeDtypeStruct((S, D), BF16), jax.ShapeDtypeStruct((D, S), BF16)],
        sem=("parallel",), args=(x, g, scale, shift), carry=carry)
    return outs[:2], outs[2:]


def _prenorm_bwd(dh, dxn, x, g, scale, name, carry=None):
    S = x.shape[0]
    tr = _row_tile(S, 256)

    def body(dh_ref, dxn_ref, x_ref, g_ref, sc_ref, dx_ref, dsh_ref, dsc_ref, dg_ref):
        i = pl.program_id(0)

        @pl.when(i == 0)
        def _():
            dsh_ref[...] = jnp.zeros((1, D), F32)
            dsc_ref[...] = jnp.zeros((1, D), F32)
            dg_ref[...] = jnp.zeros((1, D), F32)

        xv = x_ref[...]
        dhv = dh_ref[...]
        gv = g_ref[...]
        mod = 1.0 + sc_ref[...]
        r = lax.rsqrt(jnp.mean(xv * xv, axis=-1, keepdims=True) + EPS)
        xh = xv * r
        dsh_ref[...] += jnp.sum(dhv, axis=0, keepdims=True)
        dsc_ref[...] += jnp.sum(dhv * (xh * gv), axis=0, keepdims=True)
        dg_ref[...] += jnp.sum(dhv * mod * xh, axis=0, keepdims=True)
        u = dhv * mod * gv
        dx_ref[...] = dxn_ref[...] + r * u - xv * (r * r * r) * jnp.mean(u * xv, axis=-1, keepdims=True)

    tile = pl.BlockSpec((tr, D), lambda i: (i, 0))
    outs = _call(
        body, name=name, grid=(S // tr,),
        in_specs=[tile, tile, tile, _rowvec(), _rowvec()],
        out_specs=[tile, _rowvec(), _rowvec(), _rowvec()],
        out_shape=[jax.ShapeDtypeStruct((S, D), F32)] + [jax.ShapeDtypeStruct((1, D), F32)] * 3,
        sem=("arbitrary",), args=(dh, dxn, x, g, scale), carry=carry)
    return outs[:4], outs[4:]


def _postnorm_fwd(x, y, gate, g, name):
    S = x.shape[0]
    tr = _row_tile(S, 256)

    def body(x_ref, y_ref, gate_ref, g_ref, o_ref):
        yv = y_ref[...]
        r = lax.rsqrt(jnp.mean(yv * yv, axis=-1, keepdims=True) + EPS)
        o_ref[...] = x_ref[...] + gate_ref[...] * ((yv * r) * g_ref[...])

    tile = pl.BlockSpec((tr, D), lambda i: (i, 0))
    return pl.pallas_call(
        body, name=name, grid=(S // tr,), in_specs=[tile, tile, _rowvec(), _rowvec()],
        out_specs=tile, out_shape=jax.ShapeDtypeStruct((S, D), F32), compiler_params=_cp(("parallel",)),
    )(x, y, gate, g)


def _postnorm_bwd(dxn, y, gate, g, name):
    S = y.shape[0]
    tr = _row_tile(S, 256)

    def body(dxn_ref, y_ref, gate_ref, g_ref, dy_ref, dgate_ref, dg_ref):
        i = pl.program_id(0)

        @pl.when(i == 0)
        def _():
            dgate_ref[...] = jnp.zeros((1, D), F32)
            dg_ref[...] = jnp.zeros((1, D), F32)

        yv = y_ref[...]
        dv = dxn_ref[...]
        gv = g_ref[...]
        gt = gate_ref[...]
        r = lax.rsqrt(jnp.mean(yv * yv, axis=-1, keepdims=True) + EPS)
        yh = yv * r
        dgate_ref[...] += jnp.sum(dv * (yh * gv), axis=0, keepdims=True)
        dg_ref[...] += jnp.sum(dv * gt * yh, axis=0, keepdims=True)
        u = dv * gt * gv
        dy_ref[...] = (r * u - yv * (r * r * r) * jnp.mean(u * yv, axis=-1, keepdims=True)).astype(BF16)

    tile = pl.BlockSpec((tr, D), lambda i: (i, 0))
    return pl.pallas_call(
        body, name=name, grid=(S // tr,), in_specs=[tile, tile, _rowvec(), _rowvec()],
        out_specs=[tile, _rowvec(), _rowvec()],
        out_shape=[jax.ShapeDtypeStruct((S, D), BF16), jax.ShapeDtypeStruct((1, D), F32),
                   jax.ShapeDtypeStruct((1, D), F32)],
        compiler_params=_cp(("arbitrary",)),
    )(dxn, y, gate, g)


def _loss_head(xo, target, name):
    S = xo.shape[0]
    tr = _row_tile(S, 256)

    def body(x_ref, t_ref, dx_ref, l_ref):
        i = pl.program_id(0)

        @pl.when(i == 0)
        def _():
            l_ref[...] = jnp.zeros((8, 128), F32)

        err = x_ref[...] - t_ref[...]
        dx_ref[...] = err * (1.0 / D)
        l_ref[...] += 0.5 * jnp.sum(jnp.mean(err * err, axis=-1, keepdims=True))

    tile = pl.BlockSpec((tr, D), lambda i: (i, 0))
    return pl.pallas_call(
        body, name=name, grid=(S // tr,), in_specs=[tile, tile],
        out_specs=[tile, pl.BlockSpec((8, 128), lambda i: (0, 0))],
        out_shape=[jax.ShapeDtypeStruct((S, D), F32), jax.ShapeDtypeStruct((8, 128), F32)],
        compiler_params=_cp(("arbitrary",)),
    )(xo, target)


def _layer_tail_fwd(proj, a_in, b_in, x, w_po, w_ho, w_out, gate, g, name, target=None, carry=None):
    S = proj.shape[0]
    tr = _row_tile(S, 256)
    nsh, _, wsh = w_po.shape
    n_in = 10 + (target is not None)

    def body(*refs):
        (mgp_ref, mgh_ref, a_ref, b_ref, x_ref, wpo_ref, who_ref, wout_ref, gate_ref, g_ref) = refs[:10]
        bra_ref, brb_ref, mt_ref, y_ref, xn_ref = refs[n_in:n_in + 5]
        av = a_ref[...]
        bra = jnp.concatenate([jnp.dot(av, wpo_ref[j], preferred_element_type=F32) for j in range(nsh)], axis=1)
        brb = jnp.dot(b_ref[...], who_ref[...], preferred_element_type=F32)
        mv = _sig(mgp_ref[...].astype(F32)) * bra + _sig(mgh_ref[...].astype(F32)) * brb
        bra_ref[...] = bra.astype(BF16)
        brb_ref[...] = brb.astype(BF16)
        mt_ref[...] = mv.T.astype(BF16)
        yv = jnp.dot(mv.astype(BF16), wout_ref[...], preferred_element_type=F32)
        y_ref[...] = yv
        r = lax.rsqrt(jnp.mean(yv * yv, axis=-1, keepdims=True) + EPS)
        xn = x_ref[...] + gate_ref[...] * ((yv * r) * g_ref[...])
        if target is None:
            xn_ref[...] = xn
        else:
            t_ref, l_ref = refs[10], refs[n_in + 5]

            @pl.when(pl.program_id(0) == 0)
            def _():
                l_ref[...] = jnp.zeros((8, 128), F32)

            err = xn - t_ref[...]
            xn_ref[...] = err * (1.0 / D)
            l_ref[...] += 0.5 * jnp.sum(jnp.mean(err * err, axis=-1, keepdims=True))

    tile = pl.BlockSpec((tr, D), lambda i: (i, 0))
    whole = lambda t: pl.BlockSpec(t.shape, lambda i: (0,) * t.ndim)
    last = target is not None
    outs = _call(
        body, name=name, grid=(S // tr,),
        in_specs=[pl.BlockSpec((tr, D), lambda i: (i, MGP_BLK)), pl.BlockSpec((tr, D), lambda i: (i, MGH_BLK)),
                  pl.BlockSpec((tr, POOL_W), lambda i: (i, 0)), tile, tile, whole(w_po), whole(w_ho),
                  whole(w_out), _rowvec(), _rowvec()] + [tile] * last,
        out_specs=[tile, tile, pl.BlockSpec((D, tr), lambda i: (0, i)), tile, tile]
        + [pl.BlockSpec((8, 128), lambda i: (0, 0))] * last,
        out_shape=[jax.ShapeDtypeStruct((S, D), BF16), jax.ShapeDtypeStruct((S, D), BF16),
                   jax.ShapeDtypeStruct((D, S), BF16), jax.ShapeDtypeStruct((S, D), F32),
                   jax.ShapeDtypeStruct((S, D), F32)] + [jax.ShapeDtypeStruct((8, 128), F32)] * last,
        sem=("arbitrary",) if last else ("parallel",),
        args=(proj, proj, a_in, b_in, x, w_po, w_ho, w_out, gate, g) + ((target,) if last else ()), carry=carry)
    return outs[:5 + last], outs[5 + last:]


def _layer_head_bwd(dxn, y, proj, br_a, br_b, w_po, w_ho, w_out, gate, g, name):
    S = y.shape[0]
    tr = _row_tile(S, 256)
    nsh, _, wsh = w_po.shape

    def body(dxn_ref, y_ref, mgp_ref, mgh_ref, bra_ref, brb_ref, wpo_ref, who_ref, wout_ref, gate_ref, g_ref,
             dy_ref, dba_ref, dbb_ref, dproj_ref, dain_ref, dbin_ref, dgate_ref, dg_ref, dmgh_s):
        i = pl.program_id(0)
        j = pl.program_id(1)

        @pl.when((i == 0) & (j == 0))
        def _():
            dgate_ref[...] = jnp.zeros((1, D), F32)
            dg_ref[...] = jnp.zeros((1, D), F32)

        @pl.when(j == 1)
        def _():
            dproj_ref[...] = dmgh_s[...]

        @pl.when(j == 0)
        def _():
            everything(dxn_ref, y_ref, mgp_ref, mgh_ref, bra_ref, brb_ref, wpo_ref, who_ref, wout_ref, gate_ref,
                       g_ref, dy_ref, dba_ref, dbb_ref, dproj_ref, dain_ref, dbin_ref, dgate_ref, dg_ref, dmgh_s)

    def everything(dxn_ref, y_ref, mgp_ref, mgh_ref, bra_ref, brb_ref, wpo_ref, who_ref, wout_ref, gate_ref, g_ref,
                   dy_ref, dba_ref, dbb_ref, dproj_ref, dain_ref, dbin_ref, dgate_ref, dg_ref, dmgh_s):
        yv = y_ref[...]
        dv = dxn_ref[...]
        gv = g_ref[...]
        gt = gate_ref[...]
        r = lax.rsqrt(jnp.mean(yv * yv, axis=-1, keepdims=True) + EPS)
        yh = yv * r
        dgate_ref[...] += jnp.sum(dv * (yh * gv), axis=0, keepdims=True)
        dg_ref[...] += jnp.sum(dv * gt * yh, axis=0, keepdims=True)
        u = dv * gt * gv
        dy = (r * u - yv * (r * r * r) * jnp.mean(u * yv, axis=-1, keepdims=True)).astype(BF16)
        dy_ref[...] = dy
        dm = _dot_nt(dy, wout_ref[...])
        sp = _sig(mgp_ref[...].astype(F32))
        sh = _sig(mgh_ref[...].astype(F32))
        dba = (dm * sp).astype(BF16)
        dbb = (dm * sh).astype(BF16)
        dba_ref[...] = dba
        dbb_ref[...] = dbb
        dproj_ref[...] = (dm * bra_ref[...].astype(F32) * sp * (1.0 - sp)).astype(BF16)
        dmgh_s[...] = (dm * brb_ref[...].astype(F32) * sh * (1.0 - sh)).astype(BF16)
        dain = _dot_nt(dba[:, 0:wsh], wpo_ref[0])
        for k in range(1, nsh):
            dain = dain + _dot_nt(dba[:, k * wsh:(k + 1) * wsh], wpo_ref[k])
        dain_ref[...] = dain
        dbin_ref[...] = _dot_nt(dbb, who_ref[...])

    tile = pl.BlockSpec((tr, D), lambda i, j: (i, 0))
    whole = lambda t: pl.BlockSpec(t.shape, lambda i, j: (0,) * t.ndim)
    vec = pl.BlockSpec((1, D), lambda i, j: (0, 0))
    return pl.pallas_call(
        body, name=name, grid=(S // tr, 2),
        in_specs=[tile, tile, pl.BlockSpec((tr, D), lambda i, j: (i, MGP_BLK)),
                  pl.BlockSpec((tr, D), lambda i, j: (i, MGH_BLK)), tile, tile, whole(w_po), whole(w_ho),
                  whole(w_out), vec, vec],
        out_specs=[tile, tile, tile, pl.BlockSpec((tr, D), lambda i, j: (i, MGP_BLK + j)),
                   pl.BlockSpec((tr, POOL_W), lambda i, j: (i, 0)), tile, vec, vec],
        out_shape=[jax.ShapeDtypeStruct((S, D), BF16)] * 3
        + [jax.ShapeDtypeStruct((S, IN_W), BF16), jax.ShapeDtypeStruct((S, POOL_W), F32),
           jax.ShapeDtypeStruct((S, D), F32), jax.ShapeDtypeStruct((1, D), F32), jax.ShapeDtypeStruct((1, D), F32)],
        scratch_shapes=[pltpu.VMEM((tr, D), BF16)],
        compiler_params=_cp(("arbitrary", "arbitrary")),
    )(dxn, y, proj, proj, br_a, br_b, w_po, w_ho, w_out, gate, g)


def _merge_fwd(proj, br_a, br_b, name):
    S = proj.shape[0]
    tr = _row_tile(S, 256)

    def body(mgp_ref, mgh_ref, a_ref, b_ref, o_ref, ot_ref):
        mv = _sig(mgp_ref[...]) * a_ref[...] + _sig(mgh_ref[...]) * b_ref[...]
        o_ref[...] = mv.astype(BF16)
        ot_ref[...] = mv.T.astype(BF16)

    tile = pl.BlockSpec((tr, D), lambda i: (i, 0))
    return pl.pallas_call(
        body, name=name, grid=(S // tr,),
        in_specs=[pl.BlockSpec((tr, D), lambda i: (i, MGP_BLK)), pl.BlockSpec((tr, D), lambda i: (i, MGH_BLK)),
                  tile, tile],
        out_specs=[tile, pl.BlockSpec((D, tr), lambda i: (0, i))],
        out_shape=[jax.ShapeDtypeStruct((S, D), BF16), jax.ShapeDtypeStruct((D, S), BF16)],
        compiler_params=_cp(("parallel",)),
    )(proj, proj, br_a, br_b)


def _merge_bwd(dm, proj, br_a, br_b, name):
    S = proj.shape[0]
    tr = _row_tile(S, 256)

    def body(dm_ref, mgp_ref, mgh_ref, a_ref, b_ref, da_ref, db_ref, dmg_ref):
        dmv = dm_ref[...]
        sp = _sig(mgp_ref[...])
        sh = _sig(mgh_ref[...])
        da_ref[...] = (dmv * sp).astype(BF16)
        db_ref[...] = (dmv * sh).astype(BF16)
        dmg_ref[:, 0:D] = (dmv * a_ref[...] * sp * (1.0 - sp)).astype(BF16)
        dmg_ref[:, D:2 * D] = (dmv * b_ref[...] * sh * (1.0 - sh)).astype(BF16)

    tile = pl.BlockSpec((tr, D), lambda i: (i, 0))
    return pl.pallas_call(
        body, name=name, grid=(S // tr,),
        in_specs=[tile, pl.BlockSpec((tr, D), lambda i: (i, MGP_BLK)),
                  pl.BlockSpec((tr, D), lambda i: (i, MGH_BLK)), tile, tile],
        out_specs=[tile, tile, pl.BlockSpec((tr, 2 * D), lambda i: (i, 0))],
        out_shape=[jax.ShapeDtypeStruct((S, D), BF16), jax.ShapeDtypeStruct((S, D), BF16),
                   jax.ShapeDtypeStruct((S, 2 * D), BF16)],
        compiler_params=_cp(("parallel",)),
    )(dm, proj, proj, br_a, br_b)


def _pool_pieces(u, g, S):
    rowi = lax.broadcasted_iota(jnp.int32, (S, 1), 0)

    def down(z, k):
        return jnp.where(rowi >= k, pltpu.roll(z, k, axis=0), 0.0)

    s2 = u + down(u, 1)
    s4 = s2 + down(s2, 2)
    s8 = s4 + down(s4, 4)
    s16 = s8 + down(s8, 8)
    win = jnp.where(g == 0, s2, jnp.where(g == 1, s4, jnp.where(g == 2, s8, s16)))
    w = jnp.where(g == 0, 2, jnp.where(g == 1, 4, jnp.where(g == 2, 8, 16)))
    count = jnp.minimum(rowi + 1, w).astype(F32)
    return win / count - u, count, rowi


def _pool_fwd(proj, pw, pscale, name):
    S = proj.shape[0]

    def body(pv_ref, pg_ref, pw_ref, sc_ref, a_ref, at_ref):
        g = pl.program_id(0)
        pooled, _, _ = _pool_pieces(pv_ref[...].astype(F32), g, S)
        pm = jnp.dot(pooled.astype(BF16), pw_ref[...].astype(BF16), preferred_element_type=F32)
        pgv = pg_ref[...].astype(F32)
        av = pm * sc_ref[...] * (pgv * _sig(pgv))
        a_ref[...] = av.astype(BF16)
        at_ref[...] = av.T.astype(BF16)

    return pl.pallas_call(
        body, name=name, grid=(GROUPS,),
        in_specs=[pl.BlockSpec((S, 128), lambda g: (0, PV0 + g)), pl.BlockSpec((S, 128), lambda g: (0, PG0 + g)),
                  pl.BlockSpec((None, 128, 128), lambda g: (g, 0, 0)), pl.BlockSpec((1, 128), lambda g: (0, g))],
        out_specs=[pl.BlockSpec((S, 128), lambda g: (0, g)), pl.BlockSpec((128, S), lambda g: (g, 0))],
        out_shape=[jax.ShapeDtypeStruct((S, POOL_W), BF16), jax.ShapeDtypeStruct((POOL_W, S), BF16)],
        compiler_params=_cp(("parallel",)),
    )(proj, proj, pw, pscale)


def _pool_bwd(da, proj, pw, pscale, dproj, name):
    S = proj.shape[0]

    def body(da_ref, pv_ref, pg_ref, pw_ref, sc_ref, dproj_in, dproj_ref, dpw_ref, dsc_ref, dpg_s):
        @pl.when(pl.program_id(1) == 1)
        def _():
            dproj_ref[...] = dpg_s[...]

        @pl.when(pl.program_id(1) == 0)
        def _():
            group(da_ref, pv_ref, pg_ref, pw_ref, sc_ref, dproj_ref, dpg_s, dpw_ref, dsc_ref)

    def group(da_ref, pv_ref, pg_ref, pw_ref, sc_ref, dpv_ref, dpg_ref, dpw_ref, dsc_ref):
        g = pl.program_id(0)
        pooled, count, rowi = _pool_pieces(pv_ref[...].astype(F32), g, S)
        pwb = pw_ref[...].astype(BF16)
        pm = jnp.dot(pooled.astype(BF16), pwb, preferred_element_type=F32)
        scv = sc_ref[...]
        pgv = pg_ref[...].astype(F32)
        sg = _sig(pgv)
        dav = da_ref[...]
        d_ps = dav * (pgv * sg)
        dpg_ref[...] = (dav * (pm * scv) * _dsilu(pgv, sg)).astype(BF16)
        dsc_ref[...] = jnp.sum(d_ps * pm, axis=0, keepdims=True)
        d_pm = (d_ps * scv).astype(BF16)
        dpw_ref[...] = lax.dot_general(pooled.astype(BF16), d_pm, (((0,), (0,)), ((), ())),
                                       preferred_element_type=F32)
        d_pooled = lax.dot_general(d_pm, pwb, (((1,), (1,)), ((), ())), preferred_element_type=F32)
        z = d_pooled / count

        def up(v, k):
            return jnp.where(rowi < S - k, pltpu.roll(v, S - k, axis=0), 0.0)

        t2 = z + up(z, 1)
        t4 = t2 + up(t2, 2)
        t8 = t4 + up(t4, 4)
        t16 = t8 + up(t8, 8)
        adj = jnp.where(g == 0, t2, jnp.where(g == 1, t4, jnp.where(g == 2, t8, t16)))
        dpv_ref[...] = (adj - d_pooled).astype(BF16)

    col = lambda g, j: (0, g)
    return pl.pallas_call(
        body, name=name, grid=(GROUPS, 2),
        in_specs=[pl.BlockSpec((S, 128), col), pl.BlockSpec((S, 128), lambda g, j: (0, PV0 + g)),
                  pl.BlockSpec((S, 128), lambda g, j: (0, PG0 + g)),
                  pl.BlockSpec((None, 128, 128), lambda g, j: (g, 0, 0)), pl.BlockSpec((1, 128), col), ANY],
        out_specs=[pl.BlockSpec((S, 128), lambda g, j: (0, PV0 + g + (PG0 - PV0) * j)),
                   pl.BlockSpec((None, 128, 128), lambda g, j: (g, 0, 0)), pl.BlockSpec((1, 128), col)],
        out_shape=[jax.ShapeDtypeStruct(dproj.shape, dproj.dtype),
                   jax.ShapeDtypeStruct((GROUPS, 128, 128), F32), jax.ShapeDtypeStruct((1, POOL_W), F32)],
        scratch_shapes=[pltpu.VMEM((S, 128), BF16)], input_output_aliases={5: 0},
        compiler_params=_cp(("arbitrary", "arbitrary")),
    )(da, proj, proj, pw, pscale, dproj)


def _chunk_cumsum(z, rowi):
    for sh in (1, 2, 4, 8, 16, 32):
        z = z + jnp.where(rowi >= sh, pltpu.roll(z, sh, axis=0), 0.0)
    return z


def _chunk_rev_cumsum(z, rowi):
    for sh in (1, 2, 4, 8, 16, 32):
        z = z + jnp.where(rowi < CH - sh, pltpu.roll(z, CH - sh, axis=0), 0.0)
    return z


def _dot_nn(a, b):
    return jnp.dot(a.astype(BF16), b.astype(BF16), preferred_element_type=F32)


def _dot_nt(a, b):
    return lax.dot_general(a.astype(BF16), b.astype(BF16), (((1,), (1,)), ((), ())), preferred_element_type=F32)


def _dot_tn(a, b):
    return lax.dot_general(a.astype(BF16), b.astype(BF16), (((0,), (0,)), ((), ())), preferred_element_type=F32)


def _gates(hq, hf, lbv):
    hq, hf = hq.astype(F32), hf.astype(F32)
    sq = _sig(hq)
    sf = _sig(hf)
    f = lbv + (1.0 - lbv) * sf
    fc = jnp.maximum(f, 1e-30)
    return hq * sq, sq, sf, f, fc, jnp.log(fc)


DECAY_CAP = 60.0


def _block_ref(c_ref, i):
    if i == 0:
        return jnp.zeros((1, HD), F32)
    return c_ref[SB * i - 1:SB * i, :]


def _block_decay(c_ref):
    spans = [_block_ref(c_ref, i) - c_ref[SB * (i + 1) - 1:SB * (i + 1), :] for i in range(CH // SB)]
    return functools.reduce(jnp.maximum, spans)


def _hgrn_fwd(proj, lb, gn, name, carry=None):
    S = proj.shape[0]
    nch = S // CH
    W = NH * HD

    def body(hq_ref, hf_ref, hi_ref, hg_ref, lb_ref, gn_ref, bin_ref, bint_ref, oraw_ref, st_ref, mild_ref,
             cum_ref, q_s, k_s, c_s, v_s, o_s, state_s, qf_s, kf_s, cf_s):
        state_s[...] = jnp.zeros((NH, HD, HD), F32)
        rowi = lax.broadcasted_iota(jnp.int32, (CH, 1), 0)
        coli = lax.broadcasted_iota(jnp.int32, (1, CH), 1)
        sbi = lax.broadcasted_iota(jnp.int32, (SB, 1), 0)
        gnv = gn_ref[...]

        def gates_pass(n, worst):
            rows = pl.ds(pl.multiple_of(n * CH, CH), CH)
            for hh in range(NH):
                lanes = slice(hh * HD, (hh + 1) * HD)
                q, _, _, f, _, logf = _gates(hq_ref[rows, lanes], hf_ref[rows, lanes], lb_ref[:, lanes])
                c = _chunk_cumsum(logf, rowi)
                qf_s[hh, rows, :] = q
                kf_s[hh, rows, :] = 1.0 - f
                cf_s[hh, rows, :] = c
                cum_ref[rows, lanes] = c
                c_s[hh] = c
                worst = jnp.maximum(worst, _block_decay(c_s.at[hh]))
            return worst

        def between_chunks(hh, n, rows):
            lanes = slice(hh * HD, (hh + 1) * HD)
            q = qf_s[hh, rows, :]
            k = kf_s[hh, rows, :]
            c = cf_s[hh, rows, :]
            v = hi_ref[rows, lanes].astype(F32)
            q_s[hh] = q
            k_s[hh] = k
            c_s[hh] = c
            v_s[hh] = v
            st = state_s[hh]
            st_ref[hh, n] = st.astype(BF16)
            o_s[hh] = _dot_nt(q * jnp.exp(c), st)
            last = c_s[hh, CH - 1:CH, :]
            state_s[hh] = st * jnp.exp(last) + _dot_tn(v, k * jnp.exp(last - c))

        def within_chunk_matmul(hh):
            q, k, c, v = q_s[hh], k_s[hh], c_s[hh], v_s[hh]
            a = jnp.zeros((CH, CH), F32)
            for i in range(CH // SB):
                r_i = _block_ref(c_s.at[hh], i)
                qi = q * jnp.exp(jnp.minimum(c - r_i, 0.0))
                kei = k * jnp.exp(jnp.minimum(r_i - c, DECAY_CAP))
                m_i = (rowi >= SB * i) & (rowi < SB * (i + 1)) & (coli <= rowi)
                a = a + jnp.where(m_i, _dot_nt(qi, kei), 0.0)
            o_s[hh] += _dot_nn(a, v)

        def within_chunk_exact(hh):
            q, k, c, v = q_s[hh], k_s[hh], c_s[hh], v_s[hh]
            a_off = jnp.zeros((CH, CH), F32)
            for i in range(1, CH // SB):
                r_i = _block_ref(c_s.at[hh], i)
                qi = q * jnp.exp(jnp.minimum(c - r_i, 0.0))
                kei = k * jnp.exp(jnp.minimum(r_i - c, 0.0))
                m_i = (rowi >= SB * i) & (rowi < SB * (i + 1)) & (coli < SB * i)
                a_off = a_off + jnp.where(m_i, _dot_nt(qi, kei), 0.0)
            o_s[hh] += _dot_nn(a_off, v)
            for i in range(CH // SB):
                blk = slice(SB * i, SB * (i + 1))
                qb = q_s[hh, blk, :]
                cb = c_s[hh, blk, :]
                acc = jnp.zeros((SB, HD), F32)
                for s in range(SB):
                    row = SB * i + s
                    w = jnp.exp(jnp.minimum(cb - c_s[hh, row:row + 1, :], 0.0))
                    a_col = jnp.sum(qb * k_s[hh, row:row + 1, :] * w, axis=-1, keepdims=True)
                    acc = acc + jnp.where(sbi >= s, a_col, 0.0) * v_s[hh, row:row + 1, :]
                o_s[hh, blk, :] += acc

        def norm_and_gate(hh, rows):
            lanes = slice(hh * HD, (hh + 1) * HD)
            ov = o_s[hh]
            oraw_ref[rows, lanes] = ov
            r = lax.rsqrt(jnp.mean(ov * ov, axis=-1, keepdims=True) + EPS)
            hg = hg_ref[rows, lanes].astype(F32)
            bin_ref[rows, lanes] = ((ov * r) * gnv * (hg * _sig(hg))).astype(BF16)

        def chunk_with(within_chunk):
            def chunk(n, carry):
                rows = pl.ds(pl.multiple_of(n * CH, CH), CH)
                for hh in range(NH):
                    between_chunks(hh, n, rows)
                for hh in range(NH):
                    within_chunk(hh)
                for hh in range(NH):
                    norm_and_gate(hh, rows)
                return carry
            return chunk

        worst = lax.fori_loop(0, nch, gates_pass, jnp.zeros((1, HD), F32))
        mild = jnp.max(worst) <= DECAY_CAP
        mild_ref[...] = jnp.broadcast_to(jnp.where(mild, 1.0, 0.0), (8, HD))

        @pl.when(mild)
        def _():
            lax.fori_loop(0, nch, chunk_with(within_chunk_matmul), 0, unroll=4)

        @pl.when(jnp.logical_not(mild))
        def _():
            lax.fori_loop(0, nch, chunk_with(within_chunk_exact), 0)

        bint_ref[...] = bin_ref[...].astype(F32).T.astype(BF16)

    col = lambda off: pl.BlockSpec((S, W), lambda h: (0, off // NH + h))
    head = pl.BlockSpec((S, W), lambda h: (0, h))
    outs = _call(
        body, name=name, grid=(HEADS // NH,),
        in_specs=[col(HQ0), col(HF0), col(HI0), col(HG0), pl.BlockSpec((1, W), lambda h: (0, h)),
                  pl.BlockSpec((1, HD), lambda h: (0, 0))],
        out_specs=[head, pl.BlockSpec((W, S), lambda h: (h, 0)), head,
                   pl.BlockSpec((NH, nch, HD, HD), lambda h: (h, 0, 0, 0)),
                   pl.BlockSpec((8, HD), lambda h: (h, 0)), head],
        out_shape=[jax.ShapeDtypeStruct((S, D), BF16), jax.ShapeDtypeStruct((D, S), BF16),
                   jax.ShapeDtypeStruct((S, D), F32), jax.ShapeDtypeStruct((HEADS, nch, HD, HD), BF16),
                   jax.ShapeDtypeStruct((8 * HEADS // NH, HD), F32), jax.ShapeDtypeStruct((S, D), F32)],
        scratch_shapes=[pltpu.VMEM((NH, CH, HD), F32)] * 5 + [pltpu.VMEM((NH, HD, HD), F32)]
        + [pltpu.VMEM((NH, S, HD), F32)] * 3,
        sem=("parallel",), args=(proj, proj, proj, proj, lb, gn), carry=carry)
    return outs[:6], outs[6:]


def _hgrn_bwd(dbin, proj, oraw, states, mild, cum, lb, gn, dproj, name, carry=None):
    S = proj.shape[0]
    nch = S // CH
    W = NH * HD
    n_in = 12

    def body(*refs):
        ins, (dproj_ref, dlb_ref, dgn_ref) = refs[:n_in - 1], refs[n_in:n_in + 3]
        scratch, later = refs[n_in + 3:-3], refs[-3:]
        seg = pl.program_id(1)

        @pl.when(seg == 0)
        def _():
            heads(*ins, dproj_ref, *later, dlb_ref, dgn_ref, *scratch)

        for s, kept in enumerate(later):
            @pl.when(seg == s + 1)
            def _(kept=kept):
                dproj_ref[...] = kept[...]

    def heads(db_ref, hq_ref, hf_ref, hi_ref, hg_ref, or_ref, st_ref, mild_ref, cum_ref, lb_ref, gn_ref,
              dq_ref, df_ref, di_ref, dg_ref, dlb_ref, dgn_ref,
              q_s, k_s, c_s, v_s, do_s, dq_s, dk_s, dv_s, dc_s, dqd_s, dkd_s, f_s, sf_s, sq_s, dl_s, dst_s,
              dlb_s, dgn_s):
        dst_s[...] = jnp.zeros((NH, HD, HD), F32)
        dlb_s[...] = jnp.zeros((1, W), F32)
        dgn_s[...] = jnp.zeros((1, HD), F32)
        rowi = lax.broadcasted_iota(jnp.int32, (CH, 1), 0)
        rowi2 = lax.broadcasted_iota(jnp.int32, (CH, CH), 0)
        coli2 = lax.broadcasted_iota(jnp.int32, (CH, CH), 1)
        sbi = lax.broadcasted_iota(jnp.int32, (SB, 1), 0)
        gnv = gn_ref[...]
        def between_chunks(hh, n, rows):
            lanes = slice(hh * HD, (hh + 1) * HD)
            lbv = lb_ref[:, lanes]
            hq = hq_ref[rows, lanes].astype(F32)
            sq = _sig(hq)
            sf = _sig(hf_ref[rows, lanes].astype(F32))
            f = lbv + (1.0 - lbv) * sf
            q = hq * sq
            k = 1.0 - f
            f_s[hh] = f
            sf_s[hh] = sf
            sq_s[hh] = sq
            v = hi_ref[rows, lanes].astype(F32)
            c = cum_ref[rows, lanes]
            ov = or_ref[rows, lanes]
            hg = hg_ref[rows, lanes].astype(F32)
            sg = _sig(hg)
            r = lax.rsqrt(jnp.mean(ov * ov, axis=-1, keepdims=True) + EPS)
            dbv = db_ref[rows, lanes]
            d_on = dbv * (hg * sg)
            dg_ref[rows, lanes] = (dbv * ((ov * r) * gnv) * _dsilu(hg, sg)).astype(BF16)
            dgn_s[...] += jnp.sum(d_on * (ov * r), axis=0, keepdims=True)
            u = d_on * gnv
            do = r * u - ov * (r * r * r) * jnp.mean(u * ov, axis=-1, keepdims=True)
            q_s[hh] = q
            k_s[hh] = k
            c_s[hh] = c
            v_s[hh] = v
            do_s[hh] = do
            st = st_ref[hh, n].astype(F32)
            dst = dst_s[hh]
            ec = jnp.exp(c)
            last = c_s[hh, CH - 1:CH, :]
            el = jnp.exp(last - c)
            elast = jnp.exp(last)
            dq = _dot_nn(do, st) * ec
            dk = _dot_nn(v, dst) * el
            dq_s[hh] = dq
            dk_s[hh] = dk
            dv_s[hh] = _dot_nt(k * el, dst)
            dc_s[hh] = q * dq - k * dk
            dl_s[hh] = (jnp.sum(k * dk, axis=0, keepdims=True)
                        + elast * jnp.sum(st * dst, axis=0, keepdims=True))
            dst_s[hh] = dst * elast + _dot_tn(do, q * ec)

        def pairs_matmul(hh, first, cap, strict):
            q, k, c, v, do = q_s[hh], k_s[hh], c_s[hh], v_s[hh], do_s[hh]
            d_a = _dot_nt(do, v).astype(BF16).astype(F32)
            d_at = d_a.T
            at = jnp.zeros((CH, CH), F32)
            dq, dk, dcum = dq_s[hh], dk_s[hh], dc_s[hh]
            for i in range(first, CH // SB):
                r_i = _block_ref(c_s.at[hh], i)
                eq = jnp.exp(jnp.minimum(c - r_i, 0.0))
                ek = jnp.exp(jnp.minimum(r_i - c, cap))
                qi = (q * eq).astype(BF16).astype(F32)
                kei = (k * ek).astype(BF16).astype(F32)
                in_t = (rowi2 >= SB * i) & (rowi2 < SB * (i + 1))
                in_s = (coli2 >= SB * i) & (coli2 < SB * (i + 1))
                m_ts = in_t & ((coli2 < SB * i) if strict else (coli2 <= rowi2))
                m_st = in_s & ((rowi2 < SB * i) if strict else (rowi2 <= coli2))
                at = at + jnp.where(m_st, _dot_nt(kei, qi), 0.0)
                dq_i = _dot_nn(jnp.where(m_ts, d_a, 0.0), kei)
                dk_i = _dot_nn(jnp.where(m_st, d_at, 0.0), qi)
                dq = dq + dq_i * eq
                dk = dk + dk_i * ek
                dcum = dcum + (qi * dq_i - kei * dk_i)
            dq_s[hh] = dq
            dk_s[hh] = dk
            dc_s[hh] = dcum
            dv_s[hh] += _dot_nn(at, do)

        def pairs_exact(hh):
            dqd_s[hh] = jnp.zeros((CH, HD), F32)
            dkd_s[hh] = jnp.zeros((CH, HD), F32)
            for i in range(CH // SB):
                blk = slice(SB * i, SB * (i + 1))
                qb = q_s[hh, blk, :]
                cb = c_s[hh, blk, :]
                dob = do_s[hh, blk, :]
                dq_acc = jnp.zeros((SB, HD), F32)
                for s in range(SB):
                    row = SB * i + s
                    ks = k_s[hh, row:row + 1, :]
                    vs = v_s[hh, row:row + 1, :]
                    w = jnp.exp(jnp.minimum(cb - c_s[hh, row:row + 1, :], 0.0))
                    live = sbi >= s
                    a_col = jnp.where(live, jnp.sum(qb * ks * w, axis=-1, keepdims=True), 0.0)
                    da_col = jnp.where(live, jnp.sum(dob * vs, axis=-1, keepdims=True), 0.0)
                    dq_acc = dq_acc + da_col * ks * w
                    dkd_s[hh, row:row + 1, :] += jnp.sum(da_col * qb * w, axis=0, keepdims=True)
                    dv_s[hh, row:row + 1, :] += jnp.sum(a_col * dob, axis=0, keepdims=True)
                dqd_s[hh, blk, :] += dq_acc
            dq_d = dqd_s[hh]
            dk_d = dkd_s[hh]
            dq_s[hh] += dq_d
            dk_s[hh] += dk_d
            dc_s[hh] += q_s[hh] * dq_d - k_s[hh] * dk_d

        def gate_grads(hh, rows):
            lanes = slice(hh * HD, (hh + 1) * HD)
            lbv = lb_ref[:, lanes]
            hq = hq_ref[rows, lanes].astype(F32)
            f, sf, sq = f_s[hh], sf_s[hh], sq_s[hh]
            dlogf = _chunk_rev_cumsum(dc_s[hh], rowi) + dl_s[hh]
            dfv = jnp.where(f > 1e-30, dlogf / jnp.maximum(f, 1e-30), 0.0) - dk_s[hh]
            dlb_s[:, lanes] += jnp.sum(dfv * (1.0 - sf), axis=0, keepdims=True)
            df_ref[rows, lanes] = (dfv * (1.0 - lbv) * sf * (1.0 - sf)).astype(BF16)
            dq_ref[rows, lanes] = (dq_s[hh] * _dsilu(hq, sq)).astype(BF16)
            di_ref[rows, lanes] = dv_s[hh].astype(BF16)

        def chunk_with(pairs):
            def chunk(j, carry):
                n = nch - 1 - j
                rows = pl.ds(pl.multiple_of(n * CH, CH), CH)
                for hh in range(NH):
                    between_chunks(hh, n, rows)
                for hh in range(NH):
                    pairs(hh)
                for hh in range(NH):
                    gate_grads(hh, rows)
                return carry
            return chunk

        def pairs_mild(hh):
            pairs_matmul(hh, 0, DECAY_CAP, strict=False)

        def pairs_any(hh):
            pairs_matmul(hh, 1, 0.0, strict=True)
            pairs_exact(hh)

        mild = jnp.max(mild_ref[...]) > 0.5

        @pl.when(mild)
        def _():
            lax.fori_loop(0, nch, chunk_with(pairs_mild), 0, unroll=4)

        @pl.when(jnp.logical_not(mild))
        def _():
            lax.fori_loop(0, nch, chunk_with(pairs_any), 0)

        dlb_ref[...] = dlb_s[...]
        dgn_ref[...] = jnp.broadcast_to(dgn_s[...], (8, HD))

    col = lambda off: pl.BlockSpec((S, W), lambda h, s: (0, off // NH + h))
    head = pl.BlockSpec((S, W), lambda h, s: (0, h))
    vec = pl.BlockSpec((1, W), lambda h, s: (0, h))
    seg_w = (HF0 - HQ0) // NH
    outs = _call(
        body, name=name, grid=(HEADS // NH, 4),
        in_specs=[head, col(HQ0), col(HF0), col(HI0), col(HG0), head,
                  pl.BlockSpec((NH, nch, HD, HD), lambda h, s: (h, 0, 0, 0)),
                  pl.BlockSpec((8, HD), lambda h, s: (h, 0)), head, vec,
                  pl.BlockSpec((1, HD), lambda h, s: (0, 0)), ANY],
        out_specs=[pl.BlockSpec((S, W), lambda h, s: (0, HQ0 // NH + seg_w * s + h)), vec,
                   pl.BlockSpec((8, HD), lambda h, s: (h, 0))],
        out_shape=[jax.ShapeDtypeStruct(dproj.shape, dproj.dtype), jax.ShapeDtypeStruct((1, D), F32),
                   jax.ShapeDtypeStruct((8 * HEADS // NH, HD), F32)],
        scratch_shapes=[pltpu.VMEM((NH, CH, HD), F32)] * 14
        + [pltpu.VMEM((NH, 1, HD), F32), pltpu.VMEM((NH, HD, HD), F32), pltpu.VMEM((1, W), F32),
           pltpu.VMEM((1, HD), F32)] + [pltpu.VMEM((S, W), BF16)] * 3,
        sem=("arbitrary", "arbitrary"), aliases={n_in - 1: 0},
        args=(dbin, proj, proj, proj, proj, oraw, states, mild, cum, lb, gn, dproj), carry=carry)
    dproj, dlb, dgn = outs[:3]
    return (dproj, dlb, dgn.reshape(HEADS // NH, 8, HD)[:, 0, :]), outs[3:]


def _lower_bounds(l0, l1):
    m = jnp.maximum(l0, l1)
    e0 = jnp.exp(l0 - m)
    e1 = jnp.exp(l1 - m)
    tot = e0 + e1
    p0 = e0 / tot
    p1 = e1 / tot
    return jnp.clip(p0 - p0, 0.0, 1.0), jnp.clip((p0 + p1) - p0, 0.0, 1.0)


def _lb_fwd(logits):
    def body(l_ref, o_ref):
        lb0, lb1 = _lower_bounds(l_ref[0:1, :], l_ref[1:2, :])
        o_ref[0:1, :] = lb0
        o_ref[1:2, :] = lb1

    return pl.pallas_call(body, name="lb_fwd", out_shape=jax.ShapeDtypeStruct((2, D), F32))(logits)


def _lb_bwd(logits, dlb):
    def body(l_ref, d_ref, o_ref):
        _, vjp = jax.vjp(_lower_bounds, l_ref[0:1, :], l_ref[1:2, :])
        g0, g1 = vjp((d_ref[0:1, :], d_ref[1:2, :]))
        o_ref[0:1, :] = g0
        o_ref[1:2, :] = g1

    return pl.pallas_call(body, name="lb_bwd", out_shape=jax.ShapeDtypeStruct((2, D), F32))(logits, dlb)


ADA_PAD = 128


def _ada_fwd(c_pad, w_ada, b_sh):
    ns = w_ada.shape[2]

    def body(c_ref, w_ref, b_ref, o_ref):
        cv = c_ref[...]
        ca = (cv * _sig(cv)).astype(BF16)
        for l in range(2):
            res = jnp.dot(ca, w_ref[l].astype(BF16), preferred_element_type=F32)
            o_ref[:, l * ns:(l + 1) * ns] = res[0:NDEV, :] + b_ref[l:l + 1, :]

    return pl.pallas_call(body, name="ada_fwd", out_shape=jax.ShapeDtypeStruct((NDEV, 2 * ns), F32),
                          compiler_params=_cp())(c_pad, w_ada, b_sh)


def _ada_wgrad(c_pad_t, d_ada_sh):
    ns = d_ada_sh.shape[2]

    def body(c_ref, d_ref, o_ref):
        cv = c_ref[...]
        ca = (cv * _sig(cv)).astype(BF16)
        for l in range(2):
            o_ref[l] = jnp.dot(ca, d_ref[l].astype(BF16), preferred_element_type=F32)

    return pl.pallas_call(body, name="ada_wgrad", out_shape=jax.ShapeDtypeStruct((2, D, ns), F32),
                          compiler_params=_cp())(c_pad_t, d_ada_sh)


def _sum_devices(g):
    _, R, C = g.shape

    def body(g_ref, o_ref):
        acc = g_ref[0]
        for d in range(1, NDEV):
            acc = acc + g_ref[d]
        o_ref[...] = acc

    return pl.pallas_call(body, name="sum_devices", out_shape=jax.ShapeDtypeStruct((R, C), F32),
                          compiler_params=_cp())(g)


def _adamw(w, g, m, v, name, carry=None):
    R, C = w.shape
    tr = _row_tile(R, max(8, (1 << 19) // C))

    def body(w_ref, g_ref, m_ref, v_ref, d_ref, nm_ref, nv_ref):
        d_ref[...], nm_ref[...], nv_ref[...] = _adamw_update(w_ref[...], g_ref[...], m_ref[...], v_ref[...])

    tile = pl.BlockSpec((tr, C), lambda i: (i, 0))
    return _call(body, name=name, grid=(R // tr,), in_specs=[tile] * 4, out_specs=[tile] * 3,
                 out_shape=[jax.ShapeDtypeStruct((R, C), F32)] * 3, sem=("parallel",), args=(w, g, m, v),
                 carry=carry)


def _adamw_update(w, g, m, v):
    nm = B1 * m + (1.0 - B1) * g
    nv = B2 * v + (1.0 - B2) * (g * g)
    m_hat = nm / (1.0 - B1 ** STEP)
    v_hat = nv / (1.0 - B2 ** STEP)
    return -LR * (m_hat / (jnp.sqrt(v_hat) + AEPS) + WD * w), nm, nv


SMALL_PARTS = (("b_ada", 0, 6, D), ("g_pre", 8, 2, D), ("g_post", 16, 2, D), ("lb_logits", 24, 2, D),
               ("pool_w", 32, 128, D), ("pool_scale", 160, 1, D), ("hgrn_norm_g", 168, 1, 2 * HD))


def _adamw_small(g_small, g_lb_logits, wmv):
    n = len(SMALL_PARTS)

    def body(g_ref, glb_ref, *refs):
        ins, outs = refs[:3 * n], refs[3 * n:]
        for p, (key, row0, rows, width) in enumerate(SMALL_PARTS):
            gv = glb_ref[...] if key == "lb_logits" else g_ref[row0:row0 + rows, 0:width]
            res = _adamw_update(ins[3 * p][...], gv, ins[3 * p + 1][...], ins[3 * p + 2][...])
            for t in range(3):
                outs[3 * p + t][...] = res[t]

    flat = [t for triple in wmv for t in triple]
    outs = pl.pallas_call(body, name="adamw_small",
                          out_shape=[jax.ShapeDtypeStruct(t.shape, F32) for t in flat],
                          compiler_params=_cp())(g_small, g_lb_logits, *flat)
    return [outs[3 * p:3 * p + 3] for p in range(n)]


def _cast_to_slot(place, w, l, name):
    _, R, C = w.shape
    tr = _row_tile(R, max(8, (1 << 19) // C))

    def body(p_ref, w_ref, o_ref):
        o_ref[...] = w_ref[...].astype(BF16)

    return pl.pallas_call(
        body, name=name, out_shape=jax.ShapeDtypeStruct((NCHIP, R, C), BF16),
        grid_spec=pltpu.PrefetchScalarGridSpec(
            num_scalar_prefetch=1, grid=(R // tr,),
            in_specs=[pl.BlockSpec((None, tr, C), lambda i, p_ref: (l, i, 0))],
            out_specs=pl.BlockSpec((None, tr, C), lambda i, p_ref: (p_ref[0], i, 0))),
        compiler_params=_cp(("parallel",)),
    )(place, w)


def _pair_add(core, g, got, name):
    _, R, C = g.shape
    r2 = R // 2
    tr = _row_tile(r2, max(8, (1 << 19) // C))
    nt = r2 // tr

    def body(c_ref, a_ref, b_ref, o_ref):
        o_ref[...] = (a_ref[...].astype(F32) + b_ref[...].astype(F32)).astype(o_ref.dtype)

    return pl.pallas_call(
        body, name=name, out_shape=jax.ShapeDtypeStruct((NCHIP, r2, C), BF16),
        grid_spec=pltpu.PrefetchScalarGridSpec(
            num_scalar_prefetch=1, grid=(NCHIP, nt),
            in_specs=[pl.BlockSpec((None, tr, C), lambda j, i, c_ref: (j, c_ref[0] * nt + i, 0)),
                      pl.BlockSpec((None, tr, C), lambda j, i, c_ref: (j, i, 0))],
            out_specs=pl.BlockSpec((None, tr, C), lambda j, i, c_ref: (j, i, 0))),
        compiler_params=_cp(("parallel", "parallel")),
    )(core, g, got)


def _chip_sum(place, part, recv, layer, both, name):
    _, r2, C = part.shape
    tr = _row_tile(r2, max(8, (1 << 18) // C))
    nt = r2 // tr

    def body(p_ref, own_ref, r_ref, *rest):
        o_ref = rest[-1]
        me = p_ref[0]
        own = own_ref[...].astype(F32)
        acc = None
        for j in range(NCHIP):
            slot = jnp.minimum(jnp.where(j > me, j - 1, j), NCHIP - 2)
            term = jnp.where(me == j, own, r_ref[slot].astype(F32))
            acc = term if acc is None else acc + term
        o_ref[...] = acc

    args = (place, part, recv) if both is None else (place, part, recv, both)
    return pl.pallas_call(
        body, name=name, out_shape=jax.ShapeDtypeStruct((2, 2 * r2, C), F32),
        grid_spec=pltpu.PrefetchScalarGridSpec(
            num_scalar_prefetch=1, grid=(nt,),
            in_specs=[pl.BlockSpec((None, tr, C), lambda i, p_ref: (p_ref[0], i, 0)),
                      pl.BlockSpec((NCHIP - 1, tr, C), lambda i, p_ref: (0, i, 0))] + [ANY] * (len(args) - 3),
            out_specs=pl.BlockSpec((None, tr, C), lambda i, p_ref: (layer, p_ref[1] * nt + i, 0))),
        input_output_aliases={} if both is None else {3: 0},
        compiler_params=_cp(("parallel",)),
    )(*args)


def _place():
    x, y, c = lax.axis_index("x"), lax.axis_index("y"), lax.axis_index("c")
    chips = [(1 - x, y), (x, 1 - y), (1 - x, 1 - y)]
    return x, y, c, chips


def _gather_small(blk, name):
    m_per, n = blk.shape

    def body(x_ref, out_ref, send_sems, recv_sems, local_sem):
        x, y, c, chips = _place()
        me, sibling = (x, y, c), (x, y, 1 - c)

        def rows(px, py, pc):
            return out_ref.at[pl.ds((4 * px + 2 * py + pc) * m_per, m_per), :]

        def copy(k, block, to, src=None):
            return pltpu.make_async_remote_copy(
                src_ref=rows(*block) if src is None else src, dst_ref=rows(*block),
                send_sem=send_sems.at[k], recv_sem=recv_sems.at[k], device_id=to, device_id_type=MESH)

        mine = pltpu.make_async_copy(x_ref, rows(*me), local_sem)
        mine.start()
        first = [copy(0, me, sibling, src=x_ref)]
        first += [copy(1 + j, me, (*chip, c), src=x_ref) for j, chip in enumerate(chips)]
        for cp in first:
            cp.start()
        passed = [copy(4 + j, (*chip, c), sibling) for j, chip in enumerate(chips)]
        for j, chip in enumerate(chips):
            copy(1 + j, (*chip, c), me).wait_recv()
            passed[j].start()
        copy(0, sibling, me).wait_recv()
        for j, chip in enumerate(chips):
            copy(4 + j, (*chip, 1 - c), me).wait_recv()
        for cp in first + passed:
            cp.wait_send()
        mine.wait()

    return pl.pallas_call(
        body, name=name, out_shape=jax.ShapeDtypeStruct((NDEV * m_per, n), blk.dtype),
        in_specs=[pl.BlockSpec(memory_space=pltpu.VMEM)], out_specs=pl.BlockSpec(memory_space=pltpu.VMEM),
        scratch_shapes=[pltpu.SemaphoreType.DMA((7,)), pltpu.SemaphoreType.DMA((7,)), pltpu.SemaphoreType.DMA],
        compiler_params=_cp(),
    )(blk)


def _gather_rows_carry(blk):
    m_per, n = blk.shape

    def rows(ref, px, py, pc):
        return ref.at[pl.ds((4 * px + 2 * py + pc) * m_per, m_per), :]

    def copy(ins, outs, send_sems, recv_sems, k, block, to, own=False):
        return pltpu.make_async_remote_copy(
            src_ref=ins[0] if own else rows(outs[0], *block), dst_ref=rows(outs[0], *block),
            send_sem=send_sems.at[k], recv_sem=recv_sems.at[k], device_id=to, device_id_type=MESH)

    def mine(ins, outs, send_sems):
        x, y, c, _ = _place()
        return pltpu.make_async_copy(ins[0], rows(outs[0], x, y, c), send_sems.at[7])

    def start(ins, outs, send_sems, recv_sems):
        x, y, c, chips = _place()
        mine(ins, outs, send_sems).start()
        copy(ins, outs, send_sems, recv_sems, 0, (x, y, c), (x, y, 1 - c), own=True).start()
        for j, chip in enumerate(chips):
            copy(ins, outs, send_sems, recv_sems, 1 + j, (x, y, c), (*chip, c), own=True).start()

    def finish(ins, outs, send_sems, recv_sems):
        x, y, c, chips = _place()
        for j, chip in enumerate(chips):
            copy(ins, outs, send_sems, recv_sems, 1 + j, (*chip, c), (x, y, c)).wait_recv()
            copy(ins, outs, send_sems, recv_sems, 4 + j, (*chip, c), (x, y, 1 - c)).start()
        copy(ins, outs, send_sems, recv_sems, 0, (x, y, 1 - c), (x, y, c)).wait_recv()
        for j, chip in enumerate(chips):
            copy(ins, outs, send_sems, recv_sems, 4 + j, (*chip, 1 - c), (x, y, c)).wait_recv()
        copy(ins, outs, send_sems, recv_sems, 0, (x, y, c), (x, y, 1 - c), own=True).wait_send()
        for j, chip in enumerate(chips):
            copy(ins, outs, send_sems, recv_sems, 1 + j, (x, y, c), (*chip, c), own=True).wait_send()
            copy(ins, outs, send_sems, recv_sems, 4 + j, (*chip, c), (x, y, 1 - c)).wait_send()
        mine(ins, outs, send_sems).wait()

    return _Carry([blk], [jax.ShapeDtypeStruct((NDEV * m_per, n), blk.dtype)], {}, 8, start, finish)


def _gather_carry(shards, piece=(0, 1, 1)):
    n = len(shards)
    first, count, of = piece

    def rows(ref, half):
        r2 = ref.shape[1] // 2
        return pl.ds(half * r2 + first * (r2 // of), count * (r2 // of))

    def over_ici(outs, send_sems, recv_sems, a, j, chip_xy, slot):
        x, y, c, _ = _place()
        blk = outs[a].at[slot, rows(outs[a], c), :]
        return pltpu.make_async_remote_copy(
            src_ref=blk, dst_ref=blk, send_sem=send_sems.at[6 * a + j], recv_sem=recv_sems.at[6 * a + j],
            device_id=(*chip_xy, c), device_id_type=MESH)

    def over_d2d(outs, send_sems, recv_sems, a, j, slot, half):
        x, y, c, _ = _place()
        blk = outs[a].at[slot, rows(outs[a], half), :]
        return pltpu.make_async_remote_copy(
            src_ref=blk, dst_ref=blk, send_sem=send_sems.at[6 * a + 3 + j], recv_sem=recv_sems.at[6 * a + 3 + j],
            device_id=(x, y, 1 - c), device_id_type=MESH)

    def start(ins, outs, send_sems, recv_sems):
        x, y, c, chips = _place()
        for a in range(n):
            for j, chip_xy in enumerate(chips):
                over_ici(outs, send_sems, recv_sems, a, j, chip_xy, 2 * x + y).start()

    def finish(ins, outs, send_sems, recv_sems):
        x, y, c, chips = _place()
        for a in range(n):
            for j, (cx, cy) in enumerate(chips):
                over_ici(outs, send_sems, recv_sems, a, j, (cx, cy), 2 * cx + cy).wait_recv()
                over_d2d(outs, send_sems, recv_sems, a, j, 2 * cx + cy, c).start()
        for a in range(n):
            for j, (cx, cy) in enumerate(chips):
                over_d2d(outs, send_sems, recv_sems, a, j, 2 * cx + cy, 1 - c).wait_recv()
        for a in range(n):
            for j, (cx, cy) in enumerate(chips):
                over_ici(outs, send_sems, recv_sems, a, j, (cx, cy), 2 * x + y).wait_send()
                over_d2d(outs, send_sems, recv_sems, a, j, 2 * cx + cy, c).wait_send()

    return _Carry(shards, [jax.ShapeDtypeStruct(s.shape, s.dtype) for s in shards],
                  {a: a for a in range(n)}, 6 * n, start, finish)


def _rs_pair(grads, name):
    n = len(grads)

    def body(*refs):
        ins, gots = refs[:n], refs[n:2 * n]
        send_sems, recv_sems = refs[2 * n:]
        x, y, c, _ = _place()
        cps = []
        for a in range(n):
            r2 = ins[a].shape[1] // 2
            cp = pltpu.make_async_remote_copy(
                src_ref=ins[a].at[:, pl.ds((1 - c) * r2, r2), :], dst_ref=gots[a],
                send_sem=send_sems.at[a], recv_sem=recv_sems.at[a],
                device_id=(x, y, 1 - c), device_id_type=MESH)
            cp.start()
            cps.append(cp)
        for cp in cps:
            cp.wait()

    half = [jax.ShapeDtypeStruct((NCHIP, g.shape[1] // 2, g.shape[2]), g.dtype) for g in grads]
    return pl.pallas_call(
        body, name=name, out_shape=half, in_specs=[ANY] * n, out_specs=[ANY] * n,
        scratch_shapes=[pltpu.SemaphoreType.DMA((n,)), pltpu.SemaphoreType.DMA((n,))],
        compiler_params=_cp(),
    )(*grads)


def _chips_carry(parts, piece=(0, 1, 1), into=None):
    n = len(parts)
    first, count, of = piece

    def rows(ref):
        step = ref.shape[1] // of
        return pl.ds(first * step, count * step)

    def send(ins, outs, send_sems, recv_sems, a, j, chip_xy):
        x, y, c, _ = _place()
        me, them = 2 * x + y, 2 * chip_xy[0] + chip_xy[1]
        return pltpu.make_async_remote_copy(
            src_ref=ins[a].at[them, rows(ins[a]), :],
            dst_ref=outs[a].at[me - (me > them).astype(jnp.int32), rows(outs[a]), :],
            send_sem=send_sems.at[3 * a + j], recv_sem=recv_sems.at[3 * a + j],
            device_id=(*chip_xy, c), device_id_type=MESH)

    def start(ins, outs, send_sems, recv_sems):
        _, _, _, chips = _place()
        for a in range(n):
            for j, chip_xy in enumerate(chips):
                send(ins, outs, send_sems, recv_sems, a, j, chip_xy).start()

    def finish(ins, outs, send_sems, recv_sems):
        x, y, c, chips = _place()
        me = 2 * x + y
        for a in range(n):
            for j, (cx, cy) in enumerate(chips):
                them = 2 * cx + cy
                blk = outs[a].at[them - (them > me).astype(jnp.int32), rows(outs[a]), :]
                pltpu.make_async_remote_copy(
                    src_ref=blk, dst_ref=blk, send_sem=send_sems.at[3 * a + j], recv_sem=recv_sems.at[3 * a + j],
                    device_id=(cx, cy, c), device_id_type=MESH).wait_recv()
        for a in range(n):
            for j, chip_xy in enumerate(chips):
                send(ins, outs, send_sems, recv_sems, a, j, chip_xy).wait_send()

    landing = [jax.ShapeDtypeStruct((NCHIP - 1,) + p.shape[1:], p.dtype) for p in parts]
    if into is None:
        return _Carry(parts, landing, {}, 3 * n, start, finish)
    return _Carry(list(parts) + list(into), landing, {n + a: a for a in range(n)}, 3 * n, start, finish)


def _rs_swap(fulls):
    n = len(fulls)

    def body(*refs):
        outs = refs[n:2 * n]
        send_sems, recv_sems = refs[2 * n:]
        x, y, c, _ = _place()
        cps = []
        for a in range(n):
            r2 = outs[a].shape[1] // 2
            mine = outs[a].at[:, pl.ds(c * r2, r2), :]
            cp = pltpu.make_async_remote_copy(
                src_ref=mine, dst_ref=mine, send_sem=send_sems.at[a], recv_sem=recv_sems.at[a],
                device_id=(x, y, 1 - c), device_id_type=MESH)
            cp.start()
            cps.append(cp)
        for a in range(n):
            r2 = outs[a].shape[1] // 2
            blk = outs[a].at[:, pl.ds((1 - c) * r2, r2), :]
            pltpu.make_async_remote_copy(
                src_ref=blk, dst_ref=blk, send_sem=send_sems.at[a], recv_sem=recv_sems.at[a],
                device_id=(x, y, 1 - c), device_id_type=MESH).wait_recv()
        for cp in cps:
            cp.wait_send()

    return pl.pallas_call(
        body, name="rs_swap", out_shape=[jax.ShapeDtypeStruct(f.shape, f.dtype) for f in fulls],
        in_specs=[ANY] * n, out_specs=[ANY] * n, input_output_aliases={a: a for a in range(n)},
        scratch_shapes=[pltpu.SemaphoreType.DMA((n,)), pltpu.SemaphoreType.DMA((n,))],
        compiler_params=_cp(),
    )(*fulls)


def _tail_weight_grads(merged_t, b_in_t, a_in_t, dy, dbr_b, dbr_a, name, tn=256):
    S = dy.shape[0]
    nn = D // tn

    def body(mt_ref, bt_ref, at_ref, dy_ref, db_ref, da_ref, go_ref, gh_ref, gp_ref):
        go_ref[...] = jnp.dot(mt_ref[...], dy_ref[...], preferred_element_type=F32).astype(BF16)
        gh_ref[...] = jnp.dot(bt_ref[...], db_ref[...], preferred_element_type=F32).astype(BF16)
        gp_ref[...] = jnp.dot(at_ref[...], da_ref[...], preferred_element_type=F32).astype(BF16)

    left = lambda rows: pl.BlockSpec((rows, S), lambda n: (0, 0))
    right = pl.BlockSpec((S, tn), lambda n: (0, n))
    out = pl.BlockSpec((D, tn), lambda n: (0, n))
    return pl.pallas_call(
        body, name=name, grid=(nn,), in_specs=[left(D), left(D), left(POOL_W), right, right, right],
        out_specs=[out, out, pl.BlockSpec((None, POOL_W, tn), lambda n: (n, 0, 0))],
        out_shape=[jax.ShapeDtypeStruct((D, D), BF16), jax.ShapeDtypeStruct((D, D), BF16),
                   jax.ShapeDtypeStruct((NCHIP, POOL_W, D // NCHIP), BF16)],
        compiler_params=_cp(("parallel",)),
    )(merged_t, b_in_t, a_in_t, dy, dbr_b, dbr_a)


class _GatherInProj:
    def __init__(self, slot, order):
        self.slot, self.order = slot, order


def _proj_with_gather(h, w_slot, order, name, tn=256):
    S, K = h.shape
    nsh, _, ns = w_slot.shape
    tps = ns // tn
    nt = nsh * tps
    r2 = K // 2

    def body(ord_ref, h_ref, w_in_ref, o_ref, w_ref, wbuf, tile_sems, send_sems, recv_sems):
        n = pl.program_id(0)
        x, y, c, chips = _place()

        def half(slot, which):
            return w_ref.at[slot, pl.ds(which * r2, r2), :]

        def over_ici(j, slot):
            blk = half(slot, c)
            return pltpu.make_async_remote_copy(src_ref=blk, dst_ref=blk, send_sem=send_sems.at[j],
                                                recv_sem=recv_sems.at[j], device_id=(*chips[j], c),
                                                device_id_type=MESH)

        def over_d2d(j, which):
            blk = half(2 * chips[j][0] + chips[j][1], which)
            return pltpu.make_async_remote_copy(src_ref=blk, dst_ref=blk, send_sem=send_sems.at[3 + j],
                                                recv_sem=recv_sems.at[3 + j], device_id=(x, y, 1 - c),
                                                device_id_type=MESH)

        def tile_copy(step, slot):
            shard = ord_ref[step // tps]
            return pltpu.make_async_copy(w_ref.at[shard, :, pl.ds((step % tps) * tn, tn)], wbuf.at[slot],
                                         tile_sems.at[slot])

        @pl.when(n == 0)
        def _():
            for j in range(3):
                over_ici(j, 2 * x + y).start()
            tile_copy(0, 0).start()

        for j in range(3):
            @pl.when(n == (j + 1) * tps - 1)
            def _(j=j):
                over_ici(j, 2 * chips[j][0] + chips[j][1]).wait_recv()
                over_d2d(j, c).start()
                over_d2d(j, 1 - c).wait_recv()

        @pl.when(n + 1 < nt)
        def _():
            tile_copy(n + 1, (n + 1) % 2).start()

        tile_copy(n, n % 2).wait()
        o_ref[...] = jnp.dot(h_ref[...], wbuf[n % 2], preferred_element_type=F32).astype(o_ref.dtype)

        @pl.when(n == nt - 1)
        def _():
            for j in range(3):
                over_ici(j, 2 * x + y).wait_send()
                over_d2d(j, c).wait_send()

    return pl.pallas_call(
        body, name=name,
        out_shape=[jax.ShapeDtypeStruct((S, nsh * ns), BF16), jax.ShapeDtypeStruct(w_slot.shape, w_slot.dtype)],
        grid_spec=pltpu.PrefetchScalarGridSpec(
            num_scalar_prefetch=1, grid=(nt,),
            in_specs=[pl.BlockSpec((S, K), lambda n, o_ref: (0, 0)), ANY],
            out_specs=[pl.BlockSpec((S, tn), lambda n, o_ref: (0, o_ref[n // tps] * tps + n % tps)), ANY],
            scratch_shapes=[pltpu.VMEM((2, K, tn), w_slot.dtype), pltpu.SemaphoreType.DMA((2,)),
                            pltpu.SemaphoreType.DMA((6,)), pltpu.SemaphoreType.DMA((6,))]),
        input_output_aliases={2: 1},
        compiler_params=_cp(("arbitrary",)),
    )(order, h, w_slot)


def _mm_ride(a, b, carry, **kw):
    if carry is None:
        return _mm(a, b, **kw), []
    return _mm(a, b, carry=carry, **kw)


def _layer_fwd(l, x, ada, w, small, ride, target=None):
    shift, scale, gate = ada[:, 0:D], ada[:, D:2 * D], ada[:, 2 * D:3 * D]
    carry, landed = ride("prenorm")
    (h, h_t), outs = _prenorm_fwd(x, small["g_pre"][l], scale, shift, f"prenorm_fwd{l}", carry)
    landed(outs)
    carry, landed = ride("proj")
    if isinstance(carry, _GatherInProj):
        proj, full = _proj_with_gather(h, carry.slot, carry.order, f"proj{l}")
        outs = [full]
    else:
        proj, outs = _mm_ride(h, w["w_in"][l], carry, name=f"proj{l}", b_mode="nn_sh", tm=2048, out_dtype=BF16)
    landed(outs)
    a_in, a_in_t = _pool_fwd(proj, small["pool_w"][l], small["pool_scale"][l], f"pool_fwd{l}")
    carry, landed = ride("hgrn")
    (b_in, b_in_t, o_raw, states, mild, cum), outs = _hgrn_fwd(proj, small["lb"][l], small["hgrn_norm_g"][l],
                                                              f"hgrn_fwd{l}", carry=carry)
    landed(outs)
    carry, landed = ride("tail")
    (br_a, br_b, merged_t, y, *x_new), outs = _layer_tail_fwd(
        proj, a_in, b_in, x, w["w_pool_o"][l], w["w_hgrn_o"][l].reshape(D, D), w["w_out"][l].reshape(D, D),
        gate, small["g_post"][l], f"tail_fwd{l}", target=target, carry=carry)
    landed(outs)
    saved = dict(x=x, h_t=h_t, proj=proj, a_in_t=a_in_t, b_in_t=b_in_t, o_raw=o_raw, states=states, mild=mild,
                 cum=cum,
                 br_a=br_a, br_b=br_b, merged_t=merged_t, y=y, scale=scale, gate=gate)
    return x_new, saved


def _layer_bwd(l, dxn, sv, w, small, ride):
    dy, dbr_a, dbr_b, dproj, da_in, db_in, dgate, dg_post = _layer_head_bwd(
        dxn, sv["y"], sv["proj"], sv["br_a"], sv["br_b"], w["w_pool_o"][l], w["w_hgrn_o"][l].reshape(D, D),
        w["w_out"][l].reshape(D, D), sv["gate"], small["g_post"][l], f"head_bwd{l}")
    gw_out, gw_hgrn_o, gw_pool_o = _tail_weight_grads(sv["merged_t"], sv["b_in_t"], sv["a_in_t"], dy, dbr_b,
                                                      dbr_a, f"gw_tail{l}")
    big = dict(w_pool_o=gw_pool_o, w_hgrn_o=gw_hgrn_o.reshape(NCHIP, D // NCHIP, D),
               w_out=gw_out.reshape(NCHIP, D // NCHIP, D))
    carry, landed = ride["hgrn"](big)
    (dproj, dlb, dgn), outs = _hgrn_bwd(db_in, sv["proj"], sv["o_raw"], sv["states"], sv["mild"], sv["cum"],
                                        small["lb"][l], small["hgrn_norm_g"][l], dproj, f"hgrn_bwd{l}",
                                        carry=carry)
    landed(outs)
    dproj, dpw, dpsc = _pool_bwd(da_in, sv["proj"], small["pool_w"][l], small["pool_scale"][l], dproj,
                                 f"pool_bwd{l}")
    little = dict(dgate=dgate, g_post=dg_post, pool_w=dpw, pool_scale=dpsc, lb=dlb,
                  hgrn_norm_g=jnp.sum(dgn, axis=0, keepdims=True))
    carry, landed = ride["gw_in"](little)
    big["w_in"], outs = _mm_ride(sv["h_t"], dproj, carry, name=f"gw_in{l}", out_shards=NCHIP, out_dtype=BF16)
    landed(outs)
    carry, landed = ride["d_h"](big)
    dh, outs = _mm_ride(dproj, w["w_in"][l], carry, name=f"d_h{l}", b_mode="nt_shk", tn=1024)
    landed(outs)
    carry, landed = ride["prenorm"](big)
    (dx, dshift, dscale, dg_pre), outs = _prenorm_bwd(dh, dxn, sv["x"], small["g_pre"][l], sv["scale"],
                                                      f"prenorm_bwd{l}", carry)
    landed(outs)
    little.update(dshift=dshift, dscale=dscale, g_pre=dg_pre)
    return dx, big, little


SMALL_ROWS = 176


def _rows8(t):
    t = t.reshape(-1, D)
    return jnp.pad(t, ((0, -t.shape[0] % 8), (0, 0)))


def _pack_small_weights(b_ada, g_pre, g_post, lb_logits, pool_w, pool_scale, hgrn_norm_g):
    gn = jnp.pad(hgrn_norm_g.reshape(1, 2 * HD), ((0, 0), (0, D - 2 * HD)))
    return jnp.concatenate([_rows8(b_ada), _rows8(g_pre), _rows8(g_post), _rows8(lb_logits), _rows8(pool_w),
                            _rows8(pool_scale), _rows8(gn)], axis=0)


def _pack_small(parts):
    row_keys = ("dshift", "dscale", "dgate", "g_pre", "g_post", "lb", "pool_scale", "hgrn_norm_g")
    flat = [p[k] for p in parts for k in row_keys] + [p["pool_w"].reshape(GROUPS * 128 * 128 // D, D) for p in parts]
    nk = len(row_keys)

    def body(*refs):
        o_ref = refs[-1]
        o_ref[...] = jnp.zeros((SMALL_ROWS, D), F32)
        for l in range(2):
            dshift, dscale, dgate, g_pre, g_post, lb, pscale, gn = refs[l * nk:(l + 1) * nk]
            for r, ref in enumerate((dshift, dscale, dgate)):
                o_ref[3 * l + r:3 * l + r + 1, :] = ref[...]
            o_ref[8 + l:9 + l, :] = g_pre[...]
            o_ref[16 + l:17 + l, :] = g_post[...]
            o_ref[24 + l:25 + l, :] = lb[...]
            o_ref[160:161, l * POOL_W:(l + 1) * POOL_W] = pscale[...]
            o_ref[168:169, l * HD:(l + 1) * HD] = gn[...]
            pw = refs[2 * nk + l]
            rows = pw.shape[0]
            o_ref[32 + l * rows:32 + (l + 1) * rows, :] = pw[...]

    return pl.pallas_call(body, name="pack_small", out_shape=jax.ShapeDtypeStruct((SMALL_ROWS, D), F32),
                          compiler_params=_cp())(*flat)


def _unpack_small(p):
    return (p[0:6].reshape(2, 3 * D), p[8:10], p[16:18], p[24:26], p[32:160].reshape(2, GROUPS, 128, 128),
            p[160:161].reshape(2, POOL_W), p[168:169, 0:2 * HD].reshape(2, HD))


def kernel(x, c, w_ada, b_ada, g_pre, g_post, w_in, pool_w, pool_scale, lb_logits, hgrn_norm_g, w_pool_o, w_hgrn_o, w_out, loss_target, m_w_ada, m_b_ada, m_g_pre, m_g_post, m_w_in, m_pool_w, m_pool_scale, m_lb_logits, m_hgrn_norm_g, m_w_pool_o, m_w_hgrn_o, m_w_out, v_w_ada, v_b_ada, v_g_pre, v_g_post, v_w_in, v_pool_w, v_pool_scale, v_lb_logits, v_hgrn_norm_g, v_w_pool_o, v_w_hgrn_o, v_w_out):
    ax, ay, ac = lax.axis_index("x"), lax.axis_index("y"), lax.axis_index("c")
    chip = 2 * ax + ay
    dev = 2 * chip + ac
    xe, te = x[0], loss_target[0]
    ada_s = w_ada.shape[2]

    big_names = ("w_in", "w_pool_o", "w_hgrn_o", "w_out")
    big_w = (w_in, w_pool_o, w_hgrn_o, w_out)
    core = jnp.stack([ac]).astype(jnp.int32)
    place = jnp.stack([chip, ac]).astype(jnp.int32)
    slots = {(k, l): _cast_to_slot(place, t, l, f"cast_{k}{l}") for l in range(2) for k, t in zip(big_names, big_w)}
    w = {k: [None, None] for k in big_names}
    def fills(keys):
        def landed(outs):
            for (k, l), o in zip(keys, outs):
                w[k][l] = slots[k, l] = o
        return landed

    rest0 = [(k, 0) for k in big_names[1:]]
    rest1 = [(k, 1) for k in big_names[1:]]
    no_carry = (None, lambda outs: None)
    order = jnp.stack([chip, 2 * (1 - ax) + ay, 2 * ax + (1 - ay), 2 * (1 - ax) + (1 - ay)]).astype(jnp.int32)

    def ride_fwd0(stage):
        if stage == "proj":
            return _GatherInProj(slots["w_in", 0], order), fills([("w_in", 0)])
        if stage == "hgrn":
            return (_join_carries(_gather_carry([slots[t] for t in rest0]),
                                  _gather_carry([slots["w_in", 1]], piece=(0, 2, 4))),
                    fills(rest0 + [("w_in", 1)]))
        if stage == "tail":
            return _gather_carry([slots["w_in", 1]], piece=(2, 1, 4)), fills([("w_in", 1)])
        return no_carry

    def ride_fwd1(stage):
        if stage == "prenorm":
            return _gather_carry([slots["w_in", 1]], piece=(3, 1, 4)), fills([("w_in", 1)])
        if stage == "hgrn":
            return _gather_carry([slots[t] for t in rest1]), fills(rest1)
        return no_carry

    c_all = _gather_small(jnp.broadcast_to(c, (8, D)), "gather_c").reshape(NDEV, 8, D)[:, 0, :]
    c_pad = jnp.pad(c_all, ((0, ADA_PAD - NDEV), (0, 0)))
    b_sh = lax.dynamic_slice(b_ada, (0, chip * ada_s), (2, ada_s))
    ada_cols = _gather_small(_ada_fwd(c_pad, w_ada, b_sh), "gather_ada")
    ada_cols = ada_cols.reshape(NCHIP, 2, NDEV, 2, ada_s)[:, 0]
    ada_all = jnp.transpose(ada_cols, (2, 1, 0, 3)).reshape(2, NDEV, 3 * D)
    ada_me = lax.dynamic_slice(ada_all, (0, dev, 0), (2, 1, 3 * D))

    lbs = _lb_fwd(lb_logits)
    small = dict(g_pre=g_pre[:, None, :], g_post=g_post[:, None, :], pool_w=pool_w,
                 pool_scale=pool_scale[:, None, :], lb=lbs[:, None, :], hgrn_norm_g=hgrn_norm_g[:, None, :])

    (x1,), sv0 = _layer_fwd(0, xe, ada_me[0], w, small, ride_fwd0)
    (dx2, loss_blk), sv1 = _layer_fwd(1, x1, ada_me[1], w, small, ride_fwd1, target=te)

    parts, recv = {}, {}

    def pair_sums(keys, grads, tag):
        got = _rs_pair(grads, f"rs_pair_{tag}")
        for kl, g, o in zip(keys, grads, got):
            parts[kl] = _pair_add(core, g, o, f"rs_add_{kl[0]}{kl[1]}")

    def exchange(keys):
        def landed(outs):
            recv.update(zip(keys, outs))
        return _chips_carry([parts[kl] for kl in keys]), landed

    def early(l):
        return [(k, l) for k in big_names[1:]]

    def ride_hgrn1(big):
        pair_sums(early(1), [big[k] for k in big_names[1:]], "l1_early")
        return exchange(early(1))

    def ride_d_h1(big):
        pair_sums([("w_in", 1)], [big["w_in"]], "l1_w_in")
        return no_carry

    def ride_hgrn0(big):
        pair_sums(early(0), [big[k] for k in big_names[1:]], "l0_early")
        return exchange([("w_in", 1)] + early(0))

    def ride_d_h0(big):
        pair_sums([("w_in", 0)], [big["w_in"]], "l0_w_in")

        def landed(outs):
            (recv["w_in", 0],) = outs
        return _chips_carry([parts["w_in", 0]], piece=(0, 1, 2)), landed

    def ride_prenorm0(big):
        def landed(outs):
            (recv["w_in", 0],) = outs
        return _chips_carry([parts["w_in", 0]], piece=(1, 1, 2), into=[recv["w_in", 0]]), landed

    no_ride = lambda so_far: no_carry
    dx1, big1, little1 = _layer_bwd(1, dx2, sv1, w, small,
                                    dict(hgrn=ride_hgrn1, gw_in=no_ride, d_h=ride_d_h1, prenorm=no_ride))

    gathered = {}
    zero_row = jnp.zeros((1, D), F32)

    def ride_gw_in0(little):
        so_far = dict(little, dshift=zero_row, dscale=zero_row, g_pre=zero_row)

        def landed(outs):
            (gathered["early"],) = outs
        return _gather_rows_carry(_pack_small([so_far, little1])), landed

    dx0, big0, little0 = _layer_bwd(0, dx1, sv0, w, small,
                                    dict(hgrn=ride_hgrn0, gw_in=ride_gw_in0, d_h=ride_d_h0, prenorm=ride_prenorm0))
    loss = lax.psum(loss_blk[0, 0], ("x", "y", "c"))
    late = _rows8(jnp.stack([little0["dshift"], little0["dscale"], little0["g_pre"]]))
    late = _gather_small(late, "gather_small_late").reshape(NDEV, 8, D)
    packed = gathered["early"].reshape(NDEV, SMALL_ROWS, D)
    packed = packed.at[:, 0:2, :].set(late[:, 0:2, :]).at[:, 8:9, :].set(late[:, 2:3, :])
    red = []
    for k in big_names:
        both = _chip_sum(place, parts[k, 1], recv[k, 1], 1, None, f"rs_sum_{k}1")
        red.append(_chip_sum(place, parts[k, 0], recv[k, 0], 0, both, f"rs_sum_{k}0"))
    g_big = dict(zip(big_names, _rs_swap(red)))

    def upd(wt, g, m, v, name, carry=None):
        shp = wt.shape
        two = lambda t: t.reshape(-1, shp[-1])
        res = _adamw(two(wt), two(g), two(m), two(v), name, carry)
        return [t.reshape(shp) for t in res[:3]], res[3:]

    u_w_in, _ = upd(w_in, g_big["w_in"], m_w_in, v_w_in, "adamw_w_in")
    g_small = _sum_devices(packed)
    g_b_ada, g_g_pre, g_g_post, g_lb, g_pool_w, g_pool_scale, g_norm_g = _unpack_small(g_small)
    g_lb_logits = _lb_bwd(lb_logits, g_lb)
    d_ada_all = packed[:, 0:6, :].reshape(NDEV, 2, 3 * D)
    d_ada_sh = lax.dynamic_slice(jnp.transpose(d_ada_all, (1, 0, 2)), (0, 0, chip * ada_s), (2, NDEV, ada_s))
    d_ada_sh = jnp.pad(d_ada_sh, ((0, 0), (0, ADA_PAD - NDEV), (0, 0)))
    g_w_ada = _ada_wgrad(c_pad.T, d_ada_sh)

    u_w_ada, _ = upd(w_ada, g_w_ada, m_w_ada, v_w_ada, "adamw_w_ada")
    u_w_pool_o, _ = upd(w_pool_o, g_big["w_pool_o"], m_w_pool_o, v_w_pool_o, "adamw_w_pool_o")
    u_w_hgrn_o, _ = upd(w_hgrn_o, g_big["w_hgrn_o"], m_w_hgrn_o, v_w_hgrn_o, "adamw_w_hgrn_o")
    u_w_out, _ = upd(w_out, g_big["w_out"], m_w_out, v_w_out, "adamw_w_out")
    small_w = dict(b_ada=(b_ada, m_b_ada, v_b_ada), g_pre=(g_pre, m_g_pre, v_g_pre),
                   g_post=(g_post, m_g_post, v_g_post), lb_logits=(lb_logits, m_lb_logits, v_lb_logits),
                   pool_w=(pool_w, m_pool_w, v_pool_w), pool_scale=(pool_scale, m_pool_scale, v_pool_scale),
                   hgrn_norm_g=(hgrn_norm_g, m_hgrn_norm_g, v_hgrn_norm_g))
    in_rows = [tuple(t.reshape(rows, width) for t in small_w[key]) for key, _, rows, width in SMALL_PARTS]
    u_rows = _adamw_small(g_small, g_lb_logits, in_rows)
    u_small = {key: [t.reshape(small_w[key][0].shape) for t in u_rows[p]]
               for p, (key, _, _, _) in enumerate(SMALL_PARTS)}

    grads_out = (g_w_ada, g_b_ada, g_g_pre, g_g_post, g_big["w_in"], g_pool_w, g_pool_scale, g_lb_logits,
                 g_norm_g, g_big["w_pool_o"], g_big["w_hgrn_o"], g_big["w_out"])

    def ordered(k):
        s = lambda key: u_small[key][k]
        return (u_w_ada[k], s("b_ada"), s("g_pre"), s("g_post"), u_w_in[k], s("pool_w"), s("pool_scale"),
                s("lb_logits"), s("hgrn_norm_g"), u_w_pool_o[k], u_w_hgrn_o[k], u_w_out[k])

    return (loss, dx0[None], *grads_out, *ordered(0), *ordered(1), *ordered(2))
```

```python
import functools

import jax
import jax.numpy as jnp
from jax import lax
from jax.experimental import pallas as pl
from jax.experimental.pallas import tpu as pltpu

F32 = jnp.float32
BF16 = jnp.bfloat16
MESH = pl.DeviceIdType.MESH

D = 1024
HEADS = 8
HD = 128
GROUPS = 4
POOL_W = 512
WINDOWS = (2, 4, 8, 16)
CH = 64
SB = 16
NH = 2
IN_W = 7168
NCHIP = 4
NDEV = 8
EPS = 1e-6
PV0, PG0, HQ0, HF0, HI0, HG0 = 0, 4, 8, 16, 24, 32
MGP_BLK, MGH_BLK = 5, 6

LR, B1, B2, AEPS, WD, STEP = 0.001, 0.9, 0.999, 1e-08, 0.01, 10
VMEM_LIMIT = 56 * 1024 * 1024


def _cp(sem=None, **kw):
    if sem is not None:
        kw["dimension_semantics"] = sem
    return pltpu.CompilerParams(vmem_limit_bytes=VMEM_LIMIT, **kw)


def _sig(z):
    return 1.0 / (1.0 + jnp.exp(-z))


def _dsilu(z, s):
    return s * (1.0 + z * (1.0 - s))


def _row_tile(rows, cap):
    if rows <= cap:
        return rows
    t = 1 << (cap.bit_length() - 1)
    while rows % t:
        t //= 2
    return t


ANY = pl.BlockSpec(memory_space=pl.ANY)


class _Carry:
    def __init__(self, ins, outs, aliases, n_sem, start, finish):
        self.ins, self.outs, self.aliases, self.n_sem = list(ins), list(outs), dict(aliases), n_sem
        self.start, self.finish = start, finish


class _SemWindow:
    def __init__(self, ref, base):
        self._ref, self._base = ref, base

    @property
    def at(self):
        return self

    def __getitem__(self, k):
        return self._ref.at[self._base + k]


def _join_carries(*carries):
    ins, outs, aliases, spans, n_sem = [], [], {}, [], 0
    for cr in carries:
        aliases.update({len(ins) + i: len(outs) + o for i, o in cr.aliases.items()})
        spans.append((len(ins), len(cr.ins), len(outs), len(cr.outs), n_sem))
        ins, outs, n_sem = ins + cr.ins, outs + cr.outs, n_sem + cr.n_sem

    def run(which):
        def fn(i_refs, o_refs, send_sems, recv_sems):
            for cr, (i0, ni, o0, no, s0) in zip(carries, spans):
                getattr(cr, which)(i_refs[i0:i0 + ni], o_refs[o0:o0 + no], _SemWindow(send_sems, s0),
                                   _SemWindow(recv_sems, s0))
        return fn

    return _Carry(ins, outs, aliases, n_sem, run("start"), run("finish"))


def _call(body, *, name, grid, in_specs, out_specs, out_shape, args, scratch_shapes=(), sem=None, carry=None,
          aliases=None):
    in_specs, out_specs, out_shape = list(in_specs), list(out_specs), list(out_shape)
    scratch_shapes = list(scratch_shapes)
    aliases = dict(aliases or {})
    if carry is None:
        outs = pl.pallas_call(body, name=name, grid=grid, in_specs=in_specs, out_specs=out_specs,
                              out_shape=out_shape, scratch_shapes=scratch_shapes, input_output_aliases=aliases,
                              compiler_params=_cp(sem))(*args)
        return list(outs)
    n_in, n_out, n_scr = len(in_specs), len(out_specs), len(scratch_shapes)
    c_in, c_out = len(carry.ins), len(carry.outs)

    def wrapped(*refs):
        k_in, rest = refs[:n_in], refs[n_in:]
        ci, rest = rest[:c_in], rest[c_in:]
        k_out, rest = rest[:n_out], rest[n_out:]
        co, rest = rest[:c_out], rest[c_out:]
        k_scr, (ssem, rsem) = rest[:n_scr], rest[n_scr:]
        pids = [pl.program_id(d) for d in range(len(grid))]
        first = functools.reduce(jnp.logical_and, [p == 0 for p in pids])
        last = functools.reduce(jnp.logical_and, [p == g - 1 for p, g in zip(pids, grid)])

        @pl.when(first)
        def _():
            carry.start(ci, co, ssem, rsem)

        body(*k_in, *k_out, *k_scr)

        @pl.when(last)
        def _():
            carry.finish(ci, co, ssem, rsem)

    outs = pl.pallas_call(
        wrapped, name=name, grid=grid, in_specs=in_specs + [ANY] * c_in, out_specs=out_specs + [ANY] * c_out,
        out_shape=out_shape + carry.outs,
        input_output_aliases={**aliases, **{n_in + i: n_out + o for i, o in carry.aliases.items()}},
        scratch_shapes=scratch_shapes + [pltpu.SemaphoreType.DMA((carry.n_sem,))] * 2,
        compiler_params=_cp(("arbitrary",) * len(grid)),
    )(*args, *carry.ins)
    return list(outs)


def _run_carry(carry, name):
    c_in, c_out = len(carry.ins), len(carry.outs)

    def body(*refs):
        ci, co, (ssem, rsem) = refs[:c_in], refs[c_in:c_in + c_out], refs[c_in + c_out:]
        carry.start(ci, co, ssem, rsem)
        carry.finish(ci, co, ssem, rsem)

    outs = pl.pallas_call(
        body, name=name, in_specs=[ANY] * c_in, out_specs=[ANY] * c_out, out_shape=carry.outs,
        input_output_aliases=carry.aliases,
        scratch_shapes=[pltpu.SemaphoreType.DMA((carry.n_sem,))] * 2, compiler_params=_cp(),
    )(*carry.ins)
    return list(outs)


def _mm(a, b, *, name, b_mode="nn", out_shards=0, tm=1024, tn=256, tk=None, out_dtype=F32, carry=None):
    M, K = a.shape
    if b_mode == "nn":
        N = b.shape[1]
    elif b_mode == "nt":
        N = b.shape[0]
    elif b_mode == "nn_sh":
        N = b.shape[0] * b.shape[2]
    else:
        N = b.shape[1]
    tm = _row_tile(M, tm)
    if b_mode == "nn_sh":
        tn = _row_tile(b.shape[2], tn)
    elif out_shards:
        tn = _row_tile(N // out_shards, tn)
    else:
        tn = _row_tile(N, tn)
    if tk is None:
        tk = K if b_mode != "nt_shk" else b.shape[2]
    if b_mode == "nt_shk":
        tk = _row_tile(b.shape[2], tk)
    nm, nn, nk = M // tm, N // tn, K // tk

    a_spec = pl.BlockSpec((tm, tk), lambda m, n, k: (m, k))
    if b_mode == "nn":
        b_spec = pl.BlockSpec((tk, tn), lambda m, n, k: (k, n))
    elif b_mode == "nt":
        b_spec = pl.BlockSpec((tn, tk), lambda m, n, k: (n, k))
    elif b_mode == "nn_sh":
        nps = b.shape[2] // tn
        b_spec = pl.BlockSpec((None, tk, tn), lambda m, n, k: (n // nps, k, n % nps))
    else:
        kps = b.shape[2] // tk
        b_spec = pl.BlockSpec((None, tn, tk), lambda m, n, k: (k // kps, n, k % kps))
    if out_shards:
        ops = (N // out_shards) // tn
        o_spec = pl.BlockSpec((None, tm, tn), lambda m, n, k: (n // ops, m, n % ops))
        o_shape = jax.ShapeDtypeStruct((out_shards, M, N // out_shards), out_dtype)
    else:
        o_spec = pl.BlockSpec((tm, tn), lambda m, n, k: (m, n))
        o_shape = jax.ShapeDtypeStruct((M, N), out_dtype)
    trans_b = b_mode in ("nt", "nt_shk")
    dn = (((1,), (1,)), ((), ())) if trans_b else (((1,), (0,)), ((), ()))

    def body(a_ref, b_ref, o_ref, acc_ref):
        k = pl.program_id(2)

        @pl.when(k == 0)
        def _():
            acc_ref[...] = jnp.zeros(acc_ref.shape, F32)

        acc_ref[...] += lax.dot_general(a_ref[...].astype(BF16), b_ref[...].astype(BF16), dn,
                                        preferred_element_type=F32)

        @pl.when(k == nk - 1)
        def _():
            o_ref[...] = acc_ref[...].astype(o_ref.dtype)

    outs = _call(body, name=name, grid=(nm, nn, nk), in_specs=[a_spec, b_spec], out_specs=[o_spec],
                 out_shape=[o_shape], scratch_shapes=[pltpu.VMEM((tm, tn), F32)],
                 sem=("parallel", "parallel", "arbitrary"), args=(a, b), carry=carry)
    return outs[0] if carry is None else (outs[0], outs[1:])


def _rowvec(n=D):
    return pl.BlockSpec((1, n), lambda i: (0, 0))


def _prenorm_fwd(x, g, scale, shift, name, carry=None):
    S = x.shape[0]
    tr = _row_tile(S, 256)

    def body(x_ref, g_ref, sc_ref, sh_ref, h_ref, ht_ref):
        xv = x_ref[...]
        r = lax.rsqrt(jnp.mean(xv * xv, axis=-1, keepdims=True) + EPS)
        hv = (xv * r) * g_ref[...] * (1.0 + sc_ref[...]) + sh_ref[...]
        h_ref[...] = hv.astype(BF16)
        ht_ref[...] = hv.T.astype(BF16)

    outs = _call(
        body, name=name, grid=(S // tr,),
        in_specs=[pl.BlockSpec((tr, D), lambda i: (i, 0)), _rowvec(), _rowvec(), _rowvec()],
        out_specs=[pl.BlockSpec((tr, D), lambda i: (i, 0)), pl.BlockSpec((D, tr), lambda i: (0, i))],
        out_shape=[jax.ShapeDtypeStruct((S, D), BF16), jax.ShapeDtypeStruct((D, S), BF16)],
        sem=("parallel",), args=(x, g, scale, shift), carry=carry)
    return outs[:2], outs[2:]


def _prenorm_bwd(dh, dxn, x, g, scale, name, carry=None):
    S = x.shape[0]
    tr = _row_tile(S, 256)

    def body(dh_ref, dxn_ref, x_ref, g_ref, sc_ref, dx_ref, dsh_ref, dsc_ref, dg_ref):
        i = pl.program_id(0)

        @pl.when(i == 0)
        def _():
            dsh_ref[...] = jnp.zeros((1, D), F32)
            dsc_ref[...] = jnp.zeros((1, D), F32)
            dg_ref[...] = jnp.zeros((1, D), F32)

        xv = x_ref[...]
        dhv = dh_ref[...]
        gv = g_ref[...]
        mod = 1.0 + sc_ref[...]
        r = lax.rsqrt(jnp.mean(xv * xv, axis=-1, keepdims=True) + EPS)
        xh = xv * r
        dsh_ref[...] += jnp.sum(dhv, axis=0, keepdims=True)
        dsc_ref[...] += jnp.sum(dhv * (xh * gv), axis=0, keepdims=True)
        dg_ref[...] += jnp.sum(dhv * mod * xh, axis=0, keepdims=True)
        u = dhv * mod * gv
        dx_ref[...] = dxn_ref[...] + r * u - xv * (r * r * r) * jnp.mean(u * xv, axis=-1, keepdims=True)

    tile = pl.BlockSpec((tr, D), lambda i: (i, 0))
    outs = _call(
        body, name=name, grid=(S // tr,),
        in_specs=[tile, tile, tile, _rowvec(), _rowvec()],
        out_specs=[tile, _rowvec(), _rowvec(), _rowvec()],
        out_shape=[jax.ShapeDtypeStruct((S, D), F32)] + [jax.ShapeDtypeStruct((1, D), F32)] * 3,
        sem=("arbitrary",), args=(dh, dxn, x, g, scale), carry=carry)
    return outs[:4], outs[4:]


def _postnorm_fwd(x, y, gate, g, name):
    S = x.shape[0]
    tr = _row_tile(S, 256)

    def body(x_ref, y_ref, gate_ref, g_ref, o_ref):
        yv = y_ref[...]
        r = lax.rsqrt(jnp.mean(yv * yv, axis=-1, keepdims=True) + EPS)
        o_ref[...] = x_ref[...] + gate_ref[...] * ((yv * r) * g_ref[...])

    tile = pl.BlockSpec((tr, D), lambda i: (i, 0))
    return pl.pallas_call(
        body, name=name, grid=(S // tr,), in_specs=[tile, tile, _rowvec(), _rowvec()],
        out_specs=tile, out_shape=jax.ShapeDtypeStruct((S, D), F32), compiler_params=_cp(("parallel",)),
    )(x, y, gate, g)


def _postnorm_bwd(dxn, y, gate, g, name):
    S = y.shape[0]
    tr = _row_tile(S, 256)

    def body(dxn_ref, y_ref, gate_ref, g_ref, dy_ref, dgate_ref, dg_ref):
        i = pl.program_id(0)

        @pl.when(i == 0)
        def _():
            dgate_ref[...] = jnp.zeros((1, D), F32)
            dg_ref[...] = jnp.zeros((1, D), F32)

        yv = y_ref[...]
        dv = dxn_ref[...]
        gv = g_ref[...]
        gt = gate_ref[...]
        r = lax.rsqrt(jnp.mean(yv * yv, axis=-1, keepdims=True) + EPS)
        yh = yv * r
        dgate_ref[...] += jnp.sum(dv * (yh * gv), axis=0, keepdims=True)
        dg_ref[...] += jnp.sum(dv * gt * yh, axis=0, keepdims=True)
        u = dv * gt * gv
        dy_ref[...] = (r * u - yv * (r * r * r) * jnp.mean(u * yv, axis=-1, keepdims=True)).astype(BF16)

    tile = pl.BlockSpec((tr, D), lambda i: (i, 0))
    return pl.pallas_call(
        body, name=name, grid=(S // tr,), in_specs=[tile, tile, _rowvec(), _rowvec()],
        out_specs=[tile, _rowvec(), _rowvec()],
        out_shape=[jax.ShapeDtypeStruct((S, D), BF16), jax.ShapeDtypeStruct((1, D), F32),
                   jax.ShapeDtypeStruct((1, D), F32)],
        compiler_params=_cp(("arbitrary",)),
    )(dxn, y, gate, g)


def _loss_head(xo, target, name):
    S = xo.shape[0]
    tr = _row_tile(S, 256)

    def body(x_ref, t_ref, dx_ref, l_ref):
        i = pl.program_id(0)

        @pl.when(i == 0)
        def _():
            l_ref[...] = jnp.zeros((8, 128), F32)

        err = x_ref[...] - t_ref[...]
        dx_ref[...] = err * (1.0 / D)
        l_ref[...] += 0.5 * jnp.sum(jnp.mean(err * err, axis=-1, keepdims=True))

    tile = pl.BlockSpec((tr, D), lambda i: (i, 0))
    return pl.pallas_call(
        body, name=name, grid=(S // tr,), in_specs=[tile, tile],
        out_specs=[tile, pl.BlockSpec((8, 128), lambda i: (0, 0))],
        out_shape=[jax.ShapeDtypeStruct((S, D), F32), jax.ShapeDtypeStruct((8, 128), F32)],
        compiler_params=_cp(("arbitrary",)),
    )(xo, target)


def _layer_tail_fwd(proj, a_in, b_in, x, w_po, w_ho, w_out, gate, g, name, target=None, carry=None):
    S = proj.shape[0]
    tr = _row_tile(S, 256)
    nsh, _, wsh = w_po.shape
    n_in = 10 + (target is not None)

    def body(*refs):
        (mgp_ref, mgh_ref, a_ref, b_ref, x_ref, wpo_ref, who_ref, wout_ref, gate_ref, g_ref) = refs[:10]
        bra_ref, brb_ref, mt_ref, y_ref, xn_ref = refs[n_in:n_in + 5]
        av = a_ref[...]
        bra = jnp.concatenate([jnp.dot(av, wpo_ref[j], preferred_element_type=F32) for j in range(nsh)], axis=1)
        brb = jnp.dot(b_ref[...], who_ref[...], preferred_element_type=F32)
        mv = _sig(mgp_ref[...].astype(F32)) * bra + _sig(mgh_ref[...].astype(F32)) * brb
        bra_ref[...] = bra.astype(BF16)
        brb_ref[...] = brb.astype(BF16)
        mt_ref[...] = mv.T.astype(BF16)
        yv = jnp.dot(mv.astype(BF16), wout_ref[...], preferred_element_type=F32)
        y_ref[...] = yv
        r = lax.rsqrt(jnp.mean(yv * yv, axis=-1, keepdims=True) + EPS)
        xn = x_ref[...] + gate_ref[...] * ((yv * r) * g_ref[...])
        if target is None:
            xn_ref[...] = xn
        else:
            t_ref, l_ref = refs[10], refs[n_in + 5]

            @pl.when(pl.program_id(0) == 0)
            def _():
                l_ref[...] = jnp.zeros((8, 128), F32)

            err = xn - t_ref[...]
            xn_ref[...] = err * (1.0 / D)
            l_ref[...] += 0.5 * jnp.sum(jnp.mean(err * err, axis=-1, keepdims=True))

    tile = pl.BlockSpec((tr, D), lambda i: (i, 0))
    whole = lambda t: pl.BlockSpec(t.shape, lambda i: (0,) * t.ndim)
    last = target is not None
    outs = _call(
        body, name=name, grid=(S // tr,),
        in_specs=[pl.BlockSpec((tr, D), lambda i: (i, MGP_BLK)), pl.BlockSpec((tr, D), lambda i: (i, MGH_BLK)),
                  pl.BlockSpec((tr, POOL_W), lambda i: (i, 0)), tile, tile, whole(w_po), whole(w_ho),
                  whole(w_out), _rowvec(), _rowvec()] + [tile] * last,
        out_specs=[tile, tile, pl.BlockSpec((D, tr), lambda i: (0, i)), tile, tile]
        + [pl.BlockSpec((8, 128), lambda i: (0, 0))] * last,
        out_shape=[jax.ShapeDtypeStruct((S, D), BF16), jax.ShapeDtypeStruct((S, D), BF16),
                   jax.ShapeDtypeStruct((D, S), BF16), jax.ShapeDtypeStruct((S, D), F32),
                   jax.ShapeDtypeStruct((S, D), F32)] + [jax.ShapeDtypeStruct((8, 128), F32)] * last,
        sem=("arbitrary",) if last else ("parallel",),
        args=(proj, proj, a_in, b_in, x, w_po, w_ho, w_out, gate, g) + ((target,) if last else ()), carry=carry)
    return outs[:5 + last], outs[5 + last:]


def _layer_head_bwd(dxn, y, proj, br_a, br_b, w_po, w_ho, w_out, gate, g, name):
    S = y.shape[0]
    tr = _row_tile(S, 256)
    nsh, _, wsh = w_po.shape

    def body(dxn_ref, y_ref, mgp_ref, mgh_ref, bra_ref, brb_ref, wpo_ref, who_ref, wout_ref, gate_ref, g_ref,
             dy_ref, dba_ref, dbb_ref, dproj_ref, dain_ref, dbin_ref, dgate_ref, dg_ref, dmgh_s):
        i = pl.program_id(0)
        j = pl.program_id(1)

        @pl.when((i == 0) & (j == 0))
        def _():
            dgate_ref[...] = jnp.zeros((1, D), F32)
            dg_ref[...] = jnp.zeros((1, D), F32)

        @pl.when(j == 1)
        def _():
            dproj_ref[...] = dmgh_s[...]

        @pl.when(j == 0)
        def _():
            everything(dxn_ref, y_ref, mgp_ref, mgh_ref, bra_ref, brb_ref, wpo_ref, who_ref, wout_ref, gate_ref,
                       g_ref, dy_ref, dba_ref, dbb_ref, dproj_ref, dain_ref, dbin_ref, dgate_ref, dg_ref, dmgh_s)

    def everything(dxn_ref, y_ref, mgp_ref, mgh_ref, bra_ref, brb_ref, wpo_ref, who_ref, wout_ref, gate_ref, g_ref,
                   dy_ref, dba_ref, dbb_ref, dproj_ref, dain_ref, dbin_ref, dgate_ref, dg_ref, dmgh_s):
        yv = y_ref[...]
        dv = dxn_ref[...]
        gv = g_ref[...]
        gt = gate_ref[...]
        r = lax.rsqrt(jnp.mean(yv * yv, axis=-1, keepdims=True) + EPS)
        yh = yv * r
        dgate_ref[...] += jnp.sum(dv * (yh * gv), axis=0, keepdims=True)
        dg_ref[...] += jnp.sum(dv * gt * yh, axis=0, keepdims=True)
        u = dv * gt * gv
        dy = (r * u - yv * (r * r * r) * jnp.mean(u * yv, axis=-1, keepdims=True)).astype(BF16)
        dy_ref[...] = dy
        dm = _dot_nt(dy, wout_ref[...])
        sp = _sig(mgp_ref[...].astype(F32))
        sh = _sig(mgh_ref[...].astype(F32))
        dba = (dm * sp).astype(BF16)
        dbb = (dm * sh).astype(BF16)
        dba_ref[...] = dba
        dbb_ref[...] = dbb
        dproj_ref[...] = (dm * bra_ref[...].astype(F32) * sp * (1.0 - sp)).astype(BF16)
        dmgh_s[...] = (dm * brb_ref[...].astype(F32) * sh * (1.0 - sh)).astype(BF16)
        dain = _dot_nt(dba[:, 0:wsh], wpo_ref[0])
        for k in range(1, nsh):
            dain = dain + _dot_nt(dba[:, k * wsh:(k + 1) * wsh], wpo_ref[k])
        dain_ref[...] = dain
        dbin_ref[...] = _dot_nt(dbb, who_ref[...])

    tile = pl.BlockSpec((tr, D), lambda i, j: (i, 0))
    whole = lambda t: pl.BlockSpec(t.shape, lambda i, j: (0,) * t.ndim)
    vec = pl.BlockSpec((1, D), lambda i, j: (0, 0))
    ahead = lambda i, j: jnp.minimum(i + j, S // tr - 1)
    tile_in = pl.BlockSpec((tr, D), lambda i, j: (ahead(i, j), 0))
    return pl.pallas_call(
        body, name=name, grid=(S // tr, 2),
        in_specs=[tile_in, tile_in, pl.BlockSpec((tr, D), lambda i, j: (ahead(i, j), MGP_BLK)),
                  pl.BlockSpec((tr, D), lambda i, j: (ahead(i, j), MGH_BLK)), tile_in, tile_in, whole(w_po),
                  whole(w_ho), whole(w_out), vec, vec],
        out_specs=[tile, tile, tile, pl.BlockSpec((tr, D), lambda i, j: (i, MGP_BLK + j)),
                   pl.BlockSpec((tr, POOL_W), lambda i, j: (i, 0)), tile, vec, vec],
        out_shape=[jax.ShapeDtypeStruct((S, D), BF16)] * 3
        + [jax.ShapeDtypeStruct((S, IN_W), BF16), jax.ShapeDtypeStruct((S, POOL_W), F32),
           jax.ShapeDtypeStruct((S, D), F32), jax.ShapeDtypeStruct((1, D), F32), jax.ShapeDtypeStruct((1, D), F32)],
        scratch_shapes=[pltpu.VMEM((tr, D), BF16)],
        compiler_params=_cp(("arbitrary", "arbitrary")),
    )(dxn, y, proj, proj, br_a, br_b, w_po, w_ho, w_out, gate, g)


def _merge_fwd(proj, br_a, br_b, name):
    S = proj.shape[0]
    tr = _row_tile(S, 256)

    def body(mgp_ref, mgh_ref, a_ref, b_ref, o_ref, ot_ref):
        mv = _sig(mgp_ref[...]) * a_ref[...] + _sig(mgh_ref[...]) * b_ref[...]
        o_ref[...] = mv.astype(BF16)
        ot_ref[...] = mv.T.astype(BF16)

    tile = pl.BlockSpec((tr, D), lambda i: (i, 0))
    return pl.pallas_call(
        body, name=name, grid=(S // tr,),
        in_specs=[pl.BlockSpec((tr, D), lambda i: (i, MGP_BLK)), pl.BlockSpec((tr, D), lambda i: (i, MGH_BLK)),
                  tile, tile],
        out_specs=[tile, pl.BlockSpec((D, tr), lambda i: (0, i))],
        out_shape=[jax.ShapeDtypeStruct((S, D), BF16), jax.ShapeDtypeStruct((D, S), BF16)],
        compiler_params=_cp(("parallel",)),
    )(proj, proj, br_a, br_b)


def _merge_bwd(dm, proj, br_a, br_b, name):
    S = proj.shape[0]
    tr = _row_tile(S, 256)

    def body(dm_ref, mgp_ref, mgh_ref, a_ref, b_ref, da_ref, db_ref, dmg_ref):
        dmv = dm_ref[...]
        sp = _sig(mgp_ref[...])
        sh = _sig(mgh_ref[...])
        da_ref[...] = (dmv * sp).astype(BF16)
        db_ref[...] = (dmv * sh).astype(BF16)
        dmg_ref[:, 0:D] = (dmv * a_ref[...] * sp * (1.0 - sp)).astype(BF16)
        dmg_ref[:, D:2 * D] = (dmv * b_ref[...] * sh * (1.0 - sh)).astype(BF16)

    tile = pl.BlockSpec((tr, D), lambda i: (i, 0))
    return pl.pallas_call(
        body, name=name, grid=(S // tr,),
        in_specs=[tile, pl.BlockSpec((tr, D), lambda i: (i, MGP_BLK)),
                  pl.BlockSpec((tr, D), lambda i: (i, MGH_BLK)), tile, tile],
        out_specs=[tile, tile, pl.BlockSpec((tr, 2 * D), lambda i: (i, 0))],
        out_shape=[jax.ShapeDtypeStruct((S, D), BF16), jax.ShapeDtypeStruct((S, D), BF16),
                   jax.ShapeDtypeStruct((S, 2 * D), BF16)],
        compiler_params=_cp(("parallel",)),
    )(dm, proj, proj, br_a, br_b)


def _pool_pieces(u, g, S):
    rowi = lax.broadcasted_iota(jnp.int32, (S, 1), 0)

    def down(z, k):
        return jnp.where(rowi >= k, pltpu.roll(z, k, axis=0), 0.0)

    s2 = u + down(u, 1)
    s4 = s2 + down(s2, 2)
    s8 = s4 + down(s4, 4)
    s16 = s8 + down(s8, 8)
    win = jnp.where(g == 0, s2, jnp.where(g == 1, s4, jnp.where(g == 2, s8, s16)))
    w = jnp.where(g == 0, 2, jnp.where(g == 1, 4, jnp.where(g == 2, 8, 16)))
    count = jnp.minimum(rowi + 1, w).astype(F32)
    return win / count - u, count, rowi


def _pool_fwd(proj, pw, pscale, name):
    S = proj.shape[0]

    def body(pv_ref, pg_ref, pw_ref, sc_ref, a_ref, at_ref):
        g = pl.program_id(0)
        pooled, _, _ = _pool_pieces(pv_ref[...].astype(F32), g, S)
        pm = jnp.dot(pooled.astype(BF16), pw_ref[...].astype(BF16), preferred_element_type=F32)
        pgv = pg_ref[...].astype(F32)
        av = pm * sc_ref[...] * (pgv * _sig(pgv))
        a_ref[...] = av.astype(BF16)
        at_ref[...] = av.T.astype(BF16)

    return pl.pallas_call(
        body, name=name, grid=(GROUPS,),
        in_specs=[pl.BlockSpec((S, 128), lambda g: (0, PV0 + g)), pl.BlockSpec((S, 128), lambda g: (0, PG0 + g)),
                  pl.BlockSpec((None, 128, 128), lambda g: (g, 0, 0)), pl.BlockSpec((1, 128), lambda g: (0, g))],
        out_specs=[pl.BlockSpec((S, 128), lambda g: (0, g)), pl.BlockSpec((128, S), lambda g: (g, 0))],
        out_shape=[jax.ShapeDtypeStruct((S, POOL_W), BF16), jax.ShapeDtypeStruct((POOL_W, S), BF16)],
        compiler_params=_cp(("parallel",)),
    )(proj, proj, pw, pscale)


def _pool_bwd(da, proj, pw, pscale, dproj, name):
    S = proj.shape[0]

    def body(da_ref, pv_ref, pg_ref, pw_ref, sc_ref, dproj_in, dproj_ref, dpw_ref, dsc_ref, dpg_s):
        @pl.when(pl.program_id(1) == 1)
        def _():
            dproj_ref[...] = dpg_s[...]

        @pl.when(pl.program_id(1) == 0)
        def _():
            group(da_ref, pv_ref, pg_ref, pw_ref, sc_ref, dproj_ref, dpg_s, dpw_ref, dsc_ref)

    def group(da_ref, pv_ref, pg_ref, pw_ref, sc_ref, dpv_ref, dpg_ref, dpw_ref, dsc_ref):
        g = pl.program_id(0)
        pooled, count, rowi = _pool_pieces(pv_ref[...].astype(F32), g, S)
        pwb = pw_ref[...].astype(BF16)
        pm = jnp.dot(pooled.astype(BF16), pwb, preferred_element_type=F32)
        scv = sc_ref[...]
        pgv = pg_ref[...].astype(F32)
        sg = _sig(pgv)
        dav = da_ref[...]
        d_ps = dav * (pgv * sg)
        dpg_ref[...] = (dav * (pm * scv) * _dsilu(pgv, sg)).astype(BF16)
        dsc_ref[...] = jnp.sum(d_ps * pm, axis=0, keepdims=True)
        d_pm = (d_ps * scv).astype(BF16)
        dpw_ref[...] = lax.dot_general(pooled.astype(BF16), d_pm, (((0,), (0,)), ((), ())),
                                       preferred_element_type=F32)
        d_pooled = lax.dot_general(d_pm, pwb, (((1,), (1,)), ((), ())), preferred_element_type=F32)
        z = d_pooled / count

        def up(v, k):
            return jnp.where(rowi < S - k, pltpu.roll(v, S - k, axis=0), 0.0)

        t2 = z + up(z, 1)
        t4 = t2 + up(t2, 2)
        t8 = t4 + up(t4, 4)
        t16 = t8 + up(t8, 8)
        adj = jnp.where(g == 0, t2, jnp.where(g == 1, t4, jnp.where(g == 2, t8, t16)))
        dpv_ref[...] = (adj - d_pooled).astype(BF16)

    col = lambda g, j: (0, g)
    ahead = lambda g, j: jnp.minimum(g + j, GROUPS - 1)
    return pl.pallas_call(
        body, name=name, grid=(GROUPS, 2),
        in_specs=[pl.BlockSpec((S, 128), lambda g, j: (0, ahead(g, j))),
                  pl.BlockSpec((S, 128), lambda g, j: (0, PV0 + ahead(g, j))),
                  pl.BlockSpec((S, 128), lambda g, j: (0, PG0 + ahead(g, j))),
                  pl.BlockSpec((None, 128, 128), lambda g, j: (ahead(g, j), 0, 0)),
                  pl.BlockSpec((1, 128), lambda g, j: (0, ahead(g, j))), ANY],
        out_specs=[pl.BlockSpec((S, 128), lambda g, j: (0, PV0 + g + (PG0 - PV0) * j)),
                   pl.BlockSpec((None, 128, 128), lambda g, j: (g, 0, 0)), pl.BlockSpec((1, 128), col)],
        out_shape=[jax.ShapeDtypeStruct(dproj.shape, dproj.dtype),
                   jax.ShapeDtypeStruct((GROUPS, 128, 128), F32), jax.ShapeDtypeStruct((1, POOL_W), F32)],
        scratch_shapes=[pltpu.VMEM((S, 128), BF16)], input_output_aliases={5: 0},
        compiler_params=_cp(("arbitrary", "arbitrary")),
    )(da, proj, proj, pw, pscale, dproj)


def _chunk_cumsum(z, rowi):
    for sh in (1, 2, 4, 8, 16, 32):
        z = z + jnp.where(rowi >= sh, pltpu.roll(z, sh, axis=0), 0.0)
    return z


def _chunk_rev_cumsum(z, rowi):
    for sh in (1, 2, 4, 8, 16, 32):
        z = z + jnp.where(rowi < CH - sh, pltpu.roll(z, CH - sh, axis=0), 0.0)
    return z


def _dot_nn(a, b):
    return jnp.dot(a.astype(BF16), b.astype(BF16), preferred_element_type=F32)


def _dot_nt(a, b):
    return lax.dot_general(a.astype(BF16), b.astype(BF16), (((1,), (1,)), ((), ())), preferred_element_type=F32)


def _dot_tn(a, b):
    return lax.dot_general(a.astype(BF16), b.astype(BF16), (((0,), (0,)), ((), ())), preferred_element_type=F32)


def _gates(hq, hf, lbv):
    hq, hf = hq.astype(F32), hf.astype(F32)
    sq = _sig(hq)
    sf = _sig(hf)
    f = lbv + (1.0 - lbv) * sf
    fc = jnp.maximum(f, 1e-30)
    return hq * sq, sq, sf, f, fc, jnp.log(fc)


DECAY_CAP = 60.0


def _block_ref(c_ref, i):
    if i == 0:
        return jnp.zeros((1, HD), F32)
    return c_ref[SB * i - 1:SB * i, :]


def _block_decay(c_ref):
    spans = [_block_ref(c_ref, i) - c_ref[SB * (i + 1) - 1:SB * (i + 1), :] for i in range(CH // SB)]
    return functools.reduce(jnp.maximum, spans)


def _hgrn_fwd(proj, lb, gn, name, carry=None):
    S = proj.shape[0]
    nch = S // CH
    W = NH * HD

    def body(hq_ref, hf_ref, hi_ref, hg_ref, lb_ref, gn_ref, bin_ref, bint_ref, oraw_ref, st_ref, mild_ref,
             cum_ref, q_s, k_s, c_s, v_s, o_s, state_s, qf_s, kf_s, cf_s):
        state_s[...] = jnp.zeros((NH, HD, HD), F32)
        rowi = lax.broadcasted_iota(jnp.int32, (CH, 1), 0)
        coli = lax.broadcasted_iota(jnp.int32, (1, CH), 1)
        sbi = lax.broadcasted_iota(jnp.int32, (SB, 1), 0)
        gnv = gn_ref[...]

        def gates_pass(n, worst):
            rows = pl.ds(pl.multiple_of(n * CH, CH), CH)
            for hh in range(NH):
                lanes = slice(hh * HD, (hh + 1) * HD)
                q, _, _, f, _, logf = _gates(hq_ref[rows, lanes], hf_ref[rows, lanes], lb_ref[:, lanes])
                c = _chunk_cumsum(logf, rowi)
                qf_s[hh, rows, :] = q
                kf_s[hh, rows, :] = 1.0 - f
                cf_s[hh, rows, :] = c
                cum_ref[rows, lanes] = c
                c_s[hh] = c
                worst = jnp.maximum(worst, _block_decay(c_s.at[hh]))
            return worst

        def between_chunks(hh, n, rows):
            lanes = slice(hh * HD, (hh + 1) * HD)
            q = qf_s[hh, rows, :]
            k = kf_s[hh, rows, :]
            c = cf_s[hh, rows, :]
            v = hi_ref[rows, lanes].astype(F32)
            q_s[hh] = q
            k_s[hh] = k
            c_s[hh] = c
            v_s[hh] = v
            st = state_s[hh]
            st_ref[hh, n] = st.astype(BF16)
            o_s[hh] = _dot_nt(q * jnp.exp(c), st)
            last = c_s[hh, CH - 1:CH, :]
            state_s[hh] = st * jnp.exp(last) + _dot_tn(v, k * jnp.exp(last - c))

        def within_chunk_matmul(hh):
            q, k, c, v = q_s[hh], k_s[hh], c_s[hh], v_s[hh]
            a = jnp.zeros((CH, CH), F32)
            for i in range(CH // SB):
                r_i = _block_ref(c_s.at[hh], i)
                qi = q * jnp.exp(jnp.minimum(c - r_i, 0.0))
                kei = k * jnp.exp(jnp.minimum(r_i - c, DECAY_CAP))
                m_i = (rowi >= SB * i) & (rowi < SB * (i + 1)) & (coli <= rowi)
                a = a + jnp.where(m_i, _dot_nt(qi, kei), 0.0)
            o_s[hh] += _dot_nn(a, v)

        def within_chunk_exact(hh):
            q, k, c, v = q_s[hh], k_s[hh], c_s[hh], v_s[hh]
            a_off = jnp.zeros((CH, CH), F32)
            for i in range(1, CH // SB):
                r_i = _block_ref(c_s.at[hh], i)
                qi = q * jnp.exp(jnp.minimum(c - r_i, 0.0))
                kei = k * jnp.exp(jnp.minimum(r_i - c, 0.0))
                m_i = (rowi >= SB * i) & (rowi < SB * (i + 1)) & (coli < SB * i)
                a_off = a_off + jnp.where(m_i, _dot_nt(qi, kei), 0.0)
            o_s[hh] += _dot_nn(a_off, v)
            for i in range(CH // SB):
                blk = slice(SB * i, SB * (i + 1))
                qb = q_s[hh, blk, :]
                cb = c_s[hh, blk, :]
                acc = jnp.zeros((SB, HD), F32)
                for s in range(SB):
                    row = SB * i + s
                    w = jnp.exp(jnp.minimum(cb - c_s[hh, row:row + 1, :], 0.0))
                    a_col = jnp.sum(qb * k_s[hh, row:row + 1, :] * w, axis=-1, keepdims=True)
                    acc = acc + jnp.where(sbi >= s, a_col, 0.0) * v_s[hh, row:row + 1, :]
                o_s[hh, blk, :] += acc

        def norm_and_gate(hh, rows):
            lanes = slice(hh * HD, (hh + 1) * HD)
            ov = o_s[hh]
            oraw_ref[rows, lanes] = ov
            r = lax.rsqrt(jnp.mean(ov * ov, axis=-1, keepdims=True) + EPS)
            hg = hg_ref[rows, lanes].astype(F32)
            bin_ref[rows, lanes] = ((ov * r) * gnv * (hg * _sig(hg))).astype(BF16)

        def chunk_with(within_chunk):
            def chunk(n, carry):
                rows = pl.ds(pl.multiple_of(n * CH, CH), CH)
                for hh in range(NH):
                    between_chunks(hh, n, rows)
                for hh in range(NH):
                    within_chunk(hh)
                for hh in range(NH):
                    norm_and_gate(hh, rows)
                return carry
            return chunk

        worst = lax.fori_loop(0, nch, gates_pass, jnp.zeros((1, HD), F32))
        mild = jnp.max(worst) <= DECAY_CAP
        mild_ref[...] = jnp.broadcast_to(jnp.where(mild, 1.0, 0.0), (8, HD))

        @pl.when(mild)
        def _():
            lax.fori_loop(0, nch, chunk_with(within_chunk_matmul), 0, unroll=4)

        @pl.when(jnp.logical_not(mild))
        def _():
            lax.fori_loop(0, nch, chunk_with(within_chunk_exact), 0)

        bint_ref[...] = bin_ref[...].astype(F32).T.astype(BF16)

    col = lambda off: pl.BlockSpec((S, W), lambda h: (0, off // NH + h))
    head = pl.BlockSpec((S, W), lambda h: (0, h))
    outs = _call(
        body, name=name, grid=(HEADS // NH,),
        in_specs=[col(HQ0), col(HF0), col(HI0), col(HG0), pl.BlockSpec((1, W), lambda h: (0, h)),
                  pl.BlockSpec((1, HD), lambda h: (0, 0))],
        out_specs=[head, pl.BlockSpec((W, S), lambda h: (h, 0)), head,
                   pl.BlockSpec((NH, nch, HD, HD), lambda h: (h, 0, 0, 0)),
                   pl.BlockSpec((8, HD), lambda h: (h, 0)), head],
        out_shape=[jax.ShapeDtypeStruct((S, D), BF16), jax.ShapeDtypeStruct((D, S), BF16),
                   jax.ShapeDtypeStruct((S, D), F32), jax.ShapeDtypeStruct((HEADS, nch, HD, HD), BF16),
                   jax.ShapeDtypeStruct((8 * HEADS // NH, HD), F32), jax.ShapeDtypeStruct((S, D), F32)],
        scratch_shapes=[pltpu.VMEM((NH, CH, HD), F32)] * 5 + [pltpu.VMEM((NH, HD, HD), F32)]
        + [pltpu.VMEM((NH, S, HD), F32)] * 3,
        sem=("parallel",), args=(proj, proj, proj, proj, lb, gn), carry=carry)
    return outs[:6], outs[6:]


def _hgrn_bwd(dbin, proj, oraw, states, mild, cum, lb, gn, dproj, name, carry=None):
    S = proj.shape[0]
    nch = S // CH
    W = NH * HD
    n_in = 12

    def body(*refs):
        ins, (dproj_ref, dlb_ref, dgn_ref) = refs[:n_in - 1], refs[n_in:n_in + 3]
        scratch, later = refs[n_in + 3:-3], refs[-3:]
        seg = pl.program_id(1)

        @pl.when(seg == 0)
        def _():
            heads(*ins, dproj_ref, *later, dlb_ref, dgn_ref, *scratch)

        for s, kept in enumerate(later):
            @pl.when(seg == s + 1)
            def _(kept=kept):
                dproj_ref[...] = kept[...]

    def heads(db_ref, hq_ref, hf_ref, hi_ref, hg_ref, or_ref, st_ref, mild_ref, cum_ref, lb_ref, gn_ref,
              dq_ref, df_ref, di_ref, dg_ref, dlb_ref, dgn_ref,
              q_s, k_s, c_s, v_s, do_s, dq_s, dk_s, dv_s, dc_s, dqd_s, dkd_s, f_s, sf_s, sq_s, dl_s, dst_s,
              dlb_s, dgn_s):
        dst_s[...] = jnp.zeros((NH, HD, HD), F32)
        dlb_s[...] = jnp.zeros((1, W), F32)
        dgn_s[...] = jnp.zeros((1, HD), F32)
        rowi = lax.broadcasted_iota(jnp.int32, (CH, 1), 0)
        rowi2 = lax.broadcasted_iota(jnp.int32, (CH, CH), 0)
        coli2 = lax.broadcasted_iota(jnp.int32, (CH, CH), 1)
        sbi = lax.broadcasted_iota(jnp.int32, (SB, 1), 0)
        gnv = gn_ref[...]
        def between_chunks(hh, n, rows):
            lanes = slice(hh * HD, (hh + 1) * HD)
            lbv = lb_ref[:, lanes]
            hq = hq_ref[rows, lanes].astype(F32)
            sq = _sig(hq)
            sf = _sig(hf_ref[rows, lanes].astype(F32))
            f = lbv + (1.0 - lbv) * sf
            q = hq * sq
            k = 1.0 - f
            f_s[hh] = f
            sf_s[hh] = sf
            sq_s[hh] = sq
            v = hi_ref[rows, lanes].astype(F32)
            c = cum_ref[rows, lanes]
            ov = or_ref[rows, lanes]
            hg = hg_ref[rows, lanes].astype(F32)
            sg = _sig(hg)
            r = lax.rsqrt(jnp.mean(ov * ov, axis=-1, keepdims=True) + EPS)
            dbv = db_ref[rows, lanes]
            d_on = dbv * (hg * sg)
            dg_ref[rows, lanes] = (dbv * ((ov * r) * gnv) * _dsilu(hg, sg)).astype(BF16)
            dgn_s[...] += jnp.sum(d_on * (ov * r), axis=0, keepdims=True)
            u = d_on * gnv
            do = r * u - ov * (r * r * r) * jnp.mean(u * ov, axis=-1, keepdims=True)
            q_s[hh] = q
            k_s[hh] = k
            c_s[hh] = c
            v_s[hh] = v
            do_s[hh] = do
            st = st_ref[hh, n].astype(F32)
            dst = dst_s[hh]
            ec = jnp.exp(c)
            last = c_s[hh, CH - 1:CH, :]
            el = jnp.exp(last - c)
            elast = jnp.exp(last)
            dq = _dot_nn(do, st) * ec
            dk = _dot_nn(v, dst) * el
            dq_s[hh] = dq
            dk_s[hh] = dk
            dv_s[hh] = _dot_nt(k * el, dst)
            dc_s[hh] = q * dq - k * dk
            dl_s[hh] = (jnp.sum(k * dk, axis=0, keepdims=True)
                        + elast * jnp.sum(st * dst, axis=0, keepdims=True))
            dst_s[hh] = dst * elast + _dot_tn(do, q * ec)

        def pairs_matmul(hh, first, cap, strict):
            q, k, c, v, do = q_s[hh], k_s[hh], c_s[hh], v_s[hh], do_s[hh]
            d_a = _dot_nt(do, v).astype(BF16).astype(F32)
            d_at = d_a.T
            at = jnp.zeros((CH, CH), F32)
            dq, dk, dcum = dq_s[hh], dk_s[hh], dc_s[hh]
            for i in range(first, CH // SB):
                r_i = _block_ref(c_s.at[hh], i)
                eq = jnp.exp(jnp.minimum(c - r_i, 0.0))
                ek = jnp.exp(jnp.minimum(r_i - c, cap))
                qi = (q * eq).astype(BF16).astype(F32)
                kei = (k * ek).astype(BF16).astype(F32)
                in_t = (rowi2 >= SB * i) & (rowi2 < SB * (i + 1))
                in_s = (coli2 >= SB * i) & (coli2 < SB * (i + 1))
                m_ts = in_t & ((coli2 < SB * i) if strict else (coli2 <= rowi2))
                m_st = in_s & ((rowi2 < SB * i) if strict else (rowi2 <= coli2))
                at = at + jnp.where(m_st, _dot_nt(kei, qi), 0.0)
                dq_i = _dot_nn(jnp.where(m_ts, d_a, 0.0), kei)
                dk_i = _dot_nn(jnp.where(m_st, d_at, 0.0), qi)
                dq = dq + dq_i * eq
                dk = dk + dk_i * ek
                dcum = dcum + (qi * dq_i - kei * dk_i)
            dq_s[hh] = dq
            dk_s[hh] = dk
            dc_s[hh] = dcum
            dv_s[hh] += _dot_nn(at, do)

        def pairs_exact(hh):
            dqd_s[hh] = jnp.zeros((CH, HD), F32)
            dkd_s[hh] = jnp.zeros((CH, HD), F32)
            for i in range(CH // SB):
                blk = slice(SB * i, SB * (i + 1))
                qb = q_s[hh, blk, :]
                cb = c_s[hh, blk, :]
                dob = do_s[hh, blk, :]
                dq_acc = jnp.zeros((SB, HD), F32)
                for s in range(SB):
                    row = SB * i + s
                    ks = k_s[hh, row:row + 1, :]
                    vs = v_s[hh, row:row + 1, :]
                    w = jnp.exp(jnp.minimum(cb - c_s[hh, row:row + 1, :], 0.0))
                    live = sbi >= s
                    a_col = jnp.where(live, jnp.sum(qb * ks * w, axis=-1, keepdims=True), 0.0)
                    da_col = jnp.where(live, jnp.sum(dob * vs, axis=-1, keepdims=True), 0.0)
                    dq_acc = dq_acc + da_col * ks * w
                    dkd_s[hh, row:row + 1, :] += jnp.sum(da_col * qb * w, axis=0, keepdims=True)
                    dv_s[hh, row:row + 1, :] += jnp.sum(a_col * dob, axis=0, keepdims=True)
                dqd_s[hh, blk, :] += dq_acc
            dq_d = dqd_s[hh]
            dk_d = dkd_s[hh]
            dq_s[hh] += dq_d
            dk_s[hh] += dk_d
            dc_s[hh] += q_s[hh] * dq_d - k_s[hh] * dk_d

        def gate_grads(hh, rows):
            lanes = slice(hh * HD, (hh + 1) * HD)
            lbv = lb_ref[:, lanes]
            hq = hq_ref[rows, lanes].astype(F32)
            f, sf, sq = f_s[hh], sf_s[hh], sq_s[hh]
            dlogf = _chunk_rev_cumsum(dc_s[hh], rowi) + dl_s[hh]
            dfv = jnp.where(f > 1e-30, dlogf / jnp.maximum(f, 1e-30), 0.0) - dk_s[hh]
            dlb_s[:, lanes] += jnp.sum(dfv * (1.0 - sf), axis=0, keepdims=True)
            df_ref[rows, lanes] = (dfv * (1.0 - lbv) * sf * (1.0 - sf)).astype(BF16)
            dq_ref[rows, lanes] = (dq_s[hh] * _dsilu(hq, sq)).astype(BF16)
            di_ref[rows, lanes] = dv_s[hh].astype(BF16)

        def chunk_with(pairs):
            def chunk(j, carry):
                n = nch - 1 - j
                rows = pl.ds(pl.multiple_of(n * CH, CH), CH)
                for hh in range(NH):
                    between_chunks(hh, n, rows)
                for hh in range(NH):
                    pairs(hh)
                for hh in range(NH):
                    gate_grads(hh, rows)
                return carry
            return chunk

        def pairs_mild(hh):
            pairs_matmul(hh, 0, DECAY_CAP, strict=False)

        def pairs_any(hh):
            pairs_matmul(hh, 1, 0.0, strict=True)
            pairs_exact(hh)

        mild = jnp.max(mild_ref[...]) > 0.5

        @pl.when(mild)
        def _():
            lax.fori_loop(0, nch, chunk_with(pairs_mild), 0, unroll=4)

        @pl.when(jnp.logical_not(mild))
        def _():
            lax.fori_loop(0, nch, chunk_with(pairs_any), 0)

        dlb_ref[...] = dlb_s[...]
        dgn_ref[...] = jnp.broadcast_to(dgn_s[...], (8, HD))

    ahead = lambda h, s: jnp.minimum(h + jnp.minimum(s, 1), HEADS // NH - 1)
    col = lambda off: pl.BlockSpec((S, W), lambda h, s: (0, off // NH + ahead(h, s)))
    head_in = pl.BlockSpec((S, W), lambda h, s: (0, ahead(h, s)))
    vec_in = pl.BlockSpec((1, W), lambda h, s: (0, ahead(h, s)))
    vec = pl.BlockSpec((1, W), lambda h, s: (0, h))
    seg_w = (HF0 - HQ0) // NH
    outs = _call(
        body, name=name, grid=(HEADS // NH, 4),
        in_specs=[head_in, col(HQ0), col(HF0), col(HI0), col(HG0), head_in,
                  pl.BlockSpec((NH, nch, HD, HD), lambda h, s: (ahead(h, s), 0, 0, 0)),
                  pl.BlockSpec((8, HD), lambda h, s: (ahead(h, s), 0)), head_in, vec_in,
                  pl.BlockSpec((1, HD), lambda h, s: (0, 0)), ANY],
        out_specs=[pl.BlockSpec((S, W), lambda h, s: (0, HQ0 // NH + seg_w * s + h)), vec,
                   pl.BlockSpec((8, HD), lambda h, s: (h, 0))],
        out_shape=[jax.ShapeDtypeStruct(dproj.shape, dproj.dtype), jax.ShapeDtypeStruct((1, D), F32),
                   jax.ShapeDtypeStruct((8 * HEADS // NH, HD), F32)],
        scratch_shapes=[pltpu.VMEM((NH, CH, HD), F32)] * 14
        + [pltpu.VMEM((NH, 1, HD), F32), pltpu.VMEM((NH, HD, HD), F32), pltpu.VMEM((1, W), F32),
           pltpu.VMEM((1, HD), F32)] + [pltpu.VMEM((S, W), BF16)] * 3,
        sem=("arbitrary", "arbitrary"), aliases={n_in - 1: 0},
        args=(dbin, proj, proj, proj, proj, oraw, states, mild, cum, lb, gn, dproj), carry=carry)
    dproj, dlb, dgn = outs[:3]
    return (dproj, dlb, dgn.reshape(HEADS // NH, 8, HD)[:, 0, :]), outs[3:]


def _lower_bounds(l0, l1):
    m = jnp.maximum(l0, l1)
    e0 = jnp.exp(l0 - m)
    e1 = jnp.exp(l1 - m)
    tot = e0 + e1
    p0 = e0 / tot
    p1 = e1 / tot
    return jnp.clip(p0 - p0, 0.0, 1.0), jnp.clip((p0 + p1) - p0, 0.0, 1.0)


def _lb_fwd(logits):
    def body(l_ref, o_ref):
        lb0, lb1 = _lower_bounds(l_ref[0:1, :], l_ref[1:2, :])
        o_ref[0:1, :] = lb0
        o_ref[1:2, :] = lb1

    return pl.pallas_call(body, name="lb_fwd", out_shape=jax.ShapeDtypeStruct((2, D), F32))(logits)


def _lb_bwd(logits, dlb):
    def body(l_ref, d_ref, o_ref):
        _, vjp = jax.vjp(_lower_bounds, l_ref[0:1, :], l_ref[1:2, :])
        g0, g1 = vjp((d_ref[0:1, :], d_ref[1:2, :]))
        o_ref[0:1, :] = g0
        o_ref[1:2, :] = g1

    return pl.pallas_call(body, name="lb_bwd", out_shape=jax.ShapeDtypeStruct((2, D), F32))(logits, dlb)


ADA_PAD = 128


def _ada_fwd(c_pad, w_ada, b_sh):
    ns = w_ada.shape[2]

    def body(c_ref, w_ref, b_ref, o_ref):
        cv = c_ref[...]
        ca = (cv * _sig(cv)).astype(BF16)
        for l in range(2):
            res = jnp.dot(ca, w_ref[l].astype(BF16), preferred_element_type=F32)
            o_ref[:, l * ns:(l + 1) * ns] = res[0:NDEV, :] + b_ref[l:l + 1, :]

    return pl.pallas_call(body, name="ada_fwd", out_shape=jax.ShapeDtypeStruct((NDEV, 2 * ns), F32),
                          compiler_params=_cp())(c_pad, w_ada, b_sh)


def _ada_wgrad(c_pad_t, d_ada_sh):
    ns = d_ada_sh.shape[2]

    def body(c_ref, d_ref, o_ref):
        cv = c_ref[...]
        ca = (cv * _sig(cv)).astype(BF16)
        for l in range(2):
            o_ref[l] = jnp.dot(ca, d_ref[l].astype(BF16), preferred_element_type=F32)

    return pl.pallas_call(body, name="ada_wgrad", out_shape=jax.ShapeDtypeStruct((2, D, ns), F32),
                          compiler_params=_cp())(c_pad_t, d_ada_sh)


def _sum_devices(g):
    _, R, C = g.shape

    def body(g_ref, o_ref):
        acc = g_ref[0]
        for d in range(1, NDEV):
            acc = acc + g_ref[d]
        o_ref[...] = acc

    return pl.pallas_call(body, name="sum_devices", out_shape=jax.ShapeDtypeStruct((R, C), F32),
                          compiler_params=_cp())(g)


def _adamw(w, g, m, v, name, carry=None):
    R, C = w.shape
    tr = _row_tile(R, max(8, (1 << 19) // C))

    def body(w_ref, g_ref, m_ref, v_ref, d_ref, nm_ref, nv_ref):
        d_ref[...], nm_ref[...], nv_ref[...] = _adamw_update(w_ref[...], g_ref[...], m_ref[...], v_ref[...])

    tile = pl.BlockSpec((tr, C), lambda i: (i, 0))
    return _call(body, name=name, grid=(R // tr,), in_specs=[tile] * 4, out_specs=[tile] * 3,
                 out_shape=[jax.ShapeDtypeStruct((R, C), F32)] * 3, sem=("parallel",), args=(w, g, m, v),
                 carry=carry)


def _adamw_update(w, g, m, v):
    nm = B1 * m + (1.0 - B1) * g
    nv = B2 * v + (1.0 - B2) * (g * g)
    m_hat = nm / (1.0 - B1 ** STEP)
    v_hat = nv / (1.0 - B2 ** STEP)
    return -LR * (m_hat / (jnp.sqrt(v_hat) + AEPS) + WD * w), nm, nv


SMALL_PARTS = (("b_ada", 0, 6, D), ("g_pre", 8, 2, D), ("g_post", 16, 2, D), ("lb_logits", 24, 2, D),
               ("pool_w", 32, 128, D), ("pool_scale", 160, 1, D), ("hgrn_norm_g", 168, 1, 2 * HD))


def _adamw_small(g_small, g_lb_logits, wmv):
    n = len(SMALL_PARTS)

    def body(g_ref, glb_ref, *refs):
        ins, outs = refs[:3 * n], refs[3 * n:]
        for p, (key, row0, rows, width) in enumerate(SMALL_PARTS):
            gv = glb_ref[...] if key == "lb_logits" else g_ref[row0:row0 + rows, 0:width]
            res = _adamw_update(ins[3 * p][...], gv, ins[3 * p + 1][...], ins[3 * p + 2][...])
            for t in range(3):
                outs[3 * p + t][...] = res[t]

    flat = [t for triple in wmv for t in triple]
    outs = pl.pallas_call(body, name="adamw_small",
                          out_shape=[jax.ShapeDtypeStruct(t.shape, F32) for t in flat],
                          compiler_params=_cp())(g_small, g_lb_logits, *flat)
    return [outs[3 * p:3 * p + 3] for p in range(n)]


def _cast_to_slot(place, w, l, name):
    _, R, C = w.shape
    tr = _row_tile(R, max(8, (1 << 19) // C))

    def body(p_ref, w_ref, o_ref):
        o_ref[...] = w_ref[...].astype(BF16)

    return pl.pallas_call(
        body, name=name, out_shape=jax.ShapeDtypeStruct((NCHIP, R, C), BF16),
        grid_spec=pltpu.PrefetchScalarGridSpec(
            num_scalar_prefetch=1, grid=(R // tr,),
            in_specs=[pl.BlockSpec((None, tr, C), lambda i, p_ref: (l, i, 0))],
            out_specs=pl.BlockSpec((None, tr, C), lambda i, p_ref: (p_ref[0], i, 0))),
        compiler_params=_cp(("parallel",)),
    )(place, w)


def _pair_add(core, g, got, name):
    _, R, C = g.shape
    r2 = R // 2
    tr = _row_tile(r2, max(8, (1 << 19) // C))
    nt = r2 // tr

    def body(c_ref, a_ref, b_ref, o_ref):
        o_ref[...] = (a_ref[...].astype(F32) + b_ref[...].astype(F32)).astype(o_ref.dtype)

    return pl.pallas_call(
        body, name=name, out_shape=jax.ShapeDtypeStruct((NCHIP, r2, C), BF16),
        grid_spec=pltpu.PrefetchScalarGridSpec(
            num_scalar_prefetch=1, grid=(NCHIP, nt),
            in_specs=[pl.BlockSpec((None, tr, C), lambda j, i, c_ref: (j, c_ref[0] * nt + i, 0)),
                      pl.BlockSpec((None, tr, C), lambda j, i, c_ref: (j, i, 0))],
            out_specs=pl.BlockSpec((None, tr, C), lambda j, i, c_ref: (j, i, 0))),
        compiler_params=_cp(("parallel", "parallel")),
    )(core, g, got)


def _chip_sum(place, part, recv, layer, both, name):
    _, r2, C = part.shape
    tr = _row_tile(r2, max(8, (1 << 18) // C))
    nt = r2 // tr

    def body(p_ref, own_ref, r_ref, *rest):
        o_ref = rest[-1]
        me = p_ref[0]
        own = own_ref[...].astype(F32)
        acc = None
        for j in range(NCHIP):
            slot = jnp.minimum(jnp.where(j > me, j - 1, j), NCHIP - 2)
            term = jnp.where(me == j, own, r_ref[slot].astype(F32))
            acc = term if acc is None else acc + term
        o_ref[...] = acc

    args = (place, part, recv) if both is None else (place, part, recv, both)
    return pl.pallas_call(
        body, name=name, out_shape=jax.ShapeDtypeStruct((2, 2 * r2, C), F32),
        grid_spec=pltpu.PrefetchScalarGridSpec(
            num_scalar_prefetch=1, grid=(nt,),
            in_specs=[pl.BlockSpec((None, tr, C), lambda i, p_ref: (p_ref[0], i, 0)),
                      pl.BlockSpec((NCHIP - 1, tr, C), lambda i, p_ref: (0, i, 0))] + [ANY] * (len(args) - 3),
            out_specs=pl.BlockSpec((None, tr, C), lambda i, p_ref: (layer, p_ref[1] * nt + i, 0))),
        input_output_aliases={} if both is None else {3: 0},
        compiler_params=_cp(("parallel",)),
    )(*args)


def _place():
    x, y, c = lax.axis_index("x"), lax.axis_index("y"), lax.axis_index("c")
    chips = [(1 - x, y), (x, 1 - y), (1 - x, 1 - y)]
    return x, y, c, chips


def _gather_small(blk, name):
    m_per, n = blk.shape

    def body(x_ref, out_ref, send_sems, recv_sems, local_sem):
        x, y, c, chips = _place()
        me, sibling = (x, y, c), (x, y, 1 - c)

        def rows(px, py, pc):
            return out_ref.at[pl.ds((4 * px + 2 * py + pc) * m_per, m_per), :]

        def copy(k, block, to, src=None):
            return pltpu.make_async_remote_copy(
                src_ref=rows(*block) if src is None else src, dst_ref=rows(*block),
                send_sem=send_sems.at[k], recv_sem=recv_sems.at[k], device_id=to, device_id_type=MESH)

        mine = pltpu.make_async_copy(x_ref, rows(*me), local_sem)
        mine.start()
        first = [copy(0, me, sibling, src=x_ref)]
        first += [copy(1 + j, me, (*chip, c), src=x_ref) for j, chip in enumerate(chips)]
        for cp in first:
            cp.start()
        passed = [copy(4 + j, (*chip, c), sibling) for j, chip in enumerate(chips)]
        for j, chip in enumerate(chips):
            copy(1 + j, (*chip, c), me).wait_recv()
            passed[j].start()
        copy(0, sibling, me).wait_recv()
        for j, chip in enumerate(chips):
            copy(4 + j, (*chip, 1 - c), me).wait_recv()
        for cp in first + passed:
            cp.wait_send()
        mine.wait()

    return pl.pallas_call(
        body, name=name, out_shape=jax.ShapeDtypeStruct((NDEV * m_per, n), blk.dtype),
        in_specs=[pl.BlockSpec(memory_space=pltpu.VMEM)], out_specs=pl.BlockSpec(memory_space=pltpu.VMEM),
        scratch_shapes=[pltpu.SemaphoreType.DMA((7,)), pltpu.SemaphoreType.DMA((7,)), pltpu.SemaphoreType.DMA],
        compiler_params=_cp(),
    )(blk)


def _gather_rows_carry(blk):
    m_per, n = blk.shape

    def rows(ref, px, py, pc):
        return ref.at[pl.ds((4 * px + 2 * py + pc) * m_per, m_per), :]

    def copy(ins, outs, send_sems, recv_sems, k, block, to, own=False):
        return pltpu.make_async_remote_copy(
            src_ref=ins[0] if own else rows(outs[0], *block), dst_ref=rows(outs[0], *block),
            send_sem=send_sems.at[k], recv_sem=recv_sems.at[k], device_id=to, device_id_type=MESH)

    def mine(ins, outs, send_sems):
        x, y, c, _ = _place()
        return pltpu.make_async_copy(ins[0], rows(outs[0], x, y, c), send_sems.at[7])

    def start(ins, outs, send_sems, recv_sems):
        x, y, c, chips = _place()
        mine(ins, outs, send_sems).start()
        copy(ins, outs, send_sems, recv_sems, 0, (x, y, c), (x, y, 1 - c), own=True).start()
        for j, chip in enumerate(chips):
            copy(ins, outs, send_sems, recv_sems, 1 + j, (x, y, c), (*chip, c), own=True).start()

    def finish(ins, outs, send_sems, recv_sems):
        x, y, c, chips = _place()
        for j, chip in enumerate(chips):
            copy(ins, outs, send_sems, recv_sems, 1 + j, (*chip, c), (x, y, c)).wait_recv()
            copy(ins, outs, send_sems, recv_sems, 4 + j, (*chip, c), (x, y, 1 - c)).start()
        copy(ins, outs, send_sems, recv_sems, 0, (x, y, 1 - c), (x, y, c)).wait_recv()
        for j, chip in enumerate(chips):
            copy(ins, outs, send_sems, recv_sems, 4 + j, (*chip, 1 - c), (x, y, c)).wait_recv()
        copy(ins, outs, send_sems, recv_sems, 0, (x, y, c), (x, y, 1 - c), own=True).wait_send()
        for j, chip in enumerate(chips):
            copy(ins, outs, send_sems, recv_sems, 1 + j, (x, y, c), (*chip, c), own=True).wait_send()
            copy(ins, outs, send_sems, recv_sems, 4 + j, (*chip, c), (x, y, 1 - c)).wait_send()
        mine(ins, outs, send_sems).wait()

    return _Carry([blk], [jax.ShapeDtypeStruct((NDEV * m_per, n), blk.dtype)], {}, 8, start, finish)


def _gather_carry(shards, piece=(0, 1, 1)):
    n = len(shards)
    first, count, of = piece

    def rows(ref, half):
        r2 = ref.shape[1] // 2
        return pl.ds(half * r2 + first * (r2 // of), count * (r2 // of))

    def over_ici(outs, send_sems, recv_sems, a, j, chip_xy, slot):
        x, y, c, _ = _place()
        blk = outs[a].at[slot, rows(outs[a], c), :]
        return pltpu.make_async_remote_copy(
            src_ref=blk, dst_ref=blk, send_sem=send_sems.at[6 * a + j], recv_sem=recv_sems.at[6 * a + j],
            device_id=(*chip_xy, c), device_id_type=MESH)

    def over_d2d(outs, send_sems, recv_sems, a, j, slot, half):
        x, y, c, _ = _place()
        blk = outs[a].at[slot, rows(outs[a], half), :]
        return pltpu.make_async_remote_copy(
            src_ref=blk, dst_ref=blk, send_sem=send_sems.at[6 * a + 3 + j], recv_sem=recv_sems.at[6 * a + 3 + j],
            device_id=(x, y, 1 - c), device_id_type=MESH)

    def start(ins, outs, send_sems, recv_sems):
        x, y, c, chips = _place()
        for a in range(n):
            for j, chip_xy in enumerate(chips):
                over_ici(outs, send_sems, recv_sems, a, j, chip_xy, 2 * x + y).start()

    def finish(ins, outs, send_sems, recv_sems):
        x, y, c, chips = _place()
        for a in range(n):
            for j, (cx, cy) in enumerate(chips):
                over_ici(outs, send_sems, recv_sems, a, j, (cx, cy), 2 * cx + cy).wait_recv()
                over_d2d(outs, send_sems, recv_sems, a, j, 2 * cx + cy, c).start()
        for a in range(n):
            for j, (cx, cy) in enumerate(chips):
                over_d2d(outs, send_sems, recv_sems, a, j, 2 * cx + cy, 1 - c).wait_recv()
        for a in range(n):
            for j, (cx, cy) in enumerate(chips):
                over_ici(outs, send_sems, recv_sems, a, j, (cx, cy), 2 * x + y).wait_send()
                over_d2d(outs, send_sems, recv_sems, a, j, 2 * cx + cy, c).wait_send()

    return _Carry(shards, [jax.ShapeDtypeStruct(s.shape, s.dtype) for s in shards],
                  {a: a for a in range(n)}, 6 * n, start, finish)


def _rs_pair(grads, name):
    n = len(grads)

    def body(*refs):
        ins, gots = refs[:n], refs[n:2 * n]
        send_sems, recv_sems = refs[2 * n:]
        x, y, c, _ = _place()
        cps = []
        for a in range(n):
            r2 = ins[a].shape[1] // 2
            cp = pltpu.make_async_remote_copy(
                src_ref=ins[a].at[:, pl.ds((1 - c) * r2, r2), :], dst_ref=gots[a],
                send_sem=send_sems.at[a], recv_sem=recv_sems.at[a],
                device_id=(x, y, 1 - c), device_id_type=MESH)
            cp.start()
            cps.append(cp)
        for cp in cps:
            cp.wait()

    half = [jax.ShapeDtypeStruct((NCHIP, g.shape[1] // 2, g.shape[2]), g.dtype) for g in grads]
    return pl.pallas_call(
        body, name=name, out_shape=half, in_specs=[ANY] * n, out_specs=[ANY] * n,
        scratch_shapes=[pltpu.SemaphoreType.DMA((n,)), pltpu.SemaphoreType.DMA((n,))],
        compiler_params=_cp(),
    )(*grads)


def _chips_carry(parts, piece=(0, 1, 1), into=None):
    n = len(parts)
    first, count, of = piece

    def rows(ref):
        step = ref.shape[1] // of
        return pl.ds(first * step, count * step)

    def send(ins, outs, send_sems, recv_sems, a, j, chip_xy):
        x, y, c, _ = _place()
        me, them = 2 * x + y, 2 * chip_xy[0] + chip_xy[1]
        return pltpu.make_async_remote_copy(
            src_ref=ins[a].at[them, rows(ins[a]), :],
            dst_ref=outs[a].at[me - (me > them).astype(jnp.int32), rows(outs[a]), :],
            send_sem=send_sems.at[3 * a + j], recv_sem=recv_sems.at[3 * a + j],
            device_id=(*chip_xy, c), device_id_type=MESH)

    def start(ins, outs, send_sems, recv_sems):
        _, _, _, chips = _place()
        for a in range(n):
            for j, chip_xy in enumerate(chips):
                send(ins, outs, send_sems, recv_sems, a, j, chip_xy).start()

    def finish(ins, outs, send_sems, recv_sems):
        x, y, c, chips = _place()
        me = 2 * x + y
        for a in range(n):
            for j, (cx, cy) in enumerate(chips):
                them = 2 * cx + cy
                blk = outs[a].at[them - (them > me).astype(jnp.int32), rows(outs[a]), :]
                pltpu.make_async_remote_copy(
                    src_ref=blk, dst_ref=blk, send_sem=send_sems.at[3 * a + j], recv_sem=recv_sems.at[3 * a + j],
                    device_id=(cx, cy, c), device_id_type=MESH).wait_recv()
        for a in range(n):
            for j, chip_xy in enumerate(chips):
                send(ins, outs, send_sems, recv_sems, a, j, chip_xy).wait_send()

    landing = [jax.ShapeDtypeStruct((NCHIP - 1,) + p.shape[1:], p.dtype) for p in parts]
    if into is None:
        return _Carry(parts, landing, {}, 3 * n, start, finish)
    return _Carry(list(parts) + list(into), landing, {n + a: a for a in range(n)}, 3 * n, start, finish)


def _rs_swap(fulls):
    n = len(fulls)

    def body(*refs):
        outs = refs[n:2 * n]
        send_sems, recv_sems = refs[2 * n:]
        x, y, c, _ = _place()
        cps = []
        for a in range(n):
            r2 = outs[a].shape[1] // 2
            mine = outs[a].at[:, pl.ds(c * r2, r2), :]
            cp = pltpu.make_async_remote_copy(
                src_ref=mine, dst_ref=mine, send_sem=send_sems.at[a], recv_sem=recv_sems.at[a],
                device_id=(x, y, 1 - c), device_id_type=MESH)
            cp.start()
            cps.append(cp)
        for a in range(n):
            r2 = outs[a].shape[1] // 2
            blk = outs[a].at[:, pl.ds((1 - c) * r2, r2), :]
            pltpu.make_async_remote_copy(
                src_ref=blk, dst_ref=blk, send_sem=send_sems.at[a], recv_sem=recv_sems.at[a],
                device_id=(x, y, 1 - c), device_id_type=MESH).wait_recv()
        for cp in cps:
            cp.wait_send()

    return pl.pallas_call(
        body, name="rs_swap", out_shape=[jax.ShapeDtypeStruct(f.shape, f.dtype) for f in fulls],
        in_specs=[ANY] * n, out_specs=[ANY] * n, input_output_aliases={a: a for a in range(n)},
        scratch_shapes=[pltpu.SemaphoreType.DMA((n,)), pltpu.SemaphoreType.DMA((n,))],
        compiler_params=_cp(),
    )(*fulls)


def _tail_weight_grads(merged_t, b_in_t, a_in_t, dy, dbr_b, dbr_a, name, tn=256):
    S = dy.shape[0]
    nn = D // tn

    def body(mt_ref, bt_ref, at_ref, dy_ref, db_ref, da_ref, go_ref, gh_ref, gp_ref):
        go_ref[...] = jnp.dot(mt_ref[...], dy_ref[...], preferred_element_type=F32).astype(BF16)
        gh_ref[...] = jnp.dot(bt_ref[...], db_ref[...], preferred_element_type=F32).astype(BF16)
        gp_ref[...] = jnp.dot(at_ref[...], da_ref[...], preferred_element_type=F32).astype(BF16)

    left = lambda rows: pl.BlockSpec((rows, S), lambda n: (0, 0))
    right = pl.BlockSpec((S, tn), lambda n: (0, n))
    out = pl.BlockSpec((D, tn), lambda n: (0, n))
    return pl.pallas_call(
        body, name=name, grid=(nn,), in_specs=[left(D), left(D), left(POOL_W), right, right, right],
        out_specs=[out, out, pl.BlockSpec((None, POOL_W, tn), lambda n: (n, 0, 0))],
        out_shape=[jax.ShapeDtypeStruct((D, D), BF16), jax.ShapeDtypeStruct((D, D), BF16),
                   jax.ShapeDtypeStruct((NCHIP, POOL_W, D // NCHIP), BF16)],
        compiler_params=_cp(("parallel",)),
    )(merged_t, b_in_t, a_in_t, dy, dbr_b, dbr_a)


class _GatherInProj:
    def __init__(self, slot, order):
        self.slot, self.order = slot, order


def _proj_with_gather(h, w_slot, order, name, tn=256):
    S, K = h.shape
    nsh, _, ns = w_slot.shape
    tps = ns // tn
    nt = nsh * tps
    r2 = K // 2

    def body(ord_ref, h_ref, w_in_ref, o_ref, w_ref, wbuf, tile_sems, send_sems, recv_sems):
        n = pl.program_id(0)
        x, y, c, chips = _place()

        def half(slot, which):
            return w_ref.at[slot, pl.ds(which * r2, r2), :]

        def over_ici(j, slot):
            blk = half(slot, c)
            return pltpu.make_async_remote_copy(src_ref=blk, dst_ref=blk, send_sem=send_sems.at[j],
                                                recv_sem=recv_sems.at[j], device_id=(*chips[j], c),
                                                device_id_type=MESH)

        def over_d2d(j, which):
            blk = half(2 * chips[j][0] + chips[j][1], which)
            return pltpu.make_async_remote_copy(src_ref=blk, dst_ref=blk, send_sem=send_sems.at[3 + j],
                                                recv_sem=recv_sems.at[3 + j], device_id=(x, y, 1 - c),
                                                device_id_type=MESH)

        def tile_copy(step, slot):
            shard = ord_ref[step // tps]
            return pltpu.make_async_copy(w_ref.at[shard, :, pl.ds((step % tps) * tn, tn)], wbuf.at[slot],
                                         tile_sems.at[slot])

        @pl.when(n == 0)
        def _():
            for j in range(3):
                over_ici(j, 2 * x + y).start()
            tile_copy(0, 0).start()

        for j in range(3):
            @pl.when(n == (j + 1) * tps - 1)
            def _(j=j):
                over_ici(j, 2 * chips[j][0] + chips[j][1]).wait_recv()
                over_d2d(j, c).start()
                over_d2d(j, 1 - c).wait_recv()

        @pl.when(n + 1 < nt)
        def _():
            tile_copy(n + 1, (n + 1) % 2).start()

        tile_copy(n, n % 2).wait()
        o_ref[...] = jnp.dot(h_ref[...], wbuf[n % 2], preferred_element_type=F32).astype(o_ref.dtype)

        @pl.when(n == nt - 1)
        def _():
            for j in range(3):
                over_ici(j, 2 * x + y).wait_send()
                over_d2d(j, c).wait_send()

    return pl.pallas_call(
        body, name=name,
        out_shape=[jax.ShapeDtypeStruct((S, nsh * ns), BF16), jax.ShapeDtypeStruct(w_slot.shape, w_slot.dtype)],
        grid_spec=pltpu.PrefetchScalarGridSpec(
            num_scalar_prefetch=1, grid=(nt,),
            in_specs=[pl.BlockSpec((S, K), lambda n, o_ref: (0, 0)), ANY],
            out_specs=[pl.BlockSpec((S, tn), lambda n, o_ref: (0, o_ref[n // tps] * tps + n % tps)), ANY],
            scratch_shapes=[pltpu.VMEM((2, K, tn), w_slot.dtype), pltpu.SemaphoreType.DMA((2,)),
                            pltpu.SemaphoreType.DMA((6,)), pltpu.SemaphoreType.DMA((6,))]),
        input_output_aliases={2: 1},
        compiler_params=_cp(("arbitrary",)),
    )(order, h, w_slot)


def _mm_ride(a, b, carry, **kw):
    if carry is None:
        return _mm(a, b, **kw), []
    return _mm(a, b, carry=carry, **kw)


def _layer_fwd(l, x, ada, w, small, ride, target=None):
    shift, scale, gate = ada[:, 0:D], ada[:, D:2 * D], ada[:, 2 * D:3 * D]
    carry, landed = ride("prenorm")
    (h, h_t), outs = _prenorm_fwd(x, small["g_pre"][l], scale, shift, f"prenorm_fwd{l}", carry)
    landed(outs)
    carry, landed = ride("proj")
    if isinstance(carry, _GatherInProj):
        proj, full = _proj_with_gather(h, carry.slot, carry.order, f"proj{l}")
        outs = [full]
    else:
        proj, outs = _mm_ride(h, w["w_in"][l], carry, name=f"proj{l}", b_mode="nn_sh", tm=2048, out_dtype=BF16)
    landed(outs)
    a_in, a_in_t = _pool_fwd(proj, small["pool_w"][l], small["pool_scale"][l], f"pool_fwd{l}")
    carry, landed = ride("hgrn")
    (b_in, b_in_t, o_raw, states, mild, cum), outs = _hgrn_fwd(proj, small["lb"][l], small["hgrn_norm_g"][l],
                                                              f"hgrn_fwd{l}", carry=carry)
    landed(outs)
    carry, landed = ride("tail")
    (br_a, br_b, merged_t, y, *x_new), outs = _layer_tail_fwd(
        proj, a_in, b_in, x, w["w_pool_o"][l], w["w_hgrn_o"][l].reshape(D, D), w["w_out"][l].reshape(D, D),
        gate, small["g_post"][l], f"tail_fwd{l}", target=target, carry=carry)
    landed(outs)
    saved = dict(x=x, h_t=h_t, proj=proj, a_in_t=a_in_t, b_in_t=b_in_t, o_raw=o_raw, states=states, mild=mild,
                 cum=cum,
                 br_a=br_a, br_b=br_b, merged_t=merged_t, y=y, scale=scale, gate=gate)
    return x_new, saved


def _layer_bwd(l, dxn, sv, w, small, ride):
    dy, dbr_a, dbr_b, dproj, da_in, db_in, dgate, dg_post = _layer_head_bwd(
        dxn, sv["y"], sv["proj"], sv["br_a"], sv["br_b"], w["w_pool_o"][l], w["w_hgrn_o"][l].reshape(D, D),
        w["w_out"][l].reshape(D, D), sv["gate"], small["g_post"][l], f"head_bwd{l}")
    gw_out, gw_hgrn_o, gw_pool_o = _tail_weight_grads(sv["merged_t"], sv["b_in_t"], sv["a_in_t"], dy, dbr_b,
                                                      dbr_a, f"gw_tail{l}")
    big = dict(w_pool_o=gw_pool_o, w_hgrn_o=gw_hgrn_o.reshape(NCHIP, D // NCHIP, D),
               w_out=gw_out.reshape(NCHIP, D // NCHIP, D))
    carry, landed = ride["hgrn"](big)
    (dproj, dlb, dgn), outs = _hgrn_bwd(db_in, sv["proj"], sv["o_raw"], sv["states"], sv["mild"], sv["cum"],
                                        small["lb"][l], small["hgrn_norm_g"][l], dproj, f"hgrn_bwd{l}",
                                        carry=carry)
    landed(outs)
    dproj, dpw, dpsc = _pool_bwd(da_in, sv["proj"], small["pool_w"][l], small["pool_scale"][l], dproj,
                                 f"pool_bwd{l}")
    little = dict(dgate=dgate, g_post=dg_post, pool_w=dpw, pool_scale=dpsc, lb=dlb,
                  hgrn_norm_g=jnp.sum(dgn, axis=0, keepdims=True))
    carry, landed = ride["gw_in"](little)
    big["w_in"], outs = _mm_ride(sv["h_t"], dproj, carry, name=f"gw_in{l}", out_shards=NCHIP, out_dtype=BF16)
    landed(outs)
    carry, landed = ride["d_h"](big)
    dh, outs = _mm_ride(dproj, w["w_in"][l], carry, name=f"d_h{l}", b_mode="nt_shk", tn=1024)
    landed(outs)
    carry, landed = ride["prenorm"](big)
    (dx, dshift, dscale, dg_pre), outs = _prenorm_bwd(dh, dxn, sv["x"], small["g_pre"][l], sv["scale"],
                                                      f"prenorm_bwd{l}", carry)
    landed(outs)
    little.update(dshift=dshift, dscale=dscale, g_pre=dg_pre)
    return dx, big, little


SMALL_ROWS = 176


def _rows8(t):
    t = t.reshape(-1, D)
    return jnp.pad(t, ((0, -t.shape[0] % 8), (0, 0)))


def _pack_small_weights(b_ada, g_pre, g_post, lb_logits, pool_w, pool_scale, hgrn_norm_g):
    gn = jnp.pad(hgrn_norm_g.reshape(1, 2 * HD), ((0, 0), (0, D - 2 * HD)))
    return jnp.concatenate([_rows8(b_ada), _rows8(g_pre), _rows8(g_post), _rows8(lb_logits), _rows8(pool_w),
                            _rows8(pool_scale), _rows8(gn)], axis=0)


def _pack_small(parts):
    row_keys = ("dshift", "dscale", "dgate", "g_pre", "g_post", "lb", "pool_scale", "hgrn_norm_g")
    flat = [p[k] for p in parts for k in row_keys] + [p["pool_w"].reshape(GROUPS * 128 * 128 // D, D) for p in parts]
    nk = len(row_keys)

    def body(*refs):
        o_ref = refs[-1]
        o_ref[...] = jnp.zeros((SMALL_ROWS, D), F32)
        for l in range(2):
            dshift, dscale, dgate, g_pre, g_post, lb, pscale, gn = refs[l * nk:(l + 1) * nk]
            for r, ref in enumerate((dshift, dscale, dgate)):
                o_ref[3 * l + r:3 * l + r + 1, :] = ref[...]
            o_ref[8 + l:9 + l, :] = g_pre[...]
            o_ref[16 + l:17 + l, :] = g_post[...]
            o_ref[24 + l:25 + l, :] = lb[...]
            o_ref[160:161, l * POOL_W:(l + 1) * POOL_W] = pscale[...]
            o_ref[168:169, l * HD:(l + 1) * HD] = gn[...]
            pw = refs[2 * nk + l]
            rows = pw.shape[0]
            o_ref[32 + l * rows:32 + (l + 1) * rows, :] = pw[...]

    return pl.pallas_call(body, name="pack_small", out_shape=jax.ShapeDtypeStruct((SMALL_ROWS, D), F32),
                          compiler_params=_cp())(*flat)


def _unpack_small(p):
    return (p[0:6].reshape(2, 3 * D), p[8:10], p[16:18], p[24:26], p[32:160].reshape(2, GROUPS, 128, 128),
            p[160:161].reshape(2, POOL_W), p[168:169, 0:2 * HD].reshape(2, HD))


def kernel(x, c, w_ada, b_ada, g_pre, g_post, w_in, pool_w, pool_scale, lb_logits, hgrn_norm_g, w_pool_o, w_hgrn_o, w_out, loss_target, m_w_ada, m_b_ada, m_g_pre, m_g_post, m_w_in, m_pool_w, m_pool_scale, m_lb_logits, m_hgrn_norm_g, m_w_pool_o, m_w_hgrn_o, m_w_out, v_w_ada, v_b_ada, v_g_pre, v_g_post, v_w_in, v_pool_w, v_pool_scale, v_lb_logits, v_hgrn_norm_g, v_w_pool_o, v_w_hgrn_o, v_w_out):
    ax, ay, ac = lax.axis_index("x"), lax.axis_index("y"), lax.axis_index("c")
    chip = 2 * ax + ay
    dev = 2 * chip + ac
    xe, te = x[0], loss_target[0]
    ada_s = w_ada.shape[2]

    big_names = ("w_in", "w_pool_o", "w_hgrn_o", "w_out")
    big_w = (w_in, w_pool_o, w_hgrn_o, w_out)
    core = jnp.stack([ac]).astype(jnp.int32)
    place = jnp.stack([chip, ac]).astype(jnp.int32)
    slots = {(k, l): _cast_to_slot(place, t, l, f"cast_{k}{l}") for l in range(2) for k, t in zip(big_names, big_w)}
    w = {k: [None, None] for k in big_names}
    def fills(keys):
        def landed(outs):
            for (k, l), o in zip(keys, outs):
                w[k][l] = slots[k, l] = o
        return landed

    rest0 = [(k, 0) for k in big_names[1:]]
    rest1 = [(k, 1) for k in big_names[1:]]
    no_carry = (None, lambda outs: None)
    order = jnp.stack([chip, 2 * (1 - ax) + ay, 2 * ax + (1 - ay), 2 * (1 - ax) + (1 - ay)]).astype(jnp.int32)

    def ride_fwd0(stage):
        if stage == "proj":
            return _GatherInProj(slots["w_in", 0], order), fills([("w_in", 0)])
        if stage == "hgrn":
            return (_join_carries(_gather_carry([slots[t] for t in rest0]),
                                  _gather_carry([slots["w_in", 1]], piece=(0, 2, 4))),
                    fills(rest0 + [("w_in", 1)]))
        if stage == "tail":
            return _gather_carry([slots["w_in", 1]], piece=(2, 1, 4)), fills([("w_in", 1)])
        return no_carry

    def ride_fwd1(stage):
        if stage == "prenorm":
            return _gather_carry([slots["w_in", 1]], piece=(3, 1, 4)), fills([("w_in", 1)])
        if stage == "hgrn":
            return _gather_carry([slots[t] for t in rest1]), fills(rest1)
        return no_carry

    c_all = _gather_small(jnp.broadcast_to(c, (8, D)), "gather_c").reshape(NDEV, 8, D)[:, 0, :]
    c_pad = jnp.pad(c_all, ((0, ADA_PAD - NDEV), (0, 0)))
    b_sh = lax.dynamic_slice(b_ada, (0, chip * ada_s), (2, ada_s))
    ada_cols = _gather_small(_ada_fwd(c_pad, w_ada, b_sh), "gather_ada")
    ada_cols = ada_cols.reshape(NCHIP, 2, NDEV, 2, ada_s)[:, 0]
    ada_all = jnp.transpose(ada_cols, (2, 1, 0, 3)).reshape(2, NDEV, 3 * D)
    ada_me = lax.dynamic_slice(ada_all, (0, dev, 0), (2, 1, 3 * D))

    lbs = _lb_fwd(lb_logits)
    small = dict(g_pre=g_pre[:, None, :], g_post=g_post[:, None, :], pool_w=pool_w,
                 pool_scale=pool_scale[:, None, :], lb=lbs[:, None, :], hgrn_norm_g=hgrn_norm_g[:, None, :])

    (x1,), sv0 = _layer_fwd(0, xe, ada_me[0], w, small, ride_fwd0)
    (dx2, loss_blk), sv1 = _layer_fwd(1, x1, ada_me[1], w, small, ride_fwd1, target=te)

    parts, recv = {}, {}

    def pair_sums(keys, grads, tag):
        got = _rs_pair(grads, f"rs_pair_{tag}")
        for kl, g, o in zip(keys, grads, got):
            parts[kl] = _pair_add(core, g, o, f"rs_add_{kl[0]}{kl[1]}")

    def exchange(keys):
        def landed(outs):
            recv.update(zip(keys, outs))
        return _chips_carry([parts[kl] for kl in keys]), landed

    def early(l):
        return [(k, l) for k in big_names[1:]]

    def ride_hgrn1(big):
        pair_sums(early(1), [big[k] for k in big_names[1:]], "l1_early")
        return exchange(early(1))

    def ride_d_h1(big):
        pair_sums([("w_in", 1)], [big["w_in"]], "l1_w_in")
        return no_carry

    def ride_hgrn0(big):
        pair_sums(early(0), [big[k] for k in big_names[1:]], "l0_early")
        return exchange([("w_in", 1)] + early(0))

    def ride_d_h0(big):
        pair_sums([("w_in", 0)], [big["w_in"]], "l0_w_in")

        def landed(outs):
            (recv["w_in", 0],) = outs
        return _chips_carry([parts["w_in", 0]], piece=(0, 1, 2)), landed

    def ride_prenorm0(big):
        def landed(outs):
            (recv["w_in", 0],) = outs
        return _chips_carry([parts["w_in", 0]], piece=(1, 1, 2), into=[recv["w_in", 0]]), landed

    no_ride = lambda so_far: no_carry
    dx1, big1, little1 = _layer_bwd(1, dx2, sv1, w, small,
                                    dict(hgrn=ride_hgrn1, gw_in=no_ride, d_h=ride_d_h1, prenorm=no_ride))

    gathered = {}
    zero_row = jnp.zeros((1, D), F32)

    def ride_gw_in0(little):
        so_far = dict(little, dshift=zero_row, dscale=zero_row, g_pre=zero_row)

        def landed(outs):
            (gathered["early"],) = outs
        return _gather_rows_carry(_pack_small([so_far, little1])), landed

    dx0, big0, little0 = _layer_bwd(0, dx1, sv0, w, small,
                                    dict(hgrn=ride_hgrn0, gw_in=ride_gw_in0, d_h=ride_d_h0, prenorm=ride_prenorm0))
    loss = lax.psum(loss_blk[0, 0], ("x", "y", "c"))
    late = _rows8(jnp.stack([little0["dshift"], little0["dscale"], little0["g_pre"]]))
    late = _gather_small(late, "gather_small_late").reshape(NDEV, 8, D)
    packed = gathered["early"].reshape(NDEV, SMALL_ROWS, D)
    packed = packed.at[:, 0:2, :].set(late[:, 0:2, :]).at[:, 8:9, :].set(late[:, 2:3, :])
    red = []
    for k in big_names:
        both = _chip_sum(place, parts[k, 1], recv[k, 1], 1, None, f"rs_sum_{k}1")
        red.append(_chip_sum(place, parts[k, 0], recv[k, 0], 0, both, f"rs_sum_{k}0"))
    g_big = dict(zip(big_names, _rs_swap(red)))

    def upd(wt, g, m, v, name, carry=None):
        shp = wt.shape
        two = lambda t: t.reshape(-1, shp[-1])
        res = _adamw(two(wt), two(g), two(m), two(v), name, carry)
        return [t.reshape(shp) for t in res[:3]], res[3:]

    u_w_in, _ = upd(w_in, g_big["w_in"], m_w_in, v_w_in, "adamw_w_in")
    g_small = _sum_devices(packed)
    g_b_ada, g_g_pre, g_g_post, g_lb, g_pool_w, g_pool_scale, g_norm_g = _unpack_small(g_small)
    g_lb_logits = _lb_bwd(lb_logits, g_lb)
    d_ada_all = packed[:, 0:6, :].reshape(NDEV, 2, 3 * D)
    d_ada_sh = lax.dynamic_slice(jnp.transpose(d_ada_all, (1, 0, 2)), (0, 0, chip * ada_s), (2, NDEV, ada_s))
    d_ada_sh = jnp.pad(d_ada_sh, ((0, 0), (0, ADA_PAD - NDEV), (0, 0)))
    g_w_ada = _ada_wgrad(c_pad.T, d_ada_sh)

    u_w_ada, _ = upd(w_ada, g_w_ada, m_w_ada, v_w_ada, "adamw_w_ada")
    u_w_pool_o, _ = upd(w_pool_o, g_big["w_pool_o"], m_w_pool_o, v_w_pool_o, "adamw_w_pool_o")
    u_w_hgrn_o, _ = upd(w_hgrn_o, g_big["w_hgrn_o"], m_w_hgrn_o, v_w_hgrn_o, "adamw_w_hgrn_o")
    u_w_out, _ = upd(w_out, g_big["w_out"], m_w_out, v_w_out, "adamw_w_out")
    small_w = dict(b_ada=(b_ada, m_b_ada, v_b_ada), g_pre=(g_pre, m_g_pre, v_g_pre),
                   g_post=(g_post, m_g_post, v_g_post), lb_logits=(lb_logits, m_lb_logits, v_lb_logits),
                   pool_w=(pool_w, m_pool_w, v_pool_w), pool_scale=(pool_scale, m_pool_scale, v_pool_scale),
                   hgrn_norm_g=(hgrn_norm_g, m_hgrn_norm_g, v_hgrn_norm_g))
    in_rows = [tuple(t.reshape(rows, width) for t in small_w[key]) for key, _, rows, width in SMALL_PARTS]
    u_rows = _adamw_small(g_small, g_lb_logits, in_rows)
    u_small = {key: [t.reshape(small_w[key][0].shape) for t in u_rows[p]]
               for p, (key, _, _, _) in enumerate(SMALL_PARTS)}

    grads_out = (g_w_ada, g_b_ada, g_g_pre, g_g_post, g_big["w_in"], g_pool_w, g_pool_scale, g_lb_logits,
                 g_norm_g, g_big["w_pool_o"], g_big["w_hgrn_o"], g_big["w_out"])

    def ordered(k):
        s = lambda key: u_small[key][k]
        return (u_w_ada[k], s("b_ada"), s("g_pre"), s("g_post"), u_w_in[k], s("pool_w"), s("pool_scale"),
                s("lb_logits"), s("hgrn_norm_g"), u_w_pool_o[k], u_w_hgrn_o[k], u_w_out[k])

    return (loss, dx0[None], *grads_out, *ordered(0), *ordered(1), *ordered(2))
```

```python
import functools

import jax
import jax.numpy as jnp
from jax import lax
from jax.experimental import pallas as pl
from jax.experimental.pallas import tpu as pltpu

F32 = jnp.float32
BF16 = jnp.bfloat16
MESH = pl.DeviceIdType.MESH

D = 1024
HEADS = 8
HD = 128
GROUPS = 4
POOL_W = 512
WINDOWS = (2, 4, 8, 16)
CH = 128
SB = 32
NH = 2
IN_W = 7168
NCHIP = 4
NDEV = 8
EPS = 1e-6
PV0, PG0, HQ0, HF0, HI0, HG0 = 0, 4, 8, 16, 24, 32
MGP_BLK, MGH_BLK = 5, 6

LR, B1, B2, AEPS, WD, STEP = 0.001, 0.9, 0.999, 1e-08, 0.01, 10
VMEM_LIMIT = 56 * 1024 * 1024


def _cp(sem=None, **kw):
    if sem is not None:
        kw["dimension_semantics"] = sem
    return pltpu.CompilerParams(vmem_limit_bytes=VMEM_LIMIT, **kw)


def _sig(z):
    return 1.0 / (1.0 + jnp.exp(-z))


def _dsilu(z, s):
    return s * (1.0 + z * (1.0 - s))


def _row_tile(rows, cap):
    if rows <= cap:
        return rows
    t = 1 << (cap.bit_length() - 1)
    while rows % t:
        t //= 2
    return t


ANY = pl.BlockSpec(memory_space=pl.ANY)


class _Carry:
    def __init__(self, ins, outs, aliases, n_sem, start, finish):
        self.ins, self.outs, self.aliases, self.n_sem = list(ins), list(outs), dict(aliases), n_sem
        self.start, self.finish = start, finish


class _SemWindow:
    def __init__(self, ref, base):
        self._ref, self._base = ref, base

    @property
    def at(self):
        return self

    def __getitem__(self, k):
        return self._ref.at[self._base + k]


def _join_carries(*carries):
    ins, outs, aliases, spans, n_sem = [], [], {}, [], 0
    for cr in carries:
        aliases.update({len(ins) + i: len(outs) + o for i, o in cr.aliases.items()})
        spans.append((len(ins), len(cr.ins), len(outs), len(cr.outs), n_sem))
        ins, outs, n_sem = ins + cr.ins, outs + cr.outs, n_sem + cr.n_sem

    def run(which):
        def fn(i_refs, o_refs, send_sems, recv_sems):
            for cr, (i0, ni, o0, no, s0) in zip(carries, spans):
                getattr(cr, which)(i_refs[i0:i0 + ni], o_refs[o0:o0 + no], _SemWindow(send_sems, s0),
                                   _SemWindow(recv_sems, s0))
        return fn

    return _Carry(ins, outs, aliases, n_sem, run("start"), run("finish"))


def _call(body, *, name, grid, in_specs, out_specs, out_shape, args, scratch_shapes=(), sem=None, carry=None,
          aliases=None):
    in_specs, out_specs, out_shape = list(in_specs), list(out_specs), list(out_shape)
    scratch_shapes = list(scratch_shapes)
    aliases = dict(aliases or {})
    if carry is None:
        outs = pl.pallas_call(body, name=name, grid=grid, in_specs=in_specs, out_specs=out_specs,
                              out_shape=out_shape, scratch_shapes=scratch_shapes, input_output_aliases=aliases,
                              compiler_params=_cp(sem))(*args)
        return list(outs)
    n_in, n_out, n_scr = len(in_specs), len(out_specs), len(scratch_shapes)
    c_in, c_out = len(carry.ins), len(carry.outs)

    def wrapped(*refs):
        k_in, rest = refs[:n_in], refs[n_in:]
        ci, rest = rest[:c_in], rest[c_in:]
        k_out, rest = rest[:n_out], rest[n_out:]
        co, rest = rest[:c_out], rest[c_out:]
        k_scr, (ssem, rsem) = rest[:n_scr], rest[n_scr:]
        pids = [pl.program_id(d) for d in range(len(grid))]
        first = functools.reduce(jnp.logical_and, [p == 0 for p in pids])
        last = functools.reduce(jnp.logical_and, [p == g - 1 for p, g in zip(pids, grid)])

        @pl.when(first)
        def _():
            carry.start(ci, co, ssem, rsem)

        body(*k_in, *k_out, *k_scr)

        @pl.when(last)
        def _():
            carry.finish(ci, co, ssem, rsem)

    outs = pl.pallas_call(
        wrapped, name=name, grid=grid, in_specs=in_specs + [ANY] * c_in, out_specs=out_specs + [ANY] * c_out,
        out_shape=out_shape + carry.outs,
        input_output_aliases={**aliases, **{n_in + i: n_out + o for i, o in carry.aliases.items()}},
        scratch_shapes=scratch_shapes + [pltpu.SemaphoreType.DMA((carry.n_sem,))] * 2,
        compiler_params=_cp(("arbitrary",) * len(grid)),
    )(*args, *carry.ins)
    return list(outs)


def _mm(a, b, *, name, b_mode="nn", out_shards=0, tm=1024, tn=256, tk=None, out_dtype=F32, carry=None):
    M, K = a.shape
    if b_mode == "nn":
        N = b.shape[1]
    elif b_mode == "nt":
        N = b.shape[0]
    elif b_mode == "nn_sh":
        N = b.shape[0] * b.shape[2]
    else:
        N = b.shape[1]
    tm = _row_tile(M, tm)
    if b_mode == "nn_sh":
        tn = _row_tile(b.shape[2], tn)
    elif out_shards:
        tn = _row_tile(N // out_shards, tn)
    else:
        tn = _row_tile(N, tn)
    if tk is None:
        tk = K if b_mode != "nt_shk" else b.shape[2]
    if b_mode == "nt_shk":
        tk = _row_tile(b.shape[2], tk)
    nm, nn, nk = M // tm, N // tn, K // tk

    a_spec = pl.BlockSpec((tm, tk), lambda m, n, k: (m, k))
    if b_mode == "nn":
        b_spec = pl.BlockSpec((tk, tn), lambda m, n, k: (k, n))
    elif b_mode == "nt":
        b_spec = pl.BlockSpec((tn, tk), lambda m, n, k: (n, k))
    elif b_mode == "nn_sh":
        nps = b.shape[2] // tn
        b_spec = pl.BlockSpec((None, tk, tn), lambda m, n, k: (n // nps, k, n % nps))
    else:
        kps = b.shape[2] // tk
        b_spec = pl.BlockSpec((None, tn, tk), lambda m, n, k: (k // kps, n, k % kps))
    if out_shards:
        ops = (N // out_shards) // tn
        o_spec = pl.BlockSpec((None, tm, tn), lambda m, n, k: (n // ops, m, n % ops))
        o_shape = jax.ShapeDtypeStruct((out_shards, M, N // out_shards), out_dtype)
    else:
        o_spec = pl.BlockSpec((tm, tn), lambda m, n, k: (m, n))
        o_shape = jax.ShapeDtypeStruct((M, N), out_dtype)
    trans_b = b_mode in ("nt", "nt_shk")
    dn = (((1,), (1,)), ((), ())) if trans_b else (((1,), (0,)), ((), ()))

    def body(a_ref, b_ref, o_ref, acc_ref):
        k = pl.program_id(2)

        @pl.when(k == 0)
        def _():
            acc_ref[...] = jnp.zeros(acc_ref.shape, F32)

        acc_ref[...] += lax.dot_general(a_ref[...].astype(BF16), b_ref[...].astype(BF16), dn,
                                        preferred_element_type=F32)

        @pl.when(k == nk - 1)
        def _():
            o_ref[...] = acc_ref[...].astype(o_ref.dtype)

    outs = _call(body, name=name, grid=(nm, nn, nk), in_specs=[a_spec, b_spec], out_specs=[o_spec],
                 out_shape=[o_shape], scratch_shapes=[pltpu.VMEM((tm, tn), F32)],
                 sem=("parallel", "parallel", "arbitrary"), args=(a, b), carry=carry)
    return outs[0] if carry is None else (outs[0], outs[1:])


def _rowvec(n=D):
    return pl.BlockSpec((1, n), lambda i: (0, 0))


def _prenorm_fwd(x, g, scale, shift, name, carry=None):
    S = x.shape[0]
    tr = _row_tile(S, 256)

    def body(x_ref, g_ref, sc_ref, sh_ref, h_ref, ht_ref):
        xv = x_ref[...]
        r = lax.rsqrt(jnp.mean(xv * xv, axis=-1, keepdims=True) + EPS)
        hv = (xv * r) * g_ref[...] * (1.0 + sc_ref[...]) + sh_ref[...]
        h_ref[...] = hv.astype(BF16)
        ht_ref[...] = hv.T.astype(BF16)

    outs = _call(
        body, name=name, grid=(S // tr,),
        in_specs=[pl.BlockSpec((tr, D), lambda i: (i, 0)), _rowvec(), _rowvec(), _rowvec()],
        out_specs=[pl.BlockSpec((tr, D), lambda i: (i, 0)), pl.BlockSpec((D, tr), lambda i: (0, i))],
        out_shape=[jax.ShapeDtypeStruct((S, D), BF16), jax.ShapeDtypeStruct((D, S), BF16)],
        sem=("parallel",), args=(x, g, scale, shift), carry=carry)
    return outs[:2], outs[2:]


def _prenorm_bwd(dh, dxn, x, g, scale, name, carry=None):
    S = x.shape[0]
    tr = _row_tile(S, 256)

    def body(dh_ref, dxn_ref, x_ref, g_ref, sc_ref, dx_ref, dsh_ref, dsc_ref, dg_ref):
        i = pl.program_id(0)

        @pl.when(i == 0)
        def _():
            dsh_ref[...] = jnp.zeros((1, D), F32)
            dsc_ref[...] = jnp.zeros((1, D), F32)
            dg_ref[...] = jnp.zeros((1, D), F32)

        xv = x_ref[...]
        dhv = dh_ref[...]
        gv = g_ref[...]
        mod = 1.0 + sc_ref[...]
        r = lax.rsqrt(jnp.mean(xv * xv, axis=-1, keepdims=True) + EPS)
        xh = xv * r
        dsh_ref[...] += jnp.sum(dhv, axis=0, keepdims=True)
        dsc_ref[...] += jnp.sum(dhv * (xh * gv), axis=0, keepdims=True)
        dg_ref[...] += jnp.sum(dhv * mod * xh, axis=0, keepdims=True)
        u = dhv * mod * gv
        dx_ref[...] = dxn_ref[...] + r * u - xv * (r * r * r) * jnp.mean(u * xv, axis=-1, keepdims=True)

    tile = pl.BlockSpec((tr, D), lambda i: (i, 0))
    outs = _call(
        body, name=name, grid=(S // tr,),
        in_specs=[tile, tile, tile, _rowvec(), _rowvec()],
        out_specs=[tile, _rowvec(), _rowvec(), _rowvec()],
        out_shape=[jax.ShapeDtypeStruct((S, D), F32)] + [jax.ShapeDtypeStruct((1, D), F32)] * 3,
        sem=("arbitrary",), args=(dh, dxn, x, g, scale), carry=carry)
    return outs[:4], outs[4:]


def _layer_tail_fwd(proj, a_in, b_in, x, w_po, w_ho, w_out, gate, g, name, target=None, carry=None):
    S = proj.shape[0]
    tr = _row_tile(S, 256)
    nsh, _, wsh = w_po.shape
    n_in = 10 + (target is not None)

    def body(*refs):
        (mgp_ref, mgh_ref, a_ref, b_ref, x_ref, wpo_ref, who_ref, wout_ref, gate_ref, g_ref) = refs[:10]
        bra_ref, brb_ref, mt_ref, y_ref, xn_ref = refs[n_in:n_in + 5]
        av = a_ref[...]
        bra = jnp.concatenate([jnp.dot(av, wpo_ref[j], preferred_element_type=F32) for j in range(nsh)], axis=1)
        brb = jnp.dot(b_ref[...], who_ref[...], preferred_element_type=F32)
        mv = _sig(mgp_ref[...].astype(F32)) * bra + _sig(mgh_ref[...].astype(F32)) * brb
        bra_ref[...] = bra.astype(BF16)
        brb_ref[...] = brb.astype(BF16)
        mt_ref[...] = mv.T.astype(BF16)
        yv = jnp.dot(mv.astype(BF16), wout_ref[...], preferred_element_type=F32)
        y_ref[...] = yv
        r = lax.rsqrt(jnp.mean(yv * yv, axis=-1, keepdims=True) + EPS)
        xn = x_ref[...] + gate_ref[...] * ((yv * r) * g_ref[...])
        if target is None:
            xn_ref[...] = xn
        else:
            t_ref, l_ref = refs[10], refs[n_in + 5]

            @pl.when(pl.program_id(0) == 0)
            def _():
                l_ref[...] = jnp.zeros((8, 128), F32)

            err = xn - t_ref[...]
            xn_ref[...] = err * (1.0 / D)
            l_ref[...] += 0.5 * jnp.sum(jnp.mean(err * err, axis=-1, keepdims=True))

    tile = pl.BlockSpec((tr, D), lambda i: (i, 0))
    whole = lambda t: pl.BlockSpec(t.shape, lambda i: (0,) * t.ndim)
    last = target is not None
    outs = _call(
        body, name=name, grid=(S // tr,),
        in_specs=[pl.BlockSpec((tr, D), lambda i: (i, MGP_BLK)), pl.BlockSpec((tr, D), lambda i: (i, MGH_BLK)),
                  pl.BlockSpec((tr, POOL_W), lambda i: (i, 0)), tile, tile, whole(w_po), whole(w_ho),
                  whole(w_out), _rowvec(), _rowvec()] + [tile] * last,
        out_specs=[tile, tile, pl.BlockSpec((D, tr), lambda i: (0, i)), tile, tile]
        + [pl.BlockSpec((8, 128), lambda i: (0, 0))] * last,
        out_shape=[jax.ShapeDtypeStruct((S, D), BF16), jax.ShapeDtypeStruct((S, D), BF16),
                   jax.ShapeDtypeStruct((D, S), BF16), jax.ShapeDtypeStruct((S, D), F32),
                   jax.ShapeDtypeStruct((S, D), F32)] + [jax.ShapeDtypeStruct((8, 128), F32)] * last,
        sem=("arbitrary",) if last else ("parallel",),
        args=(proj, proj, a_in, b_in, x, w_po, w_ho, w_out, gate, g) + ((target,) if last else ()), carry=carry)
    return outs[:5 + last], outs[5 + last:]


def _layer_head_bwd(dxn, y, proj, br_a, br_b, w_po, w_ho, w_out, gate, g, name):
    S = y.shape[0]
    tr = _row_tile(S, 256)
    nsh, _, wsh = w_po.shape

    def body(dxn_ref, y_ref, mgp_ref, mgh_ref, bra_ref, brb_ref, wpo_ref, who_ref, wout_ref, gate_ref, g_ref,
             dy_ref, dba_ref, dbb_ref, dproj_ref, dain_ref, dbin_ref, dgate_ref, dg_ref, dmgh_s):
        i = pl.program_id(0)
        j = pl.program_id(1)

        @pl.when((i == 0) & (j == 0))
        def _():
            dgate_ref[...] = jnp.zeros((1, D), F32)
            dg_ref[...] = jnp.zeros((1, D), F32)

        @pl.when(j == 1)
        def _():
            dproj_ref[...] = dmgh_s[...]

        @pl.when(j == 0)
        def _():
            everything(dxn_ref, y_ref, mgp_ref, mgh_ref, bra_ref, brb_ref, wpo_ref, who_ref, wout_ref, gate_ref,
                       g_ref, dy_ref, dba_ref, dbb_ref, dproj_ref, dain_ref, dbin_ref, dgate_ref, dg_ref, dmgh_s)

    def everything(dxn_ref, y_ref, mgp_ref, mgh_ref, bra_ref, brb_ref, wpo_ref, who_ref, wout_ref, gate_ref, g_ref,
                   dy_ref, dba_ref, dbb_ref, dproj_ref, dain_ref, dbin_ref, dgate_ref, dg_ref, dmgh_s):
        yv = y_ref[...]
        dv = dxn_ref[...]
        gv = g_ref[...]
        gt = gate_ref[...]
        r = lax.rsqrt(jnp.mean(yv * yv, axis=-1, keepdims=True) + EPS)
        yh = yv * r
        dgate_ref[...] += jnp.sum(dv * (yh * gv), axis=0, keepdims=True)
        dg_ref[...] += jnp.sum(dv * gt * yh, axis=0, keepdims=True)
        u = dv * gt * gv
        dy = (r * u - yv * (r * r * r) * jnp.mean(u * yv, axis=-1, keepdims=True)).astype(BF16)
        dy_ref[...] = dy
        dm = _dot_nt(dy, wout_ref[...])
        sp = _sig(mgp_ref[...].astype(F32))
        sh = _sig(mgh_ref[...].astype(F32))
        dba = (dm * sp).astype(BF16)
        dbb = (dm * sh).astype(BF16)
        dba_ref[...] = dba
        dbb_ref[...] = dbb
        dproj_ref[...] = (dm * bra_ref[...].astype(F32) * sp * (1.0 - sp)).astype(BF16)
        dmgh_s[...] = (dm * brb_ref[...].astype(F32) * sh * (1.0 - sh)).astype(BF16)
        dain = _dot_nt(dba[:, 0:wsh], wpo_ref[0])
        for k in range(1, nsh):
            dain = dain + _dot_nt(dba[:, k * wsh:(k + 1) * wsh], wpo_ref[k])
        dain_ref[...] = dain
        dbin_ref[...] = _dot_nt(dbb, who_ref[...])

    tile = pl.BlockSpec((tr, D), lambda i, j: (i, 0))
    whole = lambda t: pl.BlockSpec(t.shape, lambda i, j: (0,) * t.ndim)
    vec = pl.BlockSpec((1, D), lambda i, j: (0, 0))
    ahead = lambda i, j: jnp.minimum(i + j, S // tr - 1)
    tile_in = pl.BlockSpec((tr, D), lambda i, j: (ahead(i, j), 0))
    return pl.pallas_call(
        body, name=name, grid=(S // tr, 2),
        in_specs=[tile_in, tile_in, pl.BlockSpec((tr, D), lambda i, j: (ahead(i, j), MGP_BLK)),
                  pl.BlockSpec((tr, D), lambda i, j: (ahead(i, j), MGH_BLK)), tile_in, tile_in, whole(w_po),
                  whole(w_ho), whole(w_out), vec, vec],
        out_specs=[tile, tile, tile, pl.BlockSpec((tr, D), lambda i, j: (i, MGP_BLK + j)),
                   pl.BlockSpec((tr, POOL_W), lambda i, j: (i, 0)), tile, vec, vec],
        out_shape=[jax.ShapeDtypeStruct((S, D), BF16)] * 3
        + [jax.ShapeDtypeStruct((S, IN_W), BF16), jax.ShapeDtypeStruct((S, POOL_W), F32),
           jax.ShapeDtypeStruct((S, D), F32), jax.ShapeDtypeStruct((1, D), F32), jax.ShapeDtypeStruct((1, D), F32)],
        scratch_shapes=[pltpu.VMEM((tr, D), BF16)],
        compiler_params=_cp(("arbitrary", "arbitrary")),
    )(dxn, y, proj, proj, br_a, br_b, w_po, w_ho, w_out, gate, g)


def _pool_pieces(u, g, S):
    rowi = lax.broadcasted_iota(jnp.int32, (S, 1), 0)

    def down(z, k):
        return jnp.where(rowi >= k, pltpu.roll(z, k, axis=0), 0.0)

    s2 = u + down(u, 1)
    s4 = s2 + down(s2, 2)
    s8 = s4 + down(s4, 4)
    s16 = s8 + down(s8, 8)
    win = jnp.where(g == 0, s2, jnp.where(g == 1, s4, jnp.where(g == 2, s8, s16)))
    w = jnp.where(g == 0, 2, jnp.where(g == 1, 4, jnp.where(g == 2, 8, 16)))
    count = jnp.minimum(rowi + 1, w).astype(F32)
    return win / count - u, count, rowi


def _pool_fwd(proj, pw, pscale, name):
    S = proj.shape[0]

    def body(pv_ref, pg_ref, pw_ref, sc_ref, a_ref, at_ref):
        g = pl.program_id(0)
        pooled, _, _ = _pool_pieces(pv_ref[...].astype(F32), g, S)
        pm = jnp.dot(pooled.astype(BF16), pw_ref[...].astype(BF16), preferred_element_type=F32)
        pgv = pg_ref[...].astype(F32)
        av = pm * sc_ref[...] * (pgv * _sig(pgv))
        a_ref[...] = av.astype(BF16)
        at_ref[...] = av.T.astype(BF16)

    return pl.pallas_call(
        body, name=name, grid=(GROUPS,),
        in_specs=[pl.BlockSpec((S, 128), lambda g: (0, PV0 + g)), pl.BlockSpec((S, 128), lambda g: (0, PG0 + g)),
                  pl.BlockSpec((None, 128, 128), lambda g: (g, 0, 0)), pl.BlockSpec((1, 128), lambda g: (0, g))],
        out_specs=[pl.BlockSpec((S, 128), lambda g: (0, g)), pl.BlockSpec((128, S), lambda g: (g, 0))],
        out_shape=[jax.ShapeDtypeStruct((S, POOL_W), BF16), jax.ShapeDtypeStruct((POOL_W, S), BF16)],
        compiler_params=_cp(("parallel",)),
    )(proj, proj, pw, pscale)


def _pool_bwd(da, proj, pw, pscale, dproj, name):
    S = proj.shape[0]

    def body(da_ref, pv_ref, pg_ref, pw_ref, sc_ref, dproj_in, dproj_ref, dpw_ref, dsc_ref, dpg_s):
        @pl.when(pl.program_id(1) == 1)
        def _():
            dproj_ref[...] = dpg_s[...]

        @pl.when(pl.program_id(1) == 0)
        def _():
            group(da_ref, pv_ref, pg_ref, pw_ref, sc_ref, dproj_ref, dpg_s, dpw_ref, dsc_ref)

    def group(da_ref, pv_ref, pg_ref, pw_ref, sc_ref, dpv_ref, dpg_ref, dpw_ref, dsc_ref):
        g = pl.program_id(0)
        pooled, count, rowi = _pool_pieces(pv_ref[...].astype(F32), g, S)
        pwb = pw_ref[...].astype(BF16)
        pm = jnp.dot(pooled.astype(BF16), pwb, preferred_element_type=F32)
        scv = sc_ref[...]
        pgv = pg_ref[...].astype(F32)
        sg = _sig(pgv)
        dav = da_ref[...]
        d_ps = dav * (pgv * sg)
        dpg_ref[...] = (dav * (pm * scv) * _dsilu(pgv, sg)).astype(BF16)
        dsc_ref[...] = jnp.sum(d_ps * pm, axis=0, keepdims=True)
        d_pm = (d_ps * scv).astype(BF16)
        dpw_ref[...] = lax.dot_general(pooled.astype(BF16), d_pm, (((0,), (0,)), ((), ())),
                                       preferred_element_type=F32)
        d_pooled = lax.dot_general(d_pm, pwb, (((1,), (1,)), ((), ())), preferred_element_type=F32)
        z = d_pooled / count

        def up(v, k):
            return jnp.where(rowi < S - k, pltpu.roll(v, S - k, axis=0), 0.0)

        t2 = z + up(z, 1)
        t4 = t2 + up(t2, 2)
        t8 = t4 + up(t4, 4)
        t16 = t8 + up(t8, 8)
        adj = jnp.where(g == 0, t2, jnp.where(g == 1, t4, jnp.where(g == 2, t8, t16)))
        dpv_ref[...] = (adj - d_pooled).astype(BF16)

    col = lambda g, j: (0, g)
    ahead = lambda g, j: jnp.minimum(g + j, GROUPS - 1)
    return pl.pallas_call(
        body, name=name, grid=(GROUPS, 2),
        in_specs=[pl.BlockSpec((S, 128), lambda g, j: (0, ahead(g, j))),
                  pl.BlockSpec((S, 128), lambda g, j: (0, PV0 + ahead(g, j))),
                  pl.BlockSpec((S, 128), lambda g, j: (0, PG0 + ahead(g, j))),
                  pl.BlockSpec((None, 128, 128), lambda g, j: (ahead(g, j), 0, 0)),
                  pl.BlockSpec((1, 128), lambda g, j: (0, ahead(g, j))), ANY],
        out_specs=[pl.BlockSpec((S, 128), lambda g, j: (0, PV0 + g + (PG0 - PV0) * j)),
                   pl.BlockSpec((None, 128, 128), lambda g, j: (g, 0, 0)), pl.BlockSpec((1, 128), col)],
        out_shape=[jax.ShapeDtypeStruct(dproj.shape, dproj.dtype),
                   jax.ShapeDtypeStruct((GROUPS, 128, 128), F32), jax.ShapeDtypeStruct((1, POOL_W), F32)],
        scratch_shapes=[pltpu.VMEM((S, 128), BF16)], input_output_aliases={5: 0},
        compiler_params=_cp(("arbitrary", "arbitrary")),
    )(da, proj, proj, pw, pscale, dproj)


SCAN_SHIFTS = tuple(1 << b for b in range(CH.bit_length() - 1))


def _chunk_cumsum(z, rowi):
    for sh in SCAN_SHIFTS:
        z = z + jnp.where(rowi >= sh, pltpu.roll(z, sh, axis=0), 0.0)
    return z


def _chunk_rev_cumsum(z, rowi):
    for sh in SCAN_SHIFTS:
        z = z + jnp.where(rowi < CH - sh, pltpu.roll(z, CH - sh, axis=0), 0.0)
    return z


def _dot_nn(a, b):
    return jnp.dot(a.astype(BF16), b.astype(BF16), preferred_element_type=F32)


def _dot_nt(a, b):
    return lax.dot_general(a.astype(BF16), b.astype(BF16), (((1,), (1,)), ((), ())), preferred_element_type=F32)


def _dot_tn(a, b):
    return lax.dot_general(a.astype(BF16), b.astype(BF16), (((0,), (0,)), ((), ())), preferred_element_type=F32)


def _gates(hq, hf, lbv):
    hq, hf = hq.astype(F32), hf.astype(F32)
    sq = _sig(hq)
    sf = _sig(hf)
    f = lbv + (1.0 - lbv) * sf
    fc = jnp.maximum(f, 1e-30)
    return hq * sq, sq, sf, f, fc, jnp.log(fc)


DECAY_CAP = 60.0


def _block_ref(c_ref, i):
    if i == 0:
        return jnp.zeros((1, HD), F32)
    return c_ref[SB * i - 1:SB * i, :]


def _block_decay(c_ref):
    spans = [_block_ref(c_ref, i) - c_ref[SB * (i + 1) - 1:SB * (i + 1), :] for i in range(CH // SB)]
    return functools.reduce(jnp.maximum, spans)


def _hgrn_fwd(proj, lb, gn, name, carry=None):
    S = proj.shape[0]
    nch = S // CH
    W = NH * HD

    def body(hq_ref, hf_ref, hi_ref, hg_ref, lb_ref, gn_ref, bin_ref, bint_ref, oraw_ref, st_ref, mild_ref,
             cum_ref, q_s, k_s, c_s, v_s, o_s, state_s, qf_s, kf_s, cf_s):
        state_s[...] = jnp.zeros((NH, HD, HD), F32)
        rowi = lax.broadcasted_iota(jnp.int32, (CH, 1), 0)
        coli = lax.broadcasted_iota(jnp.int32, (1, CH), 1)
        sbi = lax.broadcasted_iota(jnp.int32, (SB, 1), 0)
        gnv = gn_ref[...]

        def gates_pass(n, worst):
            rows = pl.ds(pl.multiple_of(n * CH, CH), CH)
            for hh in range(NH):
                lanes = slice(hh * HD, (hh + 1) * HD)
                q, _, _, f, _, logf = _gates(hq_ref[rows, lanes], hf_ref[rows, lanes], lb_ref[:, lanes])
                c = _chunk_cumsum(logf, rowi)
                qf_s[hh, rows, :] = q
                kf_s[hh, rows, :] = 1.0 - f
                cf_s[hh, rows, :] = c
                cum_ref[rows, lanes] = c
                c_s[hh] = c
                worst = jnp.maximum(worst, _block_decay(c_s.at[hh]))
            return worst

        def between_chunks(hh, n, rows):
            lanes = slice(hh * HD, (hh + 1) * HD)
            q = qf_s[hh, rows, :]
            k = kf_s[hh, rows, :]
            c = cf_s[hh, rows, :]
            v = hi_ref[rows, lanes].astype(F32)
            q_s[hh] = q
            k_s[hh] = k
            c_s[hh] = c
            v_s[hh] = v
            st = state_s[hh]
            st_ref[hh, n] = st.astype(BF16)
            o_s[hh] = _dot_nt(q * jnp.exp(c), st)
            last = c_s[hh, CH - 1:CH, :]
            state_s[hh] = st * jnp.exp(last) + _dot_tn(v, k * jnp.exp(last - c))

        def within_chunk_matmul(hh):
            q, k, c, v = q_s[hh], k_s[hh], c_s[hh], v_s[hh]
            a = jnp.zeros((CH, CH), F32)
            for i in range(CH // SB):
                r_i = _block_ref(c_s.at[hh], i)
                qi = q * jnp.exp(jnp.minimum(c - r_i, 0.0))
                kei = k * jnp.exp(jnp.minimum(r_i - c, DECAY_CAP))
                m_i = (rowi >= SB * i) & (rowi < SB * (i + 1)) & (coli <= rowi)
                a = a + jnp.where(m_i, _dot_nt(qi, kei), 0.0)
            o_s[hh] += _dot_nn(a, v)

        def within_chunk_exact(hh):
            q, k, c, v = q_s[hh], k_s[hh], c_s[hh], v_s[hh]
            a_off = jnp.zeros((CH, CH), F32)
            for i in range(1, CH // SB):
                r_i = _block_ref(c_s.at[hh], i)
                qi = q * jnp.exp(jnp.minimum(c - r_i, 0.0))
                kei = k * jnp.exp(jnp.minimum(r_i - c, 0.0))
                m_i = (rowi >= SB * i) & (rowi < SB * (i + 1)) & (coli < SB * i)
                a_off = a_off + jnp.where(m_i, _dot_nt(qi, kei), 0.0)
            o_s[hh] += _dot_nn(a_off, v)
            for i in range(CH // SB):
                blk = slice(SB * i, SB * (i + 1))
                qb = q_s[hh, blk, :]
                cb = c_s[hh, blk, :]
                acc = jnp.zeros((SB, HD), F32)
                for s in range(SB):
                    row = SB * i + s
                    w = jnp.exp(jnp.minimum(cb - c_s[hh, row:row + 1, :], 0.0))
                    a_col = jnp.sum(qb * k_s[hh, row:row + 1, :] * w, axis=-1, keepdims=True)
                    acc = acc + jnp.where(sbi >= s, a_col, 0.0) * v_s[hh, row:row + 1, :]
                o_s[hh, blk, :] += acc

        def norm_and_gate(hh, rows):
            lanes = slice(hh * HD, (hh + 1) * HD)
            ov = o_s[hh]
            oraw_ref[rows, lanes] = ov
            r = lax.rsqrt(jnp.mean(ov * ov, axis=-1, keepdims=True) + EPS)
            hg = hg_ref[rows, lanes].astype(F32)
            bin_ref[rows, lanes] = ((ov * r) * gnv * (hg * _sig(hg))).astype(BF16)

        def chunk_with(within_chunk):
            def chunk(n, carry):
                rows = pl.ds(pl.multiple_of(n * CH, CH), CH)
                for hh in range(NH):
                    between_chunks(hh, n, rows)
                for hh in range(NH):
                    within_chunk(hh)
                for hh in range(NH):
                    norm_and_gate(hh, rows)
                return carry
            return chunk

        worst = lax.fori_loop(0, nch, gates_pass, jnp.zeros((1, HD), F32))
        mild = jnp.max(worst) <= DECAY_CAP
        mild_ref[...] = jnp.broadcast_to(jnp.where(mild, 1.0, 0.0), (8, HD))

        @pl.when(mild)
        def _():
            lax.fori_loop(0, nch, chunk_with(within_chunk_matmul), 0, unroll=4)

        @pl.when(jnp.logical_not(mild))
        def _():
            lax.fori_loop(0, nch, chunk_with(within_chunk_exact), 0)

        bint_ref[...] = bin_ref[...].astype(F32).T.astype(BF16)

    col = lambda off: pl.BlockSpec((S, W), lambda h: (0, off // NH + h))
    head = pl.BlockSpec((S, W), lambda h: (0, h))
    outs = _call(
        body, name=name, grid=(HEADS // NH,),
        in_specs=[col(HQ0), col(HF0), col(HI0), col(HG0), pl.BlockSpec((1, W), lambda h: (0, h)),
                  pl.BlockSpec((1, HD), lambda h: (0, 0))],
        out_specs=[head, pl.BlockSpec((W, S), lambda h: (h, 0)), head,
                   pl.BlockSpec((NH, nch, HD, HD), lambda h: (h, 0, 0, 0)),
                   pl.BlockSpec((8, HD), lambda h: (h, 0)), head],
        out_shape=[jax.ShapeDtypeStruct((S, D), BF16), jax.ShapeDtypeStruct((D, S), BF16),
                   jax.ShapeDtypeStruct((S, D), F32), jax.ShapeDtypeStruct((HEADS, nch, HD, HD), BF16),
                   jax.ShapeDtypeStruct((8 * HEADS // NH, HD), F32), jax.ShapeDtypeStruct((S, D), F32)],
        scratch_shapes=[pltpu.VMEM((NH, CH, HD), F32)] * 5 + [pltpu.VMEM((NH, HD, HD), F32)]
        + [pltpu.VMEM((NH, S, HD), F32)] * 3,
        sem=("parallel",), args=(proj, proj, proj, proj, lb, gn), carry=carry)
    return outs[:6], outs[6:]


def _hgrn_bwd(dbin, proj, oraw, states, mild, cum, lb, gn, dproj, name, carry=None):
    S = proj.shape[0]
    nch = S // CH
    W = NH * HD
    n_in = 12

    def body(*refs):
        ins, (dproj_ref, dlb_ref, dgn_ref) = refs[:n_in - 1], refs[n_in:n_in + 3]
        scratch, later = refs[n_in + 3:-3], refs[-3:]
        seg = pl.program_id(1)

        @pl.when(seg == 0)
        def _():
            heads(*ins, dproj_ref, *later, dlb_ref, dgn_ref, *scratch)

        for s, kept in enumerate(later):
            @pl.when(seg == s + 1)
            def _(kept=kept):
                dproj_ref[...] = kept[...]

    def heads(db_ref, hq_ref, hf_ref, hi_ref, hg_ref, or_ref, st_ref, mild_ref, cum_ref, lb_ref, gn_ref,
              dq_ref, df_ref, di_ref, dg_ref, dlb_ref, dgn_ref,
              q_s, k_s, c_s, v_s, do_s, dq_s, dk_s, dv_s, dc_s, dqd_s, dkd_s, f_s, sf_s, sq_s, dl_s, dst_s,
              dlb_s, dgn_s):
        dst_s[...] = jnp.zeros((NH, HD, HD), F32)
        dlb_s[...] = jnp.zeros((1, W), F32)
        dgn_s[...] = jnp.zeros((1, HD), F32)
        rowi = lax.broadcasted_iota(jnp.int32, (CH, 1), 0)
        rowi2 = lax.broadcasted_iota(jnp.int32, (CH, CH), 0)
        coli2 = lax.broadcasted_iota(jnp.int32, (CH, CH), 1)
        sbi = lax.broadcasted_iota(jnp.int32, (SB, 1), 0)
        gnv = gn_ref[...]
        def between_chunks(hh, n, rows):
            lanes = slice(hh * HD, (hh + 1) * HD)
            lbv = lb_ref[:, lanes]
            hq = hq_ref[rows, lanes].astype(F32)
            sq = _sig(hq)
            sf = _sig(hf_ref[rows, lanes].astype(F32))
            f = lbv + (1.0 - lbv) * sf
            q = hq * sq
            k = 1.0 - f
            f_s[hh] = f
            sf_s[hh] = sf
            sq_s[hh] = sq
            v = hi_ref[rows, lanes].astype(F32)
            c = cum_ref[rows, lanes]
            ov = or_ref[rows, lanes]
            hg = hg_ref[rows, lanes].astype(F32)
            sg = _sig(hg)
            r = lax.rsqrt(jnp.mean(ov * ov, axis=-1, keepdims=True) + EPS)
            dbv = db_ref[rows, lanes]
            d_on = dbv * (hg * sg)
            dg_ref[rows, lanes] = (dbv * ((ov * r) * gnv) * _dsilu(hg, sg)).astype(BF16)
            dgn_s[...] += jnp.sum(d_on * (ov * r), axis=0, keepdims=True)
            u = d_on * gnv
            do = r * u - ov * (r * r * r) * jnp.mean(u * ov, axis=-1, keepdims=True)
            q_s[hh] = q
            k_s[hh] = k
            c_s[hh] = c
            v_s[hh] = v
            do_s[hh] = do
            st = st_ref[hh, n].astype(F32)
            dst = dst_s[hh]
            ec = jnp.exp(c)
            last = c_s[hh, CH - 1:CH, :]
            el = jnp.exp(last - c)
            elast = jnp.exp(last)
            dq = _dot_nn(do, st) * ec
            dk = _dot_nn(v, dst) * el
            dq_s[hh] = dq
            dk_s[hh] = dk
            dv_s[hh] = _dot_nt(k * el, dst)
            dc_s[hh] = q * dq - k * dk
            dl_s[hh] = (jnp.sum(k * dk, axis=0, keepdims=True)
                        + elast * jnp.sum(st * dst, axis=0, keepdims=True))
            dst_s[hh] = dst * elast + _dot_tn(do, q * ec)

        def pairs_matmul(hh, first, cap, strict):
            q, k, c, v, do = q_s[hh], k_s[hh], c_s[hh], v_s[hh], do_s[hh]
            d_a = _dot_nt(do, v).astype(BF16).astype(F32)
            d_at = d_a.T
            at = jnp.zeros((CH, CH), F32)
            dq, dk, dcum = dq_s[hh], dk_s[hh], dc_s[hh]
            for i in range(first, CH // SB):
                r_i = _block_ref(c_s.at[hh], i)
                eq = jnp.exp(jnp.minimum(c - r_i, 0.0))
                ek = jnp.exp(jnp.minimum(r_i - c, cap))
                qi = (q * eq).astype(BF16).astype(F32)
                kei = (k * ek).astype(BF16).astype(F32)
                in_t = (rowi2 >= SB * i) & (rowi2 < SB * (i + 1))
                in_s = (coli2 >= SB * i) & (coli2 < SB * (i + 1))
                m_ts = in_t & ((coli2 < SB * i) if strict else (coli2 <= rowi2))
                m_st = in_s & ((rowi2 < SB * i) if strict else (rowi2 <= coli2))
                at = at + jnp.where(m_st, _dot_nt(kei, qi), 0.0)
                dq_i = _dot_nn(jnp.where(m_ts, d_a, 0.0), kei)
                dk_i = _dot_nn(jnp.where(m_st, d_at, 0.0), qi)
                dq = dq + dq_i * eq
                dk = dk + dk_i * ek
                dcum = dcum + (qi * dq_i - kei * dk_i)
            dq_s[hh] = dq
            dk_s[hh] = dk
            dc_s[hh] = dcum
            dv_s[hh] += _dot_nn(at, do)

        def pairs_exact(hh):
            dqd_s[hh] = jnp.zeros((CH, HD), F32)
            dkd_s[hh] = jnp.zeros((CH, HD), F32)
            for i in range(CH // SB):
                blk = slice(SB * i, SB * (i + 1))
                qb = q_s[hh, blk, :]
                cb = c_s[hh, blk, :]
                dob = do_s[hh, blk, :]
                dq_acc = jnp.zeros((SB, HD), F32)
                for s in range(SB):
                    row = SB * i + s
                    ks = k_s[hh, row:row + 1, :]
                    vs = v_s[hh, row:row + 1, :]
                    w = jnp.exp(jnp.minimum(cb - c_s[hh, row:row + 1, :], 0.0))
                    live = sbi >= s
                    a_col = jnp.where(live, jnp.sum(qb * ks * w, axis=-1, keepdims=True), 0.0)
                    da_col = jnp.where(live, jnp.sum(dob * vs, axis=-1, keepdims=True), 0.0)
                    dq_acc = dq_acc + da_col * ks * w
                    dkd_s[hh, row:row + 1, :] += jnp.sum(da_col * qb * w, axis=0, keepdims=True)
                    dv_s[hh, row:row + 1, :] += jnp.sum(a_col * dob, axis=0, keepdims=True)
                dqd_s[hh, blk, :] += dq_acc
            dq_d = dqd_s[hh]
            dk_d = dkd_s[hh]
            dq_s[hh] += dq_d
            dk_s[hh] += dk_d
            dc_s[hh] += q_s[hh] * dq_d - k_s[hh] * dk_d

        def gate_grads(hh, rows):
            lanes = slice(hh * HD, (hh + 1) * HD)
            lbv = lb_ref[:, lanes]
            hq = hq_ref[rows, lanes].astype(F32)
            f, sf, sq = f_s[hh], sf_s[hh], sq_s[hh]
            dlogf = _chunk_rev_cumsum(dc_s[hh], rowi) + dl_s[hh]
            dfv = jnp.where(f > 1e-30, dlogf / jnp.maximum(f, 1e-30), 0.0) - dk_s[hh]
            dlb_s[:, lanes] += jnp.sum(dfv * (1.0 - sf), axis=0, keepdims=True)
            df_ref[rows, lanes] = (dfv * (1.0 - lbv) * sf * (1.0 - sf)).astype(BF16)
            dq_ref[rows, lanes] = (dq_s[hh] * _dsilu(hq, sq)).astype(BF16)
            di_ref[rows, lanes] = dv_s[hh].astype(BF16)

        def chunk_with(pairs):
            def chunk(j, carry):
                n = nch - 1 - j
                rows = pl.ds(pl.multiple_of(n * CH, CH), CH)
                for hh in range(NH):
                    between_chunks(hh, n, rows)
                for hh in range(NH):
                    pairs(hh)
                for hh in range(NH):
                    gate_grads(hh, rows)
                return carry
            return chunk

        def pairs_mild(hh):
            pairs_matmul(hh, 0, DECAY_CAP, strict=False)

        def pairs_any(hh):
            pairs_matmul(hh, 1, 0.0, strict=True)
            pairs_exact(hh)

        mild = jnp.max(mild_ref[...]) > 0.5

        @pl.when(mild)
        def _():
            lax.fori_loop(0, nch, chunk_with(pairs_mild), 0, unroll=2)

        @pl.when(jnp.logical_not(mild))
        def _():
            lax.fori_loop(0, nch, chunk_with(pairs_any), 0)

        dlb_ref[...] = dlb_s[...]
        dgn_ref[...] = jnp.broadcast_to(dgn_s[...], (8, HD))

    ahead = lambda h, s: jnp.minimum(h + jnp.minimum(s, 1), HEADS // NH - 1)
    col = lambda off: pl.BlockSpec((S, W), lambda h, s: (0, off // NH + ahead(h, s)))
    head_in = pl.BlockSpec((S, W), lambda h, s: (0, ahead(h, s)))
    vec_in = pl.BlockSpec((1, W), lambda h, s: (0, ahead(h, s)))
    vec = pl.BlockSpec((1, W), lambda h, s: (0, h))
    seg_w = (HF0 - HQ0) // NH
    outs = _call(
        body, name=name, grid=(HEADS // NH, 4),
        in_specs=[head_in, col(HQ0), col(HF0), col(HI0), col(HG0), head_in,
                  pl.BlockSpec((NH, nch, HD, HD), lambda h, s: (ahead(h, s), 0, 0, 0)),
                  pl.BlockSpec((8, HD), lambda h, s: (ahead(h, s), 0)), head_in, vec_in,
                  pl.BlockSpec((1, HD), lambda h, s: (0, 0)), ANY],
        out_specs=[pl.BlockSpec((S, W), lambda h, s: (0, HQ0 // NH + seg_w * s + h)), vec,
                   pl.BlockSpec((8, HD), lambda h, s: (h, 0))],
        out_shape=[jax.ShapeDtypeStruct(dproj.shape, dproj.dtype), jax.ShapeDtypeStruct((1, D), F32),
                   jax.ShapeDtypeStruct((8 * HEADS // NH, HD), F32)],
        scratch_shapes=[pltpu.VMEM((NH, CH, HD), F32)] * 14
        + [pltpu.VMEM((NH, 1, HD), F32), pltpu.VMEM((NH, HD, HD), F32), pltpu.VMEM((1, W), F32),
           pltpu.VMEM((1, HD), F32)] + [pltpu.VMEM((S, W), BF16)] * 3,
        sem=("arbitrary", "arbitrary"), aliases={n_in - 1: 0},
        args=(dbin, proj, proj, proj, proj, oraw, states, mild, cum, lb, gn, dproj), carry=carry)
    dproj, dlb, dgn = outs[:3]
    return (dproj, dlb, dgn.reshape(HEADS // NH, 8, HD)[:, 0, :]), outs[3:]


def _lower_bounds(l0, l1):
    m = jnp.maximum(l0, l1)
    e0 = jnp.exp(l0 - m)
    e1 = jnp.exp(l1 - m)
    tot = e0 + e1
    p0 = e0 / tot
    p1 = e1 / tot
    return jnp.clip(p0 - p0, 0.0, 1.0), jnp.clip((p0 + p1) - p0, 0.0, 1.0)


def _lb_fwd(logits):
    def body(l_ref, o_ref):
        lb0, lb1 = _lower_bounds(l_ref[0:1, :], l_ref[1:2, :])
        o_ref[0:1, :] = lb0
        o_ref[1:2, :] = lb1

    return pl.pallas_call(body, name="lb_fwd", out_shape=jax.ShapeDtypeStruct((2, D), F32))(logits)


def _lb_bwd(logits, dlb):
    def body(l_ref, d_ref, o_ref):
        _, vjp = jax.vjp(_lower_bounds, l_ref[0:1, :], l_ref[1:2, :])
        g0, g1 = vjp((d_ref[0:1, :], d_ref[1:2, :]))
        o_ref[0:1, :] = g0
        o_ref[1:2, :] = g1

    return pl.pallas_call(body, name="lb_bwd", out_shape=jax.ShapeDtypeStruct((2, D), F32))(logits, dlb)


ADA_PAD = 128


def _ada_fwd(c_pad, w_ada, b_sh):
    ns = w_ada.shape[2]

    def body(c_ref, w_ref, b_ref, o_ref):
        cv = c_ref[...]
        ca = (cv * _sig(cv)).astype(BF16)
        for l in range(2):
            res = jnp.dot(ca, w_ref[l].astype(BF16), preferred_element_type=F32)
            o_ref[:, l * ns:(l + 1) * ns] = res[0:NDEV, :] + b_ref[l:l + 1, :]

    return pl.pallas_call(body, name="ada_fwd", out_shape=jax.ShapeDtypeStruct((NDEV, 2 * ns), F32),
                          compiler_params=_cp())(c_pad, w_ada, b_sh)


def _ada_wgrad(c_pad_t, d_ada_sh):
    ns = d_ada_sh.shape[2]

    def body(c_ref, d_ref, o_ref):
        cv = c_ref[...]
        ca = (cv * _sig(cv)).astype(BF16)
        for l in range(2):
            o_ref[l] = jnp.dot(ca, d_ref[l].astype(BF16), preferred_element_type=F32)

    return pl.pallas_call(body, name="ada_wgrad", out_shape=jax.ShapeDtypeStruct((2, D, ns), F32),
                          compiler_params=_cp())(c_pad_t, d_ada_sh)


def _sum_devices(g):
    _, R, C = g.shape

    def body(g_ref, o_ref):
        acc = g_ref[0]
        for d in range(1, NDEV):
            acc = acc + g_ref[d]
        o_ref[...] = acc

    return pl.pallas_call(body, name="sum_devices", out_shape=jax.ShapeDtypeStruct((R, C), F32),
                          compiler_params=_cp())(g)


def _adamw(w, g, m, v, name, carry=None):
    R, C = w.shape
    tr = _row_tile(R, max(8, (1 << 19) // C))

    def body(w_ref, g_ref, m_ref, v_ref, d_ref, nm_ref, nv_ref):
        d_ref[...], nm_ref[...], nv_ref[...] = _adamw_update(w_ref[...], g_ref[...], m_ref[...], v_ref[...])

    tile = pl.BlockSpec((tr, C), lambda i: (i, 0))
    return _call(body, name=name, grid=(R // tr,), in_specs=[tile] * 4, out_specs=[tile] * 3,
                 out_shape=[jax.ShapeDtypeStruct((R, C), F32)] * 3, sem=("parallel",), args=(w, g, m, v),
                 carry=carry)


def _adamw_update(w, g, m, v):
    nm = B1 * m + (1.0 - B1) * g
    nv = B2 * v + (1.0 - B2) * (g * g)
    m_hat = nm / (1.0 - B1 ** STEP)
    v_hat = nv / (1.0 - B2 ** STEP)
    return -LR * (m_hat / (jnp.sqrt(v_hat) + AEPS) + WD * w), nm, nv


SMALL_PARTS = (("b_ada", 0, 6, D), ("g_pre", 8, 2, D), ("g_post", 16, 2, D), ("lb_logits", 24, 2, D),
               ("pool_w", 32, 128, D), ("pool_scale", 160, 1, D), ("hgrn_norm_g", 168, 1, 2 * HD))


def _adamw_small(g_small, g_lb_logits, wmv):
    n = len(SMALL_PARTS)

    def body(g_ref, glb_ref, *refs):
        ins, outs = refs[:3 * n], refs[3 * n:]
        for p, (key, row0, rows, width) in enumerate(SMALL_PARTS):
            gv = glb_ref[...] if key == "lb_logits" else g_ref[row0:row0 + rows, 0:width]
            res = _adamw_update(ins[3 * p][...], gv, ins[3 * p + 1][...], ins[3 * p + 2][...])
            for t in range(3):
                outs[3 * p + t][...] = res[t]

    flat = [t for triple in wmv for t in triple]
    outs = pl.pallas_call(body, name="adamw_small",
                          out_shape=[jax.ShapeDtypeStruct(t.shape, F32) for t in flat],
                          compiler_params=_cp())(g_small, g_lb_logits, *flat)
    return [outs[3 * p:3 * p + 3] for p in range(n)]


def _cast_to_slot(place, w, l, name):
    _, R, C = w.shape
    tr = _row_tile(R, max(8, (1 << 19) // C))

    def body(p_ref, w_ref, o_ref):
        o_ref[...] = w_ref[...].astype(BF16)

    return pl.pallas_call(
        body, name=name, out_shape=jax.ShapeDtypeStruct((NCHIP, R, C), BF16),
        grid_spec=pltpu.PrefetchScalarGridSpec(
            num_scalar_prefetch=1, grid=(R // tr,),
            in_specs=[pl.BlockSpec((None, tr, C), lambda i, p_ref: (l, i, 0))],
            out_specs=pl.BlockSpec((None, tr, C), lambda i, p_ref: (p_ref[0], i, 0))),
        compiler_params=_cp(("parallel",)),
    )(place, w)


def _pair_add(core, g, got, name):
    _, R, C = g.shape
    r2 = R // 2
    tr = _row_tile(r2, max(8, (1 << 19) // C))
    nt = r2 // tr

    def body(c_ref, a_ref, b_ref, o_ref):
        o_ref[...] = (a_ref[...].astype(F32) + b_ref[...].astype(F32)).astype(o_ref.dtype)

    return pl.pallas_call(
        body, name=name, out_shape=jax.ShapeDtypeStruct((NCHIP, r2, C), BF16),
        grid_spec=pltpu.PrefetchScalarGridSpec(
            num_scalar_prefetch=1, grid=(NCHIP, nt),
            in_specs=[pl.BlockSpec((None, tr, C), lambda j, i, c_ref: (j, c_ref[0] * nt + i, 0)),
                      pl.BlockSpec((None, tr, C), lambda j, i, c_ref: (j, i, 0))],
            out_specs=pl.BlockSpec((None, tr, C), lambda j, i, c_ref: (j, i, 0))),
        compiler_params=_cp(("parallel", "parallel")),
    )(core, g, got)


def _chip_sum(place, part, recv, layer, both, name):
    _, r2, C = part.shape
    tr = _row_tile(r2, max(8, (1 << 18) // C))
    nt = r2 // tr

    def body(p_ref, own_ref, r_ref, *rest):
        o_ref = rest[-1]
        me = p_ref[0]
        own = own_ref[...].astype(F32)
        acc = None
        for j in range(NCHIP):
            slot = jnp.minimum(jnp.where(j > me, j - 1, j), NCHIP - 2)
            term = jnp.where(me == j, own, r_ref[slot].astype(F32))
            acc = term if acc is None else acc + term
        o_ref[...] = acc

    args = (place, part, recv) if both is None else (place, part, recv, both)
    return pl.pallas_call(
        body, name=name, out_shape=jax.ShapeDtypeStruct((2, 2 * r2, C), F32),
        grid_spec=pltpu.PrefetchScalarGridSpec(
            num_scalar_prefetch=1, grid=(nt,),
            in_specs=[pl.BlockSpec((None, tr, C), lambda i, p_ref: (p_ref[0], i, 0)),
                      pl.BlockSpec((NCHIP - 1, tr, C), lambda i, p_ref: (0, i, 0))] + [ANY] * (len(args) - 3),
            out_specs=pl.BlockSpec((None, tr, C), lambda i, p_ref: (layer, p_ref[1] * nt + i, 0))),
        input_output_aliases={} if both is None else {3: 0},
        compiler_params=_cp(("parallel",)),
    )(*args)


def _place():
    x, y, c = lax.axis_index("x"), lax.axis_index("y"), lax.axis_index("c")
    chips = [(1 - x, y), (x, 1 - y), (1 - x, 1 - y)]
    return x, y, c, chips


def _gather_small(blk, name):
    m_per, n = blk.shape

    def body(x_ref, out_ref, send_sems, recv_sems, local_sem):
        x, y, c, chips = _place()
        me, sibling = (x, y, c), (x, y, 1 - c)

        def rows(px, py, pc):
            return out_ref.at[pl.ds((4 * px + 2 * py + pc) * m_per, m_per), :]

        def copy(k, block, to, src=None):
            return pltpu.make_async_remote_copy(
                src_ref=rows(*block) if src is None else src, dst_ref=rows(*block),
                send_sem=send_sems.at[k], recv_sem=recv_sems.at[k], device_id=to, device_id_type=MESH)

        mine = pltpu.make_async_copy(x_ref, rows(*me), local_sem)
        mine.start()
        first = [copy(0, me, sibling, src=x_ref)]
        first += [copy(1 + j, me, (*chip, c), src=x_ref) for j, chip in enumerate(chips)]
        for cp in first:
            cp.start()
        passed = [copy(4 + j, (*chip, c), sibling) for j, chip in enumerate(chips)]
        for j, chip in enumerate(chips):
            copy(1 + j, (*chip, c), me).wait_recv()
            passed[j].start()
        copy(0, sibling, me).wait_recv()
        for j, chip in enumerate(chips):
            copy(4 + j, (*chip, 1 - c), me).wait_recv()
        for cp in first + passed:
            cp.wait_send()
        mine.wait()

    return pl.pallas_call(
        body, name=name, out_shape=jax.ShapeDtypeStruct((NDEV * m_per, n), blk.dtype),
        in_specs=[pl.BlockSpec(memory_space=pltpu.VMEM)], out_specs=pl.BlockSpec(memory_space=pltpu.VMEM),
        scratch_shapes=[pltpu.SemaphoreType.DMA((7,)), pltpu.SemaphoreType.DMA((7,)), pltpu.SemaphoreType.DMA],
        compiler_params=_cp(),
    )(blk)


def _gather_rows_carry(blk):
    m_per, n = blk.shape

    def rows(ref, px, py, pc):
        return ref.at[pl.ds((4 * px + 2 * py + pc) * m_per, m_per), :]

    def copy(ins, outs, send_sems, recv_sems, k, block, to, own=False):
        return pltpu.make_async_remote_copy(
            src_ref=ins[0] if own else rows(outs[0], *block), dst_ref=rows(outs[0], *block),
            send_sem=send_sems.at[k], recv_sem=recv_sems.at[k], device_id=to, device_id_type=MESH)

    def mine(ins, outs, send_sems):
        x, y, c, _ = _place()
        return pltpu.make_async_copy(ins[0], rows(outs[0], x, y, c), send_sems.at[7])

    def start(ins, outs, send_sems, recv_sems):
        x, y, c, chips = _place()
        mine(ins, outs, send_sems).start()
        copy(ins, outs, send_sems, recv_sems, 0, (x, y, c), (x, y, 1 - c), own=True).start()
        for j, chip in enumerate(chips):
            copy(ins, outs, send_sems, recv_sems, 1 + j, (x, y, c), (*chip, c), own=True).start()

    def finish(ins, outs, send_sems, recv_sems):
        x, y, c, chips = _place()
        for j, chip in enumerate(chips):
            copy(ins, outs, send_sems, recv_sems, 1 + j, (*chip, c), (x, y, c)).wait_recv()
            copy(ins, outs, send_sems, recv_sems, 4 + j, (*chip, c), (x, y, 1 - c)).start()
        copy(ins, outs, send_sems, recv_sems, 0, (x, y, 1 - c), (x, y, c)).wait_recv()
        for j, chip in enumerate(chips):
            copy(ins, outs, send_sems, recv_sems, 4 + j, (*chip, 1 - c), (x, y, c)).wait_recv()
        copy(ins, outs, send_sems, recv_sems, 0, (x, y, c), (x, y, 1 - c), own=True).wait_send()
        for j, chip in enumerate(chips):
            copy(ins, outs, send_sems, recv_sems, 1 + j, (x, y, c), (*chip, c), own=True).wait_send()
            copy(ins, outs, send_sems, recv_sems, 4 + j, (*chip, c), (x, y, 1 - c)).wait_send()
        mine(ins, outs, send_sems).wait()

    return _Carry([blk], [jax.ShapeDtypeStruct((NDEV * m_per, n), blk.dtype)], {}, 8, start, finish)


def _gather_carry(shards, piece=(0, 1, 1)):
    n = len(shards)
    first, count, of = piece

    def rows(ref, half):
        r2 = ref.shape[1] // 2
        return pl.ds(half * r2 + first * (r2 // of), count * (r2 // of))

    def over_ici(outs, send_sems, recv_sems, a, j, chip_xy, slot):
        x, y, c, _ = _place()
        blk = outs[a].at[slot, rows(outs[a], c), :]
        return pltpu.make_async_remote_copy(
            src_ref=blk, dst_ref=blk, send_sem=send_sems.at[6 * a + j], recv_sem=recv_sems.at[6 * a + j],
            device_id=(*chip_xy, c), device_id_type=MESH)

    def over_d2d(outs, send_sems, recv_sems, a, j, slot, half):
        x, y, c, _ = _place()
        blk = outs[a].at[slot, rows(outs[a], half), :]
        return pltpu.make_async_remote_copy(
            src_ref=blk, dst_ref=blk, send_sem=send_sems.at[6 * a + 3 + j], recv_sem=recv_sems.at[6 * a + 3 + j],
            device_id=(x, y, 1 - c), device_id_type=MESH)

    def start(ins, outs, send_sems, recv_sems):
        x, y, c, chips = _place()
        for a in range(n):
            for j, chip_xy in enumerate(chips):
                over_ici(outs, send_sems, recv_sems, a, j, chip_xy, 2 * x + y).start()

    def finish(ins, outs, send_sems, recv_sems):
        x, y, c, chips = _place()
        for a in range(n):
            for j, (cx, cy) in enumerate(chips):
                over_ici(outs, send_sems, recv_sems, a, j, (cx, cy), 2 * cx + cy).wait_recv()
                over_d2d(outs, send_sems, recv_sems, a, j, 2 * cx + cy, c).start()
        for a in range(n):
            for j, (cx, cy) in enumerate(chips):
                over_d2d(outs, send_sems, recv_sems, a, j, 2 * cx + cy, 1 - c).wait_recv()
        for a in range(n):
            for j, (cx, cy) in enumerate(chips):
                over_ici(outs, send_sems, recv_sems, a, j, (cx, cy), 2 * x + y).wait_send()
                over_d2d(outs, send_sems, recv_sems, a, j, 2 * cx + cy, c).wait_send()

    return _Carry(shards, [jax.ShapeDtypeStruct(s.shape, s.dtype) for s in shards],
                  {a: a for a in range(n)}, 6 * n, start, finish)


def _rs_pair(grads, name):
    n = len(grads)

    def body(*refs):
        ins, gots = refs[:n], refs[n:2 * n]
        send_sems, recv_sems = refs[2 * n:]
        x, y, c, _ = _place()
        cps = []
        for a in range(n):
            r2 = ins[a].shape[1] // 2
            cp = pltpu.make_async_remote_copy(
                src_ref=ins[a].at[:, pl.ds((1 - c) * r2, r2), :], dst_ref=gots[a],
                send_sem=send_sems.at[a], recv_sem=recv_sems.at[a],
                device_id=(x, y, 1 - c), device_id_type=MESH)
            cp.start()
            cps.append(cp)
        for cp in cps:
            cp.wait()

    half = [jax.ShapeDtypeStruct((NCHIP, g.shape[1] // 2, g.shape[2]), g.dtype) for g in grads]
    return pl.pallas_call(
        body, name=name, out_shape=half, in_specs=[ANY] * n, out_specs=[ANY] * n,
        scratch_shapes=[pltpu.SemaphoreType.DMA((n,)), pltpu.SemaphoreType.DMA((n,))],
        compiler_params=_cp(),
    )(*grads)


def _chips_carry(parts, piece=(0, 1, 1), into=None):
    n = len(parts)
    first, count, of = piece

    def rows(ref):
        step = ref.shape[1] // of
        return pl.ds(first * step, count * step)

    def send(ins, outs, send_sems, recv_sems, a, j, chip_xy):
        x, y, c, _ = _place()
        me, them = 2 * x + y, 2 * chip_xy[0] + chip_xy[1]
        return pltpu.make_async_remote_copy(
            src_ref=ins[a].at[them, rows(ins[a]), :],
            dst_ref=outs[a].at[me - (me > them).astype(jnp.int32), rows(outs[a]), :],
            send_sem=send_sems.at[3 * a + j], recv_sem=recv_sems.at[3 * a + j],
            device_id=(*chip_xy, c), device_id_type=MESH)

    def start(ins, outs, send_sems, recv_sems):
        _, _, _, chips = _place()
        for a in range(n):
            for j, chip_xy in enumerate(chips):
                send(ins, outs, send_sems, recv_sems, a, j, chip_xy).start()

    def finish(ins, outs, send_sems, recv_sems):
        x, y, c, chips = _place()
        me = 2 * x + y
        for a in range(n):
            for j, (cx, cy) in enumerate(chips):
                them = 2 * cx + cy
                blk = outs[a].at[them - (them > me).astype(jnp.int32), rows(outs[a]), :]
                pltpu.make_async_remote_copy(
                    src_ref=blk, dst_ref=blk, send_sem=send_sems.at[3 * a + j], recv_sem=recv_sems.at[3 * a + j],
                    device_id=(cx, cy, c), device_id_type=MESH).wait_recv()
        for a in range(n):
            for j, chip_xy in enumerate(chips):
                send(ins, outs, send_sems, recv_sems, a, j, chip_xy).wait_send()

    landing = [jax.ShapeDtypeStruct((NCHIP - 1,) + p.shape[1:], p.dtype) for p in parts]
    if into is None:
        return _Carry(parts, landing, {}, 3 * n, start, finish)
    return _Carry(list(parts) + list(into), landing, {n + a: a for a in range(n)}, 3 * n, start, finish)


def _rs_swap(fulls):
    n = len(fulls)

    def body(*refs):
        outs = refs[n:2 * n]
        send_sems, recv_sems = refs[2 * n:]
        x, y, c, _ = _place()
        cps = []
        for a in range(n):
            r2 = outs[a].shape[1] // 2
            mine = outs[a].at[:, pl.ds(c * r2, r2), :]
            cp = pltpu.make_async_remote_copy(
                src_ref=mine, dst_ref=mine, send_sem=send_sems.at[a], recv_sem=recv_sems.at[a],
                device_id=(x, y, 1 - c), device_id_type=MESH)
            cp.start()
            cps.append(cp)
        for a in range(n):
            r2 = outs[a].shape[1] // 2
            blk = outs[a].at[:, pl.ds((1 - c) * r2, r2), :]
            pltpu.make_async_remote_copy(
                src_ref=blk, dst_ref=blk, send_sem=send_sems.at[a], recv_sem=recv_sems.at[a],
                device_id=(x, y, 1 - c), device_id_type=MESH).wait_recv()
        for cp in cps:
            cp.wait_send()

    return pl.pallas_call(
        body, name="rs_swap", out_shape=[jax.ShapeDtypeStruct(f.shape, f.dtype) for f in fulls],
        in_specs=[ANY] * n, out_specs=[ANY] * n, input_output_aliases={a: a for a in range(n)},
        scratch_shapes=[pltpu.SemaphoreType.DMA((n,)), pltpu.SemaphoreType.DMA((n,))],
        compiler_params=_cp(),
    )(*fulls)


def _tail_weight_grads(merged_t, b_in_t, a_in_t, dy, dbr_b, dbr_a, name, tn=256):
    S = dy.shape[0]
    nn = D // tn

    def body(mt_ref, bt_ref, at_ref, dy_ref, db_ref, da_ref, go_ref, gh_ref, gp_ref):
        go_ref[...] = jnp.dot(mt_ref[...], dy_ref[...], preferred_element_type=F32).astype(BF16)
        gh_ref[...] = jnp.dot(bt_ref[...], db_ref[...], preferred_element_type=F32).astype(BF16)
        gp_ref[...] = jnp.dot(at_ref[...], da_ref[...], preferred_element_type=F32).astype(BF16)

    left = lambda rows: pl.BlockSpec((rows, S), lambda n: (0, 0))
    right = pl.BlockSpec((S, tn), lambda n: (0, n))
    out = pl.BlockSpec((D, tn), lambda n: (0, n))
    return pl.pallas_call(
        body, name=name, grid=(nn,), in_specs=[left(D), left(D), left(POOL_W), right, right, right],
        out_specs=[out, out, pl.BlockSpec((None, POOL_W, tn), lambda n: (n, 0, 0))],
        out_shape=[jax.ShapeDtypeStruct((D, D), BF16), jax.ShapeDtypeStruct((D, D), BF16),
                   jax.ShapeDtypeStruct((NCHIP, POOL_W, D // NCHIP), BF16)],
        compiler_params=_cp(("parallel",)),
    )(merged_t, b_in_t, a_in_t, dy, dbr_b, dbr_a)


class _GatherInProj:
    def __init__(self, slot, order):
        self.slot, self.order = slot, order


def _proj_with_gather(h, w_slot, order, name, tn=256):
    S, K = h.shape
    nsh, _, ns = w_slot.shape
    tps = ns // tn
    nt = nsh * tps
    r2 = K // 2

    def body(ord_ref, h_ref, w_in_ref, o_ref, w_ref, wbuf, tile_sems, send_sems, recv_sems):
        n = pl.program_id(0)
        x, y, c, chips = _place()

        def half(slot, which):
            return w_ref.at[slot, pl.ds(which * r2, r2), :]

        def over_ici(j, slot):
            blk = half(slot, c)
            return pltpu.make_async_remote_copy(src_ref=blk, dst_ref=blk, send_sem=send_sems.at[j],
                                                recv_sem=recv_sems.at[j], device_id=(*chips[j], c),
                                                device_id_type=MESH)

        def over_d2d(j, which):
            blk = half(2 * chips[j][0] + chips[j][1], which)
            return pltpu.make_async_remote_copy(src_ref=blk, dst_ref=blk, send_sem=send_sems.at[3 + j],
                                                recv_sem=recv_sems.at[3 + j], device_id=(x, y, 1 - c),
                                                device_id_type=MESH)

        def tile_copy(step, slot):
            shard = ord_ref[step // tps]
            return pltpu.make_async_copy(w_ref.at[shard, :, pl.ds((step % tps) * tn, tn)], wbuf.at[slot],
                                         tile_sems.at[slot])

        @pl.when(n == 0)
        def _():
            for j in range(3):
                over_ici(j, 2 * x + y).start()
            tile_copy(0, 0).start()

        for j in range(3):
            @pl.when(n == (j + 1) * tps - 1)
            def _(j=j):
                over_ici(j, 2 * chips[j][0] + chips[j][1]).wait_recv()
                over_d2d(j, c).start()
                over_d2d(j, 1 - c).wait_recv()

        @pl.when(n + 1 < nt)
        def _():
            tile_copy(n + 1, (n + 1) % 2).start()

        tile_copy(n, n % 2).wait()
        o_ref[...] = jnp.dot(h_ref[...], wbuf[n % 2], preferred_element_type=F32).astype(o_ref.dtype)

        @pl.when(n == nt - 1)
        def _():
            for j in range(3):
                over_ici(j, 2 * x + y).wait_send()
                over_d2d(j, c).wait_send()

    return pl.pallas_call(
        body, name=name,
        out_shape=[jax.ShapeDtypeStruct((S, nsh * ns), BF16), jax.ShapeDtypeStruct(w_slot.shape, w_slot.dtype)],
        grid_spec=pltpu.PrefetchScalarGridSpec(
            num_scalar_prefetch=1, grid=(nt,),
            in_specs=[pl.BlockSpec((S, K), lambda n, o_ref: (0, 0)), ANY],
            out_specs=[pl.BlockSpec((S, tn), lambda n, o_ref: (0, o_ref[n // tps] * tps + n % tps)), ANY],
            scratch_shapes=[pltpu.VMEM((2, K, tn), w_slot.dtype), pltpu.SemaphoreType.DMA((2,)),
                            pltpu.SemaphoreType.DMA((6,)), pltpu.SemaphoreType.DMA((6,))]),
        input_output_aliases={2: 1},
        compiler_params=_cp(("arbitrary",)),
    )(order, h, w_slot)


def _mm_ride(a, b, carry, **kw):
    if carry is None:
        return _mm(a, b, **kw), []
    return _mm(a, b, carry=carry, **kw)


def _layer_fwd(l, x, ada, w, small, ride, target=None):
    shift, scale, gate = ada[:, 0:D], ada[:, D:2 * D], ada[:, 2 * D:3 * D]
    carry, landed = ride("prenorm")
    (h, h_t), outs = _prenorm_fwd(x, small["g_pre"][l], scale, shift, f"prenorm_fwd{l}", carry)
    landed(outs)
    carry, landed = ride("proj")
    if isinstance(carry, _GatherInProj):
        proj, full = _proj_with_gather(h, carry.slot, carry.order, f"proj{l}")
        outs = [full]
    else:
        proj, outs = _mm_ride(h, w["w_in"][l], carry, name=f"proj{l}", b_mode="nn_sh", tm=2048, out_dtype=BF16)
    landed(outs)
    a_in, a_in_t = _pool_fwd(proj, small["pool_w"][l], small["pool_scale"][l], f"pool_fwd{l}")
    carry, landed = ride("hgrn")
    (b_in, b_in_t, o_raw, states, mild, cum), outs = _hgrn_fwd(proj, small["lb"][l], small["hgrn_norm_g"][l],
                                                              f"hgrn_fwd{l}", carry=carry)
    landed(outs)
    carry, landed = ride("tail")
    (br_a, br_b, merged_t, y, *x_new), outs = _layer_tail_fwd(
        proj, a_in, b_in, x, w["w_pool_o"][l], w["w_hgrn_o"][l].reshape(D, D), w["w_out"][l].reshape(D, D),
        gate, small["g_post"][l], f"tail_fwd{l}", target=target, carry=carry)
    landed(outs)
    saved = dict(x=x, h_t=h_t, proj=proj, a_in_t=a_in_t, b_in_t=b_in_t, o_raw=o_raw, states=states, mild=mild,
                 cum=cum,
                 br_a=br_a, br_b=br_b, merged_t=merged_t, y=y, scale=scale, gate=gate)
    return x_new, saved


def _layer_bwd(l, dxn, sv, w, small, ride):
    dy, dbr_a, dbr_b, dproj, da_in, db_in, dgate, dg_post = _layer_head_bwd(
        dxn, sv["y"], sv["proj"], sv["br_a"], sv["br_b"], w["w_pool_o"][l], w["w_hgrn_o"][l].reshape(D, D),
        w["w_out"][l].reshape(D, D), sv["gate"], small["g_post"][l], f"head_bwd{l}")
    gw_out, gw_hgrn_o, gw_pool_o = _tail_weight_grads(sv["merged_t"], sv["b_in_t"], sv["a_in_t"], dy, dbr_b,
                                                      dbr_a, f"gw_tail{l}")
    big = dict(w_pool_o=gw_pool_o, w_hgrn_o=gw_hgrn_o.reshape(NCHIP, D // NCHIP, D),
               w_out=gw_out.reshape(NCHIP, D // NCHIP, D))
    carry, landed = ride["hgrn"](big)
    (dproj, dlb, dgn), outs = _hgrn_bwd(db_in, sv["proj"], sv["o_raw"], sv["states"], sv["mild"], sv["cum"],
                                        small["lb"][l], small["hgrn_norm_g"][l], dproj, f"hgrn_bwd{l}",
                                        carry=carry)
    landed(outs)
    dproj, dpw, dpsc = _pool_bwd(da_in, sv["proj"], small["pool_w"][l], small["pool_scale"][l], dproj,
                                 f"pool_bwd{l}")
    little = dict(dgate=dgate, g_post=dg_post, pool_w=dpw, pool_scale=dpsc, lb=dlb,
                  hgrn_norm_g=jnp.sum(dgn, axis=0, keepdims=True))
    carry, landed = ride["gw_in"](little)
    big["w_in"], outs = _mm_ride(sv["h_t"], dproj, carry, name=f"gw_in{l}", out_shards=NCHIP, out_dtype=BF16)
    landed(outs)
    carry, landed = ride["d_h"](big)
    dh, outs = _mm_ride(dproj, w["w_in"][l], carry, name=f"d_h{l}", b_mode="nt_shk", tn=1024)
    landed(outs)
    carry, landed = ride["prenorm"](big)
    (dx, dshift, dscale, dg_pre), outs = _prenorm_bwd(dh, dxn, sv["x"], small["g_pre"][l], sv["scale"],
                                                      f"prenorm_bwd{l}", carry)
    landed(outs)
    little.update(dshift=dshift, dscale=dscale, g_pre=dg_pre)
    return dx, big, little


SMALL_ROWS = 176


def _rows8(t):
    t = t.reshape(-1, D)
    return jnp.pad(t, ((0, -t.shape[0] % 8), (0, 0)))


def _pack_small(parts):
    row_keys = ("dshift", "dscale", "dgate", "g_pre", "g_post", "lb", "pool_scale", "hgrn_norm_g")
    flat = [p[k] for p in parts for k in row_keys] + [p["pool_w"].reshape(GROUPS * 128 * 128 // D, D) for p in parts]
    nk = len(row_keys)

    def body(*refs):
        o_ref = refs[-1]
        o_ref[...] = jnp.zeros((SMALL_ROWS, D), F32)
        for l in range(2):
            dshift, dscale, dgate, g_pre, g_post, lb, pscale, gn = refs[l * nk:(l + 1) * nk]
            for r, ref in enumerate((dshift, dscale, dgate)):
                o_ref[3 * l + r:3 * l + r + 1, :] = ref[...]
            o_ref[8 + l:9 + l, :] = g_pre[...]
            o_ref[16 + l:17 + l, :] = g_post[...]
            o_ref[24 + l:25 + l, :] = lb[...]
            o_ref[160:161, l * POOL_W:(l + 1) * POOL_W] = pscale[...]
            o_ref[168:169, l * HD:(l + 1) * HD] = gn[...]
            pw = refs[2 * nk + l]
            rows = pw.shape[0]
            o_ref[32 + l * rows:32 + (l + 1) * rows, :] = pw[...]

    return pl.pallas_call(body, name="pack_small", out_shape=jax.ShapeDtypeStruct((SMALL_ROWS, D), F32),
                          compiler_params=_cp())(*flat)


def _unpack_small(p):
    return (p[0:6].reshape(2, 3 * D), p[8:10], p[16:18], p[24:26], p[32:160].reshape(2, GROUPS, 128, 128),
            p[160:161].reshape(2, POOL_W), p[168:169, 0:2 * HD].reshape(2, HD))


def kernel(x, c, w_ada, b_ada, g_pre, g_post, w_in, pool_w, pool_scale, lb_logits, hgrn_norm_g, w_pool_o, w_hgrn_o, w_out, loss_target, m_w_ada, m_b_ada, m_g_pre, m_g_post, m_w_in, m_pool_w, m_pool_scale, m_lb_logits, m_hgrn_norm_g, m_w_pool_o, m_w_hgrn_o, m_w_out, v_w_ada, v_b_ada, v_g_pre, v_g_post, v_w_in, v_pool_w, v_pool_scale, v_lb_logits, v_hgrn_norm_g, v_w_pool_o, v_w_hgrn_o, v_w_out):
    ax, ay, ac = lax.axis_index("x"), lax.axis_index("y"), lax.axis_index("c")
    chip = 2 * ax + ay
    dev = 2 * chip + ac
    xe, te = x[0], loss_target[0]
    ada_s = w_ada.shape[2]

    big_names = ("w_in", "w_pool_o", "w_hgrn_o", "w_out")
    big_w = (w_in, w_pool_o, w_hgrn_o, w_out)
    core = jnp.stack([ac]).astype(jnp.int32)
    place = jnp.stack([chip, ac]).astype(jnp.int32)
    slots = {(k, l): _cast_to_slot(place, t, l, f"cast_{k}{l}") for l in range(2) for k, t in zip(big_names, big_w)}
    w = {k: [None, None] for k in big_names}
    def fills(keys):
        def landed(outs):
            for (k, l), o in zip(keys, outs):
                w[k][l] = slots[k, l] = o
        return landed

    rest0 = [(k, 0) for k in big_names[1:]]
    rest1 = [(k, 1) for k in big_names[1:]]
    no_carry = (None, lambda outs: None)
    order = jnp.stack([chip, 2 * (1 - ax) + ay, 2 * ax + (1 - ay), 2 * (1 - ax) + (1 - ay)]).astype(jnp.int32)

    def ride_fwd0(stage):
        if stage == "proj":
            return _GatherInProj(slots["w_in", 0], order), fills([("w_in", 0)])
        if stage == "hgrn":
            return (_join_carries(_gather_carry([slots[t] for t in rest0]),
                                  _gather_carry([slots["w_in", 1]], piece=(0, 2, 4))),
                    fills(rest0 + [("w_in", 1)]))
        if stage == "tail":
            return _gather_carry([slots["w_in", 1]], piece=(2, 1, 4)), fills([("w_in", 1)])
        return no_carry

    def ride_fwd1(stage):
        if stage == "prenorm":
            return _gather_carry([slots["w_in", 1]], piece=(3, 1, 4)), fills([("w_in", 1)])
        if stage == "hgrn":
            return _gather_carry([slots[t] for t in rest1]), fills(rest1)
        return no_carry

    c_all = _gather_small(jnp.broadcast_to(c, (8, D)), "gather_c").reshape(NDEV, 8, D)[:, 0, :]
    c_pad = jnp.pad(c_all, ((0, ADA_PAD - NDEV), (0, 0)))
    b_sh = lax.dynamic_slice(b_ada, (0, chip * ada_s), (2, ada_s))
    ada_cols = _gather_small(_ada_fwd(c_pad, w_ada, b_sh), "gather_ada")
    ada_cols = ada_cols.reshape(NCHIP, 2, NDEV, 2, ada_s)[:, 0]
    ada_all = jnp.transpose(ada_cols, (2, 1, 0, 3)).reshape(2, NDEV, 3 * D)
    ada_me = lax.dynamic_slice(ada_all, (0, dev, 0), (2, 1, 3 * D))

    lbs = _lb_fwd(lb_logits)
    small = dict(g_pre=g_pre[:, None, :], g_post=g_post[:, None, :], pool_w=pool_w,
                 pool_scale=pool_scale[:, None, :], lb=lbs[:, None, :], hgrn_norm_g=hgrn_norm_g[:, None, :])

    (x1,), sv0 = _layer_fwd(0, xe, ada_me[0], w, small, ride_fwd0)
    (dx2, loss_blk), sv1 = _layer_fwd(1, x1, ada_me[1], w, small, ride_fwd1, target=te)

    parts, recv = {}, {}

    def pair_sums(keys, grads, tag):
        got = _rs_pair(grads, f"rs_pair_{tag}")
        for kl, g, o in zip(keys, grads, got):
            parts[kl] = _pair_add(core, g, o, f"rs_add_{kl[0]}{kl[1]}")

    def exchange(keys):
        def landed(outs):
            recv.update(zip(keys, outs))
        return _chips_carry([parts[kl] for kl in keys]), landed

    def early(l):
        return [(k, l) for k in big_names[1:]]

    def ride_hgrn1(big):
        pair_sums(early(1), [big[k] for k in big_names[1:]], "l1_early")
        return exchange(early(1))

    def ride_d_h1(big):
        pair_sums([("w_in", 1)], [big["w_in"]], "l1_w_in")
        return no_carry

    def ride_hgrn0(big):
        pair_sums(early(0), [big[k] for k in big_names[1:]], "l0_early")
        return exchange([("w_in", 1)] + early(0))

    def ride_d_h0(big):
        pair_sums([("w_in", 0)], [big["w_in"]], "l0_w_in")

        def landed(outs):
            (recv["w_in", 0],) = outs
        return _chips_carry([parts["w_in", 0]], piece=(0, 1, 2)), landed

    def ride_prenorm0(big):
        def landed(outs):
            (recv["w_in", 0],) = outs
        return _chips_carry([parts["w_in", 0]], piece=(1, 1, 2), into=[recv["w_in", 0]]), landed

    no_ride = lambda so_far: no_carry
    dx1, big1, little1 = _layer_bwd(1, dx2, sv1, w, small,
                                    dict(hgrn=ride_hgrn1, gw_in=no_ride, d_h=ride_d_h1, prenorm=no_ride))

    gathered = {}
    zero_row = jnp.zeros((1, D), F32)

    def ride_gw_in0(little):
        so_far = dict(little, dshift=zero_row, dscale=zero_row, g_pre=zero_row)

        def landed(outs):
            (gathered["early"],) = outs
        return _gather_rows_carry(_pack_small([so_far, little1])), landed

    dx0, big0, little0 = _layer_bwd(0, dx1, sv0, w, small,
                                    dict(hgrn=ride_hgrn0, gw_in=ride_gw_in0, d_h=ride_d_h0, prenorm=ride_prenorm0))
    loss = lax.psum(loss_blk[0, 0], ("x", "y", "c"))
    late = _rows8(jnp.stack([little0["dshift"], little0["dscale"], little0["g_pre"]]))
    late = _gather_small(late, "gather_small_late").reshape(NDEV, 8, D)
    packed = gathered["early"].reshape(NDEV, SMALL_ROWS, D)
    packed = packed.at[:, 0:2, :].set(late[:, 0:2, :]).at[:, 8:9, :].set(late[:, 2:3, :])
    red = []
    for k in big_names:
        both = _chip_sum(place, parts[k, 1], recv[k, 1], 1, None, f"rs_sum_{k}1")
        red.append(_chip_sum(place, parts[k, 0], recv[k, 0], 0, both, f"rs_sum_{k}0"))
    g_big = dict(zip(big_names, _rs_swap(red)))

    def upd(wt, g, m, v, name, carry=None):
        shp = wt.shape
        two = lambda t: t.reshape(-1, shp[-1])
        res = _adamw(two(wt), two(g), two(m), two(v), name, carry)
        return [t.reshape(shp) for t in res[:3]], res[3:]

    u_w_in, _ = upd(w_in, g_big["w_in"], m_w_in, v_w_in, "adamw_w_in")
    g_small = _sum_devices(packed)
    g_b_ada, g_g_pre, g_g_post, g_lb, g_pool_w, g_pool_scale, g_norm_g = _unpack_small(g_small)
    g_lb_logits = _lb_bwd(lb_logits, g_lb)
    d_ada_all = packed[:, 0:6, :].reshape(NDEV, 2, 3 * D)
    d_ada_sh = lax.dynamic_slice(jnp.transpose(d_ada_all, (1, 0, 2)), (0, 0, chip * ada_s), (2, NDEV, ada_s))
    d_ada_sh = jnp.pad(d_ada_sh, ((0, 0), (0, ADA_PAD - NDEV), (0, 0)))
    g_w_ada = _ada_wgrad(c_pad.T, d_ada_sh)

    u_w_ada, _ = upd(w_ada, g_w_ada, m_w_ada, v_w_ada, "adamw_w_ada")
    u_w_pool_o, _ = upd(w_pool_o, g_big["w_pool_o"], m_w_pool_o, v_w_pool_o, "adamw_w_pool_o")
    u_w_hgrn_o, _ = upd(w_hgrn_o, g_big["w_hgrn_o"], m_w_hgrn_o, v_w_hgrn_o, "adamw_w_hgrn_o")
    u_w_out, _ = upd(w_out, g_big["w_out"], m_w_out, v_w_out, "adamw_w_out")
    small_w = dict(b_ada=(b_ada, m_b_ada, v_b_ada), g_pre=(g_pre, m_g_pre, v_g_pre),
                   g_post=(g_post, m_g_post, v_g_post), lb_logits=(lb_logits, m_lb_logits, v_lb_logits),
                   pool_w=(pool_w, m_pool_w, v_pool_w), pool_scale=(pool_scale, m_pool_scale, v_pool_scale),
                   hgrn_norm_g=(hgrn_norm_g, m_hgrn_norm_g, v_hgrn_norm_g))
    in_rows = [tuple(t.reshape(rows, width) for t in small_w[key]) for key, _, rows, width in SMALL_PARTS]
    u_rows = _adamw_small(g_small, g_lb_logits, in_rows)
    u_small = {key: [t.reshape(small_w[key][0].shape) for t in u_rows[p]]
               for p, (key, _, _, _) in enumerate(SMALL_PARTS)}

    grads_out = (g_w_ada, g_b_ada, g_g_pre, g_g_post, g_big["w_in"], g_pool_w, g_pool_scale, g_lb_logits,
                 g_norm_g, g_big["w_pool_o"], g_big["w_hgrn_o"], g_big["w_out"])

    def ordered(k):
        s = lambda key: u_small[key][k]
        return (u_w_ada[k], s("b_ada"), s("g_pre"), s("g_post"), u_w_in[k], s("pool_w"), s("pool_scale"),
                s("lb_logits"), s("hgrn_norm_g"), u_w_pool_o[k], u_w_hgrn_o[k], u_w_out[k])

    return (loss, dx0[None], *grads_out, *ordered(0), *ordered(1), *ordered(2))
```

```python
import functools

import jax
import jax.numpy as jnp
from jax import lax
from jax.experimental import pallas as pl
from jax.experimental.pallas import tpu as pltpu

F32 = jnp.float32
BF16 = jnp.bfloat16
MESH = pl.DeviceIdType.MESH

D = 1024
HEADS = 8
HD = 128
GROUPS = 4
POOL_W = 512
WINDOWS = (2, 4, 8, 16)
CH = 128
SB = 32
NH = 2
IN_W = 7168
NCHIP = 4
NDEV = 8
EPS = 1e-6
PV0, PG0, HQ0, HF0, HI0, HG0 = 0, 4, 8, 16, 24, 32
MGP_BLK, MGH_BLK = 5, 6

LR, B1, B2, AEPS, WD, STEP = 0.001, 0.9, 0.999, 1e-08, 0.01, 10
VMEM_LIMIT = 56 * 1024 * 1024


def _cp(sem=None, **kw):
    if sem is not None:
        kw["dimension_semantics"] = sem
    return pltpu.CompilerParams(vmem_limit_bytes=VMEM_LIMIT, **kw)


def _sig(z):
    return 1.0 / (1.0 + jnp.exp(-z))


def _dsilu(z, s):
    return s * (1.0 + z * (1.0 - s))


def _row_tile(rows, cap):
    if rows <= cap:
        return rows
    t = 1 << (cap.bit_length() - 1)
    while rows % t:
        t //= 2
    return t


ANY = pl.BlockSpec(memory_space=pl.ANY)


class _Carry:
    def __init__(self, ins, outs, aliases, n_sem, start, finish):
        self.ins, self.outs, self.aliases, self.n_sem = list(ins), list(outs), dict(aliases), n_sem
        self.start, self.finish = start, finish


class _SemWindow:
    def __init__(self, ref, base):
        self._ref, self._base = ref, base

    @property
    def at(self):
        return self

    def __getitem__(self, k):
        return self._ref.at[self._base + k]


def _join_carries(*carries):
    ins, outs, aliases, spans, n_sem = [], [], {}, [], 0
    for cr in carries:
        aliases.update({len(ins) + i: len(outs) + o for i, o in cr.aliases.items()})
        spans.append((len(ins), len(cr.ins), len(outs), len(cr.outs), n_sem))
        ins, outs, n_sem = ins + cr.ins, outs + cr.outs, n_sem + cr.n_sem

    def run(which):
        def fn(i_refs, o_refs, send_sems, recv_sems):
            for cr, (i0, ni, o0, no, s0) in zip(carries, spans):
                getattr(cr, which)(i_refs[i0:i0 + ni], o_refs[o0:o0 + no], _SemWindow(send_sems, s0),
                                   _SemWindow(recv_sems, s0))
        return fn

    return _Carry(ins, outs, aliases, n_sem, run("start"), run("finish"))


def _call(body, *, name, grid, in_specs, out_specs, out_shape, args, scratch_shapes=(), sem=None, carry=None,
          aliases=None):
    in_specs, out_specs, out_shape = list(in_specs), list(out_specs), list(out_shape)
    scratch_shapes = list(scratch_shapes)
    aliases = dict(aliases or {})
    if carry is None:
        outs = pl.pallas_call(body, name=name, grid=grid, in_specs=in_specs, out_specs=out_specs,
                              out_shape=out_shape, scratch_shapes=scratch_shapes, input_output_aliases=aliases,
                              compiler_params=_cp(sem))(*args)
        return list(outs)
    n_in, n_out, n_scr = len(in_specs), len(out_specs), len(scratch_shapes)
    c_in, c_out = len(carry.ins), len(carry.outs)

    def wrapped(*refs):
        k_in, rest = refs[:n_in], refs[n_in:]
        ci, rest = rest[:c_in], rest[c_in:]
        k_out, rest = rest[:n_out], rest[n_out:]
        co, rest = rest[:c_out], rest[c_out:]
        k_scr, (ssem, rsem) = rest[:n_scr], rest[n_scr:]
        pids = [pl.program_id(d) for d in range(len(grid))]
        first = functools.reduce(jnp.logical_and, [p == 0 for p in pids])
        last = functools.reduce(jnp.logical_and, [p == g - 1 for p, g in zip(pids, grid)])

        @pl.when(first)
        def _():
            carry.start(ci, co, ssem, rsem)

        body(*k_in, *k_out, *k_scr)

        @pl.when(last)
        def _():
            carry.finish(ci, co, ssem, rsem)

    outs = pl.pallas_call(
        wrapped, name=name, grid=grid, in_specs=in_specs + [ANY] * c_in, out_specs=out_specs + [ANY] * c_out,
        out_shape=out_shape + carry.outs,
        input_output_aliases={**aliases, **{n_in + i: n_out + o for i, o in carry.aliases.items()}},
        scratch_shapes=scratch_shapes + [pltpu.SemaphoreType.DMA((carry.n_sem,))] * 2,
        compiler_params=_cp(("arbitrary",) * len(grid)),
    )(*args, *carry.ins)
    return list(outs)


def _mm(a, b, *, name, b_mode="nn", out_shards=0, tm=1024, tn=256, tk=None, out_dtype=F32, carry=None):
    M, K = a.shape
    if b_mode == "nn":
        N = b.shape[1]
    elif b_mode == "nt":
        N = b.shape[0]
    elif b_mode == "nn_sh":
        N = b.shape[0] * b.shape[2]
    else:
        N = b.shape[1]
    tm = _row_tile(M, tm)
    if b_mode == "nn_sh":
        tn = _row_tile(b.shape[2], tn)
    elif out_shards:
        tn = _row_tile(N // out_shards, tn)
    else:
        tn = _row_tile(N, tn)
    if tk is None:
        tk = K if b_mode != "nt_shk" else b.shape[2]
    if b_mode == "nt_shk":
        tk = _row_tile(b.shape[2], tk)
    nm, nn, nk = M // tm, N // tn, K // tk

    a_spec = pl.BlockSpec((tm, tk), lambda m, n, k: (m, k))
    if b_mode == "nn":
        b_spec = pl.BlockSpec((tk, tn), lambda m, n, k: (k, n))
    elif b_mode == "nt":
        b_spec = pl.BlockSpec((tn, tk), lambda m, n, k: (n, k))
    elif b_mode == "nn_sh":
        nps = b.shape[2] // tn
        b_spec = pl.BlockSpec((None, tk, tn), lambda m, n, k: (n // nps, k, n % nps))
    else:
        kps = b.shape[2] // tk
        b_spec = pl.BlockSpec((None, tn, tk), lambda m, n, k: (k // kps, n, k % kps))
    if out_shards:
        ops = (N // out_shards) // tn
        o_spec = pl.BlockSpec((None, tm, tn), lambda m, n, k: (n // ops, m, n % ops))
        o_shape = jax.ShapeDtypeStruct((out_shards, M, N // out_shards), out_dtype)
    else:
        o_spec = pl.BlockSpec((tm, tn), lambda m, n, k: (m, n))
        o_shape = jax.ShapeDtypeStruct((M, N), out_dtype)
    trans_b = b_mode in ("nt", "nt_shk")
    dn = (((1,), (1,)), ((), ())) if trans_b else (((1,), (0,)), ((), ()))

    def body(a_ref, b_ref, o_ref, acc_ref):
        k = pl.program_id(2)

        @pl.when(k == 0)
        def _():
            acc_ref[...] = jnp.zeros(acc_ref.shape, F32)

        acc_ref[...] += lax.dot_general(a_ref[...].astype(BF16), b_ref[...].astype(BF16), dn,
                                        preferred_element_type=F32)

        @pl.when(k == nk - 1)
        def _():
            o_ref[...] = acc_ref[...].astype(o_ref.dtype)

    outs = _call(body, name=name, grid=(nm, nn, nk), in_specs=[a_spec, b_spec], out_specs=[o_spec],
                 out_shape=[o_shape], scratch_shapes=[pltpu.VMEM((tm, tn), F32)],
                 sem=("parallel", "parallel", "arbitrary"), args=(a, b), carry=carry)
    return outs[0] if carry is None else (outs[0], outs[1:])


def _rowvec(n=D):
    return pl.BlockSpec((1, n), lambda i: (0, 0))


def _prenorm_fwd(x, g, scale, shift, name, carry=None):
    S = x.shape[0]
    tr = _row_tile(S, 256)

    def body(x_ref, g_ref, sc_ref, sh_ref, h_ref, ht_ref):
        xv = x_ref[...]
        r = lax.rsqrt(jnp.mean(xv * xv, axis=-1, keepdims=True) + EPS)
        hv = (xv * r) * g_ref[...] * (1.0 + sc_ref[...]) + sh_ref[...]
        h_ref[...] = hv.astype(BF16)
        ht_ref[...] = hv.T.astype(BF16)

    outs = _call(
        body, name=name, grid=(S // tr,),
        in_specs=[pl.BlockSpec((tr, D), lambda i: (i, 0)), _rowvec(), _rowvec(), _rowvec()],
        out_specs=[pl.BlockSpec((tr, D), lambda i: (i, 0)), pl.BlockSpec((D, tr), lambda i: (0, i))],
        out_shape=[jax.ShapeDtypeStruct((S, D), BF16), jax.ShapeDtypeStruct((D, S), BF16)],
        sem=("parallel",), args=(x, g, scale, shift), carry=carry)
    return outs[:2], outs[2:]


def _prenorm_bwd(dh, dxn, x, g, scale, name, carry=None):
    S = x.shape[0]
    tr = _row_tile(S, 256)

    def body(dh_ref, dxn_ref, x_ref, g_ref, sc_ref, dx_ref, dsh_ref, dsc_ref, dg_ref):
        i = pl.program_id(0)

        @pl.when(i == 0)
        def _():
            dsh_ref[...] = jnp.zeros((1, D), F32)
            dsc_ref[...] = jnp.zeros((1, D), F32)
            dg_ref[...] = jnp.zeros((1, D), F32)

        xv = x_ref[...]
        dhv = dh_ref[...]
        gv = g_ref[...]
        mod = 1.0 + sc_ref[...]
        r = lax.rsqrt(jnp.mean(xv * xv, axis=-1, keepdims=True) + EPS)
        xh = xv * r
        dsh_ref[...] += jnp.sum(dhv, axis=0, keepdims=True)
        dsc_ref[...] += jnp.sum(dhv * (xh * gv), axis=0, keepdims=True)
        dg_ref[...] += jnp.sum(dhv * mod * xh, axis=0, keepdims=True)
        u = dhv * mod * gv
        dx_ref[...] = dxn_ref[...] + r * u - xv * (r * r * r) * jnp.mean(u * xv, axis=-1, keepdims=True)

    tile = pl.BlockSpec((tr, D), lambda i: (i, 0))
    outs = _call(
        body, name=name, grid=(S // tr,),
        in_specs=[tile, tile, tile, _rowvec(), _rowvec()],
        out_specs=[tile, _rowvec(), _rowvec(), _rowvec()],
        out_shape=[jax.ShapeDtypeStruct((S, D), F32)] + [jax.ShapeDtypeStruct((1, D), F32)] * 3,
        sem=("arbitrary",), args=(dh, dxn, x, g, scale), carry=carry)
    return outs[:4], outs[4:]


def _layer_tail_fwd(proj, a_in, b_in, x, w_po, w_ho, w_out, gate, g, name, target=None, carry=None):
    S = proj.shape[0]
    tr = _row_tile(S, 256)
    nsh, _, wsh = w_po.shape
    n_in = 10 + (target is not None)

    def body(*refs):
        (mgp_ref, mgh_ref, a_ref, b_ref, x_ref, wpo_ref, who_ref, wout_ref, gate_ref, g_ref) = refs[:10]
        bra_ref, brb_ref, mt_ref, y_ref, xn_ref = refs[n_in:n_in + 5]
        av = a_ref[...]
        bra = jnp.concatenate([jnp.dot(av, wpo_ref[j], preferred_element_type=F32) for j in range(nsh)], axis=1)
        brb = jnp.dot(b_ref[...], who_ref[...], preferred_element_type=F32)
        mv = _sig(mgp_ref[...].astype(F32)) * bra + _sig(mgh_ref[...].astype(F32)) * brb
        bra_ref[...] = bra.astype(BF16)
        brb_ref[...] = brb.astype(BF16)
        mt_ref[...] = mv.T.astype(BF16)
        yv = jnp.dot(mv.astype(BF16), wout_ref[...], preferred_element_type=F32)
        y_ref[...] = yv
        r = lax.rsqrt(jnp.mean(yv * yv, axis=-1, keepdims=True) + EPS)
        xn = x_ref[...] + gate_ref[...] * ((yv * r) * g_ref[...])
        if target is None:
            xn_ref[...] = xn
        else:
            t_ref, l_ref = refs[10], refs[n_in + 5]

            @pl.when(pl.program_id(0) == 0)
            def _():
                l_ref[...] = jnp.zeros((8, 128), F32)

            err = xn - t_ref[...]
            xn_ref[...] = err * (1.0 / D)
            l_ref[...] += 0.5 * jnp.sum(jnp.mean(err * err, axis=-1, keepdims=True))

    tile = pl.BlockSpec((tr, D), lambda i: (i, 0))
    whole = lambda t: pl.BlockSpec(t.shape, lambda i: (0,) * t.ndim)
    last = target is not None
    outs = _call(
        body, name=name, grid=(S // tr,),
        in_specs=[pl.BlockSpec((tr, D), lambda i: (i, MGP_BLK)), pl.BlockSpec((tr, D), lambda i: (i, MGH_BLK)),
                  pl.BlockSpec((tr, POOL_W), lambda i: (i, 0)), tile, tile, whole(w_po), whole(w_ho),
                  whole(w_out), _rowvec(), _rowvec()] + [tile] * last,
        out_specs=[tile, tile, pl.BlockSpec((D, tr), lambda i: (0, i)), tile, tile]
        + [pl.BlockSpec((8, 128), lambda i: (0, 0))] * last,
        out_shape=[jax.ShapeDtypeStruct((S, D), BF16), jax.ShapeDtypeStruct((S, D), BF16),
                   jax.ShapeDtypeStruct((D, S), BF16), jax.ShapeDtypeStruct((S, D), F32),
                   jax.ShapeDtypeStruct((S, D), F32)] + [jax.ShapeDtypeStruct((8, 128), F32)] * last,
        sem=("arbitrary",) if last else ("parallel",),
        args=(proj, proj, a_in, b_in, x, w_po, w_ho, w_out, gate, g) + ((target,) if last else ()), carry=carry)
    return outs[:5 + last], outs[5 + last:]


def _layer_head_bwd(dxn, y, proj, br_a, br_b, w_po, w_ho, w_out, gate, g, name):
    S = y.shape[0]
    tr = _row_tile(S, 256)
    nsh, _, wsh = w_po.shape

    def body(dxn_ref, y_ref, mgp_ref, mgh_ref, bra_ref, brb_ref, wpo_ref, who_ref, wout_ref, gate_ref, g_ref,
             dy_ref, dba_ref, dbb_ref, dproj_ref, dain_ref, dbin_ref, dgate_ref, dg_ref, dmgh_s):
        i = pl.program_id(0)
        j = pl.program_id(1)

        @pl.when((i == 0) & (j == 0))
        def _():
            dgate_ref[...] = jnp.zeros((1, D), F32)
            dg_ref[...] = jnp.zeros((1, D), F32)

        @pl.when(j == 1)
        def _():
            dproj_ref[...] = dmgh_s[...]

        @pl.when(j == 0)
        def _():
            everything(dxn_ref, y_ref, mgp_ref, mgh_ref, bra_ref, brb_ref, wpo_ref, who_ref, wout_ref, gate_ref,
                       g_ref, dy_ref, dba_ref, dbb_ref, dproj_ref, dain_ref, dbin_ref, dgate_ref, dg_ref, dmgh_s)

    def everything(dxn_ref, y_ref, mgp_ref, mgh_ref, bra_ref, brb_ref, wpo_ref, who_ref, wout_ref, gate_ref, g_ref,
                   dy_ref, dba_ref, dbb_ref, dproj_ref, dain_ref, dbin_ref, dgate_ref, dg_ref, dmgh_s):
        yv = y_ref[...]
        dv = dxn_ref[...]
        gv = g_ref[...]
        gt = gate_ref[...]
        r = lax.rsqrt(jnp.mean(yv * yv, axis=-1, keepdims=True) + EPS)
        yh = yv * r
        dgate_ref[...] += jnp.sum(dv * (yh * gv), axis=0, keepdims=True)
        dg_ref[...] += jnp.sum(dv * gt * yh, axis=0, keepdims=True)
        u = dv * gt * gv
        dy = (r * u - yv * (r * r * r) * jnp.mean(u * yv, axis=-1, keepdims=True)).astype(BF16)
        dy_ref[...] = dy
        dm = _dot_nt(dy, wout_ref[...])
        sp = _sig(mgp_ref[...].astype(F32))
        sh = _sig(mgh_ref[...].astype(F32))
        dba = (dm * sp).astype(BF16)
        dbb = (dm * sh).astype(BF16)
        dba_ref[...] = dba
        dbb_ref[...] = dbb
        dproj_ref[...] = (dm * bra_ref[...].astype(F32) * sp * (1.0 - sp)).astype(BF16)
        dmgh_s[...] = (dm * brb_ref[...].astype(F32) * sh * (1.0 - sh)).astype(BF16)
        dain = _dot_nt(dba[:, 0:wsh], wpo_ref[0])
        for k in range(1, nsh):
            dain = dain + _dot_nt(dba[:, k * wsh:(k + 1) * wsh], wpo_ref[k])
        dain_ref[...] = dain
        dbin_ref[...] = _dot_nt(dbb, who_ref[...])

    tile = pl.BlockSpec((tr, D), lambda i, j: (i, 0))
    whole = lambda t: pl.BlockSpec(t.shape, lambda i, j: (0,) * t.ndim)
    vec = pl.BlockSpec((1, D), lambda i, j: (0, 0))
    ahead = lambda i, j: jnp.minimum(i + j, S // tr - 1)
    tile_in = pl.BlockSpec((tr, D), lambda i, j: (ahead(i, j), 0))
    return pl.pallas_call(
        body, name=name, grid=(S // tr, 2),
        in_specs=[tile_in, tile_in, pl.BlockSpec((tr, D), lambda i, j: (ahead(i, j), MGP_BLK)),
                  pl.BlockSpec((tr, D), lambda i, j: (ahead(i, j), MGH_BLK)), tile_in, tile_in, whole(w_po),
                  whole(w_ho), whole(w_out), vec, vec],
        out_specs=[tile, tile, tile, pl.BlockSpec((tr, D), lambda i, j: (i, MGP_BLK + j)),
                   pl.BlockSpec((tr, POOL_W), lambda i, j: (i, 0)), tile, vec, vec],
        out_shape=[jax.ShapeDtypeStruct((S, D), BF16)] * 3
        + [jax.ShapeDtypeStruct((S, IN_W), BF16), jax.ShapeDtypeStruct((S, POOL_W), F32),
           jax.ShapeDtypeStruct((S, D), F32), jax.ShapeDtypeStruct((1, D), F32), jax.ShapeDtypeStruct((1, D), F32)],
        scratch_shapes=[pltpu.VMEM((tr, D), BF16)],
        compiler_params=_cp(("arbitrary", "arbitrary")),
    )(dxn, y, proj, proj, br_a, br_b, w_po, w_ho, w_out, gate, g)


def _pool_pieces(u, g, S):
    rowi = lax.broadcasted_iota(jnp.int32, (S, 1), 0)

    def down(z, k):
        return jnp.where(rowi >= k, pltpu.roll(z, k, axis=0), 0.0)

    s2 = u + down(u, 1)
    s4 = s2 + down(s2, 2)
    s8 = s4 + down(s4, 4)
    s16 = s8 + down(s8, 8)
    win = jnp.where(g == 0, s2, jnp.where(g == 1, s4, jnp.where(g == 2, s8, s16)))
    w = jnp.where(g == 0, 2, jnp.where(g == 1, 4, jnp.where(g == 2, 8, 16)))
    count = jnp.minimum(rowi + 1, w).astype(F32)
    return win / count - u, count, rowi


def _pool_fwd(proj, pw, pscale, name):
    S = proj.shape[0]

    def body(pv_ref, pg_ref, pw_ref, sc_ref, a_ref, at_ref):
        g = pl.program_id(0)
        pooled, _, _ = _pool_pieces(pv_ref[...].astype(F32), g, S)
        pm = jnp.dot(pooled.astype(BF16), pw_ref[...].astype(BF16), preferred_element_type=F32)
        pgv = pg_ref[...].astype(F32)
        av = pm * sc_ref[...] * (pgv * _sig(pgv))
        a_ref[...] = av.astype(BF16)
        at_ref[...] = av.T.astype(BF16)

    return pl.pallas_call(
        body, name=name, grid=(GROUPS,),
        in_specs=[pl.BlockSpec((S, 128), lambda g: (0, PV0 + g)), pl.BlockSpec((S, 128), lambda g: (0, PG0 + g)),
                  pl.BlockSpec((None, 128, 128), lambda g: (g, 0, 0)), pl.BlockSpec((1, 128), lambda g: (0, g))],
        out_specs=[pl.BlockSpec((S, 128), lambda g: (0, g)), pl.BlockSpec((128, S), lambda g: (g, 0))],
        out_shape=[jax.ShapeDtypeStruct((S, POOL_W), BF16), jax.ShapeDtypeStruct((POOL_W, S), BF16)],
        compiler_params=_cp(("parallel",)),
    )(proj, proj, pw, pscale)


def _pool_bwd(da, proj, pw, pscale, dproj, name):
    S = proj.shape[0]

    def body(da_ref, pv_ref, pg_ref, pw_ref, sc_ref, dproj_in, dproj_ref, dpw_ref, dsc_ref, dpg_s):
        @pl.when(pl.program_id(1) == 1)
        def _():
            dproj_ref[...] = dpg_s[...]

        @pl.when(pl.program_id(1) == 0)
        def _():
            group(da_ref, pv_ref, pg_ref, pw_ref, sc_ref, dproj_ref, dpg_s, dpw_ref, dsc_ref)

    def group(da_ref, pv_ref, pg_ref, pw_ref, sc_ref, dpv_ref, dpg_ref, dpw_ref, dsc_ref):
        g = pl.program_id(0)
        pooled, count, rowi = _pool_pieces(pv_ref[...].astype(F32), g, S)
        pwb = pw_ref[...].astype(BF16)
        pm = jnp.dot(pooled.astype(BF16), pwb, preferred_element_type=F32)
        scv = sc_ref[...]
        pgv = pg_ref[...].astype(F32)
        sg = _sig(pgv)
        dav = da_ref[...]
        d_ps = dav * (pgv * sg)
        dpg_ref[...] = (dav * (pm * scv) * _dsilu(pgv, sg)).astype(BF16)
        dsc_ref[...] = jnp.sum(d_ps * pm, axis=0, keepdims=True)
        d_pm = (d_ps * scv).astype(BF16)
        dpw_ref[...] = lax.dot_general(pooled.astype(BF16), d_pm, (((0,), (0,)), ((), ())),
                                       preferred_element_type=F32)
        d_pooled = lax.dot_general(d_pm, pwb, (((1,), (1,)), ((), ())), preferred_element_type=F32)
        z = d_pooled / count

        def up(v, k):
            return jnp.where(rowi < S - k, pltpu.roll(v, S - k, axis=0), 0.0)

        t2 = z + up(z, 1)
        t4 = t2 + up(t2, 2)
        t8 = t4 + up(t4, 4)
        t16 = t8 + up(t8, 8)
        adj = jnp.where(g == 0, t2, jnp.where(g == 1, t4, jnp.where(g == 2, t8, t16)))
        dpv_ref[...] = (adj - d_pooled).astype(BF16)

    col = lambda g, j: (0, g)
    ahead = lambda g, j: jnp.minimum(g + j, GROUPS - 1)
    return pl.pallas_call(
        body, name=name, grid=(GROUPS, 2),
        in_specs=[pl.BlockSpec((S, 128), lambda g, j: (0, ahead(g, j))),
                  pl.BlockSpec((S, 128), lambda g, j: (0, PV0 + ahead(g, j))),
                  pl.BlockSpec((S, 128), lambda g, j: (0, PG0 + ahead(g, j))),
                  pl.BlockSpec((None, 128, 128), lambda g, j: (ahead(g, j), 0, 0)),
                  pl.BlockSpec((1, 128), lambda g, j: (0, ahead(g, j))), ANY],
        out_specs=[pl.BlockSpec((S, 128), lambda g, j: (0, PV0 + g + (PG0 - PV0) * j)),
                   pl.BlockSpec((None, 128, 128), lambda g, j: (g, 0, 0)), pl.BlockSpec((1, 128), col)],
        out_shape=[jax.ShapeDtypeStruct(dproj.shape, dproj.dtype),
                   jax.ShapeDtypeStruct((GROUPS, 128, 128), F32), jax.ShapeDtypeStruct((1, POOL_W), F32)],
        scratch_shapes=[pltpu.VMEM((S, 128), BF16)], input_output_aliases={5: 0},
        compiler_params=_cp(("arbitrary", "arbitrary")),
    )(da, proj, proj, pw, pscale, dproj)


SCAN_SHIFTS = tuple(1 << b for b in range(CH.bit_length() - 1))


def _chunk_cumsum(z, rowi):
    for sh in SCAN_SHIFTS:
        z = z + jnp.where(rowi >= sh, pltpu.roll(z, sh, axis=0), 0.0)
    return z


def _chunk_rev_cumsum(z, rowi):
    for sh in SCAN_SHIFTS:
        z = z + jnp.where(rowi < CH - sh, pltpu.roll(z, CH - sh, axis=0), 0.0)
    return z


def _dot_nn(a, b):
    return jnp.dot(a.astype(BF16), b.astype(BF16), preferred_element_type=F32)


def _dot_nt(a, b):
    return lax.dot_general(a.astype(BF16), b.astype(BF16), (((1,), (1,)), ((), ())), preferred_element_type=F32)


def _dot_tn(a, b):
    return lax.dot_general(a.astype(BF16), b.astype(BF16), (((0,), (0,)), ((), ())), preferred_element_type=F32)


def _gates(hq, hf, lbv):
    hq, hf = hq.astype(F32), hf.astype(F32)
    sq = _sig(hq)
    sf = _sig(hf)
    f = lbv + (1.0 - lbv) * sf
    fc = jnp.maximum(f, 1e-30)
    return hq * sq, sq, sf, f, fc, jnp.log(fc)


DECAY_CAP = 60.0


def _block_ref(c_ref, i):
    if i == 0:
        return jnp.zeros((1, HD), F32)
    return c_ref[SB * i - 1:SB * i, :]


def _block_decay(c_ref):
    spans = [_block_ref(c_ref, i) - c_ref[SB * (i + 1) - 1:SB * (i + 1), :] for i in range(CH // SB)]
    return functools.reduce(jnp.maximum, spans)


def _pair_factors(q_ref, k, c_ref, first, cap, round_bf16):
    nb = CH // SB
    c = c_ref[...]
    zero = jnp.zeros((SB, HD), F32)
    q_groups, k_groups, eqs, eks = [], [], [], []
    for i in range(first, nb):
        blk = slice(SB * i, SB * (i + 1))
        r_i = _block_ref(c_ref, i)
        eq = jnp.exp(jnp.minimum(c_ref[blk, :] - r_i, 0.0))
        ek = jnp.exp(jnp.minimum(r_i - c, cap))
        qi, kei = q_ref[blk, :] * eq, k * ek
        if round_bf16:
            qi, kei = qi.astype(BF16).astype(F32), kei.astype(BF16).astype(F32)
        q_groups.append(jnp.concatenate([zero] * i + [qi] + [zero] * (nb - 1 - i), axis=0))
        k_groups.append(kei)
        eqs.append(eq)
        eks.append(ek)
    return jnp.concatenate(q_groups, axis=1), jnp.concatenate(k_groups, axis=1), eqs, eks


def _pair_mask(rowi, coli, strict):
    return (coli < jnp.bitwise_and(rowi, -SB)) if strict else (coli <= rowi)


def _hgrn_fwd(proj, lb, gn, name, carry=None):
    S = proj.shape[0]
    nch = S // CH
    W = NH * HD

    def body(hq_ref, hf_ref, hi_ref, hg_ref, lb_ref, gn_ref, bin_ref, bint_ref, oraw_ref, st_ref, mild_ref,
             cum_ref, q_s, k_s, c_s, v_s, o_s, state_s, qf_s, kf_s, cf_s):
        state_s[...] = jnp.zeros((NH, HD, HD), F32)
        rowi = lax.broadcasted_iota(jnp.int32, (CH, 1), 0)
        coli = lax.broadcasted_iota(jnp.int32, (1, CH), 1)
        sbi = lax.broadcasted_iota(jnp.int32, (SB, 1), 0)
        gnv = gn_ref[...]

        def gates_pass(n, worst):
            rows = pl.ds(pl.multiple_of(n * CH, CH), CH)
            for hh in range(NH):
                lanes = slice(hh * HD, (hh + 1) * HD)
                q, _, _, f, _, logf = _gates(hq_ref[rows, lanes], hf_ref[rows, lanes], lb_ref[:, lanes])
                c = _chunk_cumsum(logf, rowi)
                qf_s[hh, rows, :] = q
                kf_s[hh, rows, :] = 1.0 - f
                cf_s[hh, rows, :] = c
                cum_ref[rows, lanes] = c
                c_s[hh] = c
                worst = jnp.maximum(worst, _block_decay(c_s.at[hh]))
            return worst

        def between_chunks(hh, n, rows):
            lanes = slice(hh * HD, (hh + 1) * HD)
            q = qf_s[hh, rows, :]
            k = kf_s[hh, rows, :]
            c = cf_s[hh, rows, :]
            v = hi_ref[rows, lanes].astype(F32)
            q_s[hh] = q
            k_s[hh] = k
            c_s[hh] = c
            v_s[hh] = v
            st = state_s[hh]
            st_ref[hh, n] = st.astype(BF16)
            o_s[hh] = _dot_nt(q * jnp.exp(c), st)
            last = c_s[hh, CH - 1:CH, :]
            state_s[hh] = st * jnp.exp(last) + _dot_tn(v, k * jnp.exp(last - c))

        def pairs_matmul(hh, first, cap, strict):
            qx, kc, _, _ = _pair_factors(q_s.at[hh], k_s[hh], c_s.at[hh], first, cap, round_bf16=False)
            a = jnp.where(_pair_mask(rowi, coli, strict), _dot_nt(qx, kc), 0.0)
            o_s[hh] += _dot_nn(a, v_s[hh])

        def within_chunk_matmul(hh):
            pairs_matmul(hh, 0, DECAY_CAP, strict=False)

        def within_chunk_exact(hh):
            pairs_matmul(hh, 1, 0.0, strict=True)
            for i in range(CH // SB):
                blk = slice(SB * i, SB * (i + 1))
                qb = q_s[hh, blk, :]
                cb = c_s[hh, blk, :]
                acc = jnp.zeros((SB, HD), F32)
                for s in range(SB):
                    row = SB * i + s
                    w = jnp.exp(jnp.minimum(cb - c_s[hh, row:row + 1, :], 0.0))
                    a_col = jnp.sum(qb * k_s[hh, row:row + 1, :] * w, axis=-1, keepdims=True)
                    acc = acc + jnp.where(sbi >= s, a_col, 0.0) * v_s[hh, row:row + 1, :]
                o_s[hh, blk, :] += acc

        def norm_and_gate(hh, rows):
            lanes = slice(hh * HD, (hh + 1) * HD)
            ov = o_s[hh]
            oraw_ref[rows, lanes] = ov
            r = lax.rsqrt(jnp.mean(ov * ov, axis=-1, keepdims=True) + EPS)
            hg = hg_ref[rows, lanes].astype(F32)
            bin_ref[rows, lanes] = ((ov * r) * gnv * (hg * _sig(hg))).astype(BF16)

        def chunk_with(within_chunk):
            def chunk(n, carry):
                rows = pl.ds(pl.multiple_of(n * CH, CH), CH)
                for hh in range(NH):
                    between_chunks(hh, n, rows)
                for hh in range(NH):
                    within_chunk(hh)
                for hh in range(NH):
                    norm_and_gate(hh, rows)
                return carry
            return chunk

        worst = lax.fori_loop(0, nch, gates_pass, jnp.zeros((1, HD), F32))
        mild = jnp.max(worst) <= DECAY_CAP
        mild_ref[...] = jnp.broadcast_to(jnp.where(mild, 1.0, 0.0), (8, HD))

        @pl.when(mild)
        def _():
            lax.fori_loop(0, nch, chunk_with(within_chunk_matmul), 0, unroll=4)

        @pl.when(jnp.logical_not(mild))
        def _():
            lax.fori_loop(0, nch, chunk_with(within_chunk_exact), 0)

        bint_ref[...] = bin_ref[...].astype(F32).T.astype(BF16)

    col = lambda off: pl.BlockSpec((S, W), lambda h: (0, off // NH + h))
    head = pl.BlockSpec((S, W), lambda h: (0, h))
    outs = _call(
        body, name=name, grid=(HEADS // NH,),
        in_specs=[col(HQ0), col(HF0), col(HI0), col(HG0), pl.BlockSpec((1, W), lambda h: (0, h)),
                  pl.BlockSpec((1, HD), lambda h: (0, 0))],
        out_specs=[head, pl.BlockSpec((W, S), lambda h: (h, 0)), head,
                   pl.BlockSpec((NH, nch, HD, HD), lambda h: (h, 0, 0, 0)),
                   pl.BlockSpec((8, HD), lambda h: (h, 0)), head],
        out_shape=[jax.ShapeDtypeStruct((S, D), BF16), jax.ShapeDtypeStruct((D, S), BF16),
                   jax.ShapeDtypeStruct((S, D), F32), jax.ShapeDtypeStruct((HEADS, nch, HD, HD), BF16),
                   jax.ShapeDtypeStruct((8 * HEADS // NH, HD), F32), jax.ShapeDtypeStruct((S, D), F32)],
        scratch_shapes=[pltpu.VMEM((NH, CH, HD), F32)] * 5 + [pltpu.VMEM((NH, HD, HD), F32)]
        + [pltpu.VMEM((NH, S, HD), F32)] * 3,
        sem=("parallel",), args=(proj, proj, proj, proj, lb, gn), carry=carry)
    return outs[:6], outs[6:]


def _hgrn_bwd(dbin, proj, oraw, states, mild, cum, lb, gn, dproj, name, carry=None):
    S = proj.shape[0]
    nch = S // CH
    W = NH * HD
    n_in = 12

    def body(*refs):
        ins, (dproj_ref, dlb_ref, dgn_ref) = refs[:n_in - 1], refs[n_in:n_in + 3]
        scratch, later = refs[n_in + 3:-3], refs[-3:]
        seg = pl.program_id(1)

        @pl.when(seg == 0)
        def _():
            heads(*ins, dproj_ref, *later, dlb_ref, dgn_ref, *scratch)

        for s, kept in enumerate(later):
            @pl.when(seg == s + 1)
            def _(kept=kept):
                dproj_ref[...] = kept[...]

    def heads(db_ref, hq_ref, hf_ref, hi_ref, hg_ref, or_ref, st_ref, mild_ref, cum_ref, lb_ref, gn_ref,
              dq_ref, df_ref, di_ref, dg_ref, dlb_ref, dgn_ref,
              q_s, k_s, c_s, v_s, do_s, dq_s, dk_s, dv_s, dc_s, dqd_s, dkd_s, f_s, sf_s, sq_s, dl_s, dst_s,
              dlb_s, dgn_s):
        dst_s[...] = jnp.zeros((NH, HD, HD), F32)
        dlb_s[...] = jnp.zeros((1, W), F32)
        dgn_s[...] = jnp.zeros((1, HD), F32)
        rowi = lax.broadcasted_iota(jnp.int32, (CH, 1), 0)
        coli = lax.broadcasted_iota(jnp.int32, (1, CH), 1)
        sbi = lax.broadcasted_iota(jnp.int32, (SB, 1), 0)
        gnv = gn_ref[...]
        def between_chunks(hh, n, rows):
            lanes = slice(hh * HD, (hh + 1) * HD)
            lbv = lb_ref[:, lanes]
            hq = hq_ref[rows, lanes].astype(F32)
            sq = _sig(hq)
            sf = _sig(hf_ref[rows, lanes].astype(F32))
            f = lbv + (1.0 - lbv) * sf
            q = hq * sq
            k = 1.0 - f
            f_s[hh] = f
            sf_s[hh] = sf
            sq_s[hh] = sq
            v = hi_ref[rows, lanes].astype(F32)
            c = cum_ref[rows, lanes]
            ov = or_ref[rows, lanes]
            hg = hg_ref[rows, lanes].astype(F32)
            sg = _sig(hg)
            r = lax.rsqrt(jnp.mean(ov * ov, axis=-1, keepdims=True) + EPS)
            dbv = db_ref[rows, lanes]
            d_on = dbv * (hg * sg)
            dg_ref[rows, lanes] = (dbv * ((ov * r) * gnv) * _dsilu(hg, sg)).astype(BF16)
            dgn_s[...] += jnp.sum(d_on * (ov * r), axis=0, keepdims=True)
            u = d_on * gnv
            do = r * u - ov * (r * r * r) * jnp.mean(u * ov, axis=-1, keepdims=True)
            q_s[hh] = q
            k_s[hh] = k
            c_s[hh] = c
            v_s[hh] = v
            do_s[hh] = do
            st = st_ref[hh, n].astype(F32)
            dst = dst_s[hh]
            ec = jnp.exp(c)
            last = c_s[hh, CH - 1:CH, :]
            el = jnp.exp(last - c)
            elast = jnp.exp(last)
            dq = _dot_nn(do, st) * ec
            dk = _dot_nn(v, dst) * el
            dq_s[hh] = dq
            dk_s[hh] = dk
            dv_s[hh] = _dot_nt(k * el, dst)
            dc_s[hh] = q * dq - k * dk
            dl_s[hh] = (jnp.sum(k * dk, axis=0, keepdims=True)
                        + elast * jnp.sum(st * dst, axis=0, keepdims=True))
            dst_s[hh] = dst * elast + _dot_tn(do, q * ec)

        def pairs_matmul(hh, first, cap, strict):
            do = do_s[hh]
            qx, kc, eqs, eks = _pair_factors(q_s.at[hh], k_s[hh], c_s.at[hh], first, cap, round_bf16=True)
            mask = _pair_mask(rowi, coli, strict)
            a = jnp.where(mask, _dot_nt(qx, kc), 0.0)
            d_a = jnp.where(mask, _dot_nt(do, v_s[hh]).astype(BF16).astype(F32), 0.0)
            dqx = _dot_nn(d_a, kc)
            dkc = _dot_tn(d_a, qx)
            dv_s[hh] += _dot_tn(a, do)
            dk, dcum = dk_s[hh], dc_s[hh]
            dq_slabs = [jnp.zeros((SB, HD), F32)] * first
            dc_slabs = [jnp.zeros((SB, HD), F32)] * first
            for g, (eq, ek) in enumerate(zip(eqs, eks)):
                rows = slice(SB * (first + g), SB * (first + g + 1))
                cols = slice(HD * g, HD * (g + 1))
                dq_i = dqx[rows, cols]
                dk_i = dkc[:, cols]
                dq_slabs.append(dq_i * eq)
                dc_slabs.append(qx[rows, cols] * dq_i)
                dk = dk + dk_i * ek
                dcum = dcum - kc[:, cols] * dk_i
            dq_s[hh] += jnp.concatenate(dq_slabs, axis=0)
            dk_s[hh] = dk
            dc_s[hh] = dcum + jnp.concatenate(dc_slabs, axis=0)

        def pairs_exact(hh):
            dqd_s[hh] = jnp.zeros((CH, HD), F32)
            dkd_s[hh] = jnp.zeros((CH, HD), F32)
            for i in range(CH // SB):
                blk = slice(SB * i, SB * (i + 1))
                qb = q_s[hh, blk, :]
                cb = c_s[hh, blk, :]
                dob = do_s[hh, blk, :]
                dq_acc = jnp.zeros((SB, HD), F32)
                for s in range(SB):
                    row = SB * i + s
                    ks = k_s[hh, row:row + 1, :]
                    vs = v_s[hh, row:row + 1, :]
                    w = jnp.exp(jnp.minimum(cb - c_s[hh, row:row + 1, :], 0.0))
                    live = sbi >= s
                    a_col = jnp.where(live, jnp.sum(qb * ks * w, axis=-1, keepdims=True), 0.0)
                    da_col = jnp.where(live, jnp.sum(dob * vs, axis=-1, keepdims=True), 0.0)
                    dq_acc = dq_acc + da_col * ks * w
                    dkd_s[hh, row:row + 1, :] += jnp.sum(da_col * qb * w, axis=0, keepdims=True)
                    dv_s[hh, row:row + 1, :] += jnp.sum(a_col * dob, axis=0, keepdims=True)
                dqd_s[hh, blk, :] += dq_acc
            dq_d = dqd_s[hh]
            dk_d = dkd_s[hh]
            dq_s[hh] += dq_d
            dk_s[hh] += dk_d
            dc_s[hh] += q_s[hh] * dq_d - k_s[hh] * dk_d

        def gate_grads(hh, rows):
            lanes = slice(hh * HD, (hh + 1) * HD)
            lbv = lb_ref[:, lanes]
            hq = hq_ref[rows, lanes].astype(F32)
            f, sf, sq = f_s[hh], sf_s[hh], sq_s[hh]
            dlogf = _chunk_rev_cumsum(dc_s[hh], rowi) + dl_s[hh]
            dfv = jnp.where(f > 1e-30, dlogf / jnp.maximum(f, 1e-30), 0.0) - dk_s[hh]
            dlb_s[:, lanes] += jnp.sum(dfv * (1.0 - sf), axis=0, keepdims=True)
            df_ref[rows, lanes] = (dfv * (1.0 - lbv) * sf * (1.0 - sf)).astype(BF16)
            dq_ref[rows, lanes] = (dq_s[hh] * _dsilu(hq, sq)).astype(BF16)
            di_ref[rows, lanes] = dv_s[hh].astype(BF16)

        def chunk_with(pairs):
            def chunk(j, carry):
                n = nch - 1 - j
                rows = pl.ds(pl.multiple_of(n * CH, CH), CH)
                for hh in range(NH):
                    between_chunks(hh, n, rows)
                for hh in range(NH):
                    pairs(hh)
                for hh in range(NH):
                    gate_grads(hh, rows)
                return carry
            return chunk

        def pairs_mild(hh):
            pairs_matmul(hh, 0, DECAY_CAP, strict=False)

        def pairs_any(hh):
            pairs_matmul(hh, 1, 0.0, strict=True)
            pairs_exact(hh)

        mild = jnp.max(mild_ref[...]) > 0.5

        @pl.when(mild)
        def _():
            lax.fori_loop(0, nch, chunk_with(pairs_mild), 0, unroll=2)

        @pl.when(jnp.logical_not(mild))
        def _():
            lax.fori_loop(0, nch, chunk_with(pairs_any), 0)

        dlb_ref[...] = dlb_s[...]
        dgn_ref[...] = jnp.broadcast_to(dgn_s[...], (8, HD))

    ahead = lambda h, s: jnp.minimum(h + jnp.minimum(s, 1), HEADS // NH - 1)
    col = lambda off: pl.BlockSpec((S, W), lambda h, s: (0, off // NH + ahead(h, s)))
    head_in = pl.BlockSpec((S, W), lambda h, s: (0, ahead(h, s)))
    vec_in = pl.BlockSpec((1, W), lambda h, s: (0, ahead(h, s)))
    vec = pl.BlockSpec((1, W), lambda h, s: (0, h))
    seg_w = (HF0 - HQ0) // NH
    outs = _call(
        body, name=name, grid=(HEADS // NH, 4),
        in_specs=[head_in, col(HQ0), col(HF0), col(HI0), col(HG0), head_in,
                  pl.BlockSpec((NH, nch, HD, HD), lambda h, s: (ahead(h, s), 0, 0, 0)),
                  pl.BlockSpec((8, HD), lambda h, s: (ahead(h, s), 0)), head_in, vec_in,
                  pl.BlockSpec((1, HD), lambda h, s: (0, 0)), ANY],
        out_specs=[pl.BlockSpec((S, W), lambda h, s: (0, HQ0 // NH + seg_w * s + h)), vec,
                   pl.BlockSpec((8, HD), lambda h, s: (h, 0))],
        out_shape=[jax.ShapeDtypeStruct(dproj.shape, dproj.dtype), jax.ShapeDtypeStruct((1, D), F32),
                   jax.ShapeDtypeStruct((8 * HEADS // NH, HD), F32)],
        scratch_shapes=[pltpu.VMEM((NH, CH, HD), F32)] * 14
        + [pltpu.VMEM((NH, 1, HD), F32), pltpu.VMEM((NH, HD, HD), F32), pltpu.VMEM((1, W), F32),
           pltpu.VMEM((1, HD), F32)] + [pltpu.VMEM((S, W), BF16)] * 3,
        sem=("arbitrary", "arbitrary"), aliases={n_in - 1: 0},
        args=(dbin, proj, proj, proj, proj, oraw, states, mild, cum, lb, gn, dproj), carry=carry)
    dproj, dlb, dgn = outs[:3]
    return (dproj, dlb, dgn.reshape(HEADS // NH, 8, HD)[:, 0, :]), outs[3:]


def _lower_bounds(l0, l1):
    m = jnp.maximum(l0, l1)
    e0 = jnp.exp(l0 - m)
    e1 = jnp.exp(l1 - m)
    tot = e0 + e1
    p0 = e0 / tot
    p1 = e1 / tot
    return jnp.clip(p0 - p0, 0.0, 1.0), jnp.clip((p0 + p1) - p0, 0.0, 1.0)


def _lb_fwd(logits):
    def body(l_ref, o_ref):
        lb0, lb1 = _lower_bounds(l_ref[0:1, :], l_ref[1:2, :])
        o_ref[0:1, :] = lb0
        o_ref[1:2, :] = lb1

    return pl.pallas_call(body, name="lb_fwd", out_shape=jax.ShapeDtypeStruct((2, D), F32))(logits)


def _lb_bwd(logits, dlb):
    def body(l_ref, d_ref, o_ref):
        _, vjp = jax.vjp(_lower_bounds, l_ref[0:1, :], l_ref[1:2, :])
        g0, g1 = vjp((d_ref[0:1, :], d_ref[1:2, :]))
        o_ref[0:1, :] = g0
        o_ref[1:2, :] = g1

    return pl.pallas_call(body, name="lb_bwd", out_shape=jax.ShapeDtypeStruct((2, D), F32))(logits, dlb)


ADA_PAD = 128


def _ada_fwd(c_pad, w_ada, b_sh):
    ns = w_ada.shape[2]

    def body(c_ref, w_ref, b_ref, o_ref):
        cv = c_ref[...]
        ca = (cv * _sig(cv)).astype(BF16)
        for l in range(2):
            res = jnp.dot(ca, w_ref[l].astype(BF16), preferred_element_type=F32)
            o_ref[:, l * ns:(l + 1) * ns] = res[0:NDEV, :] + b_ref[l:l + 1, :]

    return pl.pallas_call(body, name="ada_fwd", out_shape=jax.ShapeDtypeStruct((NDEV, 2 * ns), F32),
                          compiler_params=_cp())(c_pad, w_ada, b_sh)


def _ada_wgrad(c_pad_t, d_ada_sh):
    ns = d_ada_sh.shape[2]

    def body(c_ref, d_ref, o_ref):
        cv = c_ref[...]
        ca = (cv * _sig(cv)).astype(BF16)
        for l in range(2):
            o_ref[l] = jnp.dot(ca, d_ref[l].astype(BF16), preferred_element_type=F32)

    return pl.pallas_call(body, name="ada_wgrad", out_shape=jax.ShapeDtypeStruct((2, D, ns), F32),
                          compiler_params=_cp())(c_pad_t, d_ada_sh)


def _sum_devices(g):
    _, R, C = g.shape

    def body(g_ref, o_ref):
        acc = g_ref[0]
        for d in range(1, NDEV):
            acc = acc + g_ref[d]
        o_ref[...] = acc

    return pl.pallas_call(body, name="sum_devices", out_shape=jax.ShapeDtypeStruct((R, C), F32),
                          compiler_params=_cp())(g)


def _adamw(w, g, m, v, name, carry=None):
    R, C = w.shape
    tr = _row_tile(R, max(8, (1 << 19) // C))

    def body(w_ref, g_ref, m_ref, v_ref, d_ref, nm_ref, nv_ref):
        d_ref[...], nm_ref[...], nv_ref[...] = _adamw_update(w_ref[...], g_ref[...], m_ref[...], v_ref[...])

    tile = pl.BlockSpec((tr, C), lambda i: (i, 0))
    return _call(body, name=name, grid=(R // tr,), in_specs=[tile] * 4, out_specs=[tile] * 3,
                 out_shape=[jax.ShapeDtypeStruct((R, C), F32)] * 3, sem=("parallel",), args=(w, g, m, v),
                 carry=carry)


def _adamw_update(w, g, m, v):
    nm = B1 * m + (1.0 - B1) * g
    nv = B2 * v + (1.0 - B2) * (g * g)
    m_hat = nm / (1.0 - B1 ** STEP)
    v_hat = nv / (1.0 - B2 ** STEP)
    return -LR * (m_hat / (jnp.sqrt(v_hat) + AEPS) + WD * w), nm, nv


SMALL_PARTS = (("b_ada", 0, 6, D), ("g_pre", 8, 2, D), ("g_post", 16, 2, D), ("lb_logits", 24, 2, D),
               ("pool_w", 32, 128, D), ("pool_scale", 160, 1, D), ("hgrn_norm_g", 168, 1, 2 * HD))


def _adamw_small(g_small, g_lb_logits, wmv):
    n = len(SMALL_PARTS)

    def body(g_ref, glb_ref, *refs):
        ins, outs = refs[:3 * n], refs[3 * n:]
        for p, (key, row0, rows, width) in enumerate(SMALL_PARTS):
            gv = glb_ref[...] if key == "lb_logits" else g_ref[row0:row0 + rows, 0:width]
            res = _adamw_update(ins[3 * p][...], gv, ins[3 * p + 1][...], ins[3 * p + 2][...])
            for t in range(3):
                outs[3 * p + t][...] = res[t]

    flat = [t for triple in wmv for t in triple]
    outs = pl.pallas_call(body, name="adamw_small",
                          out_shape=[jax.ShapeDtypeStruct(t.shape, F32) for t in flat],
                          compiler_params=_cp())(g_small, g_lb_logits, *flat)
    return [outs[3 * p:3 * p + 3] for p in range(n)]


def _cast_to_slot(place, w, l, name):
    _, R, C = w.shape
    tr = _row_tile(R, max(8, (1 << 19) // C))

    def body(p_ref, w_ref, o_ref):
        o_ref[...] = w_ref[...].astype(BF16)

    return pl.pallas_call(
        body, name=name, out_shape=jax.ShapeDtypeStruct((NCHIP, R, C), BF16),
        grid_spec=pltpu.PrefetchScalarGridSpec(
            num_scalar_prefetch=1, grid=(R // tr,),
            in_specs=[pl.BlockSpec((None, tr, C), lambda i, p_ref: (l, i, 0))],
            out_specs=pl.BlockSpec((None, tr, C), lambda i, p_ref: (p_ref[0], i, 0))),
        compiler_params=_cp(("parallel",)),
    )(place, w)


def _pair_add(core, g, got, name):
    _, R, C = g.shape
    r2 = R // 2
    tr = _row_tile(r2, max(8, (1 << 19) // C))
    nt = r2 // tr

    def body(c_ref, a_ref, b_ref, o_ref):
        o_ref[...] = (a_ref[...].astype(F32) + b_ref[...].astype(F32)).astype(o_ref.dtype)

    return pl.pallas_call(
        body, name=name, out_shape=jax.ShapeDtypeStruct((NCHIP, r2, C), BF16),
        grid_spec=pltpu.PrefetchScalarGridSpec(
            num_scalar_prefetch=1, grid=(NCHIP, nt),
            in_specs=[pl.BlockSpec((None, tr, C), lambda j, i, c_ref: (j, c_ref[0] * nt + i, 0)),
                      pl.BlockSpec((None, tr, C), lambda j, i, c_ref: (j, i, 0))],
            out_specs=pl.BlockSpec((None, tr, C), lambda j, i, c_ref: (j, i, 0))),
        compiler_params=_cp(("parallel", "parallel")),
    )(core, g, got)


def _chip_sum(place, part, recv, layer, both, name):
    _, r2, C = part.shape
    tr = _row_tile(r2, max(8, (1 << 18) // C))
    nt = r2 // tr

    def body(p_ref, own_ref, r_ref, *rest):
        o_ref = rest[-1]
        me = p_ref[0]
        own = own_ref[...].astype(F32)
        acc = None
        for j in range(NCHIP):
            slot = jnp.minimum(jnp.where(j > me, j - 1, j), NCHIP - 2)
            term = jnp.where(me == j, own, r_ref[slot].astype(F32))
            acc = term if acc is None else acc + term
        o_ref[...] = acc

    args = (place, part, recv) if both is None else (place, part, recv, both)
    return pl.pallas_call(
        body, name=name, out_shape=jax.ShapeDtypeStruct((2, 2 * r2, C), F32),
        grid_spec=pltpu.PrefetchScalarGridSpec(
            num_scalar_prefetch=1, grid=(nt,),
            in_specs=[pl.BlockSpec((None, tr, C), lambda i, p_ref: (p_ref[0], i, 0)),
                      pl.BlockSpec((NCHIP - 1, tr, C), lambda i, p_ref: (0, i, 0))] + [ANY] * (len(args) - 3),
            out_specs=pl.BlockSpec((None, tr, C), lambda i, p_ref: (layer, p_ref[1] * nt + i, 0))),
        input_output_aliases={} if both is None else {3: 0},
        compiler_params=_cp(("parallel",)),
    )(*args)


def _place():
    x, y, c = lax.axis_index("x"), lax.axis_index("y"), lax.axis_index("c")
    chips = [(1 - x, y), (x, 1 - y), (1 - x, 1 - y)]
    return x, y, c, chips


def _gather_small(blk, name):
    m_per, n = blk.shape

    def body(x_ref, out_ref, send_sems, recv_sems, local_sem):
        x, y, c, chips = _place()
        me, sibling = (x, y, c), (x, y, 1 - c)

        def rows(px, py, pc):
            return out_ref.at[pl.ds((4 * px + 2 * py + pc) * m_per, m_per), :]

        def copy(k, block, to, src=None):
            return pltpu.make_async_remote_copy(
                src_ref=rows(*block) if src is None else src, dst_ref=rows(*block),
                send_sem=send_sems.at[k], recv_sem=recv_sems.at[k], device_id=to, device_id_type=MESH)

        mine = pltpu.make_async_copy(x_ref, rows(*me), local_sem)
        mine.start()
        first = [copy(0, me, sibling, src=x_ref)]
        first += [copy(1 + j, me, (*chip, c), src=x_ref) for j, chip in enumerate(chips)]
        for cp in first:
            cp.start()
        passed = [copy(4 + j, (*chip, c), sibling) for j, chip in enumerate(chips)]
        for j, chip in enumerate(chips):
            copy(1 + j, (*chip, c), me).wait_recv()
            passed[j].start()
        copy(0, sibling, me).wait_recv()
        for j, chip in enumerate(chips):
            copy(4 + j, (*chip, 1 - c), me).wait_recv()
        for cp in first + passed:
            cp.wait_send()
        mine.wait()

    return pl.pallas_call(
        body, name=name, out_shape=jax.ShapeDtypeStruct((NDEV * m_per, n), blk.dtype),
        in_specs=[pl.BlockSpec(memory_space=pltpu.VMEM)], out_specs=pl.BlockSpec(memory_space=pltpu.VMEM),
        scratch_shapes=[pltpu.SemaphoreType.DMA((7,)), pltpu.SemaphoreType.DMA((7,)), pltpu.SemaphoreType.DMA],
        compiler_params=_cp(),
    )(blk)


def _gather_rows_carry(blk):
    m_per, n = blk.shape

    def rows(ref, px, py, pc):
        return ref.at[pl.ds((4 * px + 2 * py + pc) * m_per, m_per), :]

    def copy(ins, outs, send_sems, recv_sems, k, block, to, own=False):
        return pltpu.make_async_remote_copy(
            src_ref=ins[0] if own else rows(outs[0], *block), dst_ref=rows(outs[0], *block),
            send_sem=send_sems.at[k], recv_sem=recv_sems.at[k], device_id=to, device_id_type=MESH)

    def mine(ins, outs, send_sems):
        x, y, c, _ = _place()
        return pltpu.make_async_copy(ins[0], rows(outs[0], x, y, c), send_sems.at[7])

    def start(ins, outs, send_sems, recv_sems):
        x, y, c, chips = _place()
        mine(ins, outs, send_sems).start()
        copy(ins, outs, send_sems, recv_sems, 0, (x, y, c), (x, y, 1 - c), own=True).start()
        for j, chip in enumerate(chips):
            copy(ins, outs, send_sems, recv_sems, 1 + j, (x, y, c), (*chip, c), own=True).start()

    def finish(ins, outs, send_sems, recv_sems):
        x, y, c, chips = _place()
        for j, chip in enumerate(chips):
            copy(ins, outs, send_sems, recv_sems, 1 + j, (*chip, c), (x, y, c)).wait_recv()
            copy(ins, outs, send_sems, recv_sems, 4 + j, (*chip, c), (x, y, 1 - c)).start()
        copy(ins, outs, send_sems, recv_sems, 0, (x, y, 1 - c), (x, y, c)).wait_recv()
        for j, chip in enumerate(chips):
            copy(ins, outs, send_sems, recv_sems, 4 + j, (*chip, 1 - c), (x, y, c)).wait_recv()
        copy(ins, outs, send_sems, recv_sems, 0, (x, y, c), (x, y, 1 - c), own=True).wait_send()
        for j, chip in enumerate(chips):
            copy(ins, outs, send_sems, recv_sems, 1 + j, (x, y, c), (*chip, c), own=True).wait_send()
            copy(ins, outs, send_sems, recv_sems, 4 + j, (*chip, c), (x, y, 1 - c)).wait_send()
        mine(ins, outs, send_sems).wait()

    return _Carry([blk], [jax.ShapeDtypeStruct((NDEV * m_per, n), blk.dtype)], {}, 8, start, finish)


def _gather_carry(shards, piece=(0, 1, 1)):
    n = len(shards)
    first, count, of = piece

    def rows(ref, half):
        r2 = ref.shape[1] // 2
        return pl.ds(half * r2 + first * (r2 // of), count * (r2 // of))

    def over_ici(outs, send_sems, recv_sems, a, j, chip_xy, slot):
        x, y, c, _ = _place()
        blk = outs[a].at[slot, rows(outs[a], c), :]
        return pltpu.make_async_remote_copy(
            src_ref=blk, dst_ref=blk, send_sem=send_sems.at[6 * a + j], recv_sem=recv_sems.at[6 * a + j],
            device_id=(*chip_xy, c), device_id_type=MESH)

    def over_d2d(outs, send_sems, recv_sems, a, j, slot, half):
        x, y, c, _ = _place()
        blk = outs[a].at[slot, rows(outs[a], half), :]
        return pltpu.make_async_remote_copy(
            src_ref=blk, dst_ref=blk, send_sem=send_sems.at[6 * a + 3 + j], recv_sem=recv_sems.at[6 * a + 3 + j],
            device_id=(x, y, 1 - c), device_id_type=MESH)

    def start(ins, outs, send_sems, recv_sems):
        x, y, c, chips = _place()
        for a in range(n):
            for j, chip_xy in enumerate(chips):
                over_ici(outs, send_sems, recv_sems, a, j, chip_xy, 2 * x + y).start()

    def finish(ins, outs, send_sems, recv_sems):
        x, y, c, chips = _place()
        for a in range(n):
            for j, (cx, cy) in enumerate(chips):
                over_ici(outs, send_sems, recv_sems, a, j, (cx, cy), 2 * cx + cy).wait_recv()
                over_d2d(outs, send_sems, recv_sems, a, j, 2 * cx + cy, c).start()
        for a in range(n):
            for j, (cx, cy) in enumerate(chips):
                over_d2d(outs, send_sems, recv_sems, a, j, 2 * cx + cy, 1 - c).wait_recv()
        for a in range(n):
            for j, (cx, cy) in enumerate(chips):
                over_ici(outs, send_sems, recv_sems, a, j, (cx, cy), 2 * x + y).wait_send()
                over_d2d(outs, send_sems, recv_sems, a, j, 2 * cx + cy, c).wait_send()

    return _Carry(shards, [jax.ShapeDtypeStruct(s.shape, s.dtype) for s in shards],
                  {a: a for a in range(n)}, 6 * n, start, finish)


def _rs_pair(grads, name):
    n = len(grads)

    def body(*refs):
        ins, gots = refs[:n], refs[n:2 * n]
        send_sems, recv_sems = refs[2 * n:]
        x, y, c, _ = _place()
        cps = []
        for a in range(n):
            r2 = ins[a].shape[1] // 2
            cp = pltpu.make_async_remote_copy(
                src_ref=ins[a].at[:, pl.ds((1 - c) * r2, r2), :], dst_ref=gots[a],
                send_sem=send_sems.at[a], recv_sem=recv_sems.at[a],
                device_id=(x, y, 1 - c), device_id_type=MESH)
            cp.start()
            cps.append(cp)
        for cp in cps:
            cp.wait()

    half = [jax.ShapeDtypeStruct((NCHIP, g.shape[1] // 2, g.shape[2]), g.dtype) for g in grads]
    return pl.pallas_call(
        body, name=name, out_shape=half, in_specs=[ANY] * n, out_specs=[ANY] * n,
        scratch_shapes=[pltpu.SemaphoreType.DMA((n,)), pltpu.SemaphoreType.DMA((n,))],
        compiler_params=_cp(),
    )(*grads)


def _chips_carry(parts, piece=(0, 1, 1), into=None):
    n = len(parts)
    first, count, of = piece

    def rows(ref):
        step = ref.shape[1] // of
        return pl.ds(first * step, count * step)

    def send(ins, outs, send_sems, recv_sems, a, j, chip_xy):
        x, y, c, _ = _place()
        me, them = 2 * x + y, 2 * chip_xy[0] + chip_xy[1]
        return pltpu.make_async_remote_copy(
            src_ref=ins[a].at[them, rows(ins[a]), :],
            dst_ref=outs[a].at[me - (me > them).astype(jnp.int32), rows(outs[a]), :],
            send_sem=send_sems.at[3 * a + j], recv_sem=recv_sems.at[3 * a + j],
            device_id=(*chip_xy, c), device_id_type=MESH)

    def start(ins, outs, send_sems, recv_sems):
        _, _, _, chips = _place()
        for a in range(n):
            for j, chip_xy in enumerate(chips):
                send(ins, outs, send_sems, recv_sems, a, j, chip_xy).start()

    def finish(ins, outs, send_sems, recv_sems):
        x, y, c, chips = _place()
        me = 2 * x + y
        for a in range(n):
            for j, (cx, cy) in enumerate(chips):
                them = 2 * cx + cy
                blk = outs[a].at[them - (them > me).astype(jnp.int32), rows(outs[a]), :]
                pltpu.make_async_remote_copy(
                    src_ref=blk, dst_ref=blk, send_sem=send_sems.at[3 * a + j], recv_sem=recv_sems.at[3 * a + j],
                    device_id=(cx, cy, c), device_id_type=MESH).wait_recv()
        for a in range(n):
            for j, chip_xy in enumerate(chips):
                send(ins, outs, send_sems, recv_sems, a, j, chip_xy).wait_send()

    landing = [jax.ShapeDtypeStruct((NCHIP - 1,) + p.shape[1:], p.dtype) for p in parts]
    if into is None:
        return _Carry(parts, landing, {}, 3 * n, start, finish)
    return _Carry(list(parts) + list(into), landing, {n + a: a for a in range(n)}, 3 * n, start, finish)


def _rs_swap(fulls):
    n = len(fulls)

    def body(*refs):
        outs = refs[n:2 * n]
        send_sems, recv_sems = refs[2 * n:]
        x, y, c, _ = _place()
        cps = []
        for a in range(n):
            r2 = outs[a].shape[1] // 2
            mine = outs[a].at[:, pl.ds(c * r2, r2), :]
            cp = pltpu.make_async_remote_copy(
                src_ref=mine, dst_ref=mine, send_sem=send_sems.at[a], recv_sem=recv_sems.at[a],
                device_id=(x, y, 1 - c), device_id_type=MESH)
            cp.start()
            cps.append(cp)
        for a in range(n):
            r2 = outs[a].shape[1] // 2
            blk = outs[a].at[:, pl.ds((1 - c) * r2, r2), :]
            pltpu.make_async_remote_copy(
                src_ref=blk, dst_ref=blk, send_sem=send_sems.at[a], recv_sem=recv_sems.at[a],
                device_id=(x, y, 1 - c), device_id_type=MESH).wait_recv()
        for cp in cps:
            cp.wait_send()

    return pl.pallas_call(
        body, name="rs_swap", out_shape=[jax.ShapeDtypeStruct(f.shape, f.dtype) for f in fulls],
        in_specs=[ANY] * n, out_specs=[ANY] * n, input_output_aliases={a: a for a in range(n)},
        scratch_shapes=[pltpu.SemaphoreType.DMA((n,)), pltpu.SemaphoreType.DMA((n,))],
        compiler_params=_cp(),
    )(*fulls)


def _tail_weight_grads(merged_t, b_in_t, a_in_t, dy, dbr_b, dbr_a, name, tn=256):
    S = dy.shape[0]
    nn = D // tn

    def body(mt_ref, bt_ref, at_ref, dy_ref, db_ref, da_ref, go_ref, gh_ref, gp_ref):
        go_ref[...] = jnp.dot(mt_ref[...], dy_ref[...], preferred_element_type=F32).astype(BF16)
        gh_ref[...] = jnp.dot(bt_ref[...], db_ref[...], preferred_element_type=F32).astype(BF16)
        gp_ref[...] = jnp.dot(at_ref[...], da_ref[...], preferred_element_type=F32).astype(BF16)

    left = lambda rows: pl.BlockSpec((rows, S), lambda n: (0, 0))
    right = pl.BlockSpec((S, tn), lambda n: (0, n))
    out = pl.BlockSpec((D, tn), lambda n: (0, n))
    return pl.pallas_call(
        body, name=name, grid=(nn,), in_specs=[left(D), left(D), left(POOL_W), right, right, right],
        out_specs=[out, out, pl.BlockSpec((None, POOL_W, tn), lambda n: (n, 0, 0))],
        out_shape=[jax.ShapeDtypeStruct((D, D), BF16), jax.ShapeDtypeStruct((D, D), BF16),
                   jax.ShapeDtypeStruct((NCHIP, POOL_W, D // NCHIP), BF16)],
        compiler_params=_cp(("parallel",)),
    )(merged_t, b_in_t, a_in_t, dy, dbr_b, dbr_a)


class _GatherInProj:
    def __init__(self, slot, order):
        self.slot, self.order = slot, order


def _proj_with_gather(h, w_slot, order, name, tn=256):
    S, K = h.shape
    nsh, _, ns = w_slot.shape
    tps = ns // tn
    nt = nsh * tps
    r2 = K // 2

    def body(ord_ref, h_ref, w_in_ref, o_ref, w_ref, wbuf, tile_sems, send_sems, recv_sems):
        n = pl.program_id(0)
        x, y, c, chips = _place()

        def half(slot, which):
            return w_ref.at[slot, pl.ds(which * r2, r2), :]

        def over_ici(j, slot):
            blk = half(slot, c)
            return pltpu.make_async_remote_copy(src_ref=blk, dst_ref=blk, send_sem=send_sems.at[j],
                                                recv_sem=recv_sems.at[j], device_id=(*chips[j], c),
                                                device_id_type=MESH)

        def over_d2d(j, which):
            blk = half(2 * chips[j][0] + chips[j][1], which)
            return pltpu.make_async_remote_copy(src_ref=blk, dst_ref=blk, send_sem=send_sems.at[3 + j],
                                                recv_sem=recv_sems.at[3 + j], device_id=(x, y, 1 - c),
                                                device_id_type=MESH)

        def tile_copy(step, slot):
            shard = ord_ref[step // tps]
            return pltpu.make_async_copy(w_ref.at[shard, :, pl.ds((step % tps) * tn, tn)], wbuf.at[slot],
                                         tile_sems.at[slot])

        @pl.when(n == 0)
        def _():
            for j in range(3):
                over_ici(j, 2 * x + y).start()
            tile_copy(0, 0).start()

        for j in range(3):
            @pl.when(n == (j + 1) * tps - 1)
            def _(j=j):
                over_ici(j, 2 * chips[j][0] + chips[j][1]).wait_recv()
                over_d2d(j, c).start()
                over_d2d(j, 1 - c).wait_recv()

        @pl.when(n + 1 < nt)
        def _():
            tile_copy(n + 1, (n + 1) % 2).start()

        tile_copy(n, n % 2).wait()
        o_ref[...] = jnp.dot(h_ref[...], wbuf[n % 2], preferred_element_type=F32).astype(o_ref.dtype)

        @pl.when(n == nt - 1)
        def _():
            for j in range(3):
                over_ici(j, 2 * x + y).wait_send()
                over_d2d(j, c).wait_send()

    return pl.pallas_call(
        body, name=name,
        out_shape=[jax.ShapeDtypeStruct((S, nsh * ns), BF16), jax.ShapeDtypeStruct(w_slot.shape, w_slot.dtype)],
        grid_spec=pltpu.PrefetchScalarGridSpec(
            num_scalar_prefetch=1, grid=(nt,),
            in_specs=[pl.BlockSpec((S, K), lambda n, o_ref: (0, 0)), ANY],
            out_specs=[pl.BlockSpec((S, tn), lambda n, o_ref: (0, o_ref[n // tps] * tps + n % tps)), ANY],
            scratch_shapes=[pltpu.VMEM((2, K, tn), w_slot.dtype), pltpu.SemaphoreType.DMA((2,)),
                            pltpu.SemaphoreType.DMA((6,)), pltpu.SemaphoreType.DMA((6,))]),
        input_output_aliases={2: 1},
        compiler_params=_cp(("arbitrary",)),
    )(order, h, w_slot)


def _mm_ride(a, b, carry, **kw):
    if carry is None:
        return _mm(a, b, **kw), []
    return _mm(a, b, carry=carry, **kw)


def _layer_fwd(l, x, ada, w, small, ride, target=None):
    shift, scale, gate = ada[:, 0:D], ada[:, D:2 * D], ada[:, 2 * D:3 * D]
    carry, landed = ride("prenorm")
    (h, h_t), outs = _prenorm_fwd(x, small["g_pre"][l], scale, shift, f"prenorm_fwd{l}", carry)
    landed(outs)
    carry, landed = ride("proj")
    if isinstance(carry, _GatherInProj):
        proj, full = _proj_with_gather(h, carry.slot, carry.order, f"proj{l}")
        outs = [full]
    else:
        proj, outs = _mm_ride(h, w["w_in"][l], carry, name=f"proj{l}", b_mode="nn_sh", tm=2048, out_dtype=BF16)
    landed(outs)
    a_in, a_in_t = _pool_fwd(proj, small["pool_w"][l], small["pool_scale"][l], f"pool_fwd{l}")
    carry, landed = ride("hgrn")
    (b_in, b_in_t, o_raw, states, mild, cum), outs = _hgrn_fwd(proj, small["lb"][l], small["hgrn_norm_g"][l],
                                                              f"hgrn_fwd{l}", carry=carry)
    landed(outs)
    carry, landed = ride("tail")
    (br_a, br_b, merged_t, y, *x_new), outs = _layer_tail_fwd(
        proj, a_in, b_in, x, w["w_pool_o"][l], w["w_hgrn_o"][l].reshape(D, D), w["w_out"][l].reshape(D, D),
        gate, small["g_post"][l], f"tail_fwd{l}", target=target, carry=carry)
    landed(outs)
    saved = dict(x=x, h_t=h_t, proj=proj, a_in_t=a_in_t, b_in_t=b_in_t, o_raw=o_raw, states=states, mild=mild,
                 cum=cum,
                 br_a=br_a, br_b=br_b, merged_t=merged_t, y=y, scale=scale, gate=gate)
    return x_new, saved


def _layer_bwd(l, dxn, sv, w, small, ride):
    dy, dbr_a, dbr_b, dproj, da_in, db_in, dgate, dg_post = _layer_head_bwd(
        dxn, sv["y"], sv["proj"], sv["br_a"], sv["br_b"], w["w_pool_o"][l], w["w_hgrn_o"][l].reshape(D, D),
        w["w_out"][l].reshape(D, D), sv["gate"], small["g_post"][l], f"head_bwd{l}")
    gw_out, gw_hgrn_o, gw_pool_o = _tail_weight_grads(sv["merged_t"], sv["b_in_t"], sv["a_in_t"], dy, dbr_b,
                                                      dbr_a, f"gw_tail{l}")
    big = dict(w_pool_o=gw_pool_o, w_hgrn_o=gw_hgrn_o.reshape(NCHIP, D // NCHIP, D),
               w_out=gw_out.reshape(NCHIP, D // NCHIP, D))
    carry, landed = ride["hgrn"](big)
    (dproj, dlb, dgn), outs = _hgrn_bwd(db_in, sv["proj"], sv["o_raw"], sv["states"], sv["mild"], sv["cum"],
                                        small["lb"][l], small["hgrn_norm_g"][l], dproj, f"hgrn_bwd{l}",
                                        carry=carry)
    landed(outs)
    dproj, dpw, dpsc = _pool_bwd(da_in, sv["proj"], small["pool_w"][l], small["pool_scale"][l], dproj,
                                 f"pool_bwd{l}")
    little = dict(dgate=dgate, g_post=dg_post, pool_w=dpw, pool_scale=dpsc, lb=dlb,
                  hgrn_norm_g=jnp.sum(dgn, axis=0, keepdims=True))
    carry, landed = ride["gw_in"](little)
    big["w_in"], outs = _mm_ride(sv["h_t"], dproj, carry, name=f"gw_in{l}", out_shards=NCHIP, out_dtype=BF16)
    landed(outs)
    carry, landed = ride["d_h"](big)
    dh, outs = _mm_ride(dproj, w["w_in"][l], carry, name=f"d_h{l}", b_mode="nt_shk", tn=1024)
    landed(outs)
    carry, landed = ride["prenorm"](big)
    (dx, dshift, dscale, dg_pre), outs = _prenorm_bwd(dh, dxn, sv["x"], small["g_pre"][l], sv["scale"],
                                                      f"prenorm_bwd{l}", carry)
    landed(outs)
    little.update(dshift=dshift, dscale=dscale, g_pre=dg_pre)
    return dx, big, little


SMALL_ROWS = 176


def _rows8(t):
    t = t.reshape(-1, D)
    return jnp.pad(t, ((0, -t.shape[0] % 8), (0, 0)))


def _pack_small(parts):
    row_keys = ("dshift", "dscale", "dgate", "g_pre", "g_post", "lb", "pool_scale", "hgrn_norm_g")
    flat = [p[k] for p in parts for k in row_keys] + [p["pool_w"].reshape(GROUPS * 128 * 128 // D, D) for p in parts]
    nk = len(row_keys)

    def body(*refs):
        o_ref = refs[-1]
        o_ref[...] = jnp.zeros((SMALL_ROWS, D), F32)
        for l in range(2):
            dshift, dscale, dgate, g_pre, g_post, lb, pscale, gn = refs[l * nk:(l + 1) * nk]
            for r, ref in enumerate((dshift, dscale, dgate)):
                o_ref[3 * l + r:3 * l + r + 1, :] = ref[...]
            o_ref[8 + l:9 + l, :] = g_pre[...]
            o_ref[16 + l:17 + l, :] = g_post[...]
            o_ref[24 + l:25 + l, :] = lb[...]
            o_ref[160:161, l * POOL_W:(l + 1) * POOL_W] = pscale[...]
            o_ref[168:169, l * HD:(l + 1) * HD] = gn[...]
            pw = refs[2 * nk + l]
            rows = pw.shape[0]
            o_ref[32 + l * rows:32 + (l + 1) * rows, :] = pw[...]

    return pl.pallas_call(body, name="pack_small", out_shape=jax.ShapeDtypeStruct((SMALL_ROWS, D), F32),
                          compiler_params=_cp())(*flat)


def _unpack_small(p):
    return (p[0:6].reshape(2, 3 * D), p[8:10], p[16:18], p[24:26], p[32:160].reshape(2, GROUPS, 128, 128),
            p[160:161].reshape(2, POOL_W), p[168:169, 0:2 * HD].reshape(2, HD))


def kernel(x, c, w_ada, b_ada, g_pre, g_post, w_in, pool_w, pool_scale, lb_logits, hgrn_norm_g, w_pool_o, w_hgrn_o, w_out, loss_target, m_w_ada, m_b_ada, m_g_pre, m_g_post, m_w_in, m_pool_w, m_pool_scale, m_lb_logits, m_hgrn_norm_g, m_w_pool_o, m_w_hgrn_o, m_w_out, v_w_ada, v_b_ada, v_g_pre, v_g_post, v_w_in, v_pool_w, v_pool_scale, v_lb_logits, v_hgrn_norm_g, v_w_pool_o, v_w_hgrn_o, v_w_out):
    ax, ay, ac = lax.axis_index("x"), lax.axis_index("y"), lax.axis_index("c")
    chip = 2 * ax + ay
    dev = 2 * chip + ac
    xe, te = x[0], loss_target[0]
    ada_s = w_ada.shape[2]

    big_names = ("w_in", "w_pool_o", "w_hgrn_o", "w_out")
    big_w = (w_in, w_pool_o, w_hgrn_o, w_out)
    core = jnp.stack([ac]).astype(jnp.int32)
    place = jnp.stack([chip, ac]).astype(jnp.int32)
    slots = {(k, l): _cast_to_slot(place, t, l, f"cast_{k}{l}") for l in range(2) for k, t in zip(big_names, big_w)}
    w = {k: [None, None] for k in big_names}
    def fills(keys):
        def landed(outs):
            for (k, l), o in zip(keys, outs):
                w[k][l] = slots[k, l] = o
        return landed

    rest0 = [(k, 0) for k in big_names[1:]]
    rest1 = [(k, 1) for k in big_names[1:]]
    no_carry = (None, lambda outs: None)
    order = jnp.stack([chip, 2 * (1 - ax) + ay, 2 * ax + (1 - ay), 2 * (1 - ax) + (1 - ay)]).astype(jnp.int32)

    def ride_fwd0(stage):
        if stage == "proj":
            return _GatherInProj(slots["w_in", 0], order), fills([("w_in", 0)])
        if stage == "hgrn":
            return (_join_carries(_gather_carry([slots[t] for t in rest0]),
                                  _gather_carry([slots["w_in", 1]], piece=(0, 2, 4))),
                    fills(rest0 + [("w_in", 1)]))
        if stage == "tail":
            return _gather_carry([slots["w_in", 1]], piece=(2, 1, 4)), fills([("w_in", 1)])
        return no_carry

    def ride_fwd1(stage):
        if stage == "prenorm":
            return _gather_carry([slots["w_in", 1]], piece=(3, 1, 4)), fills([("w_in", 1)])
        if stage == "hgrn":
            return _gather_carry([slots[t] for t in rest1]), fills(rest1)
        return no_carry

    c_all = _gather_small(jnp.broadcast_to(c, (8, D)), "gather_c").reshape(NDEV, 8, D)[:, 0, :]
    c_pad = jnp.pad(c_all, ((0, ADA_PAD - NDEV), (0, 0)))
    b_sh = lax.dynamic_slice(b_ada, (0, chip * ada_s), (2, ada_s))
    ada_cols = _gather_small(_ada_fwd(c_pad, w_ada, b_sh), "gather_ada")
    ada_cols = ada_cols.reshape(NCHIP, 2, NDEV, 2, ada_s)[:, 0]
    ada_all = jnp.transpose(ada_cols, (2, 1, 0, 3)).reshape(2, NDEV, 3 * D)
    ada_me = lax.dynamic_slice(ada_all, (0, dev, 0), (2, 1, 3 * D))

    lbs = _lb_fwd(lb_logits)
    small = dict(g_pre=g_pre[:, None, :], g_post=g_post[:, None, :], pool_w=pool_w,
                 pool_scale=pool_scale[:, None, :], lb=lbs[:, None, :], hgrn_norm_g=hgrn_norm_g[:, None, :])

    (x1,), sv0 = _layer_fwd(0, xe, ada_me[0], w, small, ride_fwd0)
    (dx2, loss_blk), sv1 = _layer_fwd(1, x1, ada_me[1], w, small, ride_fwd1, target=te)

    parts, recv = {}, {}

    def pair_sums(keys, grads, tag):
        got = _rs_pair(grads, f"rs_pair_{tag}")
        for kl, g, o in zip(keys, grads, got):
            parts[kl] = _pair_add(core, g, o, f"rs_add_{kl[0]}{kl[1]}")

    def exchange(keys):
        def landed(outs):
            recv.update(zip(keys, outs))
        return _chips_carry([parts[kl] for kl in keys]), landed

    def early(l):
        return [(k, l) for k in big_names[1:]]

    def ride_hgrn1(big):
        pair_sums(early(1), [big[k] for k in big_names[1:]], "l1_early")
        return exchange(early(1))

    def ride_d_h1(big):
        pair_sums([("w_in", 1)], [big["w_in"]], "l1_w_in")
        return no_carry

    def ride_hgrn0(big):
        pair_sums(early(0), [big[k] for k in big_names[1:]], "l0_early")
        return exchange([("w_in", 1)] + early(0))

    def ride_d_h0(big):
        pair_sums([("w_in", 0)], [big["w_in"]], "l0_w_in")

        def landed(outs):
            (recv["w_in", 0],) = outs
        return _chips_carry([parts["w_in", 0]], piece=(0, 1, 2)), landed

    def ride_prenorm0(big):
        def landed(outs):
            (recv["w_in", 0],) = outs
        return _chips_carry([parts["w_in", 0]], piece=(1, 1, 2), into=[recv["w_in", 0]]), landed

    no_ride = lambda so_far: no_carry
    dx1, big1, little1 = _layer_bwd(1, dx2, sv1, w, small,
                                    dict(hgrn=ride_hgrn1, gw_in=no_ride, d_h=ride_d_h1, prenorm=no_ride))

    gathered = {}
    zero_row = jnp.zeros((1, D), F32)

    def ride_gw_in0(little):
        so_far = dict(little, dshift=zero_row, dscale=zero_row, g_pre=zero_row)

        def landed(outs):
            (gathered["early"],) = outs
        return _gather_rows_carry(_pack_small([so_far, little1])), landed

    dx0, big0, little0 = _layer_bwd(0, dx1, sv0, w, small,
                                    dict(hgrn=ride_hgrn0, gw_in=ride_gw_in0, d_h=ride_d_h0, prenorm=ride_prenorm0))
    loss = lax.psum(loss_blk[0, 0], ("x", "y", "c"))
    late = _rows8(jnp.stack([little0["dshift"], little0["dscale"], little0["g_pre"]]))
    late = _gather_small(late, "gather_small_late").reshape(NDEV, 8, D)
    packed = gathered["early"].reshape(NDEV, SMALL_ROWS, D)
    packed = packed.at[:, 0:2, :].set(late[:, 0:2, :]).at[:, 8:9, :].set(late[:, 2:3, :])
    red = []
    for k in big_names:
        both = _chip_sum(place, parts[k, 1], recv[k, 1], 1, None, f"rs_sum_{k}1")
        red.append(_chip_sum(place, parts[k, 0], recv[k, 0], 0, both, f"rs_sum_{k}0"))
    g_big = dict(zip(big_names, _rs_swap(red)))

    def upd(wt, g, m, v, name, carry=None):
        shp = wt.shape
        two = lambda t: t.reshape(-1, shp[-1])
        res = _adamw(two(wt), two(g), two(m), two(v), name, carry)
        return [t.reshape(shp) for t in res[:3]], res[3:]

    u_w_in, _ = upd(w_in, g_big["w_in"], m_w_in, v_w_in, "adamw_w_in")
    g_small = _sum_devices(packed)
    g_b_ada, g_g_pre, g_g_post, g_lb, g_pool_w, g_pool_scale, g_norm_g = _unpack_small(g_small)
    g_lb_logits = _lb_bwd(lb_logits, g_lb)
    d_ada_all = packed[:, 0:6, :].reshape(NDEV, 2, 3 * D)
    d_ada_sh = lax.dynamic_slice(jnp.transpose(d_ada_all, (1, 0, 2)), (0, 0, chip * ada_s), (2, NDEV, ada_s))
    d_ada_sh = jnp.pad(d_ada_sh, ((0, 0), (0, ADA_PAD - NDEV), (0, 0)))
    g_w_ada = _ada_wgrad(c_pad.T, d_ada_sh)

    u_w_ada, _ = upd(w_ada, g_w_ada, m_w_ada, v_w_ada, "adamw_w_ada")
    u_w_pool_o, _ = upd(w_pool_o, g_big["w_pool_o"], m_w_pool_o, v_w_pool_o, "adamw_w_pool_o")
    u_w_hgrn_o, _ = upd(w_hgrn_o, g_big["w_hgrn_o"], m_w_hgrn_o, v_w_hgrn_o, "adamw_w_hgrn_o")
    u_w_out, _ = upd(w_out, g_big["w_out"], m_w_out, v_w_out, "adamw_w_out")
    small_w = dict(b_ada=(b_ada, m_b_ada, v_b_ada), g_pre=(g_pre, m_g_pre, v_g_pre),
                   g_post=(g_post, m_g_post, v_g_post), lb_logits=(lb_logits, m_lb_logits, v_lb_logits),
                   pool_w=(pool_w, m_pool_w, v_pool_w), pool_scale=(pool_scale, m_pool_scale, v_pool_scale),
                   hgrn_norm_g=(hgrn_norm_g, m_hgrn_norm_g, v_hgrn_norm_g))
    in_rows = [tuple(t.reshape(rows, width) for t in small_w[key]) for key, _, rows, width in SMALL_PARTS]
    u_rows = _adamw_small(g_small, g_lb_logits, in_rows)
    u_small = {key: [t.reshape(small_w[key][0].shape) for t in u_rows[p]]
               for p, (key, _, _, _) in enumerate(SMALL_PARTS)}

    grads_out = (g_w_ada, g_b_ada, g_g_pre, g_g_post, g_big["w_in"], g_pool_w, g_pool_scale, g_lb_logits,
                 g_norm_g, g_big["w_pool_o"], g_big["w_hgrn_o"], g_big["w_out"])

    def ordered(k):
        s = lambda key: u_small[key][k]
        return (u_w_ada[k], s("b_ada"), s("g_pre"), s("g_post"), u_w_in[k], s("pool_w"), s("pool_scale"),
                s("lb_logits"), s("hgrn_norm_g"), u_w_pool_o[k], u_w_hgrn_o[k], u_w_out[k])

    return (loss, dx0[None], *grads_out, *ordered(0), *ordered(1), *ordered(2))
```

```python
import functools

import jax
import jax.numpy as jnp
from jax import lax
from jax.experimental import pallas as pl
from jax.experimental.pallas import tpu as pltpu

F32 = jnp.float32
BF16 = jnp.bfloat16
MESH = pl.DeviceIdType.MESH

D = 1024
HEADS = 8
HD = 128
GROUPS = 4
POOL_W = 512
WINDOWS = (2, 4, 8, 16)
CH = 128
SB = 32
NH = 2
IN_W = 7168
NCHIP = 4
NDEV = 8
EPS = 1e-6
PV0, PG0, HQ0, HF0, HI0, HG0 = 0, 4, 8, 16, 24, 32
MGP_BLK, MGH_BLK = 5, 6

LR, B1, B2, AEPS, WD, STEP = 0.001, 0.9, 0.999, 1e-08, 0.01, 10
VMEM_LIMIT = 56 * 1024 * 1024


def _cp(sem=None, **kw):
    if sem is not None:
        kw["dimension_semantics"] = sem
    return pltpu.CompilerParams(vmem_limit_bytes=VMEM_LIMIT, **kw)


def _sig(z):
    return 1.0 / (1.0 + jnp.exp(-z))


def _dsilu(z, s):
    return s * (1.0 + z * (1.0 - s))


def _row_tile(rows, cap):
    if rows <= cap:
        return rows
    t = 1 << (cap.bit_length() - 1)
    while rows % t:
        t //= 2
    return t


ANY = pl.BlockSpec(memory_space=pl.ANY)


class _Carry:
    def __init__(self, ins, outs, aliases, n_sem, start, finish):
        self.ins, self.outs, self.aliases, self.n_sem = list(ins), list(outs), dict(aliases), n_sem
        self.start, self.finish = start, finish


class _SemWindow:
    def __init__(self, ref, base):
        self._ref, self._base = ref, base

    @property
    def at(self):
        return self

    def __getitem__(self, k):
        return self._ref.at[self._base + k]


def _join_carries(*carries):
    ins, outs, aliases, spans, n_sem = [], [], {}, [], 0
    for cr in carries:
        aliases.update({len(ins) + i: len(outs) + o for i, o in cr.aliases.items()})
        spans.append((len(ins), len(cr.ins), len(outs), len(cr.outs), n_sem))
        ins, outs, n_sem = ins + cr.ins, outs + cr.outs, n_sem + cr.n_sem

    def run(which):
        def fn(i_refs, o_refs, send_sems, recv_sems):
            for cr, (i0, ni, o0, no, s0) in zip(carries, spans):
                getattr(cr, which)(i_refs[i0:i0 + ni], o_refs[o0:o0 + no], _SemWindow(send_sems, s0),
                                   _SemWindow(recv_sems, s0))
        return fn

    return _Carry(ins, outs, aliases, n_sem, run("start"), run("finish"))


def _call(body, *, name, grid, in_specs, out_specs, out_shape, args, scratch_shapes=(), sem=None, carry=None,
          aliases=None):
    in_specs, out_specs, out_shape = list(in_specs), list(out_specs), list(out_shape)
    scratch_shapes = list(scratch_shapes)
    aliases = dict(aliases or {})
    if carry is None:
        outs = pl.pallas_call(body, name=name, grid=grid, in_specs=in_specs, out_specs=out_specs,
                              out_shape=out_shape, scratch_shapes=scratch_shapes, input_output_aliases=aliases,
                              compiler_params=_cp(sem))(*args)
        return list(outs)
    n_in, n_out, n_scr = len(in_specs), len(out_specs), len(scratch_shapes)
    c_in, c_out = len(carry.ins), len(carry.outs)

    def wrapped(*refs):
        k_in, rest = refs[:n_in], refs[n_in:]
        ci, rest = rest[:c_in], rest[c_in:]
        k_out, rest = rest[:n_out], rest[n_out:]
        co, rest = rest[:c_out], rest[c_out:]
        k_scr, (ssem, rsem) = rest[:n_scr], rest[n_scr:]
        pids = [pl.program_id(d) for d in range(len(grid))]
        first = functools.reduce(jnp.logical_and, [p == 0 for p in pids])
        last = functools.reduce(jnp.logical_and, [p == g - 1 for p, g in zip(pids, grid)])

        @pl.when(first)
        def _():
            carry.start(ci, co, ssem, rsem)

        body(*k_in, *k_out, *k_scr)

        @pl.when(last)
        def _():
            carry.finish(ci, co, ssem, rsem)

    outs = pl.pallas_call(
        wrapped, name=name, grid=grid, in_specs=in_specs + [ANY] * c_in, out_specs=out_specs + [ANY] * c_out,
        out_shape=out_shape + carry.outs,
        input_output_aliases={**aliases, **{n_in + i: n_out + o for i, o in carry.aliases.items()}},
        scratch_shapes=scratch_shapes + [pltpu.SemaphoreType.DMA((carry.n_sem,))] * 2,
        compiler_params=_cp(("arbitrary",) * len(grid)),
    )(*args, *carry.ins)
    return list(outs)


def _mm(a, b, *, name, b_mode="nn", out_shards=0, tm=1024, tn=256, tk=None, out_dtype=F32, carry=None):
    M, K = a.shape
    if b_mode == "nn":
        N = b.shape[1]
    elif b_mode == "nt":
        N = b.shape[0]
    elif b_mode == "nn_sh":
        N = b.shape[0] * b.shape[2]
    else:
        N = b.shape[1]
    tm = _row_tile(M, tm)
    if b_mode == "nn_sh":
        tn = _row_tile(b.shape[2], tn)
    elif out_shards:
        tn = _row_tile(N // out_shards, tn)
    else:
        tn = _row_tile(N, tn)
    if tk is None:
        tk = K if b_mode != "nt_shk" else b.shape[2]
    if b_mode == "nt_shk":
        tk = _row_tile(b.shape[2], tk)
    nm, nn, nk = M // tm, N // tn, K // tk

    a_spec = pl.BlockSpec((tm, tk), lambda m, n, k: (m, k))
    if b_mode == "nn":
        b_spec = pl.BlockSpec((tk, tn), lambda m, n, k: (k, n))
    elif b_mode == "nt":
        b_spec = pl.BlockSpec((tn, tk), lambda m, n, k: (n, k))
    elif b_mode == "nn_sh":
        nps = b.shape[2] // tn
        b_spec = pl.BlockSpec((None, tk, tn), lambda m, n, k: (n // nps, k, n % nps))
    else:
        kps = b.shape[2] // tk
        b_spec = pl.BlockSpec((None, tn, tk), lambda m, n, k: (k // kps, n, k % kps))
    if out_shards:
        ops = (N // out_shards) // tn
        o_spec = pl.BlockSpec((None, tm, tn), lambda m, n, k: (n // ops, m, n % ops))
        o_shape = jax.ShapeDtypeStruct((out_shards, M, N // out_shards), out_dtype)
    else:
        o_spec = pl.BlockSpec((tm, tn), lambda m, n, k: (m, n))
        o_shape = jax.ShapeDtypeStruct((M, N), out_dtype)
    trans_b = b_mode in ("nt", "nt_shk")
    dn = (((1,), (1,)), ((), ())) if trans_b else (((1,), (0,)), ((), ()))

    def body(a_ref, b_ref, o_ref, acc_ref):
        k = pl.program_id(2)

        @pl.when(k == 0)
        def _():
            acc_ref[...] = jnp.zeros(acc_ref.shape, F32)

        acc_ref[...] += lax.dot_general(a_ref[...].astype(BF16), b_ref[...].astype(BF16), dn,
                                        preferred_element_type=F32)

        @pl.when(k == nk - 1)
        def _():
            o_ref[...] = acc_ref[...].astype(o_ref.dtype)

    outs = _call(body, name=name, grid=(nm, nn, nk), in_specs=[a_spec, b_spec], out_specs=[o_spec],
                 out_shape=[o_shape], scratch_shapes=[pltpu.VMEM((tm, tn), F32)],
                 sem=("parallel", "parallel", "arbitrary"), args=(a, b), carry=carry)
    return outs[0] if carry is None else (outs[0], outs[1:])


def _rowvec(n=D):
    return pl.BlockSpec((1, n), lambda i: (0, 0))


def _prenorm_fwd(x, g, scale, shift, name, carry=None):
    S = x.shape[0]
    tr = _row_tile(S, 256)

    def body(x_ref, g_ref, sc_ref, sh_ref, h_ref, ht_ref):
        xv = x_ref[...]
        r = lax.rsqrt(jnp.mean(xv * xv, axis=-1, keepdims=True) + EPS)
        hv = (xv * r) * g_ref[...] * (1.0 + sc_ref[...]) + sh_ref[...]
        h_ref[...] = hv.astype(BF16)
        ht_ref[...] = hv.T.astype(BF16)

    outs = _call(
        body, name=name, grid=(S // tr,),
        in_specs=[pl.BlockSpec((tr, D), lambda i: (i, 0)), _rowvec(), _rowvec(), _rowvec()],
        out_specs=[pl.BlockSpec((tr, D), lambda i: (i, 0)), pl.BlockSpec((D, tr), lambda i: (0, i))],
        out_shape=[jax.ShapeDtypeStruct((S, D), BF16), jax.ShapeDtypeStruct((D, S), BF16)],
        sem=("parallel",), args=(x, g, scale, shift), carry=carry)
    return outs[:2], outs[2:]


def _prenorm_bwd(dh, dxn, x, g, scale, name, carry=None):
    S = x.shape[0]
    tr = _row_tile(S, 256)

    def body(dh_ref, dxn_ref, x_ref, g_ref, sc_ref, dx_ref, dsh_ref, dsc_ref, dg_ref):
        i = pl.program_id(0)

        @pl.when(i == 0)
        def _():
            dsh_ref[...] = jnp.zeros((1, D), F32)
            dsc_ref[...] = jnp.zeros((1, D), F32)
            dg_ref[...] = jnp.zeros((1, D), F32)

        xv = x_ref[...]
        dhv = dh_ref[...]
        gv = g_ref[...]
        mod = 1.0 + sc_ref[...]
        r = lax.rsqrt(jnp.mean(xv * xv, axis=-1, keepdims=True) + EPS)
        xh = xv * r
        dsh_ref[...] += jnp.sum(dhv, axis=0, keepdims=True)
        dsc_ref[...] += jnp.sum(dhv * (xh * gv), axis=0, keepdims=True)
        dg_ref[...] += jnp.sum(dhv * mod * xh, axis=0, keepdims=True)
        u = dhv * mod * gv
        dx_ref[...] = dxn_ref[...] + r * u - xv * (r * r * r) * jnp.mean(u * xv, axis=-1, keepdims=True)

    tile = pl.BlockSpec((tr, D), lambda i: (i, 0))
    outs = _call(
        body, name=name, grid=(S // tr,),
        in_specs=[tile, tile, tile, _rowvec(), _rowvec()],
        out_specs=[tile, _rowvec(), _rowvec(), _rowvec()],
        out_shape=[jax.ShapeDtypeStruct((S, D), F32)] + [jax.ShapeDtypeStruct((1, D), F32)] * 3,
        sem=("arbitrary",), args=(dh, dxn, x, g, scale), carry=carry)
    return outs[:4], outs[4:]


def _layer_tail_fwd(proj, a_in, b_in, x, w_po, w_ho, w_out, gate, g, name, target=None, carry=None):
    S = proj.shape[0]
    tr = _row_tile(S, 256)
    nsh, _, wsh = w_po.shape
    n_in = 10 + (target is not None)

    def body(*refs):
        (mgp_ref, mgh_ref, a_ref, b_ref, x_ref, wpo_ref, who_ref, wout_ref, gate_ref, g_ref) = refs[:10]
        bra_ref, brb_ref, mt_ref, y_ref, xn_ref = refs[n_in:n_in + 5]
        av = a_ref[...]
        bra = jnp.concatenate([jnp.dot(av, wpo_ref[j], preferred_element_type=F32) for j in range(nsh)], axis=1)
        brb = jnp.dot(b_ref[...], who_ref[...], preferred_element_type=F32)
        mv = _sig(mgp_ref[...].astype(F32)) * bra + _sig(mgh_ref[...].astype(F32)) * brb
        bra_ref[...] = bra.astype(BF16)
        brb_ref[...] = brb.astype(BF16)
        mt_ref[...] = mv.T.astype(BF16)
        yv = jnp.dot(mv.astype(BF16), wout_ref[...], preferred_element_type=F32)
        y_ref[...] = yv
        r = lax.rsqrt(jnp.mean(yv * yv, axis=-1, keepdims=True) + EPS)
        xn = x_ref[...] + gate_ref[...] * ((yv * r) * g_ref[...])
        if target is None:
            xn_ref[...] = xn
        else:
            t_ref, l_ref = refs[10], refs[n_in + 5]

            @pl.when(pl.program_id(0) == 0)
            def _():
                l_ref[...] = jnp.zeros((8, 128), F32)

            err = xn - t_ref[...]
            xn_ref[...] = err * (1.0 / D)
            l_ref[...] += 0.5 * jnp.sum(jnp.mean(err * err, axis=-1, keepdims=True))

    tile = pl.BlockSpec((tr, D), lambda i: (i, 0))
    whole = lambda t: pl.BlockSpec(t.shape, lambda i: (0,) * t.ndim)
    last = target is not None
    outs = _call(
        body, name=name, grid=(S // tr,),
        in_specs=[pl.BlockSpec((tr, D), lambda i: (i, MGP_BLK)), pl.BlockSpec((tr, D), lambda i: (i, MGH_BLK)),
                  pl.BlockSpec((tr, POOL_W), lambda i: (i, 0)), tile, tile, whole(w_po), whole(w_ho),
                  whole(w_out), _rowvec(), _rowvec()] + [tile] * last,
        out_specs=[tile, tile, pl.BlockSpec((D, tr), lambda i: (0, i)), tile, tile]
        + [pl.BlockSpec((8, 128), lambda i: (0, 0))] * last,
        out_shape=[jax.ShapeDtypeStruct((S, D), BF16), jax.ShapeDtypeStruct((S, D), BF16),
                   jax.ShapeDtypeStruct((D, S), BF16), jax.ShapeDtypeStruct((S, D), F32),
                   jax.ShapeDtypeStruct((S, D), F32)] + [jax.ShapeDtypeStruct((8, 128), F32)] * last,
        sem=("arbitrary",) if last else ("parallel",),
        args=(proj, proj, a_in, b_in, x, w_po, w_ho, w_out, gate, g) + ((target,) if last else ()), carry=carry)
    return outs[:5 + last], outs[5 + last:]


def _layer_head_bwd(dxn, y, proj, br_a, br_b, w_po, w_ho, w_out, gate, g, name, carry=None):
    S = y.shape[0]
    tr = _row_tile(S, 256)
    nsh, _, wsh = w_po.shape

    def body(dxn_ref, y_ref, mgp_ref, mgh_ref, bra_ref, brb_ref, wpo_ref, who_ref, wout_ref, gate_ref, g_ref,
             dy_ref, dba_ref, dbb_ref, dproj_ref, dain_ref, dbin_ref, dgate_ref, dg_ref, dmgh_s):
        i = pl.program_id(0)
        j = pl.program_id(1)

        @pl.when((i == 0) & (j == 0))
        def _():
            dgate_ref[...] = jnp.zeros((1, D), F32)
            dg_ref[...] = jnp.zeros((1, D), F32)

        @pl.when(j == 1)
        def _():
            dproj_ref[...] = dmgh_s[...]

        @pl.when(j == 0)
        def _():
            everything(dxn_ref, y_ref, mgp_ref, mgh_ref, bra_ref, brb_ref, wpo_ref, who_ref, wout_ref, gate_ref,
                       g_ref, dy_ref, dba_ref, dbb_ref, dproj_ref, dain_ref, dbin_ref, dgate_ref, dg_ref, dmgh_s)

    def everything(dxn_ref, y_ref, mgp_ref, mgh_ref, bra_ref, brb_ref, wpo_ref, who_ref, wout_ref, gate_ref, g_ref,
                   dy_ref, dba_ref, dbb_ref, dproj_ref, dain_ref, dbin_ref, dgate_ref, dg_ref, dmgh_s):
        yv = y_ref[...]
        dv = dxn_ref[...]
        gv = g_ref[...]
        gt = gate_ref[...]
        r = lax.rsqrt(jnp.mean(yv * yv, axis=-1, keepdims=True) + EPS)
        yh = yv * r
        dgate_ref[...] += jnp.sum(dv * (yh * gv), axis=0, keepdims=True)
        dg_ref[...] += jnp.sum(dv * gt * yh, axis=0, keepdims=True)
        u = dv * gt * gv
        dy = (r * u - yv * (r * r * r) * jnp.mean(u * yv, axis=-1, keepdims=True)).astype(BF16)
        dy_ref[...] = dy
        dm = _dot_nt(dy, wout_ref[...])
        sp = _sig(mgp_ref[...].astype(F32))
        sh = _sig(mgh_ref[...].astype(F32))
        dba = (dm * sp).astype(BF16)
        dbb = (dm * sh).astype(BF16)
        dba_ref[...] = dba
        dbb_ref[...] = dbb
        dproj_ref[...] = (dm * bra_ref[...].astype(F32) * sp * (1.0 - sp)).astype(BF16)
        dmgh_s[...] = (dm * brb_ref[...].astype(F32) * sh * (1.0 - sh)).astype(BF16)
        dain = _dot_nt(dba[:, 0:wsh], wpo_ref[0])
        for k in range(1, nsh):
            dain = dain + _dot_nt(dba[:, k * wsh:(k + 1) * wsh], wpo_ref[k])
        dain_ref[...] = dain
        dbin_ref[...] = _dot_nt(dbb, who_ref[...])

    tile = pl.BlockSpec((tr, D), lambda i, j: (i, 0))
    whole = lambda t: pl.BlockSpec(t.shape, lambda i, j: (0,) * t.ndim)
    vec = pl.BlockSpec((1, D), lambda i, j: (0, 0))
    ahead = lambda i, j: jnp.minimum(i + j, S // tr - 1)
    tile_in = pl.BlockSpec((tr, D), lambda i, j: (ahead(i, j), 0))
    outs = _call(
        body, name=name, grid=(S // tr, 2),
        in_specs=[tile_in, tile_in, pl.BlockSpec((tr, D), lambda i, j: (ahead(i, j), MGP_BLK)),
                  pl.BlockSpec((tr, D), lambda i, j: (ahead(i, j), MGH_BLK)), tile_in, tile_in, whole(w_po),
                  whole(w_ho), whole(w_out), vec, vec],
        out_specs=[tile, tile, tile, pl.BlockSpec((tr, D), lambda i, j: (i, MGP_BLK + j)),
                   pl.BlockSpec((tr, POOL_W), lambda i, j: (i, 0)), tile, vec, vec],
        out_shape=[jax.ShapeDtypeStruct((S, D), BF16)] * 3
        + [jax.ShapeDtypeStruct((S, IN_W), BF16), jax.ShapeDtypeStruct((S, POOL_W), F32),
           jax.ShapeDtypeStruct((S, D), F32), jax.ShapeDtypeStruct((1, D), F32), jax.ShapeDtypeStruct((1, D), F32)],
        scratch_shapes=[pltpu.VMEM((tr, D), BF16)], sem=("arbitrary", "arbitrary"),
        args=(dxn, y, proj, proj, br_a, br_b, w_po, w_ho, w_out, gate, g), carry=carry)
    return outs[:8], outs[8:]


def _pool_pieces(u, g, S):
    rowi = lax.broadcasted_iota(jnp.int32, (S, 1), 0)

    def down(z, k):
        return jnp.where(rowi >= k, pltpu.roll(z, k, axis=0), 0.0)

    s2 = u + down(u, 1)
    s4 = s2 + down(s2, 2)
    s8 = s4 + down(s4, 4)
    s16 = s8 + down(s8, 8)
    win = jnp.where(g == 0, s2, jnp.where(g == 1, s4, jnp.where(g == 2, s8, s16)))
    w = jnp.where(g == 0, 2, jnp.where(g == 1, 4, jnp.where(g == 2, 8, 16)))
    count = jnp.minimum(rowi + 1, w).astype(F32)
    return win / count - u, count, rowi


def _pool_fwd(proj, pw, pscale, name):
    S = proj.shape[0]

    def body(pv_ref, pg_ref, pw_ref, sc_ref, a_ref, at_ref):
        g = pl.program_id(0)
        pooled, _, _ = _pool_pieces(pv_ref[...].astype(F32), g, S)
        pm = jnp.dot(pooled.astype(BF16), pw_ref[...].astype(BF16), preferred_element_type=F32)
        pgv = pg_ref[...].astype(F32)
        av = pm * sc_ref[...] * (pgv * _sig(pgv))
        a_ref[...] = av.astype(BF16)
        at_ref[...] = av.T.astype(BF16)

    return pl.pallas_call(
        body, name=name, grid=(GROUPS,),
        in_specs=[pl.BlockSpec((S, 128), lambda g: (0, PV0 + g)), pl.BlockSpec((S, 128), lambda g: (0, PG0 + g)),
                  pl.BlockSpec((None, 128, 128), lambda g: (g, 0, 0)), pl.BlockSpec((1, 128), lambda g: (0, g))],
        out_specs=[pl.BlockSpec((S, 128), lambda g: (0, g)), pl.BlockSpec((128, S), lambda g: (g, 0))],
        out_shape=[jax.ShapeDtypeStruct((S, POOL_W), BF16), jax.ShapeDtypeStruct((POOL_W, S), BF16)],
        compiler_params=_cp(("parallel",)),
    )(proj, proj, pw, pscale)


def _pool_bwd(da, proj, pw, pscale, dproj, name):
    S = proj.shape[0]

    def body(da_ref, pv_ref, pg_ref, pw_ref, sc_ref, dproj_in, dproj_ref, dpw_ref, dsc_ref, dpg_s):
        @pl.when(pl.program_id(1) == 1)
        def _():
            dproj_ref[...] = dpg_s[...]

        @pl.when(pl.program_id(1) == 0)
        def _():
            group(da_ref, pv_ref, pg_ref, pw_ref, sc_ref, dproj_ref, dpg_s, dpw_ref, dsc_ref)

    def group(da_ref, pv_ref, pg_ref, pw_ref, sc_ref, dpv_ref, dpg_ref, dpw_ref, dsc_ref):
        g = pl.program_id(0)
        pooled, count, rowi = _pool_pieces(pv_ref[...].astype(F32), g, S)
        pwb = pw_ref[...].astype(BF16)
        pm = jnp.dot(pooled.astype(BF16), pwb, preferred_element_type=F32)
        scv = sc_ref[...]
        pgv = pg_ref[...].astype(F32)
        sg = _sig(pgv)
        dav = da_ref[...]
        d_ps = dav * (pgv * sg)
        dpg_ref[...] = (dav * (pm * scv) * _dsilu(pgv, sg)).astype(BF16)
        dsc_ref[...] = jnp.sum(d_ps * pm, axis=0, keepdims=True)
        d_pm = (d_ps * scv).astype(BF16)
        dpw_ref[...] = lax.dot_general(pooled.astype(BF16), d_pm, (((0,), (0,)), ((), ())),
                                       preferred_element_type=F32)
        d_pooled = lax.dot_general(d_pm, pwb, (((1,), (1,)), ((), ())), preferred_element_type=F32)
        z = d_pooled / count

        def up(v, k):
            return jnp.where(rowi < S - k, pltpu.roll(v, S - k, axis=0), 0.0)

        t2 = z + up(z, 1)
        t4 = t2 + up(t2, 2)
        t8 = t4 + up(t4, 4)
        t16 = t8 + up(t8, 8)
        adj = jnp.where(g == 0, t2, jnp.where(g == 1, t4, jnp.where(g == 2, t8, t16)))
        dpv_ref[...] = (adj - d_pooled).astype(BF16)

    col = lambda g, j: (0, g)
    ahead = lambda g, j: jnp.minimum(g + j, GROUPS - 1)
    return pl.pallas_call(
        body, name=name, grid=(GROUPS, 2),
        in_specs=[pl.BlockSpec((S, 128), lambda g, j: (0, ahead(g, j))),
                  pl.BlockSpec((S, 128), lambda g, j: (0, PV0 + ahead(g, j))),
                  pl.BlockSpec((S, 128), lambda g, j: (0, PG0 + ahead(g, j))),
                  pl.BlockSpec((None, 128, 128), lambda g, j: (ahead(g, j), 0, 0)),
                  pl.BlockSpec((1, 128), lambda g, j: (0, ahead(g, j))), ANY],
        out_specs=[pl.BlockSpec((S, 128), lambda g, j: (0, PV0 + g + (PG0 - PV0) * j)),
                   pl.BlockSpec((None, 128, 128), lambda g, j: (g, 0, 0)), pl.BlockSpec((1, 128), col)],
        out_shape=[jax.ShapeDtypeStruct(dproj.shape, dproj.dtype),
                   jax.ShapeDtypeStruct((GROUPS, 128, 128), F32), jax.ShapeDtypeStruct((1, POOL_W), F32)],
        scratch_shapes=[pltpu.VMEM((S, 128), BF16)], input_output_aliases={5: 0},
        compiler_params=_cp(("arbitrary", "arbitrary")),
    )(da, proj, proj, pw, pscale, dproj)


SCAN_SHIFTS = tuple(1 << b for b in range(CH.bit_length() - 1))


def _chunk_cumsum(z, rowi):
    for sh in SCAN_SHIFTS:
        z = z + jnp.where(rowi >= sh, pltpu.roll(z, sh, axis=0), 0.0)
    return z


def _chunk_rev_cumsum(z, rowi):
    for sh in SCAN_SHIFTS:
        z = z + jnp.where(rowi < CH - sh, pltpu.roll(z, CH - sh, axis=0), 0.0)
    return z


def _dot_nn(a, b):
    return jnp.dot(a.astype(BF16), b.astype(BF16), preferred_element_type=F32)


def _dot_nt(a, b):
    return lax.dot_general(a.astype(BF16), b.astype(BF16), (((1,), (1,)), ((), ())), preferred_element_type=F32)


def _dot_tn(a, b):
    return lax.dot_general(a.astype(BF16), b.astype(BF16), (((0,), (0,)), ((), ())), preferred_element_type=F32)


def _gates(hq, hf, lbv):
    hq, hf = hq.astype(F32), hf.astype(F32)
    sq = _sig(hq)
    sf = _sig(hf)
    f = lbv + (1.0 - lbv) * sf
    fc = jnp.maximum(f, 1e-30)
    return hq * sq, sq, sf, f, fc, jnp.log(fc)


DECAY_CAP = 60.0


def _block_ref(c_ref, i):
    if i == 0:
        return jnp.zeros((1, HD), F32)
    return c_ref[SB * i - 1:SB * i, :]


def _block_decay(c_ref):
    spans = [_block_ref(c_ref, i) - c_ref[SB * (i + 1) - 1:SB * (i + 1), :] for i in range(CH // SB)]
    return functools.reduce(jnp.maximum, spans)


def _pair_factors(q_ref, k, c_ref, first, cap, round_bf16):
    nb = CH // SB
    c = c_ref[...]
    zero = jnp.zeros((SB, HD), F32)
    q_groups, k_groups, eqs, eks = [], [], [], []
    for i in range(first, nb):
        blk = slice(SB * i, SB * (i + 1))
        r_i = _block_ref(c_ref, i)
        eq = jnp.exp(jnp.minimum(c_ref[blk, :] - r_i, 0.0))
        ek = jnp.exp(jnp.minimum(r_i - c, cap))
        qi, kei = q_ref[blk, :] * eq, k * ek
        if round_bf16:
            qi, kei = qi.astype(BF16).astype(F32), kei.astype(BF16).astype(F32)
        q_groups.append(jnp.concatenate([zero] * i + [qi] + [zero] * (nb - 1 - i), axis=0))
        k_groups.append(kei)
        eqs.append(eq)
        eks.append(ek)
    return jnp.concatenate(q_groups, axis=1), jnp.concatenate(k_groups, axis=1), eqs, eks


def _pair_mask(rowi, coli, strict):
    return (coli < jnp.bitwise_and(rowi, -SB)) if strict else (coli <= rowi)


def _hgrn_fwd(proj, lb, gn, name, carry=None):
    S = proj.shape[0]
    nch = S // CH
    W = NH * HD

    def body(hq_ref, hf_ref, hi_ref, hg_ref, lb_ref, gn_ref, bin_ref, bint_ref, oraw_ref, st_ref, mild_ref,
             cum_ref, q_s, k_s, c_s, v_s, o_s, state_s, qf_s, kf_s, cf_s):
        state_s[...] = jnp.zeros((NH, HD, HD), F32)
        rowi = lax.broadcasted_iota(jnp.int32, (CH, 1), 0)
        coli = lax.broadcasted_iota(jnp.int32, (1, CH), 1)
        sbi = lax.broadcasted_iota(jnp.int32, (SB, 1), 0)
        gnv = gn_ref[...]

        def gates_pass(n, worst):
            rows = pl.ds(pl.multiple_of(n * CH, CH), CH)
            for hh in range(NH):
                lanes = slice(hh * HD, (hh + 1) * HD)
                q, _, _, f, _, logf = _gates(hq_ref[rows, lanes], hf_ref[rows, lanes], lb_ref[:, lanes])
                c = _chunk_cumsum(logf, rowi)
                qf_s[hh, rows, :] = q
                kf_s[hh, rows, :] = 1.0 - f
                cf_s[hh, rows, :] = c
                cum_ref[rows, lanes] = c
                c_s[hh] = c
                worst = jnp.maximum(worst, _block_decay(c_s.at[hh]))
            return worst

        def between_chunks(hh, n, rows):
            lanes = slice(hh * HD, (hh + 1) * HD)
            q = qf_s[hh, rows, :]
            k = kf_s[hh, rows, :]
            c = cf_s[hh, rows, :]
            v = hi_ref[rows, lanes].astype(F32)
            q_s[hh] = q
            k_s[hh] = k
            c_s[hh] = c
            v_s[hh] = v
            st = state_s[hh]
            st_ref[hh, n] = st.astype(BF16)
            o_s[hh] = _dot_nt(q * jnp.exp(c), st)
            last = c_s[hh, CH - 1:CH, :]
            state_s[hh] = st * jnp.exp(last) + _dot_tn(v, k * jnp.exp(last - c))

        def pairs_matmul(hh, first, cap, strict):
            qx, kc, _, _ = _pair_factors(q_s.at[hh], k_s[hh], c_s.at[hh], first, cap, round_bf16=False)
            a = jnp.where(_pair_mask(rowi, coli, strict), _dot_nt(qx, kc), 0.0)
            o_s[hh] += _dot_nn(a, v_s[hh])

        def within_chunk_matmul(hh):
            pairs_matmul(hh, 0, DECAY_CAP, strict=False)

        def within_chunk_exact(hh):
            pairs_matmul(hh, 1, 0.0, strict=True)
            for i in range(CH // SB):
                blk = slice(SB * i, SB * (i + 1))
                qb = q_s[hh, blk, :]
                cb = c_s[hh, blk, :]
                acc = jnp.zeros((SB, HD), F32)
                for s in range(SB):
                    row = SB * i + s
                    w = jnp.exp(jnp.minimum(cb - c_s[hh, row:row + 1, :], 0.0))
                    a_col = jnp.sum(qb * k_s[hh, row:row + 1, :] * w, axis=-1, keepdims=True)
                    acc = acc + jnp.where(sbi >= s, a_col, 0.0) * v_s[hh, row:row + 1, :]
                o_s[hh, blk, :] += acc

        def norm_and_gate(hh, rows):
            lanes = slice(hh * HD, (hh + 1) * HD)
            ov = o_s[hh]
            oraw_ref[rows, lanes] = ov
            r = lax.rsqrt(jnp.mean(ov * ov, axis=-1, keepdims=True) + EPS)
            hg = hg_ref[rows, lanes].astype(F32)
            bin_ref[rows, lanes] = ((ov * r) * gnv * (hg * _sig(hg))).astype(BF16)

        def chunk_with(within_chunk):
            def chunk(n, carry):
                rows = pl.ds(pl.multiple_of(n * CH, CH), CH)
                for hh in range(NH):
                    between_chunks(hh, n, rows)
                for hh in range(NH):
                    within_chunk(hh)
                for hh in range(NH):
                    norm_and_gate(hh, rows)
                return carry
            return chunk

        worst = lax.fori_loop(0, nch, gates_pass, jnp.zeros((1, HD), F32))
        mild = jnp.max(worst) <= DECAY_CAP
        mild_ref[...] = jnp.broadcast_to(jnp.where(mild, 1.0, 0.0), (8, HD))

        @pl.when(mild)
        def _():
            lax.fori_loop(0, nch, chunk_with(within_chunk_matmul), 0, unroll=4)

        @pl.when(jnp.logical_not(mild))
        def _():
            lax.fori_loop(0, nch, chunk_with(within_chunk_exact), 0)

        bint_ref[...] = bin_ref[...].astype(F32).T.astype(BF16)

    col = lambda off: pl.BlockSpec((S, W), lambda h: (0, off // NH + h))
    head = pl.BlockSpec((S, W), lambda h: (0, h))
    outs = _call(
        body, name=name, grid=(HEADS // NH,),
        in_specs=[col(HQ0), col(HF0), col(HI0), col(HG0), pl.BlockSpec((1, W), lambda h: (0, h)),
                  pl.BlockSpec((1, HD), lambda h: (0, 0))],
        out_specs=[head, pl.BlockSpec((W, S), lambda h: (h, 0)), head,
                   pl.BlockSpec((NH, nch, HD, HD), lambda h: (h, 0, 0, 0)),
                   pl.BlockSpec((8, HD), lambda h: (h, 0)), head],
        out_shape=[jax.ShapeDtypeStruct((S, D), BF16), jax.ShapeDtypeStruct((D, S), BF16),
                   jax.ShapeDtypeStruct((S, D), F32), jax.ShapeDtypeStruct((HEADS, nch, HD, HD), BF16),
                   jax.ShapeDtypeStruct((8 * HEADS // NH, HD), F32), jax.ShapeDtypeStruct((S, D), F32)],
        scratch_shapes=[pltpu.VMEM((NH, CH, HD), F32)] * 5 + [pltpu.VMEM((NH, HD, HD), F32)]
        + [pltpu.VMEM((NH, S, HD), F32)] * 3,
        sem=("parallel",), args=(proj, proj, proj, proj, lb, gn), carry=carry)
    return outs[:6], outs[6:]


def _hgrn_bwd(dbin, proj, oraw, states, mild, cum, lb, gn, dproj, name, carry=None):
    S = proj.shape[0]
    nch = S // CH
    W = NH * HD
    n_in = 12

    def body(*refs):
        ins, (dproj_ref, dlb_ref, dgn_ref) = refs[:n_in - 1], refs[n_in:n_in + 3]
        scratch, later = refs[n_in + 3:-3], refs[-3:]
        seg = pl.program_id(1)

        @pl.when(seg == 0)
        def _():
            heads(*ins, dproj_ref, *later, dlb_ref, dgn_ref, *scratch)

        for s, kept in enumerate(later):
            @pl.when(seg == s + 1)
            def _(kept=kept):
                dproj_ref[...] = kept[...]

    def heads(db_ref, hq_ref, hf_ref, hi_ref, hg_ref, or_ref, st_ref, mild_ref, cum_ref, lb_ref, gn_ref,
              dq_ref, df_ref, di_ref, dg_ref, dlb_ref, dgn_ref,
              q_s, k_s, c_s, v_s, do_s, dq_s, dk_s, dv_s, dc_s, dqd_s, dkd_s, f_s, sf_s, sq_s, dl_s, dst_s,
              dlb_s, dgn_s):
        dst_s[...] = jnp.zeros((NH, HD, HD), F32)
        dlb_s[...] = jnp.zeros((1, W), F32)
        dgn_s[...] = jnp.zeros((1, HD), F32)
        rowi = lax.broadcasted_iota(jnp.int32, (CH, 1), 0)
        coli = lax.broadcasted_iota(jnp.int32, (1, CH), 1)
        sbi = lax.broadcasted_iota(jnp.int32, (SB, 1), 0)
        gnv = gn_ref[...]
        def between_chunks(hh, n, rows):
            lanes = slice(hh * HD, (hh + 1) * HD)
            lbv = lb_ref[:, lanes]
            hq = hq_ref[rows, lanes].astype(F32)
            sq = _sig(hq)
            sf = _sig(hf_ref[rows, lanes].astype(F32))
            f = lbv + (1.0 - lbv) * sf
            q = hq * sq
            k = 1.0 - f
            f_s[hh] = f
            sf_s[hh] = sf
            sq_s[hh] = sq
            v = hi_ref[rows, lanes].astype(F32)
            c = cum_ref[rows, lanes]
            ov = or_ref[rows, lanes]
            hg = hg_ref[rows, lanes].astype(F32)
            sg = _sig(hg)
            r = lax.rsqrt(jnp.mean(ov * ov, axis=-1, keepdims=True) + EPS)
            dbv = db_ref[rows, lanes]
            d_on = dbv * (hg * sg)
            dg_ref[rows, lanes] = (dbv * ((ov * r) * gnv) * _dsilu(hg, sg)).astype(BF16)
            dgn_s[...] += jnp.sum(d_on * (ov * r), axis=0, keepdims=True)
            u = d_on * gnv
            do = r * u - ov * (r * r * r) * jnp.mean(u * ov, axis=-1, keepdims=True)
            q_s[hh] = q
            k_s[hh] = k
            c_s[hh] = c
            v_s[hh] = v
            do_s[hh] = do
            st = st_ref[hh, n].astype(F32)
            dst = dst_s[hh]
            ec = jnp.exp(c)
            last = c_s[hh, CH - 1:CH, :]
            el = jnp.exp(last - c)
            elast = jnp.exp(last)
            dq = _dot_nn(do, st) * ec
            dk = _dot_nn(v, dst) * el
            dq_s[hh] = dq
            dk_s[hh] = dk
            dv_s[hh] = _dot_nt(k * el, dst)
            dc_s[hh] = q * dq - k * dk
            dl_s[hh] = (jnp.sum(k * dk, axis=0, keepdims=True)
                        + elast * jnp.sum(st * dst, axis=0, keepdims=True))
            dst_s[hh] = dst * elast + _dot_tn(do, q * ec)

        def pairs_matmul(hh, first, cap, strict):
            do = do_s[hh]
            qx, kc, eqs, eks = _pair_factors(q_s.at[hh], k_s[hh], c_s.at[hh], first, cap, round_bf16=True)
            mask = _pair_mask(rowi, coli, strict)
            a = jnp.where(mask, _dot_nt(qx, kc), 0.0)
            d_a = jnp.where(mask, _dot_nt(do, v_s[hh]).astype(BF16).astype(F32), 0.0)
            dqx = _dot_nn(d_a, kc)
            dkc = _dot_tn(d_a, qx)
            dv_s[hh] += _dot_tn(a, do)
            dk, dcum = dk_s[hh], dc_s[hh]
            dq_slabs = [jnp.zeros((SB, HD), F32)] * first
            dc_slabs = [jnp.zeros((SB, HD), F32)] * first
            for g, (eq, ek) in enumerate(zip(eqs, eks)):
                rows = slice(SB * (first + g), SB * (first + g + 1))
                cols = slice(HD * g, HD * (g + 1))
                dq_i = dqx[rows, cols]
                dk_i = dkc[:, cols]
                dq_slabs.append(dq_i * eq)
                dc_slabs.append(qx[rows, cols] * dq_i)
                dk = dk + dk_i * ek
                dcum = dcum - kc[:, cols] * dk_i
            dq_s[hh] += jnp.concatenate(dq_slabs, axis=0)
            dk_s[hh] = dk
            dc_s[hh] = dcum + jnp.concatenate(dc_slabs, axis=0)

        def pairs_exact(hh):
            dqd_s[hh] = jnp.zeros((CH, HD), F32)
            dkd_s[hh] = jnp.zeros((CH, HD), F32)
            for i in range(CH // SB):
                blk = slice(SB * i, SB * (i + 1))
                qb = q_s[hh, blk, :]
                cb = c_s[hh, blk, :]
                dob = do_s[hh, blk, :]
                dq_acc = jnp.zeros((SB, HD), F32)
                for s in range(SB):
                    row = SB * i + s
                    ks = k_s[hh, row:row + 1, :]
                    vs = v_s[hh, row:row + 1, :]
                    w = jnp.exp(jnp.minimum(cb - c_s[hh, row:row + 1, :], 0.0))
                    live = sbi >= s
                    a_col = jnp.where(live, jnp.sum(qb * ks * w, axis=-1, keepdims=True), 0.0)
                    da_col = jnp.where(live, jnp.sum(dob * vs, axis=-1, keepdims=True), 0.0)
                    dq_acc = dq_acc + da_col * ks * w
                    dkd_s[hh, row:row + 1, :] += jnp.sum(da_col * qb * w, axis=0, keepdims=True)
                    dv_s[hh, row:row + 1, :] += jnp.sum(a_col * dob, axis=0, keepdims=True)
                dqd_s[hh, blk, :] += dq_acc
            dq_d = dqd_s[hh]
            dk_d = dkd_s[hh]
            dq_s[hh] += dq_d
            dk_s[hh] += dk_d
            dc_s[hh] += q_s[hh] * dq_d - k_s[hh] * dk_d

        def gate_grads(hh, rows):
            lanes = slice(hh * HD, (hh + 1) * HD)
            lbv = lb_ref[:, lanes]
            hq = hq_ref[rows, lanes].astype(F32)
            f, sf, sq = f_s[hh], sf_s[hh], sq_s[hh]
            dlogf = _chunk_rev_cumsum(dc_s[hh], rowi) + dl_s[hh]
            dfv = jnp.where(f > 1e-30, dlogf / jnp.maximum(f, 1e-30), 0.0) - dk_s[hh]
            dlb_s[:, lanes] += jnp.sum(dfv * (1.0 - sf), axis=0, keepdims=True)
            df_ref[rows, lanes] = (dfv * (1.0 - lbv) * sf * (1.0 - sf)).astype(BF16)
            dq_ref[rows, lanes] = (dq_s[hh] * _dsilu(hq, sq)).astype(BF16)
            di_ref[rows, lanes] = dv_s[hh].astype(BF16)

        def chunk_with(pairs):
            def chunk(j, carry):
                n = nch - 1 - j
                rows = pl.ds(pl.multiple_of(n * CH, CH), CH)
                for hh in range(NH):
                    between_chunks(hh, n, rows)
                for hh in range(NH):
                    pairs(hh)
                for hh in range(NH):
                    gate_grads(hh, rows)
                return carry
            return chunk

        def pairs_mild(hh):
            pairs_matmul(hh, 0, DECAY_CAP, strict=False)

        def pairs_any(hh):
            pairs_matmul(hh, 1, 0.0, strict=True)
            pairs_exact(hh)

        mild = jnp.max(mild_ref[...]) > 0.5

        @pl.when(mild)
        def _():
            lax.fori_loop(0, nch, chunk_with(pairs_mild), 0, unroll=2)

        @pl.when(jnp.logical_not(mild))
        def _():
            lax.fori_loop(0, nch, chunk_with(pairs_any), 0)

        dlb_ref[...] = dlb_s[...]
        dgn_ref[...] = jnp.broadcast_to(dgn_s[...], (8, HD))

    ahead = lambda h, s: jnp.minimum(h + jnp.minimum(s, 1), HEADS // NH - 1)
    col = lambda off: pl.BlockSpec((S, W), lambda h, s: (0, off // NH + ahead(h, s)))
    head_in = pl.BlockSpec((S, W), lambda h, s: (0, ahead(h, s)))
    vec_in = pl.BlockSpec((1, W), lambda h, s: (0, ahead(h, s)))
    vec = pl.BlockSpec((1, W), lambda h, s: (0, h))
    seg_w = (HF0 - HQ0) // NH
    outs = _call(
        body, name=name, grid=(HEADS // NH, 4),
        in_specs=[head_in, col(HQ0), col(HF0), col(HI0), col(HG0), head_in,
                  pl.BlockSpec((NH, nch, HD, HD), lambda h, s: (ahead(h, s), 0, 0, 0)),
                  pl.BlockSpec((8, HD), lambda h, s: (ahead(h, s), 0)), head_in, vec_in,
                  pl.BlockSpec((1, HD), lambda h, s: (0, 0)), ANY],
        out_specs=[pl.BlockSpec((S, W), lambda h, s: (0, HQ0 // NH + seg_w * s + h)), vec,
                   pl.BlockSpec((8, HD), lambda h, s: (h, 0))],
        out_shape=[jax.ShapeDtypeStruct(dproj.shape, dproj.dtype), jax.ShapeDtypeStruct((1, D), F32),
                   jax.ShapeDtypeStruct((8 * HEADS // NH, HD), F32)],
        scratch_shapes=[pltpu.VMEM((NH, CH, HD), F32)] * 14
        + [pltpu.VMEM((NH, 1, HD), F32), pltpu.VMEM((NH, HD, HD), F32), pltpu.VMEM((1, W), F32),
           pltpu.VMEM((1, HD), F32)] + [pltpu.VMEM((S, W), BF16)] * 3,
        sem=("arbitrary", "arbitrary"), aliases={n_in - 1: 0},
        args=(dbin, proj, proj, proj, proj, oraw, states, mild, cum, lb, gn, dproj), carry=carry)
    dproj, dlb, dgn = outs[:3]
    return (dproj, dlb, dgn.reshape(HEADS // NH, 8, HD)[:, 0, :]), outs[3:]


def _lower_bounds(l0, l1):
    m = jnp.maximum(l0, l1)
    e0 = jnp.exp(l0 - m)
    e1 = jnp.exp(l1 - m)
    tot = e0 + e1
    p0 = e0 / tot
    p1 = e1 / tot
    return jnp.clip(p0 - p0, 0.0, 1.0), jnp.clip((p0 + p1) - p0, 0.0, 1.0)


def _lb_fwd(logits):
    def body(l_ref, o_ref):
        lb0, lb1 = _lower_bounds(l_ref[0:1, :], l_ref[1:2, :])
        o_ref[0:1, :] = lb0
        o_ref[1:2, :] = lb1

    return pl.pallas_call(body, name="lb_fwd", out_shape=jax.ShapeDtypeStruct((2, D), F32))(logits)


def _lb_bwd(logits, dlb):
    def body(l_ref, d_ref, o_ref):
        _, vjp = jax.vjp(_lower_bounds, l_ref[0:1, :], l_ref[1:2, :])
        g0, g1 = vjp((d_ref[0:1, :], d_ref[1:2, :]))
        o_ref[0:1, :] = g0
        o_ref[1:2, :] = g1

    return pl.pallas_call(body, name="lb_bwd", out_shape=jax.ShapeDtypeStruct((2, D), F32))(logits, dlb)


ADA_PAD = 128


def _ada_fwd(c_pad, w_ada, b_sh):
    ns = w_ada.shape[2]

    def body(c_ref, w_ref, b_ref, o_ref):
        cv = c_ref[...]
        ca = (cv * _sig(cv)).astype(BF16)
        for l in range(2):
            res = jnp.dot(ca, w_ref[l].astype(BF16), preferred_element_type=F32)
            o_ref[:, l * ns:(l + 1) * ns] = res[0:NDEV, :] + b_ref[l:l + 1, :]

    return pl.pallas_call(body, name="ada_fwd", out_shape=jax.ShapeDtypeStruct((NDEV, 2 * ns), F32),
                          compiler_params=_cp())(c_pad, w_ada, b_sh)


def _ada_wgrad(c_pad_t, d_ada_sh):
    ns = d_ada_sh.shape[2]

    def body(c_ref, d_ref, o_ref):
        cv = c_ref[...]
        ca = (cv * _sig(cv)).astype(BF16)
        for l in range(2):
            o_ref[l] = jnp.dot(ca, d_ref[l].astype(BF16), preferred_element_type=F32)

    return pl.pallas_call(body, name="ada_wgrad", out_shape=jax.ShapeDtypeStruct((2, D, ns), F32),
                          compiler_params=_cp())(c_pad_t, d_ada_sh)


def _sum_devices(g):
    _, R, C = g.shape

    def body(g_ref, o_ref):
        acc = g_ref[0]
        for d in range(1, NDEV):
            acc = acc + g_ref[d]
        o_ref[...] = acc

    return pl.pallas_call(body, name="sum_devices", out_shape=jax.ShapeDtypeStruct((R, C), F32),
                          compiler_params=_cp())(g)


def _adamw(w, g, m, v, name, carry=None):
    R, C = w.shape
    tr = _row_tile(R, max(8, (1 << 19) // C))

    def body(w_ref, g_ref, m_ref, v_ref, d_ref, nm_ref, nv_ref):
        d_ref[...], nm_ref[...], nv_ref[...] = _adamw_update(w_ref[...], g_ref[...], m_ref[...], v_ref[...])

    tile = pl.BlockSpec((tr, C), lambda i: (i, 0))
    return _call(body, name=name, grid=(R // tr,), in_specs=[tile] * 4, out_specs=[tile] * 3,
                 out_shape=[jax.ShapeDtypeStruct((R, C), F32)] * 3, sem=("parallel",), args=(w, g, m, v),
                 carry=carry)


def _adamw_update(w, g, m, v):
    nm = B1 * m + (1.0 - B1) * g
    nv = B2 * v + (1.0 - B2) * (g * g)
    m_hat = nm / (1.0 - B1 ** STEP)
    v_hat = nv / (1.0 - B2 ** STEP)
    return -LR * (m_hat / (jnp.sqrt(v_hat) + AEPS) + WD * w), nm, nv


SMALL_PARTS = (("b_ada", 0, 6, D), ("g_pre", 8, 2, D), ("g_post", 16, 2, D), ("lb_logits", 24, 2, D),
               ("pool_w", 32, 128, D), ("pool_scale", 160, 1, D), ("hgrn_norm_g", 168, 1, 2 * HD))


def _adamw_small(g_small, g_lb_logits, wmv):
    n = len(SMALL_PARTS)

    def body(g_ref, glb_ref, *refs):
        ins, outs = refs[:3 * n], refs[3 * n:]
        for p, (key, row0, rows, width) in enumerate(SMALL_PARTS):
            gv = glb_ref[...] if key == "lb_logits" else g_ref[row0:row0 + rows, 0:width]
            res = _adamw_update(ins[3 * p][...], gv, ins[3 * p + 1][...], ins[3 * p + 2][...])
            for t in range(3):
                outs[3 * p + t][...] = res[t]

    flat = [t for triple in wmv for t in triple]
    outs = pl.pallas_call(body, name="adamw_small",
                          out_shape=[jax.ShapeDtypeStruct(t.shape, F32) for t in flat],
                          compiler_params=_cp())(g_small, g_lb_logits, *flat)
    return [outs[3 * p:3 * p + 3] for p in range(n)]


def _cast_to_slot(place, w, l, name):
    _, R, C = w.shape
    tr = _row_tile(R, max(8, (1 << 19) // C))

    def body(p_ref, w_ref, o_ref):
        o_ref[...] = w_ref[...].astype(BF16)

    return pl.pallas_call(
        body, name=name, out_shape=jax.ShapeDtypeStruct((NCHIP, R, C), BF16),
        grid_spec=pltpu.PrefetchScalarGridSpec(
            num_scalar_prefetch=1, grid=(R // tr,),
            in_specs=[pl.BlockSpec((None, tr, C), lambda i, p_ref: (l, i, 0))],
            out_specs=pl.BlockSpec((None, tr, C), lambda i, p_ref: (p_ref[0], i, 0))),
        compiler_params=_cp(("parallel",)),
    )(place, w)


def _pair_add(core, g, got, name):
    _, R, C = g.shape
    r2 = R // 2
    tr = _row_tile(r2, max(8, (1 << 19) // C))
    nt = r2 // tr

    def body(c_ref, a_ref, b_ref, o_ref):
        o_ref[...] = (a_ref[...].astype(F32) + b_ref[...].astype(F32)).astype(o_ref.dtype)

    return pl.pallas_call(
        body, name=name, out_shape=jax.ShapeDtypeStruct((NCHIP, r2, C), BF16),
        grid_spec=pltpu.PrefetchScalarGridSpec(
            num_scalar_prefetch=1, grid=(NCHIP, nt),
            in_specs=[pl.BlockSpec((None, tr, C), lambda j, i, c_ref: (j, c_ref[0] * nt + i, 0)),
                      pl.BlockSpec((None, tr, C), lambda j, i, c_ref: (j, i, 0))],
            out_specs=pl.BlockSpec((None, tr, C), lambda j, i, c_ref: (j, i, 0))),
        compiler_params=_cp(("parallel", "parallel")),
    )(core, g, got)


def _chip_sum(place, part, recv, layer, both, name):
    _, r2, C = part.shape
    tr = _row_tile(r2, max(8, (1 << 18) // C))
    nt = r2 // tr

    def body(p_ref, own_ref, r_ref, *rest):
        o_ref = rest[-1]
        me = p_ref[0]
        own = own_ref[...].astype(F32)
        acc = None
        for j in range(NCHIP):
            slot = jnp.minimum(jnp.where(j > me, j - 1, j), NCHIP - 2)
            term = jnp.where(me == j, own, r_ref[slot].astype(F32))
            acc = term if acc is None else acc + term
        o_ref[...] = acc

    args = (place, part, recv) if both is None else (place, part, recv, both)
    return pl.pallas_call(
        body, name=name, out_shape=jax.ShapeDtypeStruct((2, 2 * r2, C), F32),
        grid_spec=pltpu.PrefetchScalarGridSpec(
            num_scalar_prefetch=1, grid=(nt,),
            in_specs=[pl.BlockSpec((None, tr, C), lambda i, p_ref: (p_ref[0], i, 0)),
                      pl.BlockSpec((NCHIP - 1, tr, C), lambda i, p_ref: (0, i, 0))] + [ANY] * (len(args) - 3),
            out_specs=pl.BlockSpec((None, tr, C), lambda i, p_ref: (layer, p_ref[1] * nt + i, 0))),
        input_output_aliases={} if both is None else {3: 0},
        compiler_params=_cp(("parallel",)),
    )(*args)


def _place():
    x, y, c = lax.axis_index("x"), lax.axis_index("y"), lax.axis_index("c")
    chips = [(1 - x, y), (x, 1 - y), (1 - x, 1 - y)]
    return x, y, c, chips


def _gather_small(blk, name):
    m_per, n = blk.shape

    def body(x_ref, out_ref, send_sems, recv_sems, local_sem):
        x, y, c, chips = _place()
        me, sibling = (x, y, c), (x, y, 1 - c)

        def rows(px, py, pc):
            return out_ref.at[pl.ds((4 * px + 2 * py + pc) * m_per, m_per), :]

        def copy(k, block, to, src=None):
            return pltpu.make_async_remote_copy(
                src_ref=rows(*block) if src is None else src, dst_ref=rows(*block),
                send_sem=send_sems.at[k], recv_sem=recv_sems.at[k], device_id=to, device_id_type=MESH)

        mine = pltpu.make_async_copy(x_ref, rows(*me), local_sem)
        mine.start()
        first = [copy(0, me, sibling, src=x_ref)]
        first += [copy(1 + j, me, (*chip, c), src=x_ref) for j, chip in enumerate(chips)]
        for cp in first:
            cp.start()
        passed = [copy(4 + j, (*chip, c), sibling) for j, chip in enumerate(chips)]
        for j, chip in enumerate(chips):
            copy(1 + j, (*chip, c), me).wait_recv()
            passed[j].start()
        copy(0, sibling, me).wait_recv()
        for j, chip in enumerate(chips):
            copy(4 + j, (*chip, 1 - c), me).wait_recv()
        for cp in first + passed:
            cp.wait_send()
        mine.wait()

    return pl.pallas_call(
        body, name=name, out_shape=jax.ShapeDtypeStruct((NDEV * m_per, n), blk.dtype),
        in_specs=[pl.BlockSpec(memory_space=pltpu.VMEM)], out_specs=pl.BlockSpec(memory_space=pltpu.VMEM),
        scratch_shapes=[pltpu.SemaphoreType.DMA((7,)), pltpu.SemaphoreType.DMA((7,)), pltpu.SemaphoreType.DMA],
        compiler_params=_cp(),
    )(blk)


def _gather_rows_carry(blk):
    m_per, n = blk.shape

    def rows(ref, px, py, pc):
        return ref.at[pl.ds((4 * px + 2 * py + pc) * m_per, m_per), :]

    def copy(ins, outs, send_sems, recv_sems, k, block, to, own=False):
        return pltpu.make_async_remote_copy(
            src_ref=ins[0] if own else rows(outs[0], *block), dst_ref=rows(outs[0], *block),
            send_sem=send_sems.at[k], recv_sem=recv_sems.at[k], device_id=to, device_id_type=MESH)

    def mine(ins, outs, send_sems):
        x, y, c, _ = _place()
        return pltpu.make_async_copy(ins[0], rows(outs[0], x, y, c), send_sems.at[7])

    def start(ins, outs, send_sems, recv_sems):
        x, y, c, chips = _place()
        mine(ins, outs, send_sems).start()
        copy(ins, outs, send_sems, recv_sems, 0, (x, y, c), (x, y, 1 - c), own=True).start()
        for j, chip in enumerate(chips):
            copy(ins, outs, send_sems, recv_sems, 1 + j, (x, y, c), (*chip, c), own=True).start()

    def finish(ins, outs, send_sems, recv_sems):
        x, y, c, chips = _place()
        for j, chip in enumerate(chips):
            copy(ins, outs, send_sems, recv_sems, 1 + j, (*chip, c), (x, y, c)).wait_recv()
            copy(ins, outs, send_sems, recv_sems, 4 + j, (*chip, c), (x, y, 1 - c)).start()
        copy(ins, outs, send_sems, recv_sems, 0, (x, y, 1 - c), (x, y, c)).wait_recv()
        for j, chip in enumerate(chips):
            copy(ins, outs, send_sems, recv_sems, 4 + j, (*chip, 1 - c), (x, y, c)).wait_recv()
        copy(ins, outs, send_sems, recv_sems, 0, (x, y, c), (x, y, 1 - c), own=True).wait_send()
        for j, chip in enumerate(chips):
            copy(ins, outs, send_sems, recv_sems, 1 + j, (x, y, c), (*chip, c), own=True).wait_send()
            copy(ins, outs, send_sems, recv_sems, 4 + j, (*chip, c), (x, y, 1 - c)).wait_send()
        mine(ins, outs, send_sems).wait()

    return _Carry([blk], [jax.ShapeDtypeStruct((NDEV * m_per, n), blk.dtype)], {}, 8, start, finish)


def _gather_carry(shards, piece=(0, 1, 1)):
    n = len(shards)
    first, count, of = piece

    def rows(ref, half):
        r2 = ref.shape[1] // 2
        return pl.ds(half * r2 + first * (r2 // of), count * (r2 // of))

    def over_ici(outs, send_sems, recv_sems, a, j, chip_xy, slot):
        x, y, c, _ = _place()
        blk = outs[a].at[slot, rows(outs[a], c), :]
        return pltpu.make_async_remote_copy(
            src_ref=blk, dst_ref=blk, send_sem=send_sems.at[6 * a + j], recv_sem=recv_sems.at[6 * a + j],
            device_id=(*chip_xy, c), device_id_type=MESH)

    def over_d2d(outs, send_sems, recv_sems, a, j, slot, half):
        x, y, c, _ = _place()
        blk = outs[a].at[slot, rows(outs[a], half), :]
        return pltpu.make_async_remote_copy(
            src_ref=blk, dst_ref=blk, send_sem=send_sems.at[6 * a + 3 + j], recv_sem=recv_sems.at[6 * a + 3 + j],
            device_id=(x, y, 1 - c), device_id_type=MESH)

    def start(ins, outs, send_sems, recv_sems):
        x, y, c, chips = _place()
        for a in range(n):
            for j, chip_xy in enumerate(chips):
                over_ici(outs, send_sems, recv_sems, a, j, chip_xy, 2 * x + y).start()

    def finish(ins, outs, send_sems, recv_sems):
        x, y, c, chips = _place()
        for a in range(n):
            for j, (cx, cy) in enumerate(chips):
                over_ici(outs, send_sems, recv_sems, a, j, (cx, cy), 2 * cx + cy).wait_recv()
                over_d2d(outs, send_sems, recv_sems, a, j, 2 * cx + cy, c).start()
        for a in range(n):
            for j, (cx, cy) in enumerate(chips):
                over_d2d(outs, send_sems, recv_sems, a, j, 2 * cx + cy, 1 - c).wait_recv()
        for a in range(n):
            for j, (cx, cy) in enumerate(chips):
                over_ici(outs, send_sems, recv_sems, a, j, (cx, cy), 2 * x + y).wait_send()
                over_d2d(outs, send_sems, recv_sems, a, j, 2 * cx + cy, c).wait_send()

    return _Carry(shards, [jax.ShapeDtypeStruct(s.shape, s.dtype) for s in shards],
                  {a: a for a in range(n)}, 6 * n, start, finish)


def _rs_pair(grads, name):
    n = len(grads)

    def body(*refs):
        ins, gots = refs[:n], refs[n:2 * n]
        send_sems, recv_sems = refs[2 * n:]
        x, y, c, _ = _place()
        cps = []
        for a in range(n):
            r2 = ins[a].shape[1] // 2
            cp = pltpu.make_async_remote_copy(
                src_ref=ins[a].at[:, pl.ds((1 - c) * r2, r2), :], dst_ref=gots[a],
                send_sem=send_sems.at[a], recv_sem=recv_sems.at[a],
                device_id=(x, y, 1 - c), device_id_type=MESH)
            cp.start()
            cps.append(cp)
        for cp in cps:
            cp.wait()

    half = [jax.ShapeDtypeStruct((NCHIP, g.shape[1] // 2, g.shape[2]), g.dtype) for g in grads]
    return pl.pallas_call(
        body, name=name, out_shape=half, in_specs=[ANY] * n, out_specs=[ANY] * n,
        scratch_shapes=[pltpu.SemaphoreType.DMA((n,)), pltpu.SemaphoreType.DMA((n,))],
        compiler_params=_cp(),
    )(*grads)


def _chips_carry(parts, piece=(0, 1, 1), into=None):
    n = len(parts)
    first, count, of = piece

    def rows(ref):
        step = ref.shape[1] // of
        return pl.ds(first * step, count * step)

    def send(ins, outs, send_sems, recv_sems, a, j, chip_xy):
        x, y, c, _ = _place()
        me, them = 2 * x + y, 2 * chip_xy[0] + chip_xy[1]
        return pltpu.make_async_remote_copy(
            src_ref=ins[a].at[them, rows(ins[a]), :],
            dst_ref=outs[a].at[me - (me > them).astype(jnp.int32), rows(outs[a]), :],
            send_sem=send_sems.at[3 * a + j], recv_sem=recv_sems.at[3 * a + j],
            device_id=(*chip_xy, c), device_id_type=MESH)

    def start(ins, outs, send_sems, recv_sems):
        _, _, _, chips = _place()
        for a in range(n):
            for j, chip_xy in enumerate(chips):
                send(ins, outs, send_sems, recv_sems, a, j, chip_xy).start()

    def finish(ins, outs, send_sems, recv_sems):
        x, y, c, chips = _place()
        me = 2 * x + y
        for a in range(n):
            for j, (cx, cy) in enumerate(chips):
                them = 2 * cx + cy
                blk = outs[a].at[them - (them > me).astype(jnp.int32), rows(outs[a]), :]
                pltpu.make_async_remote_copy(
                    src_ref=blk, dst_ref=blk, send_sem=send_sems.at[3 * a + j], recv_sem=recv_sems.at[3 * a + j],
                    device_id=(cx, cy, c), device_id_type=MESH).wait_recv()
        for a in range(n):
            for j, chip_xy in enumerate(chips):
                send(ins, outs, send_sems, recv_sems, a, j, chip_xy).wait_send()

    landing = [jax.ShapeDtypeStruct((NCHIP - 1,) + p.shape[1:], p.dtype) for p in parts]
    if into is None:
        return _Carry(parts, landing, {}, 3 * n, start, finish)
    return _Carry(list(parts) + list(into), landing, {n + a: a for a in range(n)}, 3 * n, start, finish)


def _rs_swap(fulls):
    n = len(fulls)

    def body(*refs):
        outs = refs[n:2 * n]
        send_sems, recv_sems = refs[2 * n:]
        x, y, c, _ = _place()
        cps = []
        for a in range(n):
            r2 = outs[a].shape[1] // 2
            mine = outs[a].at[:, pl.ds(c * r2, r2), :]
            cp = pltpu.make_async_remote_copy(
                src_ref=mine, dst_ref=mine, send_sem=send_sems.at[a], recv_sem=recv_sems.at[a],
                device_id=(x, y, 1 - c), device_id_type=MESH)
            cp.start()
            cps.append(cp)
        for a in range(n):
            r2 = outs[a].shape[1] // 2
            blk = outs[a].at[:, pl.ds((1 - c) * r2, r2), :]
            pltpu.make_async_remote_copy(
                src_ref=blk, dst_ref=blk, send_sem=send_sems.at[a], recv_sem=recv_sems.at[a],
                device_id=(x, y, 1 - c), device_id_type=MESH).wait_recv()
        for cp in cps:
            cp.wait_send()

    return pl.pallas_call(
        body, name="rs_swap", out_shape=[jax.ShapeDtypeStruct(f.shape, f.dtype) for f in fulls],
        in_specs=[ANY] * n, out_specs=[ANY] * n, input_output_aliases={a: a for a in range(n)},
        scratch_shapes=[pltpu.SemaphoreType.DMA((n,)), pltpu.SemaphoreType.DMA((n,))],
        compiler_params=_cp(),
    )(*fulls)


def _tail_weight_grads(merged_t, b_in_t, a_in_t, dy, dbr_b, dbr_a, name, tn=256):
    S = dy.shape[0]
    nn = D // tn

    def body(mt_ref, bt_ref, at_ref, dy_ref, db_ref, da_ref, go_ref, gh_ref, gp_ref):
        go_ref[...] = jnp.dot(mt_ref[...], dy_ref[...], preferred_element_type=F32).astype(BF16)
        gh_ref[...] = jnp.dot(bt_ref[...], db_ref[...], preferred_element_type=F32).astype(BF16)
        gp_ref[...] = jnp.dot(at_ref[...], da_ref[...], preferred_element_type=F32).astype(BF16)

    left = lambda rows: pl.BlockSpec((rows, S), lambda n: (0, 0))
    right = pl.BlockSpec((S, tn), lambda n: (0, n))
    out = pl.BlockSpec((D, tn), lambda n: (0, n))
    return pl.pallas_call(
        body, name=name, grid=(nn,), in_specs=[left(D), left(D), left(POOL_W), right, right, right],
        out_specs=[out, out, pl.BlockSpec((None, POOL_W, tn), lambda n: (n, 0, 0))],
        out_shape=[jax.ShapeDtypeStruct((D, D), BF16), jax.ShapeDtypeStruct((D, D), BF16),
                   jax.ShapeDtypeStruct((NCHIP, POOL_W, D // NCHIP), BF16)],
        compiler_params=_cp(("parallel",)),
    )(merged_t, b_in_t, a_in_t, dy, dbr_b, dbr_a)


class _GatherInProj:
    def __init__(self, slot, order):
        self.slot, self.order = slot, order


def _proj_with_gather(h, w_slot, order, name, tn=256):
    S, K = h.shape
    nsh, _, ns = w_slot.shape
    tps = ns // tn
    nt = nsh * tps
    r2 = K // 2

    def body(ord_ref, h_ref, w_in_ref, o_ref, w_ref, wbuf, tile_sems, send_sems, recv_sems):
        n = pl.program_id(0)
        x, y, c, chips = _place()

        def half(slot, which):
            return w_ref.at[slot, pl.ds(which * r2, r2), :]

        def over_ici(j, slot):
            blk = half(slot, c)
            return pltpu.make_async_remote_copy(src_ref=blk, dst_ref=blk, send_sem=send_sems.at[j],
                                                recv_sem=recv_sems.at[j], device_id=(*chips[j], c),
                                                device_id_type=MESH)

        def over_d2d(j, which):
            blk = half(2 * chips[j][0] + chips[j][1], which)
            return pltpu.make_async_remote_copy(src_ref=blk, dst_ref=blk, send_sem=send_sems.at[3 + j],
                                                recv_sem=recv_sems.at[3 + j], device_id=(x, y, 1 - c),
                                                device_id_type=MESH)

        def tile_copy(step, slot):
            shard = ord_ref[step // tps]
            return pltpu.make_async_copy(w_ref.at[shard, :, pl.ds((step % tps) * tn, tn)], wbuf.at[slot],
                                         tile_sems.at[slot])

        @pl.when(n == 0)
        def _():
            for j in range(3):
                over_ici(j, 2 * x + y).start()
            tile_copy(0, 0).start()

        for j in range(3):
            @pl.when(n == (j + 1) * tps - 1)
            def _(j=j):
                over_ici(j, 2 * chips[j][0] + chips[j][1]).wait_recv()
                over_d2d(j, c).start()
                over_d2d(j, 1 - c).wait_recv()

        @pl.when(n + 1 < nt)
        def _():
            tile_copy(n + 1, (n + 1) % 2).start()

        tile_copy(n, n % 2).wait()
        o_ref[...] = jnp.dot(h_ref[...], wbuf[n % 2], preferred_element_type=F32).astype(o_ref.dtype)

        @pl.when(n == nt - 1)
        def _():
            for j in range(3):
                over_ici(j, 2 * x + y).wait_send()
                over_d2d(j, c).wait_send()

    return pl.pallas_call(
        body, name=name,
        out_shape=[jax.ShapeDtypeStruct((S, nsh * ns), BF16), jax.ShapeDtypeStruct(w_slot.shape, w_slot.dtype)],
        grid_spec=pltpu.PrefetchScalarGridSpec(
            num_scalar_prefetch=1, grid=(nt,),
            in_specs=[pl.BlockSpec((S, K), lambda n, o_ref: (0, 0)), ANY],
            out_specs=[pl.BlockSpec((S, tn), lambda n, o_ref: (0, o_ref[n // tps] * tps + n % tps)), ANY],
            scratch_shapes=[pltpu.VMEM((2, K, tn), w_slot.dtype), pltpu.SemaphoreType.DMA((2,)),
                            pltpu.SemaphoreType.DMA((6,)), pltpu.SemaphoreType.DMA((6,))]),
        input_output_aliases={2: 1},
        compiler_params=_cp(("arbitrary",)),
    )(order, h, w_slot)


def _mm_ride(a, b, carry, **kw):
    if carry is None:
        return _mm(a, b, **kw), []
    return _mm(a, b, carry=carry, **kw)


def _layer_fwd(l, x, ada, w, small, ride, target=None):
    shift, scale, gate = ada[:, 0:D], ada[:, D:2 * D], ada[:, 2 * D:3 * D]
    carry, landed = ride("prenorm")
    (h, h_t), outs = _prenorm_fwd(x, small["g_pre"][l], scale, shift, f"prenorm_fwd{l}", carry)
    landed(outs)
    carry, landed = ride("proj")
    if isinstance(carry, _GatherInProj):
        proj, full = _proj_with_gather(h, carry.slot, carry.order, f"proj{l}")
        outs = [full]
    else:
        proj, outs = _mm_ride(h, w["w_in"][l], carry, name=f"proj{l}", b_mode="nn_sh", tm=2048, out_dtype=BF16)
    landed(outs)
    a_in, a_in_t = _pool_fwd(proj, small["pool_w"][l], small["pool_scale"][l], f"pool_fwd{l}")
    carry, landed = ride("hgrn")
    (b_in, b_in_t, o_raw, states, mild, cum), outs = _hgrn_fwd(proj, small["lb"][l], small["hgrn_norm_g"][l],
                                                              f"hgrn_fwd{l}", carry=carry)
    landed(outs)
    carry, landed = ride("tail")
    (br_a, br_b, merged_t, y, *x_new), outs = _layer_tail_fwd(
        proj, a_in, b_in, x, w["w_pool_o"][l], w["w_hgrn_o"][l].reshape(D, D), w["w_out"][l].reshape(D, D),
        gate, small["g_post"][l], f"tail_fwd{l}", target=target, carry=carry)
    landed(outs)
    saved = dict(x=x, h_t=h_t, proj=proj, a_in_t=a_in_t, b_in_t=b_in_t, o_raw=o_raw, states=states, mild=mild,
                 cum=cum,
                 br_a=br_a, br_b=br_b, merged_t=merged_t, y=y, scale=scale, gate=gate)
    return x_new, saved


def _layer_bwd(l, dxn, sv, w, small, ride):
    carry, landed = ride["head"](None)
    (dy, dbr_a, dbr_b, dproj, da_in, db_in, dgate, dg_post), outs = _layer_head_bwd(
        dxn, sv["y"], sv["proj"], sv["br_a"], sv["br_b"], w["w_pool_o"][l], w["w_hgrn_o"][l].reshape(D, D),
        w["w_out"][l].reshape(D, D), sv["gate"], small["g_post"][l], f"head_bwd{l}", carry)
    landed(outs)
    gw_out, gw_hgrn_o, gw_pool_o = _tail_weight_grads(sv["merged_t"], sv["b_in_t"], sv["a_in_t"], dy, dbr_b,
                                                      dbr_a, f"gw_tail{l}")
    big = dict(w_pool_o=gw_pool_o, w_hgrn_o=gw_hgrn_o.reshape(NCHIP, D // NCHIP, D),
               w_out=gw_out.reshape(NCHIP, D // NCHIP, D))
    carry, landed = ride["hgrn"](big)
    (dproj, dlb, dgn), outs = _hgrn_bwd(db_in, sv["proj"], sv["o_raw"], sv["states"], sv["mild"], sv["cum"],
                                        small["lb"][l], small["hgrn_norm_g"][l], dproj, f"hgrn_bwd{l}",
                                        carry=carry)
    landed(outs)
    dproj, dpw, dpsc = _pool_bwd(da_in, sv["proj"], small["pool_w"][l], small["pool_scale"][l], dproj,
                                 f"pool_bwd{l}")
    little = dict(dgate=dgate, g_post=dg_post, pool_w=dpw, pool_scale=dpsc, lb=dlb,
                  hgrn_norm_g=jnp.sum(dgn, axis=0, keepdims=True))
    carry, landed = ride["gw_in"](little)
    big["w_in"], outs = _mm_ride(sv["h_t"], dproj, carry, name=f"gw_in{l}", out_shards=NCHIP, out_dtype=BF16)
    landed(outs)
    carry, landed = ride["d_h"](big)
    dh, outs = _mm_ride(dproj, w["w_in"][l], carry, name=f"d_h{l}", b_mode="nt_shk", tn=1024)
    landed(outs)
    carry, landed = ride["prenorm"](big)
    (dx, dshift, dscale, dg_pre), outs = _prenorm_bwd(dh, dxn, sv["x"], small["g_pre"][l], sv["scale"],
                                                      f"prenorm_bwd{l}", carry)
    landed(outs)
    little.update(dshift=dshift, dscale=dscale, g_pre=dg_pre)
    return dx, big, little


SMALL_ROWS = 176


def _rows8(t):
    t = t.reshape(-1, D)
    return jnp.pad(t, ((0, -t.shape[0] % 8), (0, 0)))


def _pack_small(parts):
    row_keys = ("dshift", "dscale", "dgate", "g_pre", "g_post", "lb", "pool_scale", "hgrn_norm_g")
    flat = [p[k] for p in parts for k in row_keys] + [p["pool_w"].reshape(GROUPS * 128 * 128 // D, D) for p in parts]
    nk = len(row_keys)

    def body(*refs):
        o_ref = refs[-1]
        o_ref[...] = jnp.zeros((SMALL_ROWS, D), F32)
        for l in range(2):
            dshift, dscale, dgate, g_pre, g_post, lb, pscale, gn = refs[l * nk:(l + 1) * nk]
            for r, ref in enumerate((dshift, dscale, dgate)):
                o_ref[3 * l + r:3 * l + r + 1, :] = ref[...]
            o_ref[8 + l:9 + l, :] = g_pre[...]
            o_ref[16 + l:17 + l, :] = g_post[...]
            o_ref[24 + l:25 + l, :] = lb[...]
            o_ref[160:161, l * POOL_W:(l + 1) * POOL_W] = pscale[...]
            o_ref[168:169, l * HD:(l + 1) * HD] = gn[...]
            pw = refs[2 * nk + l]
            rows = pw.shape[0]
            o_ref[32 + l * rows:32 + (l + 1) * rows, :] = pw[...]

    return pl.pallas_call(body, name="pack_small", out_shape=jax.ShapeDtypeStruct((SMALL_ROWS, D), F32),
                          compiler_params=_cp())(*flat)


def _unpack_small(p):
    return (p[0:6].reshape(2, 3 * D), p[8:10], p[16:18], p[24:26], p[32:160].reshape(2, GROUPS, 128, 128),
            p[160:161].reshape(2, POOL_W), p[168:169, 0:2 * HD].reshape(2, HD))


def kernel(x, c, w_ada, b_ada, g_pre, g_post, w_in, pool_w, pool_scale, lb_logits, hgrn_norm_g, w_pool_o, w_hgrn_o, w_out, loss_target, m_w_ada, m_b_ada, m_g_pre, m_g_post, m_w_in, m_pool_w, m_pool_scale, m_lb_logits, m_hgrn_norm_g, m_w_pool_o, m_w_hgrn_o, m_w_out, v_w_ada, v_b_ada, v_g_pre, v_g_post, v_w_in, v_pool_w, v_pool_scale, v_lb_logits, v_hgrn_norm_g, v_w_pool_o, v_w_hgrn_o, v_w_out):
    ax, ay, ac = lax.axis_index("x"), lax.axis_index("y"), lax.axis_index("c")
    chip = 2 * ax + ay
    dev = 2 * chip + ac
    xe, te = x[0], loss_target[0]
    ada_s = w_ada.shape[2]

    big_names = ("w_in", "w_pool_o", "w_hgrn_o", "w_out")
    big_w = (w_in, w_pool_o, w_hgrn_o, w_out)
    core = jnp.stack([ac]).astype(jnp.int32)
    place = jnp.stack([chip, ac]).astype(jnp.int32)
    slots = {(k, l): _cast_to_slot(place, t, l, f"cast_{k}{l}") for l in range(2) for k, t in zip(big_names, big_w)}
    w = {k: [None, None] for k in big_names}
    def fills(keys):
        def landed(outs):
            for (k, l), o in zip(keys, outs):
                w[k][l] = slots[k, l] = o
        return landed

    rest0 = [(k, 0) for k in big_names[1:]]
    rest1 = [(k, 1) for k in big_names[1:]]
    no_carry = (None, lambda outs: None)
    order = jnp.stack([chip, 2 * (1 - ax) + ay, 2 * ax + (1 - ay), 2 * (1 - ax) + (1 - ay)]).astype(jnp.int32)

    def ride_fwd0(stage):
        if stage == "proj":
            return _GatherInProj(slots["w_in", 0], order), fills([("w_in", 0)])
        if stage == "hgrn":
            return (_join_carries(_gather_carry([slots[t] for t in rest0]),
                                  _gather_carry([slots["w_in", 1]], piece=(0, 2, 4))),
                    fills(rest0 + [("w_in", 1)]))
        if stage == "tail":
            return _gather_carry([slots["w_in", 1]], piece=(2, 1, 4)), fills([("w_in", 1)])
        return no_carry

    def ride_fwd1(stage):
        if stage == "prenorm":
            return _gather_carry([slots["w_in", 1]], piece=(3, 1, 4)), fills([("w_in", 1)])
        if stage == "hgrn":
            return _gather_carry([slots[t] for t in rest1]), fills(rest1)
        return no_carry

    c_all = _gather_small(jnp.broadcast_to(c, (8, D)), "gather_c").reshape(NDEV, 8, D)[:, 0, :]
    c_pad = jnp.pad(c_all, ((0, ADA_PAD - NDEV), (0, 0)))
    b_sh = lax.dynamic_slice(b_ada, (0, chip * ada_s), (2, ada_s))
    ada_cols = _gather_small(_ada_fwd(c_pad, w_ada, b_sh), "gather_ada")
    ada_cols = ada_cols.reshape(NCHIP, 2, NDEV, 2, ada_s)[:, 0]
    ada_all = jnp.transpose(ada_cols, (2, 1, 0, 3)).reshape(2, NDEV, 3 * D)
    ada_me = lax.dynamic_slice(ada_all, (0, dev, 0), (2, 1, 3 * D))

    lbs = _lb_fwd(lb_logits)
    small = dict(g_pre=g_pre[:, None, :], g_post=g_post[:, None, :], pool_w=pool_w,
                 pool_scale=pool_scale[:, None, :], lb=lbs[:, None, :], hgrn_norm_g=hgrn_norm_g[:, None, :])

    (x1,), sv0 = _layer_fwd(0, xe, ada_me[0], w, small, ride_fwd0)
    (dx2, loss_blk), sv1 = _layer_fwd(1, x1, ada_me[1], w, small, ride_fwd1, target=te)

    parts, recv = {}, {}

    def pair_sums(keys, grads, tag):
        got = _rs_pair(grads, f"rs_pair_{tag}")
        for kl, g, o in zip(keys, grads, got):
            parts[kl] = _pair_add(core, g, o, f"rs_add_{kl[0]}{kl[1]}")

    def exchange(keys):
        def landed(outs):
            recv.update(zip(keys, outs))
        return _chips_carry([parts[kl] for kl in keys]), landed

    def early(l):
        return [(k, l) for k in big_names[1:]]

    def ride_hgrn1(big):
        pair_sums(early(1), [big[k] for k in big_names[1:]], "l1_early")
        return exchange(early(1))

    def halves(key, second):
        def landed(outs):
            (recv[key],) = outs
        if second:
            return _chips_carry([parts[key]], piece=(1, 1, 2), into=[recv[key]]), landed
        return _chips_carry([parts[key]], piece=(0, 1, 2)), landed

    def ride_d_h1(big):
        pair_sums([("w_in", 1)], [big["w_in"]], "l1_w_in")
        return halves(("w_in", 1), False)

    def ride_head0(_):
        return halves(("w_in", 1), True)

    def ride_hgrn0(big):
        pair_sums(early(0), [big[k] for k in big_names[1:]], "l0_early")
        return exchange(early(0))

    def ride_d_h0(big):
        pair_sums([("w_in", 0)], [big["w_in"]], "l0_w_in")
        return halves(("w_in", 0), False)

    def ride_prenorm0(big):
        return halves(("w_in", 0), True)

    no_ride = lambda so_far: no_carry
    dx1, big1, little1 = _layer_bwd(1, dx2, sv1, w, small, dict(head=no_ride, hgrn=ride_hgrn1, gw_in=no_ride,
                                                                d_h=ride_d_h1, prenorm=no_ride))

    gathered = {}
    zero_row = jnp.zeros((1, D), F32)

    def ride_gw_in0(little):
        so_far = dict(little, dshift=zero_row, dscale=zero_row, g_pre=zero_row)

        def landed(outs):
            (gathered["early"],) = outs
        return _gather_rows_carry(_pack_small([so_far, little1])), landed

    dx0, big0, little0 = _layer_bwd(0, dx1, sv0, w, small,
                                    dict(head=ride_head0, hgrn=ride_hgrn0, gw_in=ride_gw_in0, d_h=ride_d_h0,
                                         prenorm=ride_prenorm0))
    loss = lax.psum(loss_blk[0, 0], ("x", "y", "c"))
    late = _rows8(jnp.stack([little0["dshift"], little0["dscale"], little0["g_pre"]]))
    late = _gather_small(late, "gather_small_late").reshape(NDEV, 8, D)
    packed = gathered["early"].reshape(NDEV, SMALL_ROWS, D)
    packed = packed.at[:, 0:2, :].set(late[:, 0:2, :]).at[:, 8:9, :].set(late[:, 2:3, :])
    red = []
    for k in big_names:
        both = _chip_sum(place, parts[k, 1], recv[k, 1], 1, None, f"rs_sum_{k}1")
        red.append(_chip_sum(place, parts[k, 0], recv[k, 0], 0, both, f"rs_sum_{k}0"))
    g_big = dict(zip(big_names, _rs_swap(red)))

    def upd(wt, g, m, v, name, carry=None):
        shp = wt.shape
        two = lambda t: t.reshape(-1, shp[-1])
        res = _adamw(two(wt), two(g), two(m), two(v), name, carry)
        return [t.reshape(shp) for t in res[:3]], res[3:]

    u_w_in, _ = upd(w_in, g_big["w_in"], m_w_in, v_w_in, "adamw_w_in")
    g_small = _sum_devices(packed)
    g_b_ada, g_g_pre, g_g_post, g_lb, g_pool_w, g_pool_scale, g_norm_g = _unpack_small(g_small)
    g_lb_logits = _lb_bwd(lb_logits, g_lb)
    d_ada_all = packed[:, 0:6, :].reshape(NDEV, 2, 3 * D)
    d_ada_sh = lax.dynamic_slice(jnp.transpose(d_ada_all, (1, 0, 2)), (0, 0, chip * ada_s), (2, NDEV, ada_s))
    d_ada_sh = jnp.pad(d_ada_sh, ((0, 0), (0, ADA_PAD - NDEV), (0, 0)))
    g_w_ada = _ada_wgrad(c_pad.T, d_ada_sh)

    u_w_ada, _ = upd(w_ada, g_w_ada, m_w_ada, v_w_ada, "adamw_w_ada")
    u_w_pool_o, _ = upd(w_pool_o, g_big["w_pool_o"], m_w_pool_o, v_w_pool_o, "adamw_w_pool_o")
    u_w_hgrn_o, _ = upd(w_hgrn_o, g_big["w_hgrn_o"], m_w_hgrn_o, v_w_hgrn_o, "adamw_w_hgrn_o")
    u_w_out, _ = upd(w_out, g_big["w_out"], m_w_out, v_w_out, "adamw_w_out")
    small_w = dict(b_ada=(b_ada, m_b_ada, v_b_ada), g_pre=(g_pre, m_g_pre, v_g_pre),
                   g_post=(g_post, m_g_post, v_g_post), lb_logits=(lb_logits, m_lb_logits, v_lb_logits),
                   pool_w=(pool_w, m_pool_w, v_pool_w), pool_scale=(pool_scale, m_pool_scale, v_pool_scale),
                   hgrn_norm_g=(hgrn_norm_g, m_hgrn_norm_g, v_hgrn_norm_g))
    in_rows = [tuple(t.reshape(rows, width) for t in small_w[key]) for key, _, rows, width in SMALL_PARTS]
    u_rows = _adamw_small(g_small, g_lb_logits, in_rows)
    u_small = {key: [t.reshape(small_w[key][0].shape) for t in u_rows[p]]
               for p, (key, _, _, _) in enumerate(SMALL_PARTS)}

    grads_out = (g_w_ada, g_b_ada, g_g_pre, g_g_post, g_big["w_in"], g_pool_w, g_pool_scale, g_lb_logits,
                 g_norm_g, g_big["w_pool_o"], g_big["w_hgrn_o"], g_big["w_out"])

    def ordered(k):
        s = lambda key: u_small[key][k]
        return (u_w_ada[k], s("b_ada"), s("g_pre"), s("g_post"), u_w_in[k], s("pool_w"), s("pool_scale"),
                s("lb_logits"), s("hgrn_norm_g"), u_w_pool_o[k], u_w_hgrn_o[k], u_w_out[k])

    return (loss, dx0[None], *grads_out, *ordered(0), *ordered(1), *ordered(2))
```

```python
import functools

import jax
import jax.numpy as jnp
from jax import lax
from jax.experimental import pallas as pl
from jax.experimental.pallas import tpu as pltpu

F32 = jnp.float32
BF16 = jnp.bfloat16
MESH = pl.DeviceIdType.MESH

D = 1024
HEADS = 8
HD = 128
GROUPS = 4
POOL_W = 512
WINDOWS = (2, 4, 8, 16)
CH = 128
SB_WIDE = 32
SB = 16
NH = 2
IN_W = 7168
NCHIP = 4
NDEV = 8
EPS = 1e-6
PV0, PG0, HQ0, HF0, HI0, HG0 = 0, 4, 8, 16, 24, 32
MGP_BLK, MGH_BLK = 5, 6

LR, B1, B2, AEPS, WD, STEP = 0.001, 0.9, 0.999, 1e-08, 0.01, 10
VMEM_LIMIT = 56 * 1024 * 1024


def _cp(sem=None, **kw):
    if sem is not None:
        kw["dimension_semantics"] = sem
    return pltpu.CompilerParams(vmem_limit_bytes=VMEM_LIMIT, **kw)


def _sig(z):
    return 1.0 / (1.0 + jnp.exp(-z))


def _dsilu(z, s):
    return s * (1.0 + z * (1.0 - s))


def _row_tile(rows, cap):
    if rows <= cap:
        return rows
    t = 1 << (cap.bit_length() - 1)
    while rows % t:
        t //= 2
    return t


ANY = pl.BlockSpec(memory_space=pl.ANY)


class _Carry:
    def __init__(self, ins, outs, aliases, n_sem, start, finish):
        self.ins, self.outs, self.aliases, self.n_sem = list(ins), list(outs), dict(aliases), n_sem
        self.start, self.finish = start, finish


class _SemWindow:
    def __init__(self, ref, base):
        self._ref, self._base = ref, base

    @property
    def at(self):
        return self

    def __getitem__(self, k):
        return self._ref.at[self._base + k]


def _join_carries(*carries):
    ins, outs, aliases, spans, n_sem = [], [], {}, [], 0
    for cr in carries:
        aliases.update({len(ins) + i: len(outs) + o for i, o in cr.aliases.items()})
        spans.append((len(ins), len(cr.ins), len(outs), len(cr.outs), n_sem))
        ins, outs, n_sem = ins + cr.ins, outs + cr.outs, n_sem + cr.n_sem

    def run(which):
        def fn(i_refs, o_refs, send_sems, recv_sems):
            for cr, (i0, ni, o0, no, s0) in zip(carries, spans):
                getattr(cr, which)(i_refs[i0:i0 + ni], o_refs[o0:o0 + no], _SemWindow(send_sems, s0),
                                   _SemWindow(recv_sems, s0))
        return fn

    return _Carry(ins, outs, aliases, n_sem, run("start"), run("finish"))


def _call(body, *, name, grid, in_specs, out_specs, out_shape, args, scratch_shapes=(), sem=None, carry=None,
          aliases=None):
    in_specs, out_specs, out_shape = list(in_specs), list(out_specs), list(out_shape)
    scratch_shapes = list(scratch_shapes)
    aliases = dict(aliases or {})
    if carry is None:
        outs = pl.pallas_call(body, name=name, grid=grid, in_specs=in_specs, out_specs=out_specs,
                              out_shape=out_shape, scratch_shapes=scratch_shapes, input_output_aliases=aliases,
                              compiler_params=_cp(sem))(*args)
        return list(outs)
    n_in, n_out, n_scr = len(in_specs), len(out_specs), len(scratch_shapes)
    c_in, c_out = len(carry.ins), len(carry.outs)

    def wrapped(*refs):
        k_in, rest = refs[:n_in], refs[n_in:]
        ci, rest = rest[:c_in], rest[c_in:]
        k_out, rest = rest[:n_out], rest[n_out:]
        co, rest = rest[:c_out], rest[c_out:]
        k_scr, (ssem, rsem) = rest[:n_scr], rest[n_scr:]
        pids = [pl.program_id(d) for d in range(len(grid))]
        first = functools.reduce(jnp.logical_and, [p == 0 for p in pids])
        last = functools.reduce(jnp.logical_and, [p == g - 1 for p, g in zip(pids, grid)])

        @pl.when(first)
        def _():
            carry.start(ci, co, ssem, rsem)

        body(*k_in, *k_out, *k_scr)

        @pl.when(last)
        def _():
            carry.finish(ci, co, ssem, rsem)

    outs = pl.pallas_call(
        wrapped, name=name, grid=grid, in_specs=in_specs + [ANY] * c_in, out_specs=out_specs + [ANY] * c_out,
        out_shape=out_shape + carry.outs,
        input_output_aliases={**aliases, **{n_in + i: n_out + o for i, o in carry.aliases.items()}},
        scratch_shapes=scratch_shapes + [pltpu.SemaphoreType.DMA((carry.n_sem,))] * 2,
        compiler_params=_cp(("arbitrary",) * len(grid)),
    )(*args, *carry.ins)
    return list(outs)


def _mm(a, b, *, name, b_mode="nn", out_shards=0, tm=1024, tn=256, tk=None, out_dtype=F32, carry=None):
    M, K = a.shape
    if b_mode == "nn":
        N = b.shape[1]
    elif b_mode == "nt":
        N = b.shape[0]
    elif b_mode == "nn_sh":
        N = b.shape[0] * b.shape[2]
    else:
        N = b.shape[1]
    tm = _row_tile(M, tm)
    if b_mode == "nn_sh":
        tn = _row_tile(b.shape[2], tn)
    elif out_shards:
        tn = _row_tile(N // out_shards, tn)
    else:
        tn = _row_tile(N, tn)
    if tk is None:
        tk = K if b_mode != "nt_shk" else b.shape[2]
    if b_mode == "nt_shk":
        tk = _row_tile(b.shape[2], tk)
    nm, nn, nk = M // tm, N // tn, K // tk

    a_spec = pl.BlockSpec((tm, tk), lambda m, n, k: (m, k))
    if b_mode == "nn":
        b_spec = pl.BlockSpec((tk, tn), lambda m, n, k: (k, n))
    elif b_mode == "nt":
        b_spec = pl.BlockSpec((tn, tk), lambda m, n, k: (n, k))
    elif b_mode == "nn_sh":
        nps = b.shape[2] // tn
        b_spec = pl.BlockSpec((None, tk, tn), lambda m, n, k: (n // nps, k, n % nps))
    else:
        kps = b.shape[2] // tk
        b_spec = pl.BlockSpec((None, tn, tk), lambda m, n, k: (k // kps, n, k % kps))
    if out_shards:
        ops = (N // out_shards) // tn
        o_spec = pl.BlockSpec((None, tm, tn), lambda m, n, k: (n // ops, m, n % ops))
        o_shape = jax.ShapeDtypeStruct((out_shards, M, N // out_shards), out_dtype)
    else:
        o_spec = pl.BlockSpec((tm, tn), lambda m, n, k: (m, n))
        o_shape = jax.ShapeDtypeStruct((M, N), out_dtype)
    trans_b = b_mode in ("nt", "nt_shk")
    dn = (((1,), (1,)), ((), ())) if trans_b else (((1,), (0,)), ((), ()))

    def body(a_ref, b_ref, o_ref, acc_ref):
        k = pl.program_id(2)

        @pl.when(k == 0)
        def _():
            acc_ref[...] = jnp.zeros(acc_ref.shape, F32)

        acc_ref[...] += lax.dot_general(a_ref[...].astype(BF16), b_ref[...].astype(BF16), dn,
                                        preferred_element_type=F32)

        @pl.when(k == nk - 1)
        def _():
            o_ref[...] = acc_ref[...].astype(o_ref.dtype)

    outs = _call(body, name=name, grid=(nm, nn, nk), in_specs=[a_spec, b_spec], out_specs=[o_spec],
                 out_shape=[o_shape], scratch_shapes=[pltpu.VMEM((tm, tn), F32)],
                 sem=("parallel", "parallel", "arbitrary"), args=(a, b), carry=carry)
    return outs[0] if carry is None else (outs[0], outs[1:])


def _rowvec(n=D):
    return pl.BlockSpec((1, n), lambda i: (0, 0))


def _prenorm_fwd(x, g, scale, shift, name, carry=None):
    S = x.shape[0]
    tr = _row_tile(S, 256)

    def body(x_ref, g_ref, sc_ref, sh_ref, h_ref, ht_ref):
        xv = x_ref[...]
        r = lax.rsqrt(jnp.mean(xv * xv, axis=-1, keepdims=True) + EPS)
        hv = (xv * r) * g_ref[...] * (1.0 + sc_ref[...]) + sh_ref[...]
        h_ref[...] = hv.astype(BF16)
        ht_ref[...] = hv.T.astype(BF16)

    outs = _call(
        body, name=name, grid=(S // tr,),
        in_specs=[pl.BlockSpec((tr, D), lambda i: (i, 0)), _rowvec(), _rowvec(), _rowvec()],
        out_specs=[pl.BlockSpec((tr, D), lambda i: (i, 0)), pl.BlockSpec((D, tr), lambda i: (0, i))],
        out_shape=[jax.ShapeDtypeStruct((S, D), BF16), jax.ShapeDtypeStruct((D, S), BF16)],
        sem=("parallel",), args=(x, g, scale, shift), carry=carry)
    return outs[:2], outs[2:]


def _prenorm_bwd(dh, dxn, x, g, scale, name, carry=None):
    S = x.shape[0]
    tr = _row_tile(S, 256)

    def body(dh_ref, dxn_ref, x_ref, g_ref, sc_ref, dx_ref, dsh_ref, dsc_ref, dg_ref):
        i = pl.program_id(0)

        @pl.when(i == 0)
        def _():
            dsh_ref[...] = jnp.zeros((1, D), F32)
            dsc_ref[...] = jnp.zeros((1, D), F32)
            dg_ref[...] = jnp.zeros((1, D), F32)

        xv = x_ref[...]
        dhv = dh_ref[...]
        gv = g_ref[...]
        mod = 1.0 + sc_ref[...]
        r = lax.rsqrt(jnp.mean(xv * xv, axis=-1, keepdims=True) + EPS)
        xh = xv * r
        dsh_ref[...] += jnp.sum(dhv, axis=0, keepdims=True)
        dsc_ref[...] += jnp.sum(dhv * (xh * gv), axis=0, keepdims=True)
        dg_ref[...] += jnp.sum(dhv * mod * xh, axis=0, keepdims=True)
        u = dhv * mod * gv
        dx_ref[...] = dxn_ref[...] + r * u - xv * (r * r * r) * jnp.mean(u * xv, axis=-1, keepdims=True)

    tile = pl.BlockSpec((tr, D), lambda i: (i, 0))
    outs = _call(
        body, name=name, grid=(S // tr,),
        in_specs=[tile, tile, tile, _rowvec(), _rowvec()],
        out_specs=[tile, _rowvec(), _rowvec(), _rowvec()],
        out_shape=[jax.ShapeDtypeStruct((S, D), F32)] + [jax.ShapeDtypeStruct((1, D), F32)] * 3,
        sem=("arbitrary",), args=(dh, dxn, x, g, scale), carry=carry)
    return outs[:4], outs[4:]


def _layer_tail_fwd(proj, a_in, b_in, x, w_po, w_ho, w_out, gate, g, name, target=None, carry=None):
    S = proj.shape[0]
    tr = _row_tile(S, 256)
    nsh, _, wsh = w_po.shape
    n_in = 10 + (target is not None)

    def body(*refs):
        (mgp_ref, mgh_ref, a_ref, b_ref, x_ref, wpo_ref, who_ref, wout_ref, gate_ref, g_ref) = refs[:10]
        bra_ref, brb_ref, mt_ref, y_ref, xn_ref = refs[n_in:n_in + 5]
        av = a_ref[...]
        bra = jnp.concatenate([jnp.dot(av, wpo_ref[j], preferred_element_type=F32) for j in range(nsh)], axis=1)
        brb = jnp.dot(b_ref[...], who_ref[...], preferred_element_type=F32)
        mv = _sig(mgp_ref[...].astype(F32)) * bra + _sig(mgh_ref[...].astype(F32)) * brb
        bra_ref[...] = bra.astype(BF16)
        brb_ref[...] = brb.astype(BF16)
        mt_ref[...] = mv.T.astype(BF16)
        yv = jnp.dot(mv.astype(BF16), wout_ref[...], preferred_element_type=F32)
        y_ref[...] = yv
        r = lax.rsqrt(jnp.mean(yv * yv, axis=-1, keepdims=True) + EPS)
        xn = x_ref[...] + gate_ref[...] * ((yv * r) * g_ref[...])
        if target is None:
            xn_ref[...] = xn
        else:
            t_ref, l_ref = refs[10], refs[n_in + 5]

            @pl.when(pl.program_id(0) == 0)
            def _():
                l_ref[...] = jnp.zeros((8, 128), F32)

            err = xn - t_ref[...]
            xn_ref[...] = err * (1.0 / D)
            l_ref[...] += 0.5 * jnp.sum(jnp.mean(err * err, axis=-1, keepdims=True))

    tile = pl.BlockSpec((tr, D), lambda i: (i, 0))
    whole = lambda t: pl.BlockSpec(t.shape, lambda i: (0,) * t.ndim)
    last = target is not None
    outs = _call(
        body, name=name, grid=(S // tr,),
        in_specs=[pl.BlockSpec((tr, D), lambda i: (i, MGP_BLK)), pl.BlockSpec((tr, D), lambda i: (i, MGH_BLK)),
                  pl.BlockSpec((tr, POOL_W), lambda i: (i, 0)), tile, tile, whole(w_po), whole(w_ho),
                  whole(w_out), _rowvec(), _rowvec()] + [tile] * last,
        out_specs=[tile, tile, pl.BlockSpec((D, tr), lambda i: (0, i)), tile, tile]
        + [pl.BlockSpec((8, 128), lambda i: (0, 0))] * last,
        out_shape=[jax.ShapeDtypeStruct((S, D), BF16), jax.ShapeDtypeStruct((S, D), BF16),
                   jax.ShapeDtypeStruct((D, S), BF16), jax.ShapeDtypeStruct((S, D), F32),
                   jax.ShapeDtypeStruct((S, D), F32)] + [jax.ShapeDtypeStruct((8, 128), F32)] * last,
        sem=("arbitrary",) if last else ("parallel",),
        args=(proj, proj, a_in, b_in, x, w_po, w_ho, w_out, gate, g) + ((target,) if last else ()), carry=carry)
    return outs[:5 + last], outs[5 + last:]


def _layer_head_bwd(dxn, y, proj, br_a, br_b, w_po, w_ho, w_out, gate, g, name, carry=None):
    S = y.shape[0]
    tr = _row_tile(S, 256)
    nsh, _, wsh = w_po.shape

    def body(dxn_ref, y_ref, mgp_ref, mgh_ref, bra_ref, brb_ref, wpo_ref, who_ref, wout_ref, gate_ref, g_ref,
             dy_ref, dba_ref, dbb_ref, dproj_ref, dain_ref, dbin_ref, dgate_ref, dg_ref, dmgh_s):
        i = pl.program_id(0)
        j = pl.program_id(1)

        @pl.when((i == 0) & (j == 0))
        def _():
            dgate_ref[...] = jnp.zeros((1, D), F32)
            dg_ref[...] = jnp.zeros((1, D), F32)

        @pl.when(j == 1)
        def _():
            dproj_ref[...] = dmgh_s[...]

        @pl.when(j == 0)
        def _():
            everything(dxn_ref, y_ref, mgp_ref, mgh_ref, bra_ref, brb_ref, wpo_ref, who_ref, wout_ref, gate_ref,
                       g_ref, dy_ref, dba_ref, dbb_ref, dproj_ref, dain_ref, dbin_ref, dgate_ref, dg_ref, dmgh_s)

    def everything(dxn_ref, y_ref, mgp_ref, mgh_ref, bra_ref, brb_ref, wpo_ref, who_ref, wout_ref, gate_ref, g_ref,
                   dy_ref, dba_ref, dbb_ref, dproj_ref, dain_ref, dbin_ref, dgate_ref, dg_ref, dmgh_s):
        yv = y_ref[...]
        dv = dxn_ref[...]
        gv = g_ref[...]
        gt = gate_ref[...]
        r = lax.rsqrt(jnp.mean(yv * yv, axis=-1, keepdims=True) + EPS)
        yh = yv * r
        dgate_ref[...] += jnp.sum(dv * (yh * gv), axis=0, keepdims=True)
        dg_ref[...] += jnp.sum(dv * gt * yh, axis=0, keepdims=True)
        u = dv * gt * gv
        dy = (r * u - yv * (r * r * r) * jnp.mean(u * yv, axis=-1, keepdims=True)).astype(BF16)
        dy_ref[...] = dy
        dm = _dot_nt(dy, wout_ref[...])
        sp = _sig(mgp_ref[...].astype(F32))
        sh = _sig(mgh_ref[...].astype(F32))
        dba = (dm * sp).astype(BF16)
        dbb = (dm * sh).astype(BF16)
        dba_ref[...] = dba
        dbb_ref[...] = dbb
        dproj_ref[...] = (dm * bra_ref[...].astype(F32) * sp * (1.0 - sp)).astype(BF16)
        dmgh_s[...] = (dm * brb_ref[...].astype(F32) * sh * (1.0 - sh)).astype(BF16)
        dain = _dot_nt(dba[:, 0:wsh], wpo_ref[0])
        for k in range(1, nsh):
            dain = dain + _dot_nt(dba[:, k * wsh:(k + 1) * wsh], wpo_ref[k])
        dain_ref[...] = dain
        dbin_ref[...] = _dot_nt(dbb, who_ref[...])

    tile = pl.BlockSpec((tr, D), lambda i, j: (i, 0))
    whole = lambda t: pl.BlockSpec(t.shape, lambda i, j: (0,) * t.ndim)
    vec = pl.BlockSpec((1, D), lambda i, j: (0, 0))
    ahead = lambda i, j: jnp.minimum(i + j, S // tr - 1)
    tile_in = pl.BlockSpec((tr, D), lambda i, j: (ahead(i, j), 0))
    outs = _call(
        body, name=name, grid=(S // tr, 2),
        in_specs=[tile_in, tile_in, pl.BlockSpec((tr, D), lambda i, j: (ahead(i, j), MGP_BLK)),
                  pl.BlockSpec((tr, D), lambda i, j: (ahead(i, j), MGH_BLK)), tile_in, tile_in, whole(w_po),
                  whole(w_ho), whole(w_out), vec, vec],
        out_specs=[tile, tile, tile, pl.BlockSpec((tr, D), lambda i, j: (i, MGP_BLK + j)),
                   pl.BlockSpec((tr, POOL_W), lambda i, j: (i, 0)), tile, vec, vec],
        out_shape=[jax.ShapeDtypeStruct((S, D), BF16)] * 3
        + [jax.ShapeDtypeStruct((S, IN_W), BF16), jax.ShapeDtypeStruct((S, POOL_W), F32),
           jax.ShapeDtypeStruct((S, D), F32), jax.ShapeDtypeStruct((1, D), F32), jax.ShapeDtypeStruct((1, D), F32)],
        scratch_shapes=[pltpu.VMEM((tr, D), BF16)], sem=("arbitrary", "arbitrary"),
        args=(dxn, y, proj, proj, br_a, br_b, w_po, w_ho, w_out, gate, g), carry=carry)
    return outs[:8], outs[8:]


def _pool_pieces(u, g, S):
    rowi = lax.broadcasted_iota(jnp.int32, (S, 1), 0)

    def down(z, k):
        return jnp.where(rowi >= k, pltpu.roll(z, k, axis=0), 0.0)

    s2 = u + down(u, 1)
    s4 = s2 + down(s2, 2)
    s8 = s4 + down(s4, 4)
    s16 = s8 + down(s8, 8)
    win = jnp.where(g == 0, s2, jnp.where(g == 1, s4, jnp.where(g == 2, s8, s16)))
    w = jnp.where(g == 0, 2, jnp.where(g == 1, 4, jnp.where(g == 2, 8, 16)))
    count = jnp.minimum(rowi + 1, w).astype(F32)
    return win / count - u, count, rowi


def _pool_fwd(proj, pw, pscale, name):
    S = proj.shape[0]

    def body(pv_ref, pg_ref, pw_ref, sc_ref, a_ref, at_ref):
        g = pl.program_id(0)
        pooled, _, _ = _pool_pieces(pv_ref[...].astype(F32), g, S)
        pm = jnp.dot(pooled.astype(BF16), pw_ref[...].astype(BF16), preferred_element_type=F32)
        pgv = pg_ref[...].astype(F32)
        av = pm * sc_ref[...] * (pgv * _sig(pgv))
        a_ref[...] = av.astype(BF16)
        at_ref[...] = av.T.astype(BF16)

    return pl.pallas_call(
        body, name=name, grid=(GROUPS,),
        in_specs=[pl.BlockSpec((S, 128), lambda g: (0, PV0 + g)), pl.BlockSpec((S, 128), lambda g: (0, PG0 + g)),
                  pl.BlockSpec((None, 128, 128), lambda g: (g, 0, 0)), pl.BlockSpec((1, 128), lambda g: (0, g))],
        out_specs=[pl.BlockSpec((S, 128), lambda g: (0, g)), pl.BlockSpec((128, S), lambda g: (g, 0))],
        out_shape=[jax.ShapeDtypeStruct((S, POOL_W), BF16), jax.ShapeDtypeStruct((POOL_W, S), BF16)],
        compiler_params=_cp(("parallel",)),
    )(proj, proj, pw, pscale)


def _pool_bwd(da, proj, pw, pscale, dproj, name):
    S = proj.shape[0]

    def body(da_ref, pv_ref, pg_ref, pw_ref, sc_ref, dproj_in, dproj_ref, dpw_ref, dsc_ref, dpg_s):
        @pl.when(pl.program_id(1) == 1)
        def _():
            dproj_ref[...] = dpg_s[...]

        @pl.when(pl.program_id(1) == 0)
        def _():
            group(da_ref, pv_ref, pg_ref, pw_ref, sc_ref, dproj_ref, dpg_s, dpw_ref, dsc_ref)

    def group(da_ref, pv_ref, pg_ref, pw_ref, sc_ref, dpv_ref, dpg_ref, dpw_ref, dsc_ref):
        g = pl.program_id(0)
        pooled, count, rowi = _pool_pieces(pv_ref[...].astype(F32), g, S)
        pwb = pw_ref[...].astype(BF16)
        pm = jnp.dot(pooled.astype(BF16), pwb, preferred_element_type=F32)
        scv = sc_ref[...]
        pgv = pg_ref[...].astype(F32)
        sg = _sig(pgv)
        dav = da_ref[...]
        d_ps = dav * (pgv * sg)
        dpg_ref[...] = (dav * (pm * scv) * _dsilu(pgv, sg)).astype(BF16)
        dsc_ref[...] = jnp.sum(d_ps * pm, axis=0, keepdims=True)
        d_pm = (d_ps * scv).astype(BF16)
        dpw_ref[...] = lax.dot_general(pooled.astype(BF16), d_pm, (((0,), (0,)), ((), ())),
                                       preferred_element_type=F32)
        d_pooled = lax.dot_general(d_pm, pwb, (((1,), (1,)), ((), ())), preferred_element_type=F32)
        z = d_pooled / count

        def up(v, k):
            return jnp.where(rowi < S - k, pltpu.roll(v, S - k, axis=0), 0.0)

        t2 = z + up(z, 1)
        t4 = t2 + up(t2, 2)
        t8 = t4 + up(t4, 4)
        t16 = t8 + up(t8, 8)
        adj = jnp.where(g == 0, t2, jnp.where(g == 1, t4, jnp.where(g == 2, t8, t16)))
        dpv_ref[...] = (adj - d_pooled).astype(BF16)

    col = lambda g, j: (0, g)
    ahead = lambda g, j: jnp.minimum(g + j, GROUPS - 1)
    return pl.pallas_call(
        body, name=name, grid=(GROUPS, 2),
        in_specs=[pl.BlockSpec((S, 128), lambda g, j: (0, ahead(g, j))),
                  pl.BlockSpec((S, 128), lambda g, j: (0, PV0 + ahead(g, j))),
                  pl.BlockSpec((S, 128), lambda g, j: (0, PG0 + ahead(g, j))),
                  pl.BlockSpec((None, 128, 128), lambda g, j: (ahead(g, j), 0, 0)),
                  pl.BlockSpec((1, 128), lambda g, j: (0, ahead(g, j))), ANY],
        out_specs=[pl.BlockSpec((S, 128), lambda g, j: (0, PV0 + g + (PG0 - PV0) * j)),
                   pl.BlockSpec((None, 128, 128), lambda g, j: (g, 0, 0)), pl.BlockSpec((1, 128), col)],
        out_shape=[jax.ShapeDtypeStruct(dproj.shape, dproj.dtype),
                   jax.ShapeDtypeStruct((GROUPS, 128, 128), F32), jax.ShapeDtypeStruct((1, POOL_W), F32)],
        scratch_shapes=[pltpu.VMEM((S, 128), BF16)], input_output_aliases={5: 0},
        compiler_params=_cp(("arbitrary", "arbitrary")),
    )(da, proj, proj, pw, pscale, dproj)


SCAN_SHIFTS = tuple(1 << b for b in range(CH.bit_length() - 1))


def _chunk_cumsum(z, rowi):
    for sh in SCAN_SHIFTS:
        z = z + jnp.where(rowi >= sh, pltpu.roll(z, sh, axis=0), 0.0)
    return z


def _chunk_rev_cumsum(z, rowi):
    for sh in SCAN_SHIFTS:
        z = z + jnp.where(rowi < CH - sh, pltpu.roll(z, CH - sh, axis=0), 0.0)
    return z


def _dot_nn(a, b):
    return jnp.dot(a.astype(BF16), b.astype(BF16), preferred_element_type=F32)


def _dot_nt(a, b):
    return lax.dot_general(a.astype(BF16), b.astype(BF16), (((1,), (1,)), ((), ())), preferred_element_type=F32)


def _dot_tn(a, b):
    return lax.dot_general(a.astype(BF16), b.astype(BF16), (((0,), (0,)), ((), ())), preferred_element_type=F32)


def _gates(hq, hf, lbv):
    hq, hf = hq.astype(F32), hf.astype(F32)
    sq = _sig(hq)
    sf = _sig(hf)
    f = lbv + (1.0 - lbv) * sf
    fc = jnp.maximum(f, 1e-30)
    return hq * sq, sq, sf, f, fc, jnp.log(fc)


DECAY_CAP = 60.0


def _block_ref(c_ref, i, sb):
    if i == 0:
        return jnp.zeros((1, HD), F32)
    return c_ref[sb * i - 1:sb * i, :]


def _block_decay(c_ref, sb):
    spans = [_block_ref(c_ref, i, sb) - c_ref[sb * (i + 1) - 1:sb * (i + 1), :] for i in range(CH // sb)]
    return functools.reduce(jnp.maximum, spans)


def _pair_factors(q_ref, k, c_ref, first, cap, round_bf16, sb):
    nb = CH // sb
    c = c_ref[...]
    zero = jnp.zeros((sb, HD), F32)
    q_groups, k_groups, eqs, eks = [], [], [], []
    for i in range(first, nb):
        blk = slice(sb * i, sb * (i + 1))
        r_i = _block_ref(c_ref, i, sb)
        eq = jnp.exp(jnp.minimum(c_ref[blk, :] - r_i, 0.0))
        ek = jnp.exp(jnp.minimum(r_i - c, cap))
        qi, kei = q_ref[blk, :] * eq, k * ek
        if round_bf16:
            qi, kei = qi.astype(BF16).astype(F32), kei.astype(BF16).astype(F32)
        q_groups.append(jnp.concatenate([zero] * i + [qi] + [zero] * (nb - 1 - i), axis=0))
        k_groups.append(kei)
        eqs.append(eq)
        eks.append(ek)
    return jnp.concatenate(q_groups, axis=1), jnp.concatenate(k_groups, axis=1), eqs, eks


def _pair_mask(rowi, coli, strict, sb):
    return (coli < jnp.bitwise_and(rowi, -sb)) if strict else (coli <= rowi)


def _hgrn_fwd(proj, lb, gn, name, carry=None):
    S = proj.shape[0]
    nch = S // CH
    W = NH * HD

    def body(hq_ref, hf_ref, hi_ref, hg_ref, lb_ref, gn_ref, bin_ref, bint_ref, oraw_ref, st_ref, mild_ref,
             cum_ref, q_s, k_s, c_s, v_s, o_s, state_s, qf_s, kf_s, cf_s):
        state_s[...] = jnp.zeros((NH, HD, HD), F32)
        rowi = lax.broadcasted_iota(jnp.int32, (CH, 1), 0)
        coli = lax.broadcasted_iota(jnp.int32, (1, CH), 1)
        sbi = lax.broadcasted_iota(jnp.int32, (SB, 1), 0)
        gnv = gn_ref[...]

        def gates_pass(n, worst):
            wide, narrow = worst
            rows = pl.ds(pl.multiple_of(n * CH, CH), CH)
            for hh in range(NH):
                lanes = slice(hh * HD, (hh + 1) * HD)
                q, _, _, f, _, logf = _gates(hq_ref[rows, lanes], hf_ref[rows, lanes], lb_ref[:, lanes])
                c = _chunk_cumsum(logf, rowi)
                qf_s[hh, rows, :] = q
                kf_s[hh, rows, :] = 1.0 - f
                cf_s[hh, rows, :] = c
                cum_ref[rows, lanes] = c
                c_s[hh] = c
                wide = jnp.maximum(wide, _block_decay(c_s.at[hh], SB_WIDE))
                narrow = jnp.maximum(narrow, _block_decay(c_s.at[hh], SB))
            return wide, narrow

        def between_chunks(hh, n, rows):
            lanes = slice(hh * HD, (hh + 1) * HD)
            q = qf_s[hh, rows, :]
            k = kf_s[hh, rows, :]
            c = cf_s[hh, rows, :]
            v = hi_ref[rows, lanes].astype(F32)
            q_s[hh] = q
            k_s[hh] = k
            c_s[hh] = c
            v_s[hh] = v
            st = state_s[hh]
            st_ref[hh, n] = st.astype(BF16)
            o_s[hh] = _dot_nt(q * jnp.exp(c), st)
            last = c_s[hh, CH - 1:CH, :]
            state_s[hh] = st * jnp.exp(last) + _dot_tn(v, k * jnp.exp(last - c))

        def pairs_matmul(hh, first, cap, strict, sb):
            qx, kc, _, _ = _pair_factors(q_s.at[hh], k_s[hh], c_s.at[hh], first, cap, False, sb)
            a = jnp.where(_pair_mask(rowi, coli, strict, sb), _dot_nt(qx, kc), 0.0)
            o_s[hh] += _dot_nn(a, v_s[hh])

        def within_chunk_matmul(sb):
            return lambda hh: pairs_matmul(hh, 0, DECAY_CAP, False, sb)

        def within_chunk_exact(hh):
            pairs_matmul(hh, 1, 0.0, True, SB)
            for i in range(CH // SB):
                blk = slice(SB * i, SB * (i + 1))
                qb = q_s[hh, blk, :]
                cb = c_s[hh, blk, :]
                acc = jnp.zeros((SB, HD), F32)
                for s in range(SB):
                    row = SB * i + s
                    w = jnp.exp(jnp.minimum(cb - c_s[hh, row:row + 1, :], 0.0))
                    a_col = jnp.sum(qb * k_s[hh, row:row + 1, :] * w, axis=-1, keepdims=True)
                    acc = acc + jnp.where(sbi >= s, a_col, 0.0) * v_s[hh, row:row + 1, :]
                o_s[hh, blk, :] += acc

        def norm_and_gate(hh, rows):
            lanes = slice(hh * HD, (hh + 1) * HD)
            ov = o_s[hh]
            oraw_ref[rows, lanes] = ov
            r = lax.rsqrt(jnp.mean(ov * ov, axis=-1, keepdims=True) + EPS)
            hg = hg_ref[rows, lanes].astype(F32)
            bin_ref[rows, lanes] = ((ov * r) * gnv * (hg * _sig(hg))).astype(BF16)

        def chunk_with(within_chunk):
            def chunk(n, carry):
                rows = pl.ds(pl.multiple_of(n * CH, CH), CH)
                for hh in range(NH):
                    between_chunks(hh, n, rows)
                for hh in range(NH):
                    within_chunk(hh)
                for hh in range(NH):
                    norm_and_gate(hh, rows)
                return carry
            return chunk

        none = jnp.zeros((1, HD), F32)
        wide, narrow = lax.fori_loop(0, nch, gates_pass, (none, none))
        tier = jnp.where(jnp.max(wide) <= DECAY_CAP, 2.0, jnp.where(jnp.max(narrow) <= DECAY_CAP, 1.0, 0.0))
        mild_ref[...] = jnp.broadcast_to(tier, (8, HD))

        @pl.when(tier == 2.0)
        def _():
            lax.fori_loop(0, nch, chunk_with(within_chunk_matmul(SB_WIDE)), 0, unroll=4)

        @pl.when(tier == 1.0)
        def _():
            lax.fori_loop(0, nch, chunk_with(within_chunk_matmul(SB)), 0, unroll=2)

        @pl.when(tier == 0.0)
        def _():
            lax.fori_loop(0, nch, chunk_with(within_chunk_exact), 0)

        bint_ref[...] = bin_ref[...].astype(F32).T.astype(BF16)

    col = lambda off: pl.BlockSpec((S, W), lambda h: (0, off // NH + h))
    head = pl.BlockSpec((S, W), lambda h: (0, h))
    outs = _call(
        body, name=name, grid=(HEADS // NH,),
        in_specs=[col(HQ0), col(HF0), col(HI0), col(HG0), pl.BlockSpec((1, W), lambda h: (0, h)),
                  pl.BlockSpec((1, HD), lambda h: (0, 0))],
        out_specs=[head, pl.BlockSpec((W, S), lambda h: (h, 0)), head,
                   pl.BlockSpec((NH, nch, HD, HD), lambda h: (h, 0, 0, 0)),
                   pl.BlockSpec((8, HD), lambda h: (h, 0)), head],
        out_shape=[jax.ShapeDtypeStruct((S, D), BF16), jax.ShapeDtypeStruct((D, S), BF16),
                   jax.ShapeDtypeStruct((S, D), F32), jax.ShapeDtypeStruct((HEADS, nch, HD, HD), BF16),
                   jax.ShapeDtypeStruct((8 * HEADS // NH, HD), F32), jax.ShapeDtypeStruct((S, D), F32)],
        scratch_shapes=[pltpu.VMEM((NH, CH, HD), F32)] * 5 + [pltpu.VMEM((NH, HD, HD), F32)]
        + [pltpu.VMEM((NH, S, HD), F32)] * 3,
        sem=("parallel",), args=(proj, proj, proj, proj, lb, gn), carry=carry)
    return outs[:6], outs[6:]


def _hgrn_bwd(dbin, proj, oraw, states, mild, cum, lb, gn, dproj, name, carry=None):
    S = proj.shape[0]
    nch = S // CH
    W = NH * HD
    n_in = 12

    def body(*refs):
        ins, (dproj_ref, dlb_ref, dgn_ref) = refs[:n_in - 1], refs[n_in:n_in + 3]
        scratch, later = refs[n_in + 3:-3], refs[-3:]
        seg = pl.program_id(1)

        @pl.when(seg == 0)
        def _():
            heads(*ins, dproj_ref, *later, dlb_ref, dgn_ref, *scratch)

        for s, kept in enumerate(later):
            @pl.when(seg == s + 1)
            def _(kept=kept):
                dproj_ref[...] = kept[...]

    def heads(db_ref, hq_ref, hf_ref, hi_ref, hg_ref, or_ref, st_ref, mild_ref, cum_ref, lb_ref, gn_ref,
              dq_ref, df_ref, di_ref, dg_ref, dlb_ref, dgn_ref,
              q_s, k_s, c_s, v_s, do_s, dq_s, dk_s, dv_s, dc_s, dqd_s, dkd_s, f_s, sf_s, sq_s, dl_s, dst_s,
              dlb_s, dgn_s):
        dst_s[...] = jnp.zeros((NH, HD, HD), F32)
        dlb_s[...] = jnp.zeros((1, W), F32)
        dgn_s[...] = jnp.zeros((1, HD), F32)
        rowi = lax.broadcasted_iota(jnp.int32, (CH, 1), 0)
        coli = lax.broadcasted_iota(jnp.int32, (1, CH), 1)
        sbi = lax.broadcasted_iota(jnp.int32, (SB, 1), 0)
        gnv = gn_ref[...]
        def between_chunks(hh, n, rows):
            lanes = slice(hh * HD, (hh + 1) * HD)
            lbv = lb_ref[:, lanes]
            hq = hq_ref[rows, lanes].astype(F32)
            sq = _sig(hq)
            sf = _sig(hf_ref[rows, lanes].astype(F32))
            f = lbv + (1.0 - lbv) * sf
            q = hq * sq
            k = 1.0 - f
            f_s[hh] = f
            sf_s[hh] = sf
            sq_s[hh] = sq
            v = hi_ref[rows, lanes].astype(F32)
            c = cum_ref[rows, lanes]
            ov = or_ref[rows, lanes]
            hg = hg_ref[rows, lanes].astype(F32)
            sg = _sig(hg)
            r = lax.rsqrt(jnp.mean(ov * ov, axis=-1, keepdims=True) + EPS)
            dbv = db_ref[rows, lanes]
            d_on = dbv * (hg * sg)
            dg_ref[rows, lanes] = (dbv * ((ov * r) * gnv) * _dsilu(hg, sg)).astype(BF16)
            dgn_s[...] += jnp.sum(d_on * (ov * r), axis=0, keepdims=True)
            u = d_on * gnv
            do = r * u - ov * (r * r * r) * jnp.mean(u * ov, axis=-1, keepdims=True)
            q_s[hh] = q
            k_s[hh] = k
            c_s[hh] = c
            v_s[hh] = v
            do_s[hh] = do
            st = st_ref[hh, n].astype(F32)
            dst = dst_s[hh]
            ec = jnp.exp(c)
            last = c_s[hh, CH - 1:CH, :]
            el = jnp.exp(last - c)
            elast = jnp.exp(last)
            dq = _dot_nn(do, st) * ec
            dk = _dot_nn(v, dst) * el
            dq_s[hh] = dq
            dk_s[hh] = dk
            dv_s[hh] = _dot_nt(k * el, dst)
            dc_s[hh] = q * dq - k * dk
            dl_s[hh] = (jnp.sum(k * dk, axis=0, keepdims=True)
                        + elast * jnp.sum(st * dst, axis=0, keepdims=True))
            dst_s[hh] = dst * elast + _dot_tn(do, q * ec)

        def pairs_matmul(hh, first, cap, strict, sb):
            do = do_s[hh]
            qx, kc, eqs, eks = _pair_factors(q_s.at[hh], k_s[hh], c_s.at[hh], first, cap, True, sb)
            mask = _pair_mask(rowi, coli, strict, sb)
            a = jnp.where(mask, _dot_nt(qx, kc), 0.0)
            d_a = jnp.where(mask, _dot_nt(do, v_s[hh]).astype(BF16).astype(F32), 0.0)
            dqx = _dot_nn(d_a, kc)
            dkc = _dot_tn(d_a, qx)
            dv_s[hh] += _dot_tn(a, do)
            dk, dcum = dk_s[hh], dc_s[hh]
            dq_slabs = [jnp.zeros((sb, HD), F32)] * first
            dc_slabs = [jnp.zeros((sb, HD), F32)] * first
            for g, (eq, ek) in enumerate(zip(eqs, eks)):
                rows = slice(sb * (first + g), sb * (first + g + 1))
                cols = slice(HD * g, HD * (g + 1))
                dq_i = dqx[rows, cols]
                dk_i = dkc[:, cols]
                dq_slabs.append(dq_i * eq)
                dc_slabs.append(qx[rows, cols] * dq_i)
                dk = dk + dk_i * ek
                dcum = dcum - kc[:, cols] * dk_i
            dq_s[hh] += jnp.concatenate(dq_slabs, axis=0)
            dk_s[hh] = dk
            dc_s[hh] = dcum + jnp.concatenate(dc_slabs, axis=0)

        def pairs_exact(hh):
            dqd_s[hh] = jnp.zeros((CH, HD), F32)
            dkd_s[hh] = jnp.zeros((CH, HD), F32)
            for i in range(CH // SB):
                blk = slice(SB * i, SB * (i + 1))
                qb = q_s[hh, blk, :]
                cb = c_s[hh, blk, :]
                dob = do_s[hh, blk, :]
                dq_acc = jnp.zeros((SB, HD), F32)
                for s in range(SB):
                    row = SB * i + s
                    ks = k_s[hh, row:row + 1, :]
                    vs = v_s[hh, row:row + 1, :]
                    w = jnp.exp(jnp.minimum(cb - c_s[hh, row:row + 1, :], 0.0))
                    live = sbi >= s
                    a_col = jnp.where(live, jnp.sum(qb * ks * w, axis=-1, keepdims=True), 0.0)
                    da_col = jnp.where(live, jnp.sum(dob * vs, axis=-1, keepdims=True), 0.0)
                    dq_acc = dq_acc + da_col * ks * w
                    dkd_s[hh, row:row + 1, :] += jnp.sum(da_col * qb * w, axis=0, keepdims=True)
                    dv_s[hh, row:row + 1, :] += jnp.sum(a_col * dob, axis=0, keepdims=True)
                dqd_s[hh, blk, :] += dq_acc
            dq_d = dqd_s[hh]
            dk_d = dkd_s[hh]
            dq_s[hh] += dq_d
            dk_s[hh] += dk_d
            dc_s[hh] += q_s[hh] * dq_d - k_s[hh] * dk_d

        def gate_grads(hh, rows):
            lanes = slice(hh * HD, (hh + 1) * HD)
            lbv = lb_ref[:, lanes]
            hq = hq_ref[rows, lanes].astype(F32)
            f, sf, sq = f_s[hh], sf_s[hh], sq_s[hh]
            dlogf = _chunk_rev_cumsum(dc_s[hh], rowi) + dl_s[hh]
            dfv = jnp.where(f > 1e-30, dlogf / jnp.maximum(f, 1e-30), 0.0) - dk_s[hh]
            dlb_s[:, lanes] += jnp.sum(dfv * (1.0 - sf), axis=0, keepdims=True)
            df_ref[rows, lanes] = (dfv * (1.0 - lbv) * sf * (1.0 - sf)).astype(BF16)
            dq_ref[rows, lanes] = (dq_s[hh] * _dsilu(hq, sq)).astype(BF16)
            di_ref[rows, lanes] = dv_s[hh].astype(BF16)

        def chunk_with(pairs):
            def chunk(j, carry):
                n = nch - 1 - j
                rows = pl.ds(pl.multiple_of(n * CH, CH), CH)
                for hh in range(NH):
                    between_chunks(hh, n, rows)
                for hh in range(NH):
                    pairs(hh)
                for hh in range(NH):
                    gate_grads(hh, rows)
                return carry
            return chunk

        def pairs_mild(sb):
            return lambda hh: pairs_matmul(hh, 0, DECAY_CAP, False, sb)

        def pairs_any(hh):
            pairs_matmul(hh, 1, 0.0, True, SB)
            pairs_exact(hh)

        tier = jnp.max(mild_ref[...])

        @pl.when(tier == 2.0)
        def _():
            lax.fori_loop(0, nch, chunk_with(pairs_mild(SB_WIDE)), 0, unroll=2)

        @pl.when(tier == 1.0)
        def _():
            lax.fori_loop(0, nch, chunk_with(pairs_mild(SB)), 0)

        @pl.when(tier == 0.0)
        def _():
            lax.fori_loop(0, nch, chunk_with(pairs_any), 0)

        dlb_ref[...] = dlb_s[...]
        dgn_ref[...] = jnp.broadcast_to(dgn_s[...], (8, HD))

    ahead = lambda h, s: jnp.minimum(h + jnp.minimum(s, 1), HEADS // NH - 1)
    col = lambda off: pl.BlockSpec((S, W), lambda h, s: (0, off // NH + ahead(h, s)))
    head_in = pl.BlockSpec((S, W), lambda h, s: (0, ahead(h, s)))
    vec_in = pl.BlockSpec((1, W), lambda h, s: (0, ahead(h, s)))
    vec = pl.BlockSpec((1, W), lambda h, s: (0, h))
    seg_w = (HF0 - HQ0) // NH
    outs = _call(
        body, name=name, grid=(HEADS // NH, 4),
        in_specs=[head_in, col(HQ0), col(HF0), col(HI0), col(HG0), head_in,
                  pl.BlockSpec((NH, nch, HD, HD), lambda h, s: (ahead(h, s), 0, 0, 0)),
                  pl.BlockSpec((8, HD), lambda h, s: (ahead(h, s), 0)), head_in, vec_in,
                  pl.BlockSpec((1, HD), lambda h, s: (0, 0)), ANY],
        out_specs=[pl.BlockSpec((S, W), lambda h, s: (0, HQ0 // NH + seg_w * s + h)), vec,
                   pl.BlockSpec((8, HD), lambda h, s: (h, 0))],
        out_shape=[jax.ShapeDtypeStruct(dproj.shape, dproj.dtype), jax.ShapeDtypeStruct((1, D), F32),
                   jax.ShapeDtypeStruct((8 * HEADS // NH, HD), F32)],
        scratch_shapes=[pltpu.VMEM((NH, CH, HD), F32)] * 14
        + [pltpu.VMEM((NH, 1, HD), F32), pltpu.VMEM((NH, HD, HD), F32), pltpu.VMEM((1, W), F32),
           pltpu.VMEM((1, HD), F32)] + [pltpu.VMEM((S, W), BF16)] * 3,
        sem=("arbitrary", "arbitrary"), aliases={n_in - 1: 0},
        args=(dbin, proj, proj, proj, proj, oraw, states, mild, cum, lb, gn, dproj), carry=carry)
    dproj, dlb, dgn = outs[:3]
    return (dproj, dlb, dgn.reshape(HEADS // NH, 8, HD)[:, 0, :]), outs[3:]


def _lower_bounds(l0, l1):
    m = jnp.maximum(l0, l1)
    e0 = jnp.exp(l0 - m)
    e1 = jnp.exp(l1 - m)
    tot = e0 + e1
    p0 = e0 / tot
    p1 = e1 / tot
    return jnp.clip(p0 - p0, 0.0, 1.0), jnp.clip((p0 + p1) - p0, 0.0, 1.0)


def _lb_fwd(logits):
    def body(l_ref, o_ref):
        lb0, lb1 = _lower_bounds(l_ref[0:1, :], l_ref[1:2, :])
        o_ref[0:1, :] = lb0
        o_ref[1:2, :] = lb1

    return pl.pallas_call(body, name="lb_fwd", out_shape=jax.ShapeDtypeStruct((2, D), F32))(logits)


def _lb_bwd(logits, dlb):
    def body(l_ref, d_ref, o_ref):
        _, vjp = jax.vjp(_lower_bounds, l_ref[0:1, :], l_ref[1:2, :])
        g0, g1 = vjp((d_ref[0:1, :], d_ref[1:2, :]))
        o_ref[0:1, :] = g0
        o_ref[1:2, :] = g1

    return pl.pallas_call(body, name="lb_bwd", out_shape=jax.ShapeDtypeStruct((2, D), F32))(logits, dlb)


ADA_PAD = 128


def _ada_fwd(c_pad, w_ada, b_sh):
    ns = w_ada.shape[2]

    def body(c_ref, w_ref, b_ref, o_ref):
        cv = c_ref[...]
        ca = (cv * _sig(cv)).astype(BF16)
        for l in range(2):
            res = jnp.dot(ca, w_ref[l].astype(BF16), preferred_element_type=F32)
            o_ref[:, l * ns:(l + 1) * ns] = res[0:NDEV, :] + b_ref[l:l + 1, :]

    return pl.pallas_call(body, name="ada_fwd", out_shape=jax.ShapeDtypeStruct((NDEV, 2 * ns), F32),
                          compiler_params=_cp())(c_pad, w_ada, b_sh)


def _ada_wgrad(c_pad_t, d_ada_sh):
    ns = d_ada_sh.shape[2]

    def body(c_ref, d_ref, o_ref):
        cv = c_ref[...]
        ca = (cv * _sig(cv)).astype(BF16)
        for l in range(2):
            o_ref[l] = jnp.dot(ca, d_ref[l].astype(BF16), preferred_element_type=F32)

    return pl.pallas_call(body, name="ada_wgrad", out_shape=jax.ShapeDtypeStruct((2, D, ns), F32),
                          compiler_params=_cp())(c_pad_t, d_ada_sh)


def _sum_devices(g):
    _, R, C = g.shape

    def body(g_ref, o_ref):
        acc = g_ref[0]
        for d in range(1, NDEV):
            acc = acc + g_ref[d]
        o_ref[...] = acc

    return pl.pallas_call(body, name="sum_devices", out_shape=jax.ShapeDtypeStruct((R, C), F32),
                          compiler_params=_cp())(g)


def _adamw(w, g, m, v, name, carry=None):
    R, C = w.shape
    tr = _row_tile(R, max(8, (1 << 19) // C))

    def body(w_ref, g_ref, m_ref, v_ref, d_ref, nm_ref, nv_ref):
        d_ref[...], nm_ref[...], nv_ref[...] = _adamw_update(w_ref[...], g_ref[...], m_ref[...], v_ref[...])

    tile = pl.BlockSpec((tr, C), lambda i: (i, 0))
    return _call(body, name=name, grid=(R // tr,), in_specs=[tile] * 4, out_specs=[tile] * 3,
                 out_shape=[jax.ShapeDtypeStruct((R, C), F32)] * 3, sem=("parallel",), args=(w, g, m, v),
                 carry=carry)


def _adamw_update(w, g, m, v):
    nm = B1 * m + (1.0 - B1) * g
    nv = B2 * v + (1.0 - B2) * (g * g)
    m_hat = nm / (1.0 - B1 ** STEP)
    v_hat = nv / (1.0 - B2 ** STEP)
    return -LR * (m_hat / (jnp.sqrt(v_hat) + AEPS) + WD * w), nm, nv


SMALL_PARTS = (("b_ada", 0, 6, D), ("g_pre", 8, 2, D), ("g_post", 16, 2, D), ("lb_logits", 24, 2, D),
               ("pool_w", 32, 128, D), ("pool_scale", 160, 1, D), ("hgrn_norm_g", 168, 1, 2 * HD))


def _adamw_small(g_small, g_lb_logits, wmv):
    n = len(SMALL_PARTS)

    def body(g_ref, glb_ref, *refs):
        ins, outs = refs[:3 * n], refs[3 * n:]
        for p, (key, row0, rows, width) in enumerate(SMALL_PARTS):
            gv = glb_ref[...] if key == "lb_logits" else g_ref[row0:row0 + rows, 0:width]
            res = _adamw_update(ins[3 * p][...], gv, ins[3 * p + 1][...], ins[3 * p + 2][...])
            for t in range(3):
                outs[3 * p + t][...] = res[t]

    flat = [t for triple in wmv for t in triple]
    outs = pl.pallas_call(body, name="adamw_small",
                          out_shape=[jax.ShapeDtypeStruct(t.shape, F32) for t in flat],
                          compiler_params=_cp())(g_small, g_lb_logits, *flat)
    return [outs[3 * p:3 * p + 3] for p in range(n)]


def _cast_to_slot(place, w, l, name):
    _, R, C = w.shape
    tr = _row_tile(R, max(8, (1 << 19) // C))

    def body(p_ref, w_ref, o_ref):
        o_ref[...] = w_ref[...].astype(BF16)

    return pl.pallas_call(
        body, name=name, out_shape=jax.ShapeDtypeStruct((NCHIP, R, C), BF16),
        grid_spec=pltpu.PrefetchScalarGridSpec(
            num_scalar_prefetch=1, grid=(R // tr,),
            in_specs=[pl.BlockSpec((None, tr, C), lambda i, p_ref: (l, i, 0))],
            out_specs=pl.BlockSpec((None, tr, C), lambda i, p_ref: (p_ref[0], i, 0))),
        compiler_params=_cp(("parallel",)),
    )(place, w)


def _pair_add(core, g, got, name):
    _, R, C = g.shape
    r2 = R // 2
    tr = _row_tile(r2, max(8, (1 << 19) // C))
    nt = r2 // tr

    def body(c_ref, a_ref, b_ref, o_ref):
        o_ref[...] = (a_ref[...].astype(F32) + b_ref[...].astype(F32)).astype(o_ref.dtype)

    return pl.pallas_call(
        body, name=name, out_shape=jax.ShapeDtypeStruct((NCHIP, r2, C), BF16),
        grid_spec=pltpu.PrefetchScalarGridSpec(
            num_scalar_prefetch=1, grid=(NCHIP, nt),
            in_specs=[pl.BlockSpec((None, tr, C), lambda j, i, c_ref: (j, c_ref[0] * nt + i, 0)),
                      pl.BlockSpec((None, tr, C), lambda j, i, c_ref: (j, i, 0))],
            out_specs=pl.BlockSpec((None, tr, C), lambda j, i, c_ref: (j, i, 0))),
        compiler_params=_cp(("parallel", "parallel")),
    )(core, g, got)


def _chip_sum(place, part, recv, layer, both, name):
    _, r2, C = part.shape
    tr = _row_tile(r2, max(8, (1 << 18) // C))
    nt = r2 // tr

    def body(p_ref, own_ref, r_ref, *rest):
        o_ref = rest[-1]
        me = p_ref[0]
        own = own_ref[...].astype(F32)
        acc = None
        for j in range(NCHIP):
            slot = jnp.minimum(jnp.where(j > me, j - 1, j), NCHIP - 2)
            term = jnp.where(me == j, own, r_ref[slot].astype(F32))
            acc = term if acc is None else acc + term
        o_ref[...] = acc

    args = (place, part, recv) if both is None else (place, part, recv, both)
    return pl.pallas_call(
        body, name=name, out_shape=jax.ShapeDtypeStruct((2, 2 * r2, C), F32),
        grid_spec=pltpu.PrefetchScalarGridSpec(
            num_scalar_prefetch=1, grid=(nt,),
            in_specs=[pl.BlockSpec((None, tr, C), lambda i, p_ref: (p_ref[0], i, 0)),
                      pl.BlockSpec((NCHIP - 1, tr, C), lambda i, p_ref: (0, i, 0))] + [ANY] * (len(args) - 3),
            out_specs=pl.BlockSpec((None, tr, C), lambda i, p_ref: (layer, p_ref[1] * nt + i, 0))),
        input_output_aliases={} if both is None else {3: 0},
        compiler_params=_cp(("parallel",)),
    )(*args)


def _place():
    x, y, c = lax.axis_index("x"), lax.axis_index("y"), lax.axis_index("c")
    chips = [(1 - x, y), (x, 1 - y), (1 - x, 1 - y)]
    return x, y, c, chips


def _gather_small(blk, name):
    m_per, n = blk.shape

    def body(x_ref, out_ref, send_sems, recv_sems, local_sem):
        x, y, c, chips = _place()
        me, sibling = (x, y, c), (x, y, 1 - c)

        def rows(px, py, pc):
            return out_ref.at[pl.ds((4 * px + 2 * py + pc) * m_per, m_per), :]

        def copy(k, block, to, src=None):
            return pltpu.make_async_remote_copy(
                src_ref=rows(*block) if src is None else src, dst_ref=rows(*block),
                send_sem=send_sems.at[k], recv_sem=recv_sems.at[k], device_id=to, device_id_type=MESH)

        mine = pltpu.make_async_copy(x_ref, rows(*me), local_sem)
        mine.start()
        first = [copy(0, me, sibling, src=x_ref)]
        first += [copy(1 + j, me, (*chip, c), src=x_ref) for j, chip in enumerate(chips)]
        for cp in first:
            cp.start()
        passed = [copy(4 + j, (*chip, c), sibling) for j, chip in enumerate(chips)]
        for j, chip in enumerate(chips):
            copy(1 + j, (*chip, c), me).wait_recv()
            passed[j].start()
        copy(0, sibling, me).wait_recv()
        for j, chip in enumerate(chips):
            copy(4 + j, (*chip, 1 - c), me).wait_recv()
        for cp in first + passed:
            cp.wait_send()
        mine.wait()

    return pl.pallas_call(
        body, name=name, out_shape=jax.ShapeDtypeStruct((NDEV * m_per, n), blk.dtype),
        in_specs=[pl.BlockSpec(memory_space=pltpu.VMEM)], out_specs=pl.BlockSpec(memory_space=pltpu.VMEM),
        scratch_shapes=[pltpu.SemaphoreType.DMA((7,)), pltpu.SemaphoreType.DMA((7,)), pltpu.SemaphoreType.DMA],
        compiler_params=_cp(),
    )(blk)


def _gather_rows_carry(blk):
    m_per, n = blk.shape

    def rows(ref, px, py, pc):
        return ref.at[pl.ds((4 * px + 2 * py + pc) * m_per, m_per), :]

    def copy(ins, outs, send_sems, recv_sems, k, block, to, own=False):
        return pltpu.make_async_remote_copy(
            src_ref=ins[0] if own else rows(outs[0], *block), dst_ref=rows(outs[0], *block),
            send_sem=send_sems.at[k], recv_sem=recv_sems.at[k], device_id=to, device_id_type=MESH)

    def mine(ins, outs, send_sems):
        x, y, c, _ = _place()
        return pltpu.make_async_copy(ins[0], rows(outs[0], x, y, c), send_sems.at[7])

    def start(ins, outs, send_sems, recv_sems):
        x, y, c, chips = _place()
        mine(ins, outs, send_sems).start()
        copy(ins, outs, send_sems, recv_sems, 0, (x, y, c), (x, y, 1 - c), own=True).start()
        for j, chip in enumerate(chips):
            copy(ins, outs, send_sems, recv_sems, 1 + j, (x, y, c), (*chip, c), own=True).start()

    def finish(ins, outs, send_sems, recv_sems):
        x, y, c, chips = _place()
        for j, chip in enumerate(chips):
            copy(ins, outs, send_sems, recv_sems, 1 + j, (*chip, c), (x, y, c)).wait_recv()
            copy(ins, outs, send_sems, recv_sems, 4 + j, (*chip, c), (x, y, 1 - c)).start()
        copy(ins, outs, send_sems, recv_sems, 0, (x, y, 1 - c), (x, y, c)).wait_recv()
        for j, chip in enumerate(chips):
            copy(ins, outs, send_sems, recv_sems, 4 + j, (*chip, 1 - c), (x, y, c)).wait_recv()
        copy(ins, outs, send_sems, recv_sems, 0, (x, y, c), (x, y, 1 - c), own=True).wait_send()
        for j, chip in enumerate(chips):
            copy(ins, outs, send_sems, recv_sems, 1 + j, (x, y, c), (*chip, c), own=True).wait_send()
            copy(ins, outs, send_sems, recv_sems, 4 + j, (*chip, c), (x, y, 1 - c)).wait_send()
        mine(ins, outs, send_sems).wait()

    return _Carry([blk], [jax.ShapeDtypeStruct((NDEV * m_per, n), blk.dtype)], {}, 8, start, finish)


def _gather_carry(shards, piece=(0, 1, 1)):
    n = len(shards)
    first, count, of = piece

    def rows(ref, half):
        r2 = ref.shape[1] // 2
        return pl.ds(half * r2 + first * (r2 // of), count * (r2 // of))

    def over_ici(outs, send_sems, recv_sems, a, j, chip_xy, slot):
        x, y, c, _ = _place()
        blk = outs[a].at[slot, rows(outs[a], c), :]
        return pltpu.make_async_remote_copy(
            src_ref=blk, dst_ref=blk, send_sem=send_sems.at[6 * a + j], recv_sem=recv_sems.at[6 * a + j],
            device_id=(*chip_xy, c), device_id_type=MESH)

    def over_d2d(outs, send_sems, recv_sems, a, j, slot, half):
        x, y, c, _ = _place()
        blk = outs[a].at[slot, rows(outs[a], half), :]
        return pltpu.make_async_remote_copy(
            src_ref=blk, dst_ref=blk, send_sem=send_sems.at[6 * a + 3 + j], recv_sem=recv_sems.at[6 * a + 3 + j],
            device_id=(x, y, 1 - c), device_id_type=MESH)

    def start(ins, outs, send_sems, recv_sems):
        x, y, c, chips = _place()
        for a in range(n):
            for j, chip_xy in enumerate(chips):
                over_ici(outs, send_sems, recv_sems, a, j, chip_xy, 2 * x + y).start()

    def finish(ins, outs, send_sems, recv_sems):
        x, y, c, chips = _place()
        for a in range(n):
            for j, (cx, cy) in enumerate(chips):
                over_ici(outs, send_sems, recv_sems, a, j, (cx, cy), 2 * cx + cy).wait_recv()
                over_d2d(outs, send_sems, recv_sems, a, j, 2 * cx + cy, c).start()
        for a in range(n):
            for j, (cx, cy) in enumerate(chips):
                over_d2d(outs, send_sems, recv_sems, a, j, 2 * cx + cy, 1 - c).wait_recv()
        for a in range(n):
            for j, (cx, cy) in enumerate(chips):
                over_ici(outs, send_sems, recv_sems, a, j, (cx, cy), 2 * x + y).wait_send()
                over_d2d(outs, send_sems, recv_sems, a, j, 2 * cx + cy, c).wait_send()

    return _Carry(shards, [jax.ShapeDtypeStruct(s.shape, s.dtype) for s in shards],
                  {a: a for a in range(n)}, 6 * n, start, finish)


def _rs_pair(grads, name):
    n = len(grads)

    def body(*refs):
        ins, gots = refs[:n], refs[n:2 * n]
        send_sems, recv_sems = refs[2 * n:]
        x, y, c, _ = _place()
        cps = []
        for a in range(n):
            r2 = ins[a].shape[1] // 2
            cp = pltpu.make_async_remote_copy(
                src_ref=ins[a].at[:, pl.ds((1 - c) * r2, r2), :], dst_ref=gots[a],
                send_sem=send_sems.at[a], recv_sem=recv_sems.at[a],
                device_id=(x, y, 1 - c), device_id_type=MESH)
            cp.start()
            cps.append(cp)
        for cp in cps:
            cp.wait()

    half = [jax.ShapeDtypeStruct((NCHIP, g.shape[1] // 2, g.shape[2]), g.dtype) for g in grads]
    return pl.pallas_call(
        body, name=name, out_shape=half, in_specs=[ANY] * n, out_specs=[ANY] * n,
        scratch_shapes=[pltpu.SemaphoreType.DMA((n,)), pltpu.SemaphoreType.DMA((n,))],
        compiler_params=_cp(),
    )(*grads)


def _chips_carry(parts, piece=(0, 1, 1), into=None):
    n = len(parts)
    first, count, of = piece

    def rows(ref):
        step = ref.shape[1] // of
        return pl.ds(first * step, count * step)

    def send(ins, outs, send_sems, recv_sems, a, j, chip_xy):
        x, y, c, _ = _place()
        me, them = 2 * x + y, 2 * chip_xy[0] + chip_xy[1]
        return pltpu.make_async_remote_copy(
            src_ref=ins[a].at[them, rows(ins[a]), :],
            dst_ref=outs[a].at[me - (me > them).astype(jnp.int32), rows(outs[a]), :],
            send_sem=send_sems.at[3 * a + j], recv_sem=recv_sems.at[3 * a + j],
            device_id=(*chip_xy, c), device_id_type=MESH)

    def start(ins, outs, send_sems, recv_sems):
        _, _, _, chips = _place()
        for a in range(n):
            for j, chip_xy in enumerate(chips):
                send(ins, outs, send_sems, recv_sems, a, j, chip_xy).start()

    def finish(ins, outs, send_sems, recv_sems):
        x, y, c, chips = _place()
        me = 2 * x + y
        for a in range(n):
            for j, (cx, cy) in enumerate(chips):
                them = 2 * cx + cy
                blk = outs[a].at[them - (them > me).astype(jnp.int32), rows(outs[a]), :]
                pltpu.make_async_remote_copy(
                    src_ref=blk, dst_ref=blk, send_sem=send_sems.at[3 * a + j], recv_sem=recv_sems.at[3 * a + j],
                    device_id=(cx, cy, c), device_id_type=MESH).wait_recv()
        for a in range(n):
            for j, chip_xy in enumerate(chips):
                send(ins, outs, send_sems, recv_sems, a, j, chip_xy).wait_send()

    landing = [jax.ShapeDtypeStruct((NCHIP - 1,) + p.shape[1:], p.dtype) for p in parts]
    if into is None:
        return _Carry(parts, landing, {}, 3 * n, start, finish)
    return _Carry(list(parts) + list(into), landing, {n + a: a for a in range(n)}, 3 * n, start, finish)


def _rs_swap(fulls):
    n = len(fulls)

    def body(*refs):
        outs = refs[n:2 * n]
        send_sems, recv_sems = refs[2 * n:]
        x, y, c, _ = _place()
        cps = []
        for a in range(n):
            r2 = outs[a].shape[1] // 2
            mine = outs[a].at[:, pl.ds(c * r2, r2), :]
            cp = pltpu.make_async_remote_copy(
                src_ref=mine, dst_ref=mine, send_sem=send_sems.at[a], recv_sem=recv_sems.at[a],
                device_id=(x, y, 1 - c), device_id_type=MESH)
            cp.start()
            cps.append(cp)
        for a in range(n):
            r2 = outs[a].shape[1] // 2
            blk = outs[a].at[:, pl.ds((1 - c) * r2, r2), :]
            pltpu.make_async_remote_copy(
                src_ref=blk, dst_ref=blk, send_sem=send_sems.at[a], recv_sem=recv_sems.at[a],
                device_id=(x, y, 1 - c), device_id_type=MESH).wait_recv()
        for cp in cps:
            cp.wait_send()

    return pl.pallas_call(
        body, name="rs_swap", out_shape=[jax.ShapeDtypeStruct(f.shape, f.dtype) for f in fulls],
        in_specs=[ANY] * n, out_specs=[ANY] * n, input_output_aliases={a: a for a in range(n)},
        scratch_shapes=[pltpu.SemaphoreType.DMA((n,)), pltpu.SemaphoreType.DMA((n,))],
        compiler_params=_cp(),
    )(*fulls)


def _tail_weight_grads(merged_t, b_in_t, a_in_t, dy, dbr_b, dbr_a, name, tn=256):
    S = dy.shape[0]
    nn = D // tn

    def body(mt_ref, bt_ref, at_ref, dy_ref, db_ref, da_ref, go_ref, gh_ref, gp_ref):
        go_ref[...] = jnp.dot(mt_ref[...], dy_ref[...], preferred_element_type=F32).astype(BF16)
        gh_ref[...] = jnp.dot(bt_ref[...], db_ref[...], preferred_element_type=F32).astype(BF16)
        gp_ref[...] = jnp.dot(at_ref[...], da_ref[...], preferred_element_type=F32).astype(BF16)

    left = lambda rows: pl.BlockSpec((rows, S), lambda n: (0, 0))
    right = pl.BlockSpec((S, tn), lambda n: (0, n))
    out = pl.BlockSpec((D, tn), lambda n: (0, n))
    return pl.pallas_call(
        body, name=name, grid=(nn,), in_specs=[left(D), left(D), left(POOL_W), right, right, right],
        out_specs=[out, out, pl.BlockSpec((None, POOL_W, tn), lambda n: (n, 0, 0))],
        out_shape=[jax.ShapeDtypeStruct((D, D), BF16), jax.ShapeDtypeStruct((D, D), BF16),
                   jax.ShapeDtypeStruct((NCHIP, POOL_W, D // NCHIP), BF16)],
        compiler_params=_cp(("parallel",)),
    )(merged_t, b_in_t, a_in_t, dy, dbr_b, dbr_a)


class _GatherInProj:
    def __init__(self, slot, order):
        self.slot, self.order = slot, order


def _proj_with_gather(h, w_slot, order, name, tn=256):
    S, K = h.shape
    nsh, _, ns = w_slot.shape
    tps = ns // tn
    nt = nsh * tps
    r2 = K // 2

    def body(ord_ref, h_ref, w_in_ref, o_ref, w_ref, wbuf, tile_sems, send_sems, recv_sems):
        n = pl.program_id(0)
        x, y, c, chips = _place()

        def half(slot, which):
            return w_ref.at[slot, pl.ds(which * r2, r2), :]

        def over_ici(j, slot):
            blk = half(slot, c)
            return pltpu.make_async_remote_copy(src_ref=blk, dst_ref=blk, send_sem=send_sems.at[j],
                                                recv_sem=recv_sems.at[j], device_id=(*chips[j], c),
                                                device_id_type=MESH)

        def over_d2d(j, which):
            blk = half(2 * chips[j][0] + chips[j][1], which)
            return pltpu.make_async_remote_copy(src_ref=blk, dst_ref=blk, send_sem=send_sems.at[3 + j],
                                                recv_sem=recv_sems.at[3 + j], device_id=(x, y, 1 - c),
                                                device_id_type=MESH)

        def tile_copy(step, slot):
            shard = ord_ref[step // tps]
            return pltpu.make_async_copy(w_ref.at[shard, :, pl.ds((step % tps) * tn, tn)], wbuf.at[slot],
                                         tile_sems.at[slot])

        @pl.when(n == 0)
        def _():
            for j in range(3):
                over_ici(j, 2 * x + y).start()
            tile_copy(0, 0).start()

        for j in range(3):
            @pl.when(n == (j + 1) * tps - 1)
            def _(j=j):
                over_ici(j, 2 * chips[j][0] + chips[j][1]).wait_recv()
                over_d2d(j, c).start()
                over_d2d(j, 1 - c).wait_recv()

        @pl.when(n + 1 < nt)
        def _():
            tile_copy(n + 1, (n + 1) % 2).start()

        tile_copy(n, n % 2).wait()
        o_ref[...] = jnp.dot(h_ref[...], wbuf[n % 2], preferred_element_type=F32).astype(o_ref.dtype)

        @pl.when(n == nt - 1)
        def _():
            for j in range(3):
                over_ici(j, 2 * x + y).wait_send()
                over_d2d(j, c).wait_send()

    return pl.pallas_call(
        body, name=name,
        out_shape=[jax.ShapeDtypeStruct((S, nsh * ns), BF16), jax.ShapeDtypeStruct(w_slot.shape, w_slot.dtype)],
        grid_spec=pltpu.PrefetchScalarGridSpec(
            num_scalar_prefetch=1, grid=(nt,),
            in_specs=[pl.BlockSpec((S, K), lambda n, o_ref: (0, 0)), ANY],
            out_specs=[pl.BlockSpec((S, tn), lambda n, o_ref: (0, o_ref[n // tps] * tps + n % tps)), ANY],
            scratch_shapes=[pltpu.VMEM((2, K, tn), w_slot.dtype), pltpu.SemaphoreType.DMA((2,)),
                            pltpu.SemaphoreType.DMA((6,)), pltpu.SemaphoreType.DMA((6,))]),
        input_output_aliases={2: 1},
        compiler_params=_cp(("arbitrary",)),
    )(order, h, w_slot)


def _mm_ride(a, b, carry, **kw):
    if carry is None:
        return _mm(a, b, **kw), []
    return _mm(a, b, carry=carry, **kw)


def _layer_fwd(l, x, ada, w, small, ride, target=None):
    shift, scale, gate = ada[:, 0:D], ada[:, D:2 * D], ada[:, 2 * D:3 * D]
    carry, landed = ride("prenorm")
    (h, h_t), outs = _prenorm_fwd(x, small["g_pre"][l], scale, shift, f"prenorm_fwd{l}", carry)
    landed(outs)
    carry, landed = ride("proj")
    if isinstance(carry, _GatherInProj):
        proj, full = _proj_with_gather(h, carry.slot, carry.order, f"proj{l}")
        outs = [full]
    else:
        proj, outs = _mm_ride(h, w["w_in"][l], carry, name=f"proj{l}", b_mode="nn_sh", tm=2048, out_dtype=BF16)
    landed(outs)
    a_in, a_in_t = _pool_fwd(proj, small["pool_w"][l], small["pool_scale"][l], f"pool_fwd{l}")
    carry, landed = ride("hgrn")
    (b_in, b_in_t, o_raw, states, mild, cum), outs = _hgrn_fwd(proj, small["lb"][l], small["hgrn_norm_g"][l],
                                                              f"hgrn_fwd{l}", carry=carry)
    landed(outs)
    carry, landed = ride("tail")
    (br_a, br_b, merged_t, y, *x_new), outs = _layer_tail_fwd(
        proj, a_in, b_in, x, w["w_pool_o"][l], w["w_hgrn_o"][l].reshape(D, D), w["w_out"][l].reshape(D, D),
        gate, small["g_post"][l], f"tail_fwd{l}", target=target, carry=carry)
    landed(outs)
    saved = dict(x=x, h_t=h_t, proj=proj, a_in_t=a_in_t, b_in_t=b_in_t, o_raw=o_raw, states=states, mild=mild,
                 cum=cum,
                 br_a=br_a, br_b=br_b, merged_t=merged_t, y=y, scale=scale, gate=gate)
    return x_new, saved


def _layer_bwd(l, dxn, sv, w, small, ride):
    carry, landed = ride["head"](None)
    (dy, dbr_a, dbr_b, dproj, da_in, db_in, dgate, dg_post), outs = _layer_head_bwd(
        dxn, sv["y"], sv["proj"], sv["br_a"], sv["br_b"], w["w_pool_o"][l], w["w_hgrn_o"][l].reshape(D, D),
        w["w_out"][l].reshape(D, D), sv["gate"], small["g_post"][l], f"head_bwd{l}", carry)
    landed(outs)
    gw_out, gw_hgrn_o, gw_pool_o = _tail_weight_grads(sv["merged_t"], sv["b_in_t"], sv["a_in_t"], dy, dbr_b,
                                                      dbr_a, f"gw_tail{l}")
    big = dict(w_pool_o=gw_pool_o, w_hgrn_o=gw_hgrn_o.reshape(NCHIP, D // NCHIP, D),
               w_out=gw_out.reshape(NCHIP, D // NCHIP, D))
    carry, landed = ride["hgrn"](big)
    (dproj, dlb, dgn), outs = _hgrn_bwd(db_in, sv["proj"], sv["o_raw"], sv["states"], sv["mild"], sv["cum"],
                                        small["lb"][l], small["hgrn_norm_g"][l], dproj, f"hgrn_bwd{l}",
                                        carry=carry)
    landed(outs)
    dproj, dpw, dpsc = _pool_bwd(da_in, sv["proj"], small["pool_w"][l], small["pool_scale"][l], dproj,
                                 f"pool_bwd{l}")
    little = dict(dgate=dgate, g_post=dg_post, pool_w=dpw, pool_scale=dpsc, lb=dlb,
                  hgrn_norm_g=jnp.sum(dgn, axis=0, keepdims=True))
    carry, landed = ride["gw_in"](little)
    big["w_in"], outs = _mm_ride(sv["h_t"], dproj, carry, name=f"gw_in{l}", out_shards=NCHIP, out_dtype=BF16)
    landed(outs)
    carry, landed = ride["d_h"](big)
    dh, outs = _mm_ride(dproj, w["w_in"][l], carry, name=f"d_h{l}", b_mode="nt_shk", tn=1024)
    landed(outs)
    carry, landed = ride["prenorm"](big)
    (dx, dshift, dscale, dg_pre), outs = _prenorm_bwd(dh, dxn, sv["x"], small["g_pre"][l], sv["scale"],
                                                      f"prenorm_bwd{l}", carry)
    landed(outs)
    little.update(dshift=dshift, dscale=dscale, g_pre=dg_pre)
    return dx, big, little


SMALL_ROWS = 176


def _rows8(t):
    t = t.reshape(-1, D)
    return jnp.pad(t, ((0, -t.shape[0] % 8), (0, 0)))


def _pack_small(parts):
    row_keys = ("dshift", "dscale", "dgate", "g_pre", "g_post", "lb", "pool_scale", "hgrn_norm_g")
    flat = [p[k] for p in parts for k in row_keys] + [p["pool_w"].reshape(GROUPS * 128 * 128 // D, D) for p in parts]
    nk = len(row_keys)

    def body(*refs):
        o_ref = refs[-1]
        o_ref[...] = jnp.zeros((SMALL_ROWS, D), F32)
        for l in range(2):
            dshift, dscale, dgate, g_pre, g_post, lb, pscale, gn = refs[l * nk:(l + 1) * nk]
            for r, ref in enumerate((dshift, dscale, dgate)):
                o_ref[3 * l + r:3 * l + r + 1, :] = ref[...]
            o_ref[8 + l:9 + l, :] = g_pre[...]
            o_ref[16 + l:17 + l, :] = g_post[...]
            o_ref[24 + l:25 + l, :] = lb[...]
            o_ref[160:161, l * POOL_W:(l + 1) * POOL_W] = pscale[...]
            o_ref[168:169, l * HD:(l + 1) * HD] = gn[...]
            pw = refs[2 * nk + l]
            rows = pw.shape[0]
            o_ref[32 + l * rows:32 + (l + 1) * rows, :] = pw[...]

    return pl.pallas_call(body, name="pack_small", out_shape=jax.ShapeDtypeStruct((SMALL_ROWS, D), F32),
                          compiler_params=_cp())(*flat)


def _unpack_small(p):
    return (p[0:6].reshape(2, 3 * D), p[8:10], p[16:18], p[24:26], p[32:160].reshape(2, GROUPS, 128, 128),
            p[160:161].reshape(2, POOL_W), p[168:169, 0:2 * HD].reshape(2, HD))


def kernel(x, c, w_ada, b_ada, g_pre, g_post, w_in, pool_w, pool_scale, lb_logits, hgrn_norm_g, w_pool_o, w_hgrn_o, w_out, loss_target, m_w_ada, m_b_ada, m_g_pre, m_g_post, m_w_in, m_pool_w, m_pool_scale, m_lb_logits, m_hgrn_norm_g, m_w_pool_o, m_w_hgrn_o, m_w_out, v_w_ada, v_b_ada, v_g_pre, v_g_post, v_w_in, v_pool_w, v_pool_scale, v_lb_logits, v_hgrn_norm_g, v_w_pool_o, v_w_hgrn_o, v_w_out):
    ax, ay, ac = lax.axis_index("x"), lax.axis_index("y"), lax.axis_index("c")
    chip = 2 * ax + ay
    dev = 2 * chip + ac
    xe, te = x[0], loss_target[0]
    ada_s = w_ada.shape[2]

    big_names = ("w_in", "w_pool_o", "w_hgrn_o", "w_out")
    big_w = (w_in, w_pool_o, w_hgrn_o, w_out)
    core = jnp.stack([ac]).astype(jnp.int32)
    place = jnp.stack([chip, ac]).astype(jnp.int32)
    slots = {(k, l): _cast_to_slot(place, t, l, f"cast_{k}{l}") for l in range(2) for k, t in zip(big_names, big_w)}
    w = {k: [None, None] for k in big_names}
    def fills(keys):
        def landed(outs):
            for (k, l), o in zip(keys, outs):
                w[k][l] = slots[k, l] = o
        return landed

    rest0 = [(k, 0) for k in big_names[1:]]
    rest1 = [(k, 1) for k in big_names[1:]]
    no_carry = (None, lambda outs: None)
    order = jnp.stack([chip, 2 * (1 - ax) + ay, 2 * ax + (1 - ay), 2 * (1 - ax) + (1 - ay)]).astype(jnp.int32)

    def ride_fwd0(stage):
        if stage == "proj":
            return _GatherInProj(slots["w_in", 0], order), fills([("w_in", 0)])
        if stage == "hgrn":
            return (_join_carries(_gather_carry([slots[t] for t in rest0]),
                                  _gather_carry([slots["w_in", 1]], piece=(0, 2, 4))),
                    fills(rest0 + [("w_in", 1)]))
        if stage == "tail":
            return _gather_carry([slots["w_in", 1]], piece=(2, 1, 4)), fills([("w_in", 1)])
        return no_carry

    def ride_fwd1(stage):
        if stage == "prenorm":
            return _gather_carry([slots["w_in", 1]], piece=(3, 1, 4)), fills([("w_in", 1)])
        if stage == "hgrn":
            return _gather_carry([slots[t] for t in rest1]), fills(rest1)
        return no_carry

    c_all = _gather_small(jnp.broadcast_to(c, (8, D)), "gather_c").reshape(NDEV, 8, D)[:, 0, :]
    c_pad = jnp.pad(c_all, ((0, ADA_PAD - NDEV), (0, 0)))
    b_sh = lax.dynamic_slice(b_ada, (0, chip * ada_s), (2, ada_s))
    ada_cols = _gather_small(_ada_fwd(c_pad, w_ada, b_sh), "gather_ada")
    ada_cols = ada_cols.reshape(NCHIP, 2, NDEV, 2, ada_s)[:, 0]
    ada_all = jnp.transpose(ada_cols, (2, 1, 0, 3)).reshape(2, NDEV, 3 * D)
    ada_me = lax.dynamic_slice(ada_all, (0, dev, 0), (2, 1, 3 * D))

    lbs = _lb_fwd(lb_logits)
    small = dict(g_pre=g_pre[:, None, :], g_post=g_post[:, None, :], pool_w=pool_w,
                 pool_scale=pool_scale[:, None, :], lb=lbs[:, None, :], hgrn_norm_g=hgrn_norm_g[:, None, :])

    (x1,), sv0 = _layer_fwd(0, xe, ada_me[0], w, small, ride_fwd0)
    (dx2, loss_blk), sv1 = _layer_fwd(1, x1, ada_me[1], w, small, ride_fwd1, target=te)

    parts, recv = {}, {}

    def pair_sums(keys, grads, tag):
        got = _rs_pair(grads, f"rs_pair_{tag}")
        for kl, g, o in zip(keys, grads, got):
            parts[kl] = _pair_add(core, g, o, f"rs_add_{kl[0]}{kl[1]}")

    def exchange(keys):
        def landed(outs):
            recv.update(zip(keys, outs))
        return _chips_carry([parts[kl] for kl in keys]), landed

    def early(l):
        return [(k, l) for k in big_names[1:]]

    def ride_hgrn1(big):
        pair_sums(early(1), [big[k] for k in big_names[1:]], "l1_early")
        return exchange(early(1))

    def halves(key, second):
        def landed(outs):
            (recv[key],) = outs
        if second:
            return _chips_carry([parts[key]], piece=(1, 1, 2), into=[recv[key]]), landed
        return _chips_carry([parts[key]], piece=(0, 1, 2)), landed

    def ride_d_h1(big):
        pair_sums([("w_in", 1)], [big["w_in"]], "l1_w_in")
        return halves(("w_in", 1), False)

    def ride_head0(_):
        return halves(("w_in", 1), True)

    def ride_hgrn0(big):
        pair_sums(early(0), [big[k] for k in big_names[1:]], "l0_early")
        return exchange(early(0))

    def ride_d_h0(big):
        pair_sums([("w_in", 0)], [big["w_in"]], "l0_w_in")
        return halves(("w_in", 0), False)

    def ride_prenorm0(big):
        return halves(("w_in", 0), True)

    no_ride = lambda so_far: no_carry
    dx1, big1, little1 = _layer_bwd(1, dx2, sv1, w, small, dict(head=no_ride, hgrn=ride_hgrn1, gw_in=no_ride,
                                                                d_h=ride_d_h1, prenorm=no_ride))

    gathered = {}
    zero_row = jnp.zeros((1, D), F32)

    def ride_gw_in0(little):
        so_far = dict(little, dshift=zero_row, dscale=zero_row, g_pre=zero_row)

        def landed(outs):
            (gathered["early"],) = outs
        return _gather_rows_carry(_pack_small([so_far, little1])), landed

    dx0, big0, little0 = _layer_bwd(0, dx1, sv0, w, small,
                                    dict(head=ride_head0, hgrn=ride_hgrn0, gw_in=ride_gw_in0, d_h=ride_d_h0,
                                         prenorm=ride_prenorm0))
    loss = lax.psum(loss_blk[0, 0], ("x", "y", "c"))
    late = _rows8(jnp.stack([little0["dshift"], little0["dscale"], little0["g_pre"]]))
    late = _gather_small(late, "gather_small_late").reshape(NDEV, 8, D)
    packed = gathered["early"].reshape(NDEV, SMALL_ROWS, D)
    packed = packed.at[:, 0:2, :].set(late[:, 0:2, :]).at[:, 8:9, :].set(late[:, 2:3, :])
    red = []
    for k in big_names:
        both = _chip_sum(place, parts[k, 1], recv[k, 1], 1, None, f"rs_sum_{k}1")
        red.append(_chip_sum(place, parts[k, 0], recv[k, 0], 0, both, f"rs_sum_{k}0"))
    g_big = dict(zip(big_names, _rs_swap(red)))

    def upd(wt, g, m, v, name, carry=None):
        shp = wt.shape
        two = lambda t: t.reshape(-1, shp[-1])
        res = _adamw(two(wt), two(g), two(m), two(v), name, carry)
        return [t.reshape(shp) for t in res[:3]], res[3:]

    u_w_in, _ = upd(w_in, g_big["w_in"], m_w_in, v_w_in, "adamw_w_in")
    g_small = _sum_devices(packed)
    g_b_ada, g_g_pre, g_g_post, g_lb, g_pool_w, g_pool_scale, g_norm_g = _unpack_small(g_small)
    g_lb_logits = _lb_bwd(lb_logits, g_lb)
    d_ada_all = packed[:, 0:6, :].reshape(NDEV, 2, 3 * D)
    d_ada_sh = lax.dynamic_slice(jnp.transpose(d_ada_all, (1, 0, 2)), (0, 0, chip * ada_s), (2, NDEV, ada_s))
    d_ada_sh = jnp.pad(d_ada_sh, ((0, 0), (0, ADA_PAD - NDEV), (0, 0)))
    g_w_ada = _ada_wgrad(c_pad.T, d_ada_sh)

    u_w_ada, _ = upd(w_ada, g_w_ada, m_w_ada, v_w_ada, "adamw_w_ada")
    u_w_pool_o, _ = upd(w_pool_o, g_big["w_pool_o"], m_w_pool_o, v_w_pool_o, "adamw_w_pool_o")
    u_w_hgrn_o, _ = upd(w_hgrn_o, g_big["w_hgrn_o"], m_w_hgrn_o, v_w_hgrn_o, "adamw_w_hgrn_o")
    u_w_out, _ = upd(w_out, g_big["w_out"], m_w_out, v_w_out, "adamw_w_out")
    small_w = dict(b_ada=(b_ada, m_b_ada, v_b_ada), g_pre=(g_pre, m_g_pre, v_g_pre),
                   g_post=(g_post, m_g_post, v_g_post), lb_logits=(lb_logits, m_lb_logits, v_lb_logits),
                   pool_w=(pool_w, m_pool_w, v_pool_w), pool_scale=(pool_scale, m_pool_scale, v_pool_scale),
                   hgrn_norm_g=(hgrn_norm_g, m_hgrn_norm_g, v_hgrn_norm_g))
    in_rows = [tuple(t.reshape(rows, width) for t in small_w[key]) for key, _, rows, width in SMALL_PARTS]
    u_rows = _adamw_small(g_small, g_lb_logits, in_rows)
    u_small = {key: [t.reshape(small_w[key][0].shape) for t in u_rows[p]]
               for p, (key, _, _, _) in enumerate(SMALL_PARTS)}

    grads_out = (g_w_ada, g_b_ada, g_g_pre, g_g_post, g_big["w_in"], g_pool_w, g_pool_scale, g_lb_logits,
                 g_norm_g, g_big["w_pool_o"], g_big["w_hgrn_o"], g_big["w_out"])

    def ordered(k):
        s = lambda key: u_small[key][k]
        return (u_w_ada[k], s("b_ada"), s("g_pre"), s("g_post"), u_w_in[k], s("pool_w"), s("pool_scale"),
                s("lb_logits"), s("hgrn_norm_g"), u_w_pool_o[k], u_w_hgrn_o[k], u_w_out[k])

    return (loss, dx0[None], *grads_out, *ordered(0), *ordered(1), *ordered(2))
```

```python
import functools

import jax
import jax.numpy as jnp
from jax import lax
from jax.experimental import pallas as pl
from jax.experimental.pallas import tpu as pltpu

F32 = jnp.float32
BF16 = jnp.bfloat16
MESH = pl.DeviceIdType.MESH

D = 1024
HEADS = 8
HD = 128
GROUPS = 4
POOL_W = 512
CH = 128
SB_WIDE = 32
SB = 16
NH = 2
IN_W = 7168
NCHIP = 4
NDEV = 8
EPS = 1e-6
PV0, PG0, HQ0, HF0, HI0, HG0 = 0, 4, 8, 16, 24, 32
MGP_BLK, MGH_BLK = 5, 6

LR, B1, B2, AEPS, WD, STEP = 0.001, 0.9, 0.999, 1e-08, 0.01, 10
VMEM_LIMIT = 56 * 1024 * 1024


def _cp(sem=None, **kw):
    if sem is not None:
        kw["dimension_semantics"] = sem
    return pltpu.CompilerParams(vmem_limit_bytes=VMEM_LIMIT, **kw)


def _sig(z):
    return 1.0 / (1.0 + jnp.exp(-z))


def _dsilu(z, s):
    return s * (1.0 + z * (1.0 - s))


def _row_tile(rows, cap):
    if rows <= cap:
        return rows
    t = 1 << (cap.bit_length() - 1)
    while rows % t:
        t //= 2
    return t


ANY = pl.BlockSpec(memory_space=pl.ANY)


class _Carry:
    def __init__(self, ins, outs, aliases, n_sem, start, finish):
        self.ins, self.outs, self.aliases, self.n_sem = list(ins), list(outs), dict(aliases), n_sem
        self.start, self.finish = start, finish


class _SemWindow:
    def __init__(self, ref, base):
        self._ref, self._base = ref, base

    @property
    def at(self):
        return self

    def __getitem__(self, k):
        return self._ref.at[self._base + k]


def _join_carries(*carries):
    ins, outs, aliases, spans, n_sem = [], [], {}, [], 0
    for cr in carries:
        aliases.update({len(ins) + i: len(outs) + o for i, o in cr.aliases.items()})
        spans.append((len(ins), len(cr.ins), len(outs), len(cr.outs), n_sem))
        ins, outs, n_sem = ins + cr.ins, outs + cr.outs, n_sem + cr.n_sem

    def run(which):
        def fn(i_refs, o_refs, send_sems, recv_sems):
            for cr, (i0, ni, o0, no, s0) in zip(carries, spans):
                getattr(cr, which)(i_refs[i0:i0 + ni], o_refs[o0:o0 + no], _SemWindow(send_sems, s0),
                                   _SemWindow(recv_sems, s0))
        return fn

    return _Carry(ins, outs, aliases, n_sem, run("start"), run("finish"))


def _call(body, *, name, grid, in_specs, out_specs, out_shape, args, scratch_shapes=(), sem=None, carry=None,
          aliases=None):
    in_specs, out_specs, out_shape = list(in_specs), list(out_specs), list(out_shape)
    scratch_shapes = list(scratch_shapes)
    aliases = dict(aliases or {})
    if carry is None:
        outs = pl.pallas_call(body, name=name, grid=grid, in_specs=in_specs, out_specs=out_specs,
                              out_shape=out_shape, scratch_shapes=scratch_shapes, input_output_aliases=aliases,
                              compiler_params=_cp(sem))(*args)
        return list(outs)
    n_in, n_out, n_scr = len(in_specs), len(out_specs), len(scratch_shapes)
    c_in, c_out = len(carry.ins), len(carry.outs)

    def wrapped(*refs):
        k_in, rest = refs[:n_in], refs[n_in:]
        ci, rest = rest[:c_in], rest[c_in:]
        k_out, rest = rest[:n_out], rest[n_out:]
        co, rest = rest[:c_out], rest[c_out:]
        k_scr, (ssem, rsem) = rest[:n_scr], rest[n_scr:]
        pids = [pl.program_id(d) for d in range(len(grid))]
        first = functools.reduce(jnp.logical_and, [p == 0 for p in pids])
        last = functools.reduce(jnp.logical_and, [p == g - 1 for p, g in zip(pids, grid)])

        @pl.when(first)
        def _():
            carry.start(ci, co, ssem, rsem)

        body(*k_in, *k_out, *k_scr)

        @pl.when(last)
        def _():
            carry.finish(ci, co, ssem, rsem)

    outs = pl.pallas_call(
        wrapped, name=name, grid=grid, in_specs=in_specs + [ANY] * c_in, out_specs=out_specs + [ANY] * c_out,
        out_shape=out_shape + carry.outs,
        input_output_aliases={**aliases, **{n_in + i: n_out + o for i, o in carry.aliases.items()}},
        scratch_shapes=scratch_shapes + [pltpu.SemaphoreType.DMA((carry.n_sem,))] * 2,
        compiler_params=_cp(("arbitrary",) * len(grid)),
    )(*args, *carry.ins)
    return list(outs)


def _mm(a, b, *, name, b_mode="nn", out_shards=0, tm=1024, tn=256, tk=None, out_dtype=F32, carry=None):
    assert b_mode in ("nn", "nn_sh", "nt_shk"), b_mode
    M, K = a.shape
    if b_mode == "nn":
        N = b.shape[1]
    elif b_mode == "nn_sh":
        N = b.shape[0] * b.shape[2]
    else:
        N = b.shape[1]
    tm = _row_tile(M, tm)
    if b_mode == "nn_sh":
        tn = _row_tile(b.shape[2], tn)
    elif out_shards:
        tn = _row_tile(N // out_shards, tn)
    else:
        tn = _row_tile(N, tn)
    if tk is None:
        tk = K if b_mode != "nt_shk" else b.shape[2]
    if b_mode == "nt_shk":
        tk = _row_tile(b.shape[2], tk)
    nm, nn, nk = M // tm, N // tn, K // tk

    a_spec = pl.BlockSpec((tm, tk), lambda m, n, k: (m, k))
    if b_mode == "nn":
        b_spec = pl.BlockSpec((tk, tn), lambda m, n, k: (k, n))
    elif b_mode == "nn_sh":
        nps = b.shape[2] // tn
        b_spec = pl.BlockSpec((None, tk, tn), lambda m, n, k: (n // nps, k, n % nps))
    else:
        kps = b.shape[2] // tk
        b_spec = pl.BlockSpec((None, tn, tk), lambda m, n, k: (k // kps, n, k % kps))
    if out_shards:
        ops = (N // out_shards) // tn
        o_spec = pl.BlockSpec((None, tm, tn), lambda m, n, k: (n // ops, m, n % ops))
        o_shape = jax.ShapeDtypeStruct((out_shards, M, N // out_shards), out_dtype)
    else:
        o_spec = pl.BlockSpec((tm, tn), lambda m, n, k: (m, n))
        o_shape = jax.ShapeDtypeStruct((M, N), out_dtype)
    dn = (((1,), (1,)), ((), ())) if b_mode == "nt_shk" else (((1,), (0,)), ((), ()))

    def body(a_ref, b_ref, o_ref, acc_ref):
        k = pl.program_id(2)

        @pl.when(k == 0)
        def _():
            acc_ref[...] = jnp.zeros(acc_ref.shape, F32)

        acc_ref[...] += lax.dot_general(a_ref[...].astype(BF16), b_ref[...].astype(BF16), dn,
                                        preferred_element_type=F32)

        @pl.when(k == nk - 1)
        def _():
            o_ref[...] = acc_ref[...].astype(o_ref.dtype)

    outs = _call(body, name=name, grid=(nm, nn, nk), in_specs=[a_spec, b_spec], out_specs=[o_spec],
                 out_shape=[o_shape], scratch_shapes=[pltpu.VMEM((tm, tn), F32)],
                 sem=("parallel", "parallel", "arbitrary"), args=(a, b), carry=carry)
    return outs[0] if carry is None else (outs[0], outs[1:])


def _rowvec(n=D):
    return pl.BlockSpec((1, n), lambda i: (0, 0))


def _prenorm_fwd(x, g, scale, shift, name, carry=None):
    S = x.shape[0]
    tr = _row_tile(S, 256)

    def body(x_ref, g_ref, sc_ref, sh_ref, h_ref, ht_ref):
        xv = x_ref[...]
        r = lax.rsqrt(jnp.mean(xv * xv, axis=-1, keepdims=True) + EPS)
        hv = (xv * r) * g_ref[...] * (1.0 + sc_ref[...]) + sh_ref[...]
        h_ref[...] = hv.astype(BF16)
        ht_ref[...] = hv.T.astype(BF16)

    outs = _call(
        body, name=name, grid=(S // tr,),
        in_specs=[pl.BlockSpec((tr, D), lambda i: (i, 0)), _rowvec(), _rowvec(), _rowvec()],
        out_specs=[pl.BlockSpec((tr, D), lambda i: (i, 0)), pl.BlockSpec((D, tr), lambda i: (0, i))],
        out_shape=[jax.ShapeDtypeStruct((S, D), BF16), jax.ShapeDtypeStruct((D, S), BF16)],
        sem=("parallel",), args=(x, g, scale, shift), carry=carry)
    return outs[:2], outs[2:]


def _prenorm_bwd(dh, dxn, x, g, scale, name, carry=None):
    S = x.shape[0]
    tr = _row_tile(S, 256)

    def body(dh_ref, dxn_ref, x_ref, g_ref, sc_ref, dx_ref, dsh_ref, dsc_ref, dg_ref):
        i = pl.program_id(0)

        @pl.when(i == 0)
        def _():
            dsh_ref[...] = jnp.zeros((1, D), F32)
            dsc_ref[...] = jnp.zeros((1, D), F32)
            dg_ref[...] = jnp.zeros((1, D), F32)

        xv = x_ref[...]
        dhv = dh_ref[...]
        gv = g_ref[...]
        mod = 1.0 + sc_ref[...]
        r = lax.rsqrt(jnp.mean(xv * xv, axis=-1, keepdims=True) + EPS)
        xh = xv * r
        dsh_ref[...] += jnp.sum(dhv, axis=0, keepdims=True)
        dsc_ref[...] += jnp.sum(dhv * (xh * gv), axis=0, keepdims=True)
        dg_ref[...] += jnp.sum(dhv * mod * xh, axis=0, keepdims=True)
        u = dhv * mod * gv
        dx_ref[...] = dxn_ref[...] + r * u - xv * (r * r * r) * jnp.mean(u * xv, axis=-1, keepdims=True)

    tile = pl.BlockSpec((tr, D), lambda i: (i, 0))
    outs = _call(
        body, name=name, grid=(S // tr,),
        in_specs=[tile, tile, tile, _rowvec(), _rowvec()],
        out_specs=[tile, _rowvec(), _rowvec(), _rowvec()],
        out_shape=[jax.ShapeDtypeStruct((S, D), F32)] + [jax.ShapeDtypeStruct((1, D), F32)] * 3,
        sem=("arbitrary",), args=(dh, dxn, x, g, scale), carry=carry)
    return outs[:4], outs[4:]


def _layer_tail_fwd(proj, a_in, b_in, x, w_po, w_ho, w_out, gate, g, name, target=None, carry=None):
    S = proj.shape[0]
    tr = _row_tile(S, 256)
    nsh, _, wsh = w_po.shape
    n_in = 10 + (target is not None)

    def body(*refs):
        (mgp_ref, mgh_ref, a_ref, b_ref, x_ref, wpo_ref, who_ref, wout_ref, gate_ref, g_ref) = refs[:10]
        bra_ref, brb_ref, mt_ref, y_ref, xn_ref = refs[n_in:n_in + 5]
        av = a_ref[...]
        bra = jnp.concatenate([jnp.dot(av, wpo_ref[j], preferred_element_type=F32) for j in range(nsh)], axis=1)
        brb = jnp.dot(b_ref[...], who_ref[...], preferred_element_type=F32)
        mv = _sig(mgp_ref[...].astype(F32)) * bra + _sig(mgh_ref[...].astype(F32)) * brb
        bra_ref[...] = bra.astype(BF16)
        brb_ref[...] = brb.astype(BF16)
        mt_ref[...] = mv.T.astype(BF16)
        yv = jnp.dot(mv.astype(BF16), wout_ref[...], preferred_element_type=F32)
        y_ref[...] = yv
        r = lax.rsqrt(jnp.mean(yv * yv, axis=-1, keepdims=True) + EPS)
        xn = x_ref[...] + gate_ref[...] * ((yv * r) * g_ref[...])
        if target is None:
            xn_ref[...] = xn
        else:
            t_ref, l_ref = refs[10], refs[n_in + 5]

            @pl.when(pl.program_id(0) == 0)
            def _():
                l_ref[...] = jnp.zeros((8, 128), F32)

            err = xn - t_ref[...]
            xn_ref[...] = err * (1.0 / D)
            l_ref[...] += 0.5 * jnp.sum(jnp.mean(err * err, axis=-1, keepdims=True))

    tile = pl.BlockSpec((tr, D), lambda i: (i, 0))
    whole = lambda t: pl.BlockSpec(t.shape, lambda i: (0,) * t.ndim)
    last = target is not None
    outs = _call(
        body, name=name, grid=(S // tr,),
        in_specs=[pl.BlockSpec((tr, D), lambda i: (i, MGP_BLK)), pl.BlockSpec((tr, D), lambda i: (i, MGH_BLK)),
                  pl.BlockSpec((tr, POOL_W), lambda i: (i, 0)), tile, tile, whole(w_po), whole(w_ho),
                  whole(w_out), _rowvec(), _rowvec()] + [tile] * last,
        out_specs=[tile, tile, pl.BlockSpec((D, tr), lambda i: (0, i)), tile, tile]
        + [pl.BlockSpec((8, 128), lambda i: (0, 0))] * last,
        out_shape=[jax.ShapeDtypeStruct((S, D), BF16), jax.ShapeDtypeStruct((S, D), BF16),
                   jax.ShapeDtypeStruct((D, S), BF16), jax.ShapeDtypeStruct((S, D), F32),
                   jax.ShapeDtypeStruct((S, D), F32)] + [jax.ShapeDtypeStruct((8, 128), F32)] * last,
        sem=("arbitrary",) if last else ("parallel",),
        args=(proj, proj, a_in, b_in, x, w_po, w_ho, w_out, gate, g) + ((target,) if last else ()), carry=carry)
    return outs[:5 + last], outs[5 + last:]


def _layer_head_bwd(dxn, y, proj, br_a, br_b, w_po, w_ho, w_out, gate, g, name, carry=None):
    S = y.shape[0]
    tr = _row_tile(S, 256)
    nsh, _, wsh = w_po.shape

    def body(dxn_ref, y_ref, mgp_ref, mgh_ref, bra_ref, brb_ref, wpo_ref, who_ref, wout_ref, gate_ref, g_ref,
             dy_ref, dba_ref, dbb_ref, dproj_ref, dain_ref, dbin_ref, dgate_ref, dg_ref, dmgh_s):
        i = pl.program_id(0)
        j = pl.program_id(1)

        @pl.when((i == 0) & (j == 0))
        def _():
            dgate_ref[...] = jnp.zeros((1, D), F32)
            dg_ref[...] = jnp.zeros((1, D), F32)

        @pl.when(j == 1)
        def _():
            dproj_ref[...] = dmgh_s[...]

        @pl.when(j == 0)
        def _():
            everything(dxn_ref, y_ref, mgp_ref, mgh_ref, bra_ref, brb_ref, wpo_ref, who_ref, wout_ref, gate_ref,
                       g_ref, dy_ref, dba_ref, dbb_ref, dproj_ref, dain_ref, dbin_ref, dgate_ref, dg_ref, dmgh_s)

    def everything(dxn_ref, y_ref, mgp_ref, mgh_ref, bra_ref, brb_ref, wpo_ref, who_ref, wout_ref, gate_ref, g_ref,
                   dy_ref, dba_ref, dbb_ref, dproj_ref, dain_ref, dbin_ref, dgate_ref, dg_ref, dmgh_s):
        yv = y_ref[...]
        dv = dxn_ref[...]
        gv = g_ref[...]
        gt = gate_ref[...]
        r = lax.rsqrt(jnp.mean(yv * yv, axis=-1, keepdims=True) + EPS)
        yh = yv * r
        dgate_ref[...] += jnp.sum(dv * (yh * gv), axis=0, keepdims=True)
        dg_ref[...] += jnp.sum(dv * gt * yh, axis=0, keepdims=True)
        u = dv * gt * gv
        dy = (r * u - yv * (r * r * r) * jnp.mean(u * yv, axis=-1, keepdims=True)).astype(BF16)
        dy_ref[...] = dy
        dm = _dot_nt(dy, wout_ref[...])
        sp = _sig(mgp_ref[...].astype(F32))
        sh = _sig(mgh_ref[...].astype(F32))
        dba = (dm * sp).astype(BF16)
        dbb = (dm * sh).astype(BF16)
        dba_ref[...] = dba
        dbb_ref[...] = dbb
        dproj_ref[...] = (dm * bra_ref[...].astype(F32) * sp * (1.0 - sp)).astype(BF16)
        dmgh_s[...] = (dm * brb_ref[...].astype(F32) * sh * (1.0 - sh)).astype(BF16)
        dain = _dot_nt(dba[:, 0:wsh], wpo_ref[0])
        for k in range(1, nsh):
            dain = dain + _dot_nt(dba[:, k * wsh:(k + 1) * wsh], wpo_ref[k])
        dain_ref[...] = dain
        dbin_ref[...] = _dot_nt(dbb, who_ref[...])

    tile = pl.BlockSpec((tr, D), lambda i, j: (i, 0))
    whole = lambda t: pl.BlockSpec(t.shape, lambda i, j: (0,) * t.ndim)
    vec = pl.BlockSpec((1, D), lambda i, j: (0, 0))
    ahead = lambda i, j: jnp.minimum(i + j, S // tr - 1)
    tile_in = pl.BlockSpec((tr, D), lambda i, j: (ahead(i, j), 0))
    outs = _call(
        body, name=name, grid=(S // tr, 2),
        in_specs=[tile_in, tile_in, pl.BlockSpec((tr, D), lambda i, j: (ahead(i, j), MGP_BLK)),
                  pl.BlockSpec((tr, D), lambda i, j: (ahead(i, j), MGH_BLK)), tile_in, tile_in, whole(w_po),
                  whole(w_ho), whole(w_out), vec, vec],
        out_specs=[tile, tile, tile, pl.BlockSpec((tr, D), lambda i, j: (i, MGP_BLK + j)),
                   pl.BlockSpec((tr, POOL_W), lambda i, j: (i, 0)), tile, vec, vec],
        out_shape=[jax.ShapeDtypeStruct((S, D), BF16)] * 3
        + [jax.ShapeDtypeStruct((S, IN_W), BF16), jax.ShapeDtypeStruct((S, POOL_W), F32),
           jax.ShapeDtypeStruct((S, D), F32), jax.ShapeDtypeStruct((1, D), F32), jax.ShapeDtypeStruct((1, D), F32)],
        scratch_shapes=[pltpu.VMEM((tr, D), BF16)], sem=("arbitrary", "arbitrary"),
        args=(dxn, y, proj, proj, br_a, br_b, w_po, w_ho, w_out, gate, g), carry=carry)
    return outs[:8], outs[8:]


def _pool_pieces(u, g, S):
    rowi = lax.broadcasted_iota(jnp.int32, (S, 1), 0)

    def down(z, k):
        return jnp.where(rowi >= k, pltpu.roll(z, k, axis=0), 0.0)

    s2 = u + down(u, 1)
    s4 = s2 + down(s2, 2)
    s8 = s4 + down(s4, 4)
    s16 = s8 + down(s8, 8)
    win = jnp.where(g == 0, s2, jnp.where(g == 1, s4, jnp.where(g == 2, s8, s16)))
    w = jnp.where(g == 0, 2, jnp.where(g == 1, 4, jnp.where(g == 2, 8, 16)))
    count = jnp.minimum(rowi + 1, w).astype(F32)
    return win / count - u, count, rowi


def _pool_fwd(proj, pw, pscale, name):
    S = proj.shape[0]

    def body(pv_ref, pg_ref, pw_ref, sc_ref, a_ref, at_ref):
        g = pl.program_id(0)
        pooled, _, _ = _pool_pieces(pv_ref[...].astype(F32), g, S)
        pm = jnp.dot(pooled.astype(BF16), pw_ref[...].astype(BF16), preferred_element_type=F32)
        pgv = pg_ref[...].astype(F32)
        av = pm * sc_ref[...] * (pgv * _sig(pgv))
        a_ref[...] = av.astype(BF16)
        at_ref[...] = av.T.astype(BF16)

    return pl.pallas_call(
        body, name=name, grid=(GROUPS,),
        in_specs=[pl.BlockSpec((S, 128), lambda g: (0, PV0 + g)), pl.BlockSpec((S, 128), lambda g: (0, PG0 + g)),
                  pl.BlockSpec((None, 128, 128), lambda g: (g, 0, 0)), pl.BlockSpec((1, 128), lambda g: (0, g))],
        out_specs=[pl.BlockSpec((S, 128), lambda g: (0, g)), pl.BlockSpec((128, S), lambda g: (g, 0))],
        out_shape=[jax.ShapeDtypeStruct((S, POOL_W), BF16), jax.ShapeDtypeStruct((POOL_W, S), BF16)],
        compiler_params=_cp(("parallel",)),
    )(proj, proj, pw, pscale)


def _pool_bwd(da, proj, pw, pscale, dproj, name):
    S = proj.shape[0]

    def body(da_ref, pv_ref, pg_ref, pw_ref, sc_ref, dproj_in, dproj_ref, dpw_ref, dsc_ref, dpg_s):
        @pl.when(pl.program_id(1) == 1)
        def _():
            dproj_ref[...] = dpg_s[...]

        @pl.when(pl.program_id(1) == 0)
        def _():
            group(da_ref, pv_ref, pg_ref, pw_ref, sc_ref, dproj_ref, dpg_s, dpw_ref, dsc_ref)

    def group(da_ref, pv_ref, pg_ref, pw_ref, sc_ref, dpv_ref, dpg_ref, dpw_ref, dsc_ref):
        g = pl.program_id(0)
        pooled, count, rowi = _pool_pieces(pv_ref[...].astype(F32), g, S)
        pwb = pw_ref[...].astype(BF16)
        pm = jnp.dot(pooled.astype(BF16), pwb, preferred_element_type=F32)
        scv = sc_ref[...]
        pgv = pg_ref[...].astype(F32)
        sg = _sig(pgv)
        dav = da_ref[...]
        d_ps = dav * (pgv * sg)
        dpg_ref[...] = (dav * (pm * scv) * _dsilu(pgv, sg)).astype(BF16)
        dsc_ref[...] = jnp.sum(d_ps * pm, axis=0, keepdims=True)
        d_pm = (d_ps * scv).astype(BF16)
        dpw_ref[...] = lax.dot_general(pooled.astype(BF16), d_pm, (((0,), (0,)), ((), ())),
                                       preferred_element_type=F32)
        d_pooled = lax.dot_general(d_pm, pwb, (((1,), (1,)), ((), ())), preferred_element_type=F32)
        z = d_pooled / count

        def up(v, k):
            return jnp.where(rowi < S - k, pltpu.roll(v, S - k, axis=0), 0.0)

        t2 = z + up(z, 1)
        t4 = t2 + up(t2, 2)
        t8 = t4 + up(t4, 4)
        t16 = t8 + up(t8, 8)
        adj = jnp.where(g == 0, t2, jnp.where(g == 1, t4, jnp.where(g == 2, t8, t16)))
        dpv_ref[...] = (adj - d_pooled).astype(BF16)

    col = lambda g, j: (0, g)
    ahead = lambda g, j: jnp.minimum(g + j, GROUPS - 1)
    return pl.pallas_call(
        body, name=name, grid=(GROUPS, 2),
        in_specs=[pl.BlockSpec((S, 128), lambda g, j: (0, ahead(g, j))),
                  pl.BlockSpec((S, 128), lambda g, j: (0, PV0 + ahead(g, j))),
                  pl.BlockSpec((S, 128), lambda g, j: (0, PG0 + ahead(g, j))),
                  pl.BlockSpec((None, 128, 128), lambda g, j: (ahead(g, j), 0, 0)),
                  pl.BlockSpec((1, 128), lambda g, j: (0, ahead(g, j))), ANY],
        out_specs=[pl.BlockSpec((S, 128), lambda g, j: (0, PV0 + g + (PG0 - PV0) * j)),
                   pl.BlockSpec((None, 128, 128), lambda g, j: (g, 0, 0)), pl.BlockSpec((1, 128), col)],
        out_shape=[jax.ShapeDtypeStruct(dproj.shape, dproj.dtype),
                   jax.ShapeDtypeStruct((GROUPS, 128, 128), F32), jax.ShapeDtypeStruct((1, POOL_W), F32)],
        scratch_shapes=[pltpu.VMEM((S, 128), BF16)], input_output_aliases={5: 0},
        compiler_params=_cp(("arbitrary", "arbitrary")),
    )(da, proj, proj, pw, pscale, dproj)


SCAN_SHIFTS = tuple(1 << b for b in range(CH.bit_length() - 1))


def _chunk_cumsum(z, rowi):
    for sh in SCAN_SHIFTS:
        z = z + jnp.where(rowi >= sh, pltpu.roll(z, sh, axis=0), 0.0)
    return z


def _chunk_rev_cumsum(z, rowi):
    for sh in SCAN_SHIFTS:
        z = z + jnp.where(rowi < CH - sh, pltpu.roll(z, CH - sh, axis=0), 0.0)
    return z


def _dot_nn(a, b):
    return jnp.dot(a.astype(BF16), b.astype(BF16), preferred_element_type=F32)


def _dot_nt(a, b):
    return lax.dot_general(a.astype(BF16), b.astype(BF16), (((1,), (1,)), ((), ())), preferred_element_type=F32)


def _dot_tn(a, b):
    return lax.dot_general(a.astype(BF16), b.astype(BF16), (((0,), (0,)), ((), ())), preferred_element_type=F32)


def _gates(hq, hf, lbv):
    hq, hf = hq.astype(F32), hf.astype(F32)
    sq = _sig(hq)
    sf = _sig(hf)
    f = lbv + (1.0 - lbv) * sf
    fc = jnp.maximum(f, 1e-30)
    return hq * sq, sq, sf, f, fc, jnp.log(fc)


DECAY_CAP = 60.0


def _block_ref(c_ref, i, sb):
    if i == 0:
        return jnp.zeros((1, HD), F32)
    return c_ref[sb * i - 1:sb * i, :]


def _block_decay(c_ref, sb):
    spans = [_block_ref(c_ref, i, sb) - c_ref[sb * (i + 1) - 1:sb * (i + 1), :] for i in range(CH // sb)]
    return functools.reduce(jnp.maximum, spans)


def _pair_factors(q_ref, k, c_ref, first, cap, round_bf16, sb):
    nb = CH // sb
    c = c_ref[...]
    zero = jnp.zeros((sb, HD), F32)
    q_groups, k_groups, eqs, eks = [], [], [], []
    for i in range(first, nb):
        blk = slice(sb * i, sb * (i + 1))
        r_i = _block_ref(c_ref, i, sb)
        eq = jnp.exp(jnp.minimum(c_ref[blk, :] - r_i, 0.0))
        ek = jnp.exp(jnp.minimum(r_i - c, cap))
        qi, kei = q_ref[blk, :] * eq, k * ek
        if round_bf16:
            qi, kei = qi.astype(BF16).astype(F32), kei.astype(BF16).astype(F32)
        q_groups.append(jnp.concatenate([zero] * i + [qi] + [zero] * (nb - 1 - i), axis=0))
        k_groups.append(kei)
        eqs.append(eq)
        eks.append(ek)
    return jnp.concatenate(q_groups, axis=1), jnp.concatenate(k_groups, axis=1), eqs, eks


def _pair_mask(rowi, coli, strict, sb):
    return (coli < jnp.bitwise_and(rowi, -sb)) if strict else (coli <= rowi)


def _hgrn_fwd(proj, lb, gn, name, carry=None):
    S = proj.shape[0]
    nch = S // CH
    W = NH * HD

    def body(hq_ref, hf_ref, hi_ref, hg_ref, lb_ref, gn_ref, bin_ref, bint_ref, oraw_ref, st_ref, mild_ref,
             cum_ref, q_s, k_s, c_s, v_s, o_s, state_s, qf_s, kf_s, cf_s):
        state_s[...] = jnp.zeros((NH, HD, HD), F32)
        rowi = lax.broadcasted_iota(jnp.int32, (CH, 1), 0)
        coli = lax.broadcasted_iota(jnp.int32, (1, CH), 1)
        sbi = lax.broadcasted_iota(jnp.int32, (SB, 1), 0)
        gnv = gn_ref[...]

        def gates_pass(n, worst):
            wide, narrow = worst
            rows = pl.ds(pl.multiple_of(n * CH, CH), CH)
            for hh in range(NH):
                lanes = slice(hh * HD, (hh + 1) * HD)
                q, _, _, f, _, logf = _gates(hq_ref[rows, lanes], hf_ref[rows, lanes], lb_ref[:, lanes])
                c = _chunk_cumsum(logf, rowi)
                qf_s[hh, rows, :] = q
                kf_s[hh, rows, :] = 1.0 - f
                cf_s[hh, rows, :] = c
                cum_ref[rows, lanes] = c
                c_s[hh] = c
                wide = jnp.maximum(wide, _block_decay(c_s.at[hh], SB_WIDE))
                narrow = jnp.maximum(narrow, _block_decay(c_s.at[hh], SB))
            return wide, narrow

        def between_chunks(hh, n, rows):
            lanes = slice(hh * HD, (hh + 1) * HD)
            q = qf_s[hh, rows, :]
            k = kf_s[hh, rows, :]
            c = cf_s[hh, rows, :]
            v = hi_ref[rows, lanes].astype(F32)
            q_s[hh] = q
            k_s[hh] = k
            c_s[hh] = c
            v_s[hh] = v
            st = state_s[hh]
            st_ref[hh, n] = st.astype(BF16)
            o_s[hh] = _dot_nt(q * jnp.exp(c), st)
            last = c_s[hh, CH - 1:CH, :]
            state_s[hh] = st * jnp.exp(last) + _dot_tn(v, k * jnp.exp(last - c))

        def pairs_matmul(hh, first, cap, strict, sb):
            qx, kc, _, _ = _pair_factors(q_s.at[hh], k_s[hh], c_s.at[hh], first, cap, False, sb)
            a = jnp.where(_pair_mask(rowi, coli, strict, sb), _dot_nt(qx, kc), 0.0)
            o_s[hh] += _dot_nn(a, v_s[hh])

        def within_chunk_matmul(sb):
            return lambda hh: pairs_matmul(hh, 0, DECAY_CAP, False, sb)

        def within_chunk_exact(hh):
            pairs_matmul(hh, 1, 0.0, True, SB)
            for i in range(CH // SB):
                blk = slice(SB * i, SB * (i + 1))
                qb = q_s[hh, blk, :]
                cb = c_s[hh, blk, :]
                acc = jnp.zeros((SB, HD), F32)
                for s in range(SB):
                    row = SB * i + s
                    w = jnp.exp(jnp.minimum(cb - c_s[hh, row:row + 1, :], 0.0))
                    a_col = jnp.sum(qb * k_s[hh, row:row + 1, :] * w, axis=-1, keepdims=True)
                    acc = acc + jnp.where(sbi >= s, a_col, 0.0) * v_s[hh, row:row + 1, :]
                o_s[hh, blk, :] += acc

        def norm_and_gate(hh, rows):
            lanes = slice(hh * HD, (hh + 1) * HD)
            ov = o_s[hh]
            oraw_ref[rows, lanes] = ov
            r = lax.rsqrt(jnp.mean(ov * ov, axis=-1, keepdims=True) + EPS)
            hg = hg_ref[rows, lanes].astype(F32)
            bin_ref[rows, lanes] = ((ov * r) * gnv * (hg * _sig(hg))).astype(BF16)

        def chunk_with(within_chunk):
            def chunk(n, carry):
                rows = pl.ds(pl.multiple_of(n * CH, CH), CH)
                for hh in range(NH):
                    between_chunks(hh, n, rows)
                for hh in range(NH):
                    within_chunk(hh)
                for hh in range(NH):
                    norm_and_gate(hh, rows)
                return carry
            return chunk

        none = jnp.zeros((1, HD), F32)
        wide, narrow = lax.fori_loop(0, nch, gates_pass, (none, none))
        tier = jnp.where(jnp.max(wide) <= DECAY_CAP, 2.0, jnp.where(jnp.max(narrow) <= DECAY_CAP, 1.0, 0.0))
        mild_ref[...] = jnp.broadcast_to(tier, (8, HD))

        @pl.when(tier == 2.0)
        def _():
            lax.fori_loop(0, nch, chunk_with(within_chunk_matmul(SB_WIDE)), 0, unroll=4)

        @pl.when(tier == 1.0)
        def _():
            lax.fori_loop(0, nch, chunk_with(within_chunk_matmul(SB)), 0, unroll=2)

        @pl.when(tier == 0.0)
        def _():
            lax.fori_loop(0, nch, chunk_with(within_chunk_exact), 0)

        bint_ref[...] = bin_ref[...].astype(F32).T.astype(BF16)

    col = lambda off: pl.BlockSpec((S, W), lambda h: (0, off // NH + h))
    head = pl.BlockSpec((S, W), lambda h: (0, h))
    outs = _call(
        body, name=name, grid=(HEADS // NH,),
        in_specs=[col(HQ0), col(HF0), col(HI0), col(HG0), pl.BlockSpec((1, W), lambda h: (0, h)),
                  pl.BlockSpec((1, HD), lambda h: (0, 0))],
        out_specs=[head, pl.BlockSpec((W, S), lambda h: (h, 0)), head,
                   pl.BlockSpec((NH, nch, HD, HD), lambda h: (h, 0, 0, 0)),
                   pl.BlockSpec((8, HD), lambda h: (h, 0)), head],
        out_shape=[jax.ShapeDtypeStruct((S, D), BF16), jax.ShapeDtypeStruct((D, S), BF16),
                   jax.ShapeDtypeStruct((S, D), F32), jax.ShapeDtypeStruct((HEADS, nch, HD, HD), BF16),
                   jax.ShapeDtypeStruct((8 * HEADS // NH, HD), F32), jax.ShapeDtypeStruct((S, D), F32)],
        scratch_shapes=[pltpu.VMEM((NH, CH, HD), F32)] * 5 + [pltpu.VMEM((NH, HD, HD), F32)]
        + [pltpu.VMEM((NH, S, HD), F32)] * 3,
        sem=("parallel",), args=(proj, proj, proj, proj, lb, gn), carry=carry)
    return outs[:6], outs[6:]


def _hgrn_bwd(dbin, proj, oraw, states, mild, cum, lb, gn, dproj, name, carry=None):
    S = proj.shape[0]
    nch = S // CH
    W = NH * HD
    n_in = 12

    def body(*refs):
        ins, (dproj_ref, dlb_ref, dgn_ref) = refs[:n_in - 1], refs[n_in:n_in + 3]
        scratch, later = refs[n_in + 3:-3], refs[-3:]
        seg = pl.program_id(1)

        @pl.when(seg == 0)
        def _():
            heads(*ins, dproj_ref, *later, dlb_ref, dgn_ref, *scratch)

        for s, kept in enumerate(later):
            @pl.when(seg == s + 1)
            def _(kept=kept):
                dproj_ref[...] = kept[...]

    def heads(db_ref, hq_ref, hf_ref, hi_ref, hg_ref, or_ref, st_ref, mild_ref, cum_ref, lb_ref, gn_ref,
              dq_ref, df_ref, di_ref, dg_ref, dlb_ref, dgn_ref,
              q_s, k_s, c_s, v_s, do_s, dq_s, dk_s, dv_s, dc_s, dqd_s, dkd_s, f_s, sf_s, sq_s, dl_s, dst_s,
              dlb_s, dgn_s):
        dst_s[...] = jnp.zeros((NH, HD, HD), F32)
        dlb_s[...] = jnp.zeros((1, W), F32)
        dgn_s[...] = jnp.zeros((1, HD), F32)
        rowi = lax.broadcasted_iota(jnp.int32, (CH, 1), 0)
        coli = lax.broadcasted_iota(jnp.int32, (1, CH), 1)
        sbi = lax.broadcasted_iota(jnp.int32, (SB, 1), 0)
        gnv = gn_ref[...]
        def between_chunks(hh, n, rows):
            lanes = slice(hh * HD, (hh + 1) * HD)
            lbv = lb_ref[:, lanes]
            hq = hq_ref[rows, lanes].astype(F32)
            sq = _sig(hq)
            sf = _sig(hf_ref[rows, lanes].astype(F32))
            f = lbv + (1.0 - lbv) * sf
            q = hq * sq
            k = 1.0 - f
            f_s[hh] = f
            sf_s[hh] = sf
            sq_s[hh] = sq
            v = hi_ref[rows, lanes].astype(F32)
            c = cum_ref[rows, lanes]
            ov = or_ref[rows, lanes]
            hg = hg_ref[rows, lanes].astype(F32)
            sg = _sig(hg)
            r = lax.rsqrt(jnp.mean(ov * ov, axis=-1, keepdims=True) + EPS)
            dbv = db_ref[rows, lanes]
            d_on = dbv * (hg * sg)
            dg_ref[rows, lanes] = (dbv * ((ov * r) * gnv) * _dsilu(hg, sg)).astype(BF16)
            dgn_s[...] += jnp.sum(d_on * (ov * r), axis=0, keepdims=True)
            u = d_on * gnv
            do = r * u - ov * (r * r * r) * jnp.mean(u * ov, axis=-1, keepdims=True)
            q_s[hh] = q
            k_s[hh] = k
            c_s[hh] = c
            v_s[hh] = v
            do_s[hh] = do
            st = st_ref[hh, n].astype(F32)
            dst = dst_s[hh]
            ec = jnp.exp(c)
            last = c_s[hh, CH - 1:CH, :]
            el = jnp.exp(last - c)
            elast = jnp.exp(last)
            dq = _dot_nn(do, st) * ec
            dk = _dot_nn(v, dst) * el
            dq_s[hh] = dq
            dk_s[hh] = dk
            dv_s[hh] = _dot_nt(k * el, dst)
            dc_s[hh] = q * dq - k * dk
            dl_s[hh] = (jnp.sum(k * dk, axis=0, keepdims=True)
                        + elast * jnp.sum(st * dst, axis=0, keepdims=True))
            dst_s[hh] = dst * elast + _dot_tn(do, q * ec)

        def pairs_matmul(hh, first, cap, strict, sb):
            do = do_s[hh]
            qx, kc, eqs, eks = _pair_factors(q_s.at[hh], k_s[hh], c_s.at[hh], first, cap, True, sb)
            mask = _pair_mask(rowi, coli, strict, sb)
            a = jnp.where(mask, _dot_nt(qx, kc), 0.0)
            d_a = jnp.where(mask, _dot_nt(do, v_s[hh]).astype(BF16).astype(F32), 0.0)
            dqx = _dot_nn(d_a, kc)
            dkc = _dot_tn(d_a, qx)
            dv_s[hh] += _dot_tn(a, do)
            dk, dcum = dk_s[hh], dc_s[hh]
            dq_slabs = [jnp.zeros((sb, HD), F32)] * first
            dc_slabs = [jnp.zeros((sb, HD), F32)] * first
            for g, (eq, ek) in enumerate(zip(eqs, eks)):
                rows = slice(sb * (first + g), sb * (first + g + 1))
                cols = slice(HD * g, HD * (g + 1))
                dq_i = dqx[rows, cols]
                dk_i = dkc[:, cols]
                dq_slabs.append(dq_i * eq)
                dc_slabs.append(qx[rows, cols] * dq_i)
                dk = dk + dk_i * ek
                dcum = dcum - kc[:, cols] * dk_i
            dq_s[hh] += jnp.concatenate(dq_slabs, axis=0)
            dk_s[hh] = dk
            dc_s[hh] = dcum + jnp.concatenate(dc_slabs, axis=0)

        def pairs_exact(hh):
            dqd_s[hh] = jnp.zeros((CH, HD), F32)
            dkd_s[hh] = jnp.zeros((CH, HD), F32)
            for i in range(CH // SB):
                blk = slice(SB * i, SB * (i + 1))
                qb = q_s[hh, blk, :]
                cb = c_s[hh, blk, :]
                dob = do_s[hh, blk, :]
                dq_acc = jnp.zeros((SB, HD), F32)
                for s in range(SB):
                    row = SB * i + s
                    ks = k_s[hh, row:row + 1, :]
                    vs = v_s[hh, row:row + 1, :]
                    w = jnp.exp(jnp.minimum(cb - c_s[hh, row:row + 1, :], 0.0))
                    live = sbi >= s
                    a_col = jnp.where(live, jnp.sum(qb * ks * w, axis=-1, keepdims=True), 0.0)
                    da_col = jnp.where(live, jnp.sum(dob * vs, axis=-1, keepdims=True), 0.0)
                    dq_acc = dq_acc + da_col * ks * w
                    dkd_s[hh, row:row + 1, :] += jnp.sum(da_col * qb * w, axis=0, keepdims=True)
                    dv_s[hh, row:row + 1, :] += jnp.sum(a_col * dob, axis=0, keepdims=True)
                dqd_s[hh, blk, :] += dq_acc
            dq_d = dqd_s[hh]
            dk_d = dkd_s[hh]
            dq_s[hh] += dq_d
            dk_s[hh] += dk_d
            dc_s[hh] += q_s[hh] * dq_d - k_s[hh] * dk_d

        def gate_grads(hh, rows):
            lanes = slice(hh * HD, (hh + 1) * HD)
            lbv = lb_ref[:, lanes]
            hq = hq_ref[rows, lanes].astype(F32)
            f, sf, sq = f_s[hh], sf_s[hh], sq_s[hh]
            dlogf = _chunk_rev_cumsum(dc_s[hh], rowi) + dl_s[hh]
            dfv = jnp.where(f > 1e-30, dlogf / jnp.maximum(f, 1e-30), 0.0) - dk_s[hh]
            dlb_s[:, lanes] += jnp.sum(dfv * (1.0 - sf), axis=0, keepdims=True)
            df_ref[rows, lanes] = (dfv * (1.0 - lbv) * sf * (1.0 - sf)).astype(BF16)
            dq_ref[rows, lanes] = (dq_s[hh] * _dsilu(hq, sq)).astype(BF16)
            di_ref[rows, lanes] = dv_s[hh].astype(BF16)

        def chunk_with(pairs):
            def chunk(j, carry):
                n = nch - 1 - j
                rows = pl.ds(pl.multiple_of(n * CH, CH), CH)
                for hh in range(NH):
                    between_chunks(hh, n, rows)
                for hh in range(NH):
                    pairs(hh)
                for hh in range(NH):
                    gate_grads(hh, rows)
                return carry
            return chunk

        def pairs_mild(sb):
            return lambda hh: pairs_matmul(hh, 0, DECAY_CAP, False, sb)

        def pairs_any(hh):
            pairs_matmul(hh, 1, 0.0, True, SB)
            pairs_exact(hh)

        tier = jnp.max(mild_ref[...])

        @pl.when(tier == 2.0)
        def _():
            lax.fori_loop(0, nch, chunk_with(pairs_mild(SB_WIDE)), 0, unroll=2)

        @pl.when(tier == 1.0)
        def _():
            lax.fori_loop(0, nch, chunk_with(pairs_mild(SB)), 0)

        @pl.when(tier == 0.0)
        def _():
            lax.fori_loop(0, nch, chunk_with(pairs_any), 0)

        dlb_ref[...] = dlb_s[...]
        dgn_ref[...] = jnp.broadcast_to(dgn_s[...], (8, HD))

    ahead = lambda h, s: jnp.minimum(h + jnp.minimum(s, 1), HEADS // NH - 1)
    col = lambda off: pl.BlockSpec((S, W), lambda h, s: (0, off // NH + ahead(h, s)))
    head_in = pl.BlockSpec((S, W), lambda h, s: (0, ahead(h, s)))
    vec_in = pl.BlockSpec((1, W), lambda h, s: (0, ahead(h, s)))
    vec = pl.BlockSpec((1, W), lambda h, s: (0, h))
    seg_w = (HF0 - HQ0) // NH
    outs = _call(
        body, name=name, grid=(HEADS // NH, 4),
        in_specs=[head_in, col(HQ0), col(HF0), col(HI0), col(HG0), head_in,
                  pl.BlockSpec((NH, nch, HD, HD), lambda h, s: (ahead(h, s), 0, 0, 0)),
                  pl.BlockSpec((8, HD), lambda h, s: (ahead(h, s), 0)), head_in, vec_in,
                  pl.BlockSpec((1, HD), lambda h, s: (0, 0)), ANY],
        out_specs=[pl.BlockSpec((S, W), lambda h, s: (0, HQ0 // NH + seg_w * s + h)), vec,
                   pl.BlockSpec((8, HD), lambda h, s: (h, 0))],
        out_shape=[jax.ShapeDtypeStruct(dproj.shape, dproj.dtype), jax.ShapeDtypeStruct((1, D), F32),
                   jax.ShapeDtypeStruct((8 * HEADS // NH, HD), F32)],
        scratch_shapes=[pltpu.VMEM((NH, CH, HD), F32)] * 14
        + [pltpu.VMEM((NH, 1, HD), F32), pltpu.VMEM((NH, HD, HD), F32), pltpu.VMEM((1, W), F32),
           pltpu.VMEM((1, HD), F32)] + [pltpu.VMEM((S, W), BF16)] * 3,
        sem=("arbitrary", "arbitrary"), aliases={n_in - 1: 0},
        args=(dbin, proj, proj, proj, proj, oraw, states, mild, cum, lb, gn, dproj), carry=carry)
    dproj, dlb, dgn = outs[:3]
    return (dproj, dlb, dgn.reshape(HEADS // NH, 8, HD)[:, 0, :]), outs[3:]


def _lower_bounds(l0, l1):
    m = jnp.maximum(l0, l1)
    e0 = jnp.exp(l0 - m)
    e1 = jnp.exp(l1 - m)
    tot = e0 + e1
    p0 = e0 / tot
    p1 = e1 / tot
    return jnp.clip(p0 - p0, 0.0, 1.0), jnp.clip((p0 + p1) - p0, 0.0, 1.0)


def _lb_fwd(logits):
    def body(l_ref, o_ref):
        lb0, lb1 = _lower_bounds(l_ref[0:1, :], l_ref[1:2, :])
        o_ref[0:1, :] = lb0
        o_ref[1:2, :] = lb1

    return pl.pallas_call(body, name="lb_fwd", out_shape=jax.ShapeDtypeStruct((2, D), F32))(logits)


def _lb_bwd(logits, dlb):
    def body(l_ref, d_ref, o_ref):
        _, vjp = jax.vjp(_lower_bounds, l_ref[0:1, :], l_ref[1:2, :])
        g0, g1 = vjp((d_ref[0:1, :], d_ref[1:2, :]))
        o_ref[0:1, :] = g0
        o_ref[1:2, :] = g1

    return pl.pallas_call(body, name="lb_bwd", out_shape=jax.ShapeDtypeStruct((2, D), F32))(logits, dlb)


ADA_PAD = 128


def _ada_fwd(c_pad, w_ada, b_sh):
    ns = w_ada.shape[2]

    def body(c_ref, w_ref, b_ref, o_ref):
        cv = c_ref[...]
        ca = (cv * _sig(cv)).astype(BF16)
        for l in range(2):
            res = jnp.dot(ca, w_ref[l].astype(BF16), preferred_element_type=F32)
            o_ref[:, l * ns:(l + 1) * ns] = res[0:NDEV, :] + b_ref[l:l + 1, :]

    return pl.pallas_call(body, name="ada_fwd", out_shape=jax.ShapeDtypeStruct((NDEV, 2 * ns), F32),
                          compiler_params=_cp())(c_pad, w_ada, b_sh)


def _ada_wgrad(c_pad_t, d_ada_sh):
    ns = d_ada_sh.shape[2]

    def body(c_ref, d_ref, o_ref):
        cv = c_ref[...]
        ca = (cv * _sig(cv)).astype(BF16)
        for l in range(2):
            o_ref[l] = jnp.dot(ca, d_ref[l].astype(BF16), preferred_element_type=F32)

    return pl.pallas_call(body, name="ada_wgrad", out_shape=jax.ShapeDtypeStruct((2, D, ns), F32),
                          compiler_params=_cp())(c_pad_t, d_ada_sh)


def _sum_devices(g):
    _, R, C = g.shape

    def body(g_ref, o_ref):
        acc = g_ref[0]
        for d in range(1, NDEV):
            acc = acc + g_ref[d]
        o_ref[...] = acc

    return pl.pallas_call(body, name="sum_devices", out_shape=jax.ShapeDtypeStruct((R, C), F32),
                          compiler_params=_cp())(g)


def _adamw(w, g, m, v, name, carry=None):
    R, C = w.shape
    tr = _row_tile(R, max(8, (1 << 19) // C))

    def body(w_ref, g_ref, m_ref, v_ref, d_ref, nm_ref, nv_ref):
        d_ref[...], nm_ref[...], nv_ref[...] = _adamw_update(w_ref[...], g_ref[...], m_ref[...], v_ref[...])

    tile = pl.BlockSpec((tr, C), lambda i: (i, 0))
    return _call(body, name=name, grid=(R // tr,), in_specs=[tile] * 4, out_specs=[tile] * 3,
                 out_shape=[jax.ShapeDtypeStruct((R, C), F32)] * 3, sem=("parallel",), args=(w, g, m, v),
                 carry=carry)


def _adamw_update(w, g, m, v):
    nm = B1 * m + (1.0 - B1) * g
    nv = B2 * v + (1.0 - B2) * (g * g)
    m_hat = nm / (1.0 - B1 ** STEP)
    v_hat = nv / (1.0 - B2 ** STEP)
    return -LR * (m_hat / (jnp.sqrt(v_hat) + AEPS) + WD * w), nm, nv


SMALL_PARTS = (("b_ada", 0, 6, D), ("g_pre", 8, 2, D), ("g_post", 16, 2, D), ("lb_logits", 24, 2, D),
               ("pool_w", 32, 128, D), ("pool_scale", 160, 1, D), ("hgrn_norm_g", 168, 1, 2 * HD))


def _adamw_small(g_small, g_lb_logits, wmv):
    n = len(SMALL_PARTS)

    def body(g_ref, glb_ref, *refs):
        ins, outs = refs[:3 * n], refs[3 * n:]
        for p, (key, row0, rows, width) in enumerate(SMALL_PARTS):
            gv = glb_ref[...] if key == "lb_logits" else g_ref[row0:row0 + rows, 0:width]
            res = _adamw_update(ins[3 * p][...], gv, ins[3 * p + 1][...], ins[3 * p + 2][...])
            for t in range(3):
                outs[3 * p + t][...] = res[t]

    flat = [t for triple in wmv for t in triple]
    outs = pl.pallas_call(body, name="adamw_small",
                          out_shape=[jax.ShapeDtypeStruct(t.shape, F32) for t in flat],
                          compiler_params=_cp())(g_small, g_lb_logits, *flat)
    return [outs[3 * p:3 * p + 3] for p in range(n)]


def _cast_to_slot(place, w, l, name):
    _, R, C = w.shape
    tr = _row_tile(R, max(8, (1 << 19) // C))

    def body(p_ref, w_ref, o_ref):
        o_ref[...] = w_ref[...].astype(BF16)

    return pl.pallas_call(
        body, name=name, out_shape=jax.ShapeDtypeStruct((NCHIP, R, C), BF16),
        grid_spec=pltpu.PrefetchScalarGridSpec(
            num_scalar_prefetch=1, grid=(R // tr,),
            in_specs=[pl.BlockSpec((None, tr, C), lambda i, p_ref: (l, i, 0))],
            out_specs=pl.BlockSpec((None, tr, C), lambda i, p_ref: (p_ref[0], i, 0))),
        compiler_params=_cp(("parallel",)),
    )(place, w)


def _pair_add(core, g, got, name):
    _, R, C = g.shape
    r2 = R // 2
    tr = _row_tile(r2, max(8, (1 << 19) // C))
    nt = r2 // tr

    def body(c_ref, a_ref, b_ref, o_ref):
        o_ref[...] = (a_ref[...].astype(F32) + b_ref[...].astype(F32)).astype(o_ref.dtype)

    return pl.pallas_call(
        body, name=name, out_shape=jax.ShapeDtypeStruct((NCHIP, r2, C), BF16),
        grid_spec=pltpu.PrefetchScalarGridSpec(
            num_scalar_prefetch=1, grid=(NCHIP, nt),
            in_specs=[pl.BlockSpec((None, tr, C), lambda j, i, c_ref: (j, c_ref[0] * nt + i, 0)),
                      pl.BlockSpec((None, tr, C), lambda j, i, c_ref: (j, i, 0))],
            out_specs=pl.BlockSpec((None, tr, C), lambda j, i, c_ref: (j, i, 0))),
        compiler_params=_cp(("parallel", "parallel")),
    )(core, g, got)


def _chip_sum(place, part, recv, layer, both, name):
    _, r2, C = part.shape
    tr = _row_tile(r2, max(8, (1 << 18) // C))
    nt = r2 // tr

    def body(p_ref, own_ref, r_ref, *rest):
        o_ref = rest[-1]
        me = p_ref[0]
        own = own_ref[...].astype(F32)
        acc = None
        for j in range(NCHIP):
            slot = jnp.minimum(jnp.where(j > me, j - 1, j), NCHIP - 2)
            term = jnp.where(me == j, own, r_ref[slot].astype(F32))
            acc = term if acc is None else acc + term
        o_ref[...] = acc

    args = (place, part, recv) if both is None else (place, part, recv, both)
    return pl.pallas_call(
        body, name=name, out_shape=jax.ShapeDtypeStruct((2, 2 * r2, C), F32),
        grid_spec=pltpu.PrefetchScalarGridSpec(
            num_scalar_prefetch=1, grid=(nt,),
            in_specs=[pl.BlockSpec((None, tr, C), lambda i, p_ref: (p_ref[0], i, 0)),
                      pl.BlockSpec((NCHIP - 1, tr, C), lambda i, p_ref: (0, i, 0))] + [ANY] * (len(args) - 3),
            out_specs=pl.BlockSpec((None, tr, C), lambda i, p_ref: (layer, p_ref[1] * nt + i, 0))),
        input_output_aliases={} if both is None else {3: 0},
        compiler_params=_cp(("parallel",)),
    )(*args)


def _place():
    x, y, c = lax.axis_index("x"), lax.axis_index("y"), lax.axis_index("c")
    chips = [(1 - x, y), (x, 1 - y), (1 - x, 1 - y)]
    return x, y, c, chips


def _gather_small(blk, name):
    m_per, n = blk.shape

    def body(x_ref, out_ref, send_sems, recv_sems, local_sem):
        x, y, c, chips = _place()
        me, sibling = (x, y, c), (x, y, 1 - c)

        def rows(px, py, pc):
            return out_ref.at[pl.ds((4 * px + 2 * py + pc) * m_per, m_per), :]

        def copy(k, block, to, src=None):
            return pltpu.make_async_remote_copy(
                src_ref=rows(*block) if src is None else src, dst_ref=rows(*block),
                send_sem=send_sems.at[k], recv_sem=recv_sems.at[k], device_id=to, device_id_type=MESH)

        mine = pltpu.make_async_copy(x_ref, rows(*me), local_sem)
        mine.start()
        first = [copy(0, me, sibling, src=x_ref)]
        first += [copy(1 + j, me, (*chip, c), src=x_ref) for j, chip in enumerate(chips)]
        for cp in first:
            cp.start()
        passed = [copy(4 + j, (*chip, c), sibling) for j, chip in enumerate(chips)]
        for j, chip in enumerate(chips):
            copy(1 + j, (*chip, c), me).wait_recv()
            passed[j].start()
        copy(0, sibling, me).wait_recv()
        for j, chip in enumerate(chips):
            copy(4 + j, (*chip, 1 - c), me).wait_recv()
        for cp in first + passed:
            cp.wait_send()
        mine.wait()

    return pl.pallas_call(
        body, name=name, out_shape=jax.ShapeDtypeStruct((NDEV * m_per, n), blk.dtype),
        in_specs=[pl.BlockSpec(memory_space=pltpu.VMEM)], out_specs=pl.BlockSpec(memory_space=pltpu.VMEM),
        scratch_shapes=[pltpu.SemaphoreType.DMA((7,)), pltpu.SemaphoreType.DMA((7,)), pltpu.SemaphoreType.DMA],
        compiler_params=_cp(),
    )(blk)


def _gather_rows_carry(blk):
    m_per, n = blk.shape

    def rows(ref, px, py, pc):
        return ref.at[pl.ds((4 * px + 2 * py + pc) * m_per, m_per), :]

    def copy(ins, outs, send_sems, recv_sems, k, block, to, own=False):
        return pltpu.make_async_remote_copy(
            src_ref=ins[0] if own else rows(outs[0], *block), dst_ref=rows(outs[0], *block),
            send_sem=send_sems.at[k], recv_sem=recv_sems.at[k], device_id=to, device_id_type=MESH)

    def mine(ins, outs, send_sems):
        x, y, c, _ = _place()
        return pltpu.make_async_copy(ins[0], rows(outs[0], x, y, c), send_sems.at[7])

    def start(ins, outs, send_sems, recv_sems):
        x, y, c, chips = _place()
        mine(ins, outs, send_sems).start()
        copy(ins, outs, send_sems, recv_sems, 0, (x, y, c), (x, y, 1 - c), own=True).start()
        for j, chip in enumerate(chips):
            copy(ins, outs, send_sems, recv_sems, 1 + j, (x, y, c), (*chip, c), own=True).start()

    def finish(ins, outs, send_sems, recv_sems):
        x, y, c, chips = _place()
        for j, chip in enumerate(chips):
            copy(ins, outs, send_sems, recv_sems, 1 + j, (*chip, c), (x, y, c)).wait_recv()
            copy(ins, outs, send_sems, recv_sems, 4 + j, (*chip, c), (x, y, 1 - c)).start()
        copy(ins, outs, send_sems, recv_sems, 0, (x, y, 1 - c), (x, y, c)).wait_recv()
        for j, chip in enumerate(chips):
            copy(ins, outs, send_sems, recv_sems, 4 + j, (*chip, 1 - c), (x, y, c)).wait_recv()
        copy(ins, outs, send_sems, recv_sems, 0, (x, y, c), (x, y, 1 - c), own=True).wait_send()
        for j, chip in enumerate(chips):
            copy(ins, outs, send_sems, recv_sems, 1 + j, (x, y, c), (*chip, c), own=True).wait_send()
            copy(ins, outs, send_sems, recv_sems, 4 + j, (*chip, c), (x, y, 1 - c)).wait_send()
        mine(ins, outs, send_sems).wait()

    return _Carry([blk], [jax.ShapeDtypeStruct((NDEV * m_per, n), blk.dtype)], {}, 8, start, finish)


def _gather_carry(shards, piece=(0, 1, 1)):
    n = len(shards)
    first, count, of = piece

    def rows(ref, half):
        r2 = ref.shape[1] // 2
        return pl.ds(half * r2 + first * (r2 // of), count * (r2 // of))

    def over_ici(outs, send_sems, recv_sems, a, j, chip_xy, slot):
        x, y, c, _ = _place()
        blk = outs[a].at[slot, rows(outs[a], c), :]
        return pltpu.make_async_remote_copy(
            src_ref=blk, dst_ref=blk, send_sem=send_sems.at[6 * a + j], recv_sem=recv_sems.at[6 * a + j],
            device_id=(*chip_xy, c), device_id_type=MESH)

    def over_d2d(outs, send_sems, recv_sems, a, j, slot, half):
        x, y, c, _ = _place()
        blk = outs[a].at[slot, rows(outs[a], half), :]
        return pltpu.make_async_remote_copy(
            src_ref=blk, dst_ref=blk, send_sem=send_sems.at[6 * a + 3 + j], recv_sem=recv_sems.at[6 * a + 3 + j],
            device_id=(x, y, 1 - c), device_id_type=MESH)

    def start(ins, outs, send_sems, recv_sems):
        x, y, c, chips = _place()
        for a in range(n):
            for j, chip_xy in enumerate(chips):
                over_ici(outs, send_sems, recv_sems, a, j, chip_xy, 2 * x + y).start()

    def finish(ins, outs, send_sems, recv_sems):
        x, y, c, chips = _place()
        for a in range(n):
            for j, (cx, cy) in enumerate(chips):
                over_ici(outs, send_sems, recv_sems, a, j, (cx, cy), 2 * cx + cy).wait_recv()
                over_d2d(outs, send_sems, recv_sems, a, j, 2 * cx + cy, c).start()
        for a in range(n):
            for j, (cx, cy) in enumerate(chips):
                over_d2d(outs, send_sems, recv_sems, a, j, 2 * cx + cy, 1 - c).wait_recv()
        for a in range(n):
            for j, (cx, cy) in enumerate(chips):
                over_ici(outs, send_sems, recv_sems, a, j, (cx, cy), 2 * x + y).wait_send()
                over_d2d(outs, send_sems, recv_sems, a, j, 2 * cx + cy, c).wait_send()

    return _Carry(shards, [jax.ShapeDtypeStruct(s.shape, s.dtype) for s in shards],
                  {a: a for a in range(n)}, 6 * n, start, finish)


def _rs_pair(grads, name):
    n = len(grads)

    def body(*refs):
        ins, gots = refs[:n], refs[n:2 * n]
        send_sems, recv_sems = refs[2 * n:]
        x, y, c, _ = _place()
        cps = []
        for a in range(n):
            r2 = ins[a].shape[1] // 2
            cp = pltpu.make_async_remote_copy(
                src_ref=ins[a].at[:, pl.ds((1 - c) * r2, r2), :], dst_ref=gots[a],
                send_sem=send_sems.at[a], recv_sem=recv_sems.at[a],
                device_id=(x, y, 1 - c), device_id_type=MESH)
            cp.start()
            cps.append(cp)
        for cp in cps:
            cp.wait()

    half = [jax.ShapeDtypeStruct((NCHIP, g.shape[1] // 2, g.shape[2]), g.dtype) for g in grads]
    return pl.pallas_call(
        body, name=name, out_shape=half, in_specs=[ANY] * n, out_specs=[ANY] * n,
        scratch_shapes=[pltpu.SemaphoreType.DMA((n,)), pltpu.SemaphoreType.DMA((n,))],
        compiler_params=_cp(),
    )(*grads)


def _pair_carry(grads):
    n = len(grads)

    def copy(ins, outs, send_sems, recv_sems, a):
        x, y, c, _ = _place()
        r2 = ins[a].shape[1] // 2
        return pltpu.make_async_remote_copy(
            src_ref=ins[a].at[:, pl.ds((1 - c) * r2, r2), :], dst_ref=outs[a],
            send_sem=send_sems.at[a], recv_sem=recv_sems.at[a],
            device_id=(x, y, 1 - c), device_id_type=MESH)

    def start(ins, outs, send_sems, recv_sems):
        for a in range(n):
            copy(ins, outs, send_sems, recv_sems, a).start()

    def finish(ins, outs, send_sems, recv_sems):
        for a in range(n):
            copy(ins, outs, send_sems, recv_sems, a).wait()

    half = [jax.ShapeDtypeStruct((NCHIP, g.shape[1] // 2, g.shape[2]), g.dtype) for g in grads]
    return _Carry(grads, half, {}, n, start, finish)


def _chips_carry(parts, piece=(0, 1, 1), into=None):
    n = len(parts)
    first, count, of = piece

    def rows(ref):
        step = ref.shape[1] // of
        return pl.ds(first * step, count * step)

    def send(ins, outs, send_sems, recv_sems, a, j, chip_xy):
        x, y, c, _ = _place()
        me, them = 2 * x + y, 2 * chip_xy[0] + chip_xy[1]
        return pltpu.make_async_remote_copy(
            src_ref=ins[a].at[them, rows(ins[a]), :],
            dst_ref=outs[a].at[me - (me > them).astype(jnp.int32), rows(outs[a]), :],
            send_sem=send_sems.at[3 * a + j], recv_sem=recv_sems.at[3 * a + j],
            device_id=(*chip_xy, c), device_id_type=MESH)

    def start(ins, outs, send_sems, recv_sems):
        _, _, _, chips = _place()
        for a in range(n):
            for j, chip_xy in enumerate(chips):
                send(ins, outs, send_sems, recv_sems, a, j, chip_xy).start()

    def finish(ins, outs, send_sems, recv_sems):
        x, y, c, chips = _place()
        me = 2 * x + y
        for a in range(n):
            for j, (cx, cy) in enumerate(chips):
                them = 2 * cx + cy
                blk = outs[a].at[them - (them > me).astype(jnp.int32), rows(outs[a]), :]
                pltpu.make_async_remote_copy(
                    src_ref=blk, dst_ref=blk, send_sem=send_sems.at[3 * a + j], recv_sem=recv_sems.at[3 * a + j],
                    device_id=(cx, cy, c), device_id_type=MESH).wait_recv()
        for a in range(n):
            for j, chip_xy in enumerate(chips):
                send(ins, outs, send_sems, recv_sems, a, j, chip_xy).wait_send()

    landing = [jax.ShapeDtypeStruct((NCHIP - 1,) + p.shape[1:], p.dtype) for p in parts]
    if into is None:
        return _Carry(parts, landing, {}, 3 * n, start, finish)
    return _Carry(list(parts) + list(into), landing, {n + a: a for a in range(n)}, 3 * n, start, finish)


def _rs_swap(fulls):
    n = len(fulls)

    def body(*refs):
        outs = refs[n:2 * n]
        send_sems, recv_sems = refs[2 * n:]
        x, y, c, _ = _place()
        cps = []
        for a in range(n):
            r2 = outs[a].shape[1] // 2
            mine = outs[a].at[:, pl.ds(c * r2, r2), :]
            cp = pltpu.make_async_remote_copy(
                src_ref=mine, dst_ref=mine, send_sem=send_sems.at[a], recv_sem=recv_sems.at[a],
                device_id=(x, y, 1 - c), device_id_type=MESH)
            cp.start()
            cps.append(cp)
        for a in range(n):
            r2 = outs[a].shape[1] // 2
            blk = outs[a].at[:, pl.ds((1 - c) * r2, r2), :]
            pltpu.make_async_remote_copy(
                src_ref=blk, dst_ref=blk, send_sem=send_sems.at[a], recv_sem=recv_sems.at[a],
                device_id=(x, y, 1 - c), device_id_type=MESH).wait_recv()
        for cp in cps:
            cp.wait_send()

    return pl.pallas_call(
        body, name="rs_swap", out_shape=[jax.ShapeDtypeStruct(f.shape, f.dtype) for f in fulls],
        in_specs=[ANY] * n, out_specs=[ANY] * n, input_output_aliases={a: a for a in range(n)},
        scratch_shapes=[pltpu.SemaphoreType.DMA((n,)), pltpu.SemaphoreType.DMA((n,))],
        compiler_params=_cp(),
    )(*fulls)


def _tail_weight_grads(merged_t, b_in_t, a_in_t, dy, dbr_b, dbr_a, name, tn=256):
    S = dy.shape[0]
    nn = D // tn

    def body(mt_ref, bt_ref, at_ref, dy_ref, db_ref, da_ref, go_ref, gh_ref, gp_ref):
        go_ref[...] = jnp.dot(mt_ref[...], dy_ref[...], preferred_element_type=F32).astype(BF16)
        gh_ref[...] = jnp.dot(bt_ref[...], db_ref[...], preferred_element_type=F32).astype(BF16)
        gp_ref[...] = jnp.dot(at_ref[...], da_ref[...], preferred_element_type=F32).astype(BF16)

    left = lambda rows: pl.BlockSpec((rows, S), lambda n: (0, 0))
    right = pl.BlockSpec((S, tn), lambda n: (0, n))
    out = pl.BlockSpec((D, tn), lambda n: (0, n))
    return pl.pallas_call(
        body, name=name, grid=(nn,), in_specs=[left(D), left(D), left(POOL_W), right, right, right],
        out_specs=[out, out, pl.BlockSpec((None, POOL_W, tn), lambda n: (n, 0, 0))],
        out_shape=[jax.ShapeDtypeStruct((D, D), BF16), jax.ShapeDtypeStruct((D, D), BF16),
                   jax.ShapeDtypeStruct((NCHIP, POOL_W, D // NCHIP), BF16)],
        compiler_params=_cp(("parallel",)),
    )(merged_t, b_in_t, a_in_t, dy, dbr_b, dbr_a)


class _GatherInProj:
    def __init__(self, slot, order):
        self.slot, self.order = slot, order


def _proj_with_gather(h, w_slot, order, name, tn=256):
    S, K = h.shape
    nsh, _, ns = w_slot.shape
    tps = ns // tn
    nt = nsh * tps
    r2 = K // 2

    def body(ord_ref, h_ref, w_in_ref, o_ref, w_ref, wbuf, tile_sems, send_sems, recv_sems):
        n = pl.program_id(0)
        x, y, c, chips = _place()

        def half(slot, which):
            return w_ref.at[slot, pl.ds(which * r2, r2), :]

        def over_ici(j, slot):
            blk = half(slot, c)
            return pltpu.make_async_remote_copy(src_ref=blk, dst_ref=blk, send_sem=send_sems.at[j],
                                                recv_sem=recv_sems.at[j], device_id=(*chips[j], c),
                                                device_id_type=MESH)

        def over_d2d(j, which):
            blk = half(2 * chips[j][0] + chips[j][1], which)
            return pltpu.make_async_remote_copy(src_ref=blk, dst_ref=blk, send_sem=send_sems.at[3 + j],
                                                recv_sem=recv_sems.at[3 + j], device_id=(x, y, 1 - c),
                                                device_id_type=MESH)

        def tile_copy(step, slot):
            shard = ord_ref[step // tps]
            return pltpu.make_async_copy(w_ref.at[shard, :, pl.ds((step % tps) * tn, tn)], wbuf.at[slot],
                                         tile_sems.at[slot])

        @pl.when(n == 0)
        def _():
            for j in range(3):
                over_ici(j, 2 * x + y).start()
            tile_copy(0, 0).start()

        for j in range(3):
            @pl.when(n == (j + 1) * tps - 1)
            def _(j=j):
                over_ici(j, 2 * chips[j][0] + chips[j][1]).wait_recv()
                over_d2d(j, c).start()
                over_d2d(j, 1 - c).wait_recv()

        @pl.when(n + 1 < nt)
        def _():
            tile_copy(n + 1, (n + 1) % 2).start()

        tile_copy(n, n % 2).wait()
        o_ref[...] = jnp.dot(h_ref[...], wbuf[n % 2], preferred_element_type=F32).astype(o_ref.dtype)

        @pl.when(n == nt - 1)
        def _():
            for j in range(3):
                over_ici(j, 2 * x + y).wait_send()
                over_d2d(j, c).wait_send()

    return pl.pallas_call(
        body, name=name,
        out_shape=[jax.ShapeDtypeStruct((S, nsh * ns), BF16), jax.ShapeDtypeStruct(w_slot.shape, w_slot.dtype)],
        grid_spec=pltpu.PrefetchScalarGridSpec(
            num_scalar_prefetch=1, grid=(nt,),
            in_specs=[pl.BlockSpec((S, K), lambda n, o_ref: (0, 0)), ANY],
            out_specs=[pl.BlockSpec((S, tn), lambda n, o_ref: (0, o_ref[n // tps] * tps + n % tps)), ANY],
            scratch_shapes=[pltpu.VMEM((2, K, tn), w_slot.dtype), pltpu.SemaphoreType.DMA((2,)),
                            pltpu.SemaphoreType.DMA((6,)), pltpu.SemaphoreType.DMA((6,))]),
        input_output_aliases={2: 1},
        compiler_params=_cp(("arbitrary",)),
    )(order, h, w_slot)


def _mm_ride(a, b, carry, **kw):
    if carry is None:
        return _mm(a, b, **kw), []
    return _mm(a, b, carry=carry, **kw)


def _layer_fwd(l, x, ada, w, small, ride, target=None):
    shift, scale, gate = ada[:, 0:D], ada[:, D:2 * D], ada[:, 2 * D:3 * D]
    carry, landed = ride("prenorm")
    (h, h_t), outs = _prenorm_fwd(x, small["g_pre"][l], scale, shift, f"prenorm_fwd{l}", carry)
    landed(outs)
    carry, landed = ride("proj")
    if isinstance(carry, _GatherInProj):
        proj, full = _proj_with_gather(h, carry.slot, carry.order, f"proj{l}")
        outs = [full]
    else:
        proj, outs = _mm_ride(h, w["w_in"][l], carry, name=f"proj{l}", b_mode="nn_sh", tm=2048, out_dtype=BF16)
    landed(outs)
    a_in, a_in_t = _pool_fwd(proj, small["pool_w"][l], small["pool_scale"][l], f"pool_fwd{l}")
    carry, landed = ride("hgrn")
    (b_in, b_in_t, o_raw, states, mild, cum), outs = _hgrn_fwd(proj, small["lb"][l], small["hgrn_norm_g"][l],
                                                              f"hgrn_fwd{l}", carry=carry)
    landed(outs)
    carry, landed = ride("tail")
    (br_a, br_b, merged_t, y, *x_new), outs = _layer_tail_fwd(
        proj, a_in, b_in, x, w["w_pool_o"][l], w["w_hgrn_o"][l].reshape(D, D), w["w_out"][l].reshape(D, D),
        gate, small["g_post"][l], f"tail_fwd{l}", target=target, carry=carry)
    landed(outs)
    saved = dict(x=x, h_t=h_t, proj=proj, a_in_t=a_in_t, b_in_t=b_in_t, o_raw=o_raw, states=states, mild=mild,
                 cum=cum,
                 br_a=br_a, br_b=br_b, merged_t=merged_t, y=y, scale=scale, gate=gate)
    return x_new, saved


def _layer_bwd(l, dxn, sv, w, small, ride):
    carry, landed = ride["head"](None)
    (dy, dbr_a, dbr_b, dproj, da_in, db_in, dgate, dg_post), outs = _layer_head_bwd(
        dxn, sv["y"], sv["proj"], sv["br_a"], sv["br_b"], w["w_pool_o"][l], w["w_hgrn_o"][l].reshape(D, D),
        w["w_out"][l].reshape(D, D), sv["gate"], small["g_post"][l], f"head_bwd{l}", carry)
    landed(outs)
    gw_out, gw_hgrn_o, gw_pool_o = _tail_weight_grads(sv["merged_t"], sv["b_in_t"], sv["a_in_t"], dy, dbr_b,
                                                      dbr_a, f"gw_tail{l}")
    big = dict(w_pool_o=gw_pool_o, w_hgrn_o=gw_hgrn_o.reshape(NCHIP, D // NCHIP, D),
               w_out=gw_out.reshape(NCHIP, D // NCHIP, D))
    carry, landed = ride["hgrn"](big)
    (dproj, dlb, dgn), outs = _hgrn_bwd(db_in, sv["proj"], sv["o_raw"], sv["states"], sv["mild"], sv["cum"],
                                        small["lb"][l], small["hgrn_norm_g"][l], dproj, f"hgrn_bwd{l}",
                                        carry=carry)
    landed(outs)
    dproj, dpw, dpsc = _pool_bwd(da_in, sv["proj"], small["pool_w"][l], small["pool_scale"][l], dproj,
                                 f"pool_bwd{l}")
    little = dict(dgate=dgate, g_post=dg_post, pool_w=dpw, pool_scale=dpsc, lb=dlb,
                  hgrn_norm_g=jnp.sum(dgn, axis=0, keepdims=True))
    carry, landed = ride["gw_in"](little)
    big["w_in"], outs = _mm_ride(sv["h_t"], dproj, carry, name=f"gw_in{l}", out_shards=NCHIP, out_dtype=BF16)
    landed(outs)
    carry, landed = ride["d_h"](big)
    dh, outs = _mm_ride(dproj, w["w_in"][l], carry, name=f"d_h{l}", b_mode="nt_shk", tn=1024)
    landed(outs)
    carry, landed = ride["prenorm"](big)
    (dx, dshift, dscale, dg_pre), outs = _prenorm_bwd(dh, dxn, sv["x"], small["g_pre"][l], sv["scale"],
                                                      f"prenorm_bwd{l}", carry)
    landed(outs)
    little.update(dshift=dshift, dscale=dscale, g_pre=dg_pre)
    return dx, big, little


SMALL_ROWS = 176


def _rows8(t):
    t = t.reshape(-1, D)
    return jnp.pad(t, ((0, -t.shape[0] % 8), (0, 0)))


def _pack_small(parts):
    row_keys = ("dshift", "dscale", "dgate", "g_pre", "g_post", "lb", "pool_scale", "hgrn_norm_g")
    flat = [p[k] for p in parts for k in row_keys] + [p["pool_w"].reshape(GROUPS * 128 * 128 // D, D) for p in parts]
    nk = len(row_keys)

    def body(*refs):
        o_ref = refs[-1]
        o_ref[...] = jnp.zeros((SMALL_ROWS, D), F32)
        for l in range(2):
            dshift, dscale, dgate, g_pre, g_post, lb, pscale, gn = refs[l * nk:(l + 1) * nk]
            for r, ref in enumerate((dshift, dscale, dgate)):
                o_ref[3 * l + r:3 * l + r + 1, :] = ref[...]
            o_ref[8 + l:9 + l, :] = g_pre[...]
            o_ref[16 + l:17 + l, :] = g_post[...]
            o_ref[24 + l:25 + l, :] = lb[...]
            o_ref[160:161, l * POOL_W:(l + 1) * POOL_W] = pscale[...]
            o_ref[168:169, l * HD:(l + 1) * HD] = gn[...]
            pw = refs[2 * nk + l]
            rows = pw.shape[0]
            o_ref[32 + l * rows:32 + (l + 1) * rows, :] = pw[...]

    return pl.pallas_call(body, name="pack_small", out_shape=jax.ShapeDtypeStruct((SMALL_ROWS, D), F32),
                          compiler_params=_cp())(*flat)


def _unpack_small(p):
    return (p[0:6].reshape(2, 3 * D), p[8:10], p[16:18], p[24:26], p[32:160].reshape(2, GROUPS, 128, 128),
            p[160:161].reshape(2, POOL_W), p[168:169, 0:2 * HD].reshape(2, HD))


def kernel(x, c, w_ada, b_ada, g_pre, g_post, w_in, pool_w, pool_scale, lb_logits, hgrn_norm_g, w_pool_o, w_hgrn_o, w_out, loss_target, m_w_ada, m_b_ada, m_g_pre, m_g_post, m_w_in, m_pool_w, m_pool_scale, m_lb_logits, m_hgrn_norm_g, m_w_pool_o, m_w_hgrn_o, m_w_out, v_w_ada, v_b_ada, v_g_pre, v_g_post, v_w_in, v_pool_w, v_pool_scale, v_lb_logits, v_hgrn_norm_g, v_w_pool_o, v_w_hgrn_o, v_w_out):
    ax, ay, ac = lax.axis_index("x"), lax.axis_index("y"), lax.axis_index("c")
    chip = 2 * ax + ay
    dev = 2 * chip + ac
    xe, te = x[0], loss_target[0]
    ada_s = w_ada.shape[2]

    big_names = ("w_in", "w_pool_o", "w_hgrn_o", "w_out")
    big_w = (w_in, w_pool_o, w_hgrn_o, w_out)
    core = jnp.stack([ac]).astype(jnp.int32)
    place = jnp.stack([chip, ac]).astype(jnp.int32)
    slots = {(k, l): _cast_to_slot(place, t, l, f"cast_{k}{l}") for l in range(2) for k, t in zip(big_names, big_w)}
    w = {k: [None, None] for k in big_names}
    def fills(keys):
        def landed(outs):
            for (k, l), o in zip(keys, outs):
                w[k][l] = slots[k, l] = o
        return landed

    rest0 = [(k, 0) for k in big_names[1:]]
    rest1 = [(k, 1) for k in big_names[1:]]
    no_carry = (None, lambda outs: None)
    order = jnp.stack([chip, 2 * (1 - ax) + ay, 2 * ax + (1 - ay), 2 * (1 - ax) + (1 - ay)]).astype(jnp.int32)

    def ride_fwd0(stage):
        if stage == "proj":
            return _GatherInProj(slots["w_in", 0], order), fills([("w_in", 0)])
        if stage == "hgrn":
            return (_join_carries(_gather_carry([slots[t] for t in rest0]),
                                  _gather_carry([slots["w_in", 1]], piece=(0, 2, 4))),
                    fills(rest0 + [("w_in", 1)]))
        if stage == "tail":
            return _gather_carry([slots["w_in", 1]], piece=(2, 1, 4)), fills([("w_in", 1)])
        return no_carry

    def ride_fwd1(stage):
        if stage == "prenorm":
            return _gather_carry([slots["w_in", 1]], piece=(3, 1, 4)), fills([("w_in", 1)])
        if stage == "hgrn":
            return _gather_carry([slots[t] for t in rest1]), fills(rest1)
        return no_carry

    c_all = _gather_small(jnp.broadcast_to(c, (8, D)), "gather_c").reshape(NDEV, 8, D)[:, 0, :]
    c_pad = jnp.pad(c_all, ((0, ADA_PAD - NDEV), (0, 0)))
    b_sh = lax.dynamic_slice(b_ada, (0, chip * ada_s), (2, ada_s))
    ada_cols = _gather_small(_ada_fwd(c_pad, w_ada, b_sh), "gather_ada")
    ada_cols = ada_cols.reshape(NCHIP, 2, NDEV, 2, ada_s)[:, 0]
    ada_all = jnp.transpose(ada_cols, (2, 1, 0, 3)).reshape(2, NDEV, 3 * D)
    ada_me = lax.dynamic_slice(ada_all, (0, dev, 0), (2, 1, 3 * D))

    lbs = _lb_fwd(lb_logits)
    small = dict(g_pre=g_pre[:, None, :], g_post=g_post[:, None, :], pool_w=pool_w,
                 pool_scale=pool_scale[:, None, :], lb=lbs[:, None, :], hgrn_norm_g=hgrn_norm_g[:, None, :])

    (x1,), sv0 = _layer_fwd(0, xe, ada_me[0], w, small, ride_fwd0)
    (dx2, loss_blk), sv1 = _layer_fwd(1, x1, ada_me[1], w, small, ride_fwd1, target=te)

    parts, recv, held = {}, {}, {}

    def pair_ride(keys, grads):
        def landed(outs):
            held.update({kl: (g, o) for kl, g, o in zip(keys, grads, outs)})
        return _pair_carry(grads), landed

    def pair_adds(keys):
        for kl in keys:
            g, got = held.pop(kl)
            parts[kl] = _pair_add(core, g, got, f"rs_add_{kl[0]}{kl[1]}")

    def exchange(keys):
        def landed(outs):
            recv.update(zip(keys, outs))
        return _chips_carry([parts[kl] for kl in keys]), landed

    def halves(key, second):
        def landed(outs):
            (recv[key],) = outs
        if second:
            return _chips_carry([parts[key]], piece=(1, 1, 2), into=[recv[key]]), landed
        return _chips_carry([parts[key]], piece=(0, 1, 2)), landed

    def both(*rides):
        carries, fns = zip(*rides)

        def landed(outs):
            for cr, fn in zip(carries, fns):
                fn(outs[:len(cr.outs)])
                outs = outs[len(cr.outs):]
        return _join_carries(*carries), landed

    def early(l):
        return [(k, l) for k in big_names[1:]]

    def ride_hgrn1(big):
        return pair_ride(early(1), [big[k] for k in big_names[1:]])

    def ride_gw_in1(_):
        pair_adds(early(1))
        return exchange(early(1))

    def ride_d_h1(big):
        return pair_ride([("w_in", 1)], [big["w_in"]])

    def ride_prenorm1(_):
        pair_adds([("w_in", 1)])
        return no_carry

    def ride_head0(_):
        return halves(("w_in", 1), False)

    def ride_hgrn0(big):
        return both(pair_ride(early(0), [big[k] for k in big_names[1:]]), halves(("w_in", 1), True))

    def ride_d_h0(big):
        (got,) = _rs_pair([big["w_in"]], "rs_pair_w_in0")
        held["w_in", 0] = (big["w_in"], got)
        pair_adds([("w_in", 0)])
        return halves(("w_in", 0), False)

    def ride_prenorm0(_):
        return halves(("w_in", 0), True)

    no_ride = lambda so_far: no_carry
    dx1, big1, little1 = _layer_bwd(1, dx2, sv1, w, small, dict(head=no_ride, hgrn=ride_hgrn1, gw_in=ride_gw_in1,
                                                                d_h=ride_d_h1, prenorm=ride_prenorm1))

    gathered = {}
    zero_row = jnp.zeros((1, D), F32)

    def ride_gw_in0(little):
        so_far = dict(little, dshift=zero_row, dscale=zero_row, g_pre=zero_row)

        def landed(outs):
            (gathered["early"],) = outs
        pair_adds(early(0))
        return both(exchange(early(0)), (_gather_rows_carry(_pack_small([so_far, little1])), landed))

    dx0, big0, little0 = _layer_bwd(0, dx1, sv0, w, small,
                                    dict(head=ride_head0, hgrn=ride_hgrn0, gw_in=ride_gw_in0, d_h=ride_d_h0,
                                         prenorm=ride_prenorm0))
    loss_row = jnp.broadcast_to(loss_blk[0:1, 0:1], (1, D))
    late = _rows8(jnp.stack([little0["dshift"], little0["dscale"], little0["g_pre"], loss_row]))
    late = _gather_small(late, "gather_small_late").reshape(NDEV, 8, D)
    loss = jnp.sum(late[:, 3, 0])
    packed = gathered["early"].reshape(NDEV, SMALL_ROWS, D)
    packed = packed.at[:, 0:2, :].set(late[:, 0:2, :]).at[:, 8:9, :].set(late[:, 2:3, :])
    red = []
    for k in big_names:
        both = _chip_sum(place, parts[k, 1], recv[k, 1], 1, None, f"rs_sum_{k}1")
        red.append(_chip_sum(place, parts[k, 0], recv[k, 0], 0, both, f"rs_sum_{k}0"))
    g_big = dict(zip(big_names, _rs_swap(red)))

    def upd(wt, g, m, v, name, carry=None):
        shp = wt.shape
        two = lambda t: t.reshape(-1, shp[-1])
        res = _adamw(two(wt), two(g), two(m), two(v), name, carry)
        return [t.reshape(shp) for t in res[:3]], res[3:]

    u_w_in, _ = upd(w_in, g_big["w_in"], m_w_in, v_w_in, "adamw_w_in")
    g_small = _sum_devices(packed)
    g_b_ada, g_g_pre, g_g_post, g_lb, g_pool_w, g_pool_scale, g_norm_g = _unpack_small(g_small)
    g_lb_logits = _lb_bwd(lb_logits, g_lb)
    d_ada_all = packed[:, 0:6, :].reshape(NDEV, 2, 3 * D)
    d_ada_sh = lax.dynamic_slice(jnp.transpose(d_ada_all, (1, 0, 2)), (0, 0, chip * ada_s), (2, NDEV, ada_s))
    d_ada_sh = jnp.pad(d_ada_sh, ((0, 0), (0, ADA_PAD - NDEV), (0, 0)))
    g_w_ada = _ada_wgrad(c_pad.T, d_ada_sh)

    u_w_ada, _ = upd(w_ada, g_w_ada, m_w_ada, v_w_ada, "adamw_w_ada")
    u_w_pool_o, _ = upd(w_pool_o, g_big["w_pool_o"], m_w_pool_o, v_w_pool_o, "adamw_w_pool_o")
    u_w_hgrn_o, _ = upd(w_hgrn_o, g_big["w_hgrn_o"], m_w_hgrn_o, v_w_hgrn_o, "adamw_w_hgrn_o")
    u_w_out, _ = upd(w_out, g_big["w_out"], m_w_out, v_w_out, "adamw_w_out")
    small_w = dict(b_ada=(b_ada, m_b_ada, v_b_ada), g_pre=(g_pre, m_g_pre, v_g_pre),
                   g_post=(g_post, m_g_post, v_g_post), lb_logits=(lb_logits, m_lb_logits, v_lb_logits),
                   pool_w=(pool_w, m_pool_w, v_pool_w), pool_scale=(pool_scale, m_pool_scale, v_pool_scale),
                   hgrn_norm_g=(hgrn_norm_g, m_hgrn_norm_g, v_hgrn_norm_g))
    in_rows = [tuple(t.reshape(rows, width) for t in small_w[key]) for key, _, rows, width in SMALL_PARTS]
    u_rows = _adamw_small(g_small, g_lb_logits, in_rows)
    u_small = {key: [t.reshape(small_w[key][0].shape) for t in u_rows[p]]
               for p, (key, _, _, _) in enumerate(SMALL_PARTS)}

    grads_out = (g_w_ada, g_b_ada, g_g_pre, g_g_post, g_big["w_in"], g_pool_w, g_pool_scale, g_lb_logits,
                 g_norm_g, g_big["w_pool_o"], g_big["w_hgrn_o"], g_big["w_out"])

    def ordered(k):
        s = lambda key: u_small[key][k]
        return (u_w_ada[k], s("b_ada"), s("g_pre"), s("g_post"), u_w_in[k], s("pool_w"), s("pool_scale"),
                s("lb_logits"), s("hgrn_norm_g"), u_w_pool_o[k], u_w_hgrn_o[k], u_w_out[k])

    return (loss, dx0[None], *grads_out, *ordered(0), *ordered(1), *ordered(2))
```

```python
import functools

import jax
import jax.numpy as jnp
from jax import lax
from jax.experimental import pallas as pl
from jax.experimental.pallas import tpu as pltpu

F32 = jnp.float32
BF16 = jnp.bfloat16
MESH = pl.DeviceIdType.MESH

D = 1024
HEADS = 8
HD = 128
GROUPS = 4
POOL_W = 512
CH = 128
SB_WIDE = 32
SB = 16
NH = 2
IN_W = 7168
NCHIP = 4
NDEV = 8
EPS = 1e-6
PV0, PG0, HQ0, HF0, HI0, HG0 = 0, 4, 8, 16, 24, 32
MGP_BLK, MGH_BLK = 5, 6

LR, B1, B2, AEPS, WD, STEP = 0.001, 0.9, 0.999, 1e-08, 0.01, 10
VMEM_LIMIT = 56 * 1024 * 1024


def _cp(sem=None, **kw):
    if sem is not None:
        kw["dimension_semantics"] = sem
    return pltpu.CompilerParams(vmem_limit_bytes=VMEM_LIMIT, **kw)


def _sig(z):
    return 1.0 / (1.0 + jnp.exp(-z))


def _dsilu(z, s):
    return s * (1.0 + z * (1.0 - s))


def _row_tile(rows, cap):
    if rows <= cap:
        return rows
    t = 1 << (cap.bit_length() - 1)
    while rows % t:
        t //= 2
    return t


ANY = pl.BlockSpec(memory_space=pl.ANY)


class _Carry:
    def __init__(self, ins, outs, aliases, n_sem, start, finish):
        self.ins, self.outs, self.aliases, self.n_sem = list(ins), list(outs), dict(aliases), n_sem
        self.start, self.finish = start, finish


class _SemWindow:
    def __init__(self, ref, base):
        self._ref, self._base = ref, base

    @property
    def at(self):
        return self

    def __getitem__(self, k):
        return self._ref.at[self._base + k]


def _join_carries(*carries):
    ins, outs, aliases, spans, n_sem = [], [], {}, [], 0
    for cr in carries:
        aliases.update({len(ins) + i: len(outs) + o for i, o in cr.aliases.items()})
        spans.append((len(ins), len(cr.ins), len(outs), len(cr.outs), n_sem))
        ins, outs, n_sem = ins + cr.ins, outs + cr.outs, n_sem + cr.n_sem

    def run(which):
        def fn(i_refs, o_refs, send_sems, recv_sems):
            for cr, (i0, ni, o0, no, s0) in zip(carries, spans):
                getattr(cr, which)(i_refs[i0:i0 + ni], o_refs[o0:o0 + no], _SemWindow(send_sems, s0),
                                   _SemWindow(recv_sems, s0))
        return fn

    return _Carry(ins, outs, aliases, n_sem, run("start"), run("finish"))


def _call(body, *, name, grid, in_specs, out_specs, out_shape, args, scratch_shapes=(), sem=None, carry=None,
          aliases=None):
    in_specs, out_specs, out_shape = list(in_specs), list(out_specs), list(out_shape)
    scratch_shapes = list(scratch_shapes)
    aliases = dict(aliases or {})
    if carry is None:
        outs = pl.pallas_call(body, name=name, grid=grid, in_specs=in_specs, out_specs=out_specs,
                              out_shape=out_shape, scratch_shapes=scratch_shapes, input_output_aliases=aliases,
                              compiler_params=_cp(sem))(*args)
        return list(outs)
    n_in, n_out, n_scr = len(in_specs), len(out_specs), len(scratch_shapes)
    c_in, c_out = len(carry.ins), len(carry.outs)

    def wrapped(*refs):
        k_in, rest = refs[:n_in], refs[n_in:]
        ci, rest = rest[:c_in], rest[c_in:]
        k_out, rest = rest[:n_out], rest[n_out:]
        co, rest = rest[:c_out], rest[c_out:]
        k_scr, (ssem, rsem) = rest[:n_scr], rest[n_scr:]
        pids = [pl.program_id(d) for d in range(len(grid))]
        first = functools.reduce(jnp.logical_and, [p == 0 for p in pids])
        last = functools.reduce(jnp.logical_and, [p == g - 1 for p, g in zip(pids, grid)])

        @pl.when(first)
        def _():
            carry.start(ci, co, ssem, rsem)

        body(*k_in, *k_out, *k_scr)

        @pl.when(last)
        def _():
            carry.finish(ci, co, ssem, rsem)

    outs = pl.pallas_call(
        wrapped, name=name, grid=grid, in_specs=in_specs + [ANY] * c_in, out_specs=out_specs + [ANY] * c_out,
        out_shape=out_shape + carry.outs,
        input_output_aliases={**aliases, **{n_in + i: n_out + o for i, o in carry.aliases.items()}},
        scratch_shapes=scratch_shapes + [pltpu.SemaphoreType.DMA((carry.n_sem,))] * 2,
        compiler_params=_cp(("arbitrary",) * len(grid)),
    )(*args, *carry.ins)
    return list(outs)


def _mm(a, b, *, name, b_mode="nn", out_shards=0, tm=1024, tn=256, tk=None, out_dtype=F32, carry=None):
    assert b_mode in ("nn", "nn_sh", "nt_shk"), b_mode
    M, K = a.shape
    if b_mode == "nn":
        N = b.shape[1]
    elif b_mode == "nn_sh":
        N = b.shape[0] * b.shape[2]
    else:
        N = b.shape[1]
    tm = _row_tile(M, tm)
    if b_mode == "nn_sh":
        tn = _row_tile(b.shape[2], tn)
    elif out_shards:
        tn = _row_tile(N // out_shards, tn)
    else:
        tn = _row_tile(N, tn)
    if tk is None:
        tk = K if b_mode != "nt_shk" else b.shape[2]
    if b_mode == "nt_shk":
        tk = _row_tile(b.shape[2], tk)
    nm, nn, nk = M // tm, N // tn, K // tk

    a_spec = pl.BlockSpec((tm, tk), lambda m, n, k: (m, k))
    if b_mode == "nn":
        b_spec = pl.BlockSpec((tk, tn), lambda m, n, k: (k, n))
    elif b_mode == "nn_sh":
        nps = b.shape[2] // tn
        b_spec = pl.BlockSpec((None, tk, tn), lambda m, n, k: (n // nps, k, n % nps))
    else:
        kps = b.shape[2] // tk
        b_spec = pl.BlockSpec((None, tn, tk), lambda m, n, k: (k // kps, n, k % kps))
    if out_shards:
        ops = (N // out_shards) // tn
        o_spec = pl.BlockSpec((None, tm, tn), lambda m, n, k: (n // ops, m, n % ops))
        o_shape = jax.ShapeDtypeStruct((out_shards, M, N // out_shards), out_dtype)
    else:
        o_spec = pl.BlockSpec((tm, tn), lambda m, n, k: (m, n))
        o_shape = jax.ShapeDtypeStruct((M, N), out_dtype)
    dn = (((1,), (1,)), ((), ())) if b_mode == "nt_shk" else (((1,), (0,)), ((), ()))

    def body(a_ref, b_ref, o_ref, acc_ref):
        k = pl.program_id(2)

        @pl.when(k == 0)
        def _():
            acc_ref[...] = jnp.zeros(acc_ref.shape, F32)

        acc_ref[...] += lax.dot_general(a_ref[...].astype(BF16), b_ref[...].astype(BF16), dn,
                                        preferred_element_type=F32)

        @pl.when(k == nk - 1)
        def _():
            o_ref[...] = acc_ref[...].astype(o_ref.dtype)

    outs = _call(body, name=name, grid=(nm, nn, nk), in_specs=[a_spec, b_spec], out_specs=[o_spec],
                 out_shape=[o_shape], scratch_shapes=[pltpu.VMEM((tm, tn), F32)],
                 sem=("parallel", "parallel", "arbitrary"), args=(a, b), carry=carry)
    return outs[0] if carry is None else (outs[0], outs[1:])


def _rowvec(n=D):
    return pl.BlockSpec((1, n), lambda i: (0, 0))


def _prenorm_fwd(x, g, scale, shift, name, carry=None):
    S = x.shape[0]
    tr = _row_tile(S, 256)

    def body(x_ref, g_ref, sc_ref, sh_ref, h_ref, ht_ref):
        xv = x_ref[...]
        r = lax.rsqrt(jnp.mean(xv * xv, axis=-1, keepdims=True) + EPS)
        hv = (xv * r) * g_ref[...] * (1.0 + sc_ref[...]) + sh_ref[...]
        h_ref[...] = hv.astype(BF16)
        ht_ref[...] = hv.T.astype(BF16)

    outs = _call(
        body, name=name, grid=(S // tr,),
        in_specs=[pl.BlockSpec((tr, D), lambda i: (i, 0)), _rowvec(), _rowvec(), _rowvec()],
        out_specs=[pl.BlockSpec((tr, D), lambda i: (i, 0)), pl.BlockSpec((D, tr), lambda i: (0, i))],
        out_shape=[jax.ShapeDtypeStruct((S, D), BF16), jax.ShapeDtypeStruct((D, S), BF16)],
        sem=("parallel",), args=(x, g, scale, shift), carry=carry)
    return outs[:2], outs[2:]


def _prenorm_bwd(dh, dxn, x, g, scale, name, carry=None):
    S = x.shape[0]
    tr = _row_tile(S, 256)

    def body(dh_ref, dxn_ref, x_ref, g_ref, sc_ref, dx_ref, dsh_ref, dsc_ref, dg_ref):
        i = pl.program_id(0)

        @pl.when(i == 0)
        def _():
            dsh_ref[...] = jnp.zeros((1, D), F32)
            dsc_ref[...] = jnp.zeros((1, D), F32)
            dg_ref[...] = jnp.zeros((1, D), F32)

        xv = x_ref[...]
        dhv = dh_ref[...]
        gv = g_ref[...]
        mod = 1.0 + sc_ref[...]
        r = lax.rsqrt(jnp.mean(xv * xv, axis=-1, keepdims=True) + EPS)
        xh = xv * r
        dsh_ref[...] += jnp.sum(dhv, axis=0, keepdims=True)
        dsc_ref[...] += jnp.sum(dhv * (xh * gv), axis=0, keepdims=True)
        dg_ref[...] += jnp.sum(dhv * mod * xh, axis=0, keepdims=True)
        u = dhv * mod * gv
        dx_ref[...] = dxn_ref[...] + r * u - xv * (r * r * r) * jnp.mean(u * xv, axis=-1, keepdims=True)

    tile = pl.BlockSpec((tr, D), lambda i: (i, 0))
    outs = _call(
        body, name=name, grid=(S // tr,),
        in_specs=[tile, tile, tile, _rowvec(), _rowvec()],
        out_specs=[tile, _rowvec(), _rowvec(), _rowvec()],
        out_shape=[jax.ShapeDtypeStruct((S, D), F32)] + [jax.ShapeDtypeStruct((1, D), F32)] * 3,
        sem=("arbitrary",), args=(dh, dxn, x, g, scale), carry=carry)
    return outs[:4], outs[4:]


def _layer_tail_fwd(proj, a_in, b_in, x, w_po, w_ho, w_out, gate, g, name, target=None, carry=None):
    S = proj.shape[0]
    tr = _row_tile(S, 256)
    nsh, _, wsh = w_po.shape
    n_in = 10 + (target is not None)

    def body(*refs):
        (mgp_ref, mgh_ref, a_ref, b_ref, x_ref, wpo_ref, who_ref, wout_ref, gate_ref, g_ref) = refs[:10]
        bra_ref, brb_ref, mt_ref, y_ref, xn_ref = refs[n_in:n_in + 5]
        av = a_ref[...]
        bra = jnp.concatenate([jnp.dot(av, wpo_ref[j], preferred_element_type=F32) for j in range(nsh)], axis=1)
        brb = jnp.dot(b_ref[...], who_ref[...], preferred_element_type=F32)
        mv = _sig(mgp_ref[...].astype(F32)) * bra + _sig(mgh_ref[...].astype(F32)) * brb
        bra_ref[...] = bra.astype(BF16)
        brb_ref[...] = brb.astype(BF16)
        mt_ref[...] = mv.T.astype(BF16)
        yv = jnp.dot(mv.astype(BF16), wout_ref[...], preferred_element_type=F32)
        y_ref[...] = yv
        r = lax.rsqrt(jnp.mean(yv * yv, axis=-1, keepdims=True) + EPS)
        xn = x_ref[...] + gate_ref[...] * ((yv * r) * g_ref[...])
        if target is None:
            xn_ref[...] = xn
        else:
            t_ref, l_ref = refs[10], refs[n_in + 5]

            @pl.when(pl.program_id(0) == 0)
            def _():
                l_ref[...] = jnp.zeros((8, 128), F32)

            err = xn - t_ref[...]
            xn_ref[...] = err * (1.0 / D)
            l_ref[...] += 0.5 * jnp.sum(jnp.mean(err * err, axis=-1, keepdims=True))

    tile = pl.BlockSpec((tr, D), lambda i: (i, 0))
    whole = lambda t: pl.BlockSpec(t.shape, lambda i: (0,) * t.ndim)
    last = target is not None
    outs = _call(
        body, name=name, grid=(S // tr,),
        in_specs=[pl.BlockSpec((tr, D), lambda i: (i, MGP_BLK)), pl.BlockSpec((tr, D), lambda i: (i, MGH_BLK)),
                  pl.BlockSpec((tr, POOL_W), lambda i: (i, 0)), tile, tile, whole(w_po), whole(w_ho),
                  whole(w_out), _rowvec(), _rowvec()] + [tile] * last,
        out_specs=[tile, tile, pl.BlockSpec((D, tr), lambda i: (0, i)), tile, tile]
        + [pl.BlockSpec((8, 128), lambda i: (0, 0))] * last,
        out_shape=[jax.ShapeDtypeStruct((S, D), BF16), jax.ShapeDtypeStruct((S, D), BF16),
                   jax.ShapeDtypeStruct((D, S), BF16), jax.ShapeDtypeStruct((S, D), F32),
                   jax.ShapeDtypeStruct((S, D), F32)] + [jax.ShapeDtypeStruct((8, 128), F32)] * last,
        sem=("arbitrary",) if last else ("parallel",),
        args=(proj, proj, a_in, b_in, x, w_po, w_ho, w_out, gate, g) + ((target,) if last else ()), carry=carry)
    return outs[:5 + last], outs[5 + last:]


def _layer_head_bwd(dxn, y, proj, br_a, br_b, w_po, w_ho, w_out, gate, g, name, carry=None):
    S = y.shape[0]
    tr = _row_tile(S, 256)
    nsh, _, wsh = w_po.shape

    def body(dxn_ref, y_ref, mgp_ref, mgh_ref, bra_ref, brb_ref, wpo_ref, who_ref, wout_ref, gate_ref, g_ref,
             dy_ref, dba_ref, dbb_ref, dproj_ref, dain_ref, dbin_ref, dgate_ref, dg_ref, dmgh_s):
        i = pl.program_id(0)
        j = pl.program_id(1)

        @pl.when((i == 0) & (j == 0))
        def _():
            dgate_ref[...] = jnp.zeros((1, D), F32)
            dg_ref[...] = jnp.zeros((1, D), F32)

        @pl.when(j == 1)
        def _():
            dproj_ref[...] = dmgh_s[...]

        @pl.when(j == 0)
        def _():
            everything(dxn_ref, y_ref, mgp_ref, mgh_ref, bra_ref, brb_ref, wpo_ref, who_ref, wout_ref, gate_ref,
                       g_ref, dy_ref, dba_ref, dbb_ref, dproj_ref, dain_ref, dbin_ref, dgate_ref, dg_ref, dmgh_s)

    def everything(dxn_ref, y_ref, mgp_ref, mgh_ref, bra_ref, brb_ref, wpo_ref, who_ref, wout_ref, gate_ref, g_ref,
                   dy_ref, dba_ref, dbb_ref, dproj_ref, dain_ref, dbin_ref, dgate_ref, dg_ref, dmgh_s):
        yv = y_ref[...]
        dv = dxn_ref[...]
        gv = g_ref[...]
        gt = gate_ref[...]
        r = lax.rsqrt(jnp.mean(yv * yv, axis=-1, keepdims=True) + EPS)
        yh = yv * r
        dgate_ref[...] += jnp.sum(dv * (yh * gv), axis=0, keepdims=True)
        dg_ref[...] += jnp.sum(dv * gt * yh, axis=0, keepdims=True)
        u = dv * gt * gv
        dy = (r * u - yv * (r * r * r) * jnp.mean(u * yv, axis=-1, keepdims=True)).astype(BF16)
        dy_ref[...] = dy
        dm = _dot_nt(dy, wout_ref[...])
        sp = _sig(mgp_ref[...].astype(F32))
        sh = _sig(mgh_ref[...].astype(F32))
        dba = (dm * sp).astype(BF16)
        dbb = (dm * sh).astype(BF16)
        dba_ref[...] = dba
        dbb_ref[...] = dbb
        dproj_ref[...] = (dm * bra_ref[...].astype(F32) * sp * (1.0 - sp)).astype(BF16)
        dmgh_s[...] = (dm * brb_ref[...].astype(F32) * sh * (1.0 - sh)).astype(BF16)
        dain = _dot_nt(dba[:, 0:wsh], wpo_ref[0])
        for k in range(1, nsh):
            dain = dain + _dot_nt(dba[:, k * wsh:(k + 1) * wsh], wpo_ref[k])
        dain_ref[...] = dain
        dbin_ref[...] = _dot_nt(dbb, who_ref[...])

    tile = pl.BlockSpec((tr, D), lambda i, j: (i, 0))
    whole = lambda t: pl.BlockSpec(t.shape, lambda i, j: (0,) * t.ndim)
    vec = pl.BlockSpec((1, D), lambda i, j: (0, 0))
    ahead = lambda i, j: jnp.minimum(i + j, S // tr - 1)
    tile_in = pl.BlockSpec((tr, D), lambda i, j: (ahead(i, j), 0))
    outs = _call(
        body, name=name, grid=(S // tr, 2),
        in_specs=[tile_in, tile_in, pl.BlockSpec((tr, D), lambda i, j: (ahead(i, j), MGP_BLK)),
                  pl.BlockSpec((tr, D), lambda i, j: (ahead(i, j), MGH_BLK)), tile_in, tile_in, whole(w_po),
                  whole(w_ho), whole(w_out), vec, vec],
        out_specs=[tile, tile, tile, pl.BlockSpec((tr, D), lambda i, j: (i, MGP_BLK + j)),
                   pl.BlockSpec((tr, POOL_W), lambda i, j: (i, 0)), tile, vec, vec],
        out_shape=[jax.ShapeDtypeStruct((S, D), BF16)] * 3
        + [jax.ShapeDtypeStruct((S, IN_W), BF16), jax.ShapeDtypeStruct((S, POOL_W), F32),
           jax.ShapeDtypeStruct((S, D), F32), jax.ShapeDtypeStruct((1, D), F32), jax.ShapeDtypeStruct((1, D), F32)],
        scratch_shapes=[pltpu.VMEM((tr, D), BF16)], sem=("arbitrary", "arbitrary"),
        args=(dxn, y, proj, proj, br_a, br_b, w_po, w_ho, w_out, gate, g), carry=carry)
    return outs[:8], outs[8:]


def _pool_pieces(u, g, S):
    rowi = lax.broadcasted_iota(jnp.int32, (S, 1), 0)

    def down(z, k):
        return jnp.where(rowi >= k, pltpu.roll(z, k, axis=0), 0.0)

    s2 = u + down(u, 1)
    s4 = s2 + down(s2, 2)
    s8 = s4 + down(s4, 4)
    s16 = s8 + down(s8, 8)
    win = jnp.where(g == 0, s2, jnp.where(g == 1, s4, jnp.where(g == 2, s8, s16)))
    w = jnp.where(g == 0, 2, jnp.where(g == 1, 4, jnp.where(g == 2, 8, 16)))
    count = jnp.minimum(rowi + 1, w).astype(F32)
    return win / count - u, count, rowi


def _pool_fwd(proj, pw, pscale, name):
    S = proj.shape[0]

    def body(pv_ref, pg_ref, pw_ref, sc_ref, a_ref, at_ref):
        g = pl.program_id(0)
        pooled, _, _ = _pool_pieces(pv_ref[...].astype(F32), g, S)
        pm = jnp.dot(pooled.astype(BF16), pw_ref[...].astype(BF16), preferred_element_type=F32)
        pgv = pg_ref[...].astype(F32)
        av = pm * sc_ref[...] * (pgv * _sig(pgv))
        a_ref[...] = av.astype(BF16)
        at_ref[...] = av.T.astype(BF16)

    return pl.pallas_call(
        body, name=name, grid=(GROUPS,),
        in_specs=[pl.BlockSpec((S, 128), lambda g: (0, PV0 + g)), pl.BlockSpec((S, 128), lambda g: (0, PG0 + g)),
                  pl.BlockSpec((None, 128, 128), lambda g: (g, 0, 0)), pl.BlockSpec((1, 128), lambda g: (0, g))],
        out_specs=[pl.BlockSpec((S, 128), lambda g: (0, g)), pl.BlockSpec((128, S), lambda g: (g, 0))],
        out_shape=[jax.ShapeDtypeStruct((S, POOL_W), BF16), jax.ShapeDtypeStruct((POOL_W, S), BF16)],
        compiler_params=_cp(("parallel",)),
    )(proj, proj, pw, pscale)


def _pool_bwd(da, proj, pw, pscale, dproj, name):
    S = proj.shape[0]

    def body(da_ref, pv_ref, pg_ref, pw_ref, sc_ref, dproj_in, dproj_ref, dpw_ref, dsc_ref, dpg_s):
        @pl.when(pl.program_id(1) == 1)
        def _():
            dproj_ref[...] = dpg_s[...]

        @pl.when(pl.program_id(1) == 0)
        def _():
            group(da_ref, pv_ref, pg_ref, pw_ref, sc_ref, dproj_ref, dpg_s, dpw_ref, dsc_ref)

    def group(da_ref, pv_ref, pg_ref, pw_ref, sc_ref, dpv_ref, dpg_ref, dpw_ref, dsc_ref):
        g = pl.program_id(0)
        pooled, count, rowi = _pool_pieces(pv_ref[...].astype(F32), g, S)
        pwb = pw_ref[...].astype(BF16)
        pm = jnp.dot(pooled.astype(BF16), pwb, preferred_element_type=F32)
        scv = sc_ref[...]
        pgv = pg_ref[...].astype(F32)
        sg = _sig(pgv)
        dav = da_ref[...]
        d_ps = dav * (pgv * sg)
        dpg_ref[...] = (dav * (pm * scv) * _dsilu(pgv, sg)).astype(BF16)
        dsc_ref[...] = jnp.sum(d_ps * pm, axis=0, keepdims=True)
        d_pm = (d_ps * scv).astype(BF16)
        dpw_ref[...] = lax.dot_general(pooled.astype(BF16), d_pm, (((0,), (0,)), ((), ())),
                                       preferred_element_type=F32)
        d_pooled = lax.dot_general(d_pm, pwb, (((1,), (1,)), ((), ())), preferred_element_type=F32)
        z = d_pooled / count

        def up(v, k):
            return jnp.where(rowi < S - k, pltpu.roll(v, S - k, axis=0), 0.0)

        t2 = z + up(z, 1)
        t4 = t2 + up(t2, 2)
        t8 = t4 + up(t4, 4)
        t16 = t8 + up(t8, 8)
        adj = jnp.where(g == 0, t2, jnp.where(g == 1, t4, jnp.where(g == 2, t8, t16)))
        dpv_ref[...] = (adj - d_pooled).astype(BF16)

    col = lambda g, j: (0, g)
    ahead = lambda g, j: jnp.minimum(g + j, GROUPS - 1)
    return pl.pallas_call(
        body, name=name, grid=(GROUPS, 2),
        in_specs=[pl.BlockSpec((S, 128), lambda g, j: (0, ahead(g, j))),
                  pl.BlockSpec((S, 128), lambda g, j: (0, PV0 + ahead(g, j))),
                  pl.BlockSpec((S, 128), lambda g, j: (0, PG0 + ahead(g, j))),
                  pl.BlockSpec((None, 128, 128), lambda g, j: (ahead(g, j), 0, 0)),
                  pl.BlockSpec((1, 128), lambda g, j: (0, ahead(g, j))), ANY],
        out_specs=[pl.BlockSpec((S, 128), lambda g, j: (0, PV0 + g + (PG0 - PV0) * j)),
                   pl.BlockSpec((None, 128, 128), lambda g, j: (g, 0, 0)), pl.BlockSpec((1, 128), col)],
        out_shape=[jax.ShapeDtypeStruct(dproj.shape, dproj.dtype),
                   jax.ShapeDtypeStruct((GROUPS, 128, 128), F32), jax.ShapeDtypeStruct((1, POOL_W), F32)],
        scratch_shapes=[pltpu.VMEM((S, 128), BF16)], input_output_aliases={5: 0},
        compiler_params=_cp(("arbitrary", "arbitrary")),
    )(da, proj, proj, pw, pscale, dproj)


SCAN_SHIFTS = tuple(1 << b for b in range(CH.bit_length() - 1))


def _chunk_cumsum(z, rowi):
    for sh in SCAN_SHIFTS:
        z = z + jnp.where(rowi >= sh, pltpu.roll(z, sh, axis=0), 0.0)
    return z


def _chunk_rev_cumsum(z, rowi):
    for sh in SCAN_SHIFTS:
        z = z + jnp.where(rowi < CH - sh, pltpu.roll(z, CH - sh, axis=0), 0.0)
    return z


def _dot_nn(a, b):
    return jnp.dot(a.astype(BF16), b.astype(BF16), preferred_element_type=F32)


def _dot_nt(a, b):
    return lax.dot_general(a.astype(BF16), b.astype(BF16), (((1,), (1,)), ((), ())), preferred_element_type=F32)


def _dot_tn(a, b):
    return lax.dot_general(a.astype(BF16), b.astype(BF16), (((0,), (0,)), ((), ())), preferred_element_type=F32)


def _gates(hq, hf, lbv):
    hq, hf = hq.astype(F32), hf.astype(F32)
    sq = _sig(hq)
    sf = _sig(hf)
    f = lbv + (1.0 - lbv) * sf
    fc = jnp.maximum(f, 1e-30)
    return hq * sq, sq, sf, f, fc, jnp.log(fc)


DECAY_CAP = 60.0


def _block_ref(c_ref, i, sb):
    if i == 0:
        return jnp.zeros((1, HD), F32)
    return c_ref[sb * i - 1:sb * i, :]


def _block_decay(c_ref, sb):
    spans = [_block_ref(c_ref, i, sb) - c_ref[sb * (i + 1) - 1:sb * (i + 1), :] for i in range(CH // sb)]
    return functools.reduce(jnp.maximum, spans)


def _pair_factors(q_ref, k, c_ref, first, cap, round_bf16, sb):
    nb = CH // sb
    c = c_ref[...]
    zero = jnp.zeros((sb, HD), F32)
    q_groups, k_groups, eqs, eks = [], [], [], []
    for i in range(first, nb):
        blk = slice(sb * i, sb * (i + 1))
        r_i = _block_ref(c_ref, i, sb)
        eq = jnp.exp(jnp.minimum(c_ref[blk, :] - r_i, 0.0))
        ek = jnp.exp(jnp.minimum(r_i - c, cap))
        qi, kei = q_ref[blk, :] * eq, k * ek
        if round_bf16:
            qi, kei = qi.astype(BF16).astype(F32), kei.astype(BF16).astype(F32)
        q_groups.append(jnp.concatenate([zero] * i + [qi] + [zero] * (nb - 1 - i), axis=0))
        k_groups.append(kei)
        eqs.append(eq)
        eks.append(ek)
    return jnp.concatenate(q_groups, axis=1), jnp.concatenate(k_groups, axis=1), eqs, eks


def _pair_mask(rowi, coli, strict, sb):
    return (coli < jnp.bitwise_and(rowi, -sb)) if strict else (coli <= rowi)


def _hgrn_fwd(proj, lb, gn, name, carry=None):
    S = proj.shape[0]
    nch = S // CH
    W = NH * HD

    def body(hq_ref, hf_ref, hi_ref, hg_ref, lb_ref, gn_ref, bin_ref, bint_ref, oraw_ref, st_ref, mild_ref,
             cum_ref, q_s, k_s, c_s, v_s, o_s, state_s, qf_s, kf_s, cf_s):
        state_s[...] = jnp.zeros((NH, HD, HD), F32)
        rowi = lax.broadcasted_iota(jnp.int32, (CH, 1), 0)
        coli = lax.broadcasted_iota(jnp.int32, (1, CH), 1)
        sbi = lax.broadcasted_iota(jnp.int32, (SB, 1), 0)
        gnv = gn_ref[...]

        def gates_pass(n, worst):
            wide, narrow = worst
            rows = pl.ds(pl.multiple_of(n * CH, CH), CH)
            for hh in range(NH):
                lanes = slice(hh * HD, (hh + 1) * HD)
                q, _, _, f, _, logf = _gates(hq_ref[rows, lanes], hf_ref[rows, lanes], lb_ref[:, lanes])
                c = _chunk_cumsum(logf, rowi)
                qf_s[hh, rows, :] = q
                kf_s[hh, rows, :] = 1.0 - f
                cf_s[hh, rows, :] = c
                cum_ref[rows, lanes] = c
                c_s[hh] = c
                wide = jnp.maximum(wide, _block_decay(c_s.at[hh], SB_WIDE))
                narrow = jnp.maximum(narrow, _block_decay(c_s.at[hh], SB))
            return wide, narrow

        def between_chunks(hh, n, rows):
            lanes = slice(hh * HD, (hh + 1) * HD)
            q = qf_s[hh, rows, :]
            k = kf_s[hh, rows, :]
            c = cf_s[hh, rows, :]
            v = hi_ref[rows, lanes].astype(F32)
            q_s[hh] = q
            k_s[hh] = k
            c_s[hh] = c
            v_s[hh] = v
            st = state_s[hh]
            st_ref[hh, n] = st.astype(BF16)
            o_s[hh] = _dot_nt(q * jnp.exp(c), st)
            last = c_s[hh, CH - 1:CH, :]
            state_s[hh] = st * jnp.exp(last) + _dot_tn(v, k * jnp.exp(last - c))

        def pairs_matmul(hh, first, cap, strict, sb):
            qx, kc, _, _ = _pair_factors(q_s.at[hh], k_s[hh], c_s.at[hh], first, cap, False, sb)
            a = jnp.where(_pair_mask(rowi, coli, strict, sb), _dot_nt(qx, kc), 0.0)
            o_s[hh] += _dot_nn(a, v_s[hh])

        def within_chunk_matmul(sb):
            return lambda hh: pairs_matmul(hh, 0, DECAY_CAP, False, sb)

        def within_chunk_exact(hh):
            pairs_matmul(hh, 1, 0.0, True, SB)
            for i in range(CH // SB):
                blk = slice(SB * i, SB * (i + 1))
                qb = q_s[hh, blk, :]
                cb = c_s[hh, blk, :]
                acc = jnp.zeros((SB, HD), F32)
                for s in range(SB):
                    row = SB * i + s
                    w = jnp.exp(jnp.minimum(cb - c_s[hh, row:row + 1, :], 0.0))
                    a_col = jnp.sum(qb * k_s[hh, row:row + 1, :] * w, axis=-1, keepdims=True)
                    acc = acc + jnp.where(sbi >= s, a_col, 0.0) * v_s[hh, row:row + 1, :]
                o_s[hh, blk, :] += acc

        def norm_and_gate(hh, rows):
            lanes = slice(hh * HD, (hh + 1) * HD)
            ov = o_s[hh]
            oraw_ref[rows, lanes] = ov
            r = lax.rsqrt(jnp.mean(ov * ov, axis=-1, keepdims=True) + EPS)
            hg = hg_ref[rows, lanes].astype(F32)
            bin_ref[rows, lanes] = ((ov * r) * gnv * (hg * _sig(hg))).astype(BF16)

        def chunk_with(within_chunk):
            def chunk(n, carry):
                rows = pl.ds(pl.multiple_of(n * CH, CH), CH)
                for hh in range(NH):
                    between_chunks(hh, n, rows)
                for hh in range(NH):
                    within_chunk(hh)
                for hh in range(NH):
                    norm_and_gate(hh, rows)
                return carry
            return chunk

        none = jnp.zeros((1, HD), F32)
        wide, narrow = lax.fori_loop(0, nch, gates_pass, (none, none))
        tier = jnp.where(jnp.max(wide) <= DECAY_CAP, 2.0, jnp.where(jnp.max(narrow) <= DECAY_CAP, 1.0, 0.0))
        mild_ref[...] = jnp.broadcast_to(tier, (8, HD))

        @pl.when(tier == 2.0)
        def _():
            lax.fori_loop(0, nch, chunk_with(within_chunk_matmul(SB_WIDE)), 0, unroll=4)

        @pl.when(tier == 1.0)
        def _():
            lax.fori_loop(0, nch, chunk_with(within_chunk_matmul(SB)), 0, unroll=2)

        @pl.when(tier == 0.0)
        def _():
            lax.fori_loop(0, nch, chunk_with(within_chunk_exact), 0)

        bint_ref[...] = bin_ref[...].astype(F32).T.astype(BF16)

    col = lambda off: pl.BlockSpec((S, W), lambda h: (0, off // NH + h))
    head = pl.BlockSpec((S, W), lambda h: (0, h))
    outs = _call(
        body, name=name, grid=(HEADS // NH,),
        in_specs=[col(HQ0), col(HF0), col(HI0), col(HG0), pl.BlockSpec((1, W), lambda h: (0, h)),
                  pl.BlockSpec((1, HD), lambda h: (0, 0))],
        out_specs=[head, pl.BlockSpec((W, S), lambda h: (h, 0)), head,
                   pl.BlockSpec((NH, nch, HD, HD), lambda h: (h, 0, 0, 0)),
                   pl.BlockSpec((8, HD), lambda h: (h, 0)), head],
        out_shape=[jax.ShapeDtypeStruct((S, D), BF16), jax.ShapeDtypeStruct((D, S), BF16),
                   jax.ShapeDtypeStruct((S, D), F32), jax.ShapeDtypeStruct((HEADS, nch, HD, HD), BF16),
                   jax.ShapeDtypeStruct((8 * HEADS // NH, HD), F32), jax.ShapeDtypeStruct((S, D), F32)],
        scratch_shapes=[pltpu.VMEM((NH, CH, HD), F32)] * 5 + [pltpu.VMEM((NH, HD, HD), F32)]
        + [pltpu.VMEM((NH, S, HD), F32)] * 3,
        sem=("parallel",), args=(proj, proj, proj, proj, lb, gn), carry=carry)
    return outs[:6], outs[6:]


def _hgrn_bwd(dbin, proj, oraw, states, mild, cum, lb, gn, dproj, name, carry=None):
    S = proj.shape[0]
    nch = S // CH
    W = NH * HD
    n_in = 12

    def body(*refs):
        ins, (dproj_ref, dlb_ref, dgn_ref) = refs[:n_in - 1], refs[n_in:n_in + 3]
        scratch, later = refs[n_in + 3:-3], refs[-3:]
        seg = pl.program_id(1)

        @pl.when(seg == 0)
        def _():
            heads(*ins, dproj_ref, *later, dlb_ref, dgn_ref, *scratch)

        for s, kept in enumerate(later):
            @pl.when(seg == s + 1)
            def _(kept=kept):
                dproj_ref[...] = kept[...]

    def heads(db_ref, hq_ref, hf_ref, hi_ref, hg_ref, or_ref, st_ref, mild_ref, cum_ref, lb_ref, gn_ref,
              dq_ref, df_ref, di_ref, dg_ref, dlb_ref, dgn_ref,
              q_s, k_s, c_s, v_s, do_s, dq_s, dk_s, dv_s, dc_s, dqd_s, dkd_s, f_s, sf_s, sq_s, dl_s, dst_s,
              dlb_s, dgn_s):
        dst_s[...] = jnp.zeros((NH, HD, HD), F32)
        dlb_s[...] = jnp.zeros((1, W), F32)
        dgn_s[...] = jnp.zeros((1, HD), F32)
        rowi = lax.broadcasted_iota(jnp.int32, (CH, 1), 0)
        coli = lax.broadcasted_iota(jnp.int32, (1, CH), 1)
        sbi = lax.broadcasted_iota(jnp.int32, (SB, 1), 0)
        gnv = gn_ref[...]
        def between_chunks(hh, n, rows):
            lanes = slice(hh * HD, (hh + 1) * HD)
            lbv = lb_ref[:, lanes]
            hq = hq_ref[rows, lanes].astype(F32)
            sq = _sig(hq)
            sf = _sig(hf_ref[rows, lanes].astype(F32))
            f = lbv + (1.0 - lbv) * sf
            q = hq * sq
            k = 1.0 - f
            f_s[hh] = f
            sf_s[hh] = sf
            sq_s[hh] = sq
            v = hi_ref[rows, lanes].astype(F32)
            c = cum_ref[rows, lanes]
            ov = or_ref[rows, lanes]
            hg = hg_ref[rows, lanes].astype(F32)
            sg = _sig(hg)
            r = lax.rsqrt(jnp.mean(ov * ov, axis=-1, keepdims=True) + EPS)
            dbv = db_ref[rows, lanes]
            d_on = dbv * (hg * sg)
            dg_ref[rows, lanes] = (dbv * ((ov * r) * gnv) * _dsilu(hg, sg)).astype(BF16)
            dgn_s[...] += jnp.sum(d_on * (ov * r), axis=0, keepdims=True)
            u = d_on * gnv
            do = r * u - ov * (r * r * r) * jnp.mean(u * ov, axis=-1, keepdims=True)
            q_s[hh] = q
            k_s[hh] = k
            c_s[hh] = c
            v_s[hh] = v
            do_s[hh] = do
            st = st_ref[hh, n].astype(F32)
            dst = dst_s[hh]
            ec = jnp.exp(c)
            last = c_s[hh, CH - 1:CH, :]
            el = jnp.exp(last - c)
            elast = jnp.exp(last)
            dq = _dot_nn(do, st) * ec
            dk = _dot_nn(v, dst) * el
            dq_s[hh] = dq
            dk_s[hh] = dk
            dv_s[hh] = _dot_nt(k * el, dst)
            dc_s[hh] = q * dq - k * dk
            dl_s[hh] = (jnp.sum(k * dk, axis=0, keepdims=True)
                        + elast * jnp.sum(st * dst, axis=0, keepdims=True))
            dst_s[hh] = dst * elast + _dot_tn(do, q * ec)

        def pairs_matmul(hh, first, cap, strict, sb):
            do = do_s[hh]
            qx, kc, eqs, eks = _pair_factors(q_s.at[hh], k_s[hh], c_s.at[hh], first, cap, True, sb)
            mask = _pair_mask(rowi, coli, strict, sb)
            a = jnp.where(mask, _dot_nt(qx, kc), 0.0)
            d_a = jnp.where(mask, _dot_nt(do, v_s[hh]).astype(BF16).astype(F32), 0.0)
            dqx = _dot_nn(d_a, kc)
            dkc = _dot_tn(d_a, qx)
            dv_s[hh] += _dot_tn(a, do)
            dk, dcum = dk_s[hh], dc_s[hh]
            dq_slabs = [jnp.zeros((sb, HD), F32)] * first
            dc_slabs = [jnp.zeros((sb, HD), F32)] * first
            for g, (eq, ek) in enumerate(zip(eqs, eks)):
                rows = slice(sb * (first + g), sb * (first + g + 1))
                cols = slice(HD * g, HD * (g + 1))
                dq_i = dqx[rows, cols]
                dk_i = dkc[:, cols]
                dq_slabs.append(dq_i * eq)
                dc_slabs.append(qx[rows, cols] * dq_i)
                dk = dk + dk_i * ek
                dcum = dcum - kc[:, cols] * dk_i
            dq_s[hh] += jnp.concatenate(dq_slabs, axis=0)
            dk_s[hh] = dk
            dc_s[hh] = dcum + jnp.concatenate(dc_slabs, axis=0)

        def pairs_exact(hh):
            dqd_s[hh] = jnp.zeros((CH, HD), F32)
            dkd_s[hh] = jnp.zeros((CH, HD), F32)
            for i in range(CH // SB):
                blk = slice(SB * i, SB * (i + 1))
                qb = q_s[hh, blk, :]
                cb = c_s[hh, blk, :]
                dob = do_s[hh, blk, :]
                dq_acc = jnp.zeros((SB, HD), F32)
                for s in range(SB):
                    row = SB * i + s
                    ks = k_s[hh, row:row + 1, :]
                    vs = v_s[hh, row:row + 1, :]
                    w = jnp.exp(jnp.minimum(cb - c_s[hh, row:row + 1, :], 0.0))
                    live = sbi >= s
                    a_col = jnp.where(live, jnp.sum(qb * ks * w, axis=-1, keepdims=True), 0.0)
                    da_col = jnp.where(live, jnp.sum(dob * vs, axis=-1, keepdims=True), 0.0)
                    dq_acc = dq_acc + da_col * ks * w
                    dkd_s[hh, row:row + 1, :] += jnp.sum(da_col * qb * w, axis=0, keepdims=True)
                    dv_s[hh, row:row + 1, :] += jnp.sum(a_col * dob, axis=0, keepdims=True)
                dqd_s[hh, blk, :] += dq_acc
            dq_d = dqd_s[hh]
            dk_d = dkd_s[hh]
            dq_s[hh] += dq_d
            dk_s[hh] += dk_d
            dc_s[hh] += q_s[hh] * dq_d - k_s[hh] * dk_d

        def gate_grads(hh, rows):
            lanes = slice(hh * HD, (hh + 1) * HD)
            lbv = lb_ref[:, lanes]
            hq = hq_ref[rows, lanes].astype(F32)
            f, sf, sq = f_s[hh], sf_s[hh], sq_s[hh]
            dlogf = _chunk_rev_cumsum(dc_s[hh], rowi) + dl_s[hh]
            dfv = jnp.where(f > 1e-30, dlogf / jnp.maximum(f, 1e-30), 0.0) - dk_s[hh]
            dlb_s[:, lanes] += jnp.sum(dfv * (1.0 - sf), axis=0, keepdims=True)
            df_ref[rows, lanes] = (dfv * (1.0 - lbv) * sf * (1.0 - sf)).astype(BF16)
            dq_ref[rows, lanes] = (dq_s[hh] * _dsilu(hq, sq)).astype(BF16)
            di_ref[rows, lanes] = dv_s[hh].astype(BF16)

        def chunk_with(pairs):
            def chunk(j, carry):
                n = nch - 1 - j
                rows = pl.ds(pl.multiple_of(n * CH, CH), CH)
                for hh in range(NH):
                    between_chunks(hh, n, rows)
                for hh in range(NH):
                    pairs(hh)
                for hh in range(NH):
                    gate_grads(hh, rows)
                return carry
            return chunk

        def pairs_mild(sb):
            return lambda hh: pairs_matmul(hh, 0, DECAY_CAP, False, sb)

        def pairs_any(hh):
            pairs_matmul(hh, 1, 0.0, True, SB)
            pairs_exact(hh)

        tier = jnp.max(mild_ref[...])

        @pl.when(tier == 2.0)
        def _():
            lax.fori_loop(0, nch, chunk_with(pairs_mild(SB_WIDE)), 0, unroll=2)

        @pl.when(tier == 1.0)
        def _():
            lax.fori_loop(0, nch, chunk_with(pairs_mild(SB)), 0)

        @pl.when(tier == 0.0)
        def _():
            lax.fori_loop(0, nch, chunk_with(pairs_any), 0)

        dlb_ref[...] = dlb_s[...]
        dgn_ref[...] = jnp.broadcast_to(dgn_s[...], (8, HD))

    ahead = lambda h, s: jnp.minimum(h + jnp.minimum(s, 1), HEADS // NH - 1)
    col = lambda off: pl.BlockSpec((S, W), lambda h, s: (0, off // NH + ahead(h, s)))
    head_in = pl.BlockSpec((S, W), lambda h, s: (0, ahead(h, s)))
    vec_in = pl.BlockSpec((1, W), lambda h, s: (0, ahead(h, s)))
    vec = pl.BlockSpec((1, W), lambda h, s: (0, h))
    seg_w = (HF0 - HQ0) // NH
    outs = _call(
        body, name=name, grid=(HEADS // NH, 4),
        in_specs=[head_in, col(HQ0), col(HF0), col(HI0), col(HG0), head_in,
                  pl.BlockSpec((NH, nch, HD, HD), lambda h, s: (ahead(h, s), 0, 0, 0)),
                  pl.BlockSpec((8, HD), lambda h, s: (ahead(h, s), 0)), head_in, vec_in,
                  pl.BlockSpec((1, HD), lambda h, s: (0, 0)), ANY],
        out_specs=[pl.BlockSpec((S, W), lambda h, s: (0, HQ0 // NH + seg_w * s + h)), vec,
                   pl.BlockSpec((8, HD), lambda h, s: (h, 0))],
        out_shape=[jax.ShapeDtypeStruct(dproj.shape, dproj.dtype), jax.ShapeDtypeStruct((1, D), F32),
                   jax.ShapeDtypeStruct((8 * HEADS // NH, HD), F32)],
        scratch_shapes=[pltpu.VMEM((NH, CH, HD), F32)] * 14
        + [pltpu.VMEM((NH, 1, HD), F32), pltpu.VMEM((NH, HD, HD), F32), pltpu.VMEM((1, W), F32),
           pltpu.VMEM((1, HD), F32)] + [pltpu.VMEM((S, W), BF16)] * 3,
        sem=("arbitrary", "arbitrary"), aliases={n_in - 1: 0},
        args=(dbin, proj, proj, proj, proj, oraw, states, mild, cum, lb, gn, dproj), carry=carry)
    dproj, dlb, dgn = outs[:3]
    return (dproj, dlb, dgn.reshape(HEADS // NH, 8, HD)[:, 0, :]), outs[3:]


def _lower_bounds(l0, l1):
    m = jnp.maximum(l0, l1)
    e0 = jnp.exp(l0 - m)
    e1 = jnp.exp(l1 - m)
    tot = e0 + e1
    p0 = e0 / tot
    p1 = e1 / tot
    return jnp.clip(p0 - p0, 0.0, 1.0), jnp.clip((p0 + p1) - p0, 0.0, 1.0)


def _lb_fwd(logits):
    def body(l_ref, o_ref):
        lb0, lb1 = _lower_bounds(l_ref[0:1, :], l_ref[1:2, :])
        o_ref[0:1, :] = lb0
        o_ref[1:2, :] = lb1

    return pl.pallas_call(body, name="lb_fwd", out_shape=jax.ShapeDtypeStruct((2, D), F32))(logits)


def _lb_bwd(logits, dlb):
    def body(l_ref, d_ref, o_ref):
        _, vjp = jax.vjp(_lower_bounds, l_ref[0:1, :], l_ref[1:2, :])
        g0, g1 = vjp((d_ref[0:1, :], d_ref[1:2, :]))
        o_ref[0:1, :] = g0
        o_ref[1:2, :] = g1

    return pl.pallas_call(body, name="lb_bwd", out_shape=jax.ShapeDtypeStruct((2, D), F32))(logits, dlb)


ADA_PAD = 128


def _ada_fwd(c_pad, w_ada, b_sh):
    ns = w_ada.shape[2]

    def body(c_ref, w_ref, b_ref, o_ref):
        cv = c_ref[...]
        ca = (cv * _sig(cv)).astype(BF16)
        for l in range(2):
            res = jnp.dot(ca, w_ref[l].astype(BF16), preferred_element_type=F32)
            o_ref[:, l * ns:(l + 1) * ns] = res[0:NDEV, :] + b_ref[l:l + 1, :]

    return pl.pallas_call(body, name="ada_fwd", out_shape=jax.ShapeDtypeStruct((NDEV, 2 * ns), F32),
                          compiler_params=_cp())(c_pad, w_ada, b_sh)


def _ada_wgrad(c_pad_t, d_ada_sh):
    ns = d_ada_sh.shape[2]

    def body(c_ref, d_ref, o_ref):
        cv = c_ref[...]
        ca = (cv * _sig(cv)).astype(BF16)
        for l in range(2):
            o_ref[l] = jnp.dot(ca, d_ref[l].astype(BF16), preferred_element_type=F32)

    return pl.pallas_call(body, name="ada_wgrad", out_shape=jax.ShapeDtypeStruct((2, D, ns), F32),
                          compiler_params=_cp())(c_pad_t, d_ada_sh)


def _sum_devices(g):
    _, R, C = g.shape

    def body(g_ref, o_ref):
        acc = g_ref[0]
        for d in range(1, NDEV):
            acc = acc + g_ref[d]
        o_ref[...] = acc

    return pl.pallas_call(body, name="sum_devices", out_shape=jax.ShapeDtypeStruct((R, C), F32),
                          compiler_params=_cp())(g)


def _adamw(w, g, m, v, name, carry=None):
    R, C = w.shape
    tr = _row_tile(R, max(8, (1 << 19) // C))

    def body(w_ref, g_ref, m_ref, v_ref, d_ref, nm_ref, nv_ref):
        d_ref[...], nm_ref[...], nv_ref[...] = _adamw_update(w_ref[...], g_ref[...], m_ref[...], v_ref[...])

    tile = pl.BlockSpec((tr, C), lambda i: (i, 0))
    return _call(body, name=name, grid=(R // tr,), in_specs=[tile] * 4, out_specs=[tile] * 3,
                 out_shape=[jax.ShapeDtypeStruct((R, C), F32)] * 3, sem=("parallel",), args=(w, g, m, v),
                 carry=carry)


def _adamw_update(w, g, m, v):
    nm = B1 * m + (1.0 - B1) * g
    nv = B2 * v + (1.0 - B2) * (g * g)
    m_hat = nm / (1.0 - B1 ** STEP)
    v_hat = nv / (1.0 - B2 ** STEP)
    return -LR * (m_hat / (jnp.sqrt(v_hat) + AEPS) + WD * w), nm, nv


SMALL_PARTS = (("b_ada", 0, 6, D), ("g_pre", 8, 2, D), ("g_post", 16, 2, D), ("lb_logits", 24, 2, D),
               ("pool_w", 32, 128, D), ("pool_scale", 160, 1, D), ("hgrn_norm_g", 168, 1, 2 * HD))


def _adamw_small(g_small, g_lb_logits, wmv):
    n = len(SMALL_PARTS)

    def body(g_ref, glb_ref, *refs):
        ins, outs = refs[:3 * n], refs[3 * n:]
        for p, (key, row0, rows, width) in enumerate(SMALL_PARTS):
            gv = glb_ref[...] if key == "lb_logits" else g_ref[row0:row0 + rows, 0:width]
            res = _adamw_update(ins[3 * p][...], gv, ins[3 * p + 1][...], ins[3 * p + 2][...])
            for t in range(3):
                outs[3 * p + t][...] = res[t]

    flat = [t for triple in wmv for t in triple]
    outs = pl.pallas_call(body, name="adamw_small",
                          out_shape=[jax.ShapeDtypeStruct(t.shape, F32) for t in flat],
                          compiler_params=_cp())(g_small, g_lb_logits, *flat)
    return [outs[3 * p:3 * p + 3] for p in range(n)]


def _cast_to_slot(place, w, l, name):
    _, R, C = w.shape
    tr = _row_tile(R, max(8, (1 << 19) // C))

    def body(p_ref, w_ref, o_ref):
        o_ref[...] = w_ref[...].astype(BF16)

    return pl.pallas_call(
        body, name=name, out_shape=jax.ShapeDtypeStruct((NCHIP, R, C), BF16),
        grid_spec=pltpu.PrefetchScalarGridSpec(
            num_scalar_prefetch=1, grid=(R // tr,),
            in_specs=[pl.BlockSpec((None, tr, C), lambda i, p_ref: (l, i, 0))],
            out_specs=pl.BlockSpec((None, tr, C), lambda i, p_ref: (p_ref[0], i, 0))),
        compiler_params=_cp(("parallel",)),
    )(place, w)


def _pair_add(core, g, got, name):
    _, R, C = g.shape
    r2 = R // 2
    tr = _row_tile(r2, max(8, (1 << 19) // C))
    nt = r2 // tr

    def body(c_ref, a_ref, b_ref, o_ref):
        o_ref[...] = (a_ref[...].astype(F32) + b_ref[...].astype(F32)).astype(o_ref.dtype)

    return pl.pallas_call(
        body, name=name, out_shape=jax.ShapeDtypeStruct((NCHIP, r2, C), BF16),
        grid_spec=pltpu.PrefetchScalarGridSpec(
            num_scalar_prefetch=1, grid=(NCHIP, nt),
            in_specs=[pl.BlockSpec((None, tr, C), lambda j, i, c_ref: (j, c_ref[0] * nt + i, 0)),
                      pl.BlockSpec((None, tr, C), lambda j, i, c_ref: (j, i, 0))],
            out_specs=pl.BlockSpec((None, tr, C), lambda j, i, c_ref: (j, i, 0))),
        compiler_params=_cp(("parallel", "parallel")),
    )(core, g, got)


def _chip_sum(place, part, recv, layer, both, name):
    _, r2, C = part.shape
    tr = _row_tile(r2, max(8, (1 << 18) // C))
    nt = r2 // tr

    def body(p_ref, own_ref, r_ref, *rest):
        o_ref = rest[-1]
        me = p_ref[0]
        own = own_ref[...].astype(F32)
        acc = None
        for j in range(NCHIP):
            slot = jnp.minimum(jnp.where(j > me, j - 1, j), NCHIP - 2)
            term = jnp.where(me == j, own, r_ref[slot].astype(F32))
            acc = term if acc is None else acc + term
        o_ref[...] = acc

    args = (place, part, recv) if both is None else (place, part, recv, both)
    return pl.pallas_call(
        body, name=name, out_shape=jax.ShapeDtypeStruct((2, 2 * r2, C), F32),
        grid_spec=pltpu.PrefetchScalarGridSpec(
            num_scalar_prefetch=1, grid=(nt,),
            in_specs=[pl.BlockSpec((None, tr, C), lambda i, p_ref: (p_ref[0], i, 0)),
                      pl.BlockSpec((NCHIP - 1, tr, C), lambda i, p_ref: (0, i, 0))] + [ANY] * (len(args) - 3),
            out_specs=pl.BlockSpec((None, tr, C), lambda i, p_ref: (layer, p_ref[1] * nt + i, 0))),
        input_output_aliases={} if both is None else {3: 0},
        compiler_params=_cp(("parallel",)),
    )(*args)


def _place():
    x, y, c = lax.axis_index("x"), lax.axis_index("y"), lax.axis_index("c")
    chips = [(1 - x, y), (x, 1 - y), (1 - x, 1 - y)]
    return x, y, c, chips


def _gather_small(blk, name):
    m_per, n = blk.shape

    def body(x_ref, out_ref, send_sems, recv_sems, local_sem):
        x, y, c, chips = _place()
        me, sibling = (x, y, c), (x, y, 1 - c)

        def rows(px, py, pc):
            return out_ref.at[pl.ds((4 * px + 2 * py + pc) * m_per, m_per), :]

        def copy(k, block, to, src=None):
            return pltpu.make_async_remote_copy(
                src_ref=rows(*block) if src is None else src, dst_ref=rows(*block),
                send_sem=send_sems.at[k], recv_sem=recv_sems.at[k], device_id=to, device_id_type=MESH)

        mine = pltpu.make_async_copy(x_ref, rows(*me), local_sem)
        mine.start()
        first = [copy(0, me, sibling, src=x_ref)]
        first += [copy(1 + j, me, (*chip, c), src=x_ref) for j, chip in enumerate(chips)]
        for cp in first:
            cp.start()
        passed = [copy(4 + j, (*chip, c), sibling) for j, chip in enumerate(chips)]
        for j, chip in enumerate(chips):
            copy(1 + j, (*chip, c), me).wait_recv()
            passed[j].start()
        copy(0, sibling, me).wait_recv()
        for j, chip in enumerate(chips):
            copy(4 + j, (*chip, 1 - c), me).wait_recv()
        for cp in first + passed:
            cp.wait_send()
        mine.wait()

    return pl.pallas_call(
        body, name=name, out_shape=jax.ShapeDtypeStruct((NDEV * m_per, n), blk.dtype),
        in_specs=[pl.BlockSpec(memory_space=pltpu.VMEM)], out_specs=pl.BlockSpec(memory_space=pltpu.VMEM),
        scratch_shapes=[pltpu.SemaphoreType.DMA((7,)), pltpu.SemaphoreType.DMA((7,)), pltpu.SemaphoreType.DMA],
        compiler_params=_cp(),
    )(blk)


def _gather_rows_carry(blk):
    m_per, n = blk.shape

    def rows(ref, px, py, pc):
        return ref.at[pl.ds((4 * px + 2 * py + pc) * m_per, m_per), :]

    def copy(ins, outs, send_sems, recv_sems, k, block, to, own=False):
        return pltpu.make_async_remote_copy(
            src_ref=ins[0] if own else rows(outs[0], *block), dst_ref=rows(outs[0], *block),
            send_sem=send_sems.at[k], recv_sem=recv_sems.at[k], device_id=to, device_id_type=MESH)

    def mine(ins, outs, send_sems):
        x, y, c, _ = _place()
        return pltpu.make_async_copy(ins[0], rows(outs[0], x, y, c), send_sems.at[7])

    def start(ins, outs, send_sems, recv_sems):
        x, y, c, chips = _place()
        mine(ins, outs, send_sems).start()
        copy(ins, outs, send_sems, recv_sems, 0, (x, y, c), (x, y, 1 - c), own=True).start()
        for j, chip in enumerate(chips):
            copy(ins, outs, send_sems, recv_sems, 1 + j, (x, y, c), (*chip, c), own=True).start()

    def finish(ins, outs, send_sems, recv_sems):
        x, y, c, chips = _place()
        for j, chip in enumerate(chips):
            copy(ins, outs, send_sems, recv_sems, 1 + j, (*chip, c), (x, y, c)).wait_recv()
            copy(ins, outs, send_sems, recv_sems, 4 + j, (*chip, c), (x, y, 1 - c)).start()
        copy(ins, outs, send_sems, recv_sems, 0, (x, y, 1 - c), (x, y, c)).wait_recv()
        for j, chip in enumerate(chips):
            copy(ins, outs, send_sems, recv_sems, 4 + j, (*chip, 1 - c), (x, y, c)).wait_recv()
        copy(ins, outs, send_sems, recv_sems, 0, (x, y, c), (x, y, 1 - c), own=True).wait_send()
        for j, chip in enumerate(chips):
            copy(ins, outs, send_sems, recv_sems, 1 + j, (x, y, c), (*chip, c), own=True).wait_send()
            copy(ins, outs, send_sems, recv_sems, 4 + j, (*chip, c), (x, y, 1 - c)).wait_send()
        mine(ins, outs, send_sems).wait()

    return _Carry([blk], [jax.ShapeDtypeStruct((NDEV * m_per, n), blk.dtype)], {}, 8, start, finish)


def _gather_carry(shards, piece=(0, 1, 1)):
    n = len(shards)
    first, count, of = piece

    def rows(ref, half):
        r2 = ref.shape[1] // 2
        return pl.ds(half * r2 + first * (r2 // of), count * (r2 // of))

    def over_ici(outs, send_sems, recv_sems, a, j, chip_xy, slot):
        x, y, c, _ = _place()
        blk = outs[a].at[slot, rows(outs[a], c), :]
        return pltpu.make_async_remote_copy(
            src_ref=blk, dst_ref=blk, send_sem=send_sems.at[6 * a + j], recv_sem=recv_sems.at[6 * a + j],
            device_id=(*chip_xy, c), device_id_type=MESH)

    def over_d2d(outs, send_sems, recv_sems, a, j, slot, half):
        x, y, c, _ = _place()
        blk = outs[a].at[slot, rows(outs[a], half), :]
        return pltpu.make_async_remote_copy(
            src_ref=blk, dst_ref=blk, send_sem=send_sems.at[6 * a + 3 + j], recv_sem=recv_sems.at[6 * a + 3 + j],
            device_id=(x, y, 1 - c), device_id_type=MESH)

    def start(ins, outs, send_sems, recv_sems):
        x, y, c, chips = _place()
        for a in range(n):
            for j, chip_xy in enumerate(chips):
                over_ici(outs, send_sems, recv_sems, a, j, chip_xy, 2 * x + y).start()

    def finish(ins, outs, send_sems, recv_sems):
        x, y, c, chips = _place()
        for a in range(n):
            for j, (cx, cy) in enumerate(chips):
                over_ici(outs, send_sems, recv_sems, a, j, (cx, cy), 2 * cx + cy).wait_recv()
                over_d2d(outs, send_sems, recv_sems, a, j, 2 * cx + cy, c).start()
        for a in range(n):
            for j, (cx, cy) in enumerate(chips):
                over_d2d(outs, send_sems, recv_sems, a, j, 2 * cx + cy, 1 - c).wait_recv()
        for a in range(n):
            for j, (cx, cy) in enumerate(chips):
                over_ici(outs, send_sems, recv_sems, a, j, (cx, cy), 2 * x + y).wait_send()
                over_d2d(outs, send_sems, recv_sems, a, j, 2 * cx + cy, c).wait_send()

    return _Carry(shards, [jax.ShapeDtypeStruct(s.shape, s.dtype) for s in shards],
                  {a: a for a in range(n)}, 6 * n, start, finish)


def _rs_pair(grads, name):
    n = len(grads)

    def body(*refs):
        ins, gots = refs[:n], refs[n:2 * n]
        send_sems, recv_sems = refs[2 * n:]
        x, y, c, _ = _place()
        cps = []
        for a in range(n):
            r2 = ins[a].shape[1] // 2
            cp = pltpu.make_async_remote_copy(
                src_ref=ins[a].at[:, pl.ds((1 - c) * r2, r2), :], dst_ref=gots[a],
                send_sem=send_sems.at[a], recv_sem=recv_sems.at[a],
                device_id=(x, y, 1 - c), device_id_type=MESH)
            cp.start()
            cps.append(cp)
        for cp in cps:
            cp.wait()

    half = [jax.ShapeDtypeStruct((NCHIP, g.shape[1] // 2, g.shape[2]), g.dtype) for g in grads]
    return pl.pallas_call(
        body, name=name, out_shape=half, in_specs=[ANY] * n, out_specs=[ANY] * n,
        scratch_shapes=[pltpu.SemaphoreType.DMA((n,)), pltpu.SemaphoreType.DMA((n,))],
        compiler_params=_cp(),
    )(*grads)


def _pair_carry(grads):
    n = len(grads)

    def copy(ins, outs, send_sems, recv_sems, a):
        x, y, c, _ = _place()
        r2 = ins[a].shape[1] // 2
        return pltpu.make_async_remote_copy(
            src_ref=ins[a].at[:, pl.ds((1 - c) * r2, r2), :], dst_ref=outs[a],
            send_sem=send_sems.at[a], recv_sem=recv_sems.at[a],
            device_id=(x, y, 1 - c), device_id_type=MESH)

    def start(ins, outs, send_sems, recv_sems):
        for a in range(n):
            copy(ins, outs, send_sems, recv_sems, a).start()

    def finish(ins, outs, send_sems, recv_sems):
        for a in range(n):
            copy(ins, outs, send_sems, recv_sems, a).wait()

    half = [jax.ShapeDtypeStruct((NCHIP, g.shape[1] // 2, g.shape[2]), g.dtype) for g in grads]
    return _Carry(grads, half, {}, n, start, finish)


def _chips_carry(parts, piece=(0, 1, 1), into=None):
    n = len(parts)
    first, count, of = piece

    def rows(ref):
        step = ref.shape[1] // of
        return pl.ds(first * step, count * step)

    def send(ins, outs, send_sems, recv_sems, a, j, chip_xy):
        x, y, c, _ = _place()
        me, them = 2 * x + y, 2 * chip_xy[0] + chip_xy[1]
        return pltpu.make_async_remote_copy(
            src_ref=ins[a].at[them, rows(ins[a]), :],
            dst_ref=outs[a].at[me - (me > them).astype(jnp.int32), rows(outs[a]), :],
            send_sem=send_sems.at[3 * a + j], recv_sem=recv_sems.at[3 * a + j],
            device_id=(*chip_xy, c), device_id_type=MESH)

    def start(ins, outs, send_sems, recv_sems):
        _, _, _, chips = _place()
        for a in range(n):
            for j, chip_xy in enumerate(chips):
                send(ins, outs, send_sems, recv_sems, a, j, chip_xy).start()

    def finish(ins, outs, send_sems, recv_sems):
        x, y, c, chips = _place()
        me = 2 * x + y
        for a in range(n):
            for j, (cx, cy) in enumerate(chips):
                them = 2 * cx + cy
                blk = outs[a].at[them - (them > me).astype(jnp.int32), rows(outs[a]), :]
                pltpu.make_async_remote_copy(
                    src_ref=blk, dst_ref=blk, send_sem=send_sems.at[3 * a + j], recv_sem=recv_sems.at[3 * a + j],
                    device_id=(cx, cy, c), device_id_type=MESH).wait_recv()
        for a in range(n):
            for j, chip_xy in enumerate(chips):
                send(ins, outs, send_sems, recv_sems, a, j, chip_xy).wait_send()

    landing = [jax.ShapeDtypeStruct((NCHIP - 1,) + p.shape[1:], p.dtype) for p in parts]
    if into is None:
        return _Carry(parts, landing, {}, 3 * n, start, finish)
    return _Carry(list(parts) + list(into), landing, {n + a: a for a in range(n)}, 3 * n, start, finish)


def _rs_swap(fulls):
    n = len(fulls)

    def body(*refs):
        outs = refs[n:2 * n]
        send_sems, recv_sems = refs[2 * n:]
        x, y, c, _ = _place()
        cps = []
        for a in range(n):
            r2 = outs[a].shape[1] // 2
            mine = outs[a].at[:, pl.ds(c * r2, r2), :]
            cp = pltpu.make_async_remote_copy(
                src_ref=mine, dst_ref=mine, send_sem=send_sems.at[a], recv_sem=recv_sems.at[a],
                device_id=(x, y, 1 - c), device_id_type=MESH)
            cp.start()
            cps.append(cp)
        for a in range(n):
            r2 = outs[a].shape[1] // 2
            blk = outs[a].at[:, pl.ds((1 - c) * r2, r2), :]
            pltpu.make_async_remote_copy(
                src_ref=blk, dst_ref=blk, send_sem=send_sems.at[a], recv_sem=recv_sems.at[a],
                device_id=(x, y, 1 - c), device_id_type=MESH).wait_recv()
        for cp in cps:
            cp.wait_send()

    return pl.pallas_call(
        body, name="rs_swap", out_shape=[jax.ShapeDtypeStruct(f.shape, f.dtype) for f in fulls],
        in_specs=[ANY] * n, out_specs=[ANY] * n, input_output_aliases={a: a for a in range(n)},
        scratch_shapes=[pltpu.SemaphoreType.DMA((n,)), pltpu.SemaphoreType.DMA((n,))],
        compiler_params=_cp(),
    )(*fulls)


def _tail_weight_grads(merged_t, b_in_t, a_in_t, dy, dbr_b, dbr_a, name, tn=256):
    S = dy.shape[0]
    nn = D // tn

    def body(mt_ref, bt_ref, at_ref, dy_ref, db_ref, da_ref, go_ref, gh_ref, gp_ref):
        go_ref[...] = jnp.dot(mt_ref[...], dy_ref[...], preferred_element_type=F32).astype(BF16)
        gh_ref[...] = jnp.dot(bt_ref[...], db_ref[...], preferred_element_type=F32).astype(BF16)
        gp_ref[...] = jnp.dot(at_ref[...], da_ref[...], preferred_element_type=F32).astype(BF16)

    left = lambda rows: pl.BlockSpec((rows, S), lambda n: (0, 0))
    right = pl.BlockSpec((S, tn), lambda n: (0, n))
    out = pl.BlockSpec((D, tn), lambda n: (0, n))
    return pl.pallas_call(
        body, name=name, grid=(nn,), in_specs=[left(D), left(D), left(POOL_W), right, right, right],
        out_specs=[out, out, pl.BlockSpec((None, POOL_W, tn), lambda n: (n, 0, 0))],
        out_shape=[jax.ShapeDtypeStruct((D, D), BF16), jax.ShapeDtypeStruct((D, D), BF16),
                   jax.ShapeDtypeStruct((NCHIP, POOL_W, D // NCHIP), BF16)],
        compiler_params=_cp(("parallel",)),
    )(merged_t, b_in_t, a_in_t, dy, dbr_b, dbr_a)


class _GatherInProj:
    def __init__(self, slot, order):
        self.slot, self.order = slot, order


def _proj_with_gather(h, w_slot, order, name, tn=256):
    S, K = h.shape
    nsh, _, ns = w_slot.shape
    tps = ns // tn
    nt = nsh * tps
    r2 = K // 2

    def body(ord_ref, h_ref, w_in_ref, o_ref, w_ref, wbuf, tile_sems, send_sems, recv_sems):
        n = pl.program_id(0)
        x, y, c, chips = _place()

        def half(slot, which):
            return w_ref.at[slot, pl.ds(which * r2, r2), :]

        def over_ici(j, slot):
            blk = half(slot, c)
            return pltpu.make_async_remote_copy(src_ref=blk, dst_ref=blk, send_sem=send_sems.at[j],
                                                recv_sem=recv_sems.at[j], device_id=(*chips[j], c),
                                                device_id_type=MESH)

        def over_d2d(j, which):
            blk = half(2 * chips[j][0] + chips[j][1], which)
            return pltpu.make_async_remote_copy(src_ref=blk, dst_ref=blk, send_sem=send_sems.at[3 + j],
                                                recv_sem=recv_sems.at[3 + j], device_id=(x, y, 1 - c),
                                                device_id_type=MESH)

        def tile_copy(step, slot):
            shard = ord_ref[step // tps]
            return pltpu.make_async_copy(w_ref.at[shard, :, pl.ds((step % tps) * tn, tn)], wbuf.at[slot],
                                         tile_sems.at[slot])

        @pl.when(n == 0)
        def _():
            for j in range(3):
                over_ici(j, 2 * x + y).start()
            tile_copy(0, 0).start()

        for j in range(3):
            @pl.when(n == (j + 1) * tps - 1)
            def _(j=j):
                over_ici(j, 2 * chips[j][0] + chips[j][1]).wait_recv()
                over_d2d(j, c).start()
                over_d2d(j, 1 - c).wait_recv()

        @pl.when(n + 1 < nt)
        def _():
            tile_copy(n + 1, (n + 1) % 2).start()

        tile_copy(n, n % 2).wait()
        o_ref[...] = jnp.dot(h_ref[...], wbuf[n % 2], preferred_element_type=F32).astype(o_ref.dtype)

        @pl.when(n == nt - 1)
        def _():
            for j in range(3):
                over_ici(j, 2 * x + y).wait_send()
                over_d2d(j, c).wait_send()

    return pl.pallas_call(
        body, name=name,
        out_shape=[jax.ShapeDtypeStruct((S, nsh * ns), BF16), jax.ShapeDtypeStruct(w_slot.shape, w_slot.dtype)],
        grid_spec=pltpu.PrefetchScalarGridSpec(
            num_scalar_prefetch=1, grid=(nt,),
            in_specs=[pl.BlockSpec((S, K), lambda n, o_ref: (0, 0)), ANY],
            out_specs=[pl.BlockSpec((S, tn), lambda n, o_ref: (0, o_ref[n // tps] * tps + n % tps)), ANY],
            scratch_shapes=[pltpu.VMEM((2, K, tn), w_slot.dtype), pltpu.SemaphoreType.DMA((2,)),
                            pltpu.SemaphoreType.DMA((6,)), pltpu.SemaphoreType.DMA((6,))]),
        input_output_aliases={2: 1},
        compiler_params=_cp(("arbitrary",)),
    )(order, h, w_slot)


def _mm_ride(a, b, carry, **kw):
    if carry is None:
        return _mm(a, b, **kw), []
    return _mm(a, b, carry=carry, **kw)


def _layer_fwd(l, x, ada, w, small, ride, target=None):
    shift, scale, gate = ada[:, 0:D], ada[:, D:2 * D], ada[:, 2 * D:3 * D]
    carry, landed = ride("prenorm")
    (h, h_t), outs = _prenorm_fwd(x, small["g_pre"][l], scale, shift, f"prenorm_fwd{l}", carry)
    landed(outs)
    carry, landed = ride("proj")
    if isinstance(carry, _GatherInProj):
        proj, full = _proj_with_gather(h, carry.slot, carry.order, f"proj{l}")
        outs = [full]
    else:
        proj, outs = _mm_ride(h, w["w_in"][l], carry, name=f"proj{l}", b_mode="nn_sh", tm=2048, out_dtype=BF16)
    landed(outs)
    a_in, a_in_t = _pool_fwd(proj, small["pool_w"][l], small["pool_scale"][l], f"pool_fwd{l}")
    carry, landed = ride("hgrn")
    (b_in, b_in_t, o_raw, states, mild, cum), outs = _hgrn_fwd(proj, small["lb"][l], small["hgrn_norm_g"][l],
                                                              f"hgrn_fwd{l}", carry=carry)
    landed(outs)
    carry, landed = ride("tail")
    (br_a, br_b, merged_t, y, *x_new), outs = _layer_tail_fwd(
        proj, a_in, b_in, x, w["w_pool_o"][l], w["w_hgrn_o"][l].reshape(D, D), w["w_out"][l].reshape(D, D),
        gate, small["g_post"][l], f"tail_fwd{l}", target=target, carry=carry)
    landed(outs)
    saved = dict(x=x, h_t=h_t, proj=proj, a_in_t=a_in_t, b_in_t=b_in_t, o_raw=o_raw, states=states, mild=mild,
                 cum=cum,
                 br_a=br_a, br_b=br_b, merged_t=merged_t, y=y, scale=scale, gate=gate)
    return x_new, saved


def _layer_bwd(l, dxn, sv, w, small, ride):
    carry, landed = ride["head"](None)
    (dy, dbr_a, dbr_b, dproj, da_in, db_in, dgate, dg_post), outs = _layer_head_bwd(
        dxn, sv["y"], sv["proj"], sv["br_a"], sv["br_b"], w["w_pool_o"][l], w["w_hgrn_o"][l].reshape(D, D),
        w["w_out"][l].reshape(D, D), sv["gate"], small["g_post"][l], f"head_bwd{l}", carry)
    landed(outs)
    gw_out, gw_hgrn_o, gw_pool_o = _tail_weight_grads(sv["merged_t"], sv["b_in_t"], sv["a_in_t"], dy, dbr_b,
                                                      dbr_a, f"gw_tail{l}")
    big = dict(w_pool_o=gw_pool_o, w_hgrn_o=gw_hgrn_o.reshape(NCHIP, D // NCHIP, D),
               w_out=gw_out.reshape(NCHIP, D // NCHIP, D))
    carry, landed = ride["hgrn"](big)
    (dproj, dlb, dgn), outs = _hgrn_bwd(db_in, sv["proj"], sv["o_raw"], sv["states"], sv["mild"], sv["cum"],
                                        small["lb"][l], small["hgrn_norm_g"][l], dproj, f"hgrn_bwd{l}",
                                        carry=carry)
    landed(outs)
    dproj, dpw, dpsc = _pool_bwd(da_in, sv["proj"], small["pool_w"][l], small["pool_scale"][l], dproj,
                                 f"pool_bwd{l}")
    little = dict(dgate=dgate, g_post=dg_post, pool_w=dpw, pool_scale=dpsc, lb=dlb,
                  hgrn_norm_g=jnp.sum(dgn, axis=0, keepdims=True))
    carry, landed = ride["gw_in"](little)
    big["w_in"], outs = _mm_ride(sv["h_t"], dproj, carry, name=f"gw_in{l}", out_shards=NCHIP, out_dtype=BF16)
    landed(outs)
    carry, landed = ride["d_h"](big)
    dh, outs = _mm_ride(dproj, w["w_in"][l], carry, name=f"d_h{l}", b_mode="nt_shk", tn=1024)
    landed(outs)
    carry, landed = ride["prenorm"](big)
    (dx, dshift, dscale, dg_pre), outs = _prenorm_bwd(dh, dxn, sv["x"], small["g_pre"][l], sv["scale"],
                                                      f"prenorm_bwd{l}", carry)
    landed(outs)
    little.update(dshift=dshift, dscale=dscale, g_pre=dg_pre)
    return dx, big, little


SMALL_ROWS = 176


def _rows8(t):
    t = t.reshape(-1, D)
    return jnp.pad(t, ((0, -t.shape[0] % 8), (0, 0)))


def _pack_small(parts):
    row_keys = ("dshift", "dscale", "dgate", "g_pre", "g_post", "lb", "pool_scale", "hgrn_norm_g")
    flat = [p[k] for p in parts for k in row_keys] + [p["pool_w"].reshape(GROUPS * 128 * 128 // D, D) for p in parts]
    nk = len(row_keys)

    def body(*refs):
        o_ref = refs[-1]
        o_ref[...] = jnp.zeros((SMALL_ROWS, D), F32)
        for l in range(2):
            dshift, dscale, dgate, g_pre, g_post, lb, pscale, gn = refs[l * nk:(l + 1) * nk]
            for r, ref in enumerate((dshift, dscale, dgate)):
                o_ref[3 * l + r:3 * l + r + 1, :] = ref[...]
            o_ref[8 + l:9 + l, :] = g_pre[...]
            o_ref[16 + l:17 + l, :] = g_post[...]
            o_ref[24 + l:25 + l, :] = lb[...]
            o_ref[160:161, l * POOL_W:(l + 1) * POOL_W] = pscale[...]
            o_ref[168:169, l * HD:(l + 1) * HD] = gn[...]
            pw = refs[2 * nk + l]
            rows = pw.shape[0]
            o_ref[32 + l * rows:32 + (l + 1) * rows, :] = pw[...]

    return pl.pallas_call(body, name="pack_small", out_shape=jax.ShapeDtypeStruct((SMALL_ROWS, D), F32),
                          compiler_params=_cp())(*flat)


def _unpack_small(p):
    return (p[0:6].reshape(2, 3 * D), p[8:10], p[16:18], p[24:26], p[32:160].reshape(2, GROUPS, 128, 128),
            p[160:161].reshape(2, POOL_W), p[168:169, 0:2 * HD].reshape(2, HD))


def kernel(x, c, w_ada, b_ada, g_pre, g_post, w_in, pool_w, pool_scale, lb_logits, hgrn_norm_g, w_pool_o, w_hgrn_o, w_out, loss_target, m_w_ada, m_b_ada, m_g_pre, m_g_post, m_w_in, m_pool_w, m_pool_scale, m_lb_logits, m_hgrn_norm_g, m_w_pool_o, m_w_hgrn_o, m_w_out, v_w_ada, v_b_ada, v_g_pre, v_g_post, v_w_in, v_pool_w, v_pool_scale, v_lb_logits, v_hgrn_norm_g, v_w_pool_o, v_w_hgrn_o, v_w_out):
    ax, ay, ac = lax.axis_index("x"), lax.axis_index("y"), lax.axis_index("c")
    chip = 2 * ax + ay
    dev = 2 * chip + ac
    xe, te = x[0], loss_target[0]
    ada_s = w_ada.shape[2]

    big_names = ("w_in", "w_pool_o", "w_hgrn_o", "w_out")
    big_w = (w_in, w_pool_o, w_hgrn_o, w_out)
    core = jnp.stack([ac]).astype(jnp.int32)
    place = jnp.stack([chip, ac]).astype(jnp.int32)
    slots = {(k, l): _cast_to_slot(place, t, l, f"cast_{k}{l}") for l in range(2) for k, t in zip(big_names, big_w)}
    w = {k: [None, None] for k in big_names}
    def fills(keys):
        def landed(outs):
            for (k, l), o in zip(keys, outs):
                w[k][l] = slots[k, l] = o
        return landed

    rest0 = [(k, 0) for k in big_names[1:]]
    rest1 = [(k, 1) for k in big_names[1:]]
    no_carry = (None, lambda outs: None)
    order = jnp.stack([chip, 2 * (1 - ax) + ay, 2 * ax + (1 - ay), 2 * (1 - ax) + (1 - ay)]).astype(jnp.int32)

    def ride_fwd0(stage):
        if stage == "proj":
            return _GatherInProj(slots["w_in", 0], order), fills([("w_in", 0)])
        if stage == "hgrn":
            return (_join_carries(_gather_carry([slots[t] for t in rest0]),
                                  _gather_carry([slots["w_in", 1]], piece=(0, 2, 4))),
                    fills(rest0 + [("w_in", 1)]))
        if stage == "tail":
            return _gather_carry([slots["w_in", 1]], piece=(2, 1, 4)), fills([("w_in", 1)])
        return no_carry

    def ride_fwd1(stage):
        if stage == "prenorm":
            return _gather_carry([slots["w_in", 1]], piece=(3, 1, 4)), fills([("w_in", 1)])
        if stage == "hgrn":
            return _gather_carry([slots[t] for t in rest1]), fills(rest1)
        return no_carry

    c_all = _gather_small(jnp.broadcast_to(c, (8, D)), "gather_c").reshape(NDEV, 8, D)[:, 0, :]
    c_pad = jnp.pad(c_all, ((0, ADA_PAD - NDEV), (0, 0)))
    b_sh = lax.dynamic_slice(b_ada, (0, chip * ada_s), (2, ada_s))
    ada_cols = _gather_small(_ada_fwd(c_pad, w_ada, b_sh), "gather_ada")
    ada_cols = ada_cols.reshape(NCHIP, 2, NDEV, 2, ada_s)[:, 0]
    ada_all = jnp.transpose(ada_cols, (2, 1, 0, 3)).reshape(2, NDEV, 3 * D)
    ada_me = lax.dynamic_slice(ada_all, (0, dev, 0), (2, 1, 3 * D))

    lbs = _lb_fwd(lb_logits)
    small = dict(g_pre=g_pre[:, None, :], g_post=g_post[:, None, :], pool_w=pool_w,
                 pool_scale=pool_scale[:, None, :], lb=lbs[:, None, :], hgrn_norm_g=hgrn_norm_g[:, None, :])

    (x1,), sv0 = _layer_fwd(0, xe, ada_me[0], w, small, ride_fwd0)
    (dx2, loss_blk), sv1 = _layer_fwd(1, x1, ada_me[1], w, small, ride_fwd1, target=te)

    parts, recv, held = {}, {}, {}

    def pair_ride(keys, grads):
        def landed(outs):
            held.update({kl: (g, o) for kl, g, o in zip(keys, grads, outs)})
        return _pair_carry(grads), landed

    def pair_adds(keys):
        for kl in keys:
            g, got = held.pop(kl)
            parts[kl] = _pair_add(core, g, got, f"rs_add_{kl[0]}{kl[1]}")

    def exchange(keys):
        def landed(outs):
            recv.update(zip(keys, outs))
        return _chips_carry([parts[kl] for kl in keys]), landed

    def share(key, first, count):
        def landed(outs):
            (recv[key],) = outs
        into = [recv[key]] if key in recv else None
        return _chips_carry([parts[key]], piece=(first, count, 8), into=into), landed

    def together(*rides):
        carries, fns = zip(*rides)

        def landed(outs):
            for cr, fn in zip(carries, fns):
                fn(outs[:len(cr.outs)])
                outs = outs[len(cr.outs):]
        return _join_carries(*carries), landed

    def early(l):
        return [(k, l) for k in big_names[1:]]

    def pair_alone(keys, grads, tag):
        held.update({kl: (g, o) for kl, g, o in zip(keys, grads, _rs_pair(grads, f"rs_pair_{tag}"))})
        pair_adds(keys)

    def ride_hgrn1(big):
        return pair_ride(early(1), [big[k] for k in big_names[1:]])

    def ride_gw_in1(_):
        pair_adds(early(1))
        return exchange(early(1))

    def ride_d_h1(big):
        return pair_ride([("w_in", 1)], [big["w_in"]])

    def ride_prenorm1(_):
        pair_adds([("w_in", 1)])
        return share(("w_in", 1), 0, 1)

    def ride_head0(_):
        return share(("w_in", 1), 1, 3)

    def ride_hgrn0(big):
        pair_alone(early(0), [big[k] for k in big_names[1:]], "early0")
        return together(exchange(early(0)), share(("w_in", 1), 4, 4))

    def ride_d_h0(big):
        pair_alone([("w_in", 0)], [big["w_in"]], "w_in0")
        return share(("w_in", 0), 0, 4)

    def ride_prenorm0(_):
        return share(("w_in", 0), 4, 4)

    no_ride = lambda so_far: no_carry
    dx1, big1, little1 = _layer_bwd(1, dx2, sv1, w, small, dict(head=no_ride, hgrn=ride_hgrn1, gw_in=ride_gw_in1,
                                                                d_h=ride_d_h1, prenorm=ride_prenorm1))

    gathered = {}
    zero_row = jnp.zeros((1, D), F32)

    def ride_gw_in0(little):
        so_far = dict(little, dshift=zero_row, dscale=zero_row, g_pre=zero_row)

        def landed(outs):
            (gathered["early"],) = outs
        return _gather_rows_carry(_pack_small([so_far, little1])), landed

    dx0, big0, little0 = _layer_bwd(0, dx1, sv0, w, small,
                                    dict(head=ride_head0, hgrn=ride_hgrn0, gw_in=ride_gw_in0, d_h=ride_d_h0,
                                         prenorm=ride_prenorm0))
    loss_row = jnp.broadcast_to(loss_blk[0:1, 0:1], (1, D))
    late = _rows8(jnp.stack([little0["dshift"], little0["dscale"], little0["g_pre"], loss_row]))
    late = _gather_small(late, "gather_small_late").reshape(NDEV, 8, D)
    loss = jnp.sum(late[:, 3, 0])
    packed = gathered["early"].reshape(NDEV, SMALL_ROWS, D)
    packed = packed.at[:, 0:2, :].set(late[:, 0:2, :]).at[:, 8:9, :].set(late[:, 2:3, :])
    red = []
    for k in big_names:
        both = _chip_sum(place, parts[k, 1], recv[k, 1], 1, None, f"rs_sum_{k}1")
        red.append(_chip_sum(place, parts[k, 0], recv[k, 0], 0, both, f"rs_sum_{k}0"))
    g_big = dict(zip(big_names, _rs_swap(red)))

    def upd(wt, g, m, v, name, carry=None):
        shp = wt.shape
        two = lambda t: t.reshape(-1, shp[-1])
        res = _adamw(two(wt), two(g), two(m), two(v), name, carry)
        return [t.reshape(shp) for t in res[:3]], res[3:]

    u_w_in, _ = upd(w_in, g_big["w_in"], m_w_in, v_w_in, "adamw_w_in")
    g_small = _sum_devices(packed)
    g_b_ada, g_g_pre, g_g_post, g_lb, g_pool_w, g_pool_scale, g_norm_g = _unpack_small(g_small)
    g_lb_logits = _lb_bwd(lb_logits, g_lb)
    d_ada_all = packed[:, 0:6, :].reshape(NDEV, 2, 3 * D)
    d_ada_sh = lax.dynamic_slice(jnp.transpose(d_ada_all, (1, 0, 2)), (0, 0, chip * ada_s), (2, NDEV, ada_s))
    d_ada_sh = jnp.pad(d_ada_sh, ((0, 0), (0, ADA_PAD - NDEV), (0, 0)))
    g_w_ada = _ada_wgrad(c_pad.T, d_ada_sh)

    u_w_ada, _ = upd(w_ada, g_w_ada, m_w_ada, v_w_ada, "adamw_w_ada")
    u_w_pool_o, _ = upd(w_pool_o, g_big["w_pool_o"], m_w_pool_o, v_w_pool_o, "adamw_w_pool_o")
    u_w_hgrn_o, _ = upd(w_hgrn_o, g_big["w_hgrn_o"], m_w_hgrn_o, v_w_hgrn_o, "adamw_w_hgrn_o")
    u_w_out, _ = upd(w_out, g_big["w_out"], m_w_out, v_w_out, "adamw_w_out")
    small_w = dict(b_ada=(b_ada, m_b_ada, v_b_ada), g_pre=(g_pre, m_g_pre, v_g_pre),
                   g_post=(g_post, m_g_post, v_g_post), lb_logits=(lb_logits, m_lb_logits, v_lb_logits),
                   pool_w=(pool_w, m_pool_w, v_pool_w), pool_scale=(pool_scale, m_pool_scale, v_pool_scale),
                   hgrn_norm_g=(hgrn_norm_g, m_hgrn_norm_g, v_hgrn_norm_g))
    in_rows = [tuple(t.reshape(rows, width) for t in small_w[key]) for key, _, rows, width in SMALL_PARTS]
    u_rows = _adamw_small(g_small, g_lb_logits, in_rows)
    u_small = {key: [t.reshape(small_w[key][0].shape) for t in u_rows[p]]
               for p, (key, _, _, _) in enumerate(SMALL_PARTS)}

    grads_out = (g_w_ada, g_b_ada, g_g_pre, g_g_post, g_big["w_in"], g_pool_w, g_pool_scale, g_lb_logits,
                 g_norm_g, g_big["w_pool_o"], g_big["w_hgrn_o"], g_big["w_out"])

    def ordered(k):
        s = lambda key: u_small[key][k]
        return (u_w_ada[k], s("b_ada"), s("g_pre"), s("g_post"), u_w_in[k], s("pool_w"), s("pool_scale"),
                s("lb_logits"), s("hgrn_norm_g"), u_w_pool_o[k], u_w_hgrn_o[k], u_w_out[k])

    return (loss, dx0[None], *grads_out, *ordered(0), *ordered(1), *ordered(2))
```

```python
import functools

import jax
import jax.numpy as jnp
from jax import lax
from jax.experimental import pallas as pl
from jax.experimental.pallas import tpu as pltpu

F32 = jnp.float32
BF16 = jnp.bfloat16
MESH = pl.DeviceIdType.MESH

D = 1024
HEADS = 8
HD = 128
GROUPS = 4
POOL_W = 512
CH = 128
SB_WIDE = 32
SB = 16
NH = 2
IN_W = 7168
NCHIP = 4
NDEV = 8
EPS = 1e-6
PV0, PG0, HQ0, HF0, HI0, HG0 = 0, 4, 8, 16, 24, 32
MGP_BLK, MGH_BLK = 5, 6

LR, B1, B2, AEPS, WD, STEP = 0.001, 0.9, 0.999, 1e-08, 0.01, 10
VMEM_LIMIT = 56 * 1024 * 1024


def _cp(sem=None, **kw):
    if sem is not None:
        kw["dimension_semantics"] = sem
    return pltpu.CompilerParams(vmem_limit_bytes=VMEM_LIMIT, **kw)


def _sig(z):
    return 1.0 / (1.0 + jnp.exp(-z))


def _dsilu(z, s):
    return s * (1.0 + z * (1.0 - s))


def _row_tile(rows, cap):
    if rows <= cap:
        return rows
    t = 1 << (cap.bit_length() - 1)
    while rows % t:
        t //= 2
    return t


ANY = pl.BlockSpec(memory_space=pl.ANY)


class _Carry:
    def __init__(self, ins, outs, aliases, n_sem, start, finish):
        self.ins, self.outs, self.aliases, self.n_sem = list(ins), list(outs), dict(aliases), n_sem
        self.start, self.finish = start, finish


class _SemWindow:
    def __init__(self, ref, base):
        self._ref, self._base = ref, base

    @property
    def at(self):
        return self

    def __getitem__(self, k):
        return self._ref.at[self._base + k]


def _join_carries(*carries):
    ins, outs, aliases, spans, n_sem = [], [], {}, [], 0
    for cr in carries:
        aliases.update({len(ins) + i: len(outs) + o for i, o in cr.aliases.items()})
        spans.append((len(ins), len(cr.ins), len(outs), len(cr.outs), n_sem))
        ins, outs, n_sem = ins + cr.ins, outs + cr.outs, n_sem + cr.n_sem

    def run(which):
        def fn(i_refs, o_refs, send_sems, recv_sems):
            for cr, (i0, ni, o0, no, s0) in zip(carries, spans):
                getattr(cr, which)(i_refs[i0:i0 + ni], o_refs[o0:o0 + no], _SemWindow(send_sems, s0),
                                   _SemWindow(recv_sems, s0))
        return fn

    return _Carry(ins, outs, aliases, n_sem, run("start"), run("finish"))


def _call(body, *, name, grid, in_specs, out_specs, out_shape, args, scratch_shapes=(), sem=None, carry=None,
          aliases=None):
    in_specs, out_specs, out_shape = list(in_specs), list(out_specs), list(out_shape)
    scratch_shapes = list(scratch_shapes)
    aliases = dict(aliases or {})
    if carry is None:
        outs = pl.pallas_call(body, name=name, grid=grid, in_specs=in_specs, out_specs=out_specs,
                              out_shape=out_shape, scratch_shapes=scratch_shapes, input_output_aliases=aliases,
                              compiler_params=_cp(sem))(*args)
        return list(outs)
    n_in, n_out, n_scr = len(in_specs), len(out_specs), len(scratch_shapes)
    c_in, c_out = len(carry.ins), len(carry.outs)

    def wrapped(*refs):
        k_in, rest = refs[:n_in], refs[n_in:]
        ci, rest = rest[:c_in], rest[c_in:]
        k_out, rest = rest[:n_out], rest[n_out:]
        co, rest = rest[:c_out], rest[c_out:]
        k_scr, (ssem, rsem) = rest[:n_scr], rest[n_scr:]
        pids = [pl.program_id(d) for d in range(len(grid))]
        first = functools.reduce(jnp.logical_and, [p == 0 for p in pids])
        last = functools.reduce(jnp.logical_and, [p == g - 1 for p, g in zip(pids, grid)])

        @pl.when(first)
        def _():
            carry.start(ci, co, ssem, rsem)

        body(*k_in, *k_out, *k_scr)

        @pl.when(last)
        def _():
            carry.finish(ci, co, ssem, rsem)

    outs = pl.pallas_call(
        wrapped, name=name, grid=grid, in_specs=in_specs + [ANY] * c_in, out_specs=out_specs + [ANY] * c_out,
        out_shape=out_shape + carry.outs,
        input_output_aliases={**aliases, **{n_in + i: n_out + o for i, o in carry.aliases.items()}},
        scratch_shapes=scratch_shapes + [pltpu.SemaphoreType.DMA((carry.n_sem,))] * 2,
        compiler_params=_cp(("arbitrary",) * len(grid)),
    )(*args, *carry.ins)
    return list(outs)


def _mm(a, b, *, name, b_mode="nn", out_shards=0, tm=1024, tn=256, tk=None, out_dtype=F32, carry=None):
    assert b_mode in ("nn", "nn_sh", "nt_shk"), b_mode
    M, K = a.shape
    if b_mode == "nn":
        N = b.shape[1]
    elif b_mode == "nn_sh":
        N = b.shape[0] * b.shape[2]
    else:
        N = b.shape[1]
    tm = _row_tile(M, tm)
    if b_mode == "nn_sh":
        tn = _row_tile(b.shape[2], tn)
    elif out_shards:
        tn = _row_tile(N // out_shards, tn)
    else:
        tn = _row_tile(N, tn)
    if tk is None:
        tk = K if b_mode != "nt_shk" else b.shape[2]
    if b_mode == "nt_shk":
        tk = _row_tile(b.shape[2], tk)
    nm, nn, nk = M // tm, N // tn, K // tk

    a_spec = pl.BlockSpec((tm, tk), lambda m, n, k: (m, k))
    if b_mode == "nn":
        b_spec = pl.BlockSpec((tk, tn), lambda m, n, k: (k, n))
    elif b_mode == "nn_sh":
        nps = b.shape[2] // tn
        b_spec = pl.BlockSpec((None, tk, tn), lambda m, n, k: (n // nps, k, n % nps))
    else:
        kps = b.shape[2] // tk
        b_spec = pl.BlockSpec((None, tn, tk), lambda m, n, k: (k // kps, n, k % kps))
    if out_shards:
        ops = (N // out_shards) // tn
        o_spec = pl.BlockSpec((None, tm, tn), lambda m, n, k: (n // ops, m, n % ops))
        o_shape = jax.ShapeDtypeStruct((out_shards, M, N // out_shards), out_dtype)
    else:
        o_spec = pl.BlockSpec((tm, tn), lambda m, n, k: (m, n))
        o_shape = jax.ShapeDtypeStruct((M, N), out_dtype)
    dn = (((1,), (1,)), ((), ())) if b_mode == "nt_shk" else (((1,), (0,)), ((), ()))

    def body(a_ref, b_ref, o_ref, acc_ref):
        k = pl.program_id(2)

        @pl.when(k == 0)
        def _():
            acc_ref[...] = jnp.zeros(acc_ref.shape, F32)

        acc_ref[...] += lax.dot_general(a_ref[...].astype(BF16), b_ref[...].astype(BF16), dn,
                                        preferred_element_type=F32)

        @pl.when(k == nk - 1)
        def _():
            o_ref[...] = acc_ref[...].astype(o_ref.dtype)

    outs = _call(body, name=name, grid=(nm, nn, nk), in_specs=[a_spec, b_spec], out_specs=[o_spec],
                 out_shape=[o_shape], scratch_shapes=[pltpu.VMEM((tm, tn), F32)],
                 sem=("parallel", "parallel", "arbitrary"), args=(a, b), carry=carry)
    return outs[0] if carry is None else (outs[0], outs[1:])


def _rowvec(n=D):
    return pl.BlockSpec((1, n), lambda i: (0, 0))


def _prenorm_fwd(x, g, scale, shift, name, carry=None):
    S = x.shape[0]
    tr = _row_tile(S, 256)

    def body(x_ref, g_ref, sc_ref, sh_ref, h_ref, ht_ref):
        xv = x_ref[...]
        r = lax.rsqrt(jnp.mean(xv * xv, axis=-1, keepdims=True) + EPS)
        hv = (xv * r) * g_ref[...] * (1.0 + sc_ref[...]) + sh_ref[...]
        h_ref[...] = hv.astype(BF16)
        ht_ref[...] = hv.T.astype(BF16)

    outs = _call(
        body, name=name, grid=(S // tr,),
        in_specs=[pl.BlockSpec((tr, D), lambda i: (i, 0)), _rowvec(), _rowvec(), _rowvec()],
        out_specs=[pl.BlockSpec((tr, D), lambda i: (i, 0)), pl.BlockSpec((D, tr), lambda i: (0, i))],
        out_shape=[jax.ShapeDtypeStruct((S, D), BF16), jax.ShapeDtypeStruct((D, S), BF16)],
        sem=("parallel",), args=(x, g, scale, shift), carry=carry)
    return outs[:2], outs[2:]


def _prenorm_bwd(dh, dxn, x, g, scale, name, carry=None):
    S = x.shape[0]
    tr = _row_tile(S, 256)

    def body(dh_ref, dxn_ref, x_ref, g_ref, sc_ref, dx_ref, dsh_ref, dsc_ref, dg_ref):
        i = pl.program_id(0)

        @pl.when(i == 0)
        def _():
            dsh_ref[...] = jnp.zeros((1, D), F32)
            dsc_ref[...] = jnp.zeros((1, D), F32)
            dg_ref[...] = jnp.zeros((1, D), F32)

        xv = x_ref[...]
        dhv = dh_ref[...]
        gv = g_ref[...]
        mod = 1.0 + sc_ref[...]
        r = lax.rsqrt(jnp.mean(xv * xv, axis=-1, keepdims=True) + EPS)
        xh = xv * r
        dsh_ref[...] += jnp.sum(dhv, axis=0, keepdims=True)
        dsc_ref[...] += jnp.sum(dhv * (xh * gv), axis=0, keepdims=True)
        dg_ref[...] += jnp.sum(dhv * mod * xh, axis=0, keepdims=True)
        u = dhv * mod * gv
        dx_ref[...] = dxn_ref[...] + r * u - xv * (r * r * r) * jnp.mean(u * xv, axis=-1, keepdims=True)

    tile = pl.BlockSpec((tr, D), lambda i: (i, 0))
    outs = _call(
        body, name=name, grid=(S // tr,),
        in_specs=[tile, tile, tile, _rowvec(), _rowvec()],
        out_specs=[tile, _rowvec(), _rowvec(), _rowvec()],
        out_shape=[jax.ShapeDtypeStruct((S, D), F32)] + [jax.ShapeDtypeStruct((1, D), F32)] * 3,
        sem=("arbitrary",), args=(dh, dxn, x, g, scale), carry=carry)
    return outs[:4], outs[4:]


def _layer_tail_fwd(proj, a_in, b_in, x, w_po, w_ho, w_out, gate, g, name, target=None, carry=None):
    S = proj.shape[0]
    tr = _row_tile(S, 256)
    nsh, _, wsh = w_po.shape
    n_in = 10 + (target is not None)

    def body(*refs):
        (mgp_ref, mgh_ref, a_ref, b_ref, x_ref, wpo_ref, who_ref, wout_ref, gate_ref, g_ref) = refs[:10]
        bra_ref, brb_ref, mt_ref, y_ref, xn_ref = refs[n_in:n_in + 5]
        av = a_ref[...]
        bra = jnp.concatenate([jnp.dot(av, wpo_ref[j], preferred_element_type=F32) for j in range(nsh)], axis=1)
        brb = jnp.dot(b_ref[...], who_ref[...], preferred_element_type=F32)
        mv = _sig(mgp_ref[...].astype(F32)) * bra + _sig(mgh_ref[...].astype(F32)) * brb
        bra_ref[...] = bra.astype(BF16)
        brb_ref[...] = brb.astype(BF16)
        mt_ref[...] = mv.T.astype(BF16)
        yv = jnp.dot(mv.astype(BF16), wout_ref[...], preferred_element_type=F32)
        y_ref[...] = yv
        r = lax.rsqrt(jnp.mean(yv * yv, axis=-1, keepdims=True) + EPS)
        xn = x_ref[...] + gate_ref[...] * ((yv * r) * g_ref[...])
        if target is None:
            xn_ref[...] = xn
        else:
            t_ref, l_ref = refs[10], refs[n_in + 5]

            @pl.when(pl.program_id(0) == 0)
            def _():
                l_ref[...] = jnp.zeros((8, 128), F32)

            err = xn - t_ref[...]
            xn_ref[...] = err * (1.0 / D)
            l_ref[...] += 0.5 * jnp.sum(jnp.mean(err * err, axis=-1, keepdims=True))

    tile = pl.BlockSpec((tr, D), lambda i: (i, 0))
    whole = lambda t: pl.BlockSpec(t.shape, lambda i: (0,) * t.ndim)
    last = target is not None
    outs = _call(
        body, name=name, grid=(S // tr,),
        in_specs=[pl.BlockSpec((tr, D), lambda i: (i, MGP_BLK)), pl.BlockSpec((tr, D), lambda i: (i, MGH_BLK)),
                  pl.BlockSpec((tr, POOL_W), lambda i: (i, 0)), tile, tile, whole(w_po), whole(w_ho),
                  whole(w_out), _rowvec(), _rowvec()] + [tile] * last,
        out_specs=[tile, tile, pl.BlockSpec((D, tr), lambda i: (0, i)), tile, tile]
        + [pl.BlockSpec((8, 128), lambda i: (0, 0))] * last,
        out_shape=[jax.ShapeDtypeStruct((S, D), BF16), jax.ShapeDtypeStruct((S, D), BF16),
                   jax.ShapeDtypeStruct((D, S), BF16), jax.ShapeDtypeStruct((S, D), F32),
                   jax.ShapeDtypeStruct((S, D), F32)] + [jax.ShapeDtypeStruct((8, 128), F32)] * last,
        sem=("arbitrary",) if last else ("parallel",),
        args=(proj, proj, a_in, b_in, x, w_po, w_ho, w_out, gate, g) + ((target,) if last else ()), carry=carry)
    return outs[:5 + last], outs[5 + last:]


def _layer_head_bwd(dxn, y, proj, br_a, br_b, w_po, w_ho, w_out, gate, g, name, carry=None):
    S = y.shape[0]
    tr = _row_tile(S, 256)
    nsh, _, wsh = w_po.shape

    def body(dxn_ref, y_ref, mgp_ref, mgh_ref, bra_ref, brb_ref, wpo_ref, who_ref, wout_ref, gate_ref, g_ref,
             dy_ref, dba_ref, dbb_ref, dproj_ref, dain_ref, dbin_ref, dgate_ref, dg_ref, dmgh_s):
        i = pl.program_id(0)
        j = pl.program_id(1)

        @pl.when((i == 0) & (j == 0))
        def _():
            dgate_ref[...] = jnp.zeros((1, D), F32)
            dg_ref[...] = jnp.zeros((1, D), F32)

        @pl.when(j == 1)
        def _():
            dproj_ref[...] = dmgh_s[...]

        @pl.when(j == 0)
        def _():
            everything(dxn_ref, y_ref, mgp_ref, mgh_ref, bra_ref, brb_ref, wpo_ref, who_ref, wout_ref, gate_ref,
                       g_ref, dy_ref, dba_ref, dbb_ref, dproj_ref, dain_ref, dbin_ref, dgate_ref, dg_ref, dmgh_s)

    def everything(dxn_ref, y_ref, mgp_ref, mgh_ref, bra_ref, brb_ref, wpo_ref, who_ref, wout_ref, gate_ref, g_ref,
                   dy_ref, dba_ref, dbb_ref, dproj_ref, dain_ref, dbin_ref, dgate_ref, dg_ref, dmgh_s):
        yv = y_ref[...]
        dv = dxn_ref[...]
        gv = g_ref[...]
        gt = gate_ref[...]
        r = lax.rsqrt(jnp.mean(yv * yv, axis=-1, keepdims=True) + EPS)
        yh = yv * r
        dgate_ref[...] += jnp.sum(dv * (yh * gv), axis=0, keepdims=True)
        dg_ref[...] += jnp.sum(dv * gt * yh, axis=0, keepdims=True)
        u = dv * gt * gv
        dy = (r * u - yv * (r * r * r) * jnp.mean(u * yv, axis=-1, keepdims=True)).astype(BF16)
        dy_ref[...] = dy
        dm = _dot_nt(dy, wout_ref[...])
        sp = _sig(mgp_ref[...].astype(F32))
        sh = _sig(mgh_ref[...].astype(F32))
        dba = (dm * sp).astype(BF16)
        dbb = (dm * sh).astype(BF16)
        dba_ref[...] = dba
        dbb_ref[...] = dbb
        dproj_ref[...] = (dm * bra_ref[...].astype(F32) * sp * (1.0 - sp)).astype(BF16)
        dmgh_s[...] = (dm * brb_ref[...].astype(F32) * sh * (1.0 - sh)).astype(BF16)
        dain = _dot_nt(dba[:, 0:wsh], wpo_ref[0])
        for k in range(1, nsh):
            dain = dain + _dot_nt(dba[:, k * wsh:(k + 1) * wsh], wpo_ref[k])
        dain_ref[...] = dain
        dbin_ref[...] = _dot_nt(dbb, who_ref[...])

    tile = pl.BlockSpec((tr, D), lambda i, j: (i, 0))
    whole = lambda t: pl.BlockSpec(t.shape, lambda i, j: (0,) * t.ndim)
    vec = pl.BlockSpec((1, D), lambda i, j: (0, 0))
    ahead = lambda i, j: jnp.minimum(i + j, S // tr - 1)
    tile_in = pl.BlockSpec((tr, D), lambda i, j: (ahead(i, j), 0))
    outs = _call(
        body, name=name, grid=(S // tr, 2),
        in_specs=[tile_in, tile_in, pl.BlockSpec((tr, D), lambda i, j: (ahead(i, j), MGP_BLK)),
                  pl.BlockSpec((tr, D), lambda i, j: (ahead(i, j), MGH_BLK)), tile_in, tile_in, whole(w_po),
                  whole(w_ho), whole(w_out), vec, vec],
        out_specs=[tile, tile, tile, pl.BlockSpec((tr, D), lambda i, j: (i, MGP_BLK + j)),
                   pl.BlockSpec((tr, POOL_W), lambda i, j: (i, 0)), tile, vec, vec],
        out_shape=[jax.ShapeDtypeStruct((S, D), BF16)] * 3
        + [jax.ShapeDtypeStruct((S, IN_W), BF16), jax.ShapeDtypeStruct((S, POOL_W), F32),
           jax.ShapeDtypeStruct((S, D), F32), jax.ShapeDtypeStruct((1, D), F32), jax.ShapeDtypeStruct((1, D), F32)],
        scratch_shapes=[pltpu.VMEM((tr, D), BF16)], sem=("arbitrary", "arbitrary"),
        args=(dxn, y, proj, proj, br_a, br_b, w_po, w_ho, w_out, gate, g), carry=carry)
    return outs[:8], outs[8:]


def _pool_pieces(u, g, S):
    rowi = lax.broadcasted_iota(jnp.int32, (S, 1), 0)

    def down(z, k):
        return jnp.where(rowi >= k, pltpu.roll(z, k, axis=0), 0.0)

    s2 = u + down(u, 1)
    s4 = s2 + down(s2, 2)
    s8 = s4 + down(s4, 4)
    s16 = s8 + down(s8, 8)
    win = jnp.where(g == 0, s2, jnp.where(g == 1, s4, jnp.where(g == 2, s8, s16)))
    w = jnp.where(g == 0, 2, jnp.where(g == 1, 4, jnp.where(g == 2, 8, 16)))
    count = jnp.minimum(rowi + 1, w).astype(F32)
    return win / count - u, count, rowi


def _pool_fwd(proj, pw, pscale, name):
    S = proj.shape[0]

    def body(pv_ref, pg_ref, pw_ref, sc_ref, a_ref, at_ref):
        g = pl.program_id(0)
        pooled, _, _ = _pool_pieces(pv_ref[...].astype(F32), g, S)
        pm = jnp.dot(pooled.astype(BF16), pw_ref[...].astype(BF16), preferred_element_type=F32)
        pgv = pg_ref[...].astype(F32)
        av = pm * sc_ref[...] * (pgv * _sig(pgv))
        a_ref[...] = av.astype(BF16)
        at_ref[...] = av.T.astype(BF16)

    return pl.pallas_call(
        body, name=name, grid=(GROUPS,),
        in_specs=[pl.BlockSpec((S, 128), lambda g: (0, PV0 + g)), pl.BlockSpec((S, 128), lambda g: (0, PG0 + g)),
                  pl.BlockSpec((None, 128, 128), lambda g: (g, 0, 0)), pl.BlockSpec((1, 128), lambda g: (0, g))],
        out_specs=[pl.BlockSpec((S, 128), lambda g: (0, g)), pl.BlockSpec((128, S), lambda g: (g, 0))],
        out_shape=[jax.ShapeDtypeStruct((S, POOL_W), BF16), jax.ShapeDtypeStruct((POOL_W, S), BF16)],
        compiler_params=_cp(("parallel",)),
    )(proj, proj, pw, pscale)


def _pool_bwd(da, proj, pw, pscale, dproj, name):
    S = proj.shape[0]

    def body(da_ref, pv_ref, pg_ref, pw_ref, sc_ref, dproj_in, dproj_ref, dpw_ref, dsc_ref, dpg_s):
        @pl.when(pl.program_id(1) == 1)
        def _():
            dproj_ref[...] = dpg_s[...]

        @pl.when(pl.program_id(1) == 0)
        def _():
            group(da_ref, pv_ref, pg_ref, pw_ref, sc_ref, dproj_ref, dpg_s, dpw_ref, dsc_ref)

    def group(da_ref, pv_ref, pg_ref, pw_ref, sc_ref, dpv_ref, dpg_ref, dpw_ref, dsc_ref):
        g = pl.program_id(0)
        pooled, count, rowi = _pool_pieces(pv_ref[...].astype(F32), g, S)
        pwb = pw_ref[...].astype(BF16)
        pm = jnp.dot(pooled.astype(BF16), pwb, preferred_element_type=F32)
        scv = sc_ref[...]
        pgv = pg_ref[...].astype(F32)
        sg = _sig(pgv)
        dav = da_ref[...]
        d_ps = dav * (pgv * sg)
        dpg_ref[...] = (dav * (pm * scv) * _dsilu(pgv, sg)).astype(BF16)
        dsc_ref[...] = jnp.sum(d_ps * pm, axis=0, keepdims=True)
        d_pm = (d_ps * scv).astype(BF16)
        dpw_ref[...] = lax.dot_general(pooled.astype(BF16), d_pm, (((0,), (0,)), ((), ())),
                                       preferred_element_type=F32)
        d_pooled = lax.dot_general(d_pm, pwb, (((1,), (1,)), ((), ())), preferred_element_type=F32)
        z = d_pooled / count

        def up(v, k):
            return jnp.where(rowi < S - k, pltpu.roll(v, S - k, axis=0), 0.0)

        t2 = z + up(z, 1)
        t4 = t2 + up(t2, 2)
        t8 = t4 + up(t4, 4)
        t16 = t8 + up(t8, 8)
        adj = jnp.where(g == 0, t2, jnp.where(g == 1, t4, jnp.where(g == 2, t8, t16)))
        dpv_ref[...] = (adj - d_pooled).astype(BF16)

    col = lambda g, j: (0, g)
    ahead = lambda g, j: jnp.minimum(g + j, GROUPS - 1)
    return pl.pallas_call(
        body, name=name, grid=(GROUPS, 2),
        in_specs=[pl.BlockSpec((S, 128), lambda g, j: (0, ahead(g, j))),
                  pl.BlockSpec((S, 128), lambda g, j: (0, PV0 + ahead(g, j))),
                  pl.BlockSpec((S, 128), lambda g, j: (0, PG0 + ahead(g, j))),
                  pl.BlockSpec((None, 128, 128), lambda g, j: (ahead(g, j), 0, 0)),
                  pl.BlockSpec((1, 128), lambda g, j: (0, ahead(g, j))), ANY],
        out_specs=[pl.BlockSpec((S, 128), lambda g, j: (0, PV0 + g + (PG0 - PV0) * j)),
                   pl.BlockSpec((None, 128, 128), lambda g, j: (g, 0, 0)), pl.BlockSpec((1, 128), col)],
        out_shape=[jax.ShapeDtypeStruct(dproj.shape, dproj.dtype),
                   jax.ShapeDtypeStruct((GROUPS, 128, 128), F32), jax.ShapeDtypeStruct((1, POOL_W), F32)],
        scratch_shapes=[pltpu.VMEM((S, 128), BF16)], input_output_aliases={5: 0},
        compiler_params=_cp(("arbitrary", "arbitrary")),
    )(da, proj, proj, pw, pscale, dproj)


SCAN_SHIFTS = tuple(1 << b for b in range(CH.bit_length() - 1))


def _chunk_cumsum(z, rowi):
    for sh in SCAN_SHIFTS:
        z = z + jnp.where(rowi >= sh, pltpu.roll(z, sh, axis=0), 0.0)
    return z


def _chunk_rev_cumsum(z, rowi):
    for sh in SCAN_SHIFTS:
        z = z + jnp.where(rowi < CH - sh, pltpu.roll(z, CH - sh, axis=0), 0.0)
    return z


def _dot_nn(a, b):
    return jnp.dot(a.astype(BF16), b.astype(BF16), preferred_element_type=F32)


def _dot_nt(a, b):
    return lax.dot_general(a.astype(BF16), b.astype(BF16), (((1,), (1,)), ((), ())), preferred_element_type=F32)


def _dot_tn(a, b):
    return lax.dot_general(a.astype(BF16), b.astype(BF16), (((0,), (0,)), ((), ())), preferred_element_type=F32)


def _gates(hq, hf, lbv):
    hq, hf = hq.astype(F32), hf.astype(F32)
    sq = _sig(hq)
    sf = _sig(hf)
    f = lbv + (1.0 - lbv) * sf
    fc = jnp.maximum(f, 1e-30)
    return hq * sq, sq, sf, f, fc, jnp.log(fc)


DECAY_CAP = 60.0


def _block_ref(c_ref, i, sb):
    if i == 0:
        return jnp.zeros((1, HD), F32)
    return c_ref[sb * i - 1:sb * i, :]


def _block_decay(c_ref, sb):
    spans = [_block_ref(c_ref, i, sb) - c_ref[sb * (i + 1) - 1:sb * (i + 1), :] for i in range(CH // sb)]
    return functools.reduce(jnp.maximum, spans)


def _pair_factors(q_ref, k, c_ref, first, cap, round_bf16, sb):
    nb = CH // sb
    c = c_ref[...]
    zero = jnp.zeros((sb, HD), F32)
    q_groups, k_groups, eqs, eks = [], [], [], []
    for i in range(first, nb):
        blk = slice(sb * i, sb * (i + 1))
        r_i = _block_ref(c_ref, i, sb)
        eq = jnp.exp(jnp.minimum(c_ref[blk, :] - r_i, 0.0))
        ek = jnp.exp(jnp.minimum(r_i - c, cap))
        qi, kei = q_ref[blk, :] * eq, k * ek
        if round_bf16:
            qi, kei = qi.astype(BF16).astype(F32), kei.astype(BF16).astype(F32)
        q_groups.append(jnp.concatenate([zero] * i + [qi] + [zero] * (nb - 1 - i), axis=0))
        k_groups.append(kei)
        eqs.append(eq)
        eks.append(ek)
    return jnp.concatenate(q_groups, axis=1), jnp.concatenate(k_groups, axis=1), eqs, eks


def _pair_mask(rowi, coli, strict, sb):
    return (coli < jnp.bitwise_and(rowi, -sb)) if strict else (coli <= rowi)


def _hgrn_fwd(proj, lb, gn, name, carry=None):
    S = proj.shape[0]
    nch = S // CH
    W = NH * HD

    def body(hq_ref, hf_ref, hi_ref, hg_ref, lb_ref, gn_ref, bin_ref, bint_ref, oraw_ref, st_ref, mild_ref,
             cum_ref, q_s, k_s, c_s, v_s, o_s, state_s, qf_s, kf_s, cf_s):
        state_s[...] = jnp.zeros((NH, HD, HD), F32)
        rowi = lax.broadcasted_iota(jnp.int32, (CH, 1), 0)
        coli = lax.broadcasted_iota(jnp.int32, (1, CH), 1)
        sbi = lax.broadcasted_iota(jnp.int32, (SB, 1), 0)
        gnv = gn_ref[...]

        def gates_pass(n, worst):
            wide, narrow = worst
            rows = pl.ds(pl.multiple_of(n * CH, CH), CH)
            for hh in range(NH):
                lanes = slice(hh * HD, (hh + 1) * HD)
                q, _, _, f, _, logf = _gates(hq_ref[rows, lanes], hf_ref[rows, lanes], lb_ref[:, lanes])
                c = _chunk_cumsum(logf, rowi)
                qf_s[hh, rows, :] = q
                kf_s[hh, rows, :] = 1.0 - f
                cf_s[hh, rows, :] = c
                cum_ref[rows, lanes] = c
                c_s[hh] = c
                wide = jnp.maximum(wide, _block_decay(c_s.at[hh], SB_WIDE))
                narrow = jnp.maximum(narrow, _block_decay(c_s.at[hh], SB))
            return wide, narrow

        def between_chunks(hh, n, rows):
            lanes = slice(hh * HD, (hh + 1) * HD)
            q = qf_s[hh, rows, :]
            k = kf_s[hh, rows, :]
            c = cf_s[hh, rows, :]
            v = hi_ref[rows, lanes].astype(F32)
            q_s[hh] = q
            k_s[hh] = k
            c_s[hh] = c
            v_s[hh] = v
            st = state_s[hh]
            st_ref[hh, n] = st.astype(BF16)
            o_s[hh] = _dot_nt(q * jnp.exp(c), st)
            last = c_s[hh, CH - 1:CH, :]
            state_s[hh] = st * jnp.exp(last) + _dot_tn(v, k * jnp.exp(last - c))

        def pairs_matmul(hh, first, cap, strict, sb):
            qx, kc, _, _ = _pair_factors(q_s.at[hh], k_s[hh], c_s.at[hh], first, cap, False, sb)
            a = jnp.where(_pair_mask(rowi, coli, strict, sb), _dot_nt(qx, kc), 0.0)
            o_s[hh] += _dot_nn(a, v_s[hh])

        def within_chunk_matmul(sb):
            return lambda hh: pairs_matmul(hh, 0, DECAY_CAP, False, sb)

        def within_chunk_exact(hh):
            pairs_matmul(hh, 1, 0.0, True, SB)
            for i in range(CH // SB):
                blk = slice(SB * i, SB * (i + 1))
                qb = q_s[hh, blk, :]
                cb = c_s[hh, blk, :]
                acc = jnp.zeros((SB, HD), F32)
                for s in range(SB):
                    row = SB * i + s
                    w = jnp.exp(jnp.minimum(cb - c_s[hh, row:row + 1, :], 0.0))
                    a_col = jnp.sum(qb * k_s[hh, row:row + 1, :] * w, axis=-1, keepdims=True)
                    acc = acc + jnp.where(sbi >= s, a_col, 0.0) * v_s[hh, row:row + 1, :]
                o_s[hh, blk, :] += acc

        def norm_and_gate(hh, rows):
            lanes = slice(hh * HD, (hh + 1) * HD)
            ov = o_s[hh]
            oraw_ref[rows, lanes] = ov
            r = lax.rsqrt(jnp.mean(ov * ov, axis=-1, keepdims=True) + EPS)
            hg = hg_ref[rows, lanes].astype(F32)
            bin_ref[rows, lanes] = ((ov * r) * gnv * (hg * _sig(hg))).astype(BF16)

        def chunk_with(within_chunk):
            def chunk(n, carry):
                rows = pl.ds(pl.multiple_of(n * CH, CH), CH)
                for hh in range(NH):
                    between_chunks(hh, n, rows)
                for hh in range(NH):
                    within_chunk(hh)
                for hh in range(NH):
                    norm_and_gate(hh, rows)
                return carry
            return chunk

        none = jnp.zeros((1, HD), F32)
        wide, narrow = lax.fori_loop(0, nch, gates_pass, (none, none))
        tier = jnp.where(jnp.max(wide) <= DECAY_CAP, 2.0, jnp.where(jnp.max(narrow) <= DECAY_CAP, 1.0, 0.0))
        mild_ref[...] = jnp.broadcast_to(tier, (8, HD))

        @pl.when(tier == 2.0)
        def _():
            lax.fori_loop(0, nch, chunk_with(within_chunk_matmul(SB_WIDE)), 0, unroll=4)

        @pl.when(tier == 1.0)
        def _():
            lax.fori_loop(0, nch, chunk_with(within_chunk_matmul(SB)), 0, unroll=2)

        @pl.when(tier == 0.0)
        def _():
            lax.fori_loop(0, nch, chunk_with(within_chunk_exact), 0)

        bint_ref[...] = bin_ref[...].astype(F32).T.astype(BF16)

    col = lambda off: pl.BlockSpec((S, W), lambda h: (0, off // NH + h))
    head = pl.BlockSpec((S, W), lambda h: (0, h))
    outs = _call(
        body, name=name, grid=(HEADS // NH,),
        in_specs=[col(HQ0), col(HF0), col(HI0), col(HG0), pl.BlockSpec((1, W), lambda h: (0, h)),
                  pl.BlockSpec((1, HD), lambda h: (0, 0))],
        out_specs=[head, pl.BlockSpec((W, S), lambda h: (h, 0)), head,
                   pl.BlockSpec((NH, nch, HD, HD), lambda h: (h, 0, 0, 0)),
                   pl.BlockSpec((8, HD), lambda h: (h, 0)), head],
        out_shape=[jax.ShapeDtypeStruct((S, D), BF16), jax.ShapeDtypeStruct((D, S), BF16),
                   jax.ShapeDtypeStruct((S, D), F32), jax.ShapeDtypeStruct((HEADS, nch, HD, HD), BF16),
                   jax.ShapeDtypeStruct((8 * HEADS // NH, HD), F32), jax.ShapeDtypeStruct((S, D), F32)],
        scratch_shapes=[pltpu.VMEM((NH, CH, HD), F32)] * 5 + [pltpu.VMEM((NH, HD, HD), F32)]
        + [pltpu.VMEM((NH, S, HD), F32)] * 3,
        sem=("parallel",), args=(proj, proj, proj, proj, lb, gn), carry=carry)
    return outs[:6], outs[6:]


def _hgrn_bwd(dbin, proj, oraw, states, mild, cum, lb, gn, dproj, name, carry=None):
    S = proj.shape[0]
    nch = S // CH
    W = NH * HD
    n_in = 12

    def body(*refs):
        ins, (dproj_ref, dlb_ref, dgn_ref) = refs[:n_in - 1], refs[n_in:n_in + 3]
        scratch, later = refs[n_in + 3:-3], refs[-3:]
        seg = pl.program_id(1)

        @pl.when(seg == 0)
        def _():
            heads(*ins, dproj_ref, *later, dlb_ref, dgn_ref, *scratch)

        for s, kept in enumerate(later):
            @pl.when(seg == s + 1)
            def _(kept=kept):
                dproj_ref[...] = kept[...]

    def heads(db_ref, hq_ref, hf_ref, hi_ref, hg_ref, or_ref, st_ref, mild_ref, cum_ref, lb_ref, gn_ref,
              dq_ref, df_ref, di_ref, dg_ref, dlb_ref, dgn_ref,
              q_s, k_s, c_s, v_s, do_s, dq_s, dk_s, dv_s, dc_s, dqd_s, dkd_s, f_s, sf_s, sq_s, dl_s, dst_s,
              dlb_s, dgn_s):
        dst_s[...] = jnp.zeros((NH, HD, HD), F32)
        dlb_s[...] = jnp.zeros((1, W), F32)
        dgn_s[...] = jnp.zeros((1, HD), F32)
        rowi = lax.broadcasted_iota(jnp.int32, (CH, 1), 0)
        coli = lax.broadcasted_iota(jnp.int32, (1, CH), 1)
        sbi = lax.broadcasted_iota(jnp.int32, (SB, 1), 0)
        gnv = gn_ref[...]
        def between_chunks(hh, n, rows):
            lanes = slice(hh * HD, (hh + 1) * HD)
            lbv = lb_ref[:, lanes]
            hq = hq_ref[rows, lanes].astype(F32)
            sq = _sig(hq)
            sf = _sig(hf_ref[rows, lanes].astype(F32))
            f = lbv + (1.0 - lbv) * sf
            q = hq * sq
            k = 1.0 - f
            f_s[hh] = f
            sf_s[hh] = sf
            sq_s[hh] = sq
            v = hi_ref[rows, lanes].astype(F32)
            c = cum_ref[rows, lanes]
            ov = or_ref[rows, lanes]
            hg = hg_ref[rows, lanes].astype(F32)
            sg = _sig(hg)
            r = lax.rsqrt(jnp.mean(ov * ov, axis=-1, keepdims=True) + EPS)
            dbv = db_ref[rows, lanes]
            d_on = dbv * (hg * sg)
            dg_ref[rows, lanes] = (dbv * ((ov * r) * gnv) * _dsilu(hg, sg)).astype(BF16)
            dgn_s[...] += jnp.sum(d_on * (ov * r), axis=0, keepdims=True)
            u = d_on * gnv
            do = r * u - ov * (r * r * r) * jnp.mean(u * ov, axis=-1, keepdims=True)
            q_s[hh] = q
            k_s[hh] = k
            c_s[hh] = c
            v_s[hh] = v
            do_s[hh] = do
            st = st_ref[hh, n].astype(F32)
            dst = dst_s[hh]
            ec = jnp.exp(c)
            last = c_s[hh, CH - 1:CH, :]
            el = jnp.exp(last - c)
            elast = jnp.exp(last)
            dq = _dot_nn(do, st) * ec
            dk = _dot_nn(v, dst) * el
            dq_s[hh] = dq
            dk_s[hh] = dk
            dv_s[hh] = _dot_nt(k * el, dst)
            dc_s[hh] = q * dq - k * dk
            dl_s[hh] = (jnp.sum(k * dk, axis=0, keepdims=True)
                        + elast * jnp.sum(st * dst, axis=0, keepdims=True))
            dst_s[hh] = dst * elast + _dot_tn(do, q * ec)

        def pairs_matmul(hh, first, cap, strict, sb):
            do = do_s[hh]
            qx, kc, eqs, eks = _pair_factors(q_s.at[hh], k_s[hh], c_s.at[hh], first, cap, True, sb)
            mask = _pair_mask(rowi, coli, strict, sb)
            a = jnp.where(mask, _dot_nt(qx, kc), 0.0)
            d_a = jnp.where(mask, _dot_nt(do, v_s[hh]).astype(BF16).astype(F32), 0.0)
            dqx = _dot_nn(d_a, kc)
            dkc = _dot_tn(d_a, qx)
            dv_s[hh] += _dot_tn(a, do)
            dk, dcum = dk_s[hh], dc_s[hh]
            dq_slabs = [jnp.zeros((sb, HD), F32)] * first
            dc_slabs = [jnp.zeros((sb, HD), F32)] * first
            for g, (eq, ek) in enumerate(zip(eqs, eks)):
                rows = slice(sb * (first + g), sb * (first + g + 1))
                cols = slice(HD * g, HD * (g + 1))
                dq_i = dqx[rows, cols]
                dk_i = dkc[:, cols]
                dq_slabs.append(dq_i * eq)
                dc_slabs.append(qx[rows, cols] * dq_i)
                dk = dk + dk_i * ek
                dcum = dcum - kc[:, cols] * dk_i
            dq_s[hh] += jnp.concatenate(dq_slabs, axis=0)
            dk_s[hh] = dk
            dc_s[hh] = dcum + jnp.concatenate(dc_slabs, axis=0)

        def pairs_exact(hh):
            dqd_s[hh] = jnp.zeros((CH, HD), F32)
            dkd_s[hh] = jnp.zeros((CH, HD), F32)
            for i in range(CH // SB):
                blk = slice(SB * i, SB * (i + 1))
                qb = q_s[hh, blk, :]
                cb = c_s[hh, blk, :]
                dob = do_s[hh, blk, :]
                dq_acc = jnp.zeros((SB, HD), F32)
                for s in range(SB):
                    row = SB * i + s
                    ks = k_s[hh, row:row + 1, :]
                    vs = v_s[hh, row:row + 1, :]
                    w = jnp.exp(jnp.minimum(cb - c_s[hh, row:row + 1, :], 0.0))
                    live = sbi >= s
                    a_col = jnp.where(live, jnp.sum(qb * ks * w, axis=-1, keepdims=True), 0.0)
                    da_col = jnp.where(live, jnp.sum(dob * vs, axis=-1, keepdims=True), 0.0)
                    dq_acc = dq_acc + da_col * ks * w
                    dkd_s[hh, row:row + 1, :] += jnp.sum(da_col * qb * w, axis=0, keepdims=True)
                    dv_s[hh, row:row + 1, :] += jnp.sum(a_col * dob, axis=0, keepdims=True)
                dqd_s[hh, blk, :] += dq_acc
            dq_d = dqd_s[hh]
            dk_d = dkd_s[hh]
            dq_s[hh] += dq_d
            dk_s[hh] += dk_d
            dc_s[hh] += q_s[hh] * dq_d - k_s[hh] * dk_d

        def gate_grads(hh, rows):
            lanes = slice(hh * HD, (hh + 1) * HD)
            lbv = lb_ref[:, lanes]
            hq = hq_ref[rows, lanes].astype(F32)
            f, sf, sq = f_s[hh], sf_s[hh], sq_s[hh]
            dlogf = _chunk_rev_cumsum(dc_s[hh], rowi) + dl_s[hh]
            dfv = jnp.where(f > 1e-30, dlogf / jnp.maximum(f, 1e-30), 0.0) - dk_s[hh]
            dlb_s[:, lanes] += jnp.sum(dfv * (1.0 - sf), axis=0, keepdims=True)
            df_ref[rows, lanes] = (dfv * (1.0 - lbv) * sf * (1.0 - sf)).astype(BF16)
            dq_ref[rows, lanes] = (dq_s[hh] * _dsilu(hq, sq)).astype(BF16)
            di_ref[rows, lanes] = dv_s[hh].astype(BF16)

        def chunk_with(pairs):
            def chunk(j, carry):
                n = nch - 1 - j
                rows = pl.ds(pl.multiple_of(n * CH, CH), CH)
                for hh in range(NH):
                    between_chunks(hh, n, rows)
                for hh in range(NH):
                    pairs(hh)
                for hh in range(NH):
                    gate_grads(hh, rows)
                return carry
            return chunk

        def pairs_mild(sb):
            return lambda hh: pairs_matmul(hh, 0, DECAY_CAP, False, sb)

        def pairs_any(hh):
            pairs_matmul(hh, 1, 0.0, True, SB)
            pairs_exact(hh)

        tier = jnp.max(mild_ref[...])

        @pl.when(tier == 2.0)
        def _():
            lax.fori_loop(0, nch, chunk_with(pairs_mild(SB_WIDE)), 0, unroll=2)

        @pl.when(tier == 1.0)
        def _():
            lax.fori_loop(0, nch, chunk_with(pairs_mild(SB)), 0)

        @pl.when(tier == 0.0)
        def _():
            lax.fori_loop(0, nch, chunk_with(pairs_any), 0)

        dlb_ref[...] = dlb_s[...]
        dgn_ref[...] = jnp.broadcast_to(dgn_s[...], (8, HD))

    ahead = lambda h, s: jnp.minimum(h + jnp.minimum(s, 1), HEADS // NH - 1)
    col = lambda off: pl.BlockSpec((S, W), lambda h, s: (0, off // NH + ahead(h, s)))
    head_in = pl.BlockSpec((S, W), lambda h, s: (0, ahead(h, s)))
    vec_in = pl.BlockSpec((1, W), lambda h, s: (0, ahead(h, s)))
    vec = pl.BlockSpec((1, W), lambda h, s: (0, h))
    seg_w = (HF0 - HQ0) // NH
    outs = _call(
        body, name=name, grid=(HEADS // NH, 4),
        in_specs=[head_in, col(HQ0), col(HF0), col(HI0), col(HG0), head_in,
                  pl.BlockSpec((NH, nch, HD, HD), lambda h, s: (ahead(h, s), 0, 0, 0)),
                  pl.BlockSpec((8, HD), lambda h, s: (ahead(h, s), 0)), head_in, vec_in,
                  pl.BlockSpec((1, HD), lambda h, s: (0, 0)), ANY],
        out_specs=[pl.BlockSpec((S, W), lambda h, s: (0, HQ0 // NH + seg_w * s + h)), vec,
                   pl.BlockSpec((8, HD), lambda h, s: (h, 0))],
        out_shape=[jax.ShapeDtypeStruct(dproj.shape, dproj.dtype), jax.ShapeDtypeStruct((1, D), F32),
                   jax.ShapeDtypeStruct((8 * HEADS // NH, HD), F32)],
        scratch_shapes=[pltpu.VMEM((NH, CH, HD), F32)] * 14
        + [pltpu.VMEM((NH, 1, HD), F32), pltpu.VMEM((NH, HD, HD), F32), pltpu.VMEM((1, W), F32),
           pltpu.VMEM((1, HD), F32)] + [pltpu.VMEM((S, W), BF16)] * 3,
        sem=("arbitrary", "arbitrary"), aliases={n_in - 1: 0},
        args=(dbin, proj, proj, proj, proj, oraw, states, mild, cum, lb, gn, dproj), carry=carry)
    dproj, dlb, dgn = outs[:3]
    return (dproj, dlb, dgn.reshape(HEADS // NH, 8, HD)[:, 0, :]), outs[3:]


def _lower_bounds(l0, l1):
    m = jnp.maximum(l0, l1)
    e0 = jnp.exp(l0 - m)
    e1 = jnp.exp(l1 - m)
    tot = e0 + e1
    p0 = e0 / tot
    p1 = e1 / tot
    return jnp.clip(p0 - p0, 0.0, 1.0), jnp.clip((p0 + p1) - p0, 0.0, 1.0)


def _lb_fwd(logits):
    def body(l_ref, o_ref):
        lb0, lb1 = _lower_bounds(l_ref[0:1, :], l_ref[1:2, :])
        o_ref[0:1, :] = lb0
        o_ref[1:2, :] = lb1

    return pl.pallas_call(body, name="lb_fwd", out_shape=jax.ShapeDtypeStruct((2, D), F32))(logits)


def _lb_bwd(logits, dlb):
    def body(l_ref, d_ref, o_ref):
        _, vjp = jax.vjp(_lower_bounds, l_ref[0:1, :], l_ref[1:2, :])
        g0, g1 = vjp((d_ref[0:1, :], d_ref[1:2, :]))
        o_ref[0:1, :] = g0
        o_ref[1:2, :] = g1

    return pl.pallas_call(body, name="lb_bwd", out_shape=jax.ShapeDtypeStruct((2, D), F32))(logits, dlb)


ADA_PAD = 128


def _ada_fwd(c_pad, w_ada, b_sh):
    ns = w_ada.shape[2]

    def body(c_ref, w_ref, b_ref, o_ref):
        cv = c_ref[...]
        ca = (cv * _sig(cv)).astype(BF16)
        for l in range(2):
            res = jnp.dot(ca, w_ref[l].astype(BF16), preferred_element_type=F32)
            o_ref[:, l * ns:(l + 1) * ns] = res[0:NDEV, :] + b_ref[l:l + 1, :]

    return pl.pallas_call(body, name="ada_fwd", out_shape=jax.ShapeDtypeStruct((NDEV, 2 * ns), F32),
                          compiler_params=_cp())(c_pad, w_ada, b_sh)


def _ada_wgrad(c_pad_t, d_ada_sh):
    ns = d_ada_sh.shape[2]

    def body(c_ref, d_ref, o_ref):
        cv = c_ref[...]
        ca = (cv * _sig(cv)).astype(BF16)
        for l in range(2):
            o_ref[l] = jnp.dot(ca, d_ref[l].astype(BF16), preferred_element_type=F32)

    return pl.pallas_call(body, name="ada_wgrad", out_shape=jax.ShapeDtypeStruct((2, D, ns), F32),
                          compiler_params=_cp())(c_pad_t, d_ada_sh)


def _sum_devices(g):
    _, R, C = g.shape

    def body(g_ref, o_ref):
        acc = g_ref[0]
        for d in range(1, NDEV):
            acc = acc + g_ref[d]
        o_ref[...] = acc

    return pl.pallas_call(body, name="sum_devices", out_shape=jax.ShapeDtypeStruct((R, C), F32),
                          compiler_params=_cp())(g)


def _adamw(w, g, m, v, name, carry=None):
    R, C = w.shape
    tr = _row_tile(R, max(8, (1 << 19) // C))

    def body(w_ref, g_ref, m_ref, v_ref, d_ref, nm_ref, nv_ref):
        d_ref[...], nm_ref[...], nv_ref[...] = _adamw_update(w_ref[...], g_ref[...], m_ref[...], v_ref[...])

    tile = pl.BlockSpec((tr, C), lambda i: (i, 0))
    return _call(body, name=name, grid=(R // tr,), in_specs=[tile] * 4, out_specs=[tile] * 3,
                 out_shape=[jax.ShapeDtypeStruct((R, C), F32)] * 3, sem=("parallel",), args=(w, g, m, v),
                 carry=carry)


def _adamw_update(w, g, m, v):
    nm = B1 * m + (1.0 - B1) * g
    nv = B2 * v + (1.0 - B2) * (g * g)
    m_hat = nm / (1.0 - B1 ** STEP)
    v_hat = nv / (1.0 - B2 ** STEP)
    return -LR * (m_hat / (jnp.sqrt(v_hat) + AEPS) + WD * w), nm, nv


SMALL_PARTS = (("b_ada", 0, 6, D), ("g_pre", 8, 2, D), ("g_post", 16, 2, D), ("lb_logits", 24, 2, D),
               ("pool_w", 32, 128, D), ("pool_scale", 160, 1, D), ("hgrn_norm_g", 168, 1, 2 * HD))


def _adamw_small(g_small, g_lb_logits, wmv):
    n = len(SMALL_PARTS)

    def body(g_ref, glb_ref, *refs):
        ins, outs = refs[:3 * n], refs[3 * n:]
        for p, (key, row0, rows, width) in enumerate(SMALL_PARTS):
            gv = glb_ref[...] if key == "lb_logits" else g_ref[row0:row0 + rows, 0:width]
            res = _adamw_update(ins[3 * p][...], gv, ins[3 * p + 1][...], ins[3 * p + 2][...])
            for t in range(3):
                outs[3 * p + t][...] = res[t]

    flat = [t for triple in wmv for t in triple]
    outs = pl.pallas_call(body, name="adamw_small",
                          out_shape=[jax.ShapeDtypeStruct(t.shape, F32) for t in flat],
                          compiler_params=_cp())(g_small, g_lb_logits, *flat)
    return [outs[3 * p:3 * p + 3] for p in range(n)]


def _cast_to_slot(place, w, l, name):
    _, R, C = w.shape
    tr = _row_tile(R, max(8, (1 << 19) // C))

    def body(p_ref, w_ref, o_ref):
        o_ref[...] = w_ref[...].astype(BF16)

    return pl.pallas_call(
        body, name=name, out_shape=jax.ShapeDtypeStruct((NCHIP, R, C), BF16),
        grid_spec=pltpu.PrefetchScalarGridSpec(
            num_scalar_prefetch=1, grid=(R // tr,),
            in_specs=[pl.BlockSpec((None, tr, C), lambda i, p_ref: (l, i, 0))],
            out_specs=pl.BlockSpec((None, tr, C), lambda i, p_ref: (p_ref[0], i, 0))),
        compiler_params=_cp(("parallel",)),
    )(place, w)


def _cast_to_slots(place, ws, name):
    n = len(ws)

    def body(p_ref, *refs):
        for w_ref, o_ref in zip(refs[:n], refs[n:]):
            o_ref[...] = w_ref[...].astype(BF16)

    def layer(l):
        return lambda i, p_ref: (l, 0, 0)

    return pl.pallas_call(
        body, name=name, out_shape=[jax.ShapeDtypeStruct((NCHIP,) + w.shape[1:], BF16) for w, _ in ws],
        grid_spec=pltpu.PrefetchScalarGridSpec(
            num_scalar_prefetch=1, grid=(1,),
            in_specs=[pl.BlockSpec((None,) + w.shape[1:], layer(l)) for w, l in ws],
            out_specs=[pl.BlockSpec((None,) + w.shape[1:], lambda i, p_ref: (p_ref[0], 0, 0)) for w, _ in ws]),
        compiler_params=_cp(("arbitrary",)),
    )(place, *[w for w, _ in ws])


def _pair_adds(core, gs, gots, name):
    n = len(gs)

    def body(c_ref, *refs):
        for a_ref, b_ref, o_ref in zip(refs[:n], refs[n:2 * n], refs[2 * n:]):
            o_ref[...] = (a_ref[...].astype(F32) + b_ref[...].astype(F32)).astype(o_ref.dtype)

    def whole(t):
        return pl.BlockSpec(t.shape, lambda i, c_ref: (0, 0, 0))

    return pl.pallas_call(
        body, name=name, out_shape=[jax.ShapeDtypeStruct(t.shape, BF16) for t in gots],
        grid_spec=pltpu.PrefetchScalarGridSpec(
            num_scalar_prefetch=1, grid=(1,),
            in_specs=[pl.BlockSpec(t.shape, lambda i, c_ref: (0, c_ref[0], 0)) for t in gots]
            + [whole(t) for t in gots],
            out_specs=[whole(t) for t in gots]),
        compiler_params=_cp(("arbitrary",)),
    )(core, *gs, *gots)


def _pair_add(core, g, got, name):
    _, R, C = g.shape
    r2 = R // 2
    tr = _row_tile(r2, max(8, (1 << 19) // C))
    nt = r2 // tr

    def body(c_ref, a_ref, b_ref, o_ref):
        o_ref[...] = (a_ref[...].astype(F32) + b_ref[...].astype(F32)).astype(o_ref.dtype)

    return pl.pallas_call(
        body, name=name, out_shape=jax.ShapeDtypeStruct((NCHIP, r2, C), BF16),
        grid_spec=pltpu.PrefetchScalarGridSpec(
            num_scalar_prefetch=1, grid=(NCHIP, nt),
            in_specs=[pl.BlockSpec((None, tr, C), lambda j, i, c_ref: (j, c_ref[0] * nt + i, 0)),
                      pl.BlockSpec((None, tr, C), lambda j, i, c_ref: (j, i, 0))],
            out_specs=pl.BlockSpec((None, tr, C), lambda j, i, c_ref: (j, i, 0))),
        compiler_params=_cp(("parallel", "parallel")),
    )(core, g, got)


def _sum_in_chip_order(me, own_ref, r_ref):
    own = own_ref[...].astype(F32)
    acc = None
    for j in range(NCHIP):
        slot = jnp.minimum(jnp.where(j > me, j - 1, j), NCHIP - 2)
        term = jnp.where(me == j, own, r_ref[slot].astype(F32))
        acc = term if acc is None else acc + term
    return acc


def _chip_sums(place, parts, recvs, name):
    n = len(parts)

    def body(p_ref, *refs):
        for a in range(n):
            for l in range(2):
                refs[4 * n + a][l] = _sum_in_chip_order(p_ref[0], refs[2 * a + l], refs[2 * n + 2 * a + l])

    def own(t):
        return pl.BlockSpec((None,) + t.shape[1:], lambda i, p_ref: (p_ref[0], 0, 0))

    def whole(t):
        return pl.BlockSpec(t.shape, lambda i, p_ref: (0, 0, 0))

    flat_p = [t for pair in parts for t in pair]
    flat_r = [t for pair in recvs for t in pair]
    return pl.pallas_call(
        body, name=name,
        out_shape=[jax.ShapeDtypeStruct((2, 2 * p[0].shape[1], p[0].shape[2]), F32) for p in parts],
        grid_spec=pltpu.PrefetchScalarGridSpec(
            num_scalar_prefetch=1, grid=(1,),
            in_specs=[own(t) for t in flat_p] + [whole(t) for t in flat_r],
            out_specs=[pl.BlockSpec((2,) + p[0].shape[1:], lambda i, p_ref: (0, p_ref[1], 0)) for p in parts]),
        compiler_params=_cp(("arbitrary",)),
    )(place, *flat_p, *flat_r)


def _chip_sum(place, part, recv, layer, both, name):
    _, r2, C = part.shape
    tr = _row_tile(r2, max(8, (1 << 18) // C))
    nt = r2 // tr

    def body(p_ref, own_ref, r_ref, *rest):
        rest[-1][...] = _sum_in_chip_order(p_ref[0], own_ref, r_ref)

    args = (place, part, recv) if both is None else (place, part, recv, both)
    return pl.pallas_call(
        body, name=name, out_shape=jax.ShapeDtypeStruct((2, 2 * r2, C), F32),
        grid_spec=pltpu.PrefetchScalarGridSpec(
            num_scalar_prefetch=1, grid=(nt,),
            in_specs=[pl.BlockSpec((None, tr, C), lambda i, p_ref: (p_ref[0], i, 0)),
                      pl.BlockSpec((NCHIP - 1, tr, C), lambda i, p_ref: (0, i, 0))] + [ANY] * (len(args) - 3),
            out_specs=pl.BlockSpec((None, tr, C), lambda i, p_ref: (layer, p_ref[1] * nt + i, 0))),
        input_output_aliases={} if both is None else {3: 0},
        compiler_params=_cp(("parallel",)),
    )(*args)


def _place():
    x, y, c = lax.axis_index("x"), lax.axis_index("y"), lax.axis_index("c")
    chips = [(1 - x, y), (x, 1 - y), (1 - x, 1 - y)]
    return x, y, c, chips


def _gather_small(blk, name):
    m_per, n = blk.shape

    def body(x_ref, out_ref, send_sems, recv_sems, local_sem):
        x, y, c, chips = _place()
        me, sibling = (x, y, c), (x, y, 1 - c)

        def rows(px, py, pc):
            return out_ref.at[pl.ds((4 * px + 2 * py + pc) * m_per, m_per), :]

        def copy(k, block, to, src=None):
            return pltpu.make_async_remote_copy(
                src_ref=rows(*block) if src is None else src, dst_ref=rows(*block),
                send_sem=send_sems.at[k], recv_sem=recv_sems.at[k], device_id=to, device_id_type=MESH)

        mine = pltpu.make_async_copy(x_ref, rows(*me), local_sem)
        mine.start()
        first = [copy(0, me, sibling, src=x_ref)]
        first += [copy(1 + j, me, (*chip, c), src=x_ref) for j, chip in enumerate(chips)]
        for cp in first:
            cp.start()
        passed = [copy(4 + j, (*chip, c), sibling) for j, chip in enumerate(chips)]
        for j, chip in enumerate(chips):
            copy(1 + j, (*chip, c), me).wait_recv()
            passed[j].start()
        copy(0, sibling, me).wait_recv()
        for j, chip in enumerate(chips):
            copy(4 + j, (*chip, 1 - c), me).wait_recv()
        for cp in first + passed:
            cp.wait_send()
        mine.wait()

    return pl.pallas_call(
        body, name=name, out_shape=jax.ShapeDtypeStruct((NDEV * m_per, n), blk.dtype),
        in_specs=[pl.BlockSpec(memory_space=pltpu.VMEM)], out_specs=pl.BlockSpec(memory_space=pltpu.VMEM),
        scratch_shapes=[pltpu.SemaphoreType.DMA((7,)), pltpu.SemaphoreType.DMA((7,)), pltpu.SemaphoreType.DMA],
        compiler_params=_cp(),
    )(blk)


def _gather_rows_carry(blk):
    m_per, n = blk.shape

    def rows(ref, px, py, pc):
        return ref.at[pl.ds((4 * px + 2 * py + pc) * m_per, m_per), :]

    def copy(ins, outs, send_sems, recv_sems, k, block, to, own=False):
        return pltpu.make_async_remote_copy(
            src_ref=ins[0] if own else rows(outs[0], *block), dst_ref=rows(outs[0], *block),
            send_sem=send_sems.at[k], recv_sem=recv_sems.at[k], device_id=to, device_id_type=MESH)

    def mine(ins, outs, send_sems):
        x, y, c, _ = _place()
        return pltpu.make_async_copy(ins[0], rows(outs[0], x, y, c), send_sems.at[7])

    def start(ins, outs, send_sems, recv_sems):
        x, y, c, chips = _place()
        mine(ins, outs, send_sems).start()
        copy(ins, outs, send_sems, recv_sems, 0, (x, y, c), (x, y, 1 - c), own=True).start()
        for j, chip in enumerate(chips):
            copy(ins, outs, send_sems, recv_sems, 1 + j, (x, y, c), (*chip, c), own=True).start()

    def finish(ins, outs, send_sems, recv_sems):
        x, y, c, chips = _place()
        for j, chip in enumerate(chips):
            copy(ins, outs, send_sems, recv_sems, 1 + j, (*chip, c), (x, y, c)).wait_recv()
            copy(ins, outs, send_sems, recv_sems, 4 + j, (*chip, c), (x, y, 1 - c)).start()
        copy(ins, outs, send_sems, recv_sems, 0, (x, y, 1 - c), (x, y, c)).wait_recv()
        for j, chip in enumerate(chips):
            copy(ins, outs, send_sems, recv_sems, 4 + j, (*chip, 1 - c), (x, y, c)).wait_recv()
        copy(ins, outs, send_sems, recv_sems, 0, (x, y, c), (x, y, 1 - c), own=True).wait_send()
        for j, chip in enumerate(chips):
            copy(ins, outs, send_sems, recv_sems, 1 + j, (x, y, c), (*chip, c), own=True).wait_send()
            copy(ins, outs, send_sems, recv_sems, 4 + j, (*chip, c), (x, y, 1 - c)).wait_send()
        mine(ins, outs, send_sems).wait()

    return _Carry([blk], [jax.ShapeDtypeStruct((NDEV * m_per, n), blk.dtype)], {}, 8, start, finish)


def _gather_carry(shards, piece=(0, 1, 1)):
    n = len(shards)
    first, count, of = piece

    def rows(ref, half):
        r2 = ref.shape[1] // 2
        return pl.ds(half * r2 + first * (r2 // of), count * (r2 // of))

    def over_ici(outs, send_sems, recv_sems, a, j, chip_xy, slot):
        x, y, c, _ = _place()
        blk = outs[a].at[slot, rows(outs[a], c), :]
        return pltpu.make_async_remote_copy(
            src_ref=blk, dst_ref=blk, send_sem=send_sems.at[6 * a + j], recv_sem=recv_sems.at[6 * a + j],
            device_id=(*chip_xy, c), device_id_type=MESH)

    def over_d2d(outs, send_sems, recv_sems, a, j, slot, half):
        x, y, c, _ = _place()
        blk = outs[a].at[slot, rows(outs[a], half), :]
        return pltpu.make_async_remote_copy(
            src_ref=blk, dst_ref=blk, send_sem=send_sems.at[6 * a + 3 + j], recv_sem=recv_sems.at[6 * a + 3 + j],
            device_id=(x, y, 1 - c), device_id_type=MESH)

    def start(ins, outs, send_sems, recv_sems):
        x, y, c, chips = _place()
        for a in range(n):
            for j, chip_xy in enumerate(chips):
                over_ici(outs, send_sems, recv_sems, a, j, chip_xy, 2 * x + y).start()

    def finish(ins, outs, send_sems, recv_sems):
        x, y, c, chips = _place()
        for a in range(n):
            for j, (cx, cy) in enumerate(chips):
                over_ici(outs, send_sems, recv_sems, a, j, (cx, cy), 2 * cx + cy).wait_recv()
                over_d2d(outs, send_sems, recv_sems, a, j, 2 * cx + cy, c).start()
        for a in range(n):
            for j, (cx, cy) in enumerate(chips):
                over_d2d(outs, send_sems, recv_sems, a, j, 2 * cx + cy, 1 - c).wait_recv()
        for a in range(n):
            for j, (cx, cy) in enumerate(chips):
                over_ici(outs, send_sems, recv_sems, a, j, (cx, cy), 2 * x + y).wait_send()
                over_d2d(outs, send_sems, recv_sems, a, j, 2 * cx + cy, c).wait_send()

    return _Carry(shards, [jax.ShapeDtypeStruct(s.shape, s.dtype) for s in shards],
                  {a: a for a in range(n)}, 6 * n, start, finish)


def _rs_pair(grads, name):
    n = len(grads)

    def body(*refs):
        ins, gots = refs[:n], refs[n:2 * n]
        send_sems, recv_sems = refs[2 * n:]
        x, y, c, _ = _place()
        cps = []
        for a in range(n):
            r2 = ins[a].shape[1] // 2
            cp = pltpu.make_async_remote_copy(
                src_ref=ins[a].at[:, pl.ds((1 - c) * r2, r2), :], dst_ref=gots[a],
                send_sem=send_sems.at[a], recv_sem=recv_sems.at[a],
                device_id=(x, y, 1 - c), device_id_type=MESH)
            cp.start()
            cps.append(cp)
        for cp in cps:
            cp.wait()

    half = [jax.ShapeDtypeStruct((NCHIP, g.shape[1] // 2, g.shape[2]), g.dtype) for g in grads]
    return pl.pallas_call(
        body, name=name, out_shape=half, in_specs=[ANY] * n, out_specs=[ANY] * n,
        scratch_shapes=[pltpu.SemaphoreType.DMA((n,)), pltpu.SemaphoreType.DMA((n,))],
        compiler_params=_cp(),
    )(*grads)


def _pair_carry(grads):
    n = len(grads)

    def copy(ins, outs, send_sems, recv_sems, a):
        x, y, c, _ = _place()
        r2 = ins[a].shape[1] // 2
        return pltpu.make_async_remote_copy(
            src_ref=ins[a].at[:, pl.ds((1 - c) * r2, r2), :], dst_ref=outs[a],
            send_sem=send_sems.at[a], recv_sem=recv_sems.at[a],
            device_id=(x, y, 1 - c), device_id_type=MESH)

    def start(ins, outs, send_sems, recv_sems):
        for a in range(n):
            copy(ins, outs, send_sems, recv_sems, a).start()

    def finish(ins, outs, send_sems, recv_sems):
        for a in range(n):
            copy(ins, outs, send_sems, recv_sems, a).wait()

    half = [jax.ShapeDtypeStruct((NCHIP, g.shape[1] // 2, g.shape[2]), g.dtype) for g in grads]
    return _Carry(grads, half, {}, n, start, finish)


def _chips_carry(parts, piece=(0, 1, 1), into=None):
    n = len(parts)
    first, count, of = piece

    def rows(ref):
        step = ref.shape[1] // of
        return pl.ds(first * step, count * step)

    def send(ins, outs, send_sems, recv_sems, a, j, chip_xy):
        x, y, c, _ = _place()
        me, them = 2 * x + y, 2 * chip_xy[0] + chip_xy[1]
        return pltpu.make_async_remote_copy(
            src_ref=ins[a].at[them, rows(ins[a]), :],
            dst_ref=outs[a].at[me - (me > them).astype(jnp.int32), rows(outs[a]), :],
            send_sem=send_sems.at[3 * a + j], recv_sem=recv_sems.at[3 * a + j],
            device_id=(*chip_xy, c), device_id_type=MESH)

    def start(ins, outs, send_sems, recv_sems):
        _, _, _, chips = _place()
        for a in range(n):
            for j, chip_xy in enumerate(chips):
                send(ins, outs, send_sems, recv_sems, a, j, chip_xy).start()

    def finish(ins, outs, send_sems, recv_sems):
        x, y, c, chips = _place()
        me = 2 * x + y
        for a in range(n):
            for j, (cx, cy) in enumerate(chips):
                them = 2 * cx + cy
                blk = outs[a].at[them - (them > me).astype(jnp.int32), rows(outs[a]), :]
                pltpu.make_async_remote_copy(
                    src_ref=blk, dst_ref=blk, send_sem=send_sems.at[3 * a + j], recv_sem=recv_sems.at[3 * a + j],
                    device_id=(cx, cy, c), device_id_type=MESH).wait_recv()
        for a in range(n):
            for j, chip_xy in enumerate(chips):
                send(ins, outs, send_sems, recv_sems, a, j, chip_xy).wait_send()

    landing = [jax.ShapeDtypeStruct((NCHIP - 1,) + p.shape[1:], p.dtype) for p in parts]
    if into is None:
        return _Carry(parts, landing, {}, 3 * n, start, finish)
    return _Carry(list(parts) + list(into), landing, {n + a: a for a in range(n)}, 3 * n, start, finish)


def _rs_swap(fulls):
    n = len(fulls)

    def body(*refs):
        outs = refs[n:2 * n]
        send_sems, recv_sems = refs[2 * n:]
        x, y, c, _ = _place()
        cps = []
        for a in range(n):
            r2 = outs[a].shape[1] // 2
            mine = outs[a].at[:, pl.ds(c * r2, r2), :]
            cp = pltpu.make_async_remote_copy(
                src_ref=mine, dst_ref=mine, send_sem=send_sems.at[a], recv_sem=recv_sems.at[a],
                device_id=(x, y, 1 - c), device_id_type=MESH)
            cp.start()
            cps.append(cp)
        for a in range(n):
            r2 = outs[a].shape[1] // 2
            blk = outs[a].at[:, pl.ds((1 - c) * r2, r2), :]
            pltpu.make_async_remote_copy(
                src_ref=blk, dst_ref=blk, send_sem=send_sems.at[a], recv_sem=recv_sems.at[a],
                device_id=(x, y, 1 - c), device_id_type=MESH).wait_recv()
        for cp in cps:
            cp.wait_send()

    return pl.pallas_call(
        body, name="rs_swap", out_shape=[jax.ShapeDtypeStruct(f.shape, f.dtype) for f in fulls],
        in_specs=[ANY] * n, out_specs=[ANY] * n, input_output_aliases={a: a for a in range(n)},
        scratch_shapes=[pltpu.SemaphoreType.DMA((n,)), pltpu.SemaphoreType.DMA((n,))],
        compiler_params=_cp(),
    )(*fulls)


def _tail_weight_grads(merged_t, b_in_t, a_in_t, dy, dbr_b, dbr_a, name, tn=256):
    S = dy.shape[0]
    nn = D // tn

    def body(mt_ref, bt_ref, at_ref, dy_ref, db_ref, da_ref, go_ref, gh_ref, gp_ref):
        go_ref[...] = jnp.dot(mt_ref[...], dy_ref[...], preferred_element_type=F32).astype(BF16)
        gh_ref[...] = jnp.dot(bt_ref[...], db_ref[...], preferred_element_type=F32).astype(BF16)
        gp_ref[...] = jnp.dot(at_ref[...], da_ref[...], preferred_element_type=F32).astype(BF16)

    left = lambda rows: pl.BlockSpec((rows, S), lambda n: (0, 0))
    right = pl.BlockSpec((S, tn), lambda n: (0, n))
    out = pl.BlockSpec((D, tn), lambda n: (0, n))
    return pl.pallas_call(
        body, name=name, grid=(nn,), in_specs=[left(D), left(D), left(POOL_W), right, right, right],
        out_specs=[out, out, pl.BlockSpec((None, POOL_W, tn), lambda n: (n, 0, 0))],
        out_shape=[jax.ShapeDtypeStruct((D, D), BF16), jax.ShapeDtypeStruct((D, D), BF16),
                   jax.ShapeDtypeStruct((NCHIP, POOL_W, D // NCHIP), BF16)],
        compiler_params=_cp(("parallel",)),
    )(merged_t, b_in_t, a_in_t, dy, dbr_b, dbr_a)


class _GatherInProj:
    def __init__(self, slot, order):
        self.slot, self.order = slot, order


def _proj_with_gather(h, w_slot, order, name, tn=256):
    S, K = h.shape
    nsh, _, ns = w_slot.shape
    tps = ns // tn
    nt = nsh * tps
    r2 = K // 2

    def body(ord_ref, h_ref, w_in_ref, o_ref, w_ref, wbuf, tile_sems, send_sems, recv_sems):
        n = pl.program_id(0)
        x, y, c, chips = _place()

        def half(slot, which):
            return w_ref.at[slot, pl.ds(which * r2, r2), :]

        def over_ici(j, slot):
            blk = half(slot, c)
            return pltpu.make_async_remote_copy(src_ref=blk, dst_ref=blk, send_sem=send_sems.at[j],
                                                recv_sem=recv_sems.at[j], device_id=(*chips[j], c),
                                                device_id_type=MESH)

        def over_d2d(j, which):
            blk = half(2 * chips[j][0] + chips[j][1], which)
            return pltpu.make_async_remote_copy(src_ref=blk, dst_ref=blk, send_sem=send_sems.at[3 + j],
                                                recv_sem=recv_sems.at[3 + j], device_id=(x, y, 1 - c),
                                                device_id_type=MESH)

        def tile_copy(step, slot):
            shard = ord_ref[step // tps]
            return pltpu.make_async_copy(w_ref.at[shard, :, pl.ds((step % tps) * tn, tn)], wbuf.at[slot],
                                         tile_sems.at[slot])

        @pl.when(n == 0)
        def _():
            for j in range(3):
                over_ici(j, 2 * x + y).start()
            tile_copy(0, 0).start()

        for j in range(3):
            @pl.when(n == (j + 1) * tps - 1)
            def _(j=j):
                over_ici(j, 2 * chips[j][0] + chips[j][1]).wait_recv()
                over_d2d(j, c).start()
                over_d2d(j, 1 - c).wait_recv()

        @pl.when(n + 1 < nt)
        def _():
            tile_copy(n + 1, (n + 1) % 2).start()

        tile_copy(n, n % 2).wait()
        o_ref[...] = jnp.dot(h_ref[...], wbuf[n % 2], preferred_element_type=F32).astype(o_ref.dtype)

        @pl.when(n == nt - 1)
        def _():
            for j in range(3):
                over_ici(j, 2 * x + y).wait_send()
                over_d2d(j, c).wait_send()

    return pl.pallas_call(
        body, name=name,
        out_shape=[jax.ShapeDtypeStruct((S, nsh * ns), BF16), jax.ShapeDtypeStruct(w_slot.shape, w_slot.dtype)],
        grid_spec=pltpu.PrefetchScalarGridSpec(
            num_scalar_prefetch=1, grid=(nt,),
            in_specs=[pl.BlockSpec((S, K), lambda n, o_ref: (0, 0)), ANY],
            out_specs=[pl.BlockSpec((S, tn), lambda n, o_ref: (0, o_ref[n // tps] * tps + n % tps)), ANY],
            scratch_shapes=[pltpu.VMEM((2, K, tn), w_slot.dtype), pltpu.SemaphoreType.DMA((2,)),
                            pltpu.SemaphoreType.DMA((6,)), pltpu.SemaphoreType.DMA((6,))]),
        input_output_aliases={2: 1},
        compiler_params=_cp(("arbitrary",)),
    )(order, h, w_slot)


def _mm_ride(a, b, carry, **kw):
    if carry is None:
        return _mm(a, b, **kw), []
    return _mm(a, b, carry=carry, **kw)


def _layer_fwd(l, x, ada, w, small, ride, target=None):
    shift, scale, gate = ada[:, 0:D], ada[:, D:2 * D], ada[:, 2 * D:3 * D]
    carry, landed = ride("prenorm")
    (h, h_t), outs = _prenorm_fwd(x, small["g_pre"][l], scale, shift, f"prenorm_fwd{l}", carry)
    landed(outs)
    carry, landed = ride("proj")
    if isinstance(carry, _GatherInProj):
        proj, full = _proj_with_gather(h, carry.slot, carry.order, f"proj{l}")
        outs = [full]
    else:
        proj, outs = _mm_ride(h, w["w_in"][l], carry, name=f"proj{l}", b_mode="nn_sh", tm=2048, out_dtype=BF16)
    landed(outs)
    a_in, a_in_t = _pool_fwd(proj, small["pool_w"][l], small["pool_scale"][l], f"pool_fwd{l}")
    carry, landed = ride("hgrn")
    (b_in, b_in_t, o_raw, states, mild, cum), outs = _hgrn_fwd(proj, small["lb"][l], small["hgrn_norm_g"][l],
                                                              f"hgrn_fwd{l}", carry=carry)
    landed(outs)
    carry, landed = ride("tail")
    (br_a, br_b, merged_t, y, *x_new), outs = _layer_tail_fwd(
        proj, a_in, b_in, x, w["w_pool_o"][l], w["w_hgrn_o"][l].reshape(D, D), w["w_out"][l].reshape(D, D),
        gate, small["g_post"][l], f"tail_fwd{l}", target=target, carry=carry)
    landed(outs)
    saved = dict(x=x, h_t=h_t, proj=proj, a_in_t=a_in_t, b_in_t=b_in_t, o_raw=o_raw, states=states, mild=mild,
                 cum=cum,
                 br_a=br_a, br_b=br_b, merged_t=merged_t, y=y, scale=scale, gate=gate)
    return x_new, saved


def _layer_bwd(l, dxn, sv, w, small, ride):
    carry, landed = ride["head"](None)
    (dy, dbr_a, dbr_b, dproj, da_in, db_in, dgate, dg_post), outs = _layer_head_bwd(
        dxn, sv["y"], sv["proj"], sv["br_a"], sv["br_b"], w["w_pool_o"][l], w["w_hgrn_o"][l].reshape(D, D),
        w["w_out"][l].reshape(D, D), sv["gate"], small["g_post"][l], f"head_bwd{l}", carry)
    landed(outs)
    gw_out, gw_hgrn_o, gw_pool_o = _tail_weight_grads(sv["merged_t"], sv["b_in_t"], sv["a_in_t"], dy, dbr_b,
                                                      dbr_a, f"gw_tail{l}")
    big = dict(w_pool_o=gw_pool_o, w_hgrn_o=gw_hgrn_o.reshape(NCHIP, D // NCHIP, D),
               w_out=gw_out.reshape(NCHIP, D // NCHIP, D))
    carry, landed = ride["hgrn"](big)
    (dproj, dlb, dgn), outs = _hgrn_bwd(db_in, sv["proj"], sv["o_raw"], sv["states"], sv["mild"], sv["cum"],
                                        small["lb"][l], small["hgrn_norm_g"][l], dproj, f"hgrn_bwd{l}",
                                        carry=carry)
    landed(outs)
    dproj, dpw, dpsc = _pool_bwd(da_in, sv["proj"], small["pool_w"][l], small["pool_scale"][l], dproj,
                                 f"pool_bwd{l}")
    little = dict(dgate=dgate, g_post=dg_post, pool_w=dpw, pool_scale=dpsc, lb=dlb,
                  hgrn_norm_g=jnp.sum(dgn, axis=0, keepdims=True))
    carry, landed = ride["gw_in"](little)
    big["w_in"], outs = _mm_ride(sv["h_t"], dproj, carry, name=f"gw_in{l}", out_shards=NCHIP, out_dtype=BF16)
    landed(outs)
    carry, landed = ride["d_h"](big)
    dh, outs = _mm_ride(dproj, w["w_in"][l], carry, name=f"d_h{l}", b_mode="nt_shk", tn=1024)
    landed(outs)
    carry, landed = ride["prenorm"](big)
    (dx, dshift, dscale, dg_pre), outs = _prenorm_bwd(dh, dxn, sv["x"], small["g_pre"][l], sv["scale"],
                                                      f"prenorm_bwd{l}", carry)
    landed(outs)
    little.update(dshift=dshift, dscale=dscale, g_pre=dg_pre)
    return dx, big, little


SMALL_ROWS = 176


def _rows8(t):
    t = t.reshape(-1, D)
    return jnp.pad(t, ((0, -t.shape[0] % 8), (0, 0)))


def _pack_small(parts):
    row_keys = ("dshift", "dscale", "dgate", "g_pre", "g_post", "lb", "pool_scale", "hgrn_norm_g")
    flat = [p[k] for p in parts for k in row_keys] + [p["pool_w"].reshape(GROUPS * 128 * 128 // D, D) for p in parts]
    nk = len(row_keys)

    def body(*refs):
        o_ref = refs[-1]
        o_ref[...] = jnp.zeros((SMALL_ROWS, D), F32)
        for l in range(2):
            dshift, dscale, dgate, g_pre, g_post, lb, pscale, gn = refs[l * nk:(l + 1) * nk]
            for r, ref in enumerate((dshift, dscale, dgate)):
                o_ref[3 * l + r:3 * l + r + 1, :] = ref[...]
            o_ref[8 + l:9 + l, :] = g_pre[...]
            o_ref[16 + l:17 + l, :] = g_post[...]
            o_ref[24 + l:25 + l, :] = lb[...]
            o_ref[160:161, l * POOL_W:(l + 1) * POOL_W] = pscale[...]
            o_ref[168:169, l * HD:(l + 1) * HD] = gn[...]
            pw = refs[2 * nk + l]
            rows = pw.shape[0]
            o_ref[32 + l * rows:32 + (l + 1) * rows, :] = pw[...]

    return pl.pallas_call(body, name="pack_small", out_shape=jax.ShapeDtypeStruct((SMALL_ROWS, D), F32),
                          compiler_params=_cp())(*flat)


def _unpack_small(p):
    return (p[0:6].reshape(2, 3 * D), p[8:10], p[16:18], p[24:26], p[32:160].reshape(2, GROUPS, 128, 128),
            p[160:161].reshape(2, POOL_W), p[168:169, 0:2 * HD].reshape(2, HD))


def kernel(x, c, w_ada, b_ada, g_pre, g_post, w_in, pool_w, pool_scale, lb_logits, hgrn_norm_g, w_pool_o, w_hgrn_o, w_out, loss_target, m_w_ada, m_b_ada, m_g_pre, m_g_post, m_w_in, m_pool_w, m_pool_scale, m_lb_logits, m_hgrn_norm_g, m_w_pool_o, m_w_hgrn_o, m_w_out, v_w_ada, v_b_ada, v_g_pre, v_g_post, v_w_in, v_pool_w, v_pool_scale, v_lb_logits, v_hgrn_norm_g, v_w_pool_o, v_w_hgrn_o, v_w_out):
    ax, ay, ac = lax.axis_index("x"), lax.axis_index("y"), lax.axis_index("c")
    chip = 2 * ax + ay
    dev = 2 * chip + ac
    xe, te = x[0], loss_target[0]
    ada_s = w_ada.shape[2]

    big_names = ("w_in", "w_pool_o", "w_hgrn_o", "w_out")
    big_w = (w_in, w_pool_o, w_hgrn_o, w_out)
    core = jnp.stack([ac]).astype(jnp.int32)
    place = jnp.stack([chip, ac]).astype(jnp.int32)
    slots = {("w_in", l): _cast_to_slot(place, w_in, l, f"cast_w_in{l}") for l in range(2)}
    rest = [(k, l) for l in range(2) for k in big_names[1:]]
    slots.update(zip(rest, _cast_to_slots(place, [(dict(zip(big_names, big_w))[k], l) for k, l in rest],
                                          "cast_rest")))
    w = {k: [None, None] for k in big_names}
    def fills(keys):
        def landed(outs):
            for (k, l), o in zip(keys, outs):
                w[k][l] = slots[k, l] = o
        return landed

    rest0 = [(k, 0) for k in big_names[1:]]
    rest1 = [(k, 1) for k in big_names[1:]]
    no_carry = (None, lambda outs: None)
    order = jnp.stack([chip, 2 * (1 - ax) + ay, 2 * ax + (1 - ay), 2 * (1 - ax) + (1 - ay)]).astype(jnp.int32)

    def ride_fwd0(stage):
        if stage == "proj":
            return _GatherInProj(slots["w_in", 0], order), fills([("w_in", 0)])
        if stage == "hgrn":
            return (_join_carries(_gather_carry([slots[t] for t in rest0]),
                                  _gather_carry([slots["w_in", 1]], piece=(0, 2, 4))),
                    fills(rest0 + [("w_in", 1)]))
        if stage == "tail":
            return _gather_carry([slots["w_in", 1]], piece=(2, 1, 4)), fills([("w_in", 1)])
        return no_carry

    def ride_fwd1(stage):
        if stage == "prenorm":
            return _gather_carry([slots["w_in", 1]], piece=(3, 1, 4)), fills([("w_in", 1)])
        if stage == "hgrn":
            return _gather_carry([slots[t] for t in rest1]), fills(rest1)
        return no_carry

    c_all = _gather_small(jnp.broadcast_to(c, (8, D)), "gather_c").reshape(NDEV, 8, D)[:, 0, :]
    c_pad = jnp.pad(c_all, ((0, ADA_PAD - NDEV), (0, 0)))
    b_sh = lax.dynamic_slice(b_ada, (0, chip * ada_s), (2, ada_s))
    ada_cols = _gather_small(_ada_fwd(c_pad, w_ada, b_sh), "gather_ada")
    ada_cols = ada_cols.reshape(NCHIP, 2, NDEV, 2, ada_s)[:, 0]
    ada_all = jnp.transpose(ada_cols, (2, 1, 0, 3)).reshape(2, NDEV, 3 * D)
    ada_me = lax.dynamic_slice(ada_all, (0, dev, 0), (2, 1, 3 * D))

    lbs = _lb_fwd(lb_logits)
    small = dict(g_pre=g_pre[:, None, :], g_post=g_post[:, None, :], pool_w=pool_w,
                 pool_scale=pool_scale[:, None, :], lb=lbs[:, None, :], hgrn_norm_g=hgrn_norm_g[:, None, :])

    (x1,), sv0 = _layer_fwd(0, xe, ada_me[0], w, small, ride_fwd0)
    (dx2, loss_blk), sv1 = _layer_fwd(1, x1, ada_me[1], w, small, ride_fwd1, target=te)

    parts, recv, held = {}, {}, {}

    def pair_ride(keys, grads):
        def landed(outs):
            held.update({kl: (g, o) for kl, g, o in zip(keys, grads, outs)})
        return _pair_carry(grads), landed

    def pair_adds(keys):
        gs, gots = zip(*[held.pop(kl) for kl in keys])
        if keys[0][0] == "w_in":
            parts[keys[0]] = _pair_add(core, gs[0], gots[0], f"rs_add_w_in{keys[0][1]}")
        else:
            parts.update(zip(keys, _pair_adds(core, gs, gots, f"rs_add_early{keys[0][1]}")))

    def exchange(keys):
        def landed(outs):
            recv.update(zip(keys, outs))
        return _chips_carry([parts[kl] for kl in keys]), landed

    def share(key, first, count):
        def landed(outs):
            (recv[key],) = outs
        into = [recv[key]] if key in recv else None
        return _chips_carry([parts[key]], piece=(first, count, 8), into=into), landed

    def together(*rides):
        carries, fns = zip(*rides)

        def landed(outs):
            for cr, fn in zip(carries, fns):
                fn(outs[:len(cr.outs)])
                outs = outs[len(cr.outs):]
        return _join_carries(*carries), landed

    def early(l):
        return [(k, l) for k in big_names[1:]]

    def pair_alone(keys, grads, tag):
        held.update({kl: (g, o) for kl, g, o in zip(keys, grads, _rs_pair(grads, f"rs_pair_{tag}"))})
        pair_adds(keys)

    def ride_hgrn1(big):
        return pair_ride(early(1), [big[k] for k in big_names[1:]])

    def ride_gw_in1(_):
        pair_adds(early(1))
        return exchange(early(1))

    def ride_d_h1(big):
        return pair_ride([("w_in", 1)], [big["w_in"]])

    def ride_prenorm1(_):
        pair_adds([("w_in", 1)])
        return share(("w_in", 1), 0, 1)

    def ride_head0(_):
        return share(("w_in", 1), 1, 3)

    def ride_hgrn0(big):
        pair_alone(early(0), [big[k] for k in big_names[1:]], "early0")
        return together(exchange(early(0)), share(("w_in", 1), 4, 4))

    def ride_d_h0(big):
        pair_alone([("w_in", 0)], [big["w_in"]], "w_in0")
        return share(("w_in", 0), 0, 4)

    def ride_prenorm0(_):
        return share(("w_in", 0), 4, 4)

    no_ride = lambda so_far: no_carry
    dx1, big1, little1 = _layer_bwd(1, dx2, sv1, w, small, dict(head=no_ride, hgrn=ride_hgrn1, gw_in=ride_gw_in1,
                                                                d_h=ride_d_h1, prenorm=ride_prenorm1))

    gathered = {}
    zero_row = jnp.zeros((1, D), F32)

    def ride_gw_in0(little):
        so_far = dict(little, dshift=zero_row, dscale=zero_row, g_pre=zero_row)

        def landed(outs):
            (gathered["early"],) = outs
        return _gather_rows_carry(_pack_small([so_far, little1])), landed

    dx0, big0, little0 = _layer_bwd(0, dx1, sv0, w, small,
                                    dict(head=ride_head0, hgrn=ride_hgrn0, gw_in=ride_gw_in0, d_h=ride_d_h0,
                                         prenorm=ride_prenorm0))
    loss_row = jnp.broadcast_to(loss_blk[0:1, 0:1], (1, D))
    late = _rows8(jnp.stack([little0["dshift"], little0["dscale"], little0["g_pre"], loss_row]))
    late = _gather_small(late, "gather_small_late").reshape(NDEV, 8, D)
    loss = jnp.sum(late[:, 3, 0])
    packed = gathered["early"].reshape(NDEV, SMALL_ROWS, D)
    packed = packed.at[:, 0:2, :].set(late[:, 0:2, :]).at[:, 8:9, :].set(late[:, 2:3, :])
    red = _chip_sum(place, parts["w_in", 1], recv["w_in", 1], 1, None, "rs_sum_w_in1")
    red = [_chip_sum(place, parts["w_in", 0], recv["w_in", 0], 0, red, "rs_sum_w_in0")]
    red += _chip_sums(place, [[parts[k, l] for l in range(2)] for k in big_names[1:]],
                      [[recv[k, l] for l in range(2)] for k in big_names[1:]], "rs_sum_early")
    g_big = dict(zip(big_names, _rs_swap(red)))

    def upd(wt, g, m, v, name, carry=None):
        shp = wt.shape
        two = lambda t: t.reshape(-1, shp[-1])
        res = _adamw(two(wt), two(g), two(m), two(v), name, carry)
        return [t.reshape(shp) for t in res[:3]], res[3:]

    u_w_in, _ = upd(w_in, g_big["w_in"], m_w_in, v_w_in, "adamw_w_in")
    g_small = _sum_devices(packed)
    g_b_ada, g_g_pre, g_g_post, g_lb, g_pool_w, g_pool_scale, g_norm_g = _unpack_small(g_small)
    g_lb_logits = _lb_bwd(lb_logits, g_lb)
    d_ada_all = packed[:, 0:6, :].reshape(NDEV, 2, 3 * D)
    d_ada_sh = lax.dynamic_slice(jnp.transpose(d_ada_all, (1, 0, 2)), (0, 0, chip * ada_s), (2, NDEV, ada_s))
    d_ada_sh = jnp.pad(d_ada_sh, ((0, 0), (0, ADA_PAD - NDEV), (0, 0)))
    g_w_ada = _ada_wgrad(c_pad.T, d_ada_sh)

    u_w_ada, _ = upd(w_ada, g_w_ada, m_w_ada, v_w_ada, "adamw_w_ada")
    u_w_pool_o, _ = upd(w_pool_o, g_big["w_pool_o"], m_w_pool_o, v_w_pool_o, "adamw_w_pool_o")
    u_w_hgrn_o, _ = upd(w_hgrn_o, g_big["w_hgrn_o"], m_w_hgrn_o, v_w_hgrn_o, "adamw_w_hgrn_o")
    u_w_out, _ = upd(w_out, g_big["w_out"], m_w_out, v_w_out, "adamw_w_out")
    small_w = dict(b_ada=(b_ada, m_b_ada, v_b_ada), g_pre=(g_pre, m_g_pre, v_g_pre),
                   g_post=(g_post, m_g_post, v_g_post), lb_logits=(lb_logits, m_lb_logits, v_lb_logits),
                   pool_w=(pool_w, m_pool_w, v_pool_w), pool_scale=(pool_scale, m_pool_scale, v_pool_scale),
                   hgrn_norm_g=(hgrn_norm_g, m_hgrn_norm_g, v_hgrn_norm_g))
    in_rows = [tuple(t.reshape(rows, width) for t in small_w[key]) for key, _, rows, width in SMALL_PARTS]
    u_rows = _adamw_small(g_small, g_lb_logits, in_rows)
    u_small = {key: [t.reshape(small_w[key][0].shape) for t in u_rows[p]]
               for p, (key, _, _, _) in enumerate(SMALL_PARTS)}

    grads_out = (g_w_ada, g_b_ada, g_g_pre, g_g_post, g_big["w_in"], g_pool_w, g_pool_scale, g_lb_logits,
                 g_norm_g, g_big["w_pool_o"], g_big["w_hgrn_o"], g_big["w_out"])

    def ordered(k):
        s = lambda key: u_small[key][k]
        return (u_w_ada[k], s("b_ada"), s("g_pre"), s("g_post"), u_w_in[k], s("pool_w"), s("pool_scale"),
                s("lb_logits"), s("hgrn_norm_g"), u_w_pool_o[k], u_w_hgrn_o[k], u_w_out[k])

    return (loss, dx0[None], *grads_out, *ordered(0), *ordered(1), *ordered(2))
```

```python
import functools

import jax
import jax.numpy as jnp
from jax import lax
from jax.experimental import pallas as pl
from jax.experimental.pallas import tpu as pltpu

F32 = jnp.float32
BF16 = jnp.bfloat16
MESH = pl.DeviceIdType.MESH

D = 1024
HEADS = 8
HD = 128
GROUPS = 4
POOL_W = 512
CH = 128
SB_WIDE = 32
SB = 16
NH = 2
IN_W = 7168
NCHIP = 4
NDEV = 8
EPS = 1e-6
PV0, PG0, HQ0, HF0, HI0, HG0 = 0, 4, 8, 16, 24, 32
MGP_BLK, MGH_BLK = 5, 6

LR, B1, B2, AEPS, WD, STEP = 0.001, 0.9, 0.999, 1e-08, 0.01, 10
VMEM_LIMIT = 56 * 1024 * 1024


def _cp(sem=None, **kw):
    if sem is not None:
        kw["dimension_semantics"] = sem
    return pltpu.CompilerParams(vmem_limit_bytes=VMEM_LIMIT, **kw)


def _sig(z):
    return 1.0 / (1.0 + jnp.exp(-z))


def _dsilu(z, s):
    return s * (1.0 + z * (1.0 - s))


def _row_tile(rows, cap):
    if rows <= cap:
        return rows
    t = 1 << (cap.bit_length() - 1)
    while rows % t:
        t //= 2
    return t


ANY = pl.BlockSpec(memory_space=pl.ANY)


class _Carry:
    def __init__(self, ins, outs, aliases, n_sem, start, finish):
        self.ins, self.outs, self.aliases, self.n_sem = list(ins), list(outs), dict(aliases), n_sem
        self.start, self.finish = start, finish


class _SemWindow:
    def __init__(self, ref, base):
        self._ref, self._base = ref, base

    @property
    def at(self):
        return self

    def __getitem__(self, k):
        return self._ref.at[self._base + k]


def _join_carries(*carries):
    ins, outs, aliases, spans, n_sem = [], [], {}, [], 0
    for cr in carries:
        aliases.update({len(ins) + i: len(outs) + o for i, o in cr.aliases.items()})
        spans.append((len(ins), len(cr.ins), len(outs), len(cr.outs), n_sem))
        ins, outs, n_sem = ins + cr.ins, outs + cr.outs, n_sem + cr.n_sem

    def run(which):
        def fn(i_refs, o_refs, send_sems, recv_sems):
            for cr, (i0, ni, o0, no, s0) in zip(carries, spans):
                getattr(cr, which)(i_refs[i0:i0 + ni], o_refs[o0:o0 + no], _SemWindow(send_sems, s0),
                                   _SemWindow(recv_sems, s0))
        return fn

    return _Carry(ins, outs, aliases, n_sem, run("start"), run("finish"))


def _call(body, *, name, grid, in_specs, out_specs, out_shape, args, scratch_shapes=(), sem=None, carry=None,
          aliases=None):
    in_specs, out_specs, out_shape = list(in_specs), list(out_specs), list(out_shape)
    scratch_shapes = list(scratch_shapes)
    aliases = dict(aliases or {})
    if carry is None:
        outs = pl.pallas_call(body, name=name, grid=grid, in_specs=in_specs, out_specs=out_specs,
                              out_shape=out_shape, scratch_shapes=scratch_shapes, input_output_aliases=aliases,
                              compiler_params=_cp(sem))(*args)
        return list(outs)
    n_in, n_out, n_scr = len(in_specs), len(out_specs), len(scratch_shapes)
    c_in, c_out = len(carry.ins), len(carry.outs)

    def wrapped(*refs):
        k_in, rest = refs[:n_in], refs[n_in:]
        ci, rest = rest[:c_in], rest[c_in:]
        k_out, rest = rest[:n_out], rest[n_out:]
        co, rest = rest[:c_out], rest[c_out:]
        k_scr, (ssem, rsem) = rest[:n_scr], rest[n_scr:]
        pids = [pl.program_id(d) for d in range(len(grid))]
        first = functools.reduce(jnp.logical_and, [p == 0 for p in pids])
        last = functools.reduce(jnp.logical_and, [p == g - 1 for p, g in zip(pids, grid)])

        @pl.when(first)
        def _():
            carry.start(ci, co, ssem, rsem)

        body(*k_in, *k_out, *k_scr)

        @pl.when(last)
        def _():
            carry.finish(ci, co, ssem, rsem)

    outs = pl.pallas_call(
        wrapped, name=name, grid=grid, in_specs=in_specs + [ANY] * c_in, out_specs=out_specs + [ANY] * c_out,
        out_shape=out_shape + carry.outs,
        input_output_aliases={**aliases, **{n_in + i: n_out + o for i, o in carry.aliases.items()}},
        scratch_shapes=scratch_shapes + [pltpu.SemaphoreType.DMA((carry.n_sem,))] * 2,
        compiler_params=_cp(("arbitrary",) * len(grid)),
    )(*args, *carry.ins)
    return list(outs)


def _mm(a, b, *, name, b_mode="nn", out_shards=0, tm=1024, tn=256, tk=None, out_dtype=F32, carry=None):
    assert b_mode in ("nn", "nn_sh", "nt_shk"), b_mode
    M, K = a.shape
    if b_mode == "nn":
        N = b.shape[1]
    elif b_mode == "nn_sh":
        N = b.shape[0] * b.shape[2]
    else:
        N = b.shape[1]
    tm = _row_tile(M, tm)
    if b_mode == "nn_sh":
        tn = _row_tile(b.shape[2], tn)
    elif out_shards:
        tn = _row_tile(N // out_shards, tn)
    else:
        tn = _row_tile(N, tn)
    if tk is None:
        tk = K if b_mode != "nt_shk" else b.shape[2]
    if b_mode == "nt_shk":
        tk = _row_tile(b.shape[2], tk)
    nm, nn, nk = M // tm, N // tn, K // tk

    a_spec = pl.BlockSpec((tm, tk), lambda m, n, k: (m, k))
    if b_mode == "nn":
        b_spec = pl.BlockSpec((tk, tn), lambda m, n, k: (k, n))
    elif b_mode == "nn_sh":
        nps = b.shape[2] // tn
        b_spec = pl.BlockSpec((None, tk, tn), lambda m, n, k: (n // nps, k, n % nps))
    else:
        kps = b.shape[2] // tk
        b_spec = pl.BlockSpec((None, tn, tk), lambda m, n, k: (k // kps, n, k % kps))
    if out_shards:
        ops = (N // out_shards) // tn
        o_spec = pl.BlockSpec((None, tm, tn), lambda m, n, k: (n // ops, m, n % ops))
        o_shape = jax.ShapeDtypeStruct((out_shards, M, N // out_shards), out_dtype)
    else:
        o_spec = pl.BlockSpec((tm, tn), lambda m, n, k: (m, n))
        o_shape = jax.ShapeDtypeStruct((M, N), out_dtype)
    dn = (((1,), (1,)), ((), ())) if b_mode == "nt_shk" else (((1,), (0,)), ((), ()))

    def body(a_ref, b_ref, o_ref, acc_ref):
        k = pl.program_id(2)

        @pl.when(k == 0)
        def _():
            acc_ref[...] = jnp.zeros(acc_ref.shape, F32)

        acc_ref[...] += lax.dot_general(a_ref[...].astype(BF16), b_ref[...].astype(BF16), dn,
                                        preferred_element_type=F32)

        @pl.when(k == nk - 1)
        def _():
            o_ref[...] = acc_ref[...].astype(o_ref.dtype)

    outs = _call(body, name=name, grid=(nm, nn, nk), in_specs=[a_spec, b_spec], out_specs=[o_spec],
                 out_shape=[o_shape], scratch_shapes=[pltpu.VMEM((tm, tn), F32)],
                 sem=("parallel", "parallel", "arbitrary"), args=(a, b), carry=carry)
    return outs[0] if carry is None else (outs[0], outs[1:])


def _rowvec(n=D):
    return pl.BlockSpec((1, n), lambda i: (0, 0))


def _prenorm_fwd(x, g, scale, shift, name, carry=None):
    S = x.shape[0]
    tr = _row_tile(S, 256)

    def body(x_ref, g_ref, sc_ref, sh_ref, h_ref, ht_ref):
        xv = x_ref[...]
        r = lax.rsqrt(jnp.mean(xv * xv, axis=-1, keepdims=True) + EPS)
        hv = (xv * r) * g_ref[...] * (1.0 + sc_ref[...]) + sh_ref[...]
        h_ref[...] = hv.astype(BF16)
        ht_ref[...] = hv.T.astype(BF16)

    outs = _call(
        body, name=name, grid=(S // tr,),
        in_specs=[pl.BlockSpec((tr, D), lambda i: (i, 0)), _rowvec(), _rowvec(), _rowvec()],
        out_specs=[pl.BlockSpec((tr, D), lambda i: (i, 0)), pl.BlockSpec((D, tr), lambda i: (0, i))],
        out_shape=[jax.ShapeDtypeStruct((S, D), BF16), jax.ShapeDtypeStruct((D, S), BF16)],
        sem=("parallel",), args=(x, g, scale, shift), carry=carry)
    return outs[:2], outs[2:]


def _prenorm_bwd(dh, dxn, x, g, scale, name, carry=None):
    S = x.shape[0]
    tr = _row_tile(S, 256)

    def body(dh_ref, dxn_ref, x_ref, g_ref, sc_ref, dx_ref, dsh_ref, dsc_ref, dg_ref):
        i = pl.program_id(0)

        @pl.when(i == 0)
        def _():
            dsh_ref[...] = jnp.zeros((1, D), F32)
            dsc_ref[...] = jnp.zeros((1, D), F32)
            dg_ref[...] = jnp.zeros((1, D), F32)

        xv = x_ref[...]
        dhv = dh_ref[...]
        gv = g_ref[...]
        mod = 1.0 + sc_ref[...]
        r = lax.rsqrt(jnp.mean(xv * xv, axis=-1, keepdims=True) + EPS)
        xh = xv * r
        dsh_ref[...] += jnp.sum(dhv, axis=0, keepdims=True)
        dsc_ref[...] += jnp.sum(dhv * (xh * gv), axis=0, keepdims=True)
        dg_ref[...] += jnp.sum(dhv * mod * xh, axis=0, keepdims=True)
        u = dhv * mod * gv
        dx_ref[...] = dxn_ref[...] + r * u - xv * (r * r * r) * jnp.mean(u * xv, axis=-1, keepdims=True)

    tile = pl.BlockSpec((tr, D), lambda i: (i, 0))
    outs = _call(
        body, name=name, grid=(S // tr,),
        in_specs=[tile, tile, tile, _rowvec(), _rowvec()],
        out_specs=[tile, _rowvec(), _rowvec(), _rowvec()],
        out_shape=[jax.ShapeDtypeStruct((S, D), F32)] + [jax.ShapeDtypeStruct((1, D), F32)] * 3,
        sem=("arbitrary",), args=(dh, dxn, x, g, scale), carry=carry)
    return outs[:4], outs[4:]


def _layer_tail_fwd(proj, a_in, b_in, x, w_po, w_ho, w_out, gate, g, name, target=None, carry=None):
    S = proj.shape[0]
    tr = _row_tile(S, 256)
    nsh, _, wsh = w_po.shape
    n_in = 10 + (target is not None)

    def body(*refs):
        (mgp_ref, mgh_ref, a_ref, b_ref, x_ref, wpo_ref, who_ref, wout_ref, gate_ref, g_ref) = refs[:10]
        bra_ref, brb_ref, mt_ref, y_ref, xn_ref = refs[n_in:n_in + 5]
        av = a_ref[...]
        bra = jnp.concatenate([jnp.dot(av, wpo_ref[j], preferred_element_type=F32) for j in range(nsh)], axis=1)
        brb = jnp.dot(b_ref[...], who_ref[...], preferred_element_type=F32)
        mv = _sig(mgp_ref[...].astype(F32)) * bra + _sig(mgh_ref[...].astype(F32)) * brb
        bra_ref[...] = bra.astype(BF16)
        brb_ref[...] = brb.astype(BF16)
        mt_ref[...] = mv.T.astype(BF16)
        yv = jnp.dot(mv.astype(BF16), wout_ref[...], preferred_element_type=F32)
        y_ref[...] = yv
        r = lax.rsqrt(jnp.mean(yv * yv, axis=-1, keepdims=True) + EPS)
        xn = x_ref[...] + gate_ref[...] * ((yv * r) * g_ref[...])
        if target is None:
            xn_ref[...] = xn
        else:
            t_ref, l_ref = refs[10], refs[n_in + 5]

            @pl.when(pl.program_id(0) == 0)
            def _():
                l_ref[...] = jnp.zeros((8, 128), F32)

            err = xn - t_ref[...]
            xn_ref[...] = err * (1.0 / D)
            l_ref[...] += 0.5 * jnp.sum(jnp.mean(err * err, axis=-1, keepdims=True))

    tile = pl.BlockSpec((tr, D), lambda i: (i, 0))
    whole = lambda t: pl.BlockSpec(t.shape, lambda i: (0,) * t.ndim)
    last = target is not None
    outs = _call(
        body, name=name, grid=(S // tr,),
        in_specs=[pl.BlockSpec((tr, D), lambda i: (i, MGP_BLK)), pl.BlockSpec((tr, D), lambda i: (i, MGH_BLK)),
                  pl.BlockSpec((tr, POOL_W), lambda i: (i, 0)), tile, tile, whole(w_po), whole(w_ho),
                  whole(w_out), _rowvec(), _rowvec()] + [tile] * last,
        out_specs=[tile, tile, pl.BlockSpec((D, tr), lambda i: (0, i)), tile, tile]
        + [pl.BlockSpec((8, 128), lambda i: (0, 0))] * last,
        out_shape=[jax.ShapeDtypeStruct((S, D), BF16), jax.ShapeDtypeStruct((S, D), BF16),
                   jax.ShapeDtypeStruct((D, S), BF16), jax.ShapeDtypeStruct((S, D), F32),
                   jax.ShapeDtypeStruct((S, D), F32)] + [jax.ShapeDtypeStruct((8, 128), F32)] * last,
        sem=("arbitrary",) if last else ("parallel",),
        args=(proj, proj, a_in, b_in, x, w_po, w_ho, w_out, gate, g) + ((target,) if last else ()), carry=carry)
    return outs[:5 + last], outs[5 + last:]


def _layer_head_bwd(dxn, y, proj, br_a, br_b, w_po, w_ho, w_out, gate, g, name, carry=None):
    S = y.shape[0]
    tr = _row_tile(S, 256)
    nsh, _, wsh = w_po.shape

    def body(dxn_ref, y_ref, mgp_ref, mgh_ref, bra_ref, brb_ref, wpo_ref, who_ref, wout_ref, gate_ref, g_ref,
             dy_ref, dba_ref, dbb_ref, dproj_ref, dain_ref, dbin_ref, dgate_ref, dg_ref, dmgh_s):
        i = pl.program_id(0)
        j = pl.program_id(1)

        @pl.when((i == 0) & (j == 0))
        def _():
            dgate_ref[...] = jnp.zeros((1, D), F32)
            dg_ref[...] = jnp.zeros((1, D), F32)

        @pl.when(j == 1)
        def _():
            dproj_ref[...] = dmgh_s[...]

        @pl.when(j == 0)
        def _():
            everything(dxn_ref, y_ref, mgp_ref, mgh_ref, bra_ref, brb_ref, wpo_ref, who_ref, wout_ref, gate_ref,
                       g_ref, dy_ref, dba_ref, dbb_ref, dproj_ref, dain_ref, dbin_ref, dgate_ref, dg_ref, dmgh_s)

    def everything(dxn_ref, y_ref, mgp_ref, mgh_ref, bra_ref, brb_ref, wpo_ref, who_ref, wout_ref, gate_ref, g_ref,
                   dy_ref, dba_ref, dbb_ref, dproj_ref, dain_ref, dbin_ref, dgate_ref, dg_ref, dmgh_s):
        yv = y_ref[...]
        dv = dxn_ref[...]
        gv = g_ref[...]
        gt = gate_ref[...]
        r = lax.rsqrt(jnp.mean(yv * yv, axis=-1, keepdims=True) + EPS)
        yh = yv * r
        dgate_ref[...] += jnp.sum(dv * (yh * gv), axis=0, keepdims=True)
        dg_ref[...] += jnp.sum(dv * gt * yh, axis=0, keepdims=True)
        u = dv * gt * gv
        dy = (r * u - yv * (r * r * r) * jnp.mean(u * yv, axis=-1, keepdims=True)).astype(BF16)
        dy_ref[...] = dy
        dm = _dot_nt(dy, wout_ref[...])
        sp = _sig(mgp_ref[...].astype(F32))
        sh = _sig(mgh_ref[...].astype(F32))
        dba = (dm * sp).astype(BF16)
        dbb = (dm * sh).astype(BF16)
        dba_ref[...] = dba
        dbb_ref[...] = dbb
        dproj_ref[...] = (dm * bra_ref[...].astype(F32) * sp * (1.0 - sp)).astype(BF16)
        dmgh_s[...] = (dm * brb_ref[...].astype(F32) * sh * (1.0 - sh)).astype(BF16)
        dain = _dot_nt(dba[:, 0:wsh], wpo_ref[0])
        for k in range(1, nsh):
            dain = dain + _dot_nt(dba[:, k * wsh:(k + 1) * wsh], wpo_ref[k])
        dain_ref[...] = dain
        dbin_ref[...] = _dot_nt(dbb, who_ref[...])

    tile = pl.BlockSpec((tr, D), lambda i, j: (i, 0))
    whole = lambda t: pl.BlockSpec(t.shape, lambda i, j: (0,) * t.ndim)
    vec = pl.BlockSpec((1, D), lambda i, j: (0, 0))
    ahead = lambda i, j: jnp.minimum(i + j, S // tr - 1)
    tile_in = pl.BlockSpec((tr, D), lambda i, j: (ahead(i, j), 0))
    outs = _call(
        body, name=name, grid=(S // tr, 2),
        in_specs=[tile_in, tile_in, pl.BlockSpec((tr, D), lambda i, j: (ahead(i, j), MGP_BLK)),
                  pl.BlockSpec((tr, D), lambda i, j: (ahead(i, j), MGH_BLK)), tile_in, tile_in, whole(w_po),
                  whole(w_ho), whole(w_out), vec, vec],
        out_specs=[tile, tile, tile, pl.BlockSpec((tr, D), lambda i, j: (i, MGP_BLK + j)),
                   pl.BlockSpec((tr, POOL_W), lambda i, j: (i, 0)), tile, vec, vec],
        out_shape=[jax.ShapeDtypeStruct((S, D), BF16)] * 3
        + [jax.ShapeDtypeStruct((S, IN_W), BF16), jax.ShapeDtypeStruct((S, POOL_W), F32),
           jax.ShapeDtypeStruct((S, D), F32), jax.ShapeDtypeStruct((1, D), F32), jax.ShapeDtypeStruct((1, D), F32)],
        scratch_shapes=[pltpu.VMEM((tr, D), BF16)], sem=("arbitrary", "arbitrary"),
        args=(dxn, y, proj, proj, br_a, br_b, w_po, w_ho, w_out, gate, g), carry=carry)
    return outs[:8], outs[8:]


def _pool_pieces(u, g, S):
    rowi = lax.broadcasted_iota(jnp.int32, (S, 1), 0)

    def down(z, k):
        return jnp.where(rowi >= k, pltpu.roll(z, k, axis=0), 0.0)

    s2 = u + down(u, 1)
    s4 = s2 + down(s2, 2)
    s8 = s4 + down(s4, 4)
    s16 = s8 + down(s8, 8)
    win = jnp.where(g == 0, s2, jnp.where(g == 1, s4, jnp.where(g == 2, s8, s16)))
    w = jnp.where(g == 0, 2, jnp.where(g == 1, 4, jnp.where(g == 2, 8, 16)))
    count = jnp.minimum(rowi + 1, w).astype(F32)
    return win / count - u, count, rowi


def _pool_fwd(proj, pw, pscale, name, carry=None):
    S = proj.shape[0]

    def body(pv_ref, pg_ref, pw_ref, sc_ref, a_ref, at_ref):
        g = pl.program_id(0)
        pooled, _, _ = _pool_pieces(pv_ref[...].astype(F32), g, S)
        pm = jnp.dot(pooled.astype(BF16), pw_ref[...].astype(BF16), preferred_element_type=F32)
        pgv = pg_ref[...].astype(F32)
        av = pm * sc_ref[...] * (pgv * _sig(pgv))
        a_ref[...] = av.astype(BF16)
        at_ref[...] = av.T.astype(BF16)

    outs = _call(
        body, name=name, grid=(GROUPS,),
        in_specs=[pl.BlockSpec((S, 128), lambda g: (0, PV0 + g)), pl.BlockSpec((S, 128), lambda g: (0, PG0 + g)),
                  pl.BlockSpec((None, 128, 128), lambda g: (g, 0, 0)), pl.BlockSpec((1, 128), lambda g: (0, g))],
        out_specs=[pl.BlockSpec((S, 128), lambda g: (0, g)), pl.BlockSpec((128, S), lambda g: (g, 0))],
        out_shape=[jax.ShapeDtypeStruct((S, POOL_W), BF16), jax.ShapeDtypeStruct((POOL_W, S), BF16)],
        sem=("parallel",), args=(proj, proj, pw, pscale), carry=carry)
    return outs[:2], outs[2:]


def _pool_bwd(da, proj, pw, pscale, dproj, name):
    S = proj.shape[0]

    def body(da_ref, pv_ref, pg_ref, pw_ref, sc_ref, dproj_in, dproj_ref, dpw_ref, dsc_ref, dpg_s):
        @pl.when(pl.program_id(1) == 1)
        def _():
            dproj_ref[...] = dpg_s[...]

        @pl.when(pl.program_id(1) == 0)
        def _():
            group(da_ref, pv_ref, pg_ref, pw_ref, sc_ref, dproj_ref, dpg_s, dpw_ref, dsc_ref)

    def group(da_ref, pv_ref, pg_ref, pw_ref, sc_ref, dpv_ref, dpg_ref, dpw_ref, dsc_ref):
        g = pl.program_id(0)
        pooled, count, rowi = _pool_pieces(pv_ref[...].astype(F32), g, S)
        pwb = pw_ref[...].astype(BF16)
        pm = jnp.dot(pooled.astype(BF16), pwb, preferred_element_type=F32)
        scv = sc_ref[...]
        pgv = pg_ref[...].astype(F32)
        sg = _sig(pgv)
        dav = da_ref[...]
        d_ps = dav * (pgv * sg)
        dpg_ref[...] = (dav * (pm * scv) * _dsilu(pgv, sg)).astype(BF16)
        dsc_ref[...] = jnp.sum(d_ps * pm, axis=0, keepdims=True)
        d_pm = (d_ps * scv).astype(BF16)
        dpw_ref[...] = lax.dot_general(pooled.astype(BF16), d_pm, (((0,), (0,)), ((), ())),
                                       preferred_element_type=F32)
        d_pooled = lax.dot_general(d_pm, pwb, (((1,), (1,)), ((), ())), preferred_element_type=F32)
        z = d_pooled / count

        def up(v, k):
            return jnp.where(rowi < S - k, pltpu.roll(v, S - k, axis=0), 0.0)

        t2 = z + up(z, 1)
        t4 = t2 + up(t2, 2)
        t8 = t4 + up(t4, 4)
        t16 = t8 + up(t8, 8)
        adj = jnp.where(g == 0, t2, jnp.where(g == 1, t4, jnp.where(g == 2, t8, t16)))
        dpv_ref[...] = (adj - d_pooled).astype(BF16)

    col = lambda g, j: (0, g)
    ahead = lambda g, j: jnp.minimum(g + j, GROUPS - 1)
    return pl.pallas_call(
        body, name=name, grid=(GROUPS, 2),
        in_specs=[pl.BlockSpec((S, 128), lambda g, j: (0, ahead(g, j))),
                  pl.BlockSpec((S, 128), lambda g, j: (0, PV0 + ahead(g, j))),
                  pl.BlockSpec((S, 128), lambda g, j: (0, PG0 + ahead(g, j))),
                  pl.BlockSpec((None, 128, 128), lambda g, j: (ahead(g, j), 0, 0)),
                  pl.BlockSpec((1, 128), lambda g, j: (0, ahead(g, j))), ANY],
        out_specs=[pl.BlockSpec((S, 128), lambda g, j: (0, PV0 + g + (PG0 - PV0) * j)),
                   pl.BlockSpec((None, 128, 128), lambda g, j: (g, 0, 0)), pl.BlockSpec((1, 128), col)],
        out_shape=[jax.ShapeDtypeStruct(dproj.shape, dproj.dtype),
                   jax.ShapeDtypeStruct((GROUPS, 128, 128), F32), jax.ShapeDtypeStruct((1, POOL_W), F32)],
        scratch_shapes=[pltpu.VMEM((S, 128), BF16)], input_output_aliases={5: 0},
        compiler_params=_cp(("arbitrary", "arbitrary")),
    )(da, proj, proj, pw, pscale, dproj)


SCAN_SHIFTS = tuple(1 << b for b in range(CH.bit_length() - 1))


def _chunk_cumsum(z, rowi):
    for sh in SCAN_SHIFTS:
        z = z + jnp.where(rowi >= sh, pltpu.roll(z, sh, axis=0), 0.0)
    return z


def _chunk_rev_cumsum(z, rowi):
    for sh in SCAN_SHIFTS:
        z = z + jnp.where(rowi < CH - sh, pltpu.roll(z, CH - sh, axis=0), 0.0)
    return z


def _dot_nn(a, b):
    return jnp.dot(a.astype(BF16), b.astype(BF16), preferred_element_type=F32)


def _dot_nt(a, b):
    return lax.dot_general(a.astype(BF16), b.astype(BF16), (((1,), (1,)), ((), ())), preferred_element_type=F32)


def _dot_tn(a, b):
    return lax.dot_general(a.astype(BF16), b.astype(BF16), (((0,), (0,)), ((), ())), preferred_element_type=F32)


def _gates(hq, hf, lbv):
    hq, hf = hq.astype(F32), hf.astype(F32)
    sq = _sig(hq)
    sf = _sig(hf)
    f = lbv + (1.0 - lbv) * sf
    fc = jnp.maximum(f, 1e-30)
    return hq * sq, sq, sf, f, fc, jnp.log(fc)


DECAY_CAP = 60.0


def _block_ref(c_ref, i, sb):
    if i == 0:
        return jnp.zeros((1, HD), F32)
    return c_ref[sb * i - 1:sb * i, :]


def _block_decay(c_ref, sb):
    spans = [_block_ref(c_ref, i, sb) - c_ref[sb * (i + 1) - 1:sb * (i + 1), :] for i in range(CH // sb)]
    return functools.reduce(jnp.maximum, spans)


def _pair_factors(q_ref, k, c_ref, first, cap, round_bf16, sb):
    nb = CH // sb
    c = c_ref[...]
    zero = jnp.zeros((sb, HD), F32)
    q_groups, k_groups, eqs, eks = [], [], [], []
    for i in range(first, nb):
        blk = slice(sb * i, sb * (i + 1))
        r_i = _block_ref(c_ref, i, sb)
        eq = jnp.exp(jnp.minimum(c_ref[blk, :] - r_i, 0.0))
        ek = jnp.exp(jnp.minimum(r_i - c, cap))
        qi, kei = q_ref[blk, :] * eq, k * ek
        if round_bf16:
            qi, kei = qi.astype(BF16).astype(F32), kei.astype(BF16).astype(F32)
        q_groups.append(jnp.concatenate([zero] * i + [qi] + [zero] * (nb - 1 - i), axis=0))
        k_groups.append(kei)
        eqs.append(eq)
        eks.append(ek)
    return jnp.concatenate(q_groups, axis=1), jnp.concatenate(k_groups, axis=1), eqs, eks


def _pair_mask(rowi, coli, strict, sb):
    return (coli < jnp.bitwise_and(rowi, -sb)) if strict else (coli <= rowi)


def _hgrn_fwd(proj, lb, gn, name, carry=None):
    S = proj.shape[0]
    nch = S // CH
    W = NH * HD

    def body(hq_ref, hf_ref, hi_ref, hg_ref, lb_ref, gn_ref, bin_ref, bint_ref, oraw_ref, st_ref, mild_ref,
             cum_ref, q_s, k_s, c_s, v_s, o_s, state_s, qf_s, kf_s, cf_s):
        state_s[...] = jnp.zeros((NH, HD, HD), F32)
        rowi = lax.broadcasted_iota(jnp.int32, (CH, 1), 0)
        coli = lax.broadcasted_iota(jnp.int32, (1, CH), 1)
        sbi = lax.broadcasted_iota(jnp.int32, (SB, 1), 0)
        gnv = gn_ref[...]

        def gates_pass(n, worst):
            wide, narrow = worst
            rows = pl.ds(pl.multiple_of(n * CH, CH), CH)
            for hh in range(NH):
                lanes = slice(hh * HD, (hh + 1) * HD)
                q, _, _, f, _, logf = _gates(hq_ref[rows, lanes], hf_ref[rows, lanes], lb_ref[:, lanes])
                c = _chunk_cumsum(logf, rowi)
                qf_s[hh, rows, :] = q
                kf_s[hh, rows, :] = 1.0 - f
                cf_s[hh, rows, :] = c
                cum_ref[rows, lanes] = c
                c_s[hh] = c
                wide = jnp.maximum(wide, _block_decay(c_s.at[hh], SB_WIDE))
                narrow = jnp.maximum(narrow, _block_decay(c_s.at[hh], SB))
            return wide, narrow

        def between_chunks(hh, n, rows):
            lanes = slice(hh * HD, (hh + 1) * HD)
            q = qf_s[hh, rows, :]
            k = kf_s[hh, rows, :]
            c = cf_s[hh, rows, :]
            v = hi_ref[rows, lanes].astype(F32)
            q_s[hh] = q
            k_s[hh] = k
            c_s[hh] = c
            v_s[hh] = v
            st = state_s[hh]
            st_ref[hh, n] = st.astype(BF16)
            o_s[hh] = _dot_nt(q * jnp.exp(c), st)
            last = c_s[hh, CH - 1:CH, :]
            state_s[hh] = st * jnp.exp(last) + _dot_tn(v, k * jnp.exp(last - c))

        def pairs_matmul(hh, first, cap, strict, sb):
            qx, kc, _, _ = _pair_factors(q_s.at[hh], k_s[hh], c_s.at[hh], first, cap, False, sb)
            a = jnp.where(_pair_mask(rowi, coli, strict, sb), _dot_nt(qx, kc), 0.0)
            o_s[hh] += _dot_nn(a, v_s[hh])

        def within_chunk_matmul(sb):
            return lambda hh: pairs_matmul(hh, 0, DECAY_CAP, False, sb)

        def within_chunk_exact(hh):
            pairs_matmul(hh, 1, 0.0, True, SB)
            for i in range(CH // SB):
                blk = slice(SB * i, SB * (i + 1))
                qb = q_s[hh, blk, :]
                cb = c_s[hh, blk, :]
                acc = jnp.zeros((SB, HD), F32)
                for s in range(SB):
                    row = SB * i + s
                    w = jnp.exp(jnp.minimum(cb - c_s[hh, row:row + 1, :], 0.0))
                    a_col = jnp.sum(qb * k_s[hh, row:row + 1, :] * w, axis=-1, keepdims=True)
                    acc = acc + jnp.where(sbi >= s, a_col, 0.0) * v_s[hh, row:row + 1, :]
                o_s[hh, blk, :] += acc

        def norm_and_gate(hh, rows):
            lanes = slice(hh * HD, (hh + 1) * HD)
            ov = o_s[hh]
            oraw_ref[rows, lanes] = ov
            r = lax.rsqrt(jnp.mean(ov * ov, axis=-1, keepdims=True) + EPS)
            hg = hg_ref[rows, lanes].astype(F32)
            bin_ref[rows, lanes] = ((ov * r) * gnv * (hg * _sig(hg))).astype(BF16)

        def chunk_with(within_chunk):
            def chunk(n, carry):
                rows = pl.ds(pl.multiple_of(n * CH, CH), CH)
                for hh in range(NH):
                    between_chunks(hh, n, rows)
                for hh in range(NH):
                    within_chunk(hh)
                for hh in range(NH):
                    norm_and_gate(hh, rows)
                return carry
            return chunk

        none = jnp.zeros((1, HD), F32)
        wide, narrow = lax.fori_loop(0, nch, gates_pass, (none, none))
        tier = jnp.where(jnp.max(wide) <= DECAY_CAP, 2.0, jnp.where(jnp.max(narrow) <= DECAY_CAP, 1.0, 0.0))
        mild_ref[...] = jnp.broadcast_to(tier, (8, HD))

        @pl.when(tier == 2.0)
        def _():
            lax.fori_loop(0, nch, chunk_with(within_chunk_matmul(SB_WIDE)), 0, unroll=4)

        @pl.when(tier == 1.0)
        def _():
            lax.fori_loop(0, nch, chunk_with(within_chunk_matmul(SB)), 0, unroll=2)

        @pl.when(tier == 0.0)
        def _():
            lax.fori_loop(0, nch, chunk_with(within_chunk_exact), 0)

        bint_ref[...] = bin_ref[...].astype(F32).T.astype(BF16)

    col = lambda off: pl.BlockSpec((S, W), lambda h: (0, off // NH + h))
    head = pl.BlockSpec((S, W), lambda h: (0, h))
    outs = _call(
        body, name=name, grid=(HEADS // NH,),
        in_specs=[col(HQ0), col(HF0), col(HI0), col(HG0), pl.BlockSpec((1, W), lambda h: (0, h)),
                  pl.BlockSpec((1, HD), lambda h: (0, 0))],
        out_specs=[head, pl.BlockSpec((W, S), lambda h: (h, 0)), head,
                   pl.BlockSpec((NH, nch, HD, HD), lambda h: (h, 0, 0, 0)),
                   pl.BlockSpec((8, HD), lambda h: (h, 0)), head],
        out_shape=[jax.ShapeDtypeStruct((S, D), BF16), jax.ShapeDtypeStruct((D, S), BF16),
                   jax.ShapeDtypeStruct((S, D), F32), jax.ShapeDtypeStruct((HEADS, nch, HD, HD), BF16),
                   jax.ShapeDtypeStruct((8 * HEADS // NH, HD), F32), jax.ShapeDtypeStruct((S, D), F32)],
        scratch_shapes=[pltpu.VMEM((NH, CH, HD), F32)] * 5 + [pltpu.VMEM((NH, HD, HD), F32)]
        + [pltpu.VMEM((NH, S, HD), F32)] * 3,
        sem=("parallel",), args=(proj, proj, proj, proj, lb, gn), carry=carry)
    return outs[:6], outs[6:]


def _hgrn_bwd(dbin, proj, oraw, states, mild, cum, lb, gn, dproj, name, carry=None):
    S = proj.shape[0]
    nch = S // CH
    W = NH * HD
    n_in = 12

    def body(*refs):
        ins, (dproj_ref, dlb_ref, dgn_ref) = refs[:n_in - 1], refs[n_in:n_in + 3]
        scratch, later = refs[n_in + 3:-3], refs[-3:]
        seg = pl.program_id(1)

        @pl.when(seg == 0)
        def _():
            heads(*ins, dproj_ref, *later, dlb_ref, dgn_ref, *scratch)

        for s, kept in enumerate(later):
            @pl.when(seg == s + 1)
            def _(kept=kept):
                dproj_ref[...] = kept[...]

    def heads(db_ref, hq_ref, hf_ref, hi_ref, hg_ref, or_ref, st_ref, mild_ref, cum_ref, lb_ref, gn_ref,
              dq_ref, df_ref, di_ref, dg_ref, dlb_ref, dgn_ref,
              q_s, k_s, c_s, v_s, do_s, dq_s, dk_s, dv_s, dc_s, dqd_s, dkd_s, f_s, sf_s, sq_s, dl_s, dst_s,
              dlb_s, dgn_s):
        dst_s[...] = jnp.zeros((NH, HD, HD), F32)
        dlb_s[...] = jnp.zeros((1, W), F32)
        dgn_s[...] = jnp.zeros((1, HD), F32)
        rowi = lax.broadcasted_iota(jnp.int32, (CH, 1), 0)
        coli = lax.broadcasted_iota(jnp.int32, (1, CH), 1)
        sbi = lax.broadcasted_iota(jnp.int32, (SB, 1), 0)
        gnv = gn_ref[...]
        def between_chunks(hh, n, rows):
            lanes = slice(hh * HD, (hh + 1) * HD)
            lbv = lb_ref[:, lanes]
            hq = hq_ref[rows, lanes].astype(F32)
            sq = _sig(hq)
            sf = _sig(hf_ref[rows, lanes].astype(F32))
            f = lbv + (1.0 - lbv) * sf
            q = hq * sq
            k = 1.0 - f
            f_s[hh] = f
            sf_s[hh] = sf
            sq_s[hh] = sq
            v = hi_ref[rows, lanes].astype(F32)
            c = cum_ref[rows, lanes]
            ov = or_ref[rows, lanes]
            hg = hg_ref[rows, lanes].astype(F32)
            sg = _sig(hg)
            r = lax.rsqrt(jnp.mean(ov * ov, axis=-1, keepdims=True) + EPS)
            dbv = db_ref[rows, lanes]
            d_on = dbv * (hg * sg)
            dg_ref[rows, lanes] = (dbv * ((ov * r) * gnv) * _dsilu(hg, sg)).astype(BF16)
            dgn_s[...] += jnp.sum(d_on * (ov * r), axis=0, keepdims=True)
            u = d_on * gnv
            do = r * u - ov * (r * r * r) * jnp.mean(u * ov, axis=-1, keepdims=True)
            q_s[hh] = q
            k_s[hh] = k
            c_s[hh] = c
            v_s[hh] = v
            do_s[hh] = do
            st = st_ref[hh, n].astype(F32)
            dst = dst_s[hh]
            ec = jnp.exp(c)
            last = c_s[hh, CH - 1:CH, :]
            el = jnp.exp(last - c)
            elast = jnp.exp(last)
            dq = _dot_nn(do, st) * ec
            dk = _dot_nn(v, dst) * el
            dq_s[hh] = dq
            dk_s[hh] = dk
            dv_s[hh] = _dot_nt(k * el, dst)
            dc_s[hh] = q * dq - k * dk
            dl_s[hh] = (jnp.sum(k * dk, axis=0, keepdims=True)
                        + elast * jnp.sum(st * dst, axis=0, keepdims=True))
            dst_s[hh] = dst * elast + _dot_tn(do, q * ec)

        def pairs_matmul(hh, first, cap, strict, sb):
            do = do_s[hh]
            qx, kc, eqs, eks = _pair_factors(q_s.at[hh], k_s[hh], c_s.at[hh], first, cap, True, sb)
            mask = _pair_mask(rowi, coli, strict, sb)
            a = jnp.where(mask, _dot_nt(qx, kc), 0.0)
            d_a = jnp.where(mask, _dot_nt(do, v_s[hh]).astype(BF16).astype(F32), 0.0)
            dqx = _dot_nn(d_a, kc)
            dkc = _dot_tn(d_a, qx)
            dv_s[hh] += _dot_tn(a, do)
            dk, dcum = dk_s[hh], dc_s[hh]
            dq_slabs = [jnp.zeros((sb, HD), F32)] * first
            dc_slabs = [jnp.zeros((sb, HD), F32)] * first
            for g, (eq, ek) in enumerate(zip(eqs, eks)):
                rows = slice(sb * (first + g), sb * (first + g + 1))
                cols = slice(HD * g, HD * (g + 1))
                dq_i = dqx[rows, cols]
                dk_i = dkc[:, cols]
                dq_slabs.append(dq_i * eq)
                dc_slabs.append(qx[rows, cols] * dq_i)
                dk = dk + dk_i * ek
                dcum = dcum - kc[:, cols] * dk_i
            dq_s[hh] += jnp.concatenate(dq_slabs, axis=0)
            dk_s[hh] = dk
            dc_s[hh] = dcum + jnp.concatenate(dc_slabs, axis=0)

        def pairs_exact(hh):
            dqd_s[hh] = jnp.zeros((CH, HD), F32)
            dkd_s[hh] = jnp.zeros((CH, HD), F32)
            for i in range(CH // SB):
                blk = slice(SB * i, SB * (i + 1))
                qb = q_s[hh, blk, :]
                cb = c_s[hh, blk, :]
                dob = do_s[hh, blk, :]
                dq_acc = jnp.zeros((SB, HD), F32)
                for s in range(SB):
                    row = SB * i + s
                    ks = k_s[hh, row:row + 1, :]
                    vs = v_s[hh, row:row + 1, :]
                    w = jnp.exp(jnp.minimum(cb - c_s[hh, row:row + 1, :], 0.0))
                    live = sbi >= s
                    a_col = jnp.where(live, jnp.sum(qb * ks * w, axis=-1, keepdims=True), 0.0)
                    da_col = jnp.where(live, jnp.sum(dob * vs, axis=-1, keepdims=True), 0.0)
                    dq_acc = dq_acc + da_col * ks * w
                    dkd_s[hh, row:row + 1, :] += jnp.sum(da_col * qb * w, axis=0, keepdims=True)
                    dv_s[hh, row:row + 1, :] += jnp.sum(a_col * dob, axis=0, keepdims=True)
                dqd_s[hh, blk, :] += dq_acc
            dq_d = dqd_s[hh]
            dk_d = dkd_s[hh]
            dq_s[hh] += dq_d
            dk_s[hh] += dk_d
            dc_s[hh] += q_s[hh] * dq_d - k_s[hh] * dk_d

        def gate_grads(hh, rows):
            lanes = slice(hh * HD, (hh + 1) * HD)
            lbv = lb_ref[:, lanes]
            hq = hq_ref[rows, lanes].astype(F32)
            f, sf, sq = f_s[hh], sf_s[hh], sq_s[hh]
            dlogf = _chunk_rev_cumsum(dc_s[hh], rowi) + dl_s[hh]
            dfv = jnp.where(f > 1e-30, dlogf / jnp.maximum(f, 1e-30), 0.0) - dk_s[hh]
            dlb_s[:, lanes] += jnp.sum(dfv * (1.0 - sf), axis=0, keepdims=True)
            df_ref[rows, lanes] = (dfv * (1.0 - lbv) * sf * (1.0 - sf)).astype(BF16)
            dq_ref[rows, lanes] = (dq_s[hh] * _dsilu(hq, sq)).astype(BF16)
            di_ref[rows, lanes] = dv_s[hh].astype(BF16)

        def chunk_with(pairs):
            def chunk(j, carry):
                n = nch - 1 - j
                rows = pl.ds(pl.multiple_of(n * CH, CH), CH)
                for hh in range(NH):
                    between_chunks(hh, n, rows)
                for hh in range(NH):
                    pairs(hh)
                for hh in range(NH):
                    gate_grads(hh, rows)
                return carry
            return chunk

        def pairs_mild(sb):
            return lambda hh: pairs_matmul(hh, 0, DECAY_CAP, False, sb)

        def pairs_any(hh):
            pairs_matmul(hh, 1, 0.0, True, SB)
            pairs_exact(hh)

        tier = jnp.max(mild_ref[...])

        @pl.when(tier == 2.0)
        def _():
            lax.fori_loop(0, nch, chunk_with(pairs_mild(SB_WIDE)), 0, unroll=2)

        @pl.when(tier == 1.0)
        def _():
            lax.fori_loop(0, nch, chunk_with(pairs_mild(SB)), 0)

        @pl.when(tier == 0.0)
        def _():
            lax.fori_loop(0, nch, chunk_with(pairs_any), 0)

        dlb_ref[...] = dlb_s[...]
        dgn_ref[...] = jnp.broadcast_to(dgn_s[...], (8, HD))

    ahead = lambda h, s: jnp.minimum(h + jnp.minimum(s, 1), HEADS // NH - 1)
    col = lambda off: pl.BlockSpec((S, W), lambda h, s: (0, off // NH + ahead(h, s)))
    head_in = pl.BlockSpec((S, W), lambda h, s: (0, ahead(h, s)))
    vec_in = pl.BlockSpec((1, W), lambda h, s: (0, ahead(h, s)))
    vec = pl.BlockSpec((1, W), lambda h, s: (0, h))
    seg_w = (HF0 - HQ0) // NH
    outs = _call(
        body, name=name, grid=(HEADS // NH, 4),
        in_specs=[head_in, col(HQ0), col(HF0), col(HI0), col(HG0), head_in,
                  pl.BlockSpec((NH, nch, HD, HD), lambda h, s: (ahead(h, s), 0, 0, 0)),
                  pl.BlockSpec((8, HD), lambda h, s: (ahead(h, s), 0)), head_in, vec_in,
                  pl.BlockSpec((1, HD), lambda h, s: (0, 0)), ANY],
        out_specs=[pl.BlockSpec((S, W), lambda h, s: (0, HQ0 // NH + seg_w * s + h)), vec,
                   pl.BlockSpec((8, HD), lambda h, s: (h, 0))],
        out_shape=[jax.ShapeDtypeStruct(dproj.shape, dproj.dtype), jax.ShapeDtypeStruct((1, D), F32),
                   jax.ShapeDtypeStruct((8 * HEADS // NH, HD), F32)],
        scratch_shapes=[pltpu.VMEM((NH, CH, HD), F32)] * 14
        + [pltpu.VMEM((NH, 1, HD), F32), pltpu.VMEM((NH, HD, HD), F32), pltpu.VMEM((1, W), F32),
           pltpu.VMEM((1, HD), F32)] + [pltpu.VMEM((S, W), BF16)] * 3,
        sem=("arbitrary", "arbitrary"), aliases={n_in - 1: 0},
        args=(dbin, proj, proj, proj, proj, oraw, states, mild, cum, lb, gn, dproj), carry=carry)
    dproj, dlb, dgn = outs[:3]
    return (dproj, dlb, dgn.reshape(HEADS // NH, 8, HD)[:, 0, :]), outs[3:]


def _lower_bounds(l0, l1):
    m = jnp.maximum(l0, l1)
    e0 = jnp.exp(l0 - m)
    e1 = jnp.exp(l1 - m)
    tot = e0 + e1
    p0 = e0 / tot
    p1 = e1 / tot
    return jnp.clip(p0 - p0, 0.0, 1.0), jnp.clip((p0 + p1) - p0, 0.0, 1.0)


def _lb_fwd(logits):
    def body(l_ref, o_ref):
        lb0, lb1 = _lower_bounds(l_ref[0:1, :], l_ref[1:2, :])
        o_ref[0:1, :] = lb0
        o_ref[1:2, :] = lb1

    return pl.pallas_call(body, name="lb_fwd", out_shape=jax.ShapeDtypeStruct((2, D), F32))(logits)


def _lb_bwd(logits, dlb):
    def body(l_ref, d_ref, o_ref):
        _, vjp = jax.vjp(_lower_bounds, l_ref[0:1, :], l_ref[1:2, :])
        g0, g1 = vjp((d_ref[0:1, :], d_ref[1:2, :]))
        o_ref[0:1, :] = g0
        o_ref[1:2, :] = g1

    return pl.pallas_call(body, name="lb_bwd", out_shape=jax.ShapeDtypeStruct((2, D), F32))(logits, dlb)


ADA_PAD = 128


def _ada_fwd(c_pad, w_ada, b_sh):
    ns = w_ada.shape[2]

    def body(c_ref, w_ref, b_ref, o_ref):
        cv = c_ref[...]
        ca = (cv * _sig(cv)).astype(BF16)
        for l in range(2):
            res = jnp.dot(ca, w_ref[l].astype(BF16), preferred_element_type=F32)
            o_ref[:, l * ns:(l + 1) * ns] = res[0:NDEV, :] + b_ref[l:l + 1, :]

    return pl.pallas_call(body, name="ada_fwd", out_shape=jax.ShapeDtypeStruct((NDEV, 2 * ns), F32),
                          compiler_params=_cp())(c_pad, w_ada, b_sh)


def _ada_wgrad(c_pad_t, d_ada_sh):
    ns = d_ada_sh.shape[2]

    def body(c_ref, d_ref, o_ref):
        cv = c_ref[...]
        ca = (cv * _sig(cv)).astype(BF16)
        for l in range(2):
            o_ref[l] = jnp.dot(ca, d_ref[l].astype(BF16), preferred_element_type=F32)

    return pl.pallas_call(body, name="ada_wgrad", out_shape=jax.ShapeDtypeStruct((2, D, ns), F32),
                          compiler_params=_cp())(c_pad_t, d_ada_sh)


def _sum_devices(g):
    _, R, C = g.shape

    def body(g_ref, o_ref):
        acc = g_ref[0]
        for d in range(1, NDEV):
            acc = acc + g_ref[d]
        o_ref[...] = acc

    return pl.pallas_call(body, name="sum_devices", out_shape=jax.ShapeDtypeStruct((R, C), F32),
                          compiler_params=_cp())(g)


def _adamw(w, g, m, v, name, echo=False):
    R, C = w.shape
    tr = _row_tile(R, max(8, (1 << 19) // C))

    def body(w_ref, g_ref, m_ref, v_ref, d_ref, nm_ref, nv_ref, *g_out):
        d_ref[...], nm_ref[...], nv_ref[...] = _adamw_update(w_ref[...], g_ref[...], m_ref[...], v_ref[...])
        for o_ref in g_out:
            o_ref[...] = g_ref[...]

    tile = pl.BlockSpec((tr, C), lambda i: (i, 0))
    n_out = 4 if echo else 3
    return _call(body, name=name, grid=(R // tr,), in_specs=[tile] * 4, out_specs=[tile] * n_out,
                 out_shape=[jax.ShapeDtypeStruct((R, C), F32)] * n_out, sem=("parallel",), args=(w, g, m, v))


def _adamw_many(wgmv, name, steps=4):
    n = len(wgmv)

    def body(*refs):
        ins, outs = refs[:4 * n], refs[4 * n:]
        for a in range(n):
            w_ref, g_ref, m_ref, v_ref = ins[4 * a:4 * a + 4]
            d_ref, nm_ref, nv_ref, g_out = outs[4 * a:4 * a + 4]
            d_ref[...], nm_ref[...], nv_ref[...] = _adamw_update(w_ref[...], g_ref[...], m_ref[...], v_ref[...])
            g_out[...] = g_ref[...]

    def tile(t):
        return pl.BlockSpec((t.shape[0] // steps, t.shape[1]), lambda i: (i, 0))

    flat = [t for four in wgmv for t in four]
    outs = pl.pallas_call(
        body, name=name, grid=(steps,), in_specs=[tile(t) for t in flat],
        out_specs=[tile(four[0]) for four in wgmv for _ in range(4)],
        out_shape=[jax.ShapeDtypeStruct(four[0].shape, F32) for four in wgmv for _ in range(4)],
        compiler_params=_cp(("parallel",)))(*flat)
    return [outs[4 * a:4 * a + 4] for a in range(n)]


def _adamw_update(w, g, m, v):
    nm = B1 * m + (1.0 - B1) * g
    nv = B2 * v + (1.0 - B2) * (g * g)
    m_hat = nm / (1.0 - B1 ** STEP)
    v_hat = nv / (1.0 - B2 ** STEP)
    return -LR * (m_hat / (jnp.sqrt(v_hat) + AEPS) + WD * w), nm, nv


SMALL_PARTS = (("b_ada", 0, 6, D), ("g_pre", 8, 2, D), ("g_post", 16, 2, D), ("lb_logits", 24, 2, D),
               ("pool_w", 32, 128, D), ("pool_scale", 160, 1, D), ("hgrn_norm_g", 168, 1, 2 * HD))


def _adamw_small(g_small, g_lb_logits, wmv):
    n = len(SMALL_PARTS)

    def body(g_ref, glb_ref, *refs):
        ins, outs = refs[:3 * n], refs[3 * n:]
        for p, (key, row0, rows, width) in enumerate(SMALL_PARTS):
            gv = glb_ref[...] if key == "lb_logits" else g_ref[row0:row0 + rows, 0:width]
            res = _adamw_update(ins[3 * p][...], gv, ins[3 * p + 1][...], ins[3 * p + 2][...])
            for t in range(3):
                outs[3 * p + t][...] = res[t]

    flat = [t for triple in wmv for t in triple]
    outs = pl.pallas_call(body, name="adamw_small",
                          out_shape=[jax.ShapeDtypeStruct(t.shape, F32) for t in flat],
                          compiler_params=_cp())(g_small, g_lb_logits, *flat)
    return [outs[3 * p:3 * p + 3] for p in range(n)]


def _cast_to_slot(place, w, l, name):
    _, R, C = w.shape
    tr = _row_tile(R, max(8, (1 << 19) // C))

    def body(p_ref, w_ref, o_ref):
        o_ref[...] = w_ref[...].astype(BF16)

    return pl.pallas_call(
        body, name=name, out_shape=jax.ShapeDtypeStruct((NCHIP, R, C), BF16),
        grid_spec=pltpu.PrefetchScalarGridSpec(
            num_scalar_prefetch=1, grid=(R // tr,),
            in_specs=[pl.BlockSpec((None, tr, C), lambda i, p_ref: (l, i, 0))],
            out_specs=pl.BlockSpec((None, tr, C), lambda i, p_ref: (p_ref[0], i, 0))),
        compiler_params=_cp(("parallel",)),
    )(place, w)


def _cast_to_slots(place, ws, name):
    n = len(ws)

    def body(p_ref, *refs):
        for w_ref, o_ref in zip(refs[:n], refs[n:]):
            o_ref[...] = w_ref[...].astype(BF16)

    def layer(l):
        return lambda i, p_ref: (l, 0, 0)

    return pl.pallas_call(
        body, name=name, out_shape=[jax.ShapeDtypeStruct((NCHIP,) + w.shape[1:], BF16) for w, _ in ws],
        grid_spec=pltpu.PrefetchScalarGridSpec(
            num_scalar_prefetch=1, grid=(1,),
            in_specs=[pl.BlockSpec((None,) + w.shape[1:], layer(l)) for w, l in ws],
            out_specs=[pl.BlockSpec((None,) + w.shape[1:], lambda i, p_ref: (p_ref[0], 0, 0)) for w, _ in ws]),
        compiler_params=_cp(("arbitrary",)),
    )(place, *[w for w, _ in ws])


def _pair_adds(core, gs, gots, name):
    n = len(gs)

    def body(c_ref, *refs):
        for a_ref, b_ref, o_ref in zip(refs[:n], refs[n:2 * n], refs[2 * n:]):
            o_ref[...] = (a_ref[...].astype(F32) + b_ref[...].astype(F32)).astype(o_ref.dtype)

    def whole(t):
        return pl.BlockSpec(t.shape, lambda i, c_ref: (0, 0, 0))

    return pl.pallas_call(
        body, name=name, out_shape=[jax.ShapeDtypeStruct(t.shape, BF16) for t in gots],
        grid_spec=pltpu.PrefetchScalarGridSpec(
            num_scalar_prefetch=1, grid=(1,),
            in_specs=[pl.BlockSpec(t.shape, lambda i, c_ref: (0, c_ref[0], 0)) for t in gots]
            + [whole(t) for t in gots],
            out_specs=[whole(t) for t in gots]),
        compiler_params=_cp(("arbitrary",)),
    )(core, *gs, *gots)


def _pair_add(core, g, got, name):
    _, R, C = g.shape
    r2 = R // 2
    tr = _row_tile(r2, max(8, (1 << 19) // C))
    nt = r2 // tr

    def body(c_ref, a_ref, b_ref, o_ref):
        o_ref[...] = (a_ref[...].astype(F32) + b_ref[...].astype(F32)).astype(o_ref.dtype)

    return pl.pallas_call(
        body, name=name, out_shape=jax.ShapeDtypeStruct((NCHIP, r2, C), BF16),
        grid_spec=pltpu.PrefetchScalarGridSpec(
            num_scalar_prefetch=1, grid=(NCHIP, nt),
            in_specs=[pl.BlockSpec((None, tr, C), lambda j, i, c_ref: (j, c_ref[0] * nt + i, 0)),
                      pl.BlockSpec((None, tr, C), lambda j, i, c_ref: (j, i, 0))],
            out_specs=pl.BlockSpec((None, tr, C), lambda j, i, c_ref: (j, i, 0))),
        compiler_params=_cp(("parallel", "parallel")),
    )(core, g, got)


def _sum_in_chip_order(me, own_ref, r_ref):
    own = own_ref[...].astype(F32)
    acc = None
    for j in range(NCHIP):
        slot = jnp.minimum(jnp.where(j > me, j - 1, j), NCHIP - 2)
        term = jnp.where(me == j, own, r_ref[slot].astype(F32))
        acc = term if acc is None else acc + term
    return acc


def _chip_sums(place, parts, recvs, name):
    n = len(parts)

    def body(p_ref, *refs):
        for a in range(n):
            for l in range(2):
                refs[4 * n + a][l] = _sum_in_chip_order(p_ref[0], refs[2 * a + l], refs[2 * n + 2 * a + l])

    def own(t):
        return pl.BlockSpec((None,) + t.shape[1:], lambda i, p_ref: (p_ref[0], 0, 0))

    def whole(t):
        return pl.BlockSpec(t.shape, lambda i, p_ref: (0, 0, 0))

    flat_p = [t for pair in parts for t in pair]
    flat_r = [t for pair in recvs for t in pair]
    return pl.pallas_call(
        body, name=name,
        out_shape=[jax.ShapeDtypeStruct((2, 2 * p[0].shape[1], p[0].shape[2]), F32) for p in parts],
        grid_spec=pltpu.PrefetchScalarGridSpec(
            num_scalar_prefetch=1, grid=(1,),
            in_specs=[own(t) for t in flat_p] + [whole(t) for t in flat_r],
            out_specs=[pl.BlockSpec((2,) + p[0].shape[1:], lambda i, p_ref: (0, p_ref[1], 0)) for p in parts]),
        compiler_params=_cp(("arbitrary",)),
    )(place, *flat_p, *flat_r)


def _chip_sum(place, part, recv, layer, both, name):
    _, r2, C = part.shape
    tr = _row_tile(r2, max(8, (1 << 18) // C))
    nt = r2 // tr

    def body(p_ref, own_ref, r_ref, *rest):
        rest[-1][...] = _sum_in_chip_order(p_ref[0], own_ref, r_ref)

    args = (place, part, recv) if both is None else (place, part, recv, both)
    return pl.pallas_call(
        body, name=name, out_shape=jax.ShapeDtypeStruct((2, 2 * r2, C), F32),
        grid_spec=pltpu.PrefetchScalarGridSpec(
            num_scalar_prefetch=1, grid=(nt,),
            in_specs=[pl.BlockSpec((None, tr, C), lambda i, p_ref: (p_ref[0], i, 0)),
                      pl.BlockSpec((NCHIP - 1, tr, C), lambda i, p_ref: (0, i, 0))] + [ANY] * (len(args) - 3),
            out_specs=pl.BlockSpec((None, tr, C), lambda i, p_ref: (layer, p_ref[1] * nt + i, 0))),
        input_output_aliases={} if both is None else {3: 0},
        compiler_params=_cp(("parallel",)),
    )(*args)


def _place():
    x, y, c = lax.axis_index("x"), lax.axis_index("y"), lax.axis_index("c")
    chips = [(1 - x, y), (x, 1 - y), (1 - x, 1 - y)]
    return x, y, c, chips


def _gather_small(blk, name):
    m_per, n = blk.shape

    def body(x_ref, out_ref, send_sems, recv_sems, local_sem):
        x, y, c, chips = _place()
        me, sibling = (x, y, c), (x, y, 1 - c)

        def rows(px, py, pc):
            return out_ref.at[pl.ds((4 * px + 2 * py + pc) * m_per, m_per), :]

        def copy(k, block, to, src=None):
            return pltpu.make_async_remote_copy(
                src_ref=rows(*block) if src is None else src, dst_ref=rows(*block),
                send_sem=send_sems.at[k], recv_sem=recv_sems.at[k], device_id=to, device_id_type=MESH)

        mine = pltpu.make_async_copy(x_ref, rows(*me), local_sem)
        mine.start()
        first = [copy(0, me, sibling, src=x_ref)]
        first += [copy(1 + j, me, (*chip, c), src=x_ref) for j, chip in enumerate(chips)]
        for cp in first:
            cp.start()
        passed = [copy(4 + j, (*chip, c), sibling) for j, chip in enumerate(chips)]
        for j, chip in enumerate(chips):
            copy(1 + j, (*chip, c), me).wait_recv()
            passed[j].start()
        copy(0, sibling, me).wait_recv()
        for j, chip in enumerate(chips):
            copy(4 + j, (*chip, 1 - c), me).wait_recv()
        for cp in first + passed:
            cp.wait_send()
        mine.wait()

    return pl.pallas_call(
        body, name=name, out_shape=jax.ShapeDtypeStruct((NDEV * m_per, n), blk.dtype),
        in_specs=[pl.BlockSpec(memory_space=pltpu.VMEM)], out_specs=pl.BlockSpec(memory_space=pltpu.VMEM),
        scratch_shapes=[pltpu.SemaphoreType.DMA((7,)), pltpu.SemaphoreType.DMA((7,)), pltpu.SemaphoreType.DMA],
        compiler_params=_cp(),
    )(blk)


def _gather_rows_carry(blk):
    m_per, n = blk.shape

    def rows(ref, px, py, pc):
        return ref.at[pl.ds((4 * px + 2 * py + pc) * m_per, m_per), :]

    def copy(ins, outs, send_sems, recv_sems, k, block, to, own=False):
        return pltpu.make_async_remote_copy(
            src_ref=ins[0] if own else rows(outs[0], *block), dst_ref=rows(outs[0], *block),
            send_sem=send_sems.at[k], recv_sem=recv_sems.at[k], device_id=to, device_id_type=MESH)

    def mine(ins, outs, send_sems):
        x, y, c, _ = _place()
        return pltpu.make_async_copy(ins[0], rows(outs[0], x, y, c), send_sems.at[7])

    def start(ins, outs, send_sems, recv_sems):
        x, y, c, chips = _place()
        mine(ins, outs, send_sems).start()
        copy(ins, outs, send_sems, recv_sems, 0, (x, y, c), (x, y, 1 - c), own=True).start()
        for j, chip in enumerate(chips):
            copy(ins, outs, send_sems, recv_sems, 1 + j, (x, y, c), (*chip, c), own=True).start()

    def finish(ins, outs, send_sems, recv_sems):
        x, y, c, chips = _place()
        for j, chip in enumerate(chips):
            copy(ins, outs, send_sems, recv_sems, 1 + j, (*chip, c), (x, y, c)).wait_recv()
            copy(ins, outs, send_sems, recv_sems, 4 + j, (*chip, c), (x, y, 1 - c)).start()
        copy(ins, outs, send_sems, recv_sems, 0, (x, y, 1 - c), (x, y, c)).wait_recv()
        for j, chip in enumerate(chips):
            copy(ins, outs, send_sems, recv_sems, 4 + j, (*chip, 1 - c), (x, y, c)).wait_recv()
        copy(ins, outs, send_sems, recv_sems, 0, (x, y, c), (x, y, 1 - c), own=True).wait_send()
        for j, chip in enumerate(chips):
            copy(ins, outs, send_sems, recv_sems, 1 + j, (x, y, c), (*chip, c), own=True).wait_send()
            copy(ins, outs, send_sems, recv_sems, 4 + j, (*chip, c), (x, y, 1 - c)).wait_send()
        mine(ins, outs, send_sems).wait()

    return _Carry([blk], [jax.ShapeDtypeStruct((NDEV * m_per, n), blk.dtype)], {}, 8, start, finish)


def _gather_carry(shards, piece=(0, 1, 1)):
    n = len(shards)
    first, count, of = piece

    def rows(ref, half):
        r2 = ref.shape[1] // 2
        return pl.ds(half * r2 + first * (r2 // of), count * (r2 // of))

    def over_ici(outs, send_sems, recv_sems, a, j, chip_xy, slot):
        x, y, c, _ = _place()
        blk = outs[a].at[slot, rows(outs[a], c), :]
        return pltpu.make_async_remote_copy(
            src_ref=blk, dst_ref=blk, send_sem=send_sems.at[6 * a + j], recv_sem=recv_sems.at[6 * a + j],
            device_id=(*chip_xy, c), device_id_type=MESH)

    def over_d2d(outs, send_sems, recv_sems, a, j, slot, half):
        x, y, c, _ = _place()
        blk = outs[a].at[slot, rows(outs[a], half), :]
        return pltpu.make_async_remote_copy(
            src_ref=blk, dst_ref=blk, send_sem=send_sems.at[6 * a + 3 + j], recv_sem=recv_sems.at[6 * a + 3 + j],
            device_id=(x, y, 1 - c), device_id_type=MESH)

    def start(ins, outs, send_sems, recv_sems):
        x, y, c, chips = _place()
        for a in range(n):
            for j, chip_xy in enumerate(chips):
                over_ici(outs, send_sems, recv_sems, a, j, chip_xy, 2 * x + y).start()

    def finish(ins, outs, send_sems, recv_sems):
        x, y, c, chips = _place()
        for a in range(n):
            for j, (cx, cy) in enumerate(chips):
                over_ici(outs, send_sems, recv_sems, a, j, (cx, cy), 2 * cx + cy).wait_recv()
                over_d2d(outs, send_sems, recv_sems, a, j, 2 * cx + cy, c).start()
        for a in range(n):
            for j, (cx, cy) in enumerate(chips):
                over_d2d(outs, send_sems, recv_sems, a, j, 2 * cx + cy, 1 - c).wait_recv()
        for a in range(n):
            for j, (cx, cy) in enumerate(chips):
                over_ici(outs, send_sems, recv_sems, a, j, (cx, cy), 2 * x + y).wait_send()
                over_d2d(outs, send_sems, recv_sems, a, j, 2 * cx + cy, c).wait_send()

    return _Carry(shards, [jax.ShapeDtypeStruct(s.shape, s.dtype) for s in shards],
                  {a: a for a in range(n)}, 6 * n, start, finish)


def _rs_pair(grads, name):
    n = len(grads)

    def body(*refs):
        ins, gots = refs[:n], refs[n:2 * n]
        send_sems, recv_sems = refs[2 * n:]
        x, y, c, _ = _place()
        cps = []
        for a in range(n):
            r2 = ins[a].shape[1] // 2
            cp = pltpu.make_async_remote_copy(
                src_ref=ins[a].at[:, pl.ds((1 - c) * r2, r2), :], dst_ref=gots[a],
                send_sem=send_sems.at[a], recv_sem=recv_sems.at[a],
                device_id=(x, y, 1 - c), device_id_type=MESH)
            cp.start()
            cps.append(cp)
        for cp in cps:
            cp.wait()

    half = [jax.ShapeDtypeStruct((NCHIP, g.shape[1] // 2, g.shape[2]), g.dtype) for g in grads]
    return pl.pallas_call(
        body, name=name, out_shape=half, in_specs=[ANY] * n, out_specs=[ANY] * n,
        scratch_shapes=[pltpu.SemaphoreType.DMA((n,)), pltpu.SemaphoreType.DMA((n,))],
        compiler_params=_cp(),
    )(*grads)


def _pair_carry(grads):
    n = len(grads)

    def copy(ins, outs, send_sems, recv_sems, a):
        x, y, c, _ = _place()
        r2 = ins[a].shape[1] // 2
        return pltpu.make_async_remote_copy(
            src_ref=ins[a].at[:, pl.ds((1 - c) * r2, r2), :], dst_ref=outs[a],
            send_sem=send_sems.at[a], recv_sem=recv_sems.at[a],
            device_id=(x, y, 1 - c), device_id_type=MESH)

    def start(ins, outs, send_sems, recv_sems):
        for a in range(n):
            copy(ins, outs, send_sems, recv_sems, a).start()

    def finish(ins, outs, send_sems, recv_sems):
        for a in range(n):
            copy(ins, outs, send_sems, recv_sems, a).wait()

    half = [jax.ShapeDtypeStruct((NCHIP, g.shape[1] // 2, g.shape[2]), g.dtype) for g in grads]
    return _Carry(grads, half, {}, n, start, finish)


def _chips_carry(parts, piece=(0, 1, 1), into=None):
    n = len(parts)
    first, count, of = piece

    def rows(ref):
        step = ref.shape[1] // of
        return pl.ds(first * step, count * step)

    def send(ins, outs, send_sems, recv_sems, a, j, chip_xy):
        x, y, c, _ = _place()
        me, them = 2 * x + y, 2 * chip_xy[0] + chip_xy[1]
        return pltpu.make_async_remote_copy(
            src_ref=ins[a].at[them, rows(ins[a]), :],
            dst_ref=outs[a].at[me - (me > them).astype(jnp.int32), rows(outs[a]), :],
            send_sem=send_sems.at[3 * a + j], recv_sem=recv_sems.at[3 * a + j],
            device_id=(*chip_xy, c), device_id_type=MESH)

    def start(ins, outs, send_sems, recv_sems):
        _, _, _, chips = _place()
        for a in range(n):
            for j, chip_xy in enumerate(chips):
                send(ins, outs, send_sems, recv_sems, a, j, chip_xy).start()

    def finish(ins, outs, send_sems, recv_sems):
        x, y, c, chips = _place()
        me = 2 * x + y
        for a in range(n):
            for j, (cx, cy) in enumerate(chips):
                them = 2 * cx + cy
                blk = outs[a].at[them - (them > me).astype(jnp.int32), rows(outs[a]), :]
                pltpu.make_async_remote_copy(
                    src_ref=blk, dst_ref=blk, send_sem=send_sems.at[3 * a + j], recv_sem=recv_sems.at[3 * a + j],
                    device_id=(cx, cy, c), device_id_type=MESH).wait_recv()
        for a in range(n):
            for j, chip_xy in enumerate(chips):
                send(ins, outs, send_sems, recv_sems, a, j, chip_xy).wait_send()

    landing = [jax.ShapeDtypeStruct((NCHIP - 1,) + p.shape[1:], p.dtype) for p in parts]
    if into is None:
        return _Carry(parts, landing, {}, 3 * n, start, finish)
    return _Carry(list(parts) + list(into), landing, {n + a: a for a in range(n)}, 3 * n, start, finish)


def _rs_swap(fulls):
    n = len(fulls)

    def body(*refs):
        outs = refs[n:2 * n]
        send_sems, recv_sems = refs[2 * n:]
        x, y, c, _ = _place()
        cps = []
        for a in range(n):
            r2 = outs[a].shape[1] // 2
            mine = outs[a].at[:, pl.ds(c * r2, r2), :]
            cp = pltpu.make_async_remote_copy(
                src_ref=mine, dst_ref=mine, send_sem=send_sems.at[a], recv_sem=recv_sems.at[a],
                device_id=(x, y, 1 - c), device_id_type=MESH)
            cp.start()
            cps.append(cp)
        for a in range(n):
            r2 = outs[a].shape[1] // 2
            blk = outs[a].at[:, pl.ds((1 - c) * r2, r2), :]
            pltpu.make_async_remote_copy(
                src_ref=blk, dst_ref=blk, send_sem=send_sems.at[a], recv_sem=recv_sems.at[a],
                device_id=(x, y, 1 - c), device_id_type=MESH).wait_recv()
        for cp in cps:
            cp.wait_send()

    return pl.pallas_call(
        body, name="rs_swap", out_shape=[jax.ShapeDtypeStruct(f.shape, f.dtype) for f in fulls],
        in_specs=[ANY] * n, out_specs=[ANY] * n, input_output_aliases={a: a for a in range(n)},
        scratch_shapes=[pltpu.SemaphoreType.DMA((n,)), pltpu.SemaphoreType.DMA((n,))],
        compiler_params=_cp(),
    )(*fulls)


def _tail_weight_grads(merged_t, b_in_t, a_in_t, dy, dbr_b, dbr_a, name, tn=256):
    S = dy.shape[0]
    nn = D // tn

    def body(mt_ref, bt_ref, at_ref, dy_ref, db_ref, da_ref, go_ref, gh_ref, gp_ref):
        go_ref[...] = jnp.dot(mt_ref[...], dy_ref[...], preferred_element_type=F32).astype(BF16)
        gh_ref[...] = jnp.dot(bt_ref[...], db_ref[...], preferred_element_type=F32).astype(BF16)
        gp_ref[...] = jnp.dot(at_ref[...], da_ref[...], preferred_element_type=F32).astype(BF16)

    left = lambda rows: pl.BlockSpec((rows, S), lambda n: (0, 0))
    right = pl.BlockSpec((S, tn), lambda n: (0, n))
    out = pl.BlockSpec((D, tn), lambda n: (0, n))
    return pl.pallas_call(
        body, name=name, grid=(nn,), in_specs=[left(D), left(D), left(POOL_W), right, right, right],
        out_specs=[out, out, pl.BlockSpec((None, POOL_W, tn), lambda n: (n, 0, 0))],
        out_shape=[jax.ShapeDtypeStruct((D, D), BF16), jax.ShapeDtypeStruct((D, D), BF16),
                   jax.ShapeDtypeStruct((NCHIP, POOL_W, D // NCHIP), BF16)],
        compiler_params=_cp(("parallel",)),
    )(merged_t, b_in_t, a_in_t, dy, dbr_b, dbr_a)


class _GatherInProj:
    def __init__(self, slot, order):
        self.slot, self.order = slot, order


def _proj_with_gather(h, w_slot, order, name, tn=256):
    S, K = h.shape
    nsh, _, ns = w_slot.shape
    tps = ns // tn
    nt = nsh * tps
    r2 = K // 2

    def body(ord_ref, h_ref, w_in_ref, o_ref, w_ref, wbuf, tile_sems, send_sems, recv_sems):
        n = pl.program_id(0)
        x, y, c, chips = _place()

        def half(slot, which):
            return w_ref.at[slot, pl.ds(which * r2, r2), :]

        def over_ici(j, slot):
            blk = half(slot, c)
            return pltpu.make_async_remote_copy(src_ref=blk, dst_ref=blk, send_sem=send_sems.at[j],
                                                recv_sem=recv_sems.at[j], device_id=(*chips[j], c),
                                                device_id_type=MESH)

        def over_d2d(j, which):
            blk = half(2 * chips[j][0] + chips[j][1], which)
            return pltpu.make_async_remote_copy(src_ref=blk, dst_ref=blk, send_sem=send_sems.at[3 + j],
                                                recv_sem=recv_sems.at[3 + j], device_id=(x, y, 1 - c),
                                                device_id_type=MESH)

        def tile_copy(step, slot):
            shard = ord_ref[step // tps]
            return pltpu.make_async_copy(w_ref.at[shard, :, pl.ds((step % tps) * tn, tn)], wbuf.at[slot],
                                         tile_sems.at[slot])

        @pl.when(n == 0)
        def _():
            for j in range(3):
                over_ici(j, 2 * x + y).start()
            tile_copy(0, 0).start()

        for j in range(3):
            @pl.when(n == (j + 1) * tps - 1)
            def _(j=j):
                over_ici(j, 2 * chips[j][0] + chips[j][1]).wait_recv()
                over_d2d(j, c).start()
                over_d2d(j, 1 - c).wait_recv()

        @pl.when(n + 1 < nt)
        def _():
            tile_copy(n + 1, (n + 1) % 2).start()

        tile_copy(n, n % 2).wait()
        o_ref[...] = jnp.dot(h_ref[...], wbuf[n % 2], preferred_element_type=F32).astype(o_ref.dtype)

        @pl.when(n == nt - 1)
        def _():
            for j in range(3):
                over_ici(j, 2 * x + y).wait_send()
                over_d2d(j, c).wait_send()

    return pl.pallas_call(
        body, name=name,
        out_shape=[jax.ShapeDtypeStruct((S, nsh * ns), BF16), jax.ShapeDtypeStruct(w_slot.shape, w_slot.dtype)],
        grid_spec=pltpu.PrefetchScalarGridSpec(
            num_scalar_prefetch=1, grid=(nt,),
            in_specs=[pl.BlockSpec((S, K), lambda n, o_ref: (0, 0)), ANY],
            out_specs=[pl.BlockSpec((S, tn), lambda n, o_ref: (0, o_ref[n // tps] * tps + n % tps)), ANY],
            scratch_shapes=[pltpu.VMEM((2, K, tn), w_slot.dtype), pltpu.SemaphoreType.DMA((2,)),
                            pltpu.SemaphoreType.DMA((6,)), pltpu.SemaphoreType.DMA((6,))]),
        input_output_aliases={2: 1},
        compiler_params=_cp(("arbitrary",)),
    )(order, h, w_slot)


def _mm_ride(a, b, carry, **kw):
    if carry is None:
        return _mm(a, b, **kw), []
    return _mm(a, b, carry=carry, **kw)


def _layer_fwd(l, x, ada, w, small, ride, target=None):
    shift, scale, gate = ada[:, 0:D], ada[:, D:2 * D], ada[:, 2 * D:3 * D]
    carry, landed = ride("prenorm")
    (h, h_t), outs = _prenorm_fwd(x, small["g_pre"][l], scale, shift, f"prenorm_fwd{l}", carry)
    landed(outs)
    carry, landed = ride("proj")
    if isinstance(carry, _GatherInProj):
        proj, full = _proj_with_gather(h, carry.slot, carry.order, f"proj{l}")
        outs = [full]
    else:
        proj, outs = _mm_ride(h, w["w_in"][l], carry, name=f"proj{l}", b_mode="nn_sh", tm=2048, out_dtype=BF16)
    landed(outs)
    carry, landed = ride("pool")
    (a_in, a_in_t), outs = _pool_fwd(proj, small["pool_w"][l], small["pool_scale"][l], f"pool_fwd{l}", carry)
    landed(outs)
    carry, landed = ride("hgrn")
    (b_in, b_in_t, o_raw, states, mild, cum), outs = _hgrn_fwd(proj, small["lb"][l], small["hgrn_norm_g"][l],
                                                              f"hgrn_fwd{l}", carry=carry)
    landed(outs)
    carry, landed = ride("tail")
    (br_a, br_b, merged_t, y, *x_new), outs = _layer_tail_fwd(
        proj, a_in, b_in, x, w["w_pool_o"][l], w["w_hgrn_o"][l].reshape(D, D), w["w_out"][l].reshape(D, D),
        gate, small["g_post"][l], f"tail_fwd{l}", target=target, carry=carry)
    landed(outs)
    saved = dict(x=x, h_t=h_t, proj=proj, a_in_t=a_in_t, b_in_t=b_in_t, o_raw=o_raw, states=states, mild=mild,
                 cum=cum,
                 br_a=br_a, br_b=br_b, merged_t=merged_t, y=y, scale=scale, gate=gate)
    return x_new, saved


def _layer_bwd(l, dxn, sv, w, small, ride):
    carry, landed = ride["head"](None)
    (dy, dbr_a, dbr_b, dproj, da_in, db_in, dgate, dg_post), outs = _layer_head_bwd(
        dxn, sv["y"], sv["proj"], sv["br_a"], sv["br_b"], w["w_pool_o"][l], w["w_hgrn_o"][l].reshape(D, D),
        w["w_out"][l].reshape(D, D), sv["gate"], small["g_post"][l], f"head_bwd{l}", carry)
    landed(outs)
    gw_out, gw_hgrn_o, gw_pool_o = _tail_weight_grads(sv["merged_t"], sv["b_in_t"], sv["a_in_t"], dy, dbr_b,
                                                      dbr_a, f"gw_tail{l}")
    big = dict(w_pool_o=gw_pool_o, w_hgrn_o=gw_hgrn_o.reshape(NCHIP, D // NCHIP, D),
               w_out=gw_out.reshape(NCHIP, D // NCHIP, D))
    carry, landed = ride["hgrn"](big)
    (dproj, dlb, dgn), outs = _hgrn_bwd(db_in, sv["proj"], sv["o_raw"], sv["states"], sv["mild"], sv["cum"],
                                        small["lb"][l], small["hgrn_norm_g"][l], dproj, f"hgrn_bwd{l}",
                                        carry=carry)
    landed(outs)
    dproj, dpw, dpsc = _pool_bwd(da_in, sv["proj"], small["pool_w"][l], small["pool_scale"][l], dproj,
                                 f"pool_bwd{l}")
    little = dict(dgate=dgate, g_post=dg_post, pool_w=dpw, pool_scale=dpsc, lb=dlb,
                  hgrn_norm_g=jnp.sum(dgn, axis=0, keepdims=True))
    carry, landed = ride["gw_in"](little)
    big["w_in"], outs = _mm_ride(sv["h_t"], dproj, carry, name=f"gw_in{l}", out_shards=NCHIP, out_dtype=BF16)
    landed(outs)
    carry, landed = ride["d_h"](big)
    dh, outs = _mm_ride(dproj, w["w_in"][l], carry, name=f"d_h{l}", b_mode="nt_shk", tn=1024)
    landed(outs)
    carry, landed = ride["prenorm"](big)
    (dx, dshift, dscale, dg_pre), outs = _prenorm_bwd(dh, dxn, sv["x"], small["g_pre"][l], sv["scale"],
                                                      f"prenorm_bwd{l}", carry)
    landed(outs)
    little.update(dshift=dshift, dscale=dscale, g_pre=dg_pre)
    return dx, big, little


SMALL_ROWS = 176


def _rows8(t):
    t = t.reshape(-1, D)
    return jnp.pad(t, ((0, -t.shape[0] % 8), (0, 0)))


def _pack_small(parts):
    row_keys = ("dshift", "dscale", "dgate", "g_pre", "g_post", "lb", "pool_scale", "hgrn_norm_g")
    flat = [p[k] for p in parts for k in row_keys] + [p["pool_w"].reshape(GROUPS * 128 * 128 // D, D) for p in parts]
    nk = len(row_keys)

    def body(*refs):
        o_ref = refs[-1]
        o_ref[...] = jnp.zeros((SMALL_ROWS, D), F32)
        for l in range(2):
            dshift, dscale, dgate, g_pre, g_post, lb, pscale, gn = refs[l * nk:(l + 1) * nk]
            for r, ref in enumerate((dshift, dscale, dgate)):
                o_ref[3 * l + r:3 * l + r + 1, :] = ref[...]
            o_ref[8 + l:9 + l, :] = g_pre[...]
            o_ref[16 + l:17 + l, :] = g_post[...]
            o_ref[24 + l:25 + l, :] = lb[...]
            o_ref[160:161, l * POOL_W:(l + 1) * POOL_W] = pscale[...]
            o_ref[168:169, l * HD:(l + 1) * HD] = gn[...]
            pw = refs[2 * nk + l]
            rows = pw.shape[0]
            o_ref[32 + l * rows:32 + (l + 1) * rows, :] = pw[...]

    return pl.pallas_call(body, name="pack_small", out_shape=jax.ShapeDtypeStruct((SMALL_ROWS, D), F32),
                          compiler_params=_cp())(*flat)


def _unpack_small(p):
    return (p[0:6].reshape(2, 3 * D), p[8:10], p[16:18], p[24:26], p[32:160].reshape(2, GROUPS, 128, 128),
            p[160:161].reshape(2, POOL_W), p[168:169, 0:2 * HD].reshape(2, HD))


def kernel(x, c, w_ada, b_ada, g_pre, g_post, w_in, pool_w, pool_scale, lb_logits, hgrn_norm_g, w_pool_o, w_hgrn_o, w_out, loss_target, m_w_ada, m_b_ada, m_g_pre, m_g_post, m_w_in, m_pool_w, m_pool_scale, m_lb_logits, m_hgrn_norm_g, m_w_pool_o, m_w_hgrn_o, m_w_out, v_w_ada, v_b_ada, v_g_pre, v_g_post, v_w_in, v_pool_w, v_pool_scale, v_lb_logits, v_hgrn_norm_g, v_w_pool_o, v_w_hgrn_o, v_w_out):
    ax, ay, ac = lax.axis_index("x"), lax.axis_index("y"), lax.axis_index("c")
    chip = 2 * ax + ay
    dev = 2 * chip + ac
    xe, te = x[0], loss_target[0]
    ada_s = w_ada.shape[2]

    big_names = ("w_in", "w_pool_o", "w_hgrn_o", "w_out")
    big_w = (w_in, w_pool_o, w_hgrn_o, w_out)
    core = jnp.stack([ac]).astype(jnp.int32)
    place = jnp.stack([chip, ac]).astype(jnp.int32)
    slots = {("w_in", l): _cast_to_slot(place, w_in, l, f"cast_w_in{l}") for l in range(2)}
    rest = [(k, l) for l in range(2) for k in big_names[1:]]
    slots.update(zip(rest, _cast_to_slots(place, [(dict(zip(big_names, big_w))[k], l) for k, l in rest],
                                          "cast_rest")))
    w = {k: [None, None] for k in big_names}
    def fills(keys):
        def landed(outs):
            for (k, l), o in zip(keys, outs):
                w[k][l] = slots[k, l] = o
        return landed

    rest0 = [(k, 0) for k in big_names[1:]]
    rest1 = [(k, 1) for k in big_names[1:]]
    no_carry = (None, lambda outs: None)
    order = jnp.stack([chip, 2 * (1 - ax) + ay, 2 * ax + (1 - ay), 2 * (1 - ax) + (1 - ay)]).astype(jnp.int32)

    def ride_fwd0(stage):
        if stage == "proj":
            return _GatherInProj(slots["w_in", 0], order), fills([("w_in", 0)])
        if stage == "pool":
            return _gather_carry([slots["w_in", 1]], piece=(0, 1, 8)), fills([("w_in", 1)])
        if stage == "hgrn":
            return (_join_carries(_gather_carry([slots[t] for t in rest0]),
                                  _gather_carry([slots["w_in", 1]], piece=(1, 3, 8))),
                    fills(rest0 + [("w_in", 1)]))
        if stage == "tail":
            return _gather_carry([slots["w_in", 1]], piece=(4, 2, 8)), fills([("w_in", 1)])
        return no_carry

    def ride_fwd1(stage):
        if stage == "prenorm":
            return _gather_carry([slots["w_in", 1]], piece=(6, 2, 8)), fills([("w_in", 1)])
        if stage == "hgrn":
            return _gather_carry([slots[t] for t in rest1]), fills(rest1)
        return no_carry

    c_all = _gather_small(jnp.broadcast_to(c, (8, D)), "gather_c").reshape(NDEV, 8, D)[:, 0, :]
    c_pad = jnp.pad(c_all, ((0, ADA_PAD - NDEV), (0, 0)))
    b_sh = lax.dynamic_slice(b_ada, (0, chip * ada_s), (2, ada_s))
    ada_cols = _gather_small(_ada_fwd(c_pad, w_ada, b_sh), "gather_ada")
    ada_cols = ada_cols.reshape(NCHIP, 2, NDEV, 2, ada_s)[:, 0]
    ada_all = jnp.transpose(ada_cols, (2, 1, 0, 3)).reshape(2, NDEV, 3 * D)
    ada_me = lax.dynamic_slice(ada_all, (0, dev, 0), (2, 1, 3 * D))

    lbs = _lb_fwd(lb_logits)
    small = dict(g_pre=g_pre[:, None, :], g_post=g_post[:, None, :], pool_w=pool_w,
                 pool_scale=pool_scale[:, None, :], lb=lbs[:, None, :], hgrn_norm_g=hgrn_norm_g[:, None, :])

    (x1,), sv0 = _layer_fwd(0, xe, ada_me[0], w, small, ride_fwd0)
    (dx2, loss_blk), sv1 = _layer_fwd(1, x1, ada_me[1], w, small, ride_fwd1, target=te)

    parts, recv, held = {}, {}, {}

    def pair_ride(keys, grads):
        def landed(outs):
            held.update({kl: (g, o) for kl, g, o in zip(keys, grads, outs)})
        return _pair_carry(grads), landed

    def pair_adds(keys):
        gs, gots = zip(*[held.pop(kl) for kl in keys])
        if keys[0][0] == "w_in":
            parts[keys[0]] = _pair_add(core, gs[0], gots[0], f"rs_add_w_in{keys[0][1]}")
        else:
            parts.update(zip(keys, _pair_adds(core, gs, gots, f"rs_add_early{keys[0][1]}")))

    def exchange(keys):
        def landed(outs):
            recv.update(zip(keys, outs))
        return _chips_carry([parts[kl] for kl in keys]), landed

    def share(key, first, count):
        def landed(outs):
            (recv[key],) = outs
        into = [recv[key]] if key in recv else None
        return _chips_carry([parts[key]], piece=(first, count, 8), into=into), landed

    def together(*rides):
        carries, fns = zip(*rides)

        def landed(outs):
            for cr, fn in zip(carries, fns):
                fn(outs[:len(cr.outs)])
                outs = outs[len(cr.outs):]
        return _join_carries(*carries), landed

    def early(l):
        return [(k, l) for k in big_names[1:]]

    def pair_alone(keys, grads, tag):
        held.update({kl: (g, o) for kl, g, o in zip(keys, grads, _rs_pair(grads, f"rs_pair_{tag}"))})
        pair_adds(keys)

    def ride_hgrn1(big):
        return pair_ride(early(1), [big[k] for k in big_names[1:]])

    def ride_gw_in1(_):
        pair_adds(early(1))
        return exchange(early(1))

    def ride_d_h1(big):
        return pair_ride([("w_in", 1)], [big["w_in"]])

    def ride_prenorm1(_):
        pair_adds([("w_in", 1)])
        return share(("w_in", 1), 0, 1)

    def ride_head0(_):
        return share(("w_in", 1), 1, 3)

    def ride_hgrn0(big):
        pair_alone(early(0), [big[k] for k in big_names[1:]], "early0")
        return together(exchange(early(0)), share(("w_in", 1), 4, 4))

    def ride_d_h0(big):
        pair_alone([("w_in", 0)], [big["w_in"]], "w_in0")
        return share(("w_in", 0), 0, 4)

    def ride_prenorm0(_):
        return share(("w_in", 0), 4, 4)

    no_ride = lambda so_far: no_carry
    dx1, big1, little1 = _layer_bwd(1, dx2, sv1, w, small, dict(head=no_ride, hgrn=ride_hgrn1, gw_in=ride_gw_in1,
                                                                d_h=ride_d_h1, prenorm=ride_prenorm1))

    gathered = {}
    zero_row = jnp.zeros((1, D), F32)

    def ride_gw_in0(little):
        so_far = dict(little, dshift=zero_row, dscale=zero_row, g_pre=zero_row)

        def landed(outs):
            (gathered["early"],) = outs
        return _gather_rows_carry(_pack_small([so_far, little1])), landed

    dx0, big0, little0 = _layer_bwd(0, dx1, sv0, w, small,
                                    dict(head=ride_head0, hgrn=ride_hgrn0, gw_in=ride_gw_in0, d_h=ride_d_h0,
                                         prenorm=ride_prenorm0))
    loss_row = jnp.broadcast_to(loss_blk[0:1, 0:1], (1, D))
    late = _rows8(jnp.stack([little0["dshift"], little0["dscale"], little0["g_pre"], loss_row]))
    late = _gather_small(late, "gather_small_late").reshape(NDEV, 8, D)
    loss = jnp.sum(late[:, 3, 0])
    packed = gathered["early"].reshape(NDEV, SMALL_ROWS, D)
    packed = packed.at[:, 0:2, :].set(late[:, 0:2, :]).at[:, 8:9, :].set(late[:, 2:3, :])
    red = _chip_sum(place, parts["w_in", 1], recv["w_in", 1], 1, None, "rs_sum_w_in1")
    red = [_chip_sum(place, parts["w_in", 0], recv["w_in", 0], 0, red, "rs_sum_w_in0")]
    red += _chip_sums(place, [[parts[k, l] for l in range(2)] for k in big_names[1:]],
                      [[recv[k, l] for l in range(2)] for k in big_names[1:]], "rs_sum_early")
    g_big = dict(zip(big_names, _rs_swap(red)))

    def two(t):
        return t.reshape(-1, t.shape[-1])

    def upd(wt, g, m, v, name, echo=False):
        return [t.reshape(wt.shape) for t in _adamw(two(wt), two(g), two(m), two(v), name, echo)]

    *u_w_in, g_w_in = upd(w_in, g_big["w_in"], m_w_in, v_w_in, "adamw_w_in", echo=True)
    g_small = _sum_devices(packed)
    g_b_ada, g_g_pre, g_g_post, g_lb, g_pool_w, g_pool_scale, g_norm_g = _unpack_small(g_small)
    g_lb_logits = _lb_bwd(lb_logits, g_lb)
    d_ada_all = packed[:, 0:6, :].reshape(NDEV, 2, 3 * D)
    d_ada_sh = lax.dynamic_slice(jnp.transpose(d_ada_all, (1, 0, 2)), (0, 0, chip * ada_s), (2, NDEV, ada_s))
    d_ada_sh = jnp.pad(d_ada_sh, ((0, 0), (0, ADA_PAD - NDEV), (0, 0)))
    g_w_ada = _ada_wgrad(c_pad.T, d_ada_sh)

    u_w_ada = upd(w_ada, g_w_ada, m_w_ada, v_w_ada, "adamw_w_ada")
    early_w = dict(w_pool_o=(w_pool_o, m_w_pool_o, v_w_pool_o), w_hgrn_o=(w_hgrn_o, m_w_hgrn_o, v_w_hgrn_o),
                   w_out=(w_out, m_w_out, v_w_out))
    u_early = _adamw_many([(two(early_w[k][0]), two(g_big[k]), two(early_w[k][1]), two(early_w[k][2]))
                           for k in big_names[1:]], "adamw_early")
    (*u_w_pool_o, g_w_pool_o), (*u_w_hgrn_o, g_w_hgrn_o), (*u_w_out, g_w_out) = [
        [t.reshape(early_w[k][0].shape) for t in four] for k, four in zip(big_names[1:], u_early)]
    small_w = dict(b_ada=(b_ada, m_b_ada, v_b_ada), g_pre=(g_pre, m_g_pre, v_g_pre),
                   g_post=(g_post, m_g_post, v_g_post), lb_logits=(lb_logits, m_lb_logits, v_lb_logits),
                   pool_w=(pool_w, m_pool_w, v_pool_w), pool_scale=(pool_scale, m_pool_scale, v_pool_scale),
                   hgrn_norm_g=(hgrn_norm_g, m_hgrn_norm_g, v_hgrn_norm_g))
    in_rows = [tuple(t.reshape(rows, width) for t in small_w[key]) for key, _, rows, width in SMALL_PARTS]
    u_rows = _adamw_small(g_small, g_lb_logits, in_rows)
    u_small = {key: [t.reshape(small_w[key][0].shape) for t in u_rows[p]]
               for p, (key, _, _, _) in enumerate(SMALL_PARTS)}

    grads_out = (g_w_ada, g_b_ada, g_g_pre, g_g_post, g_w_in, g_pool_w, g_pool_scale, g_lb_logits,
                 g_norm_g, g_w_pool_o, g_w_hgrn_o, g_w_out)

    def ordered(k):
        s = lambda key: u_small[key][k]
        return (u_w_ada[k], s("b_ada"), s("g_pre"), s("g_post"), u_w_in[k], s("pool_w"), s("pool_scale"),
                s("lb_logits"), s("hgrn_norm_g"), u_w_pool_o[k], u_w_hgrn_o[k], u_w_out[k])

    return (loss, dx0[None], *grads_out, *ordered(0), *ordered(1), *ordered(2))
```

```python
import functools

import jax
import jax.numpy as jnp
from jax import lax
from jax.experimental import pallas as pl
from jax.experimental.pallas import tpu as pltpu

F32 = jnp.float32
BF16 = jnp.bfloat16
MESH = pl.DeviceIdType.MESH

D = 1024
HEADS = 8
HD = 128
GROUPS = 4
POOL_W = 512
CH = 128
SB_WIDE = 32
SB = 16
NH = 2
IN_W = 7168
NCHIP = 4
NDEV = 8
EPS = 1e-6
PV0, PG0, HQ0, HF0, HI0, HG0 = 0, 4, 8, 16, 24, 32
MGP_BLK, MGH_BLK = 5, 6

LR, B1, B2, AEPS, WD, STEP = 0.001, 0.9, 0.999, 1e-08, 0.01, 10
VMEM_LIMIT = 56 * 1024 * 1024


def _cp(sem=None, **kw):
    if sem is not None:
        kw["dimension_semantics"] = sem
    return pltpu.CompilerParams(vmem_limit_bytes=VMEM_LIMIT, **kw)


def _sig(z):
    return 1.0 / (1.0 + jnp.exp(-z))


def _dsilu(z, s):
    return s * (1.0 + z * (1.0 - s))


def _row_tile(rows, cap):
    if rows <= cap:
        return rows
    t = 1 << (cap.bit_length() - 1)
    while rows % t:
        t //= 2
    return t


ANY = pl.BlockSpec(memory_space=pl.ANY)


class _Carry:
    def __init__(self, ins, outs, aliases, n_sem, start, finish):
        self.ins, self.outs, self.aliases, self.n_sem = list(ins), list(outs), dict(aliases), n_sem
        self.start, self.finish = start, finish


class _SemWindow:
    def __init__(self, ref, base):
        self._ref, self._base = ref, base

    @property
    def at(self):
        return self

    def __getitem__(self, k):
        return self._ref.at[self._base + k]


def _join_carries(*carries):
    ins, outs, aliases, spans, n_sem = [], [], {}, [], 0
    for cr in carries:
        aliases.update({len(ins) + i: len(outs) + o for i, o in cr.aliases.items()})
        spans.append((len(ins), len(cr.ins), len(outs), len(cr.outs), n_sem))
        ins, outs, n_sem = ins + cr.ins, outs + cr.outs, n_sem + cr.n_sem

    def run(which):
        def fn(i_refs, o_refs, send_sems, recv_sems):
            for cr, (i0, ni, o0, no, s0) in zip(carries, spans):
                getattr(cr, which)(i_refs[i0:i0 + ni], o_refs[o0:o0 + no], _SemWindow(send_sems, s0),
                                   _SemWindow(recv_sems, s0))
        return fn

    return _Carry(ins, outs, aliases, n_sem, run("start"), run("finish"))


def _call(body, *, name, grid, in_specs, out_specs, out_shape, args, scratch_shapes=(), sem=None, carry=None,
          aliases=None):
    in_specs, out_specs, out_shape = list(in_specs), list(out_specs), list(out_shape)
    scratch_shapes = list(scratch_shapes)
    aliases = dict(aliases or {})
    if carry is None:
        outs = pl.pallas_call(body, name=name, grid=grid, in_specs=in_specs, out_specs=out_specs,
                              out_shape=out_shape, scratch_shapes=scratch_shapes, input_output_aliases=aliases,
                              compiler_params=_cp(sem))(*args)
        return list(outs)
    n_in, n_out, n_scr = len(in_specs), len(out_specs), len(scratch_shapes)
    c_in, c_out = len(carry.ins), len(carry.outs)

    def wrapped(*refs):
        k_in, rest = refs[:n_in], refs[n_in:]
        ci, rest = rest[:c_in], rest[c_in:]
        k_out, rest = rest[:n_out], rest[n_out:]
        co, rest = rest[:c_out], rest[c_out:]
        k_scr, (ssem, rsem) = rest[:n_scr], rest[n_scr:]
        pids = [pl.program_id(d) for d in range(len(grid))]
        first = functools.reduce(jnp.logical_and, [p == 0 for p in pids])
        last = functools.reduce(jnp.logical_and, [p == g - 1 for p, g in zip(pids, grid)])

        @pl.when(first)
        def _():
            carry.start(ci, co, ssem, rsem)

        body(*k_in, *k_out, *k_scr)

        @pl.when(last)
        def _():
            carry.finish(ci, co, ssem, rsem)

    outs = pl.pallas_call(
        wrapped, name=name, grid=grid, in_specs=in_specs + [ANY] * c_in, out_specs=out_specs + [ANY] * c_out,
        out_shape=out_shape + carry.outs,
        input_output_aliases={**aliases, **{n_in + i: n_out + o for i, o in carry.aliases.items()}},
        scratch_shapes=scratch_shapes + [pltpu.SemaphoreType.DMA((carry.n_sem,))] * 2,
        compiler_params=_cp(("arbitrary",) * len(grid)),
    )(*args, *carry.ins)
    return list(outs)


def _mm(a, b, *, name, b_mode="nn", out_shards=0, tm=1024, tn=256, tk=None, out_dtype=F32, carry=None):
    assert b_mode in ("nn", "nn_sh", "nt_shk"), b_mode
    M, K = a.shape
    if b_mode == "nn":
        N = b.shape[1]
    elif b_mode == "nn_sh":
        N = b.shape[0] * b.shape[2]
    else:
        N = b.shape[1]
    tm = _row_tile(M, tm)
    if b_mode == "nn_sh":
        tn = _row_tile(b.shape[2], tn)
    elif out_shards:
        tn = _row_tile(N // out_shards, tn)
    else:
        tn = _row_tile(N, tn)
    if tk is None:
        tk = K if b_mode != "nt_shk" else b.shape[2]
    if b_mode == "nt_shk":
        tk = _row_tile(b.shape[2], tk)
    nm, nn, nk = M // tm, N // tn, K // tk

    a_spec = pl.BlockSpec((tm, tk), lambda m, n, k: (m, k))
    if b_mode == "nn":
        b_spec = pl.BlockSpec((tk, tn), lambda m, n, k: (k, n))
    elif b_mode == "nn_sh":
        nps = b.shape[2] // tn
        b_spec = pl.BlockSpec((None, tk, tn), lambda m, n, k: (n // nps, k, n % nps))
    else:
        kps = b.shape[2] // tk
        b_spec = pl.BlockSpec((None, tn, tk), lambda m, n, k: (k // kps, n, k % kps))
    if out_shards:
        ops = (N // out_shards) // tn
        o_spec = pl.BlockSpec((None, tm, tn), lambda m, n, k: (n // ops, m, n % ops))
        o_shape = jax.ShapeDtypeStruct((out_shards, M, N // out_shards), out_dtype)
    else:
        o_spec = pl.BlockSpec((tm, tn), lambda m, n, k: (m, n))
        o_shape = jax.ShapeDtypeStruct((M, N), out_dtype)
    dn = (((1,), (1,)), ((), ())) if b_mode == "nt_shk" else (((1,), (0,)), ((), ()))

    def body(a_ref, b_ref, o_ref, acc_ref):
        k = pl.program_id(2)

        @pl.when(k == 0)
        def _():
            acc_ref[...] = jnp.zeros(acc_ref.shape, F32)

        acc_ref[...] += lax.dot_general(a_ref[...].astype(BF16), b_ref[...].astype(BF16), dn,
                                        preferred_element_type=F32)

        @pl.when(k == nk - 1)
        def _():
            o_ref[...] = acc_ref[...].astype(o_ref.dtype)

    outs = _call(body, name=name, grid=(nm, nn, nk), in_specs=[a_spec, b_spec], out_specs=[o_spec],
                 out_shape=[o_shape], scratch_shapes=[pltpu.VMEM((tm, tn), F32)],
                 sem=("parallel", "parallel", "arbitrary"), args=(a, b), carry=carry)
    return outs[0] if carry is None else (outs[0], outs[1:])


def _rowvec(n=D):
    return pl.BlockSpec((1, n), lambda i: (0, 0))


def _prenorm_fwd(x, g, scale, shift, name, carry=None):
    S = x.shape[0]
    tr = _row_tile(S, 256)

    def body(x_ref, g_ref, sc_ref, sh_ref, h_ref, ht_ref):
        xv = x_ref[...]
        r = lax.rsqrt(jnp.mean(xv * xv, axis=-1, keepdims=True) + EPS)
        hv = (xv * r) * g_ref[...] * (1.0 + sc_ref[...]) + sh_ref[...]
        h_ref[...] = hv.astype(BF16)
        ht_ref[...] = hv.T.astype(BF16)

    outs = _call(
        body, name=name, grid=(S // tr,),
        in_specs=[pl.BlockSpec((tr, D), lambda i: (i, 0)), _rowvec(), _rowvec(), _rowvec()],
        out_specs=[pl.BlockSpec((tr, D), lambda i: (i, 0)), pl.BlockSpec((D, tr), lambda i: (0, i))],
        out_shape=[jax.ShapeDtypeStruct((S, D), BF16), jax.ShapeDtypeStruct((D, S), BF16)],
        sem=("parallel",), args=(x, g, scale, shift), carry=carry)
    return outs[:2], outs[2:]


def _prenorm_bwd(dh, dxn, x, g, scale, name, carry=None):
    S = x.shape[0]
    tr = _row_tile(S, 256)

    def body(dh_ref, dxn_ref, x_ref, g_ref, sc_ref, dx_ref, dsh_ref, dsc_ref, dg_ref):
        i = pl.program_id(0)

        @pl.when(i == 0)
        def _():
            dsh_ref[...] = jnp.zeros((1, D), F32)
            dsc_ref[...] = jnp.zeros((1, D), F32)
            dg_ref[...] = jnp.zeros((1, D), F32)

        xv = x_ref[...]
        dhv = dh_ref[...]
        gv = g_ref[...]
        mod = 1.0 + sc_ref[...]
        r = lax.rsqrt(jnp.mean(xv * xv, axis=-1, keepdims=True) + EPS)
        xh = xv * r
        dsh_ref[...] += jnp.sum(dhv, axis=0, keepdims=True)
        dsc_ref[...] += jnp.sum(dhv * (xh * gv), axis=0, keepdims=True)
        dg_ref[...] += jnp.sum(dhv * mod * xh, axis=0, keepdims=True)
        u = dhv * mod * gv
        dx_ref[...] = dxn_ref[...] + r * u - xv * (r * r * r) * jnp.mean(u * xv, axis=-1, keepdims=True)

    tile = pl.BlockSpec((tr, D), lambda i: (i, 0))
    outs = _call(
        body, name=name, grid=(S // tr,),
        in_specs=[tile, tile, tile, _rowvec(), _rowvec()],
        out_specs=[tile, _rowvec(), _rowvec(), _rowvec()],
        out_shape=[jax.ShapeDtypeStruct((S, D), F32)] + [jax.ShapeDtypeStruct((1, D), F32)] * 3,
        sem=("arbitrary",), args=(dh, dxn, x, g, scale), carry=carry, aliases={1: 0})
    return outs[:4], outs[4:]


def _layer_tail_fwd(proj, a_in, b_in, x, w_po, w_ho, w_out, gate, g, name, target=None, carry=None):
    S = proj.shape[0]
    tr = _row_tile(S, 256)
    nsh, _, wsh = w_po.shape
    n_in = 10 + (target is not None)

    def body(*refs):
        (mgp_ref, mgh_ref, a_ref, b_ref, x_ref, wpo_ref, who_ref, wout_ref, gate_ref, g_ref) = refs[:10]
        bra_ref, brb_ref, mt_ref, y_ref, xn_ref = refs[n_in:n_in + 5]
        av = a_ref[...]
        bra = jnp.concatenate([jnp.dot(av, wpo_ref[j], preferred_element_type=F32) for j in range(nsh)], axis=1)
        brb = jnp.dot(b_ref[...], who_ref[...], preferred_element_type=F32)
        mv = _sig(mgp_ref[...].astype(F32)) * bra + _sig(mgh_ref[...].astype(F32)) * brb
        bra_ref[...] = bra.astype(BF16)
        brb_ref[...] = brb.astype(BF16)
        mt_ref[...] = mv.T.astype(BF16)
        yv = jnp.dot(mv.astype(BF16), wout_ref[...], preferred_element_type=F32)
        y_ref[...] = yv
        r = lax.rsqrt(jnp.mean(yv * yv, axis=-1, keepdims=True) + EPS)
        xn = x_ref[...] + gate_ref[...] * ((yv * r) * g_ref[...])
        if target is None:
            xn_ref[...] = xn
        else:
            t_ref, l_ref = refs[10], refs[n_in + 5]

            @pl.when(pl.program_id(0) == 0)
            def _():
                l_ref[...] = jnp.zeros((8, 128), F32)

            err = xn - t_ref[...]
            xn_ref[...] = err * (1.0 / D)
            l_ref[...] += 0.5 * jnp.sum(jnp.mean(err * err, axis=-1, keepdims=True))

    tile = pl.BlockSpec((tr, D), lambda i: (i, 0))
    whole = lambda t: pl.BlockSpec(t.shape, lambda i: (0,) * t.ndim)
    last = target is not None
    outs = _call(
        body, name=name, grid=(S // tr,),
        in_specs=[pl.BlockSpec((tr, D), lambda i: (i, MGP_BLK)), pl.BlockSpec((tr, D), lambda i: (i, MGH_BLK)),
                  pl.BlockSpec((tr, POOL_W), lambda i: (i, 0)), tile, tile, whole(w_po), whole(w_ho),
                  whole(w_out), _rowvec(), _rowvec()] + [tile] * last,
        out_specs=[tile, tile, pl.BlockSpec((D, tr), lambda i: (0, i)), tile, tile]
        + [pl.BlockSpec((8, 128), lambda i: (0, 0))] * last,
        out_shape=[jax.ShapeDtypeStruct((S, D), BF16), jax.ShapeDtypeStruct((S, D), BF16),
                   jax.ShapeDtypeStruct((D, S), BF16), jax.ShapeDtypeStruct((S, D), F32),
                   jax.ShapeDtypeStruct((S, D), F32)] + [jax.ShapeDtypeStruct((8, 128), F32)] * last,
        sem=("arbitrary",) if last else ("parallel",),
        args=(proj, proj, a_in, b_in, x, w_po, w_ho, w_out, gate, g) + ((target,) if last else ()), carry=carry)
    return outs[:5 + last], outs[5 + last:]


def _layer_head_bwd(dxn, y, proj, br_a, br_b, w_po, w_ho, w_out, gate, g, name, carry=None):
    S = y.shape[0]
    tr = _row_tile(S, 256)
    nsh, _, wsh = w_po.shape

    def body(dxn_ref, y_ref, mgp_ref, mgh_ref, bra_ref, brb_ref, wpo_ref, who_ref, wout_ref, gate_ref, g_ref,
             dy_ref, dba_ref, dbb_ref, dproj_ref, dain_ref, dbin_ref, dgate_ref, dg_ref, dmgh_s):
        i = pl.program_id(0)
        j = pl.program_id(1)

        @pl.when((i == 0) & (j == 0))
        def _():
            dgate_ref[...] = jnp.zeros((1, D), F32)
            dg_ref[...] = jnp.zeros((1, D), F32)

        @pl.when(j == 1)
        def _():
            dproj_ref[...] = dmgh_s[...]

        @pl.when(j == 0)
        def _():
            everything(dxn_ref, y_ref, mgp_ref, mgh_ref, bra_ref, brb_ref, wpo_ref, who_ref, wout_ref, gate_ref,
                       g_ref, dy_ref, dba_ref, dbb_ref, dproj_ref, dain_ref, dbin_ref, dgate_ref, dg_ref, dmgh_s)

    def everything(dxn_ref, y_ref, mgp_ref, mgh_ref, bra_ref, brb_ref, wpo_ref, who_ref, wout_ref, gate_ref, g_ref,
                   dy_ref, dba_ref, dbb_ref, dproj_ref, dain_ref, dbin_ref, dgate_ref, dg_ref, dmgh_s):
        yv = y_ref[...]
        dv = dxn_ref[...]
        gv = g_ref[...]
        gt = gate_ref[...]
        r = lax.rsqrt(jnp.mean(yv * yv, axis=-1, keepdims=True) + EPS)
        yh = yv * r
        dgate_ref[...] += jnp.sum(dv * (yh * gv), axis=0, keepdims=True)
        dg_ref[...] += jnp.sum(dv * gt * yh, axis=0, keepdims=True)
        u = dv * gt * gv
        dy = (r * u - yv * (r * r * r) * jnp.mean(u * yv, axis=-1, keepdims=True)).astype(BF16)
        dy_ref[...] = dy
        dm = _dot_nt(dy, wout_ref[...])
        sp = _sig(mgp_ref[...].astype(F32))
        sh = _sig(mgh_ref[...].astype(F32))
        dba = (dm * sp).astype(BF16)
        dbb = (dm * sh).astype(BF16)
        dba_ref[...] = dba
        dbb_ref[...] = dbb
        dproj_ref[...] = (dm * bra_ref[...].astype(F32) * sp * (1.0 - sp)).astype(BF16)
        dmgh_s[...] = (dm * brb_ref[...].astype(F32) * sh * (1.0 - sh)).astype(BF16)
        dain = _dot_nt(dba[:, 0:wsh], wpo_ref[0])
        for k in range(1, nsh):
            dain = dain + _dot_nt(dba[:, k * wsh:(k + 1) * wsh], wpo_ref[k])
        dain_ref[...] = dain
        dbin_ref[...] = _dot_nt(dbb, who_ref[...])

    tile = pl.BlockSpec((tr, D), lambda i, j: (i, 0))
    whole = lambda t: pl.BlockSpec(t.shape, lambda i, j: (0,) * t.ndim)
    vec = pl.BlockSpec((1, D), lambda i, j: (0, 0))
    ahead = lambda i, j: jnp.minimum(i + j, S // tr - 1)
    tile_in = pl.BlockSpec((tr, D), lambda i, j: (ahead(i, j), 0))
    outs = _call(
        body, name=name, grid=(S // tr, 2),
        in_specs=[tile_in, tile_in, pl.BlockSpec((tr, D), lambda i, j: (ahead(i, j), MGP_BLK)),
                  pl.BlockSpec((tr, D), lambda i, j: (ahead(i, j), MGH_BLK)), tile_in, tile_in, whole(w_po),
                  whole(w_ho), whole(w_out), vec, vec],
        out_specs=[tile, tile, tile, pl.BlockSpec((tr, D), lambda i, j: (i, MGP_BLK + j)),
                   pl.BlockSpec((tr, POOL_W), lambda i, j: (i, 0)), tile, vec, vec],
        out_shape=[jax.ShapeDtypeStruct((S, D), BF16)] * 3
        + [jax.ShapeDtypeStruct((S, IN_W), BF16), jax.ShapeDtypeStruct((S, POOL_W), F32),
           jax.ShapeDtypeStruct((S, D), F32), jax.ShapeDtypeStruct((1, D), F32), jax.ShapeDtypeStruct((1, D), F32)],
        scratch_shapes=[pltpu.VMEM((tr, D), BF16)], sem=("arbitrary", "arbitrary"),
        args=(dxn, y, proj, proj, br_a, br_b, w_po, w_ho, w_out, gate, g), carry=carry)
    return outs[:8], outs[8:]


def _pool_pieces(u, g, S):
    rowi = lax.broadcasted_iota(jnp.int32, (S, 1), 0)

    def down(z, k):
        return jnp.where(rowi >= k, pltpu.roll(z, k, axis=0), 0.0)

    s2 = u + down(u, 1)
    s4 = s2 + down(s2, 2)
    s8 = s4 + down(s4, 4)
    s16 = s8 + down(s8, 8)
    win = jnp.where(g == 0, s2, jnp.where(g == 1, s4, jnp.where(g == 2, s8, s16)))
    w = jnp.where(g == 0, 2, jnp.where(g == 1, 4, jnp.where(g == 2, 8, 16)))
    count = jnp.minimum(rowi + 1, w).astype(F32)
    return win / count - u, count, rowi


def _pool_fwd(proj, pw, pscale, name):
    S = proj.shape[0]

    def body(pv_ref, pg_ref, pw_ref, sc_ref, a_ref, at_ref):
        g = pl.program_id(0)
        pooled, _, _ = _pool_pieces(pv_ref[...].astype(F32), g, S)
        pm = jnp.dot(pooled.astype(BF16), pw_ref[...].astype(BF16), preferred_element_type=F32)
        pgv = pg_ref[...].astype(F32)
        av = pm * sc_ref[...] * (pgv * _sig(pgv))
        a_ref[...] = av.astype(BF16)
        at_ref[...] = av.T.astype(BF16)

    outs = _call(
        body, name=name, grid=(GROUPS,),
        in_specs=[pl.BlockSpec((S, 128), lambda g: (0, PV0 + g)), pl.BlockSpec((S, 128), lambda g: (0, PG0 + g)),
                  pl.BlockSpec((None, 128, 128), lambda g: (g, 0, 0)), pl.BlockSpec((1, 128), lambda g: (0, g))],
        out_specs=[pl.BlockSpec((S, 128), lambda g: (0, g)), pl.BlockSpec((128, S), lambda g: (g, 0))],
        out_shape=[jax.ShapeDtypeStruct((S, POOL_W), BF16), jax.ShapeDtypeStruct((POOL_W, S), BF16)],
        sem=("parallel",), args=(proj, proj, pw, pscale))
    return outs


def _pool_bwd(da, proj, pw, pscale, dproj, name):
    S = proj.shape[0]

    def body(da_ref, pv_ref, pg_ref, pw_ref, sc_ref, dproj_in, dproj_ref, dpw_ref, dsc_ref, dpg_s):
        @pl.when(pl.program_id(1) == 1)
        def _():
            dproj_ref[...] = dpg_s[...]

        @pl.when(pl.program_id(1) == 0)
        def _():
            group(da_ref, pv_ref, pg_ref, pw_ref, sc_ref, dproj_ref, dpg_s, dpw_ref, dsc_ref)

    def group(da_ref, pv_ref, pg_ref, pw_ref, sc_ref, dpv_ref, dpg_ref, dpw_ref, dsc_ref):
        g = pl.program_id(0)
        pooled, count, rowi = _pool_pieces(pv_ref[...].astype(F32), g, S)
        pwb = pw_ref[...].astype(BF16)
        pm = jnp.dot(pooled.astype(BF16), pwb, preferred_element_type=F32)
        scv = sc_ref[...]
        pgv = pg_ref[...].astype(F32)
        sg = _sig(pgv)
        dav = da_ref[...]
        d_ps = dav * (pgv * sg)
        dpg_ref[...] = (dav * (pm * scv) * _dsilu(pgv, sg)).astype(BF16)
        dsc_ref[...] = jnp.sum(d_ps * pm, axis=0, keepdims=True)
        d_pm = (d_ps * scv).astype(BF16)
        dpw_ref[...] = lax.dot_general(pooled.astype(BF16), d_pm, (((0,), (0,)), ((), ())),
                                       preferred_element_type=F32)
        d_pooled = lax.dot_general(d_pm, pwb, (((1,), (1,)), ((), ())), preferred_element_type=F32)
        z = d_pooled / count

        def up(v, k):
            return jnp.where(rowi < S - k, pltpu.roll(v, S - k, axis=0), 0.0)

        t2 = z + up(z, 1)
        t4 = t2 + up(t2, 2)
        t8 = t4 + up(t4, 4)
        t16 = t8 + up(t8, 8)
        adj = jnp.where(g == 0, t2, jnp.where(g == 1, t4, jnp.where(g == 2, t8, t16)))
        dpv_ref[...] = (adj - d_pooled).astype(BF16)

    col = lambda g, j: (0, g)
    ahead = lambda g, j: jnp.minimum(g + j, GROUPS - 1)
    return pl.pallas_call(
        body, name=name, grid=(GROUPS, 2),
        in_specs=[pl.BlockSpec((S, 128), lambda g, j: (0, ahead(g, j))),
                  pl.BlockSpec((S, 128), lambda g, j: (0, PV0 + ahead(g, j))),
                  pl.BlockSpec((S, 128), lambda g, j: (0, PG0 + ahead(g, j))),
                  pl.BlockSpec((None, 128, 128), lambda g, j: (ahead(g, j), 0, 0)),
                  pl.BlockSpec((1, 128), lambda g, j: (0, ahead(g, j))), ANY],
        out_specs=[pl.BlockSpec((S, 128), lambda g, j: (0, PV0 + g + (PG0 - PV0) * j)),
                   pl.BlockSpec((None, 128, 128), lambda g, j: (g, 0, 0)), pl.BlockSpec((1, 128), col)],
        out_shape=[jax.ShapeDtypeStruct(dproj.shape, dproj.dtype),
                   jax.ShapeDtypeStruct((GROUPS, 128, 128), F32), jax.ShapeDtypeStruct((1, POOL_W), F32)],
        scratch_shapes=[pltpu.VMEM((S, 128), BF16)], input_output_aliases={5: 0},
        compiler_params=_cp(("arbitrary", "arbitrary")),
    )(da, proj, proj, pw, pscale, dproj)


SCAN_SHIFTS = tuple(1 << b for b in range(CH.bit_length() - 1))


def _chunk_cumsum(z, rowi):
    for sh in SCAN_SHIFTS:
        z = z + jnp.where(rowi >= sh, pltpu.roll(z, sh, axis=0), 0.0)
    return z


def _chunk_rev_cumsum(z, rowi):
    for sh in SCAN_SHIFTS:
        z = z + jnp.where(rowi < CH - sh, pltpu.roll(z, CH - sh, axis=0), 0.0)
    return z


def _dot_nn(a, b):
    return jnp.dot(a.astype(BF16), b.astype(BF16), preferred_element_type=F32)


def _dot_nt(a, b):
    return lax.dot_general(a.astype(BF16), b.astype(BF16), (((1,), (1,)), ((), ())), preferred_element_type=F32)


def _dot_tn(a, b):
    return lax.dot_general(a.astype(BF16), b.astype(BF16), (((0,), (0,)), ((), ())), preferred_element_type=F32)


def _gates(hq, hf, lbv):
    hq, hf = hq.astype(F32), hf.astype(F32)
    sq = _sig(hq)
    sf = _sig(hf)
    f = lbv + (1.0 - lbv) * sf
    fc = jnp.maximum(f, 1e-30)
    return hq * sq, sq, sf, f, fc, jnp.log(fc)


DECAY_CAP = 60.0


def _block_ref(c_ref, i, sb):
    if i == 0:
        return jnp.zeros((1, HD), F32)
    return c_ref[sb * i - 1:sb * i, :]


def _block_decay(c_ref, sb):
    spans = [_block_ref(c_ref, i, sb) - c_ref[sb * (i + 1) - 1:sb * (i + 1), :] for i in range(CH // sb)]
    return functools.reduce(jnp.maximum, spans)


def _pair_factors(q_ref, k, c_ref, first, cap, round_bf16, sb):
    nb = CH // sb
    c = c_ref[...]
    zero = jnp.zeros((sb, HD), F32)
    q_groups, k_groups, eqs, eks = [], [], [], []
    for i in range(first, nb):
        blk = slice(sb * i, sb * (i + 1))
        r_i = _block_ref(c_ref, i, sb)
        eq = jnp.exp(jnp.minimum(c_ref[blk, :] - r_i, 0.0))
        ek = jnp.exp(jnp.minimum(r_i - c, cap))
        qi, kei = q_ref[blk, :] * eq, k * ek
        if round_bf16:
            qi, kei = qi.astype(BF16).astype(F32), kei.astype(BF16).astype(F32)
        q_groups.append(jnp.concatenate([zero] * i + [qi] + [zero] * (nb - 1 - i), axis=0))
        k_groups.append(kei)
        eqs.append(eq)
        eks.append(ek)
    return jnp.concatenate(q_groups, axis=1), jnp.concatenate(k_groups, axis=1), eqs, eks


def _pair_mask(rowi, coli, strict, sb):
    return (coli < jnp.bitwise_and(rowi, -sb)) if strict else (coli <= rowi)


def _hgrn_fwd(proj, lb, gn, name, carry=None):
    S = proj.shape[0]
    nch = S // CH
    W = NH * HD

    def body(hq_ref, hf_ref, hi_ref, hg_ref, lb_ref, gn_ref, bin_ref, bint_ref, oraw_ref, st_ref, mild_ref,
             cum_ref, q_s, k_s, c_s, v_s, o_s, state_s, qf_s, kf_s, cf_s):
        state_s[...] = jnp.zeros((NH, HD, HD), F32)
        rowi = lax.broadcasted_iota(jnp.int32, (CH, 1), 0)
        coli = lax.broadcasted_iota(jnp.int32, (1, CH), 1)
        sbi = lax.broadcasted_iota(jnp.int32, (SB, 1), 0)
        gnv = gn_ref[...]

        def gates_pass(n, worst):
            wide, narrow = worst
            rows = pl.ds(pl.multiple_of(n * CH, CH), CH)
            for hh in range(NH):
                lanes = slice(hh * HD, (hh + 1) * HD)
                q, _, _, f, _, logf = _gates(hq_ref[rows, lanes], hf_ref[rows, lanes], lb_ref[:, lanes])
                c = _chunk_cumsum(logf, rowi)
                qf_s[hh, rows, :] = q
                kf_s[hh, rows, :] = 1.0 - f
                cf_s[hh, rows, :] = c
                cum_ref[rows, lanes] = c
                c_s[hh] = c
                wide = jnp.maximum(wide, _block_decay(c_s.at[hh], SB_WIDE))
                narrow = jnp.maximum(narrow, _block_decay(c_s.at[hh], SB))
            return wide, narrow

        def between_chunks(hh, n, rows):
            lanes = slice(hh * HD, (hh + 1) * HD)
            q = qf_s[hh, rows, :]
            k = kf_s[hh, rows, :]
            c = cf_s[hh, rows, :]
            v = hi_ref[rows, lanes].astype(F32)
            q_s[hh] = q
            k_s[hh] = k
            c_s[hh] = c
            v_s[hh] = v
            st = state_s[hh]
            st_ref[hh, n] = st.astype(BF16)
            o_s[hh] = _dot_nt(q * jnp.exp(c), st)
            last = c_s[hh, CH - 1:CH, :]
            state_s[hh] = st * jnp.exp(last) + _dot_tn(v, k * jnp.exp(last - c))

        def pairs_matmul(hh, first, cap, strict, sb):
            qx, kc, _, _ = _pair_factors(q_s.at[hh], k_s[hh], c_s.at[hh], first, cap, False, sb)
            a = jnp.where(_pair_mask(rowi, coli, strict, sb), _dot_nt(qx, kc), 0.0)
            o_s[hh] += _dot_nn(a, v_s[hh])

        def within_chunk_matmul(sb):
            return lambda hh: pairs_matmul(hh, 0, DECAY_CAP, False, sb)

        def within_chunk_exact(hh):
            pairs_matmul(hh, 1, 0.0, True, SB)
            for i in range(CH // SB):
                blk = slice(SB * i, SB * (i + 1))
                qb = q_s[hh, blk, :]
                cb = c_s[hh, blk, :]
                acc = jnp.zeros((SB, HD), F32)
                for s in range(SB):
                    row = SB * i + s
                    w = jnp.exp(jnp.minimum(cb - c_s[hh, row:row + 1, :], 0.0))
                    a_col = jnp.sum(qb * k_s[hh, row:row + 1, :] * w, axis=-1, keepdims=True)
                    acc = acc + jnp.where(sbi >= s, a_col, 0.0) * v_s[hh, row:row + 1, :]
                o_s[hh, blk, :] += acc

        def norm_and_gate(hh, rows):
            lanes = slice(hh * HD, (hh + 1) * HD)
            ov = o_s[hh]
            oraw_ref[rows, lanes] = ov
            r = lax.rsqrt(jnp.mean(ov * ov, axis=-1, keepdims=True) + EPS)
            hg = hg_ref[rows, lanes].astype(F32)
            bin_ref[rows, lanes] = ((ov * r) * gnv * (hg * _sig(hg))).astype(BF16)

        def chunk_with(within_chunk):
            def chunk(n, carry):
                rows = pl.ds(pl.multiple_of(n * CH, CH), CH)
                for hh in range(NH):
                    between_chunks(hh, n, rows)
                for hh in range(NH):
                    within_chunk(hh)
                for hh in range(NH):
                    norm_and_gate(hh, rows)
                return carry
            return chunk

        none = jnp.zeros((1, HD), F32)
        wide, narrow = lax.fori_loop(0, nch, gates_pass, (none, none))
        tier = jnp.where(jnp.max(wide) <= DECAY_CAP, 2.0, jnp.where(jnp.max(narrow) <= DECAY_CAP, 1.0, 0.0))
        mild_ref[...] = jnp.broadcast_to(tier, (8, HD))

        @pl.when(tier == 2.0)
        def _():
            lax.fori_loop(0, nch, chunk_with(within_chunk_matmul(SB_WIDE)), 0, unroll=4)

        @pl.when(tier == 1.0)
        def _():
            lax.fori_loop(0, nch, chunk_with(within_chunk_matmul(SB)), 0, unroll=2)

        @pl.when(tier == 0.0)
        def _():
            lax.fori_loop(0, nch, chunk_with(within_chunk_exact), 0)

        bint_ref[...] = bin_ref[...].astype(F32).T.astype(BF16)

    col = lambda off: pl.BlockSpec((S, W), lambda h: (0, off // NH + h))
    head = pl.BlockSpec((S, W), lambda h: (0, h))
    outs = _call(
        body, name=name, grid=(HEADS // NH,),
        in_specs=[col(HQ0), col(HF0), col(HI0), col(HG0), pl.BlockSpec((1, W), lambda h: (0, h)),
                  pl.BlockSpec((1, HD), lambda h: (0, 0))],
        out_specs=[head, pl.BlockSpec((W, S), lambda h: (h, 0)), head,
                   pl.BlockSpec((NH, nch, HD, HD), lambda h: (h, 0, 0, 0)),
                   pl.BlockSpec((8, HD), lambda h: (h, 0)), head],
        out_shape=[jax.ShapeDtypeStruct((S, D), BF16), jax.ShapeDtypeStruct((D, S), BF16),
                   jax.ShapeDtypeStruct((S, D), F32), jax.ShapeDtypeStruct((HEADS, nch, HD, HD), BF16),
                   jax.ShapeDtypeStruct((8 * HEADS // NH, HD), F32), jax.ShapeDtypeStruct((S, D), F32)],
        scratch_shapes=[pltpu.VMEM((NH, CH, HD), F32)] * 5 + [pltpu.VMEM((NH, HD, HD), F32)]
        + [pltpu.VMEM((NH, S, HD), F32)] * 3,
        sem=("parallel",), args=(proj, proj, proj, proj, lb, gn), carry=carry)
    return outs[:6], outs[6:]


def _hgrn_bwd(dbin, proj, oraw, states, mild, cum, lb, gn, dproj, name, carry=None):
    S = proj.shape[0]
    nch = S // CH
    W = NH * HD
    n_in = 12

    def body(*refs):
        ins, (dproj_ref, dlb_ref, dgn_ref) = refs[:n_in - 1], refs[n_in:n_in + 3]
        scratch, later = refs[n_in + 3:-3], refs[-3:]
        seg = pl.program_id(1)

        @pl.when(seg == 0)
        def _():
            heads(*ins, dproj_ref, *later, dlb_ref, dgn_ref, *scratch)

        for s, kept in enumerate(later):
            @pl.when(seg == s + 1)
            def _(kept=kept):
                dproj_ref[...] = kept[...]

    def heads(db_ref, hq_ref, hf_ref, hi_ref, hg_ref, or_ref, st_ref, mild_ref, cum_ref, lb_ref, gn_ref,
              dq_ref, df_ref, di_ref, dg_ref, dlb_ref, dgn_ref,
              q_s, k_s, c_s, v_s, do_s, dq_s, dk_s, dv_s, dc_s, dqd_s, dkd_s, f_s, sf_s, sq_s, dl_s, dst_s,
              dlb_s, dgn_s):
        dst_s[...] = jnp.zeros((NH, HD, HD), F32)
        dlb_s[...] = jnp.zeros((1, W), F32)
        dgn_s[...] = jnp.zeros((1, HD), F32)
        rowi = lax.broadcasted_iota(jnp.int32, (CH, 1), 0)
        coli = lax.broadcasted_iota(jnp.int32, (1, CH), 1)
        sbi = lax.broadcasted_iota(jnp.int32, (SB, 1), 0)
        gnv = gn_ref[...]
        def between_chunks(hh, n, rows):
            lanes = slice(hh * HD, (hh + 1) * HD)
            lbv = lb_ref[:, lanes]
            hq = hq_ref[rows, lanes].astype(F32)
            sq = _sig(hq)
            sf = _sig(hf_ref[rows, lanes].astype(F32))
            f = lbv + (1.0 - lbv) * sf
            q = hq * sq
            k = 1.0 - f
            f_s[hh] = f
            sf_s[hh] = sf
            sq_s[hh] = sq
            v = hi_ref[rows, lanes].astype(F32)
            c = cum_ref[rows, lanes]
            ov = or_ref[rows, lanes]
            hg = hg_ref[rows, lanes].astype(F32)
            sg = _sig(hg)
            r = lax.rsqrt(jnp.mean(ov * ov, axis=-1, keepdims=True) + EPS)
            dbv = db_ref[rows, lanes]
            d_on = dbv * (hg * sg)
            dg_ref[rows, lanes] = (dbv * ((ov * r) * gnv) * _dsilu(hg, sg)).astype(BF16)
            dgn_s[...] += jnp.sum(d_on * (ov * r), axis=0, keepdims=True)
            u = d_on * gnv
            do = r * u - ov * (r * r * r) * jnp.mean(u * ov, axis=-1, keepdims=True)
            q_s[hh] = q
            k_s[hh] = k
            c_s[hh] = c
            v_s[hh] = v
            do_s[hh] = do
            st = st_ref[hh, n].astype(F32)
            dst = dst_s[hh]
            ec = jnp.exp(c)
            last = c_s[hh, CH - 1:CH, :]
            el = jnp.exp(last - c)
            elast = jnp.exp(last)
            dq = _dot_nn(do, st) * ec
            dk = _dot_nn(v, dst) * el
            dq_s[hh] = dq
            dk_s[hh] = dk
            dv_s[hh] = _dot_nt(k * el, dst)
            dc_s[hh] = q * dq - k * dk
            dl_s[hh] = (jnp.sum(k * dk, axis=0, keepdims=True)
                        + elast * jnp.sum(st * dst, axis=0, keepdims=True))
            dst_s[hh] = dst * elast + _dot_tn(do, q * ec)

        def pairs_matmul(hh, first, cap, strict, sb):
            do = do_s[hh]
            qx, kc, eqs, eks = _pair_factors(q_s.at[hh], k_s[hh], c_s.at[hh], first, cap, True, sb)
            mask = _pair_mask(rowi, coli, strict, sb)
            a = jnp.where(mask, _dot_nt(qx, kc), 0.0)
            d_a = jnp.where(mask, _dot_nt(do, v_s[hh]).astype(BF16).astype(F32), 0.0)
            dqx = _dot_nn(d_a, kc)
            dkc = _dot_tn(d_a, qx)
            dv_s[hh] += _dot_tn(a, do)
            dk, dcum = dk_s[hh], dc_s[hh]
            dq_slabs = [jnp.zeros((sb, HD), F32)] * first
            dc_slabs = [jnp.zeros((sb, HD), F32)] * first
            for g, (eq, ek) in enumerate(zip(eqs, eks)):
                rows = slice(sb * (first + g), sb * (first + g + 1))
                cols = slice(HD * g, HD * (g + 1))
                dq_i = dqx[rows, cols]
                dk_i = dkc[:, cols]
                dq_slabs.append(dq_i * eq)
                dc_slabs.append(qx[rows, cols] * dq_i)
                dk = dk + dk_i * ek
                dcum = dcum - kc[:, cols] * dk_i
            dq_s[hh] += jnp.concatenate(dq_slabs, axis=0)
            dk_s[hh] = dk
            dc_s[hh] = dcum + jnp.concatenate(dc_slabs, axis=0)

        def pairs_exact(hh):
            dqd_s[hh] = jnp.zeros((CH, HD), F32)
            dkd_s[hh] = jnp.zeros((CH, HD), F32)
            for i in range(CH // SB):
                blk = slice(SB * i, SB * (i + 1))
                qb = q_s[hh, blk, :]
                cb = c_s[hh, blk, :]
                dob = do_s[hh, blk, :]
                dq_acc = jnp.zeros((SB, HD), F32)
                for s in range(SB):
                    row = SB * i + s
                    ks = k_s[hh, row:row + 1, :]
                    vs = v_s[hh, row:row + 1, :]
                    w = jnp.exp(jnp.minimum(cb - c_s[hh, row:row + 1, :], 0.0))
                    live = sbi >= s
                    a_col = jnp.where(live, jnp.sum(qb * ks * w, axis=-1, keepdims=True), 0.0)
                    da_col = jnp.where(live, jnp.sum(dob * vs, axis=-1, keepdims=True), 0.0)
                    dq_acc = dq_acc + da_col * ks * w
                    dkd_s[hh, row:row + 1, :] += jnp.sum(da_col * qb * w, axis=0, keepdims=True)
                    dv_s[hh, row:row + 1, :] += jnp.sum(a_col * dob, axis=0, keepdims=True)
                dqd_s[hh, blk, :] += dq_acc
            dq_d = dqd_s[hh]
            dk_d = dkd_s[hh]
            dq_s[hh] += dq_d
            dk_s[hh] += dk_d
            dc_s[hh] += q_s[hh] * dq_d - k_s[hh] * dk_d

        def gate_grads(hh, rows):
            lanes = slice(hh * HD, (hh + 1) * HD)
            lbv = lb_ref[:, lanes]
            hq = hq_ref[rows, lanes].astype(F32)
            f, sf, sq = f_s[hh], sf_s[hh], sq_s[hh]
            dlogf = _chunk_rev_cumsum(dc_s[hh], rowi) + dl_s[hh]
            dfv = jnp.where(f > 1e-30, dlogf / jnp.maximum(f, 1e-30), 0.0) - dk_s[hh]
            dlb_s[:, lanes] += jnp.sum(dfv * (1.0 - sf), axis=0, keepdims=True)
            df_ref[rows, lanes] = (dfv * (1.0 - lbv) * sf * (1.0 - sf)).astype(BF16)
            dq_ref[rows, lanes] = (dq_s[hh] * _dsilu(hq, sq)).astype(BF16)
            di_ref[rows, lanes] = dv_s[hh].astype(BF16)

        def chunk_with(pairs):
            def chunk(j, carry):
                n = nch - 1 - j
                rows = pl.ds(pl.multiple_of(n * CH, CH), CH)
                for hh in range(NH):
                    between_chunks(hh, n, rows)
                for hh in range(NH):
                    pairs(hh)
                for hh in range(NH):
                    gate_grads(hh, rows)
                return carry
            return chunk

        def pairs_mild(sb):
            return lambda hh: pairs_matmul(hh, 0, DECAY_CAP, False, sb)

        def pairs_any(hh):
            pairs_matmul(hh, 1, 0.0, True, SB)
            pairs_exact(hh)

        tier = jnp.max(mild_ref[...])

        @pl.when(tier == 2.0)
        def _():
            lax.fori_loop(0, nch, chunk_with(pairs_mild(SB_WIDE)), 0, unroll=2)

        @pl.when(tier == 1.0)
        def _():
            lax.fori_loop(0, nch, chunk_with(pairs_mild(SB)), 0)

        @pl.when(tier == 0.0)
        def _():
            lax.fori_loop(0, nch, chunk_with(pairs_any), 0)

        dlb_ref[...] = dlb_s[...]
        dgn_ref[...] = jnp.broadcast_to(dgn_s[...], (8, HD))

    ahead = lambda h, s: jnp.minimum(h + jnp.minimum(s, 1), HEADS // NH - 1)
    col = lambda off: pl.BlockSpec((S, W), lambda h, s: (0, off // NH + ahead(h, s)))
    head_in = pl.BlockSpec((S, W), lambda h, s: (0, ahead(h, s)))
    vec_in = pl.BlockSpec((1, W), lambda h, s: (0, ahead(h, s)))
    vec = pl.BlockSpec((1, W), lambda h, s: (0, h))
    seg_w = (HF0 - HQ0) // NH
    outs = _call(
        body, name=name, grid=(HEADS // NH, 4),
        in_specs=[head_in, col(HQ0), col(HF0), col(HI0), col(HG0), head_in,
                  pl.BlockSpec((NH, nch, HD, HD), lambda h, s: (ahead(h, s), 0, 0, 0)),
                  pl.BlockSpec((8, HD), lambda h, s: (ahead(h, s), 0)), head_in, vec_in,
                  pl.BlockSpec((1, HD), lambda h, s: (0, 0)), ANY],
        out_specs=[pl.BlockSpec((S, W), lambda h, s: (0, HQ0 // NH + seg_w * s + h)), vec,
                   pl.BlockSpec((8, HD), lambda h, s: (h, 0))],
        out_shape=[jax.ShapeDtypeStruct(dproj.shape, dproj.dtype), jax.ShapeDtypeStruct((1, D), F32),
                   jax.ShapeDtypeStruct((8 * HEADS // NH, HD), F32)],
        scratch_shapes=[pltpu.VMEM((NH, CH, HD), F32)] * 14
        + [pltpu.VMEM((NH, 1, HD), F32), pltpu.VMEM((NH, HD, HD), F32), pltpu.VMEM((1, W), F32),
           pltpu.VMEM((1, HD), F32)] + [pltpu.VMEM((S, W), BF16)] * 3,
        sem=("arbitrary", "arbitrary"), aliases={n_in - 1: 0},
        args=(dbin, proj, proj, proj, proj, oraw, states, mild, cum, lb, gn, dproj), carry=carry)
    dproj, dlb, dgn = outs[:3]
    return (dproj, dlb, dgn.reshape(HEADS // NH, 8, HD)[:, 0, :]), outs[3:]


def _lower_bounds(l0, l1):
    m = jnp.maximum(l0, l1)
    e0 = jnp.exp(l0 - m)
    e1 = jnp.exp(l1 - m)
    tot = e0 + e1
    p0 = e0 / tot
    p1 = e1 / tot
    return jnp.clip(p0 - p0, 0.0, 1.0), jnp.clip((p0 + p1) - p0, 0.0, 1.0)


def _lb_fwd(logits):
    def body(l_ref, o_ref):
        lb0, lb1 = _lower_bounds(l_ref[0:1, :], l_ref[1:2, :])
        o_ref[0:1, :] = lb0
        o_ref[1:2, :] = lb1

    return pl.pallas_call(body, name="lb_fwd", out_shape=jax.ShapeDtypeStruct((2, D), F32))(logits)


def _lb_bwd(logits, dlb):
    def body(l_ref, d_ref, o_ref):
        _, vjp = jax.vjp(_lower_bounds, l_ref[0:1, :], l_ref[1:2, :])
        g0, g1 = vjp((d_ref[0:1, :], d_ref[1:2, :]))
        o_ref[0:1, :] = g0
        o_ref[1:2, :] = g1

    return pl.pallas_call(body, name="lb_bwd", out_shape=jax.ShapeDtypeStruct((2, D), F32))(logits, dlb)


ADA_PAD = 128


def _ada_fwd(c_pad, w_ada, b_sh):
    ns = w_ada.shape[2]

    def body(c_ref, w_ref, b_ref, o_ref):
        cv = c_ref[...]
        ca = (cv * _sig(cv)).astype(BF16)
        for l in range(2):
            res = jnp.dot(ca, w_ref[l].astype(BF16), preferred_element_type=F32)
            o_ref[:, l * ns:(l + 1) * ns] = res[0:NDEV, :] + b_ref[l:l + 1, :]

    return pl.pallas_call(body, name="ada_fwd", out_shape=jax.ShapeDtypeStruct((NDEV, 2 * ns), F32),
                          compiler_params=_cp())(c_pad, w_ada, b_sh)


def _ada_wgrad(c_pad_t, d_ada_sh):
    ns = d_ada_sh.shape[2]

    def body(c_ref, d_ref, o_ref):
        cv = c_ref[...]
        ca = (cv * _sig(cv)).astype(BF16)
        for l in range(2):
            o_ref[l] = jnp.dot(ca, d_ref[l].astype(BF16), preferred_element_type=F32)

    return pl.pallas_call(body, name="ada_wgrad", out_shape=jax.ShapeDtypeStruct((2, D, ns), F32),
                          compiler_params=_cp())(c_pad_t, d_ada_sh)


def _sum_devices(g):
    _, R, C = g.shape

    def body(g_ref, o_ref):
        acc = g_ref[0]
        for d in range(1, NDEV):
            acc = acc + g_ref[d]
        o_ref[...] = acc

    return pl.pallas_call(body, name="sum_devices", out_shape=jax.ShapeDtypeStruct((R, C), F32),
                          compiler_params=_cp())(g)


def _adamw(w, g, m, v, name, echo=False):
    R, C = w.shape
    tr = _row_tile(R, max(8, (1 << 19) // C))

    def body(w_ref, g_ref, m_ref, v_ref, d_ref, nm_ref, nv_ref, *g_out):
        d_ref[...], nm_ref[...], nv_ref[...] = _adamw_update(w_ref[...], g_ref[...], m_ref[...], v_ref[...])
        for o_ref in g_out:
            o_ref[...] = g_ref[...]

    tile = pl.BlockSpec((tr, C), lambda i: (i, 0))
    n_out = 4 if echo else 3
    return _call(body, name=name, grid=(R // tr,), in_specs=[tile] * 4, out_specs=[tile] * n_out,
                 out_shape=[jax.ShapeDtypeStruct((R, C), F32)] * n_out, sem=("parallel",), args=(w, g, m, v))


def _adamw_many(wgmv, name, steps=4):
    n = len(wgmv)

    def body(*refs):
        ins, outs = refs[:4 * n], refs[4 * n:]
        for a in range(n):
            w_ref, g_ref, m_ref, v_ref = ins[4 * a:4 * a + 4]
            d_ref, nm_ref, nv_ref, g_out = outs[4 * a:4 * a + 4]
            d_ref[...], nm_ref[...], nv_ref[...] = _adamw_update(w_ref[...], g_ref[...], m_ref[...], v_ref[...])
            g_out[...] = g_ref[...]

    def tile(t):
        return pl.BlockSpec((t.shape[0] // steps, t.shape[1]), lambda i: (i, 0))

    flat = [t for four in wgmv for t in four]
    outs = pl.pallas_call(
        body, name=name, grid=(steps,), in_specs=[tile(t) for t in flat],
        out_specs=[tile(four[0]) for four in wgmv for _ in range(4)],
        out_shape=[jax.ShapeDtypeStruct(four[0].shape, F32) for four in wgmv for _ in range(4)],
        compiler_params=_cp(("parallel",)))(*flat)
    return [outs[4 * a:4 * a + 4] for a in range(n)]


def _adamw_update(w, g, m, v):
    nm = B1 * m + (1.0 - B1) * g
    nv = B2 * v + (1.0 - B2) * (g * g)
    m_hat = nm / (1.0 - B1 ** STEP)
    v_hat = nv / (1.0 - B2 ** STEP)
    return -LR * (m_hat / (jnp.sqrt(v_hat) + AEPS) + WD * w), nm, nv


SMALL_PARTS = (("b_ada", 0, 6, D), ("g_pre", 8, 2, D), ("g_post", 16, 2, D), ("lb_logits", 24, 2, D),
               ("pool_w", 32, 128, D), ("pool_scale", 160, 1, D), ("hgrn_norm_g", 168, 1, 2 * HD))


def _adamw_small(g_small, g_lb_logits, wmv):
    n = len(SMALL_PARTS)

    def body(g_ref, glb_ref, *refs):
        ins, outs = refs[:3 * n], refs[3 * n:]
        for p, (key, row0, rows, width) in enumerate(SMALL_PARTS):
            gv = glb_ref[...] if key == "lb_logits" else g_ref[row0:row0 + rows, 0:width]
            res = _adamw_update(ins[3 * p][...], gv, ins[3 * p + 1][...], ins[3 * p + 2][...])
            for t in range(3):
                outs[3 * p + t][...] = res[t]

    flat = [t for triple in wmv for t in triple]
    outs = pl.pallas_call(body, name="adamw_small",
                          out_shape=[jax.ShapeDtypeStruct(t.shape, F32) for t in flat],
                          compiler_params=_cp())(g_small, g_lb_logits, *flat)
    return [outs[3 * p:3 * p + 3] for p in range(n)]


def _cast_to_slot(place, w, l, name):
    _, R, C = w.shape
    tr = _row_tile(R, max(8, (1 << 19) // C))

    def body(p_ref, w_ref, o_ref):
        o_ref[...] = w_ref[...].astype(BF16)

    return pl.pallas_call(
        body, name=name, out_shape=jax.ShapeDtypeStruct((NCHIP, R, C), BF16),
        grid_spec=pltpu.PrefetchScalarGridSpec(
            num_scalar_prefetch=1, grid=(R // tr,),
            in_specs=[pl.BlockSpec((None, tr, C), lambda i, p_ref: (l, i, 0))],
            out_specs=pl.BlockSpec((None, tr, C), lambda i, p_ref: (p_ref[0], i, 0))),
        compiler_params=_cp(("parallel",)),
    )(place, w)


def _cast_to_slots(place, ws, name):
    n = len(ws)

    def body(p_ref, *refs):
        for w_ref, o_ref in zip(refs[:n], refs[n:]):
            o_ref[...] = w_ref[...].astype(BF16)

    def layer(l):
        return lambda i, p_ref: (l, 0, 0)

    return pl.pallas_call(
        body, name=name, out_shape=[jax.ShapeDtypeStruct((NCHIP,) + w.shape[1:], BF16) for w, _ in ws],
        grid_spec=pltpu.PrefetchScalarGridSpec(
            num_scalar_prefetch=1, grid=(1,),
            in_specs=[pl.BlockSpec((None,) + w.shape[1:], layer(l)) for w, l in ws],
            out_specs=[pl.BlockSpec((None,) + w.shape[1:], lambda i, p_ref: (p_ref[0], 0, 0)) for w, _ in ws]),
        compiler_params=_cp(("arbitrary",)),
    )(place, *[w for w, _ in ws])


def _pair_adds(core, gs, gots, name):
    n = len(gs)

    def body(c_ref, *refs):
        for a_ref, b_ref, o_ref in zip(refs[:n], refs[n:2 * n], refs[2 * n:]):
            o_ref[...] = (a_ref[...].astype(F32) + b_ref[...].astype(F32)).astype(o_ref.dtype)

    def whole(t):
        return pl.BlockSpec(t.shape, lambda i, c_ref: (0, 0, 0))

    return pl.pallas_call(
        body, name=name, out_shape=[jax.ShapeDtypeStruct(t.shape, BF16) for t in gots],
        grid_spec=pltpu.PrefetchScalarGridSpec(
            num_scalar_prefetch=1, grid=(1,),
            in_specs=[pl.BlockSpec(t.shape, lambda i, c_ref: (0, c_ref[0], 0)) for t in gots]
            + [whole(t) for t in gots],
            out_specs=[whole(t) for t in gots]),
        compiler_params=_cp(("arbitrary",)),
    )(core, *gs, *gots)


def _pair_add(core, g, got, name):
    _, R, C = g.shape
    r2 = R // 2
    tr = _row_tile(r2, max(8, (1 << 19) // C))
    nt = r2 // tr

    def body(c_ref, a_ref, b_ref, o_ref):
        o_ref[...] = (a_ref[...].astype(F32) + b_ref[...].astype(F32)).astype(o_ref.dtype)

    return pl.pallas_call(
        body, name=name, out_shape=jax.ShapeDtypeStruct((NCHIP, r2, C), BF16),
        grid_spec=pltpu.PrefetchScalarGridSpec(
            num_scalar_prefetch=1, grid=(NCHIP, nt),
            in_specs=[pl.BlockSpec((None, tr, C), lambda j, i, c_ref: (j, c_ref[0] * nt + i, 0)),
                      pl.BlockSpec((None, tr, C), lambda j, i, c_ref: (j, i, 0))],
            out_specs=pl.BlockSpec((None, tr, C), lambda j, i, c_ref: (j, i, 0))),
        compiler_params=_cp(("parallel", "parallel")),
    )(core, g, got)


def _sum_in_chip_order(me, own_ref, r_ref):
    own = own_ref[...].astype(F32)
    acc = None
    for j in range(NCHIP):
        slot = jnp.minimum(jnp.where(j > me, j - 1, j), NCHIP - 2)
        term = jnp.where(me == j, own, r_ref[slot].astype(F32))
        acc = term if acc is None else acc + term
    return acc


def _chip_sums(place, parts, recvs, name):
    n = len(parts)

    def body(p_ref, *refs):
        for a in range(n):
            for l in range(2):
                refs[4 * n + a][l] = _sum_in_chip_order(p_ref[0], refs[2 * a + l], refs[2 * n + 2 * a + l])

    def own(t):
        return pl.BlockSpec((None,) + t.shape[1:], lambda i, p_ref: (p_ref[0], 0, 0))

    def whole(t):
        return pl.BlockSpec(t.shape, lambda i, p_ref: (0, 0, 0))

    flat_p = [t for pair in parts for t in pair]
    flat_r = [t for pair in recvs for t in pair]
    return pl.pallas_call(
        body, name=name,
        out_shape=[jax.ShapeDtypeStruct((2, 2 * p[0].shape[1], p[0].shape[2]), F32) for p in parts],
        grid_spec=pltpu.PrefetchScalarGridSpec(
            num_scalar_prefetch=1, grid=(1,),
            in_specs=[own(t) for t in flat_p] + [whole(t) for t in flat_r],
            out_specs=[pl.BlockSpec((2,) + p[0].shape[1:], lambda i, p_ref: (0, p_ref[1], 0)) for p in parts]),
        compiler_params=_cp(("arbitrary",)),
    )(place, *flat_p, *flat_r)


def _chip_sum(place, part, recv, layer, both, name):
    _, r2, C = part.shape
    tr = _row_tile(r2, max(8, (1 << 18) // C))
    nt = r2 // tr

    def body(p_ref, own_ref, r_ref, *rest):
        rest[-1][...] = _sum_in_chip_order(p_ref[0], own_ref, r_ref)

    args = (place, part, recv) if both is None else (place, part, recv, both)
    return pl.pallas_call(
        body, name=name, out_shape=jax.ShapeDtypeStruct((2, 2 * r2, C), F32),
        grid_spec=pltpu.PrefetchScalarGridSpec(
            num_scalar_prefetch=1, grid=(nt,),
            in_specs=[pl.BlockSpec((None, tr, C), lambda i, p_ref: (p_ref[0], i, 0)),
                      pl.BlockSpec((NCHIP - 1, tr, C), lambda i, p_ref: (0, i, 0))] + [ANY] * (len(args) - 3),
            out_specs=pl.BlockSpec((None, tr, C), lambda i, p_ref: (layer, p_ref[1] * nt + i, 0))),
        input_output_aliases={} if both is None else {3: 0},
        compiler_params=_cp(("parallel",)),
    )(*args)


def _place():
    x, y, c = lax.axis_index("x"), lax.axis_index("y"), lax.axis_index("c")
    chips = [(1 - x, y), (x, 1 - y), (1 - x, 1 - y)]
    return x, y, c, chips


def _gather_small(blk, name):
    m_per, n = blk.shape

    def body(x_ref, out_ref, send_sems, recv_sems, local_sem):
        x, y, c, chips = _place()
        me, sibling = (x, y, c), (x, y, 1 - c)

        def rows(px, py, pc):
            return out_ref.at[pl.ds((4 * px + 2 * py + pc) * m_per, m_per), :]

        def copy(k, block, to, src=None):
            return pltpu.make_async_remote_copy(
                src_ref=rows(*block) if src is None else src, dst_ref=rows(*block),
                send_sem=send_sems.at[k], recv_sem=recv_sems.at[k], device_id=to, device_id_type=MESH)

        mine = pltpu.make_async_copy(x_ref, rows(*me), local_sem)
        mine.start()
        first = [copy(0, me, sibling, src=x_ref)]
        first += [copy(1 + j, me, (*chip, c), src=x_ref) for j, chip in enumerate(chips)]
        for cp in first:
            cp.start()
        passed = [copy(4 + j, (*chip, c), sibling) for j, chip in enumerate(chips)]
        for j, chip in enumerate(chips):
            copy(1 + j, (*chip, c), me).wait_recv()
            passed[j].start()
        copy(0, sibling, me).wait_recv()
        for j, chip in enumerate(chips):
            copy(4 + j, (*chip, 1 - c), me).wait_recv()
        for cp in first + passed:
            cp.wait_send()
        mine.wait()

    return pl.pallas_call(
        body, name=name, out_shape=jax.ShapeDtypeStruct((NDEV * m_per, n), blk.dtype),
        in_specs=[pl.BlockSpec(memory_space=pltpu.VMEM)], out_specs=pl.BlockSpec(memory_space=pltpu.VMEM),
        scratch_shapes=[pltpu.SemaphoreType.DMA((7,)), pltpu.SemaphoreType.DMA((7,)), pltpu.SemaphoreType.DMA],
        compiler_params=_cp(),
    )(blk)


def _gather_rows_carry(blk):
    m_per, n = blk.shape

    def rows(ref, px, py, pc):
        return ref.at[pl.ds((4 * px + 2 * py + pc) * m_per, m_per), :]

    def copy(ins, outs, send_sems, recv_sems, k, block, to, own=False):
        return pltpu.make_async_remote_copy(
            src_ref=ins[0] if own else rows(outs[0], *block), dst_ref=rows(outs[0], *block),
            send_sem=send_sems.at[k], recv_sem=recv_sems.at[k], device_id=to, device_id_type=MESH)

    def mine(ins, outs, send_sems):
        x, y, c, _ = _place()
        return pltpu.make_async_copy(ins[0], rows(outs[0], x, y, c), send_sems.at[7])

    def start(ins, outs, send_sems, recv_sems):
        x, y, c, chips = _place()
        mine(ins, outs, send_sems).start()
        copy(ins, outs, send_sems, recv_sems, 0, (x, y, c), (x, y, 1 - c), own=True).start()
        for j, chip in enumerate(chips):
            copy(ins, outs, send_sems, recv_sems, 1 + j, (x, y, c), (*chip, c), own=True).start()

    def finish(ins, outs, send_sems, recv_sems):
        x, y, c, chips = _place()
        for j, chip in enumerate(chips):
            copy(ins, outs, send_sems, recv_sems, 1 + j, (*chip, c), (x, y, c)).wait_recv()
            copy(ins, outs, send_sems, recv_sems, 4 + j, (*chip, c), (x, y, 1 - c)).start()
        copy(ins, outs, send_sems, recv_sems, 0, (x, y, 1 - c), (x, y, c)).wait_recv()
        for j, chip in enumerate(chips):
            copy(ins, outs, send_sems, recv_sems, 4 + j, (*chip, 1 - c), (x, y, c)).wait_recv()
        copy(ins, outs, send_sems, recv_sems, 0, (x, y, c), (x, y, 1 - c), own=True).wait_send()
        for j, chip in enumerate(chips):
            copy(ins, outs, send_sems, recv_sems, 1 + j, (x, y, c), (*chip, c), own=True).wait_send()
            copy(ins, outs, send_sems, recv_sems, 4 + j, (*chip, c), (x, y, 1 - c)).wait_send()
        mine(ins, outs, send_sems).wait()

    return _Carry([blk], [jax.ShapeDtypeStruct((NDEV * m_per, n), blk.dtype)], {}, 8, start, finish)


def _gather_carry(shards, piece=(0, 1, 1)):
    n = len(shards)
    first, count, of = piece

    def rows(ref, half):
        r2 = ref.shape[1] // 2
        return pl.ds(half * r2 + first * (r2 // of), count * (r2 // of))

    def over_ici(outs, send_sems, recv_sems, a, j, chip_xy, slot):
        x, y, c, _ = _place()
        blk = outs[a].at[slot, rows(outs[a], c), :]
        return pltpu.make_async_remote_copy(
            src_ref=blk, dst_ref=blk, send_sem=send_sems.at[6 * a + j], recv_sem=recv_sems.at[6 * a + j],
            device_id=(*chip_xy, c), device_id_type=MESH)

    def over_d2d(outs, send_sems, recv_sems, a, j, slot, half):
        x, y, c, _ = _place()
        blk = outs[a].at[slot, rows(outs[a], half), :]
        return pltpu.make_async_remote_copy(
            src_ref=blk, dst_ref=blk, send_sem=send_sems.at[6 * a + 3 + j], recv_sem=recv_sems.at[6 * a + 3 + j],
            device_id=(x, y, 1 - c), device_id_type=MESH)

    def start(ins, outs, send_sems, recv_sems):
        x, y, c, chips = _place()
        for a in range(n):
            for j, chip_xy in enumerate(chips):
                over_ici(outs, send_sems, recv_sems, a, j, chip_xy, 2 * x + y).start()

    def finish(ins, outs, send_sems, recv_sems):
        x, y, c, chips = _place()
        for a in range(n):
            for j, (cx, cy) in enumerate(chips):
                over_ici(outs, send_sems, recv_sems, a, j, (cx, cy), 2 * cx + cy).wait_recv()
                over_d2d(outs, send_sems, recv_sems, a, j, 2 * cx + cy, c).start()
        for a in range(n):
            for j, (cx, cy) in enumerate(chips):
                over_d2d(outs, send_sems, recv_sems, a, j, 2 * cx + cy, 1 - c).wait_recv()
        for a in range(n):
            for j, (cx, cy) in enumerate(chips):
                over_ici(outs, send_sems, recv_sems, a, j, (cx, cy), 2 * x + y).wait_send()
                over_d2d(outs, send_sems, recv_sems, a, j, 2 * cx + cy, c).wait_send()

    return _Carry(shards, [jax.ShapeDtypeStruct(s.shape, s.dtype) for s in shards],
                  {a: a for a in range(n)}, 6 * n, start, finish)


def _rs_pair(grads, name):
    n = len(grads)

    def body(*refs):
        ins, gots = refs[:n], refs[n:2 * n]
        send_sems, recv_sems = refs[2 * n:]
        x, y, c, _ = _place()
        cps = []
        for a in range(n):
            r2 = ins[a].shape[1] // 2
            cp = pltpu.make_async_remote_copy(
                src_ref=ins[a].at[:, pl.ds((1 - c) * r2, r2), :], dst_ref=gots[a],
                send_sem=send_sems.at[a], recv_sem=recv_sems.at[a],
                device_id=(x, y, 1 - c), device_id_type=MESH)
            cp.start()
            cps.append(cp)
        for cp in cps:
            cp.wait()

    half = [jax.ShapeDtypeStruct((NCHIP, g.shape[1] // 2, g.shape[2]), g.dtype) for g in grads]
    return pl.pallas_call(
        body, name=name, out_shape=half, in_specs=[ANY] * n, out_specs=[ANY] * n,
        scratch_shapes=[pltpu.SemaphoreType.DMA((n,)), pltpu.SemaphoreType.DMA((n,))],
        compiler_params=_cp(),
    )(*grads)


def _pair_carry(grads):
    n = len(grads)

    def copy(ins, outs, send_sems, recv_sems, a):
        x, y, c, _ = _place()
        r2 = ins[a].shape[1] // 2
        return pltpu.make_async_remote_copy(
            src_ref=ins[a].at[:, pl.ds((1 - c) * r2, r2), :], dst_ref=outs[a],
            send_sem=send_sems.at[a], recv_sem=recv_sems.at[a],
            device_id=(x, y, 1 - c), device_id_type=MESH)

    def start(ins, outs, send_sems, recv_sems):
        for a in range(n):
            copy(ins, outs, send_sems, recv_sems, a).start()

    def finish(ins, outs, send_sems, recv_sems):
        for a in range(n):
            copy(ins, outs, send_sems, recv_sems, a).wait()

    half = [jax.ShapeDtypeStruct((NCHIP, g.shape[1] // 2, g.shape[2]), g.dtype) for g in grads]
    return _Carry(grads, half, {}, n, start, finish)


def _chips_carry(parts, piece=(0, 1, 1), into=None):
    n = len(parts)
    first, count, of = piece

    def rows(ref):
        step = ref.shape[1] // of
        return pl.ds(first * step, count * step)

    def send(ins, outs, send_sems, recv_sems, a, j, chip_xy):
        x, y, c, _ = _place()
        me, them = 2 * x + y, 2 * chip_xy[0] + chip_xy[1]
        return pltpu.make_async_remote_copy(
            src_ref=ins[a].at[them, rows(ins[a]), :],
            dst_ref=outs[a].at[me - (me > them).astype(jnp.int32), rows(outs[a]), :],
            send_sem=send_sems.at[3 * a + j], recv_sem=recv_sems.at[3 * a + j],
            device_id=(*chip_xy, c), device_id_type=MESH)

    def start(ins, outs, send_sems, recv_sems):
        _, _, _, chips = _place()
        for a in range(n):
            for j, chip_xy in enumerate(chips):
                send(ins, outs, send_sems, recv_sems, a, j, chip_xy).start()

    def finish(ins, outs, send_sems, recv_sems):
        x, y, c, chips = _place()
        me = 2 * x + y
        for a in range(n):
            for j, (cx, cy) in enumerate(chips):
                them = 2 * cx + cy
                blk = outs[a].at[them - (them > me).astype(jnp.int32), rows(outs[a]), :]
                pltpu.make_async_remote_copy(
                    src_ref=blk, dst_ref=blk, send_sem=send_sems.at[3 * a + j], recv_sem=recv_sems.at[3 * a + j],
                    device_id=(cx, cy, c), device_id_type=MESH).wait_recv()
        for a in range(n):
            for j, chip_xy in enumerate(chips):
                send(ins, outs, send_sems, recv_sems, a, j, chip_xy).wait_send()

    landing = [jax.ShapeDtypeStruct((NCHIP - 1,) + p.shape[1:], p.dtype) for p in parts]
    if into is None:
        return _Carry(parts, landing, {}, 3 * n, start, finish)
    return _Carry(list(parts) + list(into), landing, {n + a: a for a in range(n)}, 3 * n, start, finish)


def _swap_carry(fulls, layers):
    n = len(fulls)

    def copy(outs, send_sems, recv_sems, a, half):
        x, y, c, _ = _place()
        r2 = outs[a].shape[1] // 2
        blk = outs[a].at[pl.ds(*layers[a]), pl.ds(half * r2, r2), :]
        return pltpu.make_async_remote_copy(
            src_ref=blk, dst_ref=blk, send_sem=send_sems.at[a], recv_sem=recv_sems.at[a],
            device_id=(x, y, 1 - c), device_id_type=MESH)

    def start(ins, outs, send_sems, recv_sems):
        c = lax.axis_index("c")
        for a in range(n):
            copy(outs, send_sems, recv_sems, a, c).start()

    def finish(ins, outs, send_sems, recv_sems):
        c = lax.axis_index("c")
        for a in range(n):
            copy(outs, send_sems, recv_sems, a, 1 - c).wait_recv()
        for a in range(n):
            copy(outs, send_sems, recv_sems, a, c).wait_send()

    return _Carry(fulls, [jax.ShapeDtypeStruct(f.shape, f.dtype) for f in fulls], {a: a for a in range(n)}, n,
                  start, finish)


def _rs_swap(fulls, layers):
    n = len(fulls)
    swap = _swap_carry(fulls, layers)

    def body(*refs):
        outs, (send_sems, recv_sems) = refs[n:2 * n], refs[2 * n:]
        swap.start(None, outs, send_sems, recv_sems)
        swap.finish(None, outs, send_sems, recv_sems)

    return pl.pallas_call(
        body, name="rs_swap", out_shape=swap.outs, in_specs=[ANY] * n, out_specs=[ANY] * n,
        input_output_aliases=swap.aliases,
        scratch_shapes=[pltpu.SemaphoreType.DMA((n,)), pltpu.SemaphoreType.DMA((n,))],
        compiler_params=_cp(),
    )(*fulls)


def _tail_weight_grads(merged_t, b_in_t, a_in_t, dy, dbr_b, dbr_a, name, tn=256):
    S = dy.shape[0]
    nn = D // tn

    def body(mt_ref, bt_ref, at_ref, dy_ref, db_ref, da_ref, go_ref, gh_ref, gp_ref):
        go_ref[...] = jnp.dot(mt_ref[...], dy_ref[...], preferred_element_type=F32).astype(BF16)
        gh_ref[...] = jnp.dot(bt_ref[...], db_ref[...], preferred_element_type=F32).astype(BF16)
        gp_ref[...] = jnp.dot(at_ref[...], da_ref[...], preferred_element_type=F32).astype(BF16)

    left = lambda rows: pl.BlockSpec((rows, S), lambda n: (0, 0))
    right = pl.BlockSpec((S, tn), lambda n: (0, n))
    out = pl.BlockSpec((D, tn), lambda n: (0, n))
    return pl.pallas_call(
        body, name=name, grid=(nn,), in_specs=[left(D), left(D), left(POOL_W), right, right, right],
        out_specs=[out, out, pl.BlockSpec((None, POOL_W, tn), lambda n: (n, 0, 0))],
        out_shape=[jax.ShapeDtypeStruct((D, D), BF16), jax.ShapeDtypeStruct((D, D), BF16),
                   jax.ShapeDtypeStruct((NCHIP, POOL_W, D // NCHIP), BF16)],
        compiler_params=_cp(("parallel",)),
    )(merged_t, b_in_t, a_in_t, dy, dbr_b, dbr_a)


class _GatherInProj:
    def __init__(self, slot, order):
        self.slot, self.order = slot, order


def _proj_with_gather(h, w_slot, order, name, tn=256):
    S, K = h.shape
    nsh, _, ns = w_slot.shape
    tps = ns // tn
    nt = nsh * tps
    r2 = K // 2

    def body(ord_ref, h_ref, w_in_ref, o_ref, w_ref, wbuf, tile_sems, send_sems, recv_sems):
        n = pl.program_id(0)
        x, y, c, chips = _place()

        def half(slot, which):
            return w_ref.at[slot, pl.ds(which * r2, r2), :]

        def over_ici(j, slot):
            blk = half(slot, c)
            return pltpu.make_async_remote_copy(src_ref=blk, dst_ref=blk, send_sem=send_sems.at[j],
                                                recv_sem=recv_sems.at[j], device_id=(*chips[j], c),
                                                device_id_type=MESH)

        def over_d2d(j, which):
            blk = half(2 * chips[j][0] + chips[j][1], which)
            return pltpu.make_async_remote_copy(src_ref=blk, dst_ref=blk, send_sem=send_sems.at[3 + j],
                                                recv_sem=recv_sems.at[3 + j], device_id=(x, y, 1 - c),
                                                device_id_type=MESH)

        def tile_copy(step, slot):
            shard = ord_ref[step // tps]
            return pltpu.make_async_copy(w_ref.at[shard, :, pl.ds((step % tps) * tn, tn)], wbuf.at[slot],
                                         tile_sems.at[slot])

        @pl.when(n == 0)
        def _():
            for j in range(3):
                over_ici(j, 2 * x + y).start()
            tile_copy(0, 0).start()

        for j in range(3):
            @pl.when(n == (j + 1) * tps - 1)
            def _(j=j):
                over_ici(j, 2 * chips[j][0] + chips[j][1]).wait_recv()
                over_d2d(j, c).start()
                over_d2d(j, 1 - c).wait_recv()

        @pl.when(n + 1 < nt)
        def _():
            tile_copy(n + 1, (n + 1) % 2).start()

        tile_copy(n, n % 2).wait()
        o_ref[...] = jnp.dot(h_ref[...], wbuf[n % 2], preferred_element_type=F32).astype(o_ref.dtype)

        @pl.when(n == nt - 1)
        def _():
            for j in range(3):
                over_ici(j, 2 * x + y).wait_send()
                over_d2d(j, c).wait_send()

    return pl.pallas_call(
        body, name=name,
        out_shape=[jax.ShapeDtypeStruct((S, nsh * ns), BF16), jax.ShapeDtypeStruct(w_slot.shape, w_slot.dtype)],
        grid_spec=pltpu.PrefetchScalarGridSpec(
            num_scalar_prefetch=1, grid=(nt,),
            in_specs=[pl.BlockSpec((S, K), lambda n, o_ref: (0, 0)), ANY],
            out_specs=[pl.BlockSpec((S, tn), lambda n, o_ref: (0, o_ref[n // tps] * tps + n % tps)), ANY],
            scratch_shapes=[pltpu.VMEM((2, K, tn), w_slot.dtype), pltpu.SemaphoreType.DMA((2,)),
                            pltpu.SemaphoreType.DMA((6,)), pltpu.SemaphoreType.DMA((6,))]),
        input_output_aliases={2: 1},
        compiler_params=_cp(("arbitrary",)),
    )(order, h, w_slot)


def _mm_ride(a, b, carry, **kw):
    if carry is None:
        return _mm(a, b, **kw), []
    return _mm(a, b, carry=carry, **kw)


def _layer_fwd(l, x, ada, w, small, ride, target=None):
    shift, scale, gate = ada[:, 0:D], ada[:, D:2 * D], ada[:, 2 * D:3 * D]
    carry, landed = ride("prenorm")
    (h, h_t), outs = _prenorm_fwd(x, small["g_pre"][l], scale, shift, f"prenorm_fwd{l}", carry)
    landed(outs)
    carry, landed = ride("proj")
    if isinstance(carry, _GatherInProj):
        proj, full = _proj_with_gather(h, carry.slot, carry.order, f"proj{l}")
        outs = [full]
    else:
        proj, outs = _mm_ride(h, w["w_in"][l], carry, name=f"proj{l}", b_mode="nn_sh", tm=2048, out_dtype=BF16)
    landed(outs)
    a_in, a_in_t = _pool_fwd(proj, small["pool_w"][l], small["pool_scale"][l], f"pool_fwd{l}")
    carry, landed = ride("hgrn")
    (b_in, b_in_t, o_raw, states, mild, cum), outs = _hgrn_fwd(proj, small["lb"][l], small["hgrn_norm_g"][l],
                                                              f"hgrn_fwd{l}", carry=carry)
    landed(outs)
    carry, landed = ride("tail")
    (br_a, br_b, merged_t, y, *x_new), outs = _layer_tail_fwd(
        proj, a_in, b_in, x, w["w_pool_o"][l], w["w_hgrn_o"][l].reshape(D, D), w["w_out"][l].reshape(D, D),
        gate, small["g_post"][l], f"tail_fwd{l}", target=target, carry=carry)
    landed(outs)
    saved = dict(x=x, h_t=h_t, proj=proj, a_in_t=a_in_t, b_in_t=b_in_t, o_raw=o_raw, states=states, mild=mild,
                 cum=cum,
                 br_a=br_a, br_b=br_b, merged_t=merged_t, y=y, scale=scale, gate=gate)
    return x_new, saved


def _layer_bwd(l, dxn, sv, w, small, ride):
    carry, landed = ride["head"](None)
    (dy, dbr_a, dbr_b, dproj, da_in, db_in, dgate, dg_post), outs = _layer_head_bwd(
        dxn, sv["y"], sv["proj"], sv["br_a"], sv["br_b"], w["w_pool_o"][l], w["w_hgrn_o"][l].reshape(D, D),
        w["w_out"][l].reshape(D, D), sv["gate"], small["g_post"][l], f"head_bwd{l}", carry)
    landed(outs)
    gw_out, gw_hgrn_o, gw_pool_o = _tail_weight_grads(sv["merged_t"], sv["b_in_t"], sv["a_in_t"], dy, dbr_b,
                                                      dbr_a, f"gw_tail{l}")
    big = dict(w_pool_o=gw_pool_o, w_hgrn_o=gw_hgrn_o.reshape(NCHIP, D // NCHIP, D),
               w_out=gw_out.reshape(NCHIP, D // NCHIP, D))
    carry, landed = ride["hgrn"](big)
    (dproj, dlb, dgn), outs = _hgrn_bwd(db_in, sv["proj"], sv["o_raw"], sv["states"], sv["mild"], sv["cum"],
                                        small["lb"][l], small["hgrn_norm_g"][l], dproj, f"hgrn_bwd{l}",
                                        carry=carry)
    landed(outs)
    dproj, dpw, dpsc = _pool_bwd(da_in, sv["proj"], small["pool_w"][l], small["pool_scale"][l], dproj,
                                 f"pool_bwd{l}")
    little = dict(dgate=dgate, g_post=dg_post, pool_w=dpw, pool_scale=dpsc, lb=dlb,
                  hgrn_norm_g=jnp.sum(dgn, axis=0, keepdims=True))
    carry, landed = ride["gw_in"](little)
    big["w_in"], outs = _mm_ride(sv["h_t"], dproj, carry, name=f"gw_in{l}", out_shards=NCHIP, out_dtype=BF16)
    landed(outs)
    carry, landed = ride["d_h"](big)
    dh, outs = _mm_ride(dproj, w["w_in"][l], carry, name=f"d_h{l}", b_mode="nt_shk", tn=1024)
    landed(outs)
    carry, landed = ride["prenorm"](big)
    (dx, dshift, dscale, dg_pre), outs = _prenorm_bwd(dh, dxn, sv["x"], small["g_pre"][l], sv["scale"],
                                                      f"prenorm_bwd{l}", carry)
    landed(outs)
    little.update(dshift=dshift, dscale=dscale, g_pre=dg_pre)
    return dx, big, little


SMALL_ROWS = 176


def _rows8(t):
    t = t.reshape(-1, D)
    return jnp.pad(t, ((0, -t.shape[0] % 8), (0, 0)))


def _pack_small(parts):
    row_keys = ("dshift", "dscale", "dgate", "g_pre", "g_post", "lb", "pool_scale", "hgrn_norm_g")
    flat = [p[k] for p in parts for k in row_keys] + [p["pool_w"].reshape(GROUPS * 128 * 128 // D, D) for p in parts]
    nk = len(row_keys)

    def body(*refs):
        o_ref = refs[-1]
        o_ref[...] = jnp.zeros((SMALL_ROWS, D), F32)
        for l in range(2):
            dshift, dscale, dgate, g_pre, g_post, lb, pscale, gn = refs[l * nk:(l + 1) * nk]
            for r, ref in enumerate((dshift, dscale, dgate)):
                o_ref[3 * l + r:3 * l + r + 1, :] = ref[...]
            o_ref[8 + l:9 + l, :] = g_pre[...]
            o_ref[16 + l:17 + l, :] = g_post[...]
            o_ref[24 + l:25 + l, :] = lb[...]
            o_ref[160:161, l * POOL_W:(l + 1) * POOL_W] = pscale[...]
            o_ref[168:169, l * HD:(l + 1) * HD] = gn[...]
            pw = refs[2 * nk + l]
            rows = pw.shape[0]
            o_ref[32 + l * rows:32 + (l + 1) * rows, :] = pw[...]

    return pl.pallas_call(body, name="pack_small", out_shape=jax.ShapeDtypeStruct((SMALL_ROWS, D), F32),
                          compiler_params=_cp())(*flat)


def _unpack_small(p):
    return (p[0:6].reshape(2, 3 * D), p[8:10], p[16:18], p[24:26], p[32:160].reshape(2, GROUPS, 128, 128),
            p[160:161].reshape(2, POOL_W), p[168:169, 0:2 * HD].reshape(2, HD))


def kernel(x, c, w_ada, b_ada, g_pre, g_post, w_in, pool_w, pool_scale, lb_logits, hgrn_norm_g, w_pool_o, w_hgrn_o, w_out, loss_target, m_w_ada, m_b_ada, m_g_pre, m_g_post, m_w_in, m_pool_w, m_pool_scale, m_lb_logits, m_hgrn_norm_g, m_w_pool_o, m_w_hgrn_o, m_w_out, v_w_ada, v_b_ada, v_g_pre, v_g_post, v_w_in, v_pool_w, v_pool_scale, v_lb_logits, v_hgrn_norm_g, v_w_pool_o, v_w_hgrn_o, v_w_out):
    ax, ay, ac = lax.axis_index("x"), lax.axis_index("y"), lax.axis_index("c")
    chip = 2 * ax + ay
    dev = 2 * chip + ac
    xe, te = x[0], loss_target[0]
    ada_s = w_ada.shape[2]

    big_names = ("w_in", "w_pool_o", "w_hgrn_o", "w_out")
    big_w = (w_in, w_pool_o, w_hgrn_o, w_out)
    core = jnp.stack([ac]).astype(jnp.int32)
    place = jnp.stack([chip, ac]).astype(jnp.int32)
    slots = {("w_in", l): _cast_to_slot(place, w_in, l, f"cast_w_in{l}") for l in range(2)}
    rest = [(k, l) for l in range(2) for k in big_names[1:]]
    slots.update(zip(rest, _cast_to_slots(place, [(dict(zip(big_names, big_w))[k], l) for k, l in rest],
                                          "cast_rest")))
    w = {k: [None, None] for k in big_names}
    def fills(keys):
        def landed(outs):
            for (k, l), o in zip(keys, outs):
                w[k][l] = slots[k, l] = o
        return landed

    rest0 = [(k, 0) for k in big_names[1:]]
    rest1 = [(k, 1) for k in big_names[1:]]
    no_carry = (None, lambda outs: None)
    order = jnp.stack([chip, 2 * (1 - ax) + ay, 2 * ax + (1 - ay), 2 * (1 - ax) + (1 - ay)]).astype(jnp.int32)

    def ride_fwd0(stage):
        if stage == "proj":
            return _GatherInProj(slots["w_in", 0], order), fills([("w_in", 0)])
        if stage == "hgrn":
            return (_join_carries(_gather_carry([slots[t] for t in rest0]),
                                  _gather_carry([slots["w_in", 1]], piece=(0, 2, 4))),
                    fills(rest0 + [("w_in", 1)]))
        if stage == "tail":
            return _gather_carry([slots["w_in", 1]], piece=(2, 1, 4)), fills([("w_in", 1)])
        return no_carry

    def ride_fwd1(stage):
        if stage == "prenorm":
            return _gather_carry([slots["w_in", 1]], piece=(3, 1, 4)), fills([("w_in", 1)])
        if stage == "hgrn":
            return _gather_carry([slots[t] for t in rest1]), fills(rest1)
        return no_carry

    c_all = _gather_small(jnp.broadcast_to(c, (8, D)), "gather_c").reshape(NDEV, 8, D)[:, 0, :]
    c_pad = jnp.pad(c_all, ((0, ADA_PAD - NDEV), (0, 0)))
    b_sh = lax.dynamic_slice(b_ada, (0, chip * ada_s), (2, ada_s))
    ada_cols = _gather_small(_ada_fwd(c_pad, w_ada, b_sh), "gather_ada")
    ada_cols = ada_cols.reshape(NCHIP, 2, NDEV, 2, ada_s)[:, 0]
    ada_all = jnp.transpose(ada_cols, (2, 1, 0, 3)).reshape(2, NDEV, 3 * D)
    ada_me = lax.dynamic_slice(ada_all, (0, dev, 0), (2, 1, 3 * D))

    lbs = _lb_fwd(lb_logits)
    small = dict(g_pre=g_pre[:, None, :], g_post=g_post[:, None, :], pool_w=pool_w,
                 pool_scale=pool_scale[:, None, :], lb=lbs[:, None, :], hgrn_norm_g=hgrn_norm_g[:, None, :])

    (x1,), sv0 = _layer_fwd(0, xe, ada_me[0], w, small, ride_fwd0)
    (dx2, loss_blk), sv1 = _layer_fwd(1, x1, ada_me[1], w, small, ride_fwd1, target=te)

    parts, recv, held = {}, {}, {}

    def pair_ride(keys, grads):
        def landed(outs):
            held.update({kl: (g, o) for kl, g, o in zip(keys, grads, outs)})
        return _pair_carry(grads), landed

    def pair_adds(keys):
        gs, gots = zip(*[held.pop(kl) for kl in keys])
        if keys[0][0] == "w_in":
            parts[keys[0]] = _pair_add(core, gs[0], gots[0], f"rs_add_w_in{keys[0][1]}")
        else:
            parts.update(zip(keys, _pair_adds(core, gs, gots, f"rs_add_early{keys[0][1]}")))

    def exchange(keys):
        def landed(outs):
            recv.update(zip(keys, outs))
        return _chips_carry([parts[kl] for kl in keys]), landed

    def share(key, first, count):
        def landed(outs):
            (recv[key],) = outs
        into = [recv[key]] if key in recv else None
        return _chips_carry([parts[key]], piece=(first, count, 8), into=into), landed

    def together(*rides):
        carries, fns = zip(*rides)

        def landed(outs):
            for cr, fn in zip(carries, fns):
                fn(outs[:len(cr.outs)])
                outs = outs[len(cr.outs):]
        return _join_carries(*carries), landed

    def early(l):
        return [(k, l) for k in big_names[1:]]

    def pair_alone(keys, grads, tag):
        held.update({kl: (g, o) for kl, g, o in zip(keys, grads, _rs_pair(grads, f"rs_pair_{tag}"))})
        pair_adds(keys)

    def ride_hgrn1(big):
        return pair_ride(early(1), [big[k] for k in big_names[1:]])

    def ride_gw_in1(_):
        pair_adds(early(1))
        return exchange(early(1))

    def ride_d_h1(big):
        return pair_ride([("w_in", 1)], [big["w_in"]])

    def ride_prenorm1(_):
        pair_adds([("w_in", 1)])
        return share(("w_in", 1), 0, 1)

    def ride_head0(_):
        return share(("w_in", 1), 1, 3)

    def ride_hgrn0(big):
        pair_alone(early(0), [big[k] for k in big_names[1:]], "early0")
        return together(exchange(early(0)), share(("w_in", 1), 4, 4))

    def ride_d_h0(big):
        pair_alone([("w_in", 0)], [big["w_in"]], "w_in0")
        return share(("w_in", 0), 0, 4)

    def ride_prenorm0(_):
        return share(("w_in", 0), 4, 4)

    no_ride = lambda so_far: no_carry
    dx1, big1, little1 = _layer_bwd(1, dx2, sv1, w, small, dict(head=no_ride, hgrn=ride_hgrn1, gw_in=ride_gw_in1,
                                                                d_h=ride_d_h1, prenorm=ride_prenorm1))

    gathered = {}
    zero_row = jnp.zeros((1, D), F32)

    def ride_gw_in0(little):
        so_far = dict(little, dshift=zero_row, dscale=zero_row, g_pre=zero_row)

        def landed(outs):
            (gathered["early"],) = outs

        red = [_chip_sum(place, parts["w_in", 1], recv["w_in", 1], 1, None, "rs_sum_w_in1")]
        red += _chip_sums(place, [[parts[k, l] for l in range(2)] for k in big_names[1:]],
                          [[recv[k, l] for l in range(2)] for k in big_names[1:]], "rs_sum_early")

        def swapped(outs):
            gathered["sums"] = outs
        return together((_gather_rows_carry(_pack_small([so_far, little1])), landed),
                        (_swap_carry(red, [(1, 1)] + [(0, 2)] * 3), swapped))

    dx0, big0, little0 = _layer_bwd(0, dx1, sv0, w, small,
                                    dict(head=ride_head0, hgrn=ride_hgrn0, gw_in=ride_gw_in0, d_h=ride_d_h0,
                                         prenorm=ride_prenorm0))
    loss_row = jnp.broadcast_to(loss_blk[0:1, 0:1], (1, D))
    late = _rows8(jnp.stack([little0["dshift"], little0["dscale"], little0["g_pre"], loss_row]))
    late = _gather_small(late, "gather_small_late").reshape(NDEV, 8, D)
    loss = jnp.sum(late[:, 3, 0])
    packed = gathered["early"].reshape(NDEV, SMALL_ROWS, D)
    packed = packed.at[:, 0:2, :].set(late[:, 0:2, :]).at[:, 8:9, :].set(late[:, 2:3, :])
    red = _chip_sum(place, parts["w_in", 0], recv["w_in", 0], 0, gathered["sums"][0], "rs_sum_w_in0")
    g_big = dict(zip(big_names, list(_rs_swap([red], [(0, 1)])) + list(gathered["sums"][1:])))

    def two(t):
        return t.reshape(-1, t.shape[-1])

    def upd(wt, g, m, v, name, echo=False):
        return [t.reshape(wt.shape) for t in _adamw(two(wt), two(g), two(m), two(v), name, echo)]

    *u_w_in, g_w_in = upd(w_in, g_big["w_in"], m_w_in, v_w_in, "adamw_w_in", echo=True)
    g_small = _sum_devices(packed)
    g_b_ada, g_g_pre, g_g_post, g_lb, g_pool_w, g_pool_scale, g_norm_g = _unpack_small(g_small)
    g_lb_logits = _lb_bwd(lb_logits, g_lb)
    d_ada_all = packed[:, 0:6, :].reshape(NDEV, 2, 3 * D)
    d_ada_sh = lax.dynamic_slice(jnp.transpose(d_ada_all, (1, 0, 2)), (0, 0, chip * ada_s), (2, NDEV, ada_s))
    d_ada_sh = jnp.pad(d_ada_sh, ((0, 0), (0, ADA_PAD - NDEV), (0, 0)))
    g_w_ada = _ada_wgrad(c_pad.T, d_ada_sh)

    u_w_ada = upd(w_ada, g_w_ada, m_w_ada, v_w_ada, "adamw_w_ada")
    early_w = dict(w_pool_o=(w_pool_o, m_w_pool_o, v_w_pool_o), w_hgrn_o=(w_hgrn_o, m_w_hgrn_o, v_w_hgrn_o),
                   w_out=(w_out, m_w_out, v_w_out))
    u_early = _adamw_many([(two(early_w[k][0]), two(g_big[k]), two(early_w[k][1]), two(early_w[k][2]))
                           for k in big_names[1:]], "adamw_early")
    (*u_w_pool_o, g_w_pool_o), (*u_w_hgrn_o, g_w_hgrn_o), (*u_w_out, g_w_out) = [
        [t.reshape(early_w[k][0].shape) for t in four] for k, four in zip(big_names[1:], u_early)]
    small_w = dict(b_ada=(b_ada, m_b_ada, v_b_ada), g_pre=(g_pre, m_g_pre, v_g_pre),
                   g_post=(g_post, m_g_post, v_g_post), lb_logits=(lb_logits, m_lb_logits, v_lb_logits),
                   pool_w=(pool_w, m_pool_w, v_pool_w), pool_scale=(pool_scale, m_pool_scale, v_pool_scale),
                   hgrn_norm_g=(hgrn_norm_g, m_hgrn_norm_g, v_hgrn_norm_g))
    in_rows = [tuple(t.reshape(rows, width) for t in small_w[key]) for key, _, rows, width in SMALL_PARTS]
    u_rows = _adamw_small(g_small, g_lb_logits, in_rows)
    u_small = {key: [t.reshape(small_w[key][0].shape) for t in u_rows[p]]
               for p, (key, _, _, _) in enumerate(SMALL_PARTS)}

    grads_out = (g_w_ada, g_b_ada, g_g_pre, g_g_post, g_w_in, g_pool_w, g_pool_scale, g_lb_logits,
                 g_norm_g, g_w_pool_o, g_w_hgrn_o, g_w_out)

    def ordered(k):
        s = lambda key: u_small[key][k]
        return (u_w_ada[k], s("b_ada"), s("g_pre"), s("g_post"), u_w_in[k], s("pool_w"), s("pool_scale"),
                s("lb_logits"), s("hgrn_norm_g"), u_w_pool_o[k], u_w_hgrn_o[k], u_w_out[k])

    return (loss, dx0[None], *grads_out, *ordered(0), *ordered(1), *ordered(2))
```

```python
import functools

import jax
import jax.numpy as jnp
from jax import lax
from jax.experimental import pallas as pl
from jax.experimental.pallas import tpu as pltpu

F32 = jnp.float32
BF16 = jnp.bfloat16
MESH = pl.DeviceIdType.MESH

D = 1024
HEADS = 8
HD = 128
GROUPS = 4
POOL_W = 512
CH = 128
SB_WIDE = 32
SB = 16
NH = 2
IN_W = 7168
NCHIP = 4
NDEV = 8
EPS = 1e-6
PV0, PG0, HQ0, HF0, HI0, HG0 = 0, 4, 8, 16, 24, 32
MGP_BLK, MGH_BLK = 5, 6

LR, B1, B2, AEPS, WD, STEP = 0.001, 0.9, 0.999, 1e-08, 0.01, 10
VMEM_LIMIT = 56 * 1024 * 1024


def _cp(sem=None, **kw):
    if sem is not None:
        kw["dimension_semantics"] = sem
    return pltpu.CompilerParams(vmem_limit_bytes=VMEM_LIMIT, **kw)


def _sig(z):
    return 1.0 / (1.0 + jnp.exp(-z))


def _dsilu(z, s):
    return s * (1.0 + z * (1.0 - s))


def _row_tile(rows, cap):
    if rows <= cap:
        return rows
    t = 1 << (cap.bit_length() - 1)
    while rows % t:
        t //= 2
    return t


ANY = pl.BlockSpec(memory_space=pl.ANY)


class _Carry:
    def __init__(self, ins, outs, aliases, n_sem, start, finish):
        self.ins, self.outs, self.aliases, self.n_sem = list(ins), list(outs), dict(aliases), n_sem
        self.start, self.finish = start, finish


class _SemWindow:
    def __init__(self, ref, base):
        self._ref, self._base = ref, base

    @property
    def at(self):
        return self

    def __getitem__(self, k):
        return self._ref.at[self._base + k]


def _join_carries(*carries):
    ins, outs, aliases, spans, n_sem = [], [], {}, [], 0
    for cr in carries:
        aliases.update({len(ins) + i: len(outs) + o for i, o in cr.aliases.items()})
        spans.append((len(ins), len(cr.ins), len(outs), len(cr.outs), n_sem))
        ins, outs, n_sem = ins + cr.ins, outs + cr.outs, n_sem + cr.n_sem

    def run(which):
        def fn(i_refs, o_refs, send_sems, recv_sems):
            for cr, (i0, ni, o0, no, s0) in zip(carries, spans):
                getattr(cr, which)(i_refs[i0:i0 + ni], o_refs[o0:o0 + no], _SemWindow(send_sems, s0),
                                   _SemWindow(recv_sems, s0))
        return fn

    return _Carry(ins, outs, aliases, n_sem, run("start"), run("finish"))


def _call(body, *, name, grid, in_specs, out_specs, out_shape, args, scratch_shapes=(), sem=None, carry=None,
          aliases=None):
    in_specs, out_specs, out_shape = list(in_specs), list(out_specs), list(out_shape)
    scratch_shapes = list(scratch_shapes)
    aliases = dict(aliases or {})
    if carry is None:
        outs = pl.pallas_call(body, name=name, grid=grid, in_specs=in_specs, out_specs=out_specs,
                              out_shape=out_shape, scratch_shapes=scratch_shapes, input_output_aliases=aliases,
                              compiler_params=_cp(sem))(*args)
        return list(outs)
    n_in, n_out, n_scr = len(in_specs), len(out_specs), len(scratch_shapes)
    c_in, c_out = len(carry.ins), len(carry.outs)

    def wrapped(*refs):
        k_in, rest = refs[:n_in], refs[n_in:]
        ci, rest = rest[:c_in], rest[c_in:]
        k_out, rest = rest[:n_out], rest[n_out:]
        co, rest = rest[:c_out], rest[c_out:]
        k_scr, (ssem, rsem) = rest[:n_scr], rest[n_scr:]
        pids = [pl.program_id(d) for d in range(len(grid))]
        first = functools.reduce(jnp.logical_and, [p == 0 for p in pids])
        last = functools.reduce(jnp.logical_and, [p == g - 1 for p, g in zip(pids, grid)])

        @pl.when(first)
        def _():
            carry.start(ci, co, ssem, rsem)

        body(*k_in, *k_out, *k_scr)

        @pl.when(last)
        def _():
            carry.finish(ci, co, ssem, rsem)

    outs = pl.pallas_call(
        wrapped, name=name, grid=grid, in_specs=in_specs + [ANY] * c_in, out_specs=out_specs + [ANY] * c_out,
        out_shape=out_shape + carry.outs,
        input_output_aliases={**aliases, **{n_in + i: n_out + o for i, o in carry.aliases.items()}},
        scratch_shapes=scratch_shapes + [pltpu.SemaphoreType.DMA((carry.n_sem,))] * 2,
        compiler_params=_cp(("arbitrary",) * len(grid)),
    )(*args, *carry.ins)
    return list(outs)


def _mm(a, b, *, name, b_mode="nn", out_shards=0, tm=1024, tn=256, tk=None, out_dtype=F32, carry=None):
    assert b_mode in ("nn", "nn_sh", "nt_shk"), b_mode
    M, K = a.shape
    if b_mode == "nn":
        N = b.shape[1]
    elif b_mode == "nn_sh":
        N = b.shape[0] * b.shape[2]
    else:
        N = b.shape[1]
    tm = _row_tile(M, tm)
    if b_mode == "nn_sh":
        tn = _row_tile(b.shape[2], tn)
    elif out_shards:
        tn = _row_tile(N // out_shards, tn)
    else:
        tn = _row_tile(N, tn)
    if tk is None:
        tk = K if b_mode != "nt_shk" else b.shape[2]
    if b_mode == "nt_shk":
        tk = _row_tile(b.shape[2], tk)
    nm, nn, nk = M // tm, N // tn, K // tk

    a_spec = pl.BlockSpec((tm, tk), lambda m, n, k: (m, k))
    if b_mode == "nn":
        b_spec = pl.BlockSpec((tk, tn), lambda m, n, k: (k, n))
    elif b_mode == "nn_sh":
        nps = b.shape[2] // tn
        b_spec = pl.BlockSpec((None, tk, tn), lambda m, n, k: (n // nps, k, n % nps))
    else:
        kps = b.shape[2] // tk
        b_spec = pl.BlockSpec((None, tn, tk), lambda m, n, k: (k // kps, n, k % kps))
    if out_shards:
        ops = (N // out_shards) // tn
        o_spec = pl.BlockSpec((None, tm, tn), lambda m, n, k: (n // ops, m, n % ops))
        o_shape = jax.ShapeDtypeStruct((out_shards, M, N // out_shards), out_dtype)
    else:
        o_spec = pl.BlockSpec((tm, tn), lambda m, n, k: (m, n))
        o_shape = jax.ShapeDtypeStruct((M, N), out_dtype)
    dn = (((1,), (1,)), ((), ())) if b_mode == "nt_shk" else (((1,), (0,)), ((), ()))

    def body(a_ref, b_ref, o_ref, acc_ref):
        k = pl.program_id(2)

        @pl.when(k == 0)
        def _():
            acc_ref[...] = jnp.zeros(acc_ref.shape, F32)

        acc_ref[...] += lax.dot_general(a_ref[...].astype(BF16), b_ref[...].astype(BF16), dn,
                                        preferred_element_type=F32)

        @pl.when(k == nk - 1)
        def _():
            o_ref[...] = acc_ref[...].astype(o_ref.dtype)

    outs = _call(body, name=name, grid=(nm, nn, nk), in_specs=[a_spec, b_spec], out_specs=[o_spec],
                 out_shape=[o_shape], scratch_shapes=[pltpu.VMEM((tm, tn), F32)],
                 sem=("parallel", "parallel", "arbitrary"), args=(a, b), carry=carry)
    return outs[0] if carry is None else (outs[0], outs[1:])


def _rowvec(n=D):
    return pl.BlockSpec((1, n), lambda i: (0, 0))


def _prenorm_fwd(x, g, scale, shift, name, carry=None):
    S = x.shape[0]
    tr = _row_tile(S, 256)

    def body(x_ref, g_ref, sc_ref, sh_ref, h_ref, ht_ref):
        xv = x_ref[...]
        r = lax.rsqrt(jnp.mean(xv * xv, axis=-1, keepdims=True) + EPS)
        hv = (xv * r) * g_ref[...] * (1.0 + sc_ref[...]) + sh_ref[...]
        h_ref[...] = hv.astype(BF16)
        ht_ref[...] = hv.T.astype(BF16)

    outs = _call(
        body, name=name, grid=(S // tr,),
        in_specs=[pl.BlockSpec((tr, D), lambda i: (i, 0)), _rowvec(), _rowvec(), _rowvec()],
        out_specs=[pl.BlockSpec((tr, D), lambda i: (i, 0)), pl.BlockSpec((D, tr), lambda i: (0, i))],
        out_shape=[jax.ShapeDtypeStruct((S, D), BF16), jax.ShapeDtypeStruct((D, S), BF16)],
        sem=("parallel",), args=(x, g, scale, shift), carry=carry)
    return outs[:2], outs[2:]


def _prenorm_bwd(dh, dxn, x, g, scale, name, carry=None):
    S = x.shape[0]
    tr = _row_tile(S, 256)

    def body(dh_ref, dxn_ref, x_ref, g_ref, sc_ref, dx_ref, dsh_ref, dsc_ref, dg_ref):
        i = pl.program_id(0)

        @pl.when(i == 0)
        def _():
            dsh_ref[...] = jnp.zeros((1, D), F32)
            dsc_ref[...] = jnp.zeros((1, D), F32)
            dg_ref[...] = jnp.zeros((1, D), F32)

        xv = x_ref[...]
        dhv = dh_ref[...]
        gv = g_ref[...]
        mod = 1.0 + sc_ref[...]
        r = lax.rsqrt(jnp.mean(xv * xv, axis=-1, keepdims=True) + EPS)
        xh = xv * r
        dsh_ref[...] += jnp.sum(dhv, axis=0, keepdims=True)
        dsc_ref[...] += jnp.sum(dhv * (xh * gv), axis=0, keepdims=True)
        dg_ref[...] += jnp.sum(dhv * mod * xh, axis=0, keepdims=True)
        u = dhv * mod * gv
        dx_ref[...] = dxn_ref[...] + r * u - xv * (r * r * r) * jnp.mean(u * xv, axis=-1, keepdims=True)

    tile = pl.BlockSpec((tr, D), lambda i: (i, 0))
    outs = _call(
        body, name=name, grid=(S // tr,),
        in_specs=[tile, tile, tile, _rowvec(), _rowvec()],
        out_specs=[tile, _rowvec(), _rowvec(), _rowvec()],
        out_shape=[jax.ShapeDtypeStruct((S, D), F32)] + [jax.ShapeDtypeStruct((1, D), F32)] * 3,
        sem=("arbitrary",), args=(dh, dxn, x, g, scale), carry=carry)
    return outs[:4], outs[4:]


def _layer_tail_fwd(proj, a_in, b_in, x, w_po, w_ho, w_out, gate, g, name, target=None, carry=None):
    S = proj.shape[0]
    tr = _row_tile(S, 256)
    nsh, _, wsh = w_po.shape
    n_in = 10 + (target is not None)

    def body(*refs):
        (mgp_ref, mgh_ref, a_ref, b_ref, x_ref, wpo_ref, who_ref, wout_ref, gate_ref, g_ref) = refs[:10]
        bra_ref, brb_ref, mt_ref, y_ref, xn_ref = refs[n_in:n_in + 5]
        av = a_ref[...]
        bra = jnp.concatenate([jnp.dot(av, wpo_ref[j], preferred_element_type=F32) for j in range(nsh)], axis=1)
        brb = jnp.dot(b_ref[...], who_ref[...], preferred_element_type=F32)
        mv = _sig(mgp_ref[...].astype(F32)) * bra + _sig(mgh_ref[...].astype(F32)) * brb
        bra_ref[...] = bra.astype(BF16)
        brb_ref[...] = brb.astype(BF16)
        mt_ref[...] = mv.T.astype(BF16)
        yv = jnp.dot(mv.astype(BF16), wout_ref[...], preferred_element_type=F32)
        y_ref[...] = yv
        r = lax.rsqrt(jnp.mean(yv * yv, axis=-1, keepdims=True) + EPS)
        xn = x_ref[...] + gate_ref[...] * ((yv * r) * g_ref[...])
        if target is None:
            xn_ref[...] = xn
        else:
            t_ref, l_ref = refs[10], refs[n_in + 5]

            @pl.when(pl.program_id(0) == 0)
            def _():
                l_ref[...] = jnp.zeros((8, 128), F32)

            err = xn - t_ref[...]
            xn_ref[...] = err * (1.0 / D)
            l_ref[...] += 0.5 * jnp.sum(jnp.mean(err * err, axis=-1, keepdims=True))

    tile = pl.BlockSpec((tr, D), lambda i: (i, 0))
    whole = lambda t: pl.BlockSpec(t.shape, lambda i: (0,) * t.ndim)
    last = target is not None
    outs = _call(
        body, name=name, grid=(S // tr,),
        in_specs=[pl.BlockSpec((tr, D), lambda i: (i, MGP_BLK)), pl.BlockSpec((tr, D), lambda i: (i, MGH_BLK)),
                  pl.BlockSpec((tr, POOL_W), lambda i: (i, 0)), tile, tile, whole(w_po), whole(w_ho),
                  whole(w_out), _rowvec(), _rowvec()] + [tile] * last,
        out_specs=[tile, tile, pl.BlockSpec((D, tr), lambda i: (0, i)), tile, tile]
        + [pl.BlockSpec((8, 128), lambda i: (0, 0))] * last,
        out_shape=[jax.ShapeDtypeStruct((S, D), BF16), jax.ShapeDtypeStruct((S, D), BF16),
                   jax.ShapeDtypeStruct((D, S), BF16), jax.ShapeDtypeStruct((S, D), F32),
                   jax.ShapeDtypeStruct((S, D), F32)] + [jax.ShapeDtypeStruct((8, 128), F32)] * last,
        sem=("arbitrary",) if last else ("parallel",),
        args=(proj, proj, a_in, b_in, x, w_po, w_ho, w_out, gate, g) + ((target,) if last else ()), carry=carry)
    return outs[:5 + last], outs[5 + last:]


def _layer_head_bwd(dxn, y, proj, br_a, br_b, w_po, w_ho, w_out, gate, g, name, carry=None):
    S = y.shape[0]
    tr = _row_tile(S, 256)
    nsh, _, wsh = w_po.shape

    def body(dxn_ref, y_ref, mgp_ref, mgh_ref, bra_ref, brb_ref, wpo_ref, who_ref, wout_ref, gate_ref, g_ref,
             dy_ref, dba_ref, dbb_ref, dproj_ref, dain_ref, dbin_ref, dgate_ref, dg_ref, dmgh_s):
        i = pl.program_id(0)
        j = pl.program_id(1)

        @pl.when((i == 0) & (j == 0))
        def _():
            dgate_ref[...] = jnp.zeros((1, D), F32)
            dg_ref[...] = jnp.zeros((1, D), F32)

        @pl.when(j == 1)
        def _():
            dproj_ref[...] = dmgh_s[...]

        @pl.when(j == 0)
        def _():
            everything(dxn_ref, y_ref, mgp_ref, mgh_ref, bra_ref, brb_ref, wpo_ref, who_ref, wout_ref, gate_ref,
                       g_ref, dy_ref, dba_ref, dbb_ref, dproj_ref, dain_ref, dbin_ref, dgate_ref, dg_ref, dmgh_s)

    def everything(dxn_ref, y_ref, mgp_ref, mgh_ref, bra_ref, brb_ref, wpo_ref, who_ref, wout_ref, gate_ref, g_ref,
                   dy_ref, dba_ref, dbb_ref, dproj_ref, dain_ref, dbin_ref, dgate_ref, dg_ref, dmgh_s):
        yv = y_ref[...]
        dv = dxn_ref[...]
        gv = g_ref[...]
        gt = gate_ref[...]
        r = lax.rsqrt(jnp.mean(yv * yv, axis=-1, keepdims=True) + EPS)
        yh = yv * r
        dgate_ref[...] += jnp.sum(dv * (yh * gv), axis=0, keepdims=True)
        dg_ref[...] += jnp.sum(dv * gt * yh, axis=0, keepdims=True)
        u = dv * gt * gv
        dy = (r * u - yv * (r * r * r) * jnp.mean(u * yv, axis=-1, keepdims=True)).astype(BF16)
        dy_ref[...] = dy
        dm = _dot_nt(dy, wout_ref[...])
        sp = _sig(mgp_ref[...].astype(F32))
        sh = _sig(mgh_ref[...].astype(F32))
        dba = (dm * sp).astype(BF16)
        dbb = (dm * sh).astype(BF16)
        dba_ref[...] = dba
        dbb_ref[...] = dbb
        dproj_ref[...] = (dm * bra_ref[...].astype(F32) * sp * (1.0 - sp)).astype(BF16)
        dmgh_s[...] = (dm * brb_ref[...].astype(F32) * sh * (1.0 - sh)).astype(BF16)
        dain = _dot_nt(dba[:, 0:wsh], wpo_ref[0])
        for k in range(1, nsh):
            dain = dain + _dot_nt(dba[:, k * wsh:(k + 1) * wsh], wpo_ref[k])
        dain_ref[...] = dain
        dbin_ref[...] = _dot_nt(dbb, who_ref[...])

    tile = pl.BlockSpec((tr, D), lambda i, j: (i, 0))
    whole = lambda t: pl.BlockSpec(t.shape, lambda i, j: (0,) * t.ndim)
    vec = pl.BlockSpec((1, D), lambda i, j: (0, 0))
    ahead = lambda i, j: jnp.minimum(i + j, S // tr - 1)
    tile_in = pl.BlockSpec((tr, D), lambda i, j: (ahead(i, j), 0))
    outs = _call(
        body, name=name, grid=(S // tr, 2),
        in_specs=[tile_in, tile_in, pl.BlockSpec((tr, D), lambda i, j: (ahead(i, j), MGP_BLK)),
                  pl.BlockSpec((tr, D), lambda i, j: (ahead(i, j), MGH_BLK)), tile_in, tile_in, whole(w_po),
                  whole(w_ho), whole(w_out), vec, vec],
        out_specs=[tile, tile, tile, pl.BlockSpec((tr, D), lambda i, j: (i, MGP_BLK + j)),
                   pl.BlockSpec((tr, POOL_W), lambda i, j: (i, 0)), tile, vec, vec],
        out_shape=[jax.ShapeDtypeStruct((S, D), BF16)] * 3
        + [jax.ShapeDtypeStruct((S, IN_W), BF16), jax.ShapeDtypeStruct((S, POOL_W), F32),
           jax.ShapeDtypeStruct((S, D), F32), jax.ShapeDtypeStruct((1, D), F32), jax.ShapeDtypeStruct((1, D), F32)],
        scratch_shapes=[pltpu.VMEM((tr, D), BF16)], sem=("arbitrary", "arbitrary"),
        args=(dxn, y, proj, proj, br_a, br_b, w_po, w_ho, w_out, gate, g), carry=carry)
    return outs[:8], outs[8:]


def _pool_pieces(u, g, S):
    rowi = lax.broadcasted_iota(jnp.int32, (S, 1), 0)

    def down(z, k):
        return jnp.where(rowi >= k, pltpu.roll(z, k, axis=0), 0.0)

    s2 = u + down(u, 1)
    s4 = s2 + down(s2, 2)
    s8 = s4 + down(s4, 4)
    s16 = s8 + down(s8, 8)
    win = jnp.where(g == 0, s2, jnp.where(g == 1, s4, jnp.where(g == 2, s8, s16)))
    w = jnp.where(g == 0, 2, jnp.where(g == 1, 4, jnp.where(g == 2, 8, 16)))
    count = jnp.minimum(rowi + 1, w).astype(F32)
    return win / count - u, count, rowi


def _pool_fwd(proj, pw, pscale, name):
    S = proj.shape[0]

    def body(pv_ref, pg_ref, pw_ref, sc_ref, a_ref, at_ref):
        g = pl.program_id(0)
        pooled, _, _ = _pool_pieces(pv_ref[...].astype(F32), g, S)
        pm = jnp.dot(pooled.astype(BF16), pw_ref[...].astype(BF16), preferred_element_type=F32)
        pgv = pg_ref[...].astype(F32)
        av = pm * sc_ref[...] * (pgv * _sig(pgv))
        a_ref[...] = av.astype(BF16)
        at_ref[...] = av.T.astype(BF16)

    outs = _call(
        body, name=name, grid=(GROUPS,),
        in_specs=[pl.BlockSpec((S, 128), lambda g: (0, PV0 + g)), pl.BlockSpec((S, 128), lambda g: (0, PG0 + g)),
                  pl.BlockSpec((None, 128, 128), lambda g: (g, 0, 0)), pl.BlockSpec((1, 128), lambda g: (0, g))],
        out_specs=[pl.BlockSpec((S, 128), lambda g: (0, g)), pl.BlockSpec((128, S), lambda g: (g, 0))],
        out_shape=[jax.ShapeDtypeStruct((S, POOL_W), BF16), jax.ShapeDtypeStruct((POOL_W, S), BF16)],
        sem=("parallel",), args=(proj, proj, pw, pscale))
    return outs


def _pool_bwd(da, proj, pw, pscale, dproj, name):
    S = proj.shape[0]

    def body(da_ref, pv_ref, pg_ref, pw_ref, sc_ref, dproj_in, dproj_ref, dpw_ref, dsc_ref, dpg_s):
        @pl.when(pl.program_id(1) == 1)
        def _():
            dproj_ref[...] = dpg_s[...]

        @pl.when(pl.program_id(1) == 0)
        def _():
            group(da_ref, pv_ref, pg_ref, pw_ref, sc_ref, dproj_ref, dpg_s, dpw_ref, dsc_ref)

    def group(da_ref, pv_ref, pg_ref, pw_ref, sc_ref, dpv_ref, dpg_ref, dpw_ref, dsc_ref):
        g = pl.program_id(0)
        pooled, count, rowi = _pool_pieces(pv_ref[...].astype(F32), g, S)
        pwb = pw_ref[...].astype(BF16)
        pm = jnp.dot(pooled.astype(BF16), pwb, preferred_element_type=F32)
        scv = sc_ref[...]
        pgv = pg_ref[...].astype(F32)
        sg = _sig(pgv)
        dav = da_ref[...]
        d_ps = dav * (pgv * sg)
        dpg_ref[...] = (dav * (pm * scv) * _dsilu(pgv, sg)).astype(BF16)
        dsc_ref[...] = jnp.sum(d_ps * pm, axis=0, keepdims=True)
        d_pm = (d_ps * scv).astype(BF16)
        dpw_ref[...] = lax.dot_general(pooled.astype(BF16), d_pm, (((0,), (0,)), ((), ())),
                                       preferred_element_type=F32)
        d_pooled = lax.dot_general(d_pm, pwb, (((1,), (1,)), ((), ())), preferred_element_type=F32)
        z = d_pooled / count

        def up(v, k):
            return jnp.where(rowi < S - k, pltpu.roll(v, S - k, axis=0), 0.0)

        t2 = z + up(z, 1)
        t4 = t2 + up(t2, 2)
        t8 = t4 + up(t4, 4)
        t16 = t8 + up(t8, 8)
        adj = jnp.where(g == 0, t2, jnp.where(g == 1, t4, jnp.where(g == 2, t8, t16)))
        dpv_ref[...] = (adj - d_pooled).astype(BF16)

    col = lambda g, j: (0, g)
    ahead = lambda g, j: jnp.minimum(g + j, GROUPS - 1)
    return pl.pallas_call(
        body, name=name, grid=(GROUPS, 2),
        in_specs=[pl.BlockSpec((S, 128), lambda g, j: (0, ahead(g, j))),
                  pl.BlockSpec((S, 128), lambda g, j: (0, PV0 + ahead(g, j))),
                  pl.BlockSpec((S, 128), lambda g, j: (0, PG0 + ahead(g, j))),
                  pl.BlockSpec((None, 128, 128), lambda g, j: (ahead(g, j), 0, 0)),
                  pl.BlockSpec((1, 128), lambda g, j: (0, ahead(g, j))), ANY],
        out_specs=[pl.BlockSpec((S, 128), lambda g, j: (0, PV0 + g + (PG0 - PV0) * j)),
                   pl.BlockSpec((None, 128, 128), lambda g, j: (g, 0, 0)), pl.BlockSpec((1, 128), col)],
        out_shape=[jax.ShapeDtypeStruct(dproj.shape, dproj.dtype),
                   jax.ShapeDtypeStruct((GROUPS, 128, 128), F32), jax.ShapeDtypeStruct((1, POOL_W), F32)],
        scratch_shapes=[pltpu.VMEM((S, 128), BF16)], input_output_aliases={5: 0},
        compiler_params=_cp(("arbitrary", "arbitrary")),
    )(da, proj, proj, pw, pscale, dproj)


SCAN_SHIFTS = tuple(1 << b for b in range(CH.bit_length() - 1))


def _chunk_cumsum(z, rowi):
    for sh in SCAN_SHIFTS:
        z = z + jnp.where(rowi >= sh, pltpu.roll(z, sh, axis=0), 0.0)
    return z


def _chunk_rev_cumsum(z, rowi):
    for sh in SCAN_SHIFTS:
        z = z + jnp.where(rowi < CH - sh, pltpu.roll(z, CH - sh, axis=0), 0.0)
    return z


def _dot_nn(a, b):
    return jnp.dot(a.astype(BF16), b.astype(BF16), preferred_element_type=F32)


def _dot_nt(a, b):
    return lax.dot_general(a.astype(BF16), b.astype(BF16), (((1,), (1,)), ((), ())), preferred_element_type=F32)


def _dot_tn(a, b):
    return lax.dot_general(a.astype(BF16), b.astype(BF16), (((0,), (0,)), ((), ())), preferred_element_type=F32)


def _gates(hq, hf, lbv):
    hq, hf = hq.astype(F32), hf.astype(F32)
    sq = _sig(hq)
    sf = _sig(hf)
    f = lbv + (1.0 - lbv) * sf
    fc = jnp.maximum(f, 1e-30)
    return hq * sq, sq, sf, f, fc, jnp.log(fc)


DECAY_CAP = 60.0


def _block_ref(c_ref, i, sb):
    if i == 0:
        return jnp.zeros((1, HD), F32)
    return c_ref[sb * i - 1:sb * i, :]


def _block_decay(c_ref, sb):
    spans = [_block_ref(c_ref, i, sb) - c_ref[sb * (i + 1) - 1:sb * (i + 1), :] for i in range(CH // sb)]
    return functools.reduce(jnp.maximum, spans)


def _pair_factors(q_ref, k, c_ref, first, cap, round_bf16, sb):
    nb = CH // sb
    c = c_ref[...]
    zero = jnp.zeros((sb, HD), F32)
    q_groups, k_groups, eqs, eks = [], [], [], []
    for i in range(first, nb):
        blk = slice(sb * i, sb * (i + 1))
        r_i = _block_ref(c_ref, i, sb)
        eq = jnp.exp(jnp.minimum(c_ref[blk, :] - r_i, 0.0))
        ek = jnp.exp(jnp.minimum(r_i - c, cap))
        qi, kei = q_ref[blk, :] * eq, k * ek
        if round_bf16:
            qi, kei = qi.astype(BF16).astype(F32), kei.astype(BF16).astype(F32)
        q_groups.append(jnp.concatenate([zero] * i + [qi] + [zero] * (nb - 1 - i), axis=0))
        k_groups.append(kei)
        eqs.append(eq)
        eks.append(ek)
    return jnp.concatenate(q_groups, axis=1), jnp.concatenate(k_groups, axis=1), eqs, eks


def _pair_mask(rowi, coli, strict, sb):
    return (coli < jnp.bitwise_and(rowi, -sb)) if strict else (coli <= rowi)


def _hgrn_fwd(proj, lb, gn, name, carry=None):
    S = proj.shape[0]
    nch = S // CH
    W = NH * HD

    def body(hq_ref, hf_ref, hi_ref, hg_ref, lb_ref, gn_ref, bin_ref, bint_ref, oraw_ref, st_ref, mild_ref,
             cum_ref, q_s, k_s, c_s, v_s, o_s, state_s, qf_s, kf_s, cf_s):
        state_s[...] = jnp.zeros((NH, HD, HD), F32)
        rowi = lax.broadcasted_iota(jnp.int32, (CH, 1), 0)
        coli = lax.broadcasted_iota(jnp.int32, (1, CH), 1)
        sbi = lax.broadcasted_iota(jnp.int32, (SB, 1), 0)
        gnv = gn_ref[...]

        def gates_pass(n, worst):
            wide, narrow = worst
            rows = pl.ds(pl.multiple_of(n * CH, CH), CH)
            for hh in range(NH):
                lanes = slice(hh * HD, (hh + 1) * HD)
                q, _, _, f, _, logf = _gates(hq_ref[rows, lanes], hf_ref[rows, lanes], lb_ref[:, lanes])
                c = _chunk_cumsum(logf, rowi)
                qf_s[hh, rows, :] = q
                kf_s[hh, rows, :] = 1.0 - f
                cf_s[hh, rows, :] = c
                cum_ref[rows, lanes] = c
                c_s[hh] = c
                wide = jnp.maximum(wide, _block_decay(c_s.at[hh], SB_WIDE))
                narrow = jnp.maximum(narrow, _block_decay(c_s.at[hh], SB))
            return wide, narrow

        def between_chunks(hh, n, rows):
            lanes = slice(hh * HD, (hh + 1) * HD)
            q = qf_s[hh, rows, :]
            k = kf_s[hh, rows, :]
            c = cf_s[hh, rows, :]
            v = hi_ref[rows, lanes].astype(F32)
            q_s[hh] = q
            k_s[hh] = k
            c_s[hh] = c
            v_s[hh] = v
            st = state_s[hh]
            st_ref[hh, n] = st.astype(BF16)
            o_s[hh] = _dot_nt(q * jnp.exp(c), st)
            last = c_s[hh, CH - 1:CH, :]
            state_s[hh] = st * jnp.exp(last) + _dot_tn(v, k * jnp.exp(last - c))

        def pairs_matmul(hh, first, cap, strict, sb):
            qx, kc, _, _ = _pair_factors(q_s.at[hh], k_s[hh], c_s.at[hh], first, cap, False, sb)
            a = jnp.where(_pair_mask(rowi, coli, strict, sb), _dot_nt(qx, kc), 0.0)
            o_s[hh] += _dot_nn(a, v_s[hh])

        def within_chunk_matmul(sb):
            return lambda hh: pairs_matmul(hh, 0, DECAY_CAP, False, sb)

        def within_chunk_exact(hh):
            pairs_matmul(hh, 1, 0.0, True, SB)
            for i in range(CH // SB):
                blk = slice(SB * i, SB * (i + 1))
                qb = q_s[hh, blk, :]
                cb = c_s[hh, blk, :]
                acc = jnp.zeros((SB, HD), F32)
                for s in range(SB):
                    row = SB * i + s
                    w = jnp.exp(jnp.minimum(cb - c_s[hh, row:row + 1, :], 0.0))
                    a_col = jnp.sum(qb * k_s[hh, row:row + 1, :] * w, axis=-1, keepdims=True)
                    acc = acc + jnp.where(sbi >= s, a_col, 0.0) * v_s[hh, row:row + 1, :]
                o_s[hh, blk, :] += acc

        def norm_and_gate(hh, rows):
            lanes = slice(hh * HD, (hh + 1) * HD)
            ov = o_s[hh]
            oraw_ref[rows, lanes] = ov
            r = lax.rsqrt(jnp.mean(ov * ov, axis=-1, keepdims=True) + EPS)
            hg = hg_ref[rows, lanes].astype(F32)
            bin_ref[rows, lanes] = ((ov * r) * gnv * (hg * _sig(hg))).astype(BF16)

        def chunk_with(within_chunk):
            def chunk(n, carry):
                rows = pl.ds(pl.multiple_of(n * CH, CH), CH)
                for hh in range(NH):
                    between_chunks(hh, n, rows)
                for hh in range(NH):
                    within_chunk(hh)
                for hh in range(NH):
                    norm_and_gate(hh, rows)
                return carry
            return chunk

        none = jnp.zeros((1, HD), F32)
        wide, narrow = lax.fori_loop(0, nch, gates_pass, (none, none))
        tier = jnp.where(jnp.max(wide) <= DECAY_CAP, 2.0, jnp.where(jnp.max(narrow) <= DECAY_CAP, 1.0, 0.0))
        mild_ref[...] = jnp.broadcast_to(tier, (8, HD))

        @pl.when(tier == 2.0)
        def _():
            lax.fori_loop(0, nch, chunk_with(within_chunk_matmul(SB_WIDE)), 0, unroll=4)

        @pl.when(tier == 1.0)
        def _():
            lax.fori_loop(0, nch, chunk_with(within_chunk_matmul(SB)), 0, unroll=2)

        @pl.when(tier == 0.0)
        def _():
            lax.fori_loop(0, nch, chunk_with(within_chunk_exact), 0)

        bint_ref[...] = bin_ref[...].astype(F32).T.astype(BF16)

    col = lambda off: pl.BlockSpec((S, W), lambda h: (0, off // NH + h))
    head = pl.BlockSpec((S, W), lambda h: (0, h))
    outs = _call(
        body, name=name, grid=(HEADS // NH,),
        in_specs=[col(HQ0), col(HF0), col(HI0), col(HG0), pl.BlockSpec((1, W), lambda h: (0, h)),
                  pl.BlockSpec((1, HD), lambda h: (0, 0))],
        out_specs=[head, pl.BlockSpec((W, S), lambda h: (h, 0)), head,
                   pl.BlockSpec((NH, nch, HD, HD), lambda h: (h, 0, 0, 0)),
                   pl.BlockSpec((8, HD), lambda h: (h, 0)), head],
        out_shape=[jax.ShapeDtypeStruct((S, D), BF16), jax.ShapeDtypeStruct((D, S), BF16),
                   jax.ShapeDtypeStruct((S, D), F32), jax.ShapeDtypeStruct((HEADS, nch, HD, HD), BF16),
                   jax.ShapeDtypeStruct((8 * HEADS // NH, HD), F32), jax.ShapeDtypeStruct((S, D), F32)],
        scratch_shapes=[pltpu.VMEM((NH, CH, HD), F32)] * 5 + [pltpu.VMEM((NH, HD, HD), F32)]
        + [pltpu.VMEM((NH, S, HD), F32)] * 3,
        sem=("parallel",), args=(proj, proj, proj, proj, lb, gn), carry=carry)
    return outs[:6], outs[6:]


def _hgrn_bwd(dbin, proj, oraw, states, mild, cum, lb, gn, dproj, name, carry=None):
    S = proj.shape[0]
    nch = S // CH
    W = NH * HD
    n_in = 12

    def body(*refs):
        ins, (dproj_ref, dlb_ref, dgn_ref) = refs[:n_in - 1], refs[n_in:n_in + 3]
        scratch, later = refs[n_in + 3:-3], refs[-3:]
        seg = pl.program_id(1)

        @pl.when(seg == 0)
        def _():
            heads(*ins, dproj_ref, *later, dlb_ref, dgn_ref, *scratch)

        for s, kept in enumerate(later):
            @pl.when(seg == s + 1)
            def _(kept=kept):
                dproj_ref[...] = kept[...]

    def heads(db_ref, hq_ref, hf_ref, hi_ref, hg_ref, or_ref, st_ref, mild_ref, cum_ref, lb_ref, gn_ref,
              dq_ref, df_ref, di_ref, dg_ref, dlb_ref, dgn_ref,
              q_s, k_s, c_s, v_s, do_s, dq_s, dk_s, dv_s, dc_s, dqd_s, dkd_s, f_s, sf_s, sq_s, dl_s, dst_s,
              dlb_s, dgn_s):
        dst_s[...] = jnp.zeros((NH, HD, HD), F32)
        dlb_s[...] = jnp.zeros((1, W), F32)
        dgn_s[...] = jnp.zeros((1, HD), F32)
        rowi = lax.broadcasted_iota(jnp.int32, (CH, 1), 0)
        coli = lax.broadcasted_iota(jnp.int32, (1, CH), 1)
        sbi = lax.broadcasted_iota(jnp.int32, (SB, 1), 0)
        gnv = gn_ref[...]
        def between_chunks(hh, n, rows):
            lanes = slice(hh * HD, (hh + 1) * HD)
            lbv = lb_ref[:, lanes]
            hq = hq_ref[rows, lanes].astype(F32)
            sq = _sig(hq)
            sf = _sig(hf_ref[rows, lanes].astype(F32))
            f = lbv + (1.0 - lbv) * sf
            q = hq * sq
            k = 1.0 - f
            f_s[hh] = f
            sf_s[hh] = sf
            sq_s[hh] = sq
            v = hi_ref[rows, lanes].astype(F32)
            c = cum_ref[rows, lanes]
            ov = or_ref[rows, lanes]
            hg = hg_ref[rows, lanes].astype(F32)
            sg = _sig(hg)
            r = lax.rsqrt(jnp.mean(ov * ov, axis=-1, keepdims=True) + EPS)
            dbv = db_ref[rows, lanes]
            d_on = dbv * (hg * sg)
            dg_ref[rows, lanes] = (dbv * ((ov * r) * gnv) * _dsilu(hg, sg)).astype(BF16)
            dgn_s[...] += jnp.sum(d_on * (ov * r), axis=0, keepdims=True)
            u = d_on * gnv
            do = r * u - ov * (r * r * r) * jnp.mean(u * ov, axis=-1, keepdims=True)
            q_s[hh] = q
            k_s[hh] = k
            c_s[hh] = c
            v_s[hh] = v
            do_s[hh] = do
            st = st_ref[hh, n].astype(F32)
            dst = dst_s[hh]
            ec = jnp.exp(c)
            last = c_s[hh, CH - 1:CH, :]
            el = jnp.exp(last - c)
            elast = jnp.exp(last)
            dq = _dot_nn(do, st) * ec
            dk = _dot_nn(v, dst) * el
            dq_s[hh] = dq
            dk_s[hh] = dk
            dv_s[hh] = _dot_nt(k * el, dst)
            dc_s[hh] = q * dq - k * dk
            dl_s[hh] = (jnp.sum(k * dk, axis=0, keepdims=True)
                        + elast * jnp.sum(st * dst, axis=0, keepdims=True))
            dst_s[hh] = dst * elast + _dot_tn(do, q * ec)

        def pairs_matmul(hh, first, cap, strict, sb):
            do = do_s[hh]
            qx, kc, eqs, eks = _pair_factors(q_s.at[hh], k_s[hh], c_s.at[hh], first, cap, True, sb)
            mask = _pair_mask(rowi, coli, strict, sb)
            a = jnp.where(mask, _dot_nt(qx, kc), 0.0)
            d_a = jnp.where(mask, _dot_nt(do, v_s[hh]).astype(BF16).astype(F32), 0.0)
            dqx = _dot_nn(d_a, kc)
            dkc = _dot_tn(d_a, qx)
            dv_s[hh] += _dot_tn(a, do)
            dk, dcum = dk_s[hh], dc_s[hh]
            dq_slabs = [jnp.zeros((sb, HD), F32)] * first
            dc_slabs = [jnp.zeros((sb, HD), F32)] * first
            for g, (eq, ek) in enumerate(zip(eqs, eks)):
                rows = slice(sb * (first + g), sb * (first + g + 1))
                cols = slice(HD * g, HD * (g + 1))
                dq_i = dqx[rows, cols]
                dk_i = dkc[:, cols]
                dq_slabs.append(dq_i * eq)
                dc_slabs.append(qx[rows, cols] * dq_i)
                dk = dk + dk_i * ek
                dcum = dcum - kc[:, cols] * dk_i
            dq_s[hh] += jnp.concatenate(dq_slabs, axis=0)
            dk_s[hh] = dk
            dc_s[hh] = dcum + jnp.concatenate(dc_slabs, axis=0)

        def pairs_exact(hh):
            dqd_s[hh] = jnp.zeros((CH, HD), F32)
            dkd_s[hh] = jnp.zeros((CH, HD), F32)
            for i in range(CH // SB):
                blk = slice(SB * i, SB * (i + 1))
                qb = q_s[hh, blk, :]
                cb = c_s[hh, blk, :]
                dob = do_s[hh, blk, :]
                dq_acc = jnp.zeros((SB, HD), F32)
                for s in range(SB):
                    row = SB * i + s
                    ks = k_s[hh, row:row + 1, :]
                    vs = v_s[hh, row:row + 1, :]
                    w = jnp.exp(jnp.minimum(cb - c_s[hh, row:row + 1, :], 0.0))
                    live = sbi >= s
                    a_col = jnp.where(live, jnp.sum(qb * ks * w, axis=-1, keepdims=True), 0.0)
                    da_col = jnp.where(live, jnp.sum(dob * vs, axis=-1, keepdims=True), 0.0)
                    dq_acc = dq_acc + da_col * ks * w
                    dkd_s[hh, row:row + 1, :] += jnp.sum(da_col * qb * w, axis=0, keepdims=True)
                    dv_s[hh, row:row + 1, :] += jnp.sum(a_col * dob, axis=0, keepdims=True)
                dqd_s[hh, blk, :] += dq_acc
            dq_d = dqd_s[hh]
            dk_d = dkd_s[hh]
            dq_s[hh] += dq_d
            dk_s[hh] += dk_d
            dc_s[hh] += q_s[hh] * dq_d - k_s[hh] * dk_d

        def gate_grads(hh, rows):
            lanes = slice(hh * HD, (hh + 1) * HD)
            lbv = lb_ref[:, lanes]
            hq = hq_ref[rows, lanes].astype(F32)
            f, sf, sq = f_s[hh], sf_s[hh], sq_s[hh]
            dlogf = _chunk_rev_cumsum(dc_s[hh], rowi) + dl_s[hh]
            dfv = jnp.where(f > 1e-30, dlogf / jnp.maximum(f, 1e-30), 0.0) - dk_s[hh]
            dlb_s[:, lanes] += jnp.sum(dfv * (1.0 - sf), axis=0, keepdims=True)
            df_ref[rows, lanes] = (dfv * (1.0 - lbv) * sf * (1.0 - sf)).astype(BF16)
            dq_ref[rows, lanes] = (dq_s[hh] * _dsilu(hq, sq)).astype(BF16)
            di_ref[rows, lanes] = dv_s[hh].astype(BF16)

        def chunk_with(pairs):
            def chunk(j, carry):
                n = nch - 1 - j
                rows = pl.ds(pl.multiple_of(n * CH, CH), CH)
                for hh in range(NH):
                    between_chunks(hh, n, rows)
                for hh in range(NH):
                    pairs(hh)
                for hh in range(NH):
                    gate_grads(hh, rows)
                return carry
            return chunk

        def pairs_mild(sb):
            return lambda hh: pairs_matmul(hh, 0, DECAY_CAP, False, sb)

        def pairs_any(hh):
            pairs_matmul(hh, 1, 0.0, True, SB)
            pairs_exact(hh)

        tier = jnp.max(mild_ref[...])

        @pl.when(tier == 2.0)
        def _():
            lax.fori_loop(0, nch, chunk_with(pairs_mild(SB_WIDE)), 0, unroll=2)

        @pl.when(tier == 1.0)
        def _():
            lax.fori_loop(0, nch, chunk_with(pairs_mild(SB)), 0)

        @pl.when(tier == 0.0)
        def _():
            lax.fori_loop(0, nch, chunk_with(pairs_any), 0)

        dlb_ref[...] = dlb_s[...]
        dgn_ref[...] = jnp.broadcast_to(dgn_s[...], (8, HD))

    ahead = lambda h, s: jnp.minimum(h + jnp.minimum(s, 1), HEADS // NH - 1)
    col = lambda off: pl.BlockSpec((S, W), lambda h, s: (0, off // NH + ahead(h, s)))
    head_in = pl.BlockSpec((S, W), lambda h, s: (0, ahead(h, s)))
    vec_in = pl.BlockSpec((1, W), lambda h, s: (0, ahead(h, s)))
    vec = pl.BlockSpec((1, W), lambda h, s: (0, h))
    seg_w = (HF0 - HQ0) // NH
    outs = _call(
        body, name=name, grid=(HEADS // NH, 4),
        in_specs=[head_in, col(HQ0), col(HF0), col(HI0), col(HG0), head_in,
                  pl.BlockSpec((NH, nch, HD, HD), lambda h, s: (ahead(h, s), 0, 0, 0)),
                  pl.BlockSpec((8, HD), lambda h, s: (ahead(h, s), 0)), head_in, vec_in,
                  pl.BlockSpec((1, HD), lambda h, s: (0, 0)), ANY],
        out_specs=[pl.BlockSpec((S, W), lambda h, s: (0, HQ0 // NH + seg_w * s + h)), vec,
                   pl.BlockSpec((8, HD), lambda h, s: (h, 0))],
        out_shape=[jax.ShapeDtypeStruct(dproj.shape, dproj.dtype), jax.ShapeDtypeStruct((1, D), F32),
                   jax.ShapeDtypeStruct((8 * HEADS // NH, HD), F32)],
        scratch_shapes=[pltpu.VMEM((NH, CH, HD), F32)] * 14
        + [pltpu.VMEM((NH, 1, HD), F32), pltpu.VMEM((NH, HD, HD), F32), pltpu.VMEM((1, W), F32),
           pltpu.VMEM((1, HD), F32)] + [pltpu.VMEM((S, W), BF16)] * 3,
        sem=("arbitrary", "arbitrary"), aliases={n_in - 1: 0},
        args=(dbin, proj, proj, proj, proj, oraw, states, mild, cum, lb, gn, dproj), carry=carry)
    dproj, dlb, dgn = outs[:3]
    return (dproj, dlb, dgn.reshape(HEADS // NH, 8, HD)[:, 0, :]), outs[3:]


def _lower_bounds(l0, l1):
    m = jnp.maximum(l0, l1)
    e0 = jnp.exp(l0 - m)
    e1 = jnp.exp(l1 - m)
    tot = e0 + e1
    p0 = e0 / tot
    p1 = e1 / tot
    return jnp.clip(p0 - p0, 0.0, 1.0), jnp.clip((p0 + p1) - p0, 0.0, 1.0)


def _lb_fwd(logits):
    def body(l_ref, o_ref):
        lb0, lb1 = _lower_bounds(l_ref[0:1, :], l_ref[1:2, :])
        o_ref[0:1, :] = lb0
        o_ref[1:2, :] = lb1

    return pl.pallas_call(body, name="lb_fwd", out_shape=jax.ShapeDtypeStruct((2, D), F32))(logits)


def _lb_bwd(logits, dlb):
    def body(l_ref, d_ref, o_ref):
        _, vjp = jax.vjp(_lower_bounds, l_ref[0:1, :], l_ref[1:2, :])
        g0, g1 = vjp((d_ref[0:1, :], d_ref[1:2, :]))
        o_ref[0:1, :] = g0
        o_ref[1:2, :] = g1

    return pl.pallas_call(body, name="lb_bwd", out_shape=jax.ShapeDtypeStruct((2, D), F32))(logits, dlb)


ADA_PAD = 128


def _ada_fwd(c_pad, w_ada, b_sh):
    ns = w_ada.shape[2]

    def body(c_ref, w_ref, b_ref, o_ref):
        cv = c_ref[...]
        ca = (cv * _sig(cv)).astype(BF16)
        for l in range(2):
            res = jnp.dot(ca, w_ref[l].astype(BF16), preferred_element_type=F32)
            o_ref[:, l * ns:(l + 1) * ns] = res[0:NDEV, :] + b_ref[l:l + 1, :]

    return pl.pallas_call(body, name="ada_fwd", out_shape=jax.ShapeDtypeStruct((NDEV, 2 * ns), F32),
                          compiler_params=_cp())(c_pad, w_ada, b_sh)


def _ada_wgrad(c_pad_t, d_ada_sh):
    ns = d_ada_sh.shape[2]

    def body(c_ref, d_ref, o_ref):
        cv = c_ref[...]
        ca = (cv * _sig(cv)).astype(BF16)
        for l in range(2):
            o_ref[l] = jnp.dot(ca, d_ref[l].astype(BF16), preferred_element_type=F32)

    return pl.pallas_call(body, name="ada_wgrad", out_shape=jax.ShapeDtypeStruct((2, D, ns), F32),
                          compiler_params=_cp())(c_pad_t, d_ada_sh)


def _sum_devices(g):
    _, R, C = g.shape

    def body(g_ref, o_ref):
        acc = g_ref[0]
        for d in range(1, NDEV):
            acc = acc + g_ref[d]
        o_ref[...] = acc

    return pl.pallas_call(body, name="sum_devices", out_shape=jax.ShapeDtypeStruct((R, C), F32),
                          compiler_params=_cp())(g)


def _adamw(w, g, m, v, name, echo=False):
    R, C = w.shape
    tr = _row_tile(R, max(8, (1 << 19) // C))

    def body(w_ref, g_ref, m_ref, v_ref, d_ref, nm_ref, nv_ref, *g_out):
        d_ref[...], nm_ref[...], nv_ref[...] = _adamw_update(w_ref[...], g_ref[...], m_ref[...], v_ref[...])
        for o_ref in g_out:
            o_ref[...] = g_ref[...]

    tile = pl.BlockSpec((tr, C), lambda i: (i, 0))
    n_out = 4 if echo else 3
    return _call(body, name=name, grid=(R // tr,), in_specs=[tile] * 4, out_specs=[tile] * n_out,
                 out_shape=[jax.ShapeDtypeStruct((R, C), F32)] * n_out, sem=("parallel",), args=(w, g, m, v))


def _adamw_many(wgmv, name, steps=4):
    n = len(wgmv)

    def body(*refs):
        ins, outs = refs[:4 * n], refs[4 * n:]
        for a in range(n):
            w_ref, g_ref, m_ref, v_ref = ins[4 * a:4 * a + 4]
            d_ref, nm_ref, nv_ref, g_out = outs[4 * a:4 * a + 4]
            d_ref[...], nm_ref[...], nv_ref[...] = _adamw_update(w_ref[...], g_ref[...], m_ref[...], v_ref[...])
            g_out[...] = g_ref[...]

    def tile(t):
        return pl.BlockSpec((t.shape[0] // steps, t.shape[1]), lambda i: (i, 0))

    flat = [t for four in wgmv for t in four]
    outs = pl.pallas_call(
        body, name=name, grid=(steps,), in_specs=[tile(t) for t in flat],
        out_specs=[tile(four[0]) for four in wgmv for _ in range(4)],
        out_shape=[jax.ShapeDtypeStruct(four[0].shape, F32) for four in wgmv for _ in range(4)],
        compiler_params=_cp(("parallel",)))(*flat)
    return [outs[4 * a:4 * a + 4] for a in range(n)]


def _adamw_update(w, g, m, v):
    nm = B1 * m + (1.0 - B1) * g
    nv = B2 * v + (1.0 - B2) * (g * g)
    m_hat = nm / (1.0 - B1 ** STEP)
    v_hat = nv / (1.0 - B2 ** STEP)
    return -LR * (m_hat / (jnp.sqrt(v_hat) + AEPS) + WD * w), nm, nv


SMALL_KEYS = ("b_ada", "g_pre", "g_post", "lb_logits", "pool_w", "pool_scale", "hgrn_norm_g")


def _small_pieces(key):
    one = lambda i: slice(i, i + 1)
    if key == "b_ada":
        return [((one(l), slice(j * D, (j + 1) * D)), (one(3 * l + j), slice(0, D)))
                for l in range(2) for j in range(3)]
    if key in ("g_pre", "g_post", "lb_logits"):
        row0 = {"g_pre": 8, "g_post": 16, "lb_logits": 24}[key]
        return [((slice(0, 2), slice(0, D)), (slice(row0, row0 + 2), slice(0, D)))]
    if key == "pool_w":
        return [((l, g, pl.ds(k, 16, stride=8), slice(0, 128)),
                 (slice(32 + 64 * l + 16 * g, 48 + 64 * l + 16 * g), slice(128 * k, 128 * (k + 1))))
                for l in range(2) for g in range(GROUPS) for k in range(8)]
    width = {"pool_scale": POOL_W, "hgrn_norm_g": HD}[key]
    row = {"pool_scale": 160, "hgrn_norm_g": 168}[key]
    return [((one(l), slice(0, width)), (one(row), slice(l * width, (l + 1) * width))) for l in range(2)]


def _adamw_small(g_small, g_lb_logits, wmv):
    n = len(SMALL_KEYS)

    def body(g_ref, glb_ref, *refs):
        ins, outs = refs[:3 * n], refs[3 * n:]
        for p, key in enumerate(SMALL_KEYS):
            w_ref, m_ref, v_ref = ins[3 * p:3 * p + 3]
            for at, (rows, lanes) in _small_pieces(key):
                gv = glb_ref[at] if key == "lb_logits" else g_ref[rows, lanes]
                res = _adamw_update(w_ref[at], gv, m_ref[at], v_ref[at])
                for o_ref, val in zip(outs[4 * p:4 * p + 4], (*res, gv)):
                    o_ref[at] = val

    flat = [t for key in SMALL_KEYS for t in wmv[key]]
    outs = pl.pallas_call(body, name="adamw_small",
                          out_shape=[jax.ShapeDtypeStruct(wmv[key][0].shape, F32) for key in SMALL_KEYS
                                     for _ in range(4)],
                          compiler_params=_cp())(g_small, g_lb_logits, *flat)
    return {key: outs[4 * p:4 * p + 4] for p, key in enumerate(SMALL_KEYS)}


def _cast_to_slot(place, w, l, name):
    _, R, C = w.shape
    tr = _row_tile(R, max(8, (1 << 19) // C))

    def body(p_ref, w_ref, o_ref):
        o_ref[...] = w_ref[...].astype(BF16)

    return pl.pallas_call(
        body, name=name, out_shape=jax.ShapeDtypeStruct((NCHIP, R, C), BF16),
        grid_spec=pltpu.PrefetchScalarGridSpec(
            num_scalar_prefetch=1, grid=(R // tr,),
            in_specs=[pl.BlockSpec((None, tr, C), lambda i, p_ref: (l, i, 0))],
            out_specs=pl.BlockSpec((None, tr, C), lambda i, p_ref: (p_ref[0], i, 0))),
        compiler_params=_cp(("parallel",)),
    )(place, w)


def _cast_to_slots(place, ws, name):
    n = len(ws)

    def body(p_ref, *refs):
        for w_ref, o_ref in zip(refs[:n], refs[n:]):
            o_ref[...] = w_ref[...].astype(BF16)

    def layer(l):
        return lambda i, p_ref: (l, 0, 0)

    return pl.pallas_call(
        body, name=name, out_shape=[jax.ShapeDtypeStruct((NCHIP,) + w.shape[1:], BF16) for w, _ in ws],
        grid_spec=pltpu.PrefetchScalarGridSpec(
            num_scalar_prefetch=1, grid=(1,),
            in_specs=[pl.BlockSpec((None,) + w.shape[1:], layer(l)) for w, l in ws],
            out_specs=[pl.BlockSpec((None,) + w.shape[1:], lambda i, p_ref: (p_ref[0], 0, 0)) for w, _ in ws]),
        compiler_params=_cp(("arbitrary",)),
    )(place, *[w for w, _ in ws])


def _pair_adds(core, gs, gots, name):
    n = len(gs)

    def body(c_ref, *refs):
        for a_ref, b_ref, o_ref in zip(refs[:n], refs[n:2 * n], refs[2 * n:]):
            o_ref[...] = (a_ref[...].astype(F32) + b_ref[...].astype(F32)).astype(o_ref.dtype)

    def whole(t):
        return pl.BlockSpec(t.shape, lambda i, c_ref: (0, 0, 0))

    return pl.pallas_call(
        body, name=name, out_shape=[jax.ShapeDtypeStruct(t.shape, BF16) for t in gots],
        grid_spec=pltpu.PrefetchScalarGridSpec(
            num_scalar_prefetch=1, grid=(1,),
            in_specs=[pl.BlockSpec(t.shape, lambda i, c_ref: (0, c_ref[0], 0)) for t in gots]
            + [whole(t) for t in gots],
            out_specs=[whole(t) for t in gots]),
        compiler_params=_cp(("arbitrary",)),
    )(core, *gs, *gots)


def _pair_add(core, g, got, name):
    _, R, C = g.shape
    r2 = R // 2
    tr = _row_tile(r2, max(8, (1 << 19) // C))
    nt = r2 // tr

    def body(c_ref, a_ref, b_ref, o_ref):
        o_ref[...] = (a_ref[...].astype(F32) + b_ref[...].astype(F32)).astype(o_ref.dtype)

    return pl.pallas_call(
        body, name=name, out_shape=jax.ShapeDtypeStruct((NCHIP, r2, C), BF16),
        grid_spec=pltpu.PrefetchScalarGridSpec(
            num_scalar_prefetch=1, grid=(NCHIP, nt),
            in_specs=[pl.BlockSpec((None, tr, C), lambda j, i, c_ref: (j, c_ref[0] * nt + i, 0)),
                      pl.BlockSpec((None, tr, C), lambda j, i, c_ref: (j, i, 0))],
            out_specs=pl.BlockSpec((None, tr, C), lambda j, i, c_ref: (j, i, 0))),
        compiler_params=_cp(("parallel", "parallel")),
    )(core, g, got)


def _sum_in_chip_order(me, own_ref, r_ref):
    own = own_ref[...].astype(F32)
    acc = None
    for j in range(NCHIP):
        slot = jnp.minimum(jnp.where(j > me, j - 1, j), NCHIP - 2)
        term = jnp.where(me == j, own, r_ref[slot].astype(F32))
        acc = term if acc is None else acc + term
    return acc


def _chip_sums(place, parts, recvs, name):
    n = len(parts)

    def body(p_ref, *refs):
        for a in range(n):
            for l in range(2):
                refs[4 * n + a][l] = _sum_in_chip_order(p_ref[0], refs[2 * a + l], refs[2 * n + 2 * a + l])

    def own(t):
        return pl.BlockSpec((None,) + t.shape[1:], lambda i, p_ref: (p_ref[0], 0, 0))

    def whole(t):
        return pl.BlockSpec(t.shape, lambda i, p_ref: (0, 0, 0))

    flat_p = [t for pair in parts for t in pair]
    flat_r = [t for pair in recvs for t in pair]
    return pl.pallas_call(
        body, name=name,
        out_shape=[jax.ShapeDtypeStruct((2, 2 * p[0].shape[1], p[0].shape[2]), F32) for p in parts],
        grid_spec=pltpu.PrefetchScalarGridSpec(
            num_scalar_prefetch=1, grid=(1,),
            in_specs=[own(t) for t in flat_p] + [whole(t) for t in flat_r],
            out_specs=[pl.BlockSpec((2,) + p[0].shape[1:], lambda i, p_ref: (0, p_ref[1], 0)) for p in parts]),
        compiler_params=_cp(("arbitrary",)),
    )(place, *flat_p, *flat_r)


def _chip_sum(place, part, recv, layer, both, name):
    _, r2, C = part.shape
    tr = _row_tile(r2, max(8, (1 << 18) // C))
    nt = r2 // tr

    def body(p_ref, own_ref, r_ref, *rest):
        rest[-1][...] = _sum_in_chip_order(p_ref[0], own_ref, r_ref)

    args = (place, part, recv) if both is None else (place, part, recv, both)
    return pl.pallas_call(
        body, name=name, out_shape=jax.ShapeDtypeStruct((2, 2 * r2, C), F32),
        grid_spec=pltpu.PrefetchScalarGridSpec(
            num_scalar_prefetch=1, grid=(nt,),
            in_specs=[pl.BlockSpec((None, tr, C), lambda i, p_ref: (p_ref[0], i, 0)),
                      pl.BlockSpec((NCHIP - 1, tr, C), lambda i, p_ref: (0, i, 0))] + [ANY] * (len(args) - 3),
            out_specs=pl.BlockSpec((None, tr, C), lambda i, p_ref: (layer, p_ref[1] * nt + i, 0))),
        input_output_aliases={} if both is None else {3: 0},
        compiler_params=_cp(("parallel",)),
    )(*args)


def _place():
    x, y, c = lax.axis_index("x"), lax.axis_index("y"), lax.axis_index("c")
    chips = [(1 - x, y), (x, 1 - y), (1 - x, 1 - y)]
    return x, y, c, chips


def _gather_small(blk, name):
    m_per, n = blk.shape

    def body(x_ref, out_ref, send_sems, recv_sems, local_sem):
        x, y, c, chips = _place()
        me, sibling = (x, y, c), (x, y, 1 - c)

        def rows(px, py, pc):
            return out_ref.at[pl.ds((4 * px + 2 * py + pc) * m_per, m_per), :]

        def copy(k, block, to, src=None):
            return pltpu.make_async_remote_copy(
                src_ref=rows(*block) if src is None else src, dst_ref=rows(*block),
                send_sem=send_sems.at[k], recv_sem=recv_sems.at[k], device_id=to, device_id_type=MESH)

        mine = pltpu.make_async_copy(x_ref, rows(*me), local_sem)
        mine.start()
        first = [copy(0, me, sibling, src=x_ref)]
        first += [copy(1 + j, me, (*chip, c), src=x_ref) for j, chip in enumerate(chips)]
        for cp in first:
            cp.start()
        passed = [copy(4 + j, (*chip, c), sibling) for j, chip in enumerate(chips)]
        for j, chip in enumerate(chips):
            copy(1 + j, (*chip, c), me).wait_recv()
            passed[j].start()
        copy(0, sibling, me).wait_recv()
        for j, chip in enumerate(chips):
            copy(4 + j, (*chip, 1 - c), me).wait_recv()
        for cp in first + passed:
            cp.wait_send()
        mine.wait()

    return pl.pallas_call(
        body, name=name, out_shape=jax.ShapeDtypeStruct((NDEV * m_per, n), blk.dtype),
        in_specs=[pl.BlockSpec(memory_space=pltpu.VMEM)], out_specs=pl.BlockSpec(memory_space=pltpu.VMEM),
        scratch_shapes=[pltpu.SemaphoreType.DMA((7,)), pltpu.SemaphoreType.DMA((7,)), pltpu.SemaphoreType.DMA],
        compiler_params=_cp(),
    )(blk)


def _gather_rows_carry(blk):
    m_per, n = blk.shape

    def rows(ref, px, py, pc):
        return ref.at[pl.ds((4 * px + 2 * py + pc) * m_per, m_per), :]

    def copy(ins, outs, send_sems, recv_sems, k, block, to, own=False):
        return pltpu.make_async_remote_copy(
            src_ref=ins[0] if own else rows(outs[0], *block), dst_ref=rows(outs[0], *block),
            send_sem=send_sems.at[k], recv_sem=recv_sems.at[k], device_id=to, device_id_type=MESH)

    def mine(ins, outs, send_sems):
        x, y, c, _ = _place()
        return pltpu.make_async_copy(ins[0], rows(outs[0], x, y, c), send_sems.at[7])

    def start(ins, outs, send_sems, recv_sems):
        x, y, c, chips = _place()
        mine(ins, outs, send_sems).start()
        copy(ins, outs, send_sems, recv_sems, 0, (x, y, c), (x, y, 1 - c), own=True).start()
        for j, chip in enumerate(chips):
            copy(ins, outs, send_sems, recv_sems, 1 + j, (x, y, c), (*chip, c), own=True).start()

    def finish(ins, outs, send_sems, recv_sems):
        x, y, c, chips = _place()
        for j, chip in enumerate(chips):
            copy(ins, outs, send_sems, recv_sems, 1 + j, (*chip, c), (x, y, c)).wait_recv()
            copy(ins, outs, send_sems, recv_sems, 4 + j, (*chip, c), (x, y, 1 - c)).start()
        copy(ins, outs, send_sems, recv_sems, 0, (x, y, 1 - c), (x, y, c)).wait_recv()
        for j, chip in enumerate(chips):
            copy(ins, outs, send_sems, recv_sems, 4 + j, (*chip, 1 - c), (x, y, c)).wait_recv()
        copy(ins, outs, send_sems, recv_sems, 0, (x, y, c), (x, y, 1 - c), own=True).wait_send()
        for j, chip in enumerate(chips):
            copy(ins, outs, send_sems, recv_sems, 1 + j, (x, y, c), (*chip, c), own=True).wait_send()
            copy(ins, outs, send_sems, recv_sems, 4 + j, (*chip, c), (x, y, 1 - c)).wait_send()
        mine(ins, outs, send_sems).wait()

    return _Carry([blk], [jax.ShapeDtypeStruct((NDEV * m_per, n), blk.dtype)], {}, 8, start, finish)


def _gather_carry(shards, piece=(0, 1, 1)):
    n = len(shards)
    first, count, of = piece

    def rows(ref, half):
        r2 = ref.shape[1] // 2
        return pl.ds(half * r2 + first * (r2 // of), count * (r2 // of))

    def over_ici(outs, send_sems, recv_sems, a, j, chip_xy, slot):
        x, y, c, _ = _place()
        blk = outs[a].at[slot, rows(outs[a], c), :]
        return pltpu.make_async_remote_copy(
            src_ref=blk, dst_ref=blk, send_sem=send_sems.at[6 * a + j], recv_sem=recv_sems.at[6 * a + j],
            device_id=(*chip_xy, c), device_id_type=MESH)

    def over_d2d(outs, send_sems, recv_sems, a, j, slot, half):
        x, y, c, _ = _place()
        blk = outs[a].at[slot, rows(outs[a], half), :]
        return pltpu.make_async_remote_copy(
            src_ref=blk, dst_ref=blk, send_sem=send_sems.at[6 * a + 3 + j], recv_sem=recv_sems.at[6 * a + 3 + j],
            device_id=(x, y, 1 - c), device_id_type=MESH)

    def start(ins, outs, send_sems, recv_sems):
        x, y, c, chips = _place()
        for a in range(n):
            for j, chip_xy in enumerate(chips):
                over_ici(outs, send_sems, recv_sems, a, j, chip_xy, 2 * x + y).start()

    def finish(ins, outs, send_sems, recv_sems):
        x, y, c, chips = _place()
        for a in range(n):
            for j, (cx, cy) in enumerate(chips):
                over_ici(outs, send_sems, recv_sems, a, j, (cx, cy), 2 * cx + cy).wait_recv()
                over_d2d(outs, send_sems, recv_sems, a, j, 2 * cx + cy, c).start()
        for a in range(n):
            for j, (cx, cy) in enumerate(chips):
                over_d2d(outs, send_sems, recv_sems, a, j, 2 * cx + cy, 1 - c).wait_recv()
        for a in range(n):
            for j, (cx, cy) in enumerate(chips):
                over_ici(outs, send_sems, recv_sems, a, j, (cx, cy), 2 * x + y).wait_send()
                over_d2d(outs, send_sems, recv_sems, a, j, 2 * cx + cy, c).wait_send()

    return _Carry(shards, [jax.ShapeDtypeStruct(s.shape, s.dtype) for s in shards],
                  {a: a for a in range(n)}, 6 * n, start, finish)


def _rs_pair(grads, name):
    n = len(grads)

    def body(*refs):
        ins, gots = refs[:n], refs[n:2 * n]
        send_sems, recv_sems = refs[2 * n:]
        x, y, c, _ = _place()
        cps = []
        for a in range(n):
            r2 = ins[a].shape[1] // 2
            cp = pltpu.make_async_remote_copy(
                src_ref=ins[a].at[:, pl.ds((1 - c) * r2, r2), :], dst_ref=gots[a],
                send_sem=send_sems.at[a], recv_sem=recv_sems.at[a],
                device_id=(x, y, 1 - c), device_id_type=MESH)
            cp.start()
            cps.append(cp)
        for cp in cps:
            cp.wait()

    half = [jax.ShapeDtypeStruct((NCHIP, g.shape[1] // 2, g.shape[2]), g.dtype) for g in grads]
    return pl.pallas_call(
        body, name=name, out_shape=half, in_specs=[ANY] * n, out_specs=[ANY] * n,
        scratch_shapes=[pltpu.SemaphoreType.DMA((n,)), pltpu.SemaphoreType.DMA((n,))],
        compiler_params=_cp(),
    )(*grads)


def _pair_carry(grads):
    n = len(grads)

    def copy(ins, outs, send_sems, recv_sems, a):
        x, y, c, _ = _place()
        r2 = ins[a].shape[1] // 2
        return pltpu.make_async_remote_copy(
            src_ref=ins[a].at[:, pl.ds((1 - c) * r2, r2), :], dst_ref=outs[a],
            send_sem=send_sems.at[a], recv_sem=recv_sems.at[a],
            device_id=(x, y, 1 - c), device_id_type=MESH)

    def start(ins, outs, send_sems, recv_sems):
        for a in range(n):
            copy(ins, outs, send_sems, recv_sems, a).start()

    def finish(ins, outs, send_sems, recv_sems):
        for a in range(n):
            copy(ins, outs, send_sems, recv_sems, a).wait()

    half = [jax.ShapeDtypeStruct((NCHIP, g.shape[1] // 2, g.shape[2]), g.dtype) for g in grads]
    return _Carry(grads, half, {}, n, start, finish)


def _chips_carry(parts, piece=(0, 1, 1), into=None):
    n = len(parts)
    first, count, of = piece

    def rows(ref):
        step = ref.shape[1] // of
        return pl.ds(first * step, count * step)

    def send(ins, outs, send_sems, recv_sems, a, j, chip_xy):
        x, y, c, _ = _place()
        me, them = 2 * x + y, 2 * chip_xy[0] + chip_xy[1]
        return pltpu.make_async_remote_copy(
            src_ref=ins[a].at[them, rows(ins[a]), :],
            dst_ref=outs[a].at[me - (me > them).astype(jnp.int32), rows(outs[a]), :],
            send_sem=send_sems.at[3 * a + j], recv_sem=recv_sems.at[3 * a + j],
            device_id=(*chip_xy, c), device_id_type=MESH)

    def start(ins, outs, send_sems, recv_sems):
        _, _, _, chips = _place()
        for a in range(n):
            for j, chip_xy in enumerate(chips):
                send(ins, outs, send_sems, recv_sems, a, j, chip_xy).start()

    def finish(ins, outs, send_sems, recv_sems):
        x, y, c, chips = _place()
        me = 2 * x + y
        for a in range(n):
            for j, (cx, cy) in enumerate(chips):
                them = 2 * cx + cy
                blk = outs[a].at[them - (them > me).astype(jnp.int32), rows(outs[a]), :]
                pltpu.make_async_remote_copy(
                    src_ref=blk, dst_ref=blk, send_sem=send_sems.at[3 * a + j], recv_sem=recv_sems.at[3 * a + j],
                    device_id=(cx, cy, c), device_id_type=MESH).wait_recv()
        for a in range(n):
            for j, chip_xy in enumerate(chips):
                send(ins, outs, send_sems, recv_sems, a, j, chip_xy).wait_send()

    landing = [jax.ShapeDtypeStruct((NCHIP - 1,) + p.shape[1:], p.dtype) for p in parts]
    if into is None:
        return _Carry(parts, landing, {}, 3 * n, start, finish)
    return _Carry(list(parts) + list(into), landing, {n + a: a for a in range(n)}, 3 * n, start, finish)


def _swap_carry(fulls, layers):
    n = len(fulls)

    def copy(outs, send_sems, recv_sems, a, half):
        x, y, c, _ = _place()
        r2 = outs[a].shape[1] // 2
        blk = outs[a].at[pl.ds(*layers[a]), pl.ds(half * r2, r2), :]
        return pltpu.make_async_remote_copy(
            src_ref=blk, dst_ref=blk, send_sem=send_sems.at[a], recv_sem=recv_sems.at[a],
            device_id=(x, y, 1 - c), device_id_type=MESH)

    def start(ins, outs, send_sems, recv_sems):
        c = lax.axis_index("c")
        for a in range(n):
            copy(outs, send_sems, recv_sems, a, c).start()

    def finish(ins, outs, send_sems, recv_sems):
        c = lax.axis_index("c")
        for a in range(n):
            copy(outs, send_sems, recv_sems, a, 1 - c).wait_recv()
        for a in range(n):
            copy(outs, send_sems, recv_sems, a, c).wait_send()

    return _Carry(fulls, [jax.ShapeDtypeStruct(f.shape, f.dtype) for f in fulls], {a: a for a in range(n)}, n,
                  start, finish)


def _rs_swap(fulls, layers):
    n = len(fulls)
    swap = _swap_carry(fulls, layers)

    def body(*refs):
        outs, (send_sems, recv_sems) = refs[n:2 * n], refs[2 * n:]
        swap.start(None, outs, send_sems, recv_sems)
        swap.finish(None, outs, send_sems, recv_sems)

    return pl.pallas_call(
        body, name="rs_swap", out_shape=swap.outs, in_specs=[ANY] * n, out_specs=[ANY] * n,
        input_output_aliases=swap.aliases,
        scratch_shapes=[pltpu.SemaphoreType.DMA((n,)), pltpu.SemaphoreType.DMA((n,))],
        compiler_params=_cp(),
    )(*fulls)


def _tail_weight_grads(merged_t, b_in_t, a_in_t, dy, dbr_b, dbr_a, name, tn=256):
    S = dy.shape[0]
    nn = D // tn

    def body(mt_ref, bt_ref, at_ref, dy_ref, db_ref, da_ref, go_ref, gh_ref, gp_ref):
        go_ref[...] = jnp.dot(mt_ref[...], dy_ref[...], preferred_element_type=F32).astype(BF16)
        gh_ref[...] = jnp.dot(bt_ref[...], db_ref[...], preferred_element_type=F32).astype(BF16)
        gp_ref[...] = jnp.dot(at_ref[...], da_ref[...], preferred_element_type=F32).astype(BF16)

    left = lambda rows: pl.BlockSpec((rows, S), lambda n: (0, 0))
    right = pl.BlockSpec((S, tn), lambda n: (0, n))
    out = pl.BlockSpec((D, tn), lambda n: (0, n))
    return pl.pallas_call(
        body, name=name, grid=(nn,), in_specs=[left(D), left(D), left(POOL_W), right, right, right],
        out_specs=[out, out, pl.BlockSpec((None, POOL_W, tn), lambda n: (n, 0, 0))],
        out_shape=[jax.ShapeDtypeStruct((D, D), BF16), jax.ShapeDtypeStruct((D, D), BF16),
                   jax.ShapeDtypeStruct((NCHIP, POOL_W, D // NCHIP), BF16)],
        compiler_params=_cp(("parallel",)),
    )(merged_t, b_in_t, a_in_t, dy, dbr_b, dbr_a)


class _GatherInProj:
    def __init__(self, slot, order):
        self.slot, self.order = slot, order


def _proj_with_gather(h, w_slot, order, name, tn=256):
    S, K = h.shape
    nsh, _, ns = w_slot.shape
    tps = ns // tn
    nt = nsh * tps
    r2 = K // 2

    def body(ord_ref, h_ref, w_in_ref, o_ref, w_ref, wbuf, tile_sems, send_sems, recv_sems):
        n = pl.program_id(0)
        x, y, c, chips = _place()

        def half(slot, which):
            return w_ref.at[slot, pl.ds(which * r2, r2), :]

        def over_ici(j, slot):
            blk = half(slot, c)
            return pltpu.make_async_remote_copy(src_ref=blk, dst_ref=blk, send_sem=send_sems.at[j],
                                                recv_sem=recv_sems.at[j], device_id=(*chips[j], c),
                                                device_id_type=MESH)

        def over_d2d(j, which):
            blk = half(2 * chips[j][0] + chips[j][1], which)
            return pltpu.make_async_remote_copy(src_ref=blk, dst_ref=blk, send_sem=send_sems.at[3 + j],
                                                recv_sem=recv_sems.at[3 + j], device_id=(x, y, 1 - c),
                                                device_id_type=MESH)

        def tile_copy(step, slot):
            shard = ord_ref[step // tps]
            return pltpu.make_async_copy(w_ref.at[shard, :, pl.ds((step % tps) * tn, tn)], wbuf.at[slot],
                                         tile_sems.at[slot])

        @pl.when(n == 0)
        def _():
            for j in range(3):
                over_ici(j, 2 * x + y).start()
            tile_copy(0, 0).start()

        for j in range(3):
            @pl.when(n == (j + 1) * tps - 1)
            def _(j=j):
                over_ici(j, 2 * chips[j][0] + chips[j][1]).wait_recv()
                over_d2d(j, c).start()
                over_d2d(j, 1 - c).wait_recv()

        @pl.when(n + 1 < nt)
        def _():
            tile_copy(n + 1, (n + 1) % 2).start()

        tile_copy(n, n % 2).wait()
        o_ref[...] = jnp.dot(h_ref[...], wbuf[n % 2], preferred_element_type=F32).astype(o_ref.dtype)

        @pl.when(n == nt - 1)
        def _():
            for j in range(3):
                over_ici(j, 2 * x + y).wait_send()
                over_d2d(j, c).wait_send()

    return pl.pallas_call(
        body, name=name,
        out_shape=[jax.ShapeDtypeStruct((S, nsh * ns), BF16), jax.ShapeDtypeStruct(w_slot.shape, w_slot.dtype)],
        grid_spec=pltpu.PrefetchScalarGridSpec(
            num_scalar_prefetch=1, grid=(nt,),
            in_specs=[pl.BlockSpec((S, K), lambda n, o_ref: (0, 0)), ANY],
            out_specs=[pl.BlockSpec((S, tn), lambda n, o_ref: (0, o_ref[n // tps] * tps + n % tps)), ANY],
            scratch_shapes=[pltpu.VMEM((2, K, tn), w_slot.dtype), pltpu.SemaphoreType.DMA((2,)),
                            pltpu.SemaphoreType.DMA((6,)), pltpu.SemaphoreType.DMA((6,))]),
        input_output_aliases={2: 1},
        compiler_params=_cp(("arbitrary",)),
    )(order, h, w_slot)


def _mm_ride(a, b, carry, **kw):
    if carry is None:
        return _mm(a, b, **kw), []
    return _mm(a, b, carry=carry, **kw)


def _layer_fwd(l, x, ada, w, small, ride, target=None):
    shift, scale, gate = ada[:, 0:D], ada[:, D:2 * D], ada[:, 2 * D:3 * D]
    carry, landed = ride("prenorm")
    (h, h_t), outs = _prenorm_fwd(x, small["g_pre"][l], scale, shift, f"prenorm_fwd{l}", carry)
    landed(outs)
    carry, landed = ride("proj")
    if isinstance(carry, _GatherInProj):
        proj, full = _proj_with_gather(h, carry.slot, carry.order, f"proj{l}")
        outs = [full]
    else:
        proj, outs = _mm_ride(h, w["w_in"][l], carry, name=f"proj{l}", b_mode="nn_sh", tm=2048, out_dtype=BF16)
    landed(outs)
    a_in, a_in_t = _pool_fwd(proj, small["pool_w"][l], small["pool_scale"][l], f"pool_fwd{l}")
    carry, landed = ride("hgrn")
    (b_in, b_in_t, o_raw, states, mild, cum), outs = _hgrn_fwd(proj, small["lb"][l], small["hgrn_norm_g"][l],
                                                              f"hgrn_fwd{l}", carry=carry)
    landed(outs)
    carry, landed = ride("tail")
    (br_a, br_b, merged_t, y, *x_new), outs = _layer_tail_fwd(
        proj, a_in, b_in, x, w["w_pool_o"][l], w["w_hgrn_o"][l].reshape(D, D), w["w_out"][l].reshape(D, D),
        gate, small["g_post"][l], f"tail_fwd{l}", target=target, carry=carry)
    landed(outs)
    saved = dict(x=x, h_t=h_t, proj=proj, a_in_t=a_in_t, b_in_t=b_in_t, o_raw=o_raw, states=states, mild=mild,
                 cum=cum,
                 br_a=br_a, br_b=br_b, merged_t=merged_t, y=y, scale=scale, gate=gate)
    return x_new, saved


def _layer_bwd(l, dxn, sv, w, small, ride):
    carry, landed = ride["head"](None)
    (dy, dbr_a, dbr_b, dproj, da_in, db_in, dgate, dg_post), outs = _layer_head_bwd(
        dxn, sv["y"], sv["proj"], sv["br_a"], sv["br_b"], w["w_pool_o"][l], w["w_hgrn_o"][l].reshape(D, D),
        w["w_out"][l].reshape(D, D), sv["gate"], small["g_post"][l], f"head_bwd{l}", carry)
    landed(outs)
    gw_out, gw_hgrn_o, gw_pool_o = _tail_weight_grads(sv["merged_t"], sv["b_in_t"], sv["a_in_t"], dy, dbr_b,
                                                      dbr_a, f"gw_tail{l}")
    big = dict(w_pool_o=gw_pool_o, w_hgrn_o=gw_hgrn_o.reshape(NCHIP, D // NCHIP, D),
               w_out=gw_out.reshape(NCHIP, D // NCHIP, D))
    carry, landed = ride["hgrn"](big)
    (dproj, dlb, dgn), outs = _hgrn_bwd(db_in, sv["proj"], sv["o_raw"], sv["states"], sv["mild"], sv["cum"],
                                        small["lb"][l], small["hgrn_norm_g"][l], dproj, f"hgrn_bwd{l}",
                                        carry=carry)
    landed(outs)
    dproj, dpw, dpsc = _pool_bwd(da_in, sv["proj"], small["pool_w"][l], small["pool_scale"][l], dproj,
                                 f"pool_bwd{l}")
    little = dict(dgate=dgate, g_post=dg_post, pool_w=dpw, pool_scale=dpsc, lb=dlb,
                  hgrn_norm_g=jnp.sum(dgn, axis=0, keepdims=True))
    carry, landed = ride["gw_in"](little)
    big["w_in"], outs = _mm_ride(sv["h_t"], dproj, carry, name=f"gw_in{l}", out_shards=NCHIP, out_dtype=BF16)
    landed(outs)
    carry, landed = ride["d_h"](big)
    dh, outs = _mm_ride(dproj, w["w_in"][l], carry, name=f"d_h{l}", b_mode="nt_shk", tn=1024)
    landed(outs)
    carry, landed = ride["prenorm"](big)
    (dx, dshift, dscale, dg_pre), outs = _prenorm_bwd(dh, dxn, sv["x"], small["g_pre"][l], sv["scale"],
                                                      f"prenorm_bwd{l}", carry)
    landed(outs)
    little.update(dshift=dshift, dscale=dscale, g_pre=dg_pre)
    return dx, big, little


SMALL_ROWS = 176


def _rows8(t):
    t = t.reshape(-1, D)
    return jnp.pad(t, ((0, -t.shape[0] % 8), (0, 0)))


def _pack_small(parts):
    row_keys = ("dshift", "dscale", "dgate", "g_pre", "g_post", "lb", "pool_scale", "hgrn_norm_g")
    flat = [p[k] for p in parts for k in row_keys] + [p["pool_w"] for p in parts]
    nk = len(row_keys)

    def body(*refs):
        o_ref = refs[-1]
        o_ref[...] = jnp.zeros((SMALL_ROWS, D), F32)
        for l in range(2):
            dshift, dscale, dgate, g_pre, g_post, lb, pscale, gn = refs[l * nk:(l + 1) * nk]
            for r, ref in enumerate((dshift, dscale, dgate)):
                o_ref[3 * l + r:3 * l + r + 1, :] = ref[...]
            o_ref[8 + l:9 + l, :] = g_pre[...]
            o_ref[16 + l:17 + l, :] = g_post[...]
            o_ref[24 + l:25 + l, :] = lb[...]
            o_ref[160:161, l * POOL_W:(l + 1) * POOL_W] = pscale[...]
            o_ref[168:169, l * HD:(l + 1) * HD] = gn[...]
        for (l, *at), (rows, lanes) in _small_pieces("pool_w"):
            o_ref[rows, lanes] = refs[2 * nk + l][tuple(at)]

    return pl.pallas_call(body, name="pack_small", out_shape=jax.ShapeDtypeStruct((SMALL_ROWS, D), F32),
                          compiler_params=_cp())(*flat)


def kernel(x, c, w_ada, b_ada, g_pre, g_post, w_in, pool_w, pool_scale, lb_logits, hgrn_norm_g, w_pool_o, w_hgrn_o, w_out, loss_target, m_w_ada, m_b_ada, m_g_pre, m_g_post, m_w_in, m_pool_w, m_pool_scale, m_lb_logits, m_hgrn_norm_g, m_w_pool_o, m_w_hgrn_o, m_w_out, v_w_ada, v_b_ada, v_g_pre, v_g_post, v_w_in, v_pool_w, v_pool_scale, v_lb_logits, v_hgrn_norm_g, v_w_pool_o, v_w_hgrn_o, v_w_out):
    ax, ay, ac = lax.axis_index("x"), lax.axis_index("y"), lax.axis_index("c")
    chip = 2 * ax + ay
    dev = 2 * chip + ac
    xe, te = x[0], loss_target[0]
    ada_s = w_ada.shape[2]

    big_names = ("w_in", "w_pool_o", "w_hgrn_o", "w_out")
    big_w = (w_in, w_pool_o, w_hgrn_o, w_out)
    core = jnp.stack([ac]).astype(jnp.int32)
    place = jnp.stack([chip, ac]).astype(jnp.int32)
    slots = {("w_in", l): _cast_to_slot(place, w_in, l, f"cast_w_in{l}") for l in range(2)}
    rest = [(k, l) for l in range(2) for k in big_names[1:]]
    slots.update(zip(rest, _cast_to_slots(place, [(dict(zip(big_names, big_w))[k], l) for k, l in rest],
                                          "cast_rest")))
    w = {k: [None, None] for k in big_names}
    def fills(keys):
        def landed(outs):
            for (k, l), o in zip(keys, outs):
                w[k][l] = slots[k, l] = o
        return landed

    rest0 = [(k, 0) for k in big_names[1:]]
    rest1 = [(k, 1) for k in big_names[1:]]
    no_carry = (None, lambda outs: None)
    order = jnp.stack([chip, 2 * (1 - ax) + ay, 2 * ax + (1 - ay), 2 * (1 - ax) + (1 - ay)]).astype(jnp.int32)

    def ride_fwd0(stage):
        if stage == "proj":
            return _GatherInProj(slots["w_in", 0], order), fills([("w_in", 0)])
        if stage == "hgrn":
            return (_join_carries(_gather_carry([slots[t] for t in rest0]),
                                  _gather_carry([slots["w_in", 1]], piece=(0, 2, 4))),
                    fills(rest0 + [("w_in", 1)]))
        if stage == "tail":
            return _gather_carry([slots["w_in", 1]], piece=(2, 1, 4)), fills([("w_in", 1)])
        return no_carry

    def ride_fwd1(stage):
        if stage == "prenorm":
            return _gather_carry([slots["w_in", 1]], piece=(3, 1, 4)), fills([("w_in", 1)])
        if stage == "hgrn":
            return _gather_carry([slots[t] for t in rest1]), fills(rest1)
        return no_carry

    c_all = _gather_small(jnp.broadcast_to(c, (8, D)), "gather_c").reshape(NDEV, 8, D)[:, 0, :]
    c_pad = jnp.pad(c_all, ((0, ADA_PAD - NDEV), (0, 0)))
    b_sh = lax.dynamic_slice(b_ada, (0, chip * ada_s), (2, ada_s))
    ada_cols = _gather_small(_ada_fwd(c_pad, w_ada, b_sh), "gather_ada")
    ada_cols = ada_cols.reshape(NCHIP, 2, NDEV, 2, ada_s)[:, 0]
    ada_all = jnp.transpose(ada_cols, (2, 1, 0, 3)).reshape(2, NDEV, 3 * D)
    ada_me = lax.dynamic_slice(ada_all, (0, dev, 0), (2, 1, 3 * D))

    lbs = _lb_fwd(lb_logits)
    small = dict(g_pre=g_pre[:, None, :], g_post=g_post[:, None, :], pool_w=pool_w,
                 pool_scale=pool_scale[:, None, :], lb=lbs[:, None, :], hgrn_norm_g=hgrn_norm_g[:, None, :])

    (x1,), sv0 = _layer_fwd(0, xe, ada_me[0], w, small, ride_fwd0)
    (dx2, loss_blk), sv1 = _layer_fwd(1, x1, ada_me[1], w, small, ride_fwd1, target=te)

    parts, recv, held = {}, {}, {}

    def pair_ride(keys, grads):
        def landed(outs):
            held.update({kl: (g, o) for kl, g, o in zip(keys, grads, outs)})
        return _pair_carry(grads), landed

    def pair_adds(keys):
        gs, gots = zip(*[held.pop(kl) for kl in keys])
        if keys[0][0] == "w_in":
            parts[keys[0]] = _pair_add(core, gs[0], gots[0], f"rs_add_w_in{keys[0][1]}")
        else:
            parts.update(zip(keys, _pair_adds(core, gs, gots, f"rs_add_early{keys[0][1]}")))

    def exchange(keys):
        def landed(outs):
            recv.update(zip(keys, outs))
        return _chips_carry([parts[kl] for kl in keys]), landed

    def share(key, first, count):
        def landed(outs):
            (recv[key],) = outs
        into = [recv[key]] if key in recv else None
        return _chips_carry([parts[key]], piece=(first, count, 8), into=into), landed

    def together(*rides):
        carries, fns = zip(*rides)

        def landed(outs):
            for cr, fn in zip(carries, fns):
                fn(outs[:len(cr.outs)])
                outs = outs[len(cr.outs):]
        return _join_carries(*carries), landed

    def early(l):
        return [(k, l) for k in big_names[1:]]

    def pair_alone(keys, grads, tag):
        held.update({kl: (g, o) for kl, g, o in zip(keys, grads, _rs_pair(grads, f"rs_pair_{tag}"))})
        pair_adds(keys)

    def ride_hgrn1(big):
        return pair_ride(early(1), [big[k] for k in big_names[1:]])

    def ride_gw_in1(_):
        pair_adds(early(1))
        return exchange(early(1))

    def ride_d_h1(big):
        return pair_ride([("w_in", 1)], [big["w_in"]])

    def ride_prenorm1(_):
        pair_adds([("w_in", 1)])
        return share(("w_in", 1), 0, 1)

    def ride_head0(_):
        return share(("w_in", 1), 1, 3)

    def ride_hgrn0(big):
        pair_alone(early(0), [big[k] for k in big_names[1:]], "early0")
        return together(exchange(early(0)), share(("w_in", 1), 4, 4))

    def ride_d_h0(big):
        pair_alone([("w_in", 0)], [big["w_in"]], "w_in0")
        return share(("w_in", 0), 0, 4)

    def ride_prenorm0(_):
        return share(("w_in", 0), 4, 4)

    no_ride = lambda so_far: no_carry
    dx1, big1, little1 = _layer_bwd(1, dx2, sv1, w, small, dict(head=no_ride, hgrn=ride_hgrn1, gw_in=ride_gw_in1,
                                                                d_h=ride_d_h1, prenorm=ride_prenorm1))

    gathered = {}
    zero_row = jnp.zeros((1, D), F32)

    def ride_gw_in0(little):
        so_far = dict(little, dshift=zero_row, dscale=zero_row, g_pre=zero_row)

        def landed(outs):
            (gathered["early"],) = outs

        red = [_chip_sum(place, parts["w_in", 1], recv["w_in", 1], 1, None, "rs_sum_w_in1")]
        red += _chip_sums(place, [[parts[k, l] for l in range(2)] for k in big_names[1:]],
                          [[recv[k, l] for l in range(2)] for k in big_names[1:]], "rs_sum_early")

        def swapped(outs):
            gathered["sums"] = outs
        return together((_gather_rows_carry(_pack_small([so_far, little1])), landed),
                        (_swap_carry(red, [(1, 1)] + [(0, 2)] * 3), swapped))

    dx0, big0, little0 = _layer_bwd(0, dx1, sv0, w, small,
                                    dict(head=ride_head0, hgrn=ride_hgrn0, gw_in=ride_gw_in0, d_h=ride_d_h0,
                                         prenorm=ride_prenorm0))
    loss_row = jnp.broadcast_to(loss_blk[0:1, 0:1], (1, D))
    late = _rows8(jnp.stack([little0["dshift"], little0["dscale"], little0["g_pre"], loss_row]))
    late = _gather_small(late, "gather_small_late").reshape(NDEV, 8, D)
    loss = jnp.sum(late[:, 3, 0])
    packed = gathered["early"].reshape(NDEV, SMALL_ROWS, D)
    packed = packed.at[:, 0:2, :].set(late[:, 0:2, :]).at[:, 8:9, :].set(late[:, 2:3, :])
    red = _chip_sum(place, parts["w_in", 0], recv["w_in", 0], 0, gathered["sums"][0], "rs_sum_w_in0")
    g_big = dict(zip(big_names, list(_rs_swap([red], [(0, 1)])) + list(gathered["sums"][1:])))

    def two(t):
        return t.reshape(-1, t.shape[-1])

    def upd(wt, g, m, v, name, echo=False):
        return [t.reshape(wt.shape) for t in _adamw(two(wt), two(g), two(m), two(v), name, echo)]

    *u_w_in, g_w_in = upd(w_in, g_big["w_in"], m_w_in, v_w_in, "adamw_w_in", echo=True)
    g_small = _sum_devices(packed)
    g_lb_logits = _lb_bwd(lb_logits, g_small[24:26])
    d_ada_all = packed[:, 0:6, :].reshape(NDEV, 2, 3 * D)
    d_ada_sh = lax.dynamic_slice(jnp.transpose(d_ada_all, (1, 0, 2)), (0, 0, chip * ada_s), (2, NDEV, ada_s))
    d_ada_sh = jnp.pad(d_ada_sh, ((0, 0), (0, ADA_PAD - NDEV), (0, 0)))
    g_w_ada = _ada_wgrad(c_pad.T, d_ada_sh)

    u_w_ada = upd(w_ada, g_w_ada, m_w_ada, v_w_ada, "adamw_w_ada")
    early_w = dict(w_pool_o=(w_pool_o, m_w_pool_o, v_w_pool_o), w_hgrn_o=(w_hgrn_o, m_w_hgrn_o, v_w_hgrn_o),
                   w_out=(w_out, m_w_out, v_w_out))
    u_early = _adamw_many([(two(early_w[k][0]), two(g_big[k]), two(early_w[k][1]), two(early_w[k][2]))
                           for k in big_names[1:]], "adamw_early")
    (*u_w_pool_o, g_w_pool_o), (*u_w_hgrn_o, g_w_hgrn_o), (*u_w_out, g_w_out) = [
        [t.reshape(early_w[k][0].shape) for t in four] for k, four in zip(big_names[1:], u_early)]
    small_w = dict(b_ada=(b_ada, m_b_ada, v_b_ada), g_pre=(g_pre, m_g_pre, v_g_pre),
                   g_post=(g_post, m_g_post, v_g_post), lb_logits=(lb_logits, m_lb_logits, v_lb_logits),
                   pool_w=(pool_w, m_pool_w, v_pool_w), pool_scale=(pool_scale, m_pool_scale, v_pool_scale),
                   hgrn_norm_g=(hgrn_norm_g, m_hgrn_norm_g, v_hgrn_norm_g))
    u_small = _adamw_small(g_small, g_lb_logits, small_w)
    s = lambda key: u_small[key][3]
    grads_out = (g_w_ada, s("b_ada"), s("g_pre"), s("g_post"), g_w_in, s("pool_w"), s("pool_scale"), g_lb_logits,
                 s("hgrn_norm_g"), g_w_pool_o, g_w_hgrn_o, g_w_out)

    def ordered(k):
        s = lambda key: u_small[key][k]
        return (u_w_ada[k], s("b_ada"), s("g_pre"), s("g_post"), u_w_in[k], s("pool_w"), s("pool_scale"),
                s("lb_logits"), s("hgrn_norm_g"), u_w_pool_o[k], u_w_hgrn_o[k], u_w_out[k])

    return (loss, dx0[None], *grads_out, *ordered(0), *ordered(1), *ordered(2))
```

```python
import functools

import jax
import jax.numpy as jnp
from jax import lax
from jax.experimental import pallas as pl
from jax.experimental.pallas import tpu as pltpu

F32 = jnp.float32
BF16 = jnp.bfloat16
MESH = pl.DeviceIdType.MESH

D = 1024
HEADS = 8
HD = 128
GROUPS = 4
POOL_W = 512
CH = 128
SB_WIDE = 32
SB = 16
NH = 2
IN_W = 7168
NCHIP = 4
NDEV = 8
EPS = 1e-6
PV0, PG0, HQ0, HF0, HI0, HG0 = 0, 4, 8, 16, 24, 32
MGP_BLK, MGH_BLK = 5, 6

LR, B1, B2, AEPS, WD, STEP = 0.001, 0.9, 0.999, 1e-08, 0.01, 10
VMEM_LIMIT = 56 * 1024 * 1024


def _cp(sem=None, **kw):
    if sem is not None:
        kw["dimension_semantics"] = sem
    return pltpu.CompilerParams(vmem_limit_bytes=VMEM_LIMIT, **kw)


def _sig(z):
    return 1.0 / (1.0 + jnp.exp(-z))


def _dsilu(z, s):
    return s * (1.0 + z * (1.0 - s))


def _row_tile(rows, cap):
    if rows <= cap:
        return rows
    t = 1 << (cap.bit_length() - 1)
    while rows % t:
        t //= 2
    return t


ANY = pl.BlockSpec(memory_space=pl.ANY)


class _Carry:
    def __init__(self, ins, outs, aliases, n_sem, start, finish):
        self.ins, self.outs, self.aliases, self.n_sem = list(ins), list(outs), dict(aliases), n_sem
        self.start, self.finish = start, finish


class _SemWindow:
    def __init__(self, ref, base):
        self._ref, self._base = ref, base

    @property
    def at(self):
        return self

    def __getitem__(self, k):
        return self._ref.at[self._base + k]


def _join_carries(*carries):
    ins, outs, aliases, spans, n_sem = [], [], {}, [], 0
    for cr in carries:
        aliases.update({len(ins) + i: len(outs) + o for i, o in cr.aliases.items()})
        spans.append((len(ins), len(cr.ins), len(outs), len(cr.outs), n_sem))
        ins, outs, n_sem = ins + cr.ins, outs + cr.outs, n_sem + cr.n_sem

    def run(which):
        def fn(i_refs, o_refs, send_sems, recv_sems):
            for cr, (i0, ni, o0, no, s0) in zip(carries, spans):
                getattr(cr, which)(i_refs[i0:i0 + ni], o_refs[o0:o0 + no], _SemWindow(send_sems, s0),
                                   _SemWindow(recv_sems, s0))
        return fn

    return _Carry(ins, outs, aliases, n_sem, run("start"), run("finish"))


def _call(body, *, name, grid, in_specs, out_specs, out_shape, args, scratch_shapes=(), sem=None, carry=None,
          aliases=None):
    in_specs, out_specs, out_shape = list(in_specs), list(out_specs), list(out_shape)
    scratch_shapes = list(scratch_shapes)
    aliases = dict(aliases or {})
    if carry is None:
        outs = pl.pallas_call(body, name=name, grid=grid, in_specs=in_specs, out_specs=out_specs,
                              out_shape=out_shape, scratch_shapes=scratch_shapes, input_output_aliases=aliases,
                              compiler_params=_cp(sem))(*args)
        return list(outs)
    n_in, n_out, n_scr = len(in_specs), len(out_specs), len(scratch_shapes)
    c_in, c_out = len(carry.ins), len(carry.outs)

    def wrapped(*refs):
        k_in, rest = refs[:n_in], refs[n_in:]
        ci, rest = rest[:c_in], rest[c_in:]
        k_out, rest = rest[:n_out], rest[n_out:]
        co, rest = rest[:c_out], rest[c_out:]
        k_scr, (ssem, rsem) = rest[:n_scr], rest[n_scr:]
        pids = [pl.program_id(d) for d in range(len(grid))]
        first = functools.reduce(jnp.logical_and, [p == 0 for p in pids])
        last = functools.reduce(jnp.logical_and, [p == g - 1 for p, g in zip(pids, grid)])

        @pl.when(first)
        def _():
            carry.start(ci, co, ssem, rsem)

        body(*k_in, *k_out, *k_scr)

        @pl.when(last)
        def _():
            carry.finish(ci, co, ssem, rsem)

    outs = pl.pallas_call(
        wrapped, name=name, grid=grid, in_specs=in_specs + [ANY] * c_in, out_specs=out_specs + [ANY] * c_out,
        out_shape=out_shape + carry.outs,
        input_output_aliases={**aliases, **{n_in + i: n_out + o for i, o in carry.aliases.items()}},
        scratch_shapes=scratch_shapes + [pltpu.SemaphoreType.DMA((carry.n_sem,))] * 2,
        compiler_params=_cp(("arbitrary",) * len(grid)),
    )(*args, *carry.ins)
    return list(outs)


def _mm(a, b, *, name, b_mode="nn", out_shards=0, tm=1024, tn=256, tk=None, out_dtype=F32, carry=None):
    assert b_mode in ("nn", "nn_sh", "nt_shk"), b_mode
    M, K = a.shape
    if b_mode == "nn":
        N = b.shape[1]
    elif b_mode == "nn_sh":
        N = b.shape[0] * b.shape[2]
    else:
        N = b.shape[1]
    tm = _row_tile(M, tm)
    if b_mode == "nn_sh":
        tn = _row_tile(b.shape[2], tn)
    elif out_shards:
        tn = _row_tile(N // out_shards, tn)
    else:
        tn = _row_tile(N, tn)
    if tk is None:
        tk = K if b_mode != "nt_shk" else b.shape[2]
    if b_mode == "nt_shk":
        tk = _row_tile(b.shape[2], tk)
    nm, nn, nk = M // tm, N // tn, K // tk

    a_spec = pl.BlockSpec((tm, tk), lambda m, n, k: (m, k))
    if b_mode == "nn":
        b_spec = pl.BlockSpec((tk, tn), lambda m, n, k: (k, n))
    elif b_mode == "nn_sh":
        nps = b.shape[2] // tn
        b_spec = pl.BlockSpec((None, tk, tn), lambda m, n, k: (n // nps, k, n % nps))
    else:
        kps = b.shape[2] // tk
        b_spec = pl.BlockSpec((None, tn, tk), lambda m, n, k: (k // kps, n, k % kps))
    if out_shards:
        ops = (N // out_shards) // tn
        o_spec = pl.BlockSpec((None, tm, tn), lambda m, n, k: (n // ops, m, n % ops))
        o_shape = jax.ShapeDtypeStruct((out_shards, M, N // out_shards), out_dtype)
    else:
        o_spec = pl.BlockSpec((tm, tn), lambda m, n, k: (m, n))
        o_shape = jax.ShapeDtypeStruct((M, N), out_dtype)
    dn = (((1,), (1,)), ((), ())) if b_mode == "nt_shk" else (((1,), (0,)), ((), ()))

    def body(a_ref, b_ref, o_ref, acc_ref):
        k = pl.program_id(2)

        @pl.when(k == 0)
        def _():
            acc_ref[...] = jnp.zeros(acc_ref.shape, F32)

        acc_ref[...] += lax.dot_general(a_ref[...].astype(BF16), b_ref[...].astype(BF16), dn,
                                        preferred_element_type=F32)

        @pl.when(k == nk - 1)
        def _():
            o_ref[...] = acc_ref[...].astype(o_ref.dtype)

    outs = _call(body, name=name, grid=(nm, nn, nk), in_specs=[a_spec, b_spec], out_specs=[o_spec],
                 out_shape=[o_shape], scratch_shapes=[pltpu.VMEM((tm, tn), F32)],
                 sem=("parallel", "parallel", "arbitrary"), args=(a, b), carry=carry)
    return outs[0] if carry is None else (outs[0], outs[1:])


def _rowvec(n=D):
    return pl.BlockSpec((1, n), lambda i: (0, 0))


def _prenorm_fwd(x, g, scale, shift, name, carry=None):
    S = x.shape[0]
    tr = _row_tile(S, 256)

    def body(x_ref, g_ref, sc_ref, sh_ref, h_ref, ht_ref):
        xv = x_ref[...]
        r = lax.rsqrt(jnp.mean(xv * xv, axis=-1, keepdims=True) + EPS)
        hv = (xv * r) * g_ref[...] * (1.0 + sc_ref[...]) + sh_ref[...]
        h_ref[...] = hv.astype(BF16)
        ht_ref[...] = hv.T.astype(BF16)

    outs = _call(
        body, name=name, grid=(S // tr,),
        in_specs=[pl.BlockSpec((tr, D), lambda i: (i, 0)), _rowvec(), _rowvec(), _rowvec()],
        out_specs=[pl.BlockSpec((tr, D), lambda i: (i, 0)), pl.BlockSpec((D, tr), lambda i: (0, i))],
        out_shape=[jax.ShapeDtypeStruct((S, D), BF16), jax.ShapeDtypeStruct((D, S), BF16)],
        sem=("parallel",), args=(x, g, scale, shift), carry=carry)
    return outs[:2], outs[2:]


def _prenorm_bwd(dh, dxn, x, g, scale, name, carry=None):
    S = x.shape[0]
    tr = _row_tile(S, 256)

    def body(dh_ref, dxn_ref, x_ref, g_ref, sc_ref, dx_ref, dsh_ref, dsc_ref, dg_ref):
        i = pl.program_id(0)

        @pl.when(i == 0)
        def _():
            dsh_ref[...] = jnp.zeros((1, D), F32)
            dsc_ref[...] = jnp.zeros((1, D), F32)
            dg_ref[...] = jnp.zeros((1, D), F32)

        xv = x_ref[...]
        dhv = dh_ref[...]
        gv = g_ref[...]
        mod = 1.0 + sc_ref[...]
        r = lax.rsqrt(jnp.mean(xv * xv, axis=-1, keepdims=True) + EPS)
        xh = xv * r
        dsh_ref[...] += jnp.sum(dhv, axis=0, keepdims=True)
        dsc_ref[...] += jnp.sum(dhv * (xh * gv), axis=0, keepdims=True)
        dg_ref[...] += jnp.sum(dhv * mod * xh, axis=0, keepdims=True)
        u = dhv * mod * gv
        dx_ref[...] = dxn_ref[...] + r * u - xv * (r * r * r) * jnp.mean(u * xv, axis=-1, keepdims=True)

    tile = pl.BlockSpec((tr, D), lambda i: (i, 0))
    outs = _call(
        body, name=name, grid=(S // tr,),
        in_specs=[tile, tile, tile, _rowvec(), _rowvec()],
        out_specs=[tile, _rowvec(), _rowvec(), _rowvec()],
        out_shape=[jax.ShapeDtypeStruct((S, D), F32)] + [jax.ShapeDtypeStruct((1, D), F32)] * 3,
        sem=("arbitrary",), args=(dh, dxn, x, g, scale), carry=carry)
    return outs[:4], outs[4:]


def _layer_tail_fwd(proj, a_in, b_in, x, w_po, w_ho, w_out, gate, g, name, target=None, carry=None):
    S = proj.shape[0]
    tr = _row_tile(S, 256)
    nsh, _, wsh = w_po.shape
    n_in = 10 + (target is not None)

    def body(*refs):
        (mgp_ref, mgh_ref, a_ref, b_ref, x_ref, wpo_ref, who_ref, wout_ref, gate_ref, g_ref) = refs[:10]
        bra_ref, brb_ref, mt_ref, y_ref, xn_ref = refs[n_in:n_in + 5]
        av = a_ref[...]
        bra = jnp.concatenate([jnp.dot(av, wpo_ref[j], preferred_element_type=F32) for j in range(nsh)], axis=1)
        brb = jnp.dot(b_ref[...], who_ref[...], preferred_element_type=F32)
        mv = _sig(mgp_ref[...].astype(F32)) * bra + _sig(mgh_ref[...].astype(F32)) * brb
        bra_ref[...] = bra.astype(BF16)
        brb_ref[...] = brb.astype(BF16)
        mt_ref[...] = mv.T.astype(BF16)
        yv = jnp.dot(mv.astype(BF16), wout_ref[...], preferred_element_type=F32)
        y_ref[...] = yv
        r = lax.rsqrt(jnp.mean(yv * yv, axis=-1, keepdims=True) + EPS)
        xn = x_ref[...] + gate_ref[...] * ((yv * r) * g_ref[...])
        if target is None:
            xn_ref[...] = xn
        else:
            t_ref, l_ref = refs[10], refs[n_in + 5]

            @pl.when(pl.program_id(0) == 0)
            def _():
                l_ref[...] = jnp.zeros((8, 128), F32)

            err = xn - t_ref[...]
            xn_ref[...] = err * (1.0 / D)
            l_ref[...] += 0.5 * jnp.sum(jnp.mean(err * err, axis=-1, keepdims=True))

    tile = pl.BlockSpec((tr, D), lambda i: (i, 0))
    whole = lambda t: pl.BlockSpec(t.shape, lambda i: (0,) * t.ndim)
    last = target is not None
    outs = _call(
        body, name=name, grid=(S // tr,),
        in_specs=[pl.BlockSpec((tr, D), lambda i: (i, MGP_BLK)), pl.BlockSpec((tr, D), lambda i: (i, MGH_BLK)),
                  pl.BlockSpec((tr, POOL_W), lambda i: (i, 0)), tile, tile, whole(w_po), whole(w_ho),
                  whole(w_out), _rowvec(), _rowvec()] + [tile] * last,
        out_specs=[tile, tile, pl.BlockSpec((D, tr), lambda i: (0, i)), tile, tile]
        + [pl.BlockSpec((8, 128), lambda i: (0, 0))] * last,
        out_shape=[jax.ShapeDtypeStruct((S, D), BF16), jax.ShapeDtypeStruct((S, D), BF16),
                   jax.ShapeDtypeStruct((D, S), BF16), jax.ShapeDtypeStruct((S, D), F32),
                   jax.ShapeDtypeStruct((S, D), F32)] + [jax.ShapeDtypeStruct((8, 128), F32)] * last,
        sem=("arbitrary",) if last else ("parallel",),
        args=(proj, proj, a_in, b_in, x, w_po, w_ho, w_out, gate, g) + ((target,) if last else ()), carry=carry)
    return outs[:5 + last], outs[5 + last:]


def _layer_head_bwd(dxn, y, proj, br_a, br_b, w_po, w_ho, w_out, gate, g, name, carry=None):
    S = y.shape[0]
    tr = _row_tile(S, 256)
    nsh, _, wsh = w_po.shape

    def body(dxn_ref, y_ref, mgp_ref, mgh_ref, bra_ref, brb_ref, wpo_ref, who_ref, wout_ref, gate_ref, g_ref,
             dy_ref, dba_ref, dbb_ref, dproj_ref, dain_ref, dbin_ref, dgate_ref, dg_ref, dmgh_s):
        i = pl.program_id(0)
        j = pl.program_id(1)

        @pl.when((i == 0) & (j == 0))
        def _():
            dgate_ref[...] = jnp.zeros((1, D), F32)
            dg_ref[...] = jnp.zeros((1, D), F32)

        @pl.when(j == 1)
        def _():
            dproj_ref[...] = dmgh_s[...]

        @pl.when(j == 0)
        def _():
            everything(dxn_ref, y_ref, mgp_ref, mgh_ref, bra_ref, brb_ref, wpo_ref, who_ref, wout_ref, gate_ref,
                       g_ref, dy_ref, dba_ref, dbb_ref, dproj_ref, dain_ref, dbin_ref, dgate_ref, dg_ref, dmgh_s)

    def everything(dxn_ref, y_ref, mgp_ref, mgh_ref, bra_ref, brb_ref, wpo_ref, who_ref, wout_ref, gate_ref, g_ref,
                   dy_ref, dba_ref, dbb_ref, dproj_ref, dain_ref, dbin_ref, dgate_ref, dg_ref, dmgh_s):
        yv = y_ref[...]
        dv = dxn_ref[...]
        gv = g_ref[...]
        gt = gate_ref[...]
        r = lax.rsqrt(jnp.mean(yv * yv, axis=-1, keepdims=True) + EPS)
        yh = yv * r
        dgate_ref[...] += jnp.sum(dv * (yh * gv), axis=0, keepdims=True)
        dg_ref[...] += jnp.sum(dv * gt * yh, axis=0, keepdims=True)
        u = dv * gt * gv
        dy = (r * u - yv * (r * r * r) * jnp.mean(u * yv, axis=-1, keepdims=True)).astype(BF16)
        dy_ref[...] = dy
        dm = _dot_nt(dy, wout_ref[...])
        sp = _sig(mgp_ref[...].astype(F32))
        sh = _sig(mgh_ref[...].astype(F32))
        dba = (dm * sp).astype(BF16)
        dbb = (dm * sh).astype(BF16)
        dba_ref[...] = dba
        dbb_ref[...] = dbb
        dproj_ref[...] = (dm * bra_ref[...].astype(F32) * sp * (1.0 - sp)).astype(BF16)
        dmgh_s[...] = (dm * brb_ref[...].astype(F32) * sh * (1.0 - sh)).astype(BF16)
        dain = _dot_nt(dba[:, 0:wsh], wpo_ref[0])
        for k in range(1, nsh):
            dain = dain + _dot_nt(dba[:, k * wsh:(k + 1) * wsh], wpo_ref[k])
        dain_ref[...] = dain
        dbin_ref[...] = _dot_nt(dbb, who_ref[...])

    tile = pl.BlockSpec((tr, D), lambda i, j: (i, 0))
    whole = lambda t: pl.BlockSpec(t.shape, lambda i, j: (0,) * t.ndim)
    vec = pl.BlockSpec((1, D), lambda i, j: (0, 0))
    ahead = lambda i, j: jnp.minimum(i + j, S // tr - 1)
    tile_in = pl.BlockSpec((tr, D), lambda i, j: (ahead(i, j), 0))
    outs = _call(
        body, name=name, grid=(S // tr, 2),
        in_specs=[tile_in, tile_in, pl.BlockSpec((tr, D), lambda i, j: (ahead(i, j), MGP_BLK)),
                  pl.BlockSpec((tr, D), lambda i, j: (ahead(i, j), MGH_BLK)), tile_in, tile_in, whole(w_po),
                  whole(w_ho), whole(w_out), vec, vec],
        out_specs=[tile, tile, tile, pl.BlockSpec((tr, D), lambda i, j: (i, MGP_BLK + j)),
                   pl.BlockSpec((tr, POOL_W), lambda i, j: (i, 0)), tile, vec, vec],
        out_shape=[jax.ShapeDtypeStruct((S, D), BF16)] * 3
        + [jax.ShapeDtypeStruct((S, IN_W), BF16), jax.ShapeDtypeStruct((S, POOL_W), F32),
           jax.ShapeDtypeStruct((S, D), F32), jax.ShapeDtypeStruct((1, D), F32), jax.ShapeDtypeStruct((1, D), F32)],
        scratch_shapes=[pltpu.VMEM((tr, D), BF16)], sem=("arbitrary", "arbitrary"),
        args=(dxn, y, proj, proj, br_a, br_b, w_po, w_ho, w_out, gate, g), carry=carry)
    return outs[:8], outs[8:]


def _pool_pieces(u, g, S):
    rowi = lax.broadcasted_iota(jnp.int32, (S, 1), 0)

    def down(z, k):
        return jnp.where(rowi >= k, pltpu.roll(z, k, axis=0), 0.0)

    s2 = u + down(u, 1)
    s4 = s2 + down(s2, 2)
    s8 = s4 + down(s4, 4)
    s16 = s8 + down(s8, 8)
    win = jnp.where(g == 0, s2, jnp.where(g == 1, s4, jnp.where(g == 2, s8, s16)))
    w = jnp.where(g == 0, 2, jnp.where(g == 1, 4, jnp.where(g == 2, 8, 16)))
    count = jnp.minimum(rowi + 1, w).astype(F32)
    return win / count - u, count, rowi


def _pool_fwd(proj, pw, pscale, name):
    S = proj.shape[0]

    def body(pv_ref, pg_ref, pw_ref, sc_ref, a_ref, at_ref):
        g = pl.program_id(0)
        pooled, _, _ = _pool_pieces(pv_ref[...].astype(F32), g, S)
        pm = jnp.dot(pooled.astype(BF16), pw_ref[...].astype(BF16), preferred_element_type=F32)
        pgv = pg_ref[...].astype(F32)
        av = pm * sc_ref[...] * (pgv * _sig(pgv))
        a_ref[...] = av.astype(BF16)
        at_ref[...] = av.T.astype(BF16)

    outs = _call(
        body, name=name, grid=(GROUPS,),
        in_specs=[pl.BlockSpec((S, 128), lambda g: (0, PV0 + g)), pl.BlockSpec((S, 128), lambda g: (0, PG0 + g)),
                  pl.BlockSpec((None, 128, 128), lambda g: (g, 0, 0)), pl.BlockSpec((1, 128), lambda g: (0, g))],
        out_specs=[pl.BlockSpec((S, 128), lambda g: (0, g)), pl.BlockSpec((128, S), lambda g: (g, 0))],
        out_shape=[jax.ShapeDtypeStruct((S, POOL_W), BF16), jax.ShapeDtypeStruct((POOL_W, S), BF16)],
        sem=("parallel",), args=(proj, proj, pw, pscale))
    return outs


def _pool_bwd(da, proj, pw, pscale, dproj, name):
    S = proj.shape[0]

    def body(da_ref, pv_ref, pg_ref, pw_ref, sc_ref, dproj_in, dproj_ref, dpw_ref, dsc_ref, dpg_s):
        @pl.when(pl.program_id(1) == 1)
        def _():
            dproj_ref[...] = dpg_s[...]

        @pl.when(pl.program_id(1) == 0)
        def _():
            group(da_ref, pv_ref, pg_ref, pw_ref, sc_ref, dproj_ref, dpg_s, dpw_ref, dsc_ref)

    def group(da_ref, pv_ref, pg_ref, pw_ref, sc_ref, dpv_ref, dpg_ref, dpw_ref, dsc_ref):
        g = pl.program_id(0)
        pooled, count, rowi = _pool_pieces(pv_ref[...].astype(F32), g, S)
        pwb = pw_ref[...].astype(BF16)
        pm = jnp.dot(pooled.astype(BF16), pwb, preferred_element_type=F32)
        scv = sc_ref[...]
        pgv = pg_ref[...].astype(F32)
        sg = _sig(pgv)
        dav = da_ref[...]
        d_ps = dav * (pgv * sg)
        dpg_ref[...] = (dav * (pm * scv) * _dsilu(pgv, sg)).astype(BF16)
        dsc_ref[...] = jnp.sum(d_ps * pm, axis=0, keepdims=True)
        d_pm = (d_ps * scv).astype(BF16)
        dpw_ref[...] = lax.dot_general(pooled.astype(BF16), d_pm, (((0,), (0,)), ((), ())),
                                       preferred_element_type=F32)
        d_pooled = lax.dot_general(d_pm, pwb, (((1,), (1,)), ((), ())), preferred_element_type=F32)
        z = d_pooled / count

        def up(v, k):
            return jnp.where(rowi < S - k, pltpu.roll(v, S - k, axis=0), 0.0)

        t2 = z + up(z, 1)
        t4 = t2 + up(t2, 2)
        t8 = t4 + up(t4, 4)
        t16 = t8 + up(t8, 8)
        adj = jnp.where(g == 0, t2, jnp.where(g == 1, t4, jnp.where(g == 2, t8, t16)))
        dpv_ref[...] = (adj - d_pooled).astype(BF16)

    col = lambda g, j: (0, g)
    ahead = lambda g, j: jnp.minimum(g + j, GROUPS - 1)
    return pl.pallas_call(
        body, name=name, grid=(GROUPS, 2),
        in_specs=[pl.BlockSpec((S, 128), lambda g, j: (0, ahead(g, j))),
                  pl.BlockSpec((S, 128), lambda g, j: (0, PV0 + ahead(g, j))),
                  pl.BlockSpec((S, 128), lambda g, j: (0, PG0 + ahead(g, j))),
                  pl.BlockSpec((None, 128, 128), lambda g, j: (ahead(g, j), 0, 0)),
                  pl.BlockSpec((1, 128), lambda g, j: (0, ahead(g, j))), ANY],
        out_specs=[pl.BlockSpec((S, 128), lambda g, j: (0, PV0 + g + (PG0 - PV0) * j)),
                   pl.BlockSpec((None, 128, 128), lambda g, j: (g, 0, 0)), pl.BlockSpec((1, 128), col)],
        out_shape=[jax.ShapeDtypeStruct(dproj.shape, dproj.dtype),
                   jax.ShapeDtypeStruct((GROUPS, 128, 128), F32), jax.ShapeDtypeStruct((1, POOL_W), F32)],
        scratch_shapes=[pltpu.VMEM((S, 128), BF16)], input_output_aliases={5: 0},
        compiler_params=_cp(("arbitrary", "arbitrary")),
    )(da, proj, proj, pw, pscale, dproj)


SCAN_SHIFTS = tuple(1 << b for b in range(CH.bit_length() - 1))


def _chunk_cumsum(z, rowi):
    for sh in SCAN_SHIFTS:
        z = z + jnp.where(rowi >= sh, pltpu.roll(z, sh, axis=0), 0.0)
    return z


def _chunk_rev_cumsum(z, rowi):
    for sh in SCAN_SHIFTS:
        z = z + jnp.where(rowi < CH - sh, pltpu.roll(z, CH - sh, axis=0), 0.0)
    return z


def _dot_nn(a, b):
    return jnp.dot(a.astype(BF16), b.astype(BF16), preferred_element_type=F32)


def _dot_nt(a, b):
    return lax.dot_general(a.astype(BF16), b.astype(BF16), (((1,), (1,)), ((), ())), preferred_element_type=F32)


def _dot_tn(a, b):
    return lax.dot_general(a.astype(BF16), b.astype(BF16), (((0,), (0,)), ((), ())), preferred_element_type=F32)


def _gates(hq, hf, lbv):
    hq, hf = hq.astype(F32), hf.astype(F32)
    sq = _sig(hq)
    sf = _sig(hf)
    f = lbv + (1.0 - lbv) * sf
    fc = jnp.maximum(f, 1e-30)
    return hq * sq, sq, sf, f, fc, jnp.log(fc)


DECAY_CAP = 60.0


def _block_ref(c_ref, i, sb):
    if i == 0:
        return jnp.zeros((1, HD), F32)
    return c_ref[sb * i - 1:sb * i, :]


def _block_decay(c_ref, sb):
    spans = [_block_ref(c_ref, i, sb) - c_ref[sb * (i + 1) - 1:sb * (i + 1), :] for i in range(CH // sb)]
    return functools.reduce(jnp.maximum, spans)


def _pair_factors(q_ref, k, c_ref, first, cap, round_bf16, sb):
    nb = CH // sb
    c = c_ref[...]
    zero = jnp.zeros((sb, HD), F32)
    q_groups, k_groups, eqs, eks = [], [], [], []
    for i in range(first, nb):
        blk = slice(sb * i, sb * (i + 1))
        r_i = _block_ref(c_ref, i, sb)
        eq = jnp.exp(jnp.minimum(c_ref[blk, :] - r_i, 0.0))
        ek = jnp.exp(jnp.minimum(r_i - c, cap))
        qi, kei = q_ref[blk, :] * eq, k * ek
        if round_bf16:
            qi, kei = qi.astype(BF16).astype(F32), kei.astype(BF16).astype(F32)
        q_groups.append(jnp.concatenate([zero] * i + [qi] + [zero] * (nb - 1 - i), axis=0))
        k_groups.append(kei)
        eqs.append(eq)
        eks.append(ek)
    return jnp.concatenate(q_groups, axis=1), jnp.concatenate(k_groups, axis=1), eqs, eks


def _pair_mask(rowi, coli, strict, sb):
    return (coli < jnp.bitwise_and(rowi, -sb)) if strict else (coli <= rowi)


def _hgrn_fwd(proj, lb, gn, name, carry=None):
    S = proj.shape[0]
    nch = S // CH
    W = NH * HD

    def body(hq_ref, hf_ref, hi_ref, hg_ref, lb_ref, gn_ref, bin_ref, bint_ref, oraw_ref, st_ref, mild_ref,
             cum_ref, q_s, k_s, c_s, v_s, o_s, state_s, qf_s, kf_s, cf_s):
        state_s[...] = jnp.zeros((NH, HD, HD), F32)
        rowi = lax.broadcasted_iota(jnp.int32, (CH, 1), 0)
        coli = lax.broadcasted_iota(jnp.int32, (1, CH), 1)
        sbi = lax.broadcasted_iota(jnp.int32, (SB, 1), 0)
        gnv = gn_ref[...]

        def gates_pass(n, worst):
            wide, narrow = worst
            rows = pl.ds(pl.multiple_of(n * CH, CH), CH)
            for hh in range(NH):
                lanes = slice(hh * HD, (hh + 1) * HD)
                q, _, _, f, _, logf = _gates(hq_ref[rows, lanes], hf_ref[rows, lanes], lb_ref[:, lanes])
                c = _chunk_cumsum(logf, rowi)
                qf_s[hh, rows, :] = q
                kf_s[hh, rows, :] = 1.0 - f
                cf_s[hh, rows, :] = c
                cum_ref[rows, lanes] = c
                c_s[hh] = c
                wide = jnp.maximum(wide, _block_decay(c_s.at[hh], SB_WIDE))
                narrow = jnp.maximum(narrow, _block_decay(c_s.at[hh], SB))
            return wide, narrow

        def between_chunks(hh, n, rows):
            lanes = slice(hh * HD, (hh + 1) * HD)
            q = qf_s[hh, rows, :]
            k = kf_s[hh, rows, :]
            c = cf_s[hh, rows, :]
            v = hi_ref[rows, lanes].astype(F32)
            q_s[hh] = q
            k_s[hh] = k
            c_s[hh] = c
            v_s[hh] = v
            st = state_s[hh]
            st_ref[hh, n] = st.astype(BF16)
            o_s[hh] = _dot_nt(q * jnp.exp(c), st)
            last = c_s[hh, CH - 1:CH, :]
            state_s[hh] = st * jnp.exp(last) + _dot_tn(v, k * jnp.exp(last - c))

        def pairs_matmul(hh, first, cap, strict, sb):
            qx, kc, _, _ = _pair_factors(q_s.at[hh], k_s[hh], c_s.at[hh], first, cap, False, sb)
            a = jnp.where(_pair_mask(rowi, coli, strict, sb), _dot_nt(qx, kc), 0.0)
            o_s[hh] += _dot_nn(a, v_s[hh])

        def within_chunk_matmul(sb):
            return lambda hh: pairs_matmul(hh, 0, DECAY_CAP, False, sb)

        def within_chunk_exact(hh):
            pairs_matmul(hh, 1, 0.0, True, SB)
            for i in range(CH // SB):
                blk = slice(SB * i, SB * (i + 1))
                qb = q_s[hh, blk, :]
                cb = c_s[hh, blk, :]
                acc = jnp.zeros((SB, HD), F32)
                for s in range(SB):
                    row = SB * i + s
                    w = jnp.exp(jnp.minimum(cb - c_s[hh, row:row + 1, :], 0.0))
                    a_col = jnp.sum(qb * k_s[hh, row:row + 1, :] * w, axis=-1, keepdims=True)
                    acc = acc + jnp.where(sbi >= s, a_col, 0.0) * v_s[hh, row:row + 1, :]
                o_s[hh, blk, :] += acc

        def norm_and_gate(hh, rows):
            lanes = slice(hh * HD, (hh + 1) * HD)
            ov = o_s[hh]
            oraw_ref[rows, lanes] = ov
            r = lax.rsqrt(jnp.mean(ov * ov, axis=-1, keepdims=True) + EPS)
            hg = hg_ref[rows, lanes].astype(F32)
            bin_ref[rows, lanes] = ((ov * r) * gnv * (hg * _sig(hg))).astype(BF16)

        def chunk_with(within_chunk):
            def chunk(n, carry):
                rows = pl.ds(pl.multiple_of(n * CH, CH), CH)
                for hh in range(NH):
                    between_chunks(hh, n, rows)
                for hh in range(NH):
                    within_chunk(hh)
                for hh in range(NH):
                    norm_and_gate(hh, rows)
                return carry
            return chunk

        none = jnp.zeros((1, HD), F32)
        wide, narrow = lax.fori_loop(0, nch, gates_pass, (none, none))
        tier = jnp.where(jnp.max(wide) <= DECAY_CAP, 2.0, jnp.where(jnp.max(narrow) <= DECAY_CAP, 1.0, 0.0))
        mild_ref[...] = jnp.broadcast_to(tier, (8, HD))

        @pl.when(tier == 2.0)
        def _():
            lax.fori_loop(0, nch, chunk_with(within_chunk_matmul(SB_WIDE)), 0, unroll=4)

        @pl.when(tier == 1.0)
        def _():
            lax.fori_loop(0, nch, chunk_with(within_chunk_matmul(SB)), 0, unroll=2)

        @pl.when(tier == 0.0)
        def _():
            lax.fori_loop(0, nch, chunk_with(within_chunk_exact), 0)

        bint_ref[...] = bin_ref[...].astype(F32).T.astype(BF16)

    col = lambda off: pl.BlockSpec((S, W), lambda h: (0, off // NH + h))
    head = pl.BlockSpec((S, W), lambda h: (0, h))
    outs = _call(
        body, name=name, grid=(HEADS // NH,),
        in_specs=[col(HQ0), col(HF0), col(HI0), col(HG0), pl.BlockSpec((1, W), lambda h: (0, h)),
                  pl.BlockSpec((1, HD), lambda h: (0, 0))],
        out_specs=[head, pl.BlockSpec((W, S), lambda h: (h, 0)), head,
                   pl.BlockSpec((NH, nch, HD, HD), lambda h: (h, 0, 0, 0)),
                   pl.BlockSpec((8, HD), lambda h: (h, 0)), head],
        out_shape=[jax.ShapeDtypeStruct((S, D), BF16), jax.ShapeDtypeStruct((D, S), BF16),
                   jax.ShapeDtypeStruct((S, D), F32), jax.ShapeDtypeStruct((HEADS, nch, HD, HD), BF16),
                   jax.ShapeDtypeStruct((8 * HEADS // NH, HD), F32), jax.ShapeDtypeStruct((S, D), F32)],
        scratch_shapes=[pltpu.VMEM((NH, CH, HD), F32)] * 5 + [pltpu.VMEM((NH, HD, HD), F32)]
        + [pltpu.VMEM((NH, S, HD), F32)] * 3,
        sem=("parallel",), args=(proj, proj, proj, proj, lb, gn), carry=carry)
    return outs[:6], outs[6:]


def _hgrn_bwd(dbin, proj, oraw, states, mild, cum, lb, gn, dproj, name, carry=None):
    S = proj.shape[0]
    nch = S // CH
    W = NH * HD
    n_in = 12

    def body(*refs):
        ins, (dproj_ref, dlb_ref, dgn_ref) = refs[:n_in - 1], refs[n_in:n_in + 3]
        scratch, later = refs[n_in + 3:-3], refs[-3:]
        seg = pl.program_id(1)

        @pl.when(seg == 0)
        def _():
            heads(*ins, dproj_ref, *later, dlb_ref, dgn_ref, *scratch)

        for s, kept in enumerate(later):
            @pl.when(seg == s + 1)
            def _(kept=kept):
                dproj_ref[...] = kept[...]

    def heads(db_ref, hq_ref, hf_ref, hi_ref, hg_ref, or_ref, st_ref, mild_ref, cum_ref, lb_ref, gn_ref,
              dq_ref, df_ref, di_ref, dg_ref, dlb_ref, dgn_ref,
              q_s, k_s, c_s, v_s, do_s, dq_s, dk_s, dv_s, dc_s, dqd_s, dkd_s, f_s, sf_s, sq_s, dl_s, dst_s,
              dlb_s, dgn_s):
        dst_s[...] = jnp.zeros((NH, HD, HD), F32)
        dlb_s[...] = jnp.zeros((1, W), F32)
        dgn_s[...] = jnp.zeros((1, HD), F32)
        rowi = lax.broadcasted_iota(jnp.int32, (CH, 1), 0)
        coli = lax.broadcasted_iota(jnp.int32, (1, CH), 1)
        sbi = lax.broadcasted_iota(jnp.int32, (SB, 1), 0)
        gnv = gn_ref[...]
        def between_chunks(hh, n, rows):
            lanes = slice(hh * HD, (hh + 1) * HD)
            lbv = lb_ref[:, lanes]
            hq = hq_ref[rows, lanes].astype(F32)
            sq = _sig(hq)
            sf = _sig(hf_ref[rows, lanes].astype(F32))
            f = lbv + (1.0 - lbv) * sf
            q = hq * sq
            k = 1.0 - f
            f_s[hh] = f
            sf_s[hh] = sf
            sq_s[hh] = sq
            v = hi_ref[rows, lanes].astype(F32)
            c = cum_ref[rows, lanes]
            ov = or_ref[rows, lanes]
            hg = hg_ref[rows, lanes].astype(F32)
            sg = _sig(hg)
            r = lax.rsqrt(jnp.mean(ov * ov, axis=-1, keepdims=True) + EPS)
            dbv = db_ref[rows, lanes]
            d_on = dbv * (hg * sg)
            dg_ref[rows, lanes] = (dbv * ((ov * r) * gnv) * _dsilu(hg, sg)).astype(BF16)
            dgn_s[...] += jnp.sum(d_on * (ov * r), axis=0, keepdims=True)
            u = d_on * gnv
            do = r * u - ov * (r * r * r) * jnp.mean(u * ov, axis=-1, keepdims=True)
            q_s[hh] = q
            k_s[hh] = k
            c_s[hh] = c
            v_s[hh] = v
            do_s[hh] = do
            st = st_ref[hh, n].astype(F32)
            dst = dst_s[hh]
            ec = jnp.exp(c)
            last = c_s[hh, CH - 1:CH, :]
            el = jnp.exp(last - c)
            elast = jnp.exp(last)
            dq = _dot_nn(do, st) * ec
            dk = _dot_nn(v, dst) * el
            dq_s[hh] = dq
            dk_s[hh] = dk
            dv_s[hh] = _dot_nt(k * el, dst)
            dc_s[hh] = q * dq - k * dk
            dl_s[hh] = (jnp.sum(k * dk, axis=0, keepdims=True)
                        + elast * jnp.sum(st * dst, axis=0, keepdims=True))
            dst_s[hh] = dst * elast + _dot_tn(do, q * ec)

        def pairs_matmul(hh, first, cap, strict, sb):
            do = do_s[hh]
            qx, kc, eqs, eks = _pair_factors(q_s.at[hh], k_s[hh], c_s.at[hh], first, cap, True, sb)
            mask = _pair_mask(rowi, coli, strict, sb)
            a = jnp.where(mask, _dot_nt(qx, kc), 0.0)
            d_a = jnp.where(mask, _dot_nt(do, v_s[hh]).astype(BF16).astype(F32), 0.0)
            dqx = _dot_nn(d_a, kc)
            dkc = _dot_tn(d_a, qx)
            dv_s[hh] += _dot_tn(a, do)
            dk, dcum = dk_s[hh], dc_s[hh]
            dq_slabs = [jnp.zeros((sb, HD), F32)] * first
            dc_slabs = [jnp.zeros((sb, HD), F32)] * first
            for g, (eq, ek) in enumerate(zip(eqs, eks)):
                rows = slice(sb * (first + g), sb * (first + g + 1))
                cols = slice(HD * g, HD * (g + 1))
                dq_i = dqx[rows, cols]
                dk_i = dkc[:, cols]
                dq_slabs.append(dq_i * eq)
                dc_slabs.append(qx[rows, cols] * dq_i)
                dk = dk + dk_i * ek
                dcum = dcum - kc[:, cols] * dk_i
            dq_s[hh] += jnp.concatenate(dq_slabs, axis=0)
            dk_s[hh] = dk
            dc_s[hh] = dcum + jnp.concatenate(dc_slabs, axis=0)

        def pairs_exact(hh):
            dqd_s[hh] = jnp.zeros((CH, HD), F32)
            dkd_s[hh] = jnp.zeros((CH, HD), F32)
            for i in range(CH // SB):
                blk = slice(SB * i, SB * (i + 1))
                qb = q_s[hh, blk, :]
                cb = c_s[hh, blk, :]
                dob = do_s[hh, blk, :]
                dq_acc = jnp.zeros((SB, HD), F32)
                for s in range(SB):
                    row = SB * i + s
                    ks = k_s[hh, row:row + 1, :]
                    vs = v_s[hh, row:row + 1, :]
                    w = jnp.exp(jnp.minimum(cb - c_s[hh, row:row + 1, :], 0.0))
                    live = sbi >= s
                    a_col = jnp.where(live, jnp.sum(qb * ks * w, axis=-1, keepdims=True), 0.0)
                    da_col = jnp.where(live, jnp.sum(dob * vs, axis=-1, keepdims=True), 0.0)
                    dq_acc = dq_acc + da_col * ks * w
                    dkd_s[hh, row:row + 1, :] += jnp.sum(da_col * qb * w, axis=0, keepdims=True)
                    dv_s[hh, row:row + 1, :] += jnp.sum(a_col * dob, axis=0, keepdims=True)
                dqd_s[hh, blk, :] += dq_acc
            dq_d = dqd_s[hh]
            dk_d = dkd_s[hh]
            dq_s[hh] += dq_d
            dk_s[hh] += dk_d
            dc_s[hh] += q_s[hh] * dq_d - k_s[hh] * dk_d

        def gate_grads(hh, rows):
            lanes = slice(hh * HD, (hh + 1) * HD)
            lbv = lb_ref[:, lanes]
            hq = hq_ref[rows, lanes].astype(F32)
            f, sf, sq = f_s[hh], sf_s[hh], sq_s[hh]
            dlogf = _chunk_rev_cumsum(dc_s[hh], rowi) + dl_s[hh]
            dfv = jnp.where(f > 1e-30, dlogf / jnp.maximum(f, 1e-30), 0.0) - dk_s[hh]
            dlb_s[:, lanes] += jnp.sum(dfv * (1.0 - sf), axis=0, keepdims=True)
            df_ref[rows, lanes] = (dfv * (1.0 - lbv) * sf * (1.0 - sf)).astype(BF16)
            dq_ref[rows, lanes] = (dq_s[hh] * _dsilu(hq, sq)).astype(BF16)
            di_ref[rows, lanes] = dv_s[hh].astype(BF16)

        def chunk_with(pairs):
            def chunk(j, carry):
                n = nch - 1 - j
                rows = pl.ds(pl.multiple_of(n * CH, CH), CH)
                for hh in range(NH):
                    between_chunks(hh, n, rows)
                for hh in range(NH):
                    pairs(hh)
                for hh in range(NH):
                    gate_grads(hh, rows)
                return carry
            return chunk

        def pairs_mild(sb):
            return lambda hh: pairs_matmul(hh, 0, DECAY_CAP, False, sb)

        def pairs_any(hh):
            pairs_matmul(hh, 1, 0.0, True, SB)
            pairs_exact(hh)

        tier = jnp.max(mild_ref[...])

        @pl.when(tier == 2.0)
        def _():
            lax.fori_loop(0, nch, chunk_with(pairs_mild(SB_WIDE)), 0, unroll=2)

        @pl.when(tier == 1.0)
        def _():
            lax.fori_loop(0, nch, chunk_with(pairs_mild(SB)), 0)

        @pl.when(tier == 0.0)
        def _():
            lax.fori_loop(0, nch, chunk_with(pairs_any), 0)

        dlb_ref[...] = dlb_s[...]
        dgn_ref[...] = jnp.broadcast_to(dgn_s[...], (8, HD))

    ahead = lambda h, s: jnp.minimum(h + jnp.minimum(s, 1), HEADS // NH - 1)
    col = lambda off: pl.BlockSpec((S, W), lambda h, s: (0, off // NH + ahead(h, s)))
    head_in = pl.BlockSpec((S, W), lambda h, s: (0, ahead(h, s)))
    vec_in = pl.BlockSpec((1, W), lambda h, s: (0, ahead(h, s)))
    vec = pl.BlockSpec((1, W), lambda h, s: (0, h))
    seg_w = (HF0 - HQ0) // NH
    outs = _call(
        body, name=name, grid=(HEADS // NH, 4),
        in_specs=[head_in, col(HQ0), col(HF0), col(HI0), col(HG0), head_in,
                  pl.BlockSpec((NH, nch, HD, HD), lambda h, s: (ahead(h, s), 0, 0, 0)),
                  pl.BlockSpec((8, HD), lambda h, s: (ahead(h, s), 0)), head_in, vec_in,
                  pl.BlockSpec((1, HD), lambda h, s: (0, 0)), ANY],
        out_specs=[pl.BlockSpec((S, W), lambda h, s: (0, HQ0 // NH + seg_w * s + h)), vec,
                   pl.BlockSpec((8, HD), lambda h, s: (h, 0))],
        out_shape=[jax.ShapeDtypeStruct(dproj.shape, dproj.dtype), jax.ShapeDtypeStruct((1, D), F32),
                   jax.ShapeDtypeStruct((8 * HEADS // NH, HD), F32)],
        scratch_shapes=[pltpu.VMEM((NH, CH, HD), F32)] * 14
        + [pltpu.VMEM((NH, 1, HD), F32), pltpu.VMEM((NH, HD, HD), F32), pltpu.VMEM((1, W), F32),
           pltpu.VMEM((1, HD), F32)] + [pltpu.VMEM((S, W), BF16)] * 3,
        sem=("arbitrary", "arbitrary"), aliases={n_in - 1: 0},
        args=(dbin, proj, proj, proj, proj, oraw, states, mild, cum, lb, gn, dproj), carry=carry)
    dproj, dlb, dgn = outs[:3]
    return (dproj, dlb, dgn.reshape(HEADS // NH, 8, HD)[:, 0, :]), outs[3:]


def _lower_bounds(l0, l1):
    m = jnp.maximum(l0, l1)
    e0 = jnp.exp(l0 - m)
    e1 = jnp.exp(l1 - m)
    tot = e0 + e1
    p0 = e0 / tot
    p1 = e1 / tot
    return jnp.clip(p0 - p0, 0.0, 1.0), jnp.clip((p0 + p1) - p0, 0.0, 1.0)


def _lb_fwd(logits):
    def body(l_ref, o_ref):
        lb0, lb1 = _lower_bounds(l_ref[0:1, :], l_ref[1:2, :])
        o_ref[0:1, :] = lb0
        o_ref[1:2, :] = lb1

    return pl.pallas_call(body, name="lb_fwd", out_shape=jax.ShapeDtypeStruct((2, D), F32))(logits)


def _lb_bwd(logits, dlb):
    def body(l_ref, d_ref, o_ref):
        _, vjp = jax.vjp(_lower_bounds, l_ref[0:1, :], l_ref[1:2, :])
        g0, g1 = vjp((d_ref[0:1, :], d_ref[1:2, :]))
        o_ref[0:1, :] = g0
        o_ref[1:2, :] = g1

    return pl.pallas_call(body, name="lb_bwd", out_shape=jax.ShapeDtypeStruct((2, D), F32))(logits, dlb)


ADA_PAD = 128


def _ada_fwd(c_pad, w_ada, b_sh):
    ns = w_ada.shape[2]

    def body(c_ref, w_ref, b_ref, o_ref):
        cv = c_ref[...]
        ca = (cv * _sig(cv)).astype(BF16)
        for l in range(2):
            res = jnp.dot(ca, w_ref[l].astype(BF16), preferred_element_type=F32)
            o_ref[:, l * ns:(l + 1) * ns] = res[0:NDEV, :] + b_ref[l:l + 1, :]

    return pl.pallas_call(body, name="ada_fwd", out_shape=jax.ShapeDtypeStruct((NDEV, 2 * ns), F32),
                          compiler_params=_cp())(c_pad, w_ada, b_sh)


def _ada_wgrad(c_pad_t, d_ada_sh):
    ns = d_ada_sh.shape[2]

    def body(c_ref, d_ref, o_ref):
        cv = c_ref[...]
        ca = (cv * _sig(cv)).astype(BF16)
        for l in range(2):
            o_ref[l] = jnp.dot(ca, d_ref[l].astype(BF16), preferred_element_type=F32)

    return pl.pallas_call(body, name="ada_wgrad", out_shape=jax.ShapeDtypeStruct((2, D, ns), F32),
                          compiler_params=_cp())(c_pad_t, d_ada_sh)


def _sum_devices(g):
    _, R, C = g.shape

    def body(g_ref, o_ref):
        acc = g_ref[0]
        for d in range(1, NDEV):
            acc = acc + g_ref[d]
        o_ref[...] = acc

    return pl.pallas_call(body, name="sum_devices", out_shape=jax.ShapeDtypeStruct((R, C), F32),
                          compiler_params=_cp())(g)


def _adamw(w, g, m, v, name, echo=False):
    R, C = w.shape
    tr = _row_tile(R, max(8, (1 << 19) // C))

    def body(w_ref, g_ref, m_ref, v_ref, d_ref, nm_ref, nv_ref, *g_out):
        d_ref[...], nm_ref[...], nv_ref[...] = _adamw_update(w_ref[...], g_ref[...], m_ref[...], v_ref[...])
        for o_ref in g_out:
            o_ref[...] = g_ref[...]

    tile = pl.BlockSpec((tr, C), lambda i: (i, 0))
    n_out = 4 if echo else 3
    return _call(body, name=name, grid=(R // tr,), in_specs=[tile] * 4, out_specs=[tile] * n_out,
                 out_shape=[jax.ShapeDtypeStruct((R, C), F32)] * n_out, sem=("parallel",), args=(w, g, m, v))


def _adamw_many(wgmv, name, steps=4):
    n = len(wgmv)

    def body(*refs):
        ins, outs = refs[:4 * n], refs[4 * n:]
        for a in range(n):
            w_ref, g_ref, m_ref, v_ref = ins[4 * a:4 * a + 4]
            d_ref, nm_ref, nv_ref, g_out = outs[4 * a:4 * a + 4]
            d_ref[...], nm_ref[...], nv_ref[...] = _adamw_update(w_ref[...], g_ref[...], m_ref[...], v_ref[...])
            g_out[...] = g_ref[...]

    def tile(t):
        return pl.BlockSpec((t.shape[0] // steps, t.shape[1]), lambda i: (i, 0))

    flat = [t for four in wgmv for t in four]
    outs = pl.pallas_call(
        body, name=name, grid=(steps,), in_specs=[tile(t) for t in flat],
        out_specs=[tile(four[0]) for four in wgmv for _ in range(4)],
        out_shape=[jax.ShapeDtypeStruct(four[0].shape, F32) for four in wgmv for _ in range(4)],
        compiler_params=_cp(("parallel",)))(*flat)
    return [outs[4 * a:4 * a + 4] for a in range(n)]


def _adamw_update(w, g, m, v):
    nm = B1 * m + (1.0 - B1) * g
    nv = B2 * v + (1.0 - B2) * (g * g)
    m_hat = nm / (1.0 - B1 ** STEP)
    v_hat = nv / (1.0 - B2 ** STEP)
    return -LR * (m_hat / (jnp.sqrt(v_hat) + AEPS) + WD * w), nm, nv


SMALL_KEYS = ("b_ada", "g_pre", "g_post", "lb_logits", "pool_w", "pool_scale", "hgrn_norm_g")


def _small_pieces(key):
    one = lambda i: slice(i, i + 1)
    if key == "b_ada":
        return [((one(l), slice(j * D, (j + 1) * D)), (one(3 * l + j), slice(0, D)))
                for l in range(2) for j in range(3)]
    if key in ("g_pre", "g_post", "lb_logits"):
        row0 = {"g_pre": 8, "g_post": 16, "lb_logits": 24}[key]
        return [((slice(0, 2), slice(0, D)), (slice(row0, row0 + 2), slice(0, D)))]
    if key == "pool_w":
        return [((l, g, pl.ds(k, 16, stride=8), slice(0, 128)),
                 (slice(32 + 64 * l + 16 * g, 48 + 64 * l + 16 * g), slice(128 * k, 128 * (k + 1))))
                for l in range(2) for g in range(GROUPS) for k in range(8)]
    width = {"pool_scale": POOL_W, "hgrn_norm_g": HD}[key]
    row = {"pool_scale": 160, "hgrn_norm_g": 168}[key]
    return [((one(l), slice(0, width)), (one(row), slice(l * width, (l + 1) * width))) for l in range(2)]


def _adamw_small(g_small, g_lb_logits, wmv):
    n = len(SMALL_KEYS)

    def body(g_ref, glb_ref, *refs):
        ins, outs = refs[:3 * n], refs[3 * n:]
        for p, key in enumerate(SMALL_KEYS):
            w_ref, m_ref, v_ref = ins[3 * p:3 * p + 3]
            for at, (rows, lanes) in _small_pieces(key):
                gv = glb_ref[at] if key == "lb_logits" else g_ref[rows, lanes]
                res = _adamw_update(w_ref[at], gv, m_ref[at], v_ref[at])
                for o_ref, val in zip(outs[4 * p:4 * p + 4], (*res, gv)):
                    o_ref[at] = val

    flat = [t for key in SMALL_KEYS for t in wmv[key]]
    outs = pl.pallas_call(body, name="adamw_small",
                          out_shape=[jax.ShapeDtypeStruct(wmv[key][0].shape, F32) for key in SMALL_KEYS
                                     for _ in range(4)],
                          compiler_params=_cp())(g_small, g_lb_logits, *flat)
    return {key: outs[4 * p:4 * p + 4] for p, key in enumerate(SMALL_KEYS)}


def _cast_to_slot(place, w, l, name):
    _, R, C = w.shape
    tr = _row_tile(R, max(8, (1 << 19) // C))

    def body(p_ref, w_ref, o_ref):
        o_ref[...] = w_ref[...].astype(BF16)

    return pl.pallas_call(
        body, name=name, out_shape=jax.ShapeDtypeStruct((NCHIP, R, C), BF16),
        grid_spec=pltpu.PrefetchScalarGridSpec(
            num_scalar_prefetch=1, grid=(R // tr,),
            in_specs=[pl.BlockSpec((None, tr, C), lambda i, p_ref: (l, i, 0))],
            out_specs=pl.BlockSpec((None, tr, C), lambda i, p_ref: (p_ref[0], i, 0))),
        compiler_params=_cp(("parallel",)),
    )(place, w)


def _cast_to_slots(place, ws, name):
    n = len(ws)

    def body(p_ref, *refs):
        for w_ref, o_ref in zip(refs[:n], refs[n:]):
            o_ref[...] = w_ref[...].astype(BF16)

    def layer(l):
        return lambda i, p_ref: (l, 0, 0)

    return pl.pallas_call(
        body, name=name, out_shape=[jax.ShapeDtypeStruct((NCHIP,) + w.shape[1:], BF16) for w, _ in ws],
        grid_spec=pltpu.PrefetchScalarGridSpec(
            num_scalar_prefetch=1, grid=(1,),
            in_specs=[pl.BlockSpec((None,) + w.shape[1:], layer(l)) for w, l in ws],
            out_specs=[pl.BlockSpec((None,) + w.shape[1:], lambda i, p_ref: (p_ref[0], 0, 0)) for w, _ in ws]),
        compiler_params=_cp(("arbitrary",)),
    )(place, *[w for w, _ in ws])


def _pair_adds(core, gs, gots, name):
    n = len(gs)

    def body(c_ref, *refs):
        for a_ref, b_ref, o_ref in zip(refs[:n], refs[n:2 * n], refs[2 * n:]):
            o_ref[...] = (a_ref[...].astype(F32) + b_ref[...].astype(F32)).astype(o_ref.dtype)

    def whole(t):
        return pl.BlockSpec(t.shape, lambda i, c_ref: (0, 0, 0))

    return pl.pallas_call(
        body, name=name, out_shape=[jax.ShapeDtypeStruct(t.shape, BF16) for t in gots],
        grid_spec=pltpu.PrefetchScalarGridSpec(
            num_scalar_prefetch=1, grid=(1,),
            in_specs=[pl.BlockSpec(t.shape, lambda i, c_ref: (0, c_ref[0], 0)) for t in gots]
            + [whole(t) for t in gots],
            out_specs=[whole(t) for t in gots]),
        compiler_params=_cp(("arbitrary",)),
    )(core, *gs, *gots)


def _pair_add(core, g, got, name):
    _, R, C = g.shape
    r2 = R // 2
    tr = _row_tile(r2, max(8, (1 << 19) // C))
    nt = r2 // tr

    def body(c_ref, a_ref, b_ref, o_ref):
        o_ref[...] = (a_ref[...].astype(F32) + b_ref[...].astype(F32)).astype(o_ref.dtype)

    return pl.pallas_call(
        body, name=name, out_shape=jax.ShapeDtypeStruct((NCHIP, r2, C), BF16),
        grid_spec=pltpu.PrefetchScalarGridSpec(
            num_scalar_prefetch=1, grid=(NCHIP, nt),
            in_specs=[pl.BlockSpec((None, tr, C), lambda j, i, c_ref: (j, c_ref[0] * nt + i, 0)),
                      pl.BlockSpec((None, tr, C), lambda j, i, c_ref: (j, i, 0))],
            out_specs=pl.BlockSpec((None, tr, C), lambda j, i, c_ref: (j, i, 0))),
        compiler_params=_cp(("parallel", "parallel")),
    )(core, g, got)


def _sum_in_chip_order(me, own_ref, r_ref):
    own = own_ref[...].astype(F32)
    acc = None
    for j in range(NCHIP):
        slot = jnp.minimum(jnp.where(j > me, j - 1, j), NCHIP - 2)
        term = jnp.where(me == j, own, r_ref[slot].astype(F32))
        acc = term if acc is None else acc + term
    return acc


def _chip_sums(place, parts, recvs, name):
    n = len(parts)

    def body(p_ref, *refs):
        for a in range(n):
            for l in range(2):
                refs[4 * n + a][l] = _sum_in_chip_order(p_ref[0], refs[2 * a + l], refs[2 * n + 2 * a + l])

    def own(t):
        return pl.BlockSpec((None,) + t.shape[1:], lambda i, p_ref: (p_ref[0], 0, 0))

    def whole(t):
        return pl.BlockSpec(t.shape, lambda i, p_ref: (0, 0, 0))

    flat_p = [t for pair in parts for t in pair]
    flat_r = [t for pair in recvs for t in pair]
    return pl.pallas_call(
        body, name=name,
        out_shape=[jax.ShapeDtypeStruct((2, 2 * p[0].shape[1], p[0].shape[2]), F32) for p in parts],
        grid_spec=pltpu.PrefetchScalarGridSpec(
            num_scalar_prefetch=1, grid=(1,),
            in_specs=[own(t) for t in flat_p] + [whole(t) for t in flat_r],
            out_specs=[pl.BlockSpec((2,) + p[0].shape[1:], lambda i, p_ref: (0, p_ref[1], 0)) for p in parts]),
        compiler_params=_cp(("arbitrary",)),
    )(place, *flat_p, *flat_r)


def _chip_sum(place, part, recv, layer, both, name):
    _, r2, C = part.shape
    tr = _row_tile(r2, max(8, (1 << 18) // C))
    nt = r2 // tr

    def body(p_ref, own_ref, r_ref, *rest):
        rest[-1][...] = _sum_in_chip_order(p_ref[0], own_ref, r_ref)

    args = (place, part, recv) if both is None else (place, part, recv, both)
    return pl.pallas_call(
        body, name=name, out_shape=jax.ShapeDtypeStruct((2, 2 * r2, C), F32),
        grid_spec=pltpu.PrefetchScalarGridSpec(
            num_scalar_prefetch=1, grid=(nt,),
            in_specs=[pl.BlockSpec((None, tr, C), lambda i, p_ref: (p_ref[0], i, 0)),
                      pl.BlockSpec((NCHIP - 1, tr, C), lambda i, p_ref: (0, i, 0))] + [ANY] * (len(args) - 3),
            out_specs=pl.BlockSpec((None, tr, C), lambda i, p_ref: (layer, p_ref[1] * nt + i, 0))),
        input_output_aliases={} if both is None else {3: 0},
        compiler_params=_cp(("parallel",)),
    )(*args)


def _place():
    x, y, c = lax.axis_index("x"), lax.axis_index("y"), lax.axis_index("c")
    chips = [(1 - x, y), (x, 1 - y), (1 - x, 1 - y)]
    return x, y, c, chips


def _gather_small(blk, name):
    m_per, n = blk.shape

    def body(x_ref, out_ref, send_sems, recv_sems, local_sem):
        x, y, c, chips = _place()
        me, sibling = (x, y, c), (x, y, 1 - c)

        def rows(px, py, pc):
            return out_ref.at[pl.ds((4 * px + 2 * py + pc) * m_per, m_per), :]

        def copy(k, block, to, src=None):
            return pltpu.make_async_remote_copy(
                src_ref=rows(*block) if src is None else src, dst_ref=rows(*block),
                send_sem=send_sems.at[k], recv_sem=recv_sems.at[k], device_id=to, device_id_type=MESH)

        mine = pltpu.make_async_copy(x_ref, rows(*me), local_sem)
        mine.start()
        first = [copy(0, me, sibling, src=x_ref)]
        first += [copy(1 + j, me, (*chip, c), src=x_ref) for j, chip in enumerate(chips)]
        for cp in first:
            cp.start()
        passed = [copy(4 + j, (*chip, c), sibling) for j, chip in enumerate(chips)]
        for j, chip in enumerate(chips):
            copy(1 + j, (*chip, c), me).wait_recv()
            passed[j].start()
        copy(0, sibling, me).wait_recv()
        for j, chip in enumerate(chips):
            copy(4 + j, (*chip, 1 - c), me).wait_recv()
        for cp in first + passed:
            cp.wait_send()
        mine.wait()

    return pl.pallas_call(
        body, name=name, out_shape=jax.ShapeDtypeStruct((NDEV * m_per, n), blk.dtype),
        in_specs=[pl.BlockSpec(memory_space=pltpu.VMEM)], out_specs=pl.BlockSpec(memory_space=pltpu.VMEM),
        scratch_shapes=[pltpu.SemaphoreType.DMA((7,)), pltpu.SemaphoreType.DMA((7,)), pltpu.SemaphoreType.DMA],
        compiler_params=_cp(),
    )(blk)


def _gather_rows_carry(blk):
    m_per, n = blk.shape

    def rows(ref, px, py, pc):
        return ref.at[pl.ds((4 * px + 2 * py + pc) * m_per, m_per), :]

    def copy(ins, outs, send_sems, recv_sems, k, block, to, own=False):
        return pltpu.make_async_remote_copy(
            src_ref=ins[0] if own else rows(outs[0], *block), dst_ref=rows(outs[0], *block),
            send_sem=send_sems.at[k], recv_sem=recv_sems.at[k], device_id=to, device_id_type=MESH)

    def mine(ins, outs, send_sems):
        x, y, c, _ = _place()
        return pltpu.make_async_copy(ins[0], rows(outs[0], x, y, c), send_sems.at[7])

    def start(ins, outs, send_sems, recv_sems):
        x, y, c, chips = _place()
        mine(ins, outs, send_sems).start()
        copy(ins, outs, send_sems, recv_sems, 0, (x, y, c), (x, y, 1 - c), own=True).start()
        for j, chip in enumerate(chips):
            copy(ins, outs, send_sems, recv_sems, 1 + j, (x, y, c), (*chip, c), own=True).start()

    def finish(ins, outs, send_sems, recv_sems):
        x, y, c, chips = _place()
        for j, chip in enumerate(chips):
            copy(ins, outs, send_sems, recv_sems, 1 + j, (*chip, c), (x, y, c)).wait_recv()
            copy(ins, outs, send_sems, recv_sems, 4 + j, (*chip, c), (x, y, 1 - c)).start()
        copy(ins, outs, send_sems, recv_sems, 0, (x, y, 1 - c), (x, y, c)).wait_recv()
        for j, chip in enumerate(chips):
            copy(ins, outs, send_sems, recv_sems, 4 + j, (*chip, 1 - c), (x, y, c)).wait_recv()
        copy(ins, outs, send_sems, recv_sems, 0, (x, y, c), (x, y, 1 - c), own=True).wait_send()
        for j, chip in enumerate(chips):
            copy(ins, outs, send_sems, recv_sems, 1 + j, (x, y, c), (*chip, c), own=True).wait_send()
            copy(ins, outs, send_sems, recv_sems, 4 + j, (*chip, c), (x, y, 1 - c)).wait_send()
        mine(ins, outs, send_sems).wait()

    return _Carry([blk], [jax.ShapeDtypeStruct((NDEV * m_per, n), blk.dtype)], {}, 8, start, finish)


def _gather_carry(shards, piece=(0, 1, 1)):
    n = len(shards)
    first, count, of = piece

    def rows(ref, half):
        r2 = ref.shape[1] // 2
        return pl.ds(half * r2 + first * (r2 // of), count * (r2 // of))

    def over_ici(outs, send_sems, recv_sems, a, j, chip_xy, slot):
        x, y, c, _ = _place()
        blk = outs[a].at[slot, rows(outs[a], c), :]
        return pltpu.make_async_remote_copy(
            src_ref=blk, dst_ref=blk, send_sem=send_sems.at[6 * a + j], recv_sem=recv_sems.at[6 * a + j],
            device_id=(*chip_xy, c), device_id_type=MESH)

    def over_d2d(outs, send_sems, recv_sems, a, j, slot, half):
        x, y, c, _ = _place()
        blk = outs[a].at[slot, rows(outs[a], half), :]
        return pltpu.make_async_remote_copy(
            src_ref=blk, dst_ref=blk, send_sem=send_sems.at[6 * a + 3 + j], recv_sem=recv_sems.at[6 * a + 3 + j],
            device_id=(x, y, 1 - c), device_id_type=MESH)

    def start(ins, outs, send_sems, recv_sems):
        x, y, c, chips = _place()
        for a in range(n):
            for j, chip_xy in enumerate(chips):
                over_ici(outs, send_sems, recv_sems, a, j, chip_xy, 2 * x + y).start()

    def finish(ins, outs, send_sems, recv_sems):
        x, y, c, chips = _place()
        for a in range(n):
            for j, (cx, cy) in enumerate(chips):
                over_ici(outs, send_sems, recv_sems, a, j, (cx, cy), 2 * cx + cy).wait_recv()
                over_d2d(outs, send_sems, recv_sems, a, j, 2 * cx + cy, c).start()
        for a in range(n):
            for j, (cx, cy) in enumerate(chips):
                over_d2d(outs, send_sems, recv_sems, a, j, 2 * cx + cy, 1 - c).wait_recv()
        for a in range(n):
            for j, (cx, cy) in enumerate(chips):
                over_ici(outs, send_sems, recv_sems, a, j, (cx, cy), 2 * x + y).wait_send()
                over_d2d(outs, send_sems, recv_sems, a, j, 2 * cx + cy, c).wait_send()

    return _Carry(shards, [jax.ShapeDtypeStruct(s.shape, s.dtype) for s in shards],
                  {a: a for a in range(n)}, 6 * n, start, finish)


def _rs_pair(grads, name):
    n = len(grads)

    def body(*refs):
        ins, gots = refs[:n], refs[n:2 * n]
        send_sems, recv_sems = refs[2 * n:]
        x, y, c, _ = _place()
        cps = []
        for a in range(n):
            r2 = ins[a].shape[1] // 2
            cp = pltpu.make_async_remote_copy(
                src_ref=ins[a].at[:, pl.ds((1 - c) * r2, r2), :], dst_ref=gots[a],
                send_sem=send_sems.at[a], recv_sem=recv_sems.at[a],
                device_id=(x, y, 1 - c), device_id_type=MESH)
            cp.start()
            cps.append(cp)
        for cp in cps:
            cp.wait()

    half = [jax.ShapeDtypeStruct((NCHIP, g.shape[1] // 2, g.shape[2]), g.dtype) for g in grads]
    return pl.pallas_call(
        body, name=name, out_shape=half, in_specs=[ANY] * n, out_specs=[ANY] * n,
        scratch_shapes=[pltpu.SemaphoreType.DMA((n,)), pltpu.SemaphoreType.DMA((n,))],
        compiler_params=_cp(),
    )(*grads)


def _pair_sum(core, g, name):
    _, R, C = g.shape
    r2 = R // 2

    def body(c_ref, g_ref, own_ref, o_ref, got_ref, buf, send_sems, recv_sems, local_sem):
        j = pl.program_id(0)
        x, y, c, _ = _place()

        def remote(k):
            return pltpu.make_async_remote_copy(
                src_ref=g_ref.at[k, pl.ds((1 - c) * r2, r2), :], dst_ref=got_ref.at[k],
                send_sem=send_sems.at[k], recv_sem=recv_sems.at[k], device_id=(x, y, 1 - c), device_id_type=MESH)

        @pl.when(j == 0)
        def _():
            for k in range(NCHIP):
                remote(k).start()

        remote(j).wait_recv()
        landed = pltpu.make_async_copy(got_ref.at[j], buf, local_sem)
        landed.start()
        landed.wait()
        o_ref[...] = (own_ref[...].astype(F32) + buf[...].astype(F32)).astype(o_ref.dtype)

        @pl.when(j == NCHIP - 1)
        def _():
            for k in range(NCHIP):
                remote(k).wait_send()

    half = jax.ShapeDtypeStruct((NCHIP, r2, C), g.dtype)
    return pl.pallas_call(
        body, name=name, out_shape=[half, half],
        grid_spec=pltpu.PrefetchScalarGridSpec(
            num_scalar_prefetch=1, grid=(NCHIP,),
            in_specs=[ANY, pl.BlockSpec((None, r2, C), lambda j, c_ref: (j, c_ref[0], 0))],
            out_specs=[pl.BlockSpec((None, r2, C), lambda j, c_ref: (j, 0, 0)), ANY],
            scratch_shapes=[pltpu.VMEM((r2, C), g.dtype), pltpu.SemaphoreType.DMA((NCHIP,)),
                            pltpu.SemaphoreType.DMA((NCHIP,)), pltpu.SemaphoreType.DMA]),
        compiler_params=_cp(("arbitrary",)),
    )(core, g, g)[0]


def _pair_carry(grads):
    n = len(grads)

    def copy(ins, outs, send_sems, recv_sems, a):
        x, y, c, _ = _place()
        r2 = ins[a].shape[1] // 2
        return pltpu.make_async_remote_copy(
            src_ref=ins[a].at[:, pl.ds((1 - c) * r2, r2), :], dst_ref=outs[a],
            send_sem=send_sems.at[a], recv_sem=recv_sems.at[a],
            device_id=(x, y, 1 - c), device_id_type=MESH)

    def start(ins, outs, send_sems, recv_sems):
        for a in range(n):
            copy(ins, outs, send_sems, recv_sems, a).start()

    def finish(ins, outs, send_sems, recv_sems):
        for a in range(n):
            copy(ins, outs, send_sems, recv_sems, a).wait()

    half = [jax.ShapeDtypeStruct((NCHIP, g.shape[1] // 2, g.shape[2]), g.dtype) for g in grads]
    return _Carry(grads, half, {}, n, start, finish)


def _chips_carry(parts, piece=(0, 1, 1), into=None):
    n = len(parts)
    first, count, of = piece

    def rows(ref):
        step = ref.shape[1] // of
        return pl.ds(first * step, count * step)

    def send(ins, outs, send_sems, recv_sems, a, j, chip_xy):
        x, y, c, _ = _place()
        me, them = 2 * x + y, 2 * chip_xy[0] + chip_xy[1]
        return pltpu.make_async_remote_copy(
            src_ref=ins[a].at[them, rows(ins[a]), :],
            dst_ref=outs[a].at[me - (me > them).astype(jnp.int32), rows(outs[a]), :],
            send_sem=send_sems.at[3 * a + j], recv_sem=recv_sems.at[3 * a + j],
            device_id=(*chip_xy, c), device_id_type=MESH)

    def start(ins, outs, send_sems, recv_sems):
        _, _, _, chips = _place()
        for a in range(n):
            for j, chip_xy in enumerate(chips):
                send(ins, outs, send_sems, recv_sems, a, j, chip_xy).start()

    def finish(ins, outs, send_sems, recv_sems):
        x, y, c, chips = _place()
        me = 2 * x + y
        for a in range(n):
            for j, (cx, cy) in enumerate(chips):
                them = 2 * cx + cy
                blk = outs[a].at[them - (them > me).astype(jnp.int32), rows(outs[a]), :]
                pltpu.make_async_remote_copy(
                    src_ref=blk, dst_ref=blk, send_sem=send_sems.at[3 * a + j], recv_sem=recv_sems.at[3 * a + j],
                    device_id=(cx, cy, c), device_id_type=MESH).wait_recv()
        for a in range(n):
            for j, chip_xy in enumerate(chips):
                send(ins, outs, send_sems, recv_sems, a, j, chip_xy).wait_send()

    landing = [jax.ShapeDtypeStruct((NCHIP - 1,) + p.shape[1:], p.dtype) for p in parts]
    if into is None:
        return _Carry(parts, landing, {}, 3 * n, start, finish)
    return _Carry(list(parts) + list(into), landing, {n + a: a for a in range(n)}, 3 * n, start, finish)


def _swap_carry(fulls, layers):
    n = len(fulls)

    def copy(outs, send_sems, recv_sems, a, half):
        x, y, c, _ = _place()
        r2 = outs[a].shape[1] // 2
        blk = outs[a].at[pl.ds(*layers[a]), pl.ds(half * r2, r2), :]
        return pltpu.make_async_remote_copy(
            src_ref=blk, dst_ref=blk, send_sem=send_sems.at[a], recv_sem=recv_sems.at[a],
            device_id=(x, y, 1 - c), device_id_type=MESH)

    def start(ins, outs, send_sems, recv_sems):
        c = lax.axis_index("c")
        for a in range(n):
            copy(outs, send_sems, recv_sems, a, c).start()

    def finish(ins, outs, send_sems, recv_sems):
        c = lax.axis_index("c")
        for a in range(n):
            copy(outs, send_sems, recv_sems, a, 1 - c).wait_recv()
        for a in range(n):
            copy(outs, send_sems, recv_sems, a, c).wait_send()

    return _Carry(fulls, [jax.ShapeDtypeStruct(f.shape, f.dtype) for f in fulls], {a: a for a in range(n)}, n,
                  start, finish)


def _rs_swap(fulls, layers):
    n = len(fulls)
    swap = _swap_carry(fulls, layers)

    def body(*refs):
        outs, (send_sems, recv_sems) = refs[n:2 * n], refs[2 * n:]
        swap.start(None, outs, send_sems, recv_sems)
        swap.finish(None, outs, send_sems, recv_sems)

    return pl.pallas_call(
        body, name="rs_swap", out_shape=swap.outs, in_specs=[ANY] * n, out_specs=[ANY] * n,
        input_output_aliases=swap.aliases,
        scratch_shapes=[pltpu.SemaphoreType.DMA((n,)), pltpu.SemaphoreType.DMA((n,))],
        compiler_params=_cp(),
    )(*fulls)


def _tail_weight_grads(merged_t, b_in_t, a_in_t, dy, dbr_b, dbr_a, name, tn=256):
    S = dy.shape[0]
    nn = D // tn

    def body(mt_ref, bt_ref, at_ref, dy_ref, db_ref, da_ref, go_ref, gh_ref, gp_ref):
        go_ref[...] = jnp.dot(mt_ref[...], dy_ref[...], preferred_element_type=F32).astype(BF16)
        gh_ref[...] = jnp.dot(bt_ref[...], db_ref[...], preferred_element_type=F32).astype(BF16)
        gp_ref[...] = jnp.dot(at_ref[...], da_ref[...], preferred_element_type=F32).astype(BF16)

    left = lambda rows: pl.BlockSpec((rows, S), lambda n: (0, 0))
    right = pl.BlockSpec((S, tn), lambda n: (0, n))
    out = pl.BlockSpec((D, tn), lambda n: (0, n))
    return pl.pallas_call(
        body, name=name, grid=(nn,), in_specs=[left(D), left(D), left(POOL_W), right, right, right],
        out_specs=[out, out, pl.BlockSpec((None, POOL_W, tn), lambda n: (n, 0, 0))],
        out_shape=[jax.ShapeDtypeStruct((D, D), BF16), jax.ShapeDtypeStruct((D, D), BF16),
                   jax.ShapeDtypeStruct((NCHIP, POOL_W, D // NCHIP), BF16)],
        compiler_params=_cp(("parallel",)),
    )(merged_t, b_in_t, a_in_t, dy, dbr_b, dbr_a)


class _GatherInProj:
    def __init__(self, slot, order):
        self.slot, self.order = slot, order


def _proj_with_gather(h, w_slot, order, name, tn=256):
    S, K = h.shape
    nsh, _, ns = w_slot.shape
    tps = ns // tn
    nt = nsh * tps
    r2 = K // 2

    def body(ord_ref, h_ref, w_in_ref, o_ref, w_ref, wbuf, tile_sems, send_sems, recv_sems):
        n = pl.program_id(0)
        x, y, c, chips = _place()

        def half(slot, which):
            return w_ref.at[slot, pl.ds(which * r2, r2), :]

        def over_ici(j, slot):
            blk = half(slot, c)
            return pltpu.make_async_remote_copy(src_ref=blk, dst_ref=blk, send_sem=send_sems.at[j],
                                                recv_sem=recv_sems.at[j], device_id=(*chips[j], c),
                                                device_id_type=MESH)

        def over_d2d(j, which):
            blk = half(2 * chips[j][0] + chips[j][1], which)
            return pltpu.make_async_remote_copy(src_ref=blk, dst_ref=blk, send_sem=send_sems.at[3 + j],
                                                recv_sem=recv_sems.at[3 + j], device_id=(x, y, 1 - c),
                                                device_id_type=MESH)

        def tile_copy(step, slot):
            shard = ord_ref[step // tps]
            return pltpu.make_async_copy(w_ref.at[shard, :, pl.ds((step % tps) * tn, tn)], wbuf.at[slot],
                                         tile_sems.at[slot])

        @pl.when(n == 0)
        def _():
            for j in range(3):
                over_ici(j, 2 * x + y).start()
            tile_copy(0, 0).start()

        for j in range(3):
            @pl.when(n == (j + 1) * tps - 1)
            def _(j=j):
                over_ici(j, 2 * chips[j][0] + chips[j][1]).wait_recv()
                over_d2d(j, c).start()
                over_d2d(j, 1 - c).wait_recv()

        @pl.when(n + 1 < nt)
        def _():
            tile_copy(n + 1, (n + 1) % 2).start()

        tile_copy(n, n % 2).wait()
        o_ref[...] = jnp.dot(h_ref[...], wbuf[n % 2], preferred_element_type=F32).astype(o_ref.dtype)

        @pl.when(n == nt - 1)
        def _():
            for j in range(3):
                over_ici(j, 2 * x + y).wait_send()
                over_d2d(j, c).wait_send()

    return pl.pallas_call(
        body, name=name,
        out_shape=[jax.ShapeDtypeStruct((S, nsh * ns), BF16), jax.ShapeDtypeStruct(w_slot.shape, w_slot.dtype)],
        grid_spec=pltpu.PrefetchScalarGridSpec(
            num_scalar_prefetch=1, grid=(nt,),
            in_specs=[pl.BlockSpec((S, K), lambda n, o_ref: (0, 0)), ANY],
            out_specs=[pl.BlockSpec((S, tn), lambda n, o_ref: (0, o_ref[n // tps] * tps + n % tps)), ANY],
            scratch_shapes=[pltpu.VMEM((2, K, tn), w_slot.dtype), pltpu.SemaphoreType.DMA((2,)),
                            pltpu.SemaphoreType.DMA((6,)), pltpu.SemaphoreType.DMA((6,))]),
        input_output_aliases={2: 1},
        compiler_params=_cp(("arbitrary",)),
    )(order, h, w_slot)


def _mm_ride(a, b, carry, **kw):
    if carry is None:
        return _mm(a, b, **kw), []
    return _mm(a, b, carry=carry, **kw)


def _layer_fwd(l, x, ada, w, small, ride, target=None):
    shift, scale, gate = ada[:, 0:D], ada[:, D:2 * D], ada[:, 2 * D:3 * D]
    carry, landed = ride("prenorm")
    (h, h_t), outs = _prenorm_fwd(x, small["g_pre"][l], scale, shift, f"prenorm_fwd{l}", carry)
    landed(outs)
    carry, landed = ride("proj")
    if isinstance(carry, _GatherInProj):
        proj, full = _proj_with_gather(h, carry.slot, carry.order, f"proj{l}")
        outs = [full]
    else:
        proj, outs = _mm_ride(h, w["w_in"][l], carry, name=f"proj{l}", b_mode="nn_sh", tm=2048, out_dtype=BF16)
    landed(outs)
    a_in, a_in_t = _pool_fwd(proj, small["pool_w"][l], small["pool_scale"][l], f"pool_fwd{l}")
    carry, landed = ride("hgrn")
    (b_in, b_in_t, o_raw, states, mild, cum), outs = _hgrn_fwd(proj, small["lb"][l], small["hgrn_norm_g"][l],
                                                              f"hgrn_fwd{l}", carry=carry)
    landed(outs)
    carry, landed = ride("tail")
    (br_a, br_b, merged_t, y, *x_new), outs = _layer_tail_fwd(
        proj, a_in, b_in, x, w["w_pool_o"][l], w["w_hgrn_o"][l].reshape(D, D), w["w_out"][l].reshape(D, D),
        gate, small["g_post"][l], f"tail_fwd{l}", target=target, carry=carry)
    landed(outs)
    saved = dict(x=x, h_t=h_t, proj=proj, a_in_t=a_in_t, b_in_t=b_in_t, o_raw=o_raw, states=states, mild=mild,
                 cum=cum,
                 br_a=br_a, br_b=br_b, merged_t=merged_t, y=y, scale=scale, gate=gate)
    return x_new, saved


def _layer_bwd(l, dxn, sv, w, small, ride):
    carry, landed = ride["head"](None)
    (dy, dbr_a, dbr_b, dproj, da_in, db_in, dgate, dg_post), outs = _layer_head_bwd(
        dxn, sv["y"], sv["proj"], sv["br_a"], sv["br_b"], w["w_pool_o"][l], w["w_hgrn_o"][l].reshape(D, D),
        w["w_out"][l].reshape(D, D), sv["gate"], small["g_post"][l], f"head_bwd{l}", carry)
    landed(outs)
    gw_out, gw_hgrn_o, gw_pool_o = _tail_weight_grads(sv["merged_t"], sv["b_in_t"], sv["a_in_t"], dy, dbr_b,
                                                      dbr_a, f"gw_tail{l}")
    big = dict(w_pool_o=gw_pool_o, w_hgrn_o=gw_hgrn_o.reshape(NCHIP, D // NCHIP, D),
               w_out=gw_out.reshape(NCHIP, D // NCHIP, D))
    carry, landed = ride["hgrn"](big)
    (dproj, dlb, dgn), outs = _hgrn_bwd(db_in, sv["proj"], sv["o_raw"], sv["states"], sv["mild"], sv["cum"],
                                        small["lb"][l], small["hgrn_norm_g"][l], dproj, f"hgrn_bwd{l}",
                                        carry=carry)
    landed(outs)
    dproj, dpw, dpsc = _pool_bwd(da_in, sv["proj"], small["pool_w"][l], small["pool_scale"][l], dproj,
                                 f"pool_bwd{l}")
    little = dict(dgate=dgate, g_post=dg_post, pool_w=dpw, pool_scale=dpsc, lb=dlb,
                  hgrn_norm_g=jnp.sum(dgn, axis=0, keepdims=True))
    carry, landed = ride["gw_in"](little)
    big["w_in"], outs = _mm_ride(sv["h_t"], dproj, carry, name=f"gw_in{l}", out_shards=NCHIP, out_dtype=BF16)
    landed(outs)
    carry, landed = ride["d_h"](big)
    dh, outs = _mm_ride(dproj, w["w_in"][l], carry, name=f"d_h{l}", b_mode="nt_shk", tn=1024)
    landed(outs)
    carry, landed = ride["prenorm"](big)
    (dx, dshift, dscale, dg_pre), outs = _prenorm_bwd(dh, dxn, sv["x"], small["g_pre"][l], sv["scale"],
                                                      f"prenorm_bwd{l}", carry)
    landed(outs)
    little.update(dshift=dshift, dscale=dscale, g_pre=dg_pre)
    return dx, big, little


SMALL_ROWS = 176


def _rows8(t):
    t = t.reshape(-1, D)
    return jnp.pad(t, ((0, -t.shape[0] % 8), (0, 0)))


def _pack_small(parts):
    row_keys = ("dshift", "dscale", "dgate", "g_pre", "g_post", "lb", "pool_scale", "hgrn_norm_g")
    flat = [p[k] for p in parts for k in row_keys] + [p["pool_w"] for p in parts]
    nk = len(row_keys)

    def body(*refs):
        o_ref = refs[-1]
        o_ref[...] = jnp.zeros((SMALL_ROWS, D), F32)
        for l in range(2):
            dshift, dscale, dgate, g_pre, g_post, lb, pscale, gn = refs[l * nk:(l + 1) * nk]
            for r, ref in enumerate((dshift, dscale, dgate)):
                o_ref[3 * l + r:3 * l + r + 1, :] = ref[...]
            o_ref[8 + l:9 + l, :] = g_pre[...]
            o_ref[16 + l:17 + l, :] = g_post[...]
            o_ref[24 + l:25 + l, :] = lb[...]
            o_ref[160:161, l * POOL_W:(l + 1) * POOL_W] = pscale[...]
            o_ref[168:169, l * HD:(l + 1) * HD] = gn[...]
        for (l, *at), (rows, lanes) in _small_pieces("pool_w"):
            o_ref[rows, lanes] = refs[2 * nk + l][tuple(at)]

    return pl.pallas_call(body, name="pack_small", out_shape=jax.ShapeDtypeStruct((SMALL_ROWS, D), F32),
                          compiler_params=_cp())(*flat)


def kernel(x, c, w_ada, b_ada, g_pre, g_post, w_in, pool_w, pool_scale, lb_logits, hgrn_norm_g, w_pool_o, w_hgrn_o, w_out, loss_target, m_w_ada, m_b_ada, m_g_pre, m_g_post, m_w_in, m_pool_w, m_pool_scale, m_lb_logits, m_hgrn_norm_g, m_w_pool_o, m_w_hgrn_o, m_w_out, v_w_ada, v_b_ada, v_g_pre, v_g_post, v_w_in, v_pool_w, v_pool_scale, v_lb_logits, v_hgrn_norm_g, v_w_pool_o, v_w_hgrn_o, v_w_out):
    ax, ay, ac = lax.axis_index("x"), lax.axis_index("y"), lax.axis_index("c")
    chip = 2 * ax + ay
    dev = 2 * chip + ac
    xe, te = x[0], loss_target[0]
    ada_s = w_ada.shape[2]

    big_names = ("w_in", "w_pool_o", "w_hgrn_o", "w_out")
    big_w = (w_in, w_pool_o, w_hgrn_o, w_out)
    core = jnp.stack([ac]).astype(jnp.int32)
    place = jnp.stack([chip, ac]).astype(jnp.int32)
    slots = {("w_in", l): _cast_to_slot(place, w_in, l, f"cast_w_in{l}") for l in range(2)}
    rest = [(k, l) for l in range(2) for k in big_names[1:]]
    slots.update(zip(rest, _cast_to_slots(place, [(dict(zip(big_names, big_w))[k], l) for k, l in rest],
                                          "cast_rest")))
    w = {k: [None, None] for k in big_names}
    def fills(keys):
        def landed(outs):
            for (k, l), o in zip(keys, outs):
                w[k][l] = slots[k, l] = o
        return landed

    rest0 = [(k, 0) for k in big_names[1:]]
    rest1 = [(k, 1) for k in big_names[1:]]
    no_carry = (None, lambda outs: None)
    order = jnp.stack([chip, 2 * (1 - ax) + ay, 2 * ax + (1 - ay), 2 * (1 - ax) + (1 - ay)]).astype(jnp.int32)

    def ride_fwd0(stage):
        if stage == "proj":
            return _GatherInProj(slots["w_in", 0], order), fills([("w_in", 0)])
        if stage == "hgrn":
            return (_join_carries(_gather_carry([slots[t] for t in rest0]),
                                  _gather_carry([slots["w_in", 1]], piece=(0, 2, 4))),
                    fills(rest0 + [("w_in", 1)]))
        if stage == "tail":
            return _gather_carry([slots["w_in", 1]], piece=(2, 1, 4)), fills([("w_in", 1)])
        return no_carry

    def ride_fwd1(stage):
        if stage == "prenorm":
            return _gather_carry([slots["w_in", 1]], piece=(3, 1, 4)), fills([("w_in", 1)])
        if stage == "hgrn":
            return _gather_carry([slots[t] for t in rest1]), fills(rest1)
        return no_carry

    c_all = _gather_small(jnp.broadcast_to(c, (8, D)), "gather_c").reshape(NDEV, 8, D)[:, 0, :]
    c_pad = jnp.pad(c_all, ((0, ADA_PAD - NDEV), (0, 0)))
    b_sh = lax.dynamic_slice(b_ada, (0, chip * ada_s), (2, ada_s))
    ada_cols = _gather_small(_ada_fwd(c_pad, w_ada, b_sh), "gather_ada")
    ada_cols = ada_cols.reshape(NCHIP, 2, NDEV, 2, ada_s)[:, 0]
    ada_all = jnp.transpose(ada_cols, (2, 1, 0, 3)).reshape(2, NDEV, 3 * D)
    ada_me = lax.dynamic_slice(ada_all, (0, dev, 0), (2, 1, 3 * D))

    lbs = _lb_fwd(lb_logits)
    small = dict(g_pre=g_pre[:, None, :], g_post=g_post[:, None, :], pool_w=pool_w,
                 pool_scale=pool_scale[:, None, :], lb=lbs[:, None, :], hgrn_norm_g=hgrn_norm_g[:, None, :])

    (x1,), sv0 = _layer_fwd(0, xe, ada_me[0], w, small, ride_fwd0)
    (dx2, loss_blk), sv1 = _layer_fwd(1, x1, ada_me[1], w, small, ride_fwd1, target=te)

    parts, recv, held = {}, {}, {}

    def pair_ride(keys, grads):
        def landed(outs):
            held.update({kl: (g, o) for kl, g, o in zip(keys, grads, outs)})
        return _pair_carry(grads), landed

    def pair_adds(keys):
        gs, gots = zip(*[held.pop(kl) for kl in keys])
        if keys[0][0] == "w_in":
            parts[keys[0]] = _pair_add(core, gs[0], gots[0], f"rs_add_w_in{keys[0][1]}")
        else:
            parts.update(zip(keys, _pair_adds(core, gs, gots, f"rs_add_early{keys[0][1]}")))

    def exchange(keys):
        def landed(outs):
            recv.update(zip(keys, outs))
        return _chips_carry([parts[kl] for kl in keys]), landed

    def share(key, first, count):
        def landed(outs):
            (recv[key],) = outs
        into = [recv[key]] if key in recv else None
        return _chips_carry([parts[key]], piece=(first, count, 8), into=into), landed

    def together(*rides):
        carries, fns = zip(*rides)

        def landed(outs):
            for cr, fn in zip(carries, fns):
                fn(outs[:len(cr.outs)])
                outs = outs[len(cr.outs):]
        return _join_carries(*carries), landed

    def early(l):
        return [(k, l) for k in big_names[1:]]

    def pair_alone(keys, grads, tag):
        held.update({kl: (g, o) for kl, g, o in zip(keys, grads, _rs_pair(grads, f"rs_pair_{tag}"))})
        pair_adds(keys)

    def ride_hgrn1(big):
        return pair_ride(early(1), [big[k] for k in big_names[1:]])

    def ride_gw_in1(_):
        pair_adds(early(1))
        return exchange(early(1))

    def ride_d_h1(big):
        return pair_ride([("w_in", 1)], [big["w_in"]])

    def ride_prenorm1(_):
        pair_adds([("w_in", 1)])
        return share(("w_in", 1), 0, 1)

    def ride_head0(_):
        return share(("w_in", 1), 1, 3)

    def ride_hgrn0(big):
        pair_alone(early(0), [big[k] for k in big_names[1:]], "early0")
        return together(exchange(early(0)), share(("w_in", 1), 4, 4))

    def ride_d_h0(big):
        parts["w_in", 0] = _pair_sum(core, big["w_in"], "rs_pair_sum_w_in0")
        return share(("w_in", 0), 0, 4)

    def ride_prenorm0(_):
        return share(("w_in", 0), 4, 4)

    no_ride = lambda so_far: no_carry
    dx1, big1, little1 = _layer_bwd(1, dx2, sv1, w, small, dict(head=no_ride, hgrn=ride_hgrn1, gw_in=ride_gw_in1,
                                                                d_h=ride_d_h1, prenorm=ride_prenorm1))

    gathered = {}
    zero_row = jnp.zeros((1, D), F32)

    def ride_gw_in0(little):
        so_far = dict(little, dshift=zero_row, dscale=zero_row, g_pre=zero_row)

        def landed(outs):
            (gathered["early"],) = outs

        red = [_chip_sum(place, parts["w_in", 1], recv["w_in", 1], 1, None, "rs_sum_w_in1")]
        red += _chip_sums(place, [[parts[k, l] for l in range(2)] for k in big_names[1:]],
                          [[recv[k, l] for l in range(2)] for k in big_names[1:]], "rs_sum_early")

        def swapped(outs):
            gathered["sums"] = outs
        return together((_gather_rows_carry(_pack_small([so_far, little1])), landed),
                        (_swap_carry(red, [(1, 1)] + [(0, 2)] * 3), swapped))

    dx0, big0, little0 = _layer_bwd(0, dx1, sv0, w, small,
                                    dict(head=ride_head0, hgrn=ride_hgrn0, gw_in=ride_gw_in0, d_h=ride_d_h0,
                                         prenorm=ride_prenorm0))
    loss_row = jnp.broadcast_to(loss_blk[0:1, 0:1], (1, D))
    late = _rows8(jnp.stack([little0["dshift"], little0["dscale"], little0["g_pre"], loss_row]))
    late = _gather_small(late, "gather_small_late").reshape(NDEV, 8, D)
    loss = jnp.sum(late[:, 3, 0])
    packed = gathered["early"].reshape(NDEV, SMALL_ROWS, D)
    packed = packed.at[:, 0:2, :].set(late[:, 0:2, :]).at[:, 8:9, :].set(late[:, 2:3, :])
    red = _chip_sum(place, parts["w_in", 0], recv["w_in", 0], 0, gathered["sums"][0], "rs_sum_w_in0")
    g_big = dict(zip(big_names, list(_rs_swap([red], [(0, 1)])) + list(gathered["sums"][1:])))

    def two(t):
        return t.reshape(-1, t.shape[-1])

    def upd(wt, g, m, v, name, echo=False):
        return [t.reshape(wt.shape) for t in _adamw(two(wt), two(g), two(m), two(v), name, echo)]

    *u_w_in, g_w_in = upd(w_in, g_big["w_in"], m_w_in, v_w_in, "adamw_w_in", echo=True)
    g_small = _sum_devices(packed)
    g_lb_logits = _lb_bwd(lb_logits, g_small[24:26])
    d_ada_all = packed[:, 0:6, :].reshape(NDEV, 2, 3 * D)
    d_ada_sh = lax.dynamic_slice(jnp.transpose(d_ada_all, (1, 0, 2)), (0, 0, chip * ada_s), (2, NDEV, ada_s))
    d_ada_sh = jnp.pad(d_ada_sh, ((0, 0), (0, ADA_PAD - NDEV), (0, 0)))
    g_w_ada = _ada_wgrad(c_pad.T, d_ada_sh)

    u_w_ada = upd(w_ada, g_w_ada, m_w_ada, v_w_ada, "adamw_w_ada")
    early_w = dict(w_pool_o=(w_pool_o, m_w_pool_o, v_w_pool_o), w_hgrn_o=(w_hgrn_o, m_w_hgrn_o, v_w_hgrn_o),
                   w_out=(w_out, m_w_out, v_w_out))
    u_early = _adamw_many([(two(early_w[k][0]), two(g_big[k]), two(early_w[k][1]), two(early_w[k][2]))
                           for k in big_names[1:]], "adamw_early")
    (*u_w_pool_o, g_w_pool_o), (*u_w_hgrn_o, g_w_hgrn_o), (*u_w_out, g_w_out) = [
        [t.reshape(early_w[k][0].shape) for t in four] for k, four in zip(big_names[1:], u_early)]
    small_w = dict(b_ada=(b_ada, m_b_ada, v_b_ada), g_pre=(g_pre, m_g_pre, v_g_pre),
                   g_post=(g_post, m_g_post, v_g_post), lb_logits=(lb_logits, m_lb_logits, v_lb_logits),
                   pool_w=(pool_w, m_pool_w, v_pool_w), pool_scale=(pool_scale, m_pool_scale, v_pool_scale),
                   hgrn_norm_g=(hgrn_norm_g, m_hgrn_norm_g, v_hgrn_norm_g))
    u_small = _adamw_small(g_small, g_lb_logits, small_w)
    s = lambda key: u_small[key][3]
    grads_out = (g_w_ada, s("b_ada"), s("g_pre"), s("g_post"), g_w_in, s("pool_w"), s("pool_scale"), g_lb_logits,
                 s("hgrn_norm_g"), g_w_pool_o, g_w_hgrn_o, g_w_out)

    def ordered(k):
        s = lambda key: u_small[key][k]
        return (u_w_ada[k], s("b_ada"), s("g_pre"), s("g_post"), u_w_in[k], s("pool_w"), s("pool_scale"),
                s("lb_logits"), s("hgrn_norm_g"), u_w_pool_o[k], u_w_hgrn_o[k], u_w_out[k])

    return (loss, dx0[None], *grads_out, *ordered(0), *ordered(1), *ordered(2))
```

```python
import functools

import jax
import jax.numpy as jnp
from jax import lax
from jax.experimental import pallas as pl
from jax.experimental.pallas import tpu as pltpu

F32 = jnp.float32
BF16 = jnp.bfloat16
MESH = pl.DeviceIdType.MESH

D = 1024
HEADS = 8
HD = 128
GROUPS = 4
POOL_W = 512
CH = 128
SB_WIDE = 32
SB = 16
NH = 2
IN_W = 7168
NCHIP = 4
NDEV = 8
EPS = 1e-6
PV0, PG0, HQ0, HF0, HI0, HG0 = 0, 4, 8, 16, 24, 32
MGP_BLK, MGH_BLK = 5, 6

LR, B1, B2, AEPS, WD, STEP = 0.001, 0.9, 0.999, 1e-08, 0.01, 10
VMEM_LIMIT = 56 * 1024 * 1024


def _cp(sem=None, **kw):
    if sem is not None:
        kw["dimension_semantics"] = sem
    return pltpu.CompilerParams(vmem_limit_bytes=VMEM_LIMIT, **kw)


def _sig(z):
    return 1.0 / (1.0 + jnp.exp(-z))


def _dsilu(z, s):
    return s * (1.0 + z * (1.0 - s))


def _row_tile(rows, cap):
    if rows <= cap:
        return rows
    t = 1 << (cap.bit_length() - 1)
    while rows % t:
        t //= 2
    return t


ANY = pl.BlockSpec(memory_space=pl.ANY)


class _Carry:
    def __init__(self, ins, outs, aliases, n_sem, start, finish):
        self.ins, self.outs, self.aliases, self.n_sem = list(ins), list(outs), dict(aliases), n_sem
        self.start, self.finish = start, finish


class _SemWindow:
    def __init__(self, ref, base):
        self._ref, self._base = ref, base

    @property
    def at(self):
        return self

    def __getitem__(self, k):
        return self._ref.at[self._base + k]


def _join_carries(*carries):
    ins, outs, aliases, spans, n_sem = [], [], {}, [], 0
    for cr in carries:
        aliases.update({len(ins) + i: len(outs) + o for i, o in cr.aliases.items()})
        spans.append((len(ins), len(cr.ins), len(outs), len(cr.outs), n_sem))
        ins, outs, n_sem = ins + cr.ins, outs + cr.outs, n_sem + cr.n_sem

    def run(which):
        def fn(i_refs, o_refs, send_sems, recv_sems):
            for cr, (i0, ni, o0, no, s0) in zip(carries, spans):
                getattr(cr, which)(i_refs[i0:i0 + ni], o_refs[o0:o0 + no], _SemWindow(send_sems, s0),
                                   _SemWindow(recv_sems, s0))
        return fn

    return _Carry(ins, outs, aliases, n_sem, run("start"), run("finish"))


def _call(body, *, name, grid, in_specs, out_specs, out_shape, args, scratch_shapes=(), sem=None, carry=None,
          aliases=None):
    in_specs, out_specs, out_shape = list(in_specs), list(out_specs), list(out_shape)
    scratch_shapes = list(scratch_shapes)
    aliases = dict(aliases or {})
    if carry is None:
        outs = pl.pallas_call(body, name=name, grid=grid, in_specs=in_specs, out_specs=out_specs,
                              out_shape=out_shape, scratch_shapes=scratch_shapes, input_output_aliases=aliases,
                              compiler_params=_cp(sem))(*args)
        return list(outs)
    n_in, n_out, n_scr = len(in_specs), len(out_specs), len(scratch_shapes)
    c_in, c_out = len(carry.ins), len(carry.outs)

    def wrapped(*refs):
        k_in, rest = refs[:n_in], refs[n_in:]
        ci, rest = rest[:c_in], rest[c_in:]
        k_out, rest = rest[:n_out], rest[n_out:]
        co, rest = rest[:c_out], rest[c_out:]
        k_scr, (ssem, rsem) = rest[:n_scr], rest[n_scr:]
        pids = [pl.program_id(d) for d in range(len(grid))]
        first = functools.reduce(jnp.logical_and, [p == 0 for p in pids])
        last = functools.reduce(jnp.logical_and, [p == g - 1 for p, g in zip(pids, grid)])

        @pl.when(first)
        def _():
            carry.start(ci, co, ssem, rsem)

        body(*k_in, *k_out, *k_scr)

        @pl.when(last)
        def _():
            carry.finish(ci, co, ssem, rsem)

    outs = pl.pallas_call(
        wrapped, name=name, grid=grid, in_specs=in_specs + [ANY] * c_in, out_specs=out_specs + [ANY] * c_out,
        out_shape=out_shape + carry.outs,
        input_output_aliases={**aliases, **{n_in + i: n_out + o for i, o in carry.aliases.items()}},
        scratch_shapes=scratch_shapes + [pltpu.SemaphoreType.DMA((carry.n_sem,))] * 2,
        compiler_params=_cp(("arbitrary",) * len(grid)),
    )(*args, *carry.ins)
    return list(outs)


def _mm(a, b, *, name, b_mode="nn", out_shards=0, tm=1024, tn=256, tk=None, out_dtype=F32, carry=None):
    assert b_mode in ("nn", "nn_sh", "nt_shk"), b_mode
    M, K = a.shape
    if b_mode == "nn":
        N = b.shape[1]
    elif b_mode == "nn_sh":
        N = b.shape[0] * b.shape[2]
    else:
        N = b.shape[1]
    tm = _row_tile(M, tm)
    if b_mode == "nn_sh":
        tn = _row_tile(b.shape[2], tn)
    elif out_shards:
        tn = _row_tile(N // out_shards, tn)
    else:
        tn = _row_tile(N, tn)
    if tk is None:
        tk = K if b_mode != "nt_shk" else b.shape[2]
    if b_mode == "nt_shk":
        tk = _row_tile(b.shape[2], tk)
    nm, nn, nk = M // tm, N // tn, K // tk

    a_spec = pl.BlockSpec((tm, tk), lambda m, n, k: (m, k))
    if b_mode == "nn":
        b_spec = pl.BlockSpec((tk, tn), lambda m, n, k: (k, n))
    elif b_mode == "nn_sh":
        nps = b.shape[2] // tn
        b_spec = pl.BlockSpec((None, tk, tn), lambda m, n, k: (n // nps, k, n % nps))
    else:
        kps = b.shape[2] // tk
        b_spec = pl.BlockSpec((None, tn, tk), lambda m, n, k: (k // kps, n, k % kps))
    if out_shards:
        ops = (N // out_shards) // tn
        o_spec = pl.BlockSpec((None, tm, tn), lambda m, n, k: (n // ops, m, n % ops))
        o_shape = jax.ShapeDtypeStruct((out_shards, M, N // out_shards), out_dtype)
    else:
        o_spec = pl.BlockSpec((tm, tn), lambda m, n, k: (m, n))
        o_shape = jax.ShapeDtypeStruct((M, N), out_dtype)
    dn = (((1,), (1,)), ((), ())) if b_mode == "nt_shk" else (((1,), (0,)), ((), ()))

    def body(a_ref, b_ref, o_ref, acc_ref):
        k = pl.program_id(2)

        @pl.when(k == 0)
        def _():
            acc_ref[...] = jnp.zeros(acc_ref.shape, F32)

        acc_ref[...] += lax.dot_general(a_ref[...].astype(BF16), b_ref[...].astype(BF16), dn,
                                        preferred_element_type=F32)

        @pl.when(k == nk - 1)
        def _():
            o_ref[...] = acc_ref[...].astype(o_ref.dtype)

    outs = _call(body, name=name, grid=(nm, nn, nk), in_specs=[a_spec, b_spec], out_specs=[o_spec],
                 out_shape=[o_shape], scratch_shapes=[pltpu.VMEM((tm, tn), F32)],
                 sem=("parallel", "parallel", "arbitrary"), args=(a, b), carry=carry)
    return outs[0] if carry is None else (outs[0], outs[1:])


def _rowvec(n=D):
    return pl.BlockSpec((1, n), lambda i: (0, 0))


def _prenorm_fwd(x, g, scale, shift, name, carry=None):
    S = x.shape[0]
    tr = _row_tile(S, 256)

    def body(x_ref, g_ref, sc_ref, sh_ref, h_ref, ht_ref):
        xv = x_ref[...]
        r = lax.rsqrt(jnp.mean(xv * xv, axis=-1, keepdims=True) + EPS)
        hv = (xv * r) * g_ref[...] * (1.0 + sc_ref[...]) + sh_ref[...]
        h_ref[...] = hv.astype(BF16)
        ht_ref[...] = hv.T.astype(BF16)

    outs = _call(
        body, name=name, grid=(S // tr,),
        in_specs=[pl.BlockSpec((tr, D), lambda i: (i, 0)), _rowvec(), _rowvec(), _rowvec()],
        out_specs=[pl.BlockSpec((tr, D), lambda i: (i, 0)), pl.BlockSpec((D, tr), lambda i: (0, i))],
        out_shape=[jax.ShapeDtypeStruct((S, D), BF16), jax.ShapeDtypeStruct((D, S), BF16)],
        sem=("parallel",), args=(x, g, scale, shift), carry=carry)
    return outs[:2], outs[2:]


def _prenorm_bwd(dh, dxn, x, g, scale, name, carry=None):
    S = x.shape[0]
    tr = _row_tile(S, 256)

    def body(dh_ref, dxn_ref, x_ref, g_ref, sc_ref, dx_ref, dsh_ref, dsc_ref, dg_ref):
        i = pl.program_id(0)

        @pl.when(i == 0)
        def _():
            dsh_ref[...] = jnp.zeros((1, D), F32)
            dsc_ref[...] = jnp.zeros((1, D), F32)
            dg_ref[...] = jnp.zeros((1, D), F32)

        xv = x_ref[...]
        dhv = dh_ref[...]
        gv = g_ref[...]
        mod = 1.0 + sc_ref[...]
        r = lax.rsqrt(jnp.mean(xv * xv, axis=-1, keepdims=True) + EPS)
        xh = xv * r
        dsh_ref[...] += jnp.sum(dhv, axis=0, keepdims=True)
        dsc_ref[...] += jnp.sum(dhv * (xh * gv), axis=0, keepdims=True)
        dg_ref[...] += jnp.sum(dhv * mod * xh, axis=0, keepdims=True)
        u = dhv * mod * gv
        dx_ref[...] = dxn_ref[...] + r * u - xv * (r * r * r) * jnp.mean(u * xv, axis=-1, keepdims=True)

    tile = pl.BlockSpec((tr, D), lambda i: (i, 0))
    outs = _call(
        body, name=name, grid=(S // tr,),
        in_specs=[tile, tile, tile, _rowvec(), _rowvec()],
        out_specs=[tile, _rowvec(), _rowvec(), _rowvec()],
        out_shape=[jax.ShapeDtypeStruct((S, D), F32)] + [jax.ShapeDtypeStruct((1, D), F32)] * 3,
        sem=("arbitrary",), args=(dh, dxn, x, g, scale), carry=carry)
    return outs[:4], outs[4:]


def _layer_tail_fwd(proj, a_in, b_in, x, w_po, w_ho, w_out, gate, g, name, target=None, carry=None):
    S = proj.shape[0]
    tr = _row_tile(S, 256)
    nsh, _, wsh = w_po.shape
    n_in = 10 + (target is not None)

    def body(*refs):
        (mgp_ref, mgh_ref, a_ref, b_ref, x_ref, wpo_ref, who_ref, wout_ref, gate_ref, g_ref) = refs[:10]
        bra_ref, brb_ref, mt_ref, y_ref, xn_ref = refs[n_in:n_in + 5]
        av = a_ref[...]
        bra = jnp.concatenate([jnp.dot(av, wpo_ref[j], preferred_element_type=F32) for j in range(nsh)], axis=1)
        brb = jnp.dot(b_ref[...], who_ref[...], preferred_element_type=F32)
        mv = _sig(mgp_ref[...].astype(F32)) * bra + _sig(mgh_ref[...].astype(F32)) * brb
        bra_ref[...] = bra.astype(BF16)
        brb_ref[...] = brb.astype(BF16)
        mt_ref[...] = mv.T.astype(BF16)
        yv = jnp.dot(mv.astype(BF16), wout_ref[...], preferred_element_type=F32)
        y_ref[...] = yv
        r = lax.rsqrt(jnp.mean(yv * yv, axis=-1, keepdims=True) + EPS)
        xn = x_ref[...] + gate_ref[...] * ((yv * r) * g_ref[...])
        if target is None:
            xn_ref[...] = xn
        else:
            t_ref, l_ref = refs[10], refs[n_in + 5]

            @pl.when(pl.program_id(0) == 0)
            def _():
                l_ref[...] = jnp.zeros((8, 128), F32)

            err = xn - t_ref[...]
            xn_ref[...] = err * (1.0 / D)
            l_ref[...] += 0.5 * jnp.sum(jnp.mean(err * err, axis=-1, keepdims=True))

    tile = pl.BlockSpec((tr, D), lambda i: (i, 0))
    whole = lambda t: pl.BlockSpec(t.shape, lambda i: (0,) * t.ndim)
    last = target is not None
    outs = _call(
        body, name=name, grid=(S // tr,),
        in_specs=[pl.BlockSpec((tr, D), lambda i: (i, MGP_BLK)), pl.BlockSpec((tr, D), lambda i: (i, MGH_BLK)),
                  pl.BlockSpec((tr, POOL_W), lambda i: (i, 0)), tile, tile, whole(w_po), whole(w_ho),
                  whole(w_out), _rowvec(), _rowvec()] + [tile] * last,
        out_specs=[tile, tile, pl.BlockSpec((D, tr), lambda i: (0, i)), tile, tile]
        + [pl.BlockSpec((8, 128), lambda i: (0, 0))] * last,
        out_shape=[jax.ShapeDtypeStruct((S, D), BF16), jax.ShapeDtypeStruct((S, D), BF16),
                   jax.ShapeDtypeStruct((D, S), BF16), jax.ShapeDtypeStruct((S, D), F32),
                   jax.ShapeDtypeStruct((S, D), F32)] + [jax.ShapeDtypeStruct((8, 128), F32)] * last,
        sem=("arbitrary",) if last else ("parallel",),
        args=(proj, proj, a_in, b_in, x, w_po, w_ho, w_out, gate, g) + ((target,) if last else ()), carry=carry)
    return outs[:5 + last], outs[5 + last:]


def _layer_head_bwd(dxn, y, proj, br_a, br_b, w_po, w_ho, w_out, gate, g, name, carry=None):
    S = y.shape[0]
    tr = _row_tile(S, 256)
    nsh, _, wsh = w_po.shape

    def body(dxn_ref, y_ref, mgp_ref, mgh_ref, bra_ref, brb_ref, wpo_ref, who_ref, wout_ref, gate_ref, g_ref,
             dy_ref, dba_ref, dbb_ref, dproj_ref, dain_ref, dbin_ref, dgate_ref, dg_ref, dmgh_s):
        i = pl.program_id(0)
        j = pl.program_id(1)

        @pl.when((i == 0) & (j == 0))
        def _():
            dgate_ref[...] = jnp.zeros((1, D), F32)
            dg_ref[...] = jnp.zeros((1, D), F32)

        @pl.when(j == 1)
        def _():
            dproj_ref[...] = dmgh_s[...]

        @pl.when(j == 0)
        def _():
            everything(dxn_ref, y_ref, mgp_ref, mgh_ref, bra_ref, brb_ref, wpo_ref, who_ref, wout_ref, gate_ref,
                       g_ref, dy_ref, dba_ref, dbb_ref, dproj_ref, dain_ref, dbin_ref, dgate_ref, dg_ref, dmgh_s)

    def everything(dxn_ref, y_ref, mgp_ref, mgh_ref, bra_ref, brb_ref, wpo_ref, who_ref, wout_ref, gate_ref, g_ref,
                   dy_ref, dba_ref, dbb_ref, dproj_ref, dain_ref, dbin_ref, dgate_ref, dg_ref, dmgh_s):
        yv = y_ref[...]
        dv = dxn_ref[...]
        gv = g_ref[...]
        gt = gate_ref[...]
        r = lax.rsqrt(jnp.mean(yv * yv, axis=-1, keepdims=True) + EPS)
        yh = yv * r
        dgate_ref[...] += jnp.sum(dv * (yh * gv), axis=0, keepdims=True)
        dg_ref[...] += jnp.sum(dv * gt * yh, axis=0, keepdims=True)
        u = dv * gt * gv
        dy = (r * u - yv * (r * r * r) * jnp.mean(u * yv, axis=-1, keepdims=True)).astype(BF16)
        dy_ref[...] = dy
        dm = _dot_nt(dy, wout_ref[...])
        sp = _sig(mgp_ref[...].astype(F32))
        sh = _sig(mgh_ref[...].astype(F32))
        dba = (dm * sp).astype(BF16)
        dbb = (dm * sh).astype(BF16)
        dba_ref[...] = dba
        dbb_ref[...] = dbb
        dproj_ref[...] = (dm * bra_ref[...].astype(F32) * sp * (1.0 - sp)).astype(BF16)
        dmgh_s[...] = (dm * brb_ref[...].astype(F32) * sh * (1.0 - sh)).astype(BF16)
        dain = _dot_nt(dba[:, 0:wsh], wpo_ref[0])
        for k in range(1, nsh):
            dain = dain + _dot_nt(dba[:, k * wsh:(k + 1) * wsh], wpo_ref[k])
        dain_ref[...] = dain
        dbin_ref[...] = _dot_nt(dbb, who_ref[...])

    tile = pl.BlockSpec((tr, D), lambda i, j: (i, 0))
    whole = lambda t: pl.BlockSpec(t.shape, lambda i, j: (0,) * t.ndim)
    vec = pl.BlockSpec((1, D), lambda i, j: (0, 0))
    ahead = lambda i, j: jnp.minimum(i + j, S // tr - 1)
    tile_in = pl.BlockSpec((tr, D), lambda i, j: (ahead(i, j), 0))
    outs = _call(
        body, name=name, grid=(S // tr, 2),
        in_specs=[tile_in, tile_in, pl.BlockSpec((tr, D), lambda i, j: (ahead(i, j), MGP_BLK)),
                  pl.BlockSpec((tr, D), lambda i, j: (ahead(i, j), MGH_BLK)), tile_in, tile_in, whole(w_po),
                  whole(w_ho), whole(w_out), vec, vec],
        out_specs=[tile, tile, tile, pl.BlockSpec((tr, D), lambda i, j: (i, MGP_BLK + j)),
                   pl.BlockSpec((tr, POOL_W), lambda i, j: (i, 0)), tile, vec, vec],
        out_shape=[jax.ShapeDtypeStruct((S, D), BF16)] * 3
        + [jax.ShapeDtypeStruct((S, IN_W), BF16), jax.ShapeDtypeStruct((S, POOL_W), F32),
           jax.ShapeDtypeStruct((S, D), F32), jax.ShapeDtypeStruct((1, D), F32), jax.ShapeDtypeStruct((1, D), F32)],
        scratch_shapes=[pltpu.VMEM((tr, D), BF16)], sem=("arbitrary", "arbitrary"),
        args=(dxn, y, proj, proj, br_a, br_b, w_po, w_ho, w_out, gate, g), carry=carry)
    return outs[:8], outs[8:]


def _pool_pieces(u, g, S):
    rowi = lax.broadcasted_iota(jnp.int32, (S, 1), 0)

    def down(z, k):
        return jnp.where(rowi >= k, pltpu.roll(z, k, axis=0), 0.0)

    s2 = u + down(u, 1)
    s4 = s2 + down(s2, 2)
    s8 = s4 + down(s4, 4)
    s16 = s8 + down(s8, 8)
    win = jnp.where(g == 0, s2, jnp.where(g == 1, s4, jnp.where(g == 2, s8, s16)))
    w = jnp.where(g == 0, 2, jnp.where(g == 1, 4, jnp.where(g == 2, 8, 16)))
    count = jnp.minimum(rowi + 1, w).astype(F32)
    return win / count - u, count, rowi


def _pool_fwd(proj, pw, pscale, name):
    S = proj.shape[0]

    def body(pv_ref, pg_ref, pw_ref, sc_ref, a_ref, at_ref):
        g = pl.program_id(0)
        pooled, _, _ = _pool_pieces(pv_ref[...].astype(F32), g, S)
        pm = jnp.dot(pooled.astype(BF16), pw_ref[...].astype(BF16), preferred_element_type=F32)
        pgv = pg_ref[...].astype(F32)
        av = pm * sc_ref[...] * (pgv * _sig(pgv))
        a_ref[...] = av.astype(BF16)
        at_ref[...] = av.T.astype(BF16)

    outs = _call(
        body, name=name, grid=(GROUPS,),
        in_specs=[pl.BlockSpec((S, 128), lambda g: (0, PV0 + g)), pl.BlockSpec((S, 128), lambda g: (0, PG0 + g)),
                  pl.BlockSpec((None, 128, 128), lambda g: (g, 0, 0)), pl.BlockSpec((1, 128), lambda g: (0, g))],
        out_specs=[pl.BlockSpec((S, 128), lambda g: (0, g)), pl.BlockSpec((128, S), lambda g: (g, 0))],
        out_shape=[jax.ShapeDtypeStruct((S, POOL_W), BF16), jax.ShapeDtypeStruct((POOL_W, S), BF16)],
        sem=("parallel",), args=(proj, proj, pw, pscale))
    return outs


def _pool_bwd(da, proj, pw, pscale, dproj, name):
    S = proj.shape[0]

    def body(da_ref, pv_ref, pg_ref, pw_ref, sc_ref, dproj_in, dproj_ref, dpw_ref, dsc_ref, dpg_s):
        @pl.when(pl.program_id(1) == 1)
        def _():
            dproj_ref[...] = dpg_s[...]

        @pl.when(pl.program_id(1) == 0)
        def _():
            group(da_ref, pv_ref, pg_ref, pw_ref, sc_ref, dproj_ref, dpg_s, dpw_ref, dsc_ref)

    def group(da_ref, pv_ref, pg_ref, pw_ref, sc_ref, dpv_ref, dpg_ref, dpw_ref, dsc_ref):
        g = pl.program_id(0)
        pooled, count, rowi = _pool_pieces(pv_ref[...].astype(F32), g, S)
        pwb = pw_ref[...].astype(BF16)
        pm = jnp.dot(pooled.astype(BF16), pwb, preferred_element_type=F32)
        scv = sc_ref[...]
        pgv = pg_ref[...].astype(F32)
        sg = _sig(pgv)
        dav = da_ref[...]
        d_ps = dav * (pgv * sg)
        dpg_ref[...] = (dav * (pm * scv) * _dsilu(pgv, sg)).astype(BF16)
        dsc_ref[...] = jnp.sum(d_ps * pm, axis=0, keepdims=True)
        d_pm = (d_ps * scv).astype(BF16)
        dpw_ref[...] = lax.dot_general(pooled.astype(BF16), d_pm, (((0,), (0,)), ((), ())),
                                       preferred_element_type=F32)
        d_pooled = lax.dot_general(d_pm, pwb, (((1,), (1,)), ((), ())), preferred_element_type=F32)
        z = d_pooled / count

        def up(v, k):
            return jnp.where(rowi < S - k, pltpu.roll(v, S - k, axis=0), 0.0)

        t2 = z + up(z, 1)
        t4 = t2 + up(t2, 2)
        t8 = t4 + up(t4, 4)
        t16 = t8 + up(t8, 8)
        adj = jnp.where(g == 0, t2, jnp.where(g == 1, t4, jnp.where(g == 2, t8, t16)))
        dpv_ref[...] = (adj - d_pooled).astype(BF16)

    col = lambda g, j: (0, g)
    ahead = lambda g, j: jnp.minimum(g + j, GROUPS - 1)
    return pl.pallas_call(
        body, name=name, grid=(GROUPS, 2),
        in_specs=[pl.BlockSpec((S, 128), lambda g, j: (0, ahead(g, j))),
                  pl.BlockSpec((S, 128), lambda g, j: (0, PV0 + ahead(g, j))),
                  pl.BlockSpec((S, 128), lambda g, j: (0, PG0 + ahead(g, j))),
                  pl.BlockSpec((None, 128, 128), lambda g, j: (ahead(g, j), 0, 0)),
                  pl.BlockSpec((1, 128), lambda g, j: (0, ahead(g, j))), ANY],
        out_specs=[pl.BlockSpec((S, 128), lambda g, j: (0, PV0 + g + (PG0 - PV0) * j)),
                   pl.BlockSpec((None, 128, 128), lambda g, j: (g, 0, 0)), pl.BlockSpec((1, 128), col)],
        out_shape=[jax.ShapeDtypeStruct(dproj.shape, dproj.dtype),
                   jax.ShapeDtypeStruct((GROUPS, 128, 128), F32), jax.ShapeDtypeStruct((1, POOL_W), F32)],
        scratch_shapes=[pltpu.VMEM((S, 128), BF16)], input_output_aliases={5: 0},
        compiler_params=_cp(("arbitrary", "arbitrary")),
    )(da, proj, proj, pw, pscale, dproj)


SCAN_SHIFTS = tuple(1 << b for b in range(CH.bit_length() - 1))


def _chunk_cumsum(z, rowi):
    for sh in SCAN_SHIFTS:
        z = z + jnp.where(rowi >= sh, pltpu.roll(z, sh, axis=0), 0.0)
    return z


def _chunk_rev_cumsum(z, rowi):
    for sh in SCAN_SHIFTS:
        z = z + jnp.where(rowi < CH - sh, pltpu.roll(z, CH - sh, axis=0), 0.0)
    return z


def _dot_nn(a, b):
    return jnp.dot(a.astype(BF16), b.astype(BF16), preferred_element_type=F32)


def _dot_nt(a, b):
    return lax.dot_general(a.astype(BF16), b.astype(BF16), (((1,), (1,)), ((), ())), preferred_element_type=F32)


def _dot_tn(a, b):
    return lax.dot_general(a.astype(BF16), b.astype(BF16), (((0,), (0,)), ((), ())), preferred_element_type=F32)


def _gates(hq, hf, lbv):
    hq, hf = hq.astype(F32), hf.astype(F32)
    sq = _sig(hq)
    sf = _sig(hf)
    f = lbv + (1.0 - lbv) * sf
    fc = jnp.maximum(f, 1e-30)
    return hq * sq, sq, sf, f, fc, jnp.log(fc)


DECAY_CAP = 60.0


def _block_ref(c_ref, i, sb):
    if i == 0:
        return jnp.zeros((1, HD), F32)
    return c_ref[sb * i - 1:sb * i, :]


def _block_decay(c_ref, sb):
    spans = [_block_ref(c_ref, i, sb) - c_ref[sb * (i + 1) - 1:sb * (i + 1), :] for i in range(CH // sb)]
    return functools.reduce(jnp.maximum, spans)


def _pair_factors(q_ref, k, c_ref, first, cap, round_bf16, sb):
    nb = CH // sb
    c = c_ref[...]
    zero = jnp.zeros((sb, HD), F32)
    q_groups, k_groups, eqs, eks = [], [], [], []
    for i in range(first, nb):
        blk = slice(sb * i, sb * (i + 1))
        r_i = _block_ref(c_ref, i, sb)
        eq = jnp.exp(jnp.minimum(c_ref[blk, :] - r_i, 0.0))
        ek = jnp.exp(jnp.minimum(r_i - c, cap))
        qi, kei = q_ref[blk, :] * eq, k * ek
        if round_bf16:
            qi, kei = qi.astype(BF16).astype(F32), kei.astype(BF16).astype(F32)
        q_groups.append(jnp.concatenate([zero] * i + [qi] + [zero] * (nb - 1 - i), axis=0))
        k_groups.append(kei)
        eqs.append(eq)
        eks.append(ek)
    return jnp.concatenate(q_groups, axis=1), jnp.concatenate(k_groups, axis=1), eqs, eks


def _pair_mask(rowi, coli, strict, sb):
    return (coli < jnp.bitwise_and(rowi, -sb)) if strict else (coli <= rowi)


def _hgrn_fwd(proj, lb, gn, name, carry=None):
    S = proj.shape[0]
    nch = S // CH
    W = NH * HD

    def body(hq_ref, hf_ref, hi_ref, hg_ref, lb_ref, gn_ref, bin_ref, bint_ref, oraw_ref, st_ref, mild_ref,
             cum_ref, q_s, k_s, c_s, v_s, o_s, state_s, qf_s, kf_s, cf_s):
        state_s[...] = jnp.zeros((NH, HD, HD), F32)
        rowi = lax.broadcasted_iota(jnp.int32, (CH, 1), 0)
        coli = lax.broadcasted_iota(jnp.int32, (1, CH), 1)
        sbi = lax.broadcasted_iota(jnp.int32, (SB, 1), 0)
        gnv = gn_ref[...]

        def gates_pass(n, worst):
            wide, narrow = worst
            rows = pl.ds(pl.multiple_of(n * CH, CH), CH)
            for hh in range(NH):
                lanes = slice(hh * HD, (hh + 1) * HD)
                q, _, _, f, _, logf = _gates(hq_ref[rows, lanes], hf_ref[rows, lanes], lb_ref[:, lanes])
                c = _chunk_cumsum(logf, rowi)
                qf_s[hh, rows, :] = q
                kf_s[hh, rows, :] = 1.0 - f
                cf_s[hh, rows, :] = c
                cum_ref[rows, lanes] = c
                c_s[hh] = c
                wide = jnp.maximum(wide, _block_decay(c_s.at[hh], SB_WIDE))
                narrow = jnp.maximum(narrow, _block_decay(c_s.at[hh], SB))
            return wide, narrow

        def between_chunks(hh, n, rows):
            lanes = slice(hh * HD, (hh + 1) * HD)
            q = qf_s[hh, rows, :]
            k = kf_s[hh, rows, :]
            c = cf_s[hh, rows, :]
            v = hi_ref[rows, lanes].astype(F32)
            q_s[hh] = q
            k_s[hh] = k
            c_s[hh] = c
            v_s[hh] = v
            st = state_s[hh]
            st_ref[hh, n] = st.astype(BF16)
            o_s[hh] = _dot_nt(q * jnp.exp(c), st)
            last = c_s[hh, CH - 1:CH, :]
            state_s[hh] = st * jnp.exp(last) + _dot_tn(v, k * jnp.exp(last - c))

        def pairs_matmul(hh, first, cap, strict, sb):
            qx, kc, _, _ = _pair_factors(q_s.at[hh], k_s[hh], c_s.at[hh], first, cap, False, sb)
            a = jnp.where(_pair_mask(rowi, coli, strict, sb), _dot_nt(qx, kc), 0.0)
            o_s[hh] += _dot_nn(a, v_s[hh])

        def within_chunk_matmul(sb):
            return lambda hh: pairs_matmul(hh, 0, DECAY_CAP, False, sb)

        def within_chunk_exact(hh):
            pairs_matmul(hh, 1, 0.0, True, SB)
            for i in range(CH // SB):
                blk = slice(SB * i, SB * (i + 1))
                qb = q_s[hh, blk, :]
                cb = c_s[hh, blk, :]
                acc = jnp.zeros((SB, HD), F32)
                for s in range(SB):
                    row = SB * i + s
                    w = jnp.exp(jnp.minimum(cb - c_s[hh, row:row + 1, :], 0.0))
                    a_col = jnp.sum(qb * k_s[hh, row:row + 1, :] * w, axis=-1, keepdims=True)
                    acc = acc + jnp.where(sbi >= s, a_col, 0.0) * v_s[hh, row:row + 1, :]
                o_s[hh, blk, :] += acc

        def norm_and_gate(hh, rows):
            lanes = slice(hh * HD, (hh + 1) * HD)
            ov = o_s[hh]
            oraw_ref[rows, lanes] = ov
            r = lax.rsqrt(jnp.mean(ov * ov, axis=-1, keepdims=True) + EPS)
            hg = hg_ref[rows, lanes].astype(F32)
            bin_ref[rows, lanes] = ((ov * r) * gnv * (hg * _sig(hg))).astype(BF16)

        def chunk_with(within_chunk):
            def chunk(n, carry):
                rows = pl.ds(pl.multiple_of(n * CH, CH), CH)
                for hh in range(NH):
                    between_chunks(hh, n, rows)
                for hh in range(NH):
                    within_chunk(hh)
                for hh in range(NH):
                    norm_and_gate(hh, rows)
                return carry
            return chunk

        none = jnp.zeros((1, HD), F32)
        wide, narrow = lax.fori_loop(0, nch, gates_pass, (none, none))
        tier = jnp.where(jnp.max(wide) <= DECAY_CAP, 2.0, jnp.where(jnp.max(narrow) <= DECAY_CAP, 1.0, 0.0))
        mild_ref[...] = jnp.broadcast_to(tier, (8, HD))

        @pl.when(tier == 2.0)
        def _():
            lax.fori_loop(0, nch, chunk_with(within_chunk_matmul(SB_WIDE)), 0, unroll=4)

        @pl.when(tier == 1.0)
        def _():
            lax.fori_loop(0, nch, chunk_with(within_chunk_matmul(SB)), 0, unroll=2)

        @pl.when(tier == 0.0)
        def _():
            lax.fori_loop(0, nch, chunk_with(within_chunk_exact), 0)

        bint_ref[...] = bin_ref[...].astype(F32).T.astype(BF16)

    col = lambda off: pl.BlockSpec((S, W), lambda h: (0, off // NH + h))
    head = pl.BlockSpec((S, W), lambda h: (0, h))
    outs = _call(
        body, name=name, grid=(HEADS // NH,),
        in_specs=[col(HQ0), col(HF0), col(HI0), col(HG0), pl.BlockSpec((1, W), lambda h: (0, h)),
                  pl.BlockSpec((1, HD), lambda h: (0, 0))],
        out_specs=[head, pl.BlockSpec((W, S), lambda h: (h, 0)), head,
                   pl.BlockSpec((NH, nch, HD, HD), lambda h: (h, 0, 0, 0)),
                   pl.BlockSpec((8, HD), lambda h: (h, 0)), head],
        out_shape=[jax.ShapeDtypeStruct((S, D), BF16), jax.ShapeDtypeStruct((D, S), BF16),
                   jax.ShapeDtypeStruct((S, D), F32), jax.ShapeDtypeStruct((HEADS, nch, HD, HD), BF16),
                   jax.ShapeDtypeStruct((8 * HEADS // NH, HD), F32), jax.ShapeDtypeStruct((S, D), F32)],
        scratch_shapes=[pltpu.VMEM((NH, CH, HD), F32)] * 5 + [pltpu.VMEM((NH, HD, HD), F32)]
        + [pltpu.VMEM((NH, S, HD), F32)] * 3,
        sem=("parallel",), args=(proj, proj, proj, proj, lb, gn), carry=carry)
    return outs[:6], outs[6:]


def _hgrn_bwd(dbin, proj, oraw, states, mild, cum, lb, gn, dproj, name, carry=None):
    S = proj.shape[0]
    nch = S // CH
    W = NH * HD
    n_in = 12

    def body(*refs):
        ins, (dproj_ref, dlb_ref, dgn_ref) = refs[:n_in - 1], refs[n_in:n_in + 3]
        scratch, later = refs[n_in + 3:-3], refs[-3:]
        seg = pl.program_id(1)

        @pl.when(seg == 0)
        def _():
            heads(*ins, dproj_ref, *later, dlb_ref, dgn_ref, *scratch)

        for s, kept in enumerate(later):
            @pl.when(seg == s + 1)
            def _(kept=kept):
                dproj_ref[...] = kept[...]

    def heads(db_ref, hq_ref, hf_ref, hi_ref, hg_ref, or_ref, st_ref, mild_ref, cum_ref, lb_ref, gn_ref,
              dq_ref, df_ref, di_ref, dg_ref, dlb_ref, dgn_ref,
              q_s, k_s, c_s, v_s, do_s, dq_s, dk_s, dv_s, dc_s, dqd_s, dkd_s, f_s, sf_s, sq_s, dl_s, dst_s,
              dlb_s, dgn_s):
        dst_s[...] = jnp.zeros((NH, HD, HD), F32)
        dlb_s[...] = jnp.zeros((1, W), F32)
        dgn_s[...] = jnp.zeros((1, HD), F32)
        rowi = lax.broadcasted_iota(jnp.int32, (CH, 1), 0)
        coli = lax.broadcasted_iota(jnp.int32, (1, CH), 1)
        sbi = lax.broadcasted_iota(jnp.int32, (SB, 1), 0)
        gnv = gn_ref[...]
        def between_chunks(hh, n, rows):
            lanes = slice(hh * HD, (hh + 1) * HD)
            lbv = lb_ref[:, lanes]
            hq = hq_ref[rows, lanes].astype(F32)
            sq = _sig(hq)
            sf = _sig(hf_ref[rows, lanes].astype(F32))
            f = lbv + (1.0 - lbv) * sf
            q = hq * sq
            k = 1.0 - f
            f_s[hh] = f
            sf_s[hh] = sf
            sq_s[hh] = sq
            v = hi_ref[rows, lanes].astype(F32)
            c = cum_ref[rows, lanes]
            ov = or_ref[rows, lanes]
            hg = hg_ref[rows, lanes].astype(F32)
            sg = _sig(hg)
            r = lax.rsqrt(jnp.mean(ov * ov, axis=-1, keepdims=True) + EPS)
            dbv = db_ref[rows, lanes]
            d_on = dbv * (hg * sg)
            dg_ref[rows, lanes] = (dbv * ((ov * r) * gnv) * _dsilu(hg, sg)).astype(BF16)
            dgn_s[...] += jnp.sum(d_on * (ov * r), axis=0, keepdims=True)
            u = d_on * gnv
            do = r * u - ov * (r * r * r) * jnp.mean(u * ov, axis=-1, keepdims=True)
            q_s[hh] = q
            k_s[hh] = k
            c_s[hh] = c
            v_s[hh] = v
            do_s[hh] = do
            st = st_ref[hh, n].astype(F32)
            dst = dst_s[hh]
            ec = jnp.exp(c)
            last = c_s[hh, CH - 1:CH, :]
            el = jnp.exp(last - c)
            elast = jnp.exp(last)
            dq = _dot_nn(do, st) * ec
            dk = _dot_nn(v, dst) * el
            dq_s[hh] = dq
            dk_s[hh] = dk
            dv_s[hh] = _dot_nt(k * el, dst)
            dc_s[hh] = q * dq - k * dk
            dl_s[hh] = (jnp.sum(k * dk, axis=0, keepdims=True)
                        + elast * jnp.sum(st * dst, axis=0, keepdims=True))
            dst_s[hh] = dst * elast + _dot_tn(do, q * ec)

        def pairs_matmul(hh, first, cap, strict, sb):
            do = do_s[hh]
            qx, kc, eqs, eks = _pair_factors(q_s.at[hh], k_s[hh], c_s.at[hh], first, cap, True, sb)
            mask = _pair_mask(rowi, coli, strict, sb)
            a = jnp.where(mask, _dot_nt(qx, kc), 0.0)
            d_a = jnp.where(mask, _dot_nt(do, v_s[hh]).astype(BF16).astype(F32), 0.0)
            dqx = _dot_nn(d_a, kc)
            dkc = _dot_tn(d_a, qx)
            dv_s[hh] += _dot_tn(a, do)
            dk, dcum = dk_s[hh], dc_s[hh]
            dq_slabs = [jnp.zeros((sb, HD), F32)] * first
            dc_slabs = [jnp.zeros((sb, HD), F32)] * first
            for g, (eq, ek) in enumerate(zip(eqs, eks)):
                rows = slice(sb * (first + g), sb * (first + g + 1))
                cols = slice(HD * g, HD * (g + 1))
                dq_i = dqx[rows, cols]
                dk_i = dkc[:, cols]
                dq_slabs.append(dq_i * eq)
                dc_slabs.append(qx[rows, cols] * dq_i)
                dk = dk + dk_i * ek
                dcum = dcum - kc[:, cols] * dk_i
            dq_s[hh] += jnp.concatenate(dq_slabs, axis=0)
            dk_s[hh] = dk
            dc_s[hh] = dcum + jnp.concatenate(dc_slabs, axis=0)

        def pairs_exact(hh):
            dqd_s[hh] = jnp.zeros((CH, HD), F32)
            dkd_s[hh] = jnp.zeros((CH, HD), F32)
            for i in range(CH // SB):
                blk = slice(SB * i, SB * (i + 1))
                qb = q_s[hh, blk, :]
                cb = c_s[hh, blk, :]
                dob = do_s[hh, blk, :]
                dq_acc = jnp.zeros((SB, HD), F32)
                for s in range(SB):
                    row = SB * i + s
                    ks = k_s[hh, row:row + 1, :]
                    vs = v_s[hh, row:row + 1, :]
                    w = jnp.exp(jnp.minimum(cb - c_s[hh, row:row + 1, :], 0.0))
                    live = sbi >= s
                    a_col = jnp.where(live, jnp.sum(qb * ks * w, axis=-1, keepdims=True), 0.0)
                    da_col = jnp.where(live, jnp.sum(dob * vs, axis=-1, keepdims=True), 0.0)
                    dq_acc = dq_acc + da_col * ks * w
                    dkd_s[hh, row:row + 1, :] += jnp.sum(da_col * qb * w, axis=0, keepdims=True)
                    dv_s[hh, row:row + 1, :] += jnp.sum(a_col * dob, axis=0, keepdims=True)
                dqd_s[hh, blk, :] += dq_acc
            dq_d = dqd_s[hh]
            dk_d = dkd_s[hh]
            dq_s[hh] += dq_d
            dk_s[hh] += dk_d
            dc_s[hh] += q_s[hh] * dq_d - k_s[hh] * dk_d

        def gate_grads(hh, rows):
            lanes = slice(hh * HD, (hh + 1) * HD)
            lbv = lb_ref[:, lanes]
            hq = hq_ref[rows, lanes].astype(F32)
            f, sf, sq = f_s[hh], sf_s[hh], sq_s[hh]
            dlogf = _chunk_rev_cumsum(dc_s[hh], rowi) + dl_s[hh]
            dfv = jnp.where(f > 1e-30, dlogf / jnp.maximum(f, 1e-30), 0.0) - dk_s[hh]
            dlb_s[:, lanes] += jnp.sum(dfv * (1.0 - sf), axis=0, keepdims=True)
            df_ref[rows, lanes] = (dfv * (1.0 - lbv) * sf * (1.0 - sf)).astype(BF16)
            dq_ref[rows, lanes] = (dq_s[hh] * _dsilu(hq, sq)).astype(BF16)
            di_ref[rows, lanes] = dv_s[hh].astype(BF16)

        def chunk_with(pairs):
            def chunk(j, carry):
                n = nch - 1 - j
                rows = pl.ds(pl.multiple_of(n * CH, CH), CH)
                for hh in range(NH):
                    between_chunks(hh, n, rows)
                for hh in range(NH):
                    pairs(hh)
                for hh in range(NH):
                    gate_grads(hh, rows)
                return carry
            return chunk

        def pairs_mild(sb):
            return lambda hh: pairs_matmul(hh, 0, DECAY_CAP, False, sb)

        def pairs_any(hh):
            pairs_matmul(hh, 1, 0.0, True, SB)
            pairs_exact(hh)

        tier = jnp.max(mild_ref[...])

        @pl.when(tier == 2.0)
        def _():
            lax.fori_loop(0, nch, chunk_with(pairs_mild(SB_WIDE)), 0, unroll=2)

        @pl.when(tier == 1.0)
        def _():
            lax.fori_loop(0, nch, chunk_with(pairs_mild(SB)), 0)

        @pl.when(tier == 0.0)
        def _():
            lax.fori_loop(0, nch, chunk_with(pairs_any), 0)

        dlb_ref[...] = dlb_s[...]
        dgn_ref[...] = jnp.broadcast_to(dgn_s[...], (8, HD))

    ahead = lambda h, s: jnp.minimum(h + jnp.minimum(s, 1), HEADS // NH - 1)
    col = lambda off: pl.BlockSpec((S, W), lambda h, s: (0, off // NH + ahead(h, s)))
    head_in = pl.BlockSpec((S, W), lambda h, s: (0, ahead(h, s)))
    vec_in = pl.BlockSpec((1, W), lambda h, s: (0, ahead(h, s)))
    vec = pl.BlockSpec((1, W), lambda h, s: (0, h))
    seg_w = (HF0 - HQ0) // NH
    outs = _call(
        body, name=name, grid=(HEADS // NH, 4),
        in_specs=[head_in, col(HQ0), col(HF0), col(HI0), col(HG0), head_in,
                  pl.BlockSpec((NH, nch, HD, HD), lambda h, s: (ahead(h, s), 0, 0, 0)),
                  pl.BlockSpec((8, HD), lambda h, s: (ahead(h, s), 0)), head_in, vec_in,
                  pl.BlockSpec((1, HD), lambda h, s: (0, 0)), ANY],
        out_specs=[pl.BlockSpec((S, W), lambda h, s: (0, HQ0 // NH + seg_w * s + h)), vec,
                   pl.BlockSpec((8, HD), lambda h, s: (h, 0))],
        out_shape=[jax.ShapeDtypeStruct(dproj.shape, dproj.dtype), jax.ShapeDtypeStruct((1, D), F32),
                   jax.ShapeDtypeStruct((8 * HEADS // NH, HD), F32)],
        scratch_shapes=[pltpu.VMEM((NH, CH, HD), F32)] * 14
        + [pltpu.VMEM((NH, 1, HD), F32), pltpu.VMEM((NH, HD, HD), F32), pltpu.VMEM((1, W), F32),
           pltpu.VMEM((1, HD), F32)] + [pltpu.VMEM((S, W), BF16)] * 3,
        sem=("arbitrary", "arbitrary"), aliases={n_in - 1: 0},
        args=(dbin, proj, proj, proj, proj, oraw, states, mild, cum, lb, gn, dproj), carry=carry)
    dproj, dlb, dgn = outs[:3]
    return (dproj, dlb, dgn.reshape(HEADS // NH, 8, HD)[:, 0, :]), outs[3:]


def _lower_bounds(l0, l1):
    m = jnp.maximum(l0, l1)
    e0 = jnp.exp(l0 - m)
    e1 = jnp.exp(l1 - m)
    tot = e0 + e1
    p0 = e0 / tot
    p1 = e1 / tot
    return jnp.clip(p0 - p0, 0.0, 1.0), jnp.clip((p0 + p1) - p0, 0.0, 1.0)


def _lb_fwd(logits):
    def body(l_ref, o_ref):
        lb0, lb1 = _lower_bounds(l_ref[0:1, :], l_ref[1:2, :])
        o_ref[0:1, :] = lb0
        o_ref[1:2, :] = lb1

    return pl.pallas_call(body, name="lb_fwd", out_shape=jax.ShapeDtypeStruct((2, D), F32))(logits)


def _lb_bwd(logits, dlb):
    def body(l_ref, d_ref, o_ref):
        _, vjp = jax.vjp(_lower_bounds, l_ref[0:1, :], l_ref[1:2, :])
        g0, g1 = vjp((d_ref[0:1, :], d_ref[1:2, :]))
        o_ref[0:1, :] = g0
        o_ref[1:2, :] = g1

    return pl.pallas_call(body, name="lb_bwd", out_shape=jax.ShapeDtypeStruct((2, D), F32))(logits, dlb)


ADA_PAD = 128


def _ada_fwd(c_pad, w_ada, b_sh):
    ns = w_ada.shape[2]

    def body(c_ref, w_ref, b_ref, o_ref):
        cv = c_ref[...]
        ca = (cv * _sig(cv)).astype(BF16)
        for l in range(2):
            res = jnp.dot(ca, w_ref[l].astype(BF16), preferred_element_type=F32)
            o_ref[:, l * ns:(l + 1) * ns] = res[0:NDEV, :] + b_ref[l:l + 1, :]

    return pl.pallas_call(body, name="ada_fwd", out_shape=jax.ShapeDtypeStruct((NDEV, 2 * ns), F32),
                          compiler_params=_cp())(c_pad, w_ada, b_sh)


def _ada_wgrad(c_pad_t, d_ada_sh):
    ns = d_ada_sh.shape[2]

    def body(c_ref, d_ref, o_ref):
        cv = c_ref[...]
        ca = (cv * _sig(cv)).astype(BF16)
        for l in range(2):
            o_ref[l] = jnp.dot(ca, d_ref[l].astype(BF16), preferred_element_type=F32)

    return pl.pallas_call(body, name="ada_wgrad", out_shape=jax.ShapeDtypeStruct((2, D, ns), F32),
                          compiler_params=_cp())(c_pad_t, d_ada_sh)


def _sum_devices(g):
    _, R, C = g.shape

    def body(g_ref, o_ref):
        acc = g_ref[0]
        for d in range(1, NDEV):
            acc = acc + g_ref[d]
        o_ref[...] = acc

    return pl.pallas_call(body, name="sum_devices", out_shape=jax.ShapeDtypeStruct((R, C), F32),
                          compiler_params=_cp())(g)


def _adamw(w, g, m, v, name, echo=False):
    R, C = w.shape
    tr = _row_tile(R, max(8, (1 << 19) // C))

    def body(w_ref, g_ref, m_ref, v_ref, d_ref, nm_ref, nv_ref, *g_out):
        d_ref[...], nm_ref[...], nv_ref[...] = _adamw_update(w_ref[...], g_ref[...], m_ref[...], v_ref[...])
        for o_ref in g_out:
            o_ref[...] = g_ref[...]

    tile = pl.BlockSpec((tr, C), lambda i: (i, 0))
    n_out = 4 if echo else 3
    return _call(body, name=name, grid=(R // tr,), in_specs=[tile] * 4, out_specs=[tile] * n_out,
                 out_shape=[jax.ShapeDtypeStruct((R, C), F32)] * n_out, sem=("parallel",), args=(w, g, m, v))


def _adamw_many(wgmv, name, steps=4):
    n = len(wgmv)

    def body(*refs):
        ins, outs = refs[:4 * n], refs[4 * n:]
        for a in range(n):
            w_ref, g_ref, m_ref, v_ref = ins[4 * a:4 * a + 4]
            d_ref, nm_ref, nv_ref, g_out = outs[4 * a:4 * a + 4]
            d_ref[...], nm_ref[...], nv_ref[...] = _adamw_update(w_ref[...], g_ref[...], m_ref[...], v_ref[...])
            g_out[...] = g_ref[...]

    def tile(t):
        return pl.BlockSpec((t.shape[0] // steps, t.shape[1]), lambda i: (i, 0))

    flat = [t for four in wgmv for t in four]
    outs = pl.pallas_call(
        body, name=name, grid=(steps,), in_specs=[tile(t) for t in flat],
        out_specs=[tile(four[0]) for four in wgmv for _ in range(4)],
        out_shape=[jax.ShapeDtypeStruct(four[0].shape, F32) for four in wgmv for _ in range(4)],
        compiler_params=_cp(("parallel",)))(*flat)
    return [outs[4 * a:4 * a + 4] for a in range(n)]


def _adamw_update(w, g, m, v):
    nm = B1 * m + (1.0 - B1) * g
    nv = B2 * v + (1.0 - B2) * (g * g)
    m_hat = nm / (1.0 - B1 ** STEP)
    v_hat = nv / (1.0 - B2 ** STEP)
    return -LR * (m_hat / (jnp.sqrt(v_hat) + AEPS) + WD * w), nm, nv


SMALL_KEYS = ("b_ada", "g_pre", "g_post", "lb_logits", "pool_w", "pool_scale", "hgrn_norm_g")


def _small_pieces(key):
    one = lambda i: slice(i, i + 1)
    if key == "b_ada":
        return [((one(l), slice(j * D, (j + 1) * D)), (one(3 * l + j), slice(0, D)))
                for l in range(2) for j in range(3)]
    if key in ("g_pre", "g_post", "lb_logits"):
        row0 = {"g_pre": 8, "g_post": 16, "lb_logits": 24}[key]
        return [((slice(0, 2), slice(0, D)), (slice(row0, row0 + 2), slice(0, D)))]
    if key == "pool_w":
        return [((l, g, pl.ds(k, 16, stride=8), slice(0, 128)),
                 (slice(32 + 64 * l + 16 * g, 48 + 64 * l + 16 * g), slice(128 * k, 128 * (k + 1))))
                for l in range(2) for g in range(GROUPS) for k in range(8)]
    width = {"pool_scale": POOL_W, "hgrn_norm_g": HD}[key]
    row = {"pool_scale": 160, "hgrn_norm_g": 168}[key]
    return [((one(l), slice(0, width)), (one(row), slice(l * width, (l + 1) * width))) for l in range(2)]


def _adamw_small(g_small, g_lb_logits, wmv):
    n = len(SMALL_KEYS)

    def body(g_ref, glb_ref, *refs):
        ins, outs = refs[:3 * n], refs[3 * n:]
        for p, key in enumerate(SMALL_KEYS):
            w_ref, m_ref, v_ref = ins[3 * p:3 * p + 3]
            for at, (rows, lanes) in _small_pieces(key):
                gv = glb_ref[at] if key == "lb_logits" else g_ref[rows, lanes]
                res = _adamw_update(w_ref[at], gv, m_ref[at], v_ref[at])
                for o_ref, val in zip(outs[4 * p:4 * p + 4], (*res, gv)):
                    o_ref[at] = val

    flat = [t for key in SMALL_KEYS for t in wmv[key]]
    outs = pl.pallas_call(body, name="adamw_small",
                          out_shape=[jax.ShapeDtypeStruct(wmv[key][0].shape, F32) for key in SMALL_KEYS
                                     for _ in range(4)],
                          compiler_params=_cp())(g_small, g_lb_logits, *flat)
    return {key: outs[4 * p:4 * p + 4] for p, key in enumerate(SMALL_KEYS)}


def _cast_to_slot(place, w, l, name):
    _, R, C = w.shape
    tr = _row_tile(R, max(8, (1 << 19) // C))

    def body(p_ref, w_ref, o_ref):
        o_ref[...] = w_ref[...].astype(BF16)

    return pl.pallas_call(
        body, name=name, out_shape=jax.ShapeDtypeStruct((NCHIP, R, C), BF16),
        grid_spec=pltpu.PrefetchScalarGridSpec(
            num_scalar_prefetch=1, grid=(R // tr,),
            in_specs=[pl.BlockSpec((None, tr, C), lambda i, p_ref: (l, i, 0))],
            out_specs=pl.BlockSpec((None, tr, C), lambda i, p_ref: (p_ref[0], i, 0))),
        compiler_params=_cp(("parallel",)),
    )(place, w)


def _cast_to_slots(place, ws, name):
    n = len(ws)

    def body(p_ref, *refs):
        for w_ref, o_ref in zip(refs[:n], refs[n:]):
            o_ref[...] = w_ref[...].astype(BF16)

    def layer(l):
        return lambda i, p_ref: (l, 0, 0)

    return pl.pallas_call(
        body, name=name, out_shape=[jax.ShapeDtypeStruct((NCHIP,) + w.shape[1:], BF16) for w, _ in ws],
        grid_spec=pltpu.PrefetchScalarGridSpec(
            num_scalar_prefetch=1, grid=(1,),
            in_specs=[pl.BlockSpec((None,) + w.shape[1:], layer(l)) for w, l in ws],
            out_specs=[pl.BlockSpec((None,) + w.shape[1:], lambda i, p_ref: (p_ref[0], 0, 0)) for w, _ in ws]),
        compiler_params=_cp(("arbitrary",)),
    )(place, *[w for w, _ in ws])


def _pair_adds(core, gs, gots, name):
    n = len(gs)

    def body(c_ref, *refs):
        for a_ref, b_ref, o_ref in zip(refs[:n], refs[n:2 * n], refs[2 * n:]):
            o_ref[...] = (a_ref[...].astype(F32) + b_ref[...].astype(F32)).astype(o_ref.dtype)

    def whole(t):
        return pl.BlockSpec(t.shape, lambda i, c_ref: (0, 0, 0))

    return pl.pallas_call(
        body, name=name, out_shape=[jax.ShapeDtypeStruct(t.shape, BF16) for t in gots],
        grid_spec=pltpu.PrefetchScalarGridSpec(
            num_scalar_prefetch=1, grid=(1,),
            in_specs=[pl.BlockSpec(t.shape, lambda i, c_ref: (0, c_ref[0], 0)) for t in gots]
            + [whole(t) for t in gots],
            out_specs=[whole(t) for t in gots]),
        compiler_params=_cp(("arbitrary",)),
    )(core, *gs, *gots)


def _pair_add(core, g, got, name):
    _, R, C = g.shape
    r2 = R // 2
    tr = _row_tile(r2, max(8, (1 << 19) // C))
    nt = r2 // tr

    def body(c_ref, a_ref, b_ref, o_ref):
        o_ref[...] = (a_ref[...].astype(F32) + b_ref[...].astype(F32)).astype(o_ref.dtype)

    return pl.pallas_call(
        body, name=name, out_shape=jax.ShapeDtypeStruct((NCHIP, r2, C), BF16),
        grid_spec=pltpu.PrefetchScalarGridSpec(
            num_scalar_prefetch=1, grid=(NCHIP, nt),
            in_specs=[pl.BlockSpec((None, tr, C), lambda j, i, c_ref: (j, c_ref[0] * nt + i, 0)),
                      pl.BlockSpec((None, tr, C), lambda j, i, c_ref: (j, i, 0))],
            out_specs=pl.BlockSpec((None, tr, C), lambda j, i, c_ref: (j, i, 0))),
        compiler_params=_cp(("parallel", "parallel")),
    )(core, g, got)


def _sum_in_chip_order(me, own_ref, r_ref):
    own = own_ref[...].astype(F32)
    acc = None
    for j in range(NCHIP):
        slot = jnp.minimum(jnp.where(j > me, j - 1, j), NCHIP - 2)
        term = jnp.where(me == j, own, r_ref[slot].astype(F32))
        acc = term if acc is None else acc + term
    return acc


def _chip_sums(place, parts, recvs, name):
    n = len(parts)

    def body(p_ref, *refs):
        for a in range(n):
            for l in range(2):
                refs[4 * n + a][l] = _sum_in_chip_order(p_ref[0], refs[2 * a + l], refs[2 * n + 2 * a + l])

    def own(t):
        return pl.BlockSpec((None,) + t.shape[1:], lambda i, p_ref: (p_ref[0], 0, 0))

    def whole(t):
        return pl.BlockSpec(t.shape, lambda i, p_ref: (0, 0, 0))

    flat_p = [t for pair in parts for t in pair]
    flat_r = [t for pair in recvs for t in pair]
    return pl.pallas_call(
        body, name=name,
        out_shape=[jax.ShapeDtypeStruct((2, 2 * p[0].shape[1], p[0].shape[2]), F32) for p in parts],
        grid_spec=pltpu.PrefetchScalarGridSpec(
            num_scalar_prefetch=1, grid=(1,),
            in_specs=[own(t) for t in flat_p] + [whole(t) for t in flat_r],
            out_specs=[pl.BlockSpec((2,) + p[0].shape[1:], lambda i, p_ref: (0, p_ref[1], 0)) for p in parts]),
        compiler_params=_cp(("arbitrary",)),
    )(place, *flat_p, *flat_r)


def _chip_sum(place, part, recv, layer, both, name):
    _, r2, C = part.shape
    tr = _row_tile(r2, max(8, (1 << 18) // C))
    nt = r2 // tr

    def body(p_ref, own_ref, r_ref, *rest):
        rest[-1][...] = _sum_in_chip_order(p_ref[0], own_ref, r_ref)

    args = (place, part, recv) if both is None else (place, part, recv, both)
    return pl.pallas_call(
        body, name=name, out_shape=jax.ShapeDtypeStruct((2, 2 * r2, C), F32),
        grid_spec=pltpu.PrefetchScalarGridSpec(
            num_scalar_prefetch=1, grid=(nt,),
            in_specs=[pl.BlockSpec((None, tr, C), lambda i, p_ref: (p_ref[0], i, 0)),
                      pl.BlockSpec((NCHIP - 1, tr, C), lambda i, p_ref: (0, i, 0))] + [ANY] * (len(args) - 3),
            out_specs=pl.BlockSpec((None, tr, C), lambda i, p_ref: (layer, p_ref[1] * nt + i, 0))),
        input_output_aliases={} if both is None else {3: 0},
        compiler_params=_cp(("parallel",)),
    )(*args)


def _place():
    x, y, c = lax.axis_index("x"), lax.axis_index("y"), lax.axis_index("c")
    chips = [(1 - x, y), (x, 1 - y), (1 - x, 1 - y)]
    return x, y, c, chips


def _gather_small(blk, name):
    m_per, n = blk.shape

    def body(x_ref, out_ref, send_sems, recv_sems, local_sem):
        x, y, c, chips = _place()
        me, sibling = (x, y, c), (x, y, 1 - c)

        def rows(px, py, pc):
            return out_ref.at[pl.ds((4 * px + 2 * py + pc) * m_per, m_per), :]

        def copy(k, block, to, src=None):
            return pltpu.make_async_remote_copy(
                src_ref=rows(*block) if src is None else src, dst_ref=rows(*block),
                send_sem=send_sems.at[k], recv_sem=recv_sems.at[k], device_id=to, device_id_type=MESH)

        mine = pltpu.make_async_copy(x_ref, rows(*me), local_sem)
        mine.start()
        first = [copy(0, me, sibling, src=x_ref)]
        first += [copy(1 + j, me, (*chip, c), src=x_ref) for j, chip in enumerate(chips)]
        for cp in first:
            cp.start()
        passed = [copy(4 + j, (*chip, c), sibling) for j, chip in enumerate(chips)]
        for j, chip in enumerate(chips):
            copy(1 + j, (*chip, c), me).wait_recv()
            passed[j].start()
        copy(0, sibling, me).wait_recv()
        for j, chip in enumerate(chips):
            copy(4 + j, (*chip, 1 - c), me).wait_recv()
        for cp in first + passed:
            cp.wait_send()
        mine.wait()

    return pl.pallas_call(
        body, name=name, out_shape=jax.ShapeDtypeStruct((NDEV * m_per, n), blk.dtype),
        in_specs=[pl.BlockSpec(memory_space=pltpu.VMEM)], out_specs=pl.BlockSpec(memory_space=pltpu.VMEM),
        scratch_shapes=[pltpu.SemaphoreType.DMA((7,)), pltpu.SemaphoreType.DMA((7,)), pltpu.SemaphoreType.DMA],
        compiler_params=_cp(),
    )(blk)


def _gather_rows_carry(blk):
    m_per, n = blk.shape

    def rows(ref, px, py, pc):
        return ref.at[pl.ds((4 * px + 2 * py + pc) * m_per, m_per), :]

    def copy(ins, outs, send_sems, recv_sems, k, block, to, own=False):
        return pltpu.make_async_remote_copy(
            src_ref=ins[0] if own else rows(outs[0], *block), dst_ref=rows(outs[0], *block),
            send_sem=send_sems.at[k], recv_sem=recv_sems.at[k], device_id=to, device_id_type=MESH)

    def mine(ins, outs, send_sems):
        x, y, c, _ = _place()
        return pltpu.make_async_copy(ins[0], rows(outs[0], x, y, c), send_sems.at[7])

    def start(ins, outs, send_sems, recv_sems):
        x, y, c, chips = _place()
        mine(ins, outs, send_sems).start()
        copy(ins, outs, send_sems, recv_sems, 0, (x, y, c), (x, y, 1 - c), own=True).start()
        for j, chip in enumerate(chips):
            copy(ins, outs, send_sems, recv_sems, 1 + j, (x, y, c), (*chip, c), own=True).start()

    def finish(ins, outs, send_sems, recv_sems):
        x, y, c, chips = _place()
        for j, chip in enumerate(chips):
            copy(ins, outs, send_sems, recv_sems, 1 + j, (*chip, c), (x, y, c)).wait_recv()
            copy(ins, outs, send_sems, recv_sems, 4 + j, (*chip, c), (x, y, 1 - c)).start()
        copy(ins, outs, send_sems, recv_sems, 0, (x, y, 1 - c), (x, y, c)).wait_recv()
        for j, chip in enumerate(chips):
            copy(ins, outs, send_sems, recv_sems, 4 + j, (*chip, 1 - c), (x, y, c)).wait_recv()
        copy(ins, outs, send_sems, recv_sems, 0, (x, y, c), (x, y, 1 - c), own=True).wait_send()
        for j, chip in enumerate(chips):
            copy(ins, outs, send_sems, recv_sems, 1 + j, (x, y, c), (*chip, c), own=True).wait_send()
            copy(ins, outs, send_sems, recv_sems, 4 + j, (*chip, c), (x, y, 1 - c)).wait_send()
        mine(ins, outs, send_sems).wait()

    return _Carry([blk], [jax.ShapeDtypeStruct((NDEV * m_per, n), blk.dtype)], {}, 8, start, finish)


def _gather_carry(shards, piece=(0, 1, 1)):
    n = len(shards)
    first, count, of = piece

    def rows(ref, half):
        r2 = ref.shape[1] // 2
        return pl.ds(half * r2 + first * (r2 // of), count * (r2 // of))

    def over_ici(outs, send_sems, recv_sems, a, j, chip_xy, slot):
        x, y, c, _ = _place()
        blk = outs[a].at[slot, rows(outs[a], c), :]
        return pltpu.make_async_remote_copy(
            src_ref=blk, dst_ref=blk, send_sem=send_sems.at[6 * a + j], recv_sem=recv_sems.at[6 * a + j],
            device_id=(*chip_xy, c), device_id_type=MESH)

    def over_d2d(outs, send_sems, recv_sems, a, j, slot, half):
        x, y, c, _ = _place()
        blk = outs[a].at[slot, rows(outs[a], half), :]
        return pltpu.make_async_remote_copy(
            src_ref=blk, dst_ref=blk, send_sem=send_sems.at[6 * a + 3 + j], recv_sem=recv_sems.at[6 * a + 3 + j],
            device_id=(x, y, 1 - c), device_id_type=MESH)

    def start(ins, outs, send_sems, recv_sems):
        x, y, c, chips = _place()
        for a in range(n):
            for j, chip_xy in enumerate(chips):
                over_ici(outs, send_sems, recv_sems, a, j, chip_xy, 2 * x + y).start()

    def finish(ins, outs, send_sems, recv_sems):
        x, y, c, chips = _place()
        for a in range(n):
            for j, (cx, cy) in enumerate(chips):
                over_ici(outs, send_sems, recv_sems, a, j, (cx, cy), 2 * cx + cy).wait_recv()
                over_d2d(outs, send_sems, recv_sems, a, j, 2 * cx + cy, c).start()
        for a in range(n):
            for j, (cx, cy) in enumerate(chips):
                over_d2d(outs, send_sems, recv_sems, a, j, 2 * cx + cy, 1 - c).wait_recv()
        for a in range(n):
            for j, (cx, cy) in enumerate(chips):
                over_ici(outs, send_sems, recv_sems, a, j, (cx, cy), 2 * x + y).wait_send()
                over_d2d(outs, send_sems, recv_sems, a, j, 2 * cx + cy, c).wait_send()

    return _Carry(shards, [jax.ShapeDtypeStruct(s.shape, s.dtype) for s in shards],
                  {a: a for a in range(n)}, 6 * n, start, finish)


def _rs_pair(grads, name):
    n = len(grads)

    def body(*refs):
        ins, gots = refs[:n], refs[n:2 * n]
        send_sems, recv_sems = refs[2 * n:]
        x, y, c, _ = _place()
        cps = []
        for a in range(n):
            r2 = ins[a].shape[1] // 2
            cp = pltpu.make_async_remote_copy(
                src_ref=ins[a].at[:, pl.ds((1 - c) * r2, r2), :], dst_ref=gots[a],
                send_sem=send_sems.at[a], recv_sem=recv_sems.at[a],
                device_id=(x, y, 1 - c), device_id_type=MESH)
            cp.start()
            cps.append(cp)
        for cp in cps:
            cp.wait()

    half = [jax.ShapeDtypeStruct((NCHIP, g.shape[1] // 2, g.shape[2]), g.dtype) for g in grads]
    return pl.pallas_call(
        body, name=name, out_shape=half, in_specs=[ANY] * n, out_specs=[ANY] * n,
        scratch_shapes=[pltpu.SemaphoreType.DMA((n,)), pltpu.SemaphoreType.DMA((n,))],
        compiler_params=_cp(),
    )(*grads)


def _pair_sum(core, g, name):
    _, R, C = g.shape
    r2 = R // 2

    def body(c_ref, g_ref, own_ref, o_ref, got_ref, buf, send_sems, recv_sems, local_sems):
        j = pl.program_id(0)
        x, y, c, _ = _place()

        def remote(k):
            return pltpu.make_async_remote_copy(
                src_ref=g_ref.at[k, pl.ds((1 - c) * r2, r2), :], dst_ref=got_ref.at[k],
                send_sem=send_sems.at[k], recv_sem=recv_sems.at[k], device_id=(x, y, 1 - c), device_id_type=MESH)

        def fetch(k):
            return pltpu.make_async_copy(got_ref.at[k], buf.at[k % 2], local_sems.at[k % 2])

        @pl.when(j == 0)
        def _():
            for k in range(NCHIP):
                remote(k).start()
            remote(0).wait_recv()
            fetch(0).start()

        fetch(j).wait()

        @pl.when(j + 1 < NCHIP)
        def _():
            remote(j + 1).wait_recv()
            fetch(j + 1).start()

        o_ref[...] = (own_ref[...].astype(F32) + buf[j % 2].astype(F32)).astype(o_ref.dtype)

        @pl.when(j == NCHIP - 1)
        def _():
            for k in range(NCHIP):
                remote(k).wait_send()

    half = jax.ShapeDtypeStruct((NCHIP, r2, C), g.dtype)
    return pl.pallas_call(
        body, name=name, out_shape=[half, half],
        grid_spec=pltpu.PrefetchScalarGridSpec(
            num_scalar_prefetch=1, grid=(NCHIP,),
            in_specs=[ANY, pl.BlockSpec((None, r2, C), lambda j, c_ref: (j, c_ref[0], 0))],
            out_specs=[pl.BlockSpec((None, r2, C), lambda j, c_ref: (j, 0, 0)), ANY],
            scratch_shapes=[pltpu.VMEM((2, r2, C), g.dtype), pltpu.SemaphoreType.DMA((NCHIP,)),
                            pltpu.SemaphoreType.DMA((NCHIP,)), pltpu.SemaphoreType.DMA((2,))]),
        compiler_params=_cp(("arbitrary",)),
    )(core, g, g)[0]


def _pair_carry(grads):
    n = len(grads)

    def copy(ins, outs, send_sems, recv_sems, a):
        x, y, c, _ = _place()
        r2 = ins[a].shape[1] // 2
        return pltpu.make_async_remote_copy(
            src_ref=ins[a].at[:, pl.ds((1 - c) * r2, r2), :], dst_ref=outs[a],
            send_sem=send_sems.at[a], recv_sem=recv_sems.at[a],
            device_id=(x, y, 1 - c), device_id_type=MESH)

    def start(ins, outs, send_sems, recv_sems):
        for a in range(n):
            copy(ins, outs, send_sems, recv_sems, a).start()

    def finish(ins, outs, send_sems, recv_sems):
        for a in range(n):
            copy(ins, outs, send_sems, recv_sems, a).wait()

    half = [jax.ShapeDtypeStruct((NCHIP, g.shape[1] // 2, g.shape[2]), g.dtype) for g in grads]
    return _Carry(grads, half, {}, n, start, finish)


def _chips_carry(parts, piece=(0, 1, 1), into=None):
    n = len(parts)
    first, count, of = piece

    def rows(ref):
        step = ref.shape[1] // of
        return pl.ds(first * step, count * step)

    def send(ins, outs, send_sems, recv_sems, a, j, chip_xy):
        x, y, c, _ = _place()
        me, them = 2 * x + y, 2 * chip_xy[0] + chip_xy[1]
        return pltpu.make_async_remote_copy(
            src_ref=ins[a].at[them, rows(ins[a]), :],
            dst_ref=outs[a].at[me - (me > them).astype(jnp.int32), rows(outs[a]), :],
            send_sem=send_sems.at[3 * a + j], recv_sem=recv_sems.at[3 * a + j],
            device_id=(*chip_xy, c), device_id_type=MESH)

    def start(ins, outs, send_sems, recv_sems):
        _, _, _, chips = _place()
        for a in range(n):
            for j, chip_xy in enumerate(chips):
                send(ins, outs, send_sems, recv_sems, a, j, chip_xy).start()

    def finish(ins, outs, send_sems, recv_sems):
        x, y, c, chips = _place()
        me = 2 * x + y
        for a in range(n):
            for j, (cx, cy) in enumerate(chips):
                them = 2 * cx + cy
                blk = outs[a].at[them - (them > me).astype(jnp.int32), rows(outs[a]), :]
                pltpu.make_async_remote_copy(
                    src_ref=blk, dst_ref=blk, send_sem=send_sems.at[3 * a + j], recv_sem=recv_sems.at[3 * a + j],
                    device_id=(cx, cy, c), device_id_type=MESH).wait_recv()
        for a in range(n):
            for j, chip_xy in enumerate(chips):
                send(ins, outs, send_sems, recv_sems, a, j, chip_xy).wait_send()

    landing = [jax.ShapeDtypeStruct((NCHIP - 1,) + p.shape[1:], p.dtype) for p in parts]
    if into is None:
        return _Carry(parts, landing, {}, 3 * n, start, finish)
    return _Carry(list(parts) + list(into), landing, {n + a: a for a in range(n)}, 3 * n, start, finish)


def _swap_carry(fulls, layers):
    n = len(fulls)

    def copy(outs, send_sems, recv_sems, a, half):
        x, y, c, _ = _place()
        r2 = outs[a].shape[1] // 2
        blk = outs[a].at[pl.ds(*layers[a]), pl.ds(half * r2, r2), :]
        return pltpu.make_async_remote_copy(
            src_ref=blk, dst_ref=blk, send_sem=send_sems.at[a], recv_sem=recv_sems.at[a],
            device_id=(x, y, 1 - c), device_id_type=MESH)

    def start(ins, outs, send_sems, recv_sems):
        c = lax.axis_index("c")
        for a in range(n):
            copy(outs, send_sems, recv_sems, a, c).start()

    def finish(ins, outs, send_sems, recv_sems):
        c = lax.axis_index("c")
        for a in range(n):
            copy(outs, send_sems, recv_sems, a, 1 - c).wait_recv()
        for a in range(n):
            copy(outs, send_sems, recv_sems, a, c).wait_send()

    return _Carry(fulls, [jax.ShapeDtypeStruct(f.shape, f.dtype) for f in fulls], {a: a for a in range(n)}, n,
                  start, finish)


def _rs_swap(fulls, layers):
    n = len(fulls)
    swap = _swap_carry(fulls, layers)

    def body(*refs):
        outs, (send_sems, recv_sems) = refs[n:2 * n], refs[2 * n:]
        swap.start(None, outs, send_sems, recv_sems)
        swap.finish(None, outs, send_sems, recv_sems)

    return pl.pallas_call(
        body, name="rs_swap", out_shape=swap.outs, in_specs=[ANY] * n, out_specs=[ANY] * n,
        input_output_aliases=swap.aliases,
        scratch_shapes=[pltpu.SemaphoreType.DMA((n,)), pltpu.SemaphoreType.DMA((n,))],
        compiler_params=_cp(),
    )(*fulls)


def _tail_weight_grads(merged_t, b_in_t, a_in_t, dy, dbr_b, dbr_a, name, tn=256):
    S = dy.shape[0]
    nn = D // tn

    def body(mt_ref, bt_ref, at_ref, dy_ref, db_ref, da_ref, go_ref, gh_ref, gp_ref):
        go_ref[...] = jnp.dot(mt_ref[...], dy_ref[...], preferred_element_type=F32).astype(BF16)
        gh_ref[...] = jnp.dot(bt_ref[...], db_ref[...], preferred_element_type=F32).astype(BF16)
        gp_ref[...] = jnp.dot(at_ref[...], da_ref[...], preferred_element_type=F32).astype(BF16)

    left = lambda rows: pl.BlockSpec((rows, S), lambda n: (0, 0))
    right = pl.BlockSpec((S, tn), lambda n: (0, n))
    out = pl.BlockSpec((D, tn), lambda n: (0, n))
    return pl.pallas_call(
        body, name=name, grid=(nn,), in_specs=[left(D), left(D), left(POOL_W), right, right, right],
        out_specs=[out, out, pl.BlockSpec((None, POOL_W, tn), lambda n: (n, 0, 0))],
        out_shape=[jax.ShapeDtypeStruct((D, D), BF16), jax.ShapeDtypeStruct((D, D), BF16),
                   jax.ShapeDtypeStruct((NCHIP, POOL_W, D // NCHIP), BF16)],
        compiler_params=_cp(("parallel",)),
    )(merged_t, b_in_t, a_in_t, dy, dbr_b, dbr_a)


class _GatherInProj:
    def __init__(self, slot, order):
        self.slot, self.order = slot, order


def _proj_with_gather(h, w_slot, order, name, tn=256):
    S, K = h.shape
    nsh, _, ns = w_slot.shape
    tps = ns // tn
    nt = nsh * tps
    r2 = K // 2

    def body(ord_ref, h_ref, w_in_ref, o_ref, w_ref, wbuf, tile_sems, send_sems, recv_sems):
        n = pl.program_id(0)
        x, y, c, chips = _place()

        def half(slot, which):
            return w_ref.at[slot, pl.ds(which * r2, r2), :]

        def over_ici(j, slot):
            blk = half(slot, c)
            return pltpu.make_async_remote_copy(src_ref=blk, dst_ref=blk, send_sem=send_sems.at[j],
                                                recv_sem=recv_sems.at[j], device_id=(*chips[j], c),
                                                device_id_type=MESH)

        def over_d2d(j, which):
            blk = half(2 * chips[j][0] + chips[j][1], which)
            return pltpu.make_async_remote_copy(src_ref=blk, dst_ref=blk, send_sem=send_sems.at[3 + j],
                                                recv_sem=recv_sems.at[3 + j], device_id=(x, y, 1 - c),
                                                device_id_type=MESH)

        def tile_copy(step, slot):
            shard = ord_ref[step // tps]
            return pltpu.make_async_copy(w_ref.at[shard, :, pl.ds((step % tps) * tn, tn)], wbuf.at[slot],
                                         tile_sems.at[slot])

        @pl.when(n == 0)
        def _():
            for j in range(3):
                over_ici(j, 2 * x + y).start()
            tile_copy(0, 0).start()

        for j in range(3):
            @pl.when(n == (j + 1) * tps - 1)
            def _(j=j):
                over_ici(j, 2 * chips[j][0] + chips[j][1]).wait_recv()
                over_d2d(j, c).start()
                over_d2d(j, 1 - c).wait_recv()

        @pl.when(n + 1 < nt)
        def _():
            tile_copy(n + 1, (n + 1) % 2).start()

        tile_copy(n, n % 2).wait()
        o_ref[...] = jnp.dot(h_ref[...], wbuf[n % 2], preferred_element_type=F32).astype(o_ref.dtype)

        @pl.when(n == nt - 1)
        def _():
            for j in range(3):
                over_ici(j, 2 * x + y).wait_send()
                over_d2d(j, c).wait_send()

    return pl.pallas_call(
        body, name=name,
        out_shape=[jax.ShapeDtypeStruct((S, nsh * ns), BF16), jax.ShapeDtypeStruct(w_slot.shape, w_slot.dtype)],
        grid_spec=pltpu.PrefetchScalarGridSpec(
            num_scalar_prefetch=1, grid=(nt,),
            in_specs=[pl.BlockSpec((S, K), lambda n, o_ref: (0, 0)), ANY],
            out_specs=[pl.BlockSpec((S, tn), lambda n, o_ref: (0, o_ref[n // tps] * tps + n % tps)), ANY],
            scratch_shapes=[pltpu.VMEM((2, K, tn), w_slot.dtype), pltpu.SemaphoreType.DMA((2,)),
                            pltpu.SemaphoreType.DMA((6,)), pltpu.SemaphoreType.DMA((6,))]),
        input_output_aliases={2: 1},
        compiler_params=_cp(("arbitrary",)),
    )(order, h, w_slot)


def _mm_ride(a, b, carry, **kw):
    if carry is None:
        return _mm(a, b, **kw), []
    return _mm(a, b, carry=carry, **kw)


def _layer_fwd(l, x, ada, w, small, ride, target=None):
    shift, scale, gate = ada[:, 0:D], ada[:, D:2 * D], ada[:, 2 * D:3 * D]
    carry, landed = ride("prenorm")
    (h, h_t), outs = _prenorm_fwd(x, small["g_pre"][l], scale, shift, f"prenorm_fwd{l}", carry)
    landed(outs)
    carry, landed = ride("proj")
    if isinstance(carry, _GatherInProj):
        proj, full = _proj_with_gather(h, carry.slot, carry.order, f"proj{l}")
        outs = [full]
    else:
        proj, outs = _mm_ride(h, w["w_in"][l], carry, name=f"proj{l}", b_mode="nn_sh", tm=2048, out_dtype=BF16)
    landed(outs)
    a_in, a_in_t = _pool_fwd(proj, small["pool_w"][l], small["pool_scale"][l], f"pool_fwd{l}")
    carry, landed = ride("hgrn")
    (b_in, b_in_t, o_raw, states, mild, cum), outs = _hgrn_fwd(proj, small["lb"][l], small["hgrn_norm_g"][l],
                                                              f"hgrn_fwd{l}", carry=carry)
    landed(outs)
    carry, landed = ride("tail")
    (br_a, br_b, merged_t, y, *x_new), outs = _layer_tail_fwd(
        proj, a_in, b_in, x, w["w_pool_o"][l], w["w_hgrn_o"][l].reshape(D, D), w["w_out"][l].reshape(D, D),
        gate, small["g_post"][l], f"tail_fwd{l}", target=target, carry=carry)
    landed(outs)
    saved = dict(x=x, h_t=h_t, proj=proj, a_in_t=a_in_t, b_in_t=b_in_t, o_raw=o_raw, states=states, mild=mild,
                 cum=cum,
                 br_a=br_a, br_b=br_b, merged_t=merged_t, y=y, scale=scale, gate=gate)
    return x_new, saved


def _layer_bwd(l, dxn, sv, w, small, ride):
    carry, landed = ride["head"](None)
    (dy, dbr_a, dbr_b, dproj, da_in, db_in, dgate, dg_post), outs = _layer_head_bwd(
        dxn, sv["y"], sv["proj"], sv["br_a"], sv["br_b"], w["w_pool_o"][l], w["w_hgrn_o"][l].reshape(D, D),
        w["w_out"][l].reshape(D, D), sv["gate"], small["g_post"][l], f"head_bwd{l}", carry)
    landed(outs)
    gw_out, gw_hgrn_o, gw_pool_o = _tail_weight_grads(sv["merged_t"], sv["b_in_t"], sv["a_in_t"], dy, dbr_b,
                                                      dbr_a, f"gw_tail{l}")
    big = dict(w_pool_o=gw_pool_o, w_hgrn_o=gw_hgrn_o.reshape(NCHIP, D // NCHIP, D),
               w_out=gw_out.reshape(NCHIP, D // NCHIP, D))
    carry, landed = ride["hgrn"](big)
    (dproj, dlb, dgn), outs = _hgrn_bwd(db_in, sv["proj"], sv["o_raw"], sv["states"], sv["mild"], sv["cum"],
                                        small["lb"][l], small["hgrn_norm_g"][l], dproj, f"hgrn_bwd{l}",
                                        carry=carry)
    landed(outs)
    dproj, dpw, dpsc = _pool_bwd(da_in, sv["proj"], small["pool_w"][l], small["pool_scale"][l], dproj,
                                 f"pool_bwd{l}")
    little = dict(dgate=dgate, g_post=dg_post, pool_w=dpw, pool_scale=dpsc, lb=dlb,
                  hgrn_norm_g=jnp.sum(dgn, axis=0, keepdims=True))
    carry, landed = ride["gw_in"](little)
    big["w_in"], outs = _mm_ride(sv["h_t"], dproj, carry, name=f"gw_in{l}", out_shards=NCHIP, out_dtype=BF16)
    landed(outs)
    carry, landed = ride["d_h"](big)
    dh, outs = _mm_ride(dproj, w["w_in"][l], carry, name=f"d_h{l}", b_mode="nt_shk", tn=1024)
    landed(outs)
    carry, landed = ride["prenorm"](big)
    (dx, dshift, dscale, dg_pre), outs = _prenorm_bwd(dh, dxn, sv["x"], small["g_pre"][l], sv["scale"],
                                                      f"prenorm_bwd{l}", carry)
    landed(outs)
    little.update(dshift=dshift, dscale=dscale, g_pre=dg_pre)
    return dx, big, little


SMALL_ROWS = 176


def _rows8(t):
    t = t.reshape(-1, D)
    return jnp.pad(t, ((0, -t.shape[0] % 8), (0, 0)))


def _pack_small(parts):
    row_keys = ("dshift", "dscale", "dgate", "g_pre", "g_post", "lb", "pool_scale", "hgrn_norm_g")
    flat = [p[k] for p in parts for k in row_keys] + [p["pool_w"] for p in parts]
    nk = len(row_keys)

    def body(*refs):
        o_ref = refs[-1]
        o_ref[...] = jnp.zeros((SMALL_ROWS, D), F32)
        for l in range(2):
            dshift, dscale, dgate, g_pre, g_post, lb, pscale, gn = refs[l * nk:(l + 1) * nk]
            for r, ref in enumerate((dshift, dscale, dgate)):
                o_ref[3 * l + r:3 * l + r + 1, :] = ref[...]
            o_ref[8 + l:9 + l, :] = g_pre[...]
            o_ref[16 + l:17 + l, :] = g_post[...]
            o_ref[24 + l:25 + l, :] = lb[...]
            o_ref[160:161, l * POOL_W:(l + 1) * POOL_W] = pscale[...]
            o_ref[168:169, l * HD:(l + 1) * HD] = gn[...]
        for (l, *at), (rows, lanes) in _small_pieces("pool_w"):
            o_ref[rows, lanes] = refs[2 * nk + l][tuple(at)]

    return pl.pallas_call(body, name="pack_small", out_shape=jax.ShapeDtypeStruct((SMALL_ROWS, D), F32),
                          compiler_params=_cp())(*flat)


def kernel(x, c, w_ada, b_ada, g_pre, g_post, w_in, pool_w, pool_scale, lb_logits, hgrn_norm_g, w_pool_o, w_hgrn_o, w_out, loss_target, m_w_ada, m_b_ada, m_g_pre, m_g_post, m_w_in, m_pool_w, m_pool_scale, m_lb_logits, m_hgrn_norm_g, m_w_pool_o, m_w_hgrn_o, m_w_out, v_w_ada, v_b_ada, v_g_pre, v_g_post, v_w_in, v_pool_w, v_pool_scale, v_lb_logits, v_hgrn_norm_g, v_w_pool_o, v_w_hgrn_o, v_w_out):
    ax, ay, ac = lax.axis_index("x"), lax.axis_index("y"), lax.axis_index("c")
    chip = 2 * ax + ay
    dev = 2 * chip + ac
    xe, te = x[0], loss_target[0]
    ada_s = w_ada.shape[2]

    big_names = ("w_in", "w_pool_o", "w_hgrn_o", "w_out")
    big_w = (w_in, w_pool_o, w_hgrn_o, w_out)
    core = jnp.stack([ac]).astype(jnp.int32)
    place = jnp.stack([chip, ac]).astype(jnp.int32)
    slots = {("w_in", l): _cast_to_slot(place, w_in, l, f"cast_w_in{l}") for l in range(2)}
    rest = [(k, l) for l in range(2) for k in big_names[1:]]
    slots.update(zip(rest, _cast_to_slots(place, [(dict(zip(big_names, big_w))[k], l) for k, l in rest],
                                          "cast_rest")))
    w = {k: [None, None] for k in big_names}
    def fills(keys):
        def landed(outs):
            for (k, l), o in zip(keys, outs):
                w[k][l] = slots[k, l] = o
        return landed

    rest0 = [(k, 0) for k in big_names[1:]]
    rest1 = [(k, 1) for k in big_names[1:]]
    no_carry = (None, lambda outs: None)
    order = jnp.stack([chip, 2 * (1 - ax) + ay, 2 * ax + (1 - ay), 2 * (1 - ax) + (1 - ay)]).astype(jnp.int32)

    def ride_fwd0(stage):
        if stage == "proj":
            return _GatherInProj(slots["w_in", 0], order), fills([("w_in", 0)])
        if stage == "hgrn":
            return (_join_carries(_gather_carry([slots[t] for t in rest0]),
                                  _gather_carry([slots["w_in", 1]], piece=(0, 2, 4))),
                    fills(rest0 + [("w_in", 1)]))
        if stage == "tail":
            return _gather_carry([slots["w_in", 1]], piece=(2, 1, 4)), fills([("w_in", 1)])
        return no_carry

    def ride_fwd1(stage):
        if stage == "prenorm":
            return _gather_carry([slots["w_in", 1]], piece=(3, 1, 4)), fills([("w_in", 1)])
        if stage == "hgrn":
            return _gather_carry([slots[t] for t in rest1]), fills(rest1)
        return no_carry

    c_all = _gather_small(jnp.broadcast_to(c, (8, D)), "gather_c").reshape(NDEV, 8, D)[:, 0, :]
    c_pad = jnp.pad(c_all, ((0, ADA_PAD - NDEV), (0, 0)))
    b_sh = lax.dynamic_slice(b_ada, (0, chip * ada_s), (2, ada_s))
    ada_cols = _gather_small(_ada_fwd(c_pad, w_ada, b_sh), "gather_ada")
    ada_cols = ada_cols.reshape(NCHIP, 2, NDEV, 2, ada_s)[:, 0]
    ada_all = jnp.transpose(ada_cols, (2, 1, 0, 3)).reshape(2, NDEV, 3 * D)
    ada_me = lax.dynamic_slice(ada_all, (0, dev, 0), (2, 1, 3 * D))

    lbs = _lb_fwd(lb_logits)
    small = dict(g_pre=g_pre[:, None, :], g_post=g_post[:, None, :], pool_w=pool_w,
                 pool_scale=pool_scale[:, None, :], lb=lbs[:, None, :], hgrn_norm_g=hgrn_norm_g[:, None, :])

    (x1,), sv0 = _layer_fwd(0, xe, ada_me[0], w, small, ride_fwd0)
    (dx2, loss_blk), sv1 = _layer_fwd(1, x1, ada_me[1], w, small, ride_fwd1, target=te)

    parts, recv, held = {}, {}, {}

    def pair_ride(keys, grads):
        def landed(outs):
            held.update({kl: (g, o) for kl, g, o in zip(keys, grads, outs)})
        return _pair_carry(grads), landed

    def pair_adds(keys):
        gs, gots = zip(*[held.pop(kl) for kl in keys])
        if keys[0][0] == "w_in":
            parts[keys[0]] = _pair_add(core, gs[0], gots[0], f"rs_add_w_in{keys[0][1]}")
        else:
            parts.update(zip(keys, _pair_adds(core, gs, gots, f"rs_add_early{keys[0][1]}")))

    def exchange(keys):
        def landed(outs):
            recv.update(zip(keys, outs))
        return _chips_carry([parts[kl] for kl in keys]), landed

    def share(key, first, count):
        def landed(outs):
            (recv[key],) = outs
        into = [recv[key]] if key in recv else None
        return _chips_carry([parts[key]], piece=(first, count, 8), into=into), landed

    def together(*rides):
        carries, fns = zip(*rides)

        def landed(outs):
            for cr, fn in zip(carries, fns):
                fn(outs[:len(cr.outs)])
                outs = outs[len(cr.outs):]
        return _join_carries(*carries), landed

    def early(l):
        return [(k, l) for k in big_names[1:]]

    def pair_alone(keys, grads, tag):
        held.update({kl: (g, o) for kl, g, o in zip(keys, grads, _rs_pair(grads, f"rs_pair_{tag}"))})
        pair_adds(keys)

    def ride_hgrn1(big):
        return pair_ride(early(1), [big[k] for k in big_names[1:]])

    def ride_gw_in1(_):
        pair_adds(early(1))
        return exchange(early(1))

    def ride_d_h1(big):
        return pair_ride([("w_in", 1)], [big["w_in"]])

    def ride_prenorm1(_):
        pair_adds([("w_in", 1)])
        return share(("w_in", 1), 0, 1)

    def ride_head0(_):
        return share(("w_in", 1), 1, 3)

    def ride_hgrn0(big):
        pair_alone(early(0), [big[k] for k in big_names[1:]], "early0")
        return together(exchange(early(0)), share(("w_in", 1), 4, 4))

    def ride_d_h0(big):
        parts["w_in", 0] = _pair_sum(core, big["w_in"], "rs_pair_sum_w_in0")
        return share(("w_in", 0), 0, 4)

    def ride_prenorm0(_):
        return share(("w_in", 0), 4, 4)

    no_ride = lambda so_far: no_carry
    dx1, big1, little1 = _layer_bwd(1, dx2, sv1, w, small, dict(head=no_ride, hgrn=ride_hgrn1, gw_in=ride_gw_in1,
                                                                d_h=ride_d_h1, prenorm=ride_prenorm1))

    gathered = {}
    zero_row = jnp.zeros((1, D), F32)

    def ride_gw_in0(little):
        so_far = dict(little, dshift=zero_row, dscale=zero_row, g_pre=zero_row)

        def landed(outs):
            (gathered["early"],) = outs

        red = [_chip_sum(place, parts["w_in", 1], recv["w_in", 1], 1, None, "rs_sum_w_in1")]
        red += _chip_sums(place, [[parts[k, l] for l in range(2)] for k in big_names[1:]],
                          [[recv[k, l] for l in range(2)] for k in big_names[1:]], "rs_sum_early")

        def swapped(outs):
            gathered["sums"] = outs
        return together((_gather_rows_carry(_pack_small([so_far, little1])), landed),
                        (_swap_carry(red, [(1, 1)] + [(0, 2)] * 3), swapped))

    dx0, big0, little0 = _layer_bwd(0, dx1, sv0, w, small,
                                    dict(head=ride_head0, hgrn=ride_hgrn0, gw_in=ride_gw_in0, d_h=ride_d_h0,
                                         prenorm=ride_prenorm0))
    loss_row = jnp.broadcast_to(loss_blk[0:1, 0:1], (1, D))
    late = _rows8(jnp.stack([little0["dshift"], little0["dscale"], little0["g_pre"], loss_row]))
    late = _gather_small(late, "gather_small_late").reshape(NDEV, 8, D)
    loss = jnp.sum(late[:, 3, 0])
    packed = gathered["early"].reshape(NDEV, SMALL_ROWS, D)
    packed = packed.at[:, 0:2, :].set(late[:, 0:2, :]).at[:, 8:9, :].set(late[:, 2:3, :])
    red = _chip_sum(place, parts["w_in", 0], recv["w_in", 0], 0, gathered["sums"][0], "rs_sum_w_in0")
    g_big = dict(zip(big_names, list(_rs_swap([red], [(0, 1)])) + list(gathered["sums"][1:])))

    def two(t):
        return t.reshape(-1, t.shape[-1])

    def upd(wt, g, m, v, name, echo=False):
        return [t.reshape(wt.shape) for t in _adamw(two(wt), two(g), two(m), two(v), name, echo)]

    *u_w_in, g_w_in = upd(w_in, g_big["w_in"], m_w_in, v_w_in, "adamw_w_in", echo=True)
    g_small = _sum_devices(packed)
    g_lb_logits = _lb_bwd(lb_logits, g_small[24:26])
    d_ada_all = packed[:, 0:6, :].reshape(NDEV, 2, 3 * D)
    d_ada_sh = lax.dynamic_slice(jnp.transpose(d_ada_all, (1, 0, 2)), (0, 0, chip * ada_s), (2, NDEV, ada_s))
    d_ada_sh = jnp.pad(d_ada_sh, ((0, 0), (0, ADA_PAD - NDEV), (0, 0)))
    g_w_ada = _ada_wgrad(c_pad.T, d_ada_sh)

    u_w_ada = upd(w_ada, g_w_ada, m_w_ada, v_w_ada, "adamw_w_ada")
    early_w = dict(w_pool_o=(w_pool_o, m_w_pool_o, v_w_pool_o), w_hgrn_o=(w_hgrn_o, m_w_hgrn_o, v_w_hgrn_o),
                   w_out=(w_out, m_w_out, v_w_out))
    u_early = _adamw_many([(two(early_w[k][0]), two(g_big[k]), two(early_w[k][1]), two(early_w[k][2]))
                           for k in big_names[1:]], "adamw_early")
    (*u_w_pool_o, g_w_pool_o), (*u_w_hgrn_o, g_w_hgrn_o), (*u_w_out, g_w_out) = [
        [t.reshape(early_w[k][0].shape) for t in four] for k, four in zip(big_names[1:], u_early)]
    small_w = dict(b_ada=(b_ada, m_b_ada, v_b_ada), g_pre=(g_pre, m_g_pre, v_g_pre),
                   g_post=(g_post, m_g_post, v_g_post), lb_logits=(lb_logits, m_lb_logits, v_lb_logits),
                   pool_w=(pool_w, m_pool_w, v_pool_w), pool_scale=(pool_scale, m_pool_scale, v_pool_scale),
                   hgrn_norm_g=(hgrn_norm_g, m_hgrn_norm_g, v_hgrn_norm_g))
    u_small = _adamw_small(g_small, g_lb_logits, small_w)
    s = lambda key: u_small[key][3]
    grads_out = (g_w_ada, s("b_ada"), s("g_pre"), s("g_post"), g_w_in, s("pool_w"), s("pool_scale"), g_lb_logits,
                 s("hgrn_norm_g"), g_w_pool_o, g_w_hgrn_o, g_w_out)

    def ordered(k):
        s = lambda key: u_small[key][k]
        return (u_w_ada[k], s("b_ada"), s("g_pre"), s("g_post"), u_w_in[k], s("pool_w"), s("pool_scale"),
                s("lb_logits"), s("hgrn_norm_g"), u_w_pool_o[k], u_w_hgrn_o[k], u_w_out[k])

    return (loss, dx0[None], *grads_out, *ordered(0), *ordered(1), *ordered(2))
```

```python
import functools

import jax
import jax.numpy as jnp
from jax import lax
from jax.experimental import pallas as pl
from jax.experimental.pallas import tpu as pltpu

F32 = jnp.float32
BF16 = jnp.bfloat16
MESH = pl.DeviceIdType.MESH

D = 1024
HEADS = 8
HD = 128
GROUPS = 4
POOL_W = 512
CH = 128
SB_WIDE = 32
SB = 16
NH = 2
IN_W = 7168
NCHIP = 4
NDEV = 8
EPS = 1e-6
PV0, PG0, HQ0, HF0, HI0, HG0 = 0, 4, 8, 16, 24, 32
MGP_BLK, MGH_BLK = 5, 6

LR, B1, B2, AEPS, WD, STEP = 0.001, 0.9, 0.999, 1e-08, 0.01, 10
VMEM_LIMIT = 56 * 1024 * 1024


def _cp(sem=None, **kw):
    if sem is not None:
        kw["dimension_semantics"] = sem
    return pltpu.CompilerParams(vmem_limit_bytes=VMEM_LIMIT, **kw)


def _sig(z):
    return 1.0 / (1.0 + jnp.exp(-z))


def _dsilu(z, s):
    return s * (1.0 + z * (1.0 - s))


def _row_tile(rows, cap):
    if rows <= cap:
        return rows
    t = 1 << (cap.bit_length() - 1)
    while rows % t:
        t //= 2
    return t


ANY = pl.BlockSpec(memory_space=pl.ANY)


class _Carry:
    def __init__(self, ins, outs, aliases, n_sem, start, finish):
        self.ins, self.outs, self.aliases, self.n_sem = list(ins), list(outs), dict(aliases), n_sem
        self.start, self.finish = start, finish


class _SemWindow:
    def __init__(self, ref, base):
        self._ref, self._base = ref, base

    @property
    def at(self):
        return self

    def __getitem__(self, k):
        return self._ref.at[self._base + k]


def _join_carries(*carries):
    ins, outs, aliases, spans, n_sem = [], [], {}, [], 0
    for cr in carries:
        aliases.update({len(ins) + i: len(outs) + o for i, o in cr.aliases.items()})
        spans.append((len(ins), len(cr.ins), len(outs), len(cr.outs), n_sem))
        ins, outs, n_sem = ins + cr.ins, outs + cr.outs, n_sem + cr.n_sem

    def run(which):
        def fn(i_refs, o_refs, send_sems, recv_sems):
            for cr, (i0, ni, o0, no, s0) in zip(carries, spans):
                getattr(cr, which)(i_refs[i0:i0 + ni], o_refs[o0:o0 + no], _SemWindow(send_sems, s0),
                                   _SemWindow(recv_sems, s0))
        return fn

    return _Carry(ins, outs, aliases, n_sem, run("start"), run("finish"))


def _call(body, *, name, grid, in_specs, out_specs, out_shape, args, scratch_shapes=(), sem=None, carry=None,
          aliases=None):
    in_specs, out_specs, out_shape = list(in_specs), list(out_specs), list(out_shape)
    scratch_shapes = list(scratch_shapes)
    aliases = dict(aliases or {})
    if carry is None:
        outs = pl.pallas_call(body, name=name, grid=grid, in_specs=in_specs, out_specs=out_specs,
                              out_shape=out_shape, scratch_shapes=scratch_shapes, input_output_aliases=aliases,
                              compiler_params=_cp(sem))(*args)
        return list(outs)
    n_in, n_out, n_scr = len(in_specs), len(out_specs), len(scratch_shapes)
    c_in, c_out = len(carry.ins), len(carry.outs)

    def wrapped(*refs):
        k_in, rest = refs[:n_in], refs[n_in:]
        ci, rest = rest[:c_in], rest[c_in:]
        k_out, rest = rest[:n_out], rest[n_out:]
        co, rest = rest[:c_out], rest[c_out:]
        k_scr, (ssem, rsem) = rest[:n_scr], rest[n_scr:]
        pids = [pl.program_id(d) for d in range(len(grid))]
        first = functools.reduce(jnp.logical_and, [p == 0 for p in pids])
        last = functools.reduce(jnp.logical_and, [p == g - 1 for p, g in zip(pids, grid)])

        @pl.when(first)
        def _():
            carry.start(ci, co, ssem, rsem)

        body(*k_in, *k_out, *k_scr)

        @pl.when(last)
        def _():
            carry.finish(ci, co, ssem, rsem)

    outs = pl.pallas_call(
        wrapped, name=name, grid=grid, in_specs=in_specs + [ANY] * c_in, out_specs=out_specs + [ANY] * c_out,
        out_shape=out_shape + carry.outs,
        input_output_aliases={**aliases, **{n_in + i: n_out + o for i, o in carry.aliases.items()}},
        scratch_shapes=scratch_shapes + [pltpu.SemaphoreType.DMA((carry.n_sem,))] * 2,
        compiler_params=_cp(("arbitrary",) * len(grid)),
    )(*args, *carry.ins)
    return list(outs)


def _mm(a, b, *, name, b_mode="nn", out_shards=0, tm=1024, tn=256, tk=None, out_dtype=F32, carry=None):
    assert b_mode in ("nn", "nn_sh", "nt_shk"), b_mode
    M, K = a.shape
    if b_mode == "nn":
        N = b.shape[1]
    elif b_mode == "nn_sh":
        N = b.shape[0] * b.shape[2]
    else:
        N = b.shape[1]
    tm = _row_tile(M, tm)
    if b_mode == "nn_sh":
        tn = _row_tile(b.shape[2], tn)
    elif out_shards:
        tn = _row_tile(N // out_shards, tn)
    else:
        tn = _row_tile(N, tn)
    if tk is None:
        tk = K if b_mode != "nt_shk" else b.shape[2]
    if b_mode == "nt_shk":
        tk = _row_tile(b.shape[2], tk)
    nm, nn, nk = M // tm, N // tn, K // tk

    a_spec = pl.BlockSpec((tm, tk), lambda m, n, k: (m, k))
    if b_mode == "nn":
        b_spec = pl.BlockSpec((tk, tn), lambda m, n, k: (k, n))
    elif b_mode == "nn_sh":
        nps = b.shape[2] // tn
        b_spec = pl.BlockSpec((None, tk, tn), lambda m, n, k: (n // nps, k, n % nps))
    else:
        kps = b.shape[2] // tk
        b_spec = pl.BlockSpec((None, tn, tk), lambda m, n, k: (k // kps, n, k % kps))
    if out_shards:
        ops = (N // out_shards) // tn
        o_spec = pl.BlockSpec((None, tm, tn), lambda m, n, k: (n // ops, m, n % ops))
        o_shape = jax.ShapeDtypeStruct((out_shards, M, N // out_shards), out_dtype)
    else:
        o_spec = pl.BlockSpec((tm, tn), lambda m, n, k: (m, n))
        o_shape = jax.ShapeDtypeStruct((M, N), out_dtype)
    dn = (((1,), (1,)), ((), ())) if b_mode == "nt_shk" else (((1,), (0,)), ((), ()))

    def body(a_ref, b_ref, o_ref, acc_ref):
        k = pl.program_id(2)

        @pl.when(k == 0)
        def _():
            acc_ref[...] = jnp.zeros(acc_ref.shape, F32)

        acc_ref[...] += lax.dot_general(a_ref[...].astype(BF16), b_ref[...].astype(BF16), dn,
                                        preferred_element_type=F32)

        @pl.when(k == nk - 1)
        def _():
            o_ref[...] = acc_ref[...].astype(o_ref.dtype)

    outs = _call(body, name=name, grid=(nm, nn, nk), in_specs=[a_spec, b_spec], out_specs=[o_spec],
                 out_shape=[o_shape], scratch_shapes=[pltpu.VMEM((tm, tn), F32)],
                 sem=("parallel", "parallel", "arbitrary"), args=(a, b), carry=carry)
    return outs[0] if carry is None else (outs[0], outs[1:])


def _rowvec(n=D):
    return pl.BlockSpec((1, n), lambda i: (0, 0))


def _prenorm_fwd(x, g, scale, shift, name, carry=None):
    S = x.shape[0]
    tr = _row_tile(S, 256)

    def body(x_ref, g_ref, sc_ref, sh_ref, h_ref, ht_ref):
        xv = x_ref[...]
        r = lax.rsqrt(jnp.mean(xv * xv, axis=-1, keepdims=True) + EPS)
        hv = (xv * r) * g_ref[...] * (1.0 + sc_ref[...]) + sh_ref[...]
        h_ref[...] = hv.astype(BF16)
        ht_ref[...] = hv.T.astype(BF16)

    outs = _call(
        body, name=name, grid=(S // tr,),
        in_specs=[pl.BlockSpec((tr, D), lambda i: (i, 0)), _rowvec(), _rowvec(), _rowvec()],
        out_specs=[pl.BlockSpec((tr, D), lambda i: (i, 0)), pl.BlockSpec((D, tr), lambda i: (0, i))],
        out_shape=[jax.ShapeDtypeStruct((S, D), BF16), jax.ShapeDtypeStruct((D, S), BF16)],
        sem=("parallel",), args=(x, g, scale, shift), carry=carry)
    return outs[:2], outs[2:]


def _prenorm_bwd(dh, dxn, x, g, scale, name, carry=None):
    S = x.shape[0]
    tr = _row_tile(S, 256)

    def body(dh_ref, dxn_ref, x_ref, g_ref, sc_ref, dx_ref, dsh_ref, dsc_ref, dg_ref):
        i = pl.program_id(0)

        @pl.when(i == 0)
        def _():
            dsh_ref[...] = jnp.zeros((1, D), F32)
            dsc_ref[...] = jnp.zeros((1, D), F32)
            dg_ref[...] = jnp.zeros((1, D), F32)

        xv = x_ref[...]
        dhv = dh_ref[...]
        gv = g_ref[...]
        mod = 1.0 + sc_ref[...]
        r = lax.rsqrt(jnp.mean(xv * xv, axis=-1, keepdims=True) + EPS)
        xh = xv * r
        dsh_ref[...] += jnp.sum(dhv, axis=0, keepdims=True)
        dsc_ref[...] += jnp.sum(dhv * (xh * gv), axis=0, keepdims=True)
        dg_ref[...] += jnp.sum(dhv * mod * xh, axis=0, keepdims=True)
        u = dhv * mod * gv
        dx_ref[...] = dxn_ref[...] + r * u - xv * (r * r * r) * jnp.mean(u * xv, axis=-1, keepdims=True)

    tile = pl.BlockSpec((tr, D), lambda i: (i, 0))
    outs = _call(
        body, name=name, grid=(S // tr,),
        in_specs=[tile, tile, tile, _rowvec(), _rowvec()],
        out_specs=[tile, _rowvec(), _rowvec(), _rowvec()],
        out_shape=[jax.ShapeDtypeStruct((S, D), F32)] + [jax.ShapeDtypeStruct((1, D), F32)] * 3,
        sem=("arbitrary",), args=(dh, dxn, x, g, scale), carry=carry)
    return outs[:4], outs[4:]


def _layer_tail_fwd(proj, a_in, b_in, x, w_po, w_ho, w_out, gate, g, name, target=None, carry=None):
    S = proj.shape[0]
    tr = _row_tile(S, 256)
    nsh, _, wsh = w_po.shape
    n_in = 10 + (target is not None)

    def body(*refs):
        (mgp_ref, mgh_ref, a_ref, b_ref, x_ref, wpo_ref, who_ref, wout_ref, gate_ref, g_ref) = refs[:10]
        bra_ref, brb_ref, mt_ref, y_ref, xn_ref = refs[n_in:n_in + 5]
        av = a_ref[...]
        bra = jnp.concatenate([jnp.dot(av, wpo_ref[j], preferred_element_type=F32) for j in range(nsh)], axis=1)
        brb = jnp.dot(b_ref[...], who_ref[...], preferred_element_type=F32)
        mv = _sig(mgp_ref[...].astype(F32)) * bra + _sig(mgh_ref[...].astype(F32)) * brb
        bra_ref[...] = bra.astype(BF16)
        brb_ref[...] = brb.astype(BF16)
        mt_ref[...] = mv.T.astype(BF16)
        yv = jnp.dot(mv.astype(BF16), wout_ref[...], preferred_element_type=F32)
        y_ref[...] = yv
        r = lax.rsqrt(jnp.mean(yv * yv, axis=-1, keepdims=True) + EPS)
        xn = x_ref[...] + gate_ref[...] * ((yv * r) * g_ref[...])
        if target is None:
            xn_ref[...] = xn
        else:
            t_ref, l_ref = refs[10], refs[n_in + 5]

            @pl.when(pl.program_id(0) == 0)
            def _():
                l_ref[...] = jnp.zeros((8, 128), F32)

            err = xn - t_ref[...]
            xn_ref[...] = err * (1.0 / D)
            l_ref[...] += 0.5 * jnp.sum(jnp.mean(err * err, axis=-1, keepdims=True))

    tile = pl.BlockSpec((tr, D), lambda i: (i, 0))
    whole = lambda t: pl.BlockSpec(t.shape, lambda i: (0,) * t.ndim)
    last = target is not None
    outs = _call(
        body, name=name, grid=(S // tr,),
        in_specs=[pl.BlockSpec((tr, D), lambda i: (i, MGP_BLK)), pl.BlockSpec((tr, D), lambda i: (i, MGH_BLK)),
                  pl.BlockSpec((tr, POOL_W), lambda i: (i, 0)), tile, tile, whole(w_po), whole(w_ho),
                  whole(w_out), _rowvec(), _rowvec()] + [tile] * last,
        out_specs=[tile, tile, pl.BlockSpec((D, tr), lambda i: (0, i)), tile, tile]
        + [pl.BlockSpec((8, 128), lambda i: (0, 0))] * last,
        out_shape=[jax.ShapeDtypeStruct((S, D), BF16), jax.ShapeDtypeStruct((S, D), BF16),
                   jax.ShapeDtypeStruct((D, S), BF16), jax.ShapeDtypeStruct((S, D), F32),
                   jax.ShapeDtypeStruct((S, D), F32)] + [jax.ShapeDtypeStruct((8, 128), F32)] * last,
        sem=("arbitrary",) if last else ("parallel",),
        args=(proj, proj, a_in, b_in, x, w_po, w_ho, w_out, gate, g) + ((target,) if last else ()), carry=carry)
    return outs[:5 + last], outs[5 + last:]


def _layer_head_bwd(dxn, y, proj, br_a, br_b, w_po, w_ho, w_out, gate, g, name, carry=None):
    S = y.shape[0]
    tr = _row_tile(S, 256)
    nsh, _, wsh = w_po.shape

    def body(dxn_ref, y_ref, mgp_ref, mgh_ref, bra_ref, brb_ref, wpo_ref, who_ref, wout_ref, gate_ref, g_ref,
             dy_ref, dba_ref, dbb_ref, dproj_ref, dain_ref, dbin_ref, dgate_ref, dg_ref, dmgh_s):
        i = pl.program_id(0)
        j = pl.program_id(1)

        @pl.when((i == 0) & (j == 0))
        def _():
            dgate_ref[...] = jnp.zeros((1, D), F32)
            dg_ref[...] = jnp.zeros((1, D), F32)

        @pl.when(j == 1)
        def _():
            dproj_ref[...] = dmgh_s[...]

        @pl.when(j == 0)
        def _():
            everything(dxn_ref, y_ref, mgp_ref, mgh_ref, bra_ref, brb_ref, wpo_ref, who_ref, wout_ref, gate_ref,
                       g_ref, dy_ref, dba_ref, dbb_ref, dproj_ref, dain_ref, dbin_ref, dgate_ref, dg_ref, dmgh_s)

    def everything(dxn_ref, y_ref, mgp_ref, mgh_ref, bra_ref, brb_ref, wpo_ref, who_ref, wout_ref, gate_ref, g_ref,
                   dy_ref, dba_ref, dbb_ref, dproj_ref, dain_ref, dbin_ref, dgate_ref, dg_ref, dmgh_s):
        yv = y_ref[...]
        dv = dxn_ref[...]
        gv = g_ref[...]
        gt = gate_ref[...]
        r = lax.rsqrt(jnp.mean(yv * yv, axis=-1, keepdims=True) + EPS)
        yh = yv * r
        dgate_ref[...] += jnp.sum(dv * (yh * gv), axis=0, keepdims=True)
        dg_ref[...] += jnp.sum(dv * gt * yh, axis=0, keepdims=True)
        u = dv * gt * gv
        dy = (r * u - yv * (r * r * r) * jnp.mean(u * yv, axis=-1, keepdims=True)).astype(BF16)
        dy_ref[...] = dy
        dm = _dot_nt(dy, wout_ref[...])
        sp = _sig(mgp_ref[...].astype(F32))
        sh = _sig(mgh_ref[...].astype(F32))
        dba = (dm * sp).astype(BF16)
        dbb = (dm * sh).astype(BF16)
        dba_ref[...] = dba
        dbb_ref[...] = dbb
        dproj_ref[...] = (dm * bra_ref[...].astype(F32) * sp * (1.0 - sp)).astype(BF16)
        dmgh_s[...] = (dm * brb_ref[...].astype(F32) * sh * (1.0 - sh)).astype(BF16)
        dain = _dot_nt(dba[:, 0:wsh], wpo_ref[0])
        for k in range(1, nsh):
            dain = dain + _dot_nt(dba[:, k * wsh:(k + 1) * wsh], wpo_ref[k])
        dain_ref[...] = dain
        dbin_ref[...] = _dot_nt(dbb, who_ref[...])

    tile = pl.BlockSpec((tr, D), lambda i, j: (i, 0))
    whole = lambda t: pl.BlockSpec(t.shape, lambda i, j: (0,) * t.ndim)
    vec = pl.BlockSpec((1, D), lambda i, j: (0, 0))
    ahead = lambda i, j: jnp.minimum(i + j, S // tr - 1)
    tile_in = pl.BlockSpec((tr, D), lambda i, j: (ahead(i, j), 0))
    outs = _call(
        body, name=name, grid=(S // tr, 2),
        in_specs=[tile_in, tile_in, pl.BlockSpec((tr, D), lambda i, j: (ahead(i, j), MGP_BLK)),
                  pl.BlockSpec((tr, D), lambda i, j: (ahead(i, j), MGH_BLK)), tile_in, tile_in, whole(w_po),
                  whole(w_ho), whole(w_out), vec, vec],
        out_specs=[tile, tile, tile, pl.BlockSpec((tr, D), lambda i, j: (i, MGP_BLK + j)),
                   pl.BlockSpec((tr, POOL_W), lambda i, j: (i, 0)), tile, vec, vec],
        out_shape=[jax.ShapeDtypeStruct((S, D), BF16)] * 3
        + [jax.ShapeDtypeStruct((S, IN_W), BF16), jax.ShapeDtypeStruct((S, POOL_W), F32),
           jax.ShapeDtypeStruct((S, D), F32), jax.ShapeDtypeStruct((1, D), F32), jax.ShapeDtypeStruct((1, D), F32)],
        scratch_shapes=[pltpu.VMEM((tr, D), BF16)], sem=("arbitrary", "arbitrary"),
        args=(dxn, y, proj, proj, br_a, br_b, w_po, w_ho, w_out, gate, g), carry=carry)
    return outs[:8], outs[8:]


def _pool_pieces(u, g, S):
    rowi = lax.broadcasted_iota(jnp.int32, (S, 1), 0)

    def down(z, k):
        return jnp.where(rowi >= k, pltpu.roll(z, k, axis=0), 0.0)

    s2 = u + down(u, 1)
    s4 = s2 + down(s2, 2)
    s8 = s4 + down(s4, 4)
    s16 = s8 + down(s8, 8)
    win = jnp.where(g == 0, s2, jnp.where(g == 1, s4, jnp.where(g == 2, s8, s16)))
    w = jnp.where(g == 0, 2, jnp.where(g == 1, 4, jnp.where(g == 2, 8, 16)))
    count = jnp.minimum(rowi + 1, w).astype(F32)
    return win / count - u, count, rowi


def _pool_fwd(proj, pw, pscale, name):
    S = proj.shape[0]

    def body(pv_ref, pg_ref, pw_ref, sc_ref, a_ref, at_ref):
        g = pl.program_id(0)
        pooled, _, _ = _pool_pieces(pv_ref[...].astype(F32), g, S)
        pm = jnp.dot(pooled.astype(BF16), pw_ref[...].astype(BF16), preferred_element_type=F32)
        pgv = pg_ref[...].astype(F32)
        av = pm * sc_ref[...] * (pgv * _sig(pgv))
        a_ref[...] = av.astype(BF16)
        at_ref[...] = av.T.astype(BF16)

    outs = _call(
        body, name=name, grid=(GROUPS,),
        in_specs=[pl.BlockSpec((S, 128), lambda g: (0, PV0 + g)), pl.BlockSpec((S, 128), lambda g: (0, PG0 + g)),
                  pl.BlockSpec((None, 128, 128), lambda g: (g, 0, 0)), pl.BlockSpec((1, 128), lambda g: (0, g))],
        out_specs=[pl.BlockSpec((S, 128), lambda g: (0, g)), pl.BlockSpec((128, S), lambda g: (g, 0))],
        out_shape=[jax.ShapeDtypeStruct((S, POOL_W), BF16), jax.ShapeDtypeStruct((POOL_W, S), BF16)],
        sem=("parallel",), args=(proj, proj, pw, pscale))
    return outs


def _pool_bwd(da, proj, pw, pscale, dproj, name):
    S = proj.shape[0]

    def body(da_ref, pv_ref, pg_ref, pw_ref, sc_ref, dproj_in, dproj_ref, dpw_ref, dsc_ref, dpg_s):
        @pl.when(pl.program_id(1) == 1)
        def _():
            dproj_ref[...] = dpg_s[...]

        @pl.when(pl.program_id(1) == 0)
        def _():
            group(da_ref, pv_ref, pg_ref, pw_ref, sc_ref, dproj_ref, dpg_s, dpw_ref, dsc_ref)

    def group(da_ref, pv_ref, pg_ref, pw_ref, sc_ref, dpv_ref, dpg_ref, dpw_ref, dsc_ref):
        g = pl.program_id(0)
        pooled, count, rowi = _pool_pieces(pv_ref[...].astype(F32), g, S)
        pwb = pw_ref[...].astype(BF16)
        pm = jnp.dot(pooled.astype(BF16), pwb, preferred_element_type=F32)
        scv = sc_ref[...]
        pgv = pg_ref[...].astype(F32)
        sg = _sig(pgv)
        dav = da_ref[...]
        d_ps = dav * (pgv * sg)
        dpg_ref[...] = (dav * (pm * scv) * _dsilu(pgv, sg)).astype(BF16)
        dsc_ref[...] = jnp.sum(d_ps * pm, axis=0, keepdims=True)
        d_pm = (d_ps * scv).astype(BF16)
        dpw_ref[...] = lax.dot_general(pooled.astype(BF16), d_pm, (((0,), (0,)), ((), ())),
                                       preferred_element_type=F32)
        d_pooled = lax.dot_general(d_pm, pwb, (((1,), (1,)), ((), ())), preferred_element_type=F32)
        z = d_pooled / count

        def up(v, k):
            return jnp.where(rowi < S - k, pltpu.roll(v, S - k, axis=0), 0.0)

        t2 = z + up(z, 1)
        t4 = t2 + up(t2, 2)
        t8 = t4 + up(t4, 4)
        t16 = t8 + up(t8, 8)
        adj = jnp.where(g == 0, t2, jnp.where(g == 1, t4, jnp.where(g == 2, t8, t16)))
        dpv_ref[...] = (adj - d_pooled).astype(BF16)

    col = lambda g, j: (0, g)
    ahead = lambda g, j: jnp.minimum(g + j, GROUPS - 1)
    return pl.pallas_call(
        body, name=name, grid=(GROUPS, 2),
        in_specs=[pl.BlockSpec((S, 128), lambda g, j: (0, ahead(g, j))),
                  pl.BlockSpec((S, 128), lambda g, j: (0, PV0 + ahead(g, j))),
                  pl.BlockSpec((S, 128), lambda g, j: (0, PG0 + ahead(g, j))),
                  pl.BlockSpec((None, 128, 128), lambda g, j: (ahead(g, j), 0, 0)),
                  pl.BlockSpec((1, 128), lambda g, j: (0, ahead(g, j))), ANY],
        out_specs=[pl.BlockSpec((S, 128), lambda g, j: (0, PV0 + g + (PG0 - PV0) * j)),
                   pl.BlockSpec((None, 128, 128), lambda g, j: (g, 0, 0)), pl.BlockSpec((1, 128), col)],
        out_shape=[jax.ShapeDtypeStruct(dproj.shape, dproj.dtype),
                   jax.ShapeDtypeStruct((GROUPS, 128, 128), F32), jax.ShapeDtypeStruct((1, POOL_W), F32)],
        scratch_shapes=[pltpu.VMEM((S, 128), BF16)], input_output_aliases={5: 0},
        compiler_params=_cp(("arbitrary", "arbitrary")),
    )(da, proj, proj, pw, pscale, dproj)


SCAN_SHIFTS = tuple(1 << b for b in range(CH.bit_length() - 1))


def _chunk_cumsum(z, rowi):
    for sh in SCAN_SHIFTS:
        z = z + jnp.where(rowi >= sh, pltpu.roll(z, sh, axis=0), 0.0)
    return z


def _chunk_rev_cumsum(z, rowi):
    for sh in SCAN_SHIFTS:
        z = z + jnp.where(rowi < CH - sh, pltpu.roll(z, CH - sh, axis=0), 0.0)
    return z


def _dot_nn(a, b):
    return jnp.dot(a.astype(BF16), b.astype(BF16), preferred_element_type=F32)


def _dot_nt(a, b):
    return lax.dot_general(a.astype(BF16), b.astype(BF16), (((1,), (1,)), ((), ())), preferred_element_type=F32)


def _dot_tn(a, b):
    return lax.dot_general(a.astype(BF16), b.astype(BF16), (((0,), (0,)), ((), ())), preferred_element_type=F32)


def _gates(hq, hf, lbv):
    hq, hf = hq.astype(F32), hf.astype(F32)
    sq = _sig(hq)
    sf = _sig(hf)
    f = lbv + (1.0 - lbv) * sf
    fc = jnp.maximum(f, 1e-30)
    return hq * sq, sq, sf, f, fc, jnp.log(fc)


DECAY_CAP = 60.0


def _block_ref(c_ref, i, sb):
    if i == 0:
        return jnp.zeros((1, HD), F32)
    return c_ref[sb * i - 1:sb * i, :]


def _block_decay(c_ref, sb):
    spans = [_block_ref(c_ref, i, sb) - c_ref[sb * (i + 1) - 1:sb * (i + 1), :] for i in range(CH // sb)]
    return functools.reduce(jnp.maximum, spans)


def _pair_factors(q_ref, k, c_ref, first, cap, round_bf16, sb):
    nb = CH // sb
    c = c_ref[...]
    zero = jnp.zeros((sb, HD), F32)
    q_groups, k_groups, eqs, eks = [], [], [], []
    for i in range(first, nb):
        blk = slice(sb * i, sb * (i + 1))
        r_i = _block_ref(c_ref, i, sb)
        eq = jnp.exp(jnp.minimum(c_ref[blk, :] - r_i, 0.0))
        ek = jnp.exp(jnp.minimum(r_i - c, cap))
        qi, kei = q_ref[blk, :] * eq, k * ek
        if round_bf16:
            qi, kei = qi.astype(BF16).astype(F32), kei.astype(BF16).astype(F32)
        q_groups.append(jnp.concatenate([zero] * i + [qi] + [zero] * (nb - 1 - i), axis=0))
        k_groups.append(kei)
        eqs.append(eq)
        eks.append(ek)
    return jnp.concatenate(q_groups, axis=1), jnp.concatenate(k_groups, axis=1), eqs, eks


def _pair_mask(rowi, coli, strict, sb):
    return (coli < jnp.bitwise_and(rowi, -sb)) if strict else (coli <= rowi)


def _hgrn_fwd(proj, lb, gn, name, carry=None):
    S = proj.shape[0]
    nch = S // CH
    W = NH * HD

    def body(hq_ref, hf_ref, hi_ref, hg_ref, lb_ref, gn_ref, bin_ref, bint_ref, oraw_ref, st_ref, mild_ref,
             cum_ref, q_s, k_s, c_s, v_s, o_s, state_s, qf_s, kf_s, cf_s):
        state_s[...] = jnp.zeros((NH, HD, HD), F32)
        rowi = lax.broadcasted_iota(jnp.int32, (CH, 1), 0)
        coli = lax.broadcasted_iota(jnp.int32, (1, CH), 1)
        sbi = lax.broadcasted_iota(jnp.int32, (SB, 1), 0)
        gnv = gn_ref[...]

        def gates_pass(n, worst):
            wide, narrow = worst
            rows = pl.ds(pl.multiple_of(n * CH, CH), CH)
            for hh in range(NH):
                lanes = slice(hh * HD, (hh + 1) * HD)
                q, _, _, f, _, logf = _gates(hq_ref[rows, lanes], hf_ref[rows, lanes], lb_ref[:, lanes])
                c = _chunk_cumsum(logf, rowi)
                qf_s[hh, rows, :] = q
                kf_s[hh, rows, :] = 1.0 - f
                cf_s[hh, rows, :] = c
                cum_ref[rows, lanes] = c
                c_s[hh] = c
                wide = jnp.maximum(wide, _block_decay(c_s.at[hh], SB_WIDE))
                narrow = jnp.maximum(narrow, _block_decay(c_s.at[hh], SB))
            return wide, narrow

        def between_chunks(hh, n, rows):
            lanes = slice(hh * HD, (hh + 1) * HD)
            q = qf_s[hh, rows, :]
            k = kf_s[hh, rows, :]
            c = cf_s[hh, rows, :]
            v = hi_ref[rows, lanes].astype(F32)
            q_s[hh] = q
            k_s[hh] = k
            c_s[hh] = c
            v_s[hh] = v
            st = state_s[hh]
            st_ref[hh, n] = st.astype(BF16)
            o_s[hh] = _dot_nt(q * jnp.exp(c), st)
            last = c_s[hh, CH - 1:CH, :]
            state_s[hh] = st * jnp.exp(last) + _dot_tn(v, k * jnp.exp(last - c))

        def pairs_matmul(hh, first, cap, strict, sb):
            qx, kc, _, _ = _pair_factors(q_s.at[hh], k_s[hh], c_s.at[hh], first, cap, False, sb)
            a = jnp.where(_pair_mask(rowi, coli, strict, sb), _dot_nt(qx, kc), 0.0)
            o_s[hh] += _dot_nn(a, v_s[hh])

        def within_chunk_matmul(sb):
            return lambda hh: pairs_matmul(hh, 0, DECAY_CAP, False, sb)

        def within_chunk_exact(hh):
            pairs_matmul(hh, 1, 0.0, True, SB)
            for i in range(CH // SB):
                blk = slice(SB * i, SB * (i + 1))
                qb = q_s[hh, blk, :]
                cb = c_s[hh, blk, :]
                acc = jnp.zeros((SB, HD), F32)
                for s in range(SB):
                    row = SB * i + s
                    w = jnp.exp(jnp.minimum(cb - c_s[hh, row:row + 1, :], 0.0))
                    a_col = jnp.sum(qb * k_s[hh, row:row + 1, :] * w, axis=-1, keepdims=True)
                    acc = acc + jnp.where(sbi >= s, a_col, 0.0) * v_s[hh, row:row + 1, :]
                o_s[hh, blk, :] += acc

        def norm_and_gate(hh, rows):
            lanes = slice(hh * HD, (hh + 1) * HD)
            ov = o_s[hh]
            oraw_ref[rows, lanes] = ov
            r = lax.rsqrt(jnp.mean(ov * ov, axis=-1, keepdims=True) + EPS)
            hg = hg_ref[rows, lanes].astype(F32)
            bin_ref[rows, lanes] = ((ov * r) * gnv * (hg * _sig(hg))).astype(BF16)

        def chunk_with(within_chunk):
            def chunk(n, carry):
                rows = pl.ds(pl.multiple_of(n * CH, CH), CH)
                for hh in range(NH):
                    between_chunks(hh, n, rows)
                for hh in range(NH):
                    within_chunk(hh)
                for hh in range(NH):
                    norm_and_gate(hh, rows)
                return carry
            return chunk

        none = jnp.zeros((1, HD), F32)
        wide, narrow = lax.fori_loop(0, nch, gates_pass, (none, none))
        tier = jnp.where(jnp.max(wide) <= DECAY_CAP, 2.0, jnp.where(jnp.max(narrow) <= DECAY_CAP, 1.0, 0.0))
        mild_ref[...] = jnp.broadcast_to(tier, (8, HD))

        @pl.when(tier == 2.0)
        def _():
            lax.fori_loop(0, nch, chunk_with(within_chunk_matmul(SB_WIDE)), 0, unroll=4)

        @pl.when(tier == 1.0)
        def _():
            lax.fori_loop(0, nch, chunk_with(within_chunk_matmul(SB)), 0, unroll=2)

        @pl.when(tier == 0.0)
        def _():
            lax.fori_loop(0, nch, chunk_with(within_chunk_exact), 0)

        bint_ref[...] = bin_ref[...].astype(F32).T.astype(BF16)

    col = lambda off: pl.BlockSpec((S, W), lambda h: (0, off // NH + h))
    head = pl.BlockSpec((S, W), lambda h: (0, h))
    outs = _call(
        body, name=name, grid=(HEADS // NH,),
        in_specs=[col(HQ0), col(HF0), col(HI0), col(HG0), pl.BlockSpec((1, W), lambda h: (0, h)),
                  pl.BlockSpec((1, HD), lambda h: (0, 0))],
        out_specs=[head, pl.BlockSpec((W, S), lambda h: (h, 0)), head,
                   pl.BlockSpec((NH, nch, HD, HD), lambda h: (h, 0, 0, 0)),
                   pl.BlockSpec((8, HD), lambda h: (h, 0)), head],
        out_shape=[jax.ShapeDtypeStruct((S, D), BF16), jax.ShapeDtypeStruct((D, S), BF16),
                   jax.ShapeDtypeStruct((S, D), F32), jax.ShapeDtypeStruct((HEADS, nch, HD, HD), BF16),
                   jax.ShapeDtypeStruct((8 * HEADS // NH, HD), F32), jax.ShapeDtypeStruct((S, D), F32)],
        scratch_shapes=[pltpu.VMEM((NH, CH, HD), F32)] * 5 + [pltpu.VMEM((NH, HD, HD), F32)]
        + [pltpu.VMEM((NH, S, HD), F32)] * 3,
        sem=("parallel",), args=(proj, proj, proj, proj, lb, gn), carry=carry)
    return outs[:6], outs[6:]


def _hgrn_bwd(dbin, proj, oraw, states, mild, cum, lb, gn, dproj, name, carry=None):
    S = proj.shape[0]
    nch = S // CH
    W = NH * HD
    n_in = 12

    def body(*refs):
        ins, (dproj_ref, dlb_ref, dgn_ref) = refs[:n_in - 1], refs[n_in:n_in + 3]
        scratch, later = refs[n_in + 3:-3], refs[-3:]
        seg = pl.program_id(1)

        @pl.when(seg == 0)
        def _():
            heads(*ins, dproj_ref, *later, dlb_ref, dgn_ref, *scratch)

        for s, kept in enumerate(later):
            @pl.when(seg == s + 1)
            def _(kept=kept):
                dproj_ref[...] = kept[...]

    def heads(db_ref, hq_ref, hf_ref, hi_ref, hg_ref, or_ref, st_ref, mild_ref, cum_ref, lb_ref, gn_ref,
              dq_ref, df_ref, di_ref, dg_ref, dlb_ref, dgn_ref,
              q_s, k_s, c_s, v_s, do_s, dq_s, dk_s, dv_s, dc_s, dqd_s, dkd_s, f_s, sf_s, sq_s, dl_s, dst_s,
              dlb_s, dgn_s):
        dst_s[...] = jnp.zeros((NH, HD, HD), F32)
        dlb_s[...] = jnp.zeros((1, W), F32)
        dgn_s[...] = jnp.zeros((1, HD), F32)
        rowi = lax.broadcasted_iota(jnp.int32, (CH, 1), 0)
        coli = lax.broadcasted_iota(jnp.int32, (1, CH), 1)
        sbi = lax.broadcasted_iota(jnp.int32, (SB, 1), 0)
        gnv = gn_ref[...]
        def between_chunks(hh, n, rows):
            lanes = slice(hh * HD, (hh + 1) * HD)
            lbv = lb_ref[:, lanes]
            hq = hq_ref[rows, lanes].astype(F32)
            sq = _sig(hq)
            sf = _sig(hf_ref[rows, lanes].astype(F32))
            f = lbv + (1.0 - lbv) * sf
            q = hq * sq
            k = 1.0 - f
            f_s[hh] = f
            sf_s[hh] = sf
            sq_s[hh] = sq
            v = hi_ref[rows, lanes].astype(F32)
            c = cum_ref[rows, lanes]
            ov = or_ref[rows, lanes]
            hg = hg_ref[rows, lanes].astype(F32)
            sg = _sig(hg)
            r = lax.rsqrt(jnp.mean(ov * ov, axis=-1, keepdims=True) + EPS)
            dbv = db_ref[rows, lanes]
            d_on = dbv * (hg * sg)
            dg_ref[rows, lanes] = (dbv * ((ov * r) * gnv) * _dsilu(hg, sg)).astype(BF16)
            dgn_s[...] += jnp.sum(d_on * (ov * r), axis=0, keepdims=True)
            u = d_on * gnv
            do = r * u - ov * (r * r * r) * jnp.mean(u * ov, axis=-1, keepdims=True)
            q_s[hh] = q
            k_s[hh] = k
            c_s[hh] = c
            v_s[hh] = v
            do_s[hh] = do
            st = st_ref[hh, n].astype(F32)
            dst = dst_s[hh]
            ec = jnp.exp(c)
            last = c_s[hh, CH - 1:CH, :]
            el = jnp.exp(last - c)
            elast = jnp.exp(last)
            dq = _dot_nn(do, st) * ec
            dk = _dot_nn(v, dst) * el
            dq_s[hh] = dq
            dk_s[hh] = dk
            dv_s[hh] = _dot_nt(k * el, dst)
            dc_s[hh] = q * dq - k * dk
            dl_s[hh] = (jnp.sum(k * dk, axis=0, keepdims=True)
                        + elast * jnp.sum(st * dst, axis=0, keepdims=True))
            dst_s[hh] = dst * elast + _dot_tn(do, q * ec)

        def pairs_matmul(hh, first, cap, strict, sb):
            do = do_s[hh]
            qx, kc, eqs, eks = _pair_factors(q_s.at[hh], k_s[hh], c_s.at[hh], first, cap, True, sb)
            mask = _pair_mask(rowi, coli, strict, sb)
            a = jnp.where(mask, _dot_nt(qx, kc), 0.0)
            d_a = jnp.where(mask, _dot_nt(do, v_s[hh]).astype(BF16).astype(F32), 0.0)
            dqx = _dot_nn(d_a, kc)
            dkc = _dot_tn(d_a, qx)
            dv_s[hh] += _dot_tn(a, do)
            dk, dcum = dk_s[hh], dc_s[hh]
            dq_slabs = [jnp.zeros((sb, HD), F32)] * first
            dc_slabs = [jnp.zeros((sb, HD), F32)] * first
            for g, (eq, ek) in enumerate(zip(eqs, eks)):
                rows = slice(sb * (first + g), sb * (first + g + 1))
                cols = slice(HD * g, HD * (g + 1))
                dq_i = dqx[rows, cols]
                dk_i = dkc[:, cols]
                dq_slabs.append(dq_i * eq)
                dc_slabs.append(qx[rows, cols] * dq_i)
                dk = dk + dk_i * ek
                dcum = dcum - kc[:, cols] * dk_i
            dq_s[hh] += jnp.concatenate(dq_slabs, axis=0)
            dk_s[hh] = dk
            dc_s[hh] = dcum + jnp.concatenate(dc_slabs, axis=0)

        def pairs_exact(hh):
            dqd_s[hh] = jnp.zeros((CH, HD), F32)
            dkd_s[hh] = jnp.zeros((CH, HD), F32)
            for i in range(CH // SB):
                blk = slice(SB * i, SB * (i + 1))
                qb = q_s[hh, blk, :]
                cb = c_s[hh, blk, :]
                dob = do_s[hh, blk, :]
                dq_acc = jnp.zeros((SB, HD), F32)
                for s in range(SB):
                    row = SB * i + s
                    ks = k_s[hh, row:row + 1, :]
                    vs = v_s[hh, row:row + 1, :]
                    w = jnp.exp(jnp.minimum(cb - c_s[hh, row:row + 1, :], 0.0))
                    live = sbi >= s
                    a_col = jnp.where(live, jnp.sum(qb * ks * w, axis=-1, keepdims=True), 0.0)
                    da_col = jnp.where(live, jnp.sum(dob * vs, axis=-1, keepdims=True), 0.0)
                    dq_acc = dq_acc + da_col * ks * w
                    dkd_s[hh, row:row + 1, :] += jnp.sum(da_col * qb * w, axis=0, keepdims=True)
                    dv_s[hh, row:row + 1, :] += jnp.sum(a_col * dob, axis=0, keepdims=True)
                dqd_s[hh, blk, :] += dq_acc
            dq_d = dqd_s[hh]
            dk_d = dkd_s[hh]
            dq_s[hh] += dq_d
            dk_s[hh] += dk_d
            dc_s[hh] += q_s[hh] * dq_d - k_s[hh] * dk_d

        def gate_grads(hh, rows):
            lanes = slice(hh * HD, (hh + 1) * HD)
            lbv = lb_ref[:, lanes]
            hq = hq_ref[rows, lanes].astype(F32)
            f, sf, sq = f_s[hh], sf_s[hh], sq_s[hh]
            dlogf = _chunk_rev_cumsum(dc_s[hh], rowi) + dl_s[hh]
            dfv = jnp.where(f > 1e-30, dlogf / jnp.maximum(f, 1e-30), 0.0) - dk_s[hh]
            dlb_s[:, lanes] += jnp.sum(dfv * (1.0 - sf), axis=0, keepdims=True)
            df_ref[rows, lanes] = (dfv * (1.0 - lbv) * sf * (1.0 - sf)).astype(BF16)
            dq_ref[rows, lanes] = (dq_s[hh] * _dsilu(hq, sq)).astype(BF16)
            di_ref[rows, lanes] = dv_s[hh].astype(BF16)

        def chunk_with(pairs):
            def chunk(j, carry):
                n = nch - 1 - j
                rows = pl.ds(pl.multiple_of(n * CH, CH), CH)
                for hh in range(NH):
                    between_chunks(hh, n, rows)
                for hh in range(NH):
                    pairs(hh)
                for hh in range(NH):
                    gate_grads(hh, rows)
                return carry
            return chunk

        def pairs_mild(sb):
            return lambda hh: pairs_matmul(hh, 0, DECAY_CAP, False, sb)

        def pairs_any(hh):
            pairs_matmul(hh, 1, 0.0, True, SB)
            pairs_exact(hh)

        tier = jnp.max(mild_ref[...])

        @pl.when(tier == 2.0)
        def _():
            lax.fori_loop(0, nch, chunk_with(pairs_mild(SB_WIDE)), 0, unroll=2)

        @pl.when(tier == 1.0)
        def _():
            lax.fori_loop(0, nch, chunk_with(pairs_mild(SB)), 0)

        @pl.when(tier == 0.0)
        def _():
            lax.fori_loop(0, nch, chunk_with(pairs_any), 0)

        dlb_ref[...] = dlb_s[...]
        dgn_ref[...] = jnp.broadcast_to(dgn_s[...], (8, HD))

    ahead = lambda h, s: jnp.minimum(h + jnp.minimum(s, 1), HEADS // NH - 1)
    col = lambda off: pl.BlockSpec((S, W), lambda h, s: (0, off // NH + ahead(h, s)))
    head_in = pl.BlockSpec((S, W), lambda h, s: (0, ahead(h, s)))
    vec_in = pl.BlockSpec((1, W), lambda h, s: (0, ahead(h, s)))
    vec = pl.BlockSpec((1, W), lambda h, s: (0, h))
    seg_w = (HF0 - HQ0) // NH
    outs = _call(
        body, name=name, grid=(HEADS // NH, 4),
        in_specs=[head_in, col(HQ0), col(HF0), col(HI0), col(HG0), head_in,
                  pl.BlockSpec((NH, nch, HD, HD), lambda h, s: (ahead(h, s), 0, 0, 0)),
                  pl.BlockSpec((8, HD), lambda h, s: (ahead(h, s), 0)), head_in, vec_in,
                  pl.BlockSpec((1, HD), lambda h, s: (0, 0)), ANY],
        out_specs=[pl.BlockSpec((S, W), lambda h, s: (0, HQ0 // NH + seg_w * s + h)), vec,
                   pl.BlockSpec((8, HD), lambda h, s: (h, 0))],
        out_shape=[jax.ShapeDtypeStruct(dproj.shape, dproj.dtype), jax.ShapeDtypeStruct((1, D), F32),
                   jax.ShapeDtypeStruct((8 * HEADS // NH, HD), F32)],
        scratch_shapes=[pltpu.VMEM((NH, CH, HD), F32)] * 14
        + [pltpu.VMEM((NH, 1, HD), F32), pltpu.VMEM((NH, HD, HD), F32), pltpu.VMEM((1, W), F32),
           pltpu.VMEM((1, HD), F32)] + [pltpu.VMEM((S, W), BF16)] * 3,
        sem=("arbitrary", "arbitrary"), aliases={n_in - 1: 0},
        args=(dbin, proj, proj, proj, proj, oraw, states, mild, cum, lb, gn, dproj), carry=carry)
    dproj, dlb, dgn = outs[:3]
    return (dproj, dlb, dgn.reshape(HEADS // NH, 8, HD)[:, 0, :]), outs[3:]


def _lower_bounds(l0, l1):
    m = jnp.maximum(l0, l1)
    e0 = jnp.exp(l0 - m)
    e1 = jnp.exp(l1 - m)
    tot = e0 + e1
    p0 = e0 / tot
    p1 = e1 / tot
    return jnp.clip(p0 - p0, 0.0, 1.0), jnp.clip((p0 + p1) - p0, 0.0, 1.0)


def _lb_fwd(logits):
    def body(l_ref, o_ref):
        lb0, lb1 = _lower_bounds(l_ref[0:1, :], l_ref[1:2, :])
        o_ref[0:1, :] = lb0
        o_ref[1:2, :] = lb1

    return pl.pallas_call(body, name="lb_fwd", out_shape=jax.ShapeDtypeStruct((2, D), F32))(logits)


def _lb_bwd(logits, dlb):
    def body(l_ref, d_ref, o_ref):
        _, vjp = jax.vjp(_lower_bounds, l_ref[0:1, :], l_ref[1:2, :])
        g0, g1 = vjp((d_ref[0:1, :], d_ref[1:2, :]))
        o_ref[0:1, :] = g0
        o_ref[1:2, :] = g1

    return pl.pallas_call(body, name="lb_bwd", out_shape=jax.ShapeDtypeStruct((2, D), F32))(logits, dlb)


ADA_PAD = 128


def _ada_fwd(c_pad, w_ada, b_sh):
    ns = w_ada.shape[2]

    def body(c_ref, w_ref, b_ref, o_ref):
        cv = c_ref[...]
        ca = (cv * _sig(cv)).astype(BF16)
        for l in range(2):
            res = jnp.dot(ca, w_ref[l].astype(BF16), preferred_element_type=F32)
            o_ref[:, l * ns:(l + 1) * ns] = res[0:NDEV, :] + b_ref[l:l + 1, :]

    return pl.pallas_call(body, name="ada_fwd", out_shape=jax.ShapeDtypeStruct((NDEV, 2 * ns), F32),
                          compiler_params=_cp())(c_pad, w_ada, b_sh)


def _ada_wgrad(c_pad_t, d_ada_sh):
    ns = d_ada_sh.shape[2]

    def body(c_ref, d_ref, o_ref):
        cv = c_ref[...]
        ca = (cv * _sig(cv)).astype(BF16)
        for l in range(2):
            o_ref[l] = jnp.dot(ca, d_ref[l].astype(BF16), preferred_element_type=F32)

    return pl.pallas_call(body, name="ada_wgrad", out_shape=jax.ShapeDtypeStruct((2, D, ns), F32),
                          compiler_params=_cp())(c_pad_t, d_ada_sh)


def _sum_devices(g):
    _, R, C = g.shape

    def body(g_ref, o_ref):
        acc = g_ref[0]
        for d in range(1, NDEV):
            acc = acc + g_ref[d]
        o_ref[...] = acc

    return pl.pallas_call(body, name="sum_devices", out_shape=jax.ShapeDtypeStruct((R, C), F32),
                          compiler_params=_cp())(g)


def _adamw(w, g, m, v, name, echo=False):
    R, C = w.shape
    tr = _row_tile(R, max(8, (1 << 19) // C))

    def body(w_ref, g_ref, m_ref, v_ref, d_ref, nm_ref, nv_ref, *g_out):
        d_ref[...], nm_ref[...], nv_ref[...] = _adamw_update(w_ref[...], g_ref[...], m_ref[...], v_ref[...])
        for o_ref in g_out:
            o_ref[...] = g_ref[...]

    tile = pl.BlockSpec((tr, C), lambda i: (i, 0))
    n_out = 4 if echo else 3
    return _call(body, name=name, grid=(R // tr,), in_specs=[tile] * 4, out_specs=[tile] * n_out,
                 out_shape=[jax.ShapeDtypeStruct((R, C), F32)] * n_out, sem=("parallel",), args=(w, g, m, v))


def _adamw_many(wgmv, name, steps=4):
    n = len(wgmv)

    def body(*refs):
        ins, outs = refs[:4 * n], refs[4 * n:]
        for a in range(n):
            w_ref, g_ref, m_ref, v_ref = ins[4 * a:4 * a + 4]
            d_ref, nm_ref, nv_ref, g_out = outs[4 * a:4 * a + 4]
            d_ref[...], nm_ref[...], nv_ref[...] = _adamw_update(w_ref[...], g_ref[...], m_ref[...], v_ref[...])
            g_out[...] = g_ref[...]

    def tile(t):
        return pl.BlockSpec((t.shape[0] // steps, t.shape[1]), lambda i: (i, 0))

    flat = [t for four in wgmv for t in four]
    outs = pl.pallas_call(
        body, name=name, grid=(steps,), in_specs=[tile(t) for t in flat],
        out_specs=[tile(four[0]) for four in wgmv for _ in range(4)],
        out_shape=[jax.ShapeDtypeStruct(four[0].shape, F32) for four in wgmv for _ in range(4)],
        compiler_params=_cp(("parallel",)))(*flat)
    return [outs[4 * a:4 * a + 4] for a in range(n)]


def _adamw_update(w, g, m, v):
    nm = B1 * m + (1.0 - B1) * g
    nv = B2 * v + (1.0 - B2) * (g * g)
    m_hat = nm / (1.0 - B1 ** STEP)
    v_hat = nv / (1.0 - B2 ** STEP)
    return -LR * (m_hat / (jnp.sqrt(v_hat) + AEPS) + WD * w), nm, nv


SMALL_KEYS = ("b_ada", "g_pre", "g_post", "lb_logits", "pool_w", "pool_scale", "hgrn_norm_g")


def _small_pieces(key):
    one = lambda i: slice(i, i + 1)
    if key == "b_ada":
        return [((one(l), slice(j * D, (j + 1) * D)), (one(3 * l + j), slice(0, D)))
                for l in range(2) for j in range(3)]
    if key in ("g_pre", "g_post", "lb_logits"):
        row0 = {"g_pre": 8, "g_post": 16, "lb_logits": 24}[key]
        return [((slice(0, 2), slice(0, D)), (slice(row0, row0 + 2), slice(0, D)))]
    if key == "pool_w":
        return [((l, g, pl.ds(k, 16, stride=8), slice(0, 128)),
                 (slice(32 + 64 * l + 16 * g, 48 + 64 * l + 16 * g), slice(128 * k, 128 * (k + 1))))
                for l in range(2) for g in range(GROUPS) for k in range(8)]
    width = {"pool_scale": POOL_W, "hgrn_norm_g": HD}[key]
    row = {"pool_scale": 160, "hgrn_norm_g": 168}[key]
    return [((one(l), slice(0, width)), (one(row), slice(l * width, (l + 1) * width))) for l in range(2)]


def _adamw_small(g_small, g_lb_logits, wmv):
    n = len(SMALL_KEYS)

    def body(g_ref, glb_ref, *refs):
        ins, outs = refs[:3 * n], refs[3 * n:]
        for p, key in enumerate(SMALL_KEYS):
            w_ref, m_ref, v_ref = ins[3 * p:3 * p + 3]
            for at, (rows, lanes) in _small_pieces(key):
                gv = glb_ref[at] if key == "lb_logits" else g_ref[rows, lanes]
                res = _adamw_update(w_ref[at], gv, m_ref[at], v_ref[at])
                for o_ref, val in zip(outs[4 * p:4 * p + 4], (*res, gv)):
                    o_ref[at] = val

    flat = [t for key in SMALL_KEYS for t in wmv[key]]
    outs = pl.pallas_call(body, name="adamw_small",
                          out_shape=[jax.ShapeDtypeStruct(wmv[key][0].shape, F32) for key in SMALL_KEYS
                                     for _ in range(4)],
                          compiler_params=_cp())(g_small, g_lb_logits, *flat)
    return {key: outs[4 * p:4 * p + 4] for p, key in enumerate(SMALL_KEYS)}


def _cast_to_slot(place, w, l, name):
    _, R, C = w.shape
    tr = _row_tile(R, max(8, (1 << 19) // C))

    def body(p_ref, w_ref, o_ref):
        o_ref[...] = w_ref[...].astype(BF16)

    return pl.pallas_call(
        body, name=name, out_shape=jax.ShapeDtypeStruct((NCHIP, R, C), BF16),
        grid_spec=pltpu.PrefetchScalarGridSpec(
            num_scalar_prefetch=1, grid=(R // tr,),
            in_specs=[pl.BlockSpec((None, tr, C), lambda i, p_ref: (l, i, 0))],
            out_specs=pl.BlockSpec((None, tr, C), lambda i, p_ref: (p_ref[0], i, 0))),
        compiler_params=_cp(("parallel",)),
    )(place, w)


def _cast_to_slots(place, ws, name):
    n = len(ws)

    def body(p_ref, *refs):
        for w_ref, o_ref in zip(refs[:n], refs[n:]):
            o_ref[...] = w_ref[...].astype(BF16)

    def layer(l):
        return lambda i, p_ref: (l, 0, 0)

    return pl.pallas_call(
        body, name=name, out_shape=[jax.ShapeDtypeStruct((NCHIP,) + w.shape[1:], BF16) for w, _ in ws],
        grid_spec=pltpu.PrefetchScalarGridSpec(
            num_scalar_prefetch=1, grid=(1,),
            in_specs=[pl.BlockSpec((None,) + w.shape[1:], layer(l)) for w, l in ws],
            out_specs=[pl.BlockSpec((None,) + w.shape[1:], lambda i, p_ref: (p_ref[0], 0, 0)) for w, _ in ws]),
        compiler_params=_cp(("arbitrary",)),
    )(place, *[w for w, _ in ws])


def _pair_adds(core, gs, gots, name):
    n = len(gs)

    def body(c_ref, *refs):
        for a_ref, b_ref, o_ref in zip(refs[:n], refs[n:2 * n], refs[2 * n:]):
            o_ref[...] = (a_ref[...].astype(F32) + b_ref[...].astype(F32)).astype(o_ref.dtype)

    def whole(t):
        return pl.BlockSpec(t.shape, lambda i, c_ref: (0, 0, 0))

    return pl.pallas_call(
        body, name=name, out_shape=[jax.ShapeDtypeStruct(t.shape, BF16) for t in gots],
        grid_spec=pltpu.PrefetchScalarGridSpec(
            num_scalar_prefetch=1, grid=(1,),
            in_specs=[pl.BlockSpec(t.shape, lambda i, c_ref: (0, c_ref[0], 0)) for t in gots]
            + [whole(t) for t in gots],
            out_specs=[whole(t) for t in gots]),
        compiler_params=_cp(("arbitrary",)),
    )(core, *gs, *gots)


def _pair_add(core, g, got, name):
    _, R, C = g.shape
    r2 = R // 2
    tr = _row_tile(r2, max(8, (1 << 19) // C))
    nt = r2 // tr

    def body(c_ref, a_ref, b_ref, o_ref):
        o_ref[...] = (a_ref[...].astype(F32) + b_ref[...].astype(F32)).astype(o_ref.dtype)

    return pl.pallas_call(
        body, name=name, out_shape=jax.ShapeDtypeStruct((NCHIP, r2, C), BF16),
        grid_spec=pltpu.PrefetchScalarGridSpec(
            num_scalar_prefetch=1, grid=(NCHIP, nt),
            in_specs=[pl.BlockSpec((None, tr, C), lambda j, i, c_ref: (j, c_ref[0] * nt + i, 0)),
                      pl.BlockSpec((None, tr, C), lambda j, i, c_ref: (j, i, 0))],
            out_specs=pl.BlockSpec((None, tr, C), lambda j, i, c_ref: (j, i, 0))),
        compiler_params=_cp(("parallel", "parallel")),
    )(core, g, got)


def _sum_in_chip_order(me, own_ref, r_ref):
    own = own_ref[...].astype(F32)
    acc = None
    for j in range(NCHIP):
        slot = jnp.minimum(jnp.where(j > me, j - 1, j), NCHIP - 2)
        term = jnp.where(me == j, own, r_ref[slot].astype(F32))
        acc = term if acc is None else acc + term
    return acc


def _chip_sums(place, parts, recvs, name):
    n = len(parts)

    def body(p_ref, *refs):
        for a in range(n):
            for l in range(2):
                refs[4 * n + a][l] = _sum_in_chip_order(p_ref[0], refs[2 * a + l], refs[2 * n + 2 * a + l])

    def own(t):
        return pl.BlockSpec((None,) + t.shape[1:], lambda i, p_ref: (p_ref[0], 0, 0))

    def whole(t):
        return pl.BlockSpec(t.shape, lambda i, p_ref: (0, 0, 0))

    flat_p = [t for pair in parts for t in pair]
    flat_r = [t for pair in recvs for t in pair]
    return pl.pallas_call(
        body, name=name,
        out_shape=[jax.ShapeDtypeStruct((2, 2 * p[0].shape[1], p[0].shape[2]), F32) for p in parts],
        grid_spec=pltpu.PrefetchScalarGridSpec(
            num_scalar_prefetch=1, grid=(1,),
            in_specs=[own(t) for t in flat_p] + [whole(t) for t in flat_r],
            out_specs=[pl.BlockSpec((2,) + p[0].shape[1:], lambda i, p_ref: (0, p_ref[1], 0)) for p in parts]),
        compiler_params=_cp(("arbitrary",)),
    )(place, *flat_p, *flat_r)


def _chip_sum(place, part, recv, layer, both, name):
    _, r2, C = part.shape
    tr = _row_tile(r2, max(8, (1 << 18) // C))
    nt = r2 // tr

    def body(p_ref, own_ref, r_ref, *rest):
        rest[-1][...] = _sum_in_chip_order(p_ref[0], own_ref, r_ref)

    args = (place, part, recv) if both is None else (place, part, recv, both)
    return pl.pallas_call(
        body, name=name, out_shape=jax.ShapeDtypeStruct((2, 2 * r2, C), F32),
        grid_spec=pltpu.PrefetchScalarGridSpec(
            num_scalar_prefetch=1, grid=(nt,),
            in_specs=[pl.BlockSpec((None, tr, C), lambda i, p_ref: (p_ref[0], i, 0)),
                      pl.BlockSpec((NCHIP - 1, tr, C), lambda i, p_ref: (0, i, 0))] + [ANY] * (len(args) - 3),
            out_specs=pl.BlockSpec((None, tr, C), lambda i, p_ref: (layer, p_ref[1] * nt + i, 0))),
        input_output_aliases={} if both is None else {3: 0},
        compiler_params=_cp(("parallel",)),
    )(*args)


def _place():
    x, y, c = lax.axis_index("x"), lax.axis_index("y"), lax.axis_index("c")
    chips = [(1 - x, y), (x, 1 - y), (1 - x, 1 - y)]
    return x, y, c, chips


def _gather_small(blk, name, carry=None):
    m_per, n = blk.shape
    c_in, c_out = (len(carry.ins), len(carry.outs)) if carry else (0, 0)

    def body(x_ref, *refs):
        ci, (out_ref, *refs) = refs[:c_in], refs[c_in:]
        co, (send_sems, recv_sems, local_sem, *carry_sems) = refs[:c_out], refs[c_out:]
        if carry:
            carry.start(ci, co, *carry_sems)
        gather(x_ref, out_ref, send_sems, recv_sems, local_sem)
        if carry:
            carry.finish(ci, co, *carry_sems)

    def gather(x_ref, out_ref, send_sems, recv_sems, local_sem):
        x, y, c, chips = _place()
        me, sibling = (x, y, c), (x, y, 1 - c)

        def rows(px, py, pc):
            return out_ref.at[pl.ds((4 * px + 2 * py + pc) * m_per, m_per), :]

        def copy(k, block, to, src=None):
            return pltpu.make_async_remote_copy(
                src_ref=rows(*block) if src is None else src, dst_ref=rows(*block),
                send_sem=send_sems.at[k], recv_sem=recv_sems.at[k], device_id=to, device_id_type=MESH)

        mine = pltpu.make_async_copy(x_ref, rows(*me), local_sem)
        mine.start()
        first = [copy(0, me, sibling, src=x_ref)]
        first += [copy(1 + j, me, (*chip, c), src=x_ref) for j, chip in enumerate(chips)]
        for cp in first:
            cp.start()
        passed = [copy(4 + j, (*chip, c), sibling) for j, chip in enumerate(chips)]
        for j, chip in enumerate(chips):
            copy(1 + j, (*chip, c), me).wait_recv()
            passed[j].start()
        copy(0, sibling, me).wait_recv()
        for j, chip in enumerate(chips):
            copy(4 + j, (*chip, 1 - c), me).wait_recv()
        for cp in first + passed:
            cp.wait_send()
        mine.wait()

    in_vmem = pl.BlockSpec(memory_space=pltpu.VMEM)
    outs = pl.pallas_call(
        body, name=name,
        out_shape=[jax.ShapeDtypeStruct((NDEV * m_per, n), blk.dtype)] + (carry.outs if carry else []),
        in_specs=[in_vmem] + [ANY] * c_in, out_specs=[in_vmem] + [ANY] * c_out,
        input_output_aliases={1 + i: 1 + o for i, o in carry.aliases.items()} if carry else {},
        scratch_shapes=[pltpu.SemaphoreType.DMA((7,)), pltpu.SemaphoreType.DMA((7,)), pltpu.SemaphoreType.DMA]
        + ([pltpu.SemaphoreType.DMA((carry.n_sem,))] * 2 if carry else []),
        compiler_params=_cp(),
    )(blk, *(carry.ins if carry else []))
    return outs[0], list(outs[1:])


def _gather_rows_carry(blk):
    m_per, n = blk.shape

    def rows(ref, px, py, pc):
        return ref.at[pl.ds((4 * px + 2 * py + pc) * m_per, m_per), :]

    def copy(ins, outs, send_sems, recv_sems, k, block, to, own=False):
        return pltpu.make_async_remote_copy(
            src_ref=ins[0] if own else rows(outs[0], *block), dst_ref=rows(outs[0], *block),
            send_sem=send_sems.at[k], recv_sem=recv_sems.at[k], device_id=to, device_id_type=MESH)

    def mine(ins, outs, send_sems):
        x, y, c, _ = _place()
        return pltpu.make_async_copy(ins[0], rows(outs[0], x, y, c), send_sems.at[7])

    def start(ins, outs, send_sems, recv_sems):
        x, y, c, chips = _place()
        mine(ins, outs, send_sems).start()
        copy(ins, outs, send_sems, recv_sems, 0, (x, y, c), (x, y, 1 - c), own=True).start()
        for j, chip in enumerate(chips):
            copy(ins, outs, send_sems, recv_sems, 1 + j, (x, y, c), (*chip, c), own=True).start()

    def finish(ins, outs, send_sems, recv_sems):
        x, y, c, chips = _place()
        for j, chip in enumerate(chips):
            copy(ins, outs, send_sems, recv_sems, 1 + j, (*chip, c), (x, y, c)).wait_recv()
            copy(ins, outs, send_sems, recv_sems, 4 + j, (*chip, c), (x, y, 1 - c)).start()
        copy(ins, outs, send_sems, recv_sems, 0, (x, y, 1 - c), (x, y, c)).wait_recv()
        for j, chip in enumerate(chips):
            copy(ins, outs, send_sems, recv_sems, 4 + j, (*chip, 1 - c), (x, y, c)).wait_recv()
        copy(ins, outs, send_sems, recv_sems, 0, (x, y, c), (x, y, 1 - c), own=True).wait_send()
        for j, chip in enumerate(chips):
            copy(ins, outs, send_sems, recv_sems, 1 + j, (x, y, c), (*chip, c), own=True).wait_send()
            copy(ins, outs, send_sems, recv_sems, 4 + j, (*chip, c), (x, y, 1 - c)).wait_send()
        mine(ins, outs, send_sems).wait()

    return _Carry([blk], [jax.ShapeDtypeStruct((NDEV * m_per, n), blk.dtype)], {}, 8, start, finish)


def _gather_carry(shards, piece=(0, 1, 1)):
    n = len(shards)
    first, count, of = piece

    def rows(ref, half):
        r2 = ref.shape[1] // 2
        return pl.ds(half * r2 + first * (r2 // of), count * (r2 // of))

    def over_ici(outs, send_sems, recv_sems, a, j, chip_xy, slot):
        x, y, c, _ = _place()
        blk = outs[a].at[slot, rows(outs[a], c), :]
        return pltpu.make_async_remote_copy(
            src_ref=blk, dst_ref=blk, send_sem=send_sems.at[6 * a + j], recv_sem=recv_sems.at[6 * a + j],
            device_id=(*chip_xy, c), device_id_type=MESH)

    def over_d2d(outs, send_sems, recv_sems, a, j, slot, half):
        x, y, c, _ = _place()
        blk = outs[a].at[slot, rows(outs[a], half), :]
        return pltpu.make_async_remote_copy(
            src_ref=blk, dst_ref=blk, send_sem=send_sems.at[6 * a + 3 + j], recv_sem=recv_sems.at[6 * a + 3 + j],
            device_id=(x, y, 1 - c), device_id_type=MESH)

    def start(ins, outs, send_sems, recv_sems):
        x, y, c, chips = _place()
        for a in range(n):
            for j, chip_xy in enumerate(chips):
                over_ici(outs, send_sems, recv_sems, a, j, chip_xy, 2 * x + y).start()

    def finish(ins, outs, send_sems, recv_sems):
        x, y, c, chips = _place()
        for a in range(n):
            for j, (cx, cy) in enumerate(chips):
                over_ici(outs, send_sems, recv_sems, a, j, (cx, cy), 2 * cx + cy).wait_recv()
                over_d2d(outs, send_sems, recv_sems, a, j, 2 * cx + cy, c).start()
        for a in range(n):
            for j, (cx, cy) in enumerate(chips):
                over_d2d(outs, send_sems, recv_sems, a, j, 2 * cx + cy, 1 - c).wait_recv()
        for a in range(n):
            for j, (cx, cy) in enumerate(chips):
                over_ici(outs, send_sems, recv_sems, a, j, (cx, cy), 2 * x + y).wait_send()
                over_d2d(outs, send_sems, recv_sems, a, j, 2 * cx + cy, c).wait_send()

    return _Carry(shards, [jax.ShapeDtypeStruct(s.shape, s.dtype) for s in shards],
                  {a: a for a in range(n)}, 6 * n, start, finish)


def _rs_pair(grads, name):
    n = len(grads)

    def body(*refs):
        ins, gots = refs[:n], refs[n:2 * n]
        send_sems, recv_sems = refs[2 * n:]
        x, y, c, _ = _place()
        cps = []
        for a in range(n):
            r2 = ins[a].shape[1] // 2
            cp = pltpu.make_async_remote_copy(
                src_ref=ins[a].at[:, pl.ds((1 - c) * r2, r2), :], dst_ref=gots[a],
                send_sem=send_sems.at[a], recv_sem=recv_sems.at[a],
                device_id=(x, y, 1 - c), device_id_type=MESH)
            cp.start()
            cps.append(cp)
        for cp in cps:
            cp.wait()

    half = [jax.ShapeDtypeStruct((NCHIP, g.shape[1] // 2, g.shape[2]), g.dtype) for g in grads]
    return pl.pallas_call(
        body, name=name, out_shape=half, in_specs=[ANY] * n, out_specs=[ANY] * n,
        scratch_shapes=[pltpu.SemaphoreType.DMA((n,)), pltpu.SemaphoreType.DMA((n,))],
        compiler_params=_cp(),
    )(*grads)


def _pair_sum(core, g, name):
    _, R, C = g.shape
    r2 = R // 2

    def body(c_ref, g_ref, own_ref, o_ref, got_ref, buf, send_sems, recv_sems, local_sems):
        j = pl.program_id(0)
        x, y, c, _ = _place()

        def remote(k):
            return pltpu.make_async_remote_copy(
                src_ref=g_ref.at[k, pl.ds((1 - c) * r2, r2), :], dst_ref=got_ref.at[k],
                send_sem=send_sems.at[k], recv_sem=recv_sems.at[k], device_id=(x, y, 1 - c), device_id_type=MESH)

        def fetch(k):
            return pltpu.make_async_copy(got_ref.at[k], buf.at[k % 2], local_sems.at[k % 2])

        @pl.when(j == 0)
        def _():
            for k in range(NCHIP):
                remote(k).start()
            remote(0).wait_recv()
            fetch(0).start()

        fetch(j).wait()

        @pl.when(j + 1 < NCHIP)
        def _():
            remote(j + 1).wait_recv()
            fetch(j + 1).start()

        o_ref[...] = (own_ref[...].astype(F32) + buf[j % 2].astype(F32)).astype(o_ref.dtype)

        @pl.when(j == NCHIP - 1)
        def _():
            for k in range(NCHIP):
                remote(k).wait_send()

    half = jax.ShapeDtypeStruct((NCHIP, r2, C), g.dtype)
    return pl.pallas_call(
        body, name=name, out_shape=[half, half],
        grid_spec=pltpu.PrefetchScalarGridSpec(
            num_scalar_prefetch=1, grid=(NCHIP,),
            in_specs=[ANY, pl.BlockSpec((None, r2, C), lambda j, c_ref: (j, c_ref[0], 0))],
            out_specs=[pl.BlockSpec((None, r2, C), lambda j, c_ref: (j, 0, 0)), ANY],
            scratch_shapes=[pltpu.VMEM((2, r2, C), g.dtype), pltpu.SemaphoreType.DMA((NCHIP,)),
                            pltpu.SemaphoreType.DMA((NCHIP,)), pltpu.SemaphoreType.DMA((2,))]),
        compiler_params=_cp(("arbitrary",)),
    )(core, g, g)[0]


def _pair_carry(grads):
    n = len(grads)

    def copy(ins, outs, send_sems, recv_sems, a):
        x, y, c, _ = _place()
        r2 = ins[a].shape[1] // 2
        return pltpu.make_async_remote_copy(
            src_ref=ins[a].at[:, pl.ds((1 - c) * r2, r2), :], dst_ref=outs[a],
            send_sem=send_sems.at[a], recv_sem=recv_sems.at[a],
            device_id=(x, y, 1 - c), device_id_type=MESH)

    def start(ins, outs, send_sems, recv_sems):
        for a in range(n):
            copy(ins, outs, send_sems, recv_sems, a).start()

    def finish(ins, outs, send_sems, recv_sems):
        for a in range(n):
            copy(ins, outs, send_sems, recv_sems, a).wait()

    half = [jax.ShapeDtypeStruct((NCHIP, g.shape[1] // 2, g.shape[2]), g.dtype) for g in grads]
    return _Carry(grads, half, {}, n, start, finish)


def _chips_carry(parts, piece=(0, 1, 1), into=None):
    n = len(parts)
    first, count, of = piece

    def rows(ref):
        step = ref.shape[1] // of
        return pl.ds(first * step, count * step)

    def send(ins, outs, send_sems, recv_sems, a, j, chip_xy):
        x, y, c, _ = _place()
        me, them = 2 * x + y, 2 * chip_xy[0] + chip_xy[1]
        return pltpu.make_async_remote_copy(
            src_ref=ins[a].at[them, rows(ins[a]), :],
            dst_ref=outs[a].at[me - (me > them).astype(jnp.int32), rows(outs[a]), :],
            send_sem=send_sems.at[3 * a + j], recv_sem=recv_sems.at[3 * a + j],
            device_id=(*chip_xy, c), device_id_type=MESH)

    def start(ins, outs, send_sems, recv_sems):
        _, _, _, chips = _place()
        for a in range(n):
            for j, chip_xy in enumerate(chips):
                send(ins, outs, send_sems, recv_sems, a, j, chip_xy).start()

    def finish(ins, outs, send_sems, recv_sems):
        x, y, c, chips = _place()
        me = 2 * x + y
        for a in range(n):
            for j, (cx, cy) in enumerate(chips):
                them = 2 * cx + cy
                blk = outs[a].at[them - (them > me).astype(jnp.int32), rows(outs[a]), :]
                pltpu.make_async_remote_copy(
                    src_ref=blk, dst_ref=blk, send_sem=send_sems.at[3 * a + j], recv_sem=recv_sems.at[3 * a + j],
                    device_id=(cx, cy, c), device_id_type=MESH).wait_recv()
        for a in range(n):
            for j, chip_xy in enumerate(chips):
                send(ins, outs, send_sems, recv_sems, a, j, chip_xy).wait_send()

    landing = [jax.ShapeDtypeStruct((NCHIP - 1,) + p.shape[1:], p.dtype) for p in parts]
    if into is None:
        return _Carry(parts, landing, {}, 3 * n, start, finish)
    return _Carry(list(parts) + list(into), landing, {n + a: a for a in range(n)}, 3 * n, start, finish)


def _swap_carry(fulls, layers):
    n = len(fulls)

    def copy(outs, send_sems, recv_sems, a, half):
        x, y, c, _ = _place()
        r2 = outs[a].shape[1] // 2
        blk = outs[a].at[pl.ds(*layers[a]), pl.ds(half * r2, r2), :]
        return pltpu.make_async_remote_copy(
            src_ref=blk, dst_ref=blk, send_sem=send_sems.at[a], recv_sem=recv_sems.at[a],
            device_id=(x, y, 1 - c), device_id_type=MESH)

    def start(ins, outs, send_sems, recv_sems):
        c = lax.axis_index("c")
        for a in range(n):
            copy(outs, send_sems, recv_sems, a, c).start()

    def finish(ins, outs, send_sems, recv_sems):
        c = lax.axis_index("c")
        for a in range(n):
            copy(outs, send_sems, recv_sems, a, 1 - c).wait_recv()
        for a in range(n):
            copy(outs, send_sems, recv_sems, a, c).wait_send()

    return _Carry(fulls, [jax.ShapeDtypeStruct(f.shape, f.dtype) for f in fulls], {a: a for a in range(n)}, n,
                  start, finish)


def _rs_swap(fulls, layers):
    n = len(fulls)
    swap = _swap_carry(fulls, layers)

    def body(*refs):
        outs, (send_sems, recv_sems) = refs[n:2 * n], refs[2 * n:]
        swap.start(None, outs, send_sems, recv_sems)
        swap.finish(None, outs, send_sems, recv_sems)

    return pl.pallas_call(
        body, name="rs_swap", out_shape=swap.outs, in_specs=[ANY] * n, out_specs=[ANY] * n,
        input_output_aliases=swap.aliases,
        scratch_shapes=[pltpu.SemaphoreType.DMA((n,)), pltpu.SemaphoreType.DMA((n,))],
        compiler_params=_cp(),
    )(*fulls)


def _tail_weight_grads(merged_t, b_in_t, a_in_t, dy, dbr_b, dbr_a, name, tn=256):
    S = dy.shape[0]
    nn = D // tn

    def body(mt_ref, bt_ref, at_ref, dy_ref, db_ref, da_ref, go_ref, gh_ref, gp_ref):
        go_ref[...] = jnp.dot(mt_ref[...], dy_ref[...], preferred_element_type=F32).astype(BF16)
        gh_ref[...] = jnp.dot(bt_ref[...], db_ref[...], preferred_element_type=F32).astype(BF16)
        gp_ref[...] = jnp.dot(at_ref[...], da_ref[...], preferred_element_type=F32).astype(BF16)

    left = lambda rows: pl.BlockSpec((rows, S), lambda n: (0, 0))
    right = pl.BlockSpec((S, tn), lambda n: (0, n))
    out = pl.BlockSpec((D, tn), lambda n: (0, n))
    return pl.pallas_call(
        body, name=name, grid=(nn,), in_specs=[left(D), left(D), left(POOL_W), right, right, right],
        out_specs=[out, out, pl.BlockSpec((None, POOL_W, tn), lambda n: (n, 0, 0))],
        out_shape=[jax.ShapeDtypeStruct((D, D), BF16), jax.ShapeDtypeStruct((D, D), BF16),
                   jax.ShapeDtypeStruct((NCHIP, POOL_W, D // NCHIP), BF16)],
        compiler_params=_cp(("parallel",)),
    )(merged_t, b_in_t, a_in_t, dy, dbr_b, dbr_a)


class _GatherInProj:
    def __init__(self, slot, order, piece):
        self.slot, self.order, self.piece = slot, order, piece


def _proj_with_gather(h, w_slot, order, piece, name, tn=256):
    S, K = h.shape
    nsh, _, ns = w_slot.shape
    tps = ns // tn
    nt = nsh * tps
    r2 = K // 2
    first, count, of = piece

    def body(ord_ref, h_ref, w_in_ref, o_ref, w_ref, wbuf, tile_sems, send_sems, recv_sems):
        n = pl.program_id(0)
        x, y, c, chips = _place()

        def half(slot, which):
            return w_ref.at[slot, pl.ds(which * r2 + first * (r2 // of), count * (r2 // of)), :]

        def over_ici(j, slot):
            blk = half(slot, c)
            return pltpu.make_async_remote_copy(src_ref=blk, dst_ref=blk, send_sem=send_sems.at[j],
                                                recv_sem=recv_sems.at[j], device_id=(*chips[j], c),
                                                device_id_type=MESH)

        def over_d2d(j, which):
            blk = half(2 * chips[j][0] + chips[j][1], which)
            return pltpu.make_async_remote_copy(src_ref=blk, dst_ref=blk, send_sem=send_sems.at[3 + j],
                                                recv_sem=recv_sems.at[3 + j], device_id=(x, y, 1 - c),
                                                device_id_type=MESH)

        def tile_copy(step, slot):
            shard = ord_ref[step // tps]
            return pltpu.make_async_copy(w_ref.at[shard, :, pl.ds((step % tps) * tn, tn)], wbuf.at[slot],
                                         tile_sems.at[slot])

        @pl.when(n == 0)
        def _():
            for j in range(3):
                over_ici(j, 2 * x + y).start()
            tile_copy(0, 0).start()

        for j in range(3):
            @pl.when(n == (j + 1) * tps - 1)
            def _(j=j):
                over_ici(j, 2 * chips[j][0] + chips[j][1]).wait_recv()
                over_d2d(j, c).start()
                over_d2d(j, 1 - c).wait_recv()

        @pl.when(n + 1 < nt)
        def _():
            tile_copy(n + 1, (n + 1) % 2).start()

        tile_copy(n, n % 2).wait()
        o_ref[...] = jnp.dot(h_ref[...], wbuf[n % 2], preferred_element_type=F32).astype(o_ref.dtype)

        @pl.when(n == nt - 1)
        def _():
            for j in range(3):
                over_ici(j, 2 * x + y).wait_send()
                over_d2d(j, c).wait_send()

    return pl.pallas_call(
        body, name=name,
        out_shape=[jax.ShapeDtypeStruct((S, nsh * ns), BF16), jax.ShapeDtypeStruct(w_slot.shape, w_slot.dtype)],
        grid_spec=pltpu.PrefetchScalarGridSpec(
            num_scalar_prefetch=1, grid=(nt,),
            in_specs=[pl.BlockSpec((S, K), lambda n, o_ref: (0, 0)), ANY],
            out_specs=[pl.BlockSpec((S, tn), lambda n, o_ref: (0, o_ref[n // tps] * tps + n % tps)), ANY],
            scratch_shapes=[pltpu.VMEM((2, K, tn), w_slot.dtype), pltpu.SemaphoreType.DMA((2,)),
                            pltpu.SemaphoreType.DMA((6,)), pltpu.SemaphoreType.DMA((6,))]),
        input_output_aliases={2: 1},
        compiler_params=_cp(("arbitrary",)),
    )(order, h, w_slot)


def _mm_ride(a, b, carry, **kw):
    if carry is None:
        return _mm(a, b, **kw), []
    return _mm(a, b, carry=carry, **kw)


def _layer_fwd(l, x, ada, w, small, ride, target=None):
    shift, scale, gate = ada[:, 0:D], ada[:, D:2 * D], ada[:, 2 * D:3 * D]
    carry, landed = ride("prenorm")
    (h, h_t), outs = _prenorm_fwd(x, small["g_pre"][l], scale, shift, f"prenorm_fwd{l}", carry)
    landed(outs)
    carry, landed = ride("proj")
    if isinstance(carry, _GatherInProj):
        proj, full = _proj_with_gather(h, carry.slot, carry.order, carry.piece, f"proj{l}")
        outs = [full]
    else:
        proj, outs = _mm_ride(h, w["w_in"][l], carry, name=f"proj{l}", b_mode="nn_sh", tm=2048, out_dtype=BF16)
    landed(outs)
    a_in, a_in_t = _pool_fwd(proj, small["pool_w"][l], small["pool_scale"][l], f"pool_fwd{l}")
    carry, landed = ride("hgrn")
    (b_in, b_in_t, o_raw, states, mild, cum), outs = _hgrn_fwd(proj, small["lb"][l], small["hgrn_norm_g"][l],
                                                              f"hgrn_fwd{l}", carry=carry)
    landed(outs)
    carry, landed = ride("tail")
    (br_a, br_b, merged_t, y, *x_new), outs = _layer_tail_fwd(
        proj, a_in, b_in, x, w["w_pool_o"][l], w["w_hgrn_o"][l].reshape(D, D), w["w_out"][l].reshape(D, D),
        gate, small["g_post"][l], f"tail_fwd{l}", target=target, carry=carry)
    landed(outs)
    saved = dict(x=x, h_t=h_t, proj=proj, a_in_t=a_in_t, b_in_t=b_in_t, o_raw=o_raw, states=states, mild=mild,
                 cum=cum,
                 br_a=br_a, br_b=br_b, merged_t=merged_t, y=y, scale=scale, gate=gate)
    return x_new, saved


def _layer_bwd(l, dxn, sv, w, small, ride):
    carry, landed = ride["head"](None)
    (dy, dbr_a, dbr_b, dproj, da_in, db_in, dgate, dg_post), outs = _layer_head_bwd(
        dxn, sv["y"], sv["proj"], sv["br_a"], sv["br_b"], w["w_pool_o"][l], w["w_hgrn_o"][l].reshape(D, D),
        w["w_out"][l].reshape(D, D), sv["gate"], small["g_post"][l], f"head_bwd{l}", carry)
    landed(outs)
    gw_out, gw_hgrn_o, gw_pool_o = _tail_weight_grads(sv["merged_t"], sv["b_in_t"], sv["a_in_t"], dy, dbr_b,
                                                      dbr_a, f"gw_tail{l}")
    big = dict(w_pool_o=gw_pool_o, w_hgrn_o=gw_hgrn_o.reshape(NCHIP, D // NCHIP, D),
               w_out=gw_out.reshape(NCHIP, D // NCHIP, D))
    carry, landed = ride["hgrn"](big)
    (dproj, dlb, dgn), outs = _hgrn_bwd(db_in, sv["proj"], sv["o_raw"], sv["states"], sv["mild"], sv["cum"],
                                        small["lb"][l], small["hgrn_norm_g"][l], dproj, f"hgrn_bwd{l}",
                                        carry=carry)
    landed(outs)
    dproj, dpw, dpsc = _pool_bwd(da_in, sv["proj"], small["pool_w"][l], small["pool_scale"][l], dproj,
                                 f"pool_bwd{l}")
    little = dict(dgate=dgate, g_post=dg_post, pool_w=dpw, pool_scale=dpsc, lb=dlb,
                  hgrn_norm_g=jnp.sum(dgn, axis=0, keepdims=True))
    carry, landed = ride["gw_in"](little)
    big["w_in"], outs = _mm_ride(sv["h_t"], dproj, carry, name=f"gw_in{l}", out_shards=NCHIP, out_dtype=BF16)
    landed(outs)
    carry, landed = ride["d_h"](big)
    dh, outs = _mm_ride(dproj, w["w_in"][l], carry, name=f"d_h{l}", b_mode="nt_shk", tn=1024)
    landed(outs)
    carry, landed = ride["prenorm"](big)
    (dx, dshift, dscale, dg_pre), outs = _prenorm_bwd(dh, dxn, sv["x"], small["g_pre"][l], sv["scale"],
                                                      f"prenorm_bwd{l}", carry)
    landed(outs)
    little.update(dshift=dshift, dscale=dscale, g_pre=dg_pre)
    return dx, big, little


SMALL_ROWS = 176


def _rows8(t):
    t = t.reshape(-1, D)
    return jnp.pad(t, ((0, -t.shape[0] % 8), (0, 0)))


def _pack_small(parts):
    row_keys = ("dshift", "dscale", "dgate", "g_pre", "g_post", "lb", "pool_scale", "hgrn_norm_g")
    flat = [p[k] for p in parts for k in row_keys] + [p["pool_w"] for p in parts]
    nk = len(row_keys)

    def body(*refs):
        o_ref = refs[-1]
        o_ref[...] = jnp.zeros((SMALL_ROWS, D), F32)
        for l in range(2):
            dshift, dscale, dgate, g_pre, g_post, lb, pscale, gn = refs[l * nk:(l + 1) * nk]
            for r, ref in enumerate((dshift, dscale, dgate)):
                o_ref[3 * l + r:3 * l + r + 1, :] = ref[...]
            o_ref[8 + l:9 + l, :] = g_pre[...]
            o_ref[16 + l:17 + l, :] = g_post[...]
            o_ref[24 + l:25 + l, :] = lb[...]
            o_ref[160:161, l * POOL_W:(l + 1) * POOL_W] = pscale[...]
            o_ref[168:169, l * HD:(l + 1) * HD] = gn[...]
        for (l, *at), (rows, lanes) in _small_pieces("pool_w"):
            o_ref[rows, lanes] = refs[2 * nk + l][tuple(at)]

    return pl.pallas_call(body, name="pack_small", out_shape=jax.ShapeDtypeStruct((SMALL_ROWS, D), F32),
                          compiler_params=_cp())(*flat)


def kernel(x, c, w_ada, b_ada, g_pre, g_post, w_in, pool_w, pool_scale, lb_logits, hgrn_norm_g, w_pool_o, w_hgrn_o, w_out, loss_target, m_w_ada, m_b_ada, m_g_pre, m_g_post, m_w_in, m_pool_w, m_pool_scale, m_lb_logits, m_hgrn_norm_g, m_w_pool_o, m_w_hgrn_o, m_w_out, v_w_ada, v_b_ada, v_g_pre, v_g_post, v_w_in, v_pool_w, v_pool_scale, v_lb_logits, v_hgrn_norm_g, v_w_pool_o, v_w_hgrn_o, v_w_out):
    ax, ay, ac = lax.axis_index("x"), lax.axis_index("y"), lax.axis_index("c")
    chip = 2 * ax + ay
    dev = 2 * chip + ac
    xe, te = x[0], loss_target[0]
    ada_s = w_ada.shape[2]

    big_names = ("w_in", "w_pool_o", "w_hgrn_o", "w_out")
    big_w = (w_in, w_pool_o, w_hgrn_o, w_out)
    core = jnp.stack([ac]).astype(jnp.int32)
    place = jnp.stack([chip, ac]).astype(jnp.int32)
    slots = {("w_in", l): _cast_to_slot(place, w_in, l, f"cast_w_in{l}") for l in range(2)}
    rest = [(k, l) for l in range(2) for k in big_names[1:]]
    slots.update(zip(rest, _cast_to_slots(place, [(dict(zip(big_names, big_w))[k], l) for k, l in rest],
                                          "cast_rest")))
    w = {k: [None, None] for k in big_names}
    def fills(keys):
        def landed(outs):
            for (k, l), o in zip(keys, outs):
                w[k][l] = slots[k, l] = o
        return landed

    rest0 = [(k, 0) for k in big_names[1:]]
    rest1 = [(k, 1) for k in big_names[1:]]
    no_carry = (None, lambda outs: None)
    order = jnp.stack([chip, 2 * (1 - ax) + ay, 2 * ax + (1 - ay), 2 * (1 - ax) + (1 - ay)]).astype(jnp.int32)

    def ride_fwd0(stage):
        if stage == "proj":
            return _GatherInProj(slots["w_in", 0], order, (2, 6, 8)), fills([("w_in", 0)])
        if stage == "hgrn":
            return (_join_carries(_gather_carry([slots[t] for t in rest0]),
                                  _gather_carry([slots["w_in", 1]], piece=(0, 2, 4))),
                    fills(rest0 + [("w_in", 1)]))
        if stage == "tail":
            return _gather_carry([slots["w_in", 1]], piece=(2, 1, 4)), fills([("w_in", 1)])
        return no_carry

    def ride_fwd1(stage):
        if stage == "prenorm":
            return _gather_carry([slots["w_in", 1]], piece=(3, 1, 4)), fills([("w_in", 1)])
        if stage == "hgrn":
            return _gather_carry([slots[t] for t in rest1]), fills(rest1)
        return no_carry

    def w_in0_eighth(k):
        return _gather_carry([slots["w_in", 0]], piece=(k, 1, 8))

    c_all, (slots["w_in", 0],) = _gather_small(jnp.broadcast_to(c, (8, D)), "gather_c", w_in0_eighth(0))
    c_all = c_all.reshape(NDEV, 8, D)[:, 0, :]
    c_pad = jnp.pad(c_all, ((0, ADA_PAD - NDEV), (0, 0)))
    b_sh = lax.dynamic_slice(b_ada, (0, chip * ada_s), (2, ada_s))
    ada_cols, (slots["w_in", 0],) = _gather_small(_ada_fwd(c_pad, w_ada, b_sh), "gather_ada",
                                                  w_in0_eighth(1))
    ada_cols = ada_cols.reshape(NCHIP, 2, NDEV, 2, ada_s)[:, 0]
    ada_all = jnp.transpose(ada_cols, (2, 1, 0, 3)).reshape(2, NDEV, 3 * D)
    ada_me = lax.dynamic_slice(ada_all, (0, dev, 0), (2, 1, 3 * D))

    lbs = _lb_fwd(lb_logits)
    small = dict(g_pre=g_pre[:, None, :], g_post=g_post[:, None, :], pool_w=pool_w,
                 pool_scale=pool_scale[:, None, :], lb=lbs[:, None, :], hgrn_norm_g=hgrn_norm_g[:, None, :])

    (x1,), sv0 = _layer_fwd(0, xe, ada_me[0], w, small, ride_fwd0)
    (dx2, loss_blk), sv1 = _layer_fwd(1, x1, ada_me[1], w, small, ride_fwd1, target=te)

    parts, recv, held = {}, {}, {}

    def pair_ride(keys, grads):
        def landed(outs):
            held.update({kl: (g, o) for kl, g, o in zip(keys, grads, outs)})
        return _pair_carry(grads), landed

    def pair_adds(keys):
        gs, gots = zip(*[held.pop(kl) for kl in keys])
        if keys[0][0] == "w_in":
            parts[keys[0]] = _pair_add(core, gs[0], gots[0], f"rs_add_w_in{keys[0][1]}")
        else:
            parts.update(zip(keys, _pair_adds(core, gs, gots, f"rs_add_early{keys[0][1]}")))

    def exchange(keys):
        def landed(outs):
            recv.update(zip(keys, outs))
        return _chips_carry([parts[kl] for kl in keys]), landed

    def share(key, first, count):
        def landed(outs):
            (recv[key],) = outs
        into = [recv[key]] if key in recv else None
        return _chips_carry([parts[key]], piece=(first, count, 8), into=into), landed

    def together(*rides):
        carries, fns = zip(*rides)

        def landed(outs):
            for cr, fn in zip(carries, fns):
                fn(outs[:len(cr.outs)])
                outs = outs[len(cr.outs):]
        return _join_carries(*carries), landed

    def early(l):
        return [(k, l) for k in big_names[1:]]

    def pair_alone(keys, grads, tag):
        held.update({kl: (g, o) for kl, g, o in zip(keys, grads, _rs_pair(grads, f"rs_pair_{tag}"))})
        pair_adds(keys)

    def ride_hgrn1(big):
        return pair_ride(early(1), [big[k] for k in big_names[1:]])

    def ride_gw_in1(_):
        pair_adds(early(1))
        return exchange(early(1))

    def ride_d_h1(big):
        return pair_ride([("w_in", 1)], [big["w_in"]])

    def ride_prenorm1(_):
        pair_adds([("w_in", 1)])
        return share(("w_in", 1), 0, 1)

    def ride_head0(_):
        return share(("w_in", 1), 1, 3)

    def ride_hgrn0(big):
        pair_alone(early(0), [big[k] for k in big_names[1:]], "early0")
        return together(exchange(early(0)), share(("w_in", 1), 4, 4))

    def ride_d_h0(big):
        parts["w_in", 0] = _pair_sum(core, big["w_in"], "rs_pair_sum_w_in0")
        return share(("w_in", 0), 0, 4)

    def ride_prenorm0(_):
        return share(("w_in", 0), 4, 4)

    no_ride = lambda so_far: no_carry
    dx1, big1, little1 = _layer_bwd(1, dx2, sv1, w, small, dict(head=no_ride, hgrn=ride_hgrn1, gw_in=ride_gw_in1,
                                                                d_h=ride_d_h1, prenorm=ride_prenorm1))

    gathered = {}
    zero_row = jnp.zeros((1, D), F32)

    def ride_gw_in0(little):
        so_far = dict(little, dshift=zero_row, dscale=zero_row, g_pre=zero_row)

        def landed(outs):
            (gathered["early"],) = outs

        red = [_chip_sum(place, parts["w_in", 1], recv["w_in", 1], 1, None, "rs_sum_w_in1")]
        red += _chip_sums(place, [[parts[k, l] for l in range(2)] for k in big_names[1:]],
                          [[recv[k, l] for l in range(2)] for k in big_names[1:]], "rs_sum_early")

        def swapped(outs):
            gathered["sums"] = outs
        return together((_gather_rows_carry(_pack_small([so_far, little1])), landed),
                        (_swap_carry(red, [(1, 1)] + [(0, 2)] * 3), swapped))

    dx0, big0, little0 = _layer_bwd(0, dx1, sv0, w, small,
                                    dict(head=ride_head0, hgrn=ride_hgrn0, gw_in=ride_gw_in0, d_h=ride_d_h0,
                                         prenorm=ride_prenorm0))
    loss_row = jnp.broadcast_to(loss_blk[0:1, 0:1], (1, D))
    late = _rows8(jnp.stack([little0["dshift"], little0["dscale"], little0["g_pre"], loss_row]))
    late = _gather_small(late, "gather_small_late")[0].reshape(NDEV, 8, D)
    loss = jnp.sum(late[:, 3, 0])
    packed = gathered["early"].reshape(NDEV, SMALL_ROWS, D)
    packed = packed.at[:, 0:2, :].set(late[:, 0:2, :]).at[:, 8:9, :].set(late[:, 2:3, :])
    red = _chip_sum(place, parts["w_in", 0], recv["w_in", 0], 0, gathered["sums"][0], "rs_sum_w_in0")
    g_big = dict(zip(big_names, list(_rs_swap([red], [(0, 1)])) + list(gathered["sums"][1:])))

    def two(t):
        return t.reshape(-1, t.shape[-1])

    def upd(wt, g, m, v, name, echo=False):
        return [t.reshape(wt.shape) for t in _adamw(two(wt), two(g), two(m), two(v), name, echo)]

    *u_w_in, g_w_in = upd(w_in, g_big["w_in"], m_w_in, v_w_in, "adamw_w_in", echo=True)
    g_small = _sum_devices(packed)
    g_lb_logits = _lb_bwd(lb_logits, g_small[24:26])
    d_ada_all = packed[:, 0:6, :].reshape(NDEV, 2, 3 * D)
    d_ada_sh = lax.dynamic_slice(jnp.transpose(d_ada_all, (1, 0, 2)), (0, 0, chip * ada_s), (2, NDEV, ada_s))
    d_ada_sh = jnp.pad(d_ada_sh, ((0, 0), (0, ADA_PAD - NDEV), (0, 0)))
    g_w_ada = _ada_wgrad(c_pad.T, d_ada_sh)

    u_w_ada = upd(w_ada, g_w_ada, m_w_ada, v_w_ada, "adamw_w_ada")
    early_w = dict(w_pool_o=(w_pool_o, m_w_pool_o, v_w_pool_o), w_hgrn_o=(w_hgrn_o, m_w_hgrn_o, v_w_hgrn_o),
                   w_out=(w_out, m_w_out, v_w_out))
    u_early = _adamw_many([(two(early_w[k][0]), two(g_big[k]), two(early_w[k][1]), two(early_w[k][2]))
                           for k in big_names[1:]], "adamw_early")
    (*u_w_pool_o, g_w_pool_o), (*u_w_hgrn_o, g_w_hgrn_o), (*u_w_out, g_w_out) = [
        [t.reshape(early_w[k][0].shape) for t in four] for k, four in zip(big_names[1:], u_early)]
    small_w = dict(b_ada=(b_ada, m_b_ada, v_b_ada), g_pre=(g_pre, m_g_pre, v_g_pre),
                   g_post=(g_post, m_g_post, v_g_post), lb_logits=(lb_logits, m_lb_logits, v_lb_logits),
                   pool_w=(pool_w, m_pool_w, v_pool_w), pool_scale=(pool_scale, m_pool_scale, v_pool_scale),
                   hgrn_norm_g=(hgrn_norm_g, m_hgrn_norm_g, v_hgrn_norm_g))
    u_small = _adamw_small(g_small, g_lb_logits, small_w)
    s = lambda key: u_small[key][3]
    grads_out = (g_w_ada, s("b_ada"), s("g_pre"), s("g_post"), g_w_in, s("pool_w"), s("pool_scale"), g_lb_logits,
                 s("hgrn_norm_g"), g_w_pool_o, g_w_hgrn_o, g_w_out)

    def ordered(k):
        s = lambda key: u_small[key][k]
        return (u_w_ada[k], s("b_ada"), s("g_pre"), s("g_post"), u_w_in[k], s("pool_w"), s("pool_scale"),
                s("lb_logits"), s("hgrn_norm_g"), u_w_pool_o[k], u_w_hgrn_o[k], u_w_out[k])

    return (loss, dx0[None], *grads_out, *ordered(0), *ordered(1), *ordered(2))
```

```python
import functools

import jax
import jax.numpy as jnp
from jax import lax
from jax.experimental import pallas as pl
from jax.experimental.pallas import tpu as pltpu

F32 = jnp.float32
BF16 = jnp.bfloat16
MESH = pl.DeviceIdType.MESH

D = 1024
HEADS = 8
HD = 128
GROUPS = 4
POOL_W = 512
CH = 128
SB_WIDE = 32
SB = 16
NH = 2
IN_W = 7168
NCHIP = 4
NDEV = 8
EPS = 1e-6
PV0, PG0, HQ0, HF0, HI0, HG0 = 0, 4, 8, 16, 24, 32
MGP_BLK, MGH_BLK = 5, 6

LR, B1, B2, AEPS, WD, STEP = 0.001, 0.9, 0.999, 1e-08, 0.01, 10
VMEM_LIMIT = 56 * 1024 * 1024


def _cp(sem=None, **kw):
    if sem is not None:
        kw["dimension_semantics"] = sem
    return pltpu.CompilerParams(vmem_limit_bytes=VMEM_LIMIT, **kw)


def _sig(z):
    return 1.0 / (1.0 + jnp.exp(-z))


def _dsilu(z, s):
    return s * (1.0 + z * (1.0 - s))


def _row_tile(rows, cap):
    if rows <= cap:
        return rows
    t = 1 << (cap.bit_length() - 1)
    while rows % t:
        t //= 2
    return t


ANY = pl.BlockSpec(memory_space=pl.ANY)


class _Carry:
    def __init__(self, ins, outs, aliases, n_sem, start, finish):
        self.ins, self.outs, self.aliases, self.n_sem = list(ins), list(outs), dict(aliases), n_sem
        self.start, self.finish = start, finish


class _SemWindow:
    def __init__(self, ref, base):
        self._ref, self._base = ref, base

    @property
    def at(self):
        return self

    def __getitem__(self, k):
        return self._ref.at[self._base + k]


def _join_carries(*carries):
    ins, outs, aliases, spans, n_sem = [], [], {}, [], 0
    for cr in carries:
        aliases.update({len(ins) + i: len(outs) + o for i, o in cr.aliases.items()})
        spans.append((len(ins), len(cr.ins), len(outs), len(cr.outs), n_sem))
        ins, outs, n_sem = ins + cr.ins, outs + cr.outs, n_sem + cr.n_sem

    def run(which):
        def fn(i_refs, o_refs, send_sems, recv_sems):
            for cr, (i0, ni, o0, no, s0) in zip(carries, spans):
                getattr(cr, which)(i_refs[i0:i0 + ni], o_refs[o0:o0 + no], _SemWindow(send_sems, s0),
                                   _SemWindow(recv_sems, s0))
        return fn

    return _Carry(ins, outs, aliases, n_sem, run("start"), run("finish"))


def _call(body, *, name, grid, in_specs, out_specs, out_shape, args, scratch_shapes=(), sem=None, carry=None,
          aliases=None):
    in_specs, out_specs, out_shape = list(in_specs), list(out_specs), list(out_shape)
    scratch_shapes = list(scratch_shapes)
    aliases = dict(aliases or {})
    if carry is None:
        outs = pl.pallas_call(body, name=name, grid=grid, in_specs=in_specs, out_specs=out_specs,
                              out_shape=out_shape, scratch_shapes=scratch_shapes, input_output_aliases=aliases,
                              compiler_params=_cp(sem))(*args)
        return list(outs)
    n_in, n_out, n_scr = len(in_specs), len(out_specs), len(scratch_shapes)
    c_in, c_out = len(carry.ins), len(carry.outs)

    def wrapped(*refs):
        k_in, rest = refs[:n_in], refs[n_in:]
        ci, rest = rest[:c_in], rest[c_in:]
        k_out, rest = rest[:n_out], rest[n_out:]
        co, rest = rest[:c_out], rest[c_out:]
        k_scr, (ssem, rsem) = rest[:n_scr], rest[n_scr:]
        pids = [pl.program_id(d) for d in range(len(grid))]
        first = functools.reduce(jnp.logical_and, [p == 0 for p in pids])
        last = functools.reduce(jnp.logical_and, [p == g - 1 for p, g in zip(pids, grid)])

        @pl.when(first)
        def _():
            carry.start(ci, co, ssem, rsem)

        body(*k_in, *k_out, *k_scr)

        @pl.when(last)
        def _():
            carry.finish(ci, co, ssem, rsem)

    outs = pl.pallas_call(
        wrapped, name=name, grid=grid, in_specs=in_specs + [ANY] * c_in, out_specs=out_specs + [ANY] * c_out,
        out_shape=out_shape + carry.outs,
        input_output_aliases={**aliases, **{n_in + i: n_out + o for i, o in carry.aliases.items()}},
        scratch_shapes=scratch_shapes + [pltpu.SemaphoreType.DMA((carry.n_sem,))] * 2,
        compiler_params=_cp(("arbitrary",) * len(grid)),
    )(*args, *carry.ins)
    return list(outs)


def _mm(a, b, *, name, b_mode="nn", out_shards=0, tm=1024, tn=256, tk=None, out_dtype=F32, carry=None):
    assert b_mode in ("nn", "nn_sh", "nt_shk"), b_mode
    M, K = a.shape
    if b_mode == "nn":
        N = b.shape[1]
    elif b_mode == "nn_sh":
        N = b.shape[0] * b.shape[2]
    else:
        N = b.shape[1]
    tm = _row_tile(M, tm)
    if b_mode == "nn_sh":
        tn = _row_tile(b.shape[2], tn)
    elif out_shards:
        tn = _row_tile(N // out_shards, tn)
    else:
        tn = _row_tile(N, tn)
    if tk is None:
        tk = K if b_mode != "nt_shk" else b.shape[2]
    if b_mode == "nt_shk":
        tk = _row_tile(b.shape[2], tk)
    nm, nn, nk = M // tm, N // tn, K // tk

    a_spec = pl.BlockSpec((tm, tk), lambda m, n, k: (m, k))
    if b_mode == "nn":
        b_spec = pl.BlockSpec((tk, tn), lambda m, n, k: (k, n))
    elif b_mode == "nn_sh":
        nps = b.shape[2] // tn
        b_spec = pl.BlockSpec((None, tk, tn), lambda m, n, k: (n // nps, k, n % nps))
    else:
        kps = b.shape[2] // tk
        b_spec = pl.BlockSpec((None, tn, tk), lambda m, n, k: (k // kps, n, k % kps))
    if out_shards:
        ops = (N // out_shards) // tn
        o_spec = pl.BlockSpec((None, tm, tn), lambda m, n, k: (n // ops, m, n % ops))
        o_shape = jax.ShapeDtypeStruct((out_shards, M, N // out_shards), out_dtype)
    else:
        o_spec = pl.BlockSpec((tm, tn), lambda m, n, k: (m, n))
        o_shape = jax.ShapeDtypeStruct((M, N), out_dtype)
    dn = (((1,), (1,)), ((), ())) if b_mode == "nt_shk" else (((1,), (0,)), ((), ()))

    def body(a_ref, b_ref, o_ref, acc_ref):
        k = pl.program_id(2)

        @pl.when(k == 0)
        def _():
            acc_ref[...] = jnp.zeros(acc_ref.shape, F32)

        acc_ref[...] += lax.dot_general(a_ref[...].astype(BF16), b_ref[...].astype(BF16), dn,
                                        preferred_element_type=F32)

        @pl.when(k == nk - 1)
        def _():
            o_ref[...] = acc_ref[...].astype(o_ref.dtype)

    outs = _call(body, name=name, grid=(nm, nn, nk), in_specs=[a_spec, b_spec], out_specs=[o_spec],
                 out_shape=[o_shape], scratch_shapes=[pltpu.VMEM((tm, tn), F32)],
                 sem=("parallel", "parallel", "arbitrary"), args=(a, b), carry=carry)
    return outs[0] if carry is None else (outs[0], outs[1:])


def _rowvec(n=D):
    return pl.BlockSpec((1, n), lambda i: (0, 0))


def _prenorm_fwd(x, g, scale, shift, name, carry=None):
    S = x.shape[0]
    tr = _row_tile(S, 256)

    def body(x_ref, g_ref, sc_ref, sh_ref, h_ref, ht_ref):
        xv = x_ref[...]
        r = lax.rsqrt(jnp.mean(xv * xv, axis=-1, keepdims=True) + EPS)
        hv = (xv * r) * g_ref[...] * (1.0 + sc_ref[...]) + sh_ref[...]
        h_ref[...] = hv.astype(BF16)
        ht_ref[...] = hv.T.astype(BF16)

    outs = _call(
        body, name=name, grid=(S // tr,),
        in_specs=[pl.BlockSpec((tr, D), lambda i: (i, 0)), _rowvec(), _rowvec(), _rowvec()],
        out_specs=[pl.BlockSpec((tr, D), lambda i: (i, 0)), pl.BlockSpec((D, tr), lambda i: (0, i))],
        out_shape=[jax.ShapeDtypeStruct((S, D), BF16), jax.ShapeDtypeStruct((D, S), BF16)],
        sem=("parallel",), args=(x, g, scale, shift), carry=carry)
    return outs[:2], outs[2:]


def _prenorm_bwd(dh, dxn, x, g, scale, name, carry=None):
    S = x.shape[0]
    tr = _row_tile(S, 256)

    def body(dh_ref, dxn_ref, x_ref, g_ref, sc_ref, dx_ref, dsh_ref, dsc_ref, dg_ref):
        i = pl.program_id(0)

        @pl.when(i == 0)
        def _():
            dsh_ref[...] = jnp.zeros((1, D), F32)
            dsc_ref[...] = jnp.zeros((1, D), F32)
            dg_ref[...] = jnp.zeros((1, D), F32)

        xv = x_ref[...]
        dhv = dh_ref[...]
        gv = g_ref[...]
        mod = 1.0 + sc_ref[...]
        r = lax.rsqrt(jnp.mean(xv * xv, axis=-1, keepdims=True) + EPS)
        xh = xv * r
        dsh_ref[...] += jnp.sum(dhv, axis=0, keepdims=True)
        dsc_ref[...] += jnp.sum(dhv * (xh * gv), axis=0, keepdims=True)
        dg_ref[...] += jnp.sum(dhv * mod * xh, axis=0, keepdims=True)
        u = dhv * mod * gv
        dx_ref[...] = dxn_ref[...] + r * u - xv * (r * r * r) * jnp.mean(u * xv, axis=-1, keepdims=True)

    tile = pl.BlockSpec((tr, D), lambda i: (i, 0))
    outs = _call(
        body, name=name, grid=(S // tr,),
        in_specs=[tile, tile, tile, _rowvec(), _rowvec()],
        out_specs=[tile, _rowvec(), _rowvec(), _rowvec()],
        out_shape=[jax.ShapeDtypeStruct((S, D), F32)] + [jax.ShapeDtypeStruct((1, D), F32)] * 3,
        sem=("arbitrary",), args=(dh, dxn, x, g, scale), carry=carry)
    return outs[:4], outs[4:]


def _layer_tail_fwd(proj, a_in, b_in, x, w_po, w_ho, w_out, gate, g, name, target=None, carry=None):
    S = proj.shape[0]
    tr = _row_tile(S, 256)
    nsh, _, wsh = w_po.shape
    n_in = 10 + (target is not None)

    def body(*refs):
        (mgp_ref, mgh_ref, a_ref, b_ref, x_ref, wpo_ref, who_ref, wout_ref, gate_ref, g_ref) = refs[:10]
        bra_ref, brb_ref, mt_ref, y_ref, xn_ref = refs[n_in:n_in + 5]
        av = a_ref[...]
        bra = jnp.concatenate([jnp.dot(av, wpo_ref[j], preferred_element_type=F32) for j in range(nsh)], axis=1)
        brb = jnp.dot(b_ref[...], who_ref[...], preferred_element_type=F32)
        mv = _sig(mgp_ref[...].astype(F32)) * bra + _sig(mgh_ref[...].astype(F32)) * brb
        bra_ref[...] = bra.astype(BF16)
        brb_ref[...] = brb.astype(BF16)
        mt_ref[...] = mv.T.astype(BF16)
        yv = jnp.dot(mv.astype(BF16), wout_ref[...], preferred_element_type=F32)
        y_ref[...] = yv
        r = lax.rsqrt(jnp.mean(yv * yv, axis=-1, keepdims=True) + EPS)
        xn = x_ref[...] + gate_ref[...] * ((yv * r) * g_ref[...])
        if target is None:
            xn_ref[...] = xn
        else:
            t_ref, l_ref = refs[10], refs[n_in + 5]

            @pl.when(pl.program_id(0) == 0)
            def _():
                l_ref[...] = jnp.zeros((8, 128), F32)

            err = xn - t_ref[...]
            xn_ref[...] = err * (1.0 / D)
            l_ref[...] += 0.5 * jnp.sum(jnp.mean(err * err, axis=-1, keepdims=True))

    tile = pl.BlockSpec((tr, D), lambda i: (i, 0))
    whole = lambda t: pl.BlockSpec(t.shape, lambda i: (0,) * t.ndim)
    last = target is not None
    outs = _call(
        body, name=name, grid=(S // tr,),
        in_specs=[pl.BlockSpec((tr, D), lambda i: (i, MGP_BLK)), pl.BlockSpec((tr, D), lambda i: (i, MGH_BLK)),
                  pl.BlockSpec((tr, POOL_W), lambda i: (i, 0)), tile, tile, whole(w_po), whole(w_ho),
                  whole(w_out), _rowvec(), _rowvec()] + [tile] * last,
        out_specs=[tile, tile, pl.BlockSpec((D, tr), lambda i: (0, i)), tile, tile]
        + [pl.BlockSpec((8, 128), lambda i: (0, 0))] * last,
        out_shape=[jax.ShapeDtypeStruct((S, D), BF16), jax.ShapeDtypeStruct((S, D), BF16),
                   jax.ShapeDtypeStruct((D, S), BF16), jax.ShapeDtypeStruct((S, D), F32),
                   jax.ShapeDtypeStruct((S, D), F32)] + [jax.ShapeDtypeStruct((8, 128), F32)] * last,
        sem=("arbitrary",) if last else ("parallel",),
        args=(proj, proj, a_in, b_in, x, w_po, w_ho, w_out, gate, g) + ((target,) if last else ()), carry=carry)
    return outs[:5 + last], outs[5 + last:]


def _layer_head_bwd(dxn, y, proj, br_a, br_b, w_po, w_ho, w_out, gate, g, name, carry=None):
    S = y.shape[0]
    tr = _row_tile(S, 256)
    nsh, _, wsh = w_po.shape

    def body(dxn_ref, y_ref, mgp_ref, mgh_ref, bra_ref, brb_ref, wpo_ref, who_ref, wout_ref, gate_ref, g_ref,
             dy_ref, dba_ref, dbb_ref, dproj_ref, dain_ref, dbin_ref, dgate_ref, dg_ref, dmgh_s):
        i = pl.program_id(0)
        j = pl.program_id(1)

        @pl.when((i == 0) & (j == 0))
        def _():
            dgate_ref[...] = jnp.zeros((1, D), F32)
            dg_ref[...] = jnp.zeros((1, D), F32)

        @pl.when(j == 1)
        def _():
            dproj_ref[...] = dmgh_s[...]

        @pl.when(j == 0)
        def _():
            everything(dxn_ref, y_ref, mgp_ref, mgh_ref, bra_ref, brb_ref, wpo_ref, who_ref, wout_ref, gate_ref,
                       g_ref, dy_ref, dba_ref, dbb_ref, dproj_ref, dain_ref, dbin_ref, dgate_ref, dg_ref, dmgh_s)

    def everything(dxn_ref, y_ref, mgp_ref, mgh_ref, bra_ref, brb_ref, wpo_ref, who_ref, wout_ref, gate_ref, g_ref,
                   dy_ref, dba_ref, dbb_ref, dproj_ref, dain_ref, dbin_ref, dgate_ref, dg_ref, dmgh_s):
        yv = y_ref[...]
        dv = dxn_ref[...]
        gv = g_ref[...]
        gt = gate_ref[...]
        r = lax.rsqrt(jnp.mean(yv * yv, axis=-1, keepdims=True) + EPS)
        yh = yv * r
        dgate_ref[...] += jnp.sum(dv * (yh * gv), axis=0, keepdims=True)
        dg_ref[...] += jnp.sum(dv * gt * yh, axis=0, keepdims=True)
        u = dv * gt * gv
        dy = (r * u - yv * (r * r * r) * jnp.mean(u * yv, axis=-1, keepdims=True)).astype(BF16)
        dy_ref[...] = dy
        dm = _dot_nt(dy, wout_ref[...])
        sp = _sig(mgp_ref[...].astype(F32))
        sh = _sig(mgh_ref[...].astype(F32))
        dba = (dm * sp).astype(BF16)
        dbb = (dm * sh).astype(BF16)
        dba_ref[...] = dba
        dbb_ref[...] = dbb
        dproj_ref[...] = (dm * bra_ref[...].astype(F32) * sp * (1.0 - sp)).astype(BF16)
        dmgh_s[...] = (dm * brb_ref[...].astype(F32) * sh * (1.0 - sh)).astype(BF16)
        dain = _dot_nt(dba[:, 0:wsh], wpo_ref[0])
        for k in range(1, nsh):
            dain = dain + _dot_nt(dba[:, k * wsh:(k + 1) * wsh], wpo_ref[k])
        dain_ref[...] = dain
        dbin_ref[...] = _dot_nt(dbb, who_ref[...])

    tile = pl.BlockSpec((tr, D), lambda i, j: (i, 0))
    whole = lambda t: pl.BlockSpec(t.shape, lambda i, j: (0,) * t.ndim)
    vec = pl.BlockSpec((1, D), lambda i, j: (0, 0))
    ahead = lambda i, j: jnp.minimum(i + j, S // tr - 1)
    tile_in = pl.BlockSpec((tr, D), lambda i, j: (ahead(i, j), 0))
    outs = _call(
        body, name=name, grid=(S // tr, 2),
        in_specs=[tile_in, tile_in, pl.BlockSpec((tr, D), lambda i, j: (ahead(i, j), MGP_BLK)),
                  pl.BlockSpec((tr, D), lambda i, j: (ahead(i, j), MGH_BLK)), tile_in, tile_in, whole(w_po),
                  whole(w_ho), whole(w_out), vec, vec],
        out_specs=[tile, tile, tile, pl.BlockSpec((tr, D), lambda i, j: (i, MGP_BLK + j)),
                   pl.BlockSpec((tr, POOL_W), lambda i, j: (i, 0)), tile, vec, vec],
        out_shape=[jax.ShapeDtypeStruct((S, D), BF16)] * 3
        + [jax.ShapeDtypeStruct((S, IN_W), BF16), jax.ShapeDtypeStruct((S, POOL_W), F32),
           jax.ShapeDtypeStruct((S, D), F32), jax.ShapeDtypeStruct((1, D), F32), jax.ShapeDtypeStruct((1, D), F32)],
        scratch_shapes=[pltpu.VMEM((tr, D), BF16)], sem=("arbitrary", "arbitrary"),
        args=(dxn, y, proj, proj, br_a, br_b, w_po, w_ho, w_out, gate, g), carry=carry)
    return outs[:8], outs[8:]


def _pool_pieces(u, g, S):
    rowi = lax.broadcasted_iota(jnp.int32, (S, 1), 0)

    def down(z, k):
        return jnp.where(rowi >= k, pltpu.roll(z, k, axis=0), 0.0)

    s2 = u + down(u, 1)
    s4 = s2 + down(s2, 2)
    s8 = s4 + down(s4, 4)
    s16 = s8 + down(s8, 8)
    win = jnp.where(g == 0, s2, jnp.where(g == 1, s4, jnp.where(g == 2, s8, s16)))
    w = jnp.where(g == 0, 2, jnp.where(g == 1, 4, jnp.where(g == 2, 8, 16)))
    count = jnp.minimum(rowi + 1, w).astype(F32)
    return win / count - u, count, rowi


def _pool_fwd(proj, pw, pscale, name):
    S = proj.shape[0]

    def body(pv_ref, pg_ref, pw_ref, sc_ref, a_ref, at_ref):
        g = pl.program_id(0)
        pooled, _, _ = _pool_pieces(pv_ref[...].astype(F32), g, S)
        pm = jnp.dot(pooled.astype(BF16), pw_ref[...].astype(BF16), preferred_element_type=F32)
        pgv = pg_ref[...].astype(F32)
        av = pm * sc_ref[...] * (pgv * _sig(pgv))
        a_ref[...] = av.astype(BF16)
        at_ref[...] = av.T.astype(BF16)

    outs = _call(
        body, name=name, grid=(GROUPS,),
        in_specs=[pl.BlockSpec((S, 128), lambda g: (0, PV0 + g)), pl.BlockSpec((S, 128), lambda g: (0, PG0 + g)),
                  pl.BlockSpec((None, 128, 128), lambda g: (g, 0, 0)), pl.BlockSpec((1, 128), lambda g: (0, g))],
        out_specs=[pl.BlockSpec((S, 128), lambda g: (0, g)), pl.BlockSpec((128, S), lambda g: (g, 0))],
        out_shape=[jax.ShapeDtypeStruct((S, POOL_W), BF16), jax.ShapeDtypeStruct((POOL_W, S), BF16)],
        sem=("parallel",), args=(proj, proj, pw, pscale))
    return outs


def _pool_bwd(da, proj, pw, pscale, dproj, name):
    S = proj.shape[0]

    def body(da_ref, pv_ref, pg_ref, pw_ref, sc_ref, dproj_in, dproj_ref, dpw_ref, dsc_ref, dpg_s):
        @pl.when(pl.program_id(1) == 1)
        def _():
            dproj_ref[...] = dpg_s[...]

        @pl.when(pl.program_id(1) == 0)
        def _():
            group(da_ref, pv_ref, pg_ref, pw_ref, sc_ref, dproj_ref, dpg_s, dpw_ref, dsc_ref)

    def group(da_ref, pv_ref, pg_ref, pw_ref, sc_ref, dpv_ref, dpg_ref, dpw_ref, dsc_ref):
        g = pl.program_id(0)
        pooled, count, rowi = _pool_pieces(pv_ref[...].astype(F32), g, S)
        pwb = pw_ref[...].astype(BF16)
        pm = jnp.dot(pooled.astype(BF16), pwb, preferred_element_type=F32)
        scv = sc_ref[...]
        pgv = pg_ref[...].astype(F32)
        sg = _sig(pgv)
        dav = da_ref[...]
        d_ps = dav * (pgv * sg)
        dpg_ref[...] = (dav * (pm * scv) * _dsilu(pgv, sg)).astype(BF16)
        dsc_ref[...] = jnp.sum(d_ps * pm, axis=0, keepdims=True)
        d_pm = (d_ps * scv).astype(BF16)
        dpw_ref[...] = lax.dot_general(pooled.astype(BF16), d_pm, (((0,), (0,)), ((), ())),
                                       preferred_element_type=F32)
        d_pooled = lax.dot_general(d_pm, pwb, (((1,), (1,)), ((), ())), preferred_element_type=F32)
        z = d_pooled / count

        def up(v, k):
            return jnp.where(rowi < S - k, pltpu.roll(v, S - k, axis=0), 0.0)

        t2 = z + up(z, 1)
        t4 = t2 + up(t2, 2)
        t8 = t4 + up(t4, 4)
        t16 = t8 + up(t8, 8)
        adj = jnp.where(g == 0, t2, jnp.where(g == 1, t4, jnp.where(g == 2, t8, t16)))
        dpv_ref[...] = (adj - d_pooled).astype(BF16)

    col = lambda g, j: (0, g)
    ahead = lambda g, j: jnp.minimum(g + j, GROUPS - 1)
    return pl.pallas_call(
        body, name=name, grid=(GROUPS, 2),
        in_specs=[pl.BlockSpec((S, 128), lambda g, j: (0, ahead(g, j))),
                  pl.BlockSpec((S, 128), lambda g, j: (0, PV0 + ahead(g, j))),
                  pl.BlockSpec((S, 128), lambda g, j: (0, PG0 + ahead(g, j))),
                  pl.BlockSpec((None, 128, 128), lambda g, j: (ahead(g, j), 0, 0)),
                  pl.BlockSpec((1, 128), lambda g, j: (0, ahead(g, j))), ANY],
        out_specs=[pl.BlockSpec((S, 128), lambda g, j: (0, PV0 + g + (PG0 - PV0) * j)),
                   pl.BlockSpec((None, 128, 128), lambda g, j: (g, 0, 0)), pl.BlockSpec((1, 128), col)],
        out_shape=[jax.ShapeDtypeStruct(dproj.shape, dproj.dtype),
                   jax.ShapeDtypeStruct((GROUPS, 128, 128), F32), jax.ShapeDtypeStruct((1, POOL_W), F32)],
        scratch_shapes=[pltpu.VMEM((S, 128), BF16)], input_output_aliases={5: 0},
        compiler_params=_cp(("arbitrary", "arbitrary")),
    )(da, proj, proj, pw, pscale, dproj)


SCAN_SHIFTS = tuple(1 << b for b in range(CH.bit_length() - 1))


def _chunk_cumsum(z, rowi):
    for sh in SCAN_SHIFTS:
        z = z + jnp.where(rowi >= sh, pltpu.roll(z, sh, axis=0), 0.0)
    return z


def _chunk_rev_cumsum(z, rowi):
    for sh in SCAN_SHIFTS:
        z = z + jnp.where(rowi < CH - sh, pltpu.roll(z, CH - sh, axis=0), 0.0)
    return z


def _dot_nn(a, b):
    return jnp.dot(a.astype(BF16), b.astype(BF16), preferred_element_type=F32)


def _dot_nt(a, b):
    return lax.dot_general(a.astype(BF16), b.astype(BF16), (((1,), (1,)), ((), ())), preferred_element_type=F32)


def _dot_tn(a, b):
    return lax.dot_general(a.astype(BF16), b.astype(BF16), (((0,), (0,)), ((), ())), preferred_element_type=F32)


def _gates(hq, hf, lbv):
    hq, hf = hq.astype(F32), hf.astype(F32)
    sq = _sig(hq)
    sf = _sig(hf)
    f = lbv + (1.0 - lbv) * sf
    fc = jnp.maximum(f, 1e-30)
    return hq * sq, sq, sf, f, fc, jnp.log(fc)


DECAY_CAP = 60.0


def _block_ref(c_ref, i, sb):
    if i == 0:
        return jnp.zeros((1, HD), F32)
    return c_ref[sb * i - 1:sb * i, :]


def _block_decay(c_ref, sb):
    spans = [_block_ref(c_ref, i, sb) - c_ref[sb * (i + 1) - 1:sb * (i + 1), :] for i in range(CH // sb)]
    return functools.reduce(jnp.maximum, spans)


def _pair_factors(q_ref, k, c_ref, first, cap, round_bf16, sb):
    nb = CH // sb
    c = c_ref[...]
    zero = jnp.zeros((sb, HD), F32)
    q_groups, k_groups, eqs, eks = [], [], [], []
    for i in range(first, nb):
        blk = slice(sb * i, sb * (i + 1))
        r_i = _block_ref(c_ref, i, sb)
        eq = jnp.exp(jnp.minimum(c_ref[blk, :] - r_i, 0.0))
        ek = jnp.exp(jnp.minimum(r_i - c, cap))
        qi, kei = q_ref[blk, :] * eq, k * ek
        if round_bf16:
            qi, kei = qi.astype(BF16).astype(F32), kei.astype(BF16).astype(F32)
        q_groups.append(jnp.concatenate([zero] * i + [qi] + [zero] * (nb - 1 - i), axis=0))
        k_groups.append(kei)
        eqs.append(eq)
        eks.append(ek)
    return jnp.concatenate(q_groups, axis=1), jnp.concatenate(k_groups, axis=1), eqs, eks


def _pair_mask(rowi, coli, strict, sb):
    return (coli < jnp.bitwise_and(rowi, -sb)) if strict else (coli <= rowi)


def _hgrn_fwd(proj, lb, gn, name, carry=None):
    S = proj.shape[0]
    nch = S // CH
    W = NH * HD

    def body(hq_ref, hf_ref, hi_ref, hg_ref, lb_ref, gn_ref, bin_ref, bint_ref, oraw_ref, st_ref, mild_ref,
             cum_ref, q_s, k_s, c_s, v_s, o_s, state_s, qf_s, kf_s, cf_s):
        state_s[...] = jnp.zeros((NH, HD, HD), F32)
        rowi = lax.broadcasted_iota(jnp.int32, (CH, 1), 0)
        coli = lax.broadcasted_iota(jnp.int32, (1, CH), 1)
        sbi = lax.broadcasted_iota(jnp.int32, (SB, 1), 0)
        gnv = gn_ref[...]

        def gates_pass(n, worst):
            wide, narrow = worst
            rows = pl.ds(pl.multiple_of(n * CH, CH), CH)
            for hh in range(NH):
                lanes = slice(hh * HD, (hh + 1) * HD)
                q, _, _, f, _, logf = _gates(hq_ref[rows, lanes], hf_ref[rows, lanes], lb_ref[:, lanes])
                c = _chunk_cumsum(logf, rowi)
                qf_s[hh, rows, :] = q
                kf_s[hh, rows, :] = 1.0 - f
                cf_s[hh, rows, :] = c
                cum_ref[rows, lanes] = c
                c_s[hh] = c
                wide = jnp.maximum(wide, _block_decay(c_s.at[hh], SB_WIDE))
                narrow = jnp.maximum(narrow, _block_decay(c_s.at[hh], SB))
            return wide, narrow

        def between_chunks(hh, n, rows):
            lanes = slice(hh * HD, (hh + 1) * HD)
            q = qf_s[hh, rows, :]
            k = kf_s[hh, rows, :]
            c = cf_s[hh, rows, :]
            v = hi_ref[rows, lanes].astype(F32)
            q_s[hh] = q
            k_s[hh] = k
            c_s[hh] = c
            v_s[hh] = v
            st = state_s[hh]
            st_ref[hh, n] = st.astype(BF16)
            o_s[hh] = _dot_nt(q * jnp.exp(c), st)
            last = c_s[hh, CH - 1:CH, :]
            state_s[hh] = st * jnp.exp(last) + _dot_tn(v, k * jnp.exp(last - c))

        def pairs_matmul(hh, first, cap, strict, sb):
            qx, kc, _, _ = _pair_factors(q_s.at[hh], k_s[hh], c_s.at[hh], first, cap, False, sb)
            a = jnp.where(_pair_mask(rowi, coli, strict, sb), _dot_nt(qx, kc), 0.0)
            o_s[hh] += _dot_nn(a, v_s[hh])

        def within_chunk_matmul(sb):
            return lambda hh: pairs_matmul(hh, 0, DECAY_CAP, False, sb)

        def within_chunk_exact(hh):
            pairs_matmul(hh, 1, 0.0, True, SB)
            for i in range(CH // SB):
                blk = slice(SB * i, SB * (i + 1))
                qb = q_s[hh, blk, :]
                cb = c_s[hh, blk, :]
                acc = jnp.zeros((SB, HD), F32)
                for s in range(SB):
                    row = SB * i + s
                    w = jnp.exp(jnp.minimum(cb - c_s[hh, row:row + 1, :], 0.0))
                    a_col = jnp.sum(qb * k_s[hh, row:row + 1, :] * w, axis=-1, keepdims=True)
                    acc = acc + jnp.where(sbi >= s, a_col, 0.0) * v_s[hh, row:row + 1, :]
                o_s[hh, blk, :] += acc

        def norm_and_gate(hh, rows):
            lanes = slice(hh * HD, (hh + 1) * HD)
            ov = o_s[hh]
            oraw_ref[rows, lanes] = ov
            r = lax.rsqrt(jnp.mean(ov * ov, axis=-1, keepdims=True) + EPS)
            hg = hg_ref[rows, lanes].astype(F32)
            bin_ref[rows, lanes] = ((ov * r) * gnv * (hg * _sig(hg))).astype(BF16)

        def chunk_with(within_chunk):
            def chunk(n, carry):
                rows = pl.ds(pl.multiple_of(n * CH, CH), CH)
                for hh in range(NH):
                    between_chunks(hh, n, rows)
                for hh in range(NH):
                    within_chunk(hh)
                for hh in range(NH):
                    norm_and_gate(hh, rows)
                return carry
            return chunk

        none = jnp.zeros((1, HD), F32)
        wide, narrow = lax.fori_loop(0, nch, gates_pass, (none, none))
        tier = jnp.where(jnp.max(wide) <= DECAY_CAP, 2.0, jnp.where(jnp.max(narrow) <= DECAY_CAP, 1.0, 0.0))
        mild_ref[...] = jnp.broadcast_to(tier, (8, HD))

        @pl.when(tier == 2.0)
        def _():
            lax.fori_loop(0, nch, chunk_with(within_chunk_matmul(SB_WIDE)), 0, unroll=4)

        @pl.when(tier == 1.0)
        def _():
            lax.fori_loop(0, nch, chunk_with(within_chunk_matmul(SB)), 0, unroll=2)

        @pl.when(tier == 0.0)
        def _():
            lax.fori_loop(0, nch, chunk_with(within_chunk_exact), 0)

        bint_ref[...] = bin_ref[...].astype(F32).T.astype(BF16)

    col = lambda off: pl.BlockSpec((S, W), lambda h: (0, off // NH + h))
    head = pl.BlockSpec((S, W), lambda h: (0, h))
    outs = _call(
        body, name=name, grid=(HEADS // NH,),
        in_specs=[col(HQ0), col(HF0), col(HI0), col(HG0), pl.BlockSpec((1, W), lambda h: (0, h)),
                  pl.BlockSpec((1, HD), lambda h: (0, 0))],
        out_specs=[head, pl.BlockSpec((W, S), lambda h: (h, 0)), head,
                   pl.BlockSpec((NH, nch, HD, HD), lambda h: (h, 0, 0, 0)),
                   pl.BlockSpec((8, HD), lambda h: (h, 0)), head],
        out_shape=[jax.ShapeDtypeStruct((S, D), BF16), jax.ShapeDtypeStruct((D, S), BF16),
                   jax.ShapeDtypeStruct((S, D), F32), jax.ShapeDtypeStruct((HEADS, nch, HD, HD), BF16),
                   jax.ShapeDtypeStruct((8 * HEADS // NH, HD), F32), jax.ShapeDtypeStruct((S, D), F32)],
        scratch_shapes=[pltpu.VMEM((NH, CH, HD), F32)] * 5 + [pltpu.VMEM((NH, HD, HD), F32)]
        + [pltpu.VMEM((NH, S, HD), F32)] * 3,
        sem=("parallel",), args=(proj, proj, proj, proj, lb, gn), carry=carry)
    return outs[:6], outs[6:]


def _hgrn_bwd(dbin, proj, oraw, states, mild, cum, lb, gn, dproj, name, carry=None):
    S = proj.shape[0]
    nch = S // CH
    W = NH * HD
    n_in = 12

    def body(*refs):
        ins, (dproj_ref, dlb_ref, dgn_ref) = refs[:n_in - 1], refs[n_in:n_in + 3]
        scratch, later = refs[n_in + 3:-3], refs[-3:]
        seg = pl.program_id(1)

        @pl.when(seg == 0)
        def _():
            heads(*ins, dproj_ref, *later, dlb_ref, dgn_ref, *scratch)

        for s, kept in enumerate(later):
            @pl.when(seg == s + 1)
            def _(kept=kept):
                dproj_ref[...] = kept[...]

    def heads(db_ref, hq_ref, hf_ref, hi_ref, hg_ref, or_ref, st_ref, mild_ref, cum_ref, lb_ref, gn_ref,
              dq_ref, df_ref, di_ref, dg_ref, dlb_ref, dgn_ref,
              q_s, k_s, c_s, v_s, do_s, dq_s, dk_s, dv_s, dc_s, dqd_s, dkd_s, f_s, sf_s, sq_s, dl_s, dst_s,
              dlb_s, dgn_s):
        dst_s[...] = jnp.zeros((NH, HD, HD), F32)
        dlb_s[...] = jnp.zeros((1, W), F32)
        dgn_s[...] = jnp.zeros((1, HD), F32)
        rowi = lax.broadcasted_iota(jnp.int32, (CH, 1), 0)
        coli = lax.broadcasted_iota(jnp.int32, (1, CH), 1)
        sbi = lax.broadcasted_iota(jnp.int32, (SB, 1), 0)
        gnv = gn_ref[...]
        def between_chunks(hh, n, rows):
            lanes = slice(hh * HD, (hh + 1) * HD)
            lbv = lb_ref[:, lanes]
            hq = hq_ref[rows, lanes].astype(F32)
            sq = _sig(hq)
            sf = _sig(hf_ref[rows, lanes].astype(F32))
            f = lbv + (1.0 - lbv) * sf
            q = hq * sq
            k = 1.0 - f
            f_s[hh] = f
            sf_s[hh] = sf
            sq_s[hh] = sq
            v = hi_ref[rows, lanes].astype(F32)
            c = cum_ref[rows, lanes]
            ov = or_ref[rows, lanes]
            hg = hg_ref[rows, lanes].astype(F32)
            sg = _sig(hg)
            r = lax.rsqrt(jnp.mean(ov * ov, axis=-1, keepdims=True) + EPS)
            dbv = db_ref[rows, lanes]
            d_on = dbv * (hg * sg)
            dg_ref[rows, lanes] = (dbv * ((ov * r) * gnv) * _dsilu(hg, sg)).astype(BF16)
            dgn_s[...] += jnp.sum(d_on * (ov * r), axis=0, keepdims=True)
            u = d_on * gnv
            do = r * u - ov * (r * r * r) * jnp.mean(u * ov, axis=-1, keepdims=True)
            q_s[hh] = q
            k_s[hh] = k
            c_s[hh] = c
            v_s[hh] = v
            do_s[hh] = do
            st = st_ref[hh, n].astype(F32)
            dst = dst_s[hh]
            ec = jnp.exp(c)
            last = c_s[hh, CH - 1:CH, :]
            el = jnp.exp(last - c)
            elast = jnp.exp(last)
            dq = _dot_nn(do, st) * ec
            dk = _dot_nn(v, dst) * el
            dq_s[hh] = dq
            dk_s[hh] = dk
            dv_s[hh] = _dot_nt(k * el, dst)
            dc_s[hh] = q * dq - k * dk
            dl_s[hh] = (jnp.sum(k * dk, axis=0, keepdims=True)
                        + elast * jnp.sum(st * dst, axis=0, keepdims=True))
            dst_s[hh] = dst * elast + _dot_tn(do, q * ec)

        def pairs_matmul(hh, first, cap, strict, sb):
            do = do_s[hh]
            qx, kc, eqs, eks = _pair_factors(q_s.at[hh], k_s[hh], c_s.at[hh], first, cap, True, sb)
            mask = _pair_mask(rowi, coli, strict, sb)
            a = jnp.where(mask, _dot_nt(qx, kc), 0.0)
            d_a = jnp.where(mask, _dot_nt(do, v_s[hh]).astype(BF16).astype(F32), 0.0)
            dqx = _dot_nn(d_a, kc)
            dkc = _dot_tn(d_a, qx)
            dv_s[hh] += _dot_tn(a, do)
            dk, dcum = dk_s[hh], dc_s[hh]
            dq_slabs = [jnp.zeros((sb, HD), F32)] * first
            dc_slabs = [jnp.zeros((sb, HD), F32)] * first
            for g, (eq, ek) in enumerate(zip(eqs, eks)):
                rows = slice(sb * (first + g), sb * (first + g + 1))
                cols = slice(HD * g, HD * (g + 1))
                dq_i = dqx[rows, cols]
                dk_i = dkc[:, cols]
                dq_slabs.append(dq_i * eq)
                dc_slabs.append(qx[rows, cols] * dq_i)
                dk = dk + dk_i * ek
                dcum = dcum - kc[:, cols] * dk_i
            dq_s[hh] += jnp.concatenate(dq_slabs, axis=0)
            dk_s[hh] = dk
            dc_s[hh] = dcum + jnp.concatenate(dc_slabs, axis=0)

        def pairs_exact(hh):
            dqd_s[hh] = jnp.zeros((CH, HD), F32)
            dkd_s[hh] = jnp.zeros((CH, HD), F32)
            for i in range(CH // SB):
                blk = slice(SB * i, SB * (i + 1))
                qb = q_s[hh, blk, :]
                cb = c_s[hh, blk, :]
                dob = do_s[hh, blk, :]
                dq_acc = jnp.zeros((SB, HD), F32)
                for s in range(SB):
                    row = SB * i + s
                    ks = k_s[hh, row:row + 1, :]
                    vs = v_s[hh, row:row + 1, :]
                    w = jnp.exp(jnp.minimum(cb - c_s[hh, row:row + 1, :], 0.0))
                    live = sbi >= s
                    a_col = jnp.where(live, jnp.sum(qb * ks * w, axis=-1, keepdims=True), 0.0)
                    da_col = jnp.where(live, jnp.sum(dob * vs, axis=-1, keepdims=True), 0.0)
                    dq_acc = dq_acc + da_col * ks * w
                    dkd_s[hh, row:row + 1, :] += jnp.sum(da_col * qb * w, axis=0, keepdims=True)
                    dv_s[hh, row:row + 1, :] += jnp.sum(a_col * dob, axis=0, keepdims=True)
                dqd_s[hh, blk, :] += dq_acc
            dq_d = dqd_s[hh]
            dk_d = dkd_s[hh]
            dq_s[hh] += dq_d
            dk_s[hh] += dk_d
            dc_s[hh] += q_s[hh] * dq_d - k_s[hh] * dk_d

        def gate_grads(hh, rows):
            lanes = slice(hh * HD, (hh + 1) * HD)
            lbv = lb_ref[:, lanes]
            hq = hq_ref[rows, lanes].astype(F32)
            f, sf, sq = f_s[hh], sf_s[hh], sq_s[hh]
            dlogf = _chunk_rev_cumsum(dc_s[hh], rowi) + dl_s[hh]
            dfv = jnp.where(f > 1e-30, dlogf / jnp.maximum(f, 1e-30), 0.0) - dk_s[hh]
            dlb_s[:, lanes] += jnp.sum(dfv * (1.0 - sf), axis=0, keepdims=True)
            df_ref[rows, lanes] = (dfv * (1.0 - lbv) * sf * (1.0 - sf)).astype(BF16)
            dq_ref[rows, lanes] = (dq_s[hh] * _dsilu(hq, sq)).astype(BF16)
            di_ref[rows, lanes] = dv_s[hh].astype(BF16)

        def chunk_with(pairs):
            def chunk(j, carry):
                n = nch - 1 - j
                rows = pl.ds(pl.multiple_of(n * CH, CH), CH)
                for hh in range(NH):
                    between_chunks(hh, n, rows)
                for hh in range(NH):
                    pairs(hh)
                for hh in range(NH):
                    gate_grads(hh, rows)
                return carry
            return chunk

        def pairs_mild(sb):
            return lambda hh: pairs_matmul(hh, 0, DECAY_CAP, False, sb)

        def pairs_any(hh):
            pairs_matmul(hh, 1, 0.0, True, SB)
            pairs_exact(hh)

        tier = jnp.max(mild_ref[...])

        @pl.when(tier == 2.0)
        def _():
            lax.fori_loop(0, nch, chunk_with(pairs_mild(SB_WIDE)), 0, unroll=2)

        @pl.when(tier == 1.0)
        def _():
            lax.fori_loop(0, nch, chunk_with(pairs_mild(SB)), 0)

        @pl.when(tier == 0.0)
        def _():
            lax.fori_loop(0, nch, chunk_with(pairs_any), 0)

        dlb_ref[...] = dlb_s[...]
        dgn_ref[...] = jnp.broadcast_to(dgn_s[...], (8, HD))

    ahead = lambda h, s: jnp.minimum(h + jnp.minimum(s, 1), HEADS // NH - 1)
    col = lambda off: pl.BlockSpec((S, W), lambda h, s: (0, off // NH + ahead(h, s)))
    head_in = pl.BlockSpec((S, W), lambda h, s: (0, ahead(h, s)))
    vec_in = pl.BlockSpec((1, W), lambda h, s: (0, ahead(h, s)))
    vec = pl.BlockSpec((1, W), lambda h, s: (0, h))
    seg_w = (HF0 - HQ0) // NH
    outs = _call(
        body, name=name, grid=(HEADS // NH, 4),
        in_specs=[head_in, col(HQ0), col(HF0), col(HI0), col(HG0), head_in,
                  pl.BlockSpec((NH, nch, HD, HD), lambda h, s: (ahead(h, s), 0, 0, 0)),
                  pl.BlockSpec((8, HD), lambda h, s: (ahead(h, s), 0)), head_in, vec_in,
                  pl.BlockSpec((1, HD), lambda h, s: (0, 0)), ANY],
        out_specs=[pl.BlockSpec((S, W), lambda h, s: (0, HQ0 // NH + seg_w * s + h)), vec,
                   pl.BlockSpec((8, HD), lambda h, s: (h, 0))],
        out_shape=[jax.ShapeDtypeStruct(dproj.shape, dproj.dtype), jax.ShapeDtypeStruct((1, D), F32),
                   jax.ShapeDtypeStruct((8 * HEADS // NH, HD), F32)],
        scratch_shapes=[pltpu.VMEM((NH, CH, HD), F32)] * 14
        + [pltpu.VMEM((NH, 1, HD), F32), pltpu.VMEM((NH, HD, HD), F32), pltpu.VMEM((1, W), F32),
           pltpu.VMEM((1, HD), F32)] + [pltpu.VMEM((S, W), BF16)] * 3,
        sem=("arbitrary", "arbitrary"), aliases={n_in - 1: 0},
        args=(dbin, proj, proj, proj, proj, oraw, states, mild, cum, lb, gn, dproj), carry=carry)
    dproj, dlb, dgn = outs[:3]
    return (dproj, dlb, dgn.reshape(HEADS // NH, 8, HD)[:, 0, :]), outs[3:]


def _lower_bounds(l0, l1):
    m = jnp.maximum(l0, l1)
    e0 = jnp.exp(l0 - m)
    e1 = jnp.exp(l1 - m)
    tot = e0 + e1
    p0 = e0 / tot
    p1 = e1 / tot
    return jnp.clip(p0 - p0, 0.0, 1.0), jnp.clip((p0 + p1) - p0, 0.0, 1.0)


def _lb_fwd(logits):
    def body(l_ref, o_ref):
        lb0, lb1 = _lower_bounds(l_ref[0:1, :], l_ref[1:2, :])
        o_ref[0:1, :] = lb0
        o_ref[1:2, :] = lb1

    return pl.pallas_call(body, name="lb_fwd", out_shape=jax.ShapeDtypeStruct((2, D), F32))(logits)


def _lb_bwd(logits, dlb):
    def body(l_ref, d_ref, o_ref):
        _, vjp = jax.vjp(_lower_bounds, l_ref[0:1, :], l_ref[1:2, :])
        g0, g1 = vjp((d_ref[0:1, :], d_ref[1:2, :]))
        o_ref[0:1, :] = g0
        o_ref[1:2, :] = g1

    return pl.pallas_call(body, name="lb_bwd", out_shape=jax.ShapeDtypeStruct((2, D), F32))(logits, dlb)


ADA_PAD = 128


def _ada_fwd(c_pad, w_ada, b_sh):
    ns = w_ada.shape[2]

    def body(c_ref, w_ref, b_ref, o_ref):
        cv = c_ref[...]
        ca = (cv * _sig(cv)).astype(BF16)
        for l in range(2):
            res = jnp.dot(ca, w_ref[l].astype(BF16), preferred_element_type=F32)
            o_ref[:, l * ns:(l + 1) * ns] = res[0:NDEV, :] + b_ref[l:l + 1, :]

    return pl.pallas_call(body, name="ada_fwd", out_shape=jax.ShapeDtypeStruct((NDEV, 2 * ns), F32),
                          compiler_params=_cp())(c_pad, w_ada, b_sh)


def _ada_wgrad(c_pad_t, d_ada_sh):
    ns = d_ada_sh.shape[2]

    def body(c_ref, d_ref, o_ref):
        cv = c_ref[...]
        ca = (cv * _sig(cv)).astype(BF16)
        for l in range(2):
            o_ref[l] = jnp.dot(ca, d_ref[l].astype(BF16), preferred_element_type=F32)

    return pl.pallas_call(body, name="ada_wgrad", out_shape=jax.ShapeDtypeStruct((2, D, ns), F32),
                          compiler_params=_cp())(c_pad_t, d_ada_sh)


def _sum_devices(g):
    _, R, C = g.shape

    def body(g_ref, o_ref):
        acc = g_ref[0]
        for d in range(1, NDEV):
            acc = acc + g_ref[d]
        o_ref[...] = acc

    return pl.pallas_call(body, name="sum_devices", out_shape=jax.ShapeDtypeStruct((R, C), F32),
                          compiler_params=_cp())(g)


def _adamw(w, g, m, v, name, echo=False):
    R, C = w.shape
    tr = _row_tile(R, max(8, (1 << 19) // C))

    def body(w_ref, g_ref, m_ref, v_ref, d_ref, nm_ref, nv_ref, *g_out):
        d_ref[...], nm_ref[...], nv_ref[...] = _adamw_update(w_ref[...], g_ref[...], m_ref[...], v_ref[...])
        for o_ref in g_out:
            o_ref[...] = g_ref[...]

    tile = pl.BlockSpec((tr, C), lambda i: (i, 0))
    n_out = 4 if echo else 3
    return _call(body, name=name, grid=(R // tr,), in_specs=[tile] * 4, out_specs=[tile] * n_out,
                 out_shape=[jax.ShapeDtypeStruct((R, C), F32)] * n_out, sem=("parallel",), args=(w, g, m, v))


def _adamw_many(wgmv, name, steps=4):
    n = len(wgmv)

    def body(*refs):
        ins, outs = refs[:4 * n], refs[4 * n:]
        for a in range(n):
            w_ref, g_ref, m_ref, v_ref = ins[4 * a:4 * a + 4]
            d_ref, nm_ref, nv_ref, g_out = outs[4 * a:4 * a + 4]
            d_ref[...], nm_ref[...], nv_ref[...] = _adamw_update(w_ref[...], g_ref[...], m_ref[...], v_ref[...])
            g_out[...] = g_ref[...]

    def tile(t):
        return pl.BlockSpec((t.shape[0] // steps, t.shape[1]), lambda i: (i, 0))

    flat = [t for four in wgmv for t in four]
    outs = pl.pallas_call(
        body, name=name, grid=(steps,), in_specs=[tile(t) for t in flat],
        out_specs=[tile(four[0]) for four in wgmv for _ in range(4)],
        out_shape=[jax.ShapeDtypeStruct(four[0].shape, F32) for four in wgmv for _ in range(4)],
        compiler_params=_cp(("parallel",)))(*flat)
    return [outs[4 * a:4 * a + 4] for a in range(n)]


def _adamw_update(w, g, m, v):
    nm = B1 * m + (1.0 - B1) * g
    nv = B2 * v + (1.0 - B2) * (g * g)
    m_hat = nm / (1.0 - B1 ** STEP)
    v_hat = nv / (1.0 - B2 ** STEP)
    return -LR * (m_hat / (jnp.sqrt(v_hat) + AEPS) + WD * w), nm, nv


SMALL_KEYS = ("b_ada", "g_pre", "g_post", "lb_logits", "pool_w", "pool_scale", "hgrn_norm_g")


def _small_pieces(key):
    one = lambda i: slice(i, i + 1)
    if key == "b_ada":
        return [((one(l), slice(j * D, (j + 1) * D)), (one(3 * l + j), slice(0, D)))
                for l in range(2) for j in range(3)]
    if key in ("g_pre", "g_post", "lb_logits"):
        row0 = {"g_pre": 8, "g_post": 16, "lb_logits": 24}[key]
        return [((slice(0, 2), slice(0, D)), (slice(row0, row0 + 2), slice(0, D)))]
    if key == "pool_w":
        return [((l, g, pl.ds(k, 16, stride=8), slice(0, 128)),
                 (slice(32 + 64 * l + 16 * g, 48 + 64 * l + 16 * g), slice(128 * k, 128 * (k + 1))))
                for l in range(2) for g in range(GROUPS) for k in range(8)]
    width = {"pool_scale": POOL_W, "hgrn_norm_g": HD}[key]
    row = {"pool_scale": 160, "hgrn_norm_g": 168}[key]
    return [((one(l), slice(0, width)), (one(row), slice(l * width, (l + 1) * width))) for l in range(2)]


def _adamw_small(g_small, g_lb_logits, wmv):
    n = len(SMALL_KEYS)

    def body(g_ref, glb_ref, *refs):
        ins, outs = refs[:3 * n], refs[3 * n:]
        for p, key in enumerate(SMALL_KEYS):
            w_ref, m_ref, v_ref = ins[3 * p:3 * p + 3]
            for at, (rows, lanes) in _small_pieces(key):
                gv = glb_ref[at] if key == "lb_logits" else g_ref[rows, lanes]
                res = _adamw_update(w_ref[at], gv, m_ref[at], v_ref[at])
                for o_ref, val in zip(outs[4 * p:4 * p + 4], (*res, gv)):
                    o_ref[at] = val

    flat = [t for key in SMALL_KEYS for t in wmv[key]]
    outs = pl.pallas_call(body, name="adamw_small",
                          out_shape=[jax.ShapeDtypeStruct(wmv[key][0].shape, F32) for key in SMALL_KEYS
                                     for _ in range(4)],
                          compiler_params=_cp())(g_small, g_lb_logits, *flat)
    return {key: outs[4 * p:4 * p + 4] for p, key in enumerate(SMALL_KEYS)}


def _cast_to_slot(place, w, l, name):
    _, R, C = w.shape
    tr = _row_tile(R, max(8, (1 << 19) // C))

    def body(p_ref, w_ref, o_ref):
        o_ref[...] = w_ref[...].astype(BF16)

    return pl.pallas_call(
        body, name=name, out_shape=jax.ShapeDtypeStruct((NCHIP, R, C), BF16),
        grid_spec=pltpu.PrefetchScalarGridSpec(
            num_scalar_prefetch=1, grid=(R // tr,),
            in_specs=[pl.BlockSpec((None, tr, C), lambda i, p_ref: (l, i, 0))],
            out_specs=pl.BlockSpec((None, tr, C), lambda i, p_ref: (p_ref[0], i, 0))),
        compiler_params=_cp(("parallel",)),
    )(place, w)


def _cast_to_slots(place, ws, name):
    n = len(ws)

    def body(p_ref, *refs):
        for w_ref, o_ref in zip(refs[:n], refs[n:]):
            o_ref[...] = w_ref[...].astype(BF16)

    def layer(l):
        return lambda i, p_ref: (l, 0, 0)

    return pl.pallas_call(
        body, name=name, out_shape=[jax.ShapeDtypeStruct((NCHIP,) + w.shape[1:], BF16) for w, _ in ws],
        grid_spec=pltpu.PrefetchScalarGridSpec(
            num_scalar_prefetch=1, grid=(1,),
            in_specs=[pl.BlockSpec((None,) + w.shape[1:], layer(l)) for w, l in ws],
            out_specs=[pl.BlockSpec((None,) + w.shape[1:], lambda i, p_ref: (p_ref[0], 0, 0)) for w, _ in ws]),
        compiler_params=_cp(("arbitrary",)),
    )(place, *[w for w, _ in ws])


def _pair_adds(core, gs, gots, name):
    n = len(gs)

    def body(c_ref, *refs):
        for a_ref, b_ref, o_ref in zip(refs[:n], refs[n:2 * n], refs[2 * n:]):
            o_ref[...] = (a_ref[...].astype(F32) + b_ref[...].astype(F32)).astype(o_ref.dtype)

    def whole(t):
        return pl.BlockSpec(t.shape, lambda i, c_ref: (0, 0, 0))

    return pl.pallas_call(
        body, name=name, out_shape=[jax.ShapeDtypeStruct(t.shape, BF16) for t in gots],
        grid_spec=pltpu.PrefetchScalarGridSpec(
            num_scalar_prefetch=1, grid=(1,),
            in_specs=[pl.BlockSpec(t.shape, lambda i, c_ref: (0, c_ref[0], 0)) for t in gots]
            + [whole(t) for t in gots],
            out_specs=[whole(t) for t in gots]),
        compiler_params=_cp(("arbitrary",)),
    )(core, *gs, *gots)


def _pair_add(core, g, got, name):
    _, R, C = g.shape
    r2 = R // 2
    tr = _row_tile(r2, max(8, (1 << 19) // C))
    nt = r2 // tr

    def body(c_ref, a_ref, b_ref, o_ref):
        o_ref[...] = (a_ref[...].astype(F32) + b_ref[...].astype(F32)).astype(o_ref.dtype)

    return pl.pallas_call(
        body, name=name, out_shape=jax.ShapeDtypeStruct((NCHIP, r2, C), BF16),
        grid_spec=pltpu.PrefetchScalarGridSpec(
            num_scalar_prefetch=1, grid=(NCHIP, nt),
            in_specs=[pl.BlockSpec((None, tr, C), lambda j, i, c_ref: (j, c_ref[0] * nt + i, 0)),
                      pl.BlockSpec((None, tr, C), lambda j, i, c_ref: (j, i, 0))],
            out_specs=pl.BlockSpec((None, tr, C), lambda j, i, c_ref: (j, i, 0))),
        compiler_params=_cp(("parallel", "parallel")),
    )(core, g, got)


def _sum_in_chip_order(me, own_ref, r_ref):
    own = own_ref[...].astype(F32)
    acc = None
    for j in range(NCHIP):
        slot = jnp.minimum(jnp.where(j > me, j - 1, j), NCHIP - 2)
        term = jnp.where(me == j, own, r_ref[slot].astype(F32))
        acc = term if acc is None else acc + term
    return acc


def _chip_sums(place, parts, recvs, name):
    n = len(parts)

    def body(p_ref, *refs):
        for a in range(n):
            for l in range(2):
                refs[4 * n + a][l] = _sum_in_chip_order(p_ref[0], refs[2 * a + l], refs[2 * n + 2 * a + l])

    def own(t):
        return pl.BlockSpec((None,) + t.shape[1:], lambda i, p_ref: (p_ref[0], 0, 0))

    def whole(t):
        return pl.BlockSpec(t.shape, lambda i, p_ref: (0, 0, 0))

    flat_p = [t for pair in parts for t in pair]
    flat_r = [t for pair in recvs for t in pair]
    return pl.pallas_call(
        body, name=name,
        out_shape=[jax.ShapeDtypeStruct((2, 2 * p[0].shape[1], p[0].shape[2]), F32) for p in parts],
        grid_spec=pltpu.PrefetchScalarGridSpec(
            num_scalar_prefetch=1, grid=(1,),
            in_specs=[own(t) for t in flat_p] + [whole(t) for t in flat_r],
            out_specs=[pl.BlockSpec((2,) + p[0].shape[1:], lambda i, p_ref: (0, p_ref[1], 0)) for p in parts]),
        compiler_params=_cp(("arbitrary",)),
    )(place, *flat_p, *flat_r)


def _chip_sum(place, part, recv, layer, both, name):
    _, r2, C = part.shape
    tr = _row_tile(r2, max(8, (1 << 18) // C))
    nt = r2 // tr

    def body(p_ref, own_ref, r_ref, *rest):
        rest[-1][...] = _sum_in_chip_order(p_ref[0], own_ref, r_ref)

    args = (place, part, recv) if both is None else (place, part, recv, both)
    return pl.pallas_call(
        body, name=name, out_shape=jax.ShapeDtypeStruct((2, 2 * r2, C), F32),
        grid_spec=pltpu.PrefetchScalarGridSpec(
            num_scalar_prefetch=1, grid=(nt,),
            in_specs=[pl.BlockSpec((None, tr, C), lambda i, p_ref: (p_ref[0], i, 0)),
                      pl.BlockSpec((NCHIP - 1, tr, C), lambda i, p_ref: (0, i, 0))] + [ANY] * (len(args) - 3),
            out_specs=pl.BlockSpec((None, tr, C), lambda i, p_ref: (layer, p_ref[1] * nt + i, 0))),
        input_output_aliases={} if both is None else {3: 0},
        compiler_params=_cp(("parallel",)),
    )(*args)


def _place():
    x, y, c = lax.axis_index("x"), lax.axis_index("y"), lax.axis_index("c")
    chips = [(1 - x, y), (x, 1 - y), (1 - x, 1 - y)]
    return x, y, c, chips


def _gather_small(blk, name):
    m_per, n = blk.shape

    def body(x_ref, out_ref, send_sems, recv_sems, local_sem):
        x, y, c, chips = _place()
        me, sibling = (x, y, c), (x, y, 1 - c)

        def rows(px, py, pc):
            return out_ref.at[pl.ds((4 * px + 2 * py + pc) * m_per, m_per), :]

        def copy(k, block, to, src=None):
            return pltpu.make_async_remote_copy(
                src_ref=rows(*block) if src is None else src, dst_ref=rows(*block),
                send_sem=send_sems.at[k], recv_sem=recv_sems.at[k], device_id=to, device_id_type=MESH)

        mine = pltpu.make_async_copy(x_ref, rows(*me), local_sem)
        mine.start()
        first = [copy(0, me, sibling, src=x_ref)]
        first += [copy(1 + j, me, (*chip, c), src=x_ref) for j, chip in enumerate(chips)]
        for cp in first:
            cp.start()
        passed = [copy(4 + j, (*chip, c), sibling) for j, chip in enumerate(chips)]
        for j, chip in enumerate(chips):
            copy(1 + j, (*chip, c), me).wait_recv()
            passed[j].start()
        copy(0, sibling, me).wait_recv()
        for j, chip in enumerate(chips):
            copy(4 + j, (*chip, 1 - c), me).wait_recv()
        for cp in first + passed:
            cp.wait_send()
        mine.wait()

    return pl.pallas_call(
        body, name=name, out_shape=jax.ShapeDtypeStruct((NDEV * m_per, n), blk.dtype),
        in_specs=[pl.BlockSpec(memory_space=pltpu.VMEM)], out_specs=pl.BlockSpec(memory_space=pltpu.VMEM),
        scratch_shapes=[pltpu.SemaphoreType.DMA((7,)), pltpu.SemaphoreType.DMA((7,)), pltpu.SemaphoreType.DMA],
        compiler_params=_cp(),
    )(blk)


def _gather_rows_carry(blk):
    m_per, n = blk.shape

    def rows(ref, px, py, pc):
        return ref.at[pl.ds((4 * px + 2 * py + pc) * m_per, m_per), :]

    def copy(ins, outs, send_sems, recv_sems, k, block, to, own=False):
        return pltpu.make_async_remote_copy(
            src_ref=ins[0] if own else rows(outs[0], *block), dst_ref=rows(outs[0], *block),
            send_sem=send_sems.at[k], recv_sem=recv_sems.at[k], device_id=to, device_id_type=MESH)

    def mine(ins, outs, send_sems):
        x, y, c, _ = _place()
        return pltpu.make_async_copy(ins[0], rows(outs[0], x, y, c), send_sems.at[7])

    def start(ins, outs, send_sems, recv_sems):
        x, y, c, chips = _place()
        mine(ins, outs, send_sems).start()
        copy(ins, outs, send_sems, recv_sems, 0, (x, y, c), (x, y, 1 - c), own=True).start()
        for j, chip in enumerate(chips):
            copy(ins, outs, send_sems, recv_sems, 1 + j, (x, y, c), (*chip, c), own=True).start()

    def finish(ins, outs, send_sems, recv_sems):
        x, y, c, chips = _place()
        for j, chip in enumerate(chips):
            copy(ins, outs, send_sems, recv_sems, 1 + j, (*chip, c), (x, y, c)).wait_recv()
            copy(ins, outs, send_sems, recv_sems, 4 + j, (*chip, c), (x, y, 1 - c)).start()
        copy(ins, outs, send_sems, recv_sems, 0, (x, y, 1 - c), (x, y, c)).wait_recv()
        for j, chip in enumerate(chips):
            copy(ins, outs, send_sems, recv_sems, 4 + j, (*chip, 1 - c), (x, y, c)).wait_recv()
        copy(ins, outs, send_sems, recv_sems, 0, (x, y, c), (x, y, 1 - c), own=True).wait_send()
        for j, chip in enumerate(chips):
            copy(ins, outs, send_sems, recv_sems, 1 + j, (x, y, c), (*chip, c), own=True).wait_send()
            copy(ins, outs, send_sems, recv_sems, 4 + j, (*chip, c), (x, y, 1 - c)).wait_send()
        mine(ins, outs, send_sems).wait()

    return _Carry([blk], [jax.ShapeDtypeStruct((NDEV * m_per, n), blk.dtype)], {}, 8, start, finish)


def _gather_carry(shards, piece=(0, 1, 1), pass_on=True, late=None):
    n = len(shards)

    def rows(ref, half, pc):
        first, count, of = pc
        r2 = ref.shape[1] // 2
        return pl.ds(half * r2 + first * (r2 // of), count * (r2 // of))

    def over_ici(outs, send_sems, recv_sems, a, j, chip_xy, slot):
        x, y, c, _ = _place()
        blk = outs[a].at[slot, rows(outs[a], c, piece), :]
        return pltpu.make_async_remote_copy(
            src_ref=blk, dst_ref=blk, send_sem=send_sems.at[9 * a + j], recv_sem=recv_sems.at[9 * a + j],
            device_id=(*chip_xy, c), device_id_type=MESH)

    def over_d2d(outs, send_sems, recv_sems, a, j, slot, half, pc):
        x, y, c, _ = _place()
        blk = outs[a].at[slot, rows(outs[a], half, pc), :]
        k = 9 * a + (3 if pc is piece else 6) + j
        return pltpu.make_async_remote_copy(
            src_ref=blk, dst_ref=blk, send_sem=send_sems.at[k], recv_sem=recv_sems.at[k],
            device_id=(x, y, 1 - c), device_id_type=MESH)

    def start(ins, outs, send_sems, recv_sems):
        x, y, c, chips = _place()
        for a in range(n):
            for j, (cx, cy) in enumerate(chips):
                over_ici(outs, send_sems, recv_sems, a, j, (cx, cy), 2 * x + y).start()
                if late:
                    over_d2d(outs, send_sems, recv_sems, a, j, 2 * cx + cy, c, late).start()

    def finish(ins, outs, send_sems, recv_sems):
        x, y, c, chips = _place()
        passed = ([piece] if pass_on else []) + ([late] if late else [])
        for a in range(n):
            for j, (cx, cy) in enumerate(chips):
                over_ici(outs, send_sems, recv_sems, a, j, (cx, cy), 2 * cx + cy).wait_recv()
                if pass_on:
                    over_d2d(outs, send_sems, recv_sems, a, j, 2 * cx + cy, c, piece).start()
        for a in range(n):
            for j, (cx, cy) in enumerate(chips):
                for pc in passed:
                    over_d2d(outs, send_sems, recv_sems, a, j, 2 * cx + cy, 1 - c, pc).wait_recv()
        for a in range(n):
            for j, (cx, cy) in enumerate(chips):
                over_ici(outs, send_sems, recv_sems, a, j, (cx, cy), 2 * x + y).wait_send()
                for pc in passed:
                    over_d2d(outs, send_sems, recv_sems, a, j, 2 * cx + cy, c, pc).wait_send()

    return _Carry(shards, [jax.ShapeDtypeStruct(s.shape, s.dtype) for s in shards],
                  {a: a for a in range(n)}, 9 * n, start, finish)


def _rs_pair(grads, name):
    n = len(grads)

    def body(*refs):
        ins, gots = refs[:n], refs[n:2 * n]
        send_sems, recv_sems = refs[2 * n:]
        x, y, c, _ = _place()
        cps = []
        for a in range(n):
            r2 = ins[a].shape[1] // 2
            cp = pltpu.make_async_remote_copy(
                src_ref=ins[a].at[:, pl.ds((1 - c) * r2, r2), :], dst_ref=gots[a],
                send_sem=send_sems.at[a], recv_sem=recv_sems.at[a],
                device_id=(x, y, 1 - c), device_id_type=MESH)
            cp.start()
            cps.append(cp)
        for cp in cps:
            cp.wait()

    half = [jax.ShapeDtypeStruct((NCHIP, g.shape[1] // 2, g.shape[2]), g.dtype) for g in grads]
    return pl.pallas_call(
        body, name=name, out_shape=half, in_specs=[ANY] * n, out_specs=[ANY] * n,
        scratch_shapes=[pltpu.SemaphoreType.DMA((n,)), pltpu.SemaphoreType.DMA((n,))],
        compiler_params=_cp(),
    )(*grads)


def _pair_sum(core, g, name):
    _, R, C = g.shape
    r2 = R // 2

    def body(c_ref, g_ref, own_ref, o_ref, got_ref, buf, send_sems, recv_sems, local_sems):
        j = pl.program_id(0)
        x, y, c, _ = _place()

        def remote(k):
            return pltpu.make_async_remote_copy(
                src_ref=g_ref.at[k, pl.ds((1 - c) * r2, r2), :], dst_ref=got_ref.at[k],
                send_sem=send_sems.at[k], recv_sem=recv_sems.at[k], device_id=(x, y, 1 - c), device_id_type=MESH)

        def fetch(k):
            return pltpu.make_async_copy(got_ref.at[k], buf.at[k % 2], local_sems.at[k % 2])

        @pl.when(j == 0)
        def _():
            for k in range(NCHIP):
                remote(k).start()
            remote(0).wait_recv()
            fetch(0).start()

        fetch(j).wait()

        @pl.when(j + 1 < NCHIP)
        def _():
            remote(j + 1).wait_recv()
            fetch(j + 1).start()

        o_ref[...] = (own_ref[...].astype(F32) + buf[j % 2].astype(F32)).astype(o_ref.dtype)

        @pl.when(j == NCHIP - 1)
        def _():
            for k in range(NCHIP):
                remote(k).wait_send()

    half = jax.ShapeDtypeStruct((NCHIP, r2, C), g.dtype)
    return pl.pallas_call(
        body, name=name, out_shape=[half, half],
        grid_spec=pltpu.PrefetchScalarGridSpec(
            num_scalar_prefetch=1, grid=(NCHIP,),
            in_specs=[ANY, pl.BlockSpec((None, r2, C), lambda j, c_ref: (j, c_ref[0], 0))],
            out_specs=[pl.BlockSpec((None, r2, C), lambda j, c_ref: (j, 0, 0)), ANY],
            scratch_shapes=[pltpu.VMEM((2, r2, C), g.dtype), pltpu.SemaphoreType.DMA((NCHIP,)),
                            pltpu.SemaphoreType.DMA((NCHIP,)), pltpu.SemaphoreType.DMA((2,))]),
        compiler_params=_cp(("arbitrary",)),
    )(core, g, g)[0]


def _pair_carry(grads):
    n = len(grads)

    def copy(ins, outs, send_sems, recv_sems, a):
        x, y, c, _ = _place()
        r2 = ins[a].shape[1] // 2
        return pltpu.make_async_remote_copy(
            src_ref=ins[a].at[:, pl.ds((1 - c) * r2, r2), :], dst_ref=outs[a],
            send_sem=send_sems.at[a], recv_sem=recv_sems.at[a],
            device_id=(x, y, 1 - c), device_id_type=MESH)

    def start(ins, outs, send_sems, recv_sems):
        for a in range(n):
            copy(ins, outs, send_sems, recv_sems, a).start()

    def finish(ins, outs, send_sems, recv_sems):
        for a in range(n):
            copy(ins, outs, send_sems, recv_sems, a).wait()

    half = [jax.ShapeDtypeStruct((NCHIP, g.shape[1] // 2, g.shape[2]), g.dtype) for g in grads]
    return _Carry(grads, half, {}, n, start, finish)


def _chips_carry(parts, piece=(0, 1, 1), into=None):
    n = len(parts)
    first, count, of = piece

    def rows(ref):
        step = ref.shape[1] // of
        return pl.ds(first * step, count * step)

    def send(ins, outs, send_sems, recv_sems, a, j, chip_xy):
        x, y, c, _ = _place()
        me, them = 2 * x + y, 2 * chip_xy[0] + chip_xy[1]
        return pltpu.make_async_remote_copy(
            src_ref=ins[a].at[them, rows(ins[a]), :],
            dst_ref=outs[a].at[me - (me > them).astype(jnp.int32), rows(outs[a]), :],
            send_sem=send_sems.at[3 * a + j], recv_sem=recv_sems.at[3 * a + j],
            device_id=(*chip_xy, c), device_id_type=MESH)

    def start(ins, outs, send_sems, recv_sems):
        _, _, _, chips = _place()
        for a in range(n):
            for j, chip_xy in enumerate(chips):
                send(ins, outs, send_sems, recv_sems, a, j, chip_xy).start()

    def finish(ins, outs, send_sems, recv_sems):
        x, y, c, chips = _place()
        me = 2 * x + y
        for a in range(n):
            for j, (cx, cy) in enumerate(chips):
                them = 2 * cx + cy
                blk = outs[a].at[them - (them > me).astype(jnp.int32), rows(outs[a]), :]
                pltpu.make_async_remote_copy(
                    src_ref=blk, dst_ref=blk, send_sem=send_sems.at[3 * a + j], recv_sem=recv_sems.at[3 * a + j],
                    device_id=(cx, cy, c), device_id_type=MESH).wait_recv()
        for a in range(n):
            for j, chip_xy in enumerate(chips):
                send(ins, outs, send_sems, recv_sems, a, j, chip_xy).wait_send()

    landing = [jax.ShapeDtypeStruct((NCHIP - 1,) + p.shape[1:], p.dtype) for p in parts]
    if into is None:
        return _Carry(parts, landing, {}, 3 * n, start, finish)
    return _Carry(list(parts) + list(into), landing, {n + a: a for a in range(n)}, 3 * n, start, finish)


def _swap_carry(fulls, layers):
    n = len(fulls)

    def copy(outs, send_sems, recv_sems, a, half):
        x, y, c, _ = _place()
        r2 = outs[a].shape[1] // 2
        blk = outs[a].at[pl.ds(*layers[a]), pl.ds(half * r2, r2), :]
        return pltpu.make_async_remote_copy(
            src_ref=blk, dst_ref=blk, send_sem=send_sems.at[a], recv_sem=recv_sems.at[a],
            device_id=(x, y, 1 - c), device_id_type=MESH)

    def start(ins, outs, send_sems, recv_sems):
        c = lax.axis_index("c")
        for a in range(n):
            copy(outs, send_sems, recv_sems, a, c).start()

    def finish(ins, outs, send_sems, recv_sems):
        c = lax.axis_index("c")
        for a in range(n):
            copy(outs, send_sems, recv_sems, a, 1 - c).wait_recv()
        for a in range(n):
            copy(outs, send_sems, recv_sems, a, c).wait_send()

    return _Carry(fulls, [jax.ShapeDtypeStruct(f.shape, f.dtype) for f in fulls], {a: a for a in range(n)}, n,
                  start, finish)


def _rs_swap(fulls, layers):
    n = len(fulls)
    swap = _swap_carry(fulls, layers)

    def body(*refs):
        outs, (send_sems, recv_sems) = refs[n:2 * n], refs[2 * n:]
        swap.start(None, outs, send_sems, recv_sems)
        swap.finish(None, outs, send_sems, recv_sems)

    return pl.pallas_call(
        body, name="rs_swap", out_shape=swap.outs, in_specs=[ANY] * n, out_specs=[ANY] * n,
        input_output_aliases=swap.aliases,
        scratch_shapes=[pltpu.SemaphoreType.DMA((n,)), pltpu.SemaphoreType.DMA((n,))],
        compiler_params=_cp(),
    )(*fulls)


def _tail_weight_grads(merged_t, b_in_t, a_in_t, dy, dbr_b, dbr_a, name, tn=256):
    S = dy.shape[0]
    nn = D // tn

    def body(mt_ref, bt_ref, at_ref, dy_ref, db_ref, da_ref, go_ref, gh_ref, gp_ref):
        go_ref[...] = jnp.dot(mt_ref[...], dy_ref[...], preferred_element_type=F32).astype(BF16)
        gh_ref[...] = jnp.dot(bt_ref[...], db_ref[...], preferred_element_type=F32).astype(BF16)
        gp_ref[...] = jnp.dot(at_ref[...], da_ref[...], preferred_element_type=F32).astype(BF16)

    left = lambda rows: pl.BlockSpec((rows, S), lambda n: (0, 0))
    right = pl.BlockSpec((S, tn), lambda n: (0, n))
    out = pl.BlockSpec((D, tn), lambda n: (0, n))
    return pl.pallas_call(
        body, name=name, grid=(nn,), in_specs=[left(D), left(D), left(POOL_W), right, right, right],
        out_specs=[out, out, pl.BlockSpec((None, POOL_W, tn), lambda n: (n, 0, 0))],
        out_shape=[jax.ShapeDtypeStruct((D, D), BF16), jax.ShapeDtypeStruct((D, D), BF16),
                   jax.ShapeDtypeStruct((NCHIP, POOL_W, D // NCHIP), BF16)],
        compiler_params=_cp(("parallel",)),
    )(merged_t, b_in_t, a_in_t, dy, dbr_b, dbr_a)


class _GatherInProj:
    def __init__(self, slot, order):
        self.slot, self.order = slot, order


def _proj_with_gather(h, w_slot, order, name, tn=256):
    S, K = h.shape
    nsh, _, ns = w_slot.shape
    tps = ns // tn
    nt = nsh * tps
    r2 = K // 2

    def body(ord_ref, h_ref, w_in_ref, o_ref, w_ref, wbuf, tile_sems, send_sems, recv_sems):
        n = pl.program_id(0)
        x, y, c, chips = _place()

        def half(slot, which):
            return w_ref.at[slot, pl.ds(which * r2, r2), :]

        def over_ici(j, slot):
            blk = half(slot, c)
            return pltpu.make_async_remote_copy(src_ref=blk, dst_ref=blk, send_sem=send_sems.at[j],
                                                recv_sem=recv_sems.at[j], device_id=(*chips[j], c),
                                                device_id_type=MESH)

        def over_d2d(j, which):
            blk = half(2 * chips[j][0] + chips[j][1], which)
            return pltpu.make_async_remote_copy(src_ref=blk, dst_ref=blk, send_sem=send_sems.at[3 + j],
                                                recv_sem=recv_sems.at[3 + j], device_id=(x, y, 1 - c),
                                                device_id_type=MESH)

        def tile_copy(step, slot):
            shard = ord_ref[step // tps]
            return pltpu.make_async_copy(w_ref.at[shard, :, pl.ds((step % tps) * tn, tn)], wbuf.at[slot],
                                         tile_sems.at[slot])

        @pl.when(n == 0)
        def _():
            for j in range(3):
                over_ici(j, 2 * x + y).start()
            tile_copy(0, 0).start()

        for j in range(3):
            @pl.when(n == (j + 1) * tps - 1)
            def _(j=j):
                over_ici(j, 2 * chips[j][0] + chips[j][1]).wait_recv()
                over_d2d(j, c).start()
                over_d2d(j, 1 - c).wait_recv()

        @pl.when(n + 1 < nt)
        def _():
            tile_copy(n + 1, (n + 1) % 2).start()

        tile_copy(n, n % 2).wait()
        o_ref[...] = jnp.dot(h_ref[...], wbuf[n % 2], preferred_element_type=F32).astype(o_ref.dtype)

        @pl.when(n == nt - 1)
        def _():
            for j in range(3):
                over_ici(j, 2 * x + y).wait_send()
                over_d2d(j, c).wait_send()

    return pl.pallas_call(
        body, name=name,
        out_shape=[jax.ShapeDtypeStruct((S, nsh * ns), BF16), jax.ShapeDtypeStruct(w_slot.shape, w_slot.dtype)],
        grid_spec=pltpu.PrefetchScalarGridSpec(
            num_scalar_prefetch=1, grid=(nt,),
            in_specs=[pl.BlockSpec((S, K), lambda n, o_ref: (0, 0)), ANY],
            out_specs=[pl.BlockSpec((S, tn), lambda n, o_ref: (0, o_ref[n // tps] * tps + n % tps)), ANY],
            scratch_shapes=[pltpu.VMEM((2, K, tn), w_slot.dtype), pltpu.SemaphoreType.DMA((2,)),
                            pltpu.SemaphoreType.DMA((6,)), pltpu.SemaphoreType.DMA((6,))]),
        input_output_aliases={2: 1},
        compiler_params=_cp(("arbitrary",)),
    )(order, h, w_slot)


def _mm_ride(a, b, carry, **kw):
    if carry is None:
        return _mm(a, b, **kw), []
    return _mm(a, b, carry=carry, **kw)


def _layer_fwd(l, x, ada, w, small, ride, target=None):
    shift, scale, gate = ada[:, 0:D], ada[:, D:2 * D], ada[:, 2 * D:3 * D]
    carry, landed = ride("prenorm")
    (h, h_t), outs = _prenorm_fwd(x, small["g_pre"][l], scale, shift, f"prenorm_fwd{l}", carry)
    landed(outs)
    carry, landed = ride("proj")
    if isinstance(carry, _GatherInProj):
        proj, full = _proj_with_gather(h, carry.slot, carry.order, f"proj{l}")
        outs = [full]
    else:
        proj, outs = _mm_ride(h, w["w_in"][l], carry, name=f"proj{l}", b_mode="nn_sh", tm=2048, out_dtype=BF16)
    landed(outs)
    a_in, a_in_t = _pool_fwd(proj, small["pool_w"][l], small["pool_scale"][l], f"pool_fwd{l}")
    carry, landed = ride("hgrn")
    (b_in, b_in_t, o_raw, states, mild, cum), outs = _hgrn_fwd(proj, small["lb"][l], small["hgrn_norm_g"][l],
                                                              f"hgrn_fwd{l}", carry=carry)
    landed(outs)
    carry, landed = ride("tail")
    (br_a, br_b, merged_t, y, *x_new), outs = _layer_tail_fwd(
        proj, a_in, b_in, x, w["w_pool_o"][l], w["w_hgrn_o"][l].reshape(D, D), w["w_out"][l].reshape(D, D),
        gate, small["g_post"][l], f"tail_fwd{l}", target=target, carry=carry)
    landed(outs)
    saved = dict(x=x, h_t=h_t, proj=proj, a_in_t=a_in_t, b_in_t=b_in_t, o_raw=o_raw, states=states, mild=mild,
                 cum=cum,
                 br_a=br_a, br_b=br_b, merged_t=merged_t, y=y, scale=scale, gate=gate)
    return x_new, saved


def _layer_bwd(l, dxn, sv, w, small, ride):
    carry, landed = ride["head"](None)
    (dy, dbr_a, dbr_b, dproj, da_in, db_in, dgate, dg_post), outs = _layer_head_bwd(
        dxn, sv["y"], sv["proj"], sv["br_a"], sv["br_b"], w["w_pool_o"][l], w["w_hgrn_o"][l].reshape(D, D),
        w["w_out"][l].reshape(D, D), sv["gate"], small["g_post"][l], f"head_bwd{l}", carry)
    landed(outs)
    gw_out, gw_hgrn_o, gw_pool_o = _tail_weight_grads(sv["merged_t"], sv["b_in_t"], sv["a_in_t"], dy, dbr_b,
                                                      dbr_a, f"gw_tail{l}")
    big = dict(w_pool_o=gw_pool_o, w_hgrn_o=gw_hgrn_o.reshape(NCHIP, D // NCHIP, D),
               w_out=gw_out.reshape(NCHIP, D // NCHIP, D))
    carry, landed = ride["hgrn"](big)
    (dproj, dlb, dgn), outs = _hgrn_bwd(db_in, sv["proj"], sv["o_raw"], sv["states"], sv["mild"], sv["cum"],
                                        small["lb"][l], small["hgrn_norm_g"][l], dproj, f"hgrn_bwd{l}",
                                        carry=carry)
    landed(outs)
    dproj, dpw, dpsc = _pool_bwd(da_in, sv["proj"], small["pool_w"][l], small["pool_scale"][l], dproj,
                                 f"pool_bwd{l}")
    little = dict(dgate=dgate, g_post=dg_post, pool_w=dpw, pool_scale=dpsc, lb=dlb,
                  hgrn_norm_g=jnp.sum(dgn, axis=0, keepdims=True))
    carry, landed = ride["gw_in"](little)
    big["w_in"], outs = _mm_ride(sv["h_t"], dproj, carry, name=f"gw_in{l}", out_shards=NCHIP, out_dtype=BF16)
    landed(outs)
    carry, landed = ride["d_h"](big)
    dh, outs = _mm_ride(dproj, w["w_in"][l], carry, name=f"d_h{l}", b_mode="nt_shk", tn=1024)
    landed(outs)
    carry, landed = ride["prenorm"](big)
    (dx, dshift, dscale, dg_pre), outs = _prenorm_bwd(dh, dxn, sv["x"], small["g_pre"][l], sv["scale"],
                                                      f"prenorm_bwd{l}", carry)
    landed(outs)
    little.update(dshift=dshift, dscale=dscale, g_pre=dg_pre)
    return dx, big, little


SMALL_ROWS = 176


def _rows8(t):
    t = t.reshape(-1, D)
    return jnp.pad(t, ((0, -t.shape[0] % 8), (0, 0)))


def _pack_small(parts):
    row_keys = ("dshift", "dscale", "dgate", "g_pre", "g_post", "lb", "pool_scale", "hgrn_norm_g")
    flat = [p[k] for p in parts for k in row_keys] + [p["pool_w"] for p in parts]
    nk = len(row_keys)

    def body(*refs):
        o_ref = refs[-1]
        o_ref[...] = jnp.zeros((SMALL_ROWS, D), F32)
        for l in range(2):
            dshift, dscale, dgate, g_pre, g_post, lb, pscale, gn = refs[l * nk:(l + 1) * nk]
            for r, ref in enumerate((dshift, dscale, dgate)):
                o_ref[3 * l + r:3 * l + r + 1, :] = ref[...]
            o_ref[8 + l:9 + l, :] = g_pre[...]
            o_ref[16 + l:17 + l, :] = g_post[...]
            o_ref[24 + l:25 + l, :] = lb[...]
            o_ref[160:161, l * POOL_W:(l + 1) * POOL_W] = pscale[...]
            o_ref[168:169, l * HD:(l + 1) * HD] = gn[...]
        for (l, *at), (rows, lanes) in _small_pieces("pool_w"):
            o_ref[rows, lanes] = refs[2 * nk + l][tuple(at)]

    return pl.pallas_call(body, name="pack_small", out_shape=jax.ShapeDtypeStruct((SMALL_ROWS, D), F32),
                          compiler_params=_cp())(*flat)


def kernel(x, c, w_ada, b_ada, g_pre, g_post, w_in, pool_w, pool_scale, lb_logits, hgrn_norm_g, w_pool_o, w_hgrn_o, w_out, loss_target, m_w_ada, m_b_ada, m_g_pre, m_g_post, m_w_in, m_pool_w, m_pool_scale, m_lb_logits, m_hgrn_norm_g, m_w_pool_o, m_w_hgrn_o, m_w_out, v_w_ada, v_b_ada, v_g_pre, v_g_post, v_w_in, v_pool_w, v_pool_scale, v_lb_logits, v_hgrn_norm_g, v_w_pool_o, v_w_hgrn_o, v_w_out):
    ax, ay, ac = lax.axis_index("x"), lax.axis_index("y"), lax.axis_index("c")
    chip = 2 * ax + ay
    dev = 2 * chip + ac
    xe, te = x[0], loss_target[0]
    ada_s = w_ada.shape[2]

    big_names = ("w_in", "w_pool_o", "w_hgrn_o", "w_out")
    big_w = (w_in, w_pool_o, w_hgrn_o, w_out)
    core = jnp.stack([ac]).astype(jnp.int32)
    place = jnp.stack([chip, ac]).astype(jnp.int32)
    slots = {("w_in", l): _cast_to_slot(place, w_in, l, f"cast_w_in{l}") for l in range(2)}
    rest = [(k, l) for l in range(2) for k in big_names[1:]]
    slots.update(zip(rest, _cast_to_slots(place, [(dict(zip(big_names, big_w))[k], l) for k, l in rest],
                                          "cast_rest")))
    w = {k: [None, None] for k in big_names}
    def fills(keys):
        def landed(outs):
            for (k, l), o in zip(keys, outs):
                w[k][l] = slots[k, l] = o
        return landed

    rest0 = [(k, 0) for k in big_names[1:]]
    rest1 = [(k, 1) for k in big_names[1:]]
    no_carry = (None, lambda outs: None)
    order = jnp.stack([chip, 2 * (1 - ax) + ay, 2 * ax + (1 - ay), 2 * (1 - ax) + (1 - ay)]).astype(jnp.int32)

    def ride_fwd0(stage):
        if stage == "proj":
            return _GatherInProj(slots["w_in", 0], order), fills([("w_in", 0)])
        if stage == "hgrn":
            return (_join_carries(_gather_carry([slots[t] for t in rest0]),
                                  _gather_carry([slots["w_in", 1]], piece=(0, 2, 4), pass_on=False)),
                    fills(rest0 + [("w_in", 1)]))
        if stage == "tail":
            return (_gather_carry([slots["w_in", 1]], piece=(2, 1, 4), pass_on=False, late=(0, 2, 4)),
                    fills([("w_in", 1)]))
        return no_carry

    def ride_fwd1(stage):
        if stage == "prenorm":
            return _gather_carry([slots["w_in", 1]], piece=(3, 1, 4), late=(2, 1, 4)), fills([("w_in", 1)])
        if stage == "hgrn":
            return _gather_carry([slots[t] for t in rest1]), fills(rest1)
        return no_carry

    c_all = _gather_small(jnp.broadcast_to(c, (8, D)), "gather_c").reshape(NDEV, 8, D)[:, 0, :]
    c_pad = jnp.pad(c_all, ((0, ADA_PAD - NDEV), (0, 0)))
    b_sh = lax.dynamic_slice(b_ada, (0, chip * ada_s), (2, ada_s))
    ada_cols = _gather_small(_ada_fwd(c_pad, w_ada, b_sh), "gather_ada")
    ada_cols = ada_cols.reshape(NCHIP, 2, NDEV, 2, ada_s)[:, 0]
    ada_all = jnp.transpose(ada_cols, (2, 1, 0, 3)).reshape(2, NDEV, 3 * D)
    ada_me = lax.dynamic_slice(ada_all, (0, dev, 0), (2, 1, 3 * D))

    lbs = _lb_fwd(lb_logits)
    small = dict(g_pre=g_pre[:, None, :], g_post=g_post[:, None, :], pool_w=pool_w,
                 pool_scale=pool_scale[:, None, :], lb=lbs[:, None, :], hgrn_norm_g=hgrn_norm_g[:, None, :])

    (x1,), sv0 = _layer_fwd(0, xe, ada_me[0], w, small, ride_fwd0)
    (dx2, loss_blk), sv1 = _layer_fwd(1, x1, ada_me[1], w, small, ride_fwd1, target=te)

    parts, recv, held = {}, {}, {}

    def pair_ride(keys, grads):
        def landed(outs):
            held.update({kl: (g, o) for kl, g, o in zip(keys, grads, outs)})
        return _pair_carry(grads), landed

    def pair_adds(keys):
        gs, gots = zip(*[held.pop(kl) for kl in keys])
        if keys[0][0] == "w_in":
            parts[keys[0]] = _pair_add(core, gs[0], gots[0], f"rs_add_w_in{keys[0][1]}")
        else:
            parts.update(zip(keys, _pair_adds(core, gs, gots, f"rs_add_early{keys[0][1]}")))

    def exchange(keys):
        def landed(outs):
            recv.update(zip(keys, outs))
        return _chips_carry([parts[kl] for kl in keys]), landed

    def share(key, first, count):
        def landed(outs):
            (recv[key],) = outs
        into = [recv[key]] if key in recv else None
        return _chips_carry([parts[key]], piece=(first, count, 8), into=into), landed

    def together(*rides):
        carries, fns = zip(*rides)

        def landed(outs):
            for cr, fn in zip(carries, fns):
                fn(outs[:len(cr.outs)])
                outs = outs[len(cr.outs):]
        return _join_carries(*carries), landed

    def early(l):
        return [(k, l) for k in big_names[1:]]

    def pair_alone(keys, grads, tag):
        held.update({kl: (g, o) for kl, g, o in zip(keys, grads, _rs_pair(grads, f"rs_pair_{tag}"))})
        pair_adds(keys)

    def ride_hgrn1(big):
        return pair_ride(early(1), [big[k] for k in big_names[1:]])

    def ride_gw_in1(_):
        pair_adds(early(1))
        return exchange(early(1))

    def ride_d_h1(big):
        return pair_ride([("w_in", 1)], [big["w_in"]])

    def ride_prenorm1(_):
        pair_adds([("w_in", 1)])
        return share(("w_in", 1), 0, 1)

    def ride_head0(_):
        return share(("w_in", 1), 1, 3)

    def ride_hgrn0(big):
        pair_alone(early(0), [big[k] for k in big_names[1:]], "early0")
        return together(exchange(early(0)), share(("w_in", 1), 4, 4))

    def ride_d_h0(big):
        parts["w_in", 0] = _pair_sum(core, big["w_in"], "rs_pair_sum_w_in0")
        return share(("w_in", 0), 0, 4)

    def ride_prenorm0(_):
        return share(("w_in", 0), 4, 4)

    no_ride = lambda so_far: no_carry
    dx1, big1, little1 = _layer_bwd(1, dx2, sv1, w, small, dict(head=no_ride, hgrn=ride_hgrn1, gw_in=ride_gw_in1,
                                                                d_h=ride_d_h1, prenorm=ride_prenorm1))

    gathered = {}
    zero_row = jnp.zeros((1, D), F32)

    def ride_gw_in0(little):
        so_far = dict(little, dshift=zero_row, dscale=zero_row, g_pre=zero_row)

        def landed(outs):
            (gathered["early"],) = outs

        red = [_chip_sum(place, parts["w_in", 1], recv["w_in", 1], 1, None, "rs_sum_w_in1")]
        red += _chip_sums(place, [[parts[k, l] for l in range(2)] for k in big_names[1:]],
                          [[recv[k, l] for l in range(2)] for k in big_names[1:]], "rs_sum_early")

        def swapped(outs):
            gathered["sums"] = outs
        return together((_gather_rows_carry(_pack_small([so_far, little1])), landed),
                        (_swap_carry(red, [(1, 1)] + [(0, 2)] * 3), swapped))

    dx0, big0, little0 = _layer_bwd(0, dx1, sv0, w, small,
                                    dict(head=ride_head0, hgrn=ride_hgrn0, gw_in=ride_gw_in0, d_h=ride_d_h0,
                                         prenorm=ride_prenorm0))
    loss_row = jnp.broadcast_to(loss_blk[0:1, 0:1], (1, D))
    late = _rows8(jnp.stack([little0["dshift"], little0["dscale"], little0["g_pre"], loss_row]))
    late = _gather_small(late, "gather_small_late").reshape(NDEV, 8, D)
    loss = jnp.sum(late[:, 3, 0])
    packed = gathered["early"].reshape(NDEV, SMALL_ROWS, D)
    packed = packed.at[:, 0:2, :].set(late[:, 0:2, :]).at[:, 8:9, :].set(late[:, 2:3, :])
    red = _chip_sum(place, parts["w_in", 0], recv["w_in", 0], 0, gathered["sums"][0], "rs_sum_w_in0")
    g_big = dict(zip(big_names, list(_rs_swap([red], [(0, 1)])) + list(gathered["sums"][1:])))

    def two(t):
        return t.reshape(-1, t.shape[-1])

    def upd(wt, g, m, v, name, echo=False):
        return [t.reshape(wt.shape) for t in _adamw(two(wt), two(g), two(m), two(v), name, echo)]

    *u_w_in, g_w_in = upd(w_in, g_big["w_in"], m_w_in, v_w_in, "adamw_w_in", echo=True)
    g_small = _sum_devices(packed)
    g_lb_logits = _lb_bwd(lb_logits, g_small[24:26])
    d_ada_all = packed[:, 0:6, :].reshape(NDEV, 2, 3 * D)
    d_ada_sh = lax.dynamic_slice(jnp.transpose(d_ada_all, (1, 0, 2)), (0, 0, chip * ada_s), (2, NDEV, ada_s))
    d_ada_sh = jnp.pad(d_ada_sh, ((0, 0), (0, ADA_PAD - NDEV), (0, 0)))
    g_w_ada = _ada_wgrad(c_pad.T, d_ada_sh)

    u_w_ada = upd(w_ada, g_w_ada, m_w_ada, v_w_ada, "adamw_w_ada")
    early_w = dict(w_pool_o=(w_pool_o, m_w_pool_o, v_w_pool_o), w_hgrn_o=(w_hgrn_o, m_w_hgrn_o, v_w_hgrn_o),
                   w_out=(w_out, m_w_out, v_w_out))
    u_early = _adamw_many([(two(early_w[k][0]), two(g_big[k]), two(early_w[k][1]), two(early_w[k][2]))
                           for k in big_names[1:]], "adamw_early")
    (*u_w_pool_o, g_w_pool_o), (*u_w_hgrn_o, g_w_hgrn_o), (*u_w_out, g_w_out) = [
        [t.reshape(early_w[k][0].shape) for t in four] for k, four in zip(big_names[1:], u_early)]
    small_w = dict(b_ada=(b_ada, m_b_ada, v_b_ada), g_pre=(g_pre, m_g_pre, v_g_pre),
                   g_post=(g_post, m_g_post, v_g_post), lb_logits=(lb_logits, m_lb_logits, v_lb_logits),
                   pool_w=(pool_w, m_pool_w, v_pool_w), pool_scale=(pool_scale, m_pool_scale, v_pool_scale),
                   hgrn_norm_g=(hgrn_norm_g, m_hgrn_norm_g, v_hgrn_norm_g))
    u_small = _adamw_small(g_small, g_lb_logits, small_w)
    s = lambda key: u_small[key][3]
    grads_out = (g_w_ada, s("b_ada"), s("g_pre"), s("g_post"), g_w_in, s("pool_w"), s("pool_scale"), g_lb_logits,
                 s("hgrn_norm_g"), g_w_pool_o, g_w_hgrn_o, g_w_out)

    def ordered(k):
        s = lambda key: u_small[key][k]
        return (u_w_ada[k], s("b_ada"), s("g_pre"), s("g_post"), u_w_in[k], s("pool_w"), s("pool_scale"),
                s("lb_logits"), s("hgrn_norm_g"), u_w_pool_o[k], u_w_hgrn_o[k], u_w_out[k])

    return (loss, dx0[None], *grads_out, *ordered(0), *ordered(1), *ordered(2))
```

```python
import functools

import jax
import jax.numpy as jnp
from jax import lax
from jax.experimental import pallas as pl
from jax.experimental.pallas import tpu as pltpu

F32 = jnp.float32
BF16 = jnp.bfloat16
MESH = pl.DeviceIdType.MESH

D = 1024
HEADS = 8
HD = 128
GROUPS = 4
POOL_W = 512
CH = 128
SB_WIDE = 32
SB = 16
NH = 2
IN_W = 7168
NCHIP = 4
NDEV = 8
EPS = 1e-6
PV0, PG0, HQ0, HF0, HI0, HG0 = 0, 4, 8, 16, 24, 32
MGP_BLK, MGH_BLK = 5, 6

LR, B1, B2, AEPS, WD, STEP = 0.001, 0.9, 0.999, 1e-08, 0.01, 10
VMEM_LIMIT = 56 * 1024 * 1024


def _cp(sem=None, **kw):
    if sem is not None:
        kw["dimension_semantics"] = sem
    return pltpu.CompilerParams(vmem_limit_bytes=VMEM_LIMIT, **kw)


def _sig(z):
    return 1.0 / (1.0 + jnp.exp(-z))


def _dsilu(z, s):
    return s * (1.0 + z * (1.0 - s))


def _row_tile(rows, cap):
    if rows <= cap:
        return rows
    t = 1 << (cap.bit_length() - 1)
    while rows % t:
        t //= 2
    return t


ANY = pl.BlockSpec(memory_space=pl.ANY)


class _Carry:
    def __init__(self, ins, outs, aliases, n_sem, start, finish):
        self.ins, self.outs, self.aliases, self.n_sem = list(ins), list(outs), dict(aliases), n_sem
        self.start, self.finish = start, finish


class _SemWindow:
    def __init__(self, ref, base):
        self._ref, self._base = ref, base

    @property
    def at(self):
        return self

    def __getitem__(self, k):
        return self._ref.at[self._base + k]


def _join_carries(*carries):
    ins, outs, aliases, spans, n_sem = [], [], {}, [], 0
    for cr in carries:
        aliases.update({len(ins) + i: len(outs) + o for i, o in cr.aliases.items()})
        spans.append((len(ins), len(cr.ins), len(outs), len(cr.outs), n_sem))
        ins, outs, n_sem = ins + cr.ins, outs + cr.outs, n_sem + cr.n_sem

    def run(which):
        def fn(i_refs, o_refs, send_sems, recv_sems):
            for cr, (i0, ni, o0, no, s0) in zip(carries, spans):
                getattr(cr, which)(i_refs[i0:i0 + ni], o_refs[o0:o0 + no], _SemWindow(send_sems, s0),
                                   _SemWindow(recv_sems, s0))
        return fn

    return _Carry(ins, outs, aliases, n_sem, run("start"), run("finish"))


def _call(body, *, name, grid, in_specs, out_specs, out_shape, args, scratch_shapes=(), sem=None, carry=None,
          aliases=None):
    in_specs, out_specs, out_shape = list(in_specs), list(out_specs), list(out_shape)
    scratch_shapes = list(scratch_shapes)
    aliases = dict(aliases or {})
    if carry is None:
        outs = pl.pallas_call(body, name=name, grid=grid, in_specs=in_specs, out_specs=out_specs,
                              out_shape=out_shape, scratch_shapes=scratch_shapes, input_output_aliases=aliases,
                              compiler_params=_cp(sem))(*args)
        return list(outs)
    n_in, n_out, n_scr = len(in_specs), len(out_specs), len(scratch_shapes)
    c_in, c_out = len(carry.ins), len(carry.outs)

    def wrapped(*refs):
        k_in, rest = refs[:n_in], refs[n_in:]
        ci, rest = rest[:c_in], rest[c_in:]
        k_out, rest = rest[:n_out], rest[n_out:]
        co, rest = rest[:c_out], rest[c_out:]
        k_scr, (ssem, rsem) = rest[:n_scr], rest[n_scr:]
        pids = [pl.program_id(d) for d in range(len(grid))]
        first = functools.reduce(jnp.logical_and, [p == 0 for p in pids])
        last = functools.reduce(jnp.logical_and, [p == g - 1 for p, g in zip(pids, grid)])

        @pl.when(first)
        def _():
            carry.start(ci, co, ssem, rsem)

        body(*k_in, *k_out, *k_scr)

        @pl.when(last)
        def _():
            carry.finish(ci, co, ssem, rsem)

    outs = pl.pallas_call(
        wrapped, name=name, grid=grid, in_specs=in_specs + [ANY] * c_in, out_specs=out_specs + [ANY] * c_out,
        out_shape=out_shape + carry.outs,
        input_output_aliases={**aliases, **{n_in + i: n_out + o for i, o in carry.aliases.items()}},
        scratch_shapes=scratch_shapes + [pltpu.SemaphoreType.DMA((carry.n_sem,))] * 2,
        compiler_params=_cp(("arbitrary",) * len(grid)),
    )(*args, *carry.ins)
    return list(outs)


def _mm(a, b, *, name, b_mode="nn", out_shards=0, tm=1024, tn=256, tk=None, out_dtype=F32, carry=None):
    assert b_mode in ("nn", "nn_sh", "nt_shk"), b_mode
    M, K = a.shape
    if b_mode == "nn":
        N = b.shape[1]
    elif b_mode == "nn_sh":
        N = b.shape[0] * b.shape[2]
    else:
        N = b.shape[1]
    tm = _row_tile(M, tm)
    if b_mode == "nn_sh":
        tn = _row_tile(b.shape[2], tn)
    elif out_shards:
        tn = _row_tile(N // out_shards, tn)
    else:
        tn = _row_tile(N, tn)
    if tk is None:
        tk = K if b_mode != "nt_shk" else b.shape[2]
    if b_mode == "nt_shk":
        tk = _row_tile(b.shape[2], tk)
    nm, nn, nk = M // tm, N // tn, K // tk

    a_spec = pl.BlockSpec((tm, tk), lambda m, n, k: (m, k))
    if b_mode == "nn":
        b_spec = pl.BlockSpec((tk, tn), lambda m, n, k: (k, n))
    elif b_mode == "nn_sh":
        nps = b.shape[2] // tn
        b_spec = pl.BlockSpec((None, tk, tn), lambda m, n, k: (n // nps, k, n % nps))
    else:
        kps = b.shape[2] // tk
        b_spec = pl.BlockSpec((None, tn, tk), lambda m, n, k: (k // kps, n, k % kps))
    if out_shards:
        ops = (N // out_shards) // tn
        o_spec = pl.BlockSpec((None, tm, tn), lambda m, n, k: (n // ops, m, n % ops))
        o_shape = jax.ShapeDtypeStruct((out_shards, M, N // out_shards), out_dtype)
    else:
        o_spec = pl.BlockSpec((tm, tn), lambda m, n, k: (m, n))
        o_shape = jax.ShapeDtypeStruct((M, N), out_dtype)
    dn = (((1,), (1,)), ((), ())) if b_mode == "nt_shk" else (((1,), (0,)), ((), ()))

    def body(a_ref, b_ref, o_ref, acc_ref):
        k = pl.program_id(2)

        @pl.when(k == 0)
        def _():
            acc_ref[...] = jnp.zeros(acc_ref.shape, F32)

        acc_ref[...] += lax.dot_general(a_ref[...].astype(BF16), b_ref[...].astype(BF16), dn,
                                        preferred_element_type=F32)

        @pl.when(k == nk - 1)
        def _():
            o_ref[...] = acc_ref[...].astype(o_ref.dtype)

    outs = _call(body, name=name, grid=(nm, nn, nk), in_specs=[a_spec, b_spec], out_specs=[o_spec],
                 out_shape=[o_shape], scratch_shapes=[pltpu.VMEM((tm, tn), F32)],
                 sem=("parallel", "parallel", "arbitrary"), args=(a, b), carry=carry)
    return outs[0] if carry is None else (outs[0], outs[1:])


def _rowvec(n=D):
    return pl.BlockSpec((1, n), lambda i: (0, 0))


def _prenorm_fwd(x, g, scale, shift, name, carry=None):
    S = x.shape[0]
    tr = _row_tile(S, 256)

    def body(x_ref, g_ref, sc_ref, sh_ref, h_ref, ht_ref):
        xv = x_ref[...]
        r = lax.rsqrt(jnp.mean(xv * xv, axis=-1, keepdims=True) + EPS)
        hv = (xv * r) * g_ref[...] * (1.0 + sc_ref[...]) + sh_ref[...]
        h_ref[...] = hv.astype(BF16)
        ht_ref[...] = hv.T.astype(BF16)

    outs = _call(
        body, name=name, grid=(S // tr,),
        in_specs=[pl.BlockSpec((tr, D), lambda i: (i, 0)), _rowvec(), _rowvec(), _rowvec()],
        out_specs=[pl.BlockSpec((tr, D), lambda i: (i, 0)), pl.BlockSpec((D, tr), lambda i: (0, i))],
        out_shape=[jax.ShapeDtypeStruct((S, D), BF16), jax.ShapeDtypeStruct((D, S), BF16)],
        sem=("parallel",), args=(x, g, scale, shift), carry=carry)
    return outs[:2], outs[2:]


def _prenorm_bwd(dh, dxn, x, g, scale, name, carry=None):
    S = x.shape[0]
    tr = _row_tile(S, 256)

    def body(dh_ref, dxn_ref, x_ref, g_ref, sc_ref, dx_ref, dsh_ref, dsc_ref, dg_ref):
        i = pl.program_id(0)

        @pl.when(i == 0)
        def _():
            dsh_ref[...] = jnp.zeros((1, D), F32)
            dsc_ref[...] = jnp.zeros((1, D), F32)
            dg_ref[...] = jnp.zeros((1, D), F32)

        xv = x_ref[...]
        dhv = dh_ref[...]
        gv = g_ref[...]
        mod = 1.0 + sc_ref[...]
        r = lax.rsqrt(jnp.mean(xv * xv, axis=-1, keepdims=True) + EPS)
        xh = xv * r
        dsh_ref[...] += jnp.sum(dhv, axis=0, keepdims=True)
        dsc_ref[...] += jnp.sum(dhv * (xh * gv), axis=0, keepdims=True)
        dg_ref[...] += jnp.sum(dhv * mod * xh, axis=0, keepdims=True)
        u = dhv * mod * gv
        dx_ref[...] = dxn_ref[...] + r * u - xv * (r * r * r) * jnp.mean(u * xv, axis=-1, keepdims=True)

    tile = pl.BlockSpec((tr, D), lambda i: (i, 0))
    outs = _call(
        body, name=name, grid=(S // tr,),
        in_specs=[tile, tile, tile, _rowvec(), _rowvec()],
        out_specs=[tile, _rowvec(), _rowvec(), _rowvec()],
        out_shape=[jax.ShapeDtypeStruct((S, D), F32)] + [jax.ShapeDtypeStruct((1, D), F32)] * 3,
        sem=("arbitrary",), args=(dh, dxn, x, g, scale), carry=carry)
    return outs[:4], outs[4:]


def _layer_tail_fwd(proj, a_in, b_in, x, w_po, w_ho, w_out, gate, g, name, target=None, carry=None):
    S = proj.shape[0]
    tr = _row_tile(S, 256)
    nsh, _, wsh = w_po.shape
    n_in = 10 + (target is not None)

    def body(*refs):
        (mgp_ref, mgh_ref, a_ref, b_ref, x_ref, wpo_ref, who_ref, wout_ref, gate_ref, g_ref) = refs[:10]
        bra_ref, brb_ref, mt_ref, y_ref, xn_ref = refs[n_in:n_in + 5]
        av = a_ref[...]
        bra = jnp.concatenate([jnp.dot(av, wpo_ref[j], preferred_element_type=F32) for j in range(nsh)], axis=1)
        brb = jnp.dot(b_ref[...], who_ref[...], preferred_element_type=F32)
        mv = _sig(mgp_ref[...].astype(F32)) * bra + _sig(mgh_ref[...].astype(F32)) * brb
        bra_ref[...] = bra.astype(BF16)
        brb_ref[...] = brb.astype(BF16)
        mt_ref[...] = mv.T.astype(BF16)
        yv = jnp.dot(mv.astype(BF16), wout_ref[...], preferred_element_type=F32)
        y_ref[...] = yv
        r = lax.rsqrt(jnp.mean(yv * yv, axis=-1, keepdims=True) + EPS)
        xn = x_ref[...] + gate_ref[...] * ((yv * r) * g_ref[...])
        if target is None:
            xn_ref[...] = xn
        else:
            t_ref, l_ref = refs[10], refs[n_in + 5]

            @pl.when(pl.program_id(0) == 0)
            def _():
                l_ref[...] = jnp.zeros((8, 128), F32)

            err = xn - t_ref[...]
            xn_ref[...] = err * (1.0 / D)
            l_ref[...] += 0.5 * jnp.sum(jnp.mean(err * err, axis=-1, keepdims=True))

    tile = pl.BlockSpec((tr, D), lambda i: (i, 0))
    whole = lambda t: pl.BlockSpec(t.shape, lambda i: (0,) * t.ndim)
    last = target is not None
    outs = _call(
        body, name=name, grid=(S // tr,),
        in_specs=[pl.BlockSpec((tr, D), lambda i: (i, MGP_BLK)), pl.BlockSpec((tr, D), lambda i: (i, MGH_BLK)),
                  pl.BlockSpec((tr, POOL_W), lambda i: (i, 0)), tile, tile, whole(w_po), whole(w_ho),
                  whole(w_out), _rowvec(), _rowvec()] + [tile] * last,
        out_specs=[tile, tile, pl.BlockSpec((D, tr), lambda i: (0, i)), tile, tile]
        + [pl.BlockSpec((8, 128), lambda i: (0, 0))] * last,
        out_shape=[jax.ShapeDtypeStruct((S, D), BF16), jax.ShapeDtypeStruct((S, D), BF16),
                   jax.ShapeDtypeStruct((D, S), BF16), jax.ShapeDtypeStruct((S, D), F32),
                   jax.ShapeDtypeStruct((S, D), F32)] + [jax.ShapeDtypeStruct((8, 128), F32)] * last,
        sem=("arbitrary",) if last else ("parallel",),
        args=(proj, proj, a_in, b_in, x, w_po, w_ho, w_out, gate, g) + ((target,) if last else ()), carry=carry)
    return outs[:5 + last], outs[5 + last:]


def _layer_head_bwd(dxn, y, proj, br_a, br_b, w_po, w_ho, w_out, gate, g, name, carry=None):
    S = y.shape[0]
    tr = _row_tile(S, 256)
    nsh, _, wsh = w_po.shape

    def body(dxn_ref, y_ref, mgp_ref, mgh_ref, bra_ref, brb_ref, wpo_ref, who_ref, wout_ref, gate_ref, g_ref,
             dy_ref, dba_ref, dbb_ref, dproj_ref, dain_ref, dbin_ref, dgate_ref, dg_ref, dmgh_s):
        i = pl.program_id(0)
        j = pl.program_id(1)

        @pl.when((i == 0) & (j == 0))
        def _():
            dgate_ref[...] = jnp.zeros((1, D), F32)
            dg_ref[...] = jnp.zeros((1, D), F32)

        @pl.when(j == 1)
        def _():
            dproj_ref[...] = dmgh_s[...]

        @pl.when(j == 0)
        def _():
            everything(dxn_ref, y_ref, mgp_ref, mgh_ref, bra_ref, brb_ref, wpo_ref, who_ref, wout_ref, gate_ref,
                       g_ref, dy_ref, dba_ref, dbb_ref, dproj_ref, dain_ref, dbin_ref, dgate_ref, dg_ref, dmgh_s)

    def everything(dxn_ref, y_ref, mgp_ref, mgh_ref, bra_ref, brb_ref, wpo_ref, who_ref, wout_ref, gate_ref, g_ref,
                   dy_ref, dba_ref, dbb_ref, dproj_ref, dain_ref, dbin_ref, dgate_ref, dg_ref, dmgh_s):
        yv = y_ref[...]
        dv = dxn_ref[...]
        gv = g_ref[...]
        gt = gate_ref[...]
        r = lax.rsqrt(jnp.mean(yv * yv, axis=-1, keepdims=True) + EPS)
        yh = yv * r
        dgate_ref[...] += jnp.sum(dv * (yh * gv), axis=0, keepdims=True)
        dg_ref[...] += jnp.sum(dv * gt * yh, axis=0, keepdims=True)
        u = dv * gt * gv
        dy = (r * u - yv * (r * r * r) * jnp.mean(u * yv, axis=-1, keepdims=True)).astype(BF16)
        dy_ref[...] = dy
        dm = _dot_nt(dy, wout_ref[...])
        sp = _sig(mgp_ref[...].astype(F32))
        sh = _sig(mgh_ref[...].astype(F32))
        dba = (dm * sp).astype(BF16)
        dbb = (dm * sh).astype(BF16)
        dba_ref[...] = dba
        dbb_ref[...] = dbb
        dproj_ref[...] = (dm * bra_ref[...].astype(F32) * sp * (1.0 - sp)).astype(BF16)
        dmgh_s[...] = (dm * brb_ref[...].astype(F32) * sh * (1.0 - sh)).astype(BF16)
        dain = _dot_nt(dba[:, 0:wsh], wpo_ref[0])
        for k in range(1, nsh):
            dain = dain + _dot_nt(dba[:, k * wsh:(k + 1) * wsh], wpo_ref[k])
        dain_ref[...] = dain
        dbin_ref[...] = _dot_nt(dbb, who_ref[...])

    tile = pl.BlockSpec((tr, D), lambda i, j: (i, 0))
    whole = lambda t: pl.BlockSpec(t.shape, lambda i, j: (0,) * t.ndim)
    vec = pl.BlockSpec((1, D), lambda i, j: (0, 0))
    ahead = lambda i, j: jnp.minimum(i + j, S // tr - 1)
    tile_in = pl.BlockSpec((tr, D), lambda i, j: (ahead(i, j), 0))
    outs = _call(
        body, name=name, grid=(S // tr, 2),
        in_specs=[tile_in, tile_in, pl.BlockSpec((tr, D), lambda i, j: (ahead(i, j), MGP_BLK)),
                  pl.BlockSpec((tr, D), lambda i, j: (ahead(i, j), MGH_BLK)), tile_in, tile_in, whole(w_po),
                  whole(w_ho), whole(w_out), vec, vec],
        out_specs=[tile, tile, tile, pl.BlockSpec((tr, D), lambda i, j: (i, MGP_BLK + j)),
                   pl.BlockSpec((tr, POOL_W), lambda i, j: (i, 0)), tile, vec, vec],
        out_shape=[jax.ShapeDtypeStruct((S, D), BF16)] * 3
        + [jax.ShapeDtypeStruct((S, IN_W), BF16), jax.ShapeDtypeStruct((S, POOL_W), F32),
           jax.ShapeDtypeStruct((S, D), F32), jax.ShapeDtypeStruct((1, D), F32), jax.ShapeDtypeStruct((1, D), F32)],
        scratch_shapes=[pltpu.VMEM((tr, D), BF16)], sem=("arbitrary", "arbitrary"),
        args=(dxn, y, proj, proj, br_a, br_b, w_po, w_ho, w_out, gate, g), carry=carry)
    return outs[:8], outs[8:]


def _pool_pieces(u, g, S):
    rowi = lax.broadcasted_iota(jnp.int32, (S, 1), 0)

    def down(z, k):
        return jnp.where(rowi >= k, pltpu.roll(z, k, axis=0), 0.0)

    s2 = u + down(u, 1)
    s4 = s2 + down(s2, 2)
    s8 = s4 + down(s4, 4)
    s16 = s8 + down(s8, 8)
    win = jnp.where(g == 0, s2, jnp.where(g == 1, s4, jnp.where(g == 2, s8, s16)))
    w = jnp.where(g == 0, 2, jnp.where(g == 1, 4, jnp.where(g == 2, 8, 16)))
    count = jnp.minimum(rowi + 1, w).astype(F32)
    return win / count - u, count, rowi


def _pool_fwd(proj, pw, pscale, name):
    S = proj.shape[0]

    def body(pv_ref, pg_ref, pw_ref, sc_ref, a_ref, at_ref):
        g = pl.program_id(0)
        pooled, _, _ = _pool_pieces(pv_ref[...].astype(F32), g, S)
        pm = jnp.dot(pooled.astype(BF16), pw_ref[...].astype(BF16), preferred_element_type=F32)
        pgv = pg_ref[...].astype(F32)
        av = pm * sc_ref[...] * (pgv * _sig(pgv))
        a_ref[...] = av.astype(BF16)
        at_ref[...] = av.T.astype(BF16)

    outs = _call(
        body, name=name, grid=(GROUPS,),
        in_specs=[pl.BlockSpec((S, 128), lambda g: (0, PV0 + g)), pl.BlockSpec((S, 128), lambda g: (0, PG0 + g)),
                  pl.BlockSpec((None, 128, 128), lambda g: (g, 0, 0)), pl.BlockSpec((1, 128), lambda g: (0, g))],
        out_specs=[pl.BlockSpec((S, 128), lambda g: (0, g)), pl.BlockSpec((128, S), lambda g: (g, 0))],
        out_shape=[jax.ShapeDtypeStruct((S, POOL_W), BF16), jax.ShapeDtypeStruct((POOL_W, S), BF16)],
        sem=("parallel",), args=(proj, proj, pw, pscale))
    return outs


def _pool_bwd(da, proj, pw, pscale, dproj, name):
    S = proj.shape[0]

    def body(da_ref, pv_ref, pg_ref, pw_ref, sc_ref, dproj_in, dproj_ref, dpw_ref, dsc_ref, dpg_s):
        @pl.when(pl.program_id(1) == 1)
        def _():
            dproj_ref[...] = dpg_s[...]

        @pl.when(pl.program_id(1) == 0)
        def _():
            group(da_ref, pv_ref, pg_ref, pw_ref, sc_ref, dproj_ref, dpg_s, dpw_ref, dsc_ref)

    def group(da_ref, pv_ref, pg_ref, pw_ref, sc_ref, dpv_ref, dpg_ref, dpw_ref, dsc_ref):
        g = pl.program_id(0)
        pooled, count, rowi = _pool_pieces(pv_ref[...].astype(F32), g, S)
        pwb = pw_ref[...].astype(BF16)
        pm = jnp.dot(pooled.astype(BF16), pwb, preferred_element_type=F32)
        scv = sc_ref[...]
        pgv = pg_ref[...].astype(F32)
        sg = _sig(pgv)
        dav = da_ref[...]
        d_ps = dav * (pgv * sg)
        dpg_ref[...] = (dav * (pm * scv) * _dsilu(pgv, sg)).astype(BF16)
        dsc_ref[...] = jnp.sum(d_ps * pm, axis=0, keepdims=True)
        d_pm = (d_ps * scv).astype(BF16)
        dpw_ref[...] = lax.dot_general(pooled.astype(BF16), d_pm, (((0,), (0,)), ((), ())),
                                       preferred_element_type=F32)
        d_pooled = lax.dot_general(d_pm, pwb, (((1,), (1,)), ((), ())), preferred_element_type=F32)
        z = d_pooled / count

        def up(v, k):
            return jnp.where(rowi < S - k, pltpu.roll(v, S - k, axis=0), 0.0)

        t2 = z + up(z, 1)
        t4 = t2 + up(t2, 2)
        t8 = t4 + up(t4, 4)
        t16 = t8 + up(t8, 8)
        adj = jnp.where(g == 0, t2, jnp.where(g == 1, t4, jnp.where(g == 2, t8, t16)))
        dpv_ref[...] = (adj - d_pooled).astype(BF16)

    col = lambda g, j: (0, g)
    ahead = lambda g, j: jnp.minimum(g + j, GROUPS - 1)
    return pl.pallas_call(
        body, name=name, grid=(GROUPS, 2),
        in_specs=[pl.BlockSpec((S, 128), lambda g, j: (0, ahead(g, j))),
                  pl.BlockSpec((S, 128), lambda g, j: (0, PV0 + ahead(g, j))),
                  pl.BlockSpec((S, 128), lambda g, j: (0, PG0 + ahead(g, j))),
                  pl.BlockSpec((None, 128, 128), lambda g, j: (ahead(g, j), 0, 0)),
                  pl.BlockSpec((1, 128), lambda g, j: (0, ahead(g, j))), ANY],
        out_specs=[pl.BlockSpec((S, 128), lambda g, j: (0, PV0 + g + (PG0 - PV0) * j)),
                   pl.BlockSpec((None, 128, 128), lambda g, j: (g, 0, 0)), pl.BlockSpec((1, 128), col)],
        out_shape=[jax.ShapeDtypeStruct(dproj.shape, dproj.dtype),
                   jax.ShapeDtypeStruct((GROUPS, 128, 128), F32), jax.ShapeDtypeStruct((1, POOL_W), F32)],
        scratch_shapes=[pltpu.VMEM((S, 128), BF16)], input_output_aliases={5: 0},
        compiler_params=_cp(("arbitrary", "arbitrary")),
    )(da, proj, proj, pw, pscale, dproj)


SCAN_SHIFTS = tuple(1 << b for b in range(CH.bit_length() - 1))


def _chunk_cumsum(z, rowi):
    for sh in SCAN_SHIFTS:
        z = z + jnp.where(rowi >= sh, pltpu.roll(z, sh, axis=0), 0.0)
    return z


def _chunk_rev_cumsum(z, rowi):
    for sh in SCAN_SHIFTS:
        z = z + jnp.where(rowi < CH - sh, pltpu.roll(z, CH - sh, axis=0), 0.0)
    return z


def _dot_nn(a, b):
    return jnp.dot(a.astype(BF16), b.astype(BF16), preferred_element_type=F32)


def _dot_nt(a, b):
    return lax.dot_general(a.astype(BF16), b.astype(BF16), (((1,), (1,)), ((), ())), preferred_element_type=F32)


def _dot_tn(a, b):
    return lax.dot_general(a.astype(BF16), b.astype(BF16), (((0,), (0,)), ((), ())), preferred_element_type=F32)


def _gates(hq, hf, lbv):
    hq, hf = hq.astype(F32), hf.astype(F32)
    sq = _sig(hq)
    sf = _sig(hf)
    f = lbv + (1.0 - lbv) * sf
    fc = jnp.maximum(f, 1e-30)
    return hq * sq, sq, sf, f, fc, jnp.log(fc)


DECAY_CAP = 60.0


def _block_ref(c_ref, i, sb):
    if i == 0:
        return jnp.zeros((1, HD), F32)
    return c_ref[sb * i - 1:sb * i, :]


def _block_decay(c_ref, sb):
    spans = [_block_ref(c_ref, i, sb) - c_ref[sb * (i + 1) - 1:sb * (i + 1), :] for i in range(CH // sb)]
    return functools.reduce(jnp.maximum, spans)


def _pair_factors(q_ref, k, c_ref, first, cap, round_bf16, sb):
    nb = CH // sb
    c = c_ref[...]
    zero = jnp.zeros((sb, HD), F32)
    q_groups, k_groups, eqs, eks = [], [], [], []
    for i in range(first, nb):
        blk = slice(sb * i, sb * (i + 1))
        r_i = _block_ref(c_ref, i, sb)
        eq = jnp.exp(jnp.minimum(c_ref[blk, :] - r_i, 0.0))
        ek = jnp.exp(jnp.minimum(r_i - c, cap))
        qi, kei = q_ref[blk, :] * eq, k * ek
        if round_bf16:
            qi, kei = qi.astype(BF16).astype(F32), kei.astype(BF16).astype(F32)
        q_groups.append(jnp.concatenate([zero] * i + [qi] + [zero] * (nb - 1 - i), axis=0))
        k_groups.append(kei)
        eqs.append(eq)
        eks.append(ek)
    return jnp.concatenate(q_groups, axis=1), jnp.concatenate(k_groups, axis=1), eqs, eks


def _pair_mask(rowi, coli, strict, sb):
    return (coli < jnp.bitwise_and(rowi, -sb)) if strict else (coli <= rowi)


def _hgrn_fwd(proj, lb, gn, name, carry=None):
    S = proj.shape[0]
    nch = S // CH
    W = NH * HD

    def body(hq_ref, hf_ref, hi_ref, hg_ref, lb_ref, gn_ref, bin_ref, bint_ref, oraw_ref, st_ref, mild_ref,
             cum_ref, q_s, k_s, c_s, v_s, o_s, state_s, qf_s, kf_s, cf_s):
        state_s[...] = jnp.zeros((NH, HD, HD), F32)
        rowi = lax.broadcasted_iota(jnp.int32, (CH, 1), 0)
        coli = lax.broadcasted_iota(jnp.int32, (1, CH), 1)
        sbi = lax.broadcasted_iota(jnp.int32, (SB, 1), 0)
        gnv = gn_ref[...]

        def gates_pass(n, worst):
            wide, narrow = worst
            rows = pl.ds(pl.multiple_of(n * CH, CH), CH)
            for hh in range(NH):
                lanes = slice(hh * HD, (hh + 1) * HD)
                q, _, _, f, _, logf = _gates(hq_ref[rows, lanes], hf_ref[rows, lanes], lb_ref[:, lanes])
                c = _chunk_cumsum(logf, rowi)
                qf_s[hh, rows, :] = q
                kf_s[hh, rows, :] = 1.0 - f
                cf_s[hh, rows, :] = c
                cum_ref[rows, lanes] = c
                c_s[hh] = c
                wide = jnp.maximum(wide, _block_decay(c_s.at[hh], SB_WIDE))
                narrow = jnp.maximum(narrow, _block_decay(c_s.at[hh], SB))
            return wide, narrow

        def between_chunks(hh, n, rows):
            lanes = slice(hh * HD, (hh + 1) * HD)
            q = qf_s[hh, rows, :]
            k = kf_s[hh, rows, :]
            c = cf_s[hh, rows, :]
            v = hi_ref[rows, lanes].astype(F32)
            q_s[hh] = q
            k_s[hh] = k
            c_s[hh] = c
            v_s[hh] = v
            st = state_s[hh]
            st_ref[hh, n] = st.astype(BF16)
            o_s[hh] = _dot_nt(q * jnp.exp(c), st)
            last = c_s[hh, CH - 1:CH, :]
            state_s[hh] = st * jnp.exp(last) + _dot_tn(v, k * jnp.exp(last - c))

        def pairs_matmul(hh, first, cap, strict, sb):
            qx, kc, _, _ = _pair_factors(q_s.at[hh], k_s[hh], c_s.at[hh], first, cap, False, sb)
            a = jnp.where(_pair_mask(rowi, coli, strict, sb), _dot_nt(qx, kc), 0.0)
            o_s[hh] += _dot_nn(a, v_s[hh])

        def within_chunk_matmul(sb):
            return lambda hh: pairs_matmul(hh, 0, DECAY_CAP, False, sb)

        def within_chunk_exact(hh):
            pairs_matmul(hh, 1, 0.0, True, SB)
            for i in range(CH // SB):
                blk = slice(SB * i, SB * (i + 1))
                qb = q_s[hh, blk, :]
                cb = c_s[hh, blk, :]
                acc = jnp.zeros((SB, HD), F32)
                for s in range(SB):
                    row = SB * i + s
                    w = jnp.exp(jnp.minimum(cb - c_s[hh, row:row + 1, :], 0.0))
                    a_col = jnp.sum(qb * k_s[hh, row:row + 1, :] * w, axis=-1, keepdims=True)
                    acc = acc + jnp.where(sbi >= s, a_col, 0.0) * v_s[hh, row:row + 1, :]
                o_s[hh, blk, :] += acc

        def norm_and_gate(hh, rows):
            lanes = slice(hh * HD, (hh + 1) * HD)
            ov = o_s[hh]
            oraw_ref[rows, lanes] = ov
            r = lax.rsqrt(jnp.mean(ov * ov, axis=-1, keepdims=True) + EPS)
            hg = hg_ref[rows, lanes].astype(F32)
            bin_ref[rows, lanes] = ((ov * r) * gnv * (hg * _sig(hg))).astype(BF16)

        def chunk_with(within_chunk):
            def chunk(n, carry):
                rows = pl.ds(pl.multiple_of(n * CH, CH), CH)
                for hh in range(NH):
                    between_chunks(hh, n, rows)
                for hh in range(NH):
                    within_chunk(hh)
                for hh in range(NH):
                    norm_and_gate(hh, rows)
                return carry
            return chunk

        none = jnp.zeros((1, HD), F32)
        wide, narrow = lax.fori_loop(0, nch, gates_pass, (none, none))
        tier = jnp.where(jnp.max(wide) <= DECAY_CAP, 2.0, jnp.where(jnp.max(narrow) <= DECAY_CAP, 1.0, 0.0))
        mild_ref[...] = jnp.broadcast_to(tier, (8, HD))

        @pl.when(tier == 2.0)
        def _():
            lax.fori_loop(0, nch, chunk_with(within_chunk_matmul(SB_WIDE)), 0, unroll=4)

        @pl.when(tier == 1.0)
        def _():
            lax.fori_loop(0, nch, chunk_with(within_chunk_matmul(SB)), 0, unroll=2)

        @pl.when(tier == 0.0)
        def _():
            lax.fori_loop(0, nch, chunk_with(within_chunk_exact), 0)

        bint_ref[...] = bin_ref[...].astype(F32).T.astype(BF16)

    col = lambda off: pl.BlockSpec((S, W), lambda h: (0, off // NH + h))
    head = pl.BlockSpec((S, W), lambda h: (0, h))
    outs = _call(
        body, name=name, grid=(HEADS // NH,),
        in_specs=[col(HQ0), col(HF0), col(HI0), col(HG0), pl.BlockSpec((1, W), lambda h: (0, h)),
                  pl.BlockSpec((1, HD), lambda h: (0, 0))],
        out_specs=[head, pl.BlockSpec((W, S), lambda h: (h, 0)), head,
                   pl.BlockSpec((NH, nch, HD, HD), lambda h: (h, 0, 0, 0)),
                   pl.BlockSpec((8, HD), lambda h: (h, 0)), head],
        out_shape=[jax.ShapeDtypeStruct((S, D), BF16), jax.ShapeDtypeStruct((D, S), BF16),
                   jax.ShapeDtypeStruct((S, D), F32), jax.ShapeDtypeStruct((HEADS, nch, HD, HD), BF16),
                   jax.ShapeDtypeStruct((8 * HEADS // NH, HD), F32), jax.ShapeDtypeStruct((S, D), F32)],
        scratch_shapes=[pltpu.VMEM((NH, CH, HD), F32)] * 5 + [pltpu.VMEM((NH, HD, HD), F32)]
        + [pltpu.VMEM((NH, S, HD), F32)] * 3,
        sem=("parallel",), args=(proj, proj, proj, proj, lb, gn), carry=carry)
    return outs[:6], outs[6:]


def _hgrn_bwd(dbin, proj, oraw, states, mild, cum, lb, gn, dproj, name, carry=None):
    S = proj.shape[0]
    nch = S // CH
    W = NH * HD
    n_in = 12

    def body(*refs):
        ins, (dproj_ref, dlb_ref, dgn_ref) = refs[:n_in - 1], refs[n_in:n_in + 3]
        scratch, later = refs[n_in + 3:-3], refs[-3:]
        seg = pl.program_id(1)

        @pl.when(seg == 0)
        def _():
            heads(*ins, dproj_ref, *later, dlb_ref, dgn_ref, *scratch)

        for s, kept in enumerate(later):
            @pl.when(seg == s + 1)
            def _(kept=kept):
                dproj_ref[...] = kept[...]

    def heads(db_ref, hq_ref, hf_ref, hi_ref, hg_ref, or_ref, st_ref, mild_ref, cum_ref, lb_ref, gn_ref,
              dq_ref, df_ref, di_ref, dg_ref, dlb_ref, dgn_ref,
              q_s, k_s, c_s, v_s, do_s, dq_s, dk_s, dv_s, dc_s, dqd_s, dkd_s, f_s, sf_s, sq_s, dl_s, dst_s,
              dlb_s, dgn_s):
        dst_s[...] = jnp.zeros((NH, HD, HD), F32)
        dlb_s[...] = jnp.zeros((1, W), F32)
        dgn_s[...] = jnp.zeros((1, HD), F32)
        rowi = lax.broadcasted_iota(jnp.int32, (CH, 1), 0)
        coli = lax.broadcasted_iota(jnp.int32, (1, CH), 1)
        sbi = lax.broadcasted_iota(jnp.int32, (SB, 1), 0)
        gnv = gn_ref[...]
        def between_chunks(hh, n, rows):
            lanes = slice(hh * HD, (hh + 1) * HD)
            lbv = lb_ref[:, lanes]
            hq = hq_ref[rows, lanes].astype(F32)
            sq = _sig(hq)
            sf = _sig(hf_ref[rows, lanes].astype(F32))
            f = lbv + (1.0 - lbv) * sf
            q = hq * sq
            k = 1.0 - f
            f_s[hh] = f
            sf_s[hh] = sf
            sq_s[hh] = sq
            v = hi_ref[rows, lanes].astype(F32)
            c = cum_ref[rows, lanes]
            ov = or_ref[rows, lanes]
            hg = hg_ref[rows, lanes].astype(F32)
            sg = _sig(hg)
            r = lax.rsqrt(jnp.mean(ov * ov, axis=-1, keepdims=True) + EPS)
            dbv = db_ref[rows, lanes]
            d_on = dbv * (hg * sg)
            dg_ref[rows, lanes] = (dbv * ((ov * r) * gnv) * _dsilu(hg, sg)).astype(BF16)
            dgn_s[...] += jnp.sum(d_on * (ov * r), axis=0, keepdims=True)
            u = d_on * gnv
            do = r * u - ov * (r * r * r) * jnp.mean(u * ov, axis=-1, keepdims=True)
            q_s[hh] = q
            k_s[hh] = k
            c_s[hh] = c
            v_s[hh] = v
            do_s[hh] = do
            st = st_ref[hh, n].astype(F32)
            dst = dst_s[hh]
            ec = jnp.exp(c)
            last = c_s[hh, CH - 1:CH, :]
            el = jnp.exp(last - c)
            elast = jnp.exp(last)
            dq = _dot_nn(do, st) * ec
            dk = _dot_nn(v, dst) * el
            dq_s[hh] = dq
            dk_s[hh] = dk
            dv_s[hh] = _dot_nt(k * el, dst)
            dc_s[hh] = q * dq - k * dk
            dl_s[hh] = (jnp.sum(k * dk, axis=0, keepdims=True)
                        + elast * jnp.sum(st * dst, axis=0, keepdims=True))
            dst_s[hh] = dst * elast + _dot_tn(do, q * ec)

        def pairs_matmul(hh, first, cap, strict, sb):
            do = do_s[hh]
            qx, kc, eqs, eks = _pair_factors(q_s.at[hh], k_s[hh], c_s.at[hh], first, cap, True, sb)
            mask = _pair_mask(rowi, coli, strict, sb)
            a = jnp.where(mask, _dot_nt(qx, kc), 0.0)
            d_a = jnp.where(mask, _dot_nt(do, v_s[hh]).astype(BF16).astype(F32), 0.0)
            dqx = _dot_nn(d_a, kc)
            dkc = _dot_tn(d_a, qx)
            dv_s[hh] += _dot_tn(a, do)
            dk, dcum = dk_s[hh], dc_s[hh]
            dq_slabs = [jnp.zeros((sb, HD), F32)] * first
            dc_slabs = [jnp.zeros((sb, HD), F32)] * first
            for g, (eq, ek) in enumerate(zip(eqs, eks)):
                rows = slice(sb * (first + g), sb * (first + g + 1))
                cols = slice(HD * g, HD * (g + 1))
                dq_i = dqx[rows, cols]
                dk_i = dkc[:, cols]
                dq_slabs.append(dq_i * eq)
                dc_slabs.append(qx[rows, cols] * dq_i)
                dk = dk + dk_i * ek
                dcum = dcum - kc[:, cols] * dk_i
            dq_s[hh] += jnp.concatenate(dq_slabs, axis=0)
            dk_s[hh] = dk
            dc_s[hh] = dcum + jnp.concatenate(dc_slabs, axis=0)

        def pairs_exact(hh):
            dqd_s[hh] = jnp.zeros((CH, HD), F32)
            dkd_s[hh] = jnp.zeros((CH, HD), F32)
            for i in range(CH // SB):
                blk = slice(SB * i, SB * (i + 1))
                qb = q_s[hh, blk, :]
                cb = c_s[hh, blk, :]
                dob = do_s[hh, blk, :]
                dq_acc = jnp.zeros((SB, HD), F32)
                for s in range(SB):
                    row = SB * i + s
                    ks = k_s[hh, row:row + 1, :]
                    vs = v_s[hh, row:row + 1, :]
                    w = jnp.exp(jnp.minimum(cb - c_s[hh, row:row + 1, :], 0.0))
                    live = sbi >= s
                    a_col = jnp.where(live, jnp.sum(qb * ks * w, axis=-1, keepdims=True), 0.0)
                    da_col = jnp.where(live, jnp.sum(dob * vs, axis=-1, keepdims=True), 0.0)
                    dq_acc = dq_acc + da_col * ks * w
                    dkd_s[hh, row:row + 1, :] += jnp.sum(da_col * qb * w, axis=0, keepdims=True)
                    dv_s[hh, row:row + 1, :] += jnp.sum(a_col * dob, axis=0, keepdims=True)
                dqd_s[hh, blk, :] += dq_acc
            dq_d = dqd_s[hh]
            dk_d = dkd_s[hh]
            dq_s[hh] += dq_d
            dk_s[hh] += dk_d
            dc_s[hh] += q_s[hh] * dq_d - k_s[hh] * dk_d

        def gate_grads(hh, rows):
            lanes = slice(hh * HD, (hh + 1) * HD)
            lbv = lb_ref[:, lanes]
            hq = hq_ref[rows, lanes].astype(F32)
            f, sf, sq = f_s[hh], sf_s[hh], sq_s[hh]
            dlogf = _chunk_rev_cumsum(dc_s[hh], rowi) + dl_s[hh]
            dfv = jnp.where(f > 1e-30, dlogf / jnp.maximum(f, 1e-30), 0.0) - dk_s[hh]
            dlb_s[:, lanes] += jnp.sum(dfv * (1.0 - sf), axis=0, keepdims=True)
            df_ref[rows, lanes] = (dfv * (1.0 - lbv) * sf * (1.0 - sf)).astype(BF16)
            dq_ref[rows, lanes] = (dq_s[hh] * _dsilu(hq, sq)).astype(BF16)
            di_ref[rows, lanes] = dv_s[hh].astype(BF16)

        def chunk_with(pairs):
            def chunk(j, carry):
                n = nch - 1 - j
                rows = pl.ds(pl.multiple_of(n * CH, CH), CH)
                for hh in range(NH):
                    between_chunks(hh, n, rows)
                for hh in range(NH):
                    pairs(hh)
                for hh in range(NH):
                    gate_grads(hh, rows)
                return carry
            return chunk

        def pairs_mild(sb):
            return lambda hh: pairs_matmul(hh, 0, DECAY_CAP, False, sb)

        def pairs_any(hh):
            pairs_matmul(hh, 1, 0.0, True, SB)
            pairs_exact(hh)

        tier = jnp.max(mild_ref[...])

        @pl.when(tier == 2.0)
        def _():
            lax.fori_loop(0, nch, chunk_with(pairs_mild(SB_WIDE)), 0, unroll=2)

        @pl.when(tier == 1.0)
        def _():
            lax.fori_loop(0, nch, chunk_with(pairs_mild(SB)), 0)

        @pl.when(tier == 0.0)
        def _():
            lax.fori_loop(0, nch, chunk_with(pairs_any), 0)

        dlb_ref[...] = dlb_s[...]
        dgn_ref[...] = jnp.broadcast_to(dgn_s[...], (8, HD))

    ahead = lambda h, s: jnp.minimum(h + jnp.minimum(s, 1), HEADS // NH - 1)
    col = lambda off: pl.BlockSpec((S, W), lambda h, s: (0, off // NH + ahead(h, s)))
    head_in = pl.BlockSpec((S, W), lambda h, s: (0, ahead(h, s)))
    vec_in = pl.BlockSpec((1, W), lambda h, s: (0, ahead(h, s)))
    vec = pl.BlockSpec((1, W), lambda h, s: (0, h))
    seg_w = (HF0 - HQ0) // NH
    outs = _call(
        body, name=name, grid=(HEADS // NH, 4),
        in_specs=[head_in, col(HQ0), col(HF0), col(HI0), col(HG0), head_in,
                  pl.BlockSpec((NH, nch, HD, HD), lambda h, s: (ahead(h, s), 0, 0, 0)),
                  pl.BlockSpec((8, HD), lambda h, s: (ahead(h, s), 0)), head_in, vec_in,
                  pl.BlockSpec((1, HD), lambda h, s: (0, 0)), ANY],
        out_specs=[pl.BlockSpec((S, W), lambda h, s: (0, HQ0 // NH + seg_w * s + h)), vec,
                   pl.BlockSpec((8, HD), lambda h, s: (h, 0))],
        out_shape=[jax.ShapeDtypeStruct(dproj.shape, dproj.dtype), jax.ShapeDtypeStruct((1, D), F32),
                   jax.ShapeDtypeStruct((8 * HEADS // NH, HD), F32)],
        scratch_shapes=[pltpu.VMEM((NH, CH, HD), F32)] * 14
        + [pltpu.VMEM((NH, 1, HD), F32), pltpu.VMEM((NH, HD, HD), F32), pltpu.VMEM((1, W), F32),
           pltpu.VMEM((1, HD), F32)] + [pltpu.VMEM((S, W), BF16)] * 3,
        sem=("arbitrary", "arbitrary"), aliases={n_in - 1: 0},
        args=(dbin, proj, proj, proj, proj, oraw, states, mild, cum, lb, gn, dproj), carry=carry)
    dproj, dlb, dgn = outs[:3]
    return (dproj, dlb, dgn.reshape(HEADS // NH, 8, HD)[:, 0, :]), outs[3:]


def _lower_bounds(l0, l1):
    m = jnp.maximum(l0, l1)
    e0 = jnp.exp(l0 - m)
    e1 = jnp.exp(l1 - m)
    tot = e0 + e1
    p0 = e0 / tot
    p1 = e1 / tot
    return jnp.clip(p0 - p0, 0.0, 1.0), jnp.clip((p0 + p1) - p0, 0.0, 1.0)


def _lb_fwd(logits):
    def body(l_ref, o_ref):
        lb0, lb1 = _lower_bounds(l_ref[0:1, :], l_ref[1:2, :])
        o_ref[0:1, :] = lb0
        o_ref[1:2, :] = lb1

    return pl.pallas_call(body, name="lb_fwd", out_shape=jax.ShapeDtypeStruct((2, D), F32))(logits)


def _lb_bwd(logits, dlb):
    def body(l_ref, d_ref, o_ref):
        _, vjp = jax.vjp(_lower_bounds, l_ref[0:1, :], l_ref[1:2, :])
        g0, g1 = vjp((d_ref[0:1, :], d_ref[1:2, :]))
        o_ref[0:1, :] = g0
        o_ref[1:2, :] = g1

    return pl.pallas_call(body, name="lb_bwd", out_shape=jax.ShapeDtypeStruct((2, D), F32))(logits, dlb)


ADA_PAD = 128


def _ada_fwd(c_pad, w_ada, b_sh):
    ns = w_ada.shape[2]

    def body(c_ref, w_ref, b_ref, o_ref):
        cv = c_ref[...]
        ca = (cv * _sig(cv)).astype(BF16)
        for l in range(2):
            res = jnp.dot(ca, w_ref[l].astype(BF16), preferred_element_type=F32)
            o_ref[:, l * ns:(l + 1) * ns] = res[0:NDEV, :] + b_ref[l:l + 1, :]

    return pl.pallas_call(body, name="ada_fwd", out_shape=jax.ShapeDtypeStruct((NDEV, 2 * ns), F32),
                          compiler_params=_cp())(c_pad, w_ada, b_sh)


def _ada_wgrad(c_pad_t, d_ada_sh):
    ns = d_ada_sh.shape[2]

    def body(c_ref, d_ref, o_ref):
        cv = c_ref[...]
        ca = (cv * _sig(cv)).astype(BF16)
        for l in range(2):
            o_ref[l] = jnp.dot(ca, d_ref[l].astype(BF16), preferred_element_type=F32)

    return pl.pallas_call(body, name="ada_wgrad", out_shape=jax.ShapeDtypeStruct((2, D, ns), F32),
                          compiler_params=_cp())(c_pad_t, d_ada_sh)


def _sum_devices(g, late):
    _, R, C = g.shape

    def body(g_ref, l_ref, o_ref, head_ref):
        acc, acc_late = g_ref[0], l_ref[0]
        for d in range(1, NDEV):
            acc, acc_late = acc + g_ref[d], acc_late + l_ref[d]
        o_ref[...] = acc
        o_ref[0:2, :] = acc_late[0:2]
        o_ref[8:9, :] = acc_late[2:3]
        for d in range(NDEV):
            head_ref[d] = g_ref[d, 0:8, :]
            head_ref[d, 0:2, :] = l_ref[d, 0:2, :]

    return pl.pallas_call(body, name="sum_devices",
                          out_shape=[jax.ShapeDtypeStruct((R, C), F32), jax.ShapeDtypeStruct((NDEV, 8, C), F32)],
                          compiler_params=_cp())(g, late)


def _adamw(w, g, m, v, name, echo=False):
    R, C = w.shape
    tr = _row_tile(R, max(8, (1 << 19) // C))

    def body(w_ref, g_ref, m_ref, v_ref, d_ref, nm_ref, nv_ref, *g_out):
        d_ref[...], nm_ref[...], nv_ref[...] = _adamw_update(w_ref[...], g_ref[...], m_ref[...], v_ref[...])
        for o_ref in g_out:
            o_ref[...] = g_ref[...]

    tile = pl.BlockSpec((tr, C), lambda i: (i, 0))
    n_out = 4 if echo else 3
    return _call(body, name=name, grid=(R // tr,), in_specs=[tile] * 4, out_specs=[tile] * n_out,
                 out_shape=[jax.ShapeDtypeStruct((R, C), F32)] * n_out, sem=("parallel",), args=(w, g, m, v))


def _adamw_many(wgmv, name, steps=4):
    n = len(wgmv)

    def body(*refs):
        ins, outs = refs[:4 * n], refs[4 * n:]
        for a in range(n):
            w_ref, g_ref, m_ref, v_ref = ins[4 * a:4 * a + 4]
            d_ref, nm_ref, nv_ref, g_out = outs[4 * a:4 * a + 4]
            d_ref[...], nm_ref[...], nv_ref[...] = _adamw_update(w_ref[...], g_ref[...], m_ref[...], v_ref[...])
            g_out[...] = g_ref[...]

    def tile(t):
        return pl.BlockSpec((t.shape[0] // steps, t.shape[1]), lambda i: (i, 0))

    flat = [t for four in wgmv for t in four]
    outs = pl.pallas_call(
        body, name=name, grid=(steps,), in_specs=[tile(t) for t in flat],
        out_specs=[tile(four[0]) for four in wgmv for _ in range(4)],
        out_shape=[jax.ShapeDtypeStruct(four[0].shape, F32) for four in wgmv for _ in range(4)],
        compiler_params=_cp(("parallel",)))(*flat)
    return [outs[4 * a:4 * a + 4] for a in range(n)]


def _adamw_update(w, g, m, v):
    nm = B1 * m + (1.0 - B1) * g
    nv = B2 * v + (1.0 - B2) * (g * g)
    m_hat = nm / (1.0 - B1 ** STEP)
    v_hat = nv / (1.0 - B2 ** STEP)
    return -LR * (m_hat / (jnp.sqrt(v_hat) + AEPS) + WD * w), nm, nv


SMALL_KEYS = ("b_ada", "g_pre", "g_post", "lb_logits", "pool_w", "pool_scale", "hgrn_norm_g")


def _small_pieces(key):
    one = lambda i: slice(i, i + 1)
    if key == "b_ada":
        return [((one(l), slice(j * D, (j + 1) * D)), (one(3 * l + j), slice(0, D)))
                for l in range(2) for j in range(3)]
    if key in ("g_pre", "g_post", "lb_logits"):
        row0 = {"g_pre": 8, "g_post": 16, "lb_logits": 24}[key]
        return [((slice(0, 2), slice(0, D)), (slice(row0, row0 + 2), slice(0, D)))]
    if key == "pool_w":
        return [((l, g, pl.ds(k, 16, stride=8), slice(0, 128)),
                 (slice(32 + 64 * l + 16 * g, 48 + 64 * l + 16 * g), slice(128 * k, 128 * (k + 1))))
                for l in range(2) for g in range(GROUPS) for k in range(8)]
    width = {"pool_scale": POOL_W, "hgrn_norm_g": HD}[key]
    row = {"pool_scale": 160, "hgrn_norm_g": 168}[key]
    return [((one(l), slice(0, width)), (one(row), slice(l * width, (l + 1) * width))) for l in range(2)]


def _adamw_small(g_small, g_lb_logits, wmv):
    n = len(SMALL_KEYS)

    def body(g_ref, glb_ref, *refs):
        ins, outs = refs[:3 * n], refs[3 * n:]
        for p, key in enumerate(SMALL_KEYS):
            w_ref, m_ref, v_ref = ins[3 * p:3 * p + 3]
            for at, (rows, lanes) in _small_pieces(key):
                gv = glb_ref[at] if key == "lb_logits" else g_ref[rows, lanes]
                res = _adamw_update(w_ref[at], gv, m_ref[at], v_ref[at])
                for o_ref, val in zip(outs[4 * p:4 * p + 4], (*res, gv)):
                    o_ref[at] = val

    flat = [t for key in SMALL_KEYS for t in wmv[key]]
    outs = pl.pallas_call(body, name="adamw_small",
                          out_shape=[jax.ShapeDtypeStruct(wmv[key][0].shape, F32) for key in SMALL_KEYS
                                     for _ in range(4)],
                          compiler_params=_cp())(g_small, g_lb_logits, *flat)
    return {key: outs[4 * p:4 * p + 4] for p, key in enumerate(SMALL_KEYS)}


def _cast_to_slot(place, w, l, name):
    _, R, C = w.shape
    tr = _row_tile(R, max(8, (1 << 19) // C))

    def body(p_ref, w_ref, o_ref):
        o_ref[...] = w_ref[...].astype(BF16)

    return pl.pallas_call(
        body, name=name, out_shape=jax.ShapeDtypeStruct((NCHIP, R, C), BF16),
        grid_spec=pltpu.PrefetchScalarGridSpec(
            num_scalar_prefetch=1, grid=(R // tr,),
            in_specs=[pl.BlockSpec((None, tr, C), lambda i, p_ref: (l, i, 0))],
            out_specs=pl.BlockSpec((None, tr, C), lambda i, p_ref: (p_ref[0], i, 0))),
        compiler_params=_cp(("parallel",)),
    )(place, w)


def _cast_to_slots(place, ws, name):
    n = len(ws)

    def body(p_ref, *refs):
        for w_ref, o_ref in zip(refs[:n], refs[n:]):
            o_ref[...] = w_ref[...].astype(BF16)

    def layer(l):
        return lambda i, p_ref: (l, 0, 0)

    return pl.pallas_call(
        body, name=name, out_shape=[jax.ShapeDtypeStruct((NCHIP,) + w.shape[1:], BF16) for w, _ in ws],
        grid_spec=pltpu.PrefetchScalarGridSpec(
            num_scalar_prefetch=1, grid=(1,),
            in_specs=[pl.BlockSpec((None,) + w.shape[1:], layer(l)) for w, l in ws],
            out_specs=[pl.BlockSpec((None,) + w.shape[1:], lambda i, p_ref: (p_ref[0], 0, 0)) for w, _ in ws]),
        compiler_params=_cp(("arbitrary",)),
    )(place, *[w for w, _ in ws])


def _pair_adds(core, gs, gots, name):
    n = len(gs)

    def body(c_ref, *refs):
        for a_ref, b_ref, o_ref in zip(refs[:n], refs[n:2 * n], refs[2 * n:]):
            o_ref[...] = (a_ref[...].astype(F32) + b_ref[...].astype(F32)).astype(o_ref.dtype)

    def whole(t):
        return pl.BlockSpec(t.shape, lambda i, c_ref: (0, 0, 0))

    return pl.pallas_call(
        body, name=name, out_shape=[jax.ShapeDtypeStruct(t.shape, BF16) for t in gots],
        grid_spec=pltpu.PrefetchScalarGridSpec(
            num_scalar_prefetch=1, grid=(1,),
            in_specs=[pl.BlockSpec(t.shape, lambda i, c_ref: (0, c_ref[0], 0)) for t in gots]
            + [whole(t) for t in gots],
            out_specs=[whole(t) for t in gots]),
        compiler_params=_cp(("arbitrary",)),
    )(core, *gs, *gots)


def _pair_add(core, g, got, name):
    _, R, C = g.shape
    r2 = R // 2
    tr = _row_tile(r2, max(8, (1 << 19) // C))
    nt = r2 // tr

    def body(c_ref, a_ref, b_ref, o_ref):
        o_ref[...] = (a_ref[...].astype(F32) + b_ref[...].astype(F32)).astype(o_ref.dtype)

    return pl.pallas_call(
        body, name=name, out_shape=jax.ShapeDtypeStruct((NCHIP, r2, C), BF16),
        grid_spec=pltpu.PrefetchScalarGridSpec(
            num_scalar_prefetch=1, grid=(NCHIP, nt),
            in_specs=[pl.BlockSpec((None, tr, C), lambda j, i, c_ref: (j, c_ref[0] * nt + i, 0)),
                      pl.BlockSpec((None, tr, C), lambda j, i, c_ref: (j, i, 0))],
            out_specs=pl.BlockSpec((None, tr, C), lambda j, i, c_ref: (j, i, 0))),
        compiler_params=_cp(("parallel", "parallel")),
    )(core, g, got)


def _sum_in_chip_order(me, own_ref, r_ref):
    own = own_ref[...].astype(F32)
    acc = None
    for j in range(NCHIP):
        slot = jnp.minimum(jnp.where(j > me, j - 1, j), NCHIP - 2)
        term = jnp.where(me == j, own, r_ref[slot].astype(F32))
        acc = term if acc is None else acc + term
    return acc


def _chip_sums(place, parts, recvs, name):
    n = len(parts)

    def body(p_ref, *refs):
        for a in range(n):
            for l in range(2):
                refs[4 * n + a][l] = _sum_in_chip_order(p_ref[0], refs[2 * a + l], refs[2 * n + 2 * a + l])

    def own(t):
        return pl.BlockSpec((None,) + t.shape[1:], lambda i, p_ref: (p_ref[0], 0, 0))

    def whole(t):
        return pl.BlockSpec(t.shape, lambda i, p_ref: (0, 0, 0))

    flat_p = [t for pair in parts for t in pair]
    flat_r = [t for pair in recvs for t in pair]
    return pl.pallas_call(
        body, name=name,
        out_shape=[jax.ShapeDtypeStruct((2, 2 * p[0].shape[1], p[0].shape[2]), F32) for p in parts],
        grid_spec=pltpu.PrefetchScalarGridSpec(
            num_scalar_prefetch=1, grid=(1,),
            in_specs=[own(t) for t in flat_p] + [whole(t) for t in flat_r],
            out_specs=[pl.BlockSpec((2,) + p[0].shape[1:], lambda i, p_ref: (0, p_ref[1], 0)) for p in parts]),
        compiler_params=_cp(("arbitrary",)),
    )(place, *flat_p, *flat_r)


def _chip_sum(place, part, recv, layer, both, name):
    _, r2, C = part.shape
    tr = _row_tile(r2, max(8, (1 << 18) // C))
    nt = r2 // tr

    def body(p_ref, own_ref, r_ref, *rest):
        rest[-1][...] = _sum_in_chip_order(p_ref[0], own_ref, r_ref)

    args = (place, part, recv) if both is None else (place, part, recv, both)
    return pl.pallas_call(
        body, name=name, out_shape=jax.ShapeDtypeStruct((2, 2 * r2, C), F32),
        grid_spec=pltpu.PrefetchScalarGridSpec(
            num_scalar_prefetch=1, grid=(nt,),
            in_specs=[pl.BlockSpec((None, tr, C), lambda i, p_ref: (p_ref[0], i, 0)),
                      pl.BlockSpec((NCHIP - 1, tr, C), lambda i, p_ref: (0, i, 0))] + [ANY] * (len(args) - 3),
            out_specs=pl.BlockSpec((None, tr, C), lambda i, p_ref: (layer, p_ref[1] * nt + i, 0))),
        input_output_aliases={} if both is None else {3: 0},
        compiler_params=_cp(("parallel",)),
    )(*args)


def _place():
    x, y, c = lax.axis_index("x"), lax.axis_index("y"), lax.axis_index("c")
    chips = [(1 - x, y), (x, 1 - y), (1 - x, 1 - y)]
    return x, y, c, chips


def _gather_small(blk, name):
    m_per, n = blk.shape

    def body(x_ref, out_ref, send_sems, recv_sems, local_sem):
        x, y, c, chips = _place()
        me, sibling = (x, y, c), (x, y, 1 - c)

        def rows(px, py, pc):
            return out_ref.at[pl.ds((4 * px + 2 * py + pc) * m_per, m_per), :]

        def copy(k, block, to, src=None):
            return pltpu.make_async_remote_copy(
                src_ref=rows(*block) if src is None else src, dst_ref=rows(*block),
                send_sem=send_sems.at[k], recv_sem=recv_sems.at[k], device_id=to, device_id_type=MESH)

        mine = pltpu.make_async_copy(x_ref, rows(*me), local_sem)
        mine.start()
        first = [copy(0, me, sibling, src=x_ref)]
        first += [copy(1 + j, me, (*chip, c), src=x_ref) for j, chip in enumerate(chips)]
        for cp in first:
            cp.start()
        passed = [copy(4 + j, (*chip, c), sibling) for j, chip in enumerate(chips)]
        for j, chip in enumerate(chips):
            copy(1 + j, (*chip, c), me).wait_recv()
            passed[j].start()
        copy(0, sibling, me).wait_recv()
        for j, chip in enumerate(chips):
            copy(4 + j, (*chip, 1 - c), me).wait_recv()
        for cp in first + passed:
            cp.wait_send()
        mine.wait()

    return pl.pallas_call(
        body, name=name, out_shape=jax.ShapeDtypeStruct((NDEV * m_per, n), blk.dtype),
        in_specs=[pl.BlockSpec(memory_space=pltpu.VMEM)], out_specs=pl.BlockSpec(memory_space=pltpu.VMEM),
        scratch_shapes=[pltpu.SemaphoreType.DMA((7,)), pltpu.SemaphoreType.DMA((7,)), pltpu.SemaphoreType.DMA],
        compiler_params=_cp(),
    )(blk)


def _gather_rows_carry(blk):
    m_per, n = blk.shape

    def rows(ref, px, py, pc):
        return ref.at[pl.ds((4 * px + 2 * py + pc) * m_per, m_per), :]

    def copy(ins, outs, send_sems, recv_sems, k, block, to, own=False):
        return pltpu.make_async_remote_copy(
            src_ref=ins[0] if own else rows(outs[0], *block), dst_ref=rows(outs[0], *block),
            send_sem=send_sems.at[k], recv_sem=recv_sems.at[k], device_id=to, device_id_type=MESH)

    def mine(ins, outs, send_sems):
        x, y, c, _ = _place()
        return pltpu.make_async_copy(ins[0], rows(outs[0], x, y, c), send_sems.at[7])

    def start(ins, outs, send_sems, recv_sems):
        x, y, c, chips = _place()
        mine(ins, outs, send_sems).start()
        copy(ins, outs, send_sems, recv_sems, 0, (x, y, c), (x, y, 1 - c), own=True).start()
        for j, chip in enumerate(chips):
            copy(ins, outs, send_sems, recv_sems, 1 + j, (x, y, c), (*chip, c), own=True).start()

    def finish(ins, outs, send_sems, recv_sems):
        x, y, c, chips = _place()
        for j, chip in enumerate(chips):
            copy(ins, outs, send_sems, recv_sems, 1 + j, (*chip, c), (x, y, c)).wait_recv()
            copy(ins, outs, send_sems, recv_sems, 4 + j, (*chip, c), (x, y, 1 - c)).start()
        copy(ins, outs, send_sems, recv_sems, 0, (x, y, 1 - c), (x, y, c)).wait_recv()
        for j, chip in enumerate(chips):
            copy(ins, outs, send_sems, recv_sems, 4 + j, (*chip, 1 - c), (x, y, c)).wait_recv()
        copy(ins, outs, send_sems, recv_sems, 0, (x, y, c), (x, y, 1 - c), own=True).wait_send()
        for j, chip in enumerate(chips):
            copy(ins, outs, send_sems, recv_sems, 1 + j, (x, y, c), (*chip, c), own=True).wait_send()
            copy(ins, outs, send_sems, recv_sems, 4 + j, (*chip, c), (x, y, 1 - c)).wait_send()
        mine(ins, outs, send_sems).wait()

    return _Carry([blk], [jax.ShapeDtypeStruct((NDEV * m_per, n), blk.dtype)], {}, 8, start, finish)


def _gather_carry(shards, piece=(0, 1, 1), pass_on=True, late=None):
    n = len(shards)

    def rows(ref, half, pc):
        first, count, of = pc
        r2 = ref.shape[1] // 2
        return pl.ds(half * r2 + first * (r2 // of), count * (r2 // of))

    def over_ici(outs, send_sems, recv_sems, a, j, chip_xy, slot):
        x, y, c, _ = _place()
        blk = outs[a].at[slot, rows(outs[a], c, piece), :]
        return pltpu.make_async_remote_copy(
            src_ref=blk, dst_ref=blk, send_sem=send_sems.at[9 * a + j], recv_sem=recv_sems.at[9 * a + j],
            device_id=(*chip_xy, c), device_id_type=MESH)

    def over_d2d(outs, send_sems, recv_sems, a, j, slot, half, pc):
        x, y, c, _ = _place()
        blk = outs[a].at[slot, rows(outs[a], half, pc), :]
        k = 9 * a + (3 if pc is piece else 6) + j
        return pltpu.make_async_remote_copy(
            src_ref=blk, dst_ref=blk, send_sem=send_sems.at[k], recv_sem=recv_sems.at[k],
            device_id=(x, y, 1 - c), device_id_type=MESH)

    def start(ins, outs, send_sems, recv_sems):
        x, y, c, chips = _place()
        for a in range(n):
            for j, (cx, cy) in enumerate(chips):
                over_ici(outs, send_sems, recv_sems, a, j, (cx, cy), 2 * x + y).start()
                if late:
                    over_d2d(outs, send_sems, recv_sems, a, j, 2 * cx + cy, c, late).start()

    def finish(ins, outs, send_sems, recv_sems):
        x, y, c, chips = _place()
        passed = ([piece] if pass_on else []) + ([late] if late else [])
        for a in range(n):
            for j, (cx, cy) in enumerate(chips):
                over_ici(outs, send_sems, recv_sems, a, j, (cx, cy), 2 * cx + cy).wait_recv()
                if pass_on:
                    over_d2d(outs, send_sems, recv_sems, a, j, 2 * cx + cy, c, piece).start()
        for a in range(n):
            for j, (cx, cy) in enumerate(chips):
                for pc in passed:
                    over_d2d(outs, send_sems, recv_sems, a, j, 2 * cx + cy, 1 - c, pc).wait_recv()
        for a in range(n):
            for j, (cx, cy) in enumerate(chips):
                over_ici(outs, send_sems, recv_sems, a, j, (cx, cy), 2 * x + y).wait_send()
                for pc in passed:
                    over_d2d(outs, send_sems, recv_sems, a, j, 2 * cx + cy, c, pc).wait_send()

    return _Carry(shards, [jax.ShapeDtypeStruct(s.shape, s.dtype) for s in shards],
                  {a: a for a in range(n)}, 9 * n, start, finish)


def _rs_pair(grads, name):
    n = len(grads)

    def body(*refs):
        ins, gots = refs[:n], refs[n:2 * n]
        send_sems, recv_sems = refs[2 * n:]
        x, y, c, _ = _place()
        cps = []
        for a in range(n):
            r2 = ins[a].shape[1] // 2
            cp = pltpu.make_async_remote_copy(
                src_ref=ins[a].at[:, pl.ds((1 - c) * r2, r2), :], dst_ref=gots[a],
                send_sem=send_sems.at[a], recv_sem=recv_sems.at[a],
                device_id=(x, y, 1 - c), device_id_type=MESH)
            cp.start()
            cps.append(cp)
        for cp in cps:
            cp.wait()

    half = [jax.ShapeDtypeStruct((NCHIP, g.shape[1] // 2, g.shape[2]), g.dtype) for g in grads]
    return pl.pallas_call(
        body, name=name, out_shape=half, in_specs=[ANY] * n, out_specs=[ANY] * n,
        scratch_shapes=[pltpu.SemaphoreType.DMA((n,)), pltpu.SemaphoreType.DMA((n,))],
        compiler_params=_cp(),
    )(*grads)


def _pair_sum(core, g, name):
    _, R, C = g.shape
    r2 = R // 2

    def body(c_ref, g_ref, own_ref, o_ref, got_ref, buf, send_sems, recv_sems, local_sems):
        j = pl.program_id(0)
        x, y, c, _ = _place()

        def remote(k):
            return pltpu.make_async_remote_copy(
                src_ref=g_ref.at[k, pl.ds((1 - c) * r2, r2), :], dst_ref=got_ref.at[k],
                send_sem=send_sems.at[k], recv_sem=recv_sems.at[k], device_id=(x, y, 1 - c), device_id_type=MESH)

        def fetch(k):
            return pltpu.make_async_copy(got_ref.at[k], buf.at[k % 2], local_sems.at[k % 2])

        @pl.when(j == 0)
        def _():
            for k in range(NCHIP):
                remote(k).start()
            remote(0).wait_recv()
            fetch(0).start()

        fetch(j).wait()

        @pl.when(j + 1 < NCHIP)
        def _():
            remote(j + 1).wait_recv()
            fetch(j + 1).start()

        o_ref[...] = (own_ref[...].astype(F32) + buf[j % 2].astype(F32)).astype(o_ref.dtype)

        @pl.when(j == NCHIP - 1)
        def _():
            for k in range(NCHIP):
                remote(k).wait_send()

    half = jax.ShapeDtypeStruct((NCHIP, r2, C), g.dtype)
    return pl.pallas_call(
        body, name=name, out_shape=[half, half],
        grid_spec=pltpu.PrefetchScalarGridSpec(
            num_scalar_prefetch=1, grid=(NCHIP,),
            in_specs=[ANY, pl.BlockSpec((None, r2, C), lambda j, c_ref: (j, c_ref[0], 0))],
            out_specs=[pl.BlockSpec((None, r2, C), lambda j, c_ref: (j, 0, 0)), ANY],
            scratch_shapes=[pltpu.VMEM((2, r2, C), g.dtype), pltpu.SemaphoreType.DMA((NCHIP,)),
                            pltpu.SemaphoreType.DMA((NCHIP,)), pltpu.SemaphoreType.DMA((2,))]),
        compiler_params=_cp(("arbitrary",)),
    )(core, g, g)[0]


def _pair_carry(grads):
    n = len(grads)

    def copy(ins, outs, send_sems, recv_sems, a):
        x, y, c, _ = _place()
        r2 = ins[a].shape[1] // 2
        return pltpu.make_async_remote_copy(
            src_ref=ins[a].at[:, pl.ds((1 - c) * r2, r2), :], dst_ref=outs[a],
            send_sem=send_sems.at[a], recv_sem=recv_sems.at[a],
            device_id=(x, y, 1 - c), device_id_type=MESH)

    def start(ins, outs, send_sems, recv_sems):
        for a in range(n):
            copy(ins, outs, send_sems, recv_sems, a).start()

    def finish(ins, outs, send_sems, recv_sems):
        for a in range(n):
            copy(ins, outs, send_sems, recv_sems, a).wait()

    half = [jax.ShapeDtypeStruct((NCHIP, g.shape[1] // 2, g.shape[2]), g.dtype) for g in grads]
    return _Carry(grads, half, {}, n, start, finish)


def _chips_carry(parts, piece=(0, 1, 1), into=None):
    n = len(parts)
    first, count, of = piece

    def rows(ref):
        step = ref.shape[1] // of
        return pl.ds(first * step, count * step)

    def send(ins, outs, send_sems, recv_sems, a, j, chip_xy):
        x, y, c, _ = _place()
        me, them = 2 * x + y, 2 * chip_xy[0] + chip_xy[1]
        return pltpu.make_async_remote_copy(
            src_ref=ins[a].at[them, rows(ins[a]), :],
            dst_ref=outs[a].at[me - (me > them).astype(jnp.int32), rows(outs[a]), :],
            send_sem=send_sems.at[3 * a + j], recv_sem=recv_sems.at[3 * a + j],
            device_id=(*chip_xy, c), device_id_type=MESH)

    def start(ins, outs, send_sems, recv_sems):
        _, _, _, chips = _place()
        for a in range(n):
            for j, chip_xy in enumerate(chips):
                send(ins, outs, send_sems, recv_sems, a, j, chip_xy).start()

    def finish(ins, outs, send_sems, recv_sems):
        x, y, c, chips = _place()
        me = 2 * x + y
        for a in range(n):
            for j, (cx, cy) in enumerate(chips):
                them = 2 * cx + cy
                blk = outs[a].at[them - (them > me).astype(jnp.int32), rows(outs[a]), :]
                pltpu.make_async_remote_copy(
                    src_ref=blk, dst_ref=blk, send_sem=send_sems.at[3 * a + j], recv_sem=recv_sems.at[3 * a + j],
                    device_id=(cx, cy, c), device_id_type=MESH).wait_recv()
        for a in range(n):
            for j, chip_xy in enumerate(chips):
                send(ins, outs, send_sems, recv_sems, a, j, chip_xy).wait_send()

    landing = [jax.ShapeDtypeStruct((NCHIP - 1,) + p.shape[1:], p.dtype) for p in parts]
    if into is None:
        return _Carry(parts, landing, {}, 3 * n, start, finish)
    return _Carry(list(parts) + list(into), landing, {n + a: a for a in range(n)}, 3 * n, start, finish)


def _swap_carry(fulls, layers):
    n = len(fulls)

    def copy(outs, send_sems, recv_sems, a, half):
        x, y, c, _ = _place()
        r2 = outs[a].shape[1] // 2
        blk = outs[a].at[pl.ds(*layers[a]), pl.ds(half * r2, r2), :]
        return pltpu.make_async_remote_copy(
            src_ref=blk, dst_ref=blk, send_sem=send_sems.at[a], recv_sem=recv_sems.at[a],
            device_id=(x, y, 1 - c), device_id_type=MESH)

    def start(ins, outs, send_sems, recv_sems):
        c = lax.axis_index("c")
        for a in range(n):
            copy(outs, send_sems, recv_sems, a, c).start()

    def finish(ins, outs, send_sems, recv_sems):
        c = lax.axis_index("c")
        for a in range(n):
            copy(outs, send_sems, recv_sems, a, 1 - c).wait_recv()
        for a in range(n):
            copy(outs, send_sems, recv_sems, a, c).wait_send()

    return _Carry(fulls, [jax.ShapeDtypeStruct(f.shape, f.dtype) for f in fulls], {a: a for a in range(n)}, n,
                  start, finish)


def _rs_swap(fulls, layers):
    n = len(fulls)
    swap = _swap_carry(fulls, layers)

    def body(*refs):
        outs, (send_sems, recv_sems) = refs[n:2 * n], refs[2 * n:]
        swap.start(None, outs, send_sems, recv_sems)
        swap.finish(None, outs, send_sems, recv_sems)

    return pl.pallas_call(
        body, name="rs_swap", out_shape=swap.outs, in_specs=[ANY] * n, out_specs=[ANY] * n,
        input_output_aliases=swap.aliases,
        scratch_shapes=[pltpu.SemaphoreType.DMA((n,)), pltpu.SemaphoreType.DMA((n,))],
        compiler_params=_cp(),
    )(*fulls)


def _tail_weight_grads(merged_t, b_in_t, a_in_t, dy, dbr_b, dbr_a, name, tn=256):
    S = dy.shape[0]
    nn = D // tn

    def body(mt_ref, bt_ref, at_ref, dy_ref, db_ref, da_ref, go_ref, gh_ref, gp_ref):
        go_ref[...] = jnp.dot(mt_ref[...], dy_ref[...], preferred_element_type=F32).astype(BF16)
        gh_ref[...] = jnp.dot(bt_ref[...], db_ref[...], preferred_element_type=F32).astype(BF16)
        gp_ref[...] = jnp.dot(at_ref[...], da_ref[...], preferred_element_type=F32).astype(BF16)

    left = lambda rows: pl.BlockSpec((rows, S), lambda n: (0, 0))
    right = pl.BlockSpec((S, tn), lambda n: (0, n))
    out = pl.BlockSpec((D, tn), lambda n: (0, n))
    return pl.pallas_call(
        body, name=name, grid=(nn,), in_specs=[left(D), left(D), left(POOL_W), right, right, right],
        out_specs=[out, out, pl.BlockSpec((None, POOL_W, tn), lambda n: (n, 0, 0))],
        out_shape=[jax.ShapeDtypeStruct((D, D), BF16), jax.ShapeDtypeStruct((D, D), BF16),
                   jax.ShapeDtypeStruct((NCHIP, POOL_W, D // NCHIP), BF16)],
        compiler_params=_cp(("parallel",)),
    )(merged_t, b_in_t, a_in_t, dy, dbr_b, dbr_a)


class _GatherInProj:
    def __init__(self, slot, order):
        self.slot, self.order = slot, order


def _proj_with_gather(h, w_slot, order, name, tn=256):
    S, K = h.shape
    nsh, _, ns = w_slot.shape
    tps = ns // tn
    nt = nsh * tps
    r2 = K // 2

    def body(ord_ref, h_ref, w_in_ref, o_ref, w_ref, wbuf, tile_sems, send_sems, recv_sems):
        n = pl.program_id(0)
        x, y, c, chips = _place()

        def half(slot, which):
            return w_ref.at[slot, pl.ds(which * r2, r2), :]

        def over_ici(j, slot):
            blk = half(slot, c)
            return pltpu.make_async_remote_copy(src_ref=blk, dst_ref=blk, send_sem=send_sems.at[j],
                                                recv_sem=recv_sems.at[j], device_id=(*chips[j], c),
                                                device_id_type=MESH)

        def over_d2d(j, which):
            blk = half(2 * chips[j][0] + chips[j][1], which)
            return pltpu.make_async_remote_copy(src_ref=blk, dst_ref=blk, send_sem=send_sems.at[3 + j],
                                                recv_sem=recv_sems.at[3 + j], device_id=(x, y, 1 - c),
                                                device_id_type=MESH)

        def tile_copy(step, slot):
            shard = ord_ref[step // tps]
            return pltpu.make_async_copy(w_ref.at[shard, :, pl.ds((step % tps) * tn, tn)], wbuf.at[slot],
                                         tile_sems.at[slot])

        @pl.when(n == 0)
        def _():
            for j in range(3):
                over_ici(j, 2 * x + y).start()
            tile_copy(0, 0).start()

        for j in range(3):
            @pl.when(n == (j + 1) * tps - 1)
            def _(j=j):
                over_ici(j, 2 * chips[j][0] + chips[j][1]).wait_recv()
                over_d2d(j, c).start()
                over_d2d(j, 1 - c).wait_recv()

        @pl.when(n + 1 < nt)
        def _():
            tile_copy(n + 1, (n + 1) % 2).start()

        tile_copy(n, n % 2).wait()
        o_ref[...] = jnp.dot(h_ref[...], wbuf[n % 2], preferred_element_type=F32).astype(o_ref.dtype)

        @pl.when(n == nt - 1)
        def _():
            for j in range(3):
                over_ici(j, 2 * x + y).wait_send()
                over_d2d(j, c).wait_send()

    return pl.pallas_call(
        body, name=name,
        out_shape=[jax.ShapeDtypeStruct((S, nsh * ns), BF16), jax.ShapeDtypeStruct(w_slot.shape, w_slot.dtype)],
        grid_spec=pltpu.PrefetchScalarGridSpec(
            num_scalar_prefetch=1, grid=(nt,),
            in_specs=[pl.BlockSpec((S, K), lambda n, o_ref: (0, 0)), ANY],
            out_specs=[pl.BlockSpec((S, tn), lambda n, o_ref: (0, o_ref[n // tps] * tps + n % tps)), ANY],
            scratch_shapes=[pltpu.VMEM((2, K, tn), w_slot.dtype), pltpu.SemaphoreType.DMA((2,)),
                            pltpu.SemaphoreType.DMA((6,)), pltpu.SemaphoreType.DMA((6,))]),
        input_output_aliases={2: 1},
        compiler_params=_cp(("arbitrary",)),
    )(order, h, w_slot)


def _mm_ride(a, b, carry, **kw):
    if carry is None:
        return _mm(a, b, **kw), []
    return _mm(a, b, carry=carry, **kw)


def _layer_fwd(l, x, ada, w, small, ride, target=None):
    shift, scale, gate = ada[:, 0:D], ada[:, D:2 * D], ada[:, 2 * D:3 * D]
    carry, landed = ride("prenorm")
    (h, h_t), outs = _prenorm_fwd(x, small["g_pre"][l], scale, shift, f"prenorm_fwd{l}", carry)
    landed(outs)
    carry, landed = ride("proj")
    if isinstance(carry, _GatherInProj):
        proj, full = _proj_with_gather(h, carry.slot, carry.order, f"proj{l}")
        outs = [full]
    else:
        proj, outs = _mm_ride(h, w["w_in"][l], carry, name=f"proj{l}", b_mode="nn_sh", tm=2048, out_dtype=BF16)
    landed(outs)
    a_in, a_in_t = _pool_fwd(proj, small["pool_w"][l], small["pool_scale"][l], f"pool_fwd{l}")
    carry, landed = ride("hgrn")
    (b_in, b_in_t, o_raw, states, mild, cum), outs = _hgrn_fwd(proj, small["lb"][l], small["hgrn_norm_g"][l],
                                                              f"hgrn_fwd{l}", carry=carry)
    landed(outs)
    carry, landed = ride("tail")
    (br_a, br_b, merged_t, y, *x_new), outs = _layer_tail_fwd(
        proj, a_in, b_in, x, w["w_pool_o"][l], w["w_hgrn_o"][l].reshape(D, D), w["w_out"][l].reshape(D, D),
        gate, small["g_post"][l], f"tail_fwd{l}", target=target, carry=carry)
    landed(outs)
    saved = dict(x=x, h_t=h_t, proj=proj, a_in_t=a_in_t, b_in_t=b_in_t, o_raw=o_raw, states=states, mild=mild,
                 cum=cum,
                 br_a=br_a, br_b=br_b, merged_t=merged_t, y=y, scale=scale, gate=gate)
    return x_new, saved


def _layer_bwd(l, dxn, sv, w, small, ride):
    carry, landed = ride["head"](None)
    (dy, dbr_a, dbr_b, dproj, da_in, db_in, dgate, dg_post), outs = _layer_head_bwd(
        dxn, sv["y"], sv["proj"], sv["br_a"], sv["br_b"], w["w_pool_o"][l], w["w_hgrn_o"][l].reshape(D, D),
        w["w_out"][l].reshape(D, D), sv["gate"], small["g_post"][l], f"head_bwd{l}", carry)
    landed(outs)
    gw_out, gw_hgrn_o, gw_pool_o = _tail_weight_grads(sv["merged_t"], sv["b_in_t"], sv["a_in_t"], dy, dbr_b,
                                                      dbr_a, f"gw_tail{l}")
    big = dict(w_pool_o=gw_pool_o, w_hgrn_o=gw_hgrn_o.reshape(NCHIP, D // NCHIP, D),
               w_out=gw_out.reshape(NCHIP, D // NCHIP, D))
    carry, landed = ride["hgrn"](big)
    (dproj, dlb, dgn), outs = _hgrn_bwd(db_in, sv["proj"], sv["o_raw"], sv["states"], sv["mild"], sv["cum"],
                                        small["lb"][l], small["hgrn_norm_g"][l], dproj, f"hgrn_bwd{l}",
                                        carry=carry)
    landed(outs)
    dproj, dpw, dpsc = _pool_bwd(da_in, sv["proj"], small["pool_w"][l], small["pool_scale"][l], dproj,
                                 f"pool_bwd{l}")
    little = dict(dgate=dgate, g_post=dg_post, pool_w=dpw, pool_scale=dpsc, lb=dlb,
                  hgrn_norm_g=jnp.sum(dgn, axis=0, keepdims=True))
    carry, landed = ride["gw_in"](little)
    big["w_in"], outs = _mm_ride(sv["h_t"], dproj, carry, name=f"gw_in{l}", out_shards=NCHIP, out_dtype=BF16)
    landed(outs)
    carry, landed = ride["d_h"](big)
    dh, outs = _mm_ride(dproj, w["w_in"][l], carry, name=f"d_h{l}", b_mode="nt_shk", tn=1024)
    landed(outs)
    carry, landed = ride["prenorm"](big)
    (dx, dshift, dscale, dg_pre), outs = _prenorm_bwd(dh, dxn, sv["x"], small["g_pre"][l], sv["scale"],
                                                      f"prenorm_bwd{l}", carry)
    landed(outs)
    little.update(dshift=dshift, dscale=dscale, g_pre=dg_pre)
    return dx, big, little


SMALL_ROWS = 176


def _rows8(t):
    t = t.reshape(-1, D)
    return jnp.pad(t, ((0, -t.shape[0] % 8), (0, 0)))


def _pack_small(parts):
    row_keys = ("dshift", "dscale", "dgate", "g_pre", "g_post", "lb", "pool_scale", "hgrn_norm_g")
    flat = [p[k] for p in parts for k in row_keys] + [p["pool_w"] for p in parts]
    nk = len(row_keys)

    def body(*refs):
        o_ref = refs[-1]
        o_ref[...] = jnp.zeros((SMALL_ROWS, D), F32)
        for l in range(2):
            dshift, dscale, dgate, g_pre, g_post, lb, pscale, gn = refs[l * nk:(l + 1) * nk]
            for r, ref in enumerate((dshift, dscale, dgate)):
                o_ref[3 * l + r:3 * l + r + 1, :] = ref[...]
            o_ref[8 + l:9 + l, :] = g_pre[...]
            o_ref[16 + l:17 + l, :] = g_post[...]
            o_ref[24 + l:25 + l, :] = lb[...]
            o_ref[160:161, l * POOL_W:(l + 1) * POOL_W] = pscale[...]
            o_ref[168:169, l * HD:(l + 1) * HD] = gn[...]
        for (l, *at), (rows, lanes) in _small_pieces("pool_w"):
            o_ref[rows, lanes] = refs[2 * nk + l][tuple(at)]

    return pl.pallas_call(body, name="pack_small", out_shape=jax.ShapeDtypeStruct((SMALL_ROWS, D), F32),
                          compiler_params=_cp())(*flat)


def kernel(x, c, w_ada, b_ada, g_pre, g_post, w_in, pool_w, pool_scale, lb_logits, hgrn_norm_g, w_pool_o, w_hgrn_o, w_out, loss_target, m_w_ada, m_b_ada, m_g_pre, m_g_post, m_w_in, m_pool_w, m_pool_scale, m_lb_logits, m_hgrn_norm_g, m_w_pool_o, m_w_hgrn_o, m_w_out, v_w_ada, v_b_ada, v_g_pre, v_g_post, v_w_in, v_pool_w, v_pool_scale, v_lb_logits, v_hgrn_norm_g, v_w_pool_o, v_w_hgrn_o, v_w_out):
    ax, ay, ac = lax.axis_index("x"), lax.axis_index("y"), lax.axis_index("c")
    chip = 2 * ax + ay
    dev = 2 * chip + ac
    xe, te = x[0], loss_target[0]
    ada_s = w_ada.shape[2]

    big_names = ("w_in", "w_pool_o", "w_hgrn_o", "w_out")
    big_w = (w_in, w_pool_o, w_hgrn_o, w_out)
    core = jnp.stack([ac]).astype(jnp.int32)
    place = jnp.stack([chip, ac]).astype(jnp.int32)
    slots = {("w_in", l): _cast_to_slot(place, w_in, l, f"cast_w_in{l}") for l in range(2)}
    rest = [(k, l) for l in range(2) for k in big_names[1:]]
    slots.update(zip(rest, _cast_to_slots(place, [(dict(zip(big_names, big_w))[k], l) for k, l in rest],
                                          "cast_rest")))
    w = {k: [None, None] for k in big_names}
    def fills(keys):
        def landed(outs):
            for (k, l), o in zip(keys, outs):
                w[k][l] = slots[k, l] = o
        return landed

    rest0 = [(k, 0) for k in big_names[1:]]
    rest1 = [(k, 1) for k in big_names[1:]]
    no_carry = (None, lambda outs: None)
    order = jnp.stack([chip, 2 * (1 - ax) + ay, 2 * ax + (1 - ay), 2 * (1 - ax) + (1 - ay)]).astype(jnp.int32)

    def ride_fwd0(stage):
        if stage == "proj":
            return _GatherInProj(slots["w_in", 0], order), fills([("w_in", 0)])
        if stage == "hgrn":
            return (_join_carries(_gather_carry([slots[t] for t in rest0]),
                                  _gather_carry([slots["w_in", 1]], piece=(0, 2, 4), pass_on=False)),
                    fills(rest0 + [("w_in", 1)]))
        if stage == "tail":
            return (_gather_carry([slots["w_in", 1]], piece=(2, 1, 4), pass_on=False, late=(0, 2, 4)),
                    fills([("w_in", 1)]))
        return no_carry

    def ride_fwd1(stage):
        if stage == "prenorm":
            return _gather_carry([slots["w_in", 1]], piece=(3, 1, 4), late=(2, 1, 4)), fills([("w_in", 1)])
        if stage == "hgrn":
            return _gather_carry([slots[t] for t in rest1]), fills(rest1)
        return no_carry

    c_all = _gather_small(jnp.broadcast_to(c, (8, D)), "gather_c").reshape(NDEV, 8, D)[:, 0, :]
    c_pad = jnp.pad(c_all, ((0, ADA_PAD - NDEV), (0, 0)))
    b_sh = lax.dynamic_slice(b_ada, (0, chip * ada_s), (2, ada_s))
    ada_cols = _gather_small(_ada_fwd(c_pad, w_ada, b_sh), "gather_ada")
    ada_cols = ada_cols.reshape(NCHIP, 2, NDEV, 2, ada_s)[:, 0]
    ada_all = jnp.transpose(ada_cols, (2, 1, 0, 3)).reshape(2, NDEV, 3 * D)
    ada_me = lax.dynamic_slice(ada_all, (0, dev, 0), (2, 1, 3 * D))

    lbs = _lb_fwd(lb_logits)
    small = dict(g_pre=g_pre[:, None, :], g_post=g_post[:, None, :], pool_w=pool_w,
                 pool_scale=pool_scale[:, None, :], lb=lbs[:, None, :], hgrn_norm_g=hgrn_norm_g[:, None, :])

    (x1,), sv0 = _layer_fwd(0, xe, ada_me[0], w, small, ride_fwd0)
    (dx2, loss_blk), sv1 = _layer_fwd(1, x1, ada_me[1], w, small, ride_fwd1, target=te)

    parts, recv, held = {}, {}, {}

    def pair_ride(keys, grads):
        def landed(outs):
            held.update({kl: (g, o) for kl, g, o in zip(keys, grads, outs)})
        return _pair_carry(grads), landed

    def pair_adds(keys):
        gs, gots = zip(*[held.pop(kl) for kl in keys])
        if keys[0][0] == "w_in":
            parts[keys[0]] = _pair_add(core, gs[0], gots[0], f"rs_add_w_in{keys[0][1]}")
        else:
            parts.update(zip(keys, _pair_adds(core, gs, gots, f"rs_add_early{keys[0][1]}")))

    def exchange(keys):
        def landed(outs):
            recv.update(zip(keys, outs))
        return _chips_carry([parts[kl] for kl in keys]), landed

    def share(key, first, count):
        def landed(outs):
            (recv[key],) = outs
        into = [recv[key]] if key in recv else None
        return _chips_carry([parts[key]], piece=(first, count, 8), into=into), landed

    def together(*rides):
        carries, fns = zip(*rides)

        def landed(outs):
            for cr, fn in zip(carries, fns):
                fn(outs[:len(cr.outs)])
                outs = outs[len(cr.outs):]
        return _join_carries(*carries), landed

    def early(l):
        return [(k, l) for k in big_names[1:]]

    def pair_alone(keys, grads, tag):
        held.update({kl: (g, o) for kl, g, o in zip(keys, grads, _rs_pair(grads, f"rs_pair_{tag}"))})
        pair_adds(keys)

    def ride_hgrn1(big):
        return pair_ride(early(1), [big[k] for k in big_names[1:]])

    def ride_gw_in1(_):
        pair_adds(early(1))
        return exchange(early(1))

    def ride_d_h1(big):
        return pair_ride([("w_in", 1)], [big["w_in"]])

    def ride_prenorm1(_):
        pair_adds([("w_in", 1)])
        return share(("w_in", 1), 0, 1)

    def ride_head0(_):
        return share(("w_in", 1), 1, 3)

    def ride_hgrn0(big):
        pair_alone(early(0), [big[k] for k in big_names[1:]], "early0")
        return together(exchange(early(0)), share(("w_in", 1), 4, 4))

    def ride_d_h0(big):
        parts["w_in", 0] = _pair_sum(core, big["w_in"], "rs_pair_sum_w_in0")
        return share(("w_in", 0), 0, 4)

    def ride_prenorm0(_):
        return share(("w_in", 0), 4, 4)

    no_ride = lambda so_far: no_carry
    dx1, big1, little1 = _layer_bwd(1, dx2, sv1, w, small, dict(head=no_ride, hgrn=ride_hgrn1, gw_in=ride_gw_in1,
                                                                d_h=ride_d_h1, prenorm=ride_prenorm1))

    gathered = {}
    zero_row = jnp.zeros((1, D), F32)

    def ride_gw_in0(little):
        so_far = dict(little, dshift=zero_row, dscale=zero_row, g_pre=zero_row)

        def landed(outs):
            (gathered["early"],) = outs

        red = [_chip_sum(place, parts["w_in", 1], recv["w_in", 1], 1, None, "rs_sum_w_in1")]
        red += _chip_sums(place, [[parts[k, l] for l in range(2)] for k in big_names[1:]],
                          [[recv[k, l] for l in range(2)] for k in big_names[1:]], "rs_sum_early")

        def swapped(outs):
            gathered["sums"] = outs
        return together((_gather_rows_carry(_pack_small([so_far, little1])), landed),
                        (_swap_carry(red, [(1, 1)] + [(0, 2)] * 3), swapped))

    dx0, big0, little0 = _layer_bwd(0, dx1, sv0, w, small,
                                    dict(head=ride_head0, hgrn=ride_hgrn0, gw_in=ride_gw_in0, d_h=ride_d_h0,
                                         prenorm=ride_prenorm0))
    loss_row = jnp.broadcast_to(loss_blk[0:1, 0:1], (1, D))
    late = _rows8(jnp.stack([little0["dshift"], little0["dscale"], little0["g_pre"], loss_row]))
    late = _gather_small(late, "gather_small_late").reshape(NDEV, 8, D)
    loss = jnp.sum(late[:, 3, 0])
    red = _chip_sum(place, parts["w_in", 0], recv["w_in", 0], 0, gathered["sums"][0], "rs_sum_w_in0")
    g_big = dict(zip(big_names, list(_rs_swap([red], [(0, 1)])) + list(gathered["sums"][1:])))

    def two(t):
        return t.reshape(-1, t.shape[-1])

    def upd(wt, g, m, v, name, echo=False):
        return [t.reshape(wt.shape) for t in _adamw(two(wt), two(g), two(m), two(v), name, echo)]

    *u_w_in, g_w_in = upd(w_in, g_big["w_in"], m_w_in, v_w_in, "adamw_w_in", echo=True)
    g_small, pack_heads = _sum_devices(gathered["early"].reshape(NDEV, SMALL_ROWS, D), late)
    g_lb_logits = _lb_bwd(lb_logits, g_small[24:26])
    d_ada_all = pack_heads[:, 0:6, :].reshape(NDEV, 2, 3 * D)
    d_ada_sh = lax.dynamic_slice(jnp.transpose(d_ada_all, (1, 0, 2)), (0, 0, chip * ada_s), (2, NDEV, ada_s))
    d_ada_sh = jnp.pad(d_ada_sh, ((0, 0), (0, ADA_PAD - NDEV), (0, 0)))
    g_w_ada = _ada_wgrad(c_pad.T, d_ada_sh)

    u_w_ada = upd(w_ada, g_w_ada, m_w_ada, v_w_ada, "adamw_w_ada")
    early_w = dict(w_pool_o=(w_pool_o, m_w_pool_o, v_w_pool_o), w_hgrn_o=(w_hgrn_o, m_w_hgrn_o, v_w_hgrn_o),
                   w_out=(w_out, m_w_out, v_w_out))
    u_early = _adamw_many([(two(early_w[k][0]), two(g_big[k]), two(early_w[k][1]), two(early_w[k][2]))
                           for k in big_names[1:]], "adamw_early")
    (*u_w_pool_o, g_w_pool_o), (*u_w_hgrn_o, g_w_hgrn_o), (*u_w_out, g_w_out) = [
        [t.reshape(early_w[k][0].shape) for t in four] for k, four in zip(big_names[1:], u_early)]
    small_w = dict(b_ada=(b_ada, m_b_ada, v_b_ada), g_pre=(g_pre, m_g_pre, v_g_pre),
                   g_post=(g_post, m_g_post, v_g_post), lb_logits=(lb_logits, m_lb_logits, v_lb_logits),
                   pool_w=(pool_w, m_pool_w, v_pool_w), pool_scale=(pool_scale, m_pool_scale, v_pool_scale),
                   hgrn_norm_g=(hgrn_norm_g, m_hgrn_norm_g, v_hgrn_norm_g))
    u_small = _adamw_small(g_small, g_lb_logits, small_w)
    s = lambda key: u_small[key][3]
    grads_out = (g_w_ada, s("b_ada"), s("g_pre"), s("g_post"), g_w_in, s("pool_w"), s("pool_scale"), g_lb_logits,
                 s("hgrn_norm_g"), g_w_pool_o, g_w_hgrn_o, g_w_out)

    def ordered(k):
        s = lambda key: u_small[key][k]
        return (u_w_ada[k], s("b_ada"), s("g_pre"), s("g_post"), u_w_in[k], s("pool_w"), s("pool_scale"),
                s("lb_logits"), s("hgrn_norm_g"), u_w_pool_o[k], u_w_hgrn_o[k], u_w_out[k])

    return (loss, dx0[None], *grads_out, *ordered(0), *ordered(1), *ordered(2))
```

```python
import functools

import jax
import jax.numpy as jnp
from jax import lax
from jax.experimental import pallas as pl
from jax.experimental.pallas import tpu as pltpu

F32 = jnp.float32
BF16 = jnp.bfloat16
MESH = pl.DeviceIdType.MESH

D = 1024
HEADS = 8
HD = 128
GROUPS = 4
POOL_W = 512
CH = 128
SB_WIDE = 32
SB = 16
NH = 2
IN_W = 7168
NCHIP = 4
NDEV = 8
EPS = 1e-6
PV0, PG0, HQ0, HF0, HI0, HG0 = 0, 4, 8, 16, 24, 32
MGP_BLK, MGH_BLK = 5, 6

LR, B1, B2, AEPS, WD, STEP = 0.001, 0.9, 0.999, 1e-08, 0.01, 10
VMEM_LIMIT = 56 * 1024 * 1024


def _cp(sem=None, **kw):
    if sem is not None:
        kw["dimension_semantics"] = sem
    return pltpu.CompilerParams(vmem_limit_bytes=VMEM_LIMIT, **kw)


def _sig(z):
    return 1.0 / (1.0 + jnp.exp(-z))


def _dsilu(z, s):
    return s * (1.0 + z * (1.0 - s))


def _row_tile(rows, cap):
    if rows <= cap:
        return rows
    t = 1 << (cap.bit_length() - 1)
    while rows % t:
        t //= 2
    return t


ANY = pl.BlockSpec(memory_space=pl.ANY)


class _Carry:
    def __init__(self, ins, outs, aliases, n_sem, start, finish):
        self.ins, self.outs, self.aliases, self.n_sem = list(ins), list(outs), dict(aliases), n_sem
        self.start, self.finish = start, finish


class _SemWindow:
    def __init__(self, ref, base):
        self._ref, self._base = ref, base

    @property
    def at(self):
        return self

    def __getitem__(self, k):
        return self._ref.at[self._base + k]


def _join_carries(*carries):
    ins, outs, aliases, spans, n_sem = [], [], {}, [], 0
    for cr in carries:
        aliases.update({len(ins) + i: len(outs) + o for i, o in cr.aliases.items()})
        spans.append((len(ins), len(cr.ins), len(outs), len(cr.outs), n_sem))
        ins, outs, n_sem = ins + cr.ins, outs + cr.outs, n_sem + cr.n_sem

    def run(which):
        def fn(i_refs, o_refs, send_sems, recv_sems):
            for cr, (i0, ni, o0, no, s0) in zip(carries, spans):
                getattr(cr, which)(i_refs[i0:i0 + ni], o_refs[o0:o0 + no], _SemWindow(send_sems, s0),
                                   _SemWindow(recv_sems, s0))
        return fn

    return _Carry(ins, outs, aliases, n_sem, run("start"), run("finish"))


def _call(body, *, name, grid, in_specs, out_specs, out_shape, args, scratch_shapes=(), sem=None, carry=None,
          aliases=None):
    in_specs, out_specs, out_shape = list(in_specs), list(out_specs), list(out_shape)
    scratch_shapes = list(scratch_shapes)
    aliases = dict(aliases or {})
    if carry is None:
        outs = pl.pallas_call(body, name=name, grid=grid, in_specs=in_specs, out_specs=out_specs,
                              out_shape=out_shape, scratch_shapes=scratch_shapes, input_output_aliases=aliases,
                              compiler_params=_cp(sem))(*args)
        return list(outs)
    n_in, n_out, n_scr = len(in_specs), len(out_specs), len(scratch_shapes)
    c_in, c_out = len(carry.ins), len(carry.outs)

    def wrapped(*refs):
        k_in, rest = refs[:n_in], refs[n_in:]
        ci, rest = rest[:c_in], rest[c_in:]
        k_out, rest = rest[:n_out], rest[n_out:]
        co, rest = rest[:c_out], rest[c_out:]
        k_scr, (ssem, rsem) = rest[:n_scr], rest[n_scr:]
        pids = [pl.program_id(d) for d in range(len(grid))]
        first = functools.reduce(jnp.logical_and, [p == 0 for p in pids])
        last = functools.reduce(jnp.logical_and, [p == g - 1 for p, g in zip(pids, grid)])

        @pl.when(first)
        def _():
            carry.start(ci, co, ssem, rsem)

        body(*k_in, *k_out, *k_scr)

        @pl.when(last)
        def _():
            carry.finish(ci, co, ssem, rsem)

    outs = pl.pallas_call(
        wrapped, name=name, grid=grid, in_specs=in_specs + [ANY] * c_in, out_specs=out_specs + [ANY] * c_out,
        out_shape=out_shape + carry.outs,
        input_output_aliases={**aliases, **{n_in + i: n_out + o for i, o in carry.aliases.items()}},
        scratch_shapes=scratch_shapes + [pltpu.SemaphoreType.DMA((carry.n_sem,))] * 2,
        compiler_params=_cp(("arbitrary",) * len(grid)),
    )(*args, *carry.ins)
    return list(outs)


def _mm(a, b, *, name, b_mode="nn", out_shards=0, tm=1024, tn=256, tk=None, out_dtype=F32, carry=None):
    assert b_mode in ("nn", "nn_sh", "nt_shk"), b_mode
    M, K = a.shape
    if b_mode == "nn":
        N = b.shape[1]
    elif b_mode == "nn_sh":
        N = b.shape[0] * b.shape[2]
    else:
        N = b.shape[1]
    tm = _row_tile(M, tm)
    if b_mode == "nn_sh":
        tn = _row_tile(b.shape[2], tn)
    elif out_shards:
        tn = _row_tile(N // out_shards, tn)
    else:
        tn = _row_tile(N, tn)
    if tk is None:
        tk = K if b_mode != "nt_shk" else b.shape[2]
    if b_mode == "nt_shk":
        tk = _row_tile(b.shape[2], tk)
    nm, nn, nk = M // tm, N // tn, K // tk

    a_spec = pl.BlockSpec((tm, tk), lambda m, n, k: (m, k))
    if b_mode == "nn":
        b_spec = pl.BlockSpec((tk, tn), lambda m, n, k: (k, n))
    elif b_mode == "nn_sh":
        nps = b.shape[2] // tn
        b_spec = pl.BlockSpec((None, tk, tn), lambda m, n, k: (n // nps, k, n % nps))
    else:
        kps = b.shape[2] // tk
        b_spec = pl.BlockSpec((None, tn, tk), lambda m, n, k: (k // kps, n, k % kps))
    if out_shards:
        ops = (N // out_shards) // tn
        o_spec = pl.BlockSpec((None, tm, tn), lambda m, n, k: (n // ops, m, n % ops))
        o_shape = jax.ShapeDtypeStruct((out_shards, M, N // out_shards), out_dtype)
    else:
        o_spec = pl.BlockSpec((tm, tn), lambda m, n, k: (m, n))
        o_shape = jax.ShapeDtypeStruct((M, N), out_dtype)
    dn = (((1,), (1,)), ((), ())) if b_mode == "nt_shk" else (((1,), (0,)), ((), ()))

    def body(a_ref, b_ref, o_ref, acc_ref):
        k = pl.program_id(2)

        @pl.when(k == 0)
        def _():
            acc_ref[...] = jnp.zeros(acc_ref.shape, F32)

        acc_ref[...] += lax.dot_general(a_ref[...].astype(BF16), b_ref[...].astype(BF16), dn,
                                        preferred_element_type=F32)

        @pl.when(k == nk - 1)
        def _():
            o_ref[...] = acc_ref[...].astype(o_ref.dtype)

    outs = _call(body, name=name, grid=(nm, nn, nk), in_specs=[a_spec, b_spec], out_specs=[o_spec],
                 out_shape=[o_shape], scratch_shapes=[pltpu.VMEM((tm, tn), F32)],
                 sem=("parallel", "parallel", "arbitrary"), args=(a, b), carry=carry)
    return outs[0] if carry is None else (outs[0], outs[1:])


def _rowvec(n=D):
    return pl.BlockSpec((1, n), lambda i: (0, 0))


def _prenorm_fwd(x, g, scale, shift, name, carry=None):
    S = x.shape[0]
    tr = _row_tile(S, 256)

    def body(x_ref, g_ref, sc_ref, sh_ref, h_ref, ht_ref):
        xv = x_ref[...]
        r = lax.rsqrt(jnp.mean(xv * xv, axis=-1, keepdims=True) + EPS)
        hv = (xv * r) * g_ref[...] * (1.0 + sc_ref[...]) + sh_ref[...]
        h_ref[...] = hv.astype(BF16)
        ht_ref[...] = hv.T.astype(BF16)

    outs = _call(
        body, name=name, grid=(S // tr,),
        in_specs=[pl.BlockSpec((tr, D), lambda i: (i, 0)), _rowvec(), _rowvec(), _rowvec()],
        out_specs=[pl.BlockSpec((tr, D), lambda i: (i, 0)), pl.BlockSpec((D, tr), lambda i: (0, i))],
        out_shape=[jax.ShapeDtypeStruct((S, D), BF16), jax.ShapeDtypeStruct((D, S), BF16)],
        sem=("parallel",), args=(x, g, scale, shift), carry=carry)
    return outs[:2], outs[2:]


def _prenorm_bwd(dh, dxn, x, g, scale, name, carry=None):
    S = x.shape[0]
    tr = _row_tile(S, 256)

    def body(dh_ref, dxn_ref, x_ref, g_ref, sc_ref, dx_ref, dsh_ref, dsc_ref, dg_ref):
        i = pl.program_id(0)

        @pl.when(i == 0)
        def _():
            dsh_ref[...] = jnp.zeros((1, D), F32)
            dsc_ref[...] = jnp.zeros((1, D), F32)
            dg_ref[...] = jnp.zeros((1, D), F32)

        xv = x_ref[...]
        dhv = dh_ref[...]
        gv = g_ref[...]
        mod = 1.0 + sc_ref[...]
        r = lax.rsqrt(jnp.mean(xv * xv, axis=-1, keepdims=True) + EPS)
        xh = xv * r
        dsh_ref[...] += jnp.sum(dhv, axis=0, keepdims=True)
        dsc_ref[...] += jnp.sum(dhv * (xh * gv), axis=0, keepdims=True)
        dg_ref[...] += jnp.sum(dhv * mod * xh, axis=0, keepdims=True)
        u = dhv * mod * gv
        dx_ref[...] = dxn_ref[...] + r * u - xv * (r * r * r) * jnp.mean(u * xv, axis=-1, keepdims=True)

    tile = pl.BlockSpec((tr, D), lambda i: (i, 0))
    outs = _call(
        body, name=name, grid=(S // tr,),
        in_specs=[tile, tile, tile, _rowvec(), _rowvec()],
        out_specs=[tile, _rowvec(), _rowvec(), _rowvec()],
        out_shape=[jax.ShapeDtypeStruct((S, D), F32)] + [jax.ShapeDtypeStruct((1, D), F32)] * 3,
        sem=("arbitrary",), args=(dh, dxn, x, g, scale), carry=carry)
    return outs[:4], outs[4:]


def _layer_tail_fwd(proj, a_in, b_in, x, w_po, w_ho, w_out, gate, g, name, target=None, carry=None):
    S = proj.shape[0]
    tr = _row_tile(S, 256)
    nsh, _, wsh = w_po.shape
    n_in = 10 + (target is not None)

    def body(*refs):
        (mgp_ref, mgh_ref, a_ref, b_ref, x_ref, wpo_ref, who_ref, wout_ref, gate_ref, g_ref) = refs[:10]
        bra_ref, brb_ref, mt_ref, y_ref, xn_ref = refs[n_in:n_in + 5]
        av = a_ref[...]
        bra = jnp.concatenate([jnp.dot(av, wpo_ref[j], preferred_element_type=F32) for j in range(nsh)], axis=1)
        brb = jnp.dot(b_ref[...], who_ref[...], preferred_element_type=F32)
        mv = _sig(mgp_ref[...].astype(F32)) * bra + _sig(mgh_ref[...].astype(F32)) * brb
        bra_ref[...] = bra.astype(BF16)
        brb_ref[...] = brb.astype(BF16)
        mt_ref[...] = mv.T.astype(BF16)
        yv = jnp.dot(mv.astype(BF16), wout_ref[...], preferred_element_type=F32)
        y_ref[...] = yv
        r = lax.rsqrt(jnp.mean(yv * yv, axis=-1, keepdims=True) + EPS)
        xn = x_ref[...] + gate_ref[...] * ((yv * r) * g_ref[...])
        if target is None:
            xn_ref[...] = xn
        else:
            t_ref, l_ref = refs[10], refs[n_in + 5]

            @pl.when(pl.program_id(0) == 0)
            def _():
                l_ref[...] = jnp.zeros((8, 128), F32)

            err = xn - t_ref[...]
            xn_ref[...] = err * (1.0 / D)
            l_ref[...] += 0.5 * jnp.sum(jnp.mean(err * err, axis=-1, keepdims=True))

    tile = pl.BlockSpec((tr, D), lambda i: (i, 0))
    whole = lambda t: pl.BlockSpec(t.shape, lambda i: (0,) * t.ndim)
    last = target is not None
    outs = _call(
        body, name=name, grid=(S // tr,),
        in_specs=[pl.BlockSpec((tr, D), lambda i: (i, MGP_BLK)), pl.BlockSpec((tr, D), lambda i: (i, MGH_BLK)),
                  pl.BlockSpec((tr, POOL_W), lambda i: (i, 0)), tile, tile, whole(w_po), whole(w_ho),
                  whole(w_out), _rowvec(), _rowvec()] + [tile] * last,
        out_specs=[tile, tile, pl.BlockSpec((D, tr), lambda i: (0, i)), tile, tile]
        + [pl.BlockSpec((8, 128), lambda i: (0, 0))] * last,
        out_shape=[jax.ShapeDtypeStruct((S, D), BF16), jax.ShapeDtypeStruct((S, D), BF16),
                   jax.ShapeDtypeStruct((D, S), BF16), jax.ShapeDtypeStruct((S, D), F32),
                   jax.ShapeDtypeStruct((S, D), F32)] + [jax.ShapeDtypeStruct((8, 128), F32)] * last,
        sem=("arbitrary",) if last else ("parallel",),
        args=(proj, proj, a_in, b_in, x, w_po, w_ho, w_out, gate, g) + ((target,) if last else ()), carry=carry)
    return outs[:5 + last], outs[5 + last:]


def _layer_head_bwd(dxn, y, proj, br_a, br_b, w_po, w_ho, w_out, gate, g, name, carry=None):
    S = y.shape[0]
    tr = _row_tile(S, 256)
    nsh, _, wsh = w_po.shape

    def body(dxn_ref, y_ref, mgp_ref, mgh_ref, bra_ref, brb_ref, wpo_ref, who_ref, wout_ref, gate_ref, g_ref,
             dy_ref, dba_ref, dbb_ref, dproj_ref, dain_ref, dbin_ref, dgate_ref, dg_ref, dmgh_s):
        i = pl.program_id(0)
        j = pl.program_id(1)

        @pl.when((i == 0) & (j == 0))
        def _():
            dgate_ref[...] = jnp.zeros((1, D), F32)
            dg_ref[...] = jnp.zeros((1, D), F32)

        @pl.when(j == 1)
        def _():
            dproj_ref[...] = dmgh_s[...]

        @pl.when(j == 0)
        def _():
            everything(dxn_ref, y_ref, mgp_ref, mgh_ref, bra_ref, brb_ref, wpo_ref, who_ref, wout_ref, gate_ref,
                       g_ref, dy_ref, dba_ref, dbb_ref, dproj_ref, dain_ref, dbin_ref, dgate_ref, dg_ref, dmgh_s)

    def everything(dxn_ref, y_ref, mgp_ref, mgh_ref, bra_ref, brb_ref, wpo_ref, who_ref, wout_ref, gate_ref, g_ref,
                   dy_ref, dba_ref, dbb_ref, dproj_ref, dain_ref, dbin_ref, dgate_ref, dg_ref, dmgh_s):
        yv = y_ref[...]
        dv = dxn_ref[...]
        gv = g_ref[...]
        gt = gate_ref[...]
        r = lax.rsqrt(jnp.mean(yv * yv, axis=-1, keepdims=True) + EPS)
        yh = yv * r
        dgate_ref[...] += jnp.sum(dv * (yh * gv), axis=0, keepdims=True)
        dg_ref[...] += jnp.sum(dv * gt * yh, axis=0, keepdims=True)
        u = dv * gt * gv
        dy = (r * u - yv * (r * r * r) * jnp.mean(u * yv, axis=-1, keepdims=True)).astype(BF16)
        dy_ref[...] = dy
        dm = _dot_nt(dy, wout_ref[...])
        sp = _sig(mgp_ref[...].astype(F32))
        sh = _sig(mgh_ref[...].astype(F32))
        dba = (dm * sp).astype(BF16)
        dbb = (dm * sh).astype(BF16)
        dba_ref[...] = dba
        dbb_ref[...] = dbb
        dproj_ref[...] = (dm * bra_ref[...].astype(F32) * sp * (1.0 - sp)).astype(BF16)
        dmgh_s[...] = (dm * brb_ref[...].astype(F32) * sh * (1.0 - sh)).astype(BF16)
        dain = _dot_nt(dba[:, 0:wsh], wpo_ref[0])
        for k in range(1, nsh):
            dain = dain + _dot_nt(dba[:, k * wsh:(k + 1) * wsh], wpo_ref[k])
        dain_ref[...] = dain
        dbin_ref[...] = _dot_nt(dbb, who_ref[...])

    tile = pl.BlockSpec((tr, D), lambda i, j: (i, 0))
    whole = lambda t: pl.BlockSpec(t.shape, lambda i, j: (0,) * t.ndim)
    vec = pl.BlockSpec((1, D), lambda i, j: (0, 0))
    ahead = lambda i, j: jnp.minimum(i + j, S // tr - 1)
    tile_in = pl.BlockSpec((tr, D), lambda i, j: (ahead(i, j), 0))
    outs = _call(
        body, name=name, grid=(S // tr, 2),
        in_specs=[tile_in, tile_in, pl.BlockSpec((tr, D), lambda i, j: (ahead(i, j), MGP_BLK)),
                  pl.BlockSpec((tr, D), lambda i, j: (ahead(i, j), MGH_BLK)), tile_in, tile_in, whole(w_po),
                  whole(w_ho), whole(w_out), vec, vec],
        out_specs=[tile, tile, tile, pl.BlockSpec((tr, D), lambda i, j: (i, MGP_BLK + j)),
                   pl.BlockSpec((tr, POOL_W), lambda i, j: (i, 0)), tile, vec, vec],
        out_shape=[jax.ShapeDtypeStruct((S, D), BF16)] * 3
        + [jax.ShapeDtypeStruct((S, IN_W), BF16), jax.ShapeDtypeStruct((S, POOL_W), F32),
           jax.ShapeDtypeStruct((S, D), F32), jax.ShapeDtypeStruct((1, D), F32), jax.ShapeDtypeStruct((1, D), F32)],
        scratch_shapes=[pltpu.VMEM((tr, D), BF16)], sem=("arbitrary", "arbitrary"),
        args=(dxn, y, proj, proj, br_a, br_b, w_po, w_ho, w_out, gate, g), carry=carry)
    return outs[:8], outs[8:]


def _pool_pieces(u, g, S):
    rowi = lax.broadcasted_iota(jnp.int32, (S, 1), 0)

    def down(z, k):
        return jnp.where(rowi >= k, pltpu.roll(z, k, axis=0), 0.0)

    s2 = u + down(u, 1)
    s4 = s2 + down(s2, 2)
    s8 = s4 + down(s4, 4)
    s16 = s8 + down(s8, 8)
    win = jnp.where(g == 0, s2, jnp.where(g == 1, s4, jnp.where(g == 2, s8, s16)))
    w = jnp.where(g == 0, 2, jnp.where(g == 1, 4, jnp.where(g == 2, 8, 16)))
    count = jnp.minimum(rowi + 1, w).astype(F32)
    return win / count - u, count, rowi


def _pool_fwd(proj, pw, pscale, name):
    S = proj.shape[0]

    def body(pv_ref, pg_ref, pw_ref, sc_ref, a_ref, at_ref):
        g = pl.program_id(0)
        pooled, _, _ = _pool_pieces(pv_ref[...].astype(F32), g, S)
        pm = jnp.dot(pooled.astype(BF16), pw_ref[...].astype(BF16), preferred_element_type=F32)
        pgv = pg_ref[...].astype(F32)
        av = pm * sc_ref[...] * (pgv * _sig(pgv))
        a_ref[...] = av.astype(BF16)
        at_ref[...] = av.T.astype(BF16)

    outs = _call(
        body, name=name, grid=(GROUPS,),
        in_specs=[pl.BlockSpec((S, 128), lambda g: (0, PV0 + g)), pl.BlockSpec((S, 128), lambda g: (0, PG0 + g)),
                  pl.BlockSpec((None, 128, 128), lambda g: (g, 0, 0)), pl.BlockSpec((1, 128), lambda g: (0, g))],
        out_specs=[pl.BlockSpec((S, 128), lambda g: (0, g)), pl.BlockSpec((128, S), lambda g: (g, 0))],
        out_shape=[jax.ShapeDtypeStruct((S, POOL_W), BF16), jax.ShapeDtypeStruct((POOL_W, S), BF16)],
        sem=("parallel",), args=(proj, proj, pw, pscale))
    return outs


def _pool_bwd(da, proj, pw, pscale, dproj, name):
    S = proj.shape[0]

    def body(da_ref, pv_ref, pg_ref, pw_ref, sc_ref, dproj_in, dproj_ref, dpw_ref, dsc_ref, dpg_s):
        @pl.when(pl.program_id(1) == 1)
        def _():
            dproj_ref[...] = dpg_s[...]

        @pl.when(pl.program_id(1) == 0)
        def _():
            group(da_ref, pv_ref, pg_ref, pw_ref, sc_ref, dproj_ref, dpg_s, dpw_ref, dsc_ref)

    def group(da_ref, pv_ref, pg_ref, pw_ref, sc_ref, dpv_ref, dpg_ref, dpw_ref, dsc_ref):
        g = pl.program_id(0)
        pooled, count, rowi = _pool_pieces(pv_ref[...].astype(F32), g, S)
        pwb = pw_ref[...].astype(BF16)
        pm = jnp.dot(pooled.astype(BF16), pwb, preferred_element_type=F32)
        scv = sc_ref[...]
        pgv = pg_ref[...].astype(F32)
        sg = _sig(pgv)
        dav = da_ref[...]
        d_ps = dav * (pgv * sg)
        dpg_ref[...] = (dav * (pm * scv) * _dsilu(pgv, sg)).astype(BF16)
        dsc_ref[...] = jnp.sum(d_ps * pm, axis=0, keepdims=True)
        d_pm = (d_ps * scv).astype(BF16)
        dpw_ref[...] = lax.dot_general(pooled.astype(BF16), d_pm, (((0,), (0,)), ((), ())),
                                       preferred_element_type=F32)
        d_pooled = lax.dot_general(d_pm, pwb, (((1,), (1,)), ((), ())), preferred_element_type=F32)
        z = d_pooled / count

        def up(v, k):
            return jnp.where(rowi < S - k, pltpu.roll(v, S - k, axis=0), 0.0)

        t2 = z + up(z, 1)
        t4 = t2 + up(t2, 2)
        t8 = t4 + up(t4, 4)
        t16 = t8 + up(t8, 8)
        adj = jnp.where(g == 0, t2, jnp.where(g == 1, t4, jnp.where(g == 2, t8, t16)))
        dpv_ref[...] = (adj - d_pooled).astype(BF16)

    col = lambda g, j: (0, g)
    ahead = lambda g, j: jnp.minimum(g + j, GROUPS - 1)
    return pl.pallas_call(
        body, name=name, grid=(GROUPS, 2),
        in_specs=[pl.BlockSpec((S, 128), lambda g, j: (0, ahead(g, j))),
                  pl.BlockSpec((S, 128), lambda g, j: (0, PV0 + ahead(g, j))),
                  pl.BlockSpec((S, 128), lambda g, j: (0, PG0 + ahead(g, j))),
                  pl.BlockSpec((None, 128, 128), lambda g, j: (ahead(g, j), 0, 0)),
                  pl.BlockSpec((1, 128), lambda g, j: (0, ahead(g, j))), ANY],
        out_specs=[pl.BlockSpec((S, 128), lambda g, j: (0, PV0 + g + (PG0 - PV0) * j)),
                   pl.BlockSpec((None, 128, 128), lambda g, j: (g, 0, 0)), pl.BlockSpec((1, 128), col)],
        out_shape=[jax.ShapeDtypeStruct(dproj.shape, dproj.dtype),
                   jax.ShapeDtypeStruct((GROUPS, 128, 128), F32), jax.ShapeDtypeStruct((1, POOL_W), F32)],
        scratch_shapes=[pltpu.VMEM((S, 128), BF16)], input_output_aliases={5: 0},
        compiler_params=_cp(("arbitrary", "arbitrary")),
    )(da, proj, proj, pw, pscale, dproj)


SCAN_SHIFTS = tuple(1 << b for b in range(CH.bit_length() - 1))


def _chunk_cumsum(z, rowi):
    for sh in SCAN_SHIFTS:
        z = z + jnp.where(rowi >= sh, pltpu.roll(z, sh, axis=0), 0.0)
    return z


def _chunk_rev_cumsum(z, rowi):
    for sh in SCAN_SHIFTS:
        z = z + jnp.where(rowi < CH - sh, pltpu.roll(z, CH - sh, axis=0), 0.0)
    return z


def _dot_nn(a, b):
    return jnp.dot(a.astype(BF16), b.astype(BF16), preferred_element_type=F32)


def _dot_nt(a, b):
    return lax.dot_general(a.astype(BF16), b.astype(BF16), (((1,), (1,)), ((), ())), preferred_element_type=F32)


def _dot_tn(a, b):
    return lax.dot_general(a.astype(BF16), b.astype(BF16), (((0,), (0,)), ((), ())), preferred_element_type=F32)


def _gates(hq, hf, lbv):
    hq, hf = hq.astype(F32), hf.astype(F32)
    sq = _sig(hq)
    sf = _sig(hf)
    f = lbv + (1.0 - lbv) * sf
    fc = jnp.maximum(f, 1e-30)
    return hq * sq, sq, sf, f, fc, jnp.log(fc)


DECAY_CAP = 60.0


def _block_ref(c_ref, i, sb):
    if i == 0:
        return jnp.zeros((1, HD), F32)
    return c_ref[sb * i - 1:sb * i, :]


def _block_decay(c_ref, sb):
    spans = [_block_ref(c_ref, i, sb) - c_ref[sb * (i + 1) - 1:sb * (i + 1), :] for i in range(CH // sb)]
    return functools.reduce(jnp.maximum, spans)


def _pair_factors(q_ref, k, c_ref, first, cap, round_bf16, sb):
    nb = CH // sb
    c = c_ref[...]
    zero = jnp.zeros((sb, HD), F32)
    q_groups, k_groups, eqs, eks = [], [], [], []
    for i in range(first, nb):
        blk = slice(sb * i, sb * (i + 1))
        r_i = _block_ref(c_ref, i, sb)
        eq = jnp.exp(jnp.minimum(c_ref[blk, :] - r_i, 0.0))
        ek = jnp.exp(jnp.minimum(r_i - c, cap))
        qi, kei = q_ref[blk, :] * eq, k * ek
        if round_bf16:
            qi, kei = qi.astype(BF16).astype(F32), kei.astype(BF16).astype(F32)
        q_groups.append(jnp.concatenate([zero] * i + [qi] + [zero] * (nb - 1 - i), axis=0))
        k_groups.append(kei)
        eqs.append(eq)
        eks.append(ek)
    return jnp.concatenate(q_groups, axis=1), jnp.concatenate(k_groups, axis=1), eqs, eks


def _pair_mask(rowi, coli, strict, sb):
    return (coli < jnp.bitwise_and(rowi, -sb)) if strict else (coli <= rowi)


def _hgrn_fwd(proj, lb, gn, name, carry=None):
    S = proj.shape[0]
    nch = S // CH
    W = NH * HD

    def body(hq_ref, hf_ref, hi_ref, hg_ref, lb_ref, gn_ref, bin_ref, bint_ref, oraw_ref, st_ref, mild_ref,
             cum_ref, q_s, k_s, c_s, v_s, o_s, state_s, qf_s, kf_s, cf_s):
        state_s[...] = jnp.zeros((NH, HD, HD), F32)
        rowi = lax.broadcasted_iota(jnp.int32, (CH, 1), 0)
        coli = lax.broadcasted_iota(jnp.int32, (1, CH), 1)
        sbi = lax.broadcasted_iota(jnp.int32, (SB, 1), 0)
        gnv = gn_ref[...]

        def gates_pass(n, worst):
            wide, narrow = worst
            rows = pl.ds(pl.multiple_of(n * CH, CH), CH)
            for hh in range(NH):
                lanes = slice(hh * HD, (hh + 1) * HD)
                q, _, _, f, _, logf = _gates(hq_ref[rows, lanes], hf_ref[rows, lanes], lb_ref[:, lanes])
                c = _chunk_cumsum(logf, rowi)
                qf_s[hh, rows, :] = q
                kf_s[hh, rows, :] = 1.0 - f
                cf_s[hh, rows, :] = c
                cum_ref[rows, lanes] = c
                c_s[hh] = c
                wide = jnp.maximum(wide, _block_decay(c_s.at[hh], SB_WIDE))
                narrow = jnp.maximum(narrow, _block_decay(c_s.at[hh], SB))
            return wide, narrow

        def between_chunks(hh, n, rows):
            lanes = slice(hh * HD, (hh + 1) * HD)
            q = qf_s[hh, rows, :]
            k = kf_s[hh, rows, :]
            c = cf_s[hh, rows, :]
            v = hi_ref[rows, lanes].astype(F32)
            q_s[hh] = q
            k_s[hh] = k
            c_s[hh] = c
            v_s[hh] = v
            st = state_s[hh]
            st_ref[hh, n] = st.astype(BF16)
            o_s[hh] = _dot_nt(q * jnp.exp(c), st)
            last = c_s[hh, CH - 1:CH, :]
            state_s[hh] = st * jnp.exp(last) + _dot_tn(v, k * jnp.exp(last - c))

        def pairs_matmul(hh, first, cap, strict, sb):
            qx, kc, _, _ = _pair_factors(q_s.at[hh], k_s[hh], c_s.at[hh], first, cap, False, sb)
            a = jnp.where(_pair_mask(rowi, coli, strict, sb), _dot_nt(qx, kc), 0.0)
            o_s[hh] += _dot_nn(a, v_s[hh])

        def within_chunk_matmul(sb):
            return lambda hh: pairs_matmul(hh, 0, DECAY_CAP, False, sb)

        def within_chunk_exact(hh):
            pairs_matmul(hh, 1, 0.0, True, SB)
            for i in range(CH // SB):
                blk = slice(SB * i, SB * (i + 1))
                qb = q_s[hh, blk, :]
                cb = c_s[hh, blk, :]
                acc = jnp.zeros((SB, HD), F32)
                for s in range(SB):
                    row = SB * i + s
                    w = jnp.exp(jnp.minimum(cb - c_s[hh, row:row + 1, :], 0.0))
                    a_col = jnp.sum(qb * k_s[hh, row:row + 1, :] * w, axis=-1, keepdims=True)
                    acc = acc + jnp.where(sbi >= s, a_col, 0.0) * v_s[hh, row:row + 1, :]
                o_s[hh, blk, :] += acc

        def norm_and_gate(hh, rows):
            lanes = slice(hh * HD, (hh + 1) * HD)
            ov = o_s[hh]
            oraw_ref[rows, lanes] = ov
            r = lax.rsqrt(jnp.mean(ov * ov, axis=-1, keepdims=True) + EPS)
            hg = hg_ref[rows, lanes].astype(F32)
            bin_ref[rows, lanes] = ((ov * r) * gnv * (hg * _sig(hg))).astype(BF16)

        def chunk_with(within_chunk):
            def chunk(n, carry):
                rows = pl.ds(pl.multiple_of(n * CH, CH), CH)
                for hh in range(NH):
                    between_chunks(hh, n, rows)
                for hh in range(NH):
                    within_chunk(hh)
                for hh in range(NH):
                    norm_and_gate(hh, rows)
                return carry
            return chunk

        none = jnp.zeros((1, HD), F32)
        wide, narrow = lax.fori_loop(0, nch, gates_pass, (none, none))
        tier = jnp.where(jnp.max(wide) <= DECAY_CAP, 2.0, jnp.where(jnp.max(narrow) <= DECAY_CAP, 1.0, 0.0))
        mild_ref[...] = jnp.broadcast_to(tier, (8, HD))

        @pl.when(tier == 2.0)
        def _():
            lax.fori_loop(0, nch, chunk_with(within_chunk_matmul(SB_WIDE)), 0, unroll=4)

        @pl.when(tier == 1.0)
        def _():
            lax.fori_loop(0, nch, chunk_with(within_chunk_matmul(SB)), 0, unroll=2)

        @pl.when(tier == 0.0)
        def _():
            lax.fori_loop(0, nch, chunk_with(within_chunk_exact), 0)

        bint_ref[...] = bin_ref[...].astype(F32).T.astype(BF16)

    col = lambda off: pl.BlockSpec((S, W), lambda h: (0, off // NH + h))
    head = pl.BlockSpec((S, W), lambda h: (0, h))
    outs = _call(
        body, name=name, grid=(HEADS // NH,),
        in_specs=[col(HQ0), col(HF0), col(HI0), col(HG0), pl.BlockSpec((1, W), lambda h: (0, h)),
                  pl.BlockSpec((1, HD), lambda h: (0, 0))],
        out_specs=[head, pl.BlockSpec((W, S), lambda h: (h, 0)), head,
                   pl.BlockSpec((NH, nch, HD, HD), lambda h: (h, 0, 0, 0)),
                   pl.BlockSpec((8, HD), lambda h: (h, 0)), head],
        out_shape=[jax.ShapeDtypeStruct((S, D), BF16), jax.ShapeDtypeStruct((D, S), BF16),
                   jax.ShapeDtypeStruct((S, D), F32), jax.ShapeDtypeStruct((HEADS, nch, HD, HD), BF16),
                   jax.ShapeDtypeStruct((8 * HEADS // NH, HD), F32), jax.ShapeDtypeStruct((S, D), F32)],
        scratch_shapes=[pltpu.VMEM((NH, CH, HD), F32)] * 5 + [pltpu.VMEM((NH, HD, HD), F32)]
        + [pltpu.VMEM((NH, S, HD), F32)] * 3,
        sem=("parallel",), args=(proj, proj, proj, proj, lb, gn), carry=carry)
    return outs[:6], outs[6:]


def _hgrn_bwd(dbin, proj, oraw, states, mild, cum, lb, gn, dproj, name, carry=None):
    S = proj.shape[0]
    nch = S // CH
    W = NH * HD
    n_in = 12

    def body(*refs):
        ins, (dproj_ref, dlb_ref, dgn_ref) = refs[:n_in - 1], refs[n_in:n_in + 3]
        scratch, later = refs[n_in + 3:-3], refs[-3:]
        seg = pl.program_id(1)

        @pl.when(seg == 0)
        def _():
            heads(*ins, dproj_ref, *later, dlb_ref, dgn_ref, *scratch)

        for s, kept in enumerate(later):
            @pl.when(seg == s + 1)
            def _(kept=kept):
                dproj_ref[...] = kept[...]

    def heads(db_ref, hq_ref, hf_ref, hi_ref, hg_ref, or_ref, st_ref, mild_ref, cum_ref, lb_ref, gn_ref,
              dq_ref, df_ref, di_ref, dg_ref, dlb_ref, dgn_ref,
              q_s, k_s, c_s, v_s, do_s, dq_s, dk_s, dv_s, dc_s, dqd_s, dkd_s, f_s, sf_s, sq_s, dl_s, dst_s,
              dlb_s, dgn_s):
        dst_s[...] = jnp.zeros((NH, HD, HD), F32)
        dlb_s[...] = jnp.zeros((1, W), F32)
        dgn_s[...] = jnp.zeros((1, HD), F32)
        rowi = lax.broadcasted_iota(jnp.int32, (CH, 1), 0)
        coli = lax.broadcasted_iota(jnp.int32, (1, CH), 1)
        sbi = lax.broadcasted_iota(jnp.int32, (SB, 1), 0)
        gnv = gn_ref[...]
        def between_chunks(hh, n, rows):
            lanes = slice(hh * HD, (hh + 1) * HD)
            lbv = lb_ref[:, lanes]
            hq = hq_ref[rows, lanes].astype(F32)
            sq = _sig(hq)
            sf = _sig(hf_ref[rows, lanes].astype(F32))
            f = lbv + (1.0 - lbv) * sf
            q = hq * sq
            k = 1.0 - f
            f_s[hh] = f
            sf_s[hh] = sf
            sq_s[hh] = sq
            v = hi_ref[rows, lanes].astype(F32)
            c = cum_ref[rows, lanes]
            ov = or_ref[rows, lanes]
            hg = hg_ref[rows, lanes].astype(F32)
            sg = _sig(hg)
            r = lax.rsqrt(jnp.mean(ov * ov, axis=-1, keepdims=True) + EPS)
            dbv = db_ref[rows, lanes]
            d_on = dbv * (hg * sg)
            dg_ref[rows, lanes] = (dbv * ((ov * r) * gnv) * _dsilu(hg, sg)).astype(BF16)
            dgn_s[...] += jnp.sum(d_on * (ov * r), axis=0, keepdims=True)
            u = d_on * gnv
            do = r * u - ov * (r * r * r) * jnp.mean(u * ov, axis=-1, keepdims=True)
            q_s[hh] = q
            k_s[hh] = k
            c_s[hh] = c
            v_s[hh] = v
            do_s[hh] = do
            st = st_ref[hh, n].astype(F32)
            dst = dst_s[hh]
            ec = jnp.exp(c)
            last = c_s[hh, CH - 1:CH, :]
            el = jnp.exp(last - c)
            elast = jnp.exp(last)
            dq = _dot_nn(do, st) * ec
            dk = _dot_nn(v, dst) * el
            dq_s[hh] = dq
            dk_s[hh] = dk
            dv_s[hh] = _dot_nt(k * el, dst)
            dc_s[hh] = q * dq - k * dk
            dl_s[hh] = (jnp.sum(k * dk, axis=0, keepdims=True)
                        + elast * jnp.sum(st * dst, axis=0, keepdims=True))
            dst_s[hh] = dst * elast + _dot_tn(do, q * ec)

        def pairs_matmul(hh, first, cap, strict, sb):
            do = do_s[hh]
            qx, kc, eqs, eks = _pair_factors(q_s.at[hh], k_s[hh], c_s.at[hh], first, cap, True, sb)
            mask = _pair_mask(rowi, coli, strict, sb)
            a = jnp.where(mask, _dot_nt(qx, kc), 0.0)
            d_a = jnp.where(mask, _dot_nt(do, v_s[hh]).astype(BF16).astype(F32), 0.0)
            dqx = _dot_nn(d_a, kc)
            dkc = _dot_tn(d_a, qx)
            dv_s[hh] += _dot_tn(a, do)
            dk, dcum = dk_s[hh], dc_s[hh]
            dq_slabs = [jnp.zeros((sb, HD), F32)] * first
            dc_slabs = [jnp.zeros((sb, HD), F32)] * first
            for g, (eq, ek) in enumerate(zip(eqs, eks)):
                rows = slice(sb * (first + g), sb * (first + g + 1))
                cols = slice(HD * g, HD * (g + 1))
                dq_i = dqx[rows, cols]
                dk_i = dkc[:, cols]
                dq_slabs.append(dq_i * eq)
                dc_slabs.append(qx[rows, cols] * dq_i)
                dk = dk + dk_i * ek
                dcum = dcum - kc[:, cols] * dk_i
            dq_s[hh] += jnp.concatenate(dq_slabs, axis=0)
            dk_s[hh] = dk
            dc_s[hh] = dcum + jnp.concatenate(dc_slabs, axis=0)

        def pairs_exact(hh):
            dqd_s[hh] = jnp.zeros((CH, HD), F32)
            dkd_s[hh] = jnp.zeros((CH, HD), F32)
            for i in range(CH // SB):
                blk = slice(SB * i, SB * (i + 1))
                qb = q_s[hh, blk, :]
                cb = c_s[hh, blk, :]
                dob = do_s[hh, blk, :]
                dq_acc = jnp.zeros((SB, HD), F32)
                for s in range(SB):
                    row = SB * i + s
                    ks = k_s[hh, row:row + 1, :]
                    vs = v_s[hh, row:row + 1, :]
                    w = jnp.exp(jnp.minimum(cb - c_s[hh, row:row + 1, :], 0.0))
                    live = sbi >= s
                    a_col = jnp.where(live, jnp.sum(qb * ks * w, axis=-1, keepdims=True), 0.0)
                    da_col = jnp.where(live, jnp.sum(dob * vs, axis=-1, keepdims=True), 0.0)
                    dq_acc = dq_acc + da_col * ks * w
                    dkd_s[hh, row:row + 1, :] += jnp.sum(da_col * qb * w, axis=0, keepdims=True)
                    dv_s[hh, row:row + 1, :] += jnp.sum(a_col * dob, axis=0, keepdims=True)
                dqd_s[hh, blk, :] += dq_acc
            dq_d = dqd_s[hh]
            dk_d = dkd_s[hh]
            dq_s[hh] += dq_d
            dk_s[hh] += dk_d
            dc_s[hh] += q_s[hh] * dq_d - k_s[hh] * dk_d

        def gate_grads(hh, rows):
            lanes = slice(hh * HD, (hh + 1) * HD)
            lbv = lb_ref[:, lanes]
            hq = hq_ref[rows, lanes].astype(F32)
            f, sf, sq = f_s[hh], sf_s[hh], sq_s[hh]
            dlogf = _chunk_rev_cumsum(dc_s[hh], rowi) + dl_s[hh]
            dfv = jnp.where(f > 1e-30, dlogf / jnp.maximum(f, 1e-30), 0.0) - dk_s[hh]
            dlb_s[:, lanes] += jnp.sum(dfv * (1.0 - sf), axis=0, keepdims=True)
            df_ref[rows, lanes] = (dfv * (1.0 - lbv) * sf * (1.0 - sf)).astype(BF16)
            dq_ref[rows, lanes] = (dq_s[hh] * _dsilu(hq, sq)).astype(BF16)
            di_ref[rows, lanes] = dv_s[hh].astype(BF16)

        def chunk_with(pairs):
            def chunk(j, carry):
                n = nch - 1 - j
                rows = pl.ds(pl.multiple_of(n * CH, CH), CH)
                for hh in range(NH):
                    between_chunks(hh, n, rows)
                for hh in range(NH):
                    pairs(hh)
                for hh in range(NH):
                    gate_grads(hh, rows)
                return carry
            return chunk

        def pairs_mild(sb):
            return lambda hh: pairs_matmul(hh, 0, DECAY_CAP, False, sb)

        def pairs_any(hh):
            pairs_matmul(hh, 1, 0.0, True, SB)
            pairs_exact(hh)

        tier = jnp.max(mild_ref[...])

        @pl.when(tier == 2.0)
        def _():
            lax.fori_loop(0, nch, chunk_with(pairs_mild(SB_WIDE)), 0, unroll=2)

        @pl.when(tier == 1.0)
        def _():
            lax.fori_loop(0, nch, chunk_with(pairs_mild(SB)), 0)

        @pl.when(tier == 0.0)
        def _():
            lax.fori_loop(0, nch, chunk_with(pairs_any), 0)

        dlb_ref[...] = dlb_s[...]
        dgn_ref[...] = jnp.broadcast_to(dgn_s[...], (8, HD))

    ahead = lambda h, s: jnp.minimum(h + jnp.minimum(s, 1), HEADS // NH - 1)
    col = lambda off: pl.BlockSpec((S, W), lambda h, s: (0, off // NH + ahead(h, s)))
    head_in = pl.BlockSpec((S, W), lambda h, s: (0, ahead(h, s)))
    vec_in = pl.BlockSpec((1, W), lambda h, s: (0, ahead(h, s)))
    vec = pl.BlockSpec((1, W), lambda h, s: (0, h))
    seg_w = (HF0 - HQ0) // NH
    outs = _call(
        body, name=name, grid=(HEADS // NH, 4),
        in_specs=[head_in, col(HQ0), col(HF0), col(HI0), col(HG0), head_in,
                  pl.BlockSpec((NH, nch, HD, HD), lambda h, s: (ahead(h, s), 0, 0, 0)),
                  pl.BlockSpec((8, HD), lambda h, s: (ahead(h, s), 0)), head_in, vec_in,
                  pl.BlockSpec((1, HD), lambda h, s: (0, 0)), ANY],
        out_specs=[pl.BlockSpec((S, W), lambda h, s: (0, HQ0 // NH + seg_w * s + h)), vec,
                   pl.BlockSpec((8, HD), lambda h, s: (h, 0))],
        out_shape=[jax.ShapeDtypeStruct(dproj.shape, dproj.dtype), jax.ShapeDtypeStruct((1, D), F32),
                   jax.ShapeDtypeStruct((8 * HEADS // NH, HD), F32)],
        scratch_shapes=[pltpu.VMEM((NH, CH, HD), F32)] * 14
        + [pltpu.VMEM((NH, 1, HD), F32), pltpu.VMEM((NH, HD, HD), F32), pltpu.VMEM((1, W), F32),
           pltpu.VMEM((1, HD), F32)] + [pltpu.VMEM((S, W), BF16)] * 3,
        sem=("arbitrary", "arbitrary"), aliases={n_in - 1: 0},
        args=(dbin, proj, proj, proj, proj, oraw, states, mild, cum, lb, gn, dproj), carry=carry)
    dproj, dlb, dgn = outs[:3]
    return (dproj, dlb, dgn.reshape(HEADS // NH, 8, HD)[:, 0, :]), outs[3:]


def _lower_bounds(l0, l1):
    m = jnp.maximum(l0, l1)
    e0 = jnp.exp(l0 - m)
    e1 = jnp.exp(l1 - m)
    tot = e0 + e1
    p0 = e0 / tot
    p1 = e1 / tot
    return jnp.clip(p0 - p0, 0.0, 1.0), jnp.clip((p0 + p1) - p0, 0.0, 1.0)


def _lb_fwd(logits):
    def body(l_ref, o_ref):
        lb0, lb1 = _lower_bounds(l_ref[0:1, :], l_ref[1:2, :])
        o_ref[0:1, :] = lb0
        o_ref[1:2, :] = lb1

    return pl.pallas_call(body, name="lb_fwd", out_shape=jax.ShapeDtypeStruct((2, D), F32))(logits)


def _lb_bwd(logits, dlb):
    def body(l_ref, d_ref, o_ref):
        _, vjp = jax.vjp(_lower_bounds, l_ref[0:1, :], l_ref[1:2, :])
        g0, g1 = vjp((d_ref[0:1, :], d_ref[1:2, :]))
        o_ref[0:1, :] = g0
        o_ref[1:2, :] = g1

    return pl.pallas_call(body, name="lb_bwd", out_shape=jax.ShapeDtypeStruct((2, D), F32))(logits, dlb)


ADA_PAD = 128


def _ada_fwd(c_pad, w_ada, b_sh):
    ns = w_ada.shape[2]

    def body(c_ref, w_ref, b_ref, o_ref):
        cv = c_ref[...]
        ca = (cv * _sig(cv)).astype(BF16)
        for l in range(2):
            res = jnp.dot(ca, w_ref[l].astype(BF16), preferred_element_type=F32)
            o_ref[:, l * ns:(l + 1) * ns] = res[0:NDEV, :] + b_ref[l:l + 1, :]

    return pl.pallas_call(body, name="ada_fwd", out_shape=jax.ShapeDtypeStruct((NDEV, 2 * ns), F32),
                          compiler_params=_cp())(c_pad, w_ada, b_sh)


def _ada_wgrad(c_pad_t, d_ada_sh):
    ns = d_ada_sh.shape[2]

    def body(c_ref, d_ref, o_ref):
        cv = c_ref[...]
        ca = (cv * _sig(cv)).astype(BF16)
        for l in range(2):
            o_ref[l] = jnp.dot(ca, d_ref[l].astype(BF16), preferred_element_type=F32)

    return pl.pallas_call(body, name="ada_wgrad", out_shape=jax.ShapeDtypeStruct((2, D, ns), F32),
                          compiler_params=_cp())(c_pad_t, d_ada_sh)


def _sum_devices(g, late):
    _, R, C = g.shape

    def body(g_ref, l_ref, o_ref, head_ref):
        acc, acc_late = g_ref[0], l_ref[0]
        for d in range(1, NDEV):
            acc, acc_late = acc + g_ref[d], acc_late + l_ref[d]
        o_ref[...] = acc
        o_ref[0:2, :] = acc_late[0:2]
        o_ref[8:9, :] = acc_late[2:3]
        for d in range(NDEV):
            head_ref[d] = g_ref[d, 0:8, :]
            head_ref[d, 0:2, :] = l_ref[d, 0:2, :]

    return pl.pallas_call(body, name="sum_devices",
                          out_shape=[jax.ShapeDtypeStruct((R, C), F32), jax.ShapeDtypeStruct((NDEV, 8, C), F32)],
                          compiler_params=_cp())(g, late)


def _adamw(w, g, m, v, name, echo=False):
    R, C = w.shape
    tr = _row_tile(R, max(8, (1 << 19) // C))

    def body(w_ref, g_ref, m_ref, v_ref, d_ref, nm_ref, nv_ref, *g_out):
        d_ref[...], nm_ref[...], nv_ref[...] = _adamw_update(w_ref[...], g_ref[...], m_ref[...], v_ref[...])
        for o_ref in g_out:
            o_ref[...] = g_ref[...]

    tile = pl.BlockSpec((tr, C), lambda i: (i, 0))
    n_out = 4 if echo else 3
    return _call(body, name=name, grid=(R // tr,), in_specs=[tile] * 4, out_specs=[tile] * n_out,
                 out_shape=[jax.ShapeDtypeStruct((R, C), F32)] * n_out, sem=("parallel",), args=(w, g, m, v))


def _adamw_many(wgmv, name, steps=4):
    n = len(wgmv)

    def body(*refs):
        ins, outs = refs[:4 * n], refs[4 * n:]
        for a in range(n):
            w_ref, g_ref, m_ref, v_ref = ins[4 * a:4 * a + 4]
            d_ref, nm_ref, nv_ref, g_out = outs[4 * a:4 * a + 4]
            d_ref[...], nm_ref[...], nv_ref[...] = _adamw_update(w_ref[...], g_ref[...], m_ref[...], v_ref[...])
            g_out[...] = g_ref[...]

    def tile(t):
        return pl.BlockSpec((t.shape[0] // steps, t.shape[1]), lambda i: (i, 0))

    flat = [t for four in wgmv for t in four]
    outs = pl.pallas_call(
        body, name=name, grid=(steps,), in_specs=[tile(t) for t in flat],
        out_specs=[tile(four[0]) for four in wgmv for _ in range(4)],
        out_shape=[jax.ShapeDtypeStruct(four[0].shape, F32) for four in wgmv for _ in range(4)],
        compiler_params=_cp(("parallel",)))(*flat)
    return [outs[4 * a:4 * a + 4] for a in range(n)]


def _adamw_update(w, g, m, v):
    nm = B1 * m + (1.0 - B1) * g
    nv = B2 * v + (1.0 - B2) * (g * g)
    m_hat = nm / (1.0 - B1 ** STEP)
    v_hat = nv / (1.0 - B2 ** STEP)
    return -LR * (m_hat / (jnp.sqrt(v_hat) + AEPS) + WD * w), nm, nv


SMALL_KEYS = ("b_ada", "g_pre", "g_post", "lb_logits", "pool_w", "pool_scale", "hgrn_norm_g")


def _small_pieces(key):
    one = lambda i: slice(i, i + 1)
    if key == "b_ada":
        return [((one(l), slice(j * D, (j + 1) * D)), (one(3 * l + j), slice(0, D)))
                for l in range(2) for j in range(3)]
    if key in ("g_pre", "g_post", "lb_logits"):
        row0 = {"g_pre": 8, "g_post": 16, "lb_logits": 24}[key]
        return [((slice(0, 2), slice(0, D)), (slice(row0, row0 + 2), slice(0, D)))]
    if key == "pool_w":
        return [((l, g, pl.ds(k, 16, stride=8), slice(0, 128)),
                 (slice(32 + 64 * l + 16 * g, 48 + 64 * l + 16 * g), slice(128 * k, 128 * (k + 1))))
                for l in range(2) for g in range(GROUPS) for k in range(8)]
    width = {"pool_scale": POOL_W, "hgrn_norm_g": HD}[key]
    row = {"pool_scale": 160, "hgrn_norm_g": 168}[key]
    return [((one(l), slice(0, width)), (one(row), slice(l * width, (l + 1) * width))) for l in range(2)]


def _adamw_small(g_small, g_lb_logits, wmv):
    n = len(SMALL_KEYS)

    def body(g_ref, glb_ref, *refs):
        ins, outs = refs[:3 * n], refs[3 * n:]
        for p, key in enumerate(SMALL_KEYS):
            w_ref, m_ref, v_ref = ins[3 * p:3 * p + 3]
            for at, (rows, lanes) in _small_pieces(key):
                gv = glb_ref[at] if key == "lb_logits" else g_ref[rows, lanes]
                res = _adamw_update(w_ref[at], gv, m_ref[at], v_ref[at])
                for o_ref, val in zip(outs[4 * p:4 * p + 4], (*res, gv)):
                    o_ref[at] = val

    flat = [t for key in SMALL_KEYS for t in wmv[key]]
    outs = pl.pallas_call(body, name="adamw_small",
                          out_shape=[jax.ShapeDtypeStruct(wmv[key][0].shape, F32) for key in SMALL_KEYS
                                     for _ in range(4)],
                          compiler_params=_cp())(g_small, g_lb_logits, *flat)
    return {key: outs[4 * p:4 * p + 4] for p, key in enumerate(SMALL_KEYS)}


def _cast_to_slot(place, w, l, name):
    _, R, C = w.shape
    tr = _row_tile(R, max(8, (1 << 19) // C))

    def body(p_ref, w_ref, o_ref):
        o_ref[...] = w_ref[...].astype(BF16)

    return pl.pallas_call(
        body, name=name, out_shape=jax.ShapeDtypeStruct((NCHIP, R, C), BF16),
        grid_spec=pltpu.PrefetchScalarGridSpec(
            num_scalar_prefetch=1, grid=(R // tr,),
            in_specs=[pl.BlockSpec((None, tr, C), lambda i, p_ref: (l, i, 0))],
            out_specs=pl.BlockSpec((None, tr, C), lambda i, p_ref: (p_ref[0], i, 0))),
        compiler_params=_cp(("parallel",)),
    )(place, w)


def _cast_to_slots(place, ws, name):
    n = len(ws)

    def body(p_ref, *refs):
        for w_ref, o_ref in zip(refs[:n], refs[n:]):
            o_ref[...] = w_ref[...].astype(BF16)

    def layer(l):
        return lambda i, p_ref: (l, 0, 0)

    return pl.pallas_call(
        body, name=name, out_shape=[jax.ShapeDtypeStruct((NCHIP,) + w.shape[1:], BF16) for w, _ in ws],
        grid_spec=pltpu.PrefetchScalarGridSpec(
            num_scalar_prefetch=1, grid=(1,),
            in_specs=[pl.BlockSpec((None,) + w.shape[1:], layer(l)) for w, l in ws],
            out_specs=[pl.BlockSpec((None,) + w.shape[1:], lambda i, p_ref: (p_ref[0], 0, 0)) for w, _ in ws]),
        compiler_params=_cp(("arbitrary",)),
    )(place, *[w for w, _ in ws])


def _pair_adds(core, gs, gots, name):
    n = len(gs)

    def body(c_ref, *refs):
        for a_ref, b_ref, o_ref in zip(refs[:n], refs[n:2 * n], refs[2 * n:]):
            o_ref[...] = (a_ref[...].astype(F32) + b_ref[...].astype(F32)).astype(o_ref.dtype)

    def whole(t):
        return pl.BlockSpec(t.shape, lambda i, c_ref: (0, 0, 0))

    return pl.pallas_call(
        body, name=name, out_shape=[jax.ShapeDtypeStruct(t.shape, BF16) for t in gots],
        grid_spec=pltpu.PrefetchScalarGridSpec(
            num_scalar_prefetch=1, grid=(1,),
            in_specs=[pl.BlockSpec(t.shape, lambda i, c_ref: (0, c_ref[0], 0)) for t in gots]
            + [whole(t) for t in gots],
            out_specs=[whole(t) for t in gots]),
        compiler_params=_cp(("arbitrary",)),
    )(core, *gs, *gots)


def _pair_add(core, g, got, name):
    _, R, C = g.shape
    r2 = R // 2
    tr = _row_tile(r2, max(8, (1 << 19) // C))
    nt = r2 // tr

    def body(c_ref, a_ref, b_ref, o_ref):
        o_ref[...] = (a_ref[...].astype(F32) + b_ref[...].astype(F32)).astype(o_ref.dtype)

    return pl.pallas_call(
        body, name=name, out_shape=jax.ShapeDtypeStruct((NCHIP, r2, C), BF16),
        grid_spec=pltpu.PrefetchScalarGridSpec(
            num_scalar_prefetch=1, grid=(NCHIP, nt),
            in_specs=[pl.BlockSpec((None, tr, C), lambda j, i, c_ref: (j, c_ref[0] * nt + i, 0)),
                      pl.BlockSpec((None, tr, C), lambda j, i, c_ref: (j, i, 0))],
            out_specs=pl.BlockSpec((None, tr, C), lambda j, i, c_ref: (j, i, 0))),
        compiler_params=_cp(("parallel", "parallel")),
    )(core, g, got)


def _sum_in_chip_order(me, own_ref, r_ref):
    own = own_ref[...].astype(F32)
    acc = None
    for j in range(NCHIP):
        slot = jnp.minimum(jnp.where(j > me, j - 1, j), NCHIP - 2)
        term = jnp.where(me == j, own, r_ref[slot].astype(F32))
        acc = term if acc is None else acc + term
    return acc


def _chip_sums(place, parts, recvs, name):
    n = len(parts)

    def body(p_ref, *refs):
        for a in range(n):
            for l in range(2):
                refs[4 * n + a][l] = _sum_in_chip_order(p_ref[0], refs[2 * a + l], refs[2 * n + 2 * a + l])

    def own(t):
        return pl.BlockSpec((None,) + t.shape[1:], lambda i, p_ref: (p_ref[0], 0, 0))

    def whole(t):
        return pl.BlockSpec(t.shape, lambda i, p_ref: (0, 0, 0))

    flat_p = [t for pair in parts for t in pair]
    flat_r = [t for pair in recvs for t in pair]
    return pl.pallas_call(
        body, name=name,
        out_shape=[jax.ShapeDtypeStruct((2, 2 * p[0].shape[1], p[0].shape[2]), F32) for p in parts],
        grid_spec=pltpu.PrefetchScalarGridSpec(
            num_scalar_prefetch=1, grid=(1,),
            in_specs=[own(t) for t in flat_p] + [whole(t) for t in flat_r],
            out_specs=[pl.BlockSpec((2,) + p[0].shape[1:], lambda i, p_ref: (0, p_ref[1], 0)) for p in parts]),
        compiler_params=_cp(("arbitrary",)),
    )(place, *flat_p, *flat_r)


def _chip_sum(place, part, recv, layer, both, name):
    _, r2, C = part.shape
    tr = _row_tile(r2, max(8, (1 << 18) // C))
    nt = r2 // tr

    def body(p_ref, own_ref, r_ref, *rest):
        rest[-1][...] = _sum_in_chip_order(p_ref[0], own_ref, r_ref)

    args = (place, part, recv) if both is None else (place, part, recv, both)
    return pl.pallas_call(
        body, name=name, out_shape=jax.ShapeDtypeStruct((2, 2 * r2, C), F32),
        grid_spec=pltpu.PrefetchScalarGridSpec(
            num_scalar_prefetch=1, grid=(nt,),
            in_specs=[pl.BlockSpec((None, tr, C), lambda i, p_ref: (p_ref[0], i, 0)),
                      pl.BlockSpec((NCHIP - 1, tr, C), lambda i, p_ref: (0, i, 0))] + [ANY] * (len(args) - 3),
            out_specs=pl.BlockSpec((None, tr, C), lambda i, p_ref: (layer, p_ref[1] * nt + i, 0))),
        input_output_aliases={} if both is None else {3: 0},
        compiler_params=_cp(("parallel",)),
    )(*args)


def _place():
    x, y, c = lax.axis_index("x"), lax.axis_index("y"), lax.axis_index("c")
    chips = [(1 - x, y), (x, 1 - y), (1 - x, 1 - y)]
    return x, y, c, chips


def _gather_small(blk, name):
    m_per, n = blk.shape

    def body(x_ref, out_ref, send_sems, recv_sems, local_sem):
        x, y, c, chips = _place()
        me, sibling = (x, y, c), (x, y, 1 - c)

        def rows(px, py, pc):
            return out_ref.at[pl.ds((4 * px + 2 * py + pc) * m_per, m_per), :]

        def copy(k, block, to, src=None):
            return pltpu.make_async_remote_copy(
                src_ref=rows(*block) if src is None else src, dst_ref=rows(*block),
                send_sem=send_sems.at[k], recv_sem=recv_sems.at[k], device_id=to, device_id_type=MESH)

        mine = pltpu.make_async_copy(x_ref, rows(*me), local_sem)
        mine.start()
        first = [copy(0, me, sibling, src=x_ref)]
        first += [copy(1 + j, me, (*chip, c), src=x_ref) for j, chip in enumerate(chips)]
        for cp in first:
            cp.start()
        passed = [copy(4 + j, (*chip, c), sibling) for j, chip in enumerate(chips)]
        for j, chip in enumerate(chips):
            copy(1 + j, (*chip, c), me).wait_recv()
            passed[j].start()
        copy(0, sibling, me).wait_recv()
        for j, chip in enumerate(chips):
            copy(4 + j, (*chip, 1 - c), me).wait_recv()
        for cp in first + passed:
            cp.wait_send()
        mine.wait()

    return pl.pallas_call(
        body, name=name, out_shape=jax.ShapeDtypeStruct((NDEV * m_per, n), blk.dtype),
        in_specs=[pl.BlockSpec(memory_space=pltpu.VMEM)], out_specs=pl.BlockSpec(memory_space=pltpu.VMEM),
        scratch_shapes=[pltpu.SemaphoreType.DMA((7,)), pltpu.SemaphoreType.DMA((7,)), pltpu.SemaphoreType.DMA],
        compiler_params=_cp(),
    )(blk)


def _gather_rows_carry(blk):
    m_per, n = blk.shape

    def rows(ref, px, py, pc):
        return ref.at[pl.ds((4 * px + 2 * py + pc) * m_per, m_per), :]

    def copy(ins, outs, send_sems, recv_sems, k, block, to, own=False):
        return pltpu.make_async_remote_copy(
            src_ref=ins[0] if own else rows(outs[0], *block), dst_ref=rows(outs[0], *block),
            send_sem=send_sems.at[k], recv_sem=recv_sems.at[k], device_id=to, device_id_type=MESH)

    def mine(ins, outs, send_sems):
        x, y, c, _ = _place()
        return pltpu.make_async_copy(ins[0], rows(outs[0], x, y, c), send_sems.at[7])

    def start(ins, outs, send_sems, recv_sems):
        x, y, c, chips = _place()
        mine(ins, outs, send_sems).start()
        copy(ins, outs, send_sems, recv_sems, 0, (x, y, c), (x, y, 1 - c), own=True).start()
        for j, chip in enumerate(chips):
            copy(ins, outs, send_sems, recv_sems, 1 + j, (x, y, c), (*chip, c), own=True).start()

    def finish(ins, outs, send_sems, recv_sems):
        x, y, c, chips = _place()
        for j, chip in enumerate(chips):
            copy(ins, outs, send_sems, recv_sems, 1 + j, (*chip, c), (x, y, c)).wait_recv()
            copy(ins, outs, send_sems, recv_sems, 4 + j, (*chip, c), (x, y, 1 - c)).start()
        copy(ins, outs, send_sems, recv_sems, 0, (x, y, 1 - c), (x, y, c)).wait_recv()
        for j, chip in enumerate(chips):
            copy(ins, outs, send_sems, recv_sems, 4 + j, (*chip, 1 - c), (x, y, c)).wait_recv()
        copy(ins, outs, send_sems, recv_sems, 0, (x, y, c), (x, y, 1 - c), own=True).wait_send()
        for j, chip in enumerate(chips):
            copy(ins, outs, send_sems, recv_sems, 1 + j, (x, y, c), (*chip, c), own=True).wait_send()
            copy(ins, outs, send_sems, recv_sems, 4 + j, (*chip, c), (x, y, 1 - c)).wait_send()
        mine(ins, outs, send_sems).wait()

    return _Carry([blk], [jax.ShapeDtypeStruct((NDEV * m_per, n), blk.dtype)], {}, 8, start, finish)


def _gather_carry(shards, piece=(0, 1, 1), pass_on=True, late=None):
    n = len(shards)

    def rows(ref, half, pc):
        first, count, of = pc
        r2 = ref.shape[1] // 2
        return pl.ds(half * r2 + first * (r2 // of), count * (r2 // of))

    def over_ici(outs, send_sems, recv_sems, a, j, chip_xy, slot):
        x, y, c, _ = _place()
        blk = outs[a].at[slot, rows(outs[a], c, piece), :]
        return pltpu.make_async_remote_copy(
            src_ref=blk, dst_ref=blk, send_sem=send_sems.at[9 * a + j], recv_sem=recv_sems.at[9 * a + j],
            device_id=(*chip_xy, c), device_id_type=MESH)

    def over_d2d(outs, send_sems, recv_sems, a, j, slot, half, pc):
        x, y, c, _ = _place()
        blk = outs[a].at[slot, rows(outs[a], half, pc), :]
        k = 9 * a + (3 if pc is piece else 6) + j
        return pltpu.make_async_remote_copy(
            src_ref=blk, dst_ref=blk, send_sem=send_sems.at[k], recv_sem=recv_sems.at[k],
            device_id=(x, y, 1 - c), device_id_type=MESH)

    def start(ins, outs, send_sems, recv_sems):
        x, y, c, chips = _place()
        for a in range(n):
            for j, (cx, cy) in enumerate(chips):
                over_ici(outs, send_sems, recv_sems, a, j, (cx, cy), 2 * x + y).start()
                if late:
                    over_d2d(outs, send_sems, recv_sems, a, j, 2 * cx + cy, c, late).start()

    def finish(ins, outs, send_sems, recv_sems):
        x, y, c, chips = _place()
        passed = ([piece] if pass_on else []) + ([late] if late else [])
        for a in range(n):
            for j, (cx, cy) in enumerate(chips):
                over_ici(outs, send_sems, recv_sems, a, j, (cx, cy), 2 * cx + cy).wait_recv()
                if pass_on:
                    over_d2d(outs, send_sems, recv_sems, a, j, 2 * cx + cy, c, piece).start()
        for a in range(n):
            for j, (cx, cy) in enumerate(chips):
                for pc in passed:
                    over_d2d(outs, send_sems, recv_sems, a, j, 2 * cx + cy, 1 - c, pc).wait_recv()
        for a in range(n):
            for j, (cx, cy) in enumerate(chips):
                over_ici(outs, send_sems, recv_sems, a, j, (cx, cy), 2 * x + y).wait_send()
                for pc in passed:
                    over_d2d(outs, send_sems, recv_sems, a, j, 2 * cx + cy, c, pc).wait_send()

    return _Carry(shards, [jax.ShapeDtypeStruct(s.shape, s.dtype) for s in shards],
                  {a: a for a in range(n)}, 9 * n, start, finish)


def _rs_pair(grads, name):
    n = len(grads)

    def body(*refs):
        ins, gots = refs[:n], refs[n:2 * n]
        send_sems, recv_sems = refs[2 * n:]
        x, y, c, _ = _place()
        cps = []
        for a in range(n):
            r2 = ins[a].shape[1] // 2
            cp = pltpu.make_async_remote_copy(
                src_ref=ins[a].at[:, pl.ds((1 - c) * r2, r2), :], dst_ref=gots[a],
                send_sem=send_sems.at[a], recv_sem=recv_sems.at[a],
                device_id=(x, y, 1 - c), device_id_type=MESH)
            cp.start()
            cps.append(cp)
        for cp in cps:
            cp.wait()

    half = [jax.ShapeDtypeStruct((NCHIP, g.shape[1] // 2, g.shape[2]), g.dtype) for g in grads]
    return pl.pallas_call(
        body, name=name, out_shape=half, in_specs=[ANY] * n, out_specs=[ANY] * n,
        scratch_shapes=[pltpu.SemaphoreType.DMA((n,)), pltpu.SemaphoreType.DMA((n,))],
        compiler_params=_cp(),
    )(*grads)


def _pair_sum(core, g, name):
    _, R, C = g.shape
    r2 = R // 2

    def body(c_ref, g_ref, own_ref, o_ref, got_ref, buf, send_sems, recv_sems, local_sems):
        j = pl.program_id(0)
        x, y, c, _ = _place()

        def remote(k):
            return pltpu.make_async_remote_copy(
                src_ref=g_ref.at[k, pl.ds((1 - c) * r2, r2), :], dst_ref=got_ref.at[k],
                send_sem=send_sems.at[k], recv_sem=recv_sems.at[k], device_id=(x, y, 1 - c), device_id_type=MESH)

        def fetch(k):
            return pltpu.make_async_copy(got_ref.at[k], buf.at[k % 2], local_sems.at[k % 2])

        @pl.when(j == 0)
        def _():
            for k in range(NCHIP):
                remote(k).start()
            remote(0).wait_recv()
            fetch(0).start()

        fetch(j).wait()

        @pl.when(j + 1 < NCHIP)
        def _():
            remote(j + 1).wait_recv()
            fetch(j + 1).start()

        o_ref[...] = (own_ref[...].astype(F32) + buf[j % 2].astype(F32)).astype(o_ref.dtype)

        @pl.when(j == NCHIP - 1)
        def _():
            for k in range(NCHIP):
                remote(k).wait_send()

    half = jax.ShapeDtypeStruct((NCHIP, r2, C), g.dtype)
    return pl.pallas_call(
        body, name=name, out_shape=[half, half],
        grid_spec=pltpu.PrefetchScalarGridSpec(
            num_scalar_prefetch=1, grid=(NCHIP,),
            in_specs=[ANY, pl.BlockSpec((None, r2, C), lambda j, c_ref: (j, c_ref[0], 0))],
            out_specs=[pl.BlockSpec((None, r2, C), lambda j, c_ref: (j, 0, 0)), ANY],
            scratch_shapes=[pltpu.VMEM((2, r2, C), g.dtype), pltpu.SemaphoreType.DMA((NCHIP,)),
                            pltpu.SemaphoreType.DMA((NCHIP,)), pltpu.SemaphoreType.DMA((2,))]),
        compiler_params=_cp(("arbitrary",)),
    )(core, g, g)[0]


def _pair_carry(grads):
    n = len(grads)

    def copy(ins, outs, send_sems, recv_sems, a):
        x, y, c, _ = _place()
        r2 = ins[a].shape[1] // 2
        return pltpu.make_async_remote_copy(
            src_ref=ins[a].at[:, pl.ds((1 - c) * r2, r2), :], dst_ref=outs[a],
            send_sem=send_sems.at[a], recv_sem=recv_sems.at[a],
            device_id=(x, y, 1 - c), device_id_type=MESH)

    def start(ins, outs, send_sems, recv_sems):
        for a in range(n):
            copy(ins, outs, send_sems, recv_sems, a).start()

    def finish(ins, outs, send_sems, recv_sems):
        for a in range(n):
            copy(ins, outs, send_sems, recv_sems, a).wait()

    half = [jax.ShapeDtypeStruct((NCHIP, g.shape[1] // 2, g.shape[2]), g.dtype) for g in grads]
    return _Carry(grads, half, {}, n, start, finish)


def _chips_carry(parts, piece=(0, 1, 1), into=None):
    n = len(parts)
    first, count, of = piece

    def rows(ref):
        step = ref.shape[1] // of
        return pl.ds(first * step, count * step)

    def send(ins, outs, send_sems, recv_sems, a, j, chip_xy):
        x, y, c, _ = _place()
        me, them = 2 * x + y, 2 * chip_xy[0] + chip_xy[1]
        return pltpu.make_async_remote_copy(
            src_ref=ins[a].at[them, rows(ins[a]), :],
            dst_ref=outs[a].at[me - (me > them).astype(jnp.int32), rows(outs[a]), :],
            send_sem=send_sems.at[3 * a + j], recv_sem=recv_sems.at[3 * a + j],
            device_id=(*chip_xy, c), device_id_type=MESH)

    def start(ins, outs, send_sems, recv_sems):
        _, _, _, chips = _place()
        for a in range(n):
            for j, chip_xy in enumerate(chips):
                send(ins, outs, send_sems, recv_sems, a, j, chip_xy).start()

    def finish(ins, outs, send_sems, recv_sems):
        x, y, c, chips = _place()
        me = 2 * x + y
        for a in range(n):
            for j, (cx, cy) in enumerate(chips):
                them = 2 * cx + cy
                blk = outs[a].at[them - (them > me).astype(jnp.int32), rows(outs[a]), :]
                pltpu.make_async_remote_copy(
                    src_ref=blk, dst_ref=blk, send_sem=send_sems.at[3 * a + j], recv_sem=recv_sems.at[3 * a + j],
                    device_id=(cx, cy, c), device_id_type=MESH).wait_recv()
        for a in range(n):
            for j, chip_xy in enumerate(chips):
                send(ins, outs, send_sems, recv_sems, a, j, chip_xy).wait_send()

    landing = [jax.ShapeDtypeStruct((NCHIP - 1,) + p.shape[1:], p.dtype) for p in parts]
    if into is None:
        return _Carry(parts, landing, {}, 3 * n, start, finish)
    return _Carry(list(parts) + list(into), landing, {n + a: a for a in range(n)}, 3 * n, start, finish)


def _swap_carry(fulls, layers):
    n = len(fulls)

    def copy(outs, send_sems, recv_sems, a, half):
        x, y, c, _ = _place()
        r2 = outs[a].shape[1] // 2
        blk = outs[a].at[pl.ds(*layers[a]), pl.ds(half * r2, r2), :]
        return pltpu.make_async_remote_copy(
            src_ref=blk, dst_ref=blk, send_sem=send_sems.at[a], recv_sem=recv_sems.at[a],
            device_id=(x, y, 1 - c), device_id_type=MESH)

    def start(ins, outs, send_sems, recv_sems):
        c = lax.axis_index("c")
        for a in range(n):
            copy(outs, send_sems, recv_sems, a, c).start()

    def finish(ins, outs, send_sems, recv_sems):
        c = lax.axis_index("c")
        for a in range(n):
            copy(outs, send_sems, recv_sems, a, 1 - c).wait_recv()
        for a in range(n):
            copy(outs, send_sems, recv_sems, a, c).wait_send()

    return _Carry(fulls, [jax.ShapeDtypeStruct(f.shape, f.dtype) for f in fulls], {a: a for a in range(n)}, n,
                  start, finish)


def _rs_swap(fulls, layers):
    n = len(fulls)
    swap = _swap_carry(fulls, layers)

    def body(*refs):
        outs, (send_sems, recv_sems) = refs[n:2 * n], refs[2 * n:]
        swap.start(None, outs, send_sems, recv_sems)
        swap.finish(None, outs, send_sems, recv_sems)

    return pl.pallas_call(
        body, name="rs_swap", out_shape=swap.outs, in_specs=[ANY] * n, out_specs=[ANY] * n,
        input_output_aliases=swap.aliases,
        scratch_shapes=[pltpu.SemaphoreType.DMA((n,)), pltpu.SemaphoreType.DMA((n,))],
        compiler_params=_cp(),
    )(*fulls)


def _tail_weight_grads(merged_t, b_in_t, a_in_t, dy, dbr_b, dbr_a, name, tn=256):
    S = dy.shape[0]
    nn = D // tn

    def body(mt_ref, bt_ref, at_ref, dy_ref, db_ref, da_ref, go_ref, gh_ref, gp_ref):
        go_ref[...] = jnp.dot(mt_ref[...], dy_ref[...], preferred_element_type=F32).astype(BF16)
        gh_ref[...] = jnp.dot(bt_ref[...], db_ref[...], preferred_element_type=F32).astype(BF16)
        gp_ref[...] = jnp.dot(at_ref[...], da_ref[...], preferred_element_type=F32).astype(BF16)

    left = lambda rows: pl.BlockSpec((rows, S), lambda n: (0, 0))
    right = pl.BlockSpec((S, tn), lambda n: (0, n))
    out = pl.BlockSpec((D, tn), lambda n: (0, n))
    return pl.pallas_call(
        body, name=name, grid=(nn,), in_specs=[left(D), left(D), left(POOL_W), right, right, right],
        out_specs=[out, out, pl.BlockSpec((None, POOL_W, tn), lambda n: (n, 0, 0))],
        out_shape=[jax.ShapeDtypeStruct((D, D), BF16), jax.ShapeDtypeStruct((D, D), BF16),
                   jax.ShapeDtypeStruct((NCHIP, POOL_W, D // NCHIP), BF16)],
        compiler_params=_cp(("parallel",)),
    )(merged_t, b_in_t, a_in_t, dy, dbr_b, dbr_a)


class _GatherInProj:
    def __init__(self, slot, order):
        self.slot, self.order = slot, order


def _proj_with_gather(h, w_slot, order, name, tn=256):
    S, K = h.shape
    nsh, _, ns = w_slot.shape
    tps = ns // tn
    nt = nsh * tps
    r2 = K // 2

    def body(ord_ref, h_ref, w_in_ref, o_ref, w_ref, wbuf, tile_sems, send_sems, recv_sems):
        n = pl.program_id(0)
        x, y, c, chips = _place()

        def half(slot, which):
            return w_ref.at[slot, pl.ds(which * r2, r2), :]

        def over_ici(j, slot):
            blk = half(slot, c)
            return pltpu.make_async_remote_copy(src_ref=blk, dst_ref=blk, send_sem=send_sems.at[j],
                                                recv_sem=recv_sems.at[j], device_id=(*chips[j], c),
                                                device_id_type=MESH)

        def over_d2d(j, which):
            blk = half(2 * chips[j][0] + chips[j][1], which)
            return pltpu.make_async_remote_copy(src_ref=blk, dst_ref=blk, send_sem=send_sems.at[3 + j],
                                                recv_sem=recv_sems.at[3 + j], device_id=(x, y, 1 - c),
                                                device_id_type=MESH)

        def tile_copy(step, slot):
            shard = ord_ref[step // tps]
            return pltpu.make_async_copy(w_ref.at[shard, :, pl.ds((step % tps) * tn, tn)], wbuf.at[slot],
                                         tile_sems.at[slot])

        @pl.when(n == 0)
        def _():
            for j in range(3):
                over_ici(j, 2 * x + y).start()
            tile_copy(0, 0).start()

        for j in range(3):
            @pl.when(n == (j + 1) * tps - 1)
            def _(j=j):
                over_ici(j, 2 * chips[j][0] + chips[j][1]).wait_recv()
                over_d2d(j, c).start()
                over_d2d(j, 1 - c).wait_recv()

        @pl.when(n + 1 < nt)
        def _():
            tile_copy(n + 1, (n + 1) % 2).start()

        tile_copy(n, n % 2).wait()
        o_ref[...] = jnp.dot(h_ref[...], wbuf[n % 2], preferred_element_type=F32).astype(o_ref.dtype)

        @pl.when(n == nt - 1)
        def _():
            for j in range(3):
                over_ici(j, 2 * x + y).wait_send()
                over_d2d(j, c).wait_send()

    return pl.pallas_call(
        body, name=name,
        out_shape=[jax.ShapeDtypeStruct((S, nsh * ns), BF16), jax.ShapeDtypeStruct(w_slot.shape, w_slot.dtype)],
        grid_spec=pltpu.PrefetchScalarGridSpec(
            num_scalar_prefetch=1, grid=(nt,),
            in_specs=[pl.BlockSpec((S, K), lambda n, o_ref: (0, 0)), ANY],
            out_specs=[pl.BlockSpec((S, tn), lambda n, o_ref: (0, o_ref[n // tps] * tps + n % tps)), ANY],
            scratch_shapes=[pltpu.VMEM((2, K, tn), w_slot.dtype), pltpu.SemaphoreType.DMA((2,)),
                            pltpu.SemaphoreType.DMA((6,)), pltpu.SemaphoreType.DMA((6,))]),
        input_output_aliases={2: 1},
        compiler_params=_cp(("arbitrary",)),
    )(order, h, w_slot)


def _mm_ride(a, b, carry, **kw):
    if carry is None:
        return _mm(a, b, **kw), []
    return _mm(a, b, carry=carry, **kw)


def _layer_fwd(l, x, ada, w, small, ride, target=None):
    shift, scale, gate = ada[:, 0:D], ada[:, D:2 * D], ada[:, 2 * D:3 * D]
    carry, landed = ride("prenorm")
    (h, h_t), outs = _prenorm_fwd(x, small["g_pre"][l], scale, shift, f"prenorm_fwd{l}", carry)
    landed(outs)
    carry, landed = ride("proj")
    if isinstance(carry, _GatherInProj):
        proj, full = _proj_with_gather(h, carry.slot, carry.order, f"proj{l}")
        outs = [full]
    else:
        proj, outs = _mm_ride(h, w["w_in"][l], carry, name=f"proj{l}", b_mode="nn_sh", tm=2048, out_dtype=BF16)
    landed(outs)
    a_in, a_in_t = _pool_fwd(proj, small["pool_w"][l], small["pool_scale"][l], f"pool_fwd{l}")
    carry, landed = ride("hgrn")
    (b_in, b_in_t, o_raw, states, mild, cum), outs = _hgrn_fwd(proj, small["lb"][l], small["hgrn_norm_g"][l],
                                                              f"hgrn_fwd{l}", carry=carry)
    landed(outs)
    carry, landed = ride("tail")
    (br_a, br_b, merged_t, y, *x_new), outs = _layer_tail_fwd(
        proj, a_in, b_in, x, w["w_pool_o"][l], w["w_hgrn_o"][l].reshape(D, D), w["w_out"][l].reshape(D, D),
        gate, small["g_post"][l], f"tail_fwd{l}", target=target, carry=carry)
    landed(outs)
    saved = dict(x=x, h_t=h_t, proj=proj, a_in_t=a_in_t, b_in_t=b_in_t, o_raw=o_raw, states=states, mild=mild,
                 cum=cum,
                 br_a=br_a, br_b=br_b, merged_t=merged_t, y=y, scale=scale, gate=gate)
    return x_new, saved


def _layer_bwd(l, dxn, sv, w, small, ride):
    carry, landed = ride["head"](None)
    (dy, dbr_a, dbr_b, dproj, da_in, db_in, dgate, dg_post), outs = _layer_head_bwd(
        dxn, sv["y"], sv["proj"], sv["br_a"], sv["br_b"], w["w_pool_o"][l], w["w_hgrn_o"][l].reshape(D, D),
        w["w_out"][l].reshape(D, D), sv["gate"], small["g_post"][l], f"head_bwd{l}", carry)
    landed(outs)
    gw_out, gw_hgrn_o, gw_pool_o = _tail_weight_grads(sv["merged_t"], sv["b_in_t"], sv["a_in_t"], dy, dbr_b,
                                                      dbr_a, f"gw_tail{l}")
    big = dict(w_pool_o=gw_pool_o, w_hgrn_o=gw_hgrn_o.reshape(NCHIP, D // NCHIP, D),
               w_out=gw_out.reshape(NCHIP, D // NCHIP, D))
    carry, landed = ride["hgrn"](big)
    (dproj, dlb, dgn), outs = _hgrn_bwd(db_in, sv["proj"], sv["o_raw"], sv["states"], sv["mild"], sv["cum"],
                                        small["lb"][l], small["hgrn_norm_g"][l], dproj, f"hgrn_bwd{l}",
                                        carry=carry)
    landed(outs)
    dproj, dpw, dpsc = _pool_bwd(da_in, sv["proj"], small["pool_w"][l], small["pool_scale"][l], dproj,
                                 f"pool_bwd{l}")
    little = dict(dgate=dgate, g_post=dg_post, pool_w=dpw, pool_scale=dpsc, lb=dlb,
                  hgrn_norm_g=jnp.sum(dgn, axis=0, keepdims=True))
    carry, landed = ride["gw_in"](little)
    big["w_in"], outs = _mm_ride(sv["h_t"], dproj, carry, name=f"gw_in{l}", out_shards=NCHIP, out_dtype=BF16)
    landed(outs)
    carry, landed = ride["d_h"](big)
    dh, outs = _mm_ride(dproj, w["w_in"][l], carry, name=f"d_h{l}", b_mode="nt_shk", tn=1024)
    landed(outs)
    carry, landed = ride["prenorm"](big)
    (dx, dshift, dscale, dg_pre), outs = _prenorm_bwd(dh, dxn, sv["x"], small["g_pre"][l], sv["scale"],
                                                      f"prenorm_bwd{l}", carry)
    landed(outs)
    little.update(dshift=dshift, dscale=dscale, g_pre=dg_pre)
    return dx, big, little


SMALL_ROWS = 176


def _rows8(t):
    t = t.reshape(-1, D)
    return jnp.pad(t, ((0, -t.shape[0] % 8), (0, 0)))


def _pack_small(parts):
    row_keys = ("dshift", "dscale", "dgate", "g_pre", "g_post", "lb", "pool_scale", "hgrn_norm_g")
    flat = [p[k] for p in parts for k in row_keys] + [p["pool_w"] for p in parts]
    nk = len(row_keys)

    def body(*refs):
        o_ref = refs[-1]
        o_ref[...] = jnp.zeros((SMALL_ROWS, D), F32)
        for l in range(2):
            dshift, dscale, dgate, g_pre, g_post, lb, pscale, gn = refs[l * nk:(l + 1) * nk]
            for r, ref in enumerate((dshift, dscale, dgate)):
                o_ref[3 * l + r:3 * l + r + 1, :] = ref[...]
            o_ref[8 + l:9 + l, :] = g_pre[...]
            o_ref[16 + l:17 + l, :] = g_post[...]
            o_ref[24 + l:25 + l, :] = lb[...]
            o_ref[160:161, l * POOL_W:(l + 1) * POOL_W] = pscale[...]
            o_ref[168:169, l * HD:(l + 1) * HD] = gn[...]
        for (l, *at), (rows, lanes) in _small_pieces("pool_w"):
            o_ref[rows, lanes] = refs[2 * nk + l][tuple(at)]

    return pl.pallas_call(body, name="pack_small", out_shape=jax.ShapeDtypeStruct((SMALL_ROWS, D), F32),
                          compiler_params=_cp())(*flat)


def kernel(x, c, w_ada, b_ada, g_pre, g_post, w_in, pool_w, pool_scale, lb_logits, hgrn_norm_g, w_pool_o, w_hgrn_o, w_out, loss_target, m_w_ada, m_b_ada, m_g_pre, m_g_post, m_w_in, m_pool_w, m_pool_scale, m_lb_logits, m_hgrn_norm_g, m_w_pool_o, m_w_hgrn_o, m_w_out, v_w_ada, v_b_ada, v_g_pre, v_g_post, v_w_in, v_pool_w, v_pool_scale, v_lb_logits, v_hgrn_norm_g, v_w_pool_o, v_w_hgrn_o, v_w_out):
    ax, ay, ac = lax.axis_index("x"), lax.axis_index("y"), lax.axis_index("c")
    chip = 2 * ax + ay
    dev = 2 * chip + ac
    xe, te = x[0], loss_target[0]
    ada_s = w_ada.shape[2]

    big_names = ("w_in", "w_pool_o", "w_hgrn_o", "w_out")
    big_w = (w_in, w_pool_o, w_hgrn_o, w_out)
    core = jnp.stack([ac]).astype(jnp.int32)
    place = jnp.stack([chip, ac]).astype(jnp.int32)
    slots = {("w_in", l): _cast_to_slot(place, w_in, l, f"cast_w_in{l}") for l in range(2)}
    rest = [(k, l) for l in range(2) for k in big_names[1:]]
    slots.update(zip(rest, _cast_to_slots(place, [(dict(zip(big_names, big_w))[k], l) for k, l in rest],
                                          "cast_rest")))
    w = {k: [None, None] for k in big_names}
    def fills(keys):
        def landed(outs):
            for (k, l), o in zip(keys, outs):
                w[k][l] = slots[k, l] = o
        return landed

    rest0 = [(k, 0) for k in big_names[1:]]
    rest1 = [(k, 1) for k in big_names[1:]]
    no_carry = (None, lambda outs: None)
    order = jnp.stack([chip, 2 * (1 - ax) + ay, 2 * ax + (1 - ay), 2 * (1 - ax) + (1 - ay)]).astype(jnp.int32)

    def ride_fwd0(stage):
        if stage == "proj":
            return _GatherInProj(slots["w_in", 0], order), fills([("w_in", 0)])
        if stage == "hgrn":
            return (_join_carries(_gather_carry([slots[t] for t in rest0]),
                                  _gather_carry([slots["w_in", 1]], piece=(0, 2, 4), pass_on=False)),
                    fills(rest0 + [("w_in", 1)]))
        if stage == "tail":
            return (_gather_carry([slots["w_in", 1]], piece=(2, 1, 4), pass_on=False, late=(0, 2, 4)),
                    fills([("w_in", 1)]))
        return no_carry

    def ride_fwd1(stage):
        if stage == "prenorm":
            return _gather_carry([slots["w_in", 1]], piece=(3, 1, 4), late=(2, 1, 4)), fills([("w_in", 1)])
        if stage == "hgrn":
            return _gather_carry([slots[t] for t in rest1]), fills(rest1)
        return no_carry

    c_all = _gather_small(jnp.broadcast_to(c, (8, D)), "gather_c").reshape(NDEV, 8, D)[:, 0, :]
    c_pad = jnp.pad(c_all, ((0, ADA_PAD - NDEV), (0, 0)))
    b_sh = lax.dynamic_slice(b_ada, (0, chip * ada_s), (2, ada_s))
    ada_cols = _gather_small(_ada_fwd(c_pad, w_ada, b_sh), "gather_ada")
    ada_cols = ada_cols.reshape(NCHIP, 2, NDEV, 2, ada_s)[:, 0]
    ada_all = jnp.transpose(ada_cols, (2, 1, 0, 3)).reshape(2, NDEV, 3 * D)
    ada_me = lax.dynamic_slice(ada_all, (0, dev, 0), (2, 1, 3 * D))

    lbs = _lb_fwd(lb_logits)
    small = dict(g_pre=g_pre[:, None, :], g_post=g_post[:, None, :], pool_w=pool_w,
                 pool_scale=pool_scale[:, None, :], lb=lbs[:, None, :], hgrn_norm_g=hgrn_norm_g[:, None, :])

    (x1,), sv0 = _layer_fwd(0, xe, ada_me[0], w, small, ride_fwd0)
    (dx2, loss_blk), sv1 = _layer_fwd(1, x1, ada_me[1], w, small, ride_fwd1, target=te)

    parts, recv, held = {}, {}, {}

    def pair_ride(keys, grads):
        def landed(outs):
            held.update({kl: (g, o) for kl, g, o in zip(keys, grads, outs)})
        return _pair_carry(grads), landed

    def pair_adds(keys):
        gs, gots = zip(*[held.pop(kl) for kl in keys])
        if keys[0][0] == "w_in":
            parts[keys[0]] = _pair_add(core, gs[0], gots[0], f"rs_add_w_in{keys[0][1]}")
        else:
            parts.update(zip(keys, _pair_adds(core, gs, gots, f"rs_add_early{keys[0][1]}")))

    def exchange(keys):
        def landed(outs):
            recv.update(zip(keys, outs))
        return _chips_carry([parts[kl] for kl in keys]), landed

    def share(key, first, count):
        def landed(outs):
            (recv[key],) = outs
        into = [recv[key]] if key in recv else None
        return _chips_carry([parts[key]], piece=(first, count, 8), into=into), landed

    def together(*rides):
        carries, fns = zip(*rides)

        def landed(outs):
            for cr, fn in zip(carries, fns):
                fn(outs[:len(cr.outs)])
                outs = outs[len(cr.outs):]
        return _join_carries(*carries), landed

    def early(l):
        return [(k, l) for k in big_names[1:]]

    def pair_alone(keys, grads, tag):
        held.update({kl: (g, o) for kl, g, o in zip(keys, grads, _rs_pair(grads, f"rs_pair_{tag}"))})
        pair_adds(keys)

    def ride_hgrn1(big):
        return pair_ride(early(1), [big[k] for k in big_names[1:]])

    def ride_gw_in1(_):
        pair_adds(early(1))
        return exchange(early(1))

    def ride_d_h1(big):
        return pair_ride([("w_in", 1)], [big["w_in"]])

    def ride_prenorm1(_):
        pair_adds([("w_in", 1)])
        return share(("w_in", 1), 0, 1)

    def ride_head0(_):
        return share(("w_in", 1), 1, 3)

    def ride_hgrn0(big):
        pair_alone(early(0), [big[k] for k in big_names[1:]], "early0")
        return together(exchange(early(0)), share(("w_in", 1), 4, 4))

    def ride_d_h0(big):
        pair_alone([("w_in", 0)], [big["w_in"]], "w_in0")
        return share(("w_in", 0), 0, 4)

    def ride_prenorm0(_):
        return share(("w_in", 0), 4, 4)

    no_ride = lambda so_far: no_carry
    dx1, big1, little1 = _layer_bwd(1, dx2, sv1, w, small, dict(head=no_ride, hgrn=ride_hgrn1, gw_in=ride_gw_in1,
                                                                d_h=ride_d_h1, prenorm=ride_prenorm1))

    gathered = {}
    zero_row = jnp.zeros((1, D), F32)

    def ride_gw_in0(little):
        so_far = dict(little, dshift=zero_row, dscale=zero_row, g_pre=zero_row)

        def landed(outs):
            (gathered["early"],) = outs

        red = [_chip_sum(place, parts["w_in", 1], recv["w_in", 1], 1, None, "rs_sum_w_in1")]
        red += _chip_sums(place, [[parts[k, l] for l in range(2)] for k in big_names[1:]],
                          [[recv[k, l] for l in range(2)] for k in big_names[1:]], "rs_sum_early")

        def swapped(outs):
            gathered["sums"] = outs
        return together((_gather_rows_carry(_pack_small([so_far, little1])), landed),
                        (_swap_carry(red, [(1, 1)] + [(0, 2)] * 3), swapped))

    dx0, big0, little0 = _layer_bwd(0, dx1, sv0, w, small,
                                    dict(head=ride_head0, hgrn=ride_hgrn0, gw_in=ride_gw_in0, d_h=ride_d_h0,
                                         prenorm=ride_prenorm0))
    loss_row = jnp.broadcast_to(loss_blk[0:1, 0:1], (1, D))
    late = _rows8(jnp.stack([little0["dshift"], little0["dscale"], little0["g_pre"], loss_row]))
    late = _gather_small(late, "gather_small_late").reshape(NDEV, 8, D)
    loss = jnp.sum(late[:, 3, 0])
    red = _chip_sum(place, parts["w_in", 0], recv["w_in", 0], 0, gathered["sums"][0], "rs_sum_w_in0")
    g_big = dict(zip(big_names, list(_rs_swap([red], [(0, 1)])) + list(gathered["sums"][1:])))

    def two(t):
        return t.reshape(-1, t.shape[-1])

    def upd(wt, g, m, v, name, echo=False):
        return [t.reshape(wt.shape) for t in _adamw(two(wt), two(g), two(m), two(v), name, echo)]

    *u_w_in, g_w_in = upd(w_in, g_big["w_in"], m_w_in, v_w_in, "adamw_w_in", echo=True)
    g_small, pack_heads = _sum_devices(gathered["early"].reshape(NDEV, SMALL_ROWS, D), late)
    g_lb_logits = _lb_bwd(lb_logits, g_small[24:26])
    d_ada_all = pack_heads[:, 0:6, :].reshape(NDEV, 2, 3 * D)
    d_ada_sh = lax.dynamic_slice(jnp.transpose(d_ada_all, (1, 0, 2)), (0, 0, chip * ada_s), (2, NDEV, ada_s))
    d_ada_sh = jnp.pad(d_ada_sh, ((0, 0), (0, ADA_PAD - NDEV), (0, 0)))
    g_w_ada = _ada_wgrad(c_pad.T, d_ada_sh)

    u_w_ada = upd(w_ada, g_w_ada, m_w_ada, v_w_ada, "adamw_w_ada")
    early_w = dict(w_pool_o=(w_pool_o, m_w_pool_o, v_w_pool_o), w_hgrn_o=(w_hgrn_o, m_w_hgrn_o, v_w_hgrn_o),
                   w_out=(w_out, m_w_out, v_w_out))
    u_early = _adamw_many([(two(early_w[k][0]), two(g_big[k]), two(early_w[k][1]), two(early_w[k][2]))
                           for k in big_names[1:]], "adamw_early")
    (*u_w_pool_o, g_w_pool_o), (*u_w_hgrn_o, g_w_hgrn_o), (*u_w_out, g_w_out) = [
        [t.reshape(early_w[k][0].shape) for t in four] for k, four in zip(big_names[1:], u_early)]
    small_w = dict(b_ada=(b_ada, m_b_ada, v_b_ada), g_pre=(g_pre, m_g_pre, v_g_pre),
                   g_post=(g_post, m_g_post, v_g_post), lb_logits=(lb_logits, m_lb_logits, v_lb_logits),
                   pool_w=(pool_w, m_pool_w, v_pool_w), pool_scale=(pool_scale, m_pool_scale, v_pool_scale),
                   hgrn_norm_g=(hgrn_norm_g, m_hgrn_norm_g, v_hgrn_norm_g))
    u_small = _adamw_small(g_small, g_lb_logits, small_w)
    s = lambda key: u_small[key][3]
    grads_out = (g_w_ada, s("b_ada"), s("g_pre"), s("g_post"), g_w_in, s("pool_w"), s("pool_scale"), g_lb_logits,
                 s("hgrn_norm_g"), g_w_pool_o, g_w_hgrn_o, g_w_out)

    def ordered(k):
        s = lambda key: u_small[key][k]
        return (u_w_ada[k], s("b_ada"), s("g_pre"), s("g_post"), u_w_in[k], s("pool_w"), s("pool_scale"),
                s("lb_logits"), s("hgrn_norm_g"), u_w_pool_o[k], u_w_hgrn_o[k], u_w_out[k])

    return (loss, dx0[None], *grads_out, *ordered(0), *ordered(1), *ordered(2))
```

```python
import functools

import jax
import jax.numpy as jnp
from jax import lax
from jax.experimental import pallas as pl
from jax.experimental.pallas import tpu as pltpu

F32 = jnp.float32
BF16 = jnp.bfloat16
MESH = pl.DeviceIdType.MESH

D = 1024
HEADS = 8
HD = 128
GROUPS = 4
POOL_W = 512
CH = 128
SB_WIDE = 32
SB = 16
NH = 2
IN_W = 7168
NCHIP = 4
NDEV = 8
EPS = 1e-6
PV0, PG0, HQ0, HF0, HI0, HG0 = 0, 4, 8, 16, 24, 32
MGP_BLK, MGH_BLK = 5, 6

LR, B1, B2, AEPS, WD, STEP = 0.001, 0.9, 0.999, 1e-08, 0.01, 10
VMEM_LIMIT = 56 * 1024 * 1024


def _cp(sem=None, **kw):
    if sem is not None:
        kw["dimension_semantics"] = sem
    return pltpu.CompilerParams(vmem_limit_bytes=VMEM_LIMIT, **kw)


def _sig(z):
    return 1.0 / (1.0 + jnp.exp(-z))


def _dsilu(z, s):
    return s * (1.0 + z * (1.0 - s))


def _row_tile(rows, cap):
    if rows <= cap:
        return rows
    t = 1 << (cap.bit_length() - 1)
    while rows % t:
        t //= 2
    return t


ANY = pl.BlockSpec(memory_space=pl.ANY)


class _Carry:
    def __init__(self, ins, outs, aliases, n_sem, start, finish):
        self.ins, self.outs, self.aliases, self.n_sem = list(ins), list(outs), dict(aliases), n_sem
        self.start, self.finish = start, finish


class _SemWindow:
    def __init__(self, ref, base):
        self._ref, self._base = ref, base

    @property
    def at(self):
        return self

    def __getitem__(self, k):
        return self._ref.at[self._base + k]


def _join_carries(*carries):
    ins, outs, aliases, spans, n_sem = [], [], {}, [], 0
    for cr in carries:
        aliases.update({len(ins) + i: len(outs) + o for i, o in cr.aliases.items()})
        spans.append((len(ins), len(cr.ins), len(outs), len(cr.outs), n_sem))
        ins, outs, n_sem = ins + cr.ins, outs + cr.outs, n_sem + cr.n_sem

    def run(which):
        def fn(i_refs, o_refs, send_sems, recv_sems):
            for cr, (i0, ni, o0, no, s0) in zip(carries, spans):
                getattr(cr, which)(i_refs[i0:i0 + ni], o_refs[o0:o0 + no], _SemWindow(send_sems, s0),
                                   _SemWindow(recv_sems, s0))
        return fn

    return _Carry(ins, outs, aliases, n_sem, run("start"), run("finish"))


def _call(body, *, name, grid, in_specs, out_specs, out_shape, args, scratch_shapes=(), sem=None, carry=None,
          aliases=None):
    in_specs, out_specs, out_shape = list(in_specs), list(out_specs), list(out_shape)
    scratch_shapes = list(scratch_shapes)
    aliases = dict(aliases or {})
    if carry is None:
        outs = pl.pallas_call(body, name=name, grid=grid, in_specs=in_specs, out_specs=out_specs,
                              out_shape=out_shape, scratch_shapes=scratch_shapes, input_output_aliases=aliases,
                              compiler_params=_cp(sem))(*args)
        return list(outs)
    n_in, n_out, n_scr = len(in_specs), len(out_specs), len(scratch_shapes)
    c_in, c_out = len(carry.ins), len(carry.outs)

    def wrapped(*refs):
        k_in, rest = refs[:n_in], refs[n_in:]
        ci, rest = rest[:c_in], rest[c_in:]
        k_out, rest = rest[:n_out], rest[n_out:]
        co, rest = rest[:c_out], rest[c_out:]
        k_scr, (ssem, rsem) = rest[:n_scr], rest[n_scr:]
        pids = [pl.program_id(d) for d in range(len(grid))]
        first = functools.reduce(jnp.logical_and, [p == 0 for p in pids])
        last = functools.reduce(jnp.logical_and, [p == g - 1 for p, g in zip(pids, grid)])

        @pl.when(first)
        def _():
            carry.start(ci, co, ssem, rsem)

        body(*k_in, *k_out, *k_scr)

        @pl.when(last)
        def _():
            carry.finish(ci, co, ssem, rsem)

    outs = pl.pallas_call(
        wrapped, name=name, grid=grid, in_specs=in_specs + [ANY] * c_in, out_specs=out_specs + [ANY] * c_out,
        out_shape=out_shape + carry.outs,
        input_output_aliases={**aliases, **{n_in + i: n_out + o for i, o in carry.aliases.items()}},
        scratch_shapes=scratch_shapes + [pltpu.SemaphoreType.DMA((carry.n_sem,))] * 2,
        compiler_params=_cp(("arbitrary",) * len(grid)),
    )(*args, *carry.ins)
    return list(outs)


def _mm(a, b, *, name, b_mode="nn", out_shards=0, tm=1024, tn=256, tk=None, out_dtype=F32, carry=None):
    assert b_mode in ("nn", "nn_sh", "nt_shk"), b_mode
    M, K = a.shape
    if b_mode == "nn":
        N = b.shape[1]
    elif b_mode == "nn_sh":
        N = b.shape[0] * b.shape[2]
    else:
        N = b.shape[1]
    tm = _row_tile(M, tm)
    if b_mode == "nn_sh":
        tn = _row_tile(b.shape[2], tn)
    elif out_shards:
        tn = _row_tile(N // out_shards, tn)
    else:
        tn = _row_tile(N, tn)
    if tk is None:
        tk = K if b_mode != "nt_shk" else b.shape[2]
    if b_mode == "nt_shk":
        tk = _row_tile(b.shape[2], tk)
    nm, nn, nk = M // tm, N // tn, K // tk

    a_spec = pl.BlockSpec((tm, tk), lambda m, n, k: (m, k))
    if b_mode == "nn":
        b_spec = pl.BlockSpec((tk, tn), lambda m, n, k: (k, n))
    elif b_mode == "nn_sh":
        nps = b.shape[2] // tn
        b_spec = pl.BlockSpec((None, tk, tn), lambda m, n, k: (n // nps, k, n % nps))
    else:
        kps = b.shape[2] // tk
        b_spec = pl.BlockSpec((None, tn, tk), lambda m, n, k: (k // kps, n, k % kps))
    if out_shards:
        ops = (N // out_shards) // tn
        o_spec = pl.BlockSpec((None, tm, tn), lambda m, n, k: (n // ops, m, n % ops))
        o_shape = jax.ShapeDtypeStruct((out_shards, M, N // out_shards), out_dtype)
    else:
        o_spec = pl.BlockSpec((tm, tn), lambda m, n, k: (m, n))
        o_shape = jax.ShapeDtypeStruct((M, N), out_dtype)
    dn = (((1,), (1,)), ((), ())) if b_mode == "nt_shk" else (((1,), (0,)), ((), ()))

    def body(a_ref, b_ref, o_ref, acc_ref):
        k = pl.program_id(2)

        @pl.when(k == 0)
        def _():
            acc_ref[...] = jnp.zeros(acc_ref.shape, F32)

        acc_ref[...] += lax.dot_general(a_ref[...].astype(BF16), b_ref[...].astype(BF16), dn,
                                        preferred_element_type=F32)

        @pl.when(k == nk - 1)
        def _():
            o_ref[...] = acc_ref[...].astype(o_ref.dtype)

    outs = _call(body, name=name, grid=(nm, nn, nk), in_specs=[a_spec, b_spec], out_specs=[o_spec],
                 out_shape=[o_shape], scratch_shapes=[pltpu.VMEM((tm, tn), F32)],
                 sem=("parallel", "parallel", "arbitrary"), args=(a, b), carry=carry)
    return outs[0] if carry is None else (outs[0], outs[1:])


def _rowvec(n=D):
    return pl.BlockSpec((1, n), lambda i: (0, 0))


def _prenorm_fwd(x, g, scale, shift, name, carry=None):
    S = x.shape[0]
    tr = _row_tile(S, 256)

    def body(x_ref, g_ref, sc_ref, sh_ref, h_ref, ht_ref):
        xv = x_ref[...]
        r = lax.rsqrt(jnp.mean(xv * xv, axis=-1, keepdims=True) + EPS)
        hv = (xv * r) * g_ref[...] * (1.0 + sc_ref[...]) + sh_ref[...]
        h_ref[...] = hv.astype(BF16)
        ht_ref[...] = hv.T.astype(BF16)

    outs = _call(
        body, name=name, grid=(S // tr,),
        in_specs=[pl.BlockSpec((tr, D), lambda i: (i, 0)), _rowvec(), _rowvec(), _rowvec()],
        out_specs=[pl.BlockSpec((tr, D), lambda i: (i, 0)), pl.BlockSpec((D, tr), lambda i: (0, i))],
        out_shape=[jax.ShapeDtypeStruct((S, D), BF16), jax.ShapeDtypeStruct((D, S), BF16)],
        sem=("parallel",), args=(x, g, scale, shift), carry=carry)
    return outs[:2], outs[2:]


def _prenorm_bwd(dh, dxn, x, g, scale, name, carry=None):
    S = x.shape[0]
    tr = _row_tile(S, 256)

    def body(dh_ref, dxn_ref, x_ref, g_ref, sc_ref, dx_ref, dsh_ref, dsc_ref, dg_ref):
        i = pl.program_id(0)

        @pl.when(i == 0)
        def _():
            dsh_ref[...] = jnp.zeros((1, D), F32)
            dsc_ref[...] = jnp.zeros((1, D), F32)
            dg_ref[...] = jnp.zeros((1, D), F32)

        xv = x_ref[...]
        dhv = dh_ref[...]
        gv = g_ref[...]
        mod = 1.0 + sc_ref[...]
        r = lax.rsqrt(jnp.mean(xv * xv, axis=-1, keepdims=True) + EPS)
        xh = xv * r
        dsh_ref[...] += jnp.sum(dhv, axis=0, keepdims=True)
        dsc_ref[...] += jnp.sum(dhv * (xh * gv), axis=0, keepdims=True)
        dg_ref[...] += jnp.sum(dhv * mod * xh, axis=0, keepdims=True)
        u = dhv * mod * gv
        dx_ref[...] = dxn_ref[...] + r * u - xv * (r * r * r) * jnp.mean(u * xv, axis=-1, keepdims=True)

    tile = pl.BlockSpec((tr, D), lambda i: (i, 0))
    outs = _call(
        body, name=name, grid=(S // tr,),
        in_specs=[tile, tile, tile, _rowvec(), _rowvec()],
        out_specs=[tile, _rowvec(), _rowvec(), _rowvec()],
        out_shape=[jax.ShapeDtypeStruct((S, D), F32)] + [jax.ShapeDtypeStruct((1, D), F32)] * 3,
        sem=("arbitrary",), args=(dh, dxn, x, g, scale), carry=carry)
    return outs[:4], outs[4:]


def _layer_tail_fwd(proj, a_in, b_in, x, w_po, w_ho, w_out, gate, g, name, target=None, carry=None):
    S = proj.shape[0]
    tr = _row_tile(S, 256)
    nsh, _, wsh = w_po.shape
    n_in = 10 + (target is not None)

    def body(*refs):
        (mgp_ref, mgh_ref, a_ref, b_ref, x_ref, wpo_ref, who_ref, wout_ref, gate_ref, g_ref) = refs[:10]
        bra_ref, brb_ref, mt_ref, y_ref, xn_ref = refs[n_in:n_in + 5]
        av = a_ref[...]
        bra = jnp.concatenate([jnp.dot(av, wpo_ref[j], preferred_element_type=F32) for j in range(nsh)], axis=1)
        brb = jnp.dot(b_ref[...], who_ref[...], preferred_element_type=F32)
        mv = _sig(mgp_ref[...].astype(F32)) * bra + _sig(mgh_ref[...].astype(F32)) * brb
        bra_ref[...] = bra.astype(BF16)
        brb_ref[...] = brb.astype(BF16)
        mt_ref[...] = mv.T.astype(BF16)
        yv = jnp.dot(mv.astype(BF16), wout_ref[...], preferred_element_type=F32)
        y_ref[...] = yv
        r = lax.rsqrt(jnp.mean(yv * yv, axis=-1, keepdims=True) + EPS)
        xn = x_ref[...] + gate_ref[...] * ((yv * r) * g_ref[...])
        if target is None:
            xn_ref[...] = xn
        else:
            t_ref, l_ref = refs[10], refs[n_in + 5]

            @pl.when(pl.program_id(0) == 0)
            def _():
                l_ref[...] = jnp.zeros((8, 128), F32)

            err = xn - t_ref[...]
            xn_ref[...] = err * (1.0 / D)
            l_ref[...] += 0.5 * jnp.sum(jnp.mean(err * err, axis=-1, keepdims=True))

    tile = pl.BlockSpec((tr, D), lambda i: (i, 0))
    whole = lambda t: pl.BlockSpec(t.shape, lambda i: (0,) * t.ndim)
    last = target is not None
    outs = _call(
        body, name=name, grid=(S // tr,),
        in_specs=[pl.BlockSpec((tr, D), lambda i: (i, MGP_BLK)), pl.BlockSpec((tr, D), lambda i: (i, MGH_BLK)),
                  pl.BlockSpec((tr, POOL_W), lambda i: (i, 0)), tile, tile, whole(w_po), whole(w_ho),
                  whole(w_out), _rowvec(), _rowvec()] + [tile] * last,
        out_specs=[tile, tile, pl.BlockSpec((D, tr), lambda i: (0, i)), tile, tile]
        + [pl.BlockSpec((8, 128), lambda i: (0, 0))] * last,
        out_shape=[jax.ShapeDtypeStruct((S, D), BF16), jax.ShapeDtypeStruct((S, D), BF16),
                   jax.ShapeDtypeStruct((D, S), BF16), jax.ShapeDtypeStruct((S, D), F32),
                   jax.ShapeDtypeStruct((S, D), F32)] + [jax.ShapeDtypeStruct((8, 128), F32)] * last,
        sem=("arbitrary",) if last else ("parallel",),
        args=(proj, proj, a_in, b_in, x, w_po, w_ho, w_out, gate, g) + ((target,) if last else ()), carry=carry)
    return outs[:5 + last], outs[5 + last:]


def _layer_head_bwd(dxn, y, proj, br_a, br_b, w_po, w_ho, w_out, gate, g, name, carry=None):
    S = y.shape[0]
    tr = _row_tile(S, 256)
    nsh, _, wsh = w_po.shape

    def body(dxn_ref, y_ref, mgp_ref, mgh_ref, bra_ref, brb_ref, wpo_ref, who_ref, wout_ref, gate_ref, g_ref,
             dy_ref, dba_ref, dbb_ref, dproj_ref, dain_ref, dbin_ref, dgate_ref, dg_ref, dmgh_s):
        i = pl.program_id(0)
        j = pl.program_id(1)

        @pl.when((i == 0) & (j == 0))
        def _():
            dgate_ref[...] = jnp.zeros((1, D), F32)
            dg_ref[...] = jnp.zeros((1, D), F32)

        @pl.when(j == 1)
        def _():
            dproj_ref[...] = dmgh_s[...]

        @pl.when(j == 0)
        def _():
            everything(dxn_ref, y_ref, mgp_ref, mgh_ref, bra_ref, brb_ref, wpo_ref, who_ref, wout_ref, gate_ref,
                       g_ref, dy_ref, dba_ref, dbb_ref, dproj_ref, dain_ref, dbin_ref, dgate_ref, dg_ref, dmgh_s)

    def everything(dxn_ref, y_ref, mgp_ref, mgh_ref, bra_ref, brb_ref, wpo_ref, who_ref, wout_ref, gate_ref, g_ref,
                   dy_ref, dba_ref, dbb_ref, dproj_ref, dain_ref, dbin_ref, dgate_ref, dg_ref, dmgh_s):
        yv = y_ref[...]
        dv = dxn_ref[...]
        gv = g_ref[...]
        gt = gate_ref[...]
        r = lax.rsqrt(jnp.mean(yv * yv, axis=-1, keepdims=True) + EPS)
        yh = yv * r
        dgate_ref[...] += jnp.sum(dv * (yh * gv), axis=0, keepdims=True)
        dg_ref[...] += jnp.sum(dv * gt * yh, axis=0, keepdims=True)
        u = dv * gt * gv
        dy = (r * u - yv * (r * r * r) * jnp.mean(u * yv, axis=-1, keepdims=True)).astype(BF16)
        dy_ref[...] = dy
        dm = _dot_nt(dy, wout_ref[...])
        sp = _sig(mgp_ref[...].astype(F32))
        sh = _sig(mgh_ref[...].astype(F32))
        dba = (dm * sp).astype(BF16)
        dbb = (dm * sh).astype(BF16)
        dba_ref[...] = dba
        dbb_ref[...] = dbb
        dproj_ref[...] = (dm * bra_ref[...].astype(F32) * sp * (1.0 - sp)).astype(BF16)
        dmgh_s[...] = (dm * brb_ref[...].astype(F32) * sh * (1.0 - sh)).astype(BF16)
        dain = _dot_nt(dba[:, 0:wsh], wpo_ref[0])
        for k in range(1, nsh):
            dain = dain + _dot_nt(dba[:, k * wsh:(k + 1) * wsh], wpo_ref[k])
        dain_ref[...] = dain
        dbin_ref[...] = _dot_nt(dbb, who_ref[...])

    tile = pl.BlockSpec((tr, D), lambda i, j: (i, 0))
    whole = lambda t: pl.BlockSpec(t.shape, lambda i, j: (0,) * t.ndim)
    vec = pl.BlockSpec((1, D), lambda i, j: (0, 0))
    ahead = lambda i, j: jnp.minimum(i + j, S // tr - 1)
    tile_in = pl.BlockSpec((tr, D), lambda i, j: (ahead(i, j), 0))
    outs = _call(
        body, name=name, grid=(S // tr, 2),
        in_specs=[tile_in, tile_in, pl.BlockSpec((tr, D), lambda i, j: (ahead(i, j), MGP_BLK)),
                  pl.BlockSpec((tr, D), lambda i, j: (ahead(i, j), MGH_BLK)), tile_in, tile_in, whole(w_po),
                  whole(w_ho), whole(w_out), vec, vec],
        out_specs=[tile, tile, tile, pl.BlockSpec((tr, D), lambda i, j: (i, MGP_BLK + j)),
                   pl.BlockSpec((tr, POOL_W), lambda i, j: (i, 0)), tile, vec, vec],
        out_shape=[jax.ShapeDtypeStruct((S, D), BF16)] * 3
        + [jax.ShapeDtypeStruct((S, IN_W), BF16), jax.ShapeDtypeStruct((S, POOL_W), F32),
           jax.ShapeDtypeStruct((S, D), F32), jax.ShapeDtypeStruct((1, D), F32), jax.ShapeDtypeStruct((1, D), F32)],
        scratch_shapes=[pltpu.VMEM((tr, D), BF16)], sem=("arbitrary", "arbitrary"),
        args=(dxn, y, proj, proj, br_a, br_b, w_po, w_ho, w_out, gate, g), carry=carry)
    return outs[:8], outs[8:]


def _pool_pieces(u, g, S):
    rowi = lax.broadcasted_iota(jnp.int32, (S, 1), 0)

    def down(z, k):
        return jnp.where(rowi >= k, pltpu.roll(z, k, axis=0), 0.0)

    s2 = u + down(u, 1)
    s4 = s2 + down(s2, 2)
    s8 = s4 + down(s4, 4)
    s16 = s8 + down(s8, 8)
    win = jnp.where(g == 0, s2, jnp.where(g == 1, s4, jnp.where(g == 2, s8, s16)))
    w = jnp.where(g == 0, 2, jnp.where(g == 1, 4, jnp.where(g == 2, 8, 16)))
    count = jnp.minimum(rowi + 1, w).astype(F32)
    return win / count - u, count, rowi


def _pool_fwd(proj, pw, pscale, name):
    S = proj.shape[0]

    def body(pv_ref, pg_ref, pw_ref, sc_ref, a_ref, at_ref):
        g = pl.program_id(0)
        pooled, _, _ = _pool_pieces(pv_ref[...].astype(F32), g, S)
        pm = jnp.dot(pooled.astype(BF16), pw_ref[...].astype(BF16), preferred_element_type=F32)
        pgv = pg_ref[...].astype(F32)
        av = pm * sc_ref[...] * (pgv * _sig(pgv))
        a_ref[...] = av.astype(BF16)
        at_ref[...] = av.T.astype(BF16)

    outs = _call(
        body, name=name, grid=(GROUPS,),
        in_specs=[pl.BlockSpec((S, 128), lambda g: (0, PV0 + g)), pl.BlockSpec((S, 128), lambda g: (0, PG0 + g)),
                  pl.BlockSpec((None, 128, 128), lambda g: (g, 0, 0)), pl.BlockSpec((1, 128), lambda g: (0, g))],
        out_specs=[pl.BlockSpec((S, 128), lambda g: (0, g)), pl.BlockSpec((128, S), lambda g: (g, 0))],
        out_shape=[jax.ShapeDtypeStruct((S, POOL_W), BF16), jax.ShapeDtypeStruct((POOL_W, S), BF16)],
        sem=("parallel",), args=(proj, proj, pw, pscale))
    return outs


def _pool_bwd(da, proj, pw, pscale, dproj, name):
    S = proj.shape[0]

    def body(da_ref, pv_ref, pg_ref, pw_ref, sc_ref, dproj_in, dproj_ref, dpw_ref, dsc_ref, dpg_s):
        @pl.when(pl.program_id(1) == 1)
        def _():
            dproj_ref[...] = dpg_s[...]

        @pl.when(pl.program_id(1) == 0)
        def _():
            group(da_ref, pv_ref, pg_ref, pw_ref, sc_ref, dproj_ref, dpg_s, dpw_ref, dsc_ref)

    def group(da_ref, pv_ref, pg_ref, pw_ref, sc_ref, dpv_ref, dpg_ref, dpw_ref, dsc_ref):
        g = pl.program_id(0)
        pooled, count, rowi = _pool_pieces(pv_ref[...].astype(F32), g, S)
        pwb = pw_ref[...].astype(BF16)
        pm = jnp.dot(pooled.astype(BF16), pwb, preferred_element_type=F32)
        scv = sc_ref[...]
        pgv = pg_ref[...].astype(F32)
        sg = _sig(pgv)
        dav = da_ref[...]
        d_ps = dav * (pgv * sg)
        dpg_ref[...] = (dav * (pm * scv) * _dsilu(pgv, sg)).astype(BF16)
        dsc_ref[...] = jnp.sum(d_ps * pm, axis=0, keepdims=True)
        d_pm = (d_ps * scv).astype(BF16)
        dpw_ref[...] = lax.dot_general(pooled.astype(BF16), d_pm, (((0,), (0,)), ((), ())),
                                       preferred_element_type=F32)
        d_pooled = lax.dot_general(d_pm, pwb, (((1,), (1,)), ((), ())), preferred_element_type=F32)
        z = d_pooled / count

        def up(v, k):
            return jnp.where(rowi < S - k, pltpu.roll(v, S - k, axis=0), 0.0)

        t2 = z + up(z, 1)
        t4 = t2 + up(t2, 2)
        t8 = t4 + up(t4, 4)
        t16 = t8 + up(t8, 8)
        adj = jnp.where(g == 0, t2, jnp.where(g == 1, t4, jnp.where(g == 2, t8, t16)))
        dpv_ref[...] = (adj - d_pooled).astype(BF16)

    col = lambda g, j: (0, g)
    ahead = lambda g, j: jnp.minimum(g + j, GROUPS - 1)
    return pl.pallas_call(
        body, name=name, grid=(GROUPS, 2),
        in_specs=[pl.BlockSpec((S, 128), lambda g, j: (0, ahead(g, j))),
                  pl.BlockSpec((S, 128), lambda g, j: (0, PV0 + ahead(g, j))),
                  pl.BlockSpec((S, 128), lambda g, j: (0, PG0 + ahead(g, j))),
                  pl.BlockSpec((None, 128, 128), lambda g, j: (ahead(g, j), 0, 0)),
                  pl.BlockSpec((1, 128), lambda g, j: (0, ahead(g, j))), ANY],
        out_specs=[pl.BlockSpec((S, 128), lambda g, j: (0, PV0 + g + (PG0 - PV0) * j)),
                   pl.BlockSpec((None, 128, 128), lambda g, j: (g, 0, 0)), pl.BlockSpec((1, 128), col)],
        out_shape=[jax.ShapeDtypeStruct(dproj.shape, dproj.dtype),
                   jax.ShapeDtypeStruct((GROUPS, 128, 128), F32), jax.ShapeDtypeStruct((1, POOL_W), F32)],
        scratch_shapes=[pltpu.VMEM((S, 128), BF16)], input_output_aliases={5: 0},
        compiler_params=_cp(("arbitrary", "arbitrary")),
    )(da, proj, proj, pw, pscale, dproj)


SCAN_SHIFTS = tuple(1 << b for b in range(CH.bit_length() - 1))


def _chunk_cumsum(z, rowi):
    for sh in SCAN_SHIFTS:
        z = z + jnp.where(rowi >= sh, pltpu.roll(z, sh, axis=0), 0.0)
    return z


def _chunk_rev_cumsum(z, rowi):
    for sh in SCAN_SHIFTS:
        z = z + jnp.where(rowi < CH - sh, pltpu.roll(z, CH - sh, axis=0), 0.0)
    return z


def _dot_nn(a, b):
    return jnp.dot(a.astype(BF16), b.astype(BF16), preferred_element_type=F32)


def _dot_nt(a, b):
    return lax.dot_general(a.astype(BF16), b.astype(BF16), (((1,), (1,)), ((), ())), preferred_element_type=F32)


def _dot_tn(a, b):
    return lax.dot_general(a.astype(BF16), b.astype(BF16), (((0,), (0,)), ((), ())), preferred_element_type=F32)


def _gates(hq, hf, lbv):
    hq, hf = hq.astype(F32), hf.astype(F32)
    sq = _sig(hq)
    sf = _sig(hf)
    f = lbv + (1.0 - lbv) * sf
    fc = jnp.maximum(f, 1e-30)
    return hq * sq, sq, sf, f, fc, jnp.log(fc)


DECAY_CAP = 60.0


def _block_ref(c_ref, i, sb):
    if i == 0:
        return jnp.zeros((1, HD), F32)
    return c_ref[sb * i - 1:sb * i, :]


def _block_decay(c_ref, sb):
    spans = [_block_ref(c_ref, i, sb) - c_ref[sb * (i + 1) - 1:sb * (i + 1), :] for i in range(CH // sb)]
    return functools.reduce(jnp.maximum, spans)


def _pair_factors(q_ref, k, c_ref, first, cap, round_bf16, sb):
    nb = CH // sb
    c = c_ref[...]
    zero = jnp.zeros((sb, HD), F32)
    q_groups, k_groups, eqs, eks = [], [], [], []
    for i in range(first, nb):
        blk = slice(sb * i, sb * (i + 1))
        r_i = _block_ref(c_ref, i, sb)
        eq = jnp.exp(jnp.minimum(c_ref[blk, :] - r_i, 0.0))
        ek = jnp.exp(jnp.minimum(r_i - c, cap))
        qi, kei = q_ref[blk, :] * eq, k * ek
        if round_bf16:
            qi, kei = qi.astype(BF16).astype(F32), kei.astype(BF16).astype(F32)
        q_groups.append(jnp.concatenate([zero] * i + [qi] + [zero] * (nb - 1 - i), axis=0))
        k_groups.append(kei)
        eqs.append(eq)
        eks.append(ek)
    return jnp.concatenate(q_groups, axis=1), jnp.concatenate(k_groups, axis=1), eqs, eks


def _pair_mask(rowi, coli, strict, sb):
    return (coli < jnp.bitwise_and(rowi, -sb)) if strict else (coli <= rowi)


def _hgrn_fwd(proj, lb, gn, name, carry=None):
    S = proj.shape[0]
    nch = S // CH
    W = NH * HD

    def body(hq_ref, hf_ref, hi_ref, hg_ref, lb_ref, gn_ref, bin_ref, bint_ref, oraw_ref, st_ref, mild_ref,
             cum_ref, q_s, k_s, c_s, v_s, o_s, state_s, qf_s, kf_s, cf_s):
        state_s[...] = jnp.zeros((NH, HD, HD), F32)
        rowi = lax.broadcasted_iota(jnp.int32, (CH, 1), 0)
        coli = lax.broadcasted_iota(jnp.int32, (1, CH), 1)
        sbi = lax.broadcasted_iota(jnp.int32, (SB, 1), 0)
        gnv = gn_ref[...]

        def gates_pass(n, worst):
            wide, narrow = worst
            rows = pl.ds(pl.multiple_of(n * CH, CH), CH)
            for hh in range(NH):
                lanes = slice(hh * HD, (hh + 1) * HD)
                q, _, _, f, _, logf = _gates(hq_ref[rows, lanes], hf_ref[rows, lanes], lb_ref[:, lanes])
                c = _chunk_cumsum(logf, rowi)
                qf_s[hh, rows, :] = q
                kf_s[hh, rows, :] = 1.0 - f
                cf_s[hh, rows, :] = c
                cum_ref[rows, lanes] = c
                c_s[hh] = c
                wide = jnp.maximum(wide, _block_decay(c_s.at[hh], SB_WIDE))
                narrow = jnp.maximum(narrow, _block_decay(c_s.at[hh], SB))
            return wide, narrow

        def between_chunks(hh, n, rows):
            lanes = slice(hh * HD, (hh + 1) * HD)
            q = qf_s[hh, rows, :]
            k = kf_s[hh, rows, :]
            c = cf_s[hh, rows, :]
            v = hi_ref[rows, lanes].astype(F32)
            q_s[hh] = q
            k_s[hh] = k
            c_s[hh] = c
            v_s[hh] = v
            st = state_s[hh]
            st_ref[hh, n] = st.astype(BF16)
            o_s[hh] = _dot_nt(q * jnp.exp(c), st)
            last = c_s[hh, CH - 1:CH, :]
            state_s[hh] = st * jnp.exp(last) + _dot_tn(v, k * jnp.exp(last - c))

        def pairs_matmul(hh, first, cap, strict, sb):
            qx, kc, _, _ = _pair_factors(q_s.at[hh], k_s[hh], c_s.at[hh], first, cap, False, sb)
            a = jnp.where(_pair_mask(rowi, coli, strict, sb), _dot_nt(qx, kc), 0.0)
            o_s[hh] += _dot_nn(a, v_s[hh])

        def within_chunk_matmul(sb):
            return lambda hh: pairs_matmul(hh, 0, DECAY_CAP, False, sb)

        def within_chunk_exact(hh):
            pairs_matmul(hh, 1, 0.0, True, SB)
            for i in range(CH // SB):
                blk = slice(SB * i, SB * (i + 1))
                qb = q_s[hh, blk, :]
                cb = c_s[hh, blk, :]
                acc = jnp.zeros((SB, HD), F32)
                for s in range(SB):
                    row = SB * i + s
                    w = jnp.exp(jnp.minimum(cb - c_s[hh, row:row + 1, :], 0.0))
                    a_col = jnp.sum(qb * k_s[hh, row:row + 1, :] * w, axis=-1, keepdims=True)
                    acc = acc + jnp.where(sbi >= s, a_col, 0.0) * v_s[hh, row:row + 1, :]
                o_s[hh, blk, :] += acc

        def norm_and_gate(hh, rows):
            lanes = slice(hh * HD, (hh + 1) * HD)
            ov = o_s[hh]
            oraw_ref[rows, lanes] = ov
            r = lax.rsqrt(jnp.mean(ov * ov, axis=-1, keepdims=True) + EPS)
            hg = hg_ref[rows, lanes].astype(F32)
            bin_ref[rows, lanes] = ((ov * r) * gnv * (hg * _sig(hg))).astype(BF16)

        def chunk_with(within_chunk):
            def chunk(n, carry):
                rows = pl.ds(pl.multiple_of(n * CH, CH), CH)
                for hh in range(NH):
                    between_chunks(hh, n, rows)
                for hh in range(NH):
                    within_chunk(hh)
                for hh in range(NH):
                    norm_and_gate(hh, rows)
                return carry
            return chunk

        none = jnp.zeros((1, HD), F32)
        wide, narrow = lax.fori_loop(0, nch, gates_pass, (none, none))
        tier = jnp.where(jnp.max(wide) <= DECAY_CAP, 2.0, jnp.where(jnp.max(narrow) <= DECAY_CAP, 1.0, 0.0))
        mild_ref[...] = jnp.broadcast_to(tier, (8, HD))

        @pl.when(tier == 2.0)
        def _():
            lax.fori_loop(0, nch, chunk_with(within_chunk_matmul(SB_WIDE)), 0, unroll=4)

        @pl.when(tier == 1.0)
        def _():
            lax.fori_loop(0, nch, chunk_with(within_chunk_matmul(SB)), 0, unroll=2)

        @pl.when(tier == 0.0)
        def _():
            lax.fori_loop(0, nch, chunk_with(within_chunk_exact), 0)

        bint_ref[...] = bin_ref[...].astype(F32).T.astype(BF16)

    col = lambda off: pl.BlockSpec((S, W), lambda h: (0, off // NH + h))
    head = pl.BlockSpec((S, W), lambda h: (0, h))
    outs = _call(
        body, name=name, grid=(HEADS // NH,),
        in_specs=[col(HQ0), col(HF0), col(HI0), col(HG0), pl.BlockSpec((1, W), lambda h: (0, h)),
                  pl.BlockSpec((1, HD), lambda h: (0, 0))],
        out_specs=[head, pl.BlockSpec((W, S), lambda h: (h, 0)), head,
                   pl.BlockSpec((NH, nch, HD, HD), lambda h: (h, 0, 0, 0)),
                   pl.BlockSpec((8, HD), lambda h: (h, 0)), head],
        out_shape=[jax.ShapeDtypeStruct((S, D), BF16), jax.ShapeDtypeStruct((D, S), BF16),
                   jax.ShapeDtypeStruct((S, D), F32), jax.ShapeDtypeStruct((HEADS, nch, HD, HD), BF16),
                   jax.ShapeDtypeStruct((8 * HEADS // NH, HD), F32), jax.ShapeDtypeStruct((S, D), F32)],
        scratch_shapes=[pltpu.VMEM((NH, CH, HD), F32)] * 5 + [pltpu.VMEM((NH, HD, HD), F32)]
        + [pltpu.VMEM((NH, S, HD), F32)] * 3,
        sem=("parallel",), args=(proj, proj, proj, proj, lb, gn), carry=carry)
    return outs[:6], outs[6:]


def _hgrn_bwd(dbin, proj, oraw, states, mild, cum, lb, gn, dproj, name, carry=None):
    S = proj.shape[0]
    nch = S // CH
    W = NH * HD
    n_in = 12

    def body(*refs):
        ins, (dproj_ref, dlb_ref, dgn_ref) = refs[:n_in - 1], refs[n_in:n_in + 3]
        scratch, later = refs[n_in + 3:-3], refs[-3:]
        seg = pl.program_id(1)

        @pl.when(seg == 0)
        def _():
            heads(*ins, dproj_ref, *later, dlb_ref, dgn_ref, *scratch)

        for s, kept in enumerate(later):
            @pl.when(seg == s + 1)
            def _(kept=kept):
                dproj_ref[...] = kept[...]

    def heads(db_ref, hq_ref, hf_ref, hi_ref, hg_ref, or_ref, st_ref, mild_ref, cum_ref, lb_ref, gn_ref,
              dq_ref, df_ref, di_ref, dg_ref, dlb_ref, dgn_ref,
              q_s, k_s, c_s, v_s, do_s, dq_s, dk_s, dv_s, dc_s, dqd_s, dkd_s, f_s, sf_s, sq_s, dl_s, dst_s,
              dlb_s, dgn_s):
        dst_s[...] = jnp.zeros((NH, HD, HD), F32)
        dlb_s[...] = jnp.zeros((1, W), F32)
        dgn_s[...] = jnp.zeros((1, HD), F32)
        rowi = lax.broadcasted_iota(jnp.int32, (CH, 1), 0)
        coli = lax.broadcasted_iota(jnp.int32, (1, CH), 1)
        sbi = lax.broadcasted_iota(jnp.int32, (SB, 1), 0)
        gnv = gn_ref[...]
        def between_chunks(hh, n, rows):
            lanes = slice(hh * HD, (hh + 1) * HD)
            lbv = lb_ref[:, lanes]
            hq = hq_ref[rows, lanes].astype(F32)
            sq = _sig(hq)
            sf = _sig(hf_ref[rows, lanes].astype(F32))
            f = lbv + (1.0 - lbv) * sf
            q = hq * sq
            k = 1.0 - f
            f_s[hh] = f
            sf_s[hh] = sf
            sq_s[hh] = sq
            v = hi_ref[rows, lanes].astype(F32)
            c = cum_ref[rows, lanes]
            ov = or_ref[rows, lanes]
            hg = hg_ref[rows, lanes].astype(F32)
            sg = _sig(hg)
            r = lax.rsqrt(jnp.mean(ov * ov, axis=-1, keepdims=True) + EPS)
            dbv = db_ref[rows, lanes]
            d_on = dbv * (hg * sg)
            dg_ref[rows, lanes] = (dbv * ((ov * r) * gnv) * _dsilu(hg, sg)).astype(BF16)
            dgn_s[...] += jnp.sum(d_on * (ov * r), axis=0, keepdims=True)
            u = d_on * gnv
            do = r * u - ov * (r * r * r) * jnp.mean(u * ov, axis=-1, keepdims=True)
            q_s[hh] = q
            k_s[hh] = k
            c_s[hh] = c
            v_s[hh] = v
            do_s[hh] = do
            st = st_ref[hh, n].astype(F32)
            dst = dst_s[hh]
            ec = jnp.exp(c)
            last = c_s[hh, CH - 1:CH, :]
            el = jnp.exp(last - c)
            elast = jnp.exp(last)
            dq = _dot_nn(do, st) * ec
            dk = _dot_nn(v, dst) * el
            dq_s[hh] = dq
            dk_s[hh] = dk
            dv_s[hh] = _dot_nt(k * el, dst)
            dc_s[hh] = q * dq - k * dk
            dl_s[hh] = (jnp.sum(k * dk, axis=0, keepdims=True)
                        + elast * jnp.sum(st * dst, axis=0, keepdims=True))
            dst_s[hh] = dst * elast + _dot_tn(do, q * ec)

        def pairs_matmul(hh, first, cap, strict, sb):
            do = do_s[hh]
            qx, kc, eqs, eks = _pair_factors(q_s.at[hh], k_s[hh], c_s.at[hh], first, cap, True, sb)
            mask = _pair_mask(rowi, coli, strict, sb)
            a = jnp.where(mask, _dot_nt(qx, kc), 0.0)
            d_a = jnp.where(mask, _dot_nt(do, v_s[hh]).astype(BF16).astype(F32), 0.0)
            dqx = _dot_nn(d_a, kc)
            dkc = _dot_tn(d_a, qx)
            dv_s[hh] += _dot_tn(a, do)
            dk, dcum = dk_s[hh], dc_s[hh]
            dq_slabs = [jnp.zeros((sb, HD), F32)] * first
            dc_slabs = [jnp.zeros((sb, HD), F32)] * first
            for g, (eq, ek) in enumerate(zip(eqs, eks)):
                rows = slice(sb * (first + g), sb * (first + g + 1))
                cols = slice(HD * g, HD * (g + 1))
                dq_i = dqx[rows, cols]
                dk_i = dkc[:, cols]
                dq_slabs.append(dq_i * eq)
                dc_slabs.append(qx[rows, cols] * dq_i)
                dk = dk + dk_i * ek
                dcum = dcum - kc[:, cols] * dk_i
            dq_s[hh] += jnp.concatenate(dq_slabs, axis=0)
            dk_s[hh] = dk
            dc_s[hh] = dcum + jnp.concatenate(dc_slabs, axis=0)

        def pairs_exact(hh):
            dqd_s[hh] = jnp.zeros((CH, HD), F32)
            dkd_s[hh] = jnp.zeros((CH, HD), F32)
            for i in range(CH // SB):
                blk = slice(SB * i, SB * (i + 1))
                qb = q_s[hh, blk, :]
                cb = c_s[hh, blk, :]
                dob = do_s[hh, blk, :]
                dq_acc = jnp.zeros((SB, HD), F32)
                for s in range(SB):
                    row = SB * i + s
                    ks = k_s[hh, row:row + 1, :]
                    vs = v_s[hh, row:row + 1, :]
                    w = jnp.exp(jnp.minimum(cb - c_s[hh, row:row + 1, :], 0.0))
                    live = sbi >= s
                    a_col = jnp.where(live, jnp.sum(qb * ks * w, axis=-1, keepdims=True), 0.0)
                    da_col = jnp.where(live, jnp.sum(dob * vs, axis=-1, keepdims=True), 0.0)
                    dq_acc = dq_acc + da_col * ks * w
                    dkd_s[hh, row:row + 1, :] += jnp.sum(da_col * qb * w, axis=0, keepdims=True)
                    dv_s[hh, row:row + 1, :] += jnp.sum(a_col * dob, axis=0, keepdims=True)
                dqd_s[hh, blk, :] += dq_acc
            dq_d = dqd_s[hh]
            dk_d = dkd_s[hh]
            dq_s[hh] += dq_d
            dk_s[hh] += dk_d
            dc_s[hh] += q_s[hh] * dq_d - k_s[hh] * dk_d

        def gate_grads(hh, rows):
            lanes = slice(hh * HD, (hh + 1) * HD)
            lbv = lb_ref[:, lanes]
            hq = hq_ref[rows, lanes].astype(F32)
            f, sf, sq = f_s[hh], sf_s[hh], sq_s[hh]
            dlogf = _chunk_rev_cumsum(dc_s[hh], rowi) + dl_s[hh]
            dfv = jnp.where(f > 1e-30, dlogf / jnp.maximum(f, 1e-30), 0.0) - dk_s[hh]
            dlb_s[:, lanes] += jnp.sum(dfv * (1.0 - sf), axis=0, keepdims=True)
            df_ref[rows, lanes] = (dfv * (1.0 - lbv) * sf * (1.0 - sf)).astype(BF16)
            dq_ref[rows, lanes] = (dq_s[hh] * _dsilu(hq, sq)).astype(BF16)
            di_ref[rows, lanes] = dv_s[hh].astype(BF16)

        def chunk_with(pairs):
            def chunk(j, carry):
                n = nch - 1 - j
                rows = pl.ds(pl.multiple_of(n * CH, CH), CH)
                for hh in range(NH):
                    between_chunks(hh, n, rows)
                for hh in range(NH):
                    pairs(hh)
                for hh in range(NH):
                    gate_grads(hh, rows)
                return carry
            return chunk

        def pairs_mild(sb):
            return lambda hh: pairs_matmul(hh, 0, DECAY_CAP, False, sb)

        def pairs_any(hh):
            pairs_matmul(hh, 1, 0.0, True, SB)
            pairs_exact(hh)

        tier = jnp.max(mild_ref[...])

        @pl.when(tier == 2.0)
        def _():
            lax.fori_loop(0, nch, chunk_with(pairs_mild(SB_WIDE)), 0, unroll=2)

        @pl.when(tier == 1.0)
        def _():
            lax.fori_loop(0, nch, chunk_with(pairs_mild(SB)), 0)

        @pl.when(tier == 0.0)
        def _():
            lax.fori_loop(0, nch, chunk_with(pairs_any), 0)

        dlb_ref[...] = dlb_s[...]
        dgn_ref[...] = jnp.broadcast_to(dgn_s[...], (8, HD))

    ahead = lambda h, s: jnp.minimum(h + jnp.minimum(s, 1), HEADS // NH - 1)
    col = lambda off: pl.BlockSpec((S, W), lambda h, s: (0, off // NH + ahead(h, s)))
    head_in = pl.BlockSpec((S, W), lambda h, s: (0, ahead(h, s)))
    vec_in = pl.BlockSpec((1, W), lambda h, s: (0, ahead(h, s)))
    vec = pl.BlockSpec((1, W), lambda h, s: (0, h))
    seg_w = (HF0 - HQ0) // NH
    outs = _call(
        body, name=name, grid=(HEADS // NH, 4),
        in_specs=[head_in, col(HQ0), col(HF0), col(HI0), col(HG0), head_in,
                  pl.BlockSpec((NH, nch, HD, HD), lambda h, s: (ahead(h, s), 0, 0, 0)),
                  pl.BlockSpec((8, HD), lambda h, s: (ahead(h, s), 0)), head_in, vec_in,
                  pl.BlockSpec((1, HD), lambda h, s: (0, 0)), ANY],
        out_specs=[pl.BlockSpec((S, W), lambda h, s: (0, HQ0 // NH + seg_w * s + h)), vec,
                   pl.BlockSpec((8, HD), lambda h, s: (h, 0))],
        out_shape=[jax.ShapeDtypeStruct(dproj.shape, dproj.dtype), jax.ShapeDtypeStruct((1, D), F32),
                   jax.ShapeDtypeStruct((8 * HEADS // NH, HD), F32)],
        scratch_shapes=[pltpu.VMEM((NH, CH, HD), F32)] * 14
        + [pltpu.VMEM((NH, 1, HD), F32), pltpu.VMEM((NH, HD, HD), F32), pltpu.VMEM((1, W), F32),
           pltpu.VMEM((1, HD), F32)] + [pltpu.VMEM((S, W), BF16)] * 3,
        sem=("arbitrary", "arbitrary"), aliases={n_in - 1: 0},
        args=(dbin, proj, proj, proj, proj, oraw, states, mild, cum, lb, gn, dproj), carry=carry)
    dproj, dlb, dgn = outs[:3]
    return (dproj, dlb, dgn.reshape(HEADS // NH, 8, HD)[:, 0, :]), outs[3:]


def _lower_bounds(l0, l1):
    m = jnp.maximum(l0, l1)
    e0 = jnp.exp(l0 - m)
    e1 = jnp.exp(l1 - m)
    tot = e0 + e1
    p0 = e0 / tot
    p1 = e1 / tot
    return jnp.clip(p0 - p0, 0.0, 1.0), jnp.clip((p0 + p1) - p0, 0.0, 1.0)


def _lb_fwd(logits):
    def body(l_ref, o_ref):
        lb0, lb1 = _lower_bounds(l_ref[0:1, :], l_ref[1:2, :])
        o_ref[0:1, :] = lb0
        o_ref[1:2, :] = lb1

    return pl.pallas_call(body, name="lb_fwd", out_shape=jax.ShapeDtypeStruct((2, D), F32))(logits)


def _lb_bwd(logits, dlb):
    def body(l_ref, d_ref, o_ref):
        _, vjp = jax.vjp(_lower_bounds, l_ref[0:1, :], l_ref[1:2, :])
        g0, g1 = vjp((d_ref[0:1, :], d_ref[1:2, :]))
        o_ref[0:1, :] = g0
        o_ref[1:2, :] = g1

    return pl.pallas_call(body, name="lb_bwd", out_shape=jax.ShapeDtypeStruct((2, D), F32))(logits, dlb)


ADA_PAD = 128


def _ada_fwd(c_pad, w_ada, b_sh):
    ns = w_ada.shape[2]

    def body(c_ref, w_ref, b_ref, o_ref):
        cv = c_ref[...]
        ca = (cv * _sig(cv)).astype(BF16)
        for l in range(2):
            res = jnp.dot(ca, w_ref[l].astype(BF16), preferred_element_type=F32)
            o_ref[:, l * ns:(l + 1) * ns] = res[0:NDEV, :] + b_ref[l:l + 1, :]

    return pl.pallas_call(body, name="ada_fwd", out_shape=jax.ShapeDtypeStruct((NDEV, 2 * ns), F32),
                          compiler_params=_cp())(c_pad, w_ada, b_sh)


def _ada_wgrad(c_pad_t, d_ada_sh):
    ns = d_ada_sh.shape[2]

    def body(c_ref, d_ref, o_ref):
        cv = c_ref[...]
        ca = (cv * _sig(cv)).astype(BF16)
        for l in range(2):
            o_ref[l] = jnp.dot(ca, d_ref[l].astype(BF16), preferred_element_type=F32)

    return pl.pallas_call(body, name="ada_wgrad", out_shape=jax.ShapeDtypeStruct((2, D, ns), F32),
                          compiler_params=_cp())(c_pad_t, d_ada_sh)


def _sum_devices(g, late):
    _, R, C = g.shape

    def body(g_ref, l_ref, o_ref, head_ref):
        acc, acc_late = g_ref[0], l_ref[0]
        for d in range(1, NDEV):
            acc, acc_late = acc + g_ref[d], acc_late + l_ref[d]
        o_ref[...] = acc
        o_ref[0:2, :] = acc_late[0:2]
        o_ref[8:9, :] = acc_late[2:3]
        for d in range(NDEV):
            head_ref[d] = g_ref[d, 0:8, :]
            head_ref[d, 0:2, :] = l_ref[d, 0:2, :]

    return pl.pallas_call(body, name="sum_devices",
                          out_shape=[jax.ShapeDtypeStruct((R, C), F32), jax.ShapeDtypeStruct((NDEV, 8, C), F32)],
                          compiler_params=_cp())(g, late)


def _adamw(w, g, m, v, name, echo=False):
    R, C = w.shape
    tr = _row_tile(R, max(8, (1 << 19) // C))

    def body(w_ref, g_ref, m_ref, v_ref, d_ref, nm_ref, nv_ref, *g_out):
        d_ref[...], nm_ref[...], nv_ref[...] = _adamw_update(w_ref[...], g_ref[...], m_ref[...], v_ref[...])
        for o_ref in g_out:
            o_ref[...] = g_ref[...]

    tile = pl.BlockSpec((tr, C), lambda i: (i, 0))
    n_out = 4 if echo else 3
    return _call(body, name=name, grid=(R // tr,), in_specs=[tile] * 4, out_specs=[tile] * n_out,
                 out_shape=[jax.ShapeDtypeStruct((R, C), F32)] * n_out, sem=("parallel",), args=(w, g, m, v))


def _adamw_many(wgmv, name, steps=4):
    n = len(wgmv)

    def body(*refs):
        ins, outs = refs[:4 * n], refs[4 * n:]
        for a in range(n):
            w_ref, g_ref, m_ref, v_ref = ins[4 * a:4 * a + 4]
            d_ref, nm_ref, nv_ref, g_out = outs[4 * a:4 * a + 4]
            d_ref[...], nm_ref[...], nv_ref[...] = _adamw_update(w_ref[...], g_ref[...], m_ref[...], v_ref[...])
            g_out[...] = g_ref[...]

    def tile(t):
        return pl.BlockSpec((t.shape[0] // steps, t.shape[1]), lambda i: (i, 0))

    flat = [t for four in wgmv for t in four]
    outs = pl.pallas_call(
        body, name=name, grid=(steps,), in_specs=[tile(t) for t in flat],
        out_specs=[tile(four[0]) for four in wgmv for _ in range(4)],
        out_shape=[jax.ShapeDtypeStruct(four[0].shape, F32) for four in wgmv for _ in range(4)],
        compiler_params=_cp(("parallel",)))(*flat)
    return [outs[4 * a:4 * a + 4] for a in range(n)]


def _adamw_update(w, g, m, v):
    nm = B1 * m + (1.0 - B1) * g
    nv = B2 * v + (1.0 - B2) * (g * g)
    m_hat = nm / (1.0 - B1 ** STEP)
    v_hat = nv / (1.0 - B2 ** STEP)
    return -LR * (m_hat / (jnp.sqrt(v_hat) + AEPS) + WD * w), nm, nv


SMALL_KEYS = ("b_ada", "g_pre", "g_post", "lb_logits", "pool_w", "pool_scale", "hgrn_norm_g")


def _small_pieces(key):
    one = lambda i: slice(i, i + 1)
    if key == "b_ada":
        return [((one(l), slice(j * D, (j + 1) * D)), (one(3 * l + j), slice(0, D)))
                for l in range(2) for j in range(3)]
    if key in ("g_pre", "g_post", "lb_logits"):
        row0 = {"g_pre": 8, "g_post": 16, "lb_logits": 24}[key]
        return [((slice(0, 2), slice(0, D)), (slice(row0, row0 + 2), slice(0, D)))]
    if key == "pool_w":
        return [((l, g, pl.ds(k, 16, stride=8), slice(0, 128)),
                 (slice(32 + 64 * l + 16 * g, 48 + 64 * l + 16 * g), slice(128 * k, 128 * (k + 1))))
                for l in range(2) for g in range(GROUPS) for k in range(8)]
    width = {"pool_scale": POOL_W, "hgrn_norm_g": HD}[key]
    row = {"pool_scale": 160, "hgrn_norm_g": 168}[key]
    return [((one(l), slice(0, width)), (one(row), slice(l * width, (l + 1) * width))) for l in range(2)]


def _adamw_small(g_small, g_lb_logits, wmv):
    n = len(SMALL_KEYS)

    def body(g_ref, glb_ref, *refs):
        ins, outs = refs[:3 * n], refs[3 * n:]
        for p, key in enumerate(SMALL_KEYS):
            w_ref, m_ref, v_ref = ins[3 * p:3 * p + 3]
            for at, (rows, lanes) in _small_pieces(key):
                gv = glb_ref[at] if key == "lb_logits" else g_ref[rows, lanes]
                res = _adamw_update(w_ref[at], gv, m_ref[at], v_ref[at])
                for o_ref, val in zip(outs[4 * p:4 * p + 4], (*res, gv)):
                    o_ref[at] = val

    flat = [t for key in SMALL_KEYS for t in wmv[key]]
    outs = pl.pallas_call(body, name="adamw_small",
                          out_shape=[jax.ShapeDtypeStruct(wmv[key][0].shape, F32) for key in SMALL_KEYS
                                     for _ in range(4)],
                          compiler_params=_cp())(g_small, g_lb_logits, *flat)
    return {key: outs[4 * p:4 * p + 4] for p, key in enumerate(SMALL_KEYS)}


def _cast_to_slot(place, w, l, name):
    _, R, C = w.shape
    tr = _row_tile(R, max(8, (1 << 19) // C))

    def body(p_ref, w_ref, o_ref):
        o_ref[...] = w_ref[...].astype(BF16)

    return pl.pallas_call(
        body, name=name, out_shape=jax.ShapeDtypeStruct((NCHIP, R, C), BF16),
        grid_spec=pltpu.PrefetchScalarGridSpec(
            num_scalar_prefetch=1, grid=(R // tr,),
            in_specs=[pl.BlockSpec((None, tr, C), lambda i, p_ref: (l, i, 0))],
            out_specs=pl.BlockSpec((None, tr, C), lambda i, p_ref: (p_ref[0], i, 0))),
        compiler_params=_cp(("parallel",)),
    )(place, w)


def _cast_to_slots(place, ws, name):
    n = len(ws)

    def body(p_ref, *refs):
        for w_ref, o_ref in zip(refs[:n], refs[n:]):
            o_ref[...] = w_ref[...].astype(BF16)

    def layer(l):
        return lambda i, p_ref: (l, 0, 0)

    return pl.pallas_call(
        body, name=name, out_shape=[jax.ShapeDtypeStruct((NCHIP,) + w.shape[1:], BF16) for w, _ in ws],
        grid_spec=pltpu.PrefetchScalarGridSpec(
            num_scalar_prefetch=1, grid=(1,),
            in_specs=[pl.BlockSpec((None,) + w.shape[1:], layer(l)) for w, l in ws],
            out_specs=[pl.BlockSpec((None,) + w.shape[1:], lambda i, p_ref: (p_ref[0], 0, 0)) for w, _ in ws]),
        compiler_params=_cp(("arbitrary",)),
    )(place, *[w for w, _ in ws])


def _pair_adds(core, gs, gots, name):
    n = len(gs)

    def body(c_ref, *refs):
        for a_ref, b_ref, o_ref in zip(refs[:n], refs[n:2 * n], refs[2 * n:]):
            o_ref[...] = (a_ref[...].astype(F32) + b_ref[...].astype(F32)).astype(o_ref.dtype)

    def whole(t):
        return pl.BlockSpec(t.shape, lambda i, c_ref: (0, 0, 0))

    return pl.pallas_call(
        body, name=name, out_shape=[jax.ShapeDtypeStruct(t.shape, BF16) for t in gots],
        grid_spec=pltpu.PrefetchScalarGridSpec(
            num_scalar_prefetch=1, grid=(1,),
            in_specs=[pl.BlockSpec(t.shape, lambda i, c_ref: (0, c_ref[0], 0)) for t in gots]
            + [whole(t) for t in gots],
            out_specs=[whole(t) for t in gots]),
        compiler_params=_cp(("arbitrary",)),
    )(core, *gs, *gots)


def _pair_add(core, g, got, name):
    _, R, C = g.shape
    r2 = R // 2
    tr = _row_tile(r2, max(8, (1 << 19) // C))
    nt = r2 // tr

    def body(c_ref, a_ref, b_ref, o_ref):
        o_ref[...] = (a_ref[...].astype(F32) + b_ref[...].astype(F32)).astype(o_ref.dtype)

    return pl.pallas_call(
        body, name=name, out_shape=jax.ShapeDtypeStruct((NCHIP, r2, C), BF16),
        grid_spec=pltpu.PrefetchScalarGridSpec(
            num_scalar_prefetch=1, grid=(NCHIP, nt),
            in_specs=[pl.BlockSpec((None, tr, C), lambda j, i, c_ref: (j, c_ref[0] * nt + i, 0)),
                      pl.BlockSpec((None, tr, C), lambda j, i, c_ref: (j, i, 0))],
            out_specs=pl.BlockSpec((None, tr, C), lambda j, i, c_ref: (j, i, 0))),
        compiler_params=_cp(("parallel", "parallel")),
    )(core, g, got)


def _sum_in_chip_order(me, own_ref, r_ref):
    own = own_ref[...].astype(F32)
    acc = None
    for j in range(NCHIP):
        slot = jnp.minimum(jnp.where(j > me, j - 1, j), NCHIP - 2)
        term = jnp.where(me == j, own, r_ref[slot].astype(F32))
        acc = term if acc is None else acc + term
    return acc


def _chip_sums(place, parts, recvs, name):
    n = len(parts)

    def body(p_ref, *refs):
        for a in range(n):
            for l in range(2):
                refs[4 * n + a][l] = _sum_in_chip_order(p_ref[0], refs[2 * a + l], refs[2 * n + 2 * a + l])

    def own(t):
        return pl.BlockSpec((None,) + t.shape[1:], lambda i, p_ref: (p_ref[0], 0, 0))

    def whole(t):
        return pl.BlockSpec(t.shape, lambda i, p_ref: (0, 0, 0))

    flat_p = [t for pair in parts for t in pair]
    flat_r = [t for pair in recvs for t in pair]
    return pl.pallas_call(
        body, name=name,
        out_shape=[jax.ShapeDtypeStruct((2, 2 * p[0].shape[1], p[0].shape[2]), F32) for p in parts],
        grid_spec=pltpu.PrefetchScalarGridSpec(
            num_scalar_prefetch=1, grid=(1,),
            in_specs=[own(t) for t in flat_p] + [whole(t) for t in flat_r],
            out_specs=[pl.BlockSpec((2,) + p[0].shape[1:], lambda i, p_ref: (0, p_ref[1], 0)) for p in parts]),
        compiler_params=_cp(("arbitrary",)),
    )(place, *flat_p, *flat_r)


def _chip_sum(place, part, recv, layer, both, name):
    _, r2, C = part.shape
    tr = _row_tile(r2, max(8, (1 << 18) // C))
    nt = r2 // tr

    def body(p_ref, own_ref, r_ref, *rest):
        rest[-1][...] = _sum_in_chip_order(p_ref[0], own_ref, r_ref)

    args = (place, part, recv) if both is None else (place, part, recv, both)
    return pl.pallas_call(
        body, name=name, out_shape=jax.ShapeDtypeStruct((2, 2 * r2, C), F32),
        grid_spec=pltpu.PrefetchScalarGridSpec(
            num_scalar_prefetch=1, grid=(nt,),
            in_specs=[pl.BlockSpec((None, tr, C), lambda i, p_ref: (p_ref[0], i, 0)),
                      pl.BlockSpec((NCHIP - 1, tr, C), lambda i, p_ref: (0, i, 0))] + [ANY] * (len(args) - 3),
            out_specs=pl.BlockSpec((None, tr, C), lambda i, p_ref: (layer, p_ref[1] * nt + i, 0))),
        input_output_aliases={} if both is None else {3: 0},
        compiler_params=_cp(("parallel",)),
    )(*args)


def _place():
    x, y, c = lax.axis_index("x"), lax.axis_index("y"), lax.axis_index("c")
    chips = [(1 - x, y), (x, 1 - y), (1 - x, 1 - y)]
    return x, y, c, chips


def _gather_small(blk, name):
    m_per, n = blk.shape

    def body(x_ref, out_ref, send_sems, recv_sems, local_sem):
        x, y, c, chips = _place()
        me, sibling = (x, y, c), (x, y, 1 - c)

        def rows(px, py, pc):
            return out_ref.at[pl.ds((4 * px + 2 * py + pc) * m_per, m_per), :]

        def copy(k, block, to, src=None):
            return pltpu.make_async_remote_copy(
                src_ref=rows(*block) if src is None else src, dst_ref=rows(*block),
                send_sem=send_sems.at[k], recv_sem=recv_sems.at[k], device_id=to, device_id_type=MESH)

        mine = pltpu.make_async_copy(x_ref, rows(*me), local_sem)
        mine.start()
        first = [copy(0, me, sibling, src=x_ref)]
        first += [copy(1 + j, me, (*chip, c), src=x_ref) for j, chip in enumerate(chips)]
        for cp in first:
            cp.start()
        passed = [copy(4 + j, (*chip, c), sibling) for j, chip in enumerate(chips)]
        for j, chip in enumerate(chips):
            copy(1 + j, (*chip, c), me).wait_recv()
            passed[j].start()
        copy(0, sibling, me).wait_recv()
        for j, chip in enumerate(chips):
            copy(4 + j, (*chip, 1 - c), me).wait_recv()
        for cp in first + passed:
            cp.wait_send()
        mine.wait()

    return pl.pallas_call(
        body, name=name, out_shape=jax.ShapeDtypeStruct((NDEV * m_per, n), blk.dtype),
        in_specs=[pl.BlockSpec(memory_space=pltpu.VMEM)], out_specs=pl.BlockSpec(memory_space=pltpu.VMEM),
        scratch_shapes=[pltpu.SemaphoreType.DMA((7,)), pltpu.SemaphoreType.DMA((7,)), pltpu.SemaphoreType.DMA],
        compiler_params=_cp(),
    )(blk)


def _pass_rows_carry(gathered, m_per):
    def copy(outs, send_sems, recv_sems, j, px, py, pc):
        x, y, c, _ = _place()
        blk = outs[0].at[pl.ds((4 * px + 2 * py + pc) * m_per, m_per), :]
        return pltpu.make_async_remote_copy(src_ref=blk, dst_ref=blk, send_sem=send_sems.at[j],
                                            recv_sem=recv_sems.at[j], device_id=(x, y, 1 - c),
                                            device_id_type=MESH)

    def start(ins, outs, send_sems, recv_sems):
        _, _, c, chips = _place()
        for j, chip in enumerate(chips):
            copy(outs, send_sems, recv_sems, j, *chip, c).start()

    def finish(ins, outs, send_sems, recv_sems):
        _, _, c, chips = _place()
        for j, chip in enumerate(chips):
            copy(outs, send_sems, recv_sems, j, *chip, 1 - c).wait_recv()
        for j, chip in enumerate(chips):
            copy(outs, send_sems, recv_sems, j, *chip, c).wait_send()

    return _Carry([gathered], [jax.ShapeDtypeStruct(gathered.shape, gathered.dtype)], {0: 0}, 3, start, finish)


def _gather_rows_carry(blk, pass_on=True):
    m_per, n = blk.shape

    def rows(ref, px, py, pc):
        return ref.at[pl.ds((4 * px + 2 * py + pc) * m_per, m_per), :]

    def copy(ins, outs, send_sems, recv_sems, k, block, to, own=False):
        return pltpu.make_async_remote_copy(
            src_ref=ins[0] if own else rows(outs[0], *block), dst_ref=rows(outs[0], *block),
            send_sem=send_sems.at[k], recv_sem=recv_sems.at[k], device_id=to, device_id_type=MESH)

    def mine(ins, outs, send_sems):
        x, y, c, _ = _place()
        return pltpu.make_async_copy(ins[0], rows(outs[0], x, y, c), send_sems.at[7])

    def start(ins, outs, send_sems, recv_sems):
        x, y, c, chips = _place()
        mine(ins, outs, send_sems).start()
        copy(ins, outs, send_sems, recv_sems, 0, (x, y, c), (x, y, 1 - c), own=True).start()
        for j, chip in enumerate(chips):
            copy(ins, outs, send_sems, recv_sems, 1 + j, (x, y, c), (*chip, c), own=True).start()

    def finish(ins, outs, send_sems, recv_sems):
        x, y, c, chips = _place()
        for j, chip in enumerate(chips):
            copy(ins, outs, send_sems, recv_sems, 1 + j, (*chip, c), (x, y, c)).wait_recv()
            if pass_on:
                copy(ins, outs, send_sems, recv_sems, 4 + j, (*chip, c), (x, y, 1 - c)).start()
        copy(ins, outs, send_sems, recv_sems, 0, (x, y, 1 - c), (x, y, c)).wait_recv()
        for j, chip in enumerate(chips):
            if pass_on:
                copy(ins, outs, send_sems, recv_sems, 4 + j, (*chip, 1 - c), (x, y, c)).wait_recv()
        copy(ins, outs, send_sems, recv_sems, 0, (x, y, c), (x, y, 1 - c), own=True).wait_send()
        for j, chip in enumerate(chips):
            copy(ins, outs, send_sems, recv_sems, 1 + j, (x, y, c), (*chip, c), own=True).wait_send()
            if pass_on:
                copy(ins, outs, send_sems, recv_sems, 4 + j, (*chip, c), (x, y, 1 - c)).wait_send()
        mine(ins, outs, send_sems).wait()

    return _Carry([blk], [jax.ShapeDtypeStruct((NDEV * m_per, n), blk.dtype)], {}, 8, start, finish)


def _gather_carry(shards, piece=(0, 1, 1), pass_on=True, late=None):
    n = len(shards)

    def rows(ref, half, pc):
        first, count, of = pc
        r2 = ref.shape[1] // 2
        return pl.ds(half * r2 + first * (r2 // of), count * (r2 // of))

    def over_ici(outs, send_sems, recv_sems, a, j, chip_xy, slot):
        x, y, c, _ = _place()
        blk = outs[a].at[slot, rows(outs[a], c, piece), :]
        return pltpu.make_async_remote_copy(
            src_ref=blk, dst_ref=blk, send_sem=send_sems.at[9 * a + j], recv_sem=recv_sems.at[9 * a + j],
            device_id=(*chip_xy, c), device_id_type=MESH)

    def over_d2d(outs, send_sems, recv_sems, a, j, slot, half, pc):
        x, y, c, _ = _place()
        blk = outs[a].at[slot, rows(outs[a], half, pc), :]
        k = 9 * a + (3 if pc is piece else 6) + j
        return pltpu.make_async_remote_copy(
            src_ref=blk, dst_ref=blk, send_sem=send_sems.at[k], recv_sem=recv_sems.at[k],
            device_id=(x, y, 1 - c), device_id_type=MESH)

    def start(ins, outs, send_sems, recv_sems):
        x, y, c, chips = _place()
        for a in range(n):
            for j, (cx, cy) in enumerate(chips):
                over_ici(outs, send_sems, recv_sems, a, j, (cx, cy), 2 * x + y).start()
                if late:
                    over_d2d(outs, send_sems, recv_sems, a, j, 2 * cx + cy, c, late).start()

    def finish(ins, outs, send_sems, recv_sems):
        x, y, c, chips = _place()
        passed = ([piece] if pass_on else []) + ([late] if late else [])
        for a in range(n):
            for j, (cx, cy) in enumerate(chips):
                over_ici(outs, send_sems, recv_sems, a, j, (cx, cy), 2 * cx + cy).wait_recv()
                if pass_on:
                    over_d2d(outs, send_sems, recv_sems, a, j, 2 * cx + cy, c, piece).start()
        for a in range(n):
            for j, (cx, cy) in enumerate(chips):
                for pc in passed:
                    over_d2d(outs, send_sems, recv_sems, a, j, 2 * cx + cy, 1 - c, pc).wait_recv()
        for a in range(n):
            for j, (cx, cy) in enumerate(chips):
                over_ici(outs, send_sems, recv_sems, a, j, (cx, cy), 2 * x + y).wait_send()
                for pc in passed:
                    over_d2d(outs, send_sems, recv_sems, a, j, 2 * cx + cy, c, pc).wait_send()

    return _Carry(shards, [jax.ShapeDtypeStruct(s.shape, s.dtype) for s in shards],
                  {a: a for a in range(n)}, 9 * n, start, finish)


def _rs_pair(grads, name):
    n = len(grads)

    def body(*refs):
        ins, gots = refs[:n], refs[n:2 * n]
        send_sems, recv_sems = refs[2 * n:]
        x, y, c, _ = _place()
        cps = []
        for a in range(n):
            r2 = ins[a].shape[1] // 2
            cp = pltpu.make_async_remote_copy(
                src_ref=ins[a].at[:, pl.ds((1 - c) * r2, r2), :], dst_ref=gots[a],
                send_sem=send_sems.at[a], recv_sem=recv_sems.at[a],
                device_id=(x, y, 1 - c), device_id_type=MESH)
            cp.start()
            cps.append(cp)
        for cp in cps:
            cp.wait()

    half = [jax.ShapeDtypeStruct((NCHIP, g.shape[1] // 2, g.shape[2]), g.dtype) for g in grads]
    return pl.pallas_call(
        body, name=name, out_shape=half, in_specs=[ANY] * n, out_specs=[ANY] * n,
        scratch_shapes=[pltpu.SemaphoreType.DMA((n,)), pltpu.SemaphoreType.DMA((n,))],
        compiler_params=_cp(),
    )(*grads)


def _pair_sum(core, g, name):
    _, R, C = g.shape
    r2 = R // 2

    def body(c_ref, g_ref, own_ref, o_ref, got_ref, buf, send_sems, recv_sems, local_sems):
        j = pl.program_id(0)
        x, y, c, _ = _place()

        def remote(k):
            return pltpu.make_async_remote_copy(
                src_ref=g_ref.at[k, pl.ds((1 - c) * r2, r2), :], dst_ref=got_ref.at[k],
                send_sem=send_sems.at[k], recv_sem=recv_sems.at[k], device_id=(x, y, 1 - c), device_id_type=MESH)

        def fetch(k):
            return pltpu.make_async_copy(got_ref.at[k], buf.at[k % 2], local_sems.at[k % 2])

        @pl.when(j == 0)
        def _():
            for k in range(NCHIP):
                remote(k).start()
            remote(0).wait_recv()
            fetch(0).start()

        fetch(j).wait()

        @pl.when(j + 1 < NCHIP)
        def _():
            remote(j + 1).wait_recv()
            fetch(j + 1).start()

        o_ref[...] = (own_ref[...].astype(F32) + buf[j % 2].astype(F32)).astype(o_ref.dtype)

        @pl.when(j == NCHIP - 1)
        def _():
            for k in range(NCHIP):
                remote(k).wait_send()

    half = jax.ShapeDtypeStruct((NCHIP, r2, C), g.dtype)
    return pl.pallas_call(
        body, name=name, out_shape=[half, half],
        grid_spec=pltpu.PrefetchScalarGridSpec(
            num_scalar_prefetch=1, grid=(NCHIP,),
            in_specs=[ANY, pl.BlockSpec((None, r2, C), lambda j, c_ref: (j, c_ref[0], 0))],
            out_specs=[pl.BlockSpec((None, r2, C), lambda j, c_ref: (j, 0, 0)), ANY],
            scratch_shapes=[pltpu.VMEM((2, r2, C), g.dtype), pltpu.SemaphoreType.DMA((NCHIP,)),
                            pltpu.SemaphoreType.DMA((NCHIP,)), pltpu.SemaphoreType.DMA((2,))]),
        compiler_params=_cp(("arbitrary",)),
    )(core, g, g)[0]


def _pair_carry(grads):
    n = len(grads)

    def copy(ins, outs, send_sems, recv_sems, a):
        x, y, c, _ = _place()
        r2 = ins[a].shape[1] // 2
        return pltpu.make_async_remote_copy(
            src_ref=ins[a].at[:, pl.ds((1 - c) * r2, r2), :], dst_ref=outs[a],
            send_sem=send_sems.at[a], recv_sem=recv_sems.at[a],
            device_id=(x, y, 1 - c), device_id_type=MESH)

    def start(ins, outs, send_sems, recv_sems):
        for a in range(n):
            copy(ins, outs, send_sems, recv_sems, a).start()

    def finish(ins, outs, send_sems, recv_sems):
        for a in range(n):
            copy(ins, outs, send_sems, recv_sems, a).wait()

    half = [jax.ShapeDtypeStruct((NCHIP, g.shape[1] // 2, g.shape[2]), g.dtype) for g in grads]
    return _Carry(grads, half, {}, n, start, finish)


def _chips_carry(parts, piece=(0, 1, 1), into=None):
    n = len(parts)
    first, count, of = piece

    def rows(ref):
        step = ref.shape[1] // of
        return pl.ds(first * step, count * step)

    def send(ins, outs, send_sems, recv_sems, a, j, chip_xy):
        x, y, c, _ = _place()
        me, them = 2 * x + y, 2 * chip_xy[0] + chip_xy[1]
        return pltpu.make_async_remote_copy(
            src_ref=ins[a].at[them, rows(ins[a]), :],
            dst_ref=outs[a].at[me - (me > them).astype(jnp.int32), rows(outs[a]), :],
            send_sem=send_sems.at[3 * a + j], recv_sem=recv_sems.at[3 * a + j],
            device_id=(*chip_xy, c), device_id_type=MESH)

    def start(ins, outs, send_sems, recv_sems):
        _, _, _, chips = _place()
        for a in range(n):
            for j, chip_xy in enumerate(chips):
                send(ins, outs, send_sems, recv_sems, a, j, chip_xy).start()

    def finish(ins, outs, send_sems, recv_sems):
        x, y, c, chips = _place()
        me = 2 * x + y
        for a in range(n):
            for j, (cx, cy) in enumerate(chips):
                them = 2 * cx + cy
                blk = outs[a].at[them - (them > me).astype(jnp.int32), rows(outs[a]), :]
                pltpu.make_async_remote_copy(
                    src_ref=blk, dst_ref=blk, send_sem=send_sems.at[3 * a + j], recv_sem=recv_sems.at[3 * a + j],
                    device_id=(cx, cy, c), device_id_type=MESH).wait_recv()
        for a in range(n):
            for j, chip_xy in enumerate(chips):
                send(ins, outs, send_sems, recv_sems, a, j, chip_xy).wait_send()

    landing = [jax.ShapeDtypeStruct((NCHIP - 1,) + p.shape[1:], p.dtype) for p in parts]
    if into is None:
        return _Carry(parts, landing, {}, 3 * n, start, finish)
    return _Carry(list(parts) + list(into), landing, {n + a: a for a in range(n)}, 3 * n, start, finish)


def _swap_carry(fulls, layers):
    n = len(fulls)

    def copy(outs, send_sems, recv_sems, a, half):
        x, y, c, _ = _place()
        r2 = outs[a].shape[1] // 2
        blk = outs[a].at[pl.ds(*layers[a]), pl.ds(half * r2, r2), :]
        return pltpu.make_async_remote_copy(
            src_ref=blk, dst_ref=blk, send_sem=send_sems.at[a], recv_sem=recv_sems.at[a],
            device_id=(x, y, 1 - c), device_id_type=MESH)

    def start(ins, outs, send_sems, recv_sems):
        c = lax.axis_index("c")
        for a in range(n):
            copy(outs, send_sems, recv_sems, a, c).start()

    def finish(ins, outs, send_sems, recv_sems):
        c = lax.axis_index("c")
        for a in range(n):
            copy(outs, send_sems, recv_sems, a, 1 - c).wait_recv()
        for a in range(n):
            copy(outs, send_sems, recv_sems, a, c).wait_send()

    return _Carry(fulls, [jax.ShapeDtypeStruct(f.shape, f.dtype) for f in fulls], {a: a for a in range(n)}, n,
                  start, finish)


def _rs_swap(fulls, layers):
    n = len(fulls)
    swap = _swap_carry(fulls, layers)

    def body(*refs):
        outs, (send_sems, recv_sems) = refs[n:2 * n], refs[2 * n:]
        swap.start(None, outs, send_sems, recv_sems)
        swap.finish(None, outs, send_sems, recv_sems)

    return pl.pallas_call(
        body, name="rs_swap", out_shape=swap.outs, in_specs=[ANY] * n, out_specs=[ANY] * n,
        input_output_aliases=swap.aliases,
        scratch_shapes=[pltpu.SemaphoreType.DMA((n,)), pltpu.SemaphoreType.DMA((n,))],
        compiler_params=_cp(),
    )(*fulls)


def _tail_weight_grads(merged_t, b_in_t, a_in_t, dy, dbr_b, dbr_a, name, tn=256):
    S = dy.shape[0]
    nn = D // tn

    def body(mt_ref, bt_ref, at_ref, dy_ref, db_ref, da_ref, go_ref, gh_ref, gp_ref):
        go_ref[...] = jnp.dot(mt_ref[...], dy_ref[...], preferred_element_type=F32).astype(BF16)
        gh_ref[...] = jnp.dot(bt_ref[...], db_ref[...], preferred_element_type=F32).astype(BF16)
        gp_ref[...] = jnp.dot(at_ref[...], da_ref[...], preferred_element_type=F32).astype(BF16)

    left = lambda rows: pl.BlockSpec((rows, S), lambda n: (0, 0))
    right = pl.BlockSpec((S, tn), lambda n: (0, n))
    out = pl.BlockSpec((D, tn), lambda n: (0, n))
    return pl.pallas_call(
        body, name=name, grid=(nn,), in_specs=[left(D), left(D), left(POOL_W), right, right, right],
        out_specs=[out, out, pl.BlockSpec((None, POOL_W, tn), lambda n: (n, 0, 0))],
        out_shape=[jax.ShapeDtypeStruct((D, D), BF16), jax.ShapeDtypeStruct((D, D), BF16),
                   jax.ShapeDtypeStruct((NCHIP, POOL_W, D // NCHIP), BF16)],
        compiler_params=_cp(("parallel",)),
    )(merged_t, b_in_t, a_in_t, dy, dbr_b, dbr_a)


class _GatherInProj:
    def __init__(self, slot, order):
        self.slot, self.order = slot, order


def _proj_with_gather(h, w_slot, order, name, tn=256):
    S, K = h.shape
    nsh, _, ns = w_slot.shape
    tps = ns // tn
    nt = nsh * tps
    r2 = K // 2

    def body(ord_ref, h_ref, w_in_ref, o_ref, w_ref, wbuf, tile_sems, send_sems, recv_sems):
        n = pl.program_id(0)
        x, y, c, chips = _place()

        def half(slot, which):
            return w_ref.at[slot, pl.ds(which * r2, r2), :]

        def over_ici(j, slot):
            blk = half(slot, c)
            return pltpu.make_async_remote_copy(src_ref=blk, dst_ref=blk, send_sem=send_sems.at[j],
                                                recv_sem=recv_sems.at[j], device_id=(*chips[j], c),
                                                device_id_type=MESH)

        def over_d2d(j, which):
            blk = half(2 * chips[j][0] + chips[j][1], which)
            return pltpu.make_async_remote_copy(src_ref=blk, dst_ref=blk, send_sem=send_sems.at[3 + j],
                                                recv_sem=recv_sems.at[3 + j], device_id=(x, y, 1 - c),
                                                device_id_type=MESH)

        def tile_copy(step, slot):
            shard = ord_ref[step // tps]
            return pltpu.make_async_copy(w_ref.at[shard, :, pl.ds((step % tps) * tn, tn)], wbuf.at[slot],
                                         tile_sems.at[slot])

        @pl.when(n == 0)
        def _():
            for j in range(3):
                over_ici(j, 2 * x + y).start()
            tile_copy(0, 0).start()

        for j in range(3):
            @pl.when(n == (j + 1) * tps - 1)
            def _(j=j):
                over_ici(j, 2 * chips[j][0] + chips[j][1]).wait_recv()
                over_d2d(j, c).start()
                over_d2d(j, 1 - c).wait_recv()

        @pl.when(n + 1 < nt)
        def _():
            tile_copy(n + 1, (n + 1) % 2).start()

        tile_copy(n, n % 2).wait()
        o_ref[...] = jnp.dot(h_ref[...], wbuf[n % 2], preferred_element_type=F32).astype(o_ref.dtype)

        @pl.when(n == nt - 1)
        def _():
            for j in range(3):
                over_ici(j, 2 * x + y).wait_send()
                over_d2d(j, c).wait_send()

    return pl.pallas_call(
        body, name=name,
        out_shape=[jax.ShapeDtypeStruct((S, nsh * ns), BF16), jax.ShapeDtypeStruct(w_slot.shape, w_slot.dtype)],
        grid_spec=pltpu.PrefetchScalarGridSpec(
            num_scalar_prefetch=1, grid=(nt,),
            in_specs=[pl.BlockSpec((S, K), lambda n, o_ref: (0, 0)), ANY],
            out_specs=[pl.BlockSpec((S, tn), lambda n, o_ref: (0, o_ref[n // tps] * tps + n % tps)), ANY],
            scratch_shapes=[pltpu.VMEM((2, K, tn), w_slot.dtype), pltpu.SemaphoreType.DMA((2,)),
                            pltpu.SemaphoreType.DMA((6,)), pltpu.SemaphoreType.DMA((6,))]),
        input_output_aliases={2: 1},
        compiler_params=_cp(("arbitrary",)),
    )(order, h, w_slot)


def _mm_ride(a, b, carry, **kw):
    if carry is None:
        return _mm(a, b, **kw), []
    return _mm(a, b, carry=carry, **kw)


def _layer_fwd(l, x, ada, w, small, ride, target=None):
    shift, scale, gate = ada[:, 0:D], ada[:, D:2 * D], ada[:, 2 * D:3 * D]
    carry, landed = ride("prenorm")
    (h, h_t), outs = _prenorm_fwd(x, small["g_pre"][l], scale, shift, f"prenorm_fwd{l}", carry)
    landed(outs)
    carry, landed = ride("proj")
    if isinstance(carry, _GatherInProj):
        proj, full = _proj_with_gather(h, carry.slot, carry.order, f"proj{l}")
        outs = [full]
    else:
        proj, outs = _mm_ride(h, w["w_in"][l], carry, name=f"proj{l}", b_mode="nn_sh", tm=2048, out_dtype=BF16)
    landed(outs)
    a_in, a_in_t = _pool_fwd(proj, small["pool_w"][l], small["pool_scale"][l], f"pool_fwd{l}")
    carry, landed = ride("hgrn")
    (b_in, b_in_t, o_raw, states, mild, cum), outs = _hgrn_fwd(proj, small["lb"][l], small["hgrn_norm_g"][l],
                                                              f"hgrn_fwd{l}", carry=carry)
    landed(outs)
    carry, landed = ride("tail")
    (br_a, br_b, merged_t, y, *x_new), outs = _layer_tail_fwd(
        proj, a_in, b_in, x, w["w_pool_o"][l], w["w_hgrn_o"][l].reshape(D, D), w["w_out"][l].reshape(D, D),
        gate, small["g_post"][l], f"tail_fwd{l}", target=target, carry=carry)
    landed(outs)
    saved = dict(x=x, h_t=h_t, proj=proj, a_in_t=a_in_t, b_in_t=b_in_t, o_raw=o_raw, states=states, mild=mild,
                 cum=cum,
                 br_a=br_a, br_b=br_b, merged_t=merged_t, y=y, scale=scale, gate=gate)
    return x_new, saved


def _layer_bwd(l, dxn, sv, w, small, ride):
    carry, landed = ride["head"](None)
    (dy, dbr_a, dbr_b, dproj, da_in, db_in, dgate, dg_post), outs = _layer_head_bwd(
        dxn, sv["y"], sv["proj"], sv["br_a"], sv["br_b"], w["w_pool_o"][l], w["w_hgrn_o"][l].reshape(D, D),
        w["w_out"][l].reshape(D, D), sv["gate"], small["g_post"][l], f"head_bwd{l}", carry)
    landed(outs)
    gw_out, gw_hgrn_o, gw_pool_o = _tail_weight_grads(sv["merged_t"], sv["b_in_t"], sv["a_in_t"], dy, dbr_b,
                                                      dbr_a, f"gw_tail{l}")
    big = dict(w_pool_o=gw_pool_o, w_hgrn_o=gw_hgrn_o.reshape(NCHIP, D // NCHIP, D),
               w_out=gw_out.reshape(NCHIP, D // NCHIP, D))
    carry, landed = ride["hgrn"](big)
    (dproj, dlb, dgn), outs = _hgrn_bwd(db_in, sv["proj"], sv["o_raw"], sv["states"], sv["mild"], sv["cum"],
                                        small["lb"][l], small["hgrn_norm_g"][l], dproj, f"hgrn_bwd{l}",
                                        carry=carry)
    landed(outs)
    dproj, dpw, dpsc = _pool_bwd(da_in, sv["proj"], small["pool_w"][l], small["pool_scale"][l], dproj,
                                 f"pool_bwd{l}")
    little = dict(dgate=dgate, g_post=dg_post, pool_w=dpw, pool_scale=dpsc, lb=dlb,
                  hgrn_norm_g=jnp.sum(dgn, axis=0, keepdims=True))
    carry, landed = ride["gw_in"](little)
    big["w_in"], outs = _mm_ride(sv["h_t"], dproj, carry, name=f"gw_in{l}", out_shards=NCHIP, out_dtype=BF16)
    landed(outs)
    carry, landed = ride["d_h"](big)
    dh, outs = _mm_ride(dproj, w["w_in"][l], carry, name=f"d_h{l}", b_mode="nt_shk", tn=1024)
    landed(outs)
    carry, landed = ride["prenorm"](big)
    (dx, dshift, dscale, dg_pre), outs = _prenorm_bwd(dh, dxn, sv["x"], small["g_pre"][l], sv["scale"],
                                                      f"prenorm_bwd{l}", carry)
    landed(outs)
    little.update(dshift=dshift, dscale=dscale, g_pre=dg_pre)
    return dx, big, little


SMALL_ROWS = 176


def _rows8(t):
    t = t.reshape(-1, D)
    return jnp.pad(t, ((0, -t.shape[0] % 8), (0, 0)))


def _pack_small(parts):
    row_keys = ("dshift", "dscale", "dgate", "g_pre", "g_post", "lb", "pool_scale", "hgrn_norm_g")
    flat = [p[k] for p in parts for k in row_keys] + [p["pool_w"] for p in parts]
    nk = len(row_keys)

    def body(*refs):
        o_ref = refs[-1]
        o_ref[...] = jnp.zeros((SMALL_ROWS, D), F32)
        for l in range(2):
            dshift, dscale, dgate, g_pre, g_post, lb, pscale, gn = refs[l * nk:(l + 1) * nk]
            for r, ref in enumerate((dshift, dscale, dgate)):
                o_ref[3 * l + r:3 * l + r + 1, :] = ref[...]
            o_ref[8 + l:9 + l, :] = g_pre[...]
            o_ref[16 + l:17 + l, :] = g_post[...]
            o_ref[24 + l:25 + l, :] = lb[...]
            o_ref[160:161, l * POOL_W:(l + 1) * POOL_W] = pscale[...]
            o_ref[168:169, l * HD:(l + 1) * HD] = gn[...]
        for (l, *at), (rows, lanes) in _small_pieces("pool_w"):
            o_ref[rows, lanes] = refs[2 * nk + l][tuple(at)]

    return pl.pallas_call(body, name="pack_small", out_shape=jax.ShapeDtypeStruct((SMALL_ROWS, D), F32),
                          compiler_params=_cp())(*flat)


def kernel(x, c, w_ada, b_ada, g_pre, g_post, w_in, pool_w, pool_scale, lb_logits, hgrn_norm_g, w_pool_o, w_hgrn_o, w_out, loss_target, m_w_ada, m_b_ada, m_g_pre, m_g_post, m_w_in, m_pool_w, m_pool_scale, m_lb_logits, m_hgrn_norm_g, m_w_pool_o, m_w_hgrn_o, m_w_out, v_w_ada, v_b_ada, v_g_pre, v_g_post, v_w_in, v_pool_w, v_pool_scale, v_lb_logits, v_hgrn_norm_g, v_w_pool_o, v_w_hgrn_o, v_w_out):
    ax, ay, ac = lax.axis_index("x"), lax.axis_index("y"), lax.axis_index("c")
    chip = 2 * ax + ay
    dev = 2 * chip + ac
    xe, te = x[0], loss_target[0]
    ada_s = w_ada.shape[2]

    big_names = ("w_in", "w_pool_o", "w_hgrn_o", "w_out")
    big_w = (w_in, w_pool_o, w_hgrn_o, w_out)
    core = jnp.stack([ac]).astype(jnp.int32)
    place = jnp.stack([chip, ac]).astype(jnp.int32)
    slots = {("w_in", l): _cast_to_slot(place, w_in, l, f"cast_w_in{l}") for l in range(2)}
    rest = [(k, l) for l in range(2) for k in big_names[1:]]
    slots.update(zip(rest, _cast_to_slots(place, [(dict(zip(big_names, big_w))[k], l) for k, l in rest],
                                          "cast_rest")))
    w = {k: [None, None] for k in big_names}
    def fills(keys):
        def landed(outs):
            for (k, l), o in zip(keys, outs):
                w[k][l] = slots[k, l] = o
        return landed

    rest0 = [(k, 0) for k in big_names[1:]]
    rest1 = [(k, 1) for k in big_names[1:]]
    no_carry = (None, lambda outs: None)
    order = jnp.stack([chip, 2 * (1 - ax) + ay, 2 * ax + (1 - ay), 2 * (1 - ax) + (1 - ay)]).astype(jnp.int32)

    def ride_fwd0(stage):
        if stage == "proj":
            return _GatherInProj(slots["w_in", 0], order), fills([("w_in", 0)])
        if stage == "hgrn":
            return (_join_carries(_gather_carry([slots[t] for t in rest0]),
                                  _gather_carry([slots["w_in", 1]], piece=(0, 2, 4), pass_on=False)),
                    fills(rest0 + [("w_in", 1)]))
        if stage == "tail":
            return (_gather_carry([slots["w_in", 1]], piece=(2, 1, 4), pass_on=False, late=(0, 2, 4)),
                    fills([("w_in", 1)]))
        return no_carry

    def ride_fwd1(stage):
        if stage == "prenorm":
            return _gather_carry([slots["w_in", 1]], piece=(3, 1, 4), late=(2, 1, 4)), fills([("w_in", 1)])
        if stage == "hgrn":
            return _gather_carry([slots[t] for t in rest1]), fills(rest1)
        return no_carry

    c_all = _gather_small(jnp.broadcast_to(c, (8, D)), "gather_c").reshape(NDEV, 8, D)[:, 0, :]
    c_pad = jnp.pad(c_all, ((0, ADA_PAD - NDEV), (0, 0)))
    b_sh = lax.dynamic_slice(b_ada, (0, chip * ada_s), (2, ada_s))
    ada_cols = _gather_small(_ada_fwd(c_pad, w_ada, b_sh), "gather_ada")
    ada_cols = ada_cols.reshape(NCHIP, 2, NDEV, 2, ada_s)[:, 0]
    ada_all = jnp.transpose(ada_cols, (2, 1, 0, 3)).reshape(2, NDEV, 3 * D)
    ada_me = lax.dynamic_slice(ada_all, (0, dev, 0), (2, 1, 3 * D))

    lbs = _lb_fwd(lb_logits)
    small = dict(g_pre=g_pre[:, None, :], g_post=g_post[:, None, :], pool_w=pool_w,
                 pool_scale=pool_scale[:, None, :], lb=lbs[:, None, :], hgrn_norm_g=hgrn_norm_g[:, None, :])

    (x1,), sv0 = _layer_fwd(0, xe, ada_me[0], w, small, ride_fwd0)
    (dx2, loss_blk), sv1 = _layer_fwd(1, x1, ada_me[1], w, small, ride_fwd1, target=te)

    parts, recv, held = {}, {}, {}

    def pair_ride(keys, grads):
        def landed(outs):
            held.update({kl: (g, o) for kl, g, o in zip(keys, grads, outs)})
        return _pair_carry(grads), landed

    def pair_adds(keys):
        gs, gots = zip(*[held.pop(kl) for kl in keys])
        if keys[0][0] == "w_in":
            parts[keys[0]] = _pair_add(core, gs[0], gots[0], f"rs_add_w_in{keys[0][1]}")
        else:
            parts.update(zip(keys, _pair_adds(core, gs, gots, f"rs_add_early{keys[0][1]}")))

    def exchange(keys):
        def landed(outs):
            recv.update(zip(keys, outs))
        return _chips_carry([parts[kl] for kl in keys]), landed

    def share(key, first, count):
        def landed(outs):
            (recv[key],) = outs
        into = [recv[key]] if key in recv else None
        return _chips_carry([parts[key]], piece=(first, count, 8), into=into), landed

    def together(*rides):
        carries, fns = zip(*rides)

        def landed(outs):
            for cr, fn in zip(carries, fns):
                fn(outs[:len(cr.outs)])
                outs = outs[len(cr.outs):]
        return _join_carries(*carries), landed

    def early(l):
        return [(k, l) for k in big_names[1:]]

    def pair_alone(keys, grads, tag):
        held.update({kl: (g, o) for kl, g, o in zip(keys, grads, _rs_pair(grads, f"rs_pair_{tag}"))})
        pair_adds(keys)

    def ride_hgrn1(big):
        return pair_ride(early(1), [big[k] for k in big_names[1:]])

    def ride_gw_in1(_):
        pair_adds(early(1))
        return exchange(early(1))

    def ride_d_h1(big):
        return pair_ride([("w_in", 1)], [big["w_in"]])

    def ride_prenorm1(_):
        pair_adds([("w_in", 1)])
        return share(("w_in", 1), 0, 1)

    def ride_head0(_):
        return share(("w_in", 1), 1, 3)

    def ride_hgrn0(big):
        pair_alone(early(0), [big[k] for k in big_names[1:]], "early0")
        return together(exchange(early(0)), share(("w_in", 1), 4, 4))

    def ride_d_h0(big):
        parts["w_in", 0] = _pair_sum(core, big["w_in"], "rs_pair_sum_w_in0")

        def passed(outs):
            (gathered["early"],) = outs
        return together(share(("w_in", 0), 0, 4), (_pass_rows_carry(gathered["early"], SMALL_ROWS), passed))

    def ride_prenorm0(_):
        return share(("w_in", 0), 4, 4)

    no_ride = lambda so_far: no_carry
    dx1, big1, little1 = _layer_bwd(1, dx2, sv1, w, small, dict(head=no_ride, hgrn=ride_hgrn1, gw_in=ride_gw_in1,
                                                                d_h=ride_d_h1, prenorm=ride_prenorm1))

    gathered = {}
    zero_row = jnp.zeros((1, D), F32)

    def ride_gw_in0(little):
        so_far = dict(little, dshift=zero_row, dscale=zero_row, g_pre=zero_row)

        def landed(outs):
            (gathered["early"],) = outs

        red = [_chip_sum(place, parts["w_in", 1], recv["w_in", 1], 1, None, "rs_sum_w_in1")]
        red += _chip_sums(place, [[parts[k, l] for l in range(2)] for k in big_names[1:]],
                          [[recv[k, l] for l in range(2)] for k in big_names[1:]], "rs_sum_early")

        def swapped(outs):
            gathered["sums"] = outs
        return together((_gather_rows_carry(_pack_small([so_far, little1]), pass_on=False), landed),
                        (_swap_carry(red, [(1, 1)] + [(0, 2)] * 3), swapped))

    dx0, big0, little0 = _layer_bwd(0, dx1, sv0, w, small,
                                    dict(head=ride_head0, hgrn=ride_hgrn0, gw_in=ride_gw_in0, d_h=ride_d_h0,
                                         prenorm=ride_prenorm0))
    loss_row = jnp.broadcast_to(loss_blk[0:1, 0:1], (1, D))
    late = _rows8(jnp.stack([little0["dshift"], little0["dscale"], little0["g_pre"], loss_row]))
    late = _gather_small(late, "gather_small_late").reshape(NDEV, 8, D)
    loss = jnp.sum(late[:, 3, 0])
    red = _chip_sum(place, parts["w_in", 0], recv["w_in", 0], 0, gathered["sums"][0], "rs_sum_w_in0")
    g_big = dict(zip(big_names, list(_rs_swap([red], [(0, 1)])) + list(gathered["sums"][1:])))

    def two(t):
        return t.reshape(-1, t.shape[-1])

    def upd(wt, g, m, v, name, echo=False):
        return [t.reshape(wt.shape) for t in _adamw(two(wt), two(g), two(m), two(v), name, echo)]

    *u_w_in, g_w_in = upd(w_in, g_big["w_in"], m_w_in, v_w_in, "adamw_w_in", echo=True)
    g_small, pack_heads = _sum_devices(gathered["early"].reshape(NDEV, SMALL_ROWS, D), late)
    g_lb_logits = _lb_bwd(lb_logits, g_small[24:26])
    d_ada_all = pack_heads[:, 0:6, :].reshape(NDEV, 2, 3 * D)
    d_ada_sh = lax.dynamic_slice(jnp.transpose(d_ada_all, (1, 0, 2)), (0, 0, chip * ada_s), (2, NDEV, ada_s))
    d_ada_sh = jnp.pad(d_ada_sh, ((0, 0), (0, ADA_PAD - NDEV), (0, 0)))
    g_w_ada = _ada_wgrad(c_pad.T, d_ada_sh)

    u_w_ada = upd(w_ada, g_w_ada, m_w_ada, v_w_ada, "adamw_w_ada")
    early_w = dict(w_pool_o=(w_pool_o, m_w_pool_o, v_w_pool_o), w_hgrn_o=(w_hgrn_o, m_w_hgrn_o, v_w_hgrn_o),
                   w_out=(w_out, m_w_out, v_w_out))
    u_early = _adamw_many([(two(early_w[k][0]), two(g_big[k]), two(early_w[k][1]), two(early_w[k][2]))
                           for k in big_names[1:]], "adamw_early")
    (*u_w_pool_o, g_w_pool_o), (*u_w_hgrn_o, g_w_hgrn_o), (*u_w_out, g_w_out) = [
        [t.reshape(early_w[k][0].shape) for t in four] for k, four in zip(big_names[1:], u_early)]
    small_w = dict(b_ada=(b_ada, m_b_ada, v_b_ada), g_pre=(g_pre, m_g_pre, v_g_pre),
                   g_post=(g_post, m_g_post, v_g_post), lb_logits=(lb_logits, m_lb_logits, v_lb_logits),
                   pool_w=(pool_w, m_pool_w, v_pool_w), pool_scale=(pool_scale, m_pool_scale, v_pool_scale),
                   hgrn_norm_g=(hgrn_norm_g, m_hgrn_norm_g, v_hgrn_norm_g))
    u_small = _adamw_small(g_small, g_lb_logits, small_w)
    s = lambda key: u_small[key][3]
    grads_out = (g_w_ada, s("b_ada"), s("g_pre"), s("g_post"), g_w_in, s("pool_w"), s("pool_scale"), g_lb_logits,
                 s("hgrn_norm_g"), g_w_pool_o, g_w_hgrn_o, g_w_out)

    def ordered(k):
        s = lambda key: u_small[key][k]
        return (u_w_ada[k], s("b_ada"), s("g_pre"), s("g_post"), u_w_in[k], s("pool_w"), s("pool_scale"),
                s("lb_logits"), s("hgrn_norm_g"), u_w_pool_o[k], u_w_hgrn_o[k], u_w_out[k])

    return (loss, dx0[None], *grads_out, *ordered(0), *ordered(1), *ordered(2))
```
